```python
import math
import jax, jax.numpy as jnp
from jax import lax
import numpy as np

D_MODEL = 1024
BATCH = 16
SEQ = 2048
DEPTH = 1

MLA_HEADS = 8
MLA_NOPE = 64
MLA_ROPE = 32
MLA_V = 64
Q_LORA = 384
KV_LORA = 256
ROPE_THETA = 10000.0
MLA_QK = MLA_NOPE + MLA_ROPE
MLA_WIDTH = MLA_HEADS * MLA_V

DIL_HEADS = 8
DIL_HEAD_DIM = 64
DIL_PATTERNS = ((128, 1), (512, 4), (2048, 16))
DIL_WIDTH = DIL_HEADS * DIL_HEAD_DIM

BLOCK = 128
N_BRANCH = 2
D_FF = 4 * D_MODEL
LN_EPS = 1e-5
RMS_EPS = 1e-6
NEG = -1e30
ALPHA = (2 * DEPTH) ** 0.25
BETA = (8 * DEPTH) ** -0.25

SPLITS = (Q_LORA,
          Q_LORA + KV_LORA,
          Q_LORA + KV_LORA + MLA_ROPE,
          Q_LORA + KV_LORA + MLA_ROPE + 3 * DIL_WIDTH)
IN_WIDTH = Q_LORA + KV_LORA + MLA_ROPE + 3 * DIL_WIDTH + N_BRANCH * D_MODEL

kernel_name = 'hybrid_mla_dilated_gated_deepnorm'


def layer_norm(x, g, b):
    xf = x.astype(jnp.float32)
    mu = jnp.mean(xf, axis=-1, keepdims=True)
    var = jnp.mean(jnp.square(xf - mu), axis=-1, keepdims=True)
    y = (xf - mu) * lax.rsqrt(var + LN_EPS) * g.astype(jnp.float32) + b.astype(jnp.float32)
    return y.astype(x.dtype)


def rms_norm(x, g):
    xf = x.astype(jnp.float32)
    y = xf * lax.rsqrt(jnp.mean(jnp.square(xf), axis=-1, keepdims=True) + RMS_EPS)
    return (y * g.astype(jnp.float32)).astype(x.dtype)


def apply_rope(t, pos):
    half = t.shape[-1] // 2
    inv = jnp.power(ROPE_THETA, -jnp.arange(half, dtype=jnp.float32) / half)
    ang = pos.astype(jnp.float32)[:, None] * inv[None, :]
    cos = jnp.cos(ang)[None, :, None, :]
    sin = jnp.sin(ang)[None, :, None, :]
    tf = t.astype(jnp.float32)
    t1, t2 = tf[..., :half], tf[..., half:]
    return jnp.concatenate([t1 * cos - t2 * sin, t1 * sin + t2 * cos], axis=-1).astype(t.dtype)


def alibi_slopes(n):
    return jnp.asarray([2.0 ** (-8.0 * (i + 1) / n) for i in range(n)], dtype=jnp.float32)


def mla_attention(q_a, kv_a, k_r, g_q_a, w_uq, g_kv_a, w_ukv):
    B, S, _ = q_a.shape
    pos = jnp.arange(S)
    q = (rms_norm(q_a, g_q_a) @ w_uq).reshape(B, S, MLA_HEADS, MLA_QK)
    q = jnp.concatenate([q[..., :MLA_NOPE], apply_rope(q[..., MLA_NOPE:], pos)], axis=-1)
    kv = (rms_norm(kv_a, g_kv_a) @ w_ukv).reshape(B, S, MLA_HEADS, MLA_NOPE + MLA_V)
    k_rope = apply_rope(k_r[:, :, None, :], pos)
    k = jnp.concatenate([kv[..., :MLA_NOPE],
                         jnp.broadcast_to(k_rope, (B, S, MLA_HEADS, MLA_ROPE))], axis=-1)
    v = kv[..., MLA_NOPE:]
    scale = MLA_QK ** -0.5
    nb = S // BLOCK
    q_blocks = q.reshape(B, nb, BLOCK, MLA_HEADS, MLA_QK).transpose(1, 0, 2, 3, 4)
    kpos = jnp.arange(S)

    def one_block(args):
        qb, i = args
        s = jnp.einsum('bqhd,bkhd->bhqk', qb, k).astype(jnp.float32) * scale
        qpos = i * BLOCK + jnp.arange(BLOCK)
        s = jnp.where((kpos[None, :] <= qpos[:, None])[None, None], s, NEG)
        p = jax.nn.softmax(s, axis=-1).astype(v.dtype)
        return jnp.einsum('bhqk,bkhd->bqhd', p, v)

    o = lax.map(one_block, (q_blocks, jnp.arange(nb)))
    return o.transpose(1, 0, 2, 3, 4).reshape(B, S, MLA_WIDTH)


def dilated_pattern(q, k, v, window, dilation, slopes):
    B, S, H, Dh = q.shape
    L = S // dilation
    J = window // dilation
    C = min(BLOCK, L)
    nb = -(-L // C)
    Lp = nb * C

    def to_classes(t):
        t = t.reshape(B, L, dilation, H, Dh).transpose(0, 2, 1, 3, 4)
        t = jnp.pad(t, ((0, 0), (0, 0), (0, Lp - L), (0, 0), (0, 0)))
        return t.reshape(B, dilation, nb, C, H, Dh)

    def with_prev(t):
        prev = jnp.pad(t, ((0, 0), (0, 0), (1, 0), (0, 0), (0, 0), (0, 0)))[:, :, :-1]
        return jnp.concatenate([prev, t], axis=3)

    def from_classes(t):
        X = t.shape[-1]
        t = t.reshape(B, dilation, Lp, H, X)[:, :, :L]
        return t.transpose(0, 2, 1, 3, 4).reshape(B, S, H, X)

    qc = to_classes(q)
    kc = with_prev(to_classes(k))
    vc = with_prev(to_classes(v))
    s = jnp.einsum('brnqhd,brnkhd->brnhqk', qc, kc).astype(jnp.float32) * (Dh ** -0.5)
    qi = jnp.arange(C)[:, None] + C
    ki = jnp.arange(2 * C)[None, :]
    dist = qi - ki
    valid = (dist >= 0) & (dist <= J)
    key_exists = (jnp.arange(nb)[:, None] > 0) | (jnp.arange(2 * C)[None, :] >= C)
    mask = valid[None] & key_exists[:, None, :]
    bias = -slopes[:, None, None] * (dilation * dist).astype(jnp.float32)[None]
    s = jnp.where(mask[None, None, :, None], s + bias, NEG)
    m = jnp.max(s, axis=-1)
    p = jnp.exp(s - m[..., None])
    den = jnp.sum(p, axis=-1)
    o = jnp.einsum('brnhqk,brnkhd->brnqhd', p, vc.astype(jnp.float32))
    den_t = den.transpose(0, 1, 2, 4, 3)[..., None]
    m_t = m.transpose(0, 1, 2, 4, 3)[..., None]
    return from_classes(o / den_t), from_classes(m_t), from_classes(den_t)


def dilated_attention(q, k, v):
    slopes = alibi_slopes(DIL_HEADS)
    res = [dilated_pattern(q, k, v, w, d, slopes) for (w, d) in DIL_PATTERNS]
    m_all = res[0][1]
    for r in res[1:]:
        m_all = jnp.maximum(m_all, r[1])
    num = 0.0
    tot = 0.0
    for o, m, den in res:
        wgt = den * jnp.exp(m - m_all)
        num = num + wgt * o
        tot = tot + wgt
    return (num / tot).astype(q.dtype)


def hybrid_layer(x, w_in, b_gate, g_q_a, w_uq, g_kv_a, w_ukv, w_o_mla, w_o_dil,
                 w_out, ln1_g, ln1_b, w_ff1, w_ff2, ln2_g, ln2_b):
    B, S, D = x.shape
    proj = x @ w_in
    q_a, kv_a, k_r, qkv_d, gates = jnp.split(proj, SPLITS, axis=-1)
    y_a = mla_attention(q_a, kv_a, k_r, g_q_a, w_uq, g_kv_a, w_ukv) @ w_o_mla
    qkv_d = qkv_d.reshape(B, S, 3, DIL_HEADS, DIL_HEAD_DIM)
    o_b = dilated_attention(qkv_d[:, :, 0], qkv_d[:, :, 1], qkv_d[:, :, 2])
    y_b = o_b.reshape(B, S, DIL_WIDTH) @ w_o_dil
    g = jax.nn.sigmoid(gates.reshape(B, S, N_BRANCH, D) + b_gate)
    mixed = (g[:, :, 0] * y_a + g[:, :, 1] * y_b) @ w_out
    h = layer_norm(ALPHA * x + mixed, ln1_g, ln1_b)
    f = jnp.square(jax.nn.relu(h @ w_ff1)) @ w_ff2
    return layer_norm(ALPHA * h + f, ln2_g, ln2_b)


def _fwd_setup_inputs(seed: int = 0) -> dict:
    key = jax.random.key(seed)
    ks = jax.random.split(key, 20)
    f32 = jnp.float32

    def nrm(k, shape, fan_in, scale=1.0):
        return jax.random.normal(k, shape, f32) * (fan_in ** -0.5) * scale

    def gain(k, shape):
        return 1.0 + 0.02 * jax.random.normal(k, shape, f32)

    def small(k, shape):
        return 0.02 * jax.random.normal(k, shape, f32)

    L_ = DEPTH
    return {
        'x': jax.random.normal(ks[0], (BATCH, SEQ, D_MODEL), f32),
        'w_in': nrm(ks[1], (L_, D_MODEL, IN_WIDTH), D_MODEL),
        'b_gate': small(ks[2], (L_, N_BRANCH, D_MODEL)),
        'g_q_a': gain(ks[3], (L_, Q_LORA)),
        'w_uq': nrm(ks[4], (L_, Q_LORA, MLA_HEADS * MLA_QK), Q_LORA),
        'g_kv_a': gain(ks[5], (L_, KV_LORA)),
        'w_ukv': nrm(ks[6], (L_, KV_LORA, MLA_HEADS * (MLA_NOPE + MLA_V)), KV_LORA),
        'w_o_mla': nrm(ks[7], (L_, MLA_WIDTH, D_MODEL), MLA_WIDTH, BETA),
        'w_o_dil': nrm(ks[8], (L_, DIL_WIDTH, D_MODEL), DIL_WIDTH, BETA),
        'w_out': nrm(ks[9], (L_, D_MODEL, D_MODEL), D_MODEL, BETA),
        'ln1_g': gain(ks[10], (L_, D_MODEL)),
        'ln1_b': small(ks[11], (L_, D_MODEL)),
        'w_ff1': nrm(ks[12], (L_, D_MODEL, D_FF), D_MODEL, BETA),
        'w_ff2': nrm(ks[13], (L_, D_FF, D_MODEL), D_FF, BETA),
        'ln2_g': gain(ks[14], (L_, D_MODEL)),
        'ln2_b': small(ks[15], (L_, D_MODEL)),
    }


def _fwd_reference(x, w_in, b_gate, g_q_a, w_uq, g_kv_a, w_ukv, w_o_mla, w_o_dil,
              w_out, ln1_g, ln1_b, w_ff1, w_ff2, ln2_g, ln2_b):
    for l in range(DEPTH):
        x = hybrid_layer(x, w_in[l], b_gate[l], g_q_a[l], w_uq[l], g_kv_a[l], w_ukv[l],
                         w_o_mla[l], w_o_dil[l], w_out[l], ln1_g[l], ln1_b[l],
                         w_ff1[l], w_ff2[l], ln2_g[l], ln2_b[l])
    return x


import jax as _jax
import jax.numpy as _jnp

TWIN_FORMAT = 'train_step'
FWD_PARAMS = ['x', 'w_in', 'b_gate', 'g_q_a', 'w_uq', 'g_kv_a', 'w_ukv', 'w_o_mla', 'w_o_dil', 'w_out', 'ln1_g', 'ln1_b', 'w_ff1', 'w_ff2', 'ln2_g', 'ln2_b']
TWIN_WEIGHTS = ['w_in', 'b_gate', 'g_q_a', 'w_uq', 'g_kv_a', 'w_ukv', 'w_o_mla', 'w_o_dil', 'w_out', 'ln1_g', 'ln1_b', 'w_ff1', 'w_ff2', 'ln2_g', 'ln2_b']
TWIN_DIFF_INPUT = 'x'
TWIN_INPUTS = ['x', 'w_in', 'b_gate', 'g_q_a', 'w_uq', 'g_kv_a', 'w_ukv', 'w_o_mla', 'w_o_dil', 'w_out', 'ln1_g', 'ln1_b', 'w_ff1', 'w_ff2', 'ln2_g', 'ln2_b', 'loss_target', 'm_w_in', 'm_b_gate', 'm_g_q_a', 'm_w_uq', 'm_g_kv_a', 'm_w_ukv', 'm_w_o_mla', 'm_w_o_dil', 'm_w_out', 'm_ln1_g', 'm_ln1_b', 'm_w_ff1', 'm_w_ff2', 'm_ln2_g', 'm_ln2_b', 'v_w_in', 'v_b_gate', 'v_g_q_a', 'v_w_uq', 'v_g_kv_a', 'v_w_ukv', 'v_w_o_mla', 'v_w_o_dil', 'v_w_out', 'v_ln1_g', 'v_ln1_b', 'v_w_ff1', 'v_w_ff2', 'v_ln2_g', 'v_ln2_b']
TWIN_OUTPUTS = ['loss', 'grad_x', 'grad_w_in', 'grad_b_gate', 'grad_g_q_a', 'grad_w_uq', 'grad_g_kv_a', 'grad_w_ukv', 'grad_w_o_mla', 'grad_w_o_dil', 'grad_w_out', 'grad_ln1_g', 'grad_ln1_b', 'grad_w_ff1', 'grad_w_ff2', 'grad_ln2_g', 'grad_ln2_b', 'delta_w_in', 'delta_b_gate', 'delta_g_q_a', 'delta_w_uq', 'delta_g_kv_a', 'delta_w_ukv', 'delta_w_o_mla', 'delta_w_o_dil', 'delta_w_out', 'delta_ln1_g', 'delta_ln1_b', 'delta_w_ff1', 'delta_w_ff2', 'delta_ln2_g', 'delta_ln2_b', 'new_m_w_in', 'new_m_b_gate', 'new_m_g_q_a', 'new_m_w_uq', 'new_m_g_kv_a', 'new_m_w_ukv', 'new_m_w_o_mla', 'new_m_w_o_dil', 'new_m_w_out', 'new_m_ln1_g', 'new_m_ln1_b', 'new_m_w_ff1', 'new_m_w_ff2', 'new_m_ln2_g', 'new_m_ln2_b', 'new_v_w_in', 'new_v_b_gate', 'new_v_g_q_a', 'new_v_w_uq', 'new_v_g_kv_a', 'new_v_w_ukv', 'new_v_w_o_mla', 'new_v_w_o_dil', 'new_v_w_out', 'new_v_ln1_g', 'new_v_ln1_b', 'new_v_w_ff1', 'new_v_w_ff2', 'new_v_ln2_g', 'new_v_ln2_b']
TWIN_LEAF_KINDS = {'loss': 'loss', 'grad_x': 'grad_x', 'grad_w_in': 'grad_w', 'grad_b_gate': 'grad_w', 'grad_g_q_a': 'grad_w', 'grad_w_uq': 'grad_w', 'grad_g_kv_a': 'grad_w', 'grad_w_ukv': 'grad_w', 'grad_w_o_mla': 'grad_w', 'grad_w_o_dil': 'grad_w', 'grad_w_out': 'grad_w', 'grad_ln1_g': 'grad_w', 'grad_ln1_b': 'grad_w', 'grad_w_ff1': 'grad_w', 'grad_w_ff2': 'grad_w', 'grad_ln2_g': 'grad_w', 'grad_ln2_b': 'grad_w', 'delta_w_in': 'delta_w', 'delta_b_gate': 'delta_w', 'delta_g_q_a': 'delta_w', 'delta_w_uq': 'delta_w', 'delta_g_kv_a': 'delta_w', 'delta_w_ukv': 'delta_w', 'delta_w_o_mla': 'delta_w', 'delta_w_o_dil': 'delta_w', 'delta_w_out': 'delta_w', 'delta_ln1_g': 'delta_w', 'delta_ln1_b': 'delta_w', 'delta_w_ff1': 'delta_w', 'delta_w_ff2': 'delta_w', 'delta_ln2_g': 'delta_w', 'delta_ln2_b': 'delta_w', 'new_m_w_in': 'new_m', 'new_m_b_gate': 'new_m', 'new_m_g_q_a': 'new_m', 'new_m_w_uq': 'new_m', 'new_m_g_kv_a': 'new_m', 'new_m_w_ukv': 'new_m', 'new_m_w_o_mla': 'new_m', 'new_m_w_o_dil': 'new_m', 'new_m_w_out': 'new_m', 'new_m_ln1_g': 'new_m', 'new_m_ln1_b': 'new_m', 'new_m_w_ff1': 'new_m', 'new_m_w_ff2': 'new_m', 'new_m_ln2_g': 'new_m', 'new_m_ln2_b': 'new_m', 'new_v_w_in': 'new_v', 'new_v_b_gate': 'new_v', 'new_v_g_q_a': 'new_v', 'new_v_w_uq': 'new_v', 'new_v_g_kv_a': 'new_v', 'new_v_w_ukv': 'new_v', 'new_v_w_o_mla': 'new_v', 'new_v_w_o_dil': 'new_v', 'new_v_w_out': 'new_v', 'new_v_ln1_g': 'new_v', 'new_v_ln1_b': 'new_v', 'new_v_w_ff1': 'new_v', 'new_v_w_ff2': 'new_v', 'new_v_ln2_g': 'new_v', 'new_v_ln2_b': 'new_v'}


def _forward(args):
    return _fwd_reference(*[args[k] for k in FWD_PARAMS])


def _output_shape():
    out = _jax.eval_shape(lambda: _forward(_fwd_setup_inputs(0)))
    return out.shape, out.dtype

N_MICROBATCH = 1
ADAM_LR = 0.001
ADAM_B1 = 0.9
ADAM_B2 = 0.999
ADAM_EPS = 1e-08
ADAM_WD = 0.01
ADAM_STEP = 10
PER_EXAMPLE_BATCH_AXIS = {'x': 0, 'loss_target': 0}
SHARED_INPUTS = []
_WEIGHT_DTYPES = {'w_in': _jnp.float32, 'b_gate': _jnp.float32, 'g_q_a': _jnp.float32, 'w_uq': _jnp.float32, 'g_kv_a': _jnp.float32, 'w_ukv': _jnp.float32, 'w_o_mla': _jnp.float32, 'w_o_dil': _jnp.float32, 'w_out': _jnp.float32, 'ln1_g': _jnp.float32, 'ln1_b': _jnp.float32, 'w_ff1': _jnp.float32, 'w_ff2': _jnp.float32, 'ln2_g': _jnp.float32, 'ln2_b': _jnp.float32}
MOMENT_SCALE = {'w_in': 9.250682e-03, 'b_gate': 3.555829e-03, 'g_q_a': 8.369401e-03, 'w_uq': 6.019568e-03, 'g_kv_a': 1.600812e-02, 'w_ukv': 7.606682e-03, 'w_o_mla': 1.061081e-02, 'w_o_dil': 1.899286e-02, 'w_out': 2.175607e-02, 'ln1_g': 9.988610e-01, 'ln1_b': 4.627998e-01, 'w_ff1': 3.653354e-02, 'w_ff2': 8.189108e-02, 'ln2_g': 3.201204e+01, 'ln2_b': 2.660276e+00}


def _to_microbatches(a, axis):
    t = _jnp.moveaxis(a, axis, 0)
    t = t.reshape((N_MICROBATCH, t.shape[0] // N_MICROBATCH) + t.shape[1:])
    return _jnp.moveaxis(t, 1, axis + 1)


def setup_inputs(seed: int = 0) -> dict:
    inp = _fwd_setup_inputs(seed)
    key = _jax.random.fold_in(_jax.random.key(seed), 7919)
    shape, _ = _output_shape()
    out = dict(inp)
    out["loss_target"] = _jax.random.normal(_jax.random.fold_in(key, 0), shape, _jnp.float32)
    for i, name in enumerate(TWIN_WEIGHTS):
        w = inp[name].astype(_jnp.float32)
        if MOMENT_SCALE is None:
            s = _jnp.sqrt(_jnp.mean(_jnp.square(w)) + 1e-30)
        else:
            s = MOMENT_SCALE[name]
        km, kv = _jax.random.split(_jax.random.fold_in(key, i + 1))
        out[name] = w
        out["m_" + name] = s * _jax.random.normal(km, w.shape, _jnp.float32)
        out["v_" + name] = (s * s) * _jax.random.uniform(kv, w.shape, _jnp.float32, 0.5, 1.5)
    if N_MICROBATCH > 1:
        for name, axis in PER_EXAMPLE_BATCH_AXIS.items():
            out[name] = _to_microbatches(out[name], axis)
    return {'x': out['x'], 'w_in': out['w_in'], 'b_gate': out['b_gate'], 'g_q_a': out['g_q_a'], 'w_uq': out['w_uq'], 'g_kv_a': out['g_kv_a'], 'w_ukv': out['w_ukv'], 'w_o_mla': out['w_o_mla'], 'w_o_dil': out['w_o_dil'], 'w_out': out['w_out'], 'ln1_g': out['ln1_g'], 'ln1_b': out['ln1_b'], 'w_ff1': out['w_ff1'], 'w_ff2': out['w_ff2'], 'ln2_g': out['ln2_g'], 'ln2_b': out['ln2_b'], 'loss_target': out['loss_target'], 'm_w_in': out['m_w_in'], 'm_b_gate': out['m_b_gate'], 'm_g_q_a': out['m_g_q_a'], 'm_w_uq': out['m_w_uq'], 'm_g_kv_a': out['m_g_kv_a'], 'm_w_ukv': out['m_w_ukv'], 'm_w_o_mla': out['m_w_o_mla'], 'm_w_o_dil': out['m_w_o_dil'], 'm_w_out': out['m_w_out'], 'm_ln1_g': out['m_ln1_g'], 'm_ln1_b': out['m_ln1_b'], 'm_w_ff1': out['m_w_ff1'], 'm_w_ff2': out['m_w_ff2'], 'm_ln2_g': out['m_ln2_g'], 'm_ln2_b': out['m_ln2_b'], 'v_w_in': out['v_w_in'], 'v_b_gate': out['v_b_gate'], 'v_g_q_a': out['v_g_q_a'], 'v_w_uq': out['v_w_uq'], 'v_g_kv_a': out['v_g_kv_a'], 'v_w_ukv': out['v_w_ukv'], 'v_w_o_mla': out['v_w_o_mla'], 'v_w_o_dil': out['v_w_o_dil'], 'v_w_out': out['v_w_out'], 'v_ln1_g': out['v_ln1_g'], 'v_ln1_b': out['v_ln1_b'], 'v_w_ff1': out['v_w_ff1'], 'v_w_ff2': out['v_w_ff2'], 'v_ln2_g': out['v_ln2_g'], 'v_ln2_b': out['v_ln2_b']}


def _loss(weights, diff, rest, loss_target):
    with _jax.named_scope("forward"):
        args = {**rest, TWIN_DIFF_INPUT: diff, **{k: w.astype(_WEIGHT_DTYPES[k]) for k, w in weights.items()}}
        y = _forward(args)
    with _jax.named_scope("loss_head"):
        err = _jnp.square(y.astype(_jnp.float32) - loss_target)
        return 0.5 * _jnp.sum(_jnp.mean(err, axis=-1)) if err.ndim else 0.5 * err


def _adamw(w, g, m, v):
    m = ADAM_B1 * m + (1.0 - ADAM_B1) * g
    v = ADAM_B2 * v + (1.0 - ADAM_B2) * _jnp.square(g)
    m_hat = m / (1.0 - ADAM_B1 ** ADAM_STEP)
    v_hat = v / (1.0 - ADAM_B2 ** ADAM_STEP)
    delta = -ADAM_LR * (m_hat / (_jnp.sqrt(v_hat) + ADAM_EPS) + ADAM_WD * w)
    return delta, m, v


def reference(x, w_in, b_gate, g_q_a, w_uq, g_kv_a, w_ukv, w_o_mla, w_o_dil, w_out, ln1_g, ln1_b, w_ff1, w_ff2, ln2_g, ln2_b, loss_target, m_w_in, m_b_gate, m_g_q_a, m_w_uq, m_g_kv_a, m_w_ukv, m_w_o_mla, m_w_o_dil, m_w_out, m_ln1_g, m_ln1_b, m_w_ff1, m_w_ff2, m_ln2_g, m_ln2_b, v_w_in, v_b_gate, v_g_q_a, v_w_uq, v_g_kv_a, v_w_ukv, v_w_o_mla, v_w_o_dil, v_w_out, v_ln1_g, v_ln1_b, v_w_ff1, v_w_ff2, v_ln2_g, v_ln2_b):
    given = dict(x=x, w_in=w_in, b_gate=b_gate, g_q_a=g_q_a, w_uq=w_uq, g_kv_a=g_kv_a, w_ukv=w_ukv, w_o_mla=w_o_mla, w_o_dil=w_o_dil, w_out=w_out, ln1_g=ln1_g, ln1_b=ln1_b, w_ff1=w_ff1, w_ff2=w_ff2, ln2_g=ln2_g, ln2_b=ln2_b, loss_target=loss_target, m_w_in=m_w_in, m_b_gate=m_b_gate, m_g_q_a=m_g_q_a, m_w_uq=m_w_uq, m_g_kv_a=m_g_kv_a, m_w_ukv=m_w_ukv, m_w_o_mla=m_w_o_mla, m_w_o_dil=m_w_o_dil, m_w_out=m_w_out, m_ln1_g=m_ln1_g, m_ln1_b=m_ln1_b, m_w_ff1=m_w_ff1, m_w_ff2=m_w_ff2, m_ln2_g=m_ln2_g, m_ln2_b=m_ln2_b, v_w_in=v_w_in, v_b_gate=v_b_gate, v_g_q_a=v_g_q_a, v_w_uq=v_w_uq, v_g_kv_a=v_g_kv_a, v_w_ukv=v_w_ukv, v_w_o_mla=v_w_o_mla, v_w_o_dil=v_w_o_dil, v_w_out=v_w_out, v_ln1_g=v_ln1_g, v_ln1_b=v_ln1_b, v_w_ff1=v_w_ff1, v_w_ff2=v_w_ff2, v_ln2_g=v_ln2_g, v_ln2_b=v_ln2_b)
    weights = {n: given[n] for n in TWIN_WEIGHTS}
    shared = {n: given[n] for n in SHARED_INPUTS}
    per_example = {n: given[n] for n in ['x']}
    grad_fn = _jax.value_and_grad(_loss, argnums=(0, 1))

    def one_microbatch(ex, loss_target):
        ex = dict(ex)
        diff = ex.pop(TWIN_DIFF_INPUT)
        return grad_fn(weights, diff, {**shared, **ex}, loss_target)

    if N_MICROBATCH == 1:
        loss, (grad_w, grad_x) = one_microbatch(per_example, given["loss_target"])
    else:
        def body(carry, xs):
            loss_sum, grad_sum = carry
            l_k, (gw_k, gx_k) = one_microbatch(xs[0], xs[1])
            with _jax.named_scope("update"):
                return (loss_sum + l_k, _jax.tree.map(_jnp.add, grad_sum, gw_k)), gx_k

        init = (_jnp.zeros((), _jnp.float32), _jax.tree.map(_jnp.zeros_like, weights))
        (loss, grad_w), grad_x = _jax.lax.scan(body, init, (per_example, given["loss_target"]))
    with _jax.named_scope("update"):
        delta_w, new_m, new_v = {}, {}, {}
        for n in TWIN_WEIGHTS:
            delta_w[n], new_m[n], new_v[n] = _adamw(weights[n], grad_w[n], given["m_" + n], given["v_" + n])
    return (loss, grad_x, *[grad_w[n] for n in TWIN_WEIGHTS], *[delta_w[n] for n in TWIN_WEIGHTS],
            *[new_m[n] for n in TWIN_WEIGHTS], *[new_v[n] for n in TWIN_WEIGHTS])
```

```python
import functools
import math

import jax
import jax.numpy as jnp
from jax import lax
from jax.experimental import pallas as pl
from jax.experimental.pallas import tpu as pltpu

F32 = jnp.float32
BF16 = jnp.bfloat16
I32 = jnp.int32

D_MODEL = 1024
N_HEADS = 8
NOPE = 64
ROPE = 32
HEAD_V = 64
Q_LORA = 384
KV_LORA = 256
DIL_WIDTH = 512
D_FF = 4096
ROPE_THETA = 10000.0
LN_EPS = 1e-5
RMS_EPS = 1e-6
NEG = -1e30
ALPHA = 2.0 ** 0.25
MLA_SCALE = (NOPE + ROPE) ** -0.5
DIL_SCALE = 64 ** -0.5
ADAM_LR, ADAM_B1, ADAM_B2, ADAM_EPS, ADAM_WD, ADAM_STEP = 0.001, 0.9, 0.999, 1e-08, 0.01, 10

LANES = 128
PAIR_W = 256
N_PAIRS = N_HEADS // 2
LOW_W = 768
IN_EXT = LOW_W + 3 * DIL_WIDTH + 2 * D_MODEL
N_DEV = 8
TOKEN_TILE = 256
ATTN_TILE = 256
FF_TILE = 1024
VMEM_LIMIT = 56 << 20
PACK_ROWS = 1024

MESH = pl.DeviceIdType.MESH
ANY = pl.BlockSpec(memory_space=pl.ANY)


def _cp(*sem):
    return pltpu.CompilerParams(dimension_semantics=sem or None, vmem_limit_bytes=VMEM_LIMIT)


def _full(shape):
    nd = len(shape)
    return pl.BlockSpec(shape, lambda *_: (0,) * nd)


def _rows(tm, width):
    return pl.BlockSpec((tm, width), lambda i, *_: (i, 0))


def _dot(a, b):
    return jnp.dot(a, b, preferred_element_type=F32)


def _dot_nt(a, b):
    return lax.dot_general(a, b, (((1,), (1,)), ((), ())), preferred_element_type=F32)


def _dot_tn(a, b):
    return lax.dot_general(a, b, (((0,), (0,)), ((), ())), preferred_element_type=F32)


def _sigmoid(z):
    return 1.0 / (1.0 + jnp.exp(-z))


def _all_gather(shard, name):
    rows, cols = shard.shape

    def body(x_ref, out_ref, send_sems, recv_sems, local_sem):
        x, y, c = lax.axis_index("x"), lax.axis_index("y"), lax.axis_index("c")
        me, sibling = (x, y, c), (x, y, 1 - c)
        chips = [(1 - x, y), (x, 1 - y), (1 - x, 1 - y)]

        def slot(px, py, pc):
            return out_ref.at[4 * px + 2 * py + pc]

        def copy(k, block, to, src=None):
            return pltpu.make_async_remote_copy(
                src_ref=slot(*block) if src is None else src, dst_ref=slot(*block),
                send_sem=send_sems.at[k], recv_sem=recv_sems.at[k], device_id=to, device_id_type=MESH)

        mine = pltpu.make_async_copy(x_ref, slot(*me), local_sem)
        mine.start()
        first = [copy(0, me, sibling, src=x_ref)]
        first += [copy(1 + j, me, (*chip, c), src=x_ref) for j, chip in enumerate(chips)]
        for cp in first:
            cp.start()
        passed = [copy(4 + j, (*chip, c), sibling) for j, chip in enumerate(chips)]
        for j, chip in enumerate(chips):
            copy(1 + j, (*chip, c), me).wait_recv()
            passed[j].start()
        copy(0, sibling, me).wait_recv()
        for j, chip in enumerate(chips):
            copy(4 + j, (*chip, 1 - c), me).wait_recv()
        for cp in first + passed:
            cp.wait_send()
        mine.wait()

    return pl.pallas_call(
        body, name=name, out_shape=jax.ShapeDtypeStruct((N_DEV, rows, cols), shard.dtype),
        in_specs=[ANY], out_specs=ANY,
        scratch_shapes=[pltpu.SemaphoreType.DMA((7,)), pltpu.SemaphoreType.DMA((7,)), pltpu.SemaphoreType.DMA(())],
    )(shard)


def _exchange(src, flips, name):
    n, rows, cols = src.shape

    def body(src_ref, dst_ref, send_sems, recv_sems):
        x, y, c = lax.axis_index("x"), lax.axis_index("y"), lax.axis_index("c")
        copies = []
        for k, (fx, fy, fc) in enumerate(flips):
            peer = (1 - x if fx else x, 1 - y if fy else y, 1 - c if fc else c)
            copies.append(pltpu.make_async_remote_copy(
                src_ref=src_ref.at[k], dst_ref=dst_ref.at[k], send_sem=send_sems.at[k], recv_sem=recv_sems.at[k],
                device_id=peer, device_id_type=MESH))
        for cp in copies:
            cp.start()
        for cp in copies:
            cp.wait_send()
        for cp in copies:
            cp.wait_recv()

    return pl.pallas_call(
        body, name=name, out_shape=jax.ShapeDtypeStruct((n, rows, cols), src.dtype),
        in_specs=[ANY], out_specs=ANY,
        scratch_shapes=[pltpu.SemaphoreType.DMA((n,)), pltpu.SemaphoreType.DMA((n,))],
    )(src)


def _pair_sum(a, b, name):
    _, rows, cols = a.shape
    tr = PACK_ROWS

    def body(a_ref, b_ref, own_ref, rest_ref):
        s = a_ref[...].astype(F32) + b_ref[...].astype(F32)
        own_ref[...] = s[0]
        rest_ref[...] = s[1:].astype(BF16)

    return pl.pallas_call(
        body, name=name, grid=(rows // tr,),
        out_shape=(jax.ShapeDtypeStruct((rows, cols), F32), jax.ShapeDtypeStruct((3, rows, cols), BF16)),
        in_specs=[pl.BlockSpec((4, tr, cols), lambda i: (0, i, 0))] * 2,
        out_specs=(pl.BlockSpec((tr, cols), lambda i: (i, 0)), pl.BlockSpec((3, tr, cols), lambda i: (0, i, 0))),
        compiler_params=_cp("parallel"),
    )(a, b)


def _final_sum(own, got, name):
    rows, cols = own.shape
    tr = PACK_ROWS

    def body(own_ref, got_ref, out_ref):
        g = got_ref[...].astype(F32)
        out_ref[...] = ((own_ref[...] + g[0]) + g[1]) + g[2]

    return pl.pallas_call(
        body, name=name, grid=(rows // tr,), out_shape=jax.ShapeDtypeStruct((rows, cols), F32),
        in_specs=[pl.BlockSpec((tr, cols), lambda i: (i, 0)), pl.BlockSpec((3, tr, cols), lambda i: (0, i, 0))],
        out_specs=pl.BlockSpec((tr, cols), lambda i: (i, 0)),
        compiler_params=_cp("parallel"),
    )(own, got)


def _reduce_scatter(packed):
    x, y, c = lax.axis_index("x"), lax.axis_index("y"), lax.axis_index("c")
    by_chip = packed.reshape(4, 2, *packed.shape[1:])
    mine, theirs = [], []
    for fx, fy in ((0, 0), (0, 1), (1, 0), (1, 1)):
        chip = 2 * jnp.where(fx, 1 - x, x) + jnp.where(fy, 1 - y, y)
        both = lax.dynamic_index_in_dim(by_chip, chip, 0, keepdims=False)
        mine.append(lax.dynamic_index_in_dim(both, c, 0, keepdims=False))
        theirs.append(lax.dynamic_index_in_dim(both, 1 - c, 0, keepdims=False))
    from_sibling = _exchange(jnp.stack(theirs), [(0, 0, 1)] * 4, "rs_sibling_exchange")
    own, rest = _pair_sum(jnp.stack(mine), from_sibling, "rs_pair_sum")
    got = _exchange(rest, [(0, 1, 0), (1, 0, 0), (1, 1, 0)], "rs_chip_exchange")
    return _final_sum(own, got, "rs_final_sum")


def _head_lanes(width, h):
    lane = lax.broadcasted_iota(I32, (1, width), 1)
    if width == LANES:
        return (lane >= 64 * h) & (lane < 64 * h + 64)
    nope = (lane >= NOPE * h) & (lane < NOPE * h + NOPE)
    rope = (lane >= 2 * NOPE + ROPE * h) & (lane < 2 * NOPE + ROPE * h + ROPE)
    return nope | rope


def _score_bias(delta, dilated):
    if not dilated:
        return jnp.where(delta >= 0, 0.0, NEG)
    mult = ((delta <= 128).astype(I32) + (((delta & 3) == 0) & (delta <= 512)).astype(I32)
            + ((delta & 15) == 0).astype(I32))
    logm = jnp.where(mult == 3, math.log(3.0), jnp.where(mult == 2, math.log(2.0), 0.0))
    return jnp.where((delta >= 0) & (mult > 0), logm, NEG)


def _attn_fwd(q, k, v, slopes, *, batch, seq, width, col0, dilated, scale, name):
    t = min(ATTN_TILE, seq)
    nq = seq // t
    cq, ck, cv = col0

    def body(q_ref, k_ref, v_ref, sl_ref, o_ref, lse_ref):
        i = pl.program_id(2)
        q2 = q_ref[...]
        qh = [jnp.where(_head_lanes(width, h), q2, jnp.zeros_like(q2)) for h in (0, 1)]
        vlane = [_head_lanes(LANES, h) for h in (0, 1)]
        d0 = lax.broadcasted_iota(I32, (t, t), 0) - lax.broadcasted_iota(I32, (t, t), 1)
        sl = [sl_ref[0:1, 0:1], sl_ref[0:1, 64:65]]

        def step(j, carry):
            m0, l0, m1, l1, acc = carry
            ks = pl.multiple_of(j * t, t)
            kj = k_ref[pl.ds(ks, t), :]
            vj = v_ref[pl.ds(ks, t), :]
            delta = d0 + (i - j) * t
            base = _score_bias(delta, dilated)
            dist = delta.astype(F32)
            new, alphas, pv = [], [], []
            for h, (m, l) in enumerate(((m0, l0), (m1, l1))):
                s = _dot_nt(qh[h], kj) * scale + base
                if dilated:
                    s = s - sl[h] * dist
                m_new = jnp.maximum(m, jnp.max(s, axis=1, keepdims=True))
                a = jnp.exp(m - m_new)
                p = jnp.exp(s - m_new)
                new += [m_new, a * l + jnp.sum(p, axis=1, keepdims=True)]
                alphas.append(a)
                pv.append(_dot(p.astype(BF16), jnp.where(vlane[h], vj, jnp.zeros_like(vj))))
            acc = jnp.where(vlane[0], alphas[0], alphas[1]) * acc + pv[0] + pv[1]
            return (*new, acc)

        col = jnp.full((t, 1), NEG, F32)
        zero = jnp.zeros((t, 1), F32)
        m0, l0, m1, l1, acc = lax.fori_loop(0, i + 1, step, (col, zero, col, zero, jnp.zeros((t, LANES), F32)))
        o_ref[...] = (acc * jnp.where(vlane[0], 1.0 / l0, 1.0 / l1)).astype(BF16)
        lse_ref[...] = jnp.where(vlane[0], m0 + jnp.log(l0), m1 + jnp.log(l1))

    return pl.pallas_call(
        body, name=name, grid=(batch, N_PAIRS, nq),
        out_shape=(jax.ShapeDtypeStruct((batch * seq, DIL_WIDTH), BF16), jax.ShapeDtypeStruct((batch * seq, DIL_WIDTH), F32)),
        in_specs=[pl.BlockSpec((t, width), lambda b, p, i: (b * nq + i, cq + p)),
                  pl.BlockSpec((seq, width), lambda b, p, i: (b, ck + p)),
                  pl.BlockSpec((seq, LANES), lambda b, p, i: (b, cv + p)),
                  pl.BlockSpec((None, 8, LANES), lambda b, p, i: (p, 0, 0))],
        out_specs=(pl.BlockSpec((t, LANES), lambda b, p, i: (b * nq + i, p)),
                   pl.BlockSpec((t, LANES), lambda b, p, i: (b * nq + i, p))),
        compiler_params=_cp("parallel", "parallel", "arbitrary"),
    )(q, k, v, slopes)


def _attn_bwd(q, k, v, o, do, lse, slopes, *, batch, seq, width, col0, dilated, scale, name):
    t = min(ATTN_TILE, seq)
    nq = seq // t
    cq, ck, cv = col0

    def body(q_ref, k_ref, v_ref, o_ref, do_ref, lse_ref, sl_ref, dq_ref, dk_ref, dv_ref, dq_acc, dk_acc, dv_acc, rowdot):
        wlane = [_head_lanes(width, h) for h in (0, 1)]
        vlane = [_head_lanes(LANES, h) for h in (0, 1)]
        d0 = lax.broadcasted_iota(I32, (t, t), 0) - lax.broadcasted_iota(I32, (t, t), 1)
        sl = [sl_ref[0:1, 0:1], sl_ref[0:1, 64:65]]
        prod = do_ref[...].astype(F32) * o_ref[...].astype(F32)
        r0 = jnp.sum(jnp.where(vlane[0], prod, 0.0), axis=1, keepdims=True)
        r1 = jnp.sum(jnp.where(vlane[1], prod, 0.0), axis=1, keepdims=True)
        rowdot[...] = jnp.where(vlane[0], r0, r1)
        dq_acc[...] = jnp.zeros_like(dq_acc)

        def k_tile(j, _):
            ks = pl.multiple_of(j * t, t)
            kj = k_ref[pl.ds(ks, t), :]
            vj = v_ref[pl.ds(ks, t), :]
            kh = [jnp.where(wlane[h], kj, jnp.zeros_like(kj)) for h in (0, 1)]
            dk_acc[...] = jnp.zeros_like(dk_acc)
            dv_acc[...] = jnp.zeros_like(dv_acc)

            def q_tile(i, _):
                qs = pl.multiple_of(i * t, t)
                qi = q_ref[pl.ds(qs, t), :]
                doi = do_ref[pl.ds(qs, t), :]
                lse_i = lse_ref[pl.ds(qs, t), :]
                rd_i = rowdot[pl.ds(qs, t), :]
                delta = d0 + (i - j) * t
                base = _score_bias(delta, dilated)
                dist = delta.astype(F32)
                dq_i = jnp.zeros((t, width), F32)
                for h in (0, 1):
                    qih = jnp.where(wlane[h], qi, jnp.zeros_like(qi))
                    doih = jnp.where(vlane[h], doi, jnp.zeros_like(doi))
                    s = _dot_nt(qih, kj) * scale + base
                    if dilated:
                        s = s - sl[h] * dist
                    p = jnp.exp(s - lse_i[:, 64 * h:64 * h + 1])
                    dp = _dot_nt(doih, vj)
                    ds = (p * (dp - rd_i[:, 64 * h:64 * h + 1]) * scale).astype(BF16)
                    dv_acc[...] += _dot_tn(p.astype(BF16), doih)
                    dk_acc[...] += _dot_tn(ds, qih)
                    dq_i = dq_i + _dot(ds, kh[h])
                dq_acc[pl.ds(qs, t), :] += dq_i
                return 0

            lax.fori_loop(j, nq, q_tile, 0)
            dk_ref[pl.ds(ks, t), :] = dk_acc[...].astype(BF16)
            dv_ref[pl.ds(ks, t), :] = dv_acc[...].astype(BF16)
            return 0

        lax.fori_loop(0, nq, k_tile, 0)
        dq_ref[...] = dq_acc[...].astype(BF16)

    tokens = batch * seq
    return pl.pallas_call(
        body, name=name, grid=(batch, N_PAIRS),
        out_shape=(jax.ShapeDtypeStruct((tokens, N_PAIRS * width), BF16), jax.ShapeDtypeStruct((tokens, N_PAIRS * width), BF16),
                   jax.ShapeDtypeStruct((tokens, DIL_WIDTH), BF16)),
        in_specs=[pl.BlockSpec((seq, width), lambda b, p: (b, cq + p)),
                  pl.BlockSpec((seq, width), lambda b, p: (b, ck + p)),
                  pl.BlockSpec((seq, LANES), lambda b, p: (b, cv + p)),
                  pl.BlockSpec((seq, LANES), lambda b, p: (b, p)),
                  pl.BlockSpec((seq, LANES), lambda b, p: (b, p)),
                  pl.BlockSpec((seq, LANES), lambda b, p: (b, p)),
                  pl.BlockSpec((None, 8, LANES), lambda b, p: (p, 0, 0))],
        out_specs=(pl.BlockSpec((seq, width), lambda b, p: (b, p)),
                   pl.BlockSpec((seq, width), lambda b, p: (b, p)),
                   pl.BlockSpec((seq, LANES), lambda b, p: (b, p))),
        scratch_shapes=[pltpu.VMEM((seq, width), F32), pltpu.VMEM((t, width), F32), pltpu.VMEM((t, LANES), F32),
                        pltpu.VMEM((seq, LANES), F32)],
        compiler_params=_cp("parallel", "parallel"),
    )(q, k, v, o, do, lse, slopes)


def _rms(xf, g):
    r = lax.rsqrt(jnp.mean(xf * xf, axis=1, keepdims=True) + RMS_EPS)
    return xf * r * g, r


def _rms_bwd(dy, xf, r, g):
    gy = dy * g
    dx = r * gy - xf * (r * r * r) * jnp.mean(gy * xf, axis=1, keepdims=True)
    return dx, dy * xf * r


def _ln_bwd(dy, xhat, rstd, g):
    dxh = dy * g
    return rstd * (dxh - jnp.mean(dxh, axis=1, keepdims=True) - xhat * jnp.mean(dxh * xhat, axis=1, keepdims=True))


def _fwd_proj(xb, w_in_ext, w1, w2, wk_ext, wv, e128, g_q, g_kv, cext, sext, cs128, *, seq):
    tokens = xb.shape[0]
    tm = min(TOKEN_TILE, seq)
    ns = seq // tm

    def body(x_ref, win_ref, w1_ref, w2_ref, wk_ref, wv_ref, e_ref, gq_ref, gkv_ref, c_ref, s_ref, cs_ref,
             low_ref, gates_ref, qkvd_ref, qp_ref, kp_ref, vm_ref, qn_ref, kvn_ref):
        xt = x_ref[...]
        low = _dot(xt, win_ref[:, 0:LOW_W])
        low_ref[...] = low
        qkvd_ref[...] = _dot(xt, win_ref[:, LOW_W:LOW_W + 3 * DIL_WIDTH]).astype(BF16)
        gates_ref[...] = _dot(xt, win_ref[:, LOW_W + 3 * DIL_WIDTH:])
        qn = _rms(low[:, 0:Q_LORA], gq_ref[...])[0].astype(BF16)
        kvn = _rms(low[:, Q_LORA:Q_LORA + KV_LORA], gkv_ref[...])[0].astype(BF16)
        qn_ref[...] = qn
        kvn_ref[...] = kvn
        qp_ref[...] = (_dot(qn, w1_ref[...]) * c_ref[...] + _dot(qn, w2_ref[...]) * s_ref[...]).astype(BF16)
        kr = low[:, Q_LORA + KV_LORA:] * cs_ref[...]
        kr = kr + pltpu.roll(kr, LANES - ROPE, 1)
        lane = lax.broadcasted_iota(I32, kr.shape, 1)
        kr = jnp.where(lane < ROPE, kr, 0.0).astype(BF16)
        kp_ref[...] = (_dot(kvn, wk_ref[...]) + _dot(kr, e_ref[...])).astype(BF16)
        vm_ref[...] = _dot(kvn, wv_ref[...]).astype(BF16)

    n_gates = 2 * D_MODEL
    outs = [(LOW_W, F32), (n_gates, F32), (3 * DIL_WIDTH, BF16), (N_PAIRS * PAIR_W, BF16), (N_PAIRS * PAIR_W, BF16),
            (DIL_WIDTH, BF16), (Q_LORA, BF16), (KV_LORA, BF16)]
    return pl.pallas_call(
        body, name="fwd_proj", grid=(tokens // tm,),
        out_shape=tuple(jax.ShapeDtypeStruct((tokens, w), dt) for w, dt in outs),
        in_specs=[_rows(tm, D_MODEL), _full(w_in_ext.shape), _full(w1.shape), _full(w2.shape), _full(wk_ext.shape),
                  _full(wv.shape), _full(e128.shape), _full(g_q.shape), _full(g_kv.shape),
                  pl.BlockSpec((tm, N_PAIRS * PAIR_W), lambda i: (i % ns, 0)),
                  pl.BlockSpec((tm, N_PAIRS * PAIR_W), lambda i: (i % ns, 0)),
                  pl.BlockSpec((tm, LANES), lambda i: (i % ns, 0))],
        out_specs=tuple(_rows(tm, w) for w, _ in outs),
        compiler_params=_cp("parallel"),
    )(xb, w_in_ext, w1, w2, wk_ext, wv, e128, g_q, g_kv, cext, sext, cs128)


def _fwd_mix(o_a, o_b, gates, x, b_gate, w_oa, w_ob, w_out, ln_g, ln_b, *, seq):
    tokens = x.shape[0]
    tm = min(TOKEN_TILE, seq)

    def body(oa_ref, ob_ref, gt_ref, x_ref, bg_ref, woa_ref, wob_ref, wout_ref, g_ref, b_ref,
             h_ref, hb_ref, xhat_ref, rstd_ref, ya_ref, yb_ref, mix_ref):
        ya = _dot(oa_ref[...], woa_ref[...])
        yb = _dot(ob_ref[...], wob_ref[...])
        g0 = _sigmoid(gt_ref[:, 0:D_MODEL] + bg_ref[0:1, :])
        g1 = _sigmoid(gt_ref[:, D_MODEL:] + bg_ref[1:2, :])
        mix = (g0 * ya + g1 * yb).astype(BF16)
        z = ALPHA * x_ref[...] + _dot(mix, wout_ref[...])
        zc = z - jnp.mean(z, axis=1, keepdims=True)
        rstd = lax.rsqrt(jnp.mean(zc * zc, axis=1, keepdims=True) + LN_EPS)
        xhat = zc * rstd
        h = xhat * g_ref[...] + b_ref[...]
        h_ref[...] = h
        hb_ref[...] = h.astype(BF16)
        xhat_ref[...] = xhat
        rstd_ref[...] = jnp.broadcast_to(rstd, (tm, LANES))
        ya_ref[...] = ya.astype(BF16)
        yb_ref[...] = yb.astype(BF16)
        mix_ref[...] = mix

    outs = [(D_MODEL, F32), (D_MODEL, BF16), (D_MODEL, F32), (LANES, F32), (D_MODEL, BF16), (D_MODEL, BF16), (D_MODEL, BF16)]
    return pl.pallas_call(
        body, name="fwd_mix", grid=(tokens // tm,),
        out_shape=tuple(jax.ShapeDtypeStruct((tokens, w), dt) for w, dt in outs),
        in_specs=[_rows(tm, DIL_WIDTH), _rows(tm, DIL_WIDTH), _rows(tm, 2 * D_MODEL), _rows(tm, D_MODEL),
                  _full(b_gate.shape), _full(w_oa.shape), _full(w_ob.shape), _full(w_out.shape),
                  _full(ln_g.shape), _full(ln_b.shape)],
        out_specs=tuple(_rows(tm, w) for w, _ in outs),
        compiler_params=_cp("parallel"),
    )(o_a, o_b, gates, x, b_gate, w_oa, w_ob, w_out, ln_g, ln_b)


def _fwd_mlp(hb, h, target, w_ff1, w_ff2, ln_g, ln_b, *, seq):
    tokens = h.shape[0]
    tm = min(2 * TOKEN_TILE, seq)
    tf = FF_TILE
    nf = D_FF // tf

    def body(hb_ref, h_ref, tg_ref, w1_ref, w2_ref, g_ref, b_ref, u_ref, dz_ref, dzb_ref, stat_ref, acc):
        i, j = pl.program_id(0), pl.program_id(1)

        @pl.when((i == 0) & (j == 0))
        def _():
            stat_ref[...] = jnp.zeros_like(stat_ref)

        @pl.when(j == 0)
        def _():
            acc[...] = jnp.zeros_like(acc)

        u = _dot(hb_ref[...], w1_ref[...])
        u_ref[...] = u.astype(BF16)
        a = jnp.square(jnp.maximum(u, 0.0)).astype(BF16)
        acc[...] += _dot(a, w2_ref[...])

        @pl.when(j == nf - 1)
        def _():
            z = ALPHA * h_ref[...] + acc[...]
            zc = z - jnp.mean(z, axis=1, keepdims=True)
            rstd = lax.rsqrt(jnp.mean(zc * zc, axis=1, keepdims=True) + LN_EPS)
            xhat = zc * rstd
            err = xhat * g_ref[...] + b_ref[...] - tg_ref[...]
            dy = err * (1.0 / D_MODEL)
            dz = _ln_bwd(dy, xhat, rstd, g_ref[...])
            dz_ref[...] = dz
            dzb_ref[...] = dz.astype(BF16)
            stat_ref[0:1, :] += jnp.sum(dy * xhat, axis=0, keepdims=True)
            stat_ref[1:2, :] += jnp.sum(dy, axis=0, keepdims=True)
            stat_ref[2:3, :] += jnp.sum(jnp.sum(err * err, axis=1, keepdims=True), axis=0, keepdims=True) * (0.5 / D_MODEL)

    return pl.pallas_call(
        body, name="fwd_mlp", grid=(tokens // tm, nf),
        out_shape=(jax.ShapeDtypeStruct((tokens, D_FF), BF16), jax.ShapeDtypeStruct((tokens, D_MODEL), F32),
                   jax.ShapeDtypeStruct((tokens, D_MODEL), BF16), jax.ShapeDtypeStruct((8, D_MODEL), F32)),
        in_specs=[_rows(tm, D_MODEL), _rows(tm, D_MODEL), _rows(tm, D_MODEL),
                  pl.BlockSpec((D_MODEL, tf), lambda i, j: (0, j)), pl.BlockSpec((tf, D_MODEL), lambda i, j: (j, 0)),
                  _full(ln_g.shape), _full(ln_b.shape)],
        out_specs=(pl.BlockSpec((tm, tf), lambda i, j: (i, j)), _rows(tm, D_MODEL), _rows(tm, D_MODEL), _full((8, D_MODEL))),
        scratch_shapes=[pltpu.VMEM((tm, D_MODEL), F32)],
        compiler_params=_cp("arbitrary", "arbitrary"),
    )(hb, h, target, w_ff1, w_ff2, ln_g, ln_b)


def _bwd_mlp(dz2, dz2b, u, xhat1, rstd1, w_ff1, w_ff2, ln_g, *, seq):
    tokens = dz2.shape[0]
    tm = min(2 * TOKEN_TILE, seq)
    tf = FF_TILE
    nf = D_FF // tf

    def body(dz_ref, dzb_ref, u_ref, xh_ref, rs_ref, w1_ref, w2_ref, g_ref, du_ref, dz1_ref, dz1b_ref, stat_ref, acc):
        i, j = pl.program_id(0), pl.program_id(1)

        @pl.when((i == 0) & (j == 0))
        def _():
            stat_ref[...] = jnp.zeros_like(stat_ref)

        @pl.when(j == 0)
        def _():
            acc[...] = jnp.zeros_like(acc)

        da = _dot_nt(dzb_ref[...], w2_ref[...])
        du = (da * (2.0 * jnp.maximum(u_ref[...].astype(F32), 0.0))).astype(BF16)
        du_ref[...] = du
        acc[...] += _dot_nt(du, w1_ref[...])

        @pl.when(j == nf - 1)
        def _():
            dh = ALPHA * dz_ref[...] + acc[...]
            xhat = xh_ref[...]
            dz1 = _ln_bwd(dh, xhat, rs_ref[:, 0:1], g_ref[...])
            dz1_ref[...] = dz1
            dz1b_ref[...] = dz1.astype(BF16)
            stat_ref[0:1, :] += jnp.sum(dh * xhat, axis=0, keepdims=True)
            stat_ref[1:2, :] += jnp.sum(dh, axis=0, keepdims=True)

    return pl.pallas_call(
        body, name="bwd_mlp", grid=(tokens // tm, nf),
        out_shape=(jax.ShapeDtypeStruct((tokens, D_FF), BF16), jax.ShapeDtypeStruct((tokens, D_MODEL), F32),
                   jax.ShapeDtypeStruct((tokens, D_MODEL), BF16), jax.ShapeDtypeStruct((8, D_MODEL), F32)),
        in_specs=[_rows(tm, D_MODEL), _rows(tm, D_MODEL), pl.BlockSpec((tm, tf), lambda i, j: (i, j)),
                  _rows(tm, D_MODEL), _rows(tm, LANES),
                  pl.BlockSpec((D_MODEL, tf), lambda i, j: (0, j)), pl.BlockSpec((tf, D_MODEL), lambda i, j: (j, 0)),
                  _full(ln_g.shape)],
        out_specs=(pl.BlockSpec((tm, tf), lambda i, j: (i, j)), _rows(tm, D_MODEL), _rows(tm, D_MODEL), _full((8, D_MODEL))),
        scratch_shapes=[pltpu.VMEM((tm, D_MODEL), F32)],
        compiler_params=_cp("arbitrary", "arbitrary"),
    )(dz2, dz2b, u, xhat1, rstd1, w_ff1, w_ff2, ln_g)


def _bwd_mix(dz1b, gates, y_a, y_b, b_gate, w_oa, w_ob, w_out, *, seq):
    tokens = dz1b.shape[0]
    tm = min(TOKEN_TILE, seq)

    def body(dz_ref, gt_ref, ya_ref, yb_ref, bg_ref, woa_ref, wob_ref, wout_ref,
             dgt_ref, dya_ref, dyb_ref, doa_ref, dob_ref, stat_ref):
        @pl.when(pl.program_id(0) == 0)
        def _():
            stat_ref[...] = jnp.zeros_like(stat_ref)

        dmix = _dot_nt(dz_ref[...], wout_ref[...])
        for k, (y_ref, w_ref, dy_ref, do_ref) in enumerate(((ya_ref, woa_ref, dya_ref, doa_ref), (yb_ref, wob_ref, dyb_ref, dob_ref))):
            g = _sigmoid(gt_ref[:, k * D_MODEL:(k + 1) * D_MODEL] + bg_ref[k:k + 1, :])
            dgate = dmix * y_ref[...].astype(F32) * g * (1.0 - g)
            dgt_ref[:, k * D_MODEL:(k + 1) * D_MODEL] = dgate.astype(BF16)
            stat_ref[k:k + 1, :] += jnp.sum(dgate, axis=0, keepdims=True)
            dy = (dmix * g).astype(BF16)
            dy_ref[...] = dy
            do_ref[...] = _dot_nt(dy, w_ref[...]).astype(BF16)

    outs = [(2 * D_MODEL, BF16), (D_MODEL, BF16), (D_MODEL, BF16), (DIL_WIDTH, BF16), (DIL_WIDTH, BF16)]
    return pl.pallas_call(
        body, name="bwd_mix", grid=(tokens // tm,),
        out_shape=tuple(jax.ShapeDtypeStruct((tokens, w), dt) for w, dt in outs) + (jax.ShapeDtypeStruct((8, D_MODEL), F32),),
        in_specs=[_rows(tm, D_MODEL), _rows(tm, 2 * D_MODEL), _rows(tm, D_MODEL), _rows(tm, D_MODEL),
                  _full(b_gate.shape), _full(w_oa.shape), _full(w_ob.shape), _full(w_out.shape)],
        out_specs=tuple(_rows(tm, w) for w, _ in outs) + (_full((8, D_MODEL)),),
        compiler_params=_cp("arbitrary"),
    )(dz1b, gates, y_a, y_b, b_gate, w_oa, w_ob, w_out)


def _bwd_proj(dqp, dkp, dvm, dqkvd, dgates, dz1, low, w_in_ext, w1, w2, wk_ext, wv, e128, g_q, g_kv, cext, sext, cs128, *, seq):
    tokens = dz1.shape[0]
    tm = min(TOKEN_TILE, seq)
    ns = seq // tm

    def body(dqp_ref, dkp_ref, dvm_ref, dqkvd_ref, dgt_ref, dz_ref, low_ref, win_ref, w1_ref, w2_ref, wk_ref, wv_ref,
             e_ref, gq_ref, gkv_ref, c_ref, s_ref, cs_ref, dx_ref, dproj_ref, da_ref, db_ref, stat_ref):
        @pl.when(pl.program_id(0) == 0)
        def _():
            stat_ref[...] = jnp.zeros_like(stat_ref)

        low = low_ref[...]
        dqp = dqp_ref[...].astype(F32)
        d_a = (dqp * c_ref[...]).astype(BF16)
        d_b = (dqp * s_ref[...]).astype(BF16)
        da_ref[...] = d_a
        db_ref[...] = d_b
        q_a = low[:, 0:Q_LORA]
        _, rq = _rms(q_a, gq_ref[...])
        dq_a, gq_terms = _rms_bwd(_dot_nt(d_a, w1_ref[...]) + _dot_nt(d_b, w2_ref[...]), q_a, rq, gq_ref[...])
        kv_a = low[:, Q_LORA:Q_LORA + KV_LORA]
        _, rkv = _rms(kv_a, gkv_ref[...])
        dkp = dkp_ref[...]
        dkv_a, gkv_terms = _rms_bwd(_dot_nt(dkp, wk_ref[...]) + _dot_nt(dvm_ref[...], wv_ref[...]), kv_a, rkv, gkv_ref[...])
        dkr = _dot_nt(dkp, e_ref[...])
        dkr = (dkr + pltpu.roll(dkr, ROPE, 1)) * cs_ref[...]
        stat_ref[0:1, 0:Q_LORA] += jnp.sum(gq_terms, axis=0, keepdims=True)
        stat_ref[1:2, 0:KV_LORA] += jnp.sum(gkv_terms, axis=0, keepdims=True)
        dproj_ref[:, 0:Q_LORA] = dq_a.astype(BF16)
        dproj_ref[:, Q_LORA:Q_LORA + KV_LORA] = dkv_a.astype(BF16)
        dproj_ref[:, Q_LORA + KV_LORA:LOW_W] = dkr.astype(BF16)
        dproj_ref[:, LOW_W:LOW_W + 3 * DIL_WIDTH] = dqkvd_ref[...]
        dproj_ref[:, LOW_W + 3 * DIL_WIDTH:] = dgt_ref[...]
        dx_ref[...] = ALPHA * dz_ref[...] + _dot_nt(dproj_ref[...], win_ref[...])

    wide = N_PAIRS * PAIR_W
    return pl.pallas_call(
        body, name="bwd_proj", grid=(tokens // tm,),
        out_shape=(jax.ShapeDtypeStruct((tokens, D_MODEL), F32), jax.ShapeDtypeStruct((tokens, IN_EXT), BF16),
                   jax.ShapeDtypeStruct((tokens, wide), BF16), jax.ShapeDtypeStruct((tokens, wide), BF16),
                   jax.ShapeDtypeStruct((8, D_MODEL), F32)),
        in_specs=[_rows(tm, wide), _rows(tm, wide), _rows(tm, DIL_WIDTH), _rows(tm, 3 * DIL_WIDTH), _rows(tm, 2 * D_MODEL),
                  _rows(tm, D_MODEL), _rows(tm, LOW_W), _full(w_in_ext.shape), _full(w1.shape), _full(w2.shape),
                  _full(wk_ext.shape), _full(wv.shape), _full(e128.shape), _full(g_q.shape), _full(g_kv.shape),
                  pl.BlockSpec((tm, wide), lambda i: (i % ns, 0)), pl.BlockSpec((tm, wide), lambda i: (i % ns, 0)),
                  pl.BlockSpec((tm, LANES), lambda i: (i % ns, 0))],
        out_specs=(_rows(tm, D_MODEL), _rows(tm, IN_EXT), _rows(tm, wide), _rows(tm, wide), _full((8, D_MODEL))),
        compiler_params=_cp("arbitrary"),
    )(dqp, dkp, dvm, dqkvd, dgates, dz1, low, w_in_ext, w1, w2, wk_ext, wv, e128, g_q, g_kv, cext, sext, cs128)


def _wgrad(a, b, name, square_relu=False):
    tokens, ka = a.shape
    n = b.shape[1]
    tka = min(ka, 512)
    tn = max(w for w in range(LANES, min(n, 2304) + 1, LANES) if n % w == 0)
    tt = min(tokens, 512)
    nt = tokens // tt

    def body(a_ref, b_ref, o_ref, acc):
        kt = pl.program_id(2)

        @pl.when(kt == 0)
        def _():
            acc[...] = jnp.zeros_like(acc)

        at = a_ref[...]
        if square_relu:
            at = jnp.square(jnp.maximum(at.astype(F32), 0.0)).astype(BF16)
        acc[...] += _dot_tn(at, b_ref[...])

        @pl.when(kt == nt - 1)
        def _():
            o_ref[...] = acc[...]

    return pl.pallas_call(
        body, name=name, grid=(ka // tka, n // tn, nt), out_shape=jax.ShapeDtypeStruct((ka, n), F32),
        in_specs=[pl.BlockSpec((tt, tka), lambda i, j, k: (k, i)), pl.BlockSpec((tt, tn), lambda i, j, k: (k, j))],
        out_specs=pl.BlockSpec((tka, tn), lambda i, j, k: (i, j)),
        scratch_shapes=[pltpu.VMEM((tka, tn), F32)],
        compiler_params=_cp("parallel", "parallel", "arbitrary"),
    )(a, b)


def _adamw(w, g, m, v, name):
    rows, cols = w.shape
    tr = 256 if rows % 256 == 0 else rows

    def body(w_ref, g_ref, m_ref, v_ref, d_ref, nm_ref, nv_ref):
        g = g_ref[...]
        m = ADAM_B1 * m_ref[...] + (1.0 - ADAM_B1) * g
        v = ADAM_B2 * v_ref[...] + (1.0 - ADAM_B2) * jnp.square(g)
        m_hat = m / (1.0 - ADAM_B1 ** ADAM_STEP)
        v_hat = v / (1.0 - ADAM_B2 ** ADAM_STEP)
        d_ref[...] = -ADAM_LR * (m_hat / (jnp.sqrt(v_hat) + ADAM_EPS) + ADAM_WD * w_ref[...])
        nm_ref[...] = m
        nv_ref[...] = v

    blk = pl.BlockSpec((tr, cols), lambda i: (i, 0))
    return pl.pallas_call(
        body, name=name, grid=(rows // tr,), out_shape=(jax.ShapeDtypeStruct((rows, cols), F32),) * 3,
        in_specs=[blk] * 4, out_specs=(blk,) * 3, compiler_params=_cp("parallel"),
    )(w, g, m, v)


def _adamw_small(parts, w, m, v):
    _, rows, cols = parts.shape

    def body(p_ref, w_ref, m_ref, v_ref, g_ref, d_ref, nm_ref, nv_ref):
        g = p_ref[0]
        for d in range(1, N_DEV):
            g = g + p_ref[d]
        g_ref[...] = g
        m = ADAM_B1 * m_ref[...] + (1.0 - ADAM_B1) * g
        v = ADAM_B2 * v_ref[...] + (1.0 - ADAM_B2) * jnp.square(g)
        m_hat = m / (1.0 - ADAM_B1 ** ADAM_STEP)
        v_hat = v / (1.0 - ADAM_B2 ** ADAM_STEP)
        d_ref[...] = -ADAM_LR * (m_hat / (jnp.sqrt(v_hat) + ADAM_EPS) + ADAM_WD * w_ref[...])
        nm_ref[...] = m
        nv_ref[...] = v

    return pl.pallas_call(
        body, name="adamw_replicated", out_shape=(jax.ShapeDtypeStruct((rows, cols), F32),) * 4,
        in_specs=[_full(parts.shape)] + [_full((rows, cols))] * 3, out_specs=(_full((rows, cols)),) * 4, grid=(1,),
        compiler_params=_cp("arbitrary"),
    )(parts, w, m, v)


def _pad_rows(a2d, mult):
    pad = (-a2d.shape[0]) % mult
    return jnp.pad(a2d, ((0, pad), (0, 0))) if pad else a2d


def _rot_cols(w):
    half = ROPE // 2
    return jnp.concatenate([-w[..., half:], w[..., :half]], axis=-1)


def _unrot_cols(dw):
    half = ROPE // 2
    return jnp.concatenate([dw[..., half:], -dw[..., :half]], axis=-1)


_SHARDED = (("w_in", (D_MODEL, 532), True), ("w_uq", (Q_LORA, 96), True), ("w_ukv", (KV_LORA, 128), True),
            ("w_o_mla", (DIL_WIDTH, 128), True), ("w_o_dil", (DIL_WIDTH, 128), True), ("w_out", (128, D_MODEL), False),
            ("w_ff1", (D_MODEL, 512), True), ("w_ff2", (512, D_MODEL), False), ("b_gate", (2, 128), True))


def _pack(blocks, dtype, lead=()):
    flat = []
    for b in blocks:
        f = b.astype(dtype).reshape(*lead, -1, LANES)
        flat.append(jnp.pad(f, [(0, 0)] * len(lead) + [(0, (-f.shape[-2]) % 16), (0, 0)]))
    out = jnp.concatenate(flat, axis=len(lead))
    return jnp.pad(out, [(0, 0)] * len(lead) + [(0, (-out.shape[-2]) % PACK_ROWS), (0, 0)])


def _full_weight(stacked, by_cols):
    if by_cols:
        return stacked.transpose(1, 0, 2).reshape(stacked.shape[1], -1)
    return stacked.reshape(-1, stacked.shape[2])


def _shards_of(full, by_cols):
    if by_cols:
        r = full.shape[0]
        return full.reshape(r, N_DEV, -1).transpose(1, 0, 2)
    return full.reshape(N_DEV, -1, full.shape[1])


def _rope_tables(seq):
    half = ROPE // 2
    inv = jnp.power(ROPE_THETA, -jnp.arange(half, dtype=F32) / half)
    ang = jnp.arange(seq, dtype=F32)[:, None] * inv[None, :]
    cos = jnp.concatenate([jnp.cos(ang)] * 2, axis=1)
    sin = jnp.concatenate([jnp.sin(ang)] * 2, axis=1)
    ones, zeros = jnp.ones((seq, 2 * NOPE), F32), jnp.zeros((seq, 2 * NOPE), F32)
    pad = jnp.zeros((seq, PAIR_W - 2 * NOPE - 2 * ROPE), F32)
    cext = jnp.tile(jnp.concatenate([ones, cos, cos, pad], axis=1), (1, N_PAIRS))
    sext = jnp.tile(jnp.concatenate([zeros, sin, sin, pad], axis=1), (1, N_PAIRS))
    cs128 = jnp.concatenate([cos, sin, jnp.zeros((seq, LANES - 2 * ROPE), F32)], axis=1)
    return cext, sext, cs128


def _pair_slabs(nope, rope):
    k = nope.shape[0]
    nope = nope.reshape(k, N_PAIRS, 2 * NOPE)
    rope = jnp.zeros((k, N_PAIRS, 2 * ROPE), nope.dtype) if rope is None else rope.reshape(k, N_PAIRS, 2 * ROPE)
    pad = jnp.zeros((k, N_PAIRS, PAIR_W - 2 * NOPE - 2 * ROPE), nope.dtype)
    return jnp.concatenate([nope, rope, pad], axis=2).reshape(k, N_PAIRS * PAIR_W)


def _split_slabs(slabs):
    k = slabs.shape[0]
    s = slabs.reshape(k, N_PAIRS, PAIR_W)
    return s[:, :, :2 * NOPE].reshape(k, N_HEADS, NOPE), s[:, :, 2 * NOPE:2 * NOPE + 2 * ROPE].reshape(k, N_HEADS, ROPE)


def kernel(x, w_in, b_gate, g_q_a, w_uq, g_kv_a, w_ukv, w_o_mla, w_o_dil, w_out, ln1_g, ln1_b, w_ff1, w_ff2, ln2_g, ln2_b, loss_target, m_w_in, m_b_gate, m_g_q_a, m_w_uq, m_g_kv_a, m_w_ukv, m_w_o_mla, m_w_o_dil, m_w_out, m_ln1_g, m_ln1_b, m_w_ff1, m_w_ff2, m_ln2_g, m_ln2_b, v_w_in, v_b_gate, v_g_q_a, v_w_uq, v_g_kv_a, v_w_ukv, v_w_o_mla, v_w_o_dil, v_w_out, v_ln1_g, v_ln1_b, v_w_ff1, v_w_ff2, v_ln2_g, v_ln2_b):
    batch, seq, _ = x.shape
    tokens = batch * seq
    sharded_w = dict(w_in=w_in, w_uq=w_uq, w_ukv=w_ukv, w_o_mla=w_o_mla, w_o_dil=w_o_dil, w_out=w_out, w_ff1=w_ff1, w_ff2=w_ff2, b_gate=b_gate)
    sharded_m = dict(w_in=m_w_in, w_uq=m_w_uq, w_ukv=m_w_ukv, w_o_mla=m_w_o_mla, w_o_dil=m_w_o_dil, w_out=m_w_out, w_ff1=m_w_ff1, w_ff2=m_w_ff2, b_gate=m_b_gate)
    sharded_v = dict(w_in=v_w_in, w_uq=v_w_uq, w_ukv=v_w_ukv, w_o_mla=v_w_o_mla, w_o_dil=v_w_o_dil, w_out=v_w_out, w_ff1=v_w_ff1, w_ff2=v_w_ff2, b_gate=v_b_gate)

    bg = b_gate[0]
    bg_hi = bg.astype(BF16)
    bg_lo = (bg - bg_hi.astype(F32)).astype(BF16)
    blocks = [sharded_w[n][0] for n, _, _ in _SHARDED[:-1]] + [jnp.concatenate([bg_hi, bg_lo], axis=0)]
    gathered = _all_gather(_pack(blocks, BF16), "all_gather_weights")
    parts = _unpack_gathered(gathered)
    full = {n: _full_weight(p, by_cols) for (n, _, by_cols), p in zip(_SHARDED[:-1], parts[:-1])}
    bg_parts = parts[-1].astype(F32)
    b_gate_full = _full_weight(bg_parts[:, 0:2] + bg_parts[:, 2:4], True)

    wi = full["w_in"]
    s0, s1, s2, s3 = Q_LORA, Q_LORA + KV_LORA, Q_LORA + KV_LORA + ROPE, Q_LORA + KV_LORA + ROPE + 3 * DIL_WIDTH
    w_kr = wi[:, s1:s2]
    w_in_ext = jnp.concatenate([wi[:, :s2], _rot_cols(w_kr), jnp.zeros((D_MODEL, LOW_W - s2 - ROPE), BF16), wi[:, s2:]], axis=1)
    uq = full["w_uq"].reshape(Q_LORA, N_HEADS, NOPE + ROPE)
    w1 = _pair_slabs(uq[:, :, :NOPE], uq[:, :, NOPE:])
    w2 = _pair_slabs(jnp.zeros_like(uq[:, :, :NOPE]), _rot_cols(uq[:, :, NOPE:]))
    ukv = full["w_ukv"].reshape(KV_LORA, N_HEADS, NOPE + HEAD_V)
    wk_ext = _pair_slabs(ukv[:, :, :NOPE], None)
    wv = ukv[:, :, NOPE:].reshape(KV_LORA, N_HEADS * HEAD_V)
    eye = jnp.eye(ROPE, dtype=BF16)
    e_slab = jnp.concatenate([jnp.zeros((ROPE, 2 * NOPE), BF16), eye, eye, jnp.zeros((ROPE, PAIR_W - 2 * NOPE - 2 * ROPE), BF16)], axis=1)
    e128 = jnp.concatenate([jnp.tile(e_slab, (1, N_PAIRS)), jnp.zeros((LANES - ROPE, N_PAIRS * PAIR_W), BF16)], axis=0)
    cext, sext, cs128 = _rope_tables(seq)
    slopes_h = jnp.asarray([2.0 ** (-(h + 1)) for h in range(N_HEADS)], F32).reshape(N_PAIRS, 2)
    slopes = jnp.broadcast_to(jnp.repeat(slopes_h, 64, axis=1)[:, None, :], (N_PAIRS, 8, LANES))

    x2 = x.reshape(tokens, D_MODEL)
    xb = x2.astype(BF16)
    low, gates, qkvd, qp, kp, vm, qn, kvn = _fwd_proj(xb, w_in_ext, w1, w2, wk_ext, wv, e128, g_q_a, g_kv_a, cext, sext, cs128, seq=seq)
    mla = dict(batch=batch, seq=seq, width=PAIR_W, col0=(0, 0, 0), dilated=False, scale=MLA_SCALE)
    dil = dict(batch=batch, seq=seq, width=LANES, col0=(0, N_PAIRS, 2 * N_PAIRS), dilated=True, scale=DIL_SCALE)
    o_a, lse_a = _attn_fwd(qp, kp, vm, slopes, name="mla_attention_fwd", **mla)
    o_b, lse_b = _attn_fwd(qkvd, qkvd, qkvd, slopes, name="dilated_attention_fwd", **dil)
    h, hb, xhat1, rstd1, y_a, y_b, mix = _fwd_mix(o_a, o_b, gates, x2, b_gate_full, full["w_o_mla"], full["w_o_dil"], full["w_out"], ln1_g, ln1_b, seq=seq)
    u, dz2, dz2b, stat2 = _fwd_mlp(hb, h, loss_target.reshape(tokens, D_MODEL), full["w_ff1"], full["w_ff2"], ln2_g, ln2_b, seq=seq)

    du, dz1, dz1b, stat1 = _bwd_mlp(dz2, dz2b, u, xhat1, rstd1, full["w_ff1"], full["w_ff2"], ln1_g, seq=seq)
    dgates, dy_a, dy_b, do_a, do_b, stat_g = _bwd_mix(dz1b, gates, y_a, y_b, b_gate_full, full["w_o_mla"], full["w_o_dil"], full["w_out"], seq=seq)
    dqp, dkp, dvm = _attn_bwd(qp, kp, vm, o_a, do_a, lse_a, slopes, name="mla_attention_bwd", **mla)
    dq_d, dk_d, dv_d = _attn_bwd(qkvd, qkvd, qkvd, o_b, do_b, lse_b, slopes, name="dilated_attention_bwd", **dil)
    dqkvd = jnp.concatenate([dq_d, dk_d, dv_d], axis=1)
    grad_x, dproj, d_a, d_b, stat_r = _bwd_proj(dqp, dkp, dvm, dqkvd, dgates, dz1, low, w_in_ext, w1, w2, wk_ext, wv, e128,
                                                g_q_a, g_kv_a, cext, sext, cs128, seq=seq)

    dw_in_ext = _wgrad(xb, dproj, "wgrad_in")
    dw1 = _wgrad(qn, d_a, "wgrad_uq_direct")
    dw2 = _wgrad(qn, d_b, "wgrad_uq_rotated")
    dwk = _wgrad(kvn, dkp, "wgrad_ukv_k")
    dwv = _wgrad(kvn, dvm, "wgrad_ukv_v")
    g_full = {
        "w_o_mla": _wgrad(o_a, dy_a, "wgrad_o_mla"), "w_o_dil": _wgrad(o_b, dy_b, "wgrad_o_dil"),
        "w_out": _wgrad(mix, dz1b, "wgrad_out"), "w_ff1": _wgrad(hb, du, "wgrad_ff1"),
        "w_ff2": _wgrad(u, dz2b, "wgrad_ff2", square_relu=True),
    }
    dlow = dw_in_ext[:, :LOW_W]
    dw_kr = dlow[:, s1:s2] + _unrot_cols(dlow[:, s2:s2 + ROPE])
    g_full["w_in"] = jnp.concatenate([dlow[:, :s1], dw_kr, dw_in_ext[:, LOW_W:]], axis=1)
    n1, r1 = _split_slabs(dw1)
    _, r2 = _split_slabs(dw2)
    g_full["w_uq"] = jnp.concatenate([n1, r1 + _unrot_cols(r2)], axis=2).reshape(Q_LORA, N_HEADS * (NOPE + ROPE))
    nk, _ = _split_slabs(dwk)
    g_full["w_ukv"] = jnp.concatenate([nk, dwv.reshape(KV_LORA, N_HEADS, HEAD_V)], axis=2).reshape(KV_LORA, N_HEADS * (NOPE + HEAD_V))

    g_full["b_gate"] = stat_g[0:2]
    gblocks = [_shards_of(g_full[n], by_cols) for n, _, by_cols in _SHARDED]
    summed = _reduce_scatter(_pack(gblocks, BF16, lead=(N_DEV,)))
    g_parts = _unpack_summed(summed)
    grads = {n: p for (n, _, _), p in zip(_SHARDED, g_parts)}
    grads["b_gate"] = grads["b_gate"][0:2]

    small_w = [g_q_a, g_kv_a, ln1_g, ln1_b, ln2_g, ln2_b]
    small_m = [m_g_q_a, m_g_kv_a, m_ln1_g, m_ln1_b, m_ln2_g, m_ln2_b]
    small_v = [v_g_q_a, v_g_kv_a, v_ln1_g, v_ln1_b, v_ln2_g, v_ln2_b]
    widths = [a.shape[1] for a in small_w]
    partial = jnp.concatenate([stat_r[0:1, :Q_LORA], stat_r[1:2, :KV_LORA], stat1[0:1], stat1[1:2], stat2[0:1], stat2[1:2],
                               stat2[2:3, :LANES]], axis=1)

    def as_rows(vecs, extra):
        flat = jnp.concatenate(vecs + [jnp.zeros((1, extra), F32)], axis=1)
        return _pad_rows(flat.reshape(-1, LANES), 8)

    every = _all_gather(_pad_rows(partial.reshape(-1, LANES), 8), "all_gather_replicated_grads")
    g_s, d_s, nm_s, nv_s = _adamw_small(every, as_rows(small_w, LANES), as_rows(small_m, LANES), as_rows(small_v, LANES))

    def split_small(a):
        flat = a.reshape(1, -1)
        out, c0 = [], 0
        for w in widths:
            out.append(flat[:, c0:c0 + w])
            c0 += w
        return out, flat[0, c0]

    g_small, loss = split_small(g_s)
    d_small, nm_small, nv_small = split_small(d_s)[0], split_small(nm_s)[0], split_small(nv_s)[0]

    upd = {}
    for n, shape, _ in _SHARDED:
        upd[n] = _adamw(sharded_w[n][0], grads[n], sharded_m[n][0], sharded_v[n][0], "adamw_" + n)

    order = ["w_in", "b_gate", "g_q_a", "w_uq", "g_kv_a", "w_ukv", "w_o_mla", "w_o_dil", "w_out", "ln1_g", "ln1_b", "w_ff1", "w_ff2", "ln2_g", "ln2_b"]
    small_names = ["g_q_a", "g_kv_a", "ln1_g", "ln1_b", "ln2_g", "ln2_b"]

    def pick(kind):
        out = []
        for n in order:
            if n in small_names:
                k = small_names.index(n)
                out.append((g_small, d_small, nm_small, nv_small)[kind][k])
            elif kind == 0:
                out.append(grads[n][None])
            else:
                out.append(upd[n][kind - 1][None])
        return out

    return (loss, grad_x.reshape(batch, seq, D_MODEL), *pick(0), *pick(1), *pick(2), *pick(3))


def _unpack_gathered(gathered):
    out, r0 = [], 0
    for _, (r, c), _ in _SHARDED[:-1]:
        n_rows = r * c // LANES
        out.append(gathered[:, r0:r0 + n_rows, :].reshape(N_DEV, r, c))
        r0 += n_rows + (-n_rows) % 16
    out.append(gathered[:, r0:r0 + 4, :])
    return out


def _unpack_summed(summed):
    out, r0 = [], 0
    for _, (r, c), _ in _SHARDED[:-1]:
        n_rows = r * c // LANES
        out.append(summed[r0:r0 + n_rows, :].reshape(r, c))
        r0 += n_rows + (-n_rows) % 16
    out.append(summed[r0:r0 + 4, :])
    return out
```

```python
import functools
import math

import jax
import jax.numpy as jnp
from jax import lax
from jax.experimental import pallas as pl
from jax.experimental.pallas import tpu as pltpu

F32 = jnp.float32
BF16 = jnp.bfloat16
I32 = jnp.int32

D_MODEL = 1024
N_HEADS = 8
NOPE = 64
ROPE = 32
HEAD_V = 64
Q_LORA = 384
KV_LORA = 256
DIL_WIDTH = 512
D_FF = 4096
ROPE_THETA = 10000.0
LN_EPS = 1e-5
RMS_EPS = 1e-6
NEG = -1e30
ALPHA = 2.0 ** 0.25
MLA_SCALE = (NOPE + ROPE) ** -0.5
DIL_SCALE = 64 ** -0.5
ADAM_LR, ADAM_B1, ADAM_B2, ADAM_EPS, ADAM_WD, ADAM_STEP = 0.001, 0.9, 0.999, 1e-08, 0.01, 10

LANES = 128
PAIR_W = 256
N_PAIRS = N_HEADS // 2
LOW_W = 768
IN_EXT = LOW_W + 3 * DIL_WIDTH + 2 * D_MODEL
N_DEV = 8
TOKEN_TILE = 256
ATTN_TILE = 256
FF_TILE = 1024
VMEM_LIMIT = 56 << 20
PACK_ROWS = 1024

MESH = pl.DeviceIdType.MESH
ANY = pl.BlockSpec(memory_space=pl.ANY)


def _cp(*sem):
    return pltpu.CompilerParams(dimension_semantics=sem or None, vmem_limit_bytes=VMEM_LIMIT)


def _full(shape):
    nd = len(shape)
    return pl.BlockSpec(shape, lambda *_: (0,) * nd)


def _rows(tm, width):
    return pl.BlockSpec((tm, width), lambda i, *_: (i, 0))


def _dot(a, b):
    return jnp.dot(a, b, preferred_element_type=F32)


def _dot_nt(a, b):
    return lax.dot_general(a, b, (((1,), (1,)), ((), ())), preferred_element_type=F32)


def _dot_tn(a, b):
    return lax.dot_general(a, b, (((0,), (0,)), ((), ())), preferred_element_type=F32)


def _sigmoid(z):
    return 1.0 / (1.0 + jnp.exp(-z))


def _all_gather(shard, name):
    rows, cols = shard.shape

    def body(x_ref, out_ref, send_sems, recv_sems, local_sem):
        x, y, c = lax.axis_index("x"), lax.axis_index("y"), lax.axis_index("c")
        me, sibling = (x, y, c), (x, y, 1 - c)
        chips = [(1 - x, y), (x, 1 - y), (1 - x, 1 - y)]

        def slot(px, py, pc):
            return out_ref.at[4 * px + 2 * py + pc]

        def copy(k, block, to, src=None):
            return pltpu.make_async_remote_copy(
                src_ref=slot(*block) if src is None else src, dst_ref=slot(*block),
                send_sem=send_sems.at[k], recv_sem=recv_sems.at[k], device_id=to, device_id_type=MESH)

        mine = pltpu.make_async_copy(x_ref, slot(*me), local_sem)
        mine.start()
        first = [copy(0, me, sibling, src=x_ref)]
        first += [copy(1 + j, me, (*chip, c), src=x_ref) for j, chip in enumerate(chips)]
        for cp in first:
            cp.start()
        passed = [copy(4 + j, (*chip, c), sibling) for j, chip in enumerate(chips)]
        for j, chip in enumerate(chips):
            copy(1 + j, (*chip, c), me).wait_recv()
            passed[j].start()
        copy(0, sibling, me).wait_recv()
        for j, chip in enumerate(chips):
            copy(4 + j, (*chip, 1 - c), me).wait_recv()
        for cp in first + passed:
            cp.wait_send()
        mine.wait()

    return pl.pallas_call(
        body, name=name, out_shape=jax.ShapeDtypeStruct((N_DEV, rows, cols), shard.dtype),
        in_specs=[ANY], out_specs=ANY,
        scratch_shapes=[pltpu.SemaphoreType.DMA((7,)), pltpu.SemaphoreType.DMA((7,)), pltpu.SemaphoreType.DMA(())],
    )(shard)


def _exchange(src, flips, name):
    n, rows, cols = src.shape

    def body(src_ref, dst_ref, send_sems, recv_sems):
        x, y, c = lax.axis_index("x"), lax.axis_index("y"), lax.axis_index("c")
        copies = []
        for k, (fx, fy, fc) in enumerate(flips):
            peer = (1 - x if fx else x, 1 - y if fy else y, 1 - c if fc else c)
            copies.append(pltpu.make_async_remote_copy(
                src_ref=src_ref.at[k], dst_ref=dst_ref.at[k], send_sem=send_sems.at[k], recv_sem=recv_sems.at[k],
                device_id=peer, device_id_type=MESH))
        for cp in copies:
            cp.start()
        for cp in copies:
            cp.wait_send()
        for cp in copies:
            cp.wait_recv()

    return pl.pallas_call(
        body, name=name, out_shape=jax.ShapeDtypeStruct((n, rows, cols), src.dtype),
        in_specs=[ANY], out_specs=ANY,
        scratch_shapes=[pltpu.SemaphoreType.DMA((n,)), pltpu.SemaphoreType.DMA((n,))],
    )(src)


def _pair_sum(a, b, name):
    _, rows, cols = a.shape
    tr = PACK_ROWS

    def body(a_ref, b_ref, own_ref, rest_ref):
        s = a_ref[...].astype(F32) + b_ref[...].astype(F32)
        own_ref[...] = s[0]
        rest_ref[...] = s[1:].astype(BF16)

    return pl.pallas_call(
        body, name=name, grid=(rows // tr,),
        out_shape=(jax.ShapeDtypeStruct((rows, cols), F32), jax.ShapeDtypeStruct((3, rows, cols), BF16)),
        in_specs=[pl.BlockSpec((4, tr, cols), lambda i: (0, i, 0))] * 2,
        out_specs=(pl.BlockSpec((tr, cols), lambda i: (i, 0)), pl.BlockSpec((3, tr, cols), lambda i: (0, i, 0))),
        compiler_params=_cp("parallel"),
    )(a, b)


def _final_sum(own, got, name):
    rows, cols = own.shape
    tr = PACK_ROWS

    def body(own_ref, got_ref, out_ref):
        g = got_ref[...].astype(F32)
        out_ref[...] = ((own_ref[...] + g[0]) + g[1]) + g[2]

    return pl.pallas_call(
        body, name=name, grid=(rows // tr,), out_shape=jax.ShapeDtypeStruct((rows, cols), F32),
        in_specs=[pl.BlockSpec((tr, cols), lambda i: (i, 0)), pl.BlockSpec((3, tr, cols), lambda i: (0, i, 0))],
        out_specs=pl.BlockSpec((tr, cols), lambda i: (i, 0)),
        compiler_params=_cp("parallel"),
    )(own, got)


def _reduce_scatter(packed):
    x, y, c = lax.axis_index("x"), lax.axis_index("y"), lax.axis_index("c")
    by_chip = packed.reshape(4, 2, *packed.shape[1:])
    mine, theirs = [], []
    for fx, fy in ((0, 0), (0, 1), (1, 0), (1, 1)):
        chip = 2 * jnp.where(fx, 1 - x, x) + jnp.where(fy, 1 - y, y)
        both = lax.dynamic_index_in_dim(by_chip, chip, 0, keepdims=False)
        mine.append(lax.dynamic_index_in_dim(both, c, 0, keepdims=False))
        theirs.append(lax.dynamic_index_in_dim(both, 1 - c, 0, keepdims=False))
    from_sibling = _exchange(jnp.stack(theirs), [(0, 0, 1)] * 4, "rs_sibling_exchange")
    own, rest = _pair_sum(jnp.stack(mine), from_sibling, "rs_pair_sum")
    got = _exchange(rest, [(0, 1, 0), (1, 0, 0), (1, 1, 0)], "rs_chip_exchange")
    return _final_sum(own, got, "rs_final_sum")


def _head_lanes(width, h):
    lane = lax.broadcasted_iota(I32, (1, width), 1)
    if width == LANES:
        return (lane >= 64 * h) & (lane < 64 * h + 64)
    nope = (lane >= NOPE * h) & (lane < NOPE * h + NOPE)
    rope = (lane >= 2 * NOPE + ROPE * h) & (lane < 2 * NOPE + ROPE * h + ROPE)
    return nope | rope


def _dilated_bias_table(seq):
    t = min(ATTN_TILE, seq)
    nd = seq // t

    def body(o_ref):
        h, d = pl.program_id(0), pl.program_id(1)
        delta = d * t + lax.broadcasted_iota(I32, (t, t), 1) - lax.broadcasted_iota(I32, (t, t), 0)
        mult = ((delta <= 128).astype(I32) + (((delta & 3) == 0) & (delta <= 512)).astype(I32)
                + ((delta & 15) == 0).astype(I32))
        logm = jnp.where(mult == 3, math.log(3.0), jnp.where(mult == 2, math.log(2.0), 0.0))
        slope = lax.bitcast_convert_type(jnp.broadcast_to((126 - h) << 23, (t, t)).astype(I32), F32)
        o_ref[...] = jnp.where((delta >= 0) & (mult > 0), logm - slope * delta.astype(F32), NEG)

    return pl.pallas_call(
        body, name="dilated_bias_table", grid=(N_HEADS, nd), out_shape=jax.ShapeDtypeStruct((N_HEADS, nd, t, t), F32),
        out_specs=pl.BlockSpec((None, None, t, t), lambda h, d: (h, d, 0, 0)),
        compiler_params=_cp("parallel", "parallel"),
    )()


def _attn_fwd(q, k, v, bias, *, batch, seq, width, col0, dilated, scale, name):
    t = min(ATTN_TILE, seq)
    nq = seq // t
    cq, ck, cv = col0
    pre = scale if dilated else 1.0

    def body(q_ref, k_ref, v_ref, bias_ref, o_ref, lse_ref):
        i = pl.program_id(2)
        q2 = q_ref[...] * pre if dilated else q_ref[...]
        qh = [jnp.where(_head_lanes(width, h), q2, jnp.zeros_like(q2)) for h in (0, 1)]
        vlane = [_head_lanes(LANES, h) for h in (0, 1)]
        top = lax.broadcasted_iota(I32, (LANES, t), 0) < HEAD_V
        causal = lax.broadcasted_iota(I32, (t, t), 0) <= lax.broadcasted_iota(I32, (t, t), 1)

        def step(j, carry, diagonal):
            m0, l0, m1, l1, acc = carry
            ks = pl.multiple_of(j * t, t)
            kj = k_ref[pl.ds(ks, t), :]
            vj = v_ref[pl.ds(ks, t), :]
            new, alphas, pv = [], [], []
            for h, (m, l) in enumerate(((m0, l0), (m1, l1))):
                s = _dot_nt(kj, qh[h])
                if dilated:
                    s = s + bias_ref[h, i - j]
                else:
                    s = s * scale
                    if diagonal:
                        s = jnp.where(causal, s, NEG)
                m_new = jnp.maximum(m, jnp.max(s, axis=0, keepdims=True))
                a = jnp.exp(m - m_new)
                p = jnp.exp(s - m_new)
                new += [m_new, a * l + jnp.sum(p, axis=0, keepdims=True)]
                alphas.append(a)
                pv.append(_dot_tn(jnp.where(vlane[h], vj, jnp.zeros_like(vj)), p.astype(BF16)))
            acc = jnp.where(top, alphas[0], alphas[1]) * acc + pv[0] + pv[1]
            return (*new, acc)

        row = jnp.full((1, t), NEG, F32)
        zero = jnp.zeros((1, t), F32)
        init = (row, zero, row, zero, jnp.zeros((LANES, t), F32))
        if dilated:
            carry = lax.fori_loop(0, i + 1, functools.partial(step, diagonal=False), init)
        else:
            carry = step(i, lax.fori_loop(0, i, functools.partial(step, diagonal=False), init), True)
        m0, l0, m1, l1, acc = carry
        o_ref[...] = jnp.transpose(acc * jnp.where(top, 1.0 / l0, 1.0 / l1)).astype(BF16)
        r = lax.broadcasted_iota(I32, (8, t), 0)
        lse_ref[...] = jnp.where(r == 0, m0 + jnp.log(l0), jnp.where(r == 1, m1 + jnp.log(l1), 0.0))

    bias_spec = (pl.BlockSpec((2, nq, t, t), lambda b, p, i: (p, 0, 0, 0)) if dilated
                 else pl.BlockSpec((None, 8, LANES), lambda b, p, i: (0, 0, 0)))
    return pl.pallas_call(
        body, name=name, grid=(batch, N_PAIRS, nq),
        out_shape=(jax.ShapeDtypeStruct((batch * seq, DIL_WIDTH), BF16), jax.ShapeDtypeStruct((batch * N_PAIRS, 8, seq), F32)),
        in_specs=[pl.BlockSpec((t, width), lambda b, p, i: (b * nq + i, cq + p)),
                  pl.BlockSpec((seq, width), lambda b, p, i: (b, ck + p)),
                  pl.BlockSpec((seq, LANES), lambda b, p, i: (b, cv + p)),
                  bias_spec],
        out_specs=(pl.BlockSpec((t, LANES), lambda b, p, i: (b * nq + i, p)),
                   pl.BlockSpec((None, 8, t), lambda b, p, i: (b * N_PAIRS + p, 0, i))),
        compiler_params=_cp("parallel", "parallel", "arbitrary"),
    )(q, k, v, bias)


def _attn_bwd(q, k, v, o, do, lse, bias, *, batch, seq, width, col0, dilated, scale, name):
    t = min(ATTN_TILE, seq)
    nq = seq // t
    cq, ck, cv = col0
    pre = scale if dilated else 1.0

    def body(q_ref, k_ref, v_ref, o_ref, do_ref, lse_ref, bias_ref, dq_ref, dk_ref, dv_ref, dq_acc, dk_acc, dv_acc, rowdot):
        wlane = [_head_lanes(width, h) for h in (0, 1)]
        vlane = [_head_lanes(LANES, h) for h in (0, 1)]
        causal = lax.broadcasted_iota(I32, (t, t), 0) <= lax.broadcasted_iota(I32, (t, t), 1)
        prod = jnp.transpose(do_ref[...].astype(F32) * o_ref[...].astype(F32))
        rowdot[0:1, :] = jnp.sum(prod[0:HEAD_V], axis=0, keepdims=True)
        rowdot[1:2, :] = jnp.sum(prod[HEAD_V:], axis=0, keepdims=True)
        dq_acc[...] = jnp.zeros_like(dq_acc)

        def k_tile(j, _):
            ks = pl.multiple_of(j * t, t)
            kj = k_ref[pl.ds(ks, t), :]
            vj = v_ref[pl.ds(ks, t), :]
            kh = [jnp.where(wlane[h], kj, jnp.zeros_like(kj)) for h in (0, 1)]
            dk_acc[...] = jnp.zeros_like(dk_acc)
            dv_acc[...] = jnp.zeros_like(dv_acc)

            def q_tile(i, _, diagonal):
                qs = pl.multiple_of(i * t, t)
                qi = q_ref[pl.ds(qs, t), :] * pre if dilated else q_ref[pl.ds(qs, t), :]
                doi = do_ref[pl.ds(qs, t), :]
                dq_i = jnp.zeros((t, width), F32)
                for h in (0, 1):
                    qih = jnp.where(wlane[h], qi, jnp.zeros_like(qi))
                    doih = jnp.where(vlane[h], doi, jnp.zeros_like(doi))
                    s = _dot_nt(kj, qih)
                    if dilated:
                        s = s + bias_ref[h, i - j]
                    else:
                        s = s * scale
                        if diagonal:
                            s = jnp.where(causal, s, NEG)
                    p = jnp.exp(s - lse_ref[h:h + 1, pl.ds(qs, t)])
                    dp = _dot_nt(vj, doih)
                    ds = p * (dp - rowdot[h:h + 1, pl.ds(qs, t)])
                    ds = (ds if dilated else ds * scale).astype(BF16)
                    dv_acc[...] += _dot(p.astype(BF16), doih)
                    dk_acc[...] += _dot(ds, qih)
                    dq_i = dq_i + _dot_tn(ds, kh[h])
                dq_acc[pl.ds(qs, t), :] += dq_i
                return 0

            if dilated:
                lax.fori_loop(j, nq, functools.partial(q_tile, diagonal=False), 0)
            else:
                q_tile(j, 0, True)
                lax.fori_loop(j + 1, nq, functools.partial(q_tile, diagonal=False), 0)
            dk_ref[pl.ds(ks, t), :] = dk_acc[...].astype(BF16)
            dv_ref[pl.ds(ks, t), :] = dv_acc[...].astype(BF16)
            return 0

        lax.fori_loop(0, nq, k_tile, 0)
        dq_ref[...] = (dq_acc[...] * pre).astype(BF16)

    tokens = batch * seq
    bias_spec = (pl.BlockSpec((2, nq, t, t), lambda b, p: (p, 0, 0, 0)) if dilated
                 else pl.BlockSpec((None, 8, LANES), lambda b, p: (0, 0, 0)))
    return pl.pallas_call(
        body, name=name, grid=(batch, N_PAIRS),
        out_shape=(jax.ShapeDtypeStruct((tokens, N_PAIRS * width), BF16), jax.ShapeDtypeStruct((tokens, N_PAIRS * width), BF16),
                   jax.ShapeDtypeStruct((tokens, DIL_WIDTH), BF16)),
        in_specs=[pl.BlockSpec((seq, width), lambda b, p: (b, cq + p)),
                  pl.BlockSpec((seq, width), lambda b, p: (b, ck + p)),
                  pl.BlockSpec((seq, LANES), lambda b, p: (b, cv + p)),
                  pl.BlockSpec((seq, LANES), lambda b, p: (b, p)),
                  pl.BlockSpec((seq, LANES), lambda b, p: (b, p)),
                  pl.BlockSpec((None, 8, seq), lambda b, p: (b * N_PAIRS + p, 0, 0)),
                  bias_spec],
        out_specs=(pl.BlockSpec((seq, width), lambda b, p: (b, p)),
                   pl.BlockSpec((seq, width), lambda b, p: (b, p)),
                   pl.BlockSpec((seq, LANES), lambda b, p: (b, p))),
        scratch_shapes=[pltpu.VMEM((seq, width), F32), pltpu.VMEM((t, width), F32), pltpu.VMEM((t, LANES), F32),
                        pltpu.VMEM((8, seq), F32)],
        compiler_params=_cp("parallel", "parallel"),
    )(q, k, v, o, do, lse, bias)


def _rms(xf, g):
    r = lax.rsqrt(jnp.mean(xf * xf, axis=1, keepdims=True) + RMS_EPS)
    return xf * r * g, r


def _rms_bwd(dy, xf, r, g):
    gy = dy * g
    dx = r * gy - xf * (r * r * r) * jnp.mean(gy * xf, axis=1, keepdims=True)
    return dx, dy * xf * r


def _ln_bwd(dy, xhat, rstd, g):
    dxh = dy * g
    return rstd * (dxh - jnp.mean(dxh, axis=1, keepdims=True) - xhat * jnp.mean(dxh * xhat, axis=1, keepdims=True))


def _fwd_proj(xb, w_in_ext, w1, w2, wk_ext, wv, e128, g_q, g_kv, cext, sext, cs128, *, seq):
    tokens = xb.shape[0]
    tm = min(TOKEN_TILE, seq)
    ns = seq // tm

    def body(x_ref, win_ref, w1_ref, w2_ref, wk_ref, wv_ref, e_ref, gq_ref, gkv_ref, c_ref, s_ref, cs_ref,
             low_ref, gates_ref, qkvd_ref, qp_ref, kp_ref, vm_ref, qn_ref, kvn_ref):
        xt = x_ref[...]
        low = _dot(xt, win_ref[:, 0:LOW_W])
        low_ref[...] = low
        qkvd_ref[...] = _dot(xt, win_ref[:, LOW_W:LOW_W + 3 * DIL_WIDTH]).astype(BF16)
        gates_ref[...] = _dot(xt, win_ref[:, LOW_W + 3 * DIL_WIDTH:])
        qn = _rms(low[:, 0:Q_LORA], gq_ref[...])[0].astype(BF16)
        kvn = _rms(low[:, Q_LORA:Q_LORA + KV_LORA], gkv_ref[...])[0].astype(BF16)
        qn_ref[...] = qn
        kvn_ref[...] = kvn
        qp_ref[...] = (_dot(qn, w1_ref[...]) * c_ref[...] + _dot(qn, w2_ref[...]) * s_ref[...]).astype(BF16)
        kr = low[:, Q_LORA + KV_LORA:] * cs_ref[...]
        kr = kr + pltpu.roll(kr, LANES - ROPE, 1)
        lane = lax.broadcasted_iota(I32, kr.shape, 1)
        kr = jnp.where(lane < ROPE, kr, 0.0).astype(BF16)
        kp_ref[...] = (_dot(kvn, wk_ref[...]) + _dot(kr, e_ref[...])).astype(BF16)
        vm_ref[...] = _dot(kvn, wv_ref[...]).astype(BF16)

    n_gates = 2 * D_MODEL
    outs = [(LOW_W, F32), (n_gates, F32), (3 * DIL_WIDTH, BF16), (N_PAIRS * PAIR_W, BF16), (N_PAIRS * PAIR_W, BF16),
            (DIL_WIDTH, BF16), (Q_LORA, BF16), (KV_LORA, BF16)]
    return pl.pallas_call(
        body, name="fwd_proj", grid=(tokens // tm,),
        out_shape=tuple(jax.ShapeDtypeStruct((tokens, w), dt) for w, dt in outs),
        in_specs=[_rows(tm, D_MODEL), _full(w_in_ext.shape), _full(w1.shape), _full(w2.shape), _full(wk_ext.shape),
                  _full(wv.shape), _full(e128.shape), _full(g_q.shape), _full(g_kv.shape),
                  pl.BlockSpec((tm, N_PAIRS * PAIR_W), lambda i: (i % ns, 0)),
                  pl.BlockSpec((tm, N_PAIRS * PAIR_W), lambda i: (i % ns, 0)),
                  pl.BlockSpec((tm, LANES), lambda i: (i % ns, 0))],
        out_specs=tuple(_rows(tm, w) for w, _ in outs),
        compiler_params=_cp("parallel"),
    )(xb, w_in_ext, w1, w2, wk_ext, wv, e128, g_q, g_kv, cext, sext, cs128)


def _fwd_mix(o_a, o_b, gates, x, b_gate, w_oa, w_ob, w_out, ln_g, ln_b, *, seq):
    tokens = x.shape[0]
    tm = min(TOKEN_TILE, seq)

    def body(oa_ref, ob_ref, gt_ref, x_ref, bg_ref, woa_ref, wob_ref, wout_ref, g_ref, b_ref,
             h_ref, hb_ref, xhat_ref, rstd_ref, ya_ref, yb_ref, mix_ref):
        ya = _dot(oa_ref[...], woa_ref[...])
        yb = _dot(ob_ref[...], wob_ref[...])
        g0 = _sigmoid(gt_ref[:, 0:D_MODEL] + bg_ref[0:1, :])
        g1 = _sigmoid(gt_ref[:, D_MODEL:] + bg_ref[1:2, :])
        mix = (g0 * ya + g1 * yb).astype(BF16)
        z = ALPHA * x_ref[...] + _dot(mix, wout_ref[...])
        zc = z - jnp.mean(z, axis=1, keepdims=True)
        rstd = lax.rsqrt(jnp.mean(zc * zc, axis=1, keepdims=True) + LN_EPS)
        xhat = zc * rstd
        h = xhat * g_ref[...] + b_ref[...]
        h_ref[...] = h
        hb_ref[...] = h.astype(BF16)
        xhat_ref[...] = xhat
        rstd_ref[...] = jnp.broadcast_to(rstd, (tm, LANES))
        ya_ref[...] = ya.astype(BF16)
        yb_ref[...] = yb.astype(BF16)
        mix_ref[...] = mix

    outs = [(D_MODEL, F32), (D_MODEL, BF16), (D_MODEL, F32), (LANES, F32), (D_MODEL, BF16), (D_MODEL, BF16), (D_MODEL, BF16)]
    return pl.pallas_call(
        body, name="fwd_mix", grid=(tokens // tm,),
        out_shape=tuple(jax.ShapeDtypeStruct((tokens, w), dt) for w, dt in outs),
        in_specs=[_rows(tm, DIL_WIDTH), _rows(tm, DIL_WIDTH), _rows(tm, 2 * D_MODEL), _rows(tm, D_MODEL),
                  _full(b_gate.shape), _full(w_oa.shape), _full(w_ob.shape), _full(w_out.shape),
                  _full(ln_g.shape), _full(ln_b.shape)],
        out_specs=tuple(_rows(tm, w) for w, _ in outs),
        compiler_params=_cp("parallel"),
    )(o_a, o_b, gates, x, b_gate, w_oa, w_ob, w_out, ln_g, ln_b)


def _fwd_mlp(hb, h, target, w_ff1, w_ff2, ln_g, ln_b, *, seq):
    tokens = h.shape[0]
    tm = min(2 * TOKEN_TILE, seq)
    tf = FF_TILE
    nf = D_FF // tf

    def body(hb_ref, h_ref, tg_ref, w1_ref, w2_ref, g_ref, b_ref, u_ref, dz_ref, dzb_ref, stat_ref, acc):
        i, j = pl.program_id(0), pl.program_id(1)

        @pl.when((i == 0) & (j == 0))
        def _():
            stat_ref[...] = jnp.zeros_like(stat_ref)

        @pl.when(j == 0)
        def _():
            acc[...] = jnp.zeros_like(acc)

        u = _dot(hb_ref[...], w1_ref[...])
        u_ref[...] = u.astype(BF16)
        a = jnp.square(jnp.maximum(u, 0.0)).astype(BF16)
        acc[...] += _dot(a, w2_ref[...])

        @pl.when(j == nf - 1)
        def _():
            z = ALPHA * h_ref[...] + acc[...]
            zc = z - jnp.mean(z, axis=1, keepdims=True)
            rstd = lax.rsqrt(jnp.mean(zc * zc, axis=1, keepdims=True) + LN_EPS)
            xhat = zc * rstd
            err = xhat * g_ref[...] + b_ref[...] - tg_ref[...]
            dy = err * (1.0 / D_MODEL)
            dz = _ln_bwd(dy, xhat, rstd, g_ref[...])
            dz_ref[...] = dz
            dzb_ref[...] = dz.astype(BF16)
            stat_ref[0:1, :] += jnp.sum(dy * xhat, axis=0, keepdims=True)
            stat_ref[1:2, :] += jnp.sum(dy, axis=0, keepdims=True)
            stat_ref[2:3, :] += jnp.sum(jnp.sum(err * err, axis=1, keepdims=True), axis=0, keepdims=True) * (0.5 / D_MODEL)

    return pl.pallas_call(
        body, name="fwd_mlp", grid=(tokens // tm, nf),
        out_shape=(jax.ShapeDtypeStruct((tokens, D_FF), BF16), jax.ShapeDtypeStruct((tokens, D_MODEL), F32),
                   jax.ShapeDtypeStruct((tokens, D_MODEL), BF16), jax.ShapeDtypeStruct((8, D_MODEL), F32)),
        in_specs=[_rows(tm, D_MODEL), _rows(tm, D_MODEL), _rows(tm, D_MODEL),
                  pl.BlockSpec((D_MODEL, tf), lambda i, j: (0, j)), pl.BlockSpec((tf, D_MODEL), lambda i, j: (j, 0)),
                  _full(ln_g.shape), _full(ln_b.shape)],
        out_specs=(pl.BlockSpec((tm, tf), lambda i, j: (i, j)), _rows(tm, D_MODEL), _rows(tm, D_MODEL), _full((8, D_MODEL))),
        scratch_shapes=[pltpu.VMEM((tm, D_MODEL), F32)],
        compiler_params=_cp("arbitrary", "arbitrary"),
    )(hb, h, target, w_ff1, w_ff2, ln_g, ln_b)


def _bwd_mlp(dz2, dz2b, u, xhat1, rstd1, w_ff1, w_ff2, ln_g, *, seq):
    tokens = dz2.shape[0]
    tm = min(2 * TOKEN_TILE, seq)
    tf = FF_TILE
    nf = D_FF // tf

    def body(dz_ref, dzb_ref, u_ref, xh_ref, rs_ref, w1_ref, w2_ref, g_ref, du_ref, dz1_ref, dz1b_ref, stat_ref, acc):
        i, j = pl.program_id(0), pl.program_id(1)

        @pl.when((i == 0) & (j == 0))
        def _():
            stat_ref[...] = jnp.zeros_like(stat_ref)

        @pl.when(j == 0)
        def _():
            acc[...] = jnp.zeros_like(acc)

        da = _dot_nt(dzb_ref[...], w2_ref[...])
        du = (da * (2.0 * jnp.maximum(u_ref[...].astype(F32), 0.0))).astype(BF16)
        du_ref[...] = du
        acc[...] += _dot_nt(du, w1_ref[...])

        @pl.when(j == nf - 1)
        def _():
            dh = ALPHA * dz_ref[...] + acc[...]
            xhat = xh_ref[...]
            dz1 = _ln_bwd(dh, xhat, rs_ref[:, 0:1], g_ref[...])
            dz1_ref[...] = dz1
            dz1b_ref[...] = dz1.astype(BF16)
            stat_ref[0:1, :] += jnp.sum(dh * xhat, axis=0, keepdims=True)
            stat_ref[1:2, :] += jnp.sum(dh, axis=0, keepdims=True)

    return pl.pallas_call(
        body, name="bwd_mlp", grid=(tokens // tm, nf),
        out_shape=(jax.ShapeDtypeStruct((tokens, D_FF), BF16), jax.ShapeDtypeStruct((tokens, D_MODEL), F32),
                   jax.ShapeDtypeStruct((tokens, D_MODEL), BF16), jax.ShapeDtypeStruct((8, D_MODEL), F32)),
        in_specs=[_rows(tm, D_MODEL), _rows(tm, D_MODEL), pl.BlockSpec((tm, tf), lambda i, j: (i, j)),
                  _rows(tm, D_MODEL), _rows(tm, LANES),
                  pl.BlockSpec((D_MODEL, tf), lambda i, j: (0, j)), pl.BlockSpec((tf, D_MODEL), lambda i, j: (j, 0)),
                  _full(ln_g.shape)],
        out_specs=(pl.BlockSpec((tm, tf), lambda i, j: (i, j)), _rows(tm, D_MODEL), _rows(tm, D_MODEL), _full((8, D_MODEL))),
        scratch_shapes=[pltpu.VMEM((tm, D_MODEL), F32)],
        compiler_params=_cp("arbitrary", "arbitrary"),
    )(dz2, dz2b, u, xhat1, rstd1, w_ff1, w_ff2, ln_g)


def _bwd_mix(dz1b, gates, y_a, y_b, b_gate, w_oa, w_ob, w_out, *, seq):
    tokens = dz1b.shape[0]
    tm = min(TOKEN_TILE, seq)

    def body(dz_ref, gt_ref, ya_ref, yb_ref, bg_ref, woa_ref, wob_ref, wout_ref,
             dgt_ref, dya_ref, dyb_ref, doa_ref, dob_ref, stat_ref):
        @pl.when(pl.program_id(0) == 0)
        def _():
            stat_ref[...] = jnp.zeros_like(stat_ref)

        dmix = _dot_nt(dz_ref[...], wout_ref[...])
        for k, (y_ref, w_ref, dy_ref, do_ref) in enumerate(((ya_ref, woa_ref, dya_ref, doa_ref), (yb_ref, wob_ref, dyb_ref, dob_ref))):
            g = _sigmoid(gt_ref[:, k * D_MODEL:(k + 1) * D_MODEL] + bg_ref[k:k + 1, :])
            dgate = dmix * y_ref[...].astype(F32) * g * (1.0 - g)
            dgt_ref[:, k * D_MODEL:(k + 1) * D_MODEL] = dgate.astype(BF16)
            stat_ref[k:k + 1, :] += jnp.sum(dgate, axis=0, keepdims=True)
            dy = (dmix * g).astype(BF16)
            dy_ref[...] = dy
            do_ref[...] = _dot_nt(dy, w_ref[...]).astype(BF16)

    outs = [(2 * D_MODEL, BF16), (D_MODEL, BF16), (D_MODEL, BF16), (DIL_WIDTH, BF16), (DIL_WIDTH, BF16)]
    return pl.pallas_call(
        body, name="bwd_mix", grid=(tokens // tm,),
        out_shape=tuple(jax.ShapeDtypeStruct((tokens, w), dt) for w, dt in outs) + (jax.ShapeDtypeStruct((8, D_MODEL), F32),),
        in_specs=[_rows(tm, D_MODEL), _rows(tm, 2 * D_MODEL), _rows(tm, D_MODEL), _rows(tm, D_MODEL),
                  _full(b_gate.shape), _full(w_oa.shape), _full(w_ob.shape), _full(w_out.shape)],
        out_specs=tuple(_rows(tm, w) for w, _ in outs) + (_full((8, D_MODEL)),),
        compiler_params=_cp("arbitrary"),
    )(dz1b, gates, y_a, y_b, b_gate, w_oa, w_ob, w_out)


def _bwd_proj(dqp, dkp, dvm, dqkvd, dgates, dz1, low, w_in_ext, w1, w2, wk_ext, wv, e128, g_q, g_kv, cext, sext, cs128, *, seq):
    tokens = dz1.shape[0]
    tm = min(TOKEN_TILE, seq)
    ns = seq // tm

    def body(dqp_ref, dkp_ref, dvm_ref, dqkvd_ref, dgt_ref, dz_ref, low_ref, win_ref, w1_ref, w2_ref, wk_ref, wv_ref,
             e_ref, gq_ref, gkv_ref, c_ref, s_ref, cs_ref, dx_ref, dproj_ref, da_ref, db_ref, stat_ref):
        @pl.when(pl.program_id(0) == 0)
        def _():
            stat_ref[...] = jnp.zeros_like(stat_ref)

        low = low_ref[...]
        dqp = dqp_ref[...].astype(F32)
        d_a = (dqp * c_ref[...]).astype(BF16)
        d_b = (dqp * s_ref[...]).astype(BF16)
        da_ref[...] = d_a
        db_ref[...] = d_b
        q_a = low[:, 0:Q_LORA]
        _, rq = _rms(q_a, gq_ref[...])
        dq_a, gq_terms = _rms_bwd(_dot_nt(d_a, w1_ref[...]) + _dot_nt(d_b, w2_ref[...]), q_a, rq, gq_ref[...])
        kv_a = low[:, Q_LORA:Q_LORA + KV_LORA]
        _, rkv = _rms(kv_a, gkv_ref[...])
        dkp = dkp_ref[...]
        dkv_a, gkv_terms = _rms_bwd(_dot_nt(dkp, wk_ref[...]) + _dot_nt(dvm_ref[...], wv_ref[...]), kv_a, rkv, gkv_ref[...])
        dkr = _dot_nt(dkp, e_ref[...])
        dkr = (dkr + pltpu.roll(dkr, ROPE, 1)) * cs_ref[...]
        stat_ref[0:1, 0:Q_LORA] += jnp.sum(gq_terms, axis=0, keepdims=True)
        stat_ref[1:2, 0:KV_LORA] += jnp.sum(gkv_terms, axis=0, keepdims=True)
        dproj_ref[:, 0:Q_LORA] = dq_a.astype(BF16)
        dproj_ref[:, Q_LORA:Q_LORA + KV_LORA] = dkv_a.astype(BF16)
        dproj_ref[:, Q_LORA + KV_LORA:LOW_W] = dkr.astype(BF16)
        dproj_ref[:, LOW_W:LOW_W + 3 * DIL_WIDTH] = dqkvd_ref[...]
        dproj_ref[:, LOW_W + 3 * DIL_WIDTH:] = dgt_ref[...]
        dx_ref[...] = ALPHA * dz_ref[...] + _dot_nt(dproj_ref[...], win_ref[...])

    wide = N_PAIRS * PAIR_W
    return pl.pallas_call(
        body, name="bwd_proj", grid=(tokens // tm,),
        out_shape=(jax.ShapeDtypeStruct((tokens, D_MODEL), F32), jax.ShapeDtypeStruct((tokens, IN_EXT), BF16),
                   jax.ShapeDtypeStruct((tokens, wide), BF16), jax.ShapeDtypeStruct((tokens, wide), BF16),
                   jax.ShapeDtypeStruct((8, D_MODEL), F32)),
        in_specs=[_rows(tm, wide), _rows(tm, wide), _rows(tm, DIL_WIDTH), _rows(tm, 3 * DIL_WIDTH), _rows(tm, 2 * D_MODEL),
                  _rows(tm, D_MODEL), _rows(tm, LOW_W), _full(w_in_ext.shape), _full(w1.shape), _full(w2.shape),
                  _full(wk_ext.shape), _full(wv.shape), _full(e128.shape), _full(g_q.shape), _full(g_kv.shape),
                  pl.BlockSpec((tm, wide), lambda i: (i % ns, 0)), pl.BlockSpec((tm, wide), lambda i: (i % ns, 0)),
                  pl.BlockSpec((tm, LANES), lambda i: (i % ns, 0))],
        out_specs=(_rows(tm, D_MODEL), _rows(tm, IN_EXT), _rows(tm, wide), _rows(tm, wide), _full((8, D_MODEL))),
        compiler_params=_cp("arbitrary"),
    )(dqp, dkp, dvm, dqkvd, dgates, dz1, low, w_in_ext, w1, w2, wk_ext, wv, e128, g_q, g_kv, cext, sext, cs128)


def _wgrad(a, b, name, square_relu=False):
    tokens, ka = a.shape
    n = b.shape[1]
    tka = min(ka, 512)
    tn = max(w for w in range(LANES, min(n, 2304) + 1, LANES) if n % w == 0)
    tt = min(tokens, 512)
    nt = tokens // tt

    def body(a_ref, b_ref, o_ref, acc):
        kt = pl.program_id(2)

        @pl.when(kt == 0)
        def _():
            acc[...] = jnp.zeros_like(acc)

        at = a_ref[...]
        if square_relu:
            at = jnp.square(jnp.maximum(at.astype(F32), 0.0)).astype(BF16)
        acc[...] += _dot_tn(at, b_ref[...])

        @pl.when(kt == nt - 1)
        def _():
            o_ref[...] = acc[...]

    return pl.pallas_call(
        body, name=name, grid=(ka // tka, n // tn, nt), out_shape=jax.ShapeDtypeStruct((ka, n), F32),
        in_specs=[pl.BlockSpec((tt, tka), lambda i, j, k: (k, i)), pl.BlockSpec((tt, tn), lambda i, j, k: (k, j))],
        out_specs=pl.BlockSpec((tka, tn), lambda i, j, k: (i, j)),
        scratch_shapes=[pltpu.VMEM((tka, tn), F32)],
        compiler_params=_cp("parallel", "parallel", "arbitrary"),
    )(a, b)


def _adamw(w, g, m, v, name):
    rows, cols = w.shape
    tr = 256 if rows % 256 == 0 else rows

    def body(w_ref, g_ref, m_ref, v_ref, d_ref, nm_ref, nv_ref):
        g = g_ref[...]
        m = ADAM_B1 * m_ref[...] + (1.0 - ADAM_B1) * g
        v = ADAM_B2 * v_ref[...] + (1.0 - ADAM_B2) * jnp.square(g)
        m_hat = m / (1.0 - ADAM_B1 ** ADAM_STEP)
        v_hat = v / (1.0 - ADAM_B2 ** ADAM_STEP)
        d_ref[...] = -ADAM_LR * (m_hat / (jnp.sqrt(v_hat) + ADAM_EPS) + ADAM_WD * w_ref[...])
        nm_ref[...] = m
        nv_ref[...] = v

    blk = pl.BlockSpec((tr, cols), lambda i: (i, 0))
    return pl.pallas_call(
        body, name=name, grid=(rows // tr,), out_shape=(jax.ShapeDtypeStruct((rows, cols), F32),) * 3,
        in_specs=[blk] * 4, out_specs=(blk,) * 3, compiler_params=_cp("parallel"),
    )(w, g, m, v)


def _adamw_small(parts, w, m, v):
    _, rows, cols = parts.shape

    def body(p_ref, w_ref, m_ref, v_ref, g_ref, d_ref, nm_ref, nv_ref):
        g = p_ref[0]
        for d in range(1, N_DEV):
            g = g + p_ref[d]
        g_ref[...] = g
        m = ADAM_B1 * m_ref[...] + (1.0 - ADAM_B1) * g
        v = ADAM_B2 * v_ref[...] + (1.0 - ADAM_B2) * jnp.square(g)
        m_hat = m / (1.0 - ADAM_B1 ** ADAM_STEP)
        v_hat = v / (1.0 - ADAM_B2 ** ADAM_STEP)
        d_ref[...] = -ADAM_LR * (m_hat / (jnp.sqrt(v_hat) + ADAM_EPS) + ADAM_WD * w_ref[...])
        nm_ref[...] = m
        nv_ref[...] = v

    return pl.pallas_call(
        body, name="adamw_replicated", out_shape=(jax.ShapeDtypeStruct((rows, cols), F32),) * 4,
        in_specs=[_full(parts.shape)] + [_full((rows, cols))] * 3, out_specs=(_full((rows, cols)),) * 4, grid=(1,),
        compiler_params=_cp("arbitrary"),
    )(parts, w, m, v)


def _pad_rows(a2d, mult):
    pad = (-a2d.shape[0]) % mult
    return jnp.pad(a2d, ((0, pad), (0, 0))) if pad else a2d


def _rot_cols(w):
    half = ROPE // 2
    return jnp.concatenate([-w[..., half:], w[..., :half]], axis=-1)


def _unrot_cols(dw):
    half = ROPE // 2
    return jnp.concatenate([dw[..., half:], -dw[..., :half]], axis=-1)


_SHARDED = (("w_in", (D_MODEL, 532), True), ("w_uq", (Q_LORA, 96), True), ("w_ukv", (KV_LORA, 128), True),
            ("w_o_mla", (DIL_WIDTH, 128), True), ("w_o_dil", (DIL_WIDTH, 128), True), ("w_out", (128, D_MODEL), False),
            ("w_ff1", (D_MODEL, 512), True), ("w_ff2", (512, D_MODEL), False), ("b_gate", (2, 128), True))


def _pack(blocks, dtype, lead=()):
    flat = []
    for b in blocks:
        f = b.astype(dtype).reshape(*lead, -1, LANES)
        flat.append(jnp.pad(f, [(0, 0)] * len(lead) + [(0, (-f.shape[-2]) % 16), (0, 0)]))
    out = jnp.concatenate(flat, axis=len(lead))
    return jnp.pad(out, [(0, 0)] * len(lead) + [(0, (-out.shape[-2]) % PACK_ROWS), (0, 0)])


def _full_weight(stacked, by_cols):
    if by_cols:
        return stacked.transpose(1, 0, 2).reshape(stacked.shape[1], -1)
    return stacked.reshape(-1, stacked.shape[2])


def _shards_of(full, by_cols):
    if by_cols:
        r = full.shape[0]
        return full.reshape(r, N_DEV, -1).transpose(1, 0, 2)
    return full.reshape(N_DEV, -1, full.shape[1])


def _rope_tables(seq):
    half = ROPE // 2
    inv = jnp.power(ROPE_THETA, -jnp.arange(half, dtype=F32) / half)
    ang = jnp.arange(seq, dtype=F32)[:, None] * inv[None, :]
    cos = jnp.concatenate([jnp.cos(ang)] * 2, axis=1)
    sin = jnp.concatenate([jnp.sin(ang)] * 2, axis=1)
    ones, zeros = jnp.ones((seq, 2 * NOPE), F32), jnp.zeros((seq, 2 * NOPE), F32)
    pad = jnp.zeros((seq, PAIR_W - 2 * NOPE - 2 * ROPE), F32)
    cext = jnp.tile(jnp.concatenate([ones, cos, cos, pad], axis=1), (1, N_PAIRS))
    sext = jnp.tile(jnp.concatenate([zeros, sin, sin, pad], axis=1), (1, N_PAIRS))
    cs128 = jnp.concatenate([cos, sin, jnp.zeros((seq, LANES - 2 * ROPE), F32)], axis=1)
    return cext, sext, cs128


def _pair_slabs(nope, rope):
    k = nope.shape[0]
    nope = nope.reshape(k, N_PAIRS, 2 * NOPE)
    rope = jnp.zeros((k, N_PAIRS, 2 * ROPE), nope.dtype) if rope is None else rope.reshape(k, N_PAIRS, 2 * ROPE)
    pad = jnp.zeros((k, N_PAIRS, PAIR_W - 2 * NOPE - 2 * ROPE), nope.dtype)
    return jnp.concatenate([nope, rope, pad], axis=2).reshape(k, N_PAIRS * PAIR_W)


def _split_slabs(slabs):
    k = slabs.shape[0]
    s = slabs.reshape(k, N_PAIRS, PAIR_W)
    return s[:, :, :2 * NOPE].reshape(k, N_HEADS, NOPE), s[:, :, 2 * NOPE:2 * NOPE + 2 * ROPE].reshape(k, N_HEADS, ROPE)


def kernel(x, w_in, b_gate, g_q_a, w_uq, g_kv_a, w_ukv, w_o_mla, w_o_dil, w_out, ln1_g, ln1_b, w_ff1, w_ff2, ln2_g, ln2_b, loss_target, m_w_in, m_b_gate, m_g_q_a, m_w_uq, m_g_kv_a, m_w_ukv, m_w_o_mla, m_w_o_dil, m_w_out, m_ln1_g, m_ln1_b, m_w_ff1, m_w_ff2, m_ln2_g, m_ln2_b, v_w_in, v_b_gate, v_g_q_a, v_w_uq, v_g_kv_a, v_w_ukv, v_w_o_mla, v_w_o_dil, v_w_out, v_ln1_g, v_ln1_b, v_w_ff1, v_w_ff2, v_ln2_g, v_ln2_b):
    batch, seq, _ = x.shape
    tokens = batch * seq
    sharded_w = dict(w_in=w_in, w_uq=w_uq, w_ukv=w_ukv, w_o_mla=w_o_mla, w_o_dil=w_o_dil, w_out=w_out, w_ff1=w_ff1, w_ff2=w_ff2, b_gate=b_gate)
    sharded_m = dict(w_in=m_w_in, w_uq=m_w_uq, w_ukv=m_w_ukv, w_o_mla=m_w_o_mla, w_o_dil=m_w_o_dil, w_out=m_w_out, w_ff1=m_w_ff1, w_ff2=m_w_ff2, b_gate=m_b_gate)
    sharded_v = dict(w_in=v_w_in, w_uq=v_w_uq, w_ukv=v_w_ukv, w_o_mla=v_w_o_mla, w_o_dil=v_w_o_dil, w_out=v_w_out, w_ff1=v_w_ff1, w_ff2=v_w_ff2, b_gate=v_b_gate)

    bg = b_gate[0]
    bg_hi = bg.astype(BF16)
    bg_lo = (bg - bg_hi.astype(F32)).astype(BF16)
    blocks = [sharded_w[n][0] for n, _, _ in _SHARDED[:-1]] + [jnp.concatenate([bg_hi, bg_lo], axis=0)]
    gathered = _all_gather(_pack(blocks, BF16), "all_gather_weights")
    parts = _unpack_gathered(gathered)
    full = {n: _full_weight(p, by_cols) for (n, _, by_cols), p in zip(_SHARDED[:-1], parts[:-1])}
    bg_parts = parts[-1].astype(F32)
    b_gate_full = _full_weight(bg_parts[:, 0:2] + bg_parts[:, 2:4], True)

    wi = full["w_in"]
    s0, s1, s2, s3 = Q_LORA, Q_LORA + KV_LORA, Q_LORA + KV_LORA + ROPE, Q_LORA + KV_LORA + ROPE + 3 * DIL_WIDTH
    w_kr = wi[:, s1:s2]
    w_in_ext = jnp.concatenate([wi[:, :s2], _rot_cols(w_kr), jnp.zeros((D_MODEL, LOW_W - s2 - ROPE), BF16), wi[:, s2:]], axis=1)
    uq = full["w_uq"].reshape(Q_LORA, N_HEADS, NOPE + ROPE)
    w1 = _pair_slabs(uq[:, :, :NOPE], uq[:, :, NOPE:])
    w2 = _pair_slabs(jnp.zeros_like(uq[:, :, :NOPE]), _rot_cols(uq[:, :, NOPE:]))
    ukv = full["w_ukv"].reshape(KV_LORA, N_HEADS, NOPE + HEAD_V)
    wk_ext = _pair_slabs(ukv[:, :, :NOPE], None)
    wv = ukv[:, :, NOPE:].reshape(KV_LORA, N_HEADS * HEAD_V)
    eye = jnp.eye(ROPE, dtype=BF16)
    e_slab = jnp.concatenate([jnp.zeros((ROPE, 2 * NOPE), BF16), eye, eye, jnp.zeros((ROPE, PAIR_W - 2 * NOPE - 2 * ROPE), BF16)], axis=1)
    e128 = jnp.concatenate([jnp.tile(e_slab, (1, N_PAIRS)), jnp.zeros((LANES - ROPE, N_PAIRS * PAIR_W), BF16)], axis=0)
    cext, sext, cs128 = _rope_tables(seq)
    dil_bias = _dilated_bias_table(seq)
    no_bias = jnp.zeros((1, 8, LANES), F32)

    x2 = x.reshape(tokens, D_MODEL)
    xb = x2.astype(BF16)
    low, gates, qkvd, qp, kp, vm, qn, kvn = _fwd_proj(xb, w_in_ext, w1, w2, wk_ext, wv, e128, g_q_a, g_kv_a, cext, sext, cs128, seq=seq)
    mla = dict(batch=batch, seq=seq, width=PAIR_W, col0=(0, 0, 0), dilated=False, scale=MLA_SCALE)
    dil = dict(batch=batch, seq=seq, width=LANES, col0=(0, N_PAIRS, 2 * N_PAIRS), dilated=True, scale=DIL_SCALE)
    o_a, lse_a = _attn_fwd(qp, kp, vm, no_bias, name="mla_attention_fwd", **mla)
    o_b, lse_b = _attn_fwd(qkvd, qkvd, qkvd, dil_bias, name="dilated_attention_fwd", **dil)
    h, hb, xhat1, rstd1, y_a, y_b, mix = _fwd_mix(o_a, o_b, gates, x2, b_gate_full, full["w_o_mla"], full["w_o_dil"], full["w_out"], ln1_g, ln1_b, seq=seq)
    u, dz2, dz2b, stat2 = _fwd_mlp(hb, h, loss_target.reshape(tokens, D_MODEL), full["w_ff1"], full["w_ff2"], ln2_g, ln2_b, seq=seq)

    du, dz1, dz1b, stat1 = _bwd_mlp(dz2, dz2b, u, xhat1, rstd1, full["w_ff1"], full["w_ff2"], ln1_g, seq=seq)
    dgates, dy_a, dy_b, do_a, do_b, stat_g = _bwd_mix(dz1b, gates, y_a, y_b, b_gate_full, full["w_o_mla"], full["w_o_dil"], full["w_out"], seq=seq)
    dqp, dkp, dvm = _attn_bwd(qp, kp, vm, o_a, do_a, lse_a, no_bias, name="mla_attention_bwd", **mla)
    dq_d, dk_d, dv_d = _attn_bwd(qkvd, qkvd, qkvd, o_b, do_b, lse_b, dil_bias, name="dilated_attention_bwd", **dil)
    dqkvd = jnp.concatenate([dq_d, dk_d, dv_d], axis=1)
    grad_x, dproj, d_a, d_b, stat_r = _bwd_proj(dqp, dkp, dvm, dqkvd, dgates, dz1, low, w_in_ext, w1, w2, wk_ext, wv, e128,
                                                g_q_a, g_kv_a, cext, sext, cs128, seq=seq)

    dw_in_ext = _wgrad(xb, dproj, "wgrad_in")
    dw1 = _wgrad(qn, d_a, "wgrad_uq_direct")
    dw2 = _wgrad(qn, d_b, "wgrad_uq_rotated")
    dwk = _wgrad(kvn, dkp, "wgrad_ukv_k")
    dwv = _wgrad(kvn, dvm, "wgrad_ukv_v")
    g_full = {
        "w_o_mla": _wgrad(o_a, dy_a, "wgrad_o_mla"), "w_o_dil": _wgrad(o_b, dy_b, "wgrad_o_dil"),
        "w_out": _wgrad(mix, dz1b, "wgrad_out"), "w_ff1": _wgrad(hb, du, "wgrad_ff1"),
        "w_ff2": _wgrad(u, dz2b, "wgrad_ff2", square_relu=True),
    }
    dlow = dw_in_ext[:, :LOW_W]
    dw_kr = dlow[:, s1:s2] + _unrot_cols(dlow[:, s2:s2 + ROPE])
    g_full["w_in"] = jnp.concatenate([dlow[:, :s1], dw_kr, dw_in_ext[:, LOW_W:]], axis=1)
    n1, r1 = _split_slabs(dw1)
    _, r2 = _split_slabs(dw2)
    g_full["w_uq"] = jnp.concatenate([n1, r1 + _unrot_cols(r2)], axis=2).reshape(Q_LORA, N_HEADS * (NOPE + ROPE))
    nk, _ = _split_slabs(dwk)
    g_full["w_ukv"] = jnp.concatenate([nk, dwv.reshape(KV_LORA, N_HEADS, HEAD_V)], axis=2).reshape(KV_LORA, N_HEADS * (NOPE + HEAD_V))

    g_full["b_gate"] = stat_g[0:2]
    gblocks = [_shards_of(g_full[n], by_cols) for n, _, by_cols in _SHARDED]
    summed = _reduce_scatter(_pack(gblocks, BF16, lead=(N_DEV,)))
    g_parts = _unpack_summed(summed)
    grads = {n: p for (n, _, _), p in zip(_SHARDED, g_parts)}
    grads["b_gate"] = grads["b_gate"][0:2]

    small_w = [g_q_a, g_kv_a, ln1_g, ln1_b, ln2_g, ln2_b]
    small_m = [m_g_q_a, m_g_kv_a, m_ln1_g, m_ln1_b, m_ln2_g, m_ln2_b]
    small_v = [v_g_q_a, v_g_kv_a, v_ln1_g, v_ln1_b, v_ln2_g, v_ln2_b]
    widths = [a.shape[1] for a in small_w]
    partial = jnp.concatenate([stat_r[0:1, :Q_LORA], stat_r[1:2, :KV_LORA], stat1[0:1], stat1[1:2], stat2[0:1], stat2[1:2],
                               stat2[2:3, :LANES]], axis=1)

    def as_rows(vecs, extra):
        flat = jnp.concatenate(vecs + [jnp.zeros((1, extra), F32)], axis=1)
        return _pad_rows(flat.reshape(-1, LANES), 8)

    every = _all_gather(_pad_rows(partial.reshape(-1, LANES), 8), "all_gather_replicated_grads")
    g_s, d_s, nm_s, nv_s = _adamw_small(every, as_rows(small_w, LANES), as_rows(small_m, LANES), as_rows(small_v, LANES))

    def split_small(a):
        flat = a.reshape(1, -1)
        out, c0 = [], 0
        for w in widths:
            out.append(flat[:, c0:c0 + w])
            c0 += w
        return out, flat[0, c0]

    g_small, loss = split_small(g_s)
    d_small, nm_small, nv_small = split_small(d_s)[0], split_small(nm_s)[0], split_small(nv_s)[0]

    upd = {}
    for n, shape, _ in _SHARDED:
        upd[n] = _adamw(sharded_w[n][0], grads[n], sharded_m[n][0], sharded_v[n][0], "adamw_" + n)

    order = ["w_in", "b_gate", "g_q_a", "w_uq", "g_kv_a", "w_ukv", "w_o_mla", "w_o_dil", "w_out", "ln1_g", "ln1_b", "w_ff1", "w_ff2", "ln2_g", "ln2_b"]
    small_names = ["g_q_a", "g_kv_a", "ln1_g", "ln1_b", "ln2_g", "ln2_b"]

    def pick(kind):
        out = []
        for n in order:
            if n in small_names:
                k = small_names.index(n)
                out.append((g_small, d_small, nm_small, nv_small)[kind][k])
            elif kind == 0:
                out.append(grads[n][None])
            else:
                out.append(upd[n][kind - 1][None])
        return out

    return (loss, grad_x.reshape(batch, seq, D_MODEL), *pick(0), *pick(1), *pick(2), *pick(3))


def _unpack_gathered(gathered):
    out, r0 = [], 0
    for _, (r, c), _ in _SHARDED[:-1]:
        n_rows = r * c // LANES
        out.append(gathered[:, r0:r0 + n_rows, :].reshape(N_DEV, r, c))
        r0 += n_rows + (-n_rows) % 16
    out.append(gathered[:, r0:r0 + 4, :])
    return out


def _unpack_summed(summed):
    out, r0 = [], 0
    for _, (r, c), _ in _SHARDED[:-1]:
        n_rows = r * c // LANES
        out.append(summed[r0:r0 + n_rows, :].reshape(r, c))
        r0 += n_rows + (-n_rows) % 16
    out.append(summed[r0:r0 + 4, :])
    return out
```

```python
import functools
import math

import jax
import jax.numpy as jnp
from jax import lax
from jax.experimental import pallas as pl
from jax.experimental.pallas import tpu as pltpu

F32 = jnp.float32
BF16 = jnp.bfloat16
I32 = jnp.int32

D_MODEL = 1024
N_HEADS = 8
NOPE = 64
ROPE = 32
HEAD_V = 64
Q_LORA = 384
KV_LORA = 256
DIL_WIDTH = 512
D_FF = 4096
ROPE_THETA = 10000.0
LN_EPS = 1e-5
RMS_EPS = 1e-6
NEG = -1e30
ALPHA = 2.0 ** 0.25
MLA_SCALE = (NOPE + ROPE) ** -0.5
DIL_SCALE = 64 ** -0.5
ADAM_LR, ADAM_B1, ADAM_B2, ADAM_EPS, ADAM_WD, ADAM_STEP = 0.001, 0.9, 0.999, 1e-08, 0.01, 10

LANES = 128
PAIR_W = 256
N_PAIRS = N_HEADS // 2
LOW_W = 768
IN_EXT = LOW_W + 3 * DIL_WIDTH + 2 * D_MODEL
N_DEV = 8
FF_SHARD = D_FF // N_DEV
TOKEN_TILE = 256
ATTN_TILE = 256
VMEM_LIMIT = 56 << 20

MESH = pl.DeviceIdType.MESH
ANY = pl.BlockSpec(memory_space=pl.ANY)
CHIP_FLIPS = ((0, 0), (0, 1), (1, 0), (1, 1))
PEER_FLIPS = tuple((fx, fy, fc) for fx in (0, 1) for fy in (0, 1) for fc in (0, 1))[1:]


def _cp(*sem):
    return pltpu.CompilerParams(dimension_semantics=sem or None, vmem_limit_bytes=VMEM_LIMIT)


def _full(shape):
    nd = len(shape)
    return pl.BlockSpec(shape, lambda *_: (0,) * nd)


def _rows(tm, width):
    return pl.BlockSpec((tm, width), lambda i, *_: (i, 0))


def _dot(a, b):
    return jnp.dot(a, b, preferred_element_type=F32)


def _dot_nt(a, b):
    return lax.dot_general(a, b, (((1,), (1,)), ((), ())), preferred_element_type=F32)


def _dot_tn(a, b):
    return lax.dot_general(a, b, (((0,), (0,)), ((), ())), preferred_element_type=F32)


def _sigmoid(z):
    return 1.0 / (1.0 + jnp.exp(-z))


def _place():
    return lax.axis_index("x"), lax.axis_index("y"), lax.axis_index("c")


def _flip(v, f):
    return 1 - v if f else v


class _Gather:
    def __init__(self, shards):
        self.n = len(shards)
        self.out_shape = [jax.ShapeDtypeStruct((N_DEV, *s.shape), s.dtype) for s in shards]
        self.scratch = [pltpu.SemaphoreType.DMA((7 * self.n,)), pltpu.SemaphoreType.DMA((7 * self.n,)),
                        pltpu.SemaphoreType.DMA((self.n,))]

    def _copies(self, what, srcs, dsts, send, recv, local):
        x, y, c = _place()
        chips = [(_flip(x, fx), _flip(y, fy)) for fx, fy in CHIP_FLIPS[1:]]
        out = []
        for a in range(self.n):
            def slot(px, py, pc, a=a):
                return dsts[a].at[4 * px + 2 * py + pc]

            def copy(k, block, to, src=None, a=a, slot=slot):
                return pltpu.make_async_remote_copy(
                    src_ref=slot(*block) if src is None else src, dst_ref=slot(*block),
                    send_sem=send.at[7 * a + k], recv_sem=recv.at[7 * a + k], device_id=to, device_id_type=MESH)

            if what == "mine":
                out.append(pltpu.make_async_copy(srcs[a], slot(x, y, c), local.at[a]))
            elif what == "first":
                out.append(copy(0, (x, y, c), (x, y, 1 - c), src=srcs[a]))
                out += [copy(1 + j, (x, y, c), (*chip, c), src=srcs[a]) for j, chip in enumerate(chips)]
            elif what == "landed":
                out += [copy(1 + j, (*chip, c), (x, y, c)) for j, chip in enumerate(chips)]
            elif what == "passed":
                out += [copy(4 + j, (*chip, c), (x, y, 1 - c)) for j, chip in enumerate(chips)]
            else:
                out.append(copy(0, (x, y, 1 - c), (x, y, c)))
                out += [copy(4 + j, (*chip, 1 - c), (x, y, c)) for j, chip in enumerate(chips)]
        return out

    def start(self, *refs):
        for cp in self._copies("mine", *refs) + self._copies("first", *refs):
            cp.start()

    def forward(self, *refs):
        for landed, passed in zip(self._copies("landed", *refs), self._copies("passed", *refs)):
            landed.wait_recv()
            passed.start()

    def finish(self, *refs):
        for cp in self._copies("from_sibling", *refs):
            cp.wait_recv()
        for cp in self._copies("first", *refs) + self._copies("passed", *refs):
            cp.wait_send()
        for cp in self._copies("mine", *refs):
            cp.wait()


class _Scatter:
    def __init__(self, arrays):
        self.n = len(arrays)
        self.out_shape = [jax.ShapeDtypeStruct(a.shape, a.dtype) for a in arrays]
        self.scratch = [pltpu.SemaphoreType.DMA((7 * self.n,)), pltpu.SemaphoreType.DMA((7 * self.n,)),
                        pltpu.SemaphoreType.DMA((self.n,))]

    def _copies(self, what, srcs, dsts, send, recv, local):
        x, y, c = _place()
        me = 4 * x + 2 * y + c
        out = []
        for a in range(self.n):
            if what == "mine":
                out.append(pltpu.make_async_copy(srcs[a].at[me], dsts[a].at[me], local.at[a]))
                continue
            for k, (fx, fy, fc) in enumerate(PEER_FLIPS):
                px, py, pc = _flip(x, fx), _flip(y, fy), _flip(c, fc)
                peer = 4 * px + 2 * py + pc
                out.append(pltpu.make_async_remote_copy(
                    src_ref=srcs[a].at[peer], dst_ref=dsts[a].at[me if what == "out" else peer],
                    send_sem=send.at[7 * a + k], recv_sem=recv.at[7 * a + k], device_id=(px, py, pc), device_id_type=MESH))
        return out

    def start(self, *refs):
        for cp in self._copies("mine", *refs) + self._copies("out", *refs):
            cp.start()

    def forward(self, *refs):
        pass

    def finish(self, *refs):
        for cp in self._copies("out", *refs):
            cp.wait_send()
        for cp in self._copies("in", *refs):
            cp.wait_recv()
        for cp in self._copies("mine", *refs):
            cp.wait()


def _run_comm(comm, arrays, name):
    n = comm.n

    def body(*refs):
        args = (refs[:n], refs[n:2 * n], *refs[2 * n:])
        comm.start(*args)
        comm.forward(*args)
        comm.finish(*args)

    return pl.pallas_call(body, name=name, out_shape=comm.out_shape, in_specs=[ANY] * n, out_specs=[ANY] * n,
                          scratch_shapes=comm.scratch)(*arrays)


def _rs_sibling(arrays, name):
    n = len(arrays)
    quarter = [jax.ShapeDtypeStruct((4, *a.shape[1:]), a.dtype) for a in arrays]

    def body(*refs):
        srcs, mine, got, (send, recv, local) = refs[:n], refs[n:2 * n], refs[2 * n:3 * n], refs[3 * n:]
        x, y, c = _place()
        local_copies, remote = [], []
        for a in range(n):
            for r, (fx, fy) in enumerate(CHIP_FLIPS):
                chip = 2 * _flip(x, fx) + _flip(y, fy)
                local_copies.append(pltpu.make_async_copy(srcs[a].at[2 * chip + c], mine[a].at[r], local.at[4 * a + r]))
                remote.append(pltpu.make_async_remote_copy(
                    src_ref=srcs[a].at[2 * chip + 1 - c], dst_ref=got[a].at[r], send_sem=send.at[4 * a + r],
                    recv_sem=recv.at[4 * a + r], device_id=(x, y, 1 - c), device_id_type=MESH))
        for cp in local_copies + remote:
            cp.start()
        for cp in remote:
            cp.wait_send()
        for cp in remote:
            cp.wait_recv()
        for cp in local_copies:
            cp.wait()

    out = pl.pallas_call(
        body, name=name, out_shape=quarter + quarter, in_specs=[ANY] * n, out_specs=[ANY] * (2 * n),
        scratch_shapes=[pltpu.SemaphoreType.DMA((4 * n,)), pltpu.SemaphoreType.DMA((4 * n,)), pltpu.SemaphoreType.DMA((4 * n,))],
    )(*arrays)
    return out[:n], out[n:]


def _rs_chips(arrays, name):
    n = len(arrays)

    def body(*refs):
        srcs, dsts, (send, recv) = refs[:n], refs[n:2 * n], refs[2 * n:]
        x, y, c = _place()
        copies = []
        for a in range(n):
            for k, (fx, fy) in enumerate(CHIP_FLIPS[1:]):
                copies.append(pltpu.make_async_remote_copy(
                    src_ref=srcs[a].at[k], dst_ref=dsts[a].at[k], send_sem=send.at[3 * a + k], recv_sem=recv.at[3 * a + k],
                    device_id=(_flip(x, fx), _flip(y, fy), c), device_id_type=MESH))
        for cp in copies:
            cp.start()
        for cp in copies:
            cp.wait_send()
        for cp in copies:
            cp.wait_recv()

    return pl.pallas_call(
        body, name=name, out_shape=[jax.ShapeDtypeStruct(a.shape, a.dtype) for a in arrays], in_specs=[ANY] * n,
        out_specs=[ANY] * n, scratch_shapes=[pltpu.SemaphoreType.DMA((3 * n,)), pltpu.SemaphoreType.DMA((3 * n,))],
    )(*arrays)


def _row_tile(rows):
    return 256 if rows % 256 == 0 else rows


def _pair_sum(a, b, name):
    _, rows, cols = a.shape
    tr = _row_tile(rows)

    def body(a_ref, b_ref, own_ref, rest_ref):
        s = a_ref[...].astype(F32) + b_ref[...].astype(F32)
        own_ref[...] = s[0]
        rest_ref[...] = s[1:].astype(BF16)

    return pl.pallas_call(
        body, name=name, grid=(rows // tr,),
        out_shape=(jax.ShapeDtypeStruct((rows, cols), F32), jax.ShapeDtypeStruct((3, rows, cols), BF16)),
        in_specs=[pl.BlockSpec((4, tr, cols), lambda i: (0, i, 0))] * 2,
        out_specs=(pl.BlockSpec((tr, cols), lambda i: (i, 0)), pl.BlockSpec((3, tr, cols), lambda i: (0, i, 0))),
        compiler_params=_cp("parallel"),
    )(a, b)


def _head_lanes(width, h):
    lane = lax.broadcasted_iota(I32, (1, width), 1)
    if width == LANES:
        return (lane >= 64 * h) & (lane < 64 * h + 64)
    nope = (lane >= NOPE * h) & (lane < NOPE * h + NOPE)
    rope = (lane >= 2 * NOPE + ROPE * h) & (lane < 2 * NOPE + ROPE * h + ROPE)
    return nope | rope


def _dilated_bias_table(seq):
    t = min(ATTN_TILE, seq)
    nd = seq // t

    def body(o_ref):
        h, d = pl.program_id(0), pl.program_id(1)
        delta = d * t + lax.broadcasted_iota(I32, (t, t), 1) - lax.broadcasted_iota(I32, (t, t), 0)
        mult = ((delta <= 128).astype(I32) + (((delta & 3) == 0) & (delta <= 512)).astype(I32)
                + ((delta & 15) == 0).astype(I32))
        logm = jnp.where(mult == 3, math.log(3.0), jnp.where(mult == 2, math.log(2.0), 0.0))
        slope = lax.bitcast_convert_type(jnp.broadcast_to((126 - h) << 23, (t, t)).astype(I32), F32)
        o_ref[...] = jnp.where((delta >= 0) & (mult > 0), logm - slope * delta.astype(F32), NEG)

    return pl.pallas_call(
        body, name="dilated_bias_table", grid=(N_HEADS, nd), out_shape=jax.ShapeDtypeStruct((N_HEADS, nd, t, t), F32),
        out_specs=pl.BlockSpec((None, None, t, t), lambda h, d: (h, d, 0, 0)),
        compiler_params=_cp("parallel", "parallel"),
    )()


def _comm_hooks(comm, refs, n_in, n_out):
    if comm is None:
        return refs[:n_in], refs[n_in:n_in + n_out], refs[n_in + n_out:], None
    n = comm.n
    ins, srcs = refs[:n_in], refs[n_in:n_in + n]
    outs, dsts = refs[n_in + n:n_in + n + n_out], refs[n_in + n + n_out:n_in + 2 * n + n_out]
    rest = refs[n_in + 2 * n + n_out:]
    return ins, outs, rest[:len(rest) - 3], (srcs, dsts, *rest[len(rest) - 3:])


def _attn_fwd(q, k, v, bias, *, batch, seq, width, col0, dilated, scale, name, comm=None, comm_arrays=()):
    t = min(ATTN_TILE, seq)
    nq = seq // t
    cq, ck, cv = col0
    pre = scale if dilated else 1.0
    steps = batch * N_PAIRS * nq

    def body(*refs):
        (q_ref, k_ref, v_ref, bias_ref), (o_ref, lse_ref), _, plan = _comm_hooks(comm, refs, 4, 2)
        i = pl.program_id(2)
        step_no = (pl.program_id(0) * N_PAIRS + pl.program_id(1)) * nq + i
        if plan:
            pl.when(step_no == 0)(lambda: comm.start(*plan))
            pl.when(step_no == (3 * steps) // 4)(lambda: comm.forward(*plan))
        q2 = q_ref[...] * pre if dilated else q_ref[...]
        qh = [jnp.where(_head_lanes(width, h), q2, jnp.zeros_like(q2)) for h in (0, 1)]
        vlane = [_head_lanes(LANES, h) for h in (0, 1)]
        top = lax.broadcasted_iota(I32, (LANES, t), 0) < HEAD_V
        causal = lax.broadcasted_iota(I32, (t, t), 0) <= lax.broadcasted_iota(I32, (t, t), 1)

        def step(j, carry, diagonal):
            m0, l0, m1, l1, acc = carry
            ks = pl.multiple_of(j * t, t)
            kj = k_ref[pl.ds(ks, t), :]
            vj = v_ref[pl.ds(ks, t), :]
            new, alphas, pv = [], [], []
            for h, (m, l) in enumerate(((m0, l0), (m1, l1))):
                s = _dot_nt(kj, qh[h])
                if dilated:
                    s = s + bias_ref[h, i - j]
                else:
                    s = s * scale
                    if diagonal:
                        s = jnp.where(causal, s, NEG)
                m_new = jnp.maximum(m, jnp.max(s, axis=0, keepdims=True))
                a = jnp.exp(m - m_new)
                p = jnp.exp(s - m_new)
                new += [m_new, a * l + jnp.sum(p, axis=0, keepdims=True)]
                alphas.append(a)
                pv.append(_dot_tn(jnp.where(vlane[h], vj, jnp.zeros_like(vj)), p.astype(BF16)))
            acc = jnp.where(top, alphas[0], alphas[1]) * acc + pv[0] + pv[1]
            return (*new, acc)

        row = jnp.full((1, t), NEG, F32)
        zero = jnp.zeros((1, t), F32)
        init = (row, zero, row, zero, jnp.zeros((LANES, t), F32))
        if dilated:
            carry = lax.fori_loop(0, i + 1, functools.partial(step, diagonal=False), init)
        else:
            carry = step(i, lax.fori_loop(0, i, functools.partial(step, diagonal=False), init), True)
        m0, l0, m1, l1, acc = carry
        o_ref[...] = jnp.transpose(acc * jnp.where(top, 1.0 / l0, 1.0 / l1)).astype(BF16)
        r = lax.broadcasted_iota(I32, (8, t), 0)
        lse_ref[...] = jnp.where(r == 0, m0 + jnp.log(l0), jnp.where(r == 1, m1 + jnp.log(l1), 0.0))
        if plan:
            pl.when(step_no == steps - 1)(lambda: comm.finish(*plan))

    bias_spec = (pl.BlockSpec((2, nq, t, t), lambda b, p, i: (p, 0, 0, 0)) if dilated
                 else pl.BlockSpec((None, 8, LANES), lambda b, p, i: (0, 0, 0)))
    n = comm.n if comm else 0
    return pl.pallas_call(
        body, name=name, grid=(batch, N_PAIRS, nq),
        out_shape=[jax.ShapeDtypeStruct((batch * seq, DIL_WIDTH), BF16), jax.ShapeDtypeStruct((batch * N_PAIRS, 8, seq), F32)]
        + (comm.out_shape if comm else []),
        in_specs=[pl.BlockSpec((t, width), lambda b, p, i: (b * nq + i, cq + p)),
                  pl.BlockSpec((seq, width), lambda b, p, i: (b, ck + p)),
                  pl.BlockSpec((seq, LANES), lambda b, p, i: (b, cv + p)),
                  bias_spec] + [ANY] * n,
        out_specs=[pl.BlockSpec((t, LANES), lambda b, p, i: (b * nq + i, p)),
                   pl.BlockSpec((None, 8, t), lambda b, p, i: (b * N_PAIRS + p, 0, i))] + [ANY] * n,
        scratch_shapes=comm.scratch if comm else [],
        compiler_params=_cp("arbitrary", "arbitrary", "arbitrary") if comm else _cp("parallel", "parallel", "arbitrary"),
    )(q, k, v, bias, *comm_arrays)


def _attn_bwd(q, k, v, o, do, lse, bias, *, batch, seq, width, col0, dilated, scale, name, comm=None, comm_arrays=()):
    t = min(ATTN_TILE, seq)
    nq = seq // t
    cq, ck, cv = col0
    pre = scale if dilated else 1.0
    steps = batch * N_PAIRS

    def body(*refs):
        ins, (dq_ref, dk_ref, dv_ref), (dq_acc, dk_acc, dv_acc, rowdot), plan = _comm_hooks(comm, refs, 7, 3)
        q_ref, k_ref, v_ref, o_ref, do_ref, lse_ref, bias_ref = ins
        step_no = pl.program_id(0) * N_PAIRS + pl.program_id(1)
        if plan:
            pl.when(step_no == 0)(lambda: comm.start(*plan))
        wlane = [_head_lanes(width, h) for h in (0, 1)]
        vlane = [_head_lanes(LANES, h) for h in (0, 1)]
        causal = lax.broadcasted_iota(I32, (t, t), 0) <= lax.broadcasted_iota(I32, (t, t), 1)
        prod = jnp.transpose(do_ref[...].astype(F32) * o_ref[...].astype(F32))
        rowdot[0:1, :] = jnp.sum(prod[0:HEAD_V], axis=0, keepdims=True)
        rowdot[1:2, :] = jnp.sum(prod[HEAD_V:], axis=0, keepdims=True)
        dq_acc[...] = jnp.zeros_like(dq_acc)

        def k_tile(j, _):
            ks = pl.multiple_of(j * t, t)
            kj = k_ref[pl.ds(ks, t), :]
            vj = v_ref[pl.ds(ks, t), :]
            kh = [jnp.where(wlane[h], kj, jnp.zeros_like(kj)) for h in (0, 1)]
            dk_acc[...] = jnp.zeros_like(dk_acc)
            dv_acc[...] = jnp.zeros_like(dv_acc)

            def q_tile(i, _, diagonal):
                qs = pl.multiple_of(i * t, t)
                qi = q_ref[pl.ds(qs, t), :] * pre if dilated else q_ref[pl.ds(qs, t), :]
                doi = do_ref[pl.ds(qs, t), :]
                dq_i = jnp.zeros((t, width), F32)
                for h in (0, 1):
                    qih = jnp.where(wlane[h], qi, jnp.zeros_like(qi))
                    doih = jnp.where(vlane[h], doi, jnp.zeros_like(doi))
                    s = _dot_nt(kj, qih)
                    if dilated:
                        s = s + bias_ref[h, i - j]
                    else:
                        s = s * scale
                        if diagonal:
                            s = jnp.where(causal, s, NEG)
                    p = jnp.exp(s - lse_ref[h:h + 1, pl.ds(qs, t)])
                    dp = _dot_nt(vj, doih)
                    ds = p * (dp - rowdot[h:h + 1, pl.ds(qs, t)])
                    ds = (ds if dilated else ds * scale).astype(BF16)
                    dv_acc[...] += _dot(p.astype(BF16), doih)
                    dk_acc[...] += _dot(ds, qih)
                    dq_i = dq_i + _dot_tn(ds, kh[h])
                dq_acc[pl.ds(qs, t), :] += dq_i
                return 0

            if dilated:
                lax.fori_loop(j, nq, functools.partial(q_tile, diagonal=False), 0)
            else:
                q_tile(j, 0, True)
                lax.fori_loop(j + 1, nq, functools.partial(q_tile, diagonal=False), 0)
            dk_ref[pl.ds(ks, t), :] = dk_acc[...].astype(BF16)
            dv_ref[pl.ds(ks, t), :] = dv_acc[...].astype(BF16)
            return 0

        lax.fori_loop(0, nq, k_tile, 0)
        dq_ref[...] = (dq_acc[...] * pre).astype(BF16)
        if plan:
            pl.when(step_no == steps - 1)(lambda: comm.finish(*plan))

    tokens = batch * seq
    bias_spec = (pl.BlockSpec((2, nq, t, t), lambda b, p: (p, 0, 0, 0)) if dilated
                 else pl.BlockSpec((None, 8, LANES), lambda b, p: (0, 0, 0)))
    n = comm.n if comm else 0
    return pl.pallas_call(
        body, name=name, grid=(batch, N_PAIRS),
        out_shape=[jax.ShapeDtypeStruct((tokens, N_PAIRS * width), BF16), jax.ShapeDtypeStruct((tokens, N_PAIRS * width), BF16),
                   jax.ShapeDtypeStruct((tokens, DIL_WIDTH), BF16)] + (comm.out_shape if comm else []),
        in_specs=[pl.BlockSpec((seq, width), lambda b, p: (b, cq + p)),
                  pl.BlockSpec((seq, width), lambda b, p: (b, ck + p)),
                  pl.BlockSpec((seq, LANES), lambda b, p: (b, cv + p)),
                  pl.BlockSpec((seq, LANES), lambda b, p: (b, p)),
                  pl.BlockSpec((seq, LANES), lambda b, p: (b, p)),
                  pl.BlockSpec((None, 8, seq), lambda b, p: (b * N_PAIRS + p, 0, 0)),
                  bias_spec] + [ANY] * n,
        out_specs=[pl.BlockSpec((seq, width), lambda b, p: (b, p)),
                   pl.BlockSpec((seq, width), lambda b, p: (b, p)),
                   pl.BlockSpec((seq, LANES), lambda b, p: (b, p))] + [ANY] * n,
        scratch_shapes=[pltpu.VMEM((seq, width), F32), pltpu.VMEM((t, width), F32), pltpu.VMEM((t, LANES), F32),
                        pltpu.VMEM((8, seq), F32)] + (comm.scratch if comm else []),
        compiler_params=_cp("arbitrary", "arbitrary") if comm else _cp("parallel", "parallel"),
    )(q, k, v, o, do, lse, bias, *comm_arrays)


def _rms(xf, g):
    r = lax.rsqrt(jnp.mean(xf * xf, axis=1, keepdims=True) + RMS_EPS)
    return xf * r * g, r


def _rms_bwd(dy, xf, r, g):
    gy = dy * g
    dx = r * gy - xf * (r * r * r) * jnp.mean(gy * xf, axis=1, keepdims=True)
    return dx, dy * xf * r


def _ln_bwd(dy, xhat, rstd, g):
    dxh = dy * g
    return rstd * (dxh - jnp.mean(dxh, axis=1, keepdims=True) - xhat * jnp.mean(dxh * xhat, axis=1, keepdims=True))


def _fwd_proj(xb, w_in_ext, w1, w2, wk_ext, wv, e128, g_q, g_kv, cext, sext, cs128, *, seq):
    tokens = xb.shape[0]
    tm = min(TOKEN_TILE, seq)
    ns = seq // tm

    def body(x_ref, win_ref, w1_ref, w2_ref, wk_ref, wv_ref, e_ref, gq_ref, gkv_ref, c_ref, s_ref, cs_ref,
             low_ref, gates_ref, qkvd_ref, qp_ref, kp_ref, vm_ref, qn_ref, kvn_ref):
        xt = x_ref[...]
        low = _dot(xt, win_ref[:, 0:LOW_W])
        low_ref[...] = low
        qkvd_ref[...] = _dot(xt, win_ref[:, LOW_W:LOW_W + 3 * DIL_WIDTH]).astype(BF16)
        gates_ref[...] = _dot(xt, win_ref[:, LOW_W + 3 * DIL_WIDTH:])
        qn = _rms(low[:, 0:Q_LORA], gq_ref[...])[0].astype(BF16)
        kvn = _rms(low[:, Q_LORA:Q_LORA + KV_LORA], gkv_ref[...])[0].astype(BF16)
        qn_ref[...] = qn
        kvn_ref[...] = kvn
        qp_ref[...] = (_dot(qn, w1_ref[...]) * c_ref[...] + _dot(qn, w2_ref[...]) * s_ref[...]).astype(BF16)
        kr = low[:, Q_LORA + KV_LORA:] * cs_ref[...]
        kr = kr + pltpu.roll(kr, LANES - ROPE, 1)
        lane = lax.broadcasted_iota(I32, kr.shape, 1)
        kr = jnp.where(lane < ROPE, kr, 0.0).astype(BF16)
        kp_ref[...] = (_dot(kvn, wk_ref[...]) + _dot(kr, e_ref[...])).astype(BF16)
        vm_ref[...] = _dot(kvn, wv_ref[...]).astype(BF16)

    n_gates = 2 * D_MODEL
    outs = [(LOW_W, F32), (n_gates, F32), (3 * DIL_WIDTH, BF16), (N_PAIRS * PAIR_W, BF16), (N_PAIRS * PAIR_W, BF16),
            (DIL_WIDTH, BF16), (Q_LORA, BF16), (KV_LORA, BF16)]
    return pl.pallas_call(
        body, name="fwd_proj", grid=(tokens // tm,),
        out_shape=tuple(jax.ShapeDtypeStruct((tokens, w), dt) for w, dt in outs),
        in_specs=[_rows(tm, D_MODEL), _full(w_in_ext.shape), _full(w1.shape), _full(w2.shape), _full(wk_ext.shape),
                  _full(wv.shape), _full(e128.shape), _full(g_q.shape), _full(g_kv.shape),
                  pl.BlockSpec((tm, N_PAIRS * PAIR_W), lambda i: (i % ns, 0)),
                  pl.BlockSpec((tm, N_PAIRS * PAIR_W), lambda i: (i % ns, 0)),
                  pl.BlockSpec((tm, LANES), lambda i: (i % ns, 0))],
        out_specs=tuple(_rows(tm, w) for w, _ in outs),
        compiler_params=_cp("parallel"),
    )(xb, w_in_ext, w1, w2, wk_ext, wv, e128, g_q, g_kv, cext, sext, cs128)


def _fwd_mix(o_a, o_b, gates, x, b_gate, w_oa, w_ob, w_out, ln_g, ln_b, *, seq):
    tokens = x.shape[0]
    tm = min(TOKEN_TILE, seq)

    def body(oa_ref, ob_ref, gt_ref, x_ref, bg_ref, woa_ref, wob_ref, wout_ref, g_ref, b_ref,
             h_ref, hb_ref, xhat_ref, rstd_ref, ya_ref, yb_ref, mix_ref):
        ya = _dot(oa_ref[...], woa_ref[...])
        yb = _dot(ob_ref[...], wob_ref[...])
        g0 = _sigmoid(gt_ref[:, 0:D_MODEL] + bg_ref[0:1, :])
        g1 = _sigmoid(gt_ref[:, D_MODEL:] + bg_ref[1:2, :])
        mix = (g0 * ya + g1 * yb).astype(BF16)
        z = ALPHA * x_ref[...] + _dot(mix, wout_ref[...])
        zc = z - jnp.mean(z, axis=1, keepdims=True)
        rstd = lax.rsqrt(jnp.mean(zc * zc, axis=1, keepdims=True) + LN_EPS)
        xhat = zc * rstd
        h = xhat * g_ref[...] + b_ref[...]
        h_ref[...] = h
        hb_ref[...] = h.astype(BF16)
        xhat_ref[...] = xhat
        rstd_ref[...] = jnp.broadcast_to(rstd, (tm, LANES))
        ya_ref[...] = ya.astype(BF16)
        yb_ref[...] = yb.astype(BF16)
        mix_ref[...] = mix

    outs = [(D_MODEL, F32), (D_MODEL, BF16), (D_MODEL, F32), (LANES, F32), (D_MODEL, BF16), (D_MODEL, BF16), (D_MODEL, BF16)]
    return pl.pallas_call(
        body, name="fwd_mix", grid=(tokens // tm,),
        out_shape=tuple(jax.ShapeDtypeStruct((tokens, w), dt) for w, dt in outs),
        in_specs=[_rows(tm, DIL_WIDTH), _rows(tm, DIL_WIDTH), _rows(tm, 2 * D_MODEL), _rows(tm, D_MODEL),
                  _full(b_gate.shape), _full(w_oa.shape), _full(w_ob.shape), _full(w_out.shape),
                  _full(ln_g.shape), _full(ln_b.shape)],
        out_specs=tuple(_rows(tm, w) for w, _ in outs),
        compiler_params=_cp("parallel"),
    )(o_a, o_b, gates, x, b_gate, w_oa, w_ob, w_out, ln_g, ln_b)


def _fwd_mlp(hb, h, target, w_ff1, w_ff2, ln_g, ln_b, *, seq):
    tokens = h.shape[0]
    tm = min(2 * TOKEN_TILE, seq)
    tf = FF_SHARD

    def body(hb_ref, h_ref, tg_ref, w1_ref, w2_ref, g_ref, b_ref, u_ref, dz_ref, dzb_ref, stat_ref, acc):
        i, j = pl.program_id(0), pl.program_id(1)

        @pl.when((i == 0) & (j == 0))
        def _():
            stat_ref[...] = jnp.zeros_like(stat_ref)

        @pl.when(j == 0)
        def _():
            acc[...] = jnp.zeros_like(acc)

        u = _dot(hb_ref[...], w1_ref[...])
        u_ref[...] = u.astype(BF16)
        a = jnp.square(jnp.maximum(u, 0.0)).astype(BF16)
        acc[...] += _dot(a, w2_ref[...])

        @pl.when(j == N_DEV - 1)
        def _():
            z = ALPHA * h_ref[...] + acc[...]
            zc = z - jnp.mean(z, axis=1, keepdims=True)
            rstd = lax.rsqrt(jnp.mean(zc * zc, axis=1, keepdims=True) + LN_EPS)
            xhat = zc * rstd
            err = xhat * g_ref[...] + b_ref[...] - tg_ref[...]
            dy = err * (1.0 / D_MODEL)
            dz = _ln_bwd(dy, xhat, rstd, g_ref[...])
            dz_ref[...] = dz
            dzb_ref[...] = dz.astype(BF16)
            stat_ref[0:1, :] += jnp.sum(dy * xhat, axis=0, keepdims=True)
            stat_ref[1:2, :] += jnp.sum(dy, axis=0, keepdims=True)
            stat_ref[2:3, :] += jnp.sum(jnp.sum(err * err, axis=1, keepdims=True), axis=0, keepdims=True) * (0.5 / D_MODEL)

    return pl.pallas_call(
        body, name="fwd_mlp", grid=(tokens // tm, N_DEV),
        out_shape=(jax.ShapeDtypeStruct((tokens, D_FF), BF16), jax.ShapeDtypeStruct((tokens, D_MODEL), F32),
                   jax.ShapeDtypeStruct((tokens, D_MODEL), BF16), jax.ShapeDtypeStruct((8, D_MODEL), F32)),
        in_specs=[_rows(tm, D_MODEL), _rows(tm, D_MODEL), _rows(tm, D_MODEL),
                  pl.BlockSpec((None, D_MODEL, tf), lambda i, j: (j, 0, 0)), pl.BlockSpec((tf, D_MODEL), lambda i, j: (j, 0)),
                  _full(ln_g.shape), _full(ln_b.shape)],
        out_specs=(pl.BlockSpec((tm, tf), lambda i, j: (i, j)), _rows(tm, D_MODEL), _rows(tm, D_MODEL), _full((8, D_MODEL))),
        scratch_shapes=[pltpu.VMEM((tm, D_MODEL), F32)],
        compiler_params=_cp("arbitrary", "arbitrary"),
    )(hb, h, target, w_ff1, w_ff2, ln_g, ln_b)


def _bwd_mlp(dz2, dz2b, u, xhat1, rstd1, w_ff1, w_ff2, ln_g, *, seq):
    tokens = dz2.shape[0]
    tm = min(2 * TOKEN_TILE, seq)
    tf = FF_SHARD

    def body(dz_ref, dzb_ref, u_ref, xh_ref, rs_ref, w1_ref, w2_ref, g_ref, du_ref, dz1_ref, dz1b_ref, stat_ref, acc):
        i, j = pl.program_id(0), pl.program_id(1)

        @pl.when((i == 0) & (j == 0))
        def _():
            stat_ref[...] = jnp.zeros_like(stat_ref)

        @pl.when(j == 0)
        def _():
            acc[...] = jnp.zeros_like(acc)

        da = _dot_nt(dzb_ref[...], w2_ref[...])
        du = (da * (2.0 * jnp.maximum(u_ref[...].astype(F32), 0.0))).astype(BF16)
        du_ref[...] = du
        acc[...] += _dot_nt(du, w1_ref[...])

        @pl.when(j == N_DEV - 1)
        def _():
            dh = ALPHA * dz_ref[...] + acc[...]
            xhat = xh_ref[...]
            dz1 = _ln_bwd(dh, xhat, rs_ref[:, 0:1], g_ref[...])
            dz1_ref[...] = dz1
            dz1b_ref[...] = dz1.astype(BF16)
            stat_ref[0:1, :] += jnp.sum(dh * xhat, axis=0, keepdims=True)
            stat_ref[1:2, :] += jnp.sum(dh, axis=0, keepdims=True)

    return pl.pallas_call(
        body, name="bwd_mlp", grid=(tokens // tm, N_DEV),
        out_shape=(jax.ShapeDtypeStruct((tokens, D_FF), BF16), jax.ShapeDtypeStruct((tokens, D_MODEL), F32),
                   jax.ShapeDtypeStruct((tokens, D_MODEL), BF16), jax.ShapeDtypeStruct((8, D_MODEL), F32)),
        in_specs=[_rows(tm, D_MODEL), _rows(tm, D_MODEL), pl.BlockSpec((tm, tf), lambda i, j: (i, j)),
                  _rows(tm, D_MODEL), _rows(tm, LANES),
                  pl.BlockSpec((None, D_MODEL, tf), lambda i, j: (j, 0, 0)), pl.BlockSpec((tf, D_MODEL), lambda i, j: (j, 0)),
                  _full(ln_g.shape)],
        out_specs=(pl.BlockSpec((tm, tf), lambda i, j: (i, j)), _rows(tm, D_MODEL), _rows(tm, D_MODEL), _full((8, D_MODEL))),
        scratch_shapes=[pltpu.VMEM((tm, D_MODEL), F32)],
        compiler_params=_cp("arbitrary", "arbitrary"),
    )(dz2, dz2b, u, xhat1, rstd1, w_ff1, w_ff2, ln_g)


def _bwd_mix(dz1b, gates, y_a, y_b, b_gate, w_oa, w_ob, w_out, *, seq):
    tokens = dz1b.shape[0]
    tm = min(TOKEN_TILE, seq)

    def body(dz_ref, gt_ref, ya_ref, yb_ref, bg_ref, woa_ref, wob_ref, wout_ref,
             dgt_ref, dya_ref, dyb_ref, doa_ref, dob_ref, stat_ref):
        @pl.when(pl.program_id(0) == 0)
        def _():
            stat_ref[...] = jnp.zeros_like(stat_ref)

        dmix = _dot_nt(dz_ref[...], wout_ref[...])
        for k, (y_ref, w_ref, dy_ref, do_ref) in enumerate(((ya_ref, woa_ref, dya_ref, doa_ref), (yb_ref, wob_ref, dyb_ref, dob_ref))):
            g = _sigmoid(gt_ref[:, k * D_MODEL:(k + 1) * D_MODEL] + bg_ref[k:k + 1, :])
            dgate = dmix * y_ref[...].astype(F32) * g * (1.0 - g)
            dgt_ref[:, k * D_MODEL:(k + 1) * D_MODEL] = dgate.astype(BF16)
            stat_ref[k:k + 1, :] += jnp.sum(dgate, axis=0, keepdims=True)
            dy = (dmix * g).astype(BF16)
            dy_ref[...] = dy
            do_ref[...] = _dot_nt(dy, w_ref[...]).astype(BF16)

    outs = [(2 * D_MODEL, BF16), (D_MODEL, BF16), (D_MODEL, BF16), (DIL_WIDTH, BF16), (DIL_WIDTH, BF16)]
    return pl.pallas_call(
        body, name="bwd_mix", grid=(tokens // tm,),
        out_shape=tuple(jax.ShapeDtypeStruct((tokens, w), dt) for w, dt in outs) + (jax.ShapeDtypeStruct((8, D_MODEL), F32),),
        in_specs=[_rows(tm, D_MODEL), _rows(tm, 2 * D_MODEL), _rows(tm, D_MODEL), _rows(tm, D_MODEL),
                  _full(b_gate.shape), _full(w_oa.shape), _full(w_ob.shape), _full(w_out.shape)],
        out_specs=tuple(_rows(tm, w) for w, _ in outs) + (_full((8, D_MODEL)),),
        compiler_params=_cp("arbitrary"),
    )(dz1b, gates, y_a, y_b, b_gate, w_oa, w_ob, w_out)


def _bwd_proj(dqp, dkp, dvm, dq_d, dk_d, dv_d, dgates, dz1, low, w_in_ext, w1, w2, wk_ext, wv, e128, g_q, g_kv, cext, sext, cs128, *, seq):
    tokens = dz1.shape[0]
    tm = min(TOKEN_TILE, seq)
    ns = seq // tm

    def body(dqp_ref, dkp_ref, dvm_ref, dqd_ref, dkd_ref, dvd_ref, dgt_ref, dz_ref, low_ref, win_ref, w1_ref, w2_ref, wk_ref,
             wv_ref, e_ref, gq_ref, gkv_ref, c_ref, s_ref, cs_ref, dx_ref, dproj_ref, da_ref, db_ref, stat_ref):
        @pl.when(pl.program_id(0) == 0)
        def _():
            stat_ref[...] = jnp.zeros_like(stat_ref)

        low = low_ref[...]
        dqp = dqp_ref[...].astype(F32)
        d_a = (dqp * c_ref[...]).astype(BF16)
        d_b = (dqp * s_ref[...]).astype(BF16)
        da_ref[...] = d_a
        db_ref[...] = d_b
        q_a = low[:, 0:Q_LORA]
        _, rq = _rms(q_a, gq_ref[...])
        dq_a, gq_terms = _rms_bwd(_dot_nt(d_a, w1_ref[...]) + _dot_nt(d_b, w2_ref[...]), q_a, rq, gq_ref[...])
        kv_a = low[:, Q_LORA:Q_LORA + KV_LORA]
        _, rkv = _rms(kv_a, gkv_ref[...])
        dkp = dkp_ref[...]
        dkv_a, gkv_terms = _rms_bwd(_dot_nt(dkp, wk_ref[...]) + _dot_nt(dvm_ref[...], wv_ref[...]), kv_a, rkv, gkv_ref[...])
        dkr = _dot_nt(dkp, e_ref[...])
        dkr = (dkr + pltpu.roll(dkr, ROPE, 1)) * cs_ref[...]
        stat_ref[0:1, 0:Q_LORA] += jnp.sum(gq_terms, axis=0, keepdims=True)
        stat_ref[1:2, 0:KV_LORA] += jnp.sum(gkv_terms, axis=0, keepdims=True)
        dproj_ref[:, 0:Q_LORA] = dq_a.astype(BF16)
        dproj_ref[:, Q_LORA:Q_LORA + KV_LORA] = dkv_a.astype(BF16)
        dproj_ref[:, Q_LORA + KV_LORA:LOW_W] = dkr.astype(BF16)
        dproj_ref[:, LOW_W:LOW_W + DIL_WIDTH] = dqd_ref[...]
        dproj_ref[:, LOW_W + DIL_WIDTH:LOW_W + 2 * DIL_WIDTH] = dkd_ref[...]
        dproj_ref[:, LOW_W + 2 * DIL_WIDTH:LOW_W + 3 * DIL_WIDTH] = dvd_ref[...]
        dproj_ref[:, LOW_W + 3 * DIL_WIDTH:] = dgt_ref[...]
        dx_ref[...] = ALPHA * dz_ref[...] + _dot_nt(dproj_ref[...], win_ref[...])

    wide = N_PAIRS * PAIR_W
    return pl.pallas_call(
        body, name="bwd_proj", grid=(tokens // tm,),
        out_shape=(jax.ShapeDtypeStruct((tokens, D_MODEL), F32), jax.ShapeDtypeStruct((tokens, IN_EXT), BF16),
                   jax.ShapeDtypeStruct((tokens, wide), BF16), jax.ShapeDtypeStruct((tokens, wide), BF16),
                   jax.ShapeDtypeStruct((8, D_MODEL), F32)),
        in_specs=[_rows(tm, wide), _rows(tm, wide), _rows(tm, DIL_WIDTH), _rows(tm, DIL_WIDTH), _rows(tm, DIL_WIDTH),
                  _rows(tm, DIL_WIDTH), _rows(tm, 2 * D_MODEL),
                  _rows(tm, D_MODEL), _rows(tm, LOW_W), _full(w_in_ext.shape), _full(w1.shape), _full(w2.shape),
                  _full(wk_ext.shape), _full(wv.shape), _full(e128.shape), _full(g_q.shape), _full(g_kv.shape),
                  pl.BlockSpec((tm, wide), lambda i: (i % ns, 0)), pl.BlockSpec((tm, wide), lambda i: (i % ns, 0)),
                  pl.BlockSpec((tm, LANES), lambda i: (i % ns, 0))],
        out_specs=(_rows(tm, D_MODEL), _rows(tm, IN_EXT), _rows(tm, wide), _rows(tm, wide), _full((8, D_MODEL))),
        compiler_params=_cp("arbitrary"),
    )(dqp, dkp, dvm, dq_d, dk_d, dv_d, dgates, dz1, low, w_in_ext, w1, w2, wk_ext, wv, e128, g_q, g_kv, cext, sext, cs128)


def _wgrad(a, b, name, square_relu=False, by_shard=False):
    tokens, ka = a.shape
    n = b.shape[1]
    tka = min(ka, 512)
    tn = n // N_DEV if by_shard else max(w for w in range(LANES, min(n, 2304) + 1, LANES) if n % w == 0)
    tt = min(tokens, 512)
    nt = tokens // tt

    def body(a_ref, b_ref, o_ref, acc):
        kt = pl.program_id(2)

        @pl.when(kt == 0)
        def _():
            acc[...] = jnp.zeros_like(acc)

        at = a_ref[...]
        if square_relu:
            at = jnp.square(jnp.maximum(at.astype(F32), 0.0)).astype(BF16)
        acc[...] += _dot_tn(at, b_ref[...])

        @pl.when(kt == nt - 1)
        def _():
            o_ref[...] = acc[...].astype(BF16)

    if by_shard:
        out_shape, out_spec = (N_DEV, ka, tn), pl.BlockSpec((None, tka, tn), lambda i, j, k: (j, i, 0))
    else:
        out_shape, out_spec = (ka, n), pl.BlockSpec((tka, tn), lambda i, j, k: (i, j))
    return pl.pallas_call(
        body, name=name, grid=(ka // tka, n // tn, nt), out_shape=jax.ShapeDtypeStruct(out_shape, BF16),
        in_specs=[pl.BlockSpec((tt, tka), lambda i, j, k: (k, i)), pl.BlockSpec((tt, tn), lambda i, j, k: (k, j))],
        out_specs=out_spec,
        scratch_shapes=[pltpu.VMEM((tka, tn), F32)],
        compiler_params=_cp("parallel", "parallel", "arbitrary"),
    )(a, b)


def _adam_math(w, g, m, v):
    m = ADAM_B1 * m + (1.0 - ADAM_B1) * g
    v = ADAM_B2 * v + (1.0 - ADAM_B2) * jnp.square(g)
    m_hat = m / (1.0 - ADAM_B1 ** ADAM_STEP)
    v_hat = v / (1.0 - ADAM_B2 ** ADAM_STEP)
    return -ADAM_LR * (m_hat / (jnp.sqrt(v_hat) + ADAM_EPS) + ADAM_WD * w), m, v


def _adamw(w, m, v, own, parts, name):
    rows, cols = w.shape
    tr = _row_tile(rows)
    n_parts = parts.shape[0]

    def body(*refs):
        w_ref, m_ref, v_ref = refs[:3]
        p_ref = refs[-5]
        g_ref, d_ref, nm_ref, nv_ref = refs[-4:]
        g = refs[3][...] if own is not None else p_ref[0].astype(F32)
        for d in range(0 if own is not None else 1, n_parts):
            g = g + p_ref[d].astype(F32)
        g_ref[...] = g
        d_ref[...], nm_ref[...], nv_ref[...] = _adam_math(w_ref[...], g, m_ref[...], v_ref[...])

    blk = pl.BlockSpec((tr, cols), lambda i: (i, 0))
    args = [w, m, v] + ([own] if own is not None else []) + [parts]
    return pl.pallas_call(
        body, name=name, grid=(rows // tr,), out_shape=(jax.ShapeDtypeStruct((rows, cols), F32),) * 4,
        in_specs=[blk] * (len(args) - 1) + [pl.BlockSpec((n_parts, tr, cols), lambda i: (0, i, 0))],
        out_specs=(blk,) * 4, compiler_params=_cp("parallel"),
    )(*args)


def _adamw_small(parts, w, m, v):
    _, rows, cols = parts.shape

    def body(p_ref, w_ref, m_ref, v_ref, g_ref, d_ref, nm_ref, nv_ref):
        g = p_ref[0]
        for d in range(1, N_DEV):
            g = g + p_ref[d]
        g_ref[...] = g
        d_ref[...], nm_ref[...], nv_ref[...] = _adam_math(w_ref[...], g, m_ref[...], v_ref[...])

    return pl.pallas_call(
        body, name="adamw_replicated", out_shape=(jax.ShapeDtypeStruct((rows, cols), F32),) * 4,
        in_specs=[_full(parts.shape)] + [_full((rows, cols))] * 3, out_specs=(_full((rows, cols)),) * 4, grid=(1,),
        compiler_params=_cp("arbitrary"),
    )(parts, w, m, v)


def _pad_rows(a2d, mult):
    pad = (-a2d.shape[-2]) % mult
    return jnp.pad(a2d, [(0, 0)] * (a2d.ndim - 2) + [(0, pad), (0, 0)]) if pad else a2d


def _rot_cols(w):
    half = ROPE // 2
    return jnp.concatenate([-w[..., half:], w[..., :half]], axis=-1)


def _unrot_cols(dw):
    half = ROPE // 2
    return jnp.concatenate([dw[..., half:], -dw[..., :half]], axis=-1)


def _from_col_shards(stacked):
    return stacked.transpose(1, 0, 2).reshape(stacked.shape[1], -1)


def _to_col_shards(full):
    r = full.shape[0]
    return full.reshape(r, N_DEV, -1).transpose(1, 0, 2)


def _rope_tables(seq):
    half = ROPE // 2
    inv = jnp.power(ROPE_THETA, -jnp.arange(half, dtype=F32) / half)
    ang = jnp.arange(seq, dtype=F32)[:, None] * inv[None, :]
    cos = jnp.concatenate([jnp.cos(ang)] * 2, axis=1)
    sin = jnp.concatenate([jnp.sin(ang)] * 2, axis=1)
    ones, zeros = jnp.ones((seq, 2 * NOPE), F32), jnp.zeros((seq, 2 * NOPE), F32)
    pad = jnp.zeros((seq, PAIR_W - 2 * NOPE - 2 * ROPE), F32)
    cext = jnp.tile(jnp.concatenate([ones, cos, cos, pad], axis=1), (1, N_PAIRS))
    sext = jnp.tile(jnp.concatenate([zeros, sin, sin, pad], axis=1), (1, N_PAIRS))
    cs128 = jnp.concatenate([cos, sin, jnp.zeros((seq, LANES - 2 * ROPE), F32)], axis=1)
    return cext, sext, cs128


def _pair_slabs(nope, rope):
    k = nope.shape[0]
    nope = nope.reshape(k, N_PAIRS, 2 * NOPE)
    rope = jnp.zeros((k, N_PAIRS, 2 * ROPE), nope.dtype) if rope is None else rope.reshape(k, N_PAIRS, 2 * ROPE)
    pad = jnp.zeros((k, N_PAIRS, PAIR_W - 2 * NOPE - 2 * ROPE), nope.dtype)
    return jnp.concatenate([nope, rope, pad], axis=2).reshape(k, N_PAIRS * PAIR_W)


def _split_slabs(slabs):
    k = slabs.shape[0]
    s = slabs.reshape(k, N_PAIRS, PAIR_W)
    return s[:, :, :2 * NOPE].reshape(k, N_HEADS, NOPE), s[:, :, 2 * NOPE:2 * NOPE + 2 * ROPE].reshape(k, N_HEADS, ROPE)


def kernel(x, w_in, b_gate, g_q_a, w_uq, g_kv_a, w_ukv, w_o_mla, w_o_dil, w_out, ln1_g, ln1_b, w_ff1, w_ff2, ln2_g, ln2_b, loss_target, m_w_in, m_b_gate, m_g_q_a, m_w_uq, m_g_kv_a, m_w_ukv, m_w_o_mla, m_w_o_dil, m_w_out, m_ln1_g, m_ln1_b, m_w_ff1, m_w_ff2, m_ln2_g, m_ln2_b, v_w_in, v_b_gate, v_g_q_a, v_w_uq, v_g_kv_a, v_w_ukv, v_w_o_mla, v_w_o_dil, v_w_out, v_ln1_g, v_ln1_b, v_w_ff1, v_w_ff2, v_ln2_g, v_ln2_b):
    batch, seq, _ = x.shape
    tokens = batch * seq
    weights = dict(w_in=w_in, w_uq=w_uq, w_ukv=w_ukv, w_o_mla=w_o_mla, w_o_dil=w_o_dil, w_out=w_out, w_ff1=w_ff1, w_ff2=w_ff2, b_gate=b_gate)
    mom_m = dict(w_in=m_w_in, w_uq=m_w_uq, w_ukv=m_w_ukv, w_o_mla=m_w_o_mla, w_o_dil=m_w_o_dil, w_out=m_w_out, w_ff1=m_w_ff1, w_ff2=m_w_ff2, b_gate=m_b_gate)
    mom_v = dict(w_in=v_w_in, w_uq=v_w_uq, w_ukv=v_w_ukv, w_o_mla=v_w_o_mla, w_o_dil=v_w_o_dil, w_out=v_w_out, w_ff1=v_w_ff1, w_ff2=v_w_ff2, b_gate=v_b_gate)

    first = ["w_in", "w_uq", "w_ukv"]
    shards = [weights[n][0].astype(BF16) for n in first]
    g_in, g_uq, g_ukv = _run_comm(_Gather(shards), shards, "all_gather_first_weights")

    wi = _from_col_shards(g_in)
    s0, s1, s2, s3 = Q_LORA, Q_LORA + KV_LORA, Q_LORA + KV_LORA + ROPE, Q_LORA + KV_LORA + ROPE + 3 * DIL_WIDTH
    w_kr = wi[:, s1:s2]
    w_in_ext = jnp.concatenate([wi[:, :s2], _rot_cols(w_kr), jnp.zeros((D_MODEL, LOW_W - s2 - ROPE), BF16), wi[:, s2:]], axis=1)
    uq = _from_col_shards(g_uq).reshape(Q_LORA, N_HEADS, NOPE + ROPE)
    w1 = _pair_slabs(uq[:, :, :NOPE], uq[:, :, NOPE:])
    w2 = _pair_slabs(jnp.zeros_like(uq[:, :, :NOPE]), _rot_cols(uq[:, :, NOPE:]))
    ukv = _from_col_shards(g_ukv).reshape(KV_LORA, N_HEADS, NOPE + HEAD_V)
    wk_ext = _pair_slabs(ukv[:, :, :NOPE], None)
    wv = ukv[:, :, NOPE:].reshape(KV_LORA, N_HEADS * HEAD_V)
    eye = jnp.eye(ROPE, dtype=BF16)
    e_slab = jnp.concatenate([jnp.zeros((ROPE, 2 * NOPE), BF16), eye, eye, jnp.zeros((ROPE, PAIR_W - 2 * NOPE - 2 * ROPE), BF16)], axis=1)
    e128 = jnp.concatenate([jnp.tile(e_slab, (1, N_PAIRS)), jnp.zeros((LANES - ROPE, N_PAIRS * PAIR_W), BF16)], axis=0)
    cext, sext, cs128 = _rope_tables(seq)
    dil_bias = _dilated_bias_table(seq)
    no_bias = jnp.zeros((1, 8, LANES), F32)

    x2 = x.reshape(tokens, D_MODEL)
    xb = x2.astype(BF16)
    low, gates, qkvd, qp, kp, vm, qn, kvn = _fwd_proj(xb, w_in_ext, w1, w2, wk_ext, wv, e128, g_q_a, g_kv_a, cext, sext, cs128, seq=seq)
    bg = b_gate[0]
    bg_hi = bg.astype(BF16)
    bg_lo = (bg - bg_hi.astype(F32)).astype(BF16)
    later = [weights[n][0].astype(BF16) for n in ("w_o_mla", "w_o_dil", "w_out", "w_ff1", "w_ff2")]
    later.append(_pad_rows(jnp.concatenate([bg_hi, bg_lo], axis=0), 16))
    mla = dict(batch=batch, seq=seq, width=PAIR_W, col0=(0, 0, 0), dilated=False, scale=MLA_SCALE)
    dil = dict(batch=batch, seq=seq, width=LANES, col0=(0, N_PAIRS, 2 * N_PAIRS), dilated=True, scale=DIL_SCALE)
    o_a, lse_a, g_oa, g_ob, g_out, g_ff1, g_ff2, g_bg = _attn_fwd(
        qp, kp, vm, no_bias, name="mla_attention_fwd", comm=_Gather(later), comm_arrays=later, **mla)
    o_b, lse_b = _attn_fwd(qkvd, qkvd, qkvd, dil_bias, name="dilated_attention_fwd", **dil)
    w_oa, w_ob = _from_col_shards(g_oa), _from_col_shards(g_ob)
    w_out_full = g_out.reshape(D_MODEL, D_MODEL)
    w_ff2_full = g_ff2.reshape(D_FF, D_MODEL)
    bg_parts = g_bg.astype(F32)
    b_gate_full = _from_col_shards(bg_parts[:, 0:2] + bg_parts[:, 2:4])
    h, hb, xhat1, rstd1, y_a, y_b, mix = _fwd_mix(o_a, o_b, gates, x2, b_gate_full, w_oa, w_ob, w_out_full, ln1_g, ln1_b, seq=seq)
    u, dz2, dz2b, stat2 = _fwd_mlp(hb, h, loss_target.reshape(tokens, D_MODEL), g_ff1, w_ff2_full, ln2_g, ln2_b, seq=seq)

    du, dz1, dz1b, stat1 = _bwd_mlp(dz2, dz2b, u, xhat1, rstd1, g_ff1, w_ff2_full, ln1_g, seq=seq)
    dw_ff = [_wgrad(hb, du, "wgrad_ff1", by_shard=True),
             _wgrad(u, dz2b, "wgrad_ff2", square_relu=True).reshape(N_DEV, FF_SHARD, D_MODEL)]
    dgates, dy_a, dy_b, do_a, do_b, stat_g = _bwd_mix(dz1b, gates, y_a, y_b, b_gate_full, w_oa, w_ob, w_out_full, seq=seq)
    dqp, dkp, dvm, r_ff1, r_ff2 = _attn_bwd(qp, kp, vm, o_a, do_a, lse_a, no_bias, name="mla_attention_bwd",
                                            comm=_Scatter(dw_ff), comm_arrays=dw_ff, **mla)
    dw_mid = [_to_col_shards(_wgrad(o_a, dy_a, "wgrad_o_mla")), _to_col_shards(_wgrad(o_b, dy_b, "wgrad_o_dil")),
              _wgrad(mix, dz1b, "wgrad_out").reshape(N_DEV, D_MODEL // N_DEV, D_MODEL),
              _pad_rows(_to_col_shards(stat_g[0:2]).astype(BF16), 16)]
    dq_d, dk_d, dv_d, r_oa, r_ob, r_out, r_bg = _attn_bwd(qkvd, qkvd, qkvd, o_b, do_b, lse_b, dil_bias, name="dilated_attention_bwd",
                                                          comm=_Scatter(dw_mid), comm_arrays=dw_mid, **dil)
    grad_x, dproj, d_a, d_b, stat_r = _bwd_proj(dqp, dkp, dvm, dq_d, dk_d, dv_d, dgates, dz1, low, w_in_ext, w1, w2, wk_ext, wv,
                                                e128, g_q_a, g_kv_a, cext, sext, cs128, seq=seq)

    dw_in_ext = _wgrad(xb, dproj, "wgrad_in")
    dw1 = _wgrad(qn, d_a, "wgrad_uq_direct")
    dw2 = _wgrad(qn, d_b, "wgrad_uq_rotated")
    dwk = _wgrad(kvn, dkp, "wgrad_ukv_k")
    dwv = _wgrad(kvn, dvm, "wgrad_ukv_v")
    dlow = dw_in_ext[:, :LOW_W]
    dw_kr = dlow[:, s1:s2] + _unrot_cols(dlow[:, s2:s2 + ROPE])
    dw_in = jnp.concatenate([dlow[:, :s1], dw_kr, dw_in_ext[:, LOW_W:]], axis=1)
    n1, r1 = _split_slabs(dw1)
    _, r2 = _split_slabs(dw2)
    dw_uq = jnp.concatenate([n1, r1 + _unrot_cols(r2)], axis=2).reshape(Q_LORA, N_HEADS * (NOPE + ROPE))
    nk, _ = _split_slabs(dwk)
    dw_ukv = jnp.concatenate([nk, dwv.reshape(KV_LORA, N_HEADS, HEAD_V)], axis=2).reshape(KV_LORA, N_HEADS * (NOPE + HEAD_V))
    last = [_to_col_shards(dw_in), _to_col_shards(dw_uq), _to_col_shards(dw_ukv)]
    mine, theirs = _rs_sibling(last, "rs_last_sibling_exchange")
    sums = [_pair_sum(a, b, "rs_last_pair_sum_" + n) for a, b, n in zip(mine, theirs, first)]
    got = _rs_chips([s[1] for s in sums], "rs_last_chip_exchange")

    upd = {}
    for n, (own, _), parts in zip(first, sums, got):
        upd[n] = _adamw(weights[n][0], mom_m[n][0], mom_v[n][0], own, parts, "adamw_" + n)
    for n, parts in (("w_o_mla", r_oa), ("w_o_dil", r_ob), ("w_out", r_out), ("w_ff1", r_ff1), ("w_ff2", r_ff2)):
        upd[n] = _adamw(weights[n][0], mom_m[n][0], mom_v[n][0], None, parts, "adamw_" + n)
    bg_upd = _adamw(_pad_rows(b_gate[0], 16), _pad_rows(m_b_gate[0], 16), _pad_rows(v_b_gate[0], 16), None, r_bg, "adamw_b_gate")
    upd["b_gate"] = tuple(t[0:2] for t in bg_upd)

    small_w = [g_q_a, g_kv_a, ln1_g, ln1_b, ln2_g, ln2_b]
    small_m = [m_g_q_a, m_g_kv_a, m_ln1_g, m_ln1_b, m_ln2_g, m_ln2_b]
    small_v = [v_g_q_a, v_g_kv_a, v_ln1_g, v_ln1_b, v_ln2_g, v_ln2_b]
    widths = [a.shape[1] for a in small_w]
    partial = jnp.concatenate([stat_r[0:1, :Q_LORA], stat_r[1:2, :KV_LORA], stat1[0:1], stat1[1:2], stat2[0:1], stat2[1:2],
                               stat2[2:3, :LANES]], axis=1)

    def as_rows(vecs, extra):
        flat = jnp.concatenate(vecs + [jnp.zeros((1, extra), F32)], axis=1)
        return _pad_rows(flat.reshape(-1, LANES), 8)

    partial = _pad_rows(partial.reshape(-1, LANES), 8)
    (every,) = _run_comm(_Gather([partial]), [partial], "all_gather_replicated_grads")
    g_s, d_s, nm_s, nv_s = _adamw_small(every, as_rows(small_w, LANES), as_rows(small_m, LANES), as_rows(small_v, LANES))

    def split_small(a):
        flat = a.reshape(1, -1)
        out, c0 = [], 0
        for w in widths:
            out.append(flat[:, c0:c0 + w])
            c0 += w
        return out, flat[0, c0]

    g_small, loss = split_small(g_s)
    small = [g_small, split_small(d_s)[0], split_small(nm_s)[0], split_small(nv_s)[0]]

    order = ["w_in", "b_gate", "g_q_a", "w_uq", "g_kv_a", "w_ukv", "w_o_mla", "w_o_dil", "w_out", "ln1_g", "ln1_b", "w_ff1", "w_ff2", "ln2_g", "ln2_b"]
    small_names = ["g_q_a", "g_kv_a", "ln1_g", "ln1_b", "ln2_g", "ln2_b"]

    def pick(kind):
        return [small[kind][small_names.index(n)] if n in small_names else upd[n][kind][None] for n in order]

    return (loss, grad_x.reshape(batch, seq, D_MODEL), *pick(0), *pick(1), *pick(2), *pick(3))
```

```python
import functools
import math

import jax
import jax.numpy as jnp
from jax import lax
from jax.experimental import pallas as pl
from jax.experimental.pallas import tpu as pltpu

F32 = jnp.float32
BF16 = jnp.bfloat16
I32 = jnp.int32

D_MODEL = 1024
N_HEADS = 8
NOPE = 64
ROPE = 32
HEAD_V = 64
Q_LORA = 384
KV_LORA = 256
DIL_WIDTH = 512
D_FF = 4096
ROPE_THETA = 10000.0
LN_EPS = 1e-5
RMS_EPS = 1e-6
NEG = -1e30
ALPHA = 2.0 ** 0.25
MLA_SCALE = (NOPE + ROPE) ** -0.5
DIL_SCALE = 64 ** -0.5
ADAM_LR, ADAM_B1, ADAM_B2, ADAM_EPS, ADAM_WD, ADAM_STEP = 0.001, 0.9, 0.999, 1e-08, 0.01, 10

LANES = 128
PAIR_W = 256
N_PAIRS = N_HEADS // 2
LOW_W = 768
IN_EXT = LOW_W + 3 * DIL_WIDTH + 2 * D_MODEL
N_DEV = 8
FF_SHARD = D_FF // N_DEV
FF_STEP = 2
WGRAD_SHARDS = 4
TOKEN_TILE = 256
ATTN_TILE = 256
VMEM_LIMIT = 56 << 20

MESH = pl.DeviceIdType.MESH
ANY = pl.BlockSpec(memory_space=pl.ANY)
CHIP_FLIPS = ((0, 0), (0, 1), (1, 0), (1, 1))
PEER_FLIPS = tuple((fx, fy, fc) for fx in (0, 1) for fy in (0, 1) for fc in (0, 1))[1:]


def _cp(*sem):
    return pltpu.CompilerParams(dimension_semantics=sem or None, vmem_limit_bytes=VMEM_LIMIT)


def _full(shape):
    nd = len(shape)
    return pl.BlockSpec(shape, lambda *_: (0,) * nd)


def _rows(tm, width):
    return pl.BlockSpec((tm, width), lambda i, *_: (i, 0))


def _dot(a, b):
    return jnp.dot(a, b, preferred_element_type=F32)


def _dot_nt(a, b):
    return lax.dot_general(a, b, (((1,), (1,)), ((), ())), preferred_element_type=F32)


def _dot_tn(a, b):
    return lax.dot_general(a, b, (((0,), (0,)), ((), ())), preferred_element_type=F32)


def _sigmoid(z):
    return 1.0 / (1.0 + jnp.exp(-z))


def _place():
    return lax.axis_index("x"), lax.axis_index("y"), lax.axis_index("c")


def _flip(v, f):
    return 1 - v if f else v


class _Gather:
    def __init__(self, shards):
        self.n = len(shards)
        self.out_shape = [jax.ShapeDtypeStruct((N_DEV, *s.shape), s.dtype) for s in shards]
        self.scratch = [pltpu.SemaphoreType.DMA((7 * self.n,)), pltpu.SemaphoreType.DMA((7 * self.n,)),
                        pltpu.SemaphoreType.DMA((self.n,))]

    def _copies(self, what, srcs, dsts, send, recv, local):
        x, y, c = _place()
        chips = [(_flip(x, fx), _flip(y, fy)) for fx, fy in CHIP_FLIPS[1:]]
        out = []
        for a in range(self.n):
            def slot(px, py, pc, a=a):
                return dsts[a].at[4 * px + 2 * py + pc]

            def copy(k, block, to, src=None, a=a, slot=slot):
                return pltpu.make_async_remote_copy(
                    src_ref=slot(*block) if src is None else src, dst_ref=slot(*block),
                    send_sem=send.at[7 * a + k], recv_sem=recv.at[7 * a + k], device_id=to, device_id_type=MESH)

            if what == "mine":
                out.append(pltpu.make_async_copy(srcs[a], slot(x, y, c), local.at[a]))
            elif what == "first":
                out.append(copy(0, (x, y, c), (x, y, 1 - c), src=srcs[a]))
                out += [copy(1 + j, (x, y, c), (*chip, c), src=srcs[a]) for j, chip in enumerate(chips)]
            elif what == "landed":
                out += [copy(1 + j, (*chip, c), (x, y, c)) for j, chip in enumerate(chips)]
            elif what == "passed":
                out += [copy(4 + j, (*chip, c), (x, y, 1 - c)) for j, chip in enumerate(chips)]
            else:
                out.append(copy(0, (x, y, 1 - c), (x, y, c)))
                out += [copy(4 + j, (*chip, 1 - c), (x, y, c)) for j, chip in enumerate(chips)]
        return out

    def start(self, *refs):
        for cp in self._copies("mine", *refs) + self._copies("first", *refs):
            cp.start()

    def forward(self, *refs):
        for landed, passed in zip(self._copies("landed", *refs), self._copies("passed", *refs)):
            landed.wait_recv()
            passed.start()

    def finish(self, *refs):
        for cp in self._copies("from_sibling", *refs):
            cp.wait_recv()
        for cp in self._copies("first", *refs) + self._copies("passed", *refs):
            cp.wait_send()
        for cp in self._copies("mine", *refs):
            cp.wait()


class _Scatter:
    def __init__(self, arrays):
        self.n = len(arrays)
        self.out_shape = [jax.ShapeDtypeStruct(a.shape, a.dtype) for a in arrays]
        self.scratch = [pltpu.SemaphoreType.DMA((7 * self.n,)), pltpu.SemaphoreType.DMA((7 * self.n,)),
                        pltpu.SemaphoreType.DMA((self.n,))]

    def _copies(self, what, srcs, dsts, send, recv, local):
        x, y, c = _place()
        me = 4 * x + 2 * y + c
        out = []
        for a in range(self.n):
            if what == "mine":
                out.append(pltpu.make_async_copy(srcs[a].at[me], dsts[a].at[me], local.at[a]))
                continue
            for k, (fx, fy, fc) in enumerate(PEER_FLIPS):
                px, py, pc = _flip(x, fx), _flip(y, fy), _flip(c, fc)
                peer = 4 * px + 2 * py + pc
                out.append(pltpu.make_async_remote_copy(
                    src_ref=srcs[a].at[peer], dst_ref=dsts[a].at[me if what == "out" else peer],
                    send_sem=send.at[7 * a + k], recv_sem=recv.at[7 * a + k], device_id=(px, py, pc), device_id_type=MESH))
        return out

    def start(self, *refs):
        for cp in self._copies("mine", *refs) + self._copies("out", *refs):
            cp.start()

    def forward(self, *refs):
        pass

    def finish(self, *refs):
        for cp in self._copies("out", *refs):
            cp.wait_send()
        for cp in self._copies("in", *refs):
            cp.wait_recv()
        for cp in self._copies("mine", *refs):
            cp.wait()


def _run_comm(comm, arrays, name):
    n = comm.n

    def body(*refs):
        args = (refs[:n], refs[n:2 * n], *refs[2 * n:])
        comm.start(*args)
        comm.forward(*args)
        comm.finish(*args)

    return pl.pallas_call(body, name=name, out_shape=comm.out_shape, in_specs=[ANY] * n, out_specs=[ANY] * n,
                          scratch_shapes=comm.scratch)(*arrays)


def _rs_sibling(arrays, name):
    n = len(arrays)
    quarter = [jax.ShapeDtypeStruct((4, *a.shape[1:]), a.dtype) for a in arrays]

    def body(*refs):
        srcs, mine, got, (send, recv, local) = refs[:n], refs[n:2 * n], refs[2 * n:3 * n], refs[3 * n:]
        x, y, c = _place()
        local_copies, remote = [], []
        for a in range(n):
            for r, (fx, fy) in enumerate(CHIP_FLIPS):
                chip = 2 * _flip(x, fx) + _flip(y, fy)
                local_copies.append(pltpu.make_async_copy(srcs[a].at[2 * chip + c], mine[a].at[r], local.at[4 * a + r]))
                remote.append(pltpu.make_async_remote_copy(
                    src_ref=srcs[a].at[2 * chip + 1 - c], dst_ref=got[a].at[r], send_sem=send.at[4 * a + r],
                    recv_sem=recv.at[4 * a + r], device_id=(x, y, 1 - c), device_id_type=MESH))
        for cp in local_copies + remote:
            cp.start()
        for cp in remote:
            cp.wait_send()
        for cp in remote:
            cp.wait_recv()
        for cp in local_copies:
            cp.wait()

    out = pl.pallas_call(
        body, name=name, out_shape=quarter + quarter, in_specs=[ANY] * n, out_specs=[ANY] * (2 * n),
        scratch_shapes=[pltpu.SemaphoreType.DMA((4 * n,)), pltpu.SemaphoreType.DMA((4 * n,)), pltpu.SemaphoreType.DMA((4 * n,))],
    )(*arrays)
    return out[:n], out[n:]


def _rs_chips(arrays, name):
    n = len(arrays)

    def body(*refs):
        srcs, dsts, (send, recv) = refs[:n], refs[n:2 * n], refs[2 * n:]
        x, y, c = _place()
        copies = []
        for a in range(n):
            for k, (fx, fy) in enumerate(CHIP_FLIPS[1:]):
                copies.append(pltpu.make_async_remote_copy(
                    src_ref=srcs[a].at[k], dst_ref=dsts[a].at[k], send_sem=send.at[3 * a + k], recv_sem=recv.at[3 * a + k],
                    device_id=(_flip(x, fx), _flip(y, fy), c), device_id_type=MESH))
        for cp in copies:
            cp.start()
        for cp in copies:
            cp.wait_send()
        for cp in copies:
            cp.wait_recv()

    return pl.pallas_call(
        body, name=name, out_shape=[jax.ShapeDtypeStruct(a.shape, a.dtype) for a in arrays], in_specs=[ANY] * n,
        out_specs=[ANY] * n, scratch_shapes=[pltpu.SemaphoreType.DMA((3 * n,)), pltpu.SemaphoreType.DMA((3 * n,))],
    )(*arrays)


def _row_tile(rows):
    return 256 if rows % 256 == 0 else rows


def _pair_sum(a, b, name):
    _, rows, cols = a.shape
    tr = _row_tile(rows)

    def body(a_ref, b_ref, own_ref, rest_ref):
        s = a_ref[...].astype(F32) + b_ref[...].astype(F32)
        own_ref[...] = s[0]
        rest_ref[...] = s[1:].astype(BF16)

    return pl.pallas_call(
        body, name=name, grid=(rows // tr,),
        out_shape=(jax.ShapeDtypeStruct((rows, cols), F32), jax.ShapeDtypeStruct((3, rows, cols), BF16)),
        in_specs=[pl.BlockSpec((4, tr, cols), lambda i: (0, i, 0))] * 2,
        out_specs=(pl.BlockSpec((tr, cols), lambda i: (i, 0)), pl.BlockSpec((3, tr, cols), lambda i: (0, i, 0))),
        compiler_params=_cp("parallel"),
    )(a, b)


def _head_lanes(width, h):
    lane = lax.broadcasted_iota(I32, (1, width), 1)
    if width == LANES:
        return (lane >= 64 * h) & (lane < 64 * h + 64)
    nope = (lane >= NOPE * h) & (lane < NOPE * h + NOPE)
    rope = (lane >= 2 * NOPE + ROPE * h) & (lane < 2 * NOPE + ROPE * h + ROPE)
    return nope | rope


def _dilated_bias_table(seq):
    t = min(ATTN_TILE, seq)
    nd = seq // t

    def body(o_ref):
        delta = pl.program_id(0) * t + lax.broadcasted_iota(I32, (t, t), 1) - lax.broadcasted_iota(I32, (t, t), 0)
        mult = ((delta <= 128).astype(I32) + (((delta & 3) == 0) & (delta <= 512)).astype(I32)
                + ((delta & 15) == 0).astype(I32))
        logm = jnp.where(mult == 3, math.log(3.0), jnp.where(mult == 2, math.log(2.0), 0.0))
        valid = (delta >= 0) & (mult > 0)
        dist = delta.astype(F32)
        for h in range(N_HEADS):
            o_ref[h] = jnp.where(valid, logm - 2.0 ** (-(h + 1)) * dist, NEG)

    return pl.pallas_call(
        body, name="dilated_bias_table", grid=(nd,), out_shape=jax.ShapeDtypeStruct((N_HEADS, nd, t, t), F32),
        out_specs=pl.BlockSpec((N_HEADS, None, t, t), lambda d: (0, d, 0, 0)),
        compiler_params=_cp("parallel"),
    )()


def _comm_hooks(comm, refs, n_in, n_out):
    if comm is None:
        return refs[:n_in], refs[n_in:n_in + n_out], refs[n_in + n_out:], None
    n = comm.n
    ins, srcs = refs[:n_in], refs[n_in:n_in + n]
    outs, dsts = refs[n_in + n:n_in + n + n_out], refs[n_in + n + n_out:n_in + 2 * n + n_out]
    rest = refs[n_in + 2 * n + n_out:]
    return ins, outs, rest[:len(rest) - 3], (srcs, dsts, *rest[len(rest) - 3:])


def _attn_fwd(q, k, v, bias, *, batch, seq, width, col0, dilated, scale, name, comm=None, comm_arrays=()):
    t = min(ATTN_TILE, seq)
    nq = seq // t
    cq, ck, cv = col0
    pre = scale if dilated else 1.0
    steps = batch * N_PAIRS * nq

    def body(*refs):
        (q_ref, k_ref, v_ref, bias_ref), (o_ref, lse_ref), _, plan = _comm_hooks(comm, refs, 4, 2)
        i = pl.program_id(2)
        step_no = (pl.program_id(0) * N_PAIRS + pl.program_id(1)) * nq + i
        if plan:
            pl.when(step_no == 0)(lambda: comm.start(*plan))
            pl.when(step_no == (3 * steps) // 4)(lambda: comm.forward(*plan))
        q2 = q_ref[...] * pre if dilated else q_ref[...]
        qh = [jnp.where(_head_lanes(width, h), q2, jnp.zeros_like(q2)) for h in (0, 1)]
        vlane = [_head_lanes(LANES, h) for h in (0, 1)]
        top = lax.broadcasted_iota(I32, (LANES, t), 0) < HEAD_V
        causal = lax.broadcasted_iota(I32, (t, t), 0) <= lax.broadcasted_iota(I32, (t, t), 1)

        def step(j, carry, diagonal):
            m0, l0, m1, l1, acc = carry
            ks = pl.multiple_of(j * t, t)
            kj = k_ref[pl.ds(ks, t), :]
            vj = v_ref[pl.ds(ks, t), :]
            new, alphas, pv = [], [], []
            for h, (m, l) in enumerate(((m0, l0), (m1, l1))):
                s = _dot_nt(kj, qh[h])
                if dilated:
                    s = s + bias_ref[h, i - j]
                else:
                    s = s * scale
                    if diagonal:
                        s = jnp.where(causal, s, NEG)
                m_new = jnp.maximum(m, jnp.max(s, axis=0, keepdims=True))
                a = jnp.exp(m - m_new)
                p = jnp.exp(s - m_new)
                new += [m_new, a * l + jnp.sum(p, axis=0, keepdims=True)]
                alphas.append(a)
                pv.append(_dot_tn(jnp.where(vlane[h], vj, jnp.zeros_like(vj)), p.astype(BF16)))
            acc = jnp.where(top, alphas[0], alphas[1]) * acc + pv[0] + pv[1]
            return (*new, acc)

        row = jnp.full((1, t), NEG, F32)
        zero = jnp.zeros((1, t), F32)
        init = (row, zero, row, zero, jnp.zeros((LANES, t), F32))
        if dilated:
            carry = lax.fori_loop(0, i + 1, functools.partial(step, diagonal=False), init)
        else:
            carry = step(i, lax.fori_loop(0, i, functools.partial(step, diagonal=False), init), True)
        m0, l0, m1, l1, acc = carry
        o_ref[...] = jnp.transpose(acc * jnp.where(top, 1.0 / l0, 1.0 / l1)).astype(BF16)
        r = lax.broadcasted_iota(I32, (8, t), 0)
        lse_ref[...] = jnp.where(r == 0, m0 + jnp.log(l0), jnp.where(r == 1, m1 + jnp.log(l1), 0.0))
        if plan:
            pl.when(step_no == steps - 1)(lambda: comm.finish(*plan))

    bias_spec = (pl.BlockSpec((2, nq, t, t), lambda b, p, i: (p, 0, 0, 0)) if dilated
                 else pl.BlockSpec((None, 8, LANES), lambda b, p, i: (0, 0, 0)))
    n = comm.n if comm else 0
    return pl.pallas_call(
        body, name=name, grid=(batch, N_PAIRS, nq),
        out_shape=[jax.ShapeDtypeStruct((batch * seq, DIL_WIDTH), BF16), jax.ShapeDtypeStruct((batch * N_PAIRS, 8, seq), F32)]
        + (comm.out_shape if comm else []),
        in_specs=[pl.BlockSpec((t, width), lambda b, p, i: (b * nq + i, cq + p)),
                  pl.BlockSpec((seq, width), lambda b, p, i: (b, ck + p)),
                  pl.BlockSpec((seq, LANES), lambda b, p, i: (b, cv + p)),
                  bias_spec] + [ANY] * n,
        out_specs=[pl.BlockSpec((t, LANES), lambda b, p, i: (b * nq + i, p)),
                   pl.BlockSpec((None, 8, t), lambda b, p, i: (b * N_PAIRS + p, 0, i))] + [ANY] * n,
        scratch_shapes=comm.scratch if comm else [],
        compiler_params=_cp("arbitrary", "arbitrary", "arbitrary") if comm else _cp("parallel", "parallel", "arbitrary"),
    )(q, k, v, bias, *comm_arrays)


def _attn_bwd(q, k, v, o, do, lse, bias, *, batch, seq, width, col0, dilated, scale, name, comm=None, comm_arrays=()):
    t = min(ATTN_TILE, seq)
    nq = seq // t
    cq, ck, cv = col0
    pre = scale if dilated else 1.0
    steps = batch * N_PAIRS

    def body(*refs):
        ins, (dq_ref, dk_ref, dv_ref), (dq_acc, dk_acc, dv_acc, rowdot), plan = _comm_hooks(comm, refs, 7, 3)
        q_ref, k_ref, v_ref, o_ref, do_ref, lse_ref, bias_ref = ins
        step_no = pl.program_id(0) * N_PAIRS + pl.program_id(1)
        if plan:
            pl.when(step_no == 0)(lambda: comm.start(*plan))
        wlane = [_head_lanes(width, h) for h in (0, 1)]
        vlane = [_head_lanes(LANES, h) for h in (0, 1)]
        causal = lax.broadcasted_iota(I32, (t, t), 0) <= lax.broadcasted_iota(I32, (t, t), 1)
        prod = jnp.transpose(do_ref[...].astype(F32) * o_ref[...].astype(F32))
        rowdot[0:1, :] = jnp.sum(prod[0:HEAD_V], axis=0, keepdims=True)
        rowdot[1:2, :] = jnp.sum(prod[HEAD_V:], axis=0, keepdims=True)
        dq_acc[...] = jnp.zeros_like(dq_acc)

        def k_tile(j, _):
            ks = pl.multiple_of(j * t, t)
            kj = k_ref[pl.ds(ks, t), :]
            vj = v_ref[pl.ds(ks, t), :]
            kh = [jnp.where(wlane[h], kj, jnp.zeros_like(kj)) for h in (0, 1)]
            dk_acc[...] = jnp.zeros_like(dk_acc)
            dv_acc[...] = jnp.zeros_like(dv_acc)

            def q_tile(i, _, diagonal):
                qs = pl.multiple_of(i * t, t)
                qi = q_ref[pl.ds(qs, t), :] * pre if dilated else q_ref[pl.ds(qs, t), :]
                doi = do_ref[pl.ds(qs, t), :]
                dq_i = jnp.zeros((t, width), F32)
                for h in (0, 1):
                    qih = jnp.where(wlane[h], qi, jnp.zeros_like(qi))
                    doih = jnp.where(vlane[h], doi, jnp.zeros_like(doi))
                    s = _dot_nt(kj, qih)
                    if dilated:
                        s = s + bias_ref[h, i - j]
                    else:
                        s = s * scale
                        if diagonal:
                            s = jnp.where(causal, s, NEG)
                    p = jnp.exp(s - lse_ref[h:h + 1, pl.ds(qs, t)])
                    dp = _dot_nt(vj, doih)
                    ds = p * (dp - rowdot[h:h + 1, pl.ds(qs, t)])
                    ds = (ds if dilated else ds * scale).astype(BF16)
                    dv_acc[...] += _dot(p.astype(BF16), doih)
                    dk_acc[...] += _dot(ds, qih)
                    dq_i = dq_i + _dot_tn(ds, kh[h])
                dq_acc[pl.ds(qs, t), :] += dq_i
                return 0

            if dilated:
                lax.fori_loop(j, nq, functools.partial(q_tile, diagonal=False), 0)
            else:
                q_tile(j, 0, True)
                lax.fori_loop(j + 1, nq, functools.partial(q_tile, diagonal=False), 0)
            dk_ref[pl.ds(ks, t), :] = dk_acc[...].astype(BF16)
            dv_ref[pl.ds(ks, t), :] = dv_acc[...].astype(BF16)
            return 0

        lax.fori_loop(0, nq, k_tile, 0)
        dq_ref[...] = (dq_acc[...] * pre).astype(BF16)
        if plan:
            pl.when(step_no == steps - 1)(lambda: comm.finish(*plan))

    tokens = batch * seq
    bias_spec = (pl.BlockSpec((2, nq, t, t), lambda b, p: (p, 0, 0, 0)) if dilated
                 else pl.BlockSpec((None, 8, LANES), lambda b, p: (0, 0, 0)))
    n = comm.n if comm else 0
    return pl.pallas_call(
        body, name=name, grid=(batch, N_PAIRS),
        out_shape=[jax.ShapeDtypeStruct((tokens, N_PAIRS * width), BF16), jax.ShapeDtypeStruct((tokens, N_PAIRS * width), BF16),
                   jax.ShapeDtypeStruct((tokens, DIL_WIDTH), BF16)] + (comm.out_shape if comm else []),
        in_specs=[pl.BlockSpec((seq, width), lambda b, p: (b, cq + p)),
                  pl.BlockSpec((seq, width), lambda b, p: (b, ck + p)),
                  pl.BlockSpec((seq, LANES), lambda b, p: (b, cv + p)),
                  pl.BlockSpec((seq, LANES), lambda b, p: (b, p)),
                  pl.BlockSpec((seq, LANES), lambda b, p: (b, p)),
                  pl.BlockSpec((None, 8, seq), lambda b, p: (b * N_PAIRS + p, 0, 0)),
                  bias_spec] + [ANY] * n,
        out_specs=[pl.BlockSpec((seq, width), lambda b, p: (b, p)),
                   pl.BlockSpec((seq, width), lambda b, p: (b, p)),
                   pl.BlockSpec((seq, LANES), lambda b, p: (b, p))] + [ANY] * n,
        scratch_shapes=[pltpu.VMEM((seq, width), F32), pltpu.VMEM((t, width), F32), pltpu.VMEM((t, LANES), F32),
                        pltpu.VMEM((8, seq), F32)] + (comm.scratch if comm else []),
        compiler_params=_cp("arbitrary", "arbitrary") if comm else _cp("parallel", "parallel"),
    )(q, k, v, o, do, lse, bias, *comm_arrays)


def _rms(xf, g):
    r = lax.rsqrt(jnp.mean(xf * xf, axis=1, keepdims=True) + RMS_EPS)
    return xf * r * g, r


def _rms_bwd(dy, xf, r, g):
    gy = dy * g
    dx = r * gy - xf * (r * r * r) * jnp.mean(gy * xf, axis=1, keepdims=True)
    return dx, dy * xf * r


def _ln_bwd(dy, xhat, rstd, g):
    dxh = dy * g
    return rstd * (dxh - jnp.mean(dxh, axis=1, keepdims=True) - xhat * jnp.mean(dxh * xhat, axis=1, keepdims=True))


def _fwd_proj(xb, w_in_ext, w1, w2, wk_ext, wv, e128, g_q, g_kv, cext, sext, cs128, *, seq):
    tokens = xb.shape[0]
    tm = min(TOKEN_TILE, seq)
    ns = seq // tm

    def body(x_ref, win_ref, w1_ref, w2_ref, wk_ref, wv_ref, e_ref, gq_ref, gkv_ref, c_ref, s_ref, cs_ref,
             low_ref, gates_ref, qkvd_ref, qp_ref, kp_ref, vm_ref, qn_ref, kvn_ref):
        xt = x_ref[...]
        low = _dot(xt, win_ref[:, 0:LOW_W])
        low_ref[...] = low
        qkvd_ref[...] = _dot(xt, win_ref[:, LOW_W:LOW_W + 3 * DIL_WIDTH]).astype(BF16)
        gates_ref[...] = _dot(xt, win_ref[:, LOW_W + 3 * DIL_WIDTH:])
        qn = _rms(low[:, 0:Q_LORA], gq_ref[...])[0].astype(BF16)
        kvn = _rms(low[:, Q_LORA:Q_LORA + KV_LORA], gkv_ref[...])[0].astype(BF16)
        qn_ref[...] = qn
        kvn_ref[...] = kvn
        qp_ref[...] = (_dot(qn, w1_ref[...]) * c_ref[...] + _dot(qn, w2_ref[...]) * s_ref[...]).astype(BF16)
        kr = low[:, Q_LORA + KV_LORA:] * cs_ref[...]
        kr = kr + pltpu.roll(kr, LANES - ROPE, 1)
        lane = lax.broadcasted_iota(I32, kr.shape, 1)
        kr = jnp.where(lane < ROPE, kr, 0.0).astype(BF16)
        kp_ref[...] = (_dot(kvn, wk_ref[...]) + _dot(kr, e_ref[...])).astype(BF16)
        vm_ref[...] = _dot(kvn, wv_ref[...]).astype(BF16)

    n_gates = 2 * D_MODEL
    outs = [(LOW_W, F32), (n_gates, F32), (3 * DIL_WIDTH, BF16), (N_PAIRS * PAIR_W, BF16), (N_PAIRS * PAIR_W, BF16),
            (DIL_WIDTH, BF16), (Q_LORA, BF16), (KV_LORA, BF16)]
    return pl.pallas_call(
        body, name="fwd_proj", grid=(tokens // tm,),
        out_shape=tuple(jax.ShapeDtypeStruct((tokens, w), dt) for w, dt in outs),
        in_specs=[_rows(tm, D_MODEL), _full(w_in_ext.shape), _full(w1.shape), _full(w2.shape), _full(wk_ext.shape),
                  _full(wv.shape), _full(e128.shape), _full(g_q.shape), _full(g_kv.shape),
                  pl.BlockSpec((tm, N_PAIRS * PAIR_W), lambda i: (i % ns, 0)),
                  pl.BlockSpec((tm, N_PAIRS * PAIR_W), lambda i: (i % ns, 0)),
                  pl.BlockSpec((tm, LANES), lambda i: (i % ns, 0))],
        out_specs=tuple(_rows(tm, w) for w, _ in outs),
        compiler_params=_cp("parallel"),
    )(xb, w_in_ext, w1, w2, wk_ext, wv, e128, g_q, g_kv, cext, sext, cs128)


def _fwd_mix(o_a, o_b, gates, x, b_gate, w_oa, w_ob, w_out, ln_g, ln_b, *, seq):
    tokens = x.shape[0]
    tm = min(TOKEN_TILE, seq)

    def body(oa_ref, ob_ref, gt_ref, x_ref, bg_ref, woa_ref, wob_ref, wout_ref, g_ref, b_ref,
             h_ref, hb_ref, xhat_ref, rstd_ref, ya_ref, yb_ref, mix_ref):
        ya = _dot(oa_ref[...], woa_ref[...])
        yb = _dot(ob_ref[...], wob_ref[...])
        g0 = _sigmoid(gt_ref[:, 0:D_MODEL] + bg_ref[0:1, :])
        g1 = _sigmoid(gt_ref[:, D_MODEL:] + bg_ref[1:2, :])
        mix = (g0 * ya + g1 * yb).astype(BF16)
        z = ALPHA * x_ref[...] + _dot(mix, wout_ref[...])
        zc = z - jnp.mean(z, axis=1, keepdims=True)
        rstd = lax.rsqrt(jnp.mean(zc * zc, axis=1, keepdims=True) + LN_EPS)
        xhat = zc * rstd
        h = xhat * g_ref[...] + b_ref[...]
        h_ref[...] = h
        hb_ref[...] = h.astype(BF16)
        xhat_ref[...] = xhat
        rstd_ref[...] = jnp.broadcast_to(rstd, (tm, LANES))
        ya_ref[...] = ya.astype(BF16)
        yb_ref[...] = yb.astype(BF16)
        mix_ref[...] = mix

    outs = [(D_MODEL, F32), (D_MODEL, BF16), (D_MODEL, F32), (LANES, F32), (D_MODEL, BF16), (D_MODEL, BF16), (D_MODEL, BF16)]
    return pl.pallas_call(
        body, name="fwd_mix", grid=(tokens // tm,),
        out_shape=tuple(jax.ShapeDtypeStruct((tokens, w), dt) for w, dt in outs),
        in_specs=[_rows(tm, DIL_WIDTH), _rows(tm, DIL_WIDTH), _rows(tm, 2 * D_MODEL), _rows(tm, D_MODEL),
                  _full(b_gate.shape), _full(w_oa.shape), _full(w_ob.shape), _full(w_out.shape),
                  _full(ln_g.shape), _full(ln_b.shape)],
        out_specs=tuple(_rows(tm, w) for w, _ in outs),
        compiler_params=_cp("parallel"),
    )(o_a, o_b, gates, x, b_gate, w_oa, w_ob, w_out, ln_g, ln_b)


def _fwd_mlp(hb, h, target, w_ff1, w_ff2, ln_g, ln_b, *, seq):
    tokens = h.shape[0]
    tm = min(2 * TOKEN_TILE, seq)
    tf = FF_SHARD
    nf = N_DEV // FF_STEP

    def body(hb_ref, h_ref, tg_ref, w1_ref, w2_ref, g_ref, b_ref, u_ref, dz_ref, dzb_ref, stat_ref, acc):
        i, j = pl.program_id(0), pl.program_id(1)

        @pl.when((i == 0) & (j == 0))
        def _():
            stat_ref[...] = jnp.zeros_like(stat_ref)

        @pl.when(j == 0)
        def _():
            acc[...] = jnp.zeros_like(acc)

        for s in range(FF_STEP):
            u = _dot(hb_ref[...], w1_ref[s])
            u_ref[:, s * tf:(s + 1) * tf] = u.astype(BF16)
            a = jnp.square(jnp.maximum(u, 0.0)).astype(BF16)
            acc[...] += _dot(a, w2_ref[s * tf:(s + 1) * tf, :])

        @pl.when(j == nf - 1)
        def _():
            z = ALPHA * h_ref[...] + acc[...]
            zc = z - jnp.mean(z, axis=1, keepdims=True)
            rstd = lax.rsqrt(jnp.mean(zc * zc, axis=1, keepdims=True) + LN_EPS)
            xhat = zc * rstd
            err = xhat * g_ref[...] + b_ref[...] - tg_ref[...]
            dy = err * (1.0 / D_MODEL)
            dz = _ln_bwd(dy, xhat, rstd, g_ref[...])
            dz_ref[...] = dz
            dzb_ref[...] = dz.astype(BF16)
            stat_ref[0:1, :] += jnp.sum(dy * xhat, axis=0, keepdims=True)
            stat_ref[1:2, :] += jnp.sum(dy, axis=0, keepdims=True)
            stat_ref[2:3, :] += jnp.sum(jnp.sum(err * err, axis=1, keepdims=True), axis=0, keepdims=True) * (0.5 / D_MODEL)

    return pl.pallas_call(
        body, name="fwd_mlp", grid=(tokens // tm, nf),
        out_shape=(jax.ShapeDtypeStruct((tokens, D_FF), BF16), jax.ShapeDtypeStruct((tokens, D_MODEL), F32),
                   jax.ShapeDtypeStruct((tokens, D_MODEL), BF16), jax.ShapeDtypeStruct((8, D_MODEL), F32)),
        in_specs=[_rows(tm, D_MODEL), _rows(tm, D_MODEL), _rows(tm, D_MODEL),
                  pl.BlockSpec((FF_STEP, D_MODEL, tf), lambda i, j: (j, 0, 0)),
                  pl.BlockSpec((FF_STEP * tf, D_MODEL), lambda i, j: (j, 0)),
                  _full(ln_g.shape), _full(ln_b.shape)],
        out_specs=(pl.BlockSpec((tm, FF_STEP * tf), lambda i, j: (i, j)), _rows(tm, D_MODEL), _rows(tm, D_MODEL),
                   _full((8, D_MODEL))),
        scratch_shapes=[pltpu.VMEM((tm, D_MODEL), F32)],
        compiler_params=_cp("arbitrary", "arbitrary"),
    )(hb, h, target, w_ff1, w_ff2, ln_g, ln_b)


def _bwd_mlp(dz2, dz2b, u, xhat1, rstd1, w_ff1, w_ff2, ln_g, *, seq):
    tokens = dz2.shape[0]
    tm = min(2 * TOKEN_TILE, seq)
    tf = FF_SHARD
    nf = N_DEV // FF_STEP

    def body(dz_ref, dzb_ref, u_ref, xh_ref, rs_ref, w1_ref, w2_ref, g_ref, du_ref, dz1_ref, dz1b_ref, stat_ref, acc):
        i, j = pl.program_id(0), pl.program_id(1)

        @pl.when((i == 0) & (j == 0))
        def _():
            stat_ref[...] = jnp.zeros_like(stat_ref)

        @pl.when(j == 0)
        def _():
            acc[...] = jnp.zeros_like(acc)

        for s in range(FF_STEP):
            da = _dot_nt(dzb_ref[...], w2_ref[s * tf:(s + 1) * tf, :])
            du = (da * (2.0 * jnp.maximum(u_ref[:, s * tf:(s + 1) * tf].astype(F32), 0.0))).astype(BF16)
            du_ref[:, s * tf:(s + 1) * tf] = du
            acc[...] += _dot_nt(du, w1_ref[s])

        @pl.when(j == nf - 1)
        def _():
            dh = ALPHA * dz_ref[...] + acc[...]
            xhat = xh_ref[...]
            dz1 = _ln_bwd(dh, xhat, rs_ref[:, 0:1], g_ref[...])
            dz1_ref[...] = dz1
            dz1b_ref[...] = dz1.astype(BF16)
            stat_ref[0:1, :] += jnp.sum(dh * xhat, axis=0, keepdims=True)
            stat_ref[1:2, :] += jnp.sum(dh, axis=0, keepdims=True)

    return pl.pallas_call(
        body, name="bwd_mlp", grid=(tokens // tm, nf),
        out_shape=(jax.ShapeDtypeStruct((tokens, D_FF), BF16), jax.ShapeDtypeStruct((tokens, D_MODEL), F32),
                   jax.ShapeDtypeStruct((tokens, D_MODEL), BF16), jax.ShapeDtypeStruct((8, D_MODEL), F32)),
        in_specs=[_rows(tm, D_MODEL), _rows(tm, D_MODEL), pl.BlockSpec((tm, FF_STEP * tf), lambda i, j: (i, j)),
                  _rows(tm, D_MODEL), _rows(tm, LANES),
                  pl.BlockSpec((FF_STEP, D_MODEL, tf), lambda i, j: (j, 0, 0)),
                  pl.BlockSpec((FF_STEP * tf, D_MODEL), lambda i, j: (j, 0)),
                  _full(ln_g.shape)],
        out_specs=(pl.BlockSpec((tm, FF_STEP * tf), lambda i, j: (i, j)), _rows(tm, D_MODEL), _rows(tm, D_MODEL),
                   _full((8, D_MODEL))),
        scratch_shapes=[pltpu.VMEM((tm, D_MODEL), F32)],
        compiler_params=_cp("arbitrary", "arbitrary"),
    )(dz2, dz2b, u, xhat1, rstd1, w_ff1, w_ff2, ln_g)


def _bwd_mix(dz1b, gates, y_a, y_b, b_gate, w_oa, w_ob, w_out, *, seq):
    tokens = dz1b.shape[0]
    tm = min(TOKEN_TILE, seq)

    def body(dz_ref, gt_ref, ya_ref, yb_ref, bg_ref, woa_ref, wob_ref, wout_ref,
             dgt_ref, dya_ref, dyb_ref, doa_ref, dob_ref, stat_ref):
        @pl.when(pl.program_id(0) == 0)
        def _():
            stat_ref[...] = jnp.zeros_like(stat_ref)

        dmix = _dot_nt(dz_ref[...], wout_ref[...])
        for k, (y_ref, w_ref, dy_ref, do_ref) in enumerate(((ya_ref, woa_ref, dya_ref, doa_ref), (yb_ref, wob_ref, dyb_ref, dob_ref))):
            g = _sigmoid(gt_ref[:, k * D_MODEL:(k + 1) * D_MODEL] + bg_ref[k:k + 1, :])
            dgate = dmix * y_ref[...].astype(F32) * g * (1.0 - g)
            dgt_ref[:, k * D_MODEL:(k + 1) * D_MODEL] = dgate.astype(BF16)
            stat_ref[k:k + 1, :] += jnp.sum(dgate, axis=0, keepdims=True)
            dy = (dmix * g).astype(BF16)
            dy_ref[...] = dy
            do_ref[...] = _dot_nt(dy, w_ref[...]).astype(BF16)

    outs = [(2 * D_MODEL, BF16), (D_MODEL, BF16), (D_MODEL, BF16), (DIL_WIDTH, BF16), (DIL_WIDTH, BF16)]
    return pl.pallas_call(
        body, name="bwd_mix", grid=(tokens // tm,),
        out_shape=tuple(jax.ShapeDtypeStruct((tokens, w), dt) for w, dt in outs) + (jax.ShapeDtypeStruct((8, D_MODEL), F32),),
        in_specs=[_rows(tm, D_MODEL), _rows(tm, 2 * D_MODEL), _rows(tm, D_MODEL), _rows(tm, D_MODEL),
                  _full(b_gate.shape), _full(w_oa.shape), _full(w_ob.shape), _full(w_out.shape)],
        out_specs=tuple(_rows(tm, w) for w, _ in outs) + (_full((8, D_MODEL)),),
        compiler_params=_cp("arbitrary"),
    )(dz1b, gates, y_a, y_b, b_gate, w_oa, w_ob, w_out)


def _bwd_proj(dqp, dkp, dvm, dq_d, dk_d, dv_d, dgates, dz1, low, w_in_ext, w1, w2, wk_ext, wv, e128, g_q, g_kv, cext, sext, cs128, *, seq):
    tokens = dz1.shape[0]
    tm = min(TOKEN_TILE, seq)
    ns = seq // tm

    def body(dqp_ref, dkp_ref, dvm_ref, dqd_ref, dkd_ref, dvd_ref, dgt_ref, dz_ref, low_ref, win_ref, w1_ref, w2_ref, wk_ref,
             wv_ref, e_ref, gq_ref, gkv_ref, c_ref, s_ref, cs_ref, dx_ref, dproj_ref, da_ref, db_ref, stat_ref):
        @pl.when(pl.program_id(0) == 0)
        def _():
            stat_ref[...] = jnp.zeros_like(stat_ref)

        low = low_ref[...]
        dqp = dqp_ref[...].astype(F32)
        d_a = (dqp * c_ref[...]).astype(BF16)
        d_b = (dqp * s_ref[...]).astype(BF16)
        da_ref[...] = d_a
        db_ref[...] = d_b
        q_a = low[:, 0:Q_LORA]
        _, rq = _rms(q_a, gq_ref[...])
        dq_a, gq_terms = _rms_bwd(_dot_nt(d_a, w1_ref[...]) + _dot_nt(d_b, w2_ref[...]), q_a, rq, gq_ref[...])
        kv_a = low[:, Q_LORA:Q_LORA + KV_LORA]
        _, rkv = _rms(kv_a, gkv_ref[...])
        dkp = dkp_ref[...]
        dkv_a, gkv_terms = _rms_bwd(_dot_nt(dkp, wk_ref[...]) + _dot_nt(dvm_ref[...], wv_ref[...]), kv_a, rkv, gkv_ref[...])
        dkr = _dot_nt(dkp, e_ref[...])
        dkr = (dkr + pltpu.roll(dkr, ROPE, 1)) * cs_ref[...]
        stat_ref[0:1, 0:Q_LORA] += jnp.sum(gq_terms, axis=0, keepdims=True)
        stat_ref[1:2, 0:KV_LORA] += jnp.sum(gkv_terms, axis=0, keepdims=True)
        dproj_ref[:, 0:Q_LORA] = dq_a.astype(BF16)
        dproj_ref[:, Q_LORA:Q_LORA + KV_LORA] = dkv_a.astype(BF16)
        dproj_ref[:, Q_LORA + KV_LORA:LOW_W] = dkr.astype(BF16)
        dproj_ref[:, LOW_W:LOW_W + DIL_WIDTH] = dqd_ref[...]
        dproj_ref[:, LOW_W + DIL_WIDTH:LOW_W + 2 * DIL_WIDTH] = dkd_ref[...]
        dproj_ref[:, LOW_W + 2 * DIL_WIDTH:LOW_W + 3 * DIL_WIDTH] = dvd_ref[...]
        dproj_ref[:, LOW_W + 3 * DIL_WIDTH:] = dgt_ref[...]
        dx_ref[...] = ALPHA * dz_ref[...] + _dot_nt(dproj_ref[...], win_ref[...])

    wide = N_PAIRS * PAIR_W
    return pl.pallas_call(
        body, name="bwd_proj", grid=(tokens // tm,),
        out_shape=(jax.ShapeDtypeStruct((tokens, D_MODEL), F32), jax.ShapeDtypeStruct((tokens, IN_EXT), BF16),
                   jax.ShapeDtypeStruct((tokens, wide), BF16), jax.ShapeDtypeStruct((tokens, wide), BF16),
                   jax.ShapeDtypeStruct((8, D_MODEL), F32)),
        in_specs=[_rows(tm, wide), _rows(tm, wide), _rows(tm, DIL_WIDTH), _rows(tm, DIL_WIDTH), _rows(tm, DIL_WIDTH),
                  _rows(tm, DIL_WIDTH), _rows(tm, 2 * D_MODEL),
                  _rows(tm, D_MODEL), _rows(tm, LOW_W), _full(w_in_ext.shape), _full(w1.shape), _full(w2.shape),
                  _full(wk_ext.shape), _full(wv.shape), _full(e128.shape), _full(g_q.shape), _full(g_kv.shape),
                  pl.BlockSpec((tm, wide), lambda i: (i % ns, 0)), pl.BlockSpec((tm, wide), lambda i: (i % ns, 0)),
                  pl.BlockSpec((tm, LANES), lambda i: (i % ns, 0))],
        out_specs=(_rows(tm, D_MODEL), _rows(tm, IN_EXT), _rows(tm, wide), _rows(tm, wide), _full((8, D_MODEL))),
        compiler_params=_cp("arbitrary"),
    )(dqp, dkp, dvm, dq_d, dk_d, dv_d, dgates, dz1, low, w_in_ext, w1, w2, wk_ext, wv, e128, g_q, g_kv, cext, sext, cs128)


def _wgrad(a, b, name, square_relu=False, by_shard=False):
    tokens, ka = a.shape
    n = b.shape[1]
    tka = min(ka, 512)
    shard = n // N_DEV
    tn = WGRAD_SHARDS * shard if by_shard else max(w for w in range(LANES, min(n, 2304) + 1, LANES) if n % w == 0)
    tt = min(tokens, 512)
    nt = tokens // tt

    def body(a_ref, b_ref, o_ref, acc):
        kt = pl.program_id(2)

        @pl.when(kt == 0)
        def _():
            acc[...] = jnp.zeros_like(acc)

        at = a_ref[...]
        if square_relu:
            at = jnp.square(jnp.maximum(at.astype(F32), 0.0)).astype(BF16)
        acc[...] += _dot_tn(at, b_ref[...])

        @pl.when(kt == nt - 1)
        def _():
            if by_shard:
                for s in range(WGRAD_SHARDS):
                    o_ref[s] = acc[:, s * shard:(s + 1) * shard].astype(BF16)
            else:
                o_ref[...] = acc[...].astype(BF16)

    if by_shard:
        out_shape, out_spec = (N_DEV, ka, shard), pl.BlockSpec((WGRAD_SHARDS, tka, shard), lambda i, j, k: (j, i, 0))
    else:
        out_shape, out_spec = (ka, n), pl.BlockSpec((tka, tn), lambda i, j, k: (i, j))
    return pl.pallas_call(
        body, name=name, grid=(ka // tka, n // tn, nt), out_shape=jax.ShapeDtypeStruct(out_shape, BF16),
        in_specs=[pl.BlockSpec((tt, tka), lambda i, j, k: (k, i)), pl.BlockSpec((tt, tn), lambda i, j, k: (k, j))],
        out_specs=out_spec,
        scratch_shapes=[pltpu.VMEM((tka, tn), F32)],
        compiler_params=_cp("parallel", "parallel", "arbitrary"),
    )(a, b)


def _adam_math(w, g, m, v):
    m = ADAM_B1 * m + (1.0 - ADAM_B1) * g
    v = ADAM_B2 * v + (1.0 - ADAM_B2) * jnp.square(g)
    m_hat = m / (1.0 - ADAM_B1 ** ADAM_STEP)
    v_hat = v / (1.0 - ADAM_B2 ** ADAM_STEP)
    return -ADAM_LR * (m_hat / (jnp.sqrt(v_hat) + ADAM_EPS) + ADAM_WD * w), m, v


def _adamw(w, m, v, own, parts, name):
    rows, cols = w.shape
    tr = _row_tile(rows)
    n_parts = parts.shape[0]

    def body(*refs):
        w_ref, m_ref, v_ref = refs[:3]
        p_ref = refs[-5]
        g_ref, d_ref, nm_ref, nv_ref = refs[-4:]
        g = refs[3][...] if own is not None else p_ref[0].astype(F32)
        for d in range(0 if own is not None else 1, n_parts):
            g = g + p_ref[d].astype(F32)
        g_ref[...] = g
        d_ref[...], nm_ref[...], nv_ref[...] = _adam_math(w_ref[...], g, m_ref[...], v_ref[...])

    blk = pl.BlockSpec((tr, cols), lambda i: (i, 0))
    args = [w, m, v] + ([own] if own is not None else []) + [parts]
    return pl.pallas_call(
        body, name=name, grid=(rows // tr,), out_shape=(jax.ShapeDtypeStruct((rows, cols), F32),) * 4,
        in_specs=[blk] * (len(args) - 1) + [pl.BlockSpec((n_parts, tr, cols), lambda i: (0, i, 0))],
        out_specs=(blk,) * 4, compiler_params=_cp("parallel"),
    )(*args)


def _adamw_small(parts, w, m, v):
    _, rows, cols = parts.shape

    def body(p_ref, w_ref, m_ref, v_ref, g_ref, d_ref, nm_ref, nv_ref):
        g = p_ref[0]
        for d in range(1, N_DEV):
            g = g + p_ref[d]
        g_ref[...] = g
        d_ref[...], nm_ref[...], nv_ref[...] = _adam_math(w_ref[...], g, m_ref[...], v_ref[...])

    return pl.pallas_call(
        body, name="adamw_replicated", out_shape=(jax.ShapeDtypeStruct((rows, cols), F32),) * 4,
        in_specs=[_full(parts.shape)] + [_full((rows, cols))] * 3, out_specs=(_full((rows, cols)),) * 4, grid=(1,),
        compiler_params=_cp("arbitrary"),
    )(parts, w, m, v)


def _pad_rows(a2d, mult):
    pad = (-a2d.shape[-2]) % mult
    return jnp.pad(a2d, [(0, 0)] * (a2d.ndim - 2) + [(0, pad), (0, 0)]) if pad else a2d


def _pad_cols(a):
    pad = (-a.shape[-1]) % LANES
    return jnp.pad(a, [(0, 0)] * (a.ndim - 1) + [(0, pad)]) if pad else a


def _rot_cols(w):
    half = ROPE // 2
    return jnp.concatenate([-w[..., half:], w[..., :half]], axis=-1)


def _unrot_cols(dw):
    half = ROPE // 2
    return jnp.concatenate([dw[..., half:], -dw[..., :half]], axis=-1)


def _from_col_shards(stacked):
    return stacked.transpose(1, 0, 2).reshape(stacked.shape[1], -1)


def _to_col_shards(full):
    r = full.shape[0]
    return full.reshape(r, N_DEV, -1).transpose(1, 0, 2)


def _rope_tables(seq):
    half = ROPE // 2
    inv = jnp.power(ROPE_THETA, -jnp.arange(half, dtype=F32) / half)
    ang = jnp.arange(seq, dtype=F32)[:, None] * inv[None, :]
    cos = jnp.concatenate([jnp.cos(ang)] * 2, axis=1)
    sin = jnp.concatenate([jnp.sin(ang)] * 2, axis=1)
    ones, zeros = jnp.ones((seq, 2 * NOPE), F32), jnp.zeros((seq, 2 * NOPE), F32)
    pad = jnp.zeros((seq, PAIR_W - 2 * NOPE - 2 * ROPE), F32)
    cext = jnp.tile(jnp.concatenate([ones, cos, cos, pad], axis=1), (1, N_PAIRS))
    sext = jnp.tile(jnp.concatenate([zeros, sin, sin, pad], axis=1), (1, N_PAIRS))
    cs128 = jnp.concatenate([cos, sin, jnp.zeros((seq, LANES - 2 * ROPE), F32)], axis=1)
    return cext, sext, cs128


def _pair_slabs(nope, rope):
    k = nope.shape[0]
    nope = nope.reshape(k, N_PAIRS, 2 * NOPE)
    rope = jnp.zeros((k, N_PAIRS, 2 * ROPE), nope.dtype) if rope is None else rope.reshape(k, N_PAIRS, 2 * ROPE)
    pad = jnp.zeros((k, N_PAIRS, PAIR_W - 2 * NOPE - 2 * ROPE), nope.dtype)
    return jnp.concatenate([nope, rope, pad], axis=2).reshape(k, N_PAIRS * PAIR_W)


def _split_slabs(slabs):
    k = slabs.shape[0]
    s = slabs.reshape(k, N_PAIRS, PAIR_W)
    return s[:, :, :2 * NOPE].reshape(k, N_HEADS, NOPE), s[:, :, 2 * NOPE:2 * NOPE + 2 * ROPE].reshape(k, N_HEADS, ROPE)


def kernel(x, w_in, b_gate, g_q_a, w_uq, g_kv_a, w_ukv, w_o_mla, w_o_dil, w_out, ln1_g, ln1_b, w_ff1, w_ff2, ln2_g, ln2_b, loss_target, m_w_in, m_b_gate, m_g_q_a, m_w_uq, m_g_kv_a, m_w_ukv, m_w_o_mla, m_w_o_dil, m_w_out, m_ln1_g, m_ln1_b, m_w_ff1, m_w_ff2, m_ln2_g, m_ln2_b, v_w_in, v_b_gate, v_g_q_a, v_w_uq, v_g_kv_a, v_w_ukv, v_w_o_mla, v_w_o_dil, v_w_out, v_ln1_g, v_ln1_b, v_w_ff1, v_w_ff2, v_ln2_g, v_ln2_b):
    batch, seq, _ = x.shape
    tokens = batch * seq
    weights = dict(w_in=w_in, w_uq=w_uq, w_ukv=w_ukv, w_o_mla=w_o_mla, w_o_dil=w_o_dil, w_out=w_out, w_ff1=w_ff1, w_ff2=w_ff2, b_gate=b_gate)
    mom_m = dict(w_in=m_w_in, w_uq=m_w_uq, w_ukv=m_w_ukv, w_o_mla=m_w_o_mla, w_o_dil=m_w_o_dil, w_out=m_w_out, w_ff1=m_w_ff1, w_ff2=m_w_ff2, b_gate=m_b_gate)
    mom_v = dict(w_in=v_w_in, w_uq=v_w_uq, w_ukv=v_w_ukv, w_o_mla=v_w_o_mla, w_o_dil=v_w_o_dil, w_out=v_w_out, w_ff1=v_w_ff1, w_ff2=v_w_ff2, b_gate=v_b_gate)

    first = ["w_in", "w_uq", "w_ukv"]
    widths = [weights[n].shape[2] for n in first]
    shards = [_pad_cols(weights[n][0].astype(BF16)) for n in first]
    g_in, g_uq, g_ukv = (g[:, :, :w] for g, w in zip(_run_comm(_Gather(shards), shards, "all_gather_first_weights"), widths))

    wi = _from_col_shards(g_in)
    s0, s1, s2, s3 = Q_LORA, Q_LORA + KV_LORA, Q_LORA + KV_LORA + ROPE, Q_LORA + KV_LORA + ROPE + 3 * DIL_WIDTH
    w_kr = wi[:, s1:s2]
    w_in_ext = jnp.concatenate([wi[:, :s2], _rot_cols(w_kr), jnp.zeros((D_MODEL, LOW_W - s2 - ROPE), BF16), wi[:, s2:]], axis=1)
    uq = _from_col_shards(g_uq).reshape(Q_LORA, N_HEADS, NOPE + ROPE)
    w1 = _pair_slabs(uq[:, :, :NOPE], uq[:, :, NOPE:])
    w2 = _pair_slabs(jnp.zeros_like(uq[:, :, :NOPE]), _rot_cols(uq[:, :, NOPE:]))
    ukv = _from_col_shards(g_ukv).reshape(KV_LORA, N_HEADS, NOPE + HEAD_V)
    wk_ext = _pair_slabs(ukv[:, :, :NOPE], None)
    wv = ukv[:, :, NOPE:].reshape(KV_LORA, N_HEADS * HEAD_V)
    eye = jnp.eye(ROPE, dtype=BF16)
    e_slab = jnp.concatenate([jnp.zeros((ROPE, 2 * NOPE), BF16), eye, eye, jnp.zeros((ROPE, PAIR_W - 2 * NOPE - 2 * ROPE), BF16)], axis=1)
    e128 = jnp.concatenate([jnp.tile(e_slab, (1, N_PAIRS)), jnp.zeros((LANES - ROPE, N_PAIRS * PAIR_W), BF16)], axis=0)
    cext, sext, cs128 = _rope_tables(seq)
    dil_bias = _dilated_bias_table(seq)
    no_bias = jnp.zeros((1, 8, LANES), F32)

    x2 = x.reshape(tokens, D_MODEL)
    xb = x2.astype(BF16)
    low, gates, qkvd, qp, kp, vm, qn, kvn = _fwd_proj(xb, w_in_ext, w1, w2, wk_ext, wv, e128, g_q_a, g_kv_a, cext, sext, cs128, seq=seq)
    bg = b_gate[0]
    bg_hi = bg.astype(BF16)
    bg_lo = (bg - bg_hi.astype(F32)).astype(BF16)
    later = [weights[n][0].astype(BF16) for n in ("w_o_mla", "w_o_dil", "w_out", "w_ff1", "w_ff2")]
    later.append(_pad_rows(jnp.concatenate([bg_hi, bg_lo], axis=0), 16))
    mla = dict(batch=batch, seq=seq, width=PAIR_W, col0=(0, 0, 0), dilated=False, scale=MLA_SCALE)
    dil = dict(batch=batch, seq=seq, width=LANES, col0=(0, N_PAIRS, 2 * N_PAIRS), dilated=True, scale=DIL_SCALE)
    o_a, lse_a, g_oa, g_ob, g_out, g_ff1, g_ff2, g_bg = _attn_fwd(
        qp, kp, vm, no_bias, name="mla_attention_fwd", comm=_Gather(later), comm_arrays=later, **mla)
    o_b, lse_b = _attn_fwd(qkvd, qkvd, qkvd, dil_bias, name="dilated_attention_fwd", **dil)
    w_oa, w_ob = _from_col_shards(g_oa), _from_col_shards(g_ob)
    w_out_full = g_out.reshape(D_MODEL, D_MODEL)
    w_ff2_full = g_ff2.reshape(D_FF, D_MODEL)
    bg_parts = g_bg.astype(F32)
    b_gate_full = _from_col_shards(bg_parts[:, 0:2] + bg_parts[:, 2:4])
    h, hb, xhat1, rstd1, y_a, y_b, mix = _fwd_mix(o_a, o_b, gates, x2, b_gate_full, w_oa, w_ob, w_out_full, ln1_g, ln1_b, seq=seq)
    u, dz2, dz2b, stat2 = _fwd_mlp(hb, h, loss_target.reshape(tokens, D_MODEL), g_ff1, w_ff2_full, ln2_g, ln2_b, seq=seq)

    du, dz1, dz1b, stat1 = _bwd_mlp(dz2, dz2b, u, xhat1, rstd1, g_ff1, w_ff2_full, ln1_g, seq=seq)
    dw_ff = [_wgrad(hb, du, "wgrad_ff1", by_shard=True),
             _wgrad(u, dz2b, "wgrad_ff2", square_relu=True).reshape(N_DEV, FF_SHARD, D_MODEL)]
    dgates, dy_a, dy_b, do_a, do_b, stat_g = _bwd_mix(dz1b, gates, y_a, y_b, b_gate_full, w_oa, w_ob, w_out_full, seq=seq)
    dqp, dkp, dvm, r_ff1, r_ff2 = _attn_bwd(qp, kp, vm, o_a, do_a, lse_a, no_bias, name="mla_attention_bwd",
                                            comm=_Scatter(dw_ff), comm_arrays=dw_ff, **mla)
    dw_mid = [_to_col_shards(_wgrad(o_a, dy_a, "wgrad_o_mla")), _to_col_shards(_wgrad(o_b, dy_b, "wgrad_o_dil")),
              _wgrad(mix, dz1b, "wgrad_out").reshape(N_DEV, D_MODEL // N_DEV, D_MODEL),
              _pad_rows(_to_col_shards(stat_g[0:2]).astype(BF16), 16)]
    dq_d, dk_d, dv_d, r_oa, r_ob, r_out, r_bg = _attn_bwd(qkvd, qkvd, qkvd, o_b, do_b, lse_b, dil_bias, name="dilated_attention_bwd",
                                                          comm=_Scatter(dw_mid), comm_arrays=dw_mid, **dil)
    grad_x, dproj, d_a, d_b, stat_r = _bwd_proj(dqp, dkp, dvm, dq_d, dk_d, dv_d, dgates, dz1, low, w_in_ext, w1, w2, wk_ext, wv,
                                                e128, g_q_a, g_kv_a, cext, sext, cs128, seq=seq)

    dw_in_ext = _wgrad(xb, dproj, "wgrad_in")
    dw1 = _wgrad(qn, d_a, "wgrad_uq_direct")
    dw2 = _wgrad(qn, d_b, "wgrad_uq_rotated")
    dwk = _wgrad(kvn, dkp, "wgrad_ukv_k")
    dwv = _wgrad(kvn, dvm, "wgrad_ukv_v")
    dlow = dw_in_ext[:, :LOW_W]
    dw_kr = dlow[:, s1:s2] + _unrot_cols(dlow[:, s2:s2 + ROPE])
    dw_in = jnp.concatenate([dlow[:, :s1], dw_kr, dw_in_ext[:, LOW_W:]], axis=1)
    n1, r1 = _split_slabs(dw1)
    _, r2 = _split_slabs(dw2)
    dw_uq = jnp.concatenate([n1, r1 + _unrot_cols(r2)], axis=2).reshape(Q_LORA, N_HEADS * (NOPE + ROPE))
    nk, _ = _split_slabs(dwk)
    dw_ukv = jnp.concatenate([nk, dwv.reshape(KV_LORA, N_HEADS, HEAD_V)], axis=2).reshape(KV_LORA, N_HEADS * (NOPE + HEAD_V))
    last = [_pad_cols(_to_col_shards(dw)) for dw in (dw_in, dw_uq, dw_ukv)]
    mine, theirs = _rs_sibling(last, "rs_last_sibling_exchange")
    sums = [_pair_sum(a, b, "rs_last_pair_sum_" + n) for a, b, n in zip(mine, theirs, first)]
    got = _rs_chips([s[1] for s in sums], "rs_last_chip_exchange")

    upd = {}
    for n, w, (own, _), parts in zip(first, widths, sums, got):
        upd[n] = _adamw(weights[n][0], mom_m[n][0], mom_v[n][0], own[:, :w], parts[:, :, :w], "adamw_" + n)
    for n, parts in (("w_o_mla", r_oa), ("w_o_dil", r_ob), ("w_out", r_out), ("w_ff1", r_ff1), ("w_ff2", r_ff2)):
        upd[n] = _adamw(weights[n][0], mom_m[n][0], mom_v[n][0], None, parts, "adamw_" + n)
    bg_upd = _adamw(_pad_rows(b_gate[0], 16), _pad_rows(m_b_gate[0], 16), _pad_rows(v_b_gate[0], 16), None, r_bg, "adamw_b_gate")
    upd["b_gate"] = tuple(t[0:2] for t in bg_upd)

    small_w = [g_q_a, g_kv_a, ln1_g, ln1_b, ln2_g, ln2_b]
    small_m = [m_g_q_a, m_g_kv_a, m_ln1_g, m_ln1_b, m_ln2_g, m_ln2_b]
    small_v = [v_g_q_a, v_g_kv_a, v_ln1_g, v_ln1_b, v_ln2_g, v_ln2_b]
    small_widths = [a.shape[1] for a in small_w]
    partial = jnp.concatenate([stat_r[0:1, :Q_LORA], stat_r[1:2, :KV_LORA], stat1[0:1], stat1[1:2], stat2[0:1], stat2[1:2],
                               stat2[2:3, :LANES]], axis=1)

    def as_rows(vecs, extra):
        flat = jnp.concatenate(vecs + [jnp.zeros((1, extra), F32)], axis=1)
        return _pad_rows(flat.reshape(-1, LANES), 8)

    partial = _pad_rows(partial.reshape(-1, LANES), 8)
    (every,) = _run_comm(_Gather([partial]), [partial], "all_gather_replicated_grads")
    g_s, d_s, nm_s, nv_s = _adamw_small(every, as_rows(small_w, LANES), as_rows(small_m, LANES), as_rows(small_v, LANES))

    def split_small(a):
        flat = a.reshape(1, -1)
        out, c0 = [], 0
        for w in small_widths:
            out.append(flat[:, c0:c0 + w])
            c0 += w
        return out, flat[0, c0]

    g_small, loss = split_small(g_s)
    small = [g_small, split_small(d_s)[0], split_small(nm_s)[0], split_small(nv_s)[0]]

    order = ["w_in", "b_gate", "g_q_a", "w_uq", "g_kv_a", "w_ukv", "w_o_mla", "w_o_dil", "w_out", "ln1_g", "ln1_b", "w_ff1", "w_ff2", "ln2_g", "ln2_b"]
    small_names = ["g_q_a", "g_kv_a", "ln1_g", "ln1_b", "ln2_g", "ln2_b"]

    def pick(kind):
        return [small[kind][small_names.index(n)] if n in small_names else upd[n][kind][None] for n in order]

    return (loss, grad_x.reshape(batch, seq, D_MODEL), *pick(0), *pick(1), *pick(2), *pick(3))
```

```python
import functools
import math

import jax
import jax.numpy as jnp
from jax import lax
from jax.experimental import pallas as pl
from jax.experimental.pallas import tpu as pltpu

F32 = jnp.float32
BF16 = jnp.bfloat16
I32 = jnp.int32

D_MODEL = 1024
N_HEADS = 8
NOPE = 64
ROPE = 32
HEAD_V = 64
Q_LORA = 384
KV_LORA = 256
DIL_WIDTH = 512
D_FF = 4096
ROPE_THETA = 10000.0
LN_EPS = 1e-5
RMS_EPS = 1e-6
NEG = -1e30
ALPHA = 2.0 ** 0.25
MLA_SCALE = (NOPE + ROPE) ** -0.5
DIL_SCALE = 64 ** -0.5
ADAM_LR, ADAM_B1, ADAM_B2, ADAM_EPS, ADAM_WD, ADAM_STEP = 0.001, 0.9, 0.999, 1e-08, 0.01, 10

LANES = 128
PAIR_W = 256
N_PAIRS = N_HEADS // 2
LOW_W = 768
IN_EXT = LOW_W + 3 * DIL_WIDTH + 2 * D_MODEL
N_DEV = 8
FF_SHARD = D_FF // N_DEV
FF_STEP = 2
WGRAD_SHARDS = 4
TOKEN_TILE = 256
ATTN_TILE = 256
VMEM_LIMIT = 56 << 20

MESH = pl.DeviceIdType.MESH
ANY = pl.BlockSpec(memory_space=pl.ANY)
CHIP_FLIPS = ((0, 0), (0, 1), (1, 0), (1, 1))
PEER_FLIPS = tuple((fx, fy, fc) for fx in (0, 1) for fy in (0, 1) for fc in (0, 1))[1:]


def _cp(*sem):
    return pltpu.CompilerParams(dimension_semantics=sem or None, vmem_limit_bytes=VMEM_LIMIT)


def _full(shape):
    nd = len(shape)
    return pl.BlockSpec(shape, lambda *_: (0,) * nd)


def _rows(tm, width):
    return pl.BlockSpec((tm, width), lambda i, *_: (i, 0))


def _dot(a, b):
    return jnp.dot(a, b, preferred_element_type=F32)


def _dot_nt(a, b):
    return lax.dot_general(a, b, (((1,), (1,)), ((), ())), preferred_element_type=F32)


def _dot_tn(a, b):
    return lax.dot_general(a, b, (((0,), (0,)), ((), ())), preferred_element_type=F32)


def _sigmoid(z):
    return 1.0 / (1.0 + jnp.exp(-z))


def _place():
    return lax.axis_index("x"), lax.axis_index("y"), lax.axis_index("c")


def _flip(v, f):
    return 1 - v if f else v


class _Gather:
    def __init__(self, shards):
        self.n = len(shards)
        self.out_shape = [jax.ShapeDtypeStruct((N_DEV, *s.shape), s.dtype) for s in shards]
        self.scratch = [pltpu.SemaphoreType.DMA((7 * self.n,)), pltpu.SemaphoreType.DMA((7 * self.n,)),
                        pltpu.SemaphoreType.DMA((self.n,))]

    def _copies(self, what, srcs, dsts, send, recv, local):
        x, y, c = _place()
        chips = [(_flip(x, fx), _flip(y, fy)) for fx, fy in CHIP_FLIPS[1:]]
        out = []
        for a in range(self.n):
            def slot(px, py, pc, a=a):
                return dsts[a].at[4 * px + 2 * py + pc]

            def copy(k, block, to, src=None, a=a, slot=slot):
                return pltpu.make_async_remote_copy(
                    src_ref=slot(*block) if src is None else src, dst_ref=slot(*block),
                    send_sem=send.at[7 * a + k], recv_sem=recv.at[7 * a + k], device_id=to, device_id_type=MESH)

            if what == "mine":
                out.append(pltpu.make_async_copy(srcs[a], slot(x, y, c), local.at[a]))
            elif what == "first":
                out.append(copy(0, (x, y, c), (x, y, 1 - c), src=srcs[a]))
                out += [copy(1 + j, (x, y, c), (*chip, c), src=srcs[a]) for j, chip in enumerate(chips)]
            elif what == "landed":
                out += [copy(1 + j, (*chip, c), (x, y, c)) for j, chip in enumerate(chips)]
            elif what == "passed":
                out += [copy(4 + j, (*chip, c), (x, y, 1 - c)) for j, chip in enumerate(chips)]
            else:
                out.append(copy(0, (x, y, 1 - c), (x, y, c)))
                out += [copy(4 + j, (*chip, 1 - c), (x, y, c)) for j, chip in enumerate(chips)]
        return out

    def start(self, *refs):
        for cp in self._copies("first", *refs) + self._copies("mine", *refs):
            cp.start()

    def forward(self, *refs):
        for landed, passed in zip(self._copies("landed", *refs), self._copies("passed", *refs)):
            landed.wait_recv()
            passed.start()

    def finish(self, *refs):
        for cp in self._copies("from_sibling", *refs):
            cp.wait_recv()
        for cp in self._copies("first", *refs) + self._copies("passed", *refs):
            cp.wait_send()
        for cp in self._copies("mine", *refs):
            cp.wait()


class _Scatter:
    def __init__(self, arrays):
        self.n = len(arrays)
        self.out_shape = [jax.ShapeDtypeStruct((7, *a.shape[1:]), a.dtype) for a in arrays]
        self.scratch = [pltpu.SemaphoreType.DMA((7 * self.n,)), pltpu.SemaphoreType.DMA((7 * self.n,))]

    def _copies(self, srcs, dsts, send, recv):
        x, y, c = _place()
        out = []
        for a in range(self.n):
            for k, (fx, fy, fc) in enumerate(PEER_FLIPS):
                px, py, pc = _flip(x, fx), _flip(y, fy), _flip(c, fc)
                out.append(pltpu.make_async_remote_copy(
                    src_ref=srcs[a].at[4 * px + 2 * py + pc], dst_ref=dsts[a].at[k],
                    send_sem=send.at[7 * a + k], recv_sem=recv.at[7 * a + k], device_id=(px, py, pc), device_id_type=MESH))
        return out

    def start(self, *refs):
        for cp in self._copies(*refs):
            cp.start()

    def forward(self, *refs):
        pass

    def finish(self, *refs):
        for cp in self._copies(*refs):
            cp.wait_send()
        for cp in self._copies(*refs):
            cp.wait_recv()


def _run_comm(comm, arrays, name):
    n = comm.n

    def body(*refs):
        args = (refs[:n], refs[n:2 * n], *refs[2 * n:])
        comm.start(*args)
        comm.forward(*args)
        comm.finish(*args)

    return pl.pallas_call(body, name=name, out_shape=comm.out_shape, in_specs=[ANY] * n, out_specs=[ANY] * n,
                          scratch_shapes=comm.scratch)(*arrays)


def _rs_sibling(arrays, name):
    n = len(arrays)

    def body(*refs):
        srcs, got, (send, recv) = refs[:n], refs[n:2 * n], refs[2 * n:]
        x, y, c = _place()
        copies = []
        for a in range(n):
            for r, (fx, fy) in enumerate(CHIP_FLIPS):
                chip = 2 * _flip(x, fx) + _flip(y, fy)
                copies.append(pltpu.make_async_remote_copy(
                    src_ref=srcs[a].at[2 * chip + 1 - c], dst_ref=got[a].at[r], send_sem=send.at[4 * a + r],
                    recv_sem=recv.at[4 * a + r], device_id=(x, y, 1 - c), device_id_type=MESH))
        for cp in copies:
            cp.start()
        for cp in copies:
            cp.wait_send()
        for cp in copies:
            cp.wait_recv()

    return pl.pallas_call(
        body, name=name, out_shape=[jax.ShapeDtypeStruct((4, *a.shape[1:]), a.dtype) for a in arrays],
        in_specs=[ANY] * n, out_specs=[ANY] * n,
        scratch_shapes=[pltpu.SemaphoreType.DMA((4 * n,)), pltpu.SemaphoreType.DMA((4 * n,))],
    )(*arrays)


def _rs_chips(arrays, name):
    n = len(arrays)

    def body(*refs):
        srcs, dsts, (send, recv) = refs[:n], refs[n:2 * n], refs[2 * n:]
        x, y, c = _place()
        copies = []
        for a in range(n):
            for k, (fx, fy) in enumerate(CHIP_FLIPS[1:]):
                copies.append(pltpu.make_async_remote_copy(
                    src_ref=srcs[a].at[k], dst_ref=dsts[a].at[k], send_sem=send.at[3 * a + k], recv_sem=recv.at[3 * a + k],
                    device_id=(_flip(x, fx), _flip(y, fy), c), device_id_type=MESH))
        for cp in copies:
            cp.start()
        for cp in copies:
            cp.wait_send()
        for cp in copies:
            cp.wait_recv()

    return pl.pallas_call(
        body, name=name, out_shape=[jax.ShapeDtypeStruct(a.shape, a.dtype) for a in arrays], in_specs=[ANY] * n,
        out_specs=[ANY] * n, scratch_shapes=[pltpu.SemaphoreType.DMA((3 * n,)), pltpu.SemaphoreType.DMA((3 * n,))],
    )(*arrays)


def _row_tile(rows):
    return 256 if rows % 256 == 0 else rows


def _chip_slots():
    x, y, c = _place()
    return jnp.stack([4 * _flip(x, fx) + 2 * _flip(y, fy) + c for fx, fy in CHIP_FLIPS]).astype(I32)


def _pair_sum(full, theirs, name):
    _, rows, cols = theirs.shape
    tr = _row_tile(rows)

    def body(slots_ref, m0_ref, m1_ref, m2_ref, m3_ref, b_ref, own_ref, rest_ref):
        own_ref[...] = m0_ref[...].astype(F32) + b_ref[0].astype(F32)
        for k, m_ref in enumerate((m1_ref, m2_ref, m3_ref)):
            rest_ref[k] = (m_ref[...].astype(F32) + b_ref[k + 1].astype(F32)).astype(BF16)

    def mine(k):
        return pl.BlockSpec((None, tr, cols), lambda i, slots: (slots[k], i, 0))

    return pl.pallas_call(
        body, name=name,
        grid_spec=pltpu.PrefetchScalarGridSpec(
            num_scalar_prefetch=1, grid=(rows // tr,),
            in_specs=[mine(0), mine(1), mine(2), mine(3), pl.BlockSpec((4, tr, cols), lambda i, slots: (0, i, 0))],
            out_specs=(pl.BlockSpec((tr, cols), lambda i, slots: (i, 0)), pl.BlockSpec((3, tr, cols), lambda i, slots: (0, i, 0)))),
        out_shape=(jax.ShapeDtypeStruct((rows, cols), F32), jax.ShapeDtypeStruct((3, rows, cols), BF16)),
        compiler_params=_cp("parallel"),
    )(_chip_slots(), full, full, full, full, theirs)


def _head_lanes(width, h):
    lane = lax.broadcasted_iota(I32, (1, width), 1)
    if width == LANES:
        return (lane >= 64 * h) & (lane < 64 * h + 64)
    nope = (lane >= NOPE * h) & (lane < NOPE * h + NOPE)
    rope = (lane >= 2 * NOPE + ROPE * h) & (lane < 2 * NOPE + ROPE * h + ROPE)
    return nope | rope


def _dilated_bias_table(seq):
    t = min(ATTN_TILE, seq)
    nd = seq // t

    def body(o_ref):
        delta = pl.program_id(0) * t + lax.broadcasted_iota(I32, (t, t), 1) - lax.broadcasted_iota(I32, (t, t), 0)
        mult = ((delta <= 128).astype(I32) + (((delta & 3) == 0) & (delta <= 512)).astype(I32)
                + ((delta & 15) == 0).astype(I32))
        logm = jnp.where(mult == 3, math.log(3.0), jnp.where(mult == 2, math.log(2.0), 0.0))
        valid = (delta >= 0) & (mult > 0)
        dist = delta.astype(F32)
        for h in range(N_HEADS):
            o_ref[h] = jnp.where(valid, logm - 2.0 ** (-(h + 1)) * dist, NEG)

    return pl.pallas_call(
        body, name="dilated_bias_table", grid=(nd,), out_shape=jax.ShapeDtypeStruct((N_HEADS, nd, t, t), F32),
        out_specs=pl.BlockSpec((N_HEADS, None, t, t), lambda d: (0, d, 0, 0)),
        compiler_params=_cp("parallel"),
    )()


def _comm_hooks(comm, refs, n_in, n_out):
    if comm is None:
        return refs[:n_in], refs[n_in:n_in + n_out], refs[n_in + n_out:], None
    n = comm.n
    ins, srcs = refs[:n_in], refs[n_in:n_in + n]
    outs, dsts = refs[n_in + n:n_in + n + n_out], refs[n_in + n + n_out:n_in + 2 * n + n_out]
    rest = refs[n_in + 2 * n + n_out:]
    own = len(rest) - len(comm.scratch)
    return ins, outs, rest[:own], (srcs, dsts, *rest[own:])


def _attn_fwd(q, k, v, bias, *, batch, seq, width, col0, dilated, scale, name, comm=None, comm_arrays=()):
    t = min(ATTN_TILE, seq)
    nq = seq // t
    cq, ck, cv = col0
    pre = scale if dilated else 1.0
    steps = batch * N_PAIRS * nq

    def body(*refs):
        (q_ref, k_ref, v_ref, bias_ref), (o_ref, lse_ref), _, plan = _comm_hooks(comm, refs, 4, 2)
        i = pl.program_id(2)
        step_no = (pl.program_id(0) * N_PAIRS + pl.program_id(1)) * nq + i
        if plan:
            pl.when(step_no == 0)(lambda: comm.start(*plan))
            pl.when(step_no == (3 * steps) // 4)(lambda: comm.forward(*plan))
        q2 = q_ref[...] * pre if dilated else q_ref[...]
        qh = [jnp.where(_head_lanes(width, h), q2, jnp.zeros_like(q2)) for h in (0, 1)]
        vlane = [_head_lanes(LANES, h) for h in (0, 1)]
        top = lax.broadcasted_iota(I32, (LANES, t), 0) < HEAD_V
        causal = lax.broadcasted_iota(I32, (t, t), 0) <= lax.broadcasted_iota(I32, (t, t), 1)

        def step(j, carry, diagonal):
            m0, l0, m1, l1, acc = carry
            ks = pl.multiple_of(j * t, t)
            kj = k_ref[pl.ds(ks, t), :]
            vj = v_ref[pl.ds(ks, t), :]
            new, alphas, pv = [], [], []
            for h, (m, l) in enumerate(((m0, l0), (m1, l1))):
                s = _dot_nt(kj, qh[h])
                if dilated:
                    s = s + bias_ref[h, i - j]
                else:
                    s = s * scale
                    if diagonal:
                        s = jnp.where(causal, s, NEG)
                m_new = jnp.maximum(m, jnp.max(s, axis=0, keepdims=True))
                a = jnp.exp(m - m_new)
                p = jnp.exp(s - m_new)
                new += [m_new, a * l + jnp.sum(p, axis=0, keepdims=True)]
                alphas.append(a)
                pv.append(_dot_tn(jnp.where(vlane[h], vj, jnp.zeros_like(vj)), p.astype(BF16)))
            acc = jnp.where(top, alphas[0], alphas[1]) * acc + pv[0] + pv[1]
            return (*new, acc)

        row = jnp.full((1, t), NEG, F32)
        zero = jnp.zeros((1, t), F32)
        init = (row, zero, row, zero, jnp.zeros((LANES, t), F32))
        if dilated:
            carry = lax.fori_loop(0, i + 1, functools.partial(step, diagonal=False), init)
        else:
            carry = step(i, lax.fori_loop(0, i, functools.partial(step, diagonal=False), init), True)
        m0, l0, m1, l1, acc = carry
        o_ref[...] = jnp.transpose(acc * jnp.where(top, 1.0 / l0, 1.0 / l1)).astype(BF16)
        r = lax.broadcasted_iota(I32, (8, t), 0)
        lse_ref[...] = jnp.where(r == 0, m0 + jnp.log(l0), jnp.where(r == 1, m1 + jnp.log(l1), 0.0))
        if plan:
            pl.when(step_no == steps - 1)(lambda: comm.finish(*plan))

    bias_spec = (pl.BlockSpec((2, nq, t, t), lambda b, p, i: (p, 0, 0, 0)) if dilated
                 else pl.BlockSpec((None, 8, LANES), lambda b, p, i: (0, 0, 0)))
    n = comm.n if comm else 0
    return pl.pallas_call(
        body, name=name, grid=(batch, N_PAIRS, nq),
        out_shape=[jax.ShapeDtypeStruct((batch * seq, DIL_WIDTH), BF16), jax.ShapeDtypeStruct((batch * N_PAIRS, 8, seq), F32)]
        + (comm.out_shape if comm else []),
        in_specs=[pl.BlockSpec((t, width), lambda b, p, i: (b * nq + i, cq + p)),
                  pl.BlockSpec((seq, width), lambda b, p, i: (b, ck + p)),
                  pl.BlockSpec((seq, LANES), lambda b, p, i: (b, cv + p)),
                  bias_spec] + [ANY] * n,
        out_specs=[pl.BlockSpec((t, LANES), lambda b, p, i: (b * nq + i, p)),
                   pl.BlockSpec((None, 8, t), lambda b, p, i: (b * N_PAIRS + p, 0, i))] + [ANY] * n,
        scratch_shapes=comm.scratch if comm else [],
        compiler_params=_cp("arbitrary", "arbitrary", "arbitrary") if comm else _cp("parallel", "parallel", "arbitrary"),
    )(q, k, v, bias, *comm_arrays)


def _attn_bwd(q, k, v, o, do, lse, bias, *, batch, seq, width, col0, dilated, scale, name, comm=None, comm_arrays=()):
    t = min(ATTN_TILE, seq)
    nq = seq // t
    cq, ck, cv = col0
    pre = scale if dilated else 1.0
    steps = batch * N_PAIRS

    def body(*refs):
        ins, (dq_ref, dk_ref, dv_ref), (dq_acc, dk_acc, dv_acc, rowdot), plan = _comm_hooks(comm, refs, 7, 3)
        q_ref, k_ref, v_ref, o_ref, do_ref, lse_ref, bias_ref = ins
        step_no = pl.program_id(0) * N_PAIRS + pl.program_id(1)
        if plan:
            pl.when(step_no == 0)(lambda: comm.start(*plan))
        wlane = [_head_lanes(width, h) for h in (0, 1)]
        vlane = [_head_lanes(LANES, h) for h in (0, 1)]
        causal = lax.broadcasted_iota(I32, (t, t), 0) <= lax.broadcasted_iota(I32, (t, t), 1)
        prod = jnp.transpose(do_ref[...].astype(F32) * o_ref[...].astype(F32))
        rowdot[0:1, :] = jnp.sum(prod[0:HEAD_V], axis=0, keepdims=True)
        rowdot[1:2, :] = jnp.sum(prod[HEAD_V:], axis=0, keepdims=True)
        dq_acc[...] = jnp.zeros_like(dq_acc)

        def k_tile(j, _):
            ks = pl.multiple_of(j * t, t)
            kj = k_ref[pl.ds(ks, t), :]
            vj = v_ref[pl.ds(ks, t), :]
            kh = [jnp.where(wlane[h], kj, jnp.zeros_like(kj)) for h in (0, 1)]
            dk_acc[...] = jnp.zeros_like(dk_acc)
            dv_acc[...] = jnp.zeros_like(dv_acc)

            def q_tile(i, _, diagonal):
                qs = pl.multiple_of(i * t, t)
                qi = q_ref[pl.ds(qs, t), :] * pre if dilated else q_ref[pl.ds(qs, t), :]
                doi = do_ref[pl.ds(qs, t), :]
                dq_i = jnp.zeros((t, width), F32)
                for h in (0, 1):
                    qih = jnp.where(wlane[h], qi, jnp.zeros_like(qi))
                    doih = jnp.where(vlane[h], doi, jnp.zeros_like(doi))
                    s = _dot_nt(kj, qih)
                    if dilated:
                        s = s + bias_ref[h, i - j]
                    else:
                        s = s * scale
                        if diagonal:
                            s = jnp.where(causal, s, NEG)
                    p = jnp.exp(s - lse_ref[h:h + 1, pl.ds(qs, t)])
                    dp = _dot_nt(vj, doih)
                    ds = p * (dp - rowdot[h:h + 1, pl.ds(qs, t)])
                    ds = (ds if dilated else ds * scale).astype(BF16)
                    dv_acc[...] += _dot(p.astype(BF16), doih)
                    dk_acc[...] += _dot(ds, qih)
                    dq_i = dq_i + _dot_tn(ds, kh[h])
                dq_acc[pl.ds(qs, t), :] += dq_i
                return 0

            if dilated:
                lax.fori_loop(j, nq, functools.partial(q_tile, diagonal=False), 0)
            else:
                q_tile(j, 0, True)
                lax.fori_loop(j + 1, nq, functools.partial(q_tile, diagonal=False), 0)
            dk_ref[pl.ds(ks, t), :] = dk_acc[...].astype(BF16)
            dv_ref[pl.ds(ks, t), :] = dv_acc[...].astype(BF16)
            return 0

        lax.fori_loop(0, nq, k_tile, 0)
        dq_ref[...] = (dq_acc[...] * pre).astype(BF16)
        if plan:
            pl.when(step_no == steps - 1)(lambda: comm.finish(*plan))

    tokens = batch * seq
    bias_spec = (pl.BlockSpec((2, nq, t, t), lambda b, p: (p, 0, 0, 0)) if dilated
                 else pl.BlockSpec((None, 8, LANES), lambda b, p: (0, 0, 0)))
    n = comm.n if comm else 0
    return pl.pallas_call(
        body, name=name, grid=(batch, N_PAIRS),
        out_shape=[jax.ShapeDtypeStruct((tokens, N_PAIRS * width), BF16), jax.ShapeDtypeStruct((tokens, N_PAIRS * width), BF16),
                   jax.ShapeDtypeStruct((tokens, DIL_WIDTH), BF16)] + (comm.out_shape if comm else []),
        in_specs=[pl.BlockSpec((seq, width), lambda b, p: (b, cq + p)),
                  pl.BlockSpec((seq, width), lambda b, p: (b, ck + p)),
                  pl.BlockSpec((seq, LANES), lambda b, p: (b, cv + p)),
                  pl.BlockSpec((seq, LANES), lambda b, p: (b, p)),
                  pl.BlockSpec((seq, LANES), lambda b, p: (b, p)),
                  pl.BlockSpec((None, 8, seq), lambda b, p: (b * N_PAIRS + p, 0, 0)),
                  bias_spec] + [ANY] * n,
        out_specs=[pl.BlockSpec((seq, width), lambda b, p: (b, p)),
                   pl.BlockSpec((seq, width), lambda b, p: (b, p)),
                   pl.BlockSpec((seq, LANES), lambda b, p: (b, p))] + [ANY] * n,
        scratch_shapes=[pltpu.VMEM((seq, width), F32), pltpu.VMEM((t, width), F32), pltpu.VMEM((t, LANES), F32),
                        pltpu.VMEM((8, seq), F32)] + (comm.scratch if comm else []),
        compiler_params=_cp("arbitrary", "arbitrary") if comm else _cp("parallel", "parallel"),
    )(q, k, v, o, do, lse, bias, *comm_arrays)


def _rms(xf, g):
    r = lax.rsqrt(jnp.mean(xf * xf, axis=1, keepdims=True) + RMS_EPS)
    return xf * r * g, r


def _rms_bwd(dy, xf, r, g):
    gy = dy * g
    dx = r * gy - xf * (r * r * r) * jnp.mean(gy * xf, axis=1, keepdims=True)
    return dx, dy * xf * r


def _ln_bwd(dy, xhat, rstd, g):
    dxh = dy * g
    return rstd * (dxh - jnp.mean(dxh, axis=1, keepdims=True) - xhat * jnp.mean(dxh * xhat, axis=1, keepdims=True))


def _fwd_proj(xb, w_in_ext, w1, w2, wk_ext, wv, e128, g_q, g_kv, cext, sext, cs128, *, seq):
    tokens = xb.shape[0]
    tm = min(TOKEN_TILE, seq)
    ns = seq // tm

    def body(x_ref, win_ref, w1_ref, w2_ref, wk_ref, wv_ref, e_ref, gq_ref, gkv_ref, c_ref, s_ref, cs_ref,
             low_ref, gates_ref, qkvd_ref, qp_ref, kp_ref, vm_ref, qn_ref, kvn_ref):
        xt = x_ref[...]
        low = _dot(xt, win_ref[:, 0:LOW_W])
        low_ref[...] = low
        qkvd_ref[...] = _dot(xt, win_ref[:, LOW_W:LOW_W + 3 * DIL_WIDTH]).astype(BF16)
        gates_ref[...] = _dot(xt, win_ref[:, LOW_W + 3 * DIL_WIDTH:])
        qn = _rms(low[:, 0:Q_LORA], gq_ref[...])[0].astype(BF16)
        kvn = _rms(low[:, Q_LORA:Q_LORA + KV_LORA], gkv_ref[...])[0].astype(BF16)
        qn_ref[...] = qn
        kvn_ref[...] = kvn
        qp_ref[...] = (_dot(qn, w1_ref[...]) * c_ref[...] + _dot(qn, w2_ref[...]) * s_ref[...]).astype(BF16)
        kr = low[:, Q_LORA + KV_LORA:] * cs_ref[...]
        kr = kr + pltpu.roll(kr, LANES - ROPE, 1)
        lane = lax.broadcasted_iota(I32, kr.shape, 1)
        kr = jnp.where(lane < ROPE, kr, 0.0).astype(BF16)
        kp_ref[...] = (_dot(kvn, wk_ref[...]) + _dot(kr, e_ref[...])).astype(BF16)
        vm_ref[...] = _dot(kvn, wv_ref[...]).astype(BF16)

    n_gates = 2 * D_MODEL
    outs = [(LOW_W, F32), (n_gates, F32), (3 * DIL_WIDTH, BF16), (N_PAIRS * PAIR_W, BF16), (N_PAIRS * PAIR_W, BF16),
            (DIL_WIDTH, BF16), (Q_LORA, BF16), (KV_LORA, BF16)]
    return pl.pallas_call(
        body, name="fwd_proj", grid=(tokens // tm,),
        out_shape=tuple(jax.ShapeDtypeStruct((tokens, w), dt) for w, dt in outs),
        in_specs=[_rows(tm, D_MODEL), _full(w_in_ext.shape), _full(w1.shape), _full(w2.shape), _full(wk_ext.shape),
                  _full(wv.shape), _full(e128.shape), _full(g_q.shape), _full(g_kv.shape),
                  pl.BlockSpec((tm, N_PAIRS * PAIR_W), lambda i: (i % ns, 0)),
                  pl.BlockSpec((tm, N_PAIRS * PAIR_W), lambda i: (i % ns, 0)),
                  pl.BlockSpec((tm, LANES), lambda i: (i % ns, 0))],
        out_specs=tuple(_rows(tm, w) for w, _ in outs),
        compiler_params=_cp("parallel"),
    )(xb, w_in_ext, w1, w2, wk_ext, wv, e128, g_q, g_kv, cext, sext, cs128)


def _fwd_mix(o_a, o_b, gates, x, b_gate, w_oa, w_ob, w_out, ln_g, ln_b, *, seq):
    tokens = x.shape[0]
    tm = min(TOKEN_TILE, seq)

    def body(oa_ref, ob_ref, gt_ref, x_ref, bg_ref, woa_ref, wob_ref, wout_ref, g_ref, b_ref,
             h_ref, hb_ref, xhat_ref, rstd_ref, ya_ref, yb_ref, mix_ref):
        ya = _dot(oa_ref[...], woa_ref[...])
        yb = _dot(ob_ref[...], wob_ref[...])
        g0 = _sigmoid(gt_ref[:, 0:D_MODEL] + bg_ref[0:1, :])
        g1 = _sigmoid(gt_ref[:, D_MODEL:] + bg_ref[1:2, :])
        mix = (g0 * ya + g1 * yb).astype(BF16)
        z = ALPHA * x_ref[...] + _dot(mix, wout_ref[...])
        zc = z - jnp.mean(z, axis=1, keepdims=True)
        rstd = lax.rsqrt(jnp.mean(zc * zc, axis=1, keepdims=True) + LN_EPS)
        xhat = zc * rstd
        h = xhat * g_ref[...] + b_ref[...]
        h_ref[...] = h
        hb_ref[...] = h.astype(BF16)
        xhat_ref[...] = xhat
        rstd_ref[...] = jnp.broadcast_to(rstd, (tm, LANES))
        ya_ref[...] = ya.astype(BF16)
        yb_ref[...] = yb.astype(BF16)
        mix_ref[...] = mix

    outs = [(D_MODEL, F32), (D_MODEL, BF16), (D_MODEL, F32), (LANES, F32), (D_MODEL, BF16), (D_MODEL, BF16), (D_MODEL, BF16)]
    return pl.pallas_call(
        body, name="fwd_mix", grid=(tokens // tm,),
        out_shape=tuple(jax.ShapeDtypeStruct((tokens, w), dt) for w, dt in outs),
        in_specs=[_rows(tm, DIL_WIDTH), _rows(tm, DIL_WIDTH), _rows(tm, 2 * D_MODEL), _rows(tm, D_MODEL),
                  _full(b_gate.shape), _full(w_oa.shape), _full(w_ob.shape), _full(w_out.shape),
                  _full(ln_g.shape), _full(ln_b.shape)],
        out_specs=tuple(_rows(tm, w) for w, _ in outs),
        compiler_params=_cp("parallel"),
    )(o_a, o_b, gates, x, b_gate, w_oa, w_ob, w_out, ln_g, ln_b)


def _fwd_mlp(hb, h, target, w_ff1, w_ff2, ln_g, ln_b, *, seq):
    tokens = h.shape[0]
    tm = min(2 * TOKEN_TILE, seq)
    tf = FF_SHARD
    nf = N_DEV // FF_STEP

    def body(hb_ref, h_ref, tg_ref, w1_ref, w2_ref, g_ref, b_ref, u_ref, dz_ref, dzb_ref, stat_ref, acc):
        i, j = pl.program_id(0), pl.program_id(1)

        @pl.when((i == 0) & (j == 0))
        def _():
            stat_ref[...] = jnp.zeros_like(stat_ref)

        @pl.when(j == 0)
        def _():
            acc[...] = jnp.zeros_like(acc)

        for s in range(FF_STEP):
            u = _dot(hb_ref[...], w1_ref[s])
            u_ref[:, s * tf:(s + 1) * tf] = u.astype(BF16)
            a = jnp.square(jnp.maximum(u, 0.0)).astype(BF16)
            acc[...] += _dot(a, w2_ref[s * tf:(s + 1) * tf, :])

        @pl.when(j == nf - 1)
        def _():
            z = ALPHA * h_ref[...] + acc[...]
            zc = z - jnp.mean(z, axis=1, keepdims=True)
            rstd = lax.rsqrt(jnp.mean(zc * zc, axis=1, keepdims=True) + LN_EPS)
            xhat = zc * rstd
            err = xhat * g_ref[...] + b_ref[...] - tg_ref[...]
            dy = err * (1.0 / D_MODEL)
            dz = _ln_bwd(dy, xhat, rstd, g_ref[...])
            dz_ref[...] = dz
            dzb_ref[...] = dz.astype(BF16)
            stat_ref[0:1, :] += jnp.sum(dy * xhat, axis=0, keepdims=True)
            stat_ref[1:2, :] += jnp.sum(dy, axis=0, keepdims=True)
            stat_ref[2:3, :] += jnp.sum(jnp.sum(err * err, axis=1, keepdims=True), axis=0, keepdims=True) * (0.5 / D_MODEL)

    return pl.pallas_call(
        body, name="fwd_mlp", grid=(tokens // tm, nf),
        out_shape=(jax.ShapeDtypeStruct((tokens, D_FF), BF16), jax.ShapeDtypeStruct((tokens, D_MODEL), F32),
                   jax.ShapeDtypeStruct((tokens, D_MODEL), BF16), jax.ShapeDtypeStruct((8, D_MODEL), F32)),
        in_specs=[_rows(tm, D_MODEL), _rows(tm, D_MODEL), _rows(tm, D_MODEL),
                  pl.BlockSpec((FF_STEP, D_MODEL, tf), lambda i, j: (j, 0, 0)),
                  pl.BlockSpec((FF_STEP * tf, D_MODEL), lambda i, j: (j, 0)),
                  _full(ln_g.shape), _full(ln_b.shape)],
        out_specs=(pl.BlockSpec((tm, FF_STEP * tf), lambda i, j: (i, j)), _rows(tm, D_MODEL), _rows(tm, D_MODEL),
                   _full((8, D_MODEL))),
        scratch_shapes=[pltpu.VMEM((tm, D_MODEL), F32)],
        compiler_params=_cp("arbitrary", "arbitrary"),
    )(hb, h, target, w_ff1, w_ff2, ln_g, ln_b)


def _bwd_mlp(dz2, dz2b, u, xhat1, rstd1, w_ff1, w_ff2, ln_g, *, seq):
    tokens = dz2.shape[0]
    tm = min(2 * TOKEN_TILE, seq)
    tf = FF_SHARD
    nf = N_DEV // FF_STEP

    def body(dz_ref, dzb_ref, u_ref, xh_ref, rs_ref, w1_ref, w2_ref, g_ref, du_ref, dz1_ref, dz1b_ref, stat_ref, acc):
        i, j = pl.program_id(0), pl.program_id(1)

        @pl.when((i == 0) & (j == 0))
        def _():
            stat_ref[...] = jnp.zeros_like(stat_ref)

        @pl.when(j == 0)
        def _():
            acc[...] = jnp.zeros_like(acc)

        for s in range(FF_STEP):
            da = _dot_nt(dzb_ref[...], w2_ref[s * tf:(s + 1) * tf, :])
            du = (da * (2.0 * jnp.maximum(u_ref[:, s * tf:(s + 1) * tf].astype(F32), 0.0))).astype(BF16)
            du_ref[:, s * tf:(s + 1) * tf] = du
            acc[...] += _dot_nt(du, w1_ref[s])

        @pl.when(j == nf - 1)
        def _():
            dh = ALPHA * dz_ref[...] + acc[...]
            xhat = xh_ref[...]
            dz1 = _ln_bwd(dh, xhat, rs_ref[:, 0:1], g_ref[...])
            dz1_ref[...] = dz1
            dz1b_ref[...] = dz1.astype(BF16)
            stat_ref[0:1, :] += jnp.sum(dh * xhat, axis=0, keepdims=True)
            stat_ref[1:2, :] += jnp.sum(dh, axis=0, keepdims=True)

    return pl.pallas_call(
        body, name="bwd_mlp", grid=(tokens // tm, nf),
        out_shape=(jax.ShapeDtypeStruct((tokens, D_FF), BF16), jax.ShapeDtypeStruct((tokens, D_MODEL), F32),
                   jax.ShapeDtypeStruct((tokens, D_MODEL), BF16), jax.ShapeDtypeStruct((8, D_MODEL), F32)),
        in_specs=[_rows(tm, D_MODEL), _rows(tm, D_MODEL), pl.BlockSpec((tm, FF_STEP * tf), lambda i, j: (i, j)),
                  _rows(tm, D_MODEL), _rows(tm, LANES),
                  pl.BlockSpec((FF_STEP, D_MODEL, tf), lambda i, j: (j, 0, 0)),
                  pl.BlockSpec((FF_STEP * tf, D_MODEL), lambda i, j: (j, 0)),
                  _full(ln_g.shape)],
        out_specs=(pl.BlockSpec((tm, FF_STEP * tf), lambda i, j: (i, j)), _rows(tm, D_MODEL), _rows(tm, D_MODEL),
                   _full((8, D_MODEL))),
        scratch_shapes=[pltpu.VMEM((tm, D_MODEL), F32)],
        compiler_params=_cp("arbitrary", "arbitrary"),
    )(dz2, dz2b, u, xhat1, rstd1, w_ff1, w_ff2, ln_g)


def _bwd_mix(dz1b, gates, y_a, y_b, b_gate, w_oa, w_ob, w_out, *, seq):
    tokens = dz1b.shape[0]
    tm = min(TOKEN_TILE, seq)

    def body(dz_ref, gt_ref, ya_ref, yb_ref, bg_ref, woa_ref, wob_ref, wout_ref,
             dgt_ref, dya_ref, dyb_ref, doa_ref, dob_ref, stat_ref):
        @pl.when(pl.program_id(0) == 0)
        def _():
            stat_ref[...] = jnp.zeros_like(stat_ref)

        dmix = _dot_nt(dz_ref[...], wout_ref[...])
        for k, (y_ref, w_ref, dy_ref, do_ref) in enumerate(((ya_ref, woa_ref, dya_ref, doa_ref), (yb_ref, wob_ref, dyb_ref, dob_ref))):
            g = _sigmoid(gt_ref[:, k * D_MODEL:(k + 1) * D_MODEL] + bg_ref[k:k + 1, :])
            dgate = dmix * y_ref[...].astype(F32) * g * (1.0 - g)
            dgt_ref[:, k * D_MODEL:(k + 1) * D_MODEL] = dgate.astype(BF16)
            stat_ref[k:k + 1, :] += jnp.sum(dgate, axis=0, keepdims=True)
            dy = (dmix * g).astype(BF16)
            dy_ref[...] = dy
            do_ref[...] = _dot_nt(dy, w_ref[...]).astype(BF16)

    outs = [(2 * D_MODEL, BF16), (D_MODEL, BF16), (D_MODEL, BF16), (DIL_WIDTH, BF16), (DIL_WIDTH, BF16)]
    return pl.pallas_call(
        body, name="bwd_mix", grid=(tokens // tm,),
        out_shape=tuple(jax.ShapeDtypeStruct((tokens, w), dt) for w, dt in outs) + (jax.ShapeDtypeStruct((8, D_MODEL), F32),),
        in_specs=[_rows(tm, D_MODEL), _rows(tm, 2 * D_MODEL), _rows(tm, D_MODEL), _rows(tm, D_MODEL),
                  _full(b_gate.shape), _full(w_oa.shape), _full(w_ob.shape), _full(w_out.shape)],
        out_specs=tuple(_rows(tm, w) for w, _ in outs) + (_full((8, D_MODEL)),),
        compiler_params=_cp("arbitrary"),
    )(dz1b, gates, y_a, y_b, b_gate, w_oa, w_ob, w_out)


def _bwd_proj(dqp, dkp, dvm, dq_d, dk_d, dv_d, dgates, dz1, low, w_in_ext, w1, w2, wk_ext, wv, e128, g_q, g_kv, cext, sext, cs128, *, seq):
    tokens = dz1.shape[0]
    tm = min(TOKEN_TILE, seq)
    ns = seq // tm

    def body(dqp_ref, dkp_ref, dvm_ref, dqd_ref, dkd_ref, dvd_ref, dgt_ref, dz_ref, low_ref, win_ref, w1_ref, w2_ref, wk_ref,
             wv_ref, e_ref, gq_ref, gkv_ref, c_ref, s_ref, cs_ref, dx_ref, dproj_ref, da_ref, db_ref, stat_ref):
        @pl.when(pl.program_id(0) == 0)
        def _():
            stat_ref[...] = jnp.zeros_like(stat_ref)

        low = low_ref[...]
        dqp = dqp_ref[...].astype(F32)
        d_a = (dqp * c_ref[...]).astype(BF16)
        d_b = (dqp * s_ref[...]).astype(BF16)
        da_ref[...] = d_a
        db_ref[...] = d_b
        q_a = low[:, 0:Q_LORA]
        _, rq = _rms(q_a, gq_ref[...])
        dq_a, gq_terms = _rms_bwd(_dot_nt(d_a, w1_ref[...]) + _dot_nt(d_b, w2_ref[...]), q_a, rq, gq_ref[...])
        kv_a = low[:, Q_LORA:Q_LORA + KV_LORA]
        _, rkv = _rms(kv_a, gkv_ref[...])
        dkp = dkp_ref[...]
        dkv_a, gkv_terms = _rms_bwd(_dot_nt(dkp, wk_ref[...]) + _dot_nt(dvm_ref[...], wv_ref[...]), kv_a, rkv, gkv_ref[...])
        dkr = _dot_nt(dkp, e_ref[...])
        dkr = (dkr + pltpu.roll(dkr, ROPE, 1)) * cs_ref[...]
        stat_ref[0:1, 0:Q_LORA] += jnp.sum(gq_terms, axis=0, keepdims=True)
        stat_ref[1:2, 0:KV_LORA] += jnp.sum(gkv_terms, axis=0, keepdims=True)
        dproj_ref[:, 0:Q_LORA] = dq_a.astype(BF16)
        dproj_ref[:, Q_LORA:Q_LORA + KV_LORA] = dkv_a.astype(BF16)
        dproj_ref[:, Q_LORA + KV_LORA:LOW_W] = dkr.astype(BF16)
        dproj_ref[:, LOW_W:LOW_W + DIL_WIDTH] = dqd_ref[...]
        dproj_ref[:, LOW_W + DIL_WIDTH:LOW_W + 2 * DIL_WIDTH] = dkd_ref[...]
        dproj_ref[:, LOW_W + 2 * DIL_WIDTH:LOW_W + 3 * DIL_WIDTH] = dvd_ref[...]
        dproj_ref[:, LOW_W + 3 * DIL_WIDTH:] = dgt_ref[...]
        dx_ref[...] = ALPHA * dz_ref[...] + _dot_nt(dproj_ref[...], win_ref[...])

    wide = N_PAIRS * PAIR_W
    return pl.pallas_call(
        body, name="bwd_proj", grid=(tokens // tm,),
        out_shape=(jax.ShapeDtypeStruct((tokens, D_MODEL), F32), jax.ShapeDtypeStruct((tokens, IN_EXT), BF16),
                   jax.ShapeDtypeStruct((tokens, wide), BF16), jax.ShapeDtypeStruct((tokens, wide), BF16),
                   jax.ShapeDtypeStruct((8, D_MODEL), F32)),
        in_specs=[_rows(tm, wide), _rows(tm, wide), _rows(tm, DIL_WIDTH), _rows(tm, DIL_WIDTH), _rows(tm, DIL_WIDTH),
                  _rows(tm, DIL_WIDTH), _rows(tm, 2 * D_MODEL),
                  _rows(tm, D_MODEL), _rows(tm, LOW_W), _full(w_in_ext.shape), _full(w1.shape), _full(w2.shape),
                  _full(wk_ext.shape), _full(wv.shape), _full(e128.shape), _full(g_q.shape), _full(g_kv.shape),
                  pl.BlockSpec((tm, wide), lambda i: (i % ns, 0)), pl.BlockSpec((tm, wide), lambda i: (i % ns, 0)),
                  pl.BlockSpec((tm, LANES), lambda i: (i % ns, 0))],
        out_specs=(_rows(tm, D_MODEL), _rows(tm, IN_EXT), _rows(tm, wide), _rows(tm, wide), _full((8, D_MODEL))),
        compiler_params=_cp("arbitrary"),
    )(dqp, dkp, dvm, dq_d, dk_d, dv_d, dgates, dz1, low, w_in_ext, w1, w2, wk_ext, wv, e128, g_q, g_kv, cext, sext, cs128)


def _wgrad(a, b, name, square_relu=False, by_shard=False):
    tokens, ka = a.shape
    n = b.shape[1]
    tka = min(ka, 512)
    shard = n // N_DEV
    tn = WGRAD_SHARDS * shard if by_shard else max(w for w in range(LANES, min(n, 2304) + 1, LANES) if n % w == 0)
    tt = min(tokens, 512)
    nt = tokens // tt

    def body(a_ref, b_ref, o_ref, acc):
        kt = pl.program_id(2)

        @pl.when(kt == 0)
        def _():
            acc[...] = jnp.zeros_like(acc)

        at = a_ref[...]
        if square_relu:
            at = jnp.square(jnp.maximum(at.astype(F32), 0.0)).astype(BF16)
        acc[...] += _dot_tn(at, b_ref[...])

        @pl.when(kt == nt - 1)
        def _():
            if by_shard:
                for s in range(WGRAD_SHARDS):
                    o_ref[s] = acc[:, s * shard:(s + 1) * shard].astype(BF16)
            else:
                o_ref[...] = acc[...].astype(BF16)

    if by_shard:
        out_shape, out_spec = (N_DEV, ka, shard), pl.BlockSpec((WGRAD_SHARDS, tka, shard), lambda i, j, k: (j, i, 0))
    else:
        out_shape, out_spec = (ka, n), pl.BlockSpec((tka, tn), lambda i, j, k: (i, j))
    return pl.pallas_call(
        body, name=name, grid=(ka // tka, n // tn, nt), out_shape=jax.ShapeDtypeStruct(out_shape, BF16),
        in_specs=[pl.BlockSpec((tt, tka), lambda i, j, k: (k, i)), pl.BlockSpec((tt, tn), lambda i, j, k: (k, j))],
        out_specs=out_spec,
        scratch_shapes=[pltpu.VMEM((tka, tn), F32)],
        compiler_params=_cp("parallel", "parallel", "arbitrary"),
    )(a, b)


def _adam_math(w, g, m, v):
    m = ADAM_B1 * m + (1.0 - ADAM_B1) * g
    v = ADAM_B2 * v + (1.0 - ADAM_B2) * jnp.square(g)
    m_hat = m / (1.0 - ADAM_B1 ** ADAM_STEP)
    v_hat = v / (1.0 - ADAM_B2 ** ADAM_STEP)
    return -ADAM_LR * (m_hat / (jnp.sqrt(v_hat) + ADAM_EPS) + ADAM_WD * w), m, v


def _adamw(w, m, v, own, parts, name):
    rows, cols = w.shape
    tr = _row_tile(rows)
    n_parts = parts.shape[0]

    def body(slot_ref, w_ref, m_ref, v_ref, own_ref, p_ref, g_ref, d_ref, nm_ref, nv_ref):
        g = own_ref[...].astype(F32)
        for d in range(n_parts):
            g = g + p_ref[d].astype(F32)
        g_ref[...] = g
        d_ref[...], nm_ref[...], nv_ref[...] = _adam_math(w_ref[...], g, m_ref[...], v_ref[...])

    x, y, c = _place()
    blk = pl.BlockSpec((tr, cols), lambda i, slot: (i, 0))
    own_blk = blk if own.ndim == 2 else pl.BlockSpec((None, tr, cols), lambda i, slot: (slot[0], i, 0))
    return pl.pallas_call(
        body, name=name,
        grid_spec=pltpu.PrefetchScalarGridSpec(
            num_scalar_prefetch=1, grid=(rows // tr,),
            in_specs=[blk, blk, blk, own_blk, pl.BlockSpec((n_parts, tr, cols), lambda i, slot: (0, i, 0))],
            out_specs=(blk,) * 4),
        out_shape=(jax.ShapeDtypeStruct((rows, cols), F32),) * 4, compiler_params=_cp("parallel"),
    )(jnp.reshape(4 * x + 2 * y + c, (1,)).astype(I32), w, m, v, own, parts)


def _adamw_small(parts, w, m, v):
    _, rows, cols = parts.shape

    def body(p_ref, w_ref, m_ref, v_ref, g_ref, d_ref, nm_ref, nv_ref):
        g = p_ref[0]
        for d in range(1, N_DEV):
            g = g + p_ref[d]
        g_ref[...] = g
        d_ref[...], nm_ref[...], nv_ref[...] = _adam_math(w_ref[...], g, m_ref[...], v_ref[...])

    return pl.pallas_call(
        body, name="adamw_replicated", out_shape=(jax.ShapeDtypeStruct((rows, cols), F32),) * 4,
        in_specs=[_full(parts.shape)] + [_full((rows, cols))] * 3, out_specs=(_full((rows, cols)),) * 4, grid=(1,),
        compiler_params=_cp("arbitrary"),
    )(parts, w, m, v)


def _pad_rows(a2d, mult):
    pad = (-a2d.shape[-2]) % mult
    return jnp.pad(a2d, [(0, 0)] * (a2d.ndim - 2) + [(0, pad), (0, 0)]) if pad else a2d


def _pad_cols(a):
    pad = (-a.shape[-1]) % LANES
    return jnp.pad(a, [(0, 0)] * (a.ndim - 1) + [(0, pad)]) if pad else a


def _rot_cols(w):
    half = ROPE // 2
    return jnp.concatenate([-w[..., half:], w[..., :half]], axis=-1)


def _unrot_cols(dw):
    half = ROPE // 2
    return jnp.concatenate([dw[..., half:], -dw[..., :half]], axis=-1)


def _from_col_shards(stacked):
    return stacked.transpose(1, 0, 2).reshape(stacked.shape[1], -1)


def _to_col_shards(full):
    r = full.shape[0]
    return full.reshape(r, N_DEV, -1).transpose(1, 0, 2)


def _rope_tables(seq):
    half = ROPE // 2
    inv = jnp.power(ROPE_THETA, -jnp.arange(half, dtype=F32) / half)
    ang = jnp.arange(seq, dtype=F32)[:, None] * inv[None, :]
    cos = jnp.concatenate([jnp.cos(ang)] * 2, axis=1)
    sin = jnp.concatenate([jnp.sin(ang)] * 2, axis=1)
    ones, zeros = jnp.ones((seq, 2 * NOPE), F32), jnp.zeros((seq, 2 * NOPE), F32)
    pad = jnp.zeros((seq, PAIR_W - 2 * NOPE - 2 * ROPE), F32)
    cext = jnp.tile(jnp.concatenate([ones, cos, cos, pad], axis=1), (1, N_PAIRS))
    sext = jnp.tile(jnp.concatenate([zeros, sin, sin, pad], axis=1), (1, N_PAIRS))
    cs128 = jnp.concatenate([cos, sin, jnp.zeros((seq, LANES - 2 * ROPE), F32)], axis=1)
    return cext, sext, cs128


def _pair_slabs(nope, rope):
    k = nope.shape[0]
    nope = nope.reshape(k, N_PAIRS, 2 * NOPE)
    rope = jnp.zeros((k, N_PAIRS, 2 * ROPE), nope.dtype) if rope is None else rope.reshape(k, N_PAIRS, 2 * ROPE)
    pad = jnp.zeros((k, N_PAIRS, PAIR_W - 2 * NOPE - 2 * ROPE), nope.dtype)
    return jnp.concatenate([nope, rope, pad], axis=2).reshape(k, N_PAIRS * PAIR_W)


def _split_slabs(slabs):
    k = slabs.shape[0]
    s = slabs.reshape(k, N_PAIRS, PAIR_W)
    return s[:, :, :2 * NOPE].reshape(k, N_HEADS, NOPE), s[:, :, 2 * NOPE:2 * NOPE + 2 * ROPE].reshape(k, N_HEADS, ROPE)


def kernel(x, w_in, b_gate, g_q_a, w_uq, g_kv_a, w_ukv, w_o_mla, w_o_dil, w_out, ln1_g, ln1_b, w_ff1, w_ff2, ln2_g, ln2_b, loss_target, m_w_in, m_b_gate, m_g_q_a, m_w_uq, m_g_kv_a, m_w_ukv, m_w_o_mla, m_w_o_dil, m_w_out, m_ln1_g, m_ln1_b, m_w_ff1, m_w_ff2, m_ln2_g, m_ln2_b, v_w_in, v_b_gate, v_g_q_a, v_w_uq, v_g_kv_a, v_w_ukv, v_w_o_mla, v_w_o_dil, v_w_out, v_ln1_g, v_ln1_b, v_w_ff1, v_w_ff2, v_ln2_g, v_ln2_b):
    batch, seq, _ = x.shape
    tokens = batch * seq
    weights = dict(w_in=w_in, w_uq=w_uq, w_ukv=w_ukv, w_o_mla=w_o_mla, w_o_dil=w_o_dil, w_out=w_out, w_ff1=w_ff1, w_ff2=w_ff2, b_gate=b_gate)
    mom_m = dict(w_in=m_w_in, w_uq=m_w_uq, w_ukv=m_w_ukv, w_o_mla=m_w_o_mla, w_o_dil=m_w_o_dil, w_out=m_w_out, w_ff1=m_w_ff1, w_ff2=m_w_ff2, b_gate=m_b_gate)
    mom_v = dict(w_in=v_w_in, w_uq=v_w_uq, w_ukv=v_w_ukv, w_o_mla=v_w_o_mla, w_o_dil=v_w_o_dil, w_out=v_w_out, w_ff1=v_w_ff1, w_ff2=v_w_ff2, b_gate=v_b_gate)

    first = ["w_in", "w_uq", "w_ukv"]
    widths = [weights[n].shape[2] for n in first]
    shards = [_pad_cols(weights[n][0].astype(BF16)) for n in first]
    g_in, g_uq, g_ukv = (g[:, :, :w] for g, w in zip(_run_comm(_Gather(shards), shards, "all_gather_first_weights"), widths))

    wi = _from_col_shards(g_in)
    s0, s1, s2, s3 = Q_LORA, Q_LORA + KV_LORA, Q_LORA + KV_LORA + ROPE, Q_LORA + KV_LORA + ROPE + 3 * DIL_WIDTH
    w_kr = wi[:, s1:s2]
    w_in_ext = jnp.concatenate([wi[:, :s2], _rot_cols(w_kr), jnp.zeros((D_MODEL, LOW_W - s2 - ROPE), BF16), wi[:, s2:]], axis=1)
    uq = _from_col_shards(g_uq).reshape(Q_LORA, N_HEADS, NOPE + ROPE)
    w1 = _pair_slabs(uq[:, :, :NOPE], uq[:, :, NOPE:])
    w2 = _pair_slabs(jnp.zeros_like(uq[:, :, :NOPE]), _rot_cols(uq[:, :, NOPE:]))
    ukv = _from_col_shards(g_ukv).reshape(KV_LORA, N_HEADS, NOPE + HEAD_V)
    wk_ext = _pair_slabs(ukv[:, :, :NOPE], None)
    wv = ukv[:, :, NOPE:].reshape(KV_LORA, N_HEADS * HEAD_V)
    eye = jnp.eye(ROPE, dtype=BF16)
    e_slab = jnp.concatenate([jnp.zeros((ROPE, 2 * NOPE), BF16), eye, eye, jnp.zeros((ROPE, PAIR_W - 2 * NOPE - 2 * ROPE), BF16)], axis=1)
    e128 = jnp.concatenate([jnp.tile(e_slab, (1, N_PAIRS)), jnp.zeros((LANES - ROPE, N_PAIRS * PAIR_W), BF16)], axis=0)
    cext, sext, cs128 = _rope_tables(seq)
    dil_bias = _dilated_bias_table(seq)
    no_bias = jnp.zeros((1, 8, LANES), F32)

    x2 = x.reshape(tokens, D_MODEL)
    xb = x2.astype(BF16)
    low, gates, qkvd, qp, kp, vm, qn, kvn = _fwd_proj(xb, w_in_ext, w1, w2, wk_ext, wv, e128, g_q_a, g_kv_a, cext, sext, cs128, seq=seq)
    bg = b_gate[0]
    bg_hi = bg.astype(BF16)
    bg_lo = (bg - bg_hi.astype(F32)).astype(BF16)
    later = [weights[n][0].astype(BF16) for n in ("w_o_mla", "w_o_dil", "w_out", "w_ff1", "w_ff2")]
    later.append(_pad_rows(jnp.concatenate([bg_hi, bg_lo], axis=0), 16))
    mla = dict(batch=batch, seq=seq, width=PAIR_W, col0=(0, 0, 0), dilated=False, scale=MLA_SCALE)
    dil = dict(batch=batch, seq=seq, width=LANES, col0=(0, N_PAIRS, 2 * N_PAIRS), dilated=True, scale=DIL_SCALE)
    o_a, lse_a, g_oa, g_ob, g_out, g_ff1, g_ff2, g_bg = _attn_fwd(
        qp, kp, vm, no_bias, name="mla_attention_fwd", comm=_Gather(later), comm_arrays=later, **mla)
    o_b, lse_b = _attn_fwd(qkvd, qkvd, qkvd, dil_bias, name="dilated_attention_fwd", **dil)
    w_oa, w_ob = _from_col_shards(g_oa), _from_col_shards(g_ob)
    w_out_full = g_out.reshape(D_MODEL, D_MODEL)
    w_ff2_full = g_ff2.reshape(D_FF, D_MODEL)
    bg_parts = g_bg.astype(F32)
    b_gate_full = _from_col_shards(bg_parts[:, 0:2] + bg_parts[:, 2:4])
    h, hb, xhat1, rstd1, y_a, y_b, mix = _fwd_mix(o_a, o_b, gates, x2, b_gate_full, w_oa, w_ob, w_out_full, ln1_g, ln1_b, seq=seq)
    u, dz2, dz2b, stat2 = _fwd_mlp(hb, h, loss_target.reshape(tokens, D_MODEL), g_ff1, w_ff2_full, ln2_g, ln2_b, seq=seq)

    du, dz1, dz1b, stat1 = _bwd_mlp(dz2, dz2b, u, xhat1, rstd1, g_ff1, w_ff2_full, ln1_g, seq=seq)
    dw_ff = [_wgrad(hb, du, "wgrad_ff1", by_shard=True),
             _wgrad(u, dz2b, "wgrad_ff2", square_relu=True).reshape(N_DEV, FF_SHARD, D_MODEL)]
    dgates, dy_a, dy_b, do_a, do_b, stat_g = _bwd_mix(dz1b, gates, y_a, y_b, b_gate_full, w_oa, w_ob, w_out_full, seq=seq)
    dqp, dkp, dvm, r_ff1, r_ff2 = _attn_bwd(qp, kp, vm, o_a, do_a, lse_a, no_bias, name="mla_attention_bwd",
                                            comm=_Scatter(dw_ff), comm_arrays=dw_ff, **mla)
    dw_mid = [_to_col_shards(_wgrad(o_a, dy_a, "wgrad_o_mla")), _to_col_shards(_wgrad(o_b, dy_b, "wgrad_o_dil")),
              _wgrad(mix, dz1b, "wgrad_out").reshape(N_DEV, D_MODEL // N_DEV, D_MODEL),
              _pad_rows(_to_col_shards(stat_g[0:2]).astype(BF16), 16)]
    dq_d, dk_d, dv_d, r_oa, r_ob, r_out, r_bg = _attn_bwd(qkvd, qkvd, qkvd, o_b, do_b, lse_b, dil_bias, name="dilated_attention_bwd",
                                                          comm=_Scatter(dw_mid), comm_arrays=dw_mid, **dil)
    grad_x, dproj, d_a, d_b, stat_r = _bwd_proj(dqp, dkp, dvm, dq_d, dk_d, dv_d, dgates, dz1, low, w_in_ext, w1, w2, wk_ext, wv,
                                                e128, g_q_a, g_kv_a, cext, sext, cs128, seq=seq)

    dw_in_ext = _wgrad(xb, dproj, "wgrad_in")
    dw1 = _wgrad(qn, d_a, "wgrad_uq_direct")
    dw2 = _wgrad(qn, d_b, "wgrad_uq_rotated")
    dwk = _wgrad(kvn, dkp, "wgrad_ukv_k")
    dwv = _wgrad(kvn, dvm, "wgrad_ukv_v")
    dlow = dw_in_ext[:, :LOW_W]
    dw_kr = dlow[:, s1:s2] + _unrot_cols(dlow[:, s2:s2 + ROPE])
    dw_in = jnp.concatenate([dlow[:, :s1], dw_kr, dw_in_ext[:, LOW_W:]], axis=1)
    n1, r1 = _split_slabs(dw1)
    _, r2 = _split_slabs(dw2)
    dw_uq = jnp.concatenate([n1, r1 + _unrot_cols(r2)], axis=2).reshape(Q_LORA, N_HEADS * (NOPE + ROPE))
    nk, _ = _split_slabs(dwk)
    dw_ukv = jnp.concatenate([nk, dwv.reshape(KV_LORA, N_HEADS, HEAD_V)], axis=2).reshape(KV_LORA, N_HEADS * (NOPE + HEAD_V))
    last = [_pad_cols(_to_col_shards(dw)) for dw in (dw_in, dw_uq, dw_ukv)]
    theirs = _rs_sibling(last, "rs_last_sibling_exchange")
    sums = [_pair_sum(a, b, "rs_last_pair_sum_" + n) for a, b, n in zip(last, theirs, first)]
    got = _rs_chips([s[1] for s in sums], "rs_last_chip_exchange")

    upd = {}
    for n, w, (own, _), parts in zip(first, widths, sums, got):
        upd[n] = _adamw(weights[n][0], mom_m[n][0], mom_v[n][0], own[:, :w], parts[:, :, :w], "adamw_" + n)
    for n, own, parts in (("w_o_mla", dw_mid[0], r_oa), ("w_o_dil", dw_mid[1], r_ob), ("w_out", dw_mid[2], r_out),
                          ("w_ff1", dw_ff[0], r_ff1), ("w_ff2", dw_ff[1], r_ff2)):
        upd[n] = _adamw(weights[n][0], mom_m[n][0], mom_v[n][0], own, parts, "adamw_" + n)
    bg_upd = _adamw(_pad_rows(b_gate[0], 16), _pad_rows(m_b_gate[0], 16), _pad_rows(v_b_gate[0], 16), dw_mid[3], r_bg, "adamw_b_gate")
    upd["b_gate"] = tuple(t[0:2] for t in bg_upd)

    small_w = [g_q_a, g_kv_a, ln1_g, ln1_b, ln2_g, ln2_b]
    small_m = [m_g_q_a, m_g_kv_a, m_ln1_g, m_ln1_b, m_ln2_g, m_ln2_b]
    small_v = [v_g_q_a, v_g_kv_a, v_ln1_g, v_ln1_b, v_ln2_g, v_ln2_b]
    small_widths = [a.shape[1] for a in small_w]
    partial = jnp.concatenate([stat_r[0:1, :Q_LORA], stat_r[1:2, :KV_LORA], stat1[0:1], stat1[1:2], stat2[0:1], stat2[1:2],
                               stat2[2:3, :LANES]], axis=1)

    def as_rows(vecs, extra):
        flat = jnp.concatenate(vecs + [jnp.zeros((1, extra), F32)], axis=1)
        return _pad_rows(flat.reshape(-1, LANES), 8)

    partial = _pad_rows(partial.reshape(-1, LANES), 8)
    (every,) = _run_comm(_Gather([partial]), [partial], "all_gather_replicated_grads")
    g_s, d_s, nm_s, nv_s = _adamw_small(every, as_rows(small_w, LANES), as_rows(small_m, LANES), as_rows(small_v, LANES))

    def split_small(a):
        flat = a.reshape(1, -1)
        out, c0 = [], 0
        for w in small_widths:
            out.append(flat[:, c0:c0 + w])
            c0 += w
        return out, flat[0, c0]

    g_small, loss = split_small(g_s)
    small = [g_small, split_small(d_s)[0], split_small(nm_s)[0], split_small(nv_s)[0]]

    order = ["w_in", "b_gate", "g_q_a", "w_uq", "g_kv_a", "w_ukv", "w_o_mla", "w_o_dil", "w_out", "ln1_g", "ln1_b", "w_ff1", "w_ff2", "ln2_g", "ln2_b"]
    small_names = ["g_q_a", "g_kv_a", "ln1_g", "ln1_b", "ln2_g", "ln2_b"]

    def pick(kind):
        return [small[kind][small_names.index(n)] if n in small_names else upd[n][kind][None] for n in order]

    return (loss, grad_x.reshape(batch, seq, D_MODEL), *pick(0), *pick(1), *pick(2), *pick(3))
```

```python
import functools
import math

import jax
import jax.numpy as jnp
from jax import lax
from jax.experimental import pallas as pl
from jax.experimental.pallas import tpu as pltpu

F32 = jnp.float32
BF16 = jnp.bfloat16
I32 = jnp.int32

D_MODEL = 1024
N_HEADS = 8
NOPE = 64
ROPE = 32
HEAD_V = 64
Q_LORA = 384
KV_LORA = 256
DIL_WIDTH = 512
D_FF = 4096
ROPE_THETA = 10000.0
LN_EPS = 1e-5
RMS_EPS = 1e-6
NEG = -1e30
ALPHA = 2.0 ** 0.25
MLA_SCALE = (NOPE + ROPE) ** -0.5
DIL_SCALE = 64 ** -0.5
ADAM_LR, ADAM_B1, ADAM_B2, ADAM_EPS, ADAM_WD, ADAM_STEP = 0.001, 0.9, 0.999, 1e-08, 0.01, 10

LANES = 128
PAIR_W = 256
N_PAIRS = N_HEADS // 2
LOW_W = 768
IN_EXT = LOW_W + 3 * DIL_WIDTH + 2 * D_MODEL
N_DEV = 8
FF_SHARD = D_FF // N_DEV
FF_STEP = 2
WGRAD_SHARDS = 4
TOKEN_TILE = 256
ATTN_TILE = 256
VMEM_LIMIT = 56 << 20

MESH = pl.DeviceIdType.MESH
ANY = pl.BlockSpec(memory_space=pl.ANY)
CHIP_FLIPS = ((0, 0), (0, 1), (1, 0), (1, 1))
PEER_FLIPS = tuple((fx, fy, fc) for fx in (0, 1) for fy in (0, 1) for fc in (0, 1))[1:]


def _cp(*sem):
    return pltpu.CompilerParams(dimension_semantics=sem or None, vmem_limit_bytes=VMEM_LIMIT)


def _full(shape):
    nd = len(shape)
    return pl.BlockSpec(shape, lambda *_: (0,) * nd)


def _rows(tm, width):
    return pl.BlockSpec((tm, width), lambda i, *_: (i, 0))


def _dot(a, b):
    return jnp.dot(a, b, preferred_element_type=F32)


def _dot_nt(a, b):
    return lax.dot_general(a, b, (((1,), (1,)), ((), ())), preferred_element_type=F32)


def _dot_tn(a, b):
    return lax.dot_general(a, b, (((0,), (0,)), ((), ())), preferred_element_type=F32)


def _sigmoid(z):
    return 1.0 / (1.0 + jnp.exp(-z))


def _place():
    return lax.axis_index("x"), lax.axis_index("y"), lax.axis_index("c")


def _flip(v, f):
    return 1 - v if f else v


class _Gather:
    def __init__(self, shards):
        self.n = len(shards)
        self.out_shape = [jax.ShapeDtypeStruct((N_DEV, *s.shape), s.dtype) for s in shards]
        self.scratch = [pltpu.SemaphoreType.DMA((7 * self.n,)), pltpu.SemaphoreType.DMA((7 * self.n,)),
                        pltpu.SemaphoreType.DMA((self.n,))]

    def _copies(self, what, srcs, dsts, send, recv, local):
        x, y, c = _place()
        chips = [(_flip(x, fx), _flip(y, fy)) for fx, fy in CHIP_FLIPS[1:]]
        out = []
        for a in range(self.n):
            def slot(px, py, pc, a=a):
                return dsts[a].at[4 * px + 2 * py + pc]

            def copy(k, block, to, src=None, a=a, slot=slot):
                return pltpu.make_async_remote_copy(
                    src_ref=slot(*block) if src is None else src, dst_ref=slot(*block),
                    send_sem=send.at[7 * a + k], recv_sem=recv.at[7 * a + k], device_id=to, device_id_type=MESH)

            if what == "mine":
                out.append(pltpu.make_async_copy(srcs[a], slot(x, y, c), local.at[a]))
            elif what == "first":
                out.append(copy(0, (x, y, c), (x, y, 1 - c), src=srcs[a]))
                out += [copy(1 + j, (x, y, c), (*chip, c), src=srcs[a]) for j, chip in enumerate(chips)]
            elif what == "landed":
                out += [copy(1 + j, (*chip, c), (x, y, c)) for j, chip in enumerate(chips)]
            elif what == "passed":
                out += [copy(4 + j, (*chip, c), (x, y, 1 - c)) for j, chip in enumerate(chips)]
            else:
                out.append(copy(0, (x, y, 1 - c), (x, y, c)))
                out += [copy(4 + j, (*chip, 1 - c), (x, y, c)) for j, chip in enumerate(chips)]
        return out

    def start(self, *refs):
        for cp in self._copies("first", *refs) + self._copies("mine", *refs):
            cp.start()

    def forward(self, *refs):
        for landed, passed in zip(self._copies("landed", *refs), self._copies("passed", *refs)):
            landed.wait_recv()
            passed.start()

    def finish(self, *refs):
        for cp in self._copies("from_sibling", *refs):
            cp.wait_recv()
        for cp in self._copies("first", *refs) + self._copies("passed", *refs):
            cp.wait_send()
        for cp in self._copies("mine", *refs):
            cp.wait()


class _Scatter:
    def __init__(self, arrays):
        self.n = len(arrays)
        self.out_shape = [jax.ShapeDtypeStruct((7, *a.shape[1:]), a.dtype) for a in arrays]
        self.scratch = [pltpu.SemaphoreType.DMA((7 * self.n,)), pltpu.SemaphoreType.DMA((7 * self.n,))]

    def _copies(self, srcs, dsts, send, recv):
        x, y, c = _place()
        out = []
        for a in range(self.n):
            for k, (fx, fy, fc) in enumerate(PEER_FLIPS):
                px, py, pc = _flip(x, fx), _flip(y, fy), _flip(c, fc)
                out.append(pltpu.make_async_remote_copy(
                    src_ref=srcs[a].at[4 * px + 2 * py + pc], dst_ref=dsts[a].at[k],
                    send_sem=send.at[7 * a + k], recv_sem=recv.at[7 * a + k], device_id=(px, py, pc), device_id_type=MESH))
        return out

    def start(self, *refs):
        for cp in self._copies(*refs):
            cp.start()

    def forward(self, *refs):
        pass

    def finish(self, *refs):
        for cp in self._copies(*refs):
            cp.wait_send()
        for cp in self._copies(*refs):
            cp.wait_recv()


def _run_comm(comm, arrays, name):
    n = comm.n

    def body(*refs):
        args = (refs[:n], refs[n:2 * n], *refs[2 * n:])
        comm.start(*args)
        comm.forward(*args)
        comm.finish(*args)

    return pl.pallas_call(body, name=name, out_shape=comm.out_shape, in_specs=[ANY] * n, out_specs=[ANY] * n,
                          scratch_shapes=comm.scratch)(*arrays)


def _rs_sibling(arrays, name):
    n = len(arrays)

    def body(*refs):
        srcs, got, (send, recv) = refs[:n], refs[n:2 * n], refs[2 * n:]
        x, y, c = _place()
        copies = []
        for a in range(n):
            for r, (fx, fy) in enumerate(CHIP_FLIPS):
                chip = 2 * _flip(x, fx) + _flip(y, fy)
                copies.append(pltpu.make_async_remote_copy(
                    src_ref=srcs[a].at[2 * chip + 1 - c], dst_ref=got[a].at[r], send_sem=send.at[4 * a + r],
                    recv_sem=recv.at[4 * a + r], device_id=(x, y, 1 - c), device_id_type=MESH))
        for cp in copies:
            cp.start()
        for cp in copies:
            cp.wait_send()
        for cp in copies:
            cp.wait_recv()

    return pl.pallas_call(
        body, name=name, out_shape=[jax.ShapeDtypeStruct((4, *a.shape[1:]), a.dtype) for a in arrays],
        in_specs=[ANY] * n, out_specs=[ANY] * n,
        scratch_shapes=[pltpu.SemaphoreType.DMA((4 * n,)), pltpu.SemaphoreType.DMA((4 * n,))],
    )(*arrays)


def _rs_chips(arrays, name):
    n = len(arrays)

    def body(*refs):
        srcs, dsts, (send, recv) = refs[:n], refs[n:2 * n], refs[2 * n:]
        x, y, c = _place()
        copies = []
        for a in range(n):
            for k, (fx, fy) in enumerate(CHIP_FLIPS[1:]):
                copies.append(pltpu.make_async_remote_copy(
                    src_ref=srcs[a].at[k], dst_ref=dsts[a].at[k], send_sem=send.at[3 * a + k], recv_sem=recv.at[3 * a + k],
                    device_id=(_flip(x, fx), _flip(y, fy), c), device_id_type=MESH))
        for cp in copies:
            cp.start()
        for cp in copies:
            cp.wait_send()
        for cp in copies:
            cp.wait_recv()

    return pl.pallas_call(
        body, name=name, out_shape=[jax.ShapeDtypeStruct(a.shape, a.dtype) for a in arrays], in_specs=[ANY] * n,
        out_specs=[ANY] * n, scratch_shapes=[pltpu.SemaphoreType.DMA((3 * n,)), pltpu.SemaphoreType.DMA((3 * n,))],
    )(*arrays)


def _row_tile(rows):
    return 256 if rows % 256 == 0 else rows


def _chip_slots():
    x, y, c = _place()
    return jnp.stack([4 * _flip(x, fx) + 2 * _flip(y, fy) + c for fx, fy in CHIP_FLIPS]).astype(I32)


def _pair_sum(full, theirs, name):
    _, rows, cols = theirs.shape
    tr = _row_tile(rows)

    def body(slots_ref, m0_ref, m1_ref, m2_ref, m3_ref, b_ref, own_ref, rest_ref):
        own_ref[...] = m0_ref[...].astype(F32) + b_ref[0].astype(F32)
        for k, m_ref in enumerate((m1_ref, m2_ref, m3_ref)):
            rest_ref[k] = (m_ref[...].astype(F32) + b_ref[k + 1].astype(F32)).astype(BF16)

    def mine(k):
        return pl.BlockSpec((None, tr, cols), lambda i, slots: (slots[k], i, 0))

    return pl.pallas_call(
        body, name=name,
        grid_spec=pltpu.PrefetchScalarGridSpec(
            num_scalar_prefetch=1, grid=(rows // tr,),
            in_specs=[mine(0), mine(1), mine(2), mine(3), pl.BlockSpec((4, tr, cols), lambda i, slots: (0, i, 0))],
            out_specs=(pl.BlockSpec((tr, cols), lambda i, slots: (i, 0)), pl.BlockSpec((3, tr, cols), lambda i, slots: (0, i, 0)))),
        out_shape=(jax.ShapeDtypeStruct((rows, cols), F32), jax.ShapeDtypeStruct((3, rows, cols), BF16)),
        compiler_params=_cp("parallel"),
    )(_chip_slots(), full, full, full, full, theirs)


def _head_lanes(width, h):
    lane = lax.broadcasted_iota(I32, (1, width), 1)
    if width == LANES:
        return (lane >= 64 * h) & (lane < 64 * h + 64)
    nope = (lane >= NOPE * h) & (lane < NOPE * h + NOPE)
    rope = (lane >= 2 * NOPE + ROPE * h) & (lane < 2 * NOPE + ROPE * h + ROPE)
    return nope | rope


def _dilated_bias_table(seq):
    t = min(ATTN_TILE, seq)
    nd = seq // t

    def body(o_ref):
        delta = pl.program_id(0) * t + lax.broadcasted_iota(I32, (t, t), 1) - lax.broadcasted_iota(I32, (t, t), 0)
        mult = ((delta <= 128).astype(I32) + (((delta & 3) == 0) & (delta <= 512)).astype(I32)
                + ((delta & 15) == 0).astype(I32))
        logm = jnp.where(mult == 3, math.log(3.0), jnp.where(mult == 2, math.log(2.0), 0.0))
        valid = (delta >= 0) & (mult > 0)
        dist = delta.astype(F32)
        for h in range(N_HEADS):
            o_ref[h] = jnp.where(valid, logm - 2.0 ** (-(h + 1)) * dist, NEG)

    return pl.pallas_call(
        body, name="dilated_bias_table", grid=(nd,), out_shape=jax.ShapeDtypeStruct((N_HEADS, nd, t, t), F32),
        out_specs=pl.BlockSpec((N_HEADS, None, t, t), lambda d: (0, d, 0, 0)),
        compiler_params=_cp("parallel"),
    )()


def _comm_hooks(comm, refs, n_in, n_out):
    if comm is None:
        return refs[:n_in], refs[n_in:n_in + n_out], refs[n_in + n_out:], None
    n = comm.n
    ins, srcs = refs[:n_in], refs[n_in:n_in + n]
    outs, dsts = refs[n_in + n:n_in + n + n_out], refs[n_in + n + n_out:n_in + 2 * n + n_out]
    rest = refs[n_in + 2 * n + n_out:]
    own = len(rest) - len(comm.scratch)
    return ins, outs, rest[:own], (srcs, dsts, *rest[own:])


def _attn_fwd(q, k, v, bias, *, batch, seq, width, col0, dilated, scale, name, comm=None, comm_arrays=()):
    t = min(ATTN_TILE, seq)
    nq = seq // t
    cq, ck, cv = col0
    pre = scale if dilated else 1.0
    steps = batch * N_PAIRS * nq

    def body(*refs):
        (q_ref, k_ref, v_ref, bias_ref), (o_ref, lse_ref), _, plan = _comm_hooks(comm, refs, 4, 2)
        i = pl.program_id(2)
        step_no = (pl.program_id(0) * N_PAIRS + pl.program_id(1)) * nq + i
        if plan:
            pl.when(step_no == 0)(lambda: comm.start(*plan))
            pl.when(step_no == (3 * steps) // 4)(lambda: comm.forward(*plan))
        q2 = q_ref[...] * pre if dilated else q_ref[...]
        qh = [jnp.where(_head_lanes(width, h), q2, jnp.zeros_like(q2)) for h in (0, 1)]
        vlane = [_head_lanes(LANES, h) for h in (0, 1)]
        top = lax.broadcasted_iota(I32, (LANES, t), 0) < HEAD_V
        causal = lax.broadcasted_iota(I32, (t, t), 0) <= lax.broadcasted_iota(I32, (t, t), 1)

        def scores(j):
            kj = k_ref[pl.ds(pl.multiple_of(j * t, t), t), :]
            return [_dot_nt(kj, qh[h]) for h in (0, 1)]

        def step(j, carry, last):
            m0, l0, m1, l1, acc, s0, s1 = carry
            ahead = [] if last else scores(j + 1)
            vj = v_ref[pl.ds(pl.multiple_of(j * t, t), t), :]
            new, alphas, pv = [], [], []
            for h, (m, l, s) in enumerate(((m0, l0, s0), (m1, l1, s1))):
                if dilated:
                    s = s + bias_ref[h, i - j]
                else:
                    s = s * scale
                    if last:
                        s = jnp.where(causal, s, NEG)
                m_new = jnp.maximum(m, jnp.max(s, axis=0, keepdims=True))
                a = jnp.exp(m - m_new)
                p = jnp.exp(s - m_new)
                new += [m_new, a * l + jnp.sum(p, axis=0, keepdims=True)]
                alphas.append(a)
                pv.append(_dot_tn(jnp.where(vlane[h], vj, jnp.zeros_like(vj)), p.astype(BF16)))
            acc = jnp.where(top, alphas[0], alphas[1]) * acc + pv[0] + pv[1]
            return (*new, acc, *ahead)

        row = jnp.full((1, t), NEG, F32)
        zero = jnp.zeros((1, t), F32)
        init = (row, zero, row, zero, jnp.zeros((LANES, t), F32), *scores(0))
        m0, l0, m1, l1, acc = step(i, lax.fori_loop(0, i, functools.partial(step, last=False), init), True)
        o_ref[...] = jnp.transpose(acc * jnp.where(top, 1.0 / l0, 1.0 / l1)).astype(BF16)
        r = lax.broadcasted_iota(I32, (8, t), 0)
        lse_ref[...] = jnp.where(r == 0, m0 + jnp.log(l0), jnp.where(r == 1, m1 + jnp.log(l1), 0.0))
        if plan:
            pl.when(step_no == steps - 1)(lambda: comm.finish(*plan))

    bias_spec = (pl.BlockSpec((2, nq, t, t), lambda b, p, i: (p, 0, 0, 0)) if dilated
                 else pl.BlockSpec((None, 8, LANES), lambda b, p, i: (0, 0, 0)))
    n = comm.n if comm else 0
    return pl.pallas_call(
        body, name=name, grid=(batch, N_PAIRS, nq),
        out_shape=[jax.ShapeDtypeStruct((batch * seq, DIL_WIDTH), BF16), jax.ShapeDtypeStruct((batch * N_PAIRS, 8, seq), F32)]
        + (comm.out_shape if comm else []),
        in_specs=[pl.BlockSpec((t, width), lambda b, p, i: (b * nq + i, cq + p)),
                  pl.BlockSpec((seq, width), lambda b, p, i: (b, ck + p)),
                  pl.BlockSpec((seq, LANES), lambda b, p, i: (b, cv + p)),
                  bias_spec] + [ANY] * n,
        out_specs=[pl.BlockSpec((t, LANES), lambda b, p, i: (b * nq + i, p)),
                   pl.BlockSpec((None, 8, t), lambda b, p, i: (b * N_PAIRS + p, 0, i))] + [ANY] * n,
        scratch_shapes=comm.scratch if comm else [],
        compiler_params=_cp("arbitrary", "arbitrary", "arbitrary") if comm else _cp("parallel", "parallel", "arbitrary"),
    )(q, k, v, bias, *comm_arrays)


def _attn_bwd(q, k, v, o, do, lse, bias, *, batch, seq, width, col0, dilated, scale, name, comm=None, comm_arrays=()):
    t = min(ATTN_TILE, seq)
    nq = seq // t
    cq, ck, cv = col0
    pre = scale if dilated else 1.0
    steps = batch * N_PAIRS

    def body(*refs):
        ins, (dq_ref, dk_ref, dv_ref), (dq_acc, dk_acc, dv_acc, rowdot), plan = _comm_hooks(comm, refs, 7, 3)
        q_ref, k_ref, v_ref, o_ref, do_ref, lse_ref, bias_ref = ins
        step_no = pl.program_id(0) * N_PAIRS + pl.program_id(1)
        if plan:
            pl.when(step_no == 0)(lambda: comm.start(*plan))
        wlane = [_head_lanes(width, h) for h in (0, 1)]
        vlane = [_head_lanes(LANES, h) for h in (0, 1)]
        causal = lax.broadcasted_iota(I32, (t, t), 0) <= lax.broadcasted_iota(I32, (t, t), 1)
        prod = jnp.transpose(do_ref[...].astype(F32) * o_ref[...].astype(F32))
        rowdot[0:1, :] = jnp.sum(prod[0:HEAD_V], axis=0, keepdims=True)
        rowdot[1:2, :] = jnp.sum(prod[HEAD_V:], axis=0, keepdims=True)
        dq_acc[...] = jnp.zeros_like(dq_acc)

        def k_tile(j, _):
            ks = pl.multiple_of(j * t, t)
            kj = k_ref[pl.ds(ks, t), :]
            vj = v_ref[pl.ds(ks, t), :]
            kh = [jnp.where(wlane[h], kj, jnp.zeros_like(kj)) for h in (0, 1)]
            dk_acc[...] = jnp.zeros_like(dk_acc)
            dv_acc[...] = jnp.zeros_like(dv_acc)

            def operands(i):
                qs = pl.multiple_of(i * t, t)
                qi = q_ref[pl.ds(qs, t), :] * pre if dilated else q_ref[pl.ds(qs, t), :]
                doi = do_ref[pl.ds(qs, t), :]
                return ([jnp.where(wlane[h], qi, jnp.zeros_like(qi)) for h in (0, 1)],
                        [jnp.where(vlane[h], doi, jnp.zeros_like(doi)) for h in (0, 1)])

            def products(i):
                qih, doih = operands(i)
                return tuple(_dot_nt(kj, qih[h]) for h in (0, 1)) + tuple(_dot_nt(vj, doih[h]) for h in (0, 1))

            def q_tile(n, carry, last):
                i = nq - 1 - n
                s0, s1, dp0, dp1 = carry
                ahead = () if last else products(i - 1)
                qs = pl.multiple_of(i * t, t)
                qih, doih = operands(i)
                dq_i = jnp.zeros((t, width), F32)
                for h, (s, dp) in enumerate(((s0, dp0), (s1, dp1))):
                    if dilated:
                        s = s + bias_ref[h, i - j]
                    else:
                        s = s * scale
                        if last:
                            s = jnp.where(causal, s, NEG)
                    p = jnp.exp(s - lse_ref[h:h + 1, pl.ds(qs, t)])
                    ds = p * (dp - rowdot[h:h + 1, pl.ds(qs, t)])
                    ds = (ds if dilated else ds * scale).astype(BF16)
                    dv_acc[...] += _dot(p.astype(BF16), doih[h])
                    dk_acc[...] += _dot(ds, qih[h])
                    dq_i = dq_i + _dot_tn(ds, kh[h])
                dq_acc[pl.ds(qs, t), :] += dq_i
                return ahead

            q_tile(nq - 1 - j, lax.fori_loop(0, nq - 1 - j, functools.partial(q_tile, last=False), products(nq - 1)), True)
            dk_ref[pl.ds(ks, t), :] = dk_acc[...].astype(BF16)
            dv_ref[pl.ds(ks, t), :] = dv_acc[...].astype(BF16)
            return 0

        lax.fori_loop(0, nq, k_tile, 0)
        dq_ref[...] = (dq_acc[...] * pre).astype(BF16)
        if plan:
            pl.when(step_no == steps - 1)(lambda: comm.finish(*plan))

    tokens = batch * seq
    bias_spec = (pl.BlockSpec((2, nq, t, t), lambda b, p: (p, 0, 0, 0)) if dilated
                 else pl.BlockSpec((None, 8, LANES), lambda b, p: (0, 0, 0)))
    n = comm.n if comm else 0
    return pl.pallas_call(
        body, name=name, grid=(batch, N_PAIRS),
        out_shape=[jax.ShapeDtypeStruct((tokens, N_PAIRS * width), BF16), jax.ShapeDtypeStruct((tokens, N_PAIRS * width), BF16),
                   jax.ShapeDtypeStruct((tokens, DIL_WIDTH), BF16)] + (comm.out_shape if comm else []),
        in_specs=[pl.BlockSpec((seq, width), lambda b, p: (b, cq + p)),
                  pl.BlockSpec((seq, width), lambda b, p: (b, ck + p)),
                  pl.BlockSpec((seq, LANES), lambda b, p: (b, cv + p)),
                  pl.BlockSpec((seq, LANES), lambda b, p: (b, p)),
                  pl.BlockSpec((seq, LANES), lambda b, p: (b, p)),
                  pl.BlockSpec((None, 8, seq), lambda b, p: (b * N_PAIRS + p, 0, 0)),
                  bias_spec] + [ANY] * n,
        out_specs=[pl.BlockSpec((seq, width), lambda b, p: (b, p)),
                   pl.BlockSpec((seq, width), lambda b, p: (b, p)),
                   pl.BlockSpec((seq, LANES), lambda b, p: (b, p))] + [ANY] * n,
        scratch_shapes=[pltpu.VMEM((seq, width), F32), pltpu.VMEM((t, width), F32), pltpu.VMEM((t, LANES), F32),
                        pltpu.VMEM((8, seq), F32)] + (comm.scratch if comm else []),
        compiler_params=_cp("arbitrary", "arbitrary") if comm else _cp("parallel", "parallel"),
    )(q, k, v, o, do, lse, bias, *comm_arrays)


def _rms(xf, g):
    r = lax.rsqrt(jnp.mean(xf * xf, axis=1, keepdims=True) + RMS_EPS)
    return xf * r * g, r


def _rms_bwd(dy, xf, r, g):
    gy = dy * g
    dx = r * gy - xf * (r * r * r) * jnp.mean(gy * xf, axis=1, keepdims=True)
    return dx, dy * xf * r


def _ln_bwd(dy, xhat, rstd, g):
    dxh = dy * g
    return rstd * (dxh - jnp.mean(dxh, axis=1, keepdims=True) - xhat * jnp.mean(dxh * xhat, axis=1, keepdims=True))


def _fwd_proj(xb, w_in_ext, w1, w2, wk_ext, wv, e128, g_q, g_kv, cext, sext, cs128, *, seq):
    tokens = xb.shape[0]
    tm = min(TOKEN_TILE, seq)
    ns = seq // tm

    def body(x_ref, win_ref, w1_ref, w2_ref, wk_ref, wv_ref, e_ref, gq_ref, gkv_ref, c_ref, s_ref, cs_ref,
             low_ref, gates_ref, qkvd_ref, qp_ref, kp_ref, vm_ref, qn_ref, kvn_ref):
        xt = x_ref[...]
        low = _dot(xt, win_ref[:, 0:LOW_W])
        low_ref[...] = low
        qkvd_ref[...] = _dot(xt, win_ref[:, LOW_W:LOW_W + 3 * DIL_WIDTH]).astype(BF16)
        gates_ref[...] = _dot(xt, win_ref[:, LOW_W + 3 * DIL_WIDTH:])
        qn = _rms(low[:, 0:Q_LORA], gq_ref[...])[0].astype(BF16)
        kvn = _rms(low[:, Q_LORA:Q_LORA + KV_LORA], gkv_ref[...])[0].astype(BF16)
        qn_ref[...] = qn
        kvn_ref[...] = kvn
        cos, sin = (jnp.concatenate([r[...]] * N_PAIRS, axis=1) for r in (c_ref, s_ref))
        qp_ref[...] = (_dot(qn, w1_ref[...]) * cos + _dot(qn, w2_ref[...]) * sin).astype(BF16)
        kr = low[:, Q_LORA + KV_LORA:] * cs_ref[...]
        kr = kr + pltpu.roll(kr, LANES - ROPE, 1)
        lane = lax.broadcasted_iota(I32, kr.shape, 1)
        kr = jnp.where(lane < ROPE, kr, 0.0).astype(BF16)
        kp_ref[...] = (_dot(kvn, wk_ref[...]) + _dot(kr, e_ref[...])).astype(BF16)
        vm_ref[...] = _dot(kvn, wv_ref[...]).astype(BF16)

    n_gates = 2 * D_MODEL
    outs = [(LOW_W, F32), (n_gates, F32), (3 * DIL_WIDTH, BF16), (N_PAIRS * PAIR_W, BF16), (N_PAIRS * PAIR_W, BF16),
            (DIL_WIDTH, BF16), (Q_LORA, BF16), (KV_LORA, BF16)]
    return pl.pallas_call(
        body, name="fwd_proj", grid=(tokens // tm,),
        out_shape=tuple(jax.ShapeDtypeStruct((tokens, w), dt) for w, dt in outs),
        in_specs=[_rows(tm, D_MODEL), _full(w_in_ext.shape), _full(w1.shape), _full(w2.shape), _full(wk_ext.shape),
                  _full(wv.shape), _full(e128.shape), _full(g_q.shape), _full(g_kv.shape),
                  pl.BlockSpec((tm, PAIR_W), lambda i: (i % ns, 0)),
                  pl.BlockSpec((tm, PAIR_W), lambda i: (i % ns, 0)),
                  pl.BlockSpec((tm, LANES), lambda i: (i % ns, 0))],
        out_specs=tuple(_rows(tm, w) for w, _ in outs),
        compiler_params=_cp("parallel"),
    )(xb, w_in_ext, w1, w2, wk_ext, wv, e128, g_q, g_kv, cext, sext, cs128)


def _fwd_mix(o_a, o_b, gates, x, b_gate, w_oa, w_ob, w_out, ln_g, ln_b, *, seq):
    tokens = x.shape[0]
    tm = min(TOKEN_TILE, seq)

    def body(oa_ref, ob_ref, gt_ref, x_ref, bg_ref, woa_ref, wob_ref, wout_ref, g_ref, b_ref,
             h_ref, hb_ref, xhat_ref, rstd_ref, ya_ref, yb_ref, mix_ref):
        ya = _dot(oa_ref[...], woa_ref[...])
        yb = _dot(ob_ref[...], wob_ref[...])
        g0 = _sigmoid(gt_ref[:, 0:D_MODEL] + bg_ref[0:1, :])
        g1 = _sigmoid(gt_ref[:, D_MODEL:] + bg_ref[1:2, :])
        mix = (g0 * ya + g1 * yb).astype(BF16)
        z = ALPHA * x_ref[...] + _dot(mix, wout_ref[...])
        zc = z - jnp.mean(z, axis=1, keepdims=True)
        rstd = lax.rsqrt(jnp.mean(zc * zc, axis=1, keepdims=True) + LN_EPS)
        xhat = zc * rstd
        h = xhat * g_ref[...] + b_ref[...]
        h_ref[...] = h
        hb_ref[...] = h.astype(BF16)
        xhat_ref[...] = xhat
        rstd_ref[...] = jnp.broadcast_to(rstd, (tm, LANES))
        ya_ref[...] = ya.astype(BF16)
        yb_ref[...] = yb.astype(BF16)
        mix_ref[...] = mix

    outs = [(D_MODEL, F32), (D_MODEL, BF16), (D_MODEL, F32), (LANES, F32), (D_MODEL, BF16), (D_MODEL, BF16), (D_MODEL, BF16)]
    return pl.pallas_call(
        body, name="fwd_mix", grid=(tokens // tm,),
        out_shape=tuple(jax.ShapeDtypeStruct((tokens, w), dt) for w, dt in outs),
        in_specs=[_rows(tm, DIL_WIDTH), _rows(tm, DIL_WIDTH), _rows(tm, 2 * D_MODEL), _rows(tm, D_MODEL),
                  _full(b_gate.shape), _full(w_oa.shape), _full(w_ob.shape), _full(w_out.shape),
                  _full(ln_g.shape), _full(ln_b.shape)],
        out_specs=tuple(_rows(tm, w) for w, _ in outs),
        compiler_params=_cp("parallel"),
    )(o_a, o_b, gates, x, b_gate, w_oa, w_ob, w_out, ln_g, ln_b)


def _fwd_mlp(hb, h, target, w_ff1, w_ff2, ln_g, ln_b, *, seq):
    tokens = h.shape[0]
    tm = min(2 * TOKEN_TILE, seq)
    tf = FF_SHARD
    nf = N_DEV // FF_STEP

    def body(hb_ref, h_ref, tg_ref, w1_ref, w2_ref, g_ref, b_ref, u_ref, dz_ref, dzb_ref, stat_ref, acc):
        i, j = pl.program_id(0), pl.program_id(1)

        @pl.when((i == 0) & (j == 0))
        def _():
            stat_ref[...] = jnp.zeros_like(stat_ref)

        @pl.when(j == 0)
        def _():
            acc[...] = jnp.zeros_like(acc)

        for s in range(FF_STEP):
            u = _dot(hb_ref[...], w1_ref[s])
            u_ref[:, s * tf:(s + 1) * tf] = u.astype(BF16)
            a = jnp.square(jnp.maximum(u, 0.0)).astype(BF16)
            acc[...] += _dot(a, w2_ref[s * tf:(s + 1) * tf, :])

        @pl.when(j == nf - 1)
        def _():
            z = ALPHA * h_ref[...] + acc[...]
            zc = z - jnp.mean(z, axis=1, keepdims=True)
            rstd = lax.rsqrt(jnp.mean(zc * zc, axis=1, keepdims=True) + LN_EPS)
            xhat = zc * rstd
            err = xhat * g_ref[...] + b_ref[...] - tg_ref[...]
            dy = err * (1.0 / D_MODEL)
            dz = _ln_bwd(dy, xhat, rstd, g_ref[...])
            dz_ref[...] = dz
            dzb_ref[...] = dz.astype(BF16)
            stat_ref[0:1, :] += jnp.sum(dy * xhat, axis=0, keepdims=True)
            stat_ref[1:2, :] += jnp.sum(dy, axis=0, keepdims=True)
            stat_ref[2:3, :] += jnp.sum(jnp.sum(err * err, axis=1, keepdims=True), axis=0, keepdims=True) * (0.5 / D_MODEL)

    return pl.pallas_call(
        body, name="fwd_mlp", grid=(tokens // tm, nf),
        out_shape=(jax.ShapeDtypeStruct((tokens, D_FF), BF16), jax.ShapeDtypeStruct((tokens, D_MODEL), F32),
                   jax.ShapeDtypeStruct((tokens, D_MODEL), BF16), jax.ShapeDtypeStruct((8, D_MODEL), F32)),
        in_specs=[_rows(tm, D_MODEL), _rows(tm, D_MODEL), _rows(tm, D_MODEL),
                  pl.BlockSpec((FF_STEP, D_MODEL, tf), lambda i, j: (j, 0, 0)),
                  pl.BlockSpec((FF_STEP * tf, D_MODEL), lambda i, j: (j, 0)),
                  _full(ln_g.shape), _full(ln_b.shape)],
        out_specs=(pl.BlockSpec((tm, FF_STEP * tf), lambda i, j: (i, j)), _rows(tm, D_MODEL), _rows(tm, D_MODEL),
                   _full((8, D_MODEL))),
        scratch_shapes=[pltpu.VMEM((tm, D_MODEL), F32)],
        compiler_params=_cp("arbitrary", "arbitrary"),
    )(hb, h, target, w_ff1, w_ff2, ln_g, ln_b)


def _bwd_mlp(dz2, dz2b, u, xhat1, rstd1, w_ff1, w_ff2, ln_g, *, seq):
    tokens = dz2.shape[0]
    tm = min(2 * TOKEN_TILE, seq)
    tf = FF_SHARD
    nf = N_DEV // FF_STEP

    def body(dz_ref, dzb_ref, u_ref, xh_ref, rs_ref, w1_ref, w2_ref, g_ref, du_ref, dz1_ref, dz1b_ref, stat_ref, acc):
        i, j = pl.program_id(0), pl.program_id(1)

        @pl.when((i == 0) & (j == 0))
        def _():
            stat_ref[...] = jnp.zeros_like(stat_ref)

        @pl.when(j == 0)
        def _():
            acc[...] = jnp.zeros_like(acc)

        for s in range(FF_STEP):
            da = _dot_nt(dzb_ref[...], w2_ref[s * tf:(s + 1) * tf, :])
            du = (da * (2.0 * jnp.maximum(u_ref[:, s * tf:(s + 1) * tf].astype(F32), 0.0))).astype(BF16)
            du_ref[:, s * tf:(s + 1) * tf] = du
            acc[...] += _dot_nt(du, w1_ref[s])

        @pl.when(j == nf - 1)
        def _():
            dh = ALPHA * dz_ref[...] + acc[...]
            xhat = xh_ref[...]
            dz1 = _ln_bwd(dh, xhat, rs_ref[:, 0:1], g_ref[...])
            dz1_ref[...] = dz1
            dz1b_ref[...] = dz1.astype(BF16)
            stat_ref[0:1, :] += jnp.sum(dh * xhat, axis=0, keepdims=True)
            stat_ref[1:2, :] += jnp.sum(dh, axis=0, keepdims=True)

    return pl.pallas_call(
        body, name="bwd_mlp", grid=(tokens // tm, nf),
        out_shape=(jax.ShapeDtypeStruct((tokens, D_FF), BF16), jax.ShapeDtypeStruct((tokens, D_MODEL), F32),
                   jax.ShapeDtypeStruct((tokens, D_MODEL), BF16), jax.ShapeDtypeStruct((8, D_MODEL), F32)),
        in_specs=[_rows(tm, D_MODEL), _rows(tm, D_MODEL), pl.BlockSpec((tm, FF_STEP * tf), lambda i, j: (i, j)),
                  _rows(tm, D_MODEL), _rows(tm, LANES),
                  pl.BlockSpec((FF_STEP, D_MODEL, tf), lambda i, j: (j, 0, 0)),
                  pl.BlockSpec((FF_STEP * tf, D_MODEL), lambda i, j: (j, 0)),
                  _full(ln_g.shape)],
        out_specs=(pl.BlockSpec((tm, FF_STEP * tf), lambda i, j: (i, j)), _rows(tm, D_MODEL), _rows(tm, D_MODEL),
                   _full((8, D_MODEL))),
        scratch_shapes=[pltpu.VMEM((tm, D_MODEL), F32)],
        compiler_params=_cp("arbitrary", "arbitrary"),
    )(dz2, dz2b, u, xhat1, rstd1, w_ff1, w_ff2, ln_g)


def _bwd_mix(dz1b, gates, y_a, y_b, b_gate, w_oa, w_ob, w_out, *, seq):
    tokens = dz1b.shape[0]
    tm = min(TOKEN_TILE, seq)

    def body(dz_ref, gt_ref, ya_ref, yb_ref, bg_ref, woa_ref, wob_ref, wout_ref,
             dgt_ref, dya_ref, dyb_ref, doa_ref, dob_ref, stat_ref):
        @pl.when(pl.program_id(0) == 0)
        def _():
            stat_ref[...] = jnp.zeros_like(stat_ref)

        dmix = _dot_nt(dz_ref[...], wout_ref[...])
        for k, (y_ref, w_ref, dy_ref, do_ref) in enumerate(((ya_ref, woa_ref, dya_ref, doa_ref), (yb_ref, wob_ref, dyb_ref, dob_ref))):
            g = _sigmoid(gt_ref[:, k * D_MODEL:(k + 1) * D_MODEL] + bg_ref[k:k + 1, :])
            dgate = dmix * y_ref[...].astype(F32) * g * (1.0 - g)
            dgt_ref[:, k * D_MODEL:(k + 1) * D_MODEL] = dgate.astype(BF16)
            stat_ref[k:k + 1, :] += jnp.sum(dgate, axis=0, keepdims=True)
            dy = (dmix * g).astype(BF16)
            dy_ref[...] = dy
            do_ref[...] = _dot_nt(dy, w_ref[...]).astype(BF16)

    outs = [(2 * D_MODEL, BF16), (D_MODEL, BF16), (D_MODEL, BF16), (DIL_WIDTH, BF16), (DIL_WIDTH, BF16)]
    return pl.pallas_call(
        body, name="bwd_mix", grid=(tokens // tm,),
        out_shape=tuple(jax.ShapeDtypeStruct((tokens, w), dt) for w, dt in outs) + (jax.ShapeDtypeStruct((8, D_MODEL), F32),),
        in_specs=[_rows(tm, D_MODEL), _rows(tm, 2 * D_MODEL), _rows(tm, D_MODEL), _rows(tm, D_MODEL),
                  _full(b_gate.shape), _full(w_oa.shape), _full(w_ob.shape), _full(w_out.shape)],
        out_specs=tuple(_rows(tm, w) for w, _ in outs) + (_full((8, D_MODEL)),),
        compiler_params=_cp("arbitrary"),
    )(dz1b, gates, y_a, y_b, b_gate, w_oa, w_ob, w_out)


def _bwd_proj(dqp, dkp, dvm, dq_d, dk_d, dv_d, dgates, dz1, low, w_in_ext, w1, w2, wk_ext, wv, e128, g_q, g_kv, cext, sext, cs128, *, seq):
    tokens = dz1.shape[0]
    tm = min(TOKEN_TILE, seq)
    ns = seq // tm

    def body(dqp_ref, dkp_ref, dvm_ref, dqd_ref, dkd_ref, dvd_ref, dgt_ref, dz_ref, low_ref, win_ref, w1_ref, w2_ref, wk_ref,
             wv_ref, e_ref, gq_ref, gkv_ref, c_ref, s_ref, cs_ref, dx_ref, dproj_ref, da_ref, db_ref, stat_ref):
        @pl.when(pl.program_id(0) == 0)
        def _():
            stat_ref[...] = jnp.zeros_like(stat_ref)

        low = low_ref[...]
        dqp = dqp_ref[...].astype(F32)
        cos, sin = (jnp.concatenate([r[...]] * N_PAIRS, axis=1) for r in (c_ref, s_ref))
        d_a = (dqp * cos).astype(BF16)
        d_b = (dqp * sin).astype(BF16)
        da_ref[...] = d_a
        db_ref[...] = d_b
        q_a = low[:, 0:Q_LORA]
        _, rq = _rms(q_a, gq_ref[...])
        dq_a, gq_terms = _rms_bwd(_dot_nt(d_a, w1_ref[...]) + _dot_nt(d_b, w2_ref[...]), q_a, rq, gq_ref[...])
        kv_a = low[:, Q_LORA:Q_LORA + KV_LORA]
        _, rkv = _rms(kv_a, gkv_ref[...])
        dkp = dkp_ref[...]
        dkv_a, gkv_terms = _rms_bwd(_dot_nt(dkp, wk_ref[...]) + _dot_nt(dvm_ref[...], wv_ref[...]), kv_a, rkv, gkv_ref[...])
        dkr = _dot_nt(dkp, e_ref[...])
        dkr = (dkr + pltpu.roll(dkr, ROPE, 1)) * cs_ref[...]
        stat_ref[0:1, 0:Q_LORA] += jnp.sum(gq_terms, axis=0, keepdims=True)
        stat_ref[1:2, 0:KV_LORA] += jnp.sum(gkv_terms, axis=0, keepdims=True)
        dproj_ref[:, 0:Q_LORA] = dq_a.astype(BF16)
        dproj_ref[:, Q_LORA:Q_LORA + KV_LORA] = dkv_a.astype(BF16)
        dproj_ref[:, Q_LORA + KV_LORA:LOW_W] = dkr.astype(BF16)
        dproj_ref[:, LOW_W:LOW_W + DIL_WIDTH] = dqd_ref[...]
        dproj_ref[:, LOW_W + DIL_WIDTH:LOW_W + 2 * DIL_WIDTH] = dkd_ref[...]
        dproj_ref[:, LOW_W + 2 * DIL_WIDTH:LOW_W + 3 * DIL_WIDTH] = dvd_ref[...]
        dproj_ref[:, LOW_W + 3 * DIL_WIDTH:] = dgt_ref[...]
        dx_ref[...] = ALPHA * dz_ref[...] + _dot_nt(dproj_ref[...], win_ref[...])

    wide = N_PAIRS * PAIR_W
    return pl.pallas_call(
        body, name="bwd_proj", grid=(tokens // tm,),
        out_shape=(jax.ShapeDtypeStruct((tokens, D_MODEL), F32), jax.ShapeDtypeStruct((tokens, IN_EXT), BF16),
                   jax.ShapeDtypeStruct((tokens, wide), BF16), jax.ShapeDtypeStruct((tokens, wide), BF16),
                   jax.ShapeDtypeStruct((8, D_MODEL), F32)),
        in_specs=[_rows(tm, wide), _rows(tm, wide), _rows(tm, DIL_WIDTH), _rows(tm, DIL_WIDTH), _rows(tm, DIL_WIDTH),
                  _rows(tm, DIL_WIDTH), _rows(tm, 2 * D_MODEL),
                  _rows(tm, D_MODEL), _rows(tm, LOW_W), _full(w_in_ext.shape), _full(w1.shape), _full(w2.shape),
                  _full(wk_ext.shape), _full(wv.shape), _full(e128.shape), _full(g_q.shape), _full(g_kv.shape),
                  pl.BlockSpec((tm, PAIR_W), lambda i: (i % ns, 0)), pl.BlockSpec((tm, PAIR_W), lambda i: (i % ns, 0)),
                  pl.BlockSpec((tm, LANES), lambda i: (i % ns, 0))],
        out_specs=(_rows(tm, D_MODEL), _rows(tm, IN_EXT), _rows(tm, wide), _rows(tm, wide), _full((8, D_MODEL))),
        compiler_params=_cp("arbitrary"),
    )(dqp, dkp, dvm, dq_d, dk_d, dv_d, dgates, dz1, low, w_in_ext, w1, w2, wk_ext, wv, e128, g_q, g_kv, cext, sext, cs128)


def _wgrad(a, b, name, square_relu=False, by_shard=False):
    tokens, ka = a.shape
    n = b.shape[1]
    tka = min(ka, 512)
    shard = n // N_DEV
    tn = WGRAD_SHARDS * shard if by_shard else max(w for w in range(LANES, min(n, 2304) + 1, LANES) if n % w == 0)
    tt = min(tokens, 512)
    nt = tokens // tt

    def body(a_ref, b_ref, o_ref, acc):
        kt = pl.program_id(2)

        @pl.when(kt == 0)
        def _():
            acc[...] = jnp.zeros_like(acc)

        at = a_ref[...]
        if square_relu:
            at = jnp.square(jnp.maximum(at.astype(F32), 0.0)).astype(BF16)
        acc[...] += _dot_tn(at, b_ref[...])

        @pl.when(kt == nt - 1)
        def _():
            if by_shard:
                for s in range(WGRAD_SHARDS):
                    o_ref[s] = acc[:, s * shard:(s + 1) * shard].astype(BF16)
            else:
                o_ref[...] = acc[...].astype(BF16)

    if by_shard:
        out_shape, out_spec = (N_DEV, ka, shard), pl.BlockSpec((WGRAD_SHARDS, tka, shard), lambda i, j, k: (j, i, 0))
    else:
        out_shape, out_spec = (ka, n), pl.BlockSpec((tka, tn), lambda i, j, k: (i, j))
    return pl.pallas_call(
        body, name=name, grid=(ka // tka, n // tn, nt), out_shape=jax.ShapeDtypeStruct(out_shape, BF16),
        in_specs=[pl.BlockSpec((tt, tka), lambda i, j, k: (k, i)), pl.BlockSpec((tt, tn), lambda i, j, k: (k, j))],
        out_specs=out_spec,
        scratch_shapes=[pltpu.VMEM((tka, tn), F32)],
        compiler_params=_cp("parallel", "parallel", "arbitrary"),
    )(a, b)


def _adam_math(w, g, m, v):
    m = ADAM_B1 * m + (1.0 - ADAM_B1) * g
    v = ADAM_B2 * v + (1.0 - ADAM_B2) * jnp.square(g)
    m_hat = m / (1.0 - ADAM_B1 ** ADAM_STEP)
    v_hat = v / (1.0 - ADAM_B2 ** ADAM_STEP)
    return -ADAM_LR * (m_hat / (jnp.sqrt(v_hat) + ADAM_EPS) + ADAM_WD * w), m, v


def _adamw(w, m, v, own, parts, name):
    rows, cols = w.shape
    tr = _row_tile(rows)
    n_parts = parts.shape[0]

    def body(slot_ref, w_ref, m_ref, v_ref, own_ref, p_ref, g_ref, d_ref, nm_ref, nv_ref):
        g = own_ref[...].astype(F32)
        for d in range(n_parts):
            g = g + p_ref[d].astype(F32)
        g_ref[...] = g
        d_ref[...], nm_ref[...], nv_ref[...] = _adam_math(w_ref[...], g, m_ref[...], v_ref[...])

    x, y, c = _place()
    blk = pl.BlockSpec((tr, cols), lambda i, slot: (i, 0))
    own_blk = blk if own.ndim == 2 else pl.BlockSpec((None, tr, cols), lambda i, slot: (slot[0], i, 0))
    return pl.pallas_call(
        body, name=name,
        grid_spec=pltpu.PrefetchScalarGridSpec(
            num_scalar_prefetch=1, grid=(rows // tr,),
            in_specs=[blk, blk, blk, own_blk, pl.BlockSpec((n_parts, tr, cols), lambda i, slot: (0, i, 0))],
            out_specs=(blk,) * 4),
        out_shape=(jax.ShapeDtypeStruct((rows, cols), F32),) * 4, compiler_params=_cp("parallel"),
    )(jnp.reshape(4 * x + 2 * y + c, (1,)).astype(I32), w, m, v, own, parts)


def _adamw_small(parts, w, m, v):
    _, rows, cols = parts.shape

    def body(p_ref, w_ref, m_ref, v_ref, g_ref, d_ref, nm_ref, nv_ref):
        g = p_ref[0]
        for d in range(1, N_DEV):
            g = g + p_ref[d]
        g_ref[...] = g
        d_ref[...], nm_ref[...], nv_ref[...] = _adam_math(w_ref[...], g, m_ref[...], v_ref[...])

    return pl.pallas_call(
        body, name="adamw_replicated", out_shape=(jax.ShapeDtypeStruct((rows, cols), F32),) * 4,
        in_specs=[_full(parts.shape)] + [_full((rows, cols))] * 3, out_specs=(_full((rows, cols)),) * 4, grid=(1,),
        compiler_params=_cp("arbitrary"),
    )(parts, w, m, v)


def _pad_rows(a2d, mult):
    pad = (-a2d.shape[-2]) % mult
    return jnp.pad(a2d, [(0, 0)] * (a2d.ndim - 2) + [(0, pad), (0, 0)]) if pad else a2d


def _pad_cols(a):
    pad = (-a.shape[-1]) % LANES
    return jnp.pad(a, [(0, 0)] * (a.ndim - 1) + [(0, pad)]) if pad else a


def _rot_cols(w):
    half = ROPE // 2
    return jnp.concatenate([-w[..., half:], w[..., :half]], axis=-1)


def _unrot_cols(dw):
    half = ROPE // 2
    return jnp.concatenate([dw[..., half:], -dw[..., :half]], axis=-1)


def _from_col_shards(stacked):
    return stacked.transpose(1, 0, 2).reshape(stacked.shape[1], -1)


def _to_col_shards(full):
    r = full.shape[0]
    return full.reshape(r, N_DEV, -1).transpose(1, 0, 2)


def _rope_tables(seq):
    half = ROPE // 2
    inv = jnp.power(ROPE_THETA, -jnp.arange(half, dtype=F32) / half)
    ang = jnp.arange(seq, dtype=F32)[:, None] * inv[None, :]
    cos = jnp.concatenate([jnp.cos(ang)] * 2, axis=1)
    sin = jnp.concatenate([jnp.sin(ang)] * 2, axis=1)
    ones, zeros = jnp.ones((seq, 2 * NOPE), F32), jnp.zeros((seq, 2 * NOPE), F32)
    pad = jnp.zeros((seq, PAIR_W - 2 * NOPE - 2 * ROPE), F32)
    cext = jnp.concatenate([ones, cos, cos, pad], axis=1)
    sext = jnp.concatenate([zeros, sin, sin, pad], axis=1)
    cs128 = jnp.concatenate([cos, sin, jnp.zeros((seq, LANES - 2 * ROPE), F32)], axis=1)
    return cext, sext, cs128


def _pair_slabs(nope, rope):
    k = nope.shape[0]
    nope = nope.reshape(k, N_PAIRS, 2 * NOPE)
    rope = jnp.zeros((k, N_PAIRS, 2 * ROPE), nope.dtype) if rope is None else rope.reshape(k, N_PAIRS, 2 * ROPE)
    pad = jnp.zeros((k, N_PAIRS, PAIR_W - 2 * NOPE - 2 * ROPE), nope.dtype)
    return jnp.concatenate([nope, rope, pad], axis=2).reshape(k, N_PAIRS * PAIR_W)


def _split_slabs(slabs):
    k = slabs.shape[0]
    s = slabs.reshape(k, N_PAIRS, PAIR_W)
    return s[:, :, :2 * NOPE].reshape(k, N_HEADS, NOPE), s[:, :, 2 * NOPE:2 * NOPE + 2 * ROPE].reshape(k, N_HEADS, ROPE)


def kernel(x, w_in, b_gate, g_q_a, w_uq, g_kv_a, w_ukv, w_o_mla, w_o_dil, w_out, ln1_g, ln1_b, w_ff1, w_ff2, ln2_g, ln2_b, loss_target, m_w_in, m_b_gate, m_g_q_a, m_w_uq, m_g_kv_a, m_w_ukv, m_w_o_mla, m_w_o_dil, m_w_out, m_ln1_g, m_ln1_b, m_w_ff1, m_w_ff2, m_ln2_g, m_ln2_b, v_w_in, v_b_gate, v_g_q_a, v_w_uq, v_g_kv_a, v_w_ukv, v_w_o_mla, v_w_o_dil, v_w_out, v_ln1_g, v_ln1_b, v_w_ff1, v_w_ff2, v_ln2_g, v_ln2_b):
    batch, seq, _ = x.shape
    tokens = batch * seq
    weights = dict(w_in=w_in, w_uq=w_uq, w_ukv=w_ukv, w_o_mla=w_o_mla, w_o_dil=w_o_dil, w_out=w_out, w_ff1=w_ff1, w_ff2=w_ff2, b_gate=b_gate)
    mom_m = dict(w_in=m_w_in, w_uq=m_w_uq, w_ukv=m_w_ukv, w_o_mla=m_w_o_mla, w_o_dil=m_w_o_dil, w_out=m_w_out, w_ff1=m_w_ff1, w_ff2=m_w_ff2, b_gate=m_b_gate)
    mom_v = dict(w_in=v_w_in, w_uq=v_w_uq, w_ukv=v_w_ukv, w_o_mla=v_w_o_mla, w_o_dil=v_w_o_dil, w_out=v_w_out, w_ff1=v_w_ff1, w_ff2=v_w_ff2, b_gate=v_b_gate)

    first = ["w_in", "w_uq", "w_ukv"]
    widths = [weights[n].shape[2] for n in first]
    shards = [_pad_cols(weights[n][0].astype(BF16)) for n in first]
    g_in, g_uq, g_ukv = _run_comm(_Gather(shards), shards, "all_gather_first_weights")
    g_uq, g_ukv = g_uq[:, :, :widths[1]], g_ukv[:, :, :widths[2]]

    s1, s2, n_in = Q_LORA + KV_LORA, Q_LORA + KV_LORA + ROPE, N_DEV * widths[0]

    def w_in_cols(lo, hi):
        out = []
        while lo < hi:
            d, off = divmod(lo, widths[0])
            take = min(hi - lo, widths[0] - off)
            out.append(g_in[d][:, off:off + take])
            lo += take
        return out

    w_in_ext = jnp.concatenate(w_in_cols(0, s2) + [_rot_cols(jnp.concatenate(w_in_cols(s1, s2), axis=1)),
                                                   jnp.zeros((D_MODEL, LOW_W - s2 - ROPE), BF16)] + w_in_cols(s2, n_in), axis=1)
    uq = _from_col_shards(g_uq).reshape(Q_LORA, N_HEADS, NOPE + ROPE)
    w1 = _pair_slabs(uq[:, :, :NOPE], uq[:, :, NOPE:])
    w2 = _pair_slabs(jnp.zeros_like(uq[:, :, :NOPE]), _rot_cols(uq[:, :, NOPE:]))
    ukv = _from_col_shards(g_ukv).reshape(KV_LORA, N_HEADS, NOPE + HEAD_V)
    wk_ext = _pair_slabs(ukv[:, :, :NOPE], None)
    wv = ukv[:, :, NOPE:].reshape(KV_LORA, N_HEADS * HEAD_V)
    eye = jnp.eye(ROPE, dtype=BF16)
    e_slab = jnp.concatenate([jnp.zeros((ROPE, 2 * NOPE), BF16), eye, eye, jnp.zeros((ROPE, PAIR_W - 2 * NOPE - 2 * ROPE), BF16)], axis=1)
    e128 = jnp.concatenate([jnp.tile(e_slab, (1, N_PAIRS)), jnp.zeros((LANES - ROPE, N_PAIRS * PAIR_W), BF16)], axis=0)
    cext, sext, cs128 = _rope_tables(seq)
    dil_bias = _dilated_bias_table(seq)
    no_bias = jnp.zeros((1, 8, LANES), F32)

    x2 = x.reshape(tokens, D_MODEL)
    xb = x2.astype(BF16)
    low, gates, qkvd, qp, kp, vm, qn, kvn = _fwd_proj(xb, w_in_ext, w1, w2, wk_ext, wv, e128, g_q_a, g_kv_a, cext, sext, cs128, seq=seq)
    bg = b_gate[0]
    bg_hi = bg.astype(BF16)
    bg_lo = (bg - bg_hi.astype(F32)).astype(BF16)
    later = [weights[n][0].astype(BF16) for n in ("w_o_mla", "w_o_dil", "w_out", "w_ff1", "w_ff2")]
    later.append(_pad_rows(jnp.concatenate([bg_hi, bg_lo], axis=0), 16))
    mla = dict(batch=batch, seq=seq, width=PAIR_W, col0=(0, 0, 0), dilated=False, scale=MLA_SCALE)
    dil = dict(batch=batch, seq=seq, width=LANES, col0=(0, N_PAIRS, 2 * N_PAIRS), dilated=True, scale=DIL_SCALE)
    o_a, lse_a, g_oa, g_ob, g_out, g_ff1, g_ff2, g_bg = _attn_fwd(
        qp, kp, vm, no_bias, name="mla_attention_fwd", comm=_Gather(later), comm_arrays=later, **mla)
    o_b, lse_b = _attn_fwd(qkvd, qkvd, qkvd, dil_bias, name="dilated_attention_fwd", **dil)
    w_oa, w_ob = _from_col_shards(g_oa), _from_col_shards(g_ob)
    w_out_full = g_out.reshape(D_MODEL, D_MODEL)
    w_ff2_full = g_ff2.reshape(D_FF, D_MODEL)
    bg_parts = g_bg.astype(F32)
    b_gate_full = _from_col_shards(bg_parts[:, 0:2] + bg_parts[:, 2:4])
    h, hb, xhat1, rstd1, y_a, y_b, mix = _fwd_mix(o_a, o_b, gates, x2, b_gate_full, w_oa, w_ob, w_out_full, ln1_g, ln1_b, seq=seq)
    u, dz2, dz2b, stat2 = _fwd_mlp(hb, h, loss_target.reshape(tokens, D_MODEL), g_ff1, w_ff2_full, ln2_g, ln2_b, seq=seq)

    du, dz1, dz1b, stat1 = _bwd_mlp(dz2, dz2b, u, xhat1, rstd1, g_ff1, w_ff2_full, ln1_g, seq=seq)
    dw_ff = [_wgrad(hb, du, "wgrad_ff1", by_shard=True),
             _wgrad(u, dz2b, "wgrad_ff2", square_relu=True).reshape(N_DEV, FF_SHARD, D_MODEL)]
    dgates, dy_a, dy_b, do_a, do_b, stat_g = _bwd_mix(dz1b, gates, y_a, y_b, b_gate_full, w_oa, w_ob, w_out_full, seq=seq)
    dqp, dkp, dvm, r_ff1, r_ff2 = _attn_bwd(qp, kp, vm, o_a, do_a, lse_a, no_bias, name="mla_attention_bwd",
                                            comm=_Scatter(dw_ff), comm_arrays=dw_ff, **mla)
    dw_mid = [_to_col_shards(_wgrad(o_a, dy_a, "wgrad_o_mla")), _to_col_shards(_wgrad(o_b, dy_b, "wgrad_o_dil")),
              _wgrad(mix, dz1b, "wgrad_out").reshape(N_DEV, D_MODEL // N_DEV, D_MODEL),
              _pad_rows(_to_col_shards(stat_g[0:2]).astype(BF16), 16)]
    dq_d, dk_d, dv_d, r_oa, r_ob, r_out, r_bg = _attn_bwd(qkvd, qkvd, qkvd, o_b, do_b, lse_b, dil_bias, name="dilated_attention_bwd",
                                                          comm=_Scatter(dw_mid), comm_arrays=dw_mid, **dil)
    grad_x, dproj, d_a, d_b, stat_r = _bwd_proj(dqp, dkp, dvm, dq_d, dk_d, dv_d, dgates, dz1, low, w_in_ext, w1, w2, wk_ext, wv,
                                                e128, g_q_a, g_kv_a, cext, sext, cs128, seq=seq)

    dw_in_ext = _wgrad(xb, dproj, "wgrad_in")
    dw1 = _wgrad(qn, d_a, "wgrad_uq_direct")
    dw2 = _wgrad(qn, d_b, "wgrad_uq_rotated")
    dwk = _wgrad(kvn, dkp, "wgrad_ukv_k")
    dwv = _wgrad(kvn, dvm, "wgrad_ukv_v")
    dw_kr = dw_in_ext[:, s1:s2] + _unrot_cols(dw_in_ext[:, s2:s2 + ROPE])

    def dw_in_cols(lo, hi):
        out = []
        for a, b, piece in ((0, s1, lambda u, v: dw_in_ext[:, u:v]), (s1, s2, lambda u, v: dw_kr[:, u - s1:v - s1]),
                            (s2, n_in, lambda u, v: dw_in_ext[:, u + LOW_W - s2:v + LOW_W - s2])):
            if max(lo, a) < min(hi, b):
                out.append(piece(max(lo, a), min(hi, b)))
        return out

    dw_in = jnp.stack([_pad_cols(jnp.concatenate(dw_in_cols(d * widths[0], (d + 1) * widths[0]), axis=1)) for d in range(N_DEV)])
    n1, r1 = _split_slabs(dw1)
    _, r2 = _split_slabs(dw2)
    dw_uq = jnp.concatenate([n1, r1 + _unrot_cols(r2)], axis=2).reshape(Q_LORA, N_HEADS * (NOPE + ROPE))
    nk, _ = _split_slabs(dwk)
    dw_ukv = jnp.concatenate([nk, dwv.reshape(KV_LORA, N_HEADS, HEAD_V)], axis=2).reshape(KV_LORA, N_HEADS * (NOPE + HEAD_V))
    last = [dw_in] + [_pad_cols(_to_col_shards(dw)) for dw in (dw_uq, dw_ukv)]
    theirs = _rs_sibling(last, "rs_last_sibling_exchange")
    sums = [_pair_sum(a, b, "rs_last_pair_sum_" + n) for a, b, n in zip(last, theirs, first)]
    got = _rs_chips([s[1] for s in sums], "rs_last_chip_exchange")

    upd = {}
    for n, w, (own, _), parts in zip(first, widths, sums, got):
        upd[n] = _adamw(weights[n][0], mom_m[n][0], mom_v[n][0], own[:, :w], parts[:, :, :w], "adamw_" + n)
    for n, own, parts in (("w_o_mla", dw_mid[0], r_oa), ("w_o_dil", dw_mid[1], r_ob), ("w_out", dw_mid[2], r_out),
                          ("w_ff1", dw_ff[0], r_ff1), ("w_ff2", dw_ff[1], r_ff2)):
        upd[n] = _adamw(weights[n][0], mom_m[n][0], mom_v[n][0], own, parts, "adamw_" + n)
    bg_upd = _adamw(_pad_rows(b_gate[0], 16), _pad_rows(m_b_gate[0], 16), _pad_rows(v_b_gate[0], 16), dw_mid[3], r_bg, "adamw_b_gate")
    upd["b_gate"] = tuple(t[0:2] for t in bg_upd)

    small_w = [g_q_a, g_kv_a, ln1_g, ln1_b, ln2_g, ln2_b]
    small_m = [m_g_q_a, m_g_kv_a, m_ln1_g, m_ln1_b, m_ln2_g, m_ln2_b]
    small_v = [v_g_q_a, v_g_kv_a, v_ln1_g, v_ln1_b, v_ln2_g, v_ln2_b]
    small_widths = [a.shape[1] for a in small_w]
    partial = jnp.concatenate([stat_r[0:1, :Q_LORA], stat_r[1:2, :KV_LORA], stat1[0:1], stat1[1:2], stat2[0:1], stat2[1:2],
                               stat2[2:3, :LANES]], axis=1)

    def as_rows(vecs, extra):
        flat = jnp.concatenate(vecs + [jnp.zeros((1, extra), F32)], axis=1)
        return _pad_rows(flat.reshape(-1, LANES), 8)

    partial = _pad_rows(partial.reshape(-1, LANES), 8)
    (every,) = _run_comm(_Gather([partial]), [partial], "all_gather_replicated_grads")
    g_s, d_s, nm_s, nv_s = _adamw_small(every, as_rows(small_w, LANES), as_rows(small_m, LANES), as_rows(small_v, LANES))

    def split_small(a):
        flat = a.reshape(1, -1)
        out, c0 = [], 0
        for w in small_widths:
            out.append(flat[:, c0:c0 + w])
            c0 += w
        return out, flat[0, c0]

    g_small, loss = split_small(g_s)
    small = [g_small, split_small(d_s)[0], split_small(nm_s)[0], split_small(nv_s)[0]]

    order = ["w_in", "b_gate", "g_q_a", "w_uq", "g_kv_a", "w_ukv", "w_o_mla", "w_o_dil", "w_out", "ln1_g", "ln1_b", "w_ff1", "w_ff2", "ln2_g", "ln2_b"]
    small_names = ["g_q_a", "g_kv_a", "ln1_g", "ln1_b", "ln2_g", "ln2_b"]

    def pick(kind):
        return [small[kind][small_names.index(n)] if n in small_names else upd[n][kind][None] for n in order]

    return (loss, grad_x.reshape(batch, seq, D_MODEL), *pick(0), *pick(1), *pick(2), *pick(3))
```

```python
import functools
import math

import jax
import jax.numpy as jnp
from jax import lax
from jax.experimental import pallas as pl
from jax.experimental.pallas import tpu as pltpu

F32 = jnp.float32
BF16 = jnp.bfloat16
I32 = jnp.int32

D_MODEL = 1024
N_HEADS = 8
NOPE = 64
ROPE = 32
HEAD_V = 64
Q_LORA = 384
KV_LORA = 256
DIL_WIDTH = 512
D_FF = 4096
ROPE_THETA = 10000.0
LN_EPS = 1e-5
RMS_EPS = 1e-6
NEG = -1e30
ALPHA = 2.0 ** 0.25
MLA_SCALE = (NOPE + ROPE) ** -0.5
DIL_SCALE = 64 ** -0.5
ADAM_LR, ADAM_B1, ADAM_B2, ADAM_EPS, ADAM_WD, ADAM_STEP = 0.001, 0.9, 0.999, 1e-08, 0.01, 10

LANES = 128
PAIR_W = 256
N_PAIRS = N_HEADS // 2
LOW_W = 768
IN_EXT = LOW_W + 3 * DIL_WIDTH + 2 * D_MODEL
N_DEV = 8
FF_SHARD = D_FF // N_DEV
FF_STEP = 4
WGRAD_SHARDS = 4
TOKEN_TILE = 256
ATTN_TILE = 256
VMEM_LIMIT = 56 << 20

MESH = pl.DeviceIdType.MESH
ANY = pl.BlockSpec(memory_space=pl.ANY)
CHIP_FLIPS = ((0, 0), (0, 1), (1, 0), (1, 1))
PEER_FLIPS = tuple((fx, fy, fc) for fx in (0, 1) for fy in (0, 1) for fc in (0, 1))[1:]


def _cp(*sem):
    return pltpu.CompilerParams(dimension_semantics=sem or None, vmem_limit_bytes=VMEM_LIMIT)


def _full(shape):
    nd = len(shape)
    return pl.BlockSpec(shape, lambda *_: (0,) * nd)


def _rows(tm, width):
    return pl.BlockSpec((tm, width), lambda i, *_: (i, 0))


def _dot(a, b):
    return jnp.dot(a, b, preferred_element_type=F32)


def _dot_nt(a, b):
    return lax.dot_general(a, b, (((1,), (1,)), ((), ())), preferred_element_type=F32)


def _dot_tn(a, b):
    return lax.dot_general(a, b, (((0,), (0,)), ((), ())), preferred_element_type=F32)


def _sigmoid(z):
    return 1.0 / (1.0 + jnp.exp(-z))


def _place():
    return lax.axis_index("x"), lax.axis_index("y"), lax.axis_index("c")


def _flip(v, f):
    return 1 - v if f else v


class _Gather:
    def __init__(self, shards):
        self.n = len(shards)
        self.out_shape = [jax.ShapeDtypeStruct((N_DEV, *s.shape), s.dtype) for s in shards]
        self.scratch = [pltpu.SemaphoreType.DMA((7 * self.n,)), pltpu.SemaphoreType.DMA((7 * self.n,)),
                        pltpu.SemaphoreType.DMA((self.n,))]

    def _copies(self, what, srcs, dsts, send, recv, local):
        x, y, c = _place()
        chips = [(_flip(x, fx), _flip(y, fy)) for fx, fy in CHIP_FLIPS[1:]]
        out = []
        for a in range(self.n):
            def slot(px, py, pc, a=a):
                return dsts[a].at[4 * px + 2 * py + pc]

            def copy(k, block, to, src=None, a=a, slot=slot):
                return pltpu.make_async_remote_copy(
                    src_ref=slot(*block) if src is None else src, dst_ref=slot(*block),
                    send_sem=send.at[7 * a + k], recv_sem=recv.at[7 * a + k], device_id=to, device_id_type=MESH)

            if what == "mine":
                out.append(pltpu.make_async_copy(srcs[a], slot(x, y, c), local.at[a]))
            elif what == "first":
                out.append(copy(0, (x, y, c), (x, y, 1 - c), src=srcs[a]))
                out += [copy(1 + j, (x, y, c), (*chip, c), src=srcs[a]) for j, chip in enumerate(chips)]
            elif what == "landed":
                out += [copy(1 + j, (*chip, c), (x, y, c)) for j, chip in enumerate(chips)]
            elif what == "passed":
                out += [copy(4 + j, (*chip, c), (x, y, 1 - c)) for j, chip in enumerate(chips)]
            else:
                out.append(copy(0, (x, y, 1 - c), (x, y, c)))
                out += [copy(4 + j, (*chip, 1 - c), (x, y, c)) for j, chip in enumerate(chips)]
        return out

    def start(self, *refs):
        for cp in self._copies("first", *refs) + self._copies("mine", *refs):
            cp.start()

    def forward(self, *refs):
        for landed, passed in zip(self._copies("landed", *refs), self._copies("passed", *refs)):
            landed.wait_recv()
            passed.start()

    def finish(self, *refs):
        for cp in self._copies("from_sibling", *refs):
            cp.wait_recv()
        for cp in self._copies("first", *refs) + self._copies("passed", *refs):
            cp.wait_send()
        for cp in self._copies("mine", *refs):
            cp.wait()


class _Scatter:
    def __init__(self, arrays):
        self.n = len(arrays)
        self.out_shape = [jax.ShapeDtypeStruct((7, *a.shape[1:]), a.dtype) for a in arrays]
        self.scratch = [pltpu.SemaphoreType.DMA((7 * self.n,)), pltpu.SemaphoreType.DMA((7 * self.n,))]

    def _copies(self, srcs, dsts, send, recv):
        x, y, c = _place()
        out = []
        for a in range(self.n):
            for k, (fx, fy, fc) in enumerate(PEER_FLIPS):
                px, py, pc = _flip(x, fx), _flip(y, fy), _flip(c, fc)
                out.append(pltpu.make_async_remote_copy(
                    src_ref=srcs[a].at[4 * px + 2 * py + pc], dst_ref=dsts[a].at[k],
                    send_sem=send.at[7 * a + k], recv_sem=recv.at[7 * a + k], device_id=(px, py, pc), device_id_type=MESH))
        return out

    def start(self, *refs):
        for cp in self._copies(*refs):
            cp.start()

    def forward(self, *refs):
        pass

    def finish(self, *refs):
        for cp in self._copies(*refs):
            cp.wait_send()
        for cp in self._copies(*refs):
            cp.wait_recv()


def _run_comm(comm, arrays, name):
    n = comm.n

    def body(*refs):
        args = (refs[:n], refs[n:2 * n], *refs[2 * n:])
        comm.start(*args)
        comm.forward(*args)
        comm.finish(*args)

    return pl.pallas_call(body, name=name, out_shape=comm.out_shape, in_specs=[ANY] * n, out_specs=[ANY] * n,
                          scratch_shapes=comm.scratch)(*arrays)


def _rs_sibling(arrays, name):
    n = len(arrays)

    def body(*refs):
        srcs, got, (send, recv) = refs[:n], refs[n:2 * n], refs[2 * n:]
        x, y, c = _place()
        copies = []
        for a in range(n):
            for r, (fx, fy) in enumerate(CHIP_FLIPS):
                chip = 2 * _flip(x, fx) + _flip(y, fy)
                copies.append(pltpu.make_async_remote_copy(
                    src_ref=srcs[a].at[2 * chip + 1 - c], dst_ref=got[a].at[r], send_sem=send.at[4 * a + r],
                    recv_sem=recv.at[4 * a + r], device_id=(x, y, 1 - c), device_id_type=MESH))
        for cp in copies:
            cp.start()
        for cp in copies:
            cp.wait_send()
        for cp in copies:
            cp.wait_recv()

    return pl.pallas_call(
        body, name=name, out_shape=[jax.ShapeDtypeStruct((4, *a.shape[1:]), a.dtype) for a in arrays],
        in_specs=[ANY] * n, out_specs=[ANY] * n,
        scratch_shapes=[pltpu.SemaphoreType.DMA((4 * n,)), pltpu.SemaphoreType.DMA((4 * n,))],
    )(*arrays)


def _rs_chips(arrays, name):
    n = len(arrays)

    def body(*refs):
        srcs, dsts, (send, recv) = refs[:n], refs[n:2 * n], refs[2 * n:]
        x, y, c = _place()
        copies = []
        for a in range(n):
            for k, (fx, fy) in enumerate(CHIP_FLIPS[1:]):
                copies.append(pltpu.make_async_remote_copy(
                    src_ref=srcs[a].at[k], dst_ref=dsts[a].at[k], send_sem=send.at[3 * a + k], recv_sem=recv.at[3 * a + k],
                    device_id=(_flip(x, fx), _flip(y, fy), c), device_id_type=MESH))
        for cp in copies:
            cp.start()
        for cp in copies:
            cp.wait_send()
        for cp in copies:
            cp.wait_recv()

    return pl.pallas_call(
        body, name=name, out_shape=[jax.ShapeDtypeStruct(a.shape, a.dtype) for a in arrays], in_specs=[ANY] * n,
        out_specs=[ANY] * n, scratch_shapes=[pltpu.SemaphoreType.DMA((3 * n,)), pltpu.SemaphoreType.DMA((3 * n,))],
    )(*arrays)


def _row_tile(rows):
    return 256 if rows % 256 == 0 else rows


def _chip_slots():
    x, y, c = _place()
    return jnp.stack([4 * _flip(x, fx) + 2 * _flip(y, fy) + c for fx, fy in CHIP_FLIPS]).astype(I32)


def _pair_sum(full, theirs, name):
    _, rows, cols = theirs.shape
    tr = _row_tile(rows)

    def body(slots_ref, m0_ref, m1_ref, m2_ref, m3_ref, b_ref, own_ref, rest_ref):
        own_ref[...] = m0_ref[...].astype(F32) + b_ref[0].astype(F32)
        for k, m_ref in enumerate((m1_ref, m2_ref, m3_ref)):
            rest_ref[k] = (m_ref[...].astype(F32) + b_ref[k + 1].astype(F32)).astype(BF16)

    def mine(k):
        return pl.BlockSpec((None, tr, cols), lambda i, slots: (slots[k], i, 0))

    return pl.pallas_call(
        body, name=name,
        grid_spec=pltpu.PrefetchScalarGridSpec(
            num_scalar_prefetch=1, grid=(rows // tr,),
            in_specs=[mine(0), mine(1), mine(2), mine(3), pl.BlockSpec((4, tr, cols), lambda i, slots: (0, i, 0))],
            out_specs=(pl.BlockSpec((tr, cols), lambda i, slots: (i, 0)), pl.BlockSpec((3, tr, cols), lambda i, slots: (0, i, 0)))),
        out_shape=(jax.ShapeDtypeStruct((rows, cols), F32), jax.ShapeDtypeStruct((3, rows, cols), BF16)),
        compiler_params=_cp("parallel"),
    )(_chip_slots(), full, full, full, full, theirs)


def _head_lanes(width, h):
    lane = lax.broadcasted_iota(I32, (1, width), 1)
    if width == LANES:
        return (lane >= 64 * h) & (lane < 64 * h + 64)
    nope = (lane >= NOPE * h) & (lane < NOPE * h + NOPE)
    rope = (lane >= 2 * NOPE + ROPE * h) & (lane < 2 * NOPE + ROPE * h + ROPE)
    return nope | rope


def _dilated_bias_table(seq):
    t = min(ATTN_TILE, seq)
    nd = seq // t

    def body(o_ref):
        delta = pl.program_id(0) * t + lax.broadcasted_iota(I32, (t, t), 1) - lax.broadcasted_iota(I32, (t, t), 0)
        mult = ((delta <= 128).astype(I32) + (((delta & 3) == 0) & (delta <= 512)).astype(I32)
                + ((delta & 15) == 0).astype(I32))
        logm = jnp.where(mult == 3, math.log(3.0), jnp.where(mult == 2, math.log(2.0), 0.0))
        valid = (delta >= 0) & (mult > 0)
        dist = delta.astype(F32)
        for h in range(N_HEADS):
            o_ref[h] = jnp.where(valid, logm - 2.0 ** (-(h + 1)) * dist, NEG)

    return pl.pallas_call(
        body, name="dilated_bias_table", grid=(nd,), out_shape=jax.ShapeDtypeStruct((N_HEADS, nd, t, t), F32),
        out_specs=pl.BlockSpec((N_HEADS, None, t, t), lambda d: (0, d, 0, 0)),
        compiler_params=_cp("parallel"),
    )()


def _comm_hooks(comm, refs, n_in, n_out):
    if comm is None:
        return refs[:n_in], refs[n_in:n_in + n_out], refs[n_in + n_out:], None
    n = comm.n
    ins, srcs = refs[:n_in], refs[n_in:n_in + n]
    outs, dsts = refs[n_in + n:n_in + n + n_out], refs[n_in + n + n_out:n_in + 2 * n + n_out]
    rest = refs[n_in + 2 * n + n_out:]
    own = len(rest) - len(comm.scratch)
    return ins, outs, rest[:own], (srcs, dsts, *rest[own:])


def _attn_fwd(q, k, v, bias, *, batch, seq, width, col0, dilated, scale, name, comm=None, comm_arrays=()):
    t = min(ATTN_TILE, seq)
    nq = seq // t
    cq, ck, cv = col0
    pre = scale if dilated else 1.0
    steps = batch * N_PAIRS * nq

    def body(*refs):
        (q_ref, k_ref, v_ref, bias_ref), (o_ref, lse_ref), _, plan = _comm_hooks(comm, refs, 4, 2)
        i = pl.program_id(2)
        step_no = (pl.program_id(0) * N_PAIRS + pl.program_id(1)) * nq + i
        if plan:
            pl.when(step_no == 0)(lambda: comm.start(*plan))
            pl.when(step_no == (3 * steps) // 4)(lambda: comm.forward(*plan))
        q2 = q_ref[...] * pre if dilated else q_ref[...]
        qh = [jnp.where(_head_lanes(width, h), q2, jnp.zeros_like(q2)) for h in (0, 1)]
        vlane = [_head_lanes(LANES, h) for h in (0, 1)]
        top = lax.broadcasted_iota(I32, (LANES, t), 0) < HEAD_V
        causal = lax.broadcasted_iota(I32, (t, t), 0) <= lax.broadcasted_iota(I32, (t, t), 1)

        def scores(j):
            kj = k_ref[pl.ds(pl.multiple_of(j * t, t), t), :]
            return [_dot_nt(kj, qh[h]) for h in (0, 1)]

        def step(j, carry, last):
            m0, l0, m1, l1, acc, s0, s1 = carry
            ahead = [] if last else scores(j + 1)
            vj = v_ref[pl.ds(pl.multiple_of(j * t, t), t), :]
            new, alphas, pv = [], [], []
            for h, (m, l, s) in enumerate(((m0, l0, s0), (m1, l1, s1))):
                if dilated:
                    s = s + bias_ref[h, i - j]
                else:
                    s = s * scale
                    if last:
                        s = jnp.where(causal, s, NEG)
                m_new = jnp.maximum(m, jnp.max(s, axis=0, keepdims=True))
                a = jnp.exp(m - m_new)
                p = jnp.exp(s - m_new)
                new += [m_new, a * l + jnp.sum(p, axis=0, keepdims=True)]
                alphas.append(a)
                pv.append(_dot_tn(jnp.where(vlane[h], vj, jnp.zeros_like(vj)), p.astype(BF16)))
            acc = jnp.where(top, alphas[0], alphas[1]) * acc + pv[0] + pv[1]
            return (*new, acc, *ahead)

        row = jnp.full((1, t), NEG, F32)
        zero = jnp.zeros((1, t), F32)
        init = (row, zero, row, zero, jnp.zeros((LANES, t), F32), *scores(0))
        m0, l0, m1, l1, acc = step(i, lax.fori_loop(0, i, functools.partial(step, last=False), init), True)
        o_ref[...] = jnp.transpose(acc * jnp.where(top, 1.0 / l0, 1.0 / l1)).astype(BF16)
        r = lax.broadcasted_iota(I32, (8, t), 0)
        lse_ref[...] = jnp.where(r == 0, m0 + jnp.log(l0), jnp.where(r == 1, m1 + jnp.log(l1), 0.0))
        if plan:
            pl.when(step_no == steps - 1)(lambda: comm.finish(*plan))

    bias_spec = (pl.BlockSpec((2, nq, t, t), lambda b, p, i: (p, 0, 0, 0)) if dilated
                 else pl.BlockSpec((None, 8, LANES), lambda b, p, i: (0, 0, 0)))
    n = comm.n if comm else 0
    return pl.pallas_call(
        body, name=name, grid=(batch, N_PAIRS, nq),
        out_shape=[jax.ShapeDtypeStruct((batch * seq, DIL_WIDTH), BF16), jax.ShapeDtypeStruct((batch * N_PAIRS, 8, seq), F32)]
        + (comm.out_shape if comm else []),
        in_specs=[pl.BlockSpec((t, width), lambda b, p, i: (b * nq + i, cq + p)),
                  pl.BlockSpec((seq, width), lambda b, p, i: (b, ck + p)),
                  pl.BlockSpec((seq, LANES), lambda b, p, i: (b, cv + p)),
                  bias_spec] + [ANY] * n,
        out_specs=[pl.BlockSpec((t, LANES), lambda b, p, i: (b * nq + i, p)),
                   pl.BlockSpec((None, 8, t), lambda b, p, i: (b * N_PAIRS + p, 0, i))] + [ANY] * n,
        scratch_shapes=comm.scratch if comm else [],
        compiler_params=_cp("arbitrary", "arbitrary", "arbitrary") if comm else _cp("parallel", "parallel", "arbitrary"),
    )(q, k, v, bias, *comm_arrays)


def _attn_bwd(q, k, v, o, do, lse, bias, *, batch, seq, width, col0, dilated, scale, name, comm=None, comm_arrays=()):
    t = min(ATTN_TILE, seq)
    nq = seq // t
    cq, ck, cv = col0
    pre = scale if dilated else 1.0
    steps = batch * N_PAIRS

    def body(*refs):
        ins, (dq_ref, dk_ref, dv_ref), (dq_acc, dk_acc, dv_acc, rowdot), plan = _comm_hooks(comm, refs, 7, 3)
        q_ref, k_ref, v_ref, o_ref, do_ref, lse_ref, bias_ref = ins
        step_no = pl.program_id(0) * N_PAIRS + pl.program_id(1)
        if plan:
            pl.when(step_no == 0)(lambda: comm.start(*plan))
        wlane = [_head_lanes(width, h) for h in (0, 1)]
        vlane = [_head_lanes(LANES, h) for h in (0, 1)]
        causal = lax.broadcasted_iota(I32, (t, t), 0) <= lax.broadcasted_iota(I32, (t, t), 1)
        prod = jnp.transpose(do_ref[...].astype(F32) * o_ref[...].astype(F32))
        rowdot[0:1, :] = jnp.sum(prod[0:HEAD_V], axis=0, keepdims=True)
        rowdot[1:2, :] = jnp.sum(prod[HEAD_V:], axis=0, keepdims=True)
        dq_acc[...] = jnp.zeros_like(dq_acc)

        def k_tile(j, _):
            ks = pl.multiple_of(j * t, t)
            kj = k_ref[pl.ds(ks, t), :]
            vj = v_ref[pl.ds(ks, t), :]
            kh = [jnp.where(wlane[h], kj, jnp.zeros_like(kj)) for h in (0, 1)]
            dk_acc[...] = jnp.zeros_like(dk_acc)
            dv_acc[...] = jnp.zeros_like(dv_acc)

            def operands(i):
                qs = pl.multiple_of(i * t, t)
                qi = q_ref[pl.ds(qs, t), :] * pre if dilated else q_ref[pl.ds(qs, t), :]
                doi = do_ref[pl.ds(qs, t), :]
                return ([jnp.where(wlane[h], qi, jnp.zeros_like(qi)) for h in (0, 1)],
                        [jnp.where(vlane[h], doi, jnp.zeros_like(doi)) for h in (0, 1)])

            def products(i):
                qih, doih = operands(i)
                return tuple(_dot_nt(kj, qih[h]) for h in (0, 1)) + tuple(_dot_nt(vj, doih[h]) for h in (0, 1))

            def q_tile(n, carry, last):
                i = nq - 1 - n
                s0, s1, dp0, dp1 = carry
                ahead = () if last else products(i - 1)
                qs = pl.multiple_of(i * t, t)
                qih, doih = operands(i)
                dq_i = jnp.zeros((t, width), F32)
                for h, (s, dp) in enumerate(((s0, dp0), (s1, dp1))):
                    if dilated:
                        s = s + bias_ref[h, i - j]
                    else:
                        s = s * scale
                        if last:
                            s = jnp.where(causal, s, NEG)
                    p = jnp.exp(s - lse_ref[h:h + 1, pl.ds(qs, t)])
                    ds = p * (dp - rowdot[h:h + 1, pl.ds(qs, t)])
                    ds = (ds if dilated else ds * scale).astype(BF16)
                    dv_acc[...] += _dot(p.astype(BF16), doih[h])
                    dk_acc[...] += _dot(ds, qih[h])
                    dq_i = dq_i + _dot_tn(ds, kh[h])
                dq_acc[pl.ds(qs, t), :] += dq_i
                return ahead

            q_tile(nq - 1 - j, lax.fori_loop(0, nq - 1 - j, functools.partial(q_tile, last=False), products(nq - 1)), True)
            dk_ref[pl.ds(ks, t), :] = dk_acc[...].astype(BF16)
            dv_ref[pl.ds(ks, t), :] = dv_acc[...].astype(BF16)
            return 0

        lax.fori_loop(0, nq, k_tile, 0)
        dq_ref[...] = (dq_acc[...] * pre).astype(BF16)
        if plan:
            pl.when(step_no == steps - 1)(lambda: comm.finish(*plan))

    tokens = batch * seq
    bias_spec = (pl.BlockSpec((2, nq, t, t), lambda b, p: (p, 0, 0, 0)) if dilated
                 else pl.BlockSpec((None, 8, LANES), lambda b, p: (0, 0, 0)))
    n = comm.n if comm else 0
    return pl.pallas_call(
        body, name=name, grid=(batch, N_PAIRS),
        out_shape=[jax.ShapeDtypeStruct((tokens, N_PAIRS * width), BF16), jax.ShapeDtypeStruct((tokens, N_PAIRS * width), BF16),
                   jax.ShapeDtypeStruct((tokens, DIL_WIDTH), BF16)] + (comm.out_shape if comm else []),
        in_specs=[pl.BlockSpec((seq, width), lambda b, p: (b, cq + p)),
                  pl.BlockSpec((seq, width), lambda b, p: (b, ck + p)),
                  pl.BlockSpec((seq, LANES), lambda b, p: (b, cv + p)),
                  pl.BlockSpec((seq, LANES), lambda b, p: (b, p)),
                  pl.BlockSpec((seq, LANES), lambda b, p: (b, p)),
                  pl.BlockSpec((None, 8, seq), lambda b, p: (b * N_PAIRS + p, 0, 0)),
                  bias_spec] + [ANY] * n,
        out_specs=[pl.BlockSpec((seq, width), lambda b, p: (b, p)),
                   pl.BlockSpec((seq, width), lambda b, p: (b, p)),
                   pl.BlockSpec((seq, LANES), lambda b, p: (b, p))] + [ANY] * n,
        scratch_shapes=[pltpu.VMEM((seq, width), F32), pltpu.VMEM((t, width), F32), pltpu.VMEM((t, LANES), F32),
                        pltpu.VMEM((8, seq), F32)] + (comm.scratch if comm else []),
        compiler_params=_cp("arbitrary", "arbitrary") if comm else _cp("parallel", "parallel"),
    )(q, k, v, o, do, lse, bias, *comm_arrays)


def _rms(xf, g):
    r = lax.rsqrt(jnp.mean(xf * xf, axis=1, keepdims=True) + RMS_EPS)
    return xf * r * g, r


def _rms_bwd(dy, xf, r, g):
    gy = dy * g
    dx = r * gy - xf * (r * r * r) * jnp.mean(gy * xf, axis=1, keepdims=True)
    return dx, dy * xf * r


def _ln_bwd(dy, xhat, rstd, g):
    dxh = dy * g
    return rstd * (dxh - jnp.mean(dxh, axis=1, keepdims=True) - xhat * jnp.mean(dxh * xhat, axis=1, keepdims=True))


def _fwd_proj(xb, w_in_ext, w1, w2, wk_ext, wv, e128, g_q, g_kv, cext, sext, cs128, *, seq):
    tokens = xb.shape[0]
    tm = min(TOKEN_TILE, seq)
    ns = seq // tm

    def body(x_ref, win_ref, w1_ref, w2_ref, wk_ref, wv_ref, e_ref, gq_ref, gkv_ref, c_ref, s_ref, cs_ref,
             low_ref, gates_ref, qkvd_ref, qp_ref, kp_ref, vm_ref, qn_ref, kvn_ref):
        xt = x_ref[...]
        low = _dot(xt, win_ref[:, 0:LOW_W])
        low_ref[...] = low
        qkvd_ref[...] = _dot(xt, win_ref[:, LOW_W:LOW_W + 3 * DIL_WIDTH]).astype(BF16)
        gates_ref[...] = _dot(xt, win_ref[:, LOW_W + 3 * DIL_WIDTH:])
        qn = _rms(low[:, 0:Q_LORA], gq_ref[...])[0].astype(BF16)
        kvn = _rms(low[:, Q_LORA:Q_LORA + KV_LORA], gkv_ref[...])[0].astype(BF16)
        qn_ref[...] = qn
        kvn_ref[...] = kvn
        cos, sin = (jnp.concatenate([r[...]] * N_PAIRS, axis=1) for r in (c_ref, s_ref))
        qp_ref[...] = (_dot(qn, w1_ref[...]) * cos + _dot(qn, w2_ref[...]) * sin).astype(BF16)
        kr = low[:, Q_LORA + KV_LORA:] * cs_ref[...]
        kr = kr + pltpu.roll(kr, LANES - ROPE, 1)
        lane = lax.broadcasted_iota(I32, kr.shape, 1)
        kr = jnp.where(lane < ROPE, kr, 0.0).astype(BF16)
        kp_ref[...] = (_dot(kvn, wk_ref[...]) + _dot(kr, e_ref[...])).astype(BF16)
        vm_ref[...] = _dot(kvn, wv_ref[...]).astype(BF16)

    n_gates = 2 * D_MODEL
    outs = [(LOW_W, F32), (n_gates, F32), (3 * DIL_WIDTH, BF16), (N_PAIRS * PAIR_W, BF16), (N_PAIRS * PAIR_W, BF16),
            (DIL_WIDTH, BF16), (Q_LORA, BF16), (KV_LORA, BF16)]
    return pl.pallas_call(
        body, name="fwd_proj", grid=(tokens // tm,),
        out_shape=tuple(jax.ShapeDtypeStruct((tokens, w), dt) for w, dt in outs),
        in_specs=[_rows(tm, D_MODEL), _full(w_in_ext.shape), _full(w1.shape), _full(w2.shape), _full(wk_ext.shape),
                  _full(wv.shape), _full(e128.shape), _full(g_q.shape), _full(g_kv.shape),
                  pl.BlockSpec((tm, PAIR_W), lambda i: (i % ns, 0)),
                  pl.BlockSpec((tm, PAIR_W), lambda i: (i % ns, 0)),
                  pl.BlockSpec((tm, LANES), lambda i: (i % ns, 0))],
        out_specs=tuple(_rows(tm, w) for w, _ in outs),
        compiler_params=_cp("parallel"),
    )(xb, w_in_ext, w1, w2, wk_ext, wv, e128, g_q, g_kv, cext, sext, cs128)


def _fwd_mix(o_a, o_b, gates, x, b_gate, w_oa, w_ob, w_out, ln_g, ln_b, *, seq):
    tokens = x.shape[0]
    tm = min(TOKEN_TILE, seq)

    def body(oa_ref, ob_ref, gt_ref, x_ref, bg_ref, woa_ref, wob_ref, wout_ref, g_ref, b_ref,
             h_ref, hb_ref, xhat_ref, rstd_ref, ya_ref, yb_ref, mix_ref):
        ya = _dot(oa_ref[...], woa_ref[...])
        yb = _dot(ob_ref[...], wob_ref[...])
        g0 = _sigmoid(gt_ref[:, 0:D_MODEL] + bg_ref[0:1, :])
        g1 = _sigmoid(gt_ref[:, D_MODEL:] + bg_ref[1:2, :])
        mix = (g0 * ya + g1 * yb).astype(BF16)
        z = ALPHA * x_ref[...] + _dot(mix, wout_ref[...])
        zc = z - jnp.mean(z, axis=1, keepdims=True)
        rstd = lax.rsqrt(jnp.mean(zc * zc, axis=1, keepdims=True) + LN_EPS)
        xhat = zc * rstd
        h = xhat * g_ref[...] + b_ref[...]
        h_ref[...] = h
        hb_ref[...] = h.astype(BF16)
        xhat_ref[...] = xhat
        rstd_ref[...] = jnp.broadcast_to(rstd, (tm, LANES))
        ya_ref[...] = ya.astype(BF16)
        yb_ref[...] = yb.astype(BF16)
        mix_ref[...] = mix

    outs = [(D_MODEL, F32), (D_MODEL, BF16), (D_MODEL, F32), (LANES, F32), (D_MODEL, BF16), (D_MODEL, BF16), (D_MODEL, BF16)]
    return pl.pallas_call(
        body, name="fwd_mix", grid=(tokens // tm,),
        out_shape=tuple(jax.ShapeDtypeStruct((tokens, w), dt) for w, dt in outs),
        in_specs=[_rows(tm, DIL_WIDTH), _rows(tm, DIL_WIDTH), _rows(tm, 2 * D_MODEL), _rows(tm, D_MODEL),
                  _full(b_gate.shape), _full(w_oa.shape), _full(w_ob.shape), _full(w_out.shape),
                  _full(ln_g.shape), _full(ln_b.shape)],
        out_specs=tuple(_rows(tm, w) for w, _ in outs),
        compiler_params=_cp("parallel"),
    )(o_a, o_b, gates, x, b_gate, w_oa, w_ob, w_out, ln_g, ln_b)


def _fwd_mlp(hb, h, target, w_ff1, w_ff2, ln_g, ln_b, *, seq):
    tokens = h.shape[0]
    tm = min(2 * TOKEN_TILE, seq)
    tf = FF_SHARD
    nf = N_DEV // FF_STEP

    def body(hb_ref, h_ref, tg_ref, w1_ref, w2_ref, g_ref, b_ref, u_ref, dz_ref, dzb_ref, stat_ref, acc):
        i, j = pl.program_id(0), pl.program_id(1)

        @pl.when((i == 0) & (j == 0))
        def _():
            stat_ref[...] = jnp.zeros_like(stat_ref)

        @pl.when(j == 0)
        def _():
            acc[...] = jnp.zeros_like(acc)

        acts = []
        for s in range(FF_STEP):
            u = _dot(hb_ref[...], w1_ref[s])
            u_ref[:, s * tf:(s + 1) * tf] = u.astype(BF16)
            acts.append(jnp.square(jnp.maximum(u, 0.0)).astype(BF16))
        acc[...] += _dot(jnp.concatenate(acts, axis=1), w2_ref[...])

        @pl.when(j == nf - 1)
        def _():
            z = ALPHA * h_ref[...] + acc[...]
            zc = z - jnp.mean(z, axis=1, keepdims=True)
            rstd = lax.rsqrt(jnp.mean(zc * zc, axis=1, keepdims=True) + LN_EPS)
            xhat = zc * rstd
            err = xhat * g_ref[...] + b_ref[...] - tg_ref[...]
            dy = err * (1.0 / D_MODEL)
            dz = _ln_bwd(dy, xhat, rstd, g_ref[...])
            dz_ref[...] = dz
            dzb_ref[...] = dz.astype(BF16)
            stat_ref[0:1, :] += jnp.sum(dy * xhat, axis=0, keepdims=True)
            stat_ref[1:2, :] += jnp.sum(dy, axis=0, keepdims=True)
            stat_ref[2:3, :] += jnp.sum(jnp.sum(err * err, axis=1, keepdims=True), axis=0, keepdims=True) * (0.5 / D_MODEL)

    return pl.pallas_call(
        body, name="fwd_mlp", grid=(tokens // tm, nf),
        out_shape=(jax.ShapeDtypeStruct((tokens, D_FF), BF16), jax.ShapeDtypeStruct((tokens, D_MODEL), F32),
                   jax.ShapeDtypeStruct((tokens, D_MODEL), BF16), jax.ShapeDtypeStruct((8, D_MODEL), F32)),
        in_specs=[_rows(tm, D_MODEL), _rows(tm, D_MODEL), _rows(tm, D_MODEL),
                  pl.BlockSpec((FF_STEP, D_MODEL, tf), lambda i, j: (j, 0, 0)),
                  pl.BlockSpec((FF_STEP * tf, D_MODEL), lambda i, j: (j, 0)),
                  _full(ln_g.shape), _full(ln_b.shape)],
        out_specs=(pl.BlockSpec((tm, FF_STEP * tf), lambda i, j: (i, j)), _rows(tm, D_MODEL), _rows(tm, D_MODEL),
                   _full((8, D_MODEL))),
        scratch_shapes=[pltpu.VMEM((tm, D_MODEL), F32)],
        compiler_params=_cp("arbitrary", "arbitrary"),
    )(hb, h, target, w_ff1, w_ff2, ln_g, ln_b)


def _bwd_mlp(dz2, dz2b, u, xhat1, rstd1, w_ff1, w_ff2, ln_g, *, seq):
    tokens = dz2.shape[0]
    tm = min(2 * TOKEN_TILE, seq)
    tf = FF_SHARD
    nf = N_DEV // FF_STEP

    def body(dz_ref, dzb_ref, u_ref, xh_ref, rs_ref, w1_ref, w2_ref, g_ref, du_ref, dz1_ref, dz1b_ref, stat_ref, acc):
        i, j = pl.program_id(0), pl.program_id(1)

        @pl.when((i == 0) & (j == 0))
        def _():
            stat_ref[...] = jnp.zeros_like(stat_ref)

        @pl.when(j == 0)
        def _():
            acc[...] = jnp.zeros_like(acc)

        da = _dot_nt(dzb_ref[...], w2_ref[...])
        du = (da * (2.0 * jnp.maximum(u_ref[...].astype(F32), 0.0))).astype(BF16)
        du_ref[...] = du
        part = _dot_nt(du[:, 0:tf], w1_ref[0])
        for s in range(1, FF_STEP):
            part = part + _dot_nt(du[:, s * tf:(s + 1) * tf], w1_ref[s])
        acc[...] += part

        @pl.when(j == nf - 1)
        def _():
            dh = ALPHA * dz_ref[...] + acc[...]
            xhat = xh_ref[...]
            dz1 = _ln_bwd(dh, xhat, rs_ref[:, 0:1], g_ref[...])
            dz1_ref[...] = dz1
            dz1b_ref[...] = dz1.astype(BF16)
            stat_ref[0:1, :] += jnp.sum(dh * xhat, axis=0, keepdims=True)
            stat_ref[1:2, :] += jnp.sum(dh, axis=0, keepdims=True)

    return pl.pallas_call(
        body, name="bwd_mlp", grid=(tokens // tm, nf),
        out_shape=(jax.ShapeDtypeStruct((tokens, D_FF), BF16), jax.ShapeDtypeStruct((tokens, D_MODEL), F32),
                   jax.ShapeDtypeStruct((tokens, D_MODEL), BF16), jax.ShapeDtypeStruct((8, D_MODEL), F32)),
        in_specs=[_rows(tm, D_MODEL), _rows(tm, D_MODEL), pl.BlockSpec((tm, FF_STEP * tf), lambda i, j: (i, j)),
                  _rows(tm, D_MODEL), _rows(tm, LANES),
                  pl.BlockSpec((FF_STEP, D_MODEL, tf), lambda i, j: (j, 0, 0)),
                  pl.BlockSpec((FF_STEP * tf, D_MODEL), lambda i, j: (j, 0)),
                  _full(ln_g.shape)],
        out_specs=(pl.BlockSpec((tm, FF_STEP * tf), lambda i, j: (i, j)), _rows(tm, D_MODEL), _rows(tm, D_MODEL),
                   _full((8, D_MODEL))),
        scratch_shapes=[pltpu.VMEM((tm, D_MODEL), F32)],
        compiler_params=_cp("arbitrary", "arbitrary"),
    )(dz2, dz2b, u, xhat1, rstd1, w_ff1, w_ff2, ln_g)


def _bwd_mix(dz1b, gates, y_a, y_b, b_gate, w_oa, w_ob, w_out, *, seq):
    tokens = dz1b.shape[0]
    tm = min(TOKEN_TILE, seq)

    def body(dz_ref, gt_ref, ya_ref, yb_ref, bg_ref, woa_ref, wob_ref, wout_ref,
             dgt_ref, dya_ref, dyb_ref, doa_ref, dob_ref, stat_ref):
        @pl.when(pl.program_id(0) == 0)
        def _():
            stat_ref[...] = jnp.zeros_like(stat_ref)

        dmix = _dot_nt(dz_ref[...], wout_ref[...])
        for k, (y_ref, w_ref, dy_ref, do_ref) in enumerate(((ya_ref, woa_ref, dya_ref, doa_ref), (yb_ref, wob_ref, dyb_ref, dob_ref))):
            g = _sigmoid(gt_ref[:, k * D_MODEL:(k + 1) * D_MODEL] + bg_ref[k:k + 1, :])
            dgate = dmix * y_ref[...].astype(F32) * g * (1.0 - g)
            dgt_ref[:, k * D_MODEL:(k + 1) * D_MODEL] = dgate.astype(BF16)
            stat_ref[k:k + 1, :] += jnp.sum(dgate, axis=0, keepdims=True)
            dy = (dmix * g).astype(BF16)
            dy_ref[...] = dy
            do_ref[...] = _dot_nt(dy, w_ref[...]).astype(BF16)

    outs = [(2 * D_MODEL, BF16), (D_MODEL, BF16), (D_MODEL, BF16), (DIL_WIDTH, BF16), (DIL_WIDTH, BF16)]
    return pl.pallas_call(
        body, name="bwd_mix", grid=(tokens // tm,),
        out_shape=tuple(jax.ShapeDtypeStruct((tokens, w), dt) for w, dt in outs) + (jax.ShapeDtypeStruct((8, D_MODEL), F32),),
        in_specs=[_rows(tm, D_MODEL), _rows(tm, 2 * D_MODEL), _rows(tm, D_MODEL), _rows(tm, D_MODEL),
                  _full(b_gate.shape), _full(w_oa.shape), _full(w_ob.shape), _full(w_out.shape)],
        out_specs=tuple(_rows(tm, w) for w, _ in outs) + (_full((8, D_MODEL)),),
        compiler_params=_cp("arbitrary"),
    )(dz1b, gates, y_a, y_b, b_gate, w_oa, w_ob, w_out)


def _bwd_proj(dqp, dkp, dvm, dq_d, dk_d, dv_d, dgates, dz1, low, w_in_ext, w1, w2, wk_ext, wv, e128, g_q, g_kv, cext, sext, cs128, *, seq):
    tokens = dz1.shape[0]
    tm = min(TOKEN_TILE, seq)
    ns = seq // tm

    def body(dqp_ref, dkp_ref, dvm_ref, dqd_ref, dkd_ref, dvd_ref, dgt_ref, dz_ref, low_ref, win_ref, w1_ref, w2_ref, wk_ref,
             wv_ref, e_ref, gq_ref, gkv_ref, c_ref, s_ref, cs_ref, dx_ref, dproj_ref, da_ref, db_ref, stat_ref):
        @pl.when(pl.program_id(0) == 0)
        def _():
            stat_ref[...] = jnp.zeros_like(stat_ref)

        low = low_ref[...]
        dqp = dqp_ref[...].astype(F32)
        cos, sin = (jnp.concatenate([r[...]] * N_PAIRS, axis=1) for r in (c_ref, s_ref))
        d_a = (dqp * cos).astype(BF16)
        d_b = (dqp * sin).astype(BF16)
        da_ref[...] = d_a
        db_ref[...] = d_b
        q_a = low[:, 0:Q_LORA]
        _, rq = _rms(q_a, gq_ref[...])
        dq_a, gq_terms = _rms_bwd(_dot_nt(d_a, w1_ref[...]) + _dot_nt(d_b, w2_ref[...]), q_a, rq, gq_ref[...])
        kv_a = low[:, Q_LORA:Q_LORA + KV_LORA]
        _, rkv = _rms(kv_a, gkv_ref[...])
        dkp = dkp_ref[...]
        dkv_a, gkv_terms = _rms_bwd(_dot_nt(dkp, wk_ref[...]) + _dot_nt(dvm_ref[...], wv_ref[...]), kv_a, rkv, gkv_ref[...])
        dkr = _dot_nt(dkp, e_ref[...])
        dkr = (dkr + pltpu.roll(dkr, ROPE, 1)) * cs_ref[...]
        stat_ref[0:1, 0:Q_LORA] += jnp.sum(gq_terms, axis=0, keepdims=True)
        stat_ref[1:2, 0:KV_LORA] += jnp.sum(gkv_terms, axis=0, keepdims=True)
        dproj_ref[:, 0:Q_LORA] = dq_a.astype(BF16)
        dproj_ref[:, Q_LORA:Q_LORA + KV_LORA] = dkv_a.astype(BF16)
        dproj_ref[:, Q_LORA + KV_LORA:LOW_W] = dkr.astype(BF16)
        dproj_ref[:, LOW_W:LOW_W + DIL_WIDTH] = dqd_ref[...]
        dproj_ref[:, LOW_W + DIL_WIDTH:LOW_W + 2 * DIL_WIDTH] = dkd_ref[...]
        dproj_ref[:, LOW_W + 2 * DIL_WIDTH:LOW_W + 3 * DIL_WIDTH] = dvd_ref[...]
        dproj_ref[:, LOW_W + 3 * DIL_WIDTH:] = dgt_ref[...]
        dx_ref[...] = ALPHA * dz_ref[...] + _dot_nt(dproj_ref[...], win_ref[...])

    wide = N_PAIRS * PAIR_W
    return pl.pallas_call(
        body, name="bwd_proj", grid=(tokens // tm,),
        out_shape=(jax.ShapeDtypeStruct((tokens, D_MODEL), F32), jax.ShapeDtypeStruct((tokens, IN_EXT), BF16),
                   jax.ShapeDtypeStruct((tokens, wide), BF16), jax.ShapeDtypeStruct((tokens, wide), BF16),
                   jax.ShapeDtypeStruct((8, D_MODEL), F32)),
        in_specs=[_rows(tm, wide), _rows(tm, wide), _rows(tm, DIL_WIDTH), _rows(tm, DIL_WIDTH), _rows(tm, DIL_WIDTH),
                  _rows(tm, DIL_WIDTH), _rows(tm, 2 * D_MODEL),
                  _rows(tm, D_MODEL), _rows(tm, LOW_W), _full(w_in_ext.shape), _full(w1.shape), _full(w2.shape),
                  _full(wk_ext.shape), _full(wv.shape), _full(e128.shape), _full(g_q.shape), _full(g_kv.shape),
                  pl.BlockSpec((tm, PAIR_W), lambda i: (i % ns, 0)), pl.BlockSpec((tm, PAIR_W), lambda i: (i % ns, 0)),
                  pl.BlockSpec((tm, LANES), lambda i: (i % ns, 0))],
        out_specs=(_rows(tm, D_MODEL), _rows(tm, IN_EXT), _rows(tm, wide), _rows(tm, wide), _full((8, D_MODEL))),
        compiler_params=_cp("arbitrary"),
    )(dqp, dkp, dvm, dq_d, dk_d, dv_d, dgates, dz1, low, w_in_ext, w1, w2, wk_ext, wv, e128, g_q, g_kv, cext, sext, cs128)


def _wgrad(a, b, name, square_relu=False, by_shard=False):
    tokens, ka = a.shape
    n = b.shape[1]
    tka = min(ka, 512)
    shard = n // N_DEV
    tn = WGRAD_SHARDS * shard if by_shard else max(w for w in range(LANES, min(n, 2304) + 1, LANES) if n % w == 0)
    tt = min(tokens, 1024)
    nt = tokens // tt

    def body(a_ref, b_ref, o_ref, acc):
        kt = pl.program_id(2)

        @pl.when(kt == 0)
        def _():
            acc[...] = jnp.zeros_like(acc)

        at = a_ref[...]
        if square_relu:
            at = jnp.square(jnp.maximum(at.astype(F32), 0.0)).astype(BF16)
        acc[...] += _dot_tn(at, b_ref[...])

        @pl.when(kt == nt - 1)
        def _():
            if by_shard:
                for s in range(WGRAD_SHARDS):
                    o_ref[s] = acc[:, s * shard:(s + 1) * shard].astype(BF16)
            else:
                o_ref[...] = acc[...].astype(BF16)

    if by_shard:
        out_shape, out_spec = (N_DEV, ka, shard), pl.BlockSpec((WGRAD_SHARDS, tka, shard), lambda i, j, k: (j, i, 0))
    else:
        out_shape, out_spec = (ka, n), pl.BlockSpec((tka, tn), lambda i, j, k: (i, j))
    return pl.pallas_call(
        body, name=name, grid=(ka // tka, n // tn, nt), out_shape=jax.ShapeDtypeStruct(out_shape, BF16),
        in_specs=[pl.BlockSpec((tt, tka), lambda i, j, k: (k, i)), pl.BlockSpec((tt, tn), lambda i, j, k: (k, j))],
        out_specs=out_spec,
        scratch_shapes=[pltpu.VMEM((tka, tn), F32)],
        compiler_params=_cp("parallel", "parallel", "arbitrary"),
    )(a, b)


def _adam_math(w, g, m, v):
    m = ADAM_B1 * m + (1.0 - ADAM_B1) * g
    v = ADAM_B2 * v + (1.0 - ADAM_B2) * jnp.square(g)
    m_hat = m / (1.0 - ADAM_B1 ** ADAM_STEP)
    v_hat = v / (1.0 - ADAM_B2 ** ADAM_STEP)
    return -ADAM_LR * (m_hat / (jnp.sqrt(v_hat) + ADAM_EPS) + ADAM_WD * w), m, v


def _adamw(w, m, v, own, parts, name):
    rows, cols = w.shape
    tr = _row_tile(rows)
    n_parts = parts.shape[0]

    def body(slot_ref, w_ref, m_ref, v_ref, own_ref, p_ref, g_ref, d_ref, nm_ref, nv_ref):
        g = own_ref[...].astype(F32)
        for d in range(n_parts):
            g = g + p_ref[d].astype(F32)
        g_ref[...] = g
        d_ref[...], nm_ref[...], nv_ref[...] = _adam_math(w_ref[...], g, m_ref[...], v_ref[...])

    x, y, c = _place()
    blk = pl.BlockSpec((tr, cols), lambda i, slot: (i, 0))
    own_blk = blk if own.ndim == 2 else pl.BlockSpec((None, tr, cols), lambda i, slot: (slot[0], i, 0))
    return pl.pallas_call(
        body, name=name,
        grid_spec=pltpu.PrefetchScalarGridSpec(
            num_scalar_prefetch=1, grid=(rows // tr,),
            in_specs=[blk, blk, blk, own_blk, pl.BlockSpec((n_parts, tr, cols), lambda i, slot: (0, i, 0))],
            out_specs=(blk,) * 4),
        out_shape=(jax.ShapeDtypeStruct((rows, cols), F32),) * 4, compiler_params=_cp("parallel"),
    )(jnp.reshape(4 * x + 2 * y + c, (1,)).astype(I32), w, m, v, own, parts)


def _adamw_small(parts, w, m, v):
    _, rows, cols = parts.shape

    def body(p_ref, w_ref, m_ref, v_ref, g_ref, d_ref, nm_ref, nv_ref):
        g = p_ref[0]
        for d in range(1, N_DEV):
            g = g + p_ref[d]
        g_ref[...] = g
        d_ref[...], nm_ref[...], nv_ref[...] = _adam_math(w_ref[...], g, m_ref[...], v_ref[...])

    return pl.pallas_call(
        body, name="adamw_replicated", out_shape=(jax.ShapeDtypeStruct((rows, cols), F32),) * 4,
        in_specs=[_full(parts.shape)] + [_full((rows, cols))] * 3, out_specs=(_full((rows, cols)),) * 4, grid=(1,),
        compiler_params=_cp("arbitrary"),
    )(parts, w, m, v)


def _pad_rows(a2d, mult):
    pad = (-a2d.shape[-2]) % mult
    return jnp.pad(a2d, [(0, 0)] * (a2d.ndim - 2) + [(0, pad), (0, 0)]) if pad else a2d


def _pad_cols(a):
    pad = (-a.shape[-1]) % LANES
    return jnp.pad(a, [(0, 0)] * (a.ndim - 1) + [(0, pad)]) if pad else a


def _rot_cols(w):
    half = ROPE // 2
    return jnp.concatenate([-w[..., half:], w[..., :half]], axis=-1)


def _unrot_cols(dw):
    half = ROPE // 2
    return jnp.concatenate([dw[..., half:], -dw[..., :half]], axis=-1)


def _from_col_shards(stacked):
    return stacked.transpose(1, 0, 2).reshape(stacked.shape[1], -1)


def _to_col_shards(full):
    r = full.shape[0]
    return full.reshape(r, N_DEV, -1).transpose(1, 0, 2)


def _rope_tables(seq):
    half = ROPE // 2
    inv = jnp.power(ROPE_THETA, -jnp.arange(half, dtype=F32) / half)
    ang = jnp.arange(seq, dtype=F32)[:, None] * inv[None, :]
    cos = jnp.concatenate([jnp.cos(ang)] * 2, axis=1)
    sin = jnp.concatenate([jnp.sin(ang)] * 2, axis=1)
    ones, zeros = jnp.ones((seq, 2 * NOPE), F32), jnp.zeros((seq, 2 * NOPE), F32)
    pad = jnp.zeros((seq, PAIR_W - 2 * NOPE - 2 * ROPE), F32)
    cext = jnp.concatenate([ones, cos, cos, pad], axis=1)
    sext = jnp.concatenate([zeros, sin, sin, pad], axis=1)
    cs128 = jnp.concatenate([cos, sin, jnp.zeros((seq, LANES - 2 * ROPE), F32)], axis=1)
    return cext, sext, cs128


def _pair_slabs(nope, rope):
    k = nope.shape[0]
    nope = nope.reshape(k, N_PAIRS, 2 * NOPE)
    rope = jnp.zeros((k, N_PAIRS, 2 * ROPE), nope.dtype) if rope is None else rope.reshape(k, N_PAIRS, 2 * ROPE)
    pad = jnp.zeros((k, N_PAIRS, PAIR_W - 2 * NOPE - 2 * ROPE), nope.dtype)
    return jnp.concatenate([nope, rope, pad], axis=2).reshape(k, N_PAIRS * PAIR_W)


def _split_slabs(slabs):
    k = slabs.shape[0]
    s = slabs.reshape(k, N_PAIRS, PAIR_W)
    return s[:, :, :2 * NOPE].reshape(k, N_HEADS, NOPE), s[:, :, 2 * NOPE:2 * NOPE + 2 * ROPE].reshape(k, N_HEADS, ROPE)


def kernel(x, w_in, b_gate, g_q_a, w_uq, g_kv_a, w_ukv, w_o_mla, w_o_dil, w_out, ln1_g, ln1_b, w_ff1, w_ff2, ln2_g, ln2_b, loss_target, m_w_in, m_b_gate, m_g_q_a, m_w_uq, m_g_kv_a, m_w_ukv, m_w_o_mla, m_w_o_dil, m_w_out, m_ln1_g, m_ln1_b, m_w_ff1, m_w_ff2, m_ln2_g, m_ln2_b, v_w_in, v_b_gate, v_g_q_a, v_w_uq, v_g_kv_a, v_w_ukv, v_w_o_mla, v_w_o_dil, v_w_out, v_ln1_g, v_ln1_b, v_w_ff1, v_w_ff2, v_ln2_g, v_ln2_b):
    batch, seq, _ = x.shape
    tokens = batch * seq
    weights = dict(w_in=w_in, w_uq=w_uq, w_ukv=w_ukv, w_o_mla=w_o_mla, w_o_dil=w_o_dil, w_out=w_out, w_ff1=w_ff1, w_ff2=w_ff2, b_gate=b_gate)
    mom_m = dict(w_in=m_w_in, w_uq=m_w_uq, w_ukv=m_w_ukv, w_o_mla=m_w_o_mla, w_o_dil=m_w_o_dil, w_out=m_w_out, w_ff1=m_w_ff1, w_ff2=m_w_ff2, b_gate=m_b_gate)
    mom_v = dict(w_in=v_w_in, w_uq=v_w_uq, w_ukv=v_w_ukv, w_o_mla=v_w_o_mla, w_o_dil=v_w_o_dil, w_out=v_w_out, w_ff1=v_w_ff1, w_ff2=v_w_ff2, b_gate=v_b_gate)

    first = ["w_in", "w_uq", "w_ukv"]
    widths = [weights[n].shape[2] for n in first]
    shards = [_pad_cols(weights[n][0].astype(BF16)) for n in first]
    g_in, g_uq, g_ukv = _run_comm(_Gather(shards), shards, "all_gather_first_weights")
    g_uq, g_ukv = g_uq[:, :, :widths[1]], g_ukv[:, :, :widths[2]]

    s1, s2, n_in = Q_LORA + KV_LORA, Q_LORA + KV_LORA + ROPE, N_DEV * widths[0]

    def w_in_cols(lo, hi):
        out = []
        while lo < hi:
            d, off = divmod(lo, widths[0])
            take = min(hi - lo, widths[0] - off)
            out.append(g_in[d][:, off:off + take])
            lo += take
        return out

    w_in_ext = jnp.concatenate(w_in_cols(0, s2) + [_rot_cols(jnp.concatenate(w_in_cols(s1, s2), axis=1)),
                                                   jnp.zeros((D_MODEL, LOW_W - s2 - ROPE), BF16)] + w_in_cols(s2, n_in), axis=1)
    uq = _from_col_shards(g_uq).reshape(Q_LORA, N_HEADS, NOPE + ROPE)
    w1 = _pair_slabs(uq[:, :, :NOPE], uq[:, :, NOPE:])
    w2 = _pair_slabs(jnp.zeros_like(uq[:, :, :NOPE]), _rot_cols(uq[:, :, NOPE:]))
    ukv = _from_col_shards(g_ukv).reshape(KV_LORA, N_HEADS, NOPE + HEAD_V)
    wk_ext = _pair_slabs(ukv[:, :, :NOPE], None)
    wv = ukv[:, :, NOPE:].reshape(KV_LORA, N_HEADS * HEAD_V)
    eye = jnp.eye(ROPE, dtype=BF16)
    e_slab = jnp.concatenate([jnp.zeros((ROPE, 2 * NOPE), BF16), eye, eye, jnp.zeros((ROPE, PAIR_W - 2 * NOPE - 2 * ROPE), BF16)], axis=1)
    e128 = jnp.concatenate([jnp.tile(e_slab, (1, N_PAIRS)), jnp.zeros((LANES - ROPE, N_PAIRS * PAIR_W), BF16)], axis=0)
    cext, sext, cs128 = _rope_tables(seq)
    dil_bias = _dilated_bias_table(seq)
    no_bias = jnp.zeros((1, 8, LANES), F32)

    x2 = x.reshape(tokens, D_MODEL)
    xb = x2.astype(BF16)
    low, gates, qkvd, qp, kp, vm, qn, kvn = _fwd_proj(xb, w_in_ext, w1, w2, wk_ext, wv, e128, g_q_a, g_kv_a, cext, sext, cs128, seq=seq)
    bg = b_gate[0]
    bg_hi = bg.astype(BF16)
    bg_lo = (bg - bg_hi.astype(F32)).astype(BF16)
    later = [weights[n][0].astype(BF16) for n in ("w_o_mla", "w_o_dil", "w_out", "w_ff1", "w_ff2")]
    later.append(_pad_rows(jnp.concatenate([bg_hi, bg_lo], axis=0), 16))
    mla = dict(batch=batch, seq=seq, width=PAIR_W, col0=(0, 0, 0), dilated=False, scale=MLA_SCALE)
    dil = dict(batch=batch, seq=seq, width=LANES, col0=(0, N_PAIRS, 2 * N_PAIRS), dilated=True, scale=DIL_SCALE)
    o_a, lse_a, g_oa, g_ob, g_out, g_ff1, g_ff2, g_bg = _attn_fwd(
        qp, kp, vm, no_bias, name="mla_attention_fwd", comm=_Gather(later), comm_arrays=later, **mla)
    o_b, lse_b = _attn_fwd(qkvd, qkvd, qkvd, dil_bias, name="dilated_attention_fwd", **dil)
    w_oa, w_ob = _from_col_shards(g_oa), _from_col_shards(g_ob)
    w_out_full = g_out.reshape(D_MODEL, D_MODEL)
    w_ff2_full = g_ff2.reshape(D_FF, D_MODEL)
    bg_parts = g_bg.astype(F32)
    b_gate_full = _from_col_shards(bg_parts[:, 0:2] + bg_parts[:, 2:4])
    h, hb, xhat1, rstd1, y_a, y_b, mix = _fwd_mix(o_a, o_b, gates, x2, b_gate_full, w_oa, w_ob, w_out_full, ln1_g, ln1_b, seq=seq)
    u, dz2, dz2b, stat2 = _fwd_mlp(hb, h, loss_target.reshape(tokens, D_MODEL), g_ff1, w_ff2_full, ln2_g, ln2_b, seq=seq)

    du, dz1, dz1b, stat1 = _bwd_mlp(dz2, dz2b, u, xhat1, rstd1, g_ff1, w_ff2_full, ln1_g, seq=seq)
    dw_ff = [_wgrad(hb, du, "wgrad_ff1", by_shard=True),
             _wgrad(u, dz2b, "wgrad_ff2", square_relu=True).reshape(N_DEV, FF_SHARD, D_MODEL)]
    dgates, dy_a, dy_b, do_a, do_b, stat_g = _bwd_mix(dz1b, gates, y_a, y_b, b_gate_full, w_oa, w_ob, w_out_full, seq=seq)
    dqp, dkp, dvm, r_ff1, r_ff2 = _attn_bwd(qp, kp, vm, o_a, do_a, lse_a, no_bias, name="mla_attention_bwd",
                                            comm=_Scatter(dw_ff), comm_arrays=dw_ff, **mla)
    dw_mid = [_to_col_shards(_wgrad(o_a, dy_a, "wgrad_o_mla")), _to_col_shards(_wgrad(o_b, dy_b, "wgrad_o_dil")),
              _wgrad(mix, dz1b, "wgrad_out").reshape(N_DEV, D_MODEL // N_DEV, D_MODEL),
              _pad_rows(_to_col_shards(stat_g[0:2]).astype(BF16), 16)]
    dq_d, dk_d, dv_d, r_oa, r_ob, r_out, r_bg = _attn_bwd(qkvd, qkvd, qkvd, o_b, do_b, lse_b, dil_bias, name="dilated_attention_bwd",
                                                          comm=_Scatter(dw_mid), comm_arrays=dw_mid, **dil)
    grad_x, dproj, d_a, d_b, stat_r = _bwd_proj(dqp, dkp, dvm, dq_d, dk_d, dv_d, dgates, dz1, low, w_in_ext, w1, w2, wk_ext, wv,
                                                e128, g_q_a, g_kv_a, cext, sext, cs128, seq=seq)

    dw_in_ext = _wgrad(xb, dproj, "wgrad_in")
    dw1 = _wgrad(qn, d_a, "wgrad_uq_direct")
    dw2 = _wgrad(qn, d_b, "wgrad_uq_rotated")
    dwk = _wgrad(kvn, dkp, "wgrad_ukv_k")
    dwv = _wgrad(kvn, dvm, "wgrad_ukv_v")
    dw_kr = dw_in_ext[:, s1:s2] + _unrot_cols(dw_in_ext[:, s2:s2 + ROPE])

    def dw_in_cols(lo, hi):
        out = []
        for a, b, piece in ((0, s1, lambda u, v: dw_in_ext[:, u:v]), (s1, s2, lambda u, v: dw_kr[:, u - s1:v - s1]),
                            (s2, n_in, lambda u, v: dw_in_ext[:, u + LOW_W - s2:v + LOW_W - s2])):
            if max(lo, a) < min(hi, b):
                out.append(piece(max(lo, a), min(hi, b)))
        return out

    dw_in = jnp.stack([_pad_cols(jnp.concatenate(dw_in_cols(d * widths[0], (d + 1) * widths[0]), axis=1)) for d in range(N_DEV)])
    n1, r1 = _split_slabs(dw1)
    _, r2 = _split_slabs(dw2)
    dw_uq = jnp.concatenate([n1, r1 + _unrot_cols(r2)], axis=2).reshape(Q_LORA, N_HEADS * (NOPE + ROPE))
    nk, _ = _split_slabs(dwk)
    dw_ukv = jnp.concatenate([nk, dwv.reshape(KV_LORA, N_HEADS, HEAD_V)], axis=2).reshape(KV_LORA, N_HEADS * (NOPE + HEAD_V))
    last = [dw_in] + [_pad_cols(_to_col_shards(dw)) for dw in (dw_uq, dw_ukv)]
    theirs = _rs_sibling(last, "rs_last_sibling_exchange")
    sums = [_pair_sum(a, b, "rs_last_pair_sum_" + n) for a, b, n in zip(last, theirs, first)]
    got = _rs_chips([s[1] for s in sums], "rs_last_chip_exchange")

    upd = {}
    for n, w, (own, _), parts in zip(first, widths, sums, got):
        upd[n] = _adamw(weights[n][0], mom_m[n][0], mom_v[n][0], own[:, :w], parts[:, :, :w], "adamw_" + n)
    for n, own, parts in (("w_o_mla", dw_mid[0], r_oa), ("w_o_dil", dw_mid[1], r_ob), ("w_out", dw_mid[2], r_out),
                          ("w_ff1", dw_ff[0], r_ff1), ("w_ff2", dw_ff[1], r_ff2)):
        upd[n] = _adamw(weights[n][0], mom_m[n][0], mom_v[n][0], own, parts, "adamw_" + n)
    bg_upd = _adamw(_pad_rows(b_gate[0], 16), _pad_rows(m_b_gate[0], 16), _pad_rows(v_b_gate[0], 16), dw_mid[3], r_bg, "adamw_b_gate")
    upd["b_gate"] = tuple(t[0:2] for t in bg_upd)

    small_w = [g_q_a, g_kv_a, ln1_g, ln1_b, ln2_g, ln2_b]
    small_m = [m_g_q_a, m_g_kv_a, m_ln1_g, m_ln1_b, m_ln2_g, m_ln2_b]
    small_v = [v_g_q_a, v_g_kv_a, v_ln1_g, v_ln1_b, v_ln2_g, v_ln2_b]
    small_widths = [a.shape[1] for a in small_w]
    partial = jnp.concatenate([stat_r[0:1, :Q_LORA], stat_r[1:2, :KV_LORA], stat1[0:1], stat1[1:2], stat2[0:1], stat2[1:2],
                               stat2[2:3, :LANES]], axis=1)

    def as_rows(vecs, extra):
        flat = jnp.concatenate(vecs + [jnp.zeros((1, extra), F32)], axis=1)
        return _pad_rows(flat.reshape(-1, LANES), 8)

    partial = _pad_rows(partial.reshape(-1, LANES), 8)
    (every,) = _run_comm(_Gather([partial]), [partial], "all_gather_replicated_grads")
    g_s, d_s, nm_s, nv_s = _adamw_small(every, as_rows(small_w, LANES), as_rows(small_m, LANES), as_rows(small_v, LANES))

    def split_small(a):
        flat = a.reshape(1, -1)
        out, c0 = [], 0
        for w in small_widths:
            out.append(flat[:, c0:c0 + w])
            c0 += w
        return out, flat[0, c0]

    g_small, loss = split_small(g_s)
    small = [g_small, split_small(d_s)[0], split_small(nm_s)[0], split_small(nv_s)[0]]

    order = ["w_in", "b_gate", "g_q_a", "w_uq", "g_kv_a", "w_ukv", "w_o_mla", "w_o_dil", "w_out", "ln1_g", "ln1_b", "w_ff1", "w_ff2", "ln2_g", "ln2_b"]
    small_names = ["g_q_a", "g_kv_a", "ln1_g", "ln1_b", "ln2_g", "ln2_b"]

    def pick(kind):
        return [small[kind][small_names.index(n)] if n in small_names else upd[n][kind][None] for n in order]

    return (loss, grad_x.reshape(batch, seq, D_MODEL), *pick(0), *pick(1), *pick(2), *pick(3))
```

```python
import functools
import math

import jax
import jax.numpy as jnp
from jax import lax
from jax.experimental import pallas as pl
from jax.experimental.pallas import tpu as pltpu

F32 = jnp.float32
BF16 = jnp.bfloat16
I32 = jnp.int32

D_MODEL = 1024
N_HEADS = 8
NOPE = 64
ROPE = 32
HEAD_V = 64
Q_LORA = 384
KV_LORA = 256
DIL_WIDTH = 512
D_FF = 4096
ROPE_THETA = 10000.0
LN_EPS = 1e-5
RMS_EPS = 1e-6
NEG = -1e30
ALPHA = 2.0 ** 0.25
MLA_SCALE = (NOPE + ROPE) ** -0.5
DIL_SCALE = 64 ** -0.5
ADAM_LR, ADAM_B1, ADAM_B2, ADAM_EPS, ADAM_WD, ADAM_STEP = 0.001, 0.9, 0.999, 1e-08, 0.01, 10

LANES = 128
PAIR_W = 256
N_PAIRS = N_HEADS // 2
LOW_W = 768
IN_EXT = LOW_W + 3 * DIL_WIDTH + 2 * D_MODEL
N_DEV = 8
FF_SHARD = D_FF // N_DEV
FF_STEP = 4
WGRAD_SHARDS = 4
TOKEN_TILE = 256
ATTN_TILE = 256
VMEM_LIMIT = 56 << 20

MESH = pl.DeviceIdType.MESH
ANY = pl.BlockSpec(memory_space=pl.ANY)
CHIP_FLIPS = ((0, 0), (0, 1), (1, 0), (1, 1))
PEER_FLIPS = tuple((fx, fy, fc) for fx in (0, 1) for fy in (0, 1) for fc in (0, 1))[1:]


def _cp(*sem):
    return pltpu.CompilerParams(dimension_semantics=sem or None, vmem_limit_bytes=VMEM_LIMIT)


def _full(shape):
    nd = len(shape)
    return pl.BlockSpec(shape, lambda *_: (0,) * nd)


def _rows(tm, width):
    return pl.BlockSpec((tm, width), lambda i, *_: (i, 0))


def _dot(a, b):
    return jnp.dot(a, b, preferred_element_type=F32)


def _dot_nt(a, b):
    return lax.dot_general(a, b, (((1,), (1,)), ((), ())), preferred_element_type=F32)


def _dot_tn(a, b):
    return lax.dot_general(a, b, (((0,), (0,)), ((), ())), preferred_element_type=F32)


def _sigmoid(z):
    return 1.0 / (1.0 + jnp.exp(-z))


def _place():
    return lax.axis_index("x"), lax.axis_index("y"), lax.axis_index("c")


def _flip(v, f):
    return 1 - v if f else v


class _Gather:
    def __init__(self, shards):
        self.n = len(shards)
        self.out_shape = [jax.ShapeDtypeStruct((N_DEV, *s.shape), s.dtype) for s in shards]
        self.scratch = [pltpu.SemaphoreType.DMA((7 * self.n,)), pltpu.SemaphoreType.DMA((7 * self.n,)),
                        pltpu.SemaphoreType.DMA((self.n,))]

    def _copies(self, what, srcs, dsts, send, recv, local):
        x, y, c = _place()
        chips = [(_flip(x, fx), _flip(y, fy)) for fx, fy in CHIP_FLIPS[1:]]
        out = []
        for a in range(self.n):
            def slot(px, py, pc, a=a):
                return dsts[a].at[4 * px + 2 * py + pc]

            def copy(k, block, to, src=None, a=a, slot=slot):
                return pltpu.make_async_remote_copy(
                    src_ref=slot(*block) if src is None else src, dst_ref=slot(*block),
                    send_sem=send.at[7 * a + k], recv_sem=recv.at[7 * a + k], device_id=to, device_id_type=MESH)

            if what == "mine":
                out.append(pltpu.make_async_copy(srcs[a], slot(x, y, c), local.at[a]))
            elif what == "first":
                out.append(copy(0, (x, y, c), (x, y, 1 - c), src=srcs[a]))
                out += [copy(1 + j, (x, y, c), (*chip, c), src=srcs[a]) for j, chip in enumerate(chips)]
            elif what == "landed":
                out += [copy(1 + j, (*chip, c), (x, y, c)) for j, chip in enumerate(chips)]
            elif what == "passed":
                out += [copy(4 + j, (*chip, c), (x, y, 1 - c)) for j, chip in enumerate(chips)]
            else:
                out.append(copy(0, (x, y, 1 - c), (x, y, c)))
                out += [copy(4 + j, (*chip, 1 - c), (x, y, c)) for j, chip in enumerate(chips)]
        return out

    def start(self, *refs):
        for cp in self._copies("first", *refs) + self._copies("mine", *refs):
            cp.start()

    def forward(self, *refs):
        for landed, passed in zip(self._copies("landed", *refs), self._copies("passed", *refs)):
            landed.wait_recv()
            passed.start()

    def finish(self, *refs):
        for cp in self._copies("from_sibling", *refs):
            cp.wait_recv()
        for cp in self._copies("first", *refs) + self._copies("passed", *refs):
            cp.wait_send()
        for cp in self._copies("mine", *refs):
            cp.wait()


class _Scatter:
    def __init__(self, arrays):
        self.n = len(arrays)
        self.out_shape = [jax.ShapeDtypeStruct((7, *a.shape[1:]), a.dtype) for a in arrays]
        self.scratch = [pltpu.SemaphoreType.DMA((7 * self.n,)), pltpu.SemaphoreType.DMA((7 * self.n,))]

    def _copies(self, srcs, dsts, send, recv):
        x, y, c = _place()
        out = []
        for a in range(self.n):
            for k, (fx, fy, fc) in enumerate(PEER_FLIPS):
                px, py, pc = _flip(x, fx), _flip(y, fy), _flip(c, fc)
                out.append(pltpu.make_async_remote_copy(
                    src_ref=srcs[a].at[4 * px + 2 * py + pc], dst_ref=dsts[a].at[k],
                    send_sem=send.at[7 * a + k], recv_sem=recv.at[7 * a + k], device_id=(px, py, pc), device_id_type=MESH))
        return out

    def start(self, *refs):
        for cp in self._copies(*refs):
            cp.start()

    def forward(self, *refs):
        pass

    def finish(self, *refs):
        for cp in self._copies(*refs):
            cp.wait_send()
        for cp in self._copies(*refs):
            cp.wait_recv()


class _ChipExchange:
    def __init__(self, arrays):
        self.n = len(arrays)
        self.out_shape = [jax.ShapeDtypeStruct(a.shape, a.dtype) for a in arrays]
        self.scratch = [pltpu.SemaphoreType.DMA((3 * self.n,)), pltpu.SemaphoreType.DMA((3 * self.n,))]

    def _copies(self, srcs, dsts, send, recv):
        x, y, c = _place()
        return [pltpu.make_async_remote_copy(
            src_ref=srcs[a].at[k], dst_ref=dsts[a].at[k], send_sem=send.at[3 * a + k], recv_sem=recv.at[3 * a + k],
            device_id=(_flip(x, fx), _flip(y, fy), c), device_id_type=MESH)
            for a in range(self.n) for k, (fx, fy) in enumerate(CHIP_FLIPS[1:])]

    def start(self, *refs):
        for cp in self._copies(*refs):
            cp.start()

    def forward(self, *refs):
        pass

    def finish(self, *refs):
        for cp in self._copies(*refs):
            cp.wait_send()
        for cp in self._copies(*refs):
            cp.wait_recv()


def _run_comm(plans, name):
    n_in = sum(p.n for p, _ in plans)

    def body(*refs):
        args, i0, s0 = [], 0, 2 * n_in
        for p, _ in plans:
            args.append((refs[i0:i0 + p.n], refs[n_in + i0:n_in + i0 + p.n], *refs[s0:s0 + len(p.scratch)]))
            i0, s0 = i0 + p.n, s0 + len(p.scratch)
        for phase in ("start", "forward", "finish"):
            for (p, _), a in zip(plans, args):
                getattr(p, phase)(*a)

    out = pl.pallas_call(
        body, name=name, out_shape=[s for p, _ in plans for s in p.out_shape], in_specs=[ANY] * n_in,
        out_specs=[ANY] * n_in, scratch_shapes=[s for p, _ in plans for s in p.scratch],
    )(*[a for _, arrays in plans for a in arrays])
    split, i0 = [], 0
    for p, _ in plans:
        split.append(out[i0:i0 + p.n])
        i0 += p.n
    return split


def _rs_sibling(arrays, name):
    n = len(arrays)

    def body(*refs):
        srcs, got, (send, recv) = refs[:n], refs[n:2 * n], refs[2 * n:]
        x, y, c = _place()
        copies = []
        for a in range(n):
            for r, (fx, fy) in enumerate(CHIP_FLIPS):
                chip = 2 * _flip(x, fx) + _flip(y, fy)
                copies.append(pltpu.make_async_remote_copy(
                    src_ref=srcs[a].at[2 * chip + 1 - c], dst_ref=got[a].at[r], send_sem=send.at[4 * a + r],
                    recv_sem=recv.at[4 * a + r], device_id=(x, y, 1 - c), device_id_type=MESH))
        for cp in copies:
            cp.start()
        for cp in copies:
            cp.wait_send()
        for cp in copies:
            cp.wait_recv()

    return pl.pallas_call(
        body, name=name, out_shape=[jax.ShapeDtypeStruct((4, *a.shape[1:]), a.dtype) for a in arrays],
        in_specs=[ANY] * n, out_specs=[ANY] * n,
        scratch_shapes=[pltpu.SemaphoreType.DMA((4 * n,)), pltpu.SemaphoreType.DMA((4 * n,))],
    )(*arrays)


def _row_tile(rows):
    return 256 if rows % 256 == 0 else rows


def _chip_slots():
    x, y, c = _place()
    return jnp.stack([4 * _flip(x, fx) + 2 * _flip(y, fy) + c for fx, fy in CHIP_FLIPS]).astype(I32)


def _pair_sum(full, theirs, name):
    _, rows, cols = theirs.shape
    tr = _row_tile(rows)

    def body(slots_ref, m0_ref, m1_ref, m2_ref, m3_ref, b_ref, own_ref, rest_ref):
        own_ref[...] = m0_ref[...].astype(F32) + b_ref[0].astype(F32)
        for k, m_ref in enumerate((m1_ref, m2_ref, m3_ref)):
            rest_ref[k] = (m_ref[...].astype(F32) + b_ref[k + 1].astype(F32)).astype(BF16)

    def mine(k):
        return pl.BlockSpec((None, tr, cols), lambda i, slots: (slots[k], i, 0))

    return pl.pallas_call(
        body, name=name,
        grid_spec=pltpu.PrefetchScalarGridSpec(
            num_scalar_prefetch=1, grid=(rows // tr,),
            in_specs=[mine(0), mine(1), mine(2), mine(3), pl.BlockSpec((4, tr, cols), lambda i, slots: (0, i, 0))],
            out_specs=(pl.BlockSpec((tr, cols), lambda i, slots: (i, 0)), pl.BlockSpec((3, tr, cols), lambda i, slots: (0, i, 0)))),
        out_shape=(jax.ShapeDtypeStruct((rows, cols), F32), jax.ShapeDtypeStruct((3, rows, cols), BF16)),
        compiler_params=_cp("parallel"),
    )(_chip_slots(), full, full, full, full, theirs)


def _head_lanes(width, h):
    lane = lax.broadcasted_iota(I32, (1, width), 1)
    if width == LANES:
        return (lane >= 64 * h) & (lane < 64 * h + 64)
    nope = (lane >= NOPE * h) & (lane < NOPE * h + NOPE)
    rope = (lane >= 2 * NOPE + ROPE * h) & (lane < 2 * NOPE + ROPE * h + ROPE)
    return nope | rope


def _dilated_bias_table(seq):
    t = min(ATTN_TILE, seq)
    nd = seq // t

    def body(o_ref):
        delta = pl.program_id(0) * t + lax.broadcasted_iota(I32, (t, t), 1) - lax.broadcasted_iota(I32, (t, t), 0)
        mult = ((delta <= 128).astype(I32) + (((delta & 3) == 0) & (delta <= 512)).astype(I32)
                + ((delta & 15) == 0).astype(I32))
        logm = jnp.where(mult == 3, math.log(3.0), jnp.where(mult == 2, math.log(2.0), 0.0))
        valid = (delta >= 0) & (mult > 0)
        dist = delta.astype(F32)
        for h in range(N_HEADS):
            o_ref[h] = jnp.where(valid, logm - 2.0 ** (-(h + 1)) * dist, NEG)

    return pl.pallas_call(
        body, name="dilated_bias_table", grid=(nd,), out_shape=jax.ShapeDtypeStruct((N_HEADS, nd, t, t), F32),
        out_specs=pl.BlockSpec((N_HEADS, None, t, t), lambda d: (0, d, 0, 0)),
        compiler_params=_cp("parallel"),
    )()


def _comm_hooks(comm, refs, n_in, n_out):
    if comm is None:
        return refs[:n_in], refs[n_in:n_in + n_out], refs[n_in + n_out:], None
    n = comm.n
    ins, srcs = refs[:n_in], refs[n_in:n_in + n]
    outs, dsts = refs[n_in + n:n_in + n + n_out], refs[n_in + n + n_out:n_in + 2 * n + n_out]
    rest = refs[n_in + 2 * n + n_out:]
    own = len(rest) - len(comm.scratch)
    return ins, outs, rest[:own], (srcs, dsts, *rest[own:])


def _attn_fwd(q, k, v, bias, *, batch, seq, width, col0, dilated, scale, name, comm=None, comm_arrays=()):
    t = min(ATTN_TILE, seq)
    nq = seq // t
    cq, ck, cv = col0
    pre = scale if dilated else 1.0
    steps = batch * N_PAIRS * nq

    def body(*refs):
        (q_ref, k_ref, v_ref, bias_ref), (o_ref, lse_ref), _, plan = _comm_hooks(comm, refs, 4, 2)
        i = pl.program_id(2)
        step_no = (pl.program_id(0) * N_PAIRS + pl.program_id(1)) * nq + i
        if plan:
            pl.when(step_no == 0)(lambda: comm.start(*plan))
            pl.when(step_no == (3 * steps) // 4)(lambda: comm.forward(*plan))
        q2 = q_ref[...] * pre if dilated else q_ref[...]
        qh = [jnp.where(_head_lanes(width, h), q2, jnp.zeros_like(q2)) for h in (0, 1)]
        vlane = [_head_lanes(LANES, h) for h in (0, 1)]
        top = lax.broadcasted_iota(I32, (LANES, t), 0) < HEAD_V
        causal = lax.broadcasted_iota(I32, (t, t), 0) <= lax.broadcasted_iota(I32, (t, t), 1)

        def scores(j):
            kj = k_ref[pl.ds(pl.multiple_of(j * t, t), t), :]
            return [_dot_nt(kj, qh[h]) for h in (0, 1)]

        def step(j, carry, last):
            m0, l0, m1, l1, acc, s0, s1 = carry
            ahead = [] if last else scores(j + 1)
            vj = v_ref[pl.ds(pl.multiple_of(j * t, t), t), :]
            new, alphas, pv = [], [], []
            for h, (m, l, s) in enumerate(((m0, l0, s0), (m1, l1, s1))):
                if dilated:
                    s = s + bias_ref[h, i - j]
                else:
                    s = s * scale
                    if last:
                        s = jnp.where(causal, s, NEG)
                m_new = jnp.maximum(m, jnp.max(s, axis=0, keepdims=True))
                a = jnp.exp(m - m_new)
                p = jnp.exp(s - m_new)
                new += [m_new, a * l + jnp.sum(p, axis=0, keepdims=True)]
                alphas.append(a)
                pv.append(_dot_tn(jnp.where(vlane[h], vj, jnp.zeros_like(vj)), p.astype(BF16)))
            acc = jnp.where(top, alphas[0], alphas[1]) * acc + pv[0] + pv[1]
            return (*new, acc, *ahead)

        row = jnp.full((1, t), NEG, F32)
        zero = jnp.zeros((1, t), F32)
        init = (row, zero, row, zero, jnp.zeros((LANES, t), F32), *scores(0))
        m0, l0, m1, l1, acc = step(i, lax.fori_loop(0, i, functools.partial(step, last=False), init), True)
        o_ref[...] = jnp.transpose(acc * jnp.where(top, 1.0 / l0, 1.0 / l1)).astype(BF16)
        r = lax.broadcasted_iota(I32, (8, t), 0)
        lse_ref[...] = jnp.where(r == 0, m0 + jnp.log(l0), jnp.where(r == 1, m1 + jnp.log(l1), 0.0))
        if plan:
            pl.when(step_no == steps - 1)(lambda: comm.finish(*plan))

    bias_spec = (pl.BlockSpec((2, nq, t, t), lambda b, p, i: (p, 0, 0, 0)) if dilated
                 else pl.BlockSpec((None, 8, LANES), lambda b, p, i: (0, 0, 0)))
    n = comm.n if comm else 0
    return pl.pallas_call(
        body, name=name, grid=(batch, N_PAIRS, nq),
        out_shape=[jax.ShapeDtypeStruct((batch * seq, DIL_WIDTH), BF16), jax.ShapeDtypeStruct((batch * N_PAIRS, 8, seq), F32)]
        + (comm.out_shape if comm else []),
        in_specs=[pl.BlockSpec((t, width), lambda b, p, i: (b * nq + i, cq + p)),
                  pl.BlockSpec((seq, width), lambda b, p, i: (b, ck + p)),
                  pl.BlockSpec((seq, LANES), lambda b, p, i: (b, cv + p)),
                  bias_spec] + [ANY] * n,
        out_specs=[pl.BlockSpec((t, LANES), lambda b, p, i: (b * nq + i, p)),
                   pl.BlockSpec((None, 8, t), lambda b, p, i: (b * N_PAIRS + p, 0, i))] + [ANY] * n,
        scratch_shapes=comm.scratch if comm else [],
        compiler_params=_cp("arbitrary", "arbitrary", "arbitrary") if comm else _cp("parallel", "parallel", "arbitrary"),
    )(q, k, v, bias, *comm_arrays)


def _attn_bwd(q, k, v, o, do, lse, bias, *, batch, seq, width, col0, dilated, scale, name, comm=None, comm_arrays=()):
    t = min(ATTN_TILE, seq)
    nq = seq // t
    cq, ck, cv = col0
    pre = scale if dilated else 1.0
    steps = batch * N_PAIRS

    def body(*refs):
        ins, (dq_ref, dk_ref, dv_ref), (dq_acc, dk_acc, dv_acc, rowdot, q_heads, do_heads), plan = _comm_hooks(comm, refs, 7, 3)
        q_ref, k_ref, v_ref, o_ref, do_ref, lse_ref, bias_ref = ins
        step_no = pl.program_id(0) * N_PAIRS + pl.program_id(1)
        if plan:
            pl.when(step_no == 0)(lambda: comm.start(*plan))
        wlane = [_head_lanes(width, h) for h in (0, 1)]
        vlane = [_head_lanes(LANES, h) for h in (0, 1)]
        causal = lax.broadcasted_iota(I32, (t, t), 0) <= lax.broadcasted_iota(I32, (t, t), 1)
        q_all = q_ref[...] * pre if dilated else q_ref[...]
        for h in (0, 1):
            q_heads[h] = jnp.where(wlane[h], q_all, jnp.zeros_like(q_all))
            do_heads[h] = jnp.where(vlane[h], do_ref[...], jnp.zeros_like(do_ref[...]))
        prod = jnp.transpose(do_ref[...].astype(F32) * o_ref[...].astype(F32))
        rowdot[0:1, :] = jnp.sum(prod[0:HEAD_V], axis=0, keepdims=True)
        rowdot[1:2, :] = jnp.sum(prod[HEAD_V:], axis=0, keepdims=True)
        dq_acc[...] = jnp.zeros_like(dq_acc)

        def k_tile(j, _):
            ks = pl.multiple_of(j * t, t)
            kj = k_ref[pl.ds(ks, t), :]
            vj = v_ref[pl.ds(ks, t), :]
            kh = [jnp.where(wlane[h], kj, jnp.zeros_like(kj)) for h in (0, 1)]
            dk_acc[...] = jnp.zeros_like(dk_acc)
            dv_acc[...] = jnp.zeros_like(dv_acc)

            def operands(i):
                qs = pl.multiple_of(i * t, t)
                return [q_heads[h, pl.ds(qs, t), :] for h in (0, 1)], [do_heads[h, pl.ds(qs, t), :] for h in (0, 1)]

            def products(i):
                qih, doih = operands(i)
                return tuple(_dot_nt(kj, qih[h]) for h in (0, 1)) + tuple(_dot_nt(vj, doih[h]) for h in (0, 1))

            def q_tile(n, carry, last):
                i = nq - 1 - n
                s0, s1, dp0, dp1 = carry
                ahead = () if last else products(i - 1)
                qs = pl.multiple_of(i * t, t)
                qih, doih = operands(i)
                dq_i = jnp.zeros((t, width), F32)
                for h, (s, dp) in enumerate(((s0, dp0), (s1, dp1))):
                    if dilated:
                        s = s + bias_ref[h, i - j]
                    else:
                        s = s * scale
                        if last:
                            s = jnp.where(causal, s, NEG)
                    p = jnp.exp(s - lse_ref[h:h + 1, pl.ds(qs, t)])
                    ds = p * (dp - rowdot[h:h + 1, pl.ds(qs, t)])
                    ds = (ds if dilated else ds * scale).astype(BF16)
                    dv_acc[...] += _dot(p.astype(BF16), doih[h])
                    dk_acc[...] += _dot(ds, qih[h])
                    dq_i = dq_i + _dot_tn(ds, kh[h])
                dq_acc[pl.ds(qs, t), :] += dq_i
                return ahead

            q_tile(nq - 1 - j, lax.fori_loop(0, nq - 1 - j, functools.partial(q_tile, last=False), products(nq - 1)), True)
            dk_ref[pl.ds(ks, t), :] = dk_acc[...].astype(BF16)
            dv_ref[pl.ds(ks, t), :] = dv_acc[...].astype(BF16)
            return 0

        lax.fori_loop(0, nq, k_tile, 0)
        dq_ref[...] = (dq_acc[...] * pre).astype(BF16)
        if plan:
            pl.when(step_no == steps - 1)(lambda: comm.finish(*plan))

    tokens = batch * seq
    bias_spec = (pl.BlockSpec((2, nq, t, t), lambda b, p: (p, 0, 0, 0)) if dilated
                 else pl.BlockSpec((None, 8, LANES), lambda b, p: (0, 0, 0)))
    n = comm.n if comm else 0
    return pl.pallas_call(
        body, name=name, grid=(batch, N_PAIRS),
        out_shape=[jax.ShapeDtypeStruct((tokens, N_PAIRS * width), BF16), jax.ShapeDtypeStruct((tokens, N_PAIRS * width), BF16),
                   jax.ShapeDtypeStruct((tokens, DIL_WIDTH), BF16)] + (comm.out_shape if comm else []),
        in_specs=[pl.BlockSpec((seq, width), lambda b, p: (b, cq + p)),
                  pl.BlockSpec((seq, width), lambda b, p: (b, ck + p)),
                  pl.BlockSpec((seq, LANES), lambda b, p: (b, cv + p)),
                  pl.BlockSpec((seq, LANES), lambda b, p: (b, p)),
                  pl.BlockSpec((seq, LANES), lambda b, p: (b, p)),
                  pl.BlockSpec((None, 8, seq), lambda b, p: (b * N_PAIRS + p, 0, 0)),
                  bias_spec] + [ANY] * n,
        out_specs=[pl.BlockSpec((seq, width), lambda b, p: (b, p)),
                   pl.BlockSpec((seq, width), lambda b, p: (b, p)),
                   pl.BlockSpec((seq, LANES), lambda b, p: (b, p))] + [ANY] * n,
        scratch_shapes=[pltpu.VMEM((seq, width), F32), pltpu.VMEM((t, width), F32), pltpu.VMEM((t, LANES), F32),
                        pltpu.VMEM((8, seq), F32), pltpu.VMEM((2, seq, width), BF16), pltpu.VMEM((2, seq, LANES), BF16)]
        + (comm.scratch if comm else []),
        compiler_params=_cp("arbitrary", "arbitrary") if comm else _cp("parallel", "parallel"),
    )(q, k, v, o, do, lse, bias, *comm_arrays)


def _rms(xf, g):
    r = lax.rsqrt(jnp.mean(xf * xf, axis=1, keepdims=True) + RMS_EPS)
    return xf * r * g, r


def _rms_bwd(dy, xf, r, g):
    gy = dy * g
    dx = r * gy - xf * (r * r * r) * jnp.mean(gy * xf, axis=1, keepdims=True)
    return dx, dy * xf * r


def _ln_bwd(dy, xhat, rstd, g):
    dxh = dy * g
    return rstd * (dxh - jnp.mean(dxh, axis=1, keepdims=True) - xhat * jnp.mean(dxh * xhat, axis=1, keepdims=True))


def _fwd_proj(xb, w_in_ext, w1, w2, wk_ext, wv, e128, g_q, g_kv, cext, sext, cs128, *, seq):
    tokens = xb.shape[0]
    tm = min(TOKEN_TILE, seq)
    ns = seq // tm

    def body(x_ref, win_ref, w1_ref, w2_ref, wk_ref, wv_ref, e_ref, gq_ref, gkv_ref, c_ref, s_ref, cs_ref,
             low_ref, gates_ref, qkvd_ref, qp_ref, kp_ref, vm_ref, qn_ref, kvn_ref):
        xt = x_ref[...]
        low = _dot(xt, win_ref[:, 0:LOW_W])
        low_ref[...] = low
        qkvd_ref[...] = _dot(xt, win_ref[:, LOW_W:LOW_W + 3 * DIL_WIDTH]).astype(BF16)
        gates_ref[...] = _dot(xt, win_ref[:, LOW_W + 3 * DIL_WIDTH:]).astype(BF16)
        qn = _rms(low[:, 0:Q_LORA], gq_ref[...])[0].astype(BF16)
        kvn = _rms(low[:, Q_LORA:Q_LORA + KV_LORA], gkv_ref[...])[0].astype(BF16)
        qn_ref[...] = qn
        kvn_ref[...] = kvn
        cos, sin = (jnp.concatenate([r[...]] * N_PAIRS, axis=1) for r in (c_ref, s_ref))
        qp_ref[...] = (_dot(qn, w1_ref[...]) * cos + _dot(qn, w2_ref[...]) * sin).astype(BF16)
        kr = low[:, Q_LORA + KV_LORA:] * cs_ref[...]
        kr = kr + pltpu.roll(kr, LANES - ROPE, 1)
        lane = lax.broadcasted_iota(I32, kr.shape, 1)
        kr = jnp.where(lane < ROPE, kr, 0.0).astype(BF16)
        kp_ref[...] = (_dot(kvn, wk_ref[...]) + _dot(kr, e_ref[...])).astype(BF16)
        vm_ref[...] = _dot(kvn, wv_ref[...]).astype(BF16)

    n_gates = 2 * D_MODEL
    outs = [(LOW_W, F32), (n_gates, BF16), (3 * DIL_WIDTH, BF16), (N_PAIRS * PAIR_W, BF16), (N_PAIRS * PAIR_W, BF16),
            (DIL_WIDTH, BF16), (Q_LORA, BF16), (KV_LORA, BF16)]
    return pl.pallas_call(
        body, name="fwd_proj", grid=(tokens // tm,),
        out_shape=tuple(jax.ShapeDtypeStruct((tokens, w), dt) for w, dt in outs),
        in_specs=[_rows(tm, D_MODEL), _full(w_in_ext.shape), _full(w1.shape), _full(w2.shape), _full(wk_ext.shape),
                  _full(wv.shape), _full(e128.shape), _full(g_q.shape), _full(g_kv.shape),
                  pl.BlockSpec((tm, PAIR_W), lambda i: (i % ns, 0)),
                  pl.BlockSpec((tm, PAIR_W), lambda i: (i % ns, 0)),
                  pl.BlockSpec((tm, LANES), lambda i: (i % ns, 0))],
        out_specs=tuple(_rows(tm, w) for w, _ in outs),
        compiler_params=_cp("parallel"),
    )(xb, w_in_ext, w1, w2, wk_ext, wv, e128, g_q, g_kv, cext, sext, cs128)


def _fwd_mix(o_a, o_b, gates, x, b_gate, w_oa, w_ob, w_out, ln_g, ln_b, *, seq):
    tokens = x.shape[0]
    tm = min(TOKEN_TILE, seq)

    def body(oa_ref, ob_ref, gt_ref, x_ref, bg_ref, woa_ref, wob_ref, wout_ref, g_ref, b_ref,
             hb_ref, xhat_ref, rstd_ref, ya_ref, yb_ref, mix_ref):
        ya = _dot(oa_ref[...], woa_ref[...])
        yb = _dot(ob_ref[...], wob_ref[...])
        g0 = _sigmoid(gt_ref[:, 0:D_MODEL].astype(F32) + bg_ref[0:1, :])
        g1 = _sigmoid(gt_ref[:, D_MODEL:].astype(F32) + bg_ref[1:2, :])
        mix = (g0 * ya + g1 * yb).astype(BF16)
        z = ALPHA * x_ref[...] + _dot(mix, wout_ref[...])
        zc = z - jnp.mean(z, axis=1, keepdims=True)
        rstd = lax.rsqrt(jnp.mean(zc * zc, axis=1, keepdims=True) + LN_EPS)
        xhat = zc * rstd
        hb_ref[...] = (xhat * g_ref[...] + b_ref[...]).astype(BF16)
        xhat_ref[...] = xhat
        rstd_ref[...] = jnp.broadcast_to(rstd, (tm, LANES))
        ya_ref[...] = ya.astype(BF16)
        yb_ref[...] = yb.astype(BF16)
        mix_ref[...] = mix

    outs = [(D_MODEL, BF16), (D_MODEL, F32), (LANES, F32), (D_MODEL, BF16), (D_MODEL, BF16), (D_MODEL, BF16)]
    return pl.pallas_call(
        body, name="fwd_mix", grid=(tokens // tm,),
        out_shape=tuple(jax.ShapeDtypeStruct((tokens, w), dt) for w, dt in outs),
        in_specs=[_rows(tm, DIL_WIDTH), _rows(tm, DIL_WIDTH), _rows(tm, 2 * D_MODEL), _rows(tm, D_MODEL),
                  _full(b_gate.shape), _full(w_oa.shape), _full(w_ob.shape), _full(w_out.shape),
                  _full(ln_g.shape), _full(ln_b.shape)],
        out_specs=tuple(_rows(tm, w) for w, _ in outs),
        compiler_params=_cp("parallel"),
    )(o_a, o_b, gates, x, b_gate, w_oa, w_ob, w_out, ln_g, ln_b)


def _fwd_mlp(hb, xhat1, target, w_ff1, w_ff2, ln1_g, ln1_b, ln_g, ln_b, *, seq):
    tokens = hb.shape[0]
    tm = min(2 * TOKEN_TILE, seq)
    tf = FF_SHARD
    nf = N_DEV // FF_STEP

    def body(hb_ref, xh_ref, tg_ref, w1_ref, w2_ref, g1_ref, b1_ref, g_ref, b_ref, u_ref, dz_ref, dzb_ref, stat_ref, acc):
        i, j = pl.program_id(0), pl.program_id(1)

        @pl.when((i == 0) & (j == 0))
        def _():
            stat_ref[...] = jnp.zeros_like(stat_ref)

        @pl.when(j == 0)
        def _():
            acc[...] = jnp.zeros_like(acc)

        acts = []
        for s in range(FF_STEP):
            u = _dot(hb_ref[...], w1_ref[s])
            u_ref[:, s * tf:(s + 1) * tf] = u.astype(BF16)
            acts.append(jnp.square(jnp.maximum(u, 0.0)).astype(BF16))
        acc[...] += _dot(jnp.concatenate(acts, axis=1), w2_ref[...])

        @pl.when(j == nf - 1)
        def _():
            z = ALPHA * (xh_ref[...] * g1_ref[...] + b1_ref[...]) + acc[...]
            zc = z - jnp.mean(z, axis=1, keepdims=True)
            rstd = lax.rsqrt(jnp.mean(zc * zc, axis=1, keepdims=True) + LN_EPS)
            xhat = zc * rstd
            err = xhat * g_ref[...] + b_ref[...] - tg_ref[...]
            dy = err * (1.0 / D_MODEL)
            dz = _ln_bwd(dy, xhat, rstd, g_ref[...])
            dz_ref[...] = dz
            dzb_ref[...] = dz.astype(BF16)
            stat_ref[0:1, :] += jnp.sum(dy * xhat, axis=0, keepdims=True)
            stat_ref[1:2, :] += jnp.sum(dy, axis=0, keepdims=True)
            stat_ref[2:3, :] += jnp.sum(jnp.sum(err * err, axis=1, keepdims=True), axis=0, keepdims=True) * (0.5 / D_MODEL)

    return pl.pallas_call(
        body, name="fwd_mlp", grid=(tokens // tm, nf),
        out_shape=(jax.ShapeDtypeStruct((tokens, D_FF), BF16), jax.ShapeDtypeStruct((tokens, D_MODEL), F32),
                   jax.ShapeDtypeStruct((tokens, D_MODEL), BF16), jax.ShapeDtypeStruct((8, D_MODEL), F32)),
        in_specs=[_rows(tm, D_MODEL), _rows(tm, D_MODEL), _rows(tm, D_MODEL),
                  pl.BlockSpec((FF_STEP, D_MODEL, tf), lambda i, j: (j, 0, 0)),
                  pl.BlockSpec((FF_STEP * tf, D_MODEL), lambda i, j: (j, 0)),
                  _full(ln1_g.shape), _full(ln1_b.shape), _full(ln_g.shape), _full(ln_b.shape)],
        out_specs=(pl.BlockSpec((tm, FF_STEP * tf), lambda i, j: (i, j)), _rows(tm, D_MODEL), _rows(tm, D_MODEL),
                   _full((8, D_MODEL))),
        scratch_shapes=[pltpu.VMEM((tm, D_MODEL), F32)],
        compiler_params=_cp("arbitrary", "arbitrary"),
    )(hb, xhat1, target, w_ff1, w_ff2, ln1_g, ln1_b, ln_g, ln_b)


def _bwd_mlp(dz2, dz2b, u, xhat1, rstd1, w_ff1, w_ff2, ln_g, *, seq):
    tokens = dz2.shape[0]
    tm = min(2 * TOKEN_TILE, seq)
    tf = FF_SHARD
    nf = N_DEV // FF_STEP

    def body(dz_ref, dzb_ref, u_ref, xh_ref, rs_ref, w1_ref, w2_ref, g_ref, du_ref, dz1_ref, dz1b_ref, stat_ref, acc):
        i, j = pl.program_id(0), pl.program_id(1)

        @pl.when((i == 0) & (j == 0))
        def _():
            stat_ref[...] = jnp.zeros_like(stat_ref)

        @pl.when(j == 0)
        def _():
            acc[...] = jnp.zeros_like(acc)

        da = _dot_nt(dzb_ref[...], w2_ref[...])
        du = (da * (2.0 * jnp.maximum(u_ref[...].astype(F32), 0.0))).astype(BF16)
        du_ref[...] = du
        part = _dot_nt(du[:, 0:tf], w1_ref[0])
        for s in range(1, FF_STEP):
            part = part + _dot_nt(du[:, s * tf:(s + 1) * tf], w1_ref[s])
        acc[...] += part

        @pl.when(j == nf - 1)
        def _():
            dh = ALPHA * dz_ref[...] + acc[...]
            xhat = xh_ref[...]
            dz1 = _ln_bwd(dh, xhat, rs_ref[:, 0:1], g_ref[...])
            dz1_ref[...] = dz1
            dz1b_ref[...] = dz1.astype(BF16)
            stat_ref[0:1, :] += jnp.sum(dh * xhat, axis=0, keepdims=True)
            stat_ref[1:2, :] += jnp.sum(dh, axis=0, keepdims=True)

    return pl.pallas_call(
        body, name="bwd_mlp", grid=(tokens // tm, nf),
        out_shape=(jax.ShapeDtypeStruct((tokens, D_FF), BF16), jax.ShapeDtypeStruct((tokens, D_MODEL), F32),
                   jax.ShapeDtypeStruct((tokens, D_MODEL), BF16), jax.ShapeDtypeStruct((8, D_MODEL), F32)),
        in_specs=[_rows(tm, D_MODEL), _rows(tm, D_MODEL), pl.BlockSpec((tm, FF_STEP * tf), lambda i, j: (i, j)),
                  _rows(tm, D_MODEL), _rows(tm, LANES),
                  pl.BlockSpec((FF_STEP, D_MODEL, tf), lambda i, j: (j, 0, 0)),
                  pl.BlockSpec((FF_STEP * tf, D_MODEL), lambda i, j: (j, 0)),
                  _full(ln_g.shape)],
        out_specs=(pl.BlockSpec((tm, FF_STEP * tf), lambda i, j: (i, j)), _rows(tm, D_MODEL), _rows(tm, D_MODEL),
                   _full((8, D_MODEL))),
        scratch_shapes=[pltpu.VMEM((tm, D_MODEL), F32)],
        compiler_params=_cp("arbitrary", "arbitrary"),
    )(dz2, dz2b, u, xhat1, rstd1, w_ff1, w_ff2, ln_g)


def _bwd_mix(dz1b, gates, y_a, y_b, b_gate, w_oa, w_ob, w_out, *, seq):
    tokens = dz1b.shape[0]
    tm = min(TOKEN_TILE, seq)

    def body(dz_ref, gt_ref, ya_ref, yb_ref, bg_ref, woa_ref, wob_ref, wout_ref,
             dgt_ref, dya_ref, dyb_ref, doa_ref, dob_ref, stat_ref):
        @pl.when(pl.program_id(0) == 0)
        def _():
            stat_ref[...] = jnp.zeros_like(stat_ref)

        dmix = _dot_nt(dz_ref[...], wout_ref[...])
        for k, (y_ref, w_ref, dy_ref, do_ref) in enumerate(((ya_ref, woa_ref, dya_ref, doa_ref), (yb_ref, wob_ref, dyb_ref, dob_ref))):
            g = _sigmoid(gt_ref[:, k * D_MODEL:(k + 1) * D_MODEL].astype(F32) + bg_ref[k:k + 1, :])
            dgate = dmix * y_ref[...].astype(F32) * g * (1.0 - g)
            dgt_ref[:, k * D_MODEL:(k + 1) * D_MODEL] = dgate.astype(BF16)
            stat_ref[k:k + 1, :] += jnp.sum(dgate, axis=0, keepdims=True)
            dy = (dmix * g).astype(BF16)
            dy_ref[...] = dy
            do_ref[...] = _dot_nt(dy, w_ref[...]).astype(BF16)

    outs = [(2 * D_MODEL, BF16), (D_MODEL, BF16), (D_MODEL, BF16), (DIL_WIDTH, BF16), (DIL_WIDTH, BF16)]
    return pl.pallas_call(
        body, name="bwd_mix", grid=(tokens // tm,),
        out_shape=tuple(jax.ShapeDtypeStruct((tokens, w), dt) for w, dt in outs) + (jax.ShapeDtypeStruct((8, D_MODEL), F32),),
        in_specs=[_rows(tm, D_MODEL), _rows(tm, 2 * D_MODEL), _rows(tm, D_MODEL), _rows(tm, D_MODEL),
                  _full(b_gate.shape), _full(w_oa.shape), _full(w_ob.shape), _full(w_out.shape)],
        out_specs=tuple(_rows(tm, w) for w, _ in outs) + (_full((8, D_MODEL)),),
        compiler_params=_cp("arbitrary"),
    )(dz1b, gates, y_a, y_b, b_gate, w_oa, w_ob, w_out)


def _bwd_proj(dqp, dkp, dvm, dq_d, dk_d, dv_d, dgates, dz1, low, w_in_ext, w1, w2, wk_ext, wv, e128, g_q, g_kv, cext, sext, cs128, *, seq):
    tokens = dz1.shape[0]
    tm = min(TOKEN_TILE, seq)
    ns = seq // tm

    def body(dqp_ref, dkp_ref, dvm_ref, dqd_ref, dkd_ref, dvd_ref, dgt_ref, dz_ref, low_ref, win_ref, w1_ref, w2_ref, wk_ref,
             wv_ref, e_ref, gq_ref, gkv_ref, c_ref, s_ref, cs_ref, dx_ref, dproj_ref, da_ref, db_ref, stat_ref):
        @pl.when(pl.program_id(0) == 0)
        def _():
            stat_ref[...] = jnp.zeros_like(stat_ref)

        low = low_ref[...]
        dqp = dqp_ref[...].astype(F32)
        cos, sin = (jnp.concatenate([r[...]] * N_PAIRS, axis=1) for r in (c_ref, s_ref))
        d_a = (dqp * cos).astype(BF16)
        d_b = (dqp * sin).astype(BF16)
        da_ref[...] = d_a
        db_ref[...] = d_b
        q_a = low[:, 0:Q_LORA]
        _, rq = _rms(q_a, gq_ref[...])
        dq_a, gq_terms = _rms_bwd(_dot_nt(d_a, w1_ref[...]) + _dot_nt(d_b, w2_ref[...]), q_a, rq, gq_ref[...])
        kv_a = low[:, Q_LORA:Q_LORA + KV_LORA]
        _, rkv = _rms(kv_a, gkv_ref[...])
        dkp = dkp_ref[...]
        dkv_a, gkv_terms = _rms_bwd(_dot_nt(dkp, wk_ref[...]) + _dot_nt(dvm_ref[...], wv_ref[...]), kv_a, rkv, gkv_ref[...])
        dkr = _dot_nt(dkp, e_ref[...])
        dkr = (dkr + pltpu.roll(dkr, ROPE, 1)) * cs_ref[...]
        stat_ref[0:1, 0:Q_LORA] += jnp.sum(gq_terms, axis=0, keepdims=True)
        stat_ref[1:2, 0:KV_LORA] += jnp.sum(gkv_terms, axis=0, keepdims=True)
        dproj_ref[:, 0:Q_LORA] = dq_a.astype(BF16)
        dproj_ref[:, Q_LORA:Q_LORA + KV_LORA] = dkv_a.astype(BF16)
        dproj_ref[:, Q_LORA + KV_LORA:LOW_W] = dkr.astype(BF16)
        dproj_ref[:, LOW_W:LOW_W + DIL_WIDTH] = dqd_ref[...]
        dproj_ref[:, LOW_W + DIL_WIDTH:LOW_W + 2 * DIL_WIDTH] = dkd_ref[...]
        dproj_ref[:, LOW_W + 2 * DIL_WIDTH:LOW_W + 3 * DIL_WIDTH] = dvd_ref[...]
        dproj_ref[:, LOW_W + 3 * DIL_WIDTH:] = dgt_ref[...]
        dx_ref[...] = ALPHA * dz_ref[...] + _dot_nt(dproj_ref[...], win_ref[...])

    wide = N_PAIRS * PAIR_W
    return pl.pallas_call(
        body, name="bwd_proj", grid=(tokens // tm,),
        out_shape=(jax.ShapeDtypeStruct((tokens, D_MODEL), F32), jax.ShapeDtypeStruct((tokens, IN_EXT), BF16),
                   jax.ShapeDtypeStruct((tokens, wide), BF16), jax.ShapeDtypeStruct((tokens, wide), BF16),
                   jax.ShapeDtypeStruct((8, D_MODEL), F32)),
        in_specs=[_rows(tm, wide), _rows(tm, wide), _rows(tm, DIL_WIDTH), _rows(tm, DIL_WIDTH), _rows(tm, DIL_WIDTH),
                  _rows(tm, DIL_WIDTH), _rows(tm, 2 * D_MODEL),
                  _rows(tm, D_MODEL), _rows(tm, LOW_W), _full(w_in_ext.shape), _full(w1.shape), _full(w2.shape),
                  _full(wk_ext.shape), _full(wv.shape), _full(e128.shape), _full(g_q.shape), _full(g_kv.shape),
                  pl.BlockSpec((tm, PAIR_W), lambda i: (i % ns, 0)), pl.BlockSpec((tm, PAIR_W), lambda i: (i % ns, 0)),
                  pl.BlockSpec((tm, LANES), lambda i: (i % ns, 0))],
        out_specs=(_rows(tm, D_MODEL), _rows(tm, IN_EXT), _rows(tm, wide), _rows(tm, wide), _full((8, D_MODEL))),
        compiler_params=_cp("arbitrary"),
    )(dqp, dkp, dvm, dq_d, dk_d, dv_d, dgates, dz1, low, w_in_ext, w1, w2, wk_ext, wv, e128, g_q, g_kv, cext, sext, cs128)


def _wgrad(a, b, name, square_relu=False, by_shard=False):
    tokens, ka = a.shape
    n = b.shape[1]
    tka = min(ka, 512)
    shard = n // N_DEV
    tn = WGRAD_SHARDS * shard if by_shard else max(w for w in range(LANES, min(n, 2304) + 1, LANES) if n % w == 0)
    tt = min(tokens, 1024)
    nt = tokens // tt

    def body(a_ref, b_ref, o_ref, acc):
        kt = pl.program_id(2)

        @pl.when(kt == 0)
        def _():
            acc[...] = jnp.zeros_like(acc)

        at = a_ref[...]
        if square_relu:
            at = jnp.square(jnp.maximum(at.astype(F32), 0.0)).astype(BF16)
        acc[...] += _dot_tn(at, b_ref[...])

        @pl.when(kt == nt - 1)
        def _():
            if by_shard:
                for s in range(WGRAD_SHARDS):
                    o_ref[s] = acc[:, s * shard:(s + 1) * shard].astype(BF16)
            else:
                o_ref[...] = acc[...].astype(BF16)

    if by_shard:
        out_shape, out_spec = (N_DEV, ka, shard), pl.BlockSpec((WGRAD_SHARDS, tka, shard), lambda i, j, k: (j, i, 0))
    else:
        out_shape, out_spec = (ka, n), pl.BlockSpec((tka, tn), lambda i, j, k: (i, j))
    return pl.pallas_call(
        body, name=name, grid=(ka // tka, n // tn, nt), out_shape=jax.ShapeDtypeStruct(out_shape, BF16),
        in_specs=[pl.BlockSpec((tt, tka), lambda i, j, k: (k, i)), pl.BlockSpec((tt, tn), lambda i, j, k: (k, j))],
        out_specs=out_spec,
        scratch_shapes=[pltpu.VMEM((tka, tn), F32)],
        compiler_params=_cp("parallel", "parallel", "arbitrary"),
    )(a, b)


def _adam_math(w, g, m, v):
    m = ADAM_B1 * m + (1.0 - ADAM_B1) * g
    v = ADAM_B2 * v + (1.0 - ADAM_B2) * jnp.square(g)
    m_hat = m / (1.0 - ADAM_B1 ** ADAM_STEP)
    v_hat = v / (1.0 - ADAM_B2 ** ADAM_STEP)
    return -ADAM_LR * (m_hat / (jnp.sqrt(v_hat) + ADAM_EPS) + ADAM_WD * w), m, v


def _adamw(w, m, v, own, parts, name):
    rows, cols = w.shape
    tr = _row_tile(rows)
    n_parts = parts.shape[0]

    def body(slot_ref, w_ref, m_ref, v_ref, own_ref, p_ref, g_ref, d_ref, nm_ref, nv_ref):
        g = own_ref[...].astype(F32)
        for d in range(n_parts):
            g = g + p_ref[d].astype(F32)
        g_ref[...] = g
        d_ref[...], nm_ref[...], nv_ref[...] = _adam_math(w_ref[...], g, m_ref[...], v_ref[...])

    x, y, c = _place()
    blk = pl.BlockSpec((tr, cols), lambda i, slot: (i, 0))
    own_blk = blk if own.ndim == 2 else pl.BlockSpec((None, tr, cols), lambda i, slot: (slot[0], i, 0))
    return pl.pallas_call(
        body, name=name,
        grid_spec=pltpu.PrefetchScalarGridSpec(
            num_scalar_prefetch=1, grid=(rows // tr,),
            in_specs=[blk, blk, blk, own_blk, pl.BlockSpec((n_parts, tr, cols), lambda i, slot: (0, i, 0))],
            out_specs=(blk,) * 4),
        out_shape=(jax.ShapeDtypeStruct((rows, cols), F32),) * 4, compiler_params=_cp("parallel"),
    )(jnp.reshape(4 * x + 2 * y + c, (1,)).astype(I32), w, m, v, own, parts)


def _adamw_small(parts, w, m, v):
    _, rows, cols = parts.shape

    def body(p_ref, w_ref, m_ref, v_ref, g_ref, d_ref, nm_ref, nv_ref):
        g = p_ref[0]
        for d in range(1, N_DEV):
            g = g + p_ref[d]
        g_ref[...] = g
        d_ref[...], nm_ref[...], nv_ref[...] = _adam_math(w_ref[...], g, m_ref[...], v_ref[...])

    return pl.pallas_call(
        body, name="adamw_replicated", out_shape=(jax.ShapeDtypeStruct((rows, cols), F32),) * 4,
        in_specs=[_full(parts.shape)] + [_full((rows, cols))] * 3, out_specs=(_full((rows, cols)),) * 4, grid=(1,),
        compiler_params=_cp("arbitrary"),
    )(parts, w, m, v)


def _pad_rows(a2d, mult):
    pad = (-a2d.shape[-2]) % mult
    return jnp.pad(a2d, [(0, 0)] * (a2d.ndim - 2) + [(0, pad), (0, 0)]) if pad else a2d


def _pad_cols(a):
    pad = (-a.shape[-1]) % LANES
    return jnp.pad(a, [(0, 0)] * (a.ndim - 1) + [(0, pad)]) if pad else a


def _rot_cols(w):
    half = ROPE // 2
    return jnp.concatenate([-w[..., half:], w[..., :half]], axis=-1)


def _unrot_cols(dw):
    half = ROPE // 2
    return jnp.concatenate([dw[..., half:], -dw[..., :half]], axis=-1)


def _from_col_shards(stacked):
    return stacked.transpose(1, 0, 2).reshape(stacked.shape[1], -1)


def _to_col_shards(full):
    r = full.shape[0]
    return full.reshape(r, N_DEV, -1).transpose(1, 0, 2)


def _rope_tables(seq):
    half = ROPE // 2
    inv = jnp.power(ROPE_THETA, -jnp.arange(half, dtype=F32) / half)
    ang = jnp.arange(seq, dtype=F32)[:, None] * inv[None, :]
    cos = jnp.concatenate([jnp.cos(ang)] * 2, axis=1)
    sin = jnp.concatenate([jnp.sin(ang)] * 2, axis=1)
    ones, zeros = jnp.ones((seq, 2 * NOPE), F32), jnp.zeros((seq, 2 * NOPE), F32)
    pad = jnp.zeros((seq, PAIR_W - 2 * NOPE - 2 * ROPE), F32)
    cext = jnp.concatenate([ones, cos, cos, pad], axis=1)
    sext = jnp.concatenate([zeros, sin, sin, pad], axis=1)
    cs128 = jnp.concatenate([cos, sin, jnp.zeros((seq, LANES - 2 * ROPE), F32)], axis=1)
    return cext, sext, cs128


def _pair_slabs(nope, rope):
    k = nope.shape[0]
    nope = nope.reshape(k, N_PAIRS, 2 * NOPE)
    rope = jnp.zeros((k, N_PAIRS, 2 * ROPE), nope.dtype) if rope is None else rope.reshape(k, N_PAIRS, 2 * ROPE)
    pad = jnp.zeros((k, N_PAIRS, PAIR_W - 2 * NOPE - 2 * ROPE), nope.dtype)
    return jnp.concatenate([nope, rope, pad], axis=2).reshape(k, N_PAIRS * PAIR_W)


def _split_slabs(slabs):
    k = slabs.shape[0]
    s = slabs.reshape(k, N_PAIRS, PAIR_W)
    return s[:, :, :2 * NOPE].reshape(k, N_HEADS, NOPE), s[:, :, 2 * NOPE:2 * NOPE + 2 * ROPE].reshape(k, N_HEADS, ROPE)


def kernel(x, w_in, b_gate, g_q_a, w_uq, g_kv_a, w_ukv, w_o_mla, w_o_dil, w_out, ln1_g, ln1_b, w_ff1, w_ff2, ln2_g, ln2_b, loss_target, m_w_in, m_b_gate, m_g_q_a, m_w_uq, m_g_kv_a, m_w_ukv, m_w_o_mla, m_w_o_dil, m_w_out, m_ln1_g, m_ln1_b, m_w_ff1, m_w_ff2, m_ln2_g, m_ln2_b, v_w_in, v_b_gate, v_g_q_a, v_w_uq, v_g_kv_a, v_w_ukv, v_w_o_mla, v_w_o_dil, v_w_out, v_ln1_g, v_ln1_b, v_w_ff1, v_w_ff2, v_ln2_g, v_ln2_b):
    batch, seq, _ = x.shape
    tokens = batch * seq
    weights = dict(w_in=w_in, w_uq=w_uq, w_ukv=w_ukv, w_o_mla=w_o_mla, w_o_dil=w_o_dil, w_out=w_out, w_ff1=w_ff1, w_ff2=w_ff2, b_gate=b_gate)
    mom_m = dict(w_in=m_w_in, w_uq=m_w_uq, w_ukv=m_w_ukv, w_o_mla=m_w_o_mla, w_o_dil=m_w_o_dil, w_out=m_w_out, w_ff1=m_w_ff1, w_ff2=m_w_ff2, b_gate=m_b_gate)
    mom_v = dict(w_in=v_w_in, w_uq=v_w_uq, w_ukv=v_w_ukv, w_o_mla=v_w_o_mla, w_o_dil=v_w_o_dil, w_out=v_w_out, w_ff1=v_w_ff1, w_ff2=v_w_ff2, b_gate=v_b_gate)

    first = ["w_in", "w_uq", "w_ukv"]
    widths = [weights[n].shape[2] for n in first]
    shards = [_pad_cols(weights[n][0].astype(BF16)) for n in first]
    (g_in, g_uq, g_ukv), = _run_comm([(_Gather(shards), shards)], "all_gather_first_weights")
    g_uq, g_ukv = g_uq[:, :, :widths[1]], g_ukv[:, :, :widths[2]]

    s1, s2, n_in = Q_LORA + KV_LORA, Q_LORA + KV_LORA + ROPE, N_DEV * widths[0]

    def w_in_cols(lo, hi):
        out = []
        while lo < hi:
            d, off = divmod(lo, widths[0])
            take = min(hi - lo, widths[0] - off)
            out.append(g_in[d][:, off:off + take])
            lo += take
        return out

    w_in_ext = jnp.concatenate(w_in_cols(0, s2) + [_rot_cols(jnp.concatenate(w_in_cols(s1, s2), axis=1)),
                                                   jnp.zeros((D_MODEL, LOW_W - s2 - ROPE), BF16)] + w_in_cols(s2, n_in), axis=1)
    uq = _from_col_shards(g_uq).reshape(Q_LORA, N_HEADS, NOPE + ROPE)
    w1 = _pair_slabs(uq[:, :, :NOPE], uq[:, :, NOPE:])
    w2 = _pair_slabs(jnp.zeros_like(uq[:, :, :NOPE]), _rot_cols(uq[:, :, NOPE:]))
    ukv = _from_col_shards(g_ukv).reshape(KV_LORA, N_HEADS, NOPE + HEAD_V)
    wk_ext = _pair_slabs(ukv[:, :, :NOPE], None)
    wv = ukv[:, :, NOPE:].reshape(KV_LORA, N_HEADS * HEAD_V)
    eye = jnp.eye(ROPE, dtype=BF16)
    e_slab = jnp.concatenate([jnp.zeros((ROPE, 2 * NOPE), BF16), eye, eye, jnp.zeros((ROPE, PAIR_W - 2 * NOPE - 2 * ROPE), BF16)], axis=1)
    e128 = jnp.concatenate([jnp.tile(e_slab, (1, N_PAIRS)), jnp.zeros((LANES - ROPE, N_PAIRS * PAIR_W), BF16)], axis=0)
    cext, sext, cs128 = _rope_tables(seq)
    dil_bias = _dilated_bias_table(seq)
    no_bias = jnp.zeros((1, 8, LANES), F32)

    x2 = x.reshape(tokens, D_MODEL)
    xb = x2.astype(BF16)
    low, gates, qkvd, qp, kp, vm, qn, kvn = _fwd_proj(xb, w_in_ext, w1, w2, wk_ext, wv, e128, g_q_a, g_kv_a, cext, sext, cs128, seq=seq)
    bg = b_gate[0]
    bg_hi = bg.astype(BF16)
    bg_lo = (bg - bg_hi.astype(F32)).astype(BF16)
    later = [weights[n][0].astype(BF16) for n in ("w_o_mla", "w_o_dil", "w_out", "w_ff1", "w_ff2")]
    later.append(_pad_rows(jnp.concatenate([bg_hi, bg_lo], axis=0), 16))
    mla = dict(batch=batch, seq=seq, width=PAIR_W, col0=(0, 0, 0), dilated=False, scale=MLA_SCALE)
    dil = dict(batch=batch, seq=seq, width=LANES, col0=(0, N_PAIRS, 2 * N_PAIRS), dilated=True, scale=DIL_SCALE)
    o_a, lse_a, g_oa, g_ob, g_out, g_ff1, g_ff2, g_bg = _attn_fwd(
        qp, kp, vm, no_bias, name="mla_attention_fwd", comm=_Gather(later), comm_arrays=later, **mla)
    o_b, lse_b = _attn_fwd(qkvd, qkvd, qkvd, dil_bias, name="dilated_attention_fwd", **dil)
    w_oa, w_ob = _from_col_shards(g_oa), _from_col_shards(g_ob)
    w_out_full = g_out.reshape(D_MODEL, D_MODEL)
    w_ff2_full = g_ff2.reshape(D_FF, D_MODEL)
    bg_parts = g_bg.astype(F32)
    b_gate_full = _from_col_shards(bg_parts[:, 0:2] + bg_parts[:, 2:4])
    hb, xhat1, rstd1, y_a, y_b, mix = _fwd_mix(o_a, o_b, gates, x2, b_gate_full, w_oa, w_ob, w_out_full, ln1_g, ln1_b, seq=seq)
    u, dz2, dz2b, stat2 = _fwd_mlp(hb, xhat1, loss_target.reshape(tokens, D_MODEL), g_ff1, w_ff2_full, ln1_g, ln1_b, ln2_g, ln2_b, seq=seq)

    du, dz1, dz1b, stat1 = _bwd_mlp(dz2, dz2b, u, xhat1, rstd1, g_ff1, w_ff2_full, ln1_g, seq=seq)
    dw_ff = [_wgrad(hb, du, "wgrad_ff1", by_shard=True),
             _wgrad(u, dz2b, "wgrad_ff2", square_relu=True).reshape(N_DEV, FF_SHARD, D_MODEL)]
    dgates, dy_a, dy_b, do_a, do_b, stat_g = _bwd_mix(dz1b, gates, y_a, y_b, b_gate_full, w_oa, w_ob, w_out_full, seq=seq)
    dqp, dkp, dvm, r_ff1, r_ff2 = _attn_bwd(qp, kp, vm, o_a, do_a, lse_a, no_bias, name="mla_attention_bwd",
                                            comm=_Scatter(dw_ff), comm_arrays=dw_ff, **mla)
    dw_mid = [_to_col_shards(_wgrad(o_a, dy_a, "wgrad_o_mla")), _to_col_shards(_wgrad(o_b, dy_b, "wgrad_o_dil")),
              _wgrad(mix, dz1b, "wgrad_out").reshape(N_DEV, D_MODEL // N_DEV, D_MODEL),
              _pad_rows(_to_col_shards(stat_g[0:2]).astype(BF16), 16)]
    dq_d, dk_d, dv_d, r_oa, r_ob, r_out, r_bg = _attn_bwd(qkvd, qkvd, qkvd, o_b, do_b, lse_b, dil_bias, name="dilated_attention_bwd",
                                                          comm=_Scatter(dw_mid), comm_arrays=dw_mid, **dil)
    grad_x, dproj, d_a, d_b, stat_r = _bwd_proj(dqp, dkp, dvm, dq_d, dk_d, dv_d, dgates, dz1, low, w_in_ext, w1, w2, wk_ext, wv,
                                                e128, g_q_a, g_kv_a, cext, sext, cs128, seq=seq)

    dw_in_ext = _wgrad(xb, dproj, "wgrad_in")
    dw1 = _wgrad(qn, d_a, "wgrad_uq_direct")
    dw2 = _wgrad(qn, d_b, "wgrad_uq_rotated")
    dwk = _wgrad(kvn, dkp, "wgrad_ukv_k")
    dwv = _wgrad(kvn, dvm, "wgrad_ukv_v")
    dw_kr = dw_in_ext[:, s1:s2] + _unrot_cols(dw_in_ext[:, s2:s2 + ROPE])

    def dw_in_cols(lo, hi):
        out = []
        for a, b, piece in ((0, s1, lambda u, v: dw_in_ext[:, u:v]), (s1, s2, lambda u, v: dw_kr[:, u - s1:v - s1]),
                            (s2, n_in, lambda u, v: dw_in_ext[:, u + LOW_W - s2:v + LOW_W - s2])):
            if max(lo, a) < min(hi, b):
                out.append(piece(max(lo, a), min(hi, b)))
        return out

    dw_in = jnp.stack([_pad_cols(jnp.concatenate(dw_in_cols(d * widths[0], (d + 1) * widths[0]), axis=1)) for d in range(N_DEV)])
    n1, r1 = _split_slabs(dw1)
    _, r2 = _split_slabs(dw2)
    dw_uq = jnp.concatenate([n1, r1 + _unrot_cols(r2)], axis=2).reshape(Q_LORA, N_HEADS * (NOPE + ROPE))
    nk, _ = _split_slabs(dwk)
    dw_ukv = jnp.concatenate([nk, dwv.reshape(KV_LORA, N_HEADS, HEAD_V)], axis=2).reshape(KV_LORA, N_HEADS * (NOPE + HEAD_V))
    last = [dw_in] + [_pad_cols(_to_col_shards(dw)) for dw in (dw_uq, dw_ukv)]
    theirs = _rs_sibling(last, "rs_last_sibling_exchange")
    sums = [_pair_sum(a, b, "rs_last_pair_sum_" + n) for a, b, n in zip(last, theirs, first)]
    partial = jnp.concatenate([stat_r[0:1, :Q_LORA], stat_r[1:2, :KV_LORA], stat1[0:1], stat1[1:2], stat2[0:1], stat2[1:2],
                               stat2[2:3, :LANES]], axis=1)
    partial = _pad_rows(partial.reshape(-1, LANES), 8)
    rest = [s[1] for s in sums]
    got, (every,) = _run_comm([(_ChipExchange(rest), rest), (_Gather([partial]), [partial])], "rs_last_chip_exchange")

    upd = {}
    for n, w, (own, _), parts in zip(first, widths, sums, got):
        upd[n] = _adamw(weights[n][0], mom_m[n][0], mom_v[n][0], own[:, :w], parts[:, :, :w], "adamw_" + n)
    for n, own, parts in (("w_o_mla", dw_mid[0], r_oa), ("w_o_dil", dw_mid[1], r_ob), ("w_out", dw_mid[2], r_out),
                          ("w_ff1", dw_ff[0], r_ff1), ("w_ff2", dw_ff[1], r_ff2)):
        upd[n] = _adamw(weights[n][0], mom_m[n][0], mom_v[n][0], own, parts, "adamw_" + n)
    bg_upd = _adamw(_pad_rows(b_gate[0], 16), _pad_rows(m_b_gate[0], 16), _pad_rows(v_b_gate[0], 16), dw_mid[3], r_bg, "adamw_b_gate")
    upd["b_gate"] = tuple(t[0:2] for t in bg_upd)

    small_w = [g_q_a, g_kv_a, ln1_g, ln1_b, ln2_g, ln2_b]
    small_m = [m_g_q_a, m_g_kv_a, m_ln1_g, m_ln1_b, m_ln2_g, m_ln2_b]
    small_v = [v_g_q_a, v_g_kv_a, v_ln1_g, v_ln1_b, v_ln2_g, v_ln2_b]
    small_widths = [a.shape[1] for a in small_w]

    def as_rows(vecs, extra):
        flat = jnp.concatenate(vecs + [jnp.zeros((1, extra), F32)], axis=1)
        return _pad_rows(flat.reshape(-1, LANES), 8)

    g_s, d_s, nm_s, nv_s = _adamw_small(every, as_rows(small_w, LANES), as_rows(small_m, LANES), as_rows(small_v, LANES))

    def split_small(a):
        flat = a.reshape(1, -1)
        out, c0 = [], 0
        for w in small_widths:
            out.append(flat[:, c0:c0 + w])
            c0 += w
        return out, flat[0, c0]

    g_small, loss = split_small(g_s)
    small = [g_small, split_small(d_s)[0], split_small(nm_s)[0], split_small(nv_s)[0]]

    order = ["w_in", "b_gate", "g_q_a", "w_uq", "g_kv_a", "w_ukv", "w_o_mla", "w_o_dil", "w_out", "ln1_g", "ln1_b", "w_ff1", "w_ff2", "ln2_g", "ln2_b"]
    small_names = ["g_q_a", "g_kv_a", "ln1_g", "ln1_b", "ln2_g", "ln2_b"]

    def pick(kind):
        return [small[kind][small_names.index(n)] if n in small_names else upd[n][kind][None] for n in order]

    return (loss, grad_x.reshape(batch, seq, D_MODEL), *pick(0), *pick(1), *pick(2), *pick(3))
```

```python
import functools
import math

import jax
import jax.numpy as jnp
from jax import lax
from jax.experimental import pallas as pl
from jax.experimental.pallas import tpu as pltpu

F32 = jnp.float32
BF16 = jnp.bfloat16
I32 = jnp.int32

D_MODEL = 1024
N_HEADS = 8
NOPE = 64
ROPE = 32
HEAD_V = 64
Q_LORA = 384
KV_LORA = 256
DIL_WIDTH = 512
D_FF = 4096
ROPE_THETA = 10000.0
LN_EPS = 1e-5
RMS_EPS = 1e-6
NEG = -1e30
ALPHA = 2.0 ** 0.25
MLA_SCALE = (NOPE + ROPE) ** -0.5
DIL_SCALE = 64 ** -0.5
ADAM_LR, ADAM_B1, ADAM_B2, ADAM_EPS, ADAM_WD, ADAM_STEP = 0.001, 0.9, 0.999, 1e-08, 0.01, 10

LANES = 128
PAIR_W = 256
N_PAIRS = N_HEADS // 2
LOW_W = 768
IN_EXT = LOW_W + 3 * DIL_WIDTH + 2 * D_MODEL
N_DEV = 8
FF_SHARD = D_FF // N_DEV
FF_STEP = 4
WGRAD_SHARDS = 4
TOKEN_TILE = 256
ATTN_TILE = 256
VMEM_LIMIT = 56 << 20

MESH = pl.DeviceIdType.MESH
ANY = pl.BlockSpec(memory_space=pl.ANY)
CHIP_FLIPS = ((0, 0), (0, 1), (1, 0), (1, 1))
PEER_FLIPS = tuple((fx, fy, fc) for fx in (0, 1) for fy in (0, 1) for fc in (0, 1))[1:]


def _cp(*sem):
    return pltpu.CompilerParams(dimension_semantics=sem or None, vmem_limit_bytes=VMEM_LIMIT)


def _full(shape):
    nd = len(shape)
    return pl.BlockSpec(shape, lambda *_: (0,) * nd)


def _rows(tm, width):
    return pl.BlockSpec((tm, width), lambda i, *_: (i, 0))


def _dot(a, b):
    return jnp.dot(a, b, preferred_element_type=F32)


def _dot_nt(a, b):
    return lax.dot_general(a, b, (((1,), (1,)), ((), ())), preferred_element_type=F32)


def _dot_tn(a, b):
    return lax.dot_general(a, b, (((0,), (0,)), ((), ())), preferred_element_type=F32)


def _sigmoid(z):
    return 1.0 / (1.0 + jnp.exp(-z))


def _place():
    return lax.axis_index("x"), lax.axis_index("y"), lax.axis_index("c")


def _flip(v, f):
    return 1 - v if f else v


class _Gather:
    def __init__(self, shards):
        self.n = len(shards)
        self.out_shape = [jax.ShapeDtypeStruct((N_DEV, *s.shape), s.dtype) for s in shards]
        self.scratch = [pltpu.SemaphoreType.DMA((7 * self.n,)), pltpu.SemaphoreType.DMA((7 * self.n,)),
                        pltpu.SemaphoreType.DMA((self.n,))]

    def _copies(self, what, srcs, dsts, send, recv, local):
        x, y, c = _place()
        chips = [(_flip(x, fx), _flip(y, fy)) for fx, fy in CHIP_FLIPS[1:]]
        out = []
        for a in range(self.n):
            def slot(px, py, pc, a=a):
                return dsts[a].at[4 * px + 2 * py + pc]

            def copy(k, block, to, src=None, a=a, slot=slot):
                return pltpu.make_async_remote_copy(
                    src_ref=slot(*block) if src is None else src, dst_ref=slot(*block),
                    send_sem=send.at[7 * a + k], recv_sem=recv.at[7 * a + k], device_id=to, device_id_type=MESH)

            if what == "mine":
                out.append(pltpu.make_async_copy(srcs[a], slot(x, y, c), local.at[a]))
            elif what == "first":
                out.append(copy(0, (x, y, c), (x, y, 1 - c), src=srcs[a]))
                out += [copy(1 + j, (x, y, c), (*chip, c), src=srcs[a]) for j, chip in enumerate(chips)]
            elif what == "landed":
                out += [copy(1 + j, (*chip, c), (x, y, c)) for j, chip in enumerate(chips)]
            elif what == "passed":
                out += [copy(4 + j, (*chip, c), (x, y, 1 - c)) for j, chip in enumerate(chips)]
            else:
                out.append(copy(0, (x, y, 1 - c), (x, y, c)))
                out += [copy(4 + j, (*chip, 1 - c), (x, y, c)) for j, chip in enumerate(chips)]
        return out

    def start(self, *refs):
        for cp in self._copies("first", *refs) + self._copies("mine", *refs):
            cp.start()

    def forward(self, *refs):
        for landed, passed in zip(self._copies("landed", *refs), self._copies("passed", *refs)):
            landed.wait_recv()
            passed.start()

    def finish(self, *refs):
        for cp in self._copies("from_sibling", *refs):
            cp.wait_recv()
        for cp in self._copies("first", *refs) + self._copies("passed", *refs):
            cp.wait_send()
        for cp in self._copies("mine", *refs):
            cp.wait()


class _Scatter:
    def __init__(self, arrays):
        self.n = len(arrays)
        self.out_shape = [jax.ShapeDtypeStruct((7, *a.shape[1:]), a.dtype) for a in arrays]
        self.scratch = [pltpu.SemaphoreType.DMA((7 * self.n,)), pltpu.SemaphoreType.DMA((7 * self.n,))]

    def _copies(self, srcs, dsts, send, recv):
        x, y, c = _place()
        out = []
        for a in range(self.n):
            for k, (fx, fy, fc) in enumerate(PEER_FLIPS):
                px, py, pc = _flip(x, fx), _flip(y, fy), _flip(c, fc)
                out.append(pltpu.make_async_remote_copy(
                    src_ref=srcs[a].at[4 * px + 2 * py + pc], dst_ref=dsts[a].at[k],
                    send_sem=send.at[7 * a + k], recv_sem=recv.at[7 * a + k], device_id=(px, py, pc), device_id_type=MESH))
        return out

    def start(self, *refs):
        for cp in self._copies(*refs):
            cp.start()

    def forward(self, *refs):
        pass

    def finish(self, *refs):
        for cp in self._copies(*refs):
            cp.wait_send()
        for cp in self._copies(*refs):
            cp.wait_recv()


class _ChipExchange:
    def __init__(self, arrays):
        self.n = len(arrays)
        self.out_shape = [jax.ShapeDtypeStruct(a.shape, a.dtype) for a in arrays]
        self.scratch = [pltpu.SemaphoreType.DMA((3 * self.n,)), pltpu.SemaphoreType.DMA((3 * self.n,))]

    def _copies(self, srcs, dsts, send, recv):
        x, y, c = _place()
        return [pltpu.make_async_remote_copy(
            src_ref=srcs[a].at[k], dst_ref=dsts[a].at[k], send_sem=send.at[3 * a + k], recv_sem=recv.at[3 * a + k],
            device_id=(_flip(x, fx), _flip(y, fy), c), device_id_type=MESH)
            for a in range(self.n) for k, (fx, fy) in enumerate(CHIP_FLIPS[1:])]

    def start(self, *refs):
        for cp in self._copies(*refs):
            cp.start()

    def forward(self, *refs):
        pass

    def finish(self, *refs):
        for cp in self._copies(*refs):
            cp.wait_send()
        for cp in self._copies(*refs):
            cp.wait_recv()


def _run_comm(plans, name):
    n_in = sum(p.n for p, _ in plans)

    def body(*refs):
        args, i0, s0 = [], 0, 2 * n_in
        for p, _ in plans:
            args.append((refs[i0:i0 + p.n], refs[n_in + i0:n_in + i0 + p.n], *refs[s0:s0 + len(p.scratch)]))
            i0, s0 = i0 + p.n, s0 + len(p.scratch)
        for phase in ("start", "forward", "finish"):
            for (p, _), a in zip(plans, args):
                getattr(p, phase)(*a)

    out = pl.pallas_call(
        body, name=name, out_shape=[s for p, _ in plans for s in p.out_shape], in_specs=[ANY] * n_in,
        out_specs=[ANY] * n_in, scratch_shapes=[s for p, _ in plans for s in p.scratch],
    )(*[a for _, arrays in plans for a in arrays])
    split, i0 = [], 0
    for p, _ in plans:
        split.append(out[i0:i0 + p.n])
        i0 += p.n
    return split


def _rs_sibling(arrays, name):
    n = len(arrays)

    def body(*refs):
        srcs, got, (send, recv) = refs[:n], refs[n:2 * n], refs[2 * n:]
        x, y, c = _place()
        copies = []
        for a in range(n):
            for r, (fx, fy) in enumerate(CHIP_FLIPS):
                chip = 2 * _flip(x, fx) + _flip(y, fy)
                copies.append(pltpu.make_async_remote_copy(
                    src_ref=srcs[a].at[2 * chip + 1 - c], dst_ref=got[a].at[r], send_sem=send.at[4 * a + r],
                    recv_sem=recv.at[4 * a + r], device_id=(x, y, 1 - c), device_id_type=MESH))
        for cp in copies:
            cp.start()
        for cp in copies:
            cp.wait_send()
        for cp in copies:
            cp.wait_recv()

    return pl.pallas_call(
        body, name=name, out_shape=[jax.ShapeDtypeStruct((4, *a.shape[1:]), a.dtype) for a in arrays],
        in_specs=[ANY] * n, out_specs=[ANY] * n,
        scratch_shapes=[pltpu.SemaphoreType.DMA((4 * n,)), pltpu.SemaphoreType.DMA((4 * n,))],
    )(*arrays)


def _row_tile(rows):
    return 256 if rows % 256 == 0 else rows


def _chip_slots():
    x, y, c = _place()
    return jnp.stack([4 * _flip(x, fx) + 2 * _flip(y, fy) + c for fx, fy in CHIP_FLIPS]).astype(I32)


def _pair_sum(full, theirs, name):
    _, rows, cols = theirs.shape
    tr = _row_tile(rows)

    def body(slots_ref, m0_ref, m1_ref, m2_ref, m3_ref, b_ref, own_ref, rest_ref):
        own_ref[...] = m0_ref[...].astype(F32) + b_ref[0].astype(F32)
        for k, m_ref in enumerate((m1_ref, m2_ref, m3_ref)):
            rest_ref[k] = (m_ref[...].astype(F32) + b_ref[k + 1].astype(F32)).astype(BF16)

    def mine(k):
        return pl.BlockSpec((None, tr, cols), lambda i, slots: (slots[k], i, 0))

    return pl.pallas_call(
        body, name=name,
        grid_spec=pltpu.PrefetchScalarGridSpec(
            num_scalar_prefetch=1, grid=(rows // tr,),
            in_specs=[mine(0), mine(1), mine(2), mine(3), pl.BlockSpec((4, tr, cols), lambda i, slots: (0, i, 0))],
            out_specs=(pl.BlockSpec((tr, cols), lambda i, slots: (i, 0)), pl.BlockSpec((3, tr, cols), lambda i, slots: (0, i, 0)))),
        out_shape=(jax.ShapeDtypeStruct((rows, cols), F32), jax.ShapeDtypeStruct((3, rows, cols), BF16)),
        compiler_params=_cp("parallel"),
    )(_chip_slots(), full, full, full, full, theirs)


def _head_lanes(width, h):
    lane = lax.broadcasted_iota(I32, (1, width), 1)
    if width == LANES:
        return (lane >= 64 * h) & (lane < 64 * h + 64)
    nope = (lane >= NOPE * h) & (lane < NOPE * h + NOPE)
    rope = (lane >= 2 * NOPE + ROPE * h) & (lane < 2 * NOPE + ROPE * h + ROPE)
    return nope | rope


def _dilated_bias_table(seq):
    t = min(ATTN_TILE, seq)
    nd = seq // t

    def body(o_ref):
        delta = pl.program_id(0) * t + lax.broadcasted_iota(I32, (t, t), 1) - lax.broadcasted_iota(I32, (t, t), 0)
        mult = ((delta <= 128).astype(I32) + (((delta & 3) == 0) & (delta <= 512)).astype(I32)
                + ((delta & 15) == 0).astype(I32))
        logm = jnp.where(mult == 3, math.log(3.0), jnp.where(mult == 2, math.log(2.0), 0.0))
        valid = (delta >= 0) & (mult > 0)
        dist = delta.astype(F32)
        for h in range(N_HEADS):
            o_ref[h] = jnp.where(valid, logm - 2.0 ** (-(h + 1)) * dist, NEG)

    return pl.pallas_call(
        body, name="dilated_bias_table", grid=(nd,), out_shape=jax.ShapeDtypeStruct((N_HEADS, nd, t, t), F32),
        out_specs=pl.BlockSpec((N_HEADS, None, t, t), lambda d: (0, d, 0, 0)),
        compiler_params=_cp("parallel"),
    )()


def _comm_hooks(comm, refs, n_in, n_out):
    if comm is None:
        return refs[:n_in], refs[n_in:n_in + n_out], refs[n_in + n_out:], None
    n = comm.n
    ins, srcs = refs[:n_in], refs[n_in:n_in + n]
    outs, dsts = refs[n_in + n:n_in + n + n_out], refs[n_in + n + n_out:n_in + 2 * n + n_out]
    rest = refs[n_in + 2 * n + n_out:]
    own = len(rest) - len(comm.scratch)
    return ins, outs, rest[:own], (srcs, dsts, *rest[own:])


def _attn_fwd(q, k, v, bias, *, batch, seq, width, col0, dilated, scale, name, comm=None, comm_arrays=()):
    t = min(ATTN_TILE, seq)
    nq = seq // t
    cq, ck, cv = col0
    pre = scale if dilated else 1.0
    steps = batch * N_PAIRS * nq

    def body(*refs):
        (q_ref, k_ref, v_ref, bias_ref), (o_ref, lse_ref), _, plan = _comm_hooks(comm, refs, 4, 2)
        i = pl.program_id(2)
        step_no = (pl.program_id(0) * N_PAIRS + pl.program_id(1)) * nq + i
        if plan:
            pl.when(step_no == 0)(lambda: comm.start(*plan))
            pl.when(step_no == (3 * steps) // 4)(lambda: comm.forward(*plan))
        q2 = q_ref[...] * pre if dilated else q_ref[...]
        qh = [jnp.where(_head_lanes(width, h), q2, jnp.zeros_like(q2)) for h in (0, 1)]
        vlane = [_head_lanes(LANES, h) for h in (0, 1)]
        top = lax.broadcasted_iota(I32, (LANES, t), 0) < HEAD_V
        causal = lax.broadcasted_iota(I32, (t, t), 0) <= lax.broadcasted_iota(I32, (t, t), 1)

        def scores(j):
            kj = k_ref[pl.ds(pl.multiple_of(j * t, t), t), :]
            return [_dot_nt(kj, qh[h]) for h in (0, 1)]

        def step(j, carry, last):
            m0, l0, m1, l1, acc, s0, s1 = carry
            ahead = [] if last else scores(j + 1)
            vj = v_ref[pl.ds(pl.multiple_of(j * t, t), t), :]
            new, alphas, pv = [], [], []
            for h, (m, l, s) in enumerate(((m0, l0, s0), (m1, l1, s1))):
                if dilated:
                    s = s + bias_ref[h, i - j]
                else:
                    s = s * scale
                    if last:
                        s = jnp.where(causal, s, NEG)
                m_new = jnp.maximum(m, jnp.max(s, axis=0, keepdims=True))
                a = jnp.exp(m - m_new)
                p = jnp.exp(s - m_new)
                new += [m_new, a * l + jnp.sum(p, axis=0, keepdims=True)]
                alphas.append(a)
                pv.append(_dot_tn(jnp.where(vlane[h], vj, jnp.zeros_like(vj)), p.astype(BF16)))
            acc = jnp.where(top, alphas[0], alphas[1]) * acc + pv[0] + pv[1]
            return (*new, acc, *ahead)

        row = jnp.full((1, t), NEG, F32)
        zero = jnp.zeros((1, t), F32)
        init = (row, zero, row, zero, jnp.zeros((LANES, t), F32), *scores(0))
        m0, l0, m1, l1, acc = step(i, lax.fori_loop(0, i, functools.partial(step, last=False), init), True)
        o_ref[...] = jnp.transpose(acc * jnp.where(top, 1.0 / l0, 1.0 / l1)).astype(BF16)
        r = lax.broadcasted_iota(I32, (8, t), 0)
        lse_ref[...] = jnp.where(r == 0, m0 + jnp.log(l0), jnp.where(r == 1, m1 + jnp.log(l1), 0.0))
        if plan:
            pl.when(step_no == steps - 1)(lambda: comm.finish(*plan))

    bias_spec = (pl.BlockSpec((2, nq, t, t), lambda b, p, i: (p, 0, 0, 0)) if dilated
                 else pl.BlockSpec((None, 8, LANES), lambda b, p, i: (0, 0, 0)))
    n = comm.n if comm else 0
    return pl.pallas_call(
        body, name=name, grid=(batch, N_PAIRS, nq),
        out_shape=[jax.ShapeDtypeStruct((batch * seq, DIL_WIDTH), BF16), jax.ShapeDtypeStruct((batch * N_PAIRS, 8, seq), F32)]
        + (comm.out_shape if comm else []),
        in_specs=[pl.BlockSpec((t, width), lambda b, p, i: (b * nq + i, cq + p)),
                  pl.BlockSpec((seq, width), lambda b, p, i: (b, ck + p)),
                  pl.BlockSpec((seq, LANES), lambda b, p, i: (b, cv + p)),
                  bias_spec] + [ANY] * n,
        out_specs=[pl.BlockSpec((t, LANES), lambda b, p, i: (b * nq + i, p)),
                   pl.BlockSpec((None, 8, t), lambda b, p, i: (b * N_PAIRS + p, 0, i))] + [ANY] * n,
        scratch_shapes=comm.scratch if comm else [],
        compiler_params=_cp("arbitrary", "arbitrary", "arbitrary") if comm else _cp("parallel", "parallel", "arbitrary"),
    )(q, k, v, bias, *comm_arrays)


def _attn_bwd(q, k, v, o, do, lse, bias, *, batch, seq, width, col0, dilated, scale, name, comm=None, comm_arrays=()):
    t = min(ATTN_TILE, seq)
    nq = seq // t
    cq, ck, cv = col0
    pre = scale if dilated else 1.0
    steps = batch * N_PAIRS

    def body(*refs):
        ins, (dq_ref, dk_ref, dv_ref), (dq_acc, dk_acc, dv_acc, rowdot, q_heads, do_heads), plan = _comm_hooks(comm, refs, 7, 3)
        q_ref, k_ref, v_ref, o_ref, do_ref, lse_ref, bias_ref = ins
        step_no = pl.program_id(0) * N_PAIRS + pl.program_id(1)
        if plan:
            pl.when(step_no == 0)(lambda: comm.start(*plan))
        wlane = [_head_lanes(width, h) for h in (0, 1)]
        vlane = [_head_lanes(LANES, h) for h in (0, 1)]
        causal = lax.broadcasted_iota(I32, (t, t), 0) <= lax.broadcasted_iota(I32, (t, t), 1)
        q_all = q_ref[...] * pre if dilated else q_ref[...]
        for h in (0, 1):
            q_heads[h] = jnp.where(wlane[h], q_all, jnp.zeros_like(q_all))
            do_heads[h] = jnp.where(vlane[h], do_ref[...], jnp.zeros_like(do_ref[...]))
        prod = jnp.transpose(do_ref[...].astype(F32) * o_ref[...].astype(F32))
        rowdot[0:1, :] = jnp.sum(prod[0:HEAD_V], axis=0, keepdims=True)
        rowdot[1:2, :] = jnp.sum(prod[HEAD_V:], axis=0, keepdims=True)
        dq_acc[...] = jnp.zeros_like(dq_acc)

        def k_tile(j, _):
            ks = pl.multiple_of(j * t, t)
            kj = k_ref[pl.ds(ks, t), :]
            vj = v_ref[pl.ds(ks, t), :]
            kh = [jnp.where(wlane[h], kj, jnp.zeros_like(kj)) for h in (0, 1)]
            dk_acc[...] = jnp.zeros_like(dk_acc)
            dv_acc[...] = jnp.zeros_like(dv_acc)

            def operands(i):
                qs = pl.multiple_of(i * t, t)
                return [q_heads[h, pl.ds(qs, t), :] for h in (0, 1)], [do_heads[h, pl.ds(qs, t), :] for h in (0, 1)]

            def products(i):
                qih, doih = operands(i)
                scores = tuple(_dot_nt(kj, qih[h]) for h in (0, 1))
                return scores + tuple(_dot_nt(vj, doih[h]) for h in (0, 1)) if width > LANES else scores

            def q_tile(n, carry, last):
                i = nq - 1 - n
                ahead = () if last else products(i - 1)
                qs = pl.multiple_of(i * t, t)
                qih, doih = operands(i)
                s0, s1 = carry[:2]
                dps = carry[2:] if width > LANES else [_dot_nt(vj, doih[h]) for h in (0, 1)]
                dq_i = jnp.zeros((t, width), F32)
                for h, (s, dp) in enumerate(((s0, dps[0]), (s1, dps[1]))):
                    if dilated:
                        s = s + bias_ref[h, i - j]
                    else:
                        s = s * scale
                        if last:
                            s = jnp.where(causal, s, NEG)
                    p = jnp.exp(s - lse_ref[h:h + 1, pl.ds(qs, t)])
                    ds = p * (dp - rowdot[h:h + 1, pl.ds(qs, t)])
                    ds = (ds if dilated else ds * scale).astype(BF16)
                    dv_acc[...] += _dot(p.astype(BF16), doih[h])
                    dk_acc[...] += _dot(ds, qih[h])
                    dq_i = dq_i + _dot_tn(ds, kh[h])
                dq_acc[pl.ds(qs, t), :] += dq_i
                return ahead

            q_tile(nq - 1 - j, lax.fori_loop(0, nq - 1 - j, functools.partial(q_tile, last=False), products(nq - 1)), True)
            dk_ref[pl.ds(ks, t), :] = dk_acc[...].astype(BF16)
            dv_ref[pl.ds(ks, t), :] = dv_acc[...].astype(BF16)
            return 0

        lax.fori_loop(0, nq, k_tile, 0)
        dq_ref[...] = (dq_acc[...] * pre).astype(BF16)
        if plan:
            pl.when(step_no == steps - 1)(lambda: comm.finish(*plan))

    tokens = batch * seq
    bias_spec = (pl.BlockSpec((2, nq, t, t), lambda b, p: (p, 0, 0, 0)) if dilated
                 else pl.BlockSpec((None, 8, LANES), lambda b, p: (0, 0, 0)))
    n = comm.n if comm else 0
    return pl.pallas_call(
        body, name=name, grid=(batch, N_PAIRS),
        out_shape=[jax.ShapeDtypeStruct((tokens, N_PAIRS * width), BF16), jax.ShapeDtypeStruct((tokens, N_PAIRS * width), BF16),
                   jax.ShapeDtypeStruct((tokens, DIL_WIDTH), BF16)] + (comm.out_shape if comm else []),
        in_specs=[pl.BlockSpec((seq, width), lambda b, p: (b, cq + p)),
                  pl.BlockSpec((seq, width), lambda b, p: (b, ck + p)),
                  pl.BlockSpec((seq, LANES), lambda b, p: (b, cv + p)),
                  pl.BlockSpec((seq, LANES), lambda b, p: (b, p)),
                  pl.BlockSpec((seq, LANES), lambda b, p: (b, p)),
                  pl.BlockSpec((None, 8, seq), lambda b, p: (b * N_PAIRS + p, 0, 0)),
                  bias_spec] + [ANY] * n,
        out_specs=[pl.BlockSpec((seq, width), lambda b, p: (b, p)),
                   pl.BlockSpec((seq, width), lambda b, p: (b, p)),
                   pl.BlockSpec((seq, LANES), lambda b, p: (b, p))] + [ANY] * n,
        scratch_shapes=[pltpu.VMEM((seq, width), F32), pltpu.VMEM((t, width), F32), pltpu.VMEM((t, LANES), F32),
                        pltpu.VMEM((8, seq), F32), pltpu.VMEM((2, seq, width), BF16), pltpu.VMEM((2, seq, LANES), BF16)]
        + (comm.scratch if comm else []),
        compiler_params=_cp("arbitrary", "arbitrary") if comm else _cp("parallel", "parallel"),
    )(q, k, v, o, do, lse, bias, *comm_arrays)


def _rms(xf, g):
    r = lax.rsqrt(jnp.mean(xf * xf, axis=1, keepdims=True) + RMS_EPS)
    return xf * r * g, r


def _rms_bwd(dy, xf, r, g):
    gy = dy * g
    dx = r * gy - xf * (r * r * r) * jnp.mean(gy * xf, axis=1, keepdims=True)
    return dx, dy * xf * r


def _ln_bwd(dy, xhat, rstd, g):
    dxh = dy * g
    return rstd * (dxh - jnp.mean(dxh, axis=1, keepdims=True) - xhat * jnp.mean(dxh * xhat, axis=1, keepdims=True))


def _rope_slabs(q, cos, sin, transpose):
    first_half = (lax.broadcasted_iota(I32, (1, LANES), 1) % ROPE) < ROPE // 2
    out = []
    for p in range(N_PAIRS):
        blk = q[:, p * PAIR_W + LANES:(p + 1) * PAIR_W]
        y = blk * sin if transpose else blk
        up, down = pltpu.roll(y, LANES - ROPE // 2, 1), pltpu.roll(y, ROPE // 2, 1)
        rot = jnp.where(first_half, up, -down) if transpose else jnp.where(first_half, -up, down) * sin
        out += [q[:, p * PAIR_W:p * PAIR_W + LANES], blk * cos + rot]
    return jnp.concatenate(out, axis=1)


def _fwd_proj(xb, w_in_ext, w1, wk, wv, g_q, g_kv, cext, sext, cs128, *, seq):
    tokens = xb.shape[0]
    tm = min(TOKEN_TILE, seq)
    ns = seq // tm

    def body(x_ref, win_ref, w1_ref, wk_ref, wv_ref, gq_ref, gkv_ref, c_ref, s_ref, cs_ref,
             low_ref, gates_ref, qkvd_ref, qp_ref, kp_ref, vm_ref, qn_ref, kvn_ref):
        xt = x_ref[...]
        low = _dot(xt, win_ref[:, 0:LOW_W])
        low_ref[...] = low
        qkvd_ref[...] = _dot(xt, win_ref[:, LOW_W:LOW_W + 3 * DIL_WIDTH]).astype(BF16)
        gates_ref[...] = _dot(xt, win_ref[:, LOW_W + 3 * DIL_WIDTH:]).astype(BF16)
        qn = _rms(low[:, 0:Q_LORA], gq_ref[...])[0].astype(BF16)
        kvn = _rms(low[:, Q_LORA:Q_LORA + KV_LORA], gkv_ref[...])[0].astype(BF16)
        qn_ref[...] = qn
        kvn_ref[...] = kvn
        qp_ref[...] = _rope_slabs(_dot(qn, w1_ref[...]), c_ref[...], s_ref[...], False).astype(BF16)
        kr = low[:, Q_LORA + KV_LORA:] * cs_ref[...]
        kr = kr + pltpu.roll(kr, LANES - ROPE, 1)
        lane = lax.broadcasted_iota(I32, kr.shape, 1)
        kr = jnp.where(lane < ROPE, kr, 0.0)
        kr = (kr + pltpu.roll(kr, ROPE, 1)).astype(BF16)
        kn = _dot(kvn, wk_ref[...]).astype(BF16)
        kp_ref[...] = jnp.concatenate([blk for p in range(N_PAIRS) for blk in (kn[:, p * LANES:(p + 1) * LANES], kr)], axis=1)
        vm_ref[...] = _dot(kvn, wv_ref[...]).astype(BF16)

    n_gates = 2 * D_MODEL
    outs = [(LOW_W, F32), (n_gates, BF16), (3 * DIL_WIDTH, BF16), (N_PAIRS * PAIR_W, BF16), (N_PAIRS * PAIR_W, BF16),
            (DIL_WIDTH, BF16), (Q_LORA, BF16), (KV_LORA, BF16)]
    return pl.pallas_call(
        body, name="fwd_proj", grid=(tokens // tm,),
        out_shape=tuple(jax.ShapeDtypeStruct((tokens, w), dt) for w, dt in outs),
        in_specs=[_rows(tm, D_MODEL), _full(w_in_ext.shape), _full(w1.shape), _full(wk.shape),
                  _full(wv.shape), _full(g_q.shape), _full(g_kv.shape),
                  pl.BlockSpec((tm, LANES), lambda i: (i % ns, 1)),
                  pl.BlockSpec((tm, LANES), lambda i: (i % ns, 1)),
                  pl.BlockSpec((tm, LANES), lambda i: (i % ns, 0))],
        out_specs=tuple(_rows(tm, w) for w, _ in outs),
        compiler_params=_cp("parallel"),
    )(xb, w_in_ext, w1, wk, wv, g_q, g_kv, cext, sext, cs128)


def _fwd_mix(o_a, o_b, gates, x, b_gate, w_oa, w_ob, w_out, ln_g, ln_b, *, seq):
    tokens = x.shape[0]
    tm = min(TOKEN_TILE, seq)

    def body(oa_ref, ob_ref, gt_ref, x_ref, bg_ref, woa_ref, wob_ref, wout_ref, g_ref, b_ref,
             hb_ref, xhat_ref, rstd_ref, ya_ref, yb_ref, mix_ref):
        ya = _dot(oa_ref[...], woa_ref[...])
        yb = _dot(ob_ref[...], wob_ref[...])
        g0 = _sigmoid(gt_ref[:, 0:D_MODEL].astype(F32) + bg_ref[0:1, :])
        g1 = _sigmoid(gt_ref[:, D_MODEL:].astype(F32) + bg_ref[1:2, :])
        mix = (g0 * ya + g1 * yb).astype(BF16)
        z = ALPHA * x_ref[...] + _dot(mix, wout_ref[...])
        zc = z - jnp.mean(z, axis=1, keepdims=True)
        rstd = lax.rsqrt(jnp.mean(zc * zc, axis=1, keepdims=True) + LN_EPS)
        xhat = zc * rstd
        hb_ref[...] = (xhat * g_ref[...] + b_ref[...]).astype(BF16)
        xhat_ref[...] = xhat
        rstd_ref[...] = jnp.broadcast_to(rstd, (tm, LANES))
        ya_ref[...] = ya.astype(BF16)
        yb_ref[...] = yb.astype(BF16)
        mix_ref[...] = mix

    outs = [(D_MODEL, BF16), (D_MODEL, F32), (LANES, F32), (D_MODEL, BF16), (D_MODEL, BF16), (D_MODEL, BF16)]
    return pl.pallas_call(
        body, name="fwd_mix", grid=(tokens // tm,),
        out_shape=tuple(jax.ShapeDtypeStruct((tokens, w), dt) for w, dt in outs),
        in_specs=[_rows(tm, DIL_WIDTH), _rows(tm, DIL_WIDTH), _rows(tm, 2 * D_MODEL), _rows(tm, D_MODEL),
                  _full(b_gate.shape), _full(w_oa.shape), _full(w_ob.shape), _full(w_out.shape),
                  _full(ln_g.shape), _full(ln_b.shape)],
        out_specs=tuple(_rows(tm, w) for w, _ in outs),
        compiler_params=_cp("parallel"),
    )(o_a, o_b, gates, x, b_gate, w_oa, w_ob, w_out, ln_g, ln_b)


def _fwd_mlp(hb, xhat1, target, w_ff1, w_ff2, ln1_g, ln1_b, ln_g, ln_b, *, seq):
    tokens = hb.shape[0]
    tm = min(2 * TOKEN_TILE, seq)
    tf = FF_SHARD
    nf = N_DEV // FF_STEP

    def body(hb_ref, xh_ref, tg_ref, w1_ref, w2_ref, g1_ref, b1_ref, g_ref, b_ref, u_ref, dz_ref, dzb_ref, stat_ref, acc):
        i, j = pl.program_id(0), pl.program_id(1)

        @pl.when((i == 0) & (j == 0))
        def _():
            stat_ref[...] = jnp.zeros_like(stat_ref)

        @pl.when(j == 0)
        def _():
            acc[...] = jnp.zeros_like(acc)

        acts = []
        for s in range(FF_STEP):
            u = _dot(hb_ref[...], w1_ref[s])
            u_ref[:, s * tf:(s + 1) * tf] = u.astype(BF16)
            acts.append(jnp.square(jnp.maximum(u, 0.0)).astype(BF16))
        acc[...] += _dot(jnp.concatenate(acts, axis=1), w2_ref[...])

        @pl.when(j == nf - 1)
        def _():
            z = ALPHA * (xh_ref[...] * g1_ref[...] + b1_ref[...]) + acc[...]
            zc = z - jnp.mean(z, axis=1, keepdims=True)
            rstd = lax.rsqrt(jnp.mean(zc * zc, axis=1, keepdims=True) + LN_EPS)
            xhat = zc * rstd
            err = xhat * g_ref[...] + b_ref[...] - tg_ref[...]
            dy = err * (1.0 / D_MODEL)
            dz = _ln_bwd(dy, xhat, rstd, g_ref[...])
            dz_ref[...] = dz
            dzb_ref[...] = dz.astype(BF16)
            stat_ref[0:1, :] += jnp.sum(dy * xhat, axis=0, keepdims=True)
            stat_ref[1:2, :] += jnp.sum(dy, axis=0, keepdims=True)
            stat_ref[2:3, :] += jnp.sum(jnp.sum(err * err, axis=1, keepdims=True), axis=0, keepdims=True) * (0.5 / D_MODEL)

    return pl.pallas_call(
        body, name="fwd_mlp", grid=(tokens // tm, nf),
        out_shape=(jax.ShapeDtypeStruct((tokens, D_FF), BF16), jax.ShapeDtypeStruct((tokens, D_MODEL), F32),
                   jax.ShapeDtypeStruct((tokens, D_MODEL), BF16), jax.ShapeDtypeStruct((8, D_MODEL), F32)),
        in_specs=[_rows(tm, D_MODEL), _rows(tm, D_MODEL), _rows(tm, D_MODEL),
                  pl.BlockSpec((FF_STEP, D_MODEL, tf), lambda i, j: (j, 0, 0)),
                  pl.BlockSpec((FF_STEP * tf, D_MODEL), lambda i, j: (j, 0)),
                  _full(ln1_g.shape), _full(ln1_b.shape), _full(ln_g.shape), _full(ln_b.shape)],
        out_specs=(pl.BlockSpec((tm, FF_STEP * tf), lambda i, j: (i, j)), _rows(tm, D_MODEL), _rows(tm, D_MODEL),
                   _full((8, D_MODEL))),
        scratch_shapes=[pltpu.VMEM((tm, D_MODEL), F32)],
        compiler_params=_cp("arbitrary", "arbitrary"),
    )(hb, xhat1, target, w_ff1, w_ff2, ln1_g, ln1_b, ln_g, ln_b)


def _bwd_mlp(dz2, dz2b, u, xhat1, rstd1, w_ff1, w_ff2, ln_g, *, seq):
    tokens = dz2.shape[0]
    tm = min(2 * TOKEN_TILE, seq)
    tf = FF_SHARD
    nf = N_DEV // FF_STEP

    def body(dz_ref, dzb_ref, u_ref, xh_ref, rs_ref, w1_ref, w2_ref, g_ref, du_ref, dz1_ref, dz1b_ref, stat_ref, acc):
        i, j = pl.program_id(0), pl.program_id(1)

        @pl.when((i == 0) & (j == 0))
        def _():
            stat_ref[...] = jnp.zeros_like(stat_ref)

        @pl.when(j == 0)
        def _():
            acc[...] = jnp.zeros_like(acc)

        da = _dot_nt(dzb_ref[...], w2_ref[...])
        du = (da * (2.0 * jnp.maximum(u_ref[...].astype(F32), 0.0))).astype(BF16)
        du_ref[...] = du
        part = _dot_nt(du[:, 0:tf], w1_ref[0])
        for s in range(1, FF_STEP):
            part = part + _dot_nt(du[:, s * tf:(s + 1) * tf], w1_ref[s])
        acc[...] += part

        @pl.when(j == nf - 1)
        def _():
            dh = ALPHA * dz_ref[...] + acc[...]
            xhat = xh_ref[...]
            dz1 = _ln_bwd(dh, xhat, rs_ref[:, 0:1], g_ref[...])
            dz1_ref[...] = dz1
            dz1b_ref[...] = dz1.astype(BF16)
            stat_ref[0:1, :] += jnp.sum(dh * xhat, axis=0, keepdims=True)
            stat_ref[1:2, :] += jnp.sum(dh, axis=0, keepdims=True)

    return pl.pallas_call(
        body, name="bwd_mlp", grid=(tokens // tm, nf),
        out_shape=(jax.ShapeDtypeStruct((tokens, D_FF), BF16), jax.ShapeDtypeStruct((tokens, D_MODEL), F32),
                   jax.ShapeDtypeStruct((tokens, D_MODEL), BF16), jax.ShapeDtypeStruct((8, D_MODEL), F32)),
        in_specs=[_rows(tm, D_MODEL), _rows(tm, D_MODEL), pl.BlockSpec((tm, FF_STEP * tf), lambda i, j: (i, j)),
                  _rows(tm, D_MODEL), _rows(tm, LANES),
                  pl.BlockSpec((FF_STEP, D_MODEL, tf), lambda i, j: (j, 0, 0)),
                  pl.BlockSpec((FF_STEP * tf, D_MODEL), lambda i, j: (j, 0)),
                  _full(ln_g.shape)],
        out_specs=(pl.BlockSpec((tm, FF_STEP * tf), lambda i, j: (i, j)), _rows(tm, D_MODEL), _rows(tm, D_MODEL),
                   _full((8, D_MODEL))),
        scratch_shapes=[pltpu.VMEM((tm, D_MODEL), F32)],
        compiler_params=_cp("arbitrary", "arbitrary"),
    )(dz2, dz2b, u, xhat1, rstd1, w_ff1, w_ff2, ln_g)


def _bwd_mix(dz1b, gates, y_a, y_b, b_gate, w_oa, w_ob, w_out, *, seq):
    tokens = dz1b.shape[0]
    tm = min(TOKEN_TILE, seq)

    def body(dz_ref, gt_ref, ya_ref, yb_ref, bg_ref, woa_ref, wob_ref, wout_ref,
             dgt_ref, dya_ref, dyb_ref, doa_ref, dob_ref, stat_ref):
        @pl.when(pl.program_id(0) == 0)
        def _():
            stat_ref[...] = jnp.zeros_like(stat_ref)

        dmix = _dot_nt(dz_ref[...], wout_ref[...])
        for k, (y_ref, w_ref, dy_ref, do_ref) in enumerate(((ya_ref, woa_ref, dya_ref, doa_ref), (yb_ref, wob_ref, dyb_ref, dob_ref))):
            g = _sigmoid(gt_ref[:, k * D_MODEL:(k + 1) * D_MODEL].astype(F32) + bg_ref[k:k + 1, :])
            dgate = dmix * y_ref[...].astype(F32) * g * (1.0 - g)
            dgt_ref[:, k * D_MODEL:(k + 1) * D_MODEL] = dgate.astype(BF16)
            stat_ref[k:k + 1, :] += jnp.sum(dgate, axis=0, keepdims=True)
            dy = (dmix * g).astype(BF16)
            dy_ref[...] = dy
            do_ref[...] = _dot_nt(dy, w_ref[...]).astype(BF16)

    outs = [(2 * D_MODEL, BF16), (D_MODEL, BF16), (D_MODEL, BF16), (DIL_WIDTH, BF16), (DIL_WIDTH, BF16)]
    return pl.pallas_call(
        body, name="bwd_mix", grid=(tokens // tm,),
        out_shape=tuple(jax.ShapeDtypeStruct((tokens, w), dt) for w, dt in outs) + (jax.ShapeDtypeStruct((8, D_MODEL), F32),),
        in_specs=[_rows(tm, D_MODEL), _rows(tm, 2 * D_MODEL), _rows(tm, D_MODEL), _rows(tm, D_MODEL),
                  _full(b_gate.shape), _full(w_oa.shape), _full(w_ob.shape), _full(w_out.shape)],
        out_specs=tuple(_rows(tm, w) for w, _ in outs) + (_full((8, D_MODEL)),),
        compiler_params=_cp("arbitrary"),
    )(dz1b, gates, y_a, y_b, b_gate, w_oa, w_ob, w_out)


def _bwd_proj(dqp, dkp, dvm, dq_d, dk_d, dv_d, dgates, dz1, low, w_in_ext, w1, wk, wv, g_q, g_kv, cext, sext, cs128, *, seq):
    tokens = dz1.shape[0]
    tm = min(TOKEN_TILE, seq)
    ns = seq // tm

    def body(dqp_ref, dkp_ref, dvm_ref, dqd_ref, dkd_ref, dvd_ref, dgt_ref, dz_ref, low_ref, win_ref, w1_ref, wk_ref,
             wv_ref, gq_ref, gkv_ref, c_ref, s_ref, cs_ref, dx_ref, dproj_ref, da_ref, dkn_ref, stat_ref):
        @pl.when(pl.program_id(0) == 0)
        def _():
            stat_ref[...] = jnp.zeros_like(stat_ref)

        low = low_ref[...]
        d_a = _rope_slabs(dqp_ref[...].astype(F32), c_ref[...], s_ref[...], True).astype(BF16)
        da_ref[...] = d_a
        q_a = low[:, 0:Q_LORA]
        _, rq = _rms(q_a, gq_ref[...])
        dq_a, gq_terms = _rms_bwd(_dot_nt(d_a, w1_ref[...]), q_a, rq, gq_ref[...])
        kv_a = low[:, Q_LORA:Q_LORA + KV_LORA]
        _, rkv = _rms(kv_a, gkv_ref[...])
        dkn = jnp.concatenate([dkp_ref[:, p * PAIR_W:p * PAIR_W + LANES] for p in range(N_PAIRS)], axis=1)
        dkn_ref[...] = dkn
        dkv_a, gkv_terms = _rms_bwd(_dot_nt(dkn, wk_ref[...]) + _dot_nt(dvm_ref[...], wv_ref[...]), kv_a, rkv, gkv_ref[...])
        dkr = sum(dkp_ref[:, p * PAIR_W + LANES:(p + 1) * PAIR_W].astype(F32) for p in range(N_PAIRS))
        dkr = dkr + pltpu.roll(dkr, LANES - ROPE, 1)
        dkr = jnp.where(lax.broadcasted_iota(I32, dkr.shape, 1) < ROPE, dkr, 0.0)
        dkr = (dkr + pltpu.roll(dkr, ROPE, 1)) * cs_ref[...]
        stat_ref[0:1, 0:Q_LORA] += jnp.sum(gq_terms, axis=0, keepdims=True)
        stat_ref[1:2, 0:KV_LORA] += jnp.sum(gkv_terms, axis=0, keepdims=True)
        dproj_ref[:, 0:Q_LORA] = dq_a.astype(BF16)
        dproj_ref[:, Q_LORA:Q_LORA + KV_LORA] = dkv_a.astype(BF16)
        dproj_ref[:, Q_LORA + KV_LORA:LOW_W] = dkr.astype(BF16)
        dproj_ref[:, LOW_W:LOW_W + DIL_WIDTH] = dqd_ref[...]
        dproj_ref[:, LOW_W + DIL_WIDTH:LOW_W + 2 * DIL_WIDTH] = dkd_ref[...]
        dproj_ref[:, LOW_W + 2 * DIL_WIDTH:LOW_W + 3 * DIL_WIDTH] = dvd_ref[...]
        dproj_ref[:, LOW_W + 3 * DIL_WIDTH:] = dgt_ref[...]
        dx_ref[...] = ALPHA * dz_ref[...] + _dot_nt(dproj_ref[...], win_ref[...])

    wide = N_PAIRS * PAIR_W
    return pl.pallas_call(
        body, name="bwd_proj", grid=(tokens // tm,),
        out_shape=(jax.ShapeDtypeStruct((tokens, D_MODEL), F32), jax.ShapeDtypeStruct((tokens, IN_EXT), BF16),
                   jax.ShapeDtypeStruct((tokens, wide), BF16), jax.ShapeDtypeStruct((tokens, N_HEADS * NOPE), BF16),
                   jax.ShapeDtypeStruct((8, D_MODEL), F32)),
        in_specs=[_rows(tm, wide), _rows(tm, wide), _rows(tm, DIL_WIDTH), _rows(tm, DIL_WIDTH), _rows(tm, DIL_WIDTH),
                  _rows(tm, DIL_WIDTH), _rows(tm, 2 * D_MODEL),
                  _rows(tm, D_MODEL), _rows(tm, LOW_W), _full(w_in_ext.shape), _full(w1.shape),
                  _full(wk.shape), _full(wv.shape), _full(g_q.shape), _full(g_kv.shape),
                  pl.BlockSpec((tm, LANES), lambda i: (i % ns, 1)), pl.BlockSpec((tm, LANES), lambda i: (i % ns, 1)),
                  pl.BlockSpec((tm, LANES), lambda i: (i % ns, 0))],
        out_specs=(_rows(tm, D_MODEL), _rows(tm, IN_EXT), _rows(tm, wide), _rows(tm, N_HEADS * NOPE), _full((8, D_MODEL))),
        compiler_params=_cp("arbitrary"),
    )(dqp, dkp, dvm, dq_d, dk_d, dv_d, dgates, dz1, low, w_in_ext, w1, wk, wv, g_q, g_kv, cext, sext, cs128)


def _wgrad(a, b, name, square_relu=False, by_shard=False):
    tokens, ka = a.shape
    n = b.shape[1]
    tka = min(ka, 512)
    shard = n // N_DEV
    tn = WGRAD_SHARDS * shard if by_shard else max(w for w in range(LANES, min(n, 2304) + 1, LANES) if n % w == 0)
    tt = min(tokens, 1024)
    nt = tokens // tt

    def body(a_ref, b_ref, o_ref, acc):
        kt = pl.program_id(2)

        @pl.when(kt == 0)
        def _():
            acc[...] = jnp.zeros_like(acc)

        at = a_ref[...]
        if square_relu:
            at = jnp.square(jnp.maximum(at.astype(F32), 0.0)).astype(BF16)
        acc[...] += _dot_tn(at, b_ref[...])

        @pl.when(kt == nt - 1)
        def _():
            if by_shard:
                for s in range(WGRAD_SHARDS):
                    o_ref[s] = acc[:, s * shard:(s + 1) * shard].astype(BF16)
            else:
                o_ref[...] = acc[...].astype(BF16)

    if by_shard:
        out_shape, out_spec = (N_DEV, ka, shard), pl.BlockSpec((WGRAD_SHARDS, tka, shard), lambda i, j, k: (j, i, 0))
    else:
        out_shape, out_spec = (ka, n), pl.BlockSpec((tka, tn), lambda i, j, k: (i, j))
    return pl.pallas_call(
        body, name=name, grid=(ka // tka, n // tn, nt), out_shape=jax.ShapeDtypeStruct(out_shape, BF16),
        in_specs=[pl.BlockSpec((tt, tka), lambda i, j, k: (k, i)), pl.BlockSpec((tt, tn), lambda i, j, k: (k, j))],
        out_specs=out_spec,
        scratch_shapes=[pltpu.VMEM((tka, tn), F32)],
        compiler_params=_cp("parallel", "parallel", "arbitrary"),
    )(a, b)


def _adam_math(w, g, m, v):
    m = ADAM_B1 * m + (1.0 - ADAM_B1) * g
    v = ADAM_B2 * v + (1.0 - ADAM_B2) * jnp.square(g)
    m_hat = m / (1.0 - ADAM_B1 ** ADAM_STEP)
    v_hat = v / (1.0 - ADAM_B2 ** ADAM_STEP)
    return -ADAM_LR * (m_hat / (jnp.sqrt(v_hat) + ADAM_EPS) + ADAM_WD * w), m, v


def _adamw(w, m, v, own, parts, name):
    rows, cols = w.shape
    tr = _row_tile(rows)
    n_parts = parts.shape[0]

    def body(slot_ref, w_ref, m_ref, v_ref, own_ref, p_ref, g_ref, d_ref, nm_ref, nv_ref):
        g = own_ref[...].astype(F32)
        for d in range(n_parts):
            g = g + p_ref[d].astype(F32)
        g_ref[...] = g
        d_ref[...], nm_ref[...], nv_ref[...] = _adam_math(w_ref[...], g, m_ref[...], v_ref[...])

    x, y, c = _place()
    blk = pl.BlockSpec((tr, cols), lambda i, slot: (i, 0))
    own_blk = blk if own.ndim == 2 else pl.BlockSpec((None, tr, cols), lambda i, slot: (slot[0], i, 0))
    return pl.pallas_call(
        body, name=name,
        grid_spec=pltpu.PrefetchScalarGridSpec(
            num_scalar_prefetch=1, grid=(rows // tr,),
            in_specs=[blk, blk, blk, own_blk, pl.BlockSpec((n_parts, tr, cols), lambda i, slot: (0, i, 0))],
            out_specs=(blk,) * 4),
        out_shape=(jax.ShapeDtypeStruct((rows, cols), F32),) * 4, compiler_params=_cp("parallel"),
    )(jnp.reshape(4 * x + 2 * y + c, (1,)).astype(I32), w, m, v, own, parts)


def _adamw_small(parts, w, m, v):
    _, rows, cols = parts.shape

    def body(p_ref, w_ref, m_ref, v_ref, g_ref, d_ref, nm_ref, nv_ref):
        g = p_ref[0]
        for d in range(1, N_DEV):
            g = g + p_ref[d]
        g_ref[...] = g
        d_ref[...], nm_ref[...], nv_ref[...] = _adam_math(w_ref[...], g, m_ref[...], v_ref[...])

    return pl.pallas_call(
        body, name="adamw_replicated", out_shape=(jax.ShapeDtypeStruct((rows, cols), F32),) * 4,
        in_specs=[_full(parts.shape)] + [_full((rows, cols))] * 3, out_specs=(_full((rows, cols)),) * 4, grid=(1,),
        compiler_params=_cp("arbitrary"),
    )(parts, w, m, v)


def _pad_rows(a2d, mult):
    pad = (-a2d.shape[-2]) % mult
    return jnp.pad(a2d, [(0, 0)] * (a2d.ndim - 2) + [(0, pad), (0, 0)]) if pad else a2d


def _pad_cols(a):
    pad = (-a.shape[-1]) % LANES
    return jnp.pad(a, [(0, 0)] * (a.ndim - 1) + [(0, pad)]) if pad else a


def _rot_cols(w):
    half = ROPE // 2
    return jnp.concatenate([-w[..., half:], w[..., :half]], axis=-1)


def _unrot_cols(dw):
    half = ROPE // 2
    return jnp.concatenate([dw[..., half:], -dw[..., :half]], axis=-1)


def _from_col_shards(stacked):
    return stacked.transpose(1, 0, 2).reshape(stacked.shape[1], -1)


def _to_col_shards(full):
    r = full.shape[0]
    return full.reshape(r, N_DEV, -1).transpose(1, 0, 2)


def _rope_tables(seq):
    half = ROPE // 2
    inv = jnp.power(ROPE_THETA, -jnp.arange(half, dtype=F32) / half)
    ang = jnp.arange(seq, dtype=F32)[:, None] * inv[None, :]
    cos = jnp.concatenate([jnp.cos(ang)] * 2, axis=1)
    sin = jnp.concatenate([jnp.sin(ang)] * 2, axis=1)
    ones, zeros = jnp.ones((seq, 2 * NOPE), F32), jnp.zeros((seq, 2 * NOPE), F32)
    pad = jnp.zeros((seq, PAIR_W - 2 * NOPE - 2 * ROPE), F32)
    cext = jnp.concatenate([ones, cos, cos, pad], axis=1)
    sext = jnp.concatenate([zeros, sin, sin, pad], axis=1)
    cs128 = jnp.concatenate([cos, sin, jnp.zeros((seq, LANES - 2 * ROPE), F32)], axis=1)
    return cext, sext, cs128


def _pair_slabs(nope, rope):
    k = nope.shape[0]
    nope = nope.reshape(k, N_PAIRS, 2 * NOPE)
    rope = jnp.zeros((k, N_PAIRS, 2 * ROPE), nope.dtype) if rope is None else rope.reshape(k, N_PAIRS, 2 * ROPE)
    pad = jnp.zeros((k, N_PAIRS, PAIR_W - 2 * NOPE - 2 * ROPE), nope.dtype)
    return jnp.concatenate([nope, rope, pad], axis=2).reshape(k, N_PAIRS * PAIR_W)


def _split_slabs(slabs):
    k = slabs.shape[0]
    s = slabs.reshape(k, N_PAIRS, PAIR_W)
    return s[:, :, :2 * NOPE].reshape(k, N_HEADS, NOPE), s[:, :, 2 * NOPE:2 * NOPE + 2 * ROPE].reshape(k, N_HEADS, ROPE)


def kernel(x, w_in, b_gate, g_q_a, w_uq, g_kv_a, w_ukv, w_o_mla, w_o_dil, w_out, ln1_g, ln1_b, w_ff1, w_ff2, ln2_g, ln2_b, loss_target, m_w_in, m_b_gate, m_g_q_a, m_w_uq, m_g_kv_a, m_w_ukv, m_w_o_mla, m_w_o_dil, m_w_out, m_ln1_g, m_ln1_b, m_w_ff1, m_w_ff2, m_ln2_g, m_ln2_b, v_w_in, v_b_gate, v_g_q_a, v_w_uq, v_g_kv_a, v_w_ukv, v_w_o_mla, v_w_o_dil, v_w_out, v_ln1_g, v_ln1_b, v_w_ff1, v_w_ff2, v_ln2_g, v_ln2_b):
    batch, seq, _ = x.shape
    tokens = batch * seq
    weights = dict(w_in=w_in, w_uq=w_uq, w_ukv=w_ukv, w_o_mla=w_o_mla, w_o_dil=w_o_dil, w_out=w_out, w_ff1=w_ff1, w_ff2=w_ff2, b_gate=b_gate)
    mom_m = dict(w_in=m_w_in, w_uq=m_w_uq, w_ukv=m_w_ukv, w_o_mla=m_w_o_mla, w_o_dil=m_w_o_dil, w_out=m_w_out, w_ff1=m_w_ff1, w_ff2=m_w_ff2, b_gate=m_b_gate)
    mom_v = dict(w_in=v_w_in, w_uq=v_w_uq, w_ukv=v_w_ukv, w_o_mla=v_w_o_mla, w_o_dil=v_w_o_dil, w_out=v_w_out, w_ff1=v_w_ff1, w_ff2=v_w_ff2, b_gate=v_b_gate)

    first = ["w_in", "w_uq", "w_ukv"]
    widths = [weights[n].shape[2] for n in first]
    shards = [_pad_cols(weights[n][0].astype(BF16)) for n in first]
    (g_in, g_uq, g_ukv), = _run_comm([(_Gather(shards), shards)], "all_gather_first_weights")
    g_uq, g_ukv = g_uq[:, :, :widths[1]], g_ukv[:, :, :widths[2]]

    s1, s2, n_in = Q_LORA + KV_LORA, Q_LORA + KV_LORA + ROPE, N_DEV * widths[0]

    def w_in_cols(lo, hi):
        out = []
        while lo < hi:
            d, off = divmod(lo, widths[0])
            take = min(hi - lo, widths[0] - off)
            out.append(g_in[d][:, off:off + take])
            lo += take
        return out

    w_in_ext = jnp.concatenate(w_in_cols(0, s2) + [_rot_cols(jnp.concatenate(w_in_cols(s1, s2), axis=1)),
                                                   jnp.zeros((D_MODEL, LOW_W - s2 - ROPE), BF16)] + w_in_cols(s2, n_in), axis=1)
    uq = _from_col_shards(g_uq).reshape(Q_LORA, N_HEADS, NOPE + ROPE)
    w1 = _pair_slabs(uq[:, :, :NOPE], uq[:, :, NOPE:])
    ukv = _from_col_shards(g_ukv).reshape(KV_LORA, N_HEADS, NOPE + HEAD_V)
    wk = ukv[:, :, :NOPE].reshape(KV_LORA, N_HEADS * NOPE)
    wv = ukv[:, :, NOPE:].reshape(KV_LORA, N_HEADS * HEAD_V)
    cext, sext, cs128 = _rope_tables(seq)
    dil_bias = _dilated_bias_table(seq)
    no_bias = jnp.zeros((1, 8, LANES), F32)

    x2 = x.reshape(tokens, D_MODEL)
    xb = x2.astype(BF16)
    low, gates, qkvd, qp, kp, vm, qn, kvn = _fwd_proj(xb, w_in_ext, w1, wk, wv, g_q_a, g_kv_a, cext, sext, cs128, seq=seq)
    bg = b_gate[0]
    bg_hi = bg.astype(BF16)
    bg_lo = (bg - bg_hi.astype(F32)).astype(BF16)
    later = [weights[n][0].astype(BF16) for n in ("w_o_mla", "w_o_dil", "w_out", "w_ff1", "w_ff2")]
    later.append(_pad_rows(jnp.concatenate([bg_hi, bg_lo], axis=0), 16))
    mla = dict(batch=batch, seq=seq, width=PAIR_W, col0=(0, 0, 0), dilated=False, scale=MLA_SCALE)
    dil = dict(batch=batch, seq=seq, width=LANES, col0=(0, N_PAIRS, 2 * N_PAIRS), dilated=True, scale=DIL_SCALE)
    o_a, lse_a, g_oa, g_ob, g_out, g_ff1, g_ff2, g_bg = _attn_fwd(
        qp, kp, vm, no_bias, name="mla_attention_fwd", comm=_Gather(later), comm_arrays=later, **mla)
    o_b, lse_b = _attn_fwd(qkvd, qkvd, qkvd, dil_bias, name="dilated_attention_fwd", **dil)
    w_oa, w_ob = _from_col_shards(g_oa), _from_col_shards(g_ob)
    w_out_full = g_out.reshape(D_MODEL, D_MODEL)
    w_ff2_full = g_ff2.reshape(D_FF, D_MODEL)
    bg_parts = g_bg.astype(F32)
    b_gate_full = _from_col_shards(bg_parts[:, 0:2] + bg_parts[:, 2:4])
    hb, xhat1, rstd1, y_a, y_b, mix = _fwd_mix(o_a, o_b, gates, x2, b_gate_full, w_oa, w_ob, w_out_full, ln1_g, ln1_b, seq=seq)
    u, dz2, dz2b, stat2 = _fwd_mlp(hb, xhat1, loss_target.reshape(tokens, D_MODEL), g_ff1, w_ff2_full, ln1_g, ln1_b, ln2_g, ln2_b, seq=seq)

    du, dz1, dz1b, stat1 = _bwd_mlp(dz2, dz2b, u, xhat1, rstd1, g_ff1, w_ff2_full, ln1_g, seq=seq)
    dw_ff = [_wgrad(hb, du, "wgrad_ff1", by_shard=True),
             _wgrad(u, dz2b, "wgrad_ff2", square_relu=True).reshape(N_DEV, FF_SHARD, D_MODEL)]
    dgates, dy_a, dy_b, do_a, do_b, stat_g = _bwd_mix(dz1b, gates, y_a, y_b, b_gate_full, w_oa, w_ob, w_out_full, seq=seq)
    dqp, dkp, dvm, r_ff1, r_ff2 = _attn_bwd(qp, kp, vm, o_a, do_a, lse_a, no_bias, name="mla_attention_bwd",
                                            comm=_Scatter(dw_ff), comm_arrays=dw_ff, **mla)
    dw_mid = [_to_col_shards(_wgrad(o_a, dy_a, "wgrad_o_mla")), _to_col_shards(_wgrad(o_b, dy_b, "wgrad_o_dil")),
              _wgrad(mix, dz1b, "wgrad_out").reshape(N_DEV, D_MODEL // N_DEV, D_MODEL),
              _pad_rows(_to_col_shards(stat_g[0:2]).astype(BF16), 16)]
    dq_d, dk_d, dv_d, r_oa, r_ob, r_out, r_bg = _attn_bwd(qkvd, qkvd, qkvd, o_b, do_b, lse_b, dil_bias, name="dilated_attention_bwd",
                                                          comm=_Scatter(dw_mid), comm_arrays=dw_mid, **dil)
    grad_x, dproj, d_a, dkn, stat_r = _bwd_proj(dqp, dkp, dvm, dq_d, dk_d, dv_d, dgates, dz1, low, w_in_ext, w1, wk, wv,
                                                g_q_a, g_kv_a, cext, sext, cs128, seq=seq)

    dw_in_ext = _wgrad(xb, dproj, "wgrad_in")
    dw1 = _wgrad(qn, d_a, "wgrad_uq")
    dwk = _wgrad(kvn, dkn, "wgrad_ukv_k")
    dwv = _wgrad(kvn, dvm, "wgrad_ukv_v")
    dw_kr = dw_in_ext[:, s1:s2] + _unrot_cols(dw_in_ext[:, s2:s2 + ROPE])

    def dw_in_cols(lo, hi):
        out = []
        for a, b, piece in ((0, s1, lambda u, v: dw_in_ext[:, u:v]), (s1, s2, lambda u, v: dw_kr[:, u - s1:v - s1]),
                            (s2, n_in, lambda u, v: dw_in_ext[:, u + LOW_W - s2:v + LOW_W - s2])):
            if max(lo, a) < min(hi, b):
                out.append(piece(max(lo, a), min(hi, b)))
        return out

    dw_in = jnp.stack([_pad_cols(jnp.concatenate(dw_in_cols(d * widths[0], (d + 1) * widths[0]), axis=1)) for d in range(N_DEV)])
    n1, r1 = _split_slabs(dw1)
    dw_uq = jnp.concatenate([n1, r1], axis=2).reshape(Q_LORA, N_HEADS * (NOPE + ROPE))
    dw_ukv = jnp.concatenate([dwk.reshape(KV_LORA, N_HEADS, NOPE), dwv.reshape(KV_LORA, N_HEADS, HEAD_V)], axis=2).reshape(KV_LORA, N_HEADS * (NOPE + HEAD_V))
    last = [dw_in] + [_pad_cols(_to_col_shards(dw)) for dw in (dw_uq, dw_ukv)]
    theirs = _rs_sibling(last, "rs_last_sibling_exchange")
    sums = [_pair_sum(a, b, "rs_last_pair_sum_" + n) for a, b, n in zip(last, theirs, first)]
    partial = jnp.concatenate([stat_r[0:1, :Q_LORA], stat_r[1:2, :KV_LORA], stat1[0:1], stat1[1:2], stat2[0:1], stat2[1:2],
                               stat2[2:3, :LANES]], axis=1)
    partial = _pad_rows(partial.reshape(-1, LANES), 8)
    rest = [s[1] for s in sums]
    got, (every,) = _run_comm([(_ChipExchange(rest), rest), (_Gather([partial]), [partial])], "rs_last_chip_exchange")

    upd = {}
    for n, w, (own, _), parts in zip(first, widths, sums, got):
        upd[n] = _adamw(weights[n][0], mom_m[n][0], mom_v[n][0], own[:, :w], parts[:, :, :w], "adamw_" + n)
    for n, own, parts in (("w_o_mla", dw_mid[0], r_oa), ("w_o_dil", dw_mid[1], r_ob), ("w_out", dw_mid[2], r_out),
                          ("w_ff1", dw_ff[0], r_ff1), ("w_ff2", dw_ff[1], r_ff2)):
        upd[n] = _adamw(weights[n][0], mom_m[n][0], mom_v[n][0], own, parts, "adamw_" + n)
    bg_upd = _adamw(_pad_rows(b_gate[0], 16), _pad_rows(m_b_gate[0], 16), _pad_rows(v_b_gate[0], 16), dw_mid[3], r_bg, "adamw_b_gate")
    upd["b_gate"] = tuple(t[0:2] for t in bg_upd)

    small_w = [g_q_a, g_kv_a, ln1_g, ln1_b, ln2_g, ln2_b]
    small_m = [m_g_q_a, m_g_kv_a, m_ln1_g, m_ln1_b, m_ln2_g, m_ln2_b]
    small_v = [v_g_q_a, v_g_kv_a, v_ln1_g, v_ln1_b, v_ln2_g, v_ln2_b]
    small_widths = [a.shape[1] for a in small_w]

    def as_rows(vecs, extra):
        flat = jnp.concatenate(vecs + [jnp.zeros((1, extra), F32)], axis=1)
        return _pad_rows(flat.reshape(-1, LANES), 8)

    g_s, d_s, nm_s, nv_s = _adamw_small(every, as_rows(small_w, LANES), as_rows(small_m, LANES), as_rows(small_v, LANES))

    def split_small(a):
        flat = a.reshape(1, -1)
        out, c0 = [], 0
        for w in small_widths:
            out.append(flat[:, c0:c0 + w])
            c0 += w
        return out, flat[0, c0]

    g_small, loss = split_small(g_s)
    small = [g_small, split_small(d_s)[0], split_small(nm_s)[0], split_small(nv_s)[0]]

    order = ["w_in", "b_gate", "g_q_a", "w_uq", "g_kv_a", "w_ukv", "w_o_mla", "w_o_dil", "w_out", "ln1_g", "ln1_b", "w_ff1", "w_ff2", "ln2_g", "ln2_b"]
    small_names = ["g_q_a", "g_kv_a", "ln1_g", "ln1_b", "ln2_g", "ln2_b"]

    def pick(kind):
        return [small[kind][small_names.index(n)] if n in small_names else upd[n][kind][None] for n in order]

    return (loss, grad_x.reshape(batch, seq, D_MODEL), *pick(0), *pick(1), *pick(2), *pick(3))
```

```python
import functools
import math

import jax
import jax.numpy as jnp
from jax import lax
from jax.experimental import pallas as pl
from jax.experimental.pallas import tpu as pltpu

F32 = jnp.float32
BF16 = jnp.bfloat16
I32 = jnp.int32

D_MODEL = 1024
N_HEADS = 8
NOPE = 64
ROPE = 32
HEAD_V = 64
Q_LORA = 384
KV_LORA = 256
DIL_WIDTH = 512
D_FF = 4096
ROPE_THETA = 10000.0
LN_EPS = 1e-5
RMS_EPS = 1e-6
NEG = -1e30
ALPHA = 2.0 ** 0.25
MLA_SCALE = (NOPE + ROPE) ** -0.5
DIL_SCALE = 64 ** -0.5
ADAM_LR, ADAM_B1, ADAM_B2, ADAM_EPS, ADAM_WD, ADAM_STEP = 0.001, 0.9, 0.999, 1e-08, 0.01, 10

LANES = 128
PAIR_W = 256
N_PAIRS = N_HEADS // 2
LOW_W = 768
IN_EXT = LOW_W + 3 * DIL_WIDTH + 2 * D_MODEL
N_DEV = 8
FF_SHARD = D_FF // N_DEV
FF_STEP = 4
WGRAD_SHARDS = 4
TOKEN_TILE = 256
ATTN_TILE = 256
VMEM_LIMIT = 56 << 20

MESH = pl.DeviceIdType.MESH
ANY = pl.BlockSpec(memory_space=pl.ANY)
CHIP_FLIPS = ((0, 0), (0, 1), (1, 0), (1, 1))
PEER_FLIPS = tuple((fx, fy, fc) for fx in (0, 1) for fy in (0, 1) for fc in (0, 1))[1:]


def _cp(*sem):
    return pltpu.CompilerParams(dimension_semantics=sem or None, vmem_limit_bytes=VMEM_LIMIT)


def _full(shape):
    nd = len(shape)
    return pl.BlockSpec(shape, lambda *_: (0,) * nd)


def _rows(tm, width):
    return pl.BlockSpec((tm, width), lambda i, *_: (i, 0))


def _dot(a, b):
    return jnp.dot(a, b, preferred_element_type=F32)


def _dot_nt(a, b):
    return lax.dot_general(a, b, (((1,), (1,)), ((), ())), preferred_element_type=F32)


def _dot_tn(a, b):
    return lax.dot_general(a, b, (((0,), (0,)), ((), ())), preferred_element_type=F32)


def _sigmoid(z):
    return 1.0 / (1.0 + jnp.exp(-z))


def _place():
    return lax.axis_index("x"), lax.axis_index("y"), lax.axis_index("c")


def _flip(v, f):
    return 1 - v if f else v


class _Gather:
    def __init__(self, shards):
        self.n = len(shards)
        self.out_shape = [jax.ShapeDtypeStruct((N_DEV, *s.shape), s.dtype) for s in shards]
        self.scratch = [pltpu.SemaphoreType.DMA((7 * self.n,)), pltpu.SemaphoreType.DMA((7 * self.n,)),
                        pltpu.SemaphoreType.DMA((self.n,))]

    def _copies(self, what, srcs, dsts, send, recv, local):
        x, y, c = _place()
        chips = [(_flip(x, fx), _flip(y, fy)) for fx, fy in CHIP_FLIPS[1:]]
        out = []
        for a in range(self.n):
            def slot(px, py, pc, a=a):
                return dsts[a].at[4 * px + 2 * py + pc]

            def copy(k, block, to, src=None, a=a, slot=slot):
                return pltpu.make_async_remote_copy(
                    src_ref=slot(*block) if src is None else src, dst_ref=slot(*block),
                    send_sem=send.at[7 * a + k], recv_sem=recv.at[7 * a + k], device_id=to, device_id_type=MESH)

            if what == "mine":
                out.append(pltpu.make_async_copy(srcs[a], slot(x, y, c), local.at[a]))
            elif what == "first":
                out.append(copy(0, (x, y, c), (x, y, 1 - c), src=srcs[a]))
                out += [copy(1 + j, (x, y, c), (*chip, c), src=srcs[a]) for j, chip in enumerate(chips)]
            elif what == "landed":
                out += [copy(1 + j, (*chip, c), (x, y, c)) for j, chip in enumerate(chips)]
            elif what == "passed":
                out += [copy(4 + j, (*chip, c), (x, y, 1 - c)) for j, chip in enumerate(chips)]
            else:
                out.append(copy(0, (x, y, 1 - c), (x, y, c)))
                out += [copy(4 + j, (*chip, 1 - c), (x, y, c)) for j, chip in enumerate(chips)]
        return out

    def start(self, *refs):
        for cp in self._copies("first", *refs) + self._copies("mine", *refs):
            cp.start()

    def forward(self, *refs):
        for landed, passed in zip(self._copies("landed", *refs), self._copies("passed", *refs)):
            landed.wait_recv()
            passed.start()

    def finish(self, *refs):
        for cp in self._copies("from_sibling", *refs):
            cp.wait_recv()
        for cp in self._copies("first", *refs) + self._copies("passed", *refs):
            cp.wait_send()
        for cp in self._copies("mine", *refs):
            cp.wait()


class _Scatter:
    def __init__(self, arrays):
        self.n = len(arrays)
        self.out_shape = [jax.ShapeDtypeStruct((7, *a.shape[1:]), a.dtype) for a in arrays]
        self.scratch = [pltpu.SemaphoreType.DMA((7 * self.n,)), pltpu.SemaphoreType.DMA((7 * self.n,))]

    def _copies(self, srcs, dsts, send, recv):
        x, y, c = _place()
        out = []
        for a in range(self.n):
            for k, (fx, fy, fc) in enumerate(PEER_FLIPS):
                px, py, pc = _flip(x, fx), _flip(y, fy), _flip(c, fc)
                out.append(pltpu.make_async_remote_copy(
                    src_ref=srcs[a].at[4 * px + 2 * py + pc], dst_ref=dsts[a].at[k],
                    send_sem=send.at[7 * a + k], recv_sem=recv.at[7 * a + k], device_id=(px, py, pc), device_id_type=MESH))
        return out

    def start(self, *refs):
        for cp in self._copies(*refs):
            cp.start()

    def forward(self, *refs):
        pass

    def finish(self, *refs):
        for cp in self._copies(*refs):
            cp.wait_send()
        for cp in self._copies(*refs):
            cp.wait_recv()


class _ChipExchange:
    def __init__(self, arrays):
        self.n = len(arrays)
        self.out_shape = [jax.ShapeDtypeStruct(a.shape, a.dtype) for a in arrays]
        self.scratch = [pltpu.SemaphoreType.DMA((3 * self.n,)), pltpu.SemaphoreType.DMA((3 * self.n,))]

    def _copies(self, srcs, dsts, send, recv):
        x, y, c = _place()
        return [pltpu.make_async_remote_copy(
            src_ref=srcs[a].at[k], dst_ref=dsts[a].at[k], send_sem=send.at[3 * a + k], recv_sem=recv.at[3 * a + k],
            device_id=(_flip(x, fx), _flip(y, fy), c), device_id_type=MESH)
            for a in range(self.n) for k, (fx, fy) in enumerate(CHIP_FLIPS[1:])]

    def start(self, *refs):
        for cp in self._copies(*refs):
            cp.start()

    def forward(self, *refs):
        pass

    def finish(self, *refs):
        for cp in self._copies(*refs):
            cp.wait_send()
        for cp in self._copies(*refs):
            cp.wait_recv()


def _run_comm(plans, name):
    n_in = sum(p.n for p, _ in plans)

    def body(*refs):
        args, i0, s0 = [], 0, 2 * n_in
        for p, _ in plans:
            args.append((refs[i0:i0 + p.n], refs[n_in + i0:n_in + i0 + p.n], *refs[s0:s0 + len(p.scratch)]))
            i0, s0 = i0 + p.n, s0 + len(p.scratch)
        for phase in ("start", "forward", "finish"):
            for (p, _), a in zip(plans, args):
                getattr(p, phase)(*a)

    out = pl.pallas_call(
        body, name=name, out_shape=[s for p, _ in plans for s in p.out_shape], in_specs=[ANY] * n_in,
        out_specs=[ANY] * n_in, scratch_shapes=[s for p, _ in plans for s in p.scratch],
    )(*[a for _, arrays in plans for a in arrays])
    split, i0 = [], 0
    for p, _ in plans:
        split.append(out[i0:i0 + p.n])
        i0 += p.n
    return split


def _rs_sibling(arrays, name):
    n = len(arrays)

    def body(*refs):
        srcs, got, (send, recv) = refs[:n], refs[n:2 * n], refs[2 * n:]
        x, y, c = _place()
        copies = []
        for a in range(n):
            for r, (fx, fy) in enumerate(CHIP_FLIPS):
                chip = 2 * _flip(x, fx) + _flip(y, fy)
                copies.append(pltpu.make_async_remote_copy(
                    src_ref=srcs[a].at[2 * chip + 1 - c], dst_ref=got[a].at[r], send_sem=send.at[4 * a + r],
                    recv_sem=recv.at[4 * a + r], device_id=(x, y, 1 - c), device_id_type=MESH))
        for cp in copies:
            cp.start()
        for cp in copies:
            cp.wait_send()
        for cp in copies:
            cp.wait_recv()

    return pl.pallas_call(
        body, name=name, out_shape=[jax.ShapeDtypeStruct((4, *a.shape[1:]), a.dtype) for a in arrays],
        in_specs=[ANY] * n, out_specs=[ANY] * n,
        scratch_shapes=[pltpu.SemaphoreType.DMA((4 * n,)), pltpu.SemaphoreType.DMA((4 * n,))],
    )(*arrays)


def _row_tile(rows):
    return 256 if rows % 256 == 0 else rows


def _chip_slots():
    x, y, c = _place()
    return jnp.stack([4 * _flip(x, fx) + 2 * _flip(y, fy) + c for fx, fy in CHIP_FLIPS]).astype(I32)


def _pair_sum(full, theirs, name):
    _, rows, cols = theirs.shape
    tr = _row_tile(rows)

    def body(slots_ref, m0_ref, m1_ref, m2_ref, m3_ref, b_ref, own_ref, rest_ref):
        own_ref[...] = m0_ref[...].astype(F32) + b_ref[0].astype(F32)
        for k, m_ref in enumerate((m1_ref, m2_ref, m3_ref)):
            rest_ref[k] = (m_ref[...].astype(F32) + b_ref[k + 1].astype(F32)).astype(BF16)

    def mine(k):
        return pl.BlockSpec((None, tr, cols), lambda i, slots: (slots[k], i, 0))

    return pl.pallas_call(
        body, name=name,
        grid_spec=pltpu.PrefetchScalarGridSpec(
            num_scalar_prefetch=1, grid=(rows // tr,),
            in_specs=[mine(0), mine(1), mine(2), mine(3), pl.BlockSpec((4, tr, cols), lambda i, slots: (0, i, 0))],
            out_specs=(pl.BlockSpec((tr, cols), lambda i, slots: (i, 0)), pl.BlockSpec((3, tr, cols), lambda i, slots: (0, i, 0)))),
        out_shape=(jax.ShapeDtypeStruct((rows, cols), F32), jax.ShapeDtypeStruct((3, rows, cols), BF16)),
        compiler_params=_cp("parallel"),
    )(_chip_slots(), full, full, full, full, theirs)


def _head_lanes(width, h):
    lane = lax.broadcasted_iota(I32, (1, width), 1)
    if width == LANES:
        return (lane >= 64 * h) & (lane < 64 * h + 64)
    nope = (lane >= NOPE * h) & (lane < NOPE * h + NOPE)
    rope = (lane >= 2 * NOPE + ROPE * h) & (lane < 2 * NOPE + ROPE * h + ROPE)
    return nope | rope


def _dilated_bias_table(seq):
    t = min(ATTN_TILE, seq)
    nd = seq // t

    def body(o_ref):
        delta = pl.program_id(0) * t + lax.broadcasted_iota(I32, (t, t), 1) - lax.broadcasted_iota(I32, (t, t), 0)
        mult = ((delta <= 128).astype(I32) + (((delta & 3) == 0) & (delta <= 512)).astype(I32)
                + ((delta & 15) == 0).astype(I32))
        logm = jnp.where(mult == 3, math.log(3.0), jnp.where(mult == 2, math.log(2.0), 0.0))
        valid = (delta >= 0) & (mult > 0)
        dist = delta.astype(F32)
        for h in range(N_HEADS):
            o_ref[h] = jnp.where(valid, logm - 2.0 ** (-(h + 1)) * dist, NEG)

    return pl.pallas_call(
        body, name="dilated_bias_table", grid=(nd,), out_shape=jax.ShapeDtypeStruct((N_HEADS, nd, t, t), F32),
        out_specs=pl.BlockSpec((N_HEADS, None, t, t), lambda d: (0, d, 0, 0)),
        compiler_params=_cp("parallel"),
    )()


def _comm_hooks(comm, refs, n_in, n_out):
    if comm is None:
        return refs[:n_in], refs[n_in:n_in + n_out], refs[n_in + n_out:], None
    n = comm.n
    ins, srcs = refs[:n_in], refs[n_in:n_in + n]
    outs, dsts = refs[n_in + n:n_in + n + n_out], refs[n_in + n + n_out:n_in + 2 * n + n_out]
    rest = refs[n_in + 2 * n + n_out:]
    own = len(rest) - len(comm.scratch)
    return ins, outs, rest[:own], (srcs, dsts, *rest[own:])


def _attn_fwd(q, k, v, bias, *, batch, seq, width, col0, dilated, scale, name, comm=None, comm_arrays=()):
    t = min(ATTN_TILE, seq)
    nq = seq // t
    cq, ck, cv = col0
    pre = scale if dilated else 1.0
    steps = batch * N_PAIRS * nq

    def body(*refs):
        (q_ref, k_ref, v_ref, bias_ref), (o_ref, lse_ref), (v_heads,), plan = _comm_hooks(comm, refs, 4, 2)
        i = pl.program_id(2)
        step_no = (pl.program_id(0) * N_PAIRS + pl.program_id(1)) * nq + i
        if plan:
            pl.when(step_no == 0)(lambda: comm.start(*plan))
            pl.when(step_no == (3 * steps) // 4)(lambda: comm.forward(*plan))

        @pl.when(i == 0)
        def _():
            v_all = v_ref[...].astype(F32)
            for h in (0, 1):
                v_heads[h] = jnp.transpose(jnp.where(_head_lanes(LANES, h), v_all, 0.0)).astype(BF16)

        q2 = q_ref[...] * pre if dilated else q_ref[...]
        qh = [jnp.where(_head_lanes(width, h), q2, jnp.zeros_like(q2)) for h in (0, 1)]
        top = lax.broadcasted_iota(I32, (LANES, t), 0) < HEAD_V
        causal = lax.broadcasted_iota(I32, (t, t), 0) <= lax.broadcasted_iota(I32, (t, t), 1)

        def scores(j):
            kj = k_ref[pl.ds(pl.multiple_of(j * t, t), t), :]
            return [_dot_nt(kj, qh[h]) for h in (0, 1)]

        def step(j, carry, last):
            m0, l0, m1, l1, acc, s0, s1 = carry
            ahead = [] if last else scores(j + 1)
            ks = pl.multiple_of(j * t, t)
            new, alphas, pv = [], [], []
            for h, (m, l, s) in enumerate(((m0, l0, s0), (m1, l1, s1))):
                if dilated:
                    s = s + bias_ref[h, i - j]
                else:
                    s = s * scale
                    if last:
                        s = jnp.where(causal, s, NEG)
                m_new = jnp.maximum(m, jnp.max(s, axis=0, keepdims=True))
                a = jnp.exp(m - m_new)
                p = jnp.exp(s - m_new)
                new += [m_new, a * l + jnp.sum(p, axis=0, keepdims=True)]
                alphas.append(a)
                pv.append(_dot(v_heads[h, :, pl.ds(ks, t)], p.astype(BF16)))
            acc = jnp.where(top, alphas[0], alphas[1]) * acc + pv[0] + pv[1]
            return (*new, acc, *ahead)

        row = jnp.full((1, t), NEG, F32)
        zero = jnp.zeros((1, t), F32)
        init = (row, zero, row, zero, jnp.zeros((LANES, t), F32), *scores(0))
        m0, l0, m1, l1, acc = step(i, lax.fori_loop(0, i, functools.partial(step, last=False), init), True)
        o_ref[...] = jnp.transpose(acc * jnp.where(top, 1.0 / l0, 1.0 / l1)).astype(BF16)
        r = lax.broadcasted_iota(I32, (8, t), 0)
        lse_ref[...] = jnp.where(r == 0, m0 + jnp.log(l0), jnp.where(r == 1, m1 + jnp.log(l1), 0.0))
        if plan:
            pl.when(step_no == steps - 1)(lambda: comm.finish(*plan))

    bias_spec = (pl.BlockSpec((2, nq, t, t), lambda b, p, i: (p, 0, 0, 0)) if dilated
                 else pl.BlockSpec((None, 8, LANES), lambda b, p, i: (0, 0, 0)))
    n = comm.n if comm else 0
    return pl.pallas_call(
        body, name=name, grid=(batch, N_PAIRS, nq),
        out_shape=[jax.ShapeDtypeStruct((batch * seq, DIL_WIDTH), BF16), jax.ShapeDtypeStruct((batch * N_PAIRS, 8, seq), F32)]
        + (comm.out_shape if comm else []),
        in_specs=[pl.BlockSpec((t, width), lambda b, p, i: (b * nq + i, cq + p)),
                  pl.BlockSpec((seq, width), lambda b, p, i: (b, ck + p)),
                  pl.BlockSpec((seq, LANES), lambda b, p, i: (b, cv + p)),
                  bias_spec] + [ANY] * n,
        out_specs=[pl.BlockSpec((t, LANES), lambda b, p, i: (b * nq + i, p)),
                   pl.BlockSpec((None, 8, t), lambda b, p, i: (b * N_PAIRS + p, 0, i))] + [ANY] * n,
        scratch_shapes=[pltpu.VMEM((2, LANES, seq), BF16)] + (comm.scratch if comm else []),
        compiler_params=_cp("arbitrary", "arbitrary", "arbitrary") if comm else _cp("parallel", "parallel", "arbitrary"),
    )(q, k, v, bias, *comm_arrays)


def _attn_bwd(q, k, v, o, do, lse, bias, *, batch, seq, width, col0, dilated, scale, name, comm=None, comm_arrays=()):
    t = min(ATTN_TILE, seq)
    nq = seq // t
    cq, ck, cv = col0
    pre = scale if dilated else 1.0
    steps = batch * N_PAIRS

    def body(*refs):
        ins, (dq_ref, dk_ref, dv_ref), (dq_acc, dk_acc, dv_acc, rowdot, q_heads, do_heads), plan = _comm_hooks(comm, refs, 7, 3)
        q_ref, k_ref, v_ref, o_ref, do_ref, lse_ref, bias_ref = ins
        step_no = pl.program_id(0) * N_PAIRS + pl.program_id(1)
        if plan:
            pl.when(step_no == 0)(lambda: comm.start(*plan))
        wlane = [_head_lanes(width, h) for h in (0, 1)]
        vlane = [_head_lanes(LANES, h) for h in (0, 1)]
        causal = lax.broadcasted_iota(I32, (t, t), 0) <= lax.broadcasted_iota(I32, (t, t), 1)
        q_all = q_ref[...] * pre if dilated else q_ref[...]
        for h in (0, 1):
            q_heads[h] = jnp.where(wlane[h], q_all, jnp.zeros_like(q_all))
            do_heads[h] = jnp.where(vlane[h], do_ref[...], jnp.zeros_like(do_ref[...]))
        prod = jnp.transpose(do_ref[...].astype(F32) * o_ref[...].astype(F32))
        rowdot[0:1, :] = jnp.sum(prod[0:HEAD_V], axis=0, keepdims=True)
        rowdot[1:2, :] = jnp.sum(prod[HEAD_V:], axis=0, keepdims=True)
        dq_acc[...] = jnp.zeros_like(dq_acc)

        def k_tile(j, _):
            ks = pl.multiple_of(j * t, t)
            kj = k_ref[pl.ds(ks, t), :]
            vj = v_ref[pl.ds(ks, t), :]
            kh = [jnp.transpose(jnp.where(wlane[h], kj, jnp.zeros_like(kj)).astype(F32)).astype(BF16) for h in (0, 1)]
            dk_acc[...] = jnp.zeros_like(dk_acc)
            dv_acc[...] = jnp.zeros_like(dv_acc)

            def operands(i):
                qs = pl.multiple_of(i * t, t)
                return [q_heads[h, pl.ds(qs, t), :] for h in (0, 1)], [do_heads[h, pl.ds(qs, t), :] for h in (0, 1)]

            def products(i):
                qih, doih = operands(i)
                scores = tuple(_dot_nt(kj, qih[h]) for h in (0, 1))
                return scores + tuple(_dot_nt(vj, doih[h]) for h in (0, 1)) if width > LANES else scores

            def q_tile(n, carry, last):
                i = nq - 1 - n
                ahead = () if last else products(i - 1)
                qs = pl.multiple_of(i * t, t)
                qih, doih = operands(i)
                s0, s1 = carry[:2]
                dps = carry[2:] if width > LANES else [_dot_nt(vj, doih[h]) for h in (0, 1)]
                dq_i = jnp.zeros((width, t), F32)
                for h, (s, dp) in enumerate(((s0, dps[0]), (s1, dps[1]))):
                    if dilated:
                        s = s + bias_ref[h, i - j]
                    else:
                        s = s * scale
                        if last:
                            s = jnp.where(causal, s, NEG)
                    p = jnp.exp(s - lse_ref[h:h + 1, pl.ds(qs, t)])
                    ds = p * (dp - rowdot[h:h + 1, pl.ds(qs, t)])
                    ds = (ds if dilated else ds * scale).astype(BF16)
                    dv_acc[...] += _dot(p.astype(BF16), doih[h])
                    dk_acc[...] += _dot(ds, qih[h])
                    dq_i = dq_i + _dot(kh[h], ds)
                dq_acc[:, pl.ds(qs, t)] += dq_i
                return ahead

            q_tile(nq - 1 - j, lax.fori_loop(0, nq - 1 - j, functools.partial(q_tile, last=False), products(nq - 1)), True)
            dk_ref[pl.ds(ks, t), :] = dk_acc[...].astype(BF16)
            dv_ref[pl.ds(ks, t), :] = dv_acc[...].astype(BF16)
            return 0

        lax.fori_loop(0, nq, k_tile, 0)
        dq_ref[...] = (jnp.transpose(dq_acc[...]) * pre).astype(BF16)
        if plan:
            pl.when(step_no == steps - 1)(lambda: comm.finish(*plan))

    tokens = batch * seq
    bias_spec = (pl.BlockSpec((2, nq, t, t), lambda b, p: (p, 0, 0, 0)) if dilated
                 else pl.BlockSpec((None, 8, LANES), lambda b, p: (0, 0, 0)))
    n = comm.n if comm else 0
    return pl.pallas_call(
        body, name=name, grid=(batch, N_PAIRS),
        out_shape=[jax.ShapeDtypeStruct((tokens, N_PAIRS * width), BF16), jax.ShapeDtypeStruct((tokens, N_PAIRS * width), BF16),
                   jax.ShapeDtypeStruct((tokens, DIL_WIDTH), BF16)] + (comm.out_shape if comm else []),
        in_specs=[pl.BlockSpec((seq, width), lambda b, p: (b, cq + p)),
                  pl.BlockSpec((seq, width), lambda b, p: (b, ck + p)),
                  pl.BlockSpec((seq, LANES), lambda b, p: (b, cv + p)),
                  pl.BlockSpec((seq, LANES), lambda b, p: (b, p)),
                  pl.BlockSpec((seq, LANES), lambda b, p: (b, p)),
                  pl.BlockSpec((None, 8, seq), lambda b, p: (b * N_PAIRS + p, 0, 0)),
                  bias_spec] + [ANY] * n,
        out_specs=[pl.BlockSpec((seq, width), lambda b, p: (b, p)),
                   pl.BlockSpec((seq, width), lambda b, p: (b, p)),
                   pl.BlockSpec((seq, LANES), lambda b, p: (b, p))] + [ANY] * n,
        scratch_shapes=[pltpu.VMEM((width, seq), F32), pltpu.VMEM((t, width), F32), pltpu.VMEM((t, LANES), F32),
                        pltpu.VMEM((8, seq), F32), pltpu.VMEM((2, seq, width), BF16), pltpu.VMEM((2, seq, LANES), BF16)]
        + (comm.scratch if comm else []),
        compiler_params=_cp("arbitrary", "arbitrary") if comm else _cp("parallel", "parallel"),
    )(q, k, v, o, do, lse, bias, *comm_arrays)


def _rms(xf, g):
    r = lax.rsqrt(jnp.mean(xf * xf, axis=1, keepdims=True) + RMS_EPS)
    return xf * r * g, r


def _rms_bwd(dy, xf, r, g):
    gy = dy * g
    dx = r * gy - xf * (r * r * r) * jnp.mean(gy * xf, axis=1, keepdims=True)
    return dx, dy * xf * r


def _ln_bwd(dy, xhat, rstd, g):
    dxh = dy * g
    return rstd * (dxh - jnp.mean(dxh, axis=1, keepdims=True) - xhat * jnp.mean(dxh * xhat, axis=1, keepdims=True))


def _rope_slabs(q, cos, sin, transpose):
    first_half = (lax.broadcasted_iota(I32, (1, LANES), 1) % ROPE) < ROPE // 2
    out = []
    for p in range(N_PAIRS):
        blk = q[:, p * PAIR_W + LANES:(p + 1) * PAIR_W]
        y = blk * sin if transpose else blk
        up, down = pltpu.roll(y, LANES - ROPE // 2, 1), pltpu.roll(y, ROPE // 2, 1)
        rot = jnp.where(first_half, up, -down) if transpose else jnp.where(first_half, -up, down) * sin
        out += [q[:, p * PAIR_W:p * PAIR_W + LANES], blk * cos + rot]
    return jnp.concatenate(out, axis=1)


def _fwd_proj(x, w_in_ext, w1, wk, wv, g_q, g_kv, cext, sext, cs128, *, seq):
    tokens = x.shape[0]
    tm = min(TOKEN_TILE, seq)
    ns = seq // tm

    def body(x_ref, win_ref, w1_ref, wk_ref, wv_ref, gq_ref, gkv_ref, c_ref, s_ref, cs_ref,
             low_ref, gates_ref, qkvd_ref, qp_ref, kp_ref, vm_ref, qn_ref, kvn_ref, xb_ref):
        xt = x_ref[...].astype(BF16)
        xb_ref[...] = xt
        low = _dot(xt, win_ref[:, 0:LOW_W])
        low_ref[...] = low
        qkvd_ref[...] = _dot(xt, win_ref[:, LOW_W:LOW_W + 3 * DIL_WIDTH]).astype(BF16)
        gates_ref[...] = _dot(xt, win_ref[:, LOW_W + 3 * DIL_WIDTH:]).astype(BF16)
        qn = _rms(low[:, 0:Q_LORA], gq_ref[...])[0].astype(BF16)
        kvn = _rms(low[:, Q_LORA:Q_LORA + KV_LORA], gkv_ref[...])[0].astype(BF16)
        qn_ref[...] = qn
        kvn_ref[...] = kvn
        qp_ref[...] = _rope_slabs(_dot(qn, w1_ref[...]), c_ref[...], s_ref[...], False).astype(BF16)
        kr = low[:, Q_LORA + KV_LORA:] * cs_ref[...]
        kr = kr + pltpu.roll(kr, LANES - ROPE, 1)
        lane = lax.broadcasted_iota(I32, kr.shape, 1)
        kr = jnp.where(lane < ROPE, kr, 0.0)
        kr = (kr + pltpu.roll(kr, ROPE, 1)).astype(BF16)
        kn = _dot(kvn, wk_ref[...]).astype(BF16)
        kp_ref[...] = jnp.concatenate([blk for p in range(N_PAIRS) for blk in (kn[:, p * LANES:(p + 1) * LANES], kr)], axis=1)
        vm_ref[...] = _dot(kvn, wv_ref[...]).astype(BF16)

    n_gates = 2 * D_MODEL
    outs = [(LOW_W, F32), (n_gates, BF16), (3 * DIL_WIDTH, BF16), (N_PAIRS * PAIR_W, BF16), (N_PAIRS * PAIR_W, BF16),
            (DIL_WIDTH, BF16), (Q_LORA, BF16), (KV_LORA, BF16), (D_MODEL, BF16)]
    return pl.pallas_call(
        body, name="fwd_proj", grid=(tokens // tm,),
        out_shape=tuple(jax.ShapeDtypeStruct((tokens, w), dt) for w, dt in outs),
        in_specs=[_rows(tm, D_MODEL), _full(w_in_ext.shape), _full(w1.shape), _full(wk.shape),
                  _full(wv.shape), _full(g_q.shape), _full(g_kv.shape),
                  pl.BlockSpec((tm, LANES), lambda i: (i % ns, 1)),
                  pl.BlockSpec((tm, LANES), lambda i: (i % ns, 1)),
                  pl.BlockSpec((tm, LANES), lambda i: (i % ns, 0))],
        out_specs=tuple(_rows(tm, w) for w, _ in outs),
        compiler_params=_cp("parallel"),
    )(x, w_in_ext, w1, wk, wv, g_q, g_kv, cext, sext, cs128)


def _fwd_mix(o_a, o_b, gates, x, b_gate, w_oa, w_ob, w_out, ln_g, ln_b, *, seq):
    tokens = x.shape[0]
    tm = min(TOKEN_TILE, seq)

    def body(oa_ref, ob_ref, gt_ref, x_ref, bg_ref, woa_ref, wob_ref, wout_ref, g_ref, b_ref,
             hb_ref, xhat_ref, rstd_ref, ya_ref, yb_ref, mix_ref):
        ya = _dot(oa_ref[...], woa_ref[...])
        yb = _dot(ob_ref[...], wob_ref[...])
        g0 = _sigmoid(gt_ref[:, 0:D_MODEL].astype(F32) + bg_ref[0:1, :])
        g1 = _sigmoid(gt_ref[:, D_MODEL:].astype(F32) + bg_ref[1:2, :])
        mix = (g0 * ya + g1 * yb).astype(BF16)
        z = ALPHA * x_ref[...] + _dot(mix, wout_ref[...])
        zc = z - jnp.mean(z, axis=1, keepdims=True)
        rstd = lax.rsqrt(jnp.mean(zc * zc, axis=1, keepdims=True) + LN_EPS)
        xhat = zc * rstd
        hb_ref[...] = (xhat * g_ref[...] + b_ref[...]).astype(BF16)
        xhat_ref[...] = xhat
        rstd_ref[...] = jnp.broadcast_to(rstd, (tm, LANES))
        ya_ref[...] = ya.astype(BF16)
        yb_ref[...] = yb.astype(BF16)
        mix_ref[...] = mix

    outs = [(D_MODEL, BF16), (D_MODEL, F32), (LANES, F32), (D_MODEL, BF16), (D_MODEL, BF16), (D_MODEL, BF16)]
    return pl.pallas_call(
        body, name="fwd_mix", grid=(tokens // tm,),
        out_shape=tuple(jax.ShapeDtypeStruct((tokens, w), dt) for w, dt in outs),
        in_specs=[_rows(tm, DIL_WIDTH), _rows(tm, DIL_WIDTH), _rows(tm, 2 * D_MODEL), _rows(tm, D_MODEL),
                  _full(b_gate.shape), _full(w_oa.shape), _full(w_ob.shape), _full(w_out.shape),
                  _full(ln_g.shape), _full(ln_b.shape)],
        out_specs=tuple(_rows(tm, w) for w, _ in outs),
        compiler_params=_cp("parallel"),
    )(o_a, o_b, gates, x, b_gate, w_oa, w_ob, w_out, ln_g, ln_b)


def _fwd_mlp(hb, xhat1, target, w_ff1, w_ff2, ln1_g, ln1_b, ln_g, ln_b, *, seq):
    tokens = hb.shape[0]
    tm = min(2 * TOKEN_TILE, seq)
    tf = FF_SHARD
    nf = N_DEV // FF_STEP

    def body(hb_ref, xh_ref, tg_ref, w1_ref, w2_ref, g1_ref, b1_ref, g_ref, b_ref, u_ref, dz_ref, dzb_ref, stat_ref, acc):
        i, j = pl.program_id(0), pl.program_id(1)

        @pl.when((i == 0) & (j == 0))
        def _():
            stat_ref[...] = jnp.zeros_like(stat_ref)

        @pl.when(j == 0)
        def _():
            acc[...] = jnp.zeros_like(acc)

        acts = []
        for s in range(FF_STEP):
            u = _dot(hb_ref[...], w1_ref[s])
            u_ref[:, s * tf:(s + 1) * tf] = u.astype(BF16)
            acts.append(jnp.square(jnp.maximum(u, 0.0)).astype(BF16))
        acc[...] += _dot(jnp.concatenate(acts, axis=1), w2_ref[...])

        @pl.when(j == nf - 1)
        def _():
            z = ALPHA * (xh_ref[...] * g1_ref[...] + b1_ref[...]) + acc[...]
            zc = z - jnp.mean(z, axis=1, keepdims=True)
            rstd = lax.rsqrt(jnp.mean(zc * zc, axis=1, keepdims=True) + LN_EPS)
            xhat = zc * rstd
            err = xhat * g_ref[...] + b_ref[...] - tg_ref[...]
            dy = err * (1.0 / D_MODEL)
            dz = _ln_bwd(dy, xhat, rstd, g_ref[...])
            dz_ref[...] = dz
            dzb_ref[...] = dz.astype(BF16)
            stat_ref[0:1, :] += jnp.sum(dy * xhat, axis=0, keepdims=True)
            stat_ref[1:2, :] += jnp.sum(dy, axis=0, keepdims=True)
            stat_ref[2:3, :] += jnp.sum(jnp.sum(err * err, axis=1, keepdims=True), axis=0, keepdims=True) * (0.5 / D_MODEL)

    return pl.pallas_call(
        body, name="fwd_mlp", grid=(tokens // tm, nf),
        out_shape=(jax.ShapeDtypeStruct((tokens, D_FF), BF16), jax.ShapeDtypeStruct((tokens, D_MODEL), F32),
                   jax.ShapeDtypeStruct((tokens, D_MODEL), BF16), jax.ShapeDtypeStruct((8, D_MODEL), F32)),
        in_specs=[_rows(tm, D_MODEL), _rows(tm, D_MODEL), _rows(tm, D_MODEL),
                  pl.BlockSpec((FF_STEP, D_MODEL, tf), lambda i, j: (j, 0, 0)),
                  pl.BlockSpec((FF_STEP * tf, D_MODEL), lambda i, j: (j, 0)),
                  _full(ln1_g.shape), _full(ln1_b.shape), _full(ln_g.shape), _full(ln_b.shape)],
        out_specs=(pl.BlockSpec((tm, FF_STEP * tf), lambda i, j: (i, j)), _rows(tm, D_MODEL), _rows(tm, D_MODEL),
                   _full((8, D_MODEL))),
        scratch_shapes=[pltpu.VMEM((tm, D_MODEL), F32)],
        compiler_params=_cp("arbitrary", "arbitrary"),
    )(hb, xhat1, target, w_ff1, w_ff2, ln1_g, ln1_b, ln_g, ln_b)


def _bwd_mlp(dz2, dz2b, u, xhat1, rstd1, w_ff1, w_ff2, ln_g, *, seq):
    tokens = dz2.shape[0]
    tm = min(2 * TOKEN_TILE, seq)
    tf = FF_SHARD
    nf = N_DEV // FF_STEP

    def body(dz_ref, dzb_ref, u_ref, xh_ref, rs_ref, w1_ref, w2_ref, g_ref, du_ref, dz1_ref, dz1b_ref, stat_ref, acc):
        i, j = pl.program_id(0), pl.program_id(1)

        @pl.when((i == 0) & (j == 0))
        def _():
            stat_ref[...] = jnp.zeros_like(stat_ref)

        @pl.when(j == 0)
        def _():
            acc[...] = jnp.zeros_like(acc)

        da = _dot_nt(dzb_ref[...], w2_ref[...])
        du = (da * (2.0 * jnp.maximum(u_ref[...].astype(F32), 0.0))).astype(BF16)
        du_ref[...] = du
        part = _dot_nt(du[:, 0:tf], w1_ref[0])
        for s in range(1, FF_STEP):
            part = part + _dot_nt(du[:, s * tf:(s + 1) * tf], w1_ref[s])
        acc[...] += part

        @pl.when(j == nf - 1)
        def _():
            dh = ALPHA * dz_ref[...] + acc[...]
            xhat = xh_ref[...]
            dz1 = _ln_bwd(dh, xhat, rs_ref[:, 0:1], g_ref[...])
            dz1_ref[...] = dz1
            dz1b_ref[...] = dz1.astype(BF16)
            stat_ref[0:1, :] += jnp.sum(dh * xhat, axis=0, keepdims=True)
            stat_ref[1:2, :] += jnp.sum(dh, axis=0, keepdims=True)

    return pl.pallas_call(
        body, name="bwd_mlp", grid=(tokens // tm, nf),
        out_shape=(jax.ShapeDtypeStruct((tokens, D_FF), BF16), jax.ShapeDtypeStruct((tokens, D_MODEL), F32),
                   jax.ShapeDtypeStruct((tokens, D_MODEL), BF16), jax.ShapeDtypeStruct((8, D_MODEL), F32)),
        in_specs=[_rows(tm, D_MODEL), _rows(tm, D_MODEL), pl.BlockSpec((tm, FF_STEP * tf), lambda i, j: (i, j)),
                  _rows(tm, D_MODEL), _rows(tm, LANES),
                  pl.BlockSpec((FF_STEP, D_MODEL, tf), lambda i, j: (j, 0, 0)),
                  pl.BlockSpec((FF_STEP * tf, D_MODEL), lambda i, j: (j, 0)),
                  _full(ln_g.shape)],
        out_specs=(pl.BlockSpec((tm, FF_STEP * tf), lambda i, j: (i, j)), _rows(tm, D_MODEL), _rows(tm, D_MODEL),
                   _full((8, D_MODEL))),
        scratch_shapes=[pltpu.VMEM((tm, D_MODEL), F32)],
        compiler_params=_cp("arbitrary", "arbitrary"),
    )(dz2, dz2b, u, xhat1, rstd1, w_ff1, w_ff2, ln_g)


def _bwd_mix(dz1b, gates, y_a, y_b, b_gate, w_oa, w_ob, w_out, *, seq):
    tokens = dz1b.shape[0]
    tm = min(TOKEN_TILE, seq)

    def body(dz_ref, gt_ref, ya_ref, yb_ref, bg_ref, woa_ref, wob_ref, wout_ref,
             dgt_ref, dya_ref, dyb_ref, doa_ref, dob_ref, stat_ref):
        @pl.when(pl.program_id(0) == 0)
        def _():
            stat_ref[...] = jnp.zeros_like(stat_ref)

        dmix = _dot_nt(dz_ref[...], wout_ref[...])
        for k, (y_ref, w_ref, dy_ref, do_ref) in enumerate(((ya_ref, woa_ref, dya_ref, doa_ref), (yb_ref, wob_ref, dyb_ref, dob_ref))):
            g = _sigmoid(gt_ref[:, k * D_MODEL:(k + 1) * D_MODEL].astype(F32) + bg_ref[k:k + 1, :])
            dgate = dmix * y_ref[...].astype(F32) * g * (1.0 - g)
            dgt_ref[:, k * D_MODEL:(k + 1) * D_MODEL] = dgate.astype(BF16)
            stat_ref[k:k + 1, :] += jnp.sum(dgate, axis=0, keepdims=True)
            dy = (dmix * g).astype(BF16)
            dy_ref[...] = dy
            do_ref[...] = _dot_nt(dy, w_ref[...]).astype(BF16)

    outs = [(2 * D_MODEL, BF16), (D_MODEL, BF16), (D_MODEL, BF16), (DIL_WIDTH, BF16), (DIL_WIDTH, BF16)]
    return pl.pallas_call(
        body, name="bwd_mix", grid=(tokens // tm,),
        out_shape=tuple(jax.ShapeDtypeStruct((tokens, w), dt) for w, dt in outs) + (jax.ShapeDtypeStruct((8, D_MODEL), F32),),
        in_specs=[_rows(tm, D_MODEL), _rows(tm, 2 * D_MODEL), _rows(tm, D_MODEL), _rows(tm, D_MODEL),
                  _full(b_gate.shape), _full(w_oa.shape), _full(w_ob.shape), _full(w_out.shape)],
        out_specs=tuple(_rows(tm, w) for w, _ in outs) + (_full((8, D_MODEL)),),
        compiler_params=_cp("arbitrary"),
    )(dz1b, gates, y_a, y_b, b_gate, w_oa, w_ob, w_out)


def _bwd_proj(dqp, dkp, dvm, dq_d, dk_d, dv_d, dgates, dz1, low, w_in_ext, w1, wk, wv, g_q, g_kv, cext, sext, cs128, *, seq):
    tokens = dz1.shape[0]
    tm = min(TOKEN_TILE, seq)
    ns = seq // tm

    def body(dqp_ref, dkp_ref, dvm_ref, dqd_ref, dkd_ref, dvd_ref, dgt_ref, dz_ref, low_ref, win_ref, w1_ref, wk_ref,
             wv_ref, gq_ref, gkv_ref, c_ref, s_ref, cs_ref, dx_ref, dproj_ref, da_ref, dkn_ref, stat_ref):
        @pl.when(pl.program_id(0) == 0)
        def _():
            stat_ref[...] = jnp.zeros_like(stat_ref)

        low = low_ref[...]
        d_a = _rope_slabs(dqp_ref[...].astype(F32), c_ref[...], s_ref[...], True).astype(BF16)
        da_ref[...] = d_a
        q_a = low[:, 0:Q_LORA]
        _, rq = _rms(q_a, gq_ref[...])
        dq_a, gq_terms = _rms_bwd(_dot_nt(d_a, w1_ref[...]), q_a, rq, gq_ref[...])
        kv_a = low[:, Q_LORA:Q_LORA + KV_LORA]
        _, rkv = _rms(kv_a, gkv_ref[...])
        dkn = jnp.concatenate([dkp_ref[:, p * PAIR_W:p * PAIR_W + LANES] for p in range(N_PAIRS)], axis=1)
        dkn_ref[...] = dkn
        dkv_a, gkv_terms = _rms_bwd(_dot_nt(dkn, wk_ref[...]) + _dot_nt(dvm_ref[...], wv_ref[...]), kv_a, rkv, gkv_ref[...])
        dkr = sum(dkp_ref[:, p * PAIR_W + LANES:(p + 1) * PAIR_W].astype(F32) for p in range(N_PAIRS))
        dkr = dkr + pltpu.roll(dkr, LANES - ROPE, 1)
        dkr = jnp.where(lax.broadcasted_iota(I32, dkr.shape, 1) < ROPE, dkr, 0.0)
        dkr = (dkr + pltpu.roll(dkr, ROPE, 1)) * cs_ref[...]
        stat_ref[0:1, 0:Q_LORA] += jnp.sum(gq_terms, axis=0, keepdims=True)
        stat_ref[1:2, 0:KV_LORA] += jnp.sum(gkv_terms, axis=0, keepdims=True)
        dproj_ref[:, 0:Q_LORA] = dq_a.astype(BF16)
        dproj_ref[:, Q_LORA:Q_LORA + KV_LORA] = dkv_a.astype(BF16)
        dproj_ref[:, Q_LORA + KV_LORA:LOW_W] = dkr.astype(BF16)
        dproj_ref[:, LOW_W:LOW_W + DIL_WIDTH] = dqd_ref[...]
        dproj_ref[:, LOW_W + DIL_WIDTH:LOW_W + 2 * DIL_WIDTH] = dkd_ref[...]
        dproj_ref[:, LOW_W + 2 * DIL_WIDTH:LOW_W + 3 * DIL_WIDTH] = dvd_ref[...]
        dproj_ref[:, LOW_W + 3 * DIL_WIDTH:] = dgt_ref[...]
        dx_ref[...] = ALPHA * dz_ref[...] + _dot_nt(dproj_ref[...], win_ref[...])

    wide = N_PAIRS * PAIR_W
    return pl.pallas_call(
        body, name="bwd_proj", grid=(tokens // tm,),
        out_shape=(jax.ShapeDtypeStruct((tokens, D_MODEL), F32), jax.ShapeDtypeStruct((tokens, IN_EXT), BF16),
                   jax.ShapeDtypeStruct((tokens, wide), BF16), jax.ShapeDtypeStruct((tokens, N_HEADS * NOPE), BF16),
                   jax.ShapeDtypeStruct((8, D_MODEL), F32)),
        in_specs=[_rows(tm, wide), _rows(tm, wide), _rows(tm, DIL_WIDTH), _rows(tm, DIL_WIDTH), _rows(tm, DIL_WIDTH),
                  _rows(tm, DIL_WIDTH), _rows(tm, 2 * D_MODEL),
                  _rows(tm, D_MODEL), _rows(tm, LOW_W), _full(w_in_ext.shape), _full(w1.shape),
                  _full(wk.shape), _full(wv.shape), _full(g_q.shape), _full(g_kv.shape),
                  pl.BlockSpec((tm, LANES), lambda i: (i % ns, 1)), pl.BlockSpec((tm, LANES), lambda i: (i % ns, 1)),
                  pl.BlockSpec((tm, LANES), lambda i: (i % ns, 0))],
        out_specs=(_rows(tm, D_MODEL), _rows(tm, IN_EXT), _rows(tm, wide), _rows(tm, N_HEADS * NOPE), _full((8, D_MODEL))),
        compiler_params=_cp("arbitrary"),
    )(dqp, dkp, dvm, dq_d, dk_d, dv_d, dgates, dz1, low, w_in_ext, w1, wk, wv, g_q, g_kv, cext, sext, cs128)


def _wgrad(a, b, name, square_relu=False, by_shard=False):
    tokens, ka = a.shape
    n = b.shape[1]
    tka = min(ka, 512)
    shard = n // N_DEV
    tn = WGRAD_SHARDS * shard if by_shard else max(w for w in range(LANES, min(n, 2304) + 1, LANES) if n % w == 0)
    tt = min(tokens, 1024)
    nt = tokens // tt

    def body(a_ref, b_ref, o_ref, acc):
        kt = pl.program_id(2)

        @pl.when(kt == 0)
        def _():
            acc[...] = jnp.zeros_like(acc)

        at = a_ref[...]
        if square_relu:
            at = jnp.square(jnp.maximum(at.astype(F32), 0.0)).astype(BF16)
        acc[...] += _dot_tn(at, b_ref[...])

        @pl.when(kt == nt - 1)
        def _():
            if by_shard:
                for s in range(WGRAD_SHARDS):
                    o_ref[s] = acc[:, s * shard:(s + 1) * shard].astype(BF16)
            else:
                o_ref[...] = acc[...].astype(BF16)

    if by_shard:
        out_shape, out_spec = (N_DEV, ka, shard), pl.BlockSpec((WGRAD_SHARDS, tka, shard), lambda i, j, k: (j, i, 0))
    else:
        out_shape, out_spec = (ka, n), pl.BlockSpec((tka, tn), lambda i, j, k: (i, j))
    return pl.pallas_call(
        body, name=name, grid=(ka // tka, n // tn, nt), out_shape=jax.ShapeDtypeStruct(out_shape, BF16),
        in_specs=[pl.BlockSpec((tt, tka), lambda i, j, k: (k, i)), pl.BlockSpec((tt, tn), lambda i, j, k: (k, j))],
        out_specs=out_spec,
        scratch_shapes=[pltpu.VMEM((tka, tn), F32)],
        compiler_params=_cp("parallel", "parallel", "arbitrary"),
    )(a, b)


def _adam_math(w, g, m, v):
    m = ADAM_B1 * m + (1.0 - ADAM_B1) * g
    v = ADAM_B2 * v + (1.0 - ADAM_B2) * jnp.square(g)
    m_hat = m / (1.0 - ADAM_B1 ** ADAM_STEP)
    v_hat = v / (1.0 - ADAM_B2 ** ADAM_STEP)
    return -ADAM_LR * (m_hat / (jnp.sqrt(v_hat) + ADAM_EPS) + ADAM_WD * w), m, v


def _adamw(w, m, v, own, parts, name):
    rows, cols = w.shape
    tr = _row_tile(rows)
    n_parts = parts.shape[0]

    def body(slot_ref, w_ref, m_ref, v_ref, own_ref, p_ref, g_ref, d_ref, nm_ref, nv_ref):
        g = own_ref[...].astype(F32)
        for d in range(n_parts):
            g = g + p_ref[d].astype(F32)
        g_ref[...] = g
        d_ref[...], nm_ref[...], nv_ref[...] = _adam_math(w_ref[...], g, m_ref[...], v_ref[...])

    x, y, c = _place()
    blk = pl.BlockSpec((tr, cols), lambda i, slot: (i, 0))
    own_blk = blk if own.ndim == 2 else pl.BlockSpec((None, tr, cols), lambda i, slot: (slot[0], i, 0))
    return pl.pallas_call(
        body, name=name,
        grid_spec=pltpu.PrefetchScalarGridSpec(
            num_scalar_prefetch=1, grid=(rows // tr,),
            in_specs=[blk, blk, blk, own_blk, pl.BlockSpec((n_parts, tr, cols), lambda i, slot: (0, i, 0))],
            out_specs=(blk,) * 4),
        out_shape=(jax.ShapeDtypeStruct((rows, cols), F32),) * 4, compiler_params=_cp("parallel"),
    )(jnp.reshape(4 * x + 2 * y + c, (1,)).astype(I32), w, m, v, own, parts)


def _adamw_small(parts, w, m, v):
    _, rows, cols = parts.shape

    def body(p_ref, w_ref, m_ref, v_ref, g_ref, d_ref, nm_ref, nv_ref):
        g = p_ref[0]
        for d in range(1, N_DEV):
            g = g + p_ref[d]
        g_ref[...] = g
        d_ref[...], nm_ref[...], nv_ref[...] = _adam_math(w_ref[...], g, m_ref[...], v_ref[...])

    return pl.pallas_call(
        body, name="adamw_replicated", out_shape=(jax.ShapeDtypeStruct((rows, cols), F32),) * 4,
        in_specs=[_full(parts.shape)] + [_full((rows, cols))] * 3, out_specs=(_full((rows, cols)),) * 4, grid=(1,),
        compiler_params=_cp("arbitrary"),
    )(parts, w, m, v)


def _pad_rows(a2d, mult):
    pad = (-a2d.shape[-2]) % mult
    return jnp.pad(a2d, [(0, 0)] * (a2d.ndim - 2) + [(0, pad), (0, 0)]) if pad else a2d


def _pad_cols(a):
    pad = (-a.shape[-1]) % LANES
    return jnp.pad(a, [(0, 0)] * (a.ndim - 1) + [(0, pad)]) if pad else a


def _rot_cols(w):
    half = ROPE // 2
    return jnp.concatenate([-w[..., half:], w[..., :half]], axis=-1)


def _unrot_cols(dw):
    half = ROPE // 2
    return jnp.concatenate([dw[..., half:], -dw[..., :half]], axis=-1)


def _from_col_shards(stacked):
    return stacked.transpose(1, 0, 2).reshape(stacked.shape[1], -1)


def _to_col_shards(full):
    r = full.shape[0]
    return full.reshape(r, N_DEV, -1).transpose(1, 0, 2)


def _rope_tables(seq):
    half = ROPE // 2
    inv = jnp.power(ROPE_THETA, -jnp.arange(half, dtype=F32) / half)
    ang = jnp.arange(seq, dtype=F32)[:, None] * inv[None, :]
    cos = jnp.concatenate([jnp.cos(ang)] * 2, axis=1)
    sin = jnp.concatenate([jnp.sin(ang)] * 2, axis=1)
    ones, zeros = jnp.ones((seq, 2 * NOPE), F32), jnp.zeros((seq, 2 * NOPE), F32)
    pad = jnp.zeros((seq, PAIR_W - 2 * NOPE - 2 * ROPE), F32)
    cext = jnp.concatenate([ones, cos, cos, pad], axis=1)
    sext = jnp.concatenate([zeros, sin, sin, pad], axis=1)
    cs128 = jnp.concatenate([cos, sin, jnp.zeros((seq, LANES - 2 * ROPE), F32)], axis=1)
    return cext, sext, cs128


def _pair_slabs(nope, rope):
    k = nope.shape[0]
    nope = nope.reshape(k, N_PAIRS, 2 * NOPE)
    rope = jnp.zeros((k, N_PAIRS, 2 * ROPE), nope.dtype) if rope is None else rope.reshape(k, N_PAIRS, 2 * ROPE)
    pad = jnp.zeros((k, N_PAIRS, PAIR_W - 2 * NOPE - 2 * ROPE), nope.dtype)
    return jnp.concatenate([nope, rope, pad], axis=2).reshape(k, N_PAIRS * PAIR_W)


def _split_slabs(slabs):
    k = slabs.shape[0]
    s = slabs.reshape(k, N_PAIRS, PAIR_W)
    return s[:, :, :2 * NOPE].reshape(k, N_HEADS, NOPE), s[:, :, 2 * NOPE:2 * NOPE + 2 * ROPE].reshape(k, N_HEADS, ROPE)


def kernel(x, w_in, b_gate, g_q_a, w_uq, g_kv_a, w_ukv, w_o_mla, w_o_dil, w_out, ln1_g, ln1_b, w_ff1, w_ff2, ln2_g, ln2_b, loss_target, m_w_in, m_b_gate, m_g_q_a, m_w_uq, m_g_kv_a, m_w_ukv, m_w_o_mla, m_w_o_dil, m_w_out, m_ln1_g, m_ln1_b, m_w_ff1, m_w_ff2, m_ln2_g, m_ln2_b, v_w_in, v_b_gate, v_g_q_a, v_w_uq, v_g_kv_a, v_w_ukv, v_w_o_mla, v_w_o_dil, v_w_out, v_ln1_g, v_ln1_b, v_w_ff1, v_w_ff2, v_ln2_g, v_ln2_b):
    batch, seq, _ = x.shape
    tokens = batch * seq
    weights = dict(w_in=w_in, w_uq=w_uq, w_ukv=w_ukv, w_o_mla=w_o_mla, w_o_dil=w_o_dil, w_out=w_out, w_ff1=w_ff1, w_ff2=w_ff2, b_gate=b_gate)
    mom_m = dict(w_in=m_w_in, w_uq=m_w_uq, w_ukv=m_w_ukv, w_o_mla=m_w_o_mla, w_o_dil=m_w_o_dil, w_out=m_w_out, w_ff1=m_w_ff1, w_ff2=m_w_ff2, b_gate=m_b_gate)
    mom_v = dict(w_in=v_w_in, w_uq=v_w_uq, w_ukv=v_w_ukv, w_o_mla=v_w_o_mla, w_o_dil=v_w_o_dil, w_out=v_w_out, w_ff1=v_w_ff1, w_ff2=v_w_ff2, b_gate=v_b_gate)

    first = ["w_in", "w_uq", "w_ukv"]
    widths = [weights[n].shape[2] for n in first]
    shards = [_pad_cols(weights[n][0].astype(BF16)) for n in first]
    (g_in, g_uq, g_ukv), = _run_comm([(_Gather(shards), shards)], "all_gather_first_weights")
    g_uq, g_ukv = g_uq[:, :, :widths[1]], g_ukv[:, :, :widths[2]]

    s1, s2, n_in = Q_LORA + KV_LORA, Q_LORA + KV_LORA + ROPE, N_DEV * widths[0]

    def w_in_cols(lo, hi):
        out = []
        while lo < hi:
            d, off = divmod(lo, widths[0])
            take = min(hi - lo, widths[0] - off)
            out.append(g_in[d][:, off:off + take])
            lo += take
        return out

    w_in_ext = jnp.concatenate(w_in_cols(0, s2) + [_rot_cols(jnp.concatenate(w_in_cols(s1, s2), axis=1)),
                                                   jnp.zeros((D_MODEL, LOW_W - s2 - ROPE), BF16)] + w_in_cols(s2, n_in), axis=1)
    uq = _from_col_shards(g_uq).reshape(Q_LORA, N_HEADS, NOPE + ROPE)
    w1 = _pair_slabs(uq[:, :, :NOPE], uq[:, :, NOPE:])
    ukv = _from_col_shards(g_ukv).reshape(KV_LORA, N_HEADS, NOPE + HEAD_V)
    wk = ukv[:, :, :NOPE].reshape(KV_LORA, N_HEADS * NOPE)
    wv = ukv[:, :, NOPE:].reshape(KV_LORA, N_HEADS * HEAD_V)
    cext, sext, cs128 = _rope_tables(seq)
    dil_bias = _dilated_bias_table(seq)
    no_bias = jnp.zeros((1, 8, LANES), F32)

    x2 = x.reshape(tokens, D_MODEL)
    low, gates, qkvd, qp, kp, vm, qn, kvn, xb = _fwd_proj(x2, w_in_ext, w1, wk, wv, g_q_a, g_kv_a, cext, sext, cs128, seq=seq)
    bg = b_gate[0]
    bg_hi = bg.astype(BF16)
    bg_lo = (bg - bg_hi.astype(F32)).astype(BF16)
    later = [weights[n][0].astype(BF16) for n in ("w_o_mla", "w_o_dil", "w_out", "w_ff1", "w_ff2")]
    later.append(_pad_rows(jnp.concatenate([bg_hi, bg_lo], axis=0), 16))
    mla = dict(batch=batch, seq=seq, width=PAIR_W, col0=(0, 0, 0), dilated=False, scale=MLA_SCALE)
    dil = dict(batch=batch, seq=seq, width=LANES, col0=(0, N_PAIRS, 2 * N_PAIRS), dilated=True, scale=DIL_SCALE)
    o_a, lse_a, g_oa, g_ob, g_out, g_ff1, g_ff2, g_bg = _attn_fwd(
        qp, kp, vm, no_bias, name="mla_attention_fwd", comm=_Gather(later), comm_arrays=later, **mla)
    o_b, lse_b = _attn_fwd(qkvd, qkvd, qkvd, dil_bias, name="dilated_attention_fwd", **dil)
    w_oa, w_ob = _from_col_shards(g_oa), _from_col_shards(g_ob)
    w_out_full = g_out.reshape(D_MODEL, D_MODEL)
    w_ff2_full = g_ff2.reshape(D_FF, D_MODEL)
    bg_parts = g_bg.astype(F32)
    b_gate_full = _from_col_shards(bg_parts[:, 0:2] + bg_parts[:, 2:4])
    hb, xhat1, rstd1, y_a, y_b, mix = _fwd_mix(o_a, o_b, gates, x2, b_gate_full, w_oa, w_ob, w_out_full, ln1_g, ln1_b, seq=seq)
    u, dz2, dz2b, stat2 = _fwd_mlp(hb, xhat1, loss_target.reshape(tokens, D_MODEL), g_ff1, w_ff2_full, ln1_g, ln1_b, ln2_g, ln2_b, seq=seq)

    du, dz1, dz1b, stat1 = _bwd_mlp(dz2, dz2b, u, xhat1, rstd1, g_ff1, w_ff2_full, ln1_g, seq=seq)
    dw_ff = [_wgrad(hb, du, "wgrad_ff1", by_shard=True),
             _wgrad(u, dz2b, "wgrad_ff2", square_relu=True).reshape(N_DEV, FF_SHARD, D_MODEL)]
    dgates, dy_a, dy_b, do_a, do_b, stat_g = _bwd_mix(dz1b, gates, y_a, y_b, b_gate_full, w_oa, w_ob, w_out_full, seq=seq)
    dqp, dkp, dvm, r_ff1, r_ff2 = _attn_bwd(qp, kp, vm, o_a, do_a, lse_a, no_bias, name="mla_attention_bwd",
                                            comm=_Scatter(dw_ff), comm_arrays=dw_ff, **mla)
    dw_mid = [_to_col_shards(_wgrad(o_a, dy_a, "wgrad_o_mla")), _to_col_shards(_wgrad(o_b, dy_b, "wgrad_o_dil")),
              _wgrad(mix, dz1b, "wgrad_out").reshape(N_DEV, D_MODEL // N_DEV, D_MODEL),
              _pad_rows(_to_col_shards(stat_g[0:2]).astype(BF16), 16)]
    dq_d, dk_d, dv_d, r_oa, r_ob, r_out, r_bg = _attn_bwd(qkvd, qkvd, qkvd, o_b, do_b, lse_b, dil_bias, name="dilated_attention_bwd",
                                                          comm=_Scatter(dw_mid), comm_arrays=dw_mid, **dil)
    grad_x, dproj, d_a, dkn, stat_r = _bwd_proj(dqp, dkp, dvm, dq_d, dk_d, dv_d, dgates, dz1, low, w_in_ext, w1, wk, wv,
                                                g_q_a, g_kv_a, cext, sext, cs128, seq=seq)

    dw_in_ext = _wgrad(xb, dproj, "wgrad_in")
    dw1 = _wgrad(qn, d_a, "wgrad_uq")
    dwk = _wgrad(kvn, dkn, "wgrad_ukv_k")
    dwv = _wgrad(kvn, dvm, "wgrad_ukv_v")
    dw_kr = dw_in_ext[:, s1:s2] + _unrot_cols(dw_in_ext[:, s2:s2 + ROPE])

    def dw_in_cols(lo, hi):
        out = []
        for a, b, piece in ((0, s1, lambda u, v: dw_in_ext[:, u:v]), (s1, s2, lambda u, v: dw_kr[:, u - s1:v - s1]),
                            (s2, n_in, lambda u, v: dw_in_ext[:, u + LOW_W - s2:v + LOW_W - s2])):
            if max(lo, a) < min(hi, b):
                out.append(piece(max(lo, a), min(hi, b)))
        return out

    dw_in = jnp.stack([_pad_cols(jnp.concatenate(dw_in_cols(d * widths[0], (d + 1) * widths[0]), axis=1)) for d in range(N_DEV)])
    n1, r1 = _split_slabs(dw1)
    dw_uq = jnp.concatenate([n1, r1], axis=2).reshape(Q_LORA, N_HEADS * (NOPE + ROPE))
    dw_ukv = jnp.concatenate([dwk.reshape(KV_LORA, N_HEADS, NOPE), dwv.reshape(KV_LORA, N_HEADS, HEAD_V)], axis=2).reshape(KV_LORA, N_HEADS * (NOPE + HEAD_V))
    last = [dw_in] + [_pad_cols(_to_col_shards(dw)) for dw in (dw_uq, dw_ukv)]
    theirs = _rs_sibling(last, "rs_last_sibling_exchange")
    sums = [_pair_sum(a, b, "rs_last_pair_sum_" + n) for a, b, n in zip(last, theirs, first)]
    partial = jnp.concatenate([stat_r[0:1, :Q_LORA], stat_r[1:2, :KV_LORA], stat1[0:1], stat1[1:2], stat2[0:1], stat2[1:2],
                               stat2[2:3, :LANES]], axis=1)
    partial = _pad_rows(partial.reshape(-1, LANES), 8)
    rest = [s[1] for s in sums]
    got, (every,) = _run_comm([(_ChipExchange(rest), rest), (_Gather([partial]), [partial])], "rs_last_chip_exchange")

    upd = {}
    for n, w, (own, _), parts in zip(first, widths, sums, got):
        upd[n] = _adamw(weights[n][0], mom_m[n][0], mom_v[n][0], own[:, :w], parts[:, :, :w], "adamw_" + n)
    for n, own, parts in (("w_o_mla", dw_mid[0], r_oa), ("w_o_dil", dw_mid[1], r_ob), ("w_out", dw_mid[2], r_out),
                          ("w_ff1", dw_ff[0], r_ff1), ("w_ff2", dw_ff[1], r_ff2)):
        upd[n] = _adamw(weights[n][0], mom_m[n][0], mom_v[n][0], own, parts, "adamw_" + n)
    bg_upd = _adamw(_pad_rows(b_gate[0], 16), _pad_rows(m_b_gate[0], 16), _pad_rows(v_b_gate[0], 16), dw_mid[3], r_bg, "adamw_b_gate")
    upd["b_gate"] = tuple(t[0:2] for t in bg_upd)

    small_w = [g_q_a, g_kv_a, ln1_g, ln1_b, ln2_g, ln2_b]
    small_m = [m_g_q_a, m_g_kv_a, m_ln1_g, m_ln1_b, m_ln2_g, m_ln2_b]
    small_v = [v_g_q_a, v_g_kv_a, v_ln1_g, v_ln1_b, v_ln2_g, v_ln2_b]
    small_widths = [a.shape[1] for a in small_w]

    def as_rows(vecs, extra):
        flat = jnp.concatenate(vecs + [jnp.zeros((1, extra), F32)], axis=1)
        return _pad_rows(flat.reshape(-1, LANES), 8)

    g_s, d_s, nm_s, nv_s = _adamw_small(every, as_rows(small_w, LANES), as_rows(small_m, LANES), as_rows(small_v, LANES))

    def split_small(a):
        flat = a.reshape(1, -1)
        out, c0 = [], 0
        for w in small_widths:
            out.append(flat[:, c0:c0 + w])
            c0 += w
        return out, flat[0, c0]

    g_small, loss = split_small(g_s)
    small = [g_small, split_small(d_s)[0], split_small(nm_s)[0], split_small(nv_s)[0]]

    order = ["w_in", "b_gate", "g_q_a", "w_uq", "g_kv_a", "w_ukv", "w_o_mla", "w_o_dil", "w_out", "ln1_g", "ln1_b", "w_ff1", "w_ff2", "ln2_g", "ln2_b"]
    small_names = ["g_q_a", "g_kv_a", "ln1_g", "ln1_b", "ln2_g", "ln2_b"]

    def pick(kind):
        return [small[kind][small_names.index(n)] if n in small_names else upd[n][kind][None] for n in order]

    return (loss, grad_x.reshape(batch, seq, D_MODEL), *pick(0), *pick(1), *pick(2), *pick(3))
```

```python
import functools
import math

import jax
import jax.numpy as jnp
from jax import lax
from jax.experimental import pallas as pl
from jax.experimental.pallas import tpu as pltpu

F32 = jnp.float32
BF16 = jnp.bfloat16
I32 = jnp.int32

D_MODEL = 1024
N_HEADS = 8
NOPE = 64
ROPE = 32
HEAD_V = 64
Q_LORA = 384
KV_LORA = 256
DIL_WIDTH = 512
D_FF = 4096
ROPE_THETA = 10000.0
LN_EPS = 1e-5
RMS_EPS = 1e-6
NEG = -1e30
ALPHA = 2.0 ** 0.25
MLA_SCALE = (NOPE + ROPE) ** -0.5
DIL_SCALE = 64 ** -0.5
ADAM_LR, ADAM_B1, ADAM_B2, ADAM_EPS, ADAM_WD, ADAM_STEP = 0.001, 0.9, 0.999, 1e-08, 0.01, 10

LANES = 128
PAIR_W = 256
N_PAIRS = N_HEADS // 2
LOW_W = 768
IN_EXT = LOW_W + 3 * DIL_WIDTH + 2 * D_MODEL
N_DEV = 8
FF_SHARD = D_FF // N_DEV
FF_STEP = 4
WGRAD_SHARDS = 4
TOKEN_TILE = 256
MIX_TILE = 512
ATTN_TILE = 256
VMEM_LIMIT = 56 << 20

MESH = pl.DeviceIdType.MESH
ANY = pl.BlockSpec(memory_space=pl.ANY)
CHIP_FLIPS = ((0, 0), (0, 1), (1, 0), (1, 1))
PEER_FLIPS = tuple((fx, fy, fc) for fx in (0, 1) for fy in (0, 1) for fc in (0, 1))[1:]


def _cp(*sem):
    return pltpu.CompilerParams(dimension_semantics=sem or None, vmem_limit_bytes=VMEM_LIMIT)


def _full(shape):
    nd = len(shape)
    return pl.BlockSpec(shape, lambda *_: (0,) * nd)


def _rows(tm, width):
    return pl.BlockSpec((tm, width), lambda i, *_: (i, 0))


def _dot(a, b):
    return jnp.dot(a, b, preferred_element_type=F32)


def _dot_nt(a, b):
    return lax.dot_general(a, b, (((1,), (1,)), ((), ())), preferred_element_type=F32)


def _dot_tn(a, b):
    return lax.dot_general(a, b, (((0,), (0,)), ((), ())), preferred_element_type=F32)


def _sigmoid(z):
    return 1.0 / (1.0 + jnp.exp(-z))


def _place():
    return lax.axis_index("x"), lax.axis_index("y"), lax.axis_index("c")


def _flip(v, f):
    return 1 - v if f else v


class _Gather:
    def __init__(self, shards):
        self.n = len(shards)
        self.out_shape = [jax.ShapeDtypeStruct((N_DEV, *s.shape), s.dtype) for s in shards]
        self.scratch = [pltpu.SemaphoreType.DMA((7 * self.n,)), pltpu.SemaphoreType.DMA((7 * self.n,)),
                        pltpu.SemaphoreType.DMA((self.n,))]

    def _copies(self, what, srcs, dsts, send, recv, local):
        x, y, c = _place()
        chips = [(_flip(x, fx), _flip(y, fy)) for fx, fy in CHIP_FLIPS[1:]]
        out = []
        for a in range(self.n):
            def slot(px, py, pc, a=a):
                return dsts[a].at[4 * px + 2 * py + pc]

            def copy(k, block, to, src=None, a=a, slot=slot):
                return pltpu.make_async_remote_copy(
                    src_ref=slot(*block) if src is None else src, dst_ref=slot(*block),
                    send_sem=send.at[7 * a + k], recv_sem=recv.at[7 * a + k], device_id=to, device_id_type=MESH)

            if what == "mine":
                out.append(pltpu.make_async_copy(srcs[a], slot(x, y, c), local.at[a]))
            elif what == "first":
                out.append(copy(0, (x, y, c), (x, y, 1 - c), src=srcs[a]))
                out += [copy(1 + j, (x, y, c), (*chip, c), src=srcs[a]) for j, chip in enumerate(chips)]
            elif what == "landed":
                out += [copy(1 + j, (*chip, c), (x, y, c)) for j, chip in enumerate(chips)]
            elif what == "passed":
                out += [copy(4 + j, (*chip, c), (x, y, 1 - c)) for j, chip in enumerate(chips)]
            else:
                out.append(copy(0, (x, y, 1 - c), (x, y, c)))
                out += [copy(4 + j, (*chip, 1 - c), (x, y, c)) for j, chip in enumerate(chips)]
        return out

    def start(self, *refs):
        for cp in self._copies("first", *refs) + self._copies("mine", *refs):
            cp.start()

    def forward(self, *refs):
        for landed, passed in zip(self._copies("landed", *refs), self._copies("passed", *refs)):
            landed.wait_recv()
            passed.start()

    def finish(self, *refs):
        for cp in self._copies("from_sibling", *refs):
            cp.wait_recv()
        for cp in self._copies("first", *refs) + self._copies("passed", *refs):
            cp.wait_send()
        for cp in self._copies("mine", *refs):
            cp.wait()


class _Scatter:
    def __init__(self, arrays):
        self.n = len(arrays)
        self.out_shape = [jax.ShapeDtypeStruct((7, *a.shape[1:]), a.dtype) for a in arrays]
        self.scratch = [pltpu.SemaphoreType.DMA((7 * self.n,)), pltpu.SemaphoreType.DMA((7 * self.n,))]

    def _copies(self, srcs, dsts, send, recv):
        x, y, c = _place()
        out = []
        for a in range(self.n):
            for k, (fx, fy, fc) in enumerate(PEER_FLIPS):
                px, py, pc = _flip(x, fx), _flip(y, fy), _flip(c, fc)
                out.append(pltpu.make_async_remote_copy(
                    src_ref=srcs[a].at[4 * px + 2 * py + pc], dst_ref=dsts[a].at[k],
                    send_sem=send.at[7 * a + k], recv_sem=recv.at[7 * a + k], device_id=(px, py, pc), device_id_type=MESH))
        return out

    def start(self, *refs):
        for cp in self._copies(*refs):
            cp.start()

    def forward(self, *refs):
        pass

    def finish(self, *refs):
        for cp in self._copies(*refs):
            cp.wait_send()
        for cp in self._copies(*refs):
            cp.wait_recv()


class _ChipExchange:
    def __init__(self, arrays):
        self.n = len(arrays)
        self.out_shape = [jax.ShapeDtypeStruct(a.shape, a.dtype) for a in arrays]
        self.scratch = [pltpu.SemaphoreType.DMA((3 * self.n,)), pltpu.SemaphoreType.DMA((3 * self.n,))]

    def _copies(self, srcs, dsts, send, recv):
        x, y, c = _place()
        return [pltpu.make_async_remote_copy(
            src_ref=srcs[a].at[k], dst_ref=dsts[a].at[k], send_sem=send.at[3 * a + k], recv_sem=recv.at[3 * a + k],
            device_id=(_flip(x, fx), _flip(y, fy), c), device_id_type=MESH)
            for a in range(self.n) for k, (fx, fy) in enumerate(CHIP_FLIPS[1:])]

    def start(self, *refs):
        for cp in self._copies(*refs):
            cp.start()

    def forward(self, *refs):
        pass

    def finish(self, *refs):
        for cp in self._copies(*refs):
            cp.wait_send()
        for cp in self._copies(*refs):
            cp.wait_recv()


def _run_comm(plans, name):
    n_in = sum(p.n for p, _ in plans)

    def body(*refs):
        args, i0, s0 = [], 0, 2 * n_in
        for p, _ in plans:
            args.append((refs[i0:i0 + p.n], refs[n_in + i0:n_in + i0 + p.n], *refs[s0:s0 + len(p.scratch)]))
            i0, s0 = i0 + p.n, s0 + len(p.scratch)
        for phase in ("start", "forward", "finish"):
            for (p, _), a in zip(plans, args):
                getattr(p, phase)(*a)

    out = pl.pallas_call(
        body, name=name, out_shape=[s for p, _ in plans for s in p.out_shape], in_specs=[ANY] * n_in,
        out_specs=[ANY] * n_in, scratch_shapes=[s for p, _ in plans for s in p.scratch],
    )(*[a for _, arrays in plans for a in arrays])
    split, i0 = [], 0
    for p, _ in plans:
        split.append(out[i0:i0 + p.n])
        i0 += p.n
    return split


def _rs_sibling(arrays, name):
    n = len(arrays)

    def body(*refs):
        srcs, got, (send, recv) = refs[:n], refs[n:2 * n], refs[2 * n:]
        x, y, c = _place()
        copies = []
        for a in range(n):
            for r, (fx, fy) in enumerate(CHIP_FLIPS):
                chip = 2 * _flip(x, fx) + _flip(y, fy)
                copies.append(pltpu.make_async_remote_copy(
                    src_ref=srcs[a].at[2 * chip + 1 - c], dst_ref=got[a].at[r], send_sem=send.at[4 * a + r],
                    recv_sem=recv.at[4 * a + r], device_id=(x, y, 1 - c), device_id_type=MESH))
        for cp in copies:
            cp.start()
        for cp in copies:
            cp.wait_send()
        for cp in copies:
            cp.wait_recv()

    return pl.pallas_call(
        body, name=name, out_shape=[jax.ShapeDtypeStruct((4, *a.shape[1:]), a.dtype) for a in arrays],
        in_specs=[ANY] * n, out_specs=[ANY] * n,
        scratch_shapes=[pltpu.SemaphoreType.DMA((4 * n,)), pltpu.SemaphoreType.DMA((4 * n,))],
    )(*arrays)


def _row_tile(rows):
    return 256 if rows % 256 == 0 else rows


def _chip_slots():
    x, y, c = _place()
    return jnp.stack([4 * _flip(x, fx) + 2 * _flip(y, fy) + c for fx, fy in CHIP_FLIPS]).astype(I32)


def _pair_sum(full, theirs, name):
    _, rows, cols = theirs.shape
    tr = _row_tile(rows)

    def body(slots_ref, m0_ref, m1_ref, m2_ref, m3_ref, b_ref, own_ref, rest_ref):
        own_ref[...] = m0_ref[...].astype(F32) + b_ref[0].astype(F32)
        for k, m_ref in enumerate((m1_ref, m2_ref, m3_ref)):
            rest_ref[k] = (m_ref[...].astype(F32) + b_ref[k + 1].astype(F32)).astype(BF16)

    def mine(k):
        return pl.BlockSpec((None, tr, cols), lambda i, slots: (slots[k], i, 0))

    return pl.pallas_call(
        body, name=name,
        grid_spec=pltpu.PrefetchScalarGridSpec(
            num_scalar_prefetch=1, grid=(rows // tr,),
            in_specs=[mine(0), mine(1), mine(2), mine(3), pl.BlockSpec((4, tr, cols), lambda i, slots: (0, i, 0))],
            out_specs=(pl.BlockSpec((tr, cols), lambda i, slots: (i, 0)), pl.BlockSpec((3, tr, cols), lambda i, slots: (0, i, 0)))),
        out_shape=(jax.ShapeDtypeStruct((rows, cols), F32), jax.ShapeDtypeStruct((3, rows, cols), BF16)),
        compiler_params=_cp("parallel"),
    )(_chip_slots(), full, full, full, full, theirs)


def _head_lanes(width, h):
    lane = lax.broadcasted_iota(I32, (1, width), 1)
    if width == LANES:
        return (lane >= 64 * h) & (lane < 64 * h + 64)
    nope = (lane >= NOPE * h) & (lane < NOPE * h + NOPE)
    rope = (lane >= 2 * NOPE + ROPE * h) & (lane < 2 * NOPE + ROPE * h + ROPE)
    return nope | rope


def _dilated_bias_table(seq):
    t = min(ATTN_TILE, seq)
    nd = seq // t

    def body(o_ref):
        delta = pl.program_id(0) * t + lax.broadcasted_iota(I32, (t, t), 1) - lax.broadcasted_iota(I32, (t, t), 0)
        mult = ((delta <= 128).astype(I32) + (((delta & 3) == 0) & (delta <= 512)).astype(I32)
                + ((delta & 15) == 0).astype(I32))
        logm = jnp.where(mult == 3, math.log(3.0), jnp.where(mult == 2, math.log(2.0), 0.0))
        valid = (delta >= 0) & (mult > 0)
        dist = delta.astype(F32)
        for h in range(N_HEADS):
            o_ref[h] = jnp.where(valid, logm - 2.0 ** (-(h + 1)) * dist, NEG)

    return pl.pallas_call(
        body, name="dilated_bias_table", grid=(nd,), out_shape=jax.ShapeDtypeStruct((N_HEADS, nd, t, t), F32),
        out_specs=pl.BlockSpec((N_HEADS, None, t, t), lambda d: (0, d, 0, 0)),
        compiler_params=_cp("parallel"),
    )()


def _comm_hooks(comm, refs, n_in, n_out):
    if comm is None:
        return refs[:n_in], refs[n_in:n_in + n_out], refs[n_in + n_out:], None
    n = comm.n
    ins, srcs = refs[:n_in], refs[n_in:n_in + n]
    outs, dsts = refs[n_in + n:n_in + n + n_out], refs[n_in + n + n_out:n_in + 2 * n + n_out]
    rest = refs[n_in + 2 * n + n_out:]
    own = len(rest) - len(comm.scratch)
    return ins, outs, rest[:own], (srcs, dsts, *rest[own:])


def _attn_fwd(q, k, v, bias, *, batch, seq, width, col0, dilated, scale, name, comm=None, comm_arrays=()):
    t = min(ATTN_TILE, seq)
    nq = seq // t
    cq, ck, cv = col0
    pre = scale if dilated else 1.0
    steps = batch * N_PAIRS * nq

    def body(*refs):
        (q_ref, k_ref, v_ref, bias_ref), (o_ref, lse_ref), (v_heads,), plan = _comm_hooks(comm, refs, 4, 2)
        i = pl.program_id(2)
        step_no = (pl.program_id(0) * N_PAIRS + pl.program_id(1)) * nq + i
        if plan:
            pl.when(step_no == 0)(lambda: comm.start(*plan))
            pl.when(step_no == (3 * steps) // 4)(lambda: comm.forward(*plan))

        @pl.when(i == 0)
        def _():
            v_all = v_ref[...].astype(F32)
            for h in (0, 1):
                v_heads[h] = jnp.transpose(jnp.where(_head_lanes(LANES, h), v_all, 0.0)).astype(BF16)

        q2 = q_ref[...] * pre if dilated else q_ref[...]
        qh = [jnp.where(_head_lanes(width, h), q2, jnp.zeros_like(q2)) for h in (0, 1)]
        top = lax.broadcasted_iota(I32, (LANES, t), 0) < HEAD_V
        causal = lax.broadcasted_iota(I32, (t, t), 0) <= lax.broadcasted_iota(I32, (t, t), 1)

        def scores(j):
            kj = k_ref[pl.ds(pl.multiple_of(j * t, t), t), :]
            return [_dot_nt(kj, qh[h]) for h in (0, 1)]

        def step(j, carry, last):
            m0, l0, m1, l1, acc, s0, s1 = carry
            ahead = [] if last else scores(j + 1)
            ks = pl.multiple_of(j * t, t)
            new, alphas, pv = [], [], []
            for h, (m, l, s) in enumerate(((m0, l0, s0), (m1, l1, s1))):
                if dilated:
                    s = s + bias_ref[h, i - j]
                else:
                    s = s * scale
                    if last:
                        s = jnp.where(causal, s, NEG)
                m_new = jnp.maximum(m, jnp.max(s, axis=0, keepdims=True))
                a = jnp.exp(m - m_new)
                p = jnp.exp(s - m_new)
                new += [m_new, a * l + jnp.sum(p, axis=0, keepdims=True)]
                alphas.append(a)
                pv.append(_dot(v_heads[h, :, pl.ds(ks, t)], p.astype(BF16)))
            acc = jnp.where(top, alphas[0], alphas[1]) * acc + pv[0] + pv[1]
            return (*new, acc, *ahead)

        row = jnp.full((1, t), NEG, F32)
        zero = jnp.zeros((1, t), F32)
        init = (row, zero, row, zero, jnp.zeros((LANES, t), F32), *scores(0))
        m0, l0, m1, l1, acc = step(i, lax.fori_loop(0, i, functools.partial(step, last=False), init), True)
        o_ref[...] = jnp.transpose(acc * jnp.where(top, 1.0 / l0, 1.0 / l1)).astype(BF16)
        r = lax.broadcasted_iota(I32, (8, t), 0)
        lse_ref[...] = jnp.where(r == 0, m0 + jnp.log(l0), jnp.where(r == 1, m1 + jnp.log(l1), 0.0))
        if plan:
            pl.when(step_no == steps - 1)(lambda: comm.finish(*plan))

    bias_spec = (pl.BlockSpec((2, nq, t, t), lambda b, p, i: (p, 0, 0, 0)) if dilated
                 else pl.BlockSpec((None, 8, LANES), lambda b, p, i: (0, 0, 0)))
    n = comm.n if comm else 0
    return pl.pallas_call(
        body, name=name, grid=(batch, N_PAIRS, nq),
        out_shape=[jax.ShapeDtypeStruct((batch * seq, DIL_WIDTH), BF16), jax.ShapeDtypeStruct((batch * N_PAIRS, 8, seq), F32)]
        + (comm.out_shape if comm else []),
        in_specs=[pl.BlockSpec((t, width), lambda b, p, i: (b * nq + i, cq + p)),
                  pl.BlockSpec((seq, width), lambda b, p, i: (b, ck + p)),
                  pl.BlockSpec((seq, LANES), lambda b, p, i: (b, cv + p)),
                  bias_spec] + [ANY] * n,
        out_specs=[pl.BlockSpec((t, LANES), lambda b, p, i: (b * nq + i, p)),
                   pl.BlockSpec((None, 8, t), lambda b, p, i: (b * N_PAIRS + p, 0, i))] + [ANY] * n,
        scratch_shapes=[pltpu.VMEM((2, LANES, seq), BF16)] + (comm.scratch if comm else []),
        compiler_params=_cp("arbitrary", "arbitrary", "arbitrary") if comm else _cp("parallel", "parallel", "arbitrary"),
    )(q, k, v, bias, *comm_arrays)


def _attn_bwd(q, k, v, o, do, lse, bias, *, batch, seq, width, col0, dilated, scale, name, comm=None, comm_arrays=()):
    t = min(ATTN_TILE, seq)
    nq = seq // t
    cq, ck, cv = col0
    pre = scale if dilated else 1.0
    dq_transposed = width == LANES
    steps = batch * N_PAIRS

    def body(*refs):
        ins, (dq_ref, dk_ref, dv_ref), (dq_acc, dk_acc, dv_acc, rowdot, q_heads, do_heads), plan = _comm_hooks(comm, refs, 7, 3)
        q_ref, k_ref, v_ref, o_ref, do_ref, lse_ref, bias_ref = ins
        step_no = pl.program_id(0) * N_PAIRS + pl.program_id(1)
        if plan:
            pl.when(step_no == 0)(lambda: comm.start(*plan))
        wlane = [_head_lanes(width, h) for h in (0, 1)]
        vlane = [_head_lanes(LANES, h) for h in (0, 1)]
        causal = lax.broadcasted_iota(I32, (t, t), 0) <= lax.broadcasted_iota(I32, (t, t), 1)
        q_all = q_ref[...] * pre if dilated else q_ref[...]
        for h in (0, 1):
            q_heads[h] = jnp.where(wlane[h], q_all, jnp.zeros_like(q_all))
            do_heads[h] = jnp.where(vlane[h], do_ref[...], jnp.zeros_like(do_ref[...]))
        prod = jnp.transpose(do_ref[...].astype(F32) * o_ref[...].astype(F32))
        rowdot[0:1, :] = jnp.sum(prod[0:HEAD_V], axis=0, keepdims=True)
        rowdot[1:2, :] = jnp.sum(prod[HEAD_V:], axis=0, keepdims=True)
        dq_acc[...] = jnp.zeros_like(dq_acc)

        def k_tile(j, _):
            ks = pl.multiple_of(j * t, t)
            kj = k_ref[pl.ds(ks, t), :]
            vj = v_ref[pl.ds(ks, t), :]
            kh = [jnp.where(wlane[h], kj, jnp.zeros_like(kj)) for h in (0, 1)]
            if dq_transposed:
                kh = [jnp.transpose(kh[h].astype(F32)).astype(BF16) for h in (0, 1)]
            dk_acc[...] = jnp.zeros_like(dk_acc)
            dv_acc[...] = jnp.zeros_like(dv_acc)

            def operands(i):
                qs = pl.multiple_of(i * t, t)
                return [q_heads[h, pl.ds(qs, t), :] for h in (0, 1)], [do_heads[h, pl.ds(qs, t), :] for h in (0, 1)]

            def products(i):
                qih, doih = operands(i)
                scores = tuple(_dot_nt(kj, qih[h]) for h in (0, 1))
                return scores + tuple(_dot_nt(vj, doih[h]) for h in (0, 1)) if width > LANES else scores

            def q_tile(n, carry, last):
                i = nq - 1 - n
                ahead = () if last else products(i - 1)
                qs = pl.multiple_of(i * t, t)
                qih, doih = operands(i)
                s0, s1 = carry[:2]
                dps = carry[2:] if width > LANES else [_dot_nt(vj, doih[h]) for h in (0, 1)]
                dq_i = jnp.zeros((width, t) if dq_transposed else (t, width), F32)
                for h, (s, dp) in enumerate(((s0, dps[0]), (s1, dps[1]))):
                    if dilated:
                        s = s + bias_ref[h, i - j]
                    else:
                        s = s * scale
                        if last:
                            s = jnp.where(causal, s, NEG)
                    p = jnp.exp(s - lse_ref[h:h + 1, pl.ds(qs, t)])
                    ds = p * (dp - rowdot[h:h + 1, pl.ds(qs, t)])
                    ds = (ds if dilated else ds * scale).astype(BF16)
                    dv_acc[...] += _dot(p.astype(BF16), doih[h])
                    dk_acc[...] += _dot(ds, qih[h])
                    dq_i = dq_i + (_dot(kh[h], ds) if dq_transposed else _dot_tn(ds, kh[h]))
                if dq_transposed:
                    dq_acc[:, pl.ds(qs, t)] += dq_i
                else:
                    dq_acc[pl.ds(qs, t), :] += dq_i
                return ahead

            q_tile(nq - 1 - j, lax.fori_loop(0, nq - 1 - j, functools.partial(q_tile, last=False), products(nq - 1)), True)
            dk_ref[pl.ds(ks, t), :] = dk_acc[...].astype(BF16)
            dv_ref[pl.ds(ks, t), :] = dv_acc[...].astype(BF16)
            return 0

        lax.fori_loop(0, nq, k_tile, 0)
        dq_ref[...] = ((jnp.transpose(dq_acc[...]) if dq_transposed else dq_acc[...]) * pre).astype(BF16)
        if plan:
            pl.when(step_no == steps - 1)(lambda: comm.finish(*plan))

    tokens = batch * seq
    bias_spec = (pl.BlockSpec((2, nq, t, t), lambda b, p: (p, 0, 0, 0)) if dilated
                 else pl.BlockSpec((None, 8, LANES), lambda b, p: (0, 0, 0)))
    n = comm.n if comm else 0
    return pl.pallas_call(
        body, name=name, grid=(batch, N_PAIRS),
        out_shape=[jax.ShapeDtypeStruct((tokens, N_PAIRS * width), BF16), jax.ShapeDtypeStruct((tokens, N_PAIRS * width), BF16),
                   jax.ShapeDtypeStruct((tokens, DIL_WIDTH), BF16)] + (comm.out_shape if comm else []),
        in_specs=[pl.BlockSpec((seq, width), lambda b, p: (b, cq + p)),
                  pl.BlockSpec((seq, width), lambda b, p: (b, ck + p)),
                  pl.BlockSpec((seq, LANES), lambda b, p: (b, cv + p)),
                  pl.BlockSpec((seq, LANES), lambda b, p: (b, p)),
                  pl.BlockSpec((seq, LANES), lambda b, p: (b, p)),
                  pl.BlockSpec((None, 8, seq), lambda b, p: (b * N_PAIRS + p, 0, 0)),
                  bias_spec] + [ANY] * n,
        out_specs=[pl.BlockSpec((seq, width), lambda b, p: (b, p)),
                   pl.BlockSpec((seq, width), lambda b, p: (b, p)),
                   pl.BlockSpec((seq, LANES), lambda b, p: (b, p))] + [ANY] * n,
        scratch_shapes=[pltpu.VMEM((width, seq) if dq_transposed else (seq, width), F32),
                        pltpu.VMEM((t, width), F32), pltpu.VMEM((t, LANES), F32),
                        pltpu.VMEM((8, seq), F32), pltpu.VMEM((2, seq, width), BF16), pltpu.VMEM((2, seq, LANES), BF16)]
        + (comm.scratch if comm else []),
        compiler_params=_cp("arbitrary", "arbitrary") if comm else _cp("parallel", "parallel"),
    )(q, k, v, o, do, lse, bias, *comm_arrays)


def _rms(xf, g):
    r = lax.rsqrt(jnp.mean(xf * xf, axis=1, keepdims=True) + RMS_EPS)
    return xf * r * g, r


def _rms_bwd(dy, xf, r, g):
    gy = dy * g
    dx = r * gy - xf * (r * r * r) * jnp.mean(gy * xf, axis=1, keepdims=True)
    return dx, dy * xf * r


def _ln_bwd(dy, xhat, rstd, g):
    dxh = dy * g
    return rstd * (dxh - jnp.mean(dxh, axis=1, keepdims=True) - xhat * jnp.mean(dxh * xhat, axis=1, keepdims=True))


def _rope_slabs(q, cos, sin, transpose):
    first_half = (lax.broadcasted_iota(I32, (1, LANES), 1) % ROPE) < ROPE // 2
    out = []
    for p in range(N_PAIRS):
        blk = q[:, p * PAIR_W + LANES:(p + 1) * PAIR_W]
        y = blk * sin if transpose else blk
        up, down = pltpu.roll(y, LANES - ROPE // 2, 1), pltpu.roll(y, ROPE // 2, 1)
        rot = jnp.where(first_half, up, -down) if transpose else jnp.where(first_half, -up, down) * sin
        out += [q[:, p * PAIR_W:p * PAIR_W + LANES], blk * cos + rot]
    return jnp.concatenate(out, axis=1)


def _fwd_proj(x, w_in_ext, w1, wk, wv, g_q, g_kv, cext, sext, cs128, *, seq):
    tokens = x.shape[0]
    tm = min(TOKEN_TILE, seq)
    ns = seq // tm

    def body(x_ref, win_ref, w1_ref, wk_ref, wv_ref, gq_ref, gkv_ref, c_ref, s_ref, cs_ref,
             low_ref, gates_ref, qkvd_ref, qp_ref, kp_ref, vm_ref, qn_ref, kvn_ref, xb_ref):
        xt = x_ref[...].astype(BF16)
        xb_ref[...] = xt
        low = _dot(xt, win_ref[:, 0:LOW_W])
        low_ref[...] = low
        qkvd_ref[...] = _dot(xt, win_ref[:, LOW_W:LOW_W + 3 * DIL_WIDTH]).astype(BF16)
        gates_ref[...] = _dot(xt, win_ref[:, LOW_W + 3 * DIL_WIDTH:]).astype(BF16)
        qn = _rms(low[:, 0:Q_LORA], gq_ref[...])[0].astype(BF16)
        kvn = _rms(low[:, Q_LORA:Q_LORA + KV_LORA], gkv_ref[...])[0].astype(BF16)
        qn_ref[...] = qn
        kvn_ref[...] = kvn
        qp_ref[...] = _rope_slabs(_dot(qn, w1_ref[...]), c_ref[...], s_ref[...], False).astype(BF16)
        kr = low[:, Q_LORA + KV_LORA:] * cs_ref[...]
        kr = kr + pltpu.roll(kr, LANES - ROPE, 1)
        lane = lax.broadcasted_iota(I32, kr.shape, 1)
        kr = jnp.where(lane < ROPE, kr, 0.0)
        kr = (kr + pltpu.roll(kr, ROPE, 1)).astype(BF16)
        kn = _dot(kvn, wk_ref[...]).astype(BF16)
        kp_ref[...] = jnp.concatenate([blk for p in range(N_PAIRS) for blk in (kn[:, p * LANES:(p + 1) * LANES], kr)], axis=1)
        vm_ref[...] = _dot(kvn, wv_ref[...]).astype(BF16)

    n_gates = 2 * D_MODEL
    outs = [(LOW_W, F32), (n_gates, BF16), (3 * DIL_WIDTH, BF16), (N_PAIRS * PAIR_W, BF16), (N_PAIRS * PAIR_W, BF16),
            (DIL_WIDTH, BF16), (Q_LORA, BF16), (KV_LORA, BF16), (D_MODEL, BF16)]
    return pl.pallas_call(
        body, name="fwd_proj", grid=(tokens // tm,),
        out_shape=tuple(jax.ShapeDtypeStruct((tokens, w), dt) for w, dt in outs),
        in_specs=[_rows(tm, D_MODEL), _full(w_in_ext.shape), _full(w1.shape), _full(wk.shape),
                  _full(wv.shape), _full(g_q.shape), _full(g_kv.shape),
                  pl.BlockSpec((tm, LANES), lambda i: (i % ns, 1)),
                  pl.BlockSpec((tm, LANES), lambda i: (i % ns, 1)),
                  pl.BlockSpec((tm, LANES), lambda i: (i % ns, 0))],
        out_specs=tuple(_rows(tm, w) for w, _ in outs),
        compiler_params=_cp("parallel"),
    )(x, w_in_ext, w1, wk, wv, g_q, g_kv, cext, sext, cs128)


def _fwd_mix(o_a, o_b, gates, x, b_gate, w_oa, w_ob, w_out, ln_g, ln_b, *, seq):
    tokens = x.shape[0]
    tm = min(MIX_TILE, seq)

    def body(oa_ref, ob_ref, gt_ref, x_ref, bg_ref, woa_ref, wob_ref, wout_ref, g_ref, b_ref,
             hb_ref, xhat_ref, rstd_ref, ya_ref, yb_ref, mix_ref):
        ya = _dot(oa_ref[...], woa_ref[...])
        yb = _dot(ob_ref[...], wob_ref[...])
        g0 = _sigmoid(gt_ref[:, 0:D_MODEL].astype(F32) + bg_ref[0:1, :])
        g1 = _sigmoid(gt_ref[:, D_MODEL:].astype(F32) + bg_ref[1:2, :])
        mix = (g0 * ya + g1 * yb).astype(BF16)
        z = ALPHA * x_ref[...] + _dot(mix, wout_ref[...])
        zc = z - jnp.mean(z, axis=1, keepdims=True)
        rstd = lax.rsqrt(jnp.mean(zc * zc, axis=1, keepdims=True) + LN_EPS)
        xhat = zc * rstd
        hb_ref[...] = (xhat * g_ref[...] + b_ref[...]).astype(BF16)
        xhat_ref[...] = xhat
        rstd_ref[...] = jnp.broadcast_to(rstd, (tm, LANES))
        ya_ref[...] = ya.astype(BF16)
        yb_ref[...] = yb.astype(BF16)
        mix_ref[...] = mix

    outs = [(D_MODEL, BF16), (D_MODEL, F32), (LANES, F32), (D_MODEL, BF16), (D_MODEL, BF16), (D_MODEL, BF16)]
    return pl.pallas_call(
        body, name="fwd_mix", grid=(tokens // tm,),
        out_shape=tuple(jax.ShapeDtypeStruct((tokens, w), dt) for w, dt in outs),
        in_specs=[_rows(tm, DIL_WIDTH), _rows(tm, DIL_WIDTH), _rows(tm, 2 * D_MODEL), _rows(tm, D_MODEL),
                  _full(b_gate.shape), _full(w_oa.shape), _full(w_ob.shape), _full(w_out.shape),
                  _full(ln_g.shape), _full(ln_b.shape)],
        out_specs=tuple(_rows(tm, w) for w, _ in outs),
        compiler_params=_cp("parallel"),
    )(o_a, o_b, gates, x, b_gate, w_oa, w_ob, w_out, ln_g, ln_b)


def _fwd_mlp(hb, xhat1, target, w_ff1, w_ff2, ln1_g, ln1_b, ln_g, ln_b, *, seq):
    tokens = hb.shape[0]
    tm = min(2 * TOKEN_TILE, seq)
    tf = FF_SHARD
    nf = N_DEV // FF_STEP

    def body(hb_ref, xh_ref, tg_ref, w1_ref, w2_ref, g1_ref, b1_ref, g_ref, b_ref, u_ref, dz_ref, dzb_ref, stat_ref, acc):
        i, j = pl.program_id(0), pl.program_id(1)

        @pl.when((i == 0) & (j == 0))
        def _():
            stat_ref[...] = jnp.zeros_like(stat_ref)

        @pl.when(j == 0)
        def _():
            acc[...] = jnp.zeros_like(acc)

        acts = []
        for s in range(FF_STEP):
            u = _dot(hb_ref[...], w1_ref[s])
            u_ref[:, s * tf:(s + 1) * tf] = u.astype(BF16)
            acts.append(jnp.square(jnp.maximum(u, 0.0)).astype(BF16))
        acc[...] += _dot(jnp.concatenate(acts, axis=1), w2_ref[...])

        @pl.when(j == nf - 1)
        def _():
            z = ALPHA * (xh_ref[...] * g1_ref[...] + b1_ref[...]) + acc[...]
            zc = z - jnp.mean(z, axis=1, keepdims=True)
            rstd = lax.rsqrt(jnp.mean(zc * zc, axis=1, keepdims=True) + LN_EPS)
            xhat = zc * rstd
            err = xhat * g_ref[...] + b_ref[...] - tg_ref[...]
            dy = err * (1.0 / D_MODEL)
            dz = _ln_bwd(dy, xhat, rstd, g_ref[...])
            dz_ref[...] = dz
            dzb_ref[...] = dz.astype(BF16)
            stat_ref[0:1, :] += jnp.sum(dy * xhat, axis=0, keepdims=True)
            stat_ref[1:2, :] += jnp.sum(dy, axis=0, keepdims=True)
            stat_ref[2:3, :] += jnp.sum(jnp.sum(err * err, axis=1, keepdims=True), axis=0, keepdims=True) * (0.5 / D_MODEL)

    return pl.pallas_call(
        body, name="fwd_mlp", grid=(tokens // tm, nf),
        out_shape=(jax.ShapeDtypeStruct((tokens, D_FF), BF16), jax.ShapeDtypeStruct((tokens, D_MODEL), F32),
                   jax.ShapeDtypeStruct((tokens, D_MODEL), BF16), jax.ShapeDtypeStruct((8, D_MODEL), F32)),
        in_specs=[_rows(tm, D_MODEL), _rows(tm, D_MODEL), _rows(tm, D_MODEL),
                  pl.BlockSpec((FF_STEP, D_MODEL, tf), lambda i, j: (j, 0, 0)),
                  pl.BlockSpec((FF_STEP * tf, D_MODEL), lambda i, j: (j, 0)),
                  _full(ln1_g.shape), _full(ln1_b.shape), _full(ln_g.shape), _full(ln_b.shape)],
        out_specs=(pl.BlockSpec((tm, FF_STEP * tf), lambda i, j: (i, j)), _rows(tm, D_MODEL), _rows(tm, D_MODEL),
                   _full((8, D_MODEL))),
        scratch_shapes=[pltpu.VMEM((tm, D_MODEL), F32)],
        compiler_params=_cp("arbitrary", "arbitrary"),
    )(hb, xhat1, target, w_ff1, w_ff2, ln1_g, ln1_b, ln_g, ln_b)


def _bwd_mlp(dz2, dz2b, u, xhat1, rstd1, w_ff1, w_ff2, ln_g, *, seq):
    tokens = dz2.shape[0]
    tm = min(2 * TOKEN_TILE, seq)
    tf = FF_SHARD
    nf = N_DEV // FF_STEP

    def body(dz_ref, dzb_ref, u_ref, xh_ref, rs_ref, w1_ref, w2_ref, g_ref, du_ref, dz1_ref, dz1b_ref, stat_ref, acc):
        i, j = pl.program_id(0), pl.program_id(1)

        @pl.when((i == 0) & (j == 0))
        def _():
            stat_ref[...] = jnp.zeros_like(stat_ref)

        @pl.when(j == 0)
        def _():
            acc[...] = jnp.zeros_like(acc)

        da = _dot_nt(dzb_ref[...], w2_ref[...])
        du = (da * (2.0 * jnp.maximum(u_ref[...].astype(F32), 0.0))).astype(BF16)
        du_ref[...] = du
        part = _dot_nt(du[:, 0:tf], w1_ref[0])
        for s in range(1, FF_STEP):
            part = part + _dot_nt(du[:, s * tf:(s + 1) * tf], w1_ref[s])
        acc[...] += part

        @pl.when(j == nf - 1)
        def _():
            dh = ALPHA * dz_ref[...] + acc[...]
            xhat = xh_ref[...]
            dz1 = _ln_bwd(dh, xhat, rs_ref[:, 0:1], g_ref[...])
            dz1_ref[...] = dz1
            dz1b_ref[...] = dz1.astype(BF16)
            stat_ref[0:1, :] += jnp.sum(dh * xhat, axis=0, keepdims=True)
            stat_ref[1:2, :] += jnp.sum(dh, axis=0, keepdims=True)

    return pl.pallas_call(
        body, name="bwd_mlp", grid=(tokens // tm, nf),
        out_shape=(jax.ShapeDtypeStruct((tokens, D_FF), BF16), jax.ShapeDtypeStruct((tokens, D_MODEL), F32),
                   jax.ShapeDtypeStruct((tokens, D_MODEL), BF16), jax.ShapeDtypeStruct((8, D_MODEL), F32)),
        in_specs=[_rows(tm, D_MODEL), _rows(tm, D_MODEL), pl.BlockSpec((tm, FF_STEP * tf), lambda i, j: (i, j)),
                  _rows(tm, D_MODEL), _rows(tm, LANES),
                  pl.BlockSpec((FF_STEP, D_MODEL, tf), lambda i, j: (j, 0, 0)),
                  pl.BlockSpec((FF_STEP * tf, D_MODEL), lambda i, j: (j, 0)),
                  _full(ln_g.shape)],
        out_specs=(pl.BlockSpec((tm, FF_STEP * tf), lambda i, j: (i, j)), _rows(tm, D_MODEL), _rows(tm, D_MODEL),
                   _full((8, D_MODEL))),
        scratch_shapes=[pltpu.VMEM((tm, D_MODEL), F32)],
        compiler_params=_cp("arbitrary", "arbitrary"),
    )(dz2, dz2b, u, xhat1, rstd1, w_ff1, w_ff2, ln_g)


def _bwd_mix(dz1b, gates, y_a, y_b, b_gate, w_oa, w_ob, w_out, *, seq):
    tokens = dz1b.shape[0]
    tm = min(MIX_TILE, seq)

    def body(dz_ref, gt_ref, ya_ref, yb_ref, bg_ref, woa_ref, wob_ref, wout_ref,
             dgt_ref, dya_ref, dyb_ref, doa_ref, dob_ref, stat_ref):
        @pl.when(pl.program_id(0) == 0)
        def _():
            stat_ref[...] = jnp.zeros_like(stat_ref)

        dmix = _dot_nt(dz_ref[...], wout_ref[...])
        for k, (y_ref, w_ref, dy_ref, do_ref) in enumerate(((ya_ref, woa_ref, dya_ref, doa_ref), (yb_ref, wob_ref, dyb_ref, dob_ref))):
            g = _sigmoid(gt_ref[:, k * D_MODEL:(k + 1) * D_MODEL].astype(F32) + bg_ref[k:k + 1, :])
            dgate = dmix * y_ref[...].astype(F32) * g * (1.0 - g)
            dgt_ref[:, k * D_MODEL:(k + 1) * D_MODEL] = dgate.astype(BF16)
            stat_ref[k:k + 1, :] += jnp.sum(dgate, axis=0, keepdims=True)
            dy = (dmix * g).astype(BF16)
            dy_ref[...] = dy
            do_ref[...] = _dot_nt(dy, w_ref[...]).astype(BF16)

    outs = [(2 * D_MODEL, BF16), (D_MODEL, BF16), (D_MODEL, BF16), (DIL_WIDTH, BF16), (DIL_WIDTH, BF16)]
    return pl.pallas_call(
        body, name="bwd_mix", grid=(tokens // tm,),
        out_shape=tuple(jax.ShapeDtypeStruct((tokens, w), dt) for w, dt in outs) + (jax.ShapeDtypeStruct((8, D_MODEL), F32),),
        in_specs=[_rows(tm, D_MODEL), _rows(tm, 2 * D_MODEL), _rows(tm, D_MODEL), _rows(tm, D_MODEL),
                  _full(b_gate.shape), _full(w_oa.shape), _full(w_ob.shape), _full(w_out.shape)],
        out_specs=tuple(_rows(tm, w) for w, _ in outs) + (_full((8, D_MODEL)),),
        compiler_params=_cp("arbitrary"),
    )(dz1b, gates, y_a, y_b, b_gate, w_oa, w_ob, w_out)


def _bwd_proj(dqp, dkp, dvm, dq_d, dk_d, dv_d, dgates, dz1, low, w_in_ext, w1, wk, wv, g_q, g_kv, cext, sext, cs128, *, seq):
    tokens = dz1.shape[0]
    tm = min(TOKEN_TILE, seq)
    ns = seq // tm

    def body(dqp_ref, dkp_ref, dvm_ref, dqd_ref, dkd_ref, dvd_ref, dgt_ref, dz_ref, low_ref, win_ref, w1_ref, wk_ref,
             wv_ref, gq_ref, gkv_ref, c_ref, s_ref, cs_ref, dx_ref, dproj_ref, da_ref, dkn_ref, stat_ref):
        @pl.when(pl.program_id(0) == 0)
        def _():
            stat_ref[...] = jnp.zeros_like(stat_ref)

        low = low_ref[...]
        d_a = _rope_slabs(dqp_ref[...].astype(F32), c_ref[...], s_ref[...], True).astype(BF16)
        da_ref[...] = d_a
        q_a = low[:, 0:Q_LORA]
        _, rq = _rms(q_a, gq_ref[...])
        dq_a, gq_terms = _rms_bwd(_dot_nt(d_a, w1_ref[...]), q_a, rq, gq_ref[...])
        kv_a = low[:, Q_LORA:Q_LORA + KV_LORA]
        _, rkv = _rms(kv_a, gkv_ref[...])
        dkn = jnp.concatenate([dkp_ref[:, p * PAIR_W:p * PAIR_W + LANES] for p in range(N_PAIRS)], axis=1)
        dkn_ref[...] = dkn
        dkv_a, gkv_terms = _rms_bwd(_dot_nt(dkn, wk_ref[...]) + _dot_nt(dvm_ref[...], wv_ref[...]), kv_a, rkv, gkv_ref[...])
        dkr = sum(dkp_ref[:, p * PAIR_W + LANES:(p + 1) * PAIR_W].astype(F32) for p in range(N_PAIRS))
        dkr = dkr + pltpu.roll(dkr, LANES - ROPE, 1)
        dkr = jnp.where(lax.broadcasted_iota(I32, dkr.shape, 1) < ROPE, dkr, 0.0)
        dkr = (dkr + pltpu.roll(dkr, ROPE, 1)) * cs_ref[...]
        stat_ref[0:1, 0:Q_LORA] += jnp.sum(gq_terms, axis=0, keepdims=True)
        stat_ref[1:2, 0:KV_LORA] += jnp.sum(gkv_terms, axis=0, keepdims=True)
        dproj_ref[:, 0:Q_LORA] = dq_a.astype(BF16)
        dproj_ref[:, Q_LORA:Q_LORA + KV_LORA] = dkv_a.astype(BF16)
        dproj_ref[:, Q_LORA + KV_LORA:LOW_W] = dkr.astype(BF16)
        dproj_ref[:, LOW_W:LOW_W + DIL_WIDTH] = dqd_ref[...]
        dproj_ref[:, LOW_W + DIL_WIDTH:LOW_W + 2 * DIL_WIDTH] = dkd_ref[...]
        dproj_ref[:, LOW_W + 2 * DIL_WIDTH:LOW_W + 3 * DIL_WIDTH] = dvd_ref[...]
        dproj_ref[:, LOW_W + 3 * DIL_WIDTH:] = dgt_ref[...]
        dx_ref[...] = ALPHA * dz_ref[...] + _dot_nt(dproj_ref[...], win_ref[...])

    wide = N_PAIRS * PAIR_W
    return pl.pallas_call(
        body, name="bwd_proj", grid=(tokens // tm,),
        out_shape=(jax.ShapeDtypeStruct((tokens, D_MODEL), F32), jax.ShapeDtypeStruct((tokens, IN_EXT), BF16),
                   jax.ShapeDtypeStruct((tokens, wide), BF16), jax.ShapeDtypeStruct((tokens, N_HEADS * NOPE), BF16),
                   jax.ShapeDtypeStruct((8, D_MODEL), F32)),
        in_specs=[_rows(tm, wide), _rows(tm, wide), _rows(tm, DIL_WIDTH), _rows(tm, DIL_WIDTH), _rows(tm, DIL_WIDTH),
                  _rows(tm, DIL_WIDTH), _rows(tm, 2 * D_MODEL),
                  _rows(tm, D_MODEL), _rows(tm, LOW_W), _full(w_in_ext.shape), _full(w1.shape),
                  _full(wk.shape), _full(wv.shape), _full(g_q.shape), _full(g_kv.shape),
                  pl.BlockSpec((tm, LANES), lambda i: (i % ns, 1)), pl.BlockSpec((tm, LANES), lambda i: (i % ns, 1)),
                  pl.BlockSpec((tm, LANES), lambda i: (i % ns, 0))],
        out_specs=(_rows(tm, D_MODEL), _rows(tm, IN_EXT), _rows(tm, wide), _rows(tm, N_HEADS * NOPE), _full((8, D_MODEL))),
        compiler_params=_cp("arbitrary"),
    )(dqp, dkp, dvm, dq_d, dk_d, dv_d, dgates, dz1, low, w_in_ext, w1, wk, wv, g_q, g_kv, cext, sext, cs128)


def _wgrad(a, b, name, square_relu=False, by_shard=False):
    tokens, ka = a.shape
    n = b.shape[1]
    tka = min(ka, 512)
    shard = n // N_DEV
    tn = WGRAD_SHARDS * shard if by_shard else max(w for w in range(LANES, min(n, 2304) + 1, LANES) if n % w == 0)
    tt = min(tokens, 1024)
    nt = tokens // tt

    def body(a_ref, b_ref, o_ref, acc):
        kt = pl.program_id(2)

        @pl.when(kt == 0)
        def _():
            acc[...] = jnp.zeros_like(acc)

        at = a_ref[...]
        if square_relu:
            at = jnp.square(jnp.maximum(at.astype(F32), 0.0)).astype(BF16)
        acc[...] += _dot_tn(at, b_ref[...])

        @pl.when(kt == nt - 1)
        def _():
            if by_shard:
                for s in range(WGRAD_SHARDS):
                    o_ref[s] = acc[:, s * shard:(s + 1) * shard].astype(BF16)
            else:
                o_ref[...] = acc[...].astype(BF16)

    if by_shard:
        out_shape, out_spec = (N_DEV, ka, shard), pl.BlockSpec((WGRAD_SHARDS, tka, shard), lambda i, j, k: (j, i, 0))
    else:
        out_shape, out_spec = (ka, n), pl.BlockSpec((tka, tn), lambda i, j, k: (i, j))
    return pl.pallas_call(
        body, name=name, grid=(ka // tka, n // tn, nt), out_shape=jax.ShapeDtypeStruct(out_shape, BF16),
        in_specs=[pl.BlockSpec((tt, tka), lambda i, j, k: (k, i)), pl.BlockSpec((tt, tn), lambda i, j, k: (k, j))],
        out_specs=out_spec,
        scratch_shapes=[pltpu.VMEM((tka, tn), F32)],
        compiler_params=_cp("parallel", "parallel", "arbitrary"),
    )(a, b)


def _adam_math(w, g, m, v):
    m = ADAM_B1 * m + (1.0 - ADAM_B1) * g
    v = ADAM_B2 * v + (1.0 - ADAM_B2) * jnp.square(g)
    m_hat = m / (1.0 - ADAM_B1 ** ADAM_STEP)
    v_hat = v / (1.0 - ADAM_B2 ** ADAM_STEP)
    return -ADAM_LR * (m_hat / (jnp.sqrt(v_hat) + ADAM_EPS) + ADAM_WD * w), m, v


def _adamw(w, m, v, own, parts, name):
    rows, cols = w.shape
    tr = _row_tile(rows)
    n_parts = parts.shape[0]

    def body(slot_ref, w_ref, m_ref, v_ref, own_ref, p_ref, g_ref, d_ref, nm_ref, nv_ref):
        g = own_ref[...].astype(F32)
        for d in range(n_parts):
            g = g + p_ref[d].astype(F32)
        g_ref[...] = g
        d_ref[...], nm_ref[...], nv_ref[...] = _adam_math(w_ref[...], g, m_ref[...], v_ref[...])

    x, y, c = _place()
    blk = pl.BlockSpec((tr, cols), lambda i, slot: (i, 0))
    own_blk = blk if own.ndim == 2 else pl.BlockSpec((None, tr, cols), lambda i, slot: (slot[0], i, 0))
    return pl.pallas_call(
        body, name=name,
        grid_spec=pltpu.PrefetchScalarGridSpec(
            num_scalar_prefetch=1, grid=(rows // tr,),
            in_specs=[blk, blk, blk, own_blk, pl.BlockSpec((n_parts, tr, cols), lambda i, slot: (0, i, 0))],
            out_specs=(blk,) * 4),
        out_shape=(jax.ShapeDtypeStruct((rows, cols), F32),) * 4, compiler_params=_cp("parallel"),
    )(jnp.reshape(4 * x + 2 * y + c, (1,)).astype(I32), w, m, v, own, parts)


def _adamw_small(parts, w, m, v):
    _, rows, cols = parts.shape

    def body(p_ref, w_ref, m_ref, v_ref, g_ref, d_ref, nm_ref, nv_ref):
        g = p_ref[0]
        for d in range(1, N_DEV):
            g = g + p_ref[d]
        g_ref[...] = g
        d_ref[...], nm_ref[...], nv_ref[...] = _adam_math(w_ref[...], g, m_ref[...], v_ref[...])

    return pl.pallas_call(
        body, name="adamw_replicated", out_shape=(jax.ShapeDtypeStruct((rows, cols), F32),) * 4,
        in_specs=[_full(parts.shape)] + [_full((rows, cols))] * 3, out_specs=(_full((rows, cols)),) * 4, grid=(1,),
        compiler_params=_cp("arbitrary"),
    )(parts, w, m, v)


def _pad_rows(a2d, mult):
    pad = (-a2d.shape[-2]) % mult
    return jnp.pad(a2d, [(0, 0)] * (a2d.ndim - 2) + [(0, pad), (0, 0)]) if pad else a2d


def _pad_cols(a):
    pad = (-a.shape[-1]) % LANES
    return jnp.pad(a, [(0, 0)] * (a.ndim - 1) + [(0, pad)]) if pad else a


def _rot_cols(w):
    half = ROPE // 2
    return jnp.concatenate([-w[..., half:], w[..., :half]], axis=-1)


def _unrot_cols(dw):
    half = ROPE // 2
    return jnp.concatenate([dw[..., half:], -dw[..., :half]], axis=-1)


def _from_col_shards(stacked):
    return stacked.transpose(1, 0, 2).reshape(stacked.shape[1], -1)


def _to_col_shards(full):
    r = full.shape[0]
    return full.reshape(r, N_DEV, -1).transpose(1, 0, 2)


def _rope_tables(seq):
    half = ROPE // 2
    inv = jnp.power(ROPE_THETA, -jnp.arange(half, dtype=F32) / half)
    ang = jnp.arange(seq, dtype=F32)[:, None] * inv[None, :]
    cos = jnp.concatenate([jnp.cos(ang)] * 2, axis=1)
    sin = jnp.concatenate([jnp.sin(ang)] * 2, axis=1)
    ones, zeros = jnp.ones((seq, 2 * NOPE), F32), jnp.zeros((seq, 2 * NOPE), F32)
    pad = jnp.zeros((seq, PAIR_W - 2 * NOPE - 2 * ROPE), F32)
    cext = jnp.concatenate([ones, cos, cos, pad], axis=1)
    sext = jnp.concatenate([zeros, sin, sin, pad], axis=1)
    cs128 = jnp.concatenate([cos, sin, jnp.zeros((seq, LANES - 2 * ROPE), F32)], axis=1)
    return cext, sext, cs128


def _pair_slabs(nope, rope):
    k = nope.shape[0]
    nope = nope.reshape(k, N_PAIRS, 2 * NOPE)
    rope = jnp.zeros((k, N_PAIRS, 2 * ROPE), nope.dtype) if rope is None else rope.reshape(k, N_PAIRS, 2 * ROPE)
    pad = jnp.zeros((k, N_PAIRS, PAIR_W - 2 * NOPE - 2 * ROPE), nope.dtype)
    return jnp.concatenate([nope, rope, pad], axis=2).reshape(k, N_PAIRS * PAIR_W)


def _split_slabs(slabs):
    k = slabs.shape[0]
    s = slabs.reshape(k, N_PAIRS, PAIR_W)
    return s[:, :, :2 * NOPE].reshape(k, N_HEADS, NOPE), s[:, :, 2 * NOPE:2 * NOPE + 2 * ROPE].reshape(k, N_HEADS, ROPE)


def kernel(x, w_in, b_gate, g_q_a, w_uq, g_kv_a, w_ukv, w_o_mla, w_o_dil, w_out, ln1_g, ln1_b, w_ff1, w_ff2, ln2_g, ln2_b, loss_target, m_w_in, m_b_gate, m_g_q_a, m_w_uq, m_g_kv_a, m_w_ukv, m_w_o_mla, m_w_o_dil, m_w_out, m_ln1_g, m_ln1_b, m_w_ff1, m_w_ff2, m_ln2_g, m_ln2_b, v_w_in, v_b_gate, v_g_q_a, v_w_uq, v_g_kv_a, v_w_ukv, v_w_o_mla, v_w_o_dil, v_w_out, v_ln1_g, v_ln1_b, v_w_ff1, v_w_ff2, v_ln2_g, v_ln2_b):
    batch, seq, _ = x.shape
    tokens = batch * seq
    weights = dict(w_in=w_in, w_uq=w_uq, w_ukv=w_ukv, w_o_mla=w_o_mla, w_o_dil=w_o_dil, w_out=w_out, w_ff1=w_ff1, w_ff2=w_ff2, b_gate=b_gate)
    mom_m = dict(w_in=m_w_in, w_uq=m_w_uq, w_ukv=m_w_ukv, w_o_mla=m_w_o_mla, w_o_dil=m_w_o_dil, w_out=m_w_out, w_ff1=m_w_ff1, w_ff2=m_w_ff2, b_gate=m_b_gate)
    mom_v = dict(w_in=v_w_in, w_uq=v_w_uq, w_ukv=v_w_ukv, w_o_mla=v_w_o_mla, w_o_dil=v_w_o_dil, w_out=v_w_out, w_ff1=v_w_ff1, w_ff2=v_w_ff2, b_gate=v_b_gate)

    first = ["w_in", "w_uq", "w_ukv"]
    widths = [weights[n].shape[2] for n in first]
    shards = [_pad_cols(weights[n][0].astype(BF16)) for n in first]
    (g_in, g_uq, g_ukv), = _run_comm([(_Gather(shards), shards)], "all_gather_first_weights")
    g_uq, g_ukv = g_uq[:, :, :widths[1]], g_ukv[:, :, :widths[2]]

    s1, s2, n_in = Q_LORA + KV_LORA, Q_LORA + KV_LORA + ROPE, N_DEV * widths[0]

    def w_in_cols(lo, hi):
        out = []
        while lo < hi:
            d, off = divmod(lo, widths[0])
            take = min(hi - lo, widths[0] - off)
            out.append(g_in[d][:, off:off + take])
            lo += take
        return out

    w_in_ext = jnp.concatenate(w_in_cols(0, s2) + [_rot_cols(jnp.concatenate(w_in_cols(s1, s2), axis=1)),
                                                   jnp.zeros((D_MODEL, LOW_W - s2 - ROPE), BF16)] + w_in_cols(s2, n_in), axis=1)
    uq = _from_col_shards(g_uq).reshape(Q_LORA, N_HEADS, NOPE + ROPE)
    w1 = _pair_slabs(uq[:, :, :NOPE], uq[:, :, NOPE:])
    ukv = _from_col_shards(g_ukv).reshape(KV_LORA, N_HEADS, NOPE + HEAD_V)
    wk = ukv[:, :, :NOPE].reshape(KV_LORA, N_HEADS * NOPE)
    wv = ukv[:, :, NOPE:].reshape(KV_LORA, N_HEADS * HEAD_V)
    cext, sext, cs128 = _rope_tables(seq)
    dil_bias = _dilated_bias_table(seq)
    no_bias = jnp.zeros((1, 8, LANES), F32)

    x2 = x.reshape(tokens, D_MODEL)
    low, gates, qkvd, qp, kp, vm, qn, kvn, xb = _fwd_proj(x2, w_in_ext, w1, wk, wv, g_q_a, g_kv_a, cext, sext, cs128, seq=seq)
    bg = b_gate[0]
    bg_hi = bg.astype(BF16)
    bg_lo = (bg - bg_hi.astype(F32)).astype(BF16)
    later = [weights[n][0].astype(BF16) for n in ("w_o_mla", "w_o_dil", "w_out", "w_ff1", "w_ff2")]
    later.append(_pad_rows(jnp.concatenate([bg_hi, bg_lo], axis=0), 16))
    mla = dict(batch=batch, seq=seq, width=PAIR_W, col0=(0, 0, 0), dilated=False, scale=MLA_SCALE)
    dil = dict(batch=batch, seq=seq, width=LANES, col0=(0, N_PAIRS, 2 * N_PAIRS), dilated=True, scale=DIL_SCALE)
    o_a, lse_a, g_oa, g_ob, g_out, g_ff1, g_ff2, g_bg = _attn_fwd(
        qp, kp, vm, no_bias, name="mla_attention_fwd", comm=_Gather(later), comm_arrays=later, **mla)
    o_b, lse_b = _attn_fwd(qkvd, qkvd, qkvd, dil_bias, name="dilated_attention_fwd", **dil)
    w_oa, w_ob = _from_col_shards(g_oa), _from_col_shards(g_ob)
    w_out_full = g_out.reshape(D_MODEL, D_MODEL)
    w_ff2_full = g_ff2.reshape(D_FF, D_MODEL)
    bg_parts = g_bg.astype(F32)
    b_gate_full = _from_col_shards(bg_parts[:, 0:2] + bg_parts[:, 2:4])
    hb, xhat1, rstd1, y_a, y_b, mix = _fwd_mix(o_a, o_b, gates, x2, b_gate_full, w_oa, w_ob, w_out_full, ln1_g, ln1_b, seq=seq)
    u, dz2, dz2b, stat2 = _fwd_mlp(hb, xhat1, loss_target.reshape(tokens, D_MODEL), g_ff1, w_ff2_full, ln1_g, ln1_b, ln2_g, ln2_b, seq=seq)

    du, dz1, dz1b, stat1 = _bwd_mlp(dz2, dz2b, u, xhat1, rstd1, g_ff1, w_ff2_full, ln1_g, seq=seq)
    dw_ff = [_wgrad(hb, du, "wgrad_ff1", by_shard=True),
             _wgrad(u, dz2b, "wgrad_ff2", square_relu=True).reshape(N_DEV, FF_SHARD, D_MODEL)]
    dgates, dy_a, dy_b, do_a, do_b, stat_g = _bwd_mix(dz1b, gates, y_a, y_b, b_gate_full, w_oa, w_ob, w_out_full, seq=seq)
    dqp, dkp, dvm, r_ff1, r_ff2 = _attn_bwd(qp, kp, vm, o_a, do_a, lse_a, no_bias, name="mla_attention_bwd",
                                            comm=_Scatter(dw_ff), comm_arrays=dw_ff, **mla)
    dw_mid = [_to_col_shards(_wgrad(o_a, dy_a, "wgrad_o_mla")), _to_col_shards(_wgrad(o_b, dy_b, "wgrad_o_dil")),
              _wgrad(mix, dz1b, "wgrad_out").reshape(N_DEV, D_MODEL // N_DEV, D_MODEL),
              _pad_rows(_to_col_shards(stat_g[0:2]).astype(BF16), 16)]
    dq_d, dk_d, dv_d, r_oa, r_ob, r_out, r_bg = _attn_bwd(qkvd, qkvd, qkvd, o_b, do_b, lse_b, dil_bias, name="dilated_attention_bwd",
                                                          comm=_Scatter(dw_mid), comm_arrays=dw_mid, **dil)
    grad_x, dproj, d_a, dkn, stat_r = _bwd_proj(dqp, dkp, dvm, dq_d, dk_d, dv_d, dgates, dz1, low, w_in_ext, w1, wk, wv,
                                                g_q_a, g_kv_a, cext, sext, cs128, seq=seq)

    dw_in_ext = _wgrad(xb, dproj, "wgrad_in")
    dw1 = _wgrad(qn, d_a, "wgrad_uq")
    dwk = _wgrad(kvn, dkn, "wgrad_ukv_k")
    dwv = _wgrad(kvn, dvm, "wgrad_ukv_v")
    dw_kr = dw_in_ext[:, s1:s2] + _unrot_cols(dw_in_ext[:, s2:s2 + ROPE])

    def dw_in_cols(lo, hi):
        out = []
        for a, b, piece in ((0, s1, lambda u, v: dw_in_ext[:, u:v]), (s1, s2, lambda u, v: dw_kr[:, u - s1:v - s1]),
                            (s2, n_in, lambda u, v: dw_in_ext[:, u + LOW_W - s2:v + LOW_W - s2])):
            if max(lo, a) < min(hi, b):
                out.append(piece(max(lo, a), min(hi, b)))
        return out

    dw_in = jnp.stack([_pad_cols(jnp.concatenate(dw_in_cols(d * widths[0], (d + 1) * widths[0]), axis=1)) for d in range(N_DEV)])
    n1, r1 = _split_slabs(dw1)
    dw_uq = jnp.concatenate([n1, r1], axis=2).reshape(Q_LORA, N_HEADS * (NOPE + ROPE))
    dw_ukv = jnp.concatenate([dwk.reshape(KV_LORA, N_HEADS, NOPE), dwv.reshape(KV_LORA, N_HEADS, HEAD_V)], axis=2).reshape(KV_LORA, N_HEADS * (NOPE + HEAD_V))
    last = [dw_in] + [_pad_cols(_to_col_shards(dw)) for dw in (dw_uq, dw_ukv)]
    theirs = _rs_sibling(last, "rs_last_sibling_exchange")
    sums = [_pair_sum(a, b, "rs_last_pair_sum_" + n) for a, b, n in zip(last, theirs, first)]
    partial = jnp.concatenate([stat_r[0:1, :Q_LORA], stat_r[1:2, :KV_LORA], stat1[0:1], stat1[1:2], stat2[0:1], stat2[1:2],
                               stat2[2:3, :LANES]], axis=1)
    partial = _pad_rows(partial.reshape(-1, LANES), 8)
    rest = [s[1] for s in sums]
    got, (every,) = _run_comm([(_ChipExchange(rest), rest), (_Gather([partial]), [partial])], "rs_last_chip_exchange")

    upd = {}
    for n, w, (own, _), parts in zip(first, widths, sums, got):
        upd[n] = _adamw(weights[n][0], mom_m[n][0], mom_v[n][0], own[:, :w], parts[:, :, :w], "adamw_" + n)
    for n, own, parts in (("w_o_mla", dw_mid[0], r_oa), ("w_o_dil", dw_mid[1], r_ob), ("w_out", dw_mid[2], r_out),
                          ("w_ff1", dw_ff[0], r_ff1), ("w_ff2", dw_ff[1], r_ff2)):
        upd[n] = _adamw(weights[n][0], mom_m[n][0], mom_v[n][0], own, parts, "adamw_" + n)
    bg_upd = _adamw(_pad_rows(b_gate[0], 16), _pad_rows(m_b_gate[0], 16), _pad_rows(v_b_gate[0], 16), dw_mid[3], r_bg, "adamw_b_gate")
    upd["b_gate"] = tuple(t[0:2] for t in bg_upd)

    small_w = [g_q_a, g_kv_a, ln1_g, ln1_b, ln2_g, ln2_b]
    small_m = [m_g_q_a, m_g_kv_a, m_ln1_g, m_ln1_b, m_ln2_g, m_ln2_b]
    small_v = [v_g_q_a, v_g_kv_a, v_ln1_g, v_ln1_b, v_ln2_g, v_ln2_b]
    small_widths = [a.shape[1] for a in small_w]

    def as_rows(vecs, extra):
        flat = jnp.concatenate(vecs + [jnp.zeros((1, extra), F32)], axis=1)
        return _pad_rows(flat.reshape(-1, LANES), 8)

    g_s, d_s, nm_s, nv_s = _adamw_small(every, as_rows(small_w, LANES), as_rows(small_m, LANES), as_rows(small_v, LANES))

    def split_small(a):
        flat = a.reshape(1, -1)
        out, c0 = [], 0
        for w in small_widths:
            out.append(flat[:, c0:c0 + w])
            c0 += w
        return out, flat[0, c0]

    g_small, loss = split_small(g_s)
    small = [g_small, split_small(d_s)[0], split_small(nm_s)[0], split_small(nv_s)[0]]

    order = ["w_in", "b_gate", "g_q_a", "w_uq", "g_kv_a", "w_ukv", "w_o_mla", "w_o_dil", "w_out", "ln1_g", "ln1_b", "w_ff1", "w_ff2", "ln2_g", "ln2_b"]
    small_names = ["g_q_a", "g_kv_a", "ln1_g", "ln1_b", "ln2_g", "ln2_b"]

    def pick(kind):
        return [small[kind][small_names.index(n)] if n in small_names else upd[n][kind][None] for n in order]

    return (loss, grad_x.reshape(batch, seq, D_MODEL), *pick(0), *pick(1), *pick(2), *pick(3))
```

```python
import functools
import math

import jax
import jax.numpy as jnp
from jax import lax
from jax.experimental import pallas as pl
from jax.experimental.pallas import tpu as pltpu

F32 = jnp.float32
BF16 = jnp.bfloat16
I32 = jnp.int32

D_MODEL = 1024
N_HEADS = 8
NOPE = 64
ROPE = 32
HEAD_V = 64
Q_LORA = 384
KV_LORA = 256
DIL_WIDTH = 512
D_FF = 4096
ROPE_THETA = 10000.0
LN_EPS = 1e-5
RMS_EPS = 1e-6
NEG = -1e30
ALPHA = 2.0 ** 0.25
MLA_SCALE = (NOPE + ROPE) ** -0.5
DIL_SCALE = 64 ** -0.5
ADAM_LR, ADAM_B1, ADAM_B2, ADAM_EPS, ADAM_WD, ADAM_STEP = 0.001, 0.9, 0.999, 1e-08, 0.01, 10

LANES = 128
PAIR_W = 256
N_PAIRS = N_HEADS // 2
LOW_W = 768
IN_EXT = LOW_W + 3 * DIL_WIDTH + 2 * D_MODEL
N_DEV = 8
FF_SHARD = D_FF // N_DEV
FF_STEP = 4
WGRAD_SHARDS = 4
TOKEN_TILE = 256
MIX_TILE = 512
ATTN_TILE = 256
VMEM_LIMIT = 56 << 20

MESH = pl.DeviceIdType.MESH
ANY = pl.BlockSpec(memory_space=pl.ANY)
CHIP_FLIPS = ((0, 0), (0, 1), (1, 0), (1, 1))
PEER_FLIPS = tuple((fx, fy, fc) for fx in (0, 1) for fy in (0, 1) for fc in (0, 1))[1:]


def _cp(*sem):
    return pltpu.CompilerParams(dimension_semantics=sem or None, vmem_limit_bytes=VMEM_LIMIT)


def _full(shape):
    nd = len(shape)
    return pl.BlockSpec(shape, lambda *_: (0,) * nd)


def _rows(tm, width):
    return pl.BlockSpec((tm, width), lambda i, *_: (i, 0))


def _dot(a, b):
    return jnp.dot(a, b, preferred_element_type=F32)


def _dot_nt(a, b):
    return lax.dot_general(a, b, (((1,), (1,)), ((), ())), preferred_element_type=F32)


def _dot_tn(a, b):
    return lax.dot_general(a, b, (((0,), (0,)), ((), ())), preferred_element_type=F32)


def _sigmoid(z):
    return 1.0 / (1.0 + jnp.exp(-z))


def _place():
    return lax.axis_index("x"), lax.axis_index("y"), lax.axis_index("c")


def _flip(v, f):
    return 1 - v if f else v


class _Gather:
    def __init__(self, shards):
        self.n = len(shards)
        self.out_shape = [jax.ShapeDtypeStruct((N_DEV, *s.shape), s.dtype) for s in shards]
        self.scratch = [pltpu.SemaphoreType.DMA((7 * self.n,)), pltpu.SemaphoreType.DMA((7 * self.n,)),
                        pltpu.SemaphoreType.DMA((self.n,))]

    def _copies(self, what, srcs, dsts, send, recv, local):
        x, y, c = _place()
        chips = [(_flip(x, fx), _flip(y, fy)) for fx, fy in CHIP_FLIPS[1:]]
        out = []
        for a in range(self.n):
            def slot(px, py, pc, a=a):
                return dsts[a].at[4 * px + 2 * py + pc]

            def copy(k, block, to, src=None, a=a, slot=slot):
                return pltpu.make_async_remote_copy(
                    src_ref=slot(*block) if src is None else src, dst_ref=slot(*block),
                    send_sem=send.at[7 * a + k], recv_sem=recv.at[7 * a + k], device_id=to, device_id_type=MESH)

            if what == "mine":
                out.append(pltpu.make_async_copy(srcs[a], slot(x, y, c), local.at[a]))
            elif what == "first":
                out.append(copy(0, (x, y, c), (x, y, 1 - c), src=srcs[a]))
                out += [copy(1 + j, (x, y, c), (*chip, c), src=srcs[a]) for j, chip in enumerate(chips)]
            elif what == "landed":
                out += [copy(1 + j, (*chip, c), (x, y, c)) for j, chip in enumerate(chips)]
            elif what == "passed":
                out += [copy(4 + j, (*chip, c), (x, y, 1 - c)) for j, chip in enumerate(chips)]
            else:
                out.append(copy(0, (x, y, 1 - c), (x, y, c)))
                out += [copy(4 + j, (*chip, 1 - c), (x, y, c)) for j, chip in enumerate(chips)]
        return out

    def start(self, *refs):
        for cp in self._copies("first", *refs) + self._copies("mine", *refs):
            cp.start()

    def forward(self, *refs):
        for landed, passed in zip(self._copies("landed", *refs), self._copies("passed", *refs)):
            landed.wait_recv()
            passed.start()

    def finish(self, *refs):
        for cp in self._copies("from_sibling", *refs):
            cp.wait_recv()
        for cp in self._copies("first", *refs) + self._copies("passed", *refs):
            cp.wait_send()
        for cp in self._copies("mine", *refs):
            cp.wait()


class _Scatter:
    def __init__(self, arrays):
        self.n = len(arrays)
        self.out_shape = [jax.ShapeDtypeStruct((7, *a.shape[1:]), a.dtype) for a in arrays]
        self.scratch = [pltpu.SemaphoreType.DMA((7 * self.n,)), pltpu.SemaphoreType.DMA((7 * self.n,))]

    def _copies(self, srcs, dsts, send, recv):
        x, y, c = _place()
        out = []
        for a in range(self.n):
            for k, (fx, fy, fc) in enumerate(PEER_FLIPS):
                px, py, pc = _flip(x, fx), _flip(y, fy), _flip(c, fc)
                out.append(pltpu.make_async_remote_copy(
                    src_ref=srcs[a].at[4 * px + 2 * py + pc], dst_ref=dsts[a].at[k],
                    send_sem=send.at[7 * a + k], recv_sem=recv.at[7 * a + k], device_id=(px, py, pc), device_id_type=MESH))
        return out

    def start(self, *refs):
        for cp in self._copies(*refs):
            cp.start()

    def forward(self, *refs):
        pass

    def finish(self, *refs):
        for cp in self._copies(*refs):
            cp.wait_send()
        for cp in self._copies(*refs):
            cp.wait_recv()


class _ChipExchange:
    def __init__(self, arrays):
        self.n = len(arrays)
        self.out_shape = [jax.ShapeDtypeStruct(a.shape, a.dtype) for a in arrays]
        self.scratch = [pltpu.SemaphoreType.DMA((3 * self.n,)), pltpu.SemaphoreType.DMA((3 * self.n,))]

    def _copies(self, srcs, dsts, send, recv):
        x, y, c = _place()
        return [pltpu.make_async_remote_copy(
            src_ref=srcs[a].at[k], dst_ref=dsts[a].at[k], send_sem=send.at[3 * a + k], recv_sem=recv.at[3 * a + k],
            device_id=(_flip(x, fx), _flip(y, fy), c), device_id_type=MESH)
            for a in range(self.n) for k, (fx, fy) in enumerate(CHIP_FLIPS[1:])]

    def start(self, *refs):
        for cp in self._copies(*refs):
            cp.start()

    def forward(self, *refs):
        pass

    def finish(self, *refs):
        for cp in self._copies(*refs):
            cp.wait_send()
        for cp in self._copies(*refs):
            cp.wait_recv()


def _run_comm(plans, name):
    n_in = sum(p.n for p, _ in plans)

    def body(*refs):
        args, i0, s0 = [], 0, 2 * n_in
        for p, _ in plans:
            args.append((refs[i0:i0 + p.n], refs[n_in + i0:n_in + i0 + p.n], *refs[s0:s0 + len(p.scratch)]))
            i0, s0 = i0 + p.n, s0 + len(p.scratch)
        for phase in ("start", "forward", "finish"):
            for (p, _), a in zip(plans, args):
                getattr(p, phase)(*a)

    out = pl.pallas_call(
        body, name=name, out_shape=[s for p, _ in plans for s in p.out_shape], in_specs=[ANY] * n_in,
        out_specs=[ANY] * n_in, scratch_shapes=[s for p, _ in plans for s in p.scratch],
    )(*[a for _, arrays in plans for a in arrays])
    split, i0 = [], 0
    for p, _ in plans:
        split.append(out[i0:i0 + p.n])
        i0 += p.n
    return split


def _rs_sibling(arrays, name):
    n = len(arrays)

    def body(*refs):
        srcs, got, (send, recv) = refs[:n], refs[n:2 * n], refs[2 * n:]
        x, y, c = _place()
        copies = []
        for a in range(n):
            for r, (fx, fy) in enumerate(CHIP_FLIPS):
                chip = 2 * _flip(x, fx) + _flip(y, fy)
                copies.append(pltpu.make_async_remote_copy(
                    src_ref=srcs[a].at[2 * chip + 1 - c], dst_ref=got[a].at[r], send_sem=send.at[4 * a + r],
                    recv_sem=recv.at[4 * a + r], device_id=(x, y, 1 - c), device_id_type=MESH))
        for cp in copies:
            cp.start()
        for cp in copies:
            cp.wait_send()
        for cp in copies:
            cp.wait_recv()

    return pl.pallas_call(
        body, name=name, out_shape=[jax.ShapeDtypeStruct((4, *a.shape[1:]), a.dtype) for a in arrays],
        in_specs=[ANY] * n, out_specs=[ANY] * n,
        scratch_shapes=[pltpu.SemaphoreType.DMA((4 * n,)), pltpu.SemaphoreType.DMA((4 * n,))],
    )(*arrays)


def _row_tile(rows):
    return 256 if rows % 256 == 0 else rows


def _chip_slots():
    x, y, c = _place()
    return jnp.stack([4 * _flip(x, fx) + 2 * _flip(y, fy) + c for fx, fy in CHIP_FLIPS]).astype(I32)


def _pair_sum(full, theirs, name):
    _, rows, cols = theirs.shape
    tr = _row_tile(rows)

    def body(slots_ref, m0_ref, m1_ref, m2_ref, m3_ref, b_ref, own_ref, rest_ref):
        own_ref[...] = m0_ref[...].astype(F32) + b_ref[0].astype(F32)
        for k, m_ref in enumerate((m1_ref, m2_ref, m3_ref)):
            rest_ref[k] = (m_ref[...].astype(F32) + b_ref[k + 1].astype(F32)).astype(BF16)

    def mine(k):
        return pl.BlockSpec((None, tr, cols), lambda i, slots: (slots[k], i, 0))

    return pl.pallas_call(
        body, name=name,
        grid_spec=pltpu.PrefetchScalarGridSpec(
            num_scalar_prefetch=1, grid=(rows // tr,),
            in_specs=[mine(0), mine(1), mine(2), mine(3), pl.BlockSpec((4, tr, cols), lambda i, slots: (0, i, 0))],
            out_specs=(pl.BlockSpec((tr, cols), lambda i, slots: (i, 0)), pl.BlockSpec((3, tr, cols), lambda i, slots: (0, i, 0)))),
        out_shape=(jax.ShapeDtypeStruct((rows, cols), F32), jax.ShapeDtypeStruct((3, rows, cols), BF16)),
        compiler_params=_cp("parallel"),
    )(_chip_slots(), full, full, full, full, theirs)


def _head_lanes(width, h):
    lane = lax.broadcasted_iota(I32, (1, width), 1)
    if width == LANES:
        return (lane >= 64 * h) & (lane < 64 * h + 64)
    nope = (lane >= NOPE * h) & (lane < NOPE * h + NOPE)
    rope = (lane >= 2 * NOPE + ROPE * h) & (lane < 2 * NOPE + ROPE * h + ROPE)
    return nope | rope


def _dilated_bias_table(seq):
    t = min(ATTN_TILE, seq)
    nd = seq // t

    def body(o_ref):
        delta = pl.program_id(0) * t + lax.broadcasted_iota(I32, (t, t), 1) - lax.broadcasted_iota(I32, (t, t), 0)
        mult = ((delta <= 128).astype(I32) + (((delta & 3) == 0) & (delta <= 512)).astype(I32)
                + ((delta & 15) == 0).astype(I32))
        logm = jnp.where(mult == 3, math.log(3.0), jnp.where(mult == 2, math.log(2.0), 0.0))
        valid = (delta >= 0) & (mult > 0)
        dist = delta.astype(F32)
        for h in range(N_HEADS):
            o_ref[h] = jnp.where(valid, logm - 2.0 ** (-(h + 1)) * dist, NEG)

    return pl.pallas_call(
        body, name="dilated_bias_table", grid=(nd,), out_shape=jax.ShapeDtypeStruct((N_HEADS, nd, t, t), F32),
        out_specs=pl.BlockSpec((N_HEADS, None, t, t), lambda d: (0, d, 0, 0)),
        compiler_params=_cp("parallel"),
    )()


def _comm_hooks(comm, refs, n_in, n_out):
    if comm is None:
        return refs[:n_in], refs[n_in:n_in + n_out], refs[n_in + n_out:], None
    n = comm.n
    ins, srcs = refs[:n_in], refs[n_in:n_in + n]
    outs, dsts = refs[n_in + n:n_in + n + n_out], refs[n_in + n + n_out:n_in + 2 * n + n_out]
    rest = refs[n_in + 2 * n + n_out:]
    own = len(rest) - len(comm.scratch)
    return ins, outs, rest[:own], (srcs, dsts, *rest[own:])


def _attn_fwd(q, k, v, bias, *, batch, seq, width, col0, dilated, scale, name, comm=None, comm_arrays=()):
    t = min(ATTN_TILE, seq)
    nq = seq // t
    cq, ck, cv = col0
    pre = scale if dilated else 1.0
    steps = batch * N_PAIRS

    def body(*refs):
        (q_ref, k_ref, v_ref, bias_ref), (o_ref, lse_ref), (v_heads,), plan = _comm_hooks(comm, refs, 4, 2)
        step_no = pl.program_id(0) * N_PAIRS + pl.program_id(1)
        if plan:
            pl.when(step_no == 0)(lambda: comm.start(*plan))
            pl.when(step_no == (3 * steps) // 4)(lambda: comm.forward(*plan))
        v_all = v_ref[...].astype(F32)
        for h in (0, 1):
            v_heads[h] = jnp.transpose(jnp.where(_head_lanes(LANES, h), v_all, 0.0)).astype(BF16)
        top = lax.broadcasted_iota(I32, (LANES, t), 0) < HEAD_V
        causal = lax.broadcasted_iota(I32, (t, t), 0) <= lax.broadcasted_iota(I32, (t, t), 1)
        lax.fori_loop(0, nq, functools.partial(query_tile, q_ref, k_ref, bias_ref, o_ref, lse_ref, v_heads, top, causal), 0)
        if plan:
            pl.when(step_no == steps - 1)(lambda: comm.finish(*plan))

    def query_tile(q_ref, k_ref, bias_ref, o_ref, lse_ref, v_heads, top, causal, i, _):
        qs = pl.multiple_of(i * t, t)
        q2 = q_ref[pl.ds(qs, t), :] * pre if dilated else q_ref[pl.ds(qs, t), :]
        qh = [jnp.where(_head_lanes(width, h), q2, jnp.zeros_like(q2)) for h in (0, 1)]

        def scores(j):
            kj = k_ref[pl.ds(pl.multiple_of(j * t, t), t), :]
            return [_dot_nt(kj, qh[h]) for h in (0, 1)]

        def step(j, carry, last):
            m0, l0, m1, l1, acc, s0, s1 = carry
            ahead = [] if last else scores(j + 1)
            ks = pl.multiple_of(j * t, t)
            new, alphas, pv = [], [], []
            for h, (m, l, s) in enumerate(((m0, l0, s0), (m1, l1, s1))):
                if dilated:
                    s = s + bias_ref[h, i - j]
                else:
                    s = s * scale
                    if last:
                        s = jnp.where(causal, s, NEG)
                m_new = jnp.maximum(m, jnp.max(s, axis=0, keepdims=True))
                a = jnp.exp(m - m_new)
                p = jnp.exp(s - m_new)
                new += [m_new, a * l + jnp.sum(p, axis=0, keepdims=True)]
                alphas.append(a)
                pv.append(_dot(v_heads[h, :, pl.ds(ks, t)], p.astype(BF16)))
            acc = jnp.where(top, alphas[0], alphas[1]) * acc + pv[0] + pv[1]
            return (*new, acc, *ahead)

        row = jnp.full((1, t), NEG, F32)
        zero = jnp.zeros((1, t), F32)
        init = (row, zero, row, zero, jnp.zeros((LANES, t), F32), *scores(0))
        m0, l0, m1, l1, acc = step(i, lax.fori_loop(0, i, functools.partial(step, last=False), init), True)
        o_ref[pl.ds(qs, t), :] = jnp.transpose(acc * jnp.where(top, 1.0 / l0, 1.0 / l1)).astype(BF16)
        r = lax.broadcasted_iota(I32, (8, t), 0)
        lse_ref[:, pl.ds(qs, t)] = jnp.where(r == 0, m0 + jnp.log(l0), jnp.where(r == 1, m1 + jnp.log(l1), 0.0))
        return 0

    bias_spec = (pl.BlockSpec((2, nq, t, t), lambda b, p: (p, 0, 0, 0)) if dilated
                 else pl.BlockSpec((None, 8, LANES), lambda b, p: (0, 0, 0)))
    n = comm.n if comm else 0
    return pl.pallas_call(
        body, name=name, grid=(batch, N_PAIRS),
        out_shape=[jax.ShapeDtypeStruct((batch * seq, DIL_WIDTH), BF16), jax.ShapeDtypeStruct((batch * N_PAIRS, 8, seq), F32)]
        + (comm.out_shape if comm else []),
        in_specs=[pl.BlockSpec((seq, width), lambda b, p: (b, cq + p)),
                  pl.BlockSpec((seq, width), lambda b, p: (b, ck + p)),
                  pl.BlockSpec((seq, LANES), lambda b, p: (b, cv + p)),
                  bias_spec] + [ANY] * n,
        out_specs=[pl.BlockSpec((seq, LANES), lambda b, p: (b, p)),
                   pl.BlockSpec((None, 8, seq), lambda b, p: (b * N_PAIRS + p, 0, 0))] + [ANY] * n,
        scratch_shapes=[pltpu.VMEM((2, LANES, seq), BF16)] + (comm.scratch if comm else []),
        compiler_params=_cp("arbitrary", "arbitrary") if comm else _cp("parallel", "parallel"),
    )(q, k, v, bias, *comm_arrays)


def _attn_bwd(q, k, v, o, do, lse, bias, *, batch, seq, width, col0, dilated, scale, name, comm=None, comm_arrays=()):
    t = min(ATTN_TILE, seq)
    nq = seq // t
    cq, ck, cv = col0
    pre = scale if dilated else 1.0
    dq_transposed = width == LANES
    steps = batch * N_PAIRS

    def body(*refs):
        ins, (dq_ref, dk_ref, dv_ref), (dq_acc, dk_acc, dv_acc, rowdot, q_heads, do_heads), plan = _comm_hooks(comm, refs, 7, 3)
        q_ref, k_ref, v_ref, o_ref, do_ref, lse_ref, bias_ref = ins
        step_no = pl.program_id(0) * N_PAIRS + pl.program_id(1)
        if plan:
            pl.when(step_no == 0)(lambda: comm.start(*plan))
        wlane = [_head_lanes(width, h) for h in (0, 1)]
        vlane = [_head_lanes(LANES, h) for h in (0, 1)]
        causal = lax.broadcasted_iota(I32, (t, t), 0) <= lax.broadcasted_iota(I32, (t, t), 1)
        q_all = q_ref[...] * pre if dilated else q_ref[...]
        for h in (0, 1):
            q_heads[h] = jnp.where(wlane[h], q_all, jnp.zeros_like(q_all))
            do_heads[h] = jnp.where(vlane[h], do_ref[...], jnp.zeros_like(do_ref[...]))
        prod = jnp.transpose(do_ref[...].astype(F32) * o_ref[...].astype(F32))
        rowdot[0:1, :] = jnp.sum(prod[0:HEAD_V], axis=0, keepdims=True)
        rowdot[1:2, :] = jnp.sum(prod[HEAD_V:], axis=0, keepdims=True)
        dq_acc[...] = jnp.zeros_like(dq_acc)

        def k_tile(j, _):
            ks = pl.multiple_of(j * t, t)
            kj = k_ref[pl.ds(ks, t), :]
            vj = v_ref[pl.ds(ks, t), :]
            kh = [jnp.where(wlane[h], kj, jnp.zeros_like(kj)) for h in (0, 1)]
            if dq_transposed:
                kh = [jnp.transpose(kh[h].astype(F32)).astype(BF16) for h in (0, 1)]
            dk_acc[...] = jnp.zeros_like(dk_acc)
            dv_acc[...] = jnp.zeros_like(dv_acc)

            def operands(i):
                qs = pl.multiple_of(i * t, t)
                return [q_heads[h, pl.ds(qs, t), :] for h in (0, 1)], [do_heads[h, pl.ds(qs, t), :] for h in (0, 1)]

            def products(i):
                qih, doih = operands(i)
                scores = tuple(_dot_nt(kj, qih[h]) for h in (0, 1))
                return scores + tuple(_dot_nt(vj, doih[h]) for h in (0, 1)) if width > LANES else scores

            def q_tile(n, carry, last):
                i = nq - 1 - n
                ahead = () if last else products(i - 1)
                qs = pl.multiple_of(i * t, t)
                qih, doih = operands(i)
                s0, s1 = carry[:2]
                dps = carry[2:] if width > LANES else [_dot_nt(vj, doih[h]) for h in (0, 1)]
                dq_i = jnp.zeros((width, t) if dq_transposed else (t, width), F32)
                for h, (s, dp) in enumerate(((s0, dps[0]), (s1, dps[1]))):
                    if dilated:
                        s = s + bias_ref[h, i - j]
                    else:
                        s = s * scale
                        if last:
                            s = jnp.where(causal, s, NEG)
                    p = jnp.exp(s - lse_ref[h:h + 1, pl.ds(qs, t)])
                    ds = p * (dp - rowdot[h:h + 1, pl.ds(qs, t)])
                    ds = (ds if dilated else ds * scale).astype(BF16)
                    dv_acc[...] += _dot(p.astype(BF16), doih[h])
                    dk_acc[...] += _dot(ds, qih[h])
                    dq_i = dq_i + (_dot(kh[h], ds) if dq_transposed else _dot_tn(ds, kh[h]))
                if dq_transposed:
                    dq_acc[:, pl.ds(qs, t)] += dq_i
                else:
                    dq_acc[pl.ds(qs, t), :] += dq_i
                return ahead

            q_tile(nq - 1 - j, lax.fori_loop(0, nq - 1 - j, functools.partial(q_tile, last=False), products(nq - 1)), True)
            dk_ref[pl.ds(ks, t), :] = dk_acc[...].astype(BF16)
            dv_ref[pl.ds(ks, t), :] = dv_acc[...].astype(BF16)
            return 0

        lax.fori_loop(0, nq, k_tile, 0)
        dq_ref[...] = ((jnp.transpose(dq_acc[...]) if dq_transposed else dq_acc[...]) * pre).astype(BF16)
        if plan:
            pl.when(step_no == steps - 1)(lambda: comm.finish(*plan))

    tokens = batch * seq
    bias_spec = (pl.BlockSpec((2, nq, t, t), lambda b, p: (p, 0, 0, 0)) if dilated
                 else pl.BlockSpec((None, 8, LANES), lambda b, p: (0, 0, 0)))
    n = comm.n if comm else 0
    return pl.pallas_call(
        body, name=name, grid=(batch, N_PAIRS),
        out_shape=[jax.ShapeDtypeStruct((tokens, N_PAIRS * width), BF16), jax.ShapeDtypeStruct((tokens, N_PAIRS * width), BF16),
                   jax.ShapeDtypeStruct((tokens, DIL_WIDTH), BF16)] + (comm.out_shape if comm else []),
        in_specs=[pl.BlockSpec((seq, width), lambda b, p: (b, cq + p)),
                  pl.BlockSpec((seq, width), lambda b, p: (b, ck + p)),
                  pl.BlockSpec((seq, LANES), lambda b, p: (b, cv + p)),
                  pl.BlockSpec((seq, LANES), lambda b, p: (b, p)),
                  pl.BlockSpec((seq, LANES), lambda b, p: (b, p)),
                  pl.BlockSpec((None, 8, seq), lambda b, p: (b * N_PAIRS + p, 0, 0)),
                  bias_spec] + [ANY] * n,
        out_specs=[pl.BlockSpec((seq, width), lambda b, p: (b, p)),
                   pl.BlockSpec((seq, width), lambda b, p: (b, p)),
                   pl.BlockSpec((seq, LANES), lambda b, p: (b, p))] + [ANY] * n,
        scratch_shapes=[pltpu.VMEM((width, seq) if dq_transposed else (seq, width), F32),
                        pltpu.VMEM((t, width), F32), pltpu.VMEM((t, LANES), F32),
                        pltpu.VMEM((8, seq), F32), pltpu.VMEM((2, seq, width), BF16), pltpu.VMEM((2, seq, LANES), BF16)]
        + (comm.scratch if comm else []),
        compiler_params=_cp("arbitrary", "arbitrary") if comm else _cp("parallel", "parallel"),
    )(q, k, v, o, do, lse, bias, *comm_arrays)


def _rms(xf, g):
    r = lax.rsqrt(jnp.mean(xf * xf, axis=1, keepdims=True) + RMS_EPS)
    return xf * r * g, r


def _rms_bwd(dy, xf, r, g):
    gy = dy * g
    dx = r * gy - xf * (r * r * r) * jnp.mean(gy * xf, axis=1, keepdims=True)
    return dx, dy * xf * r


def _ln_bwd(dy, xhat, rstd, g):
    dxh = dy * g
    return rstd * (dxh - jnp.mean(dxh, axis=1, keepdims=True) - xhat * jnp.mean(dxh * xhat, axis=1, keepdims=True))


def _rope_slabs(q, cos, sin, transpose):
    first_half = (lax.broadcasted_iota(I32, (1, LANES), 1) % ROPE) < ROPE // 2
    out = []
    for p in range(N_PAIRS):
        blk = q[:, p * PAIR_W + LANES:(p + 1) * PAIR_W]
        y = blk * sin if transpose else blk
        up, down = pltpu.roll(y, LANES - ROPE // 2, 1), pltpu.roll(y, ROPE // 2, 1)
        rot = jnp.where(first_half, up, -down) if transpose else jnp.where(first_half, -up, down) * sin
        out += [q[:, p * PAIR_W:p * PAIR_W + LANES], blk * cos + rot]
    return jnp.concatenate(out, axis=1)


def _fwd_proj(x, w_in_ext, w1, wk, wv, g_q, g_kv, cext, sext, cs128, *, seq):
    tokens = x.shape[0]
    tm = min(TOKEN_TILE, seq)
    ns = seq // tm

    def body(x_ref, win_ref, w1_ref, wk_ref, wv_ref, gq_ref, gkv_ref, c_ref, s_ref, cs_ref,
             low_ref, gates_ref, qkvd_ref, qp_ref, kp_ref, vm_ref, qn_ref, kvn_ref, xb_ref):
        xt = x_ref[...].astype(BF16)
        xb_ref[...] = xt
        low = _dot(xt, win_ref[:, 0:LOW_W])
        low_ref[...] = low
        qkvd_ref[...] = _dot(xt, win_ref[:, LOW_W:LOW_W + 3 * DIL_WIDTH]).astype(BF16)
        gates_ref[...] = _dot(xt, win_ref[:, LOW_W + 3 * DIL_WIDTH:]).astype(BF16)
        qn = _rms(low[:, 0:Q_LORA], gq_ref[...])[0].astype(BF16)
        kvn = _rms(low[:, Q_LORA:Q_LORA + KV_LORA], gkv_ref[...])[0].astype(BF16)
        qn_ref[...] = qn
        kvn_ref[...] = kvn
        qp_ref[...] = _rope_slabs(_dot(qn, w1_ref[...]), c_ref[...], s_ref[...], False).astype(BF16)
        kr = low[:, Q_LORA + KV_LORA:] * cs_ref[...]
        kr = kr + pltpu.roll(kr, LANES - ROPE, 1)
        lane = lax.broadcasted_iota(I32, kr.shape, 1)
        kr = jnp.where(lane < ROPE, kr, 0.0)
        kr = (kr + pltpu.roll(kr, ROPE, 1)).astype(BF16)
        kn = _dot(kvn, wk_ref[...]).astype(BF16)
        kp_ref[...] = jnp.concatenate([blk for p in range(N_PAIRS) for blk in (kn[:, p * LANES:(p + 1) * LANES], kr)], axis=1)
        vm_ref[...] = _dot(kvn, wv_ref[...]).astype(BF16)

    n_gates = 2 * D_MODEL
    outs = [(LOW_W, F32), (n_gates, BF16), (3 * DIL_WIDTH, BF16), (N_PAIRS * PAIR_W, BF16), (N_PAIRS * PAIR_W, BF16),
            (DIL_WIDTH, BF16), (Q_LORA, BF16), (KV_LORA, BF16), (D_MODEL, BF16)]
    return pl.pallas_call(
        body, name="fwd_proj", grid=(tokens // tm,),
        out_shape=tuple(jax.ShapeDtypeStruct((tokens, w), dt) for w, dt in outs),
        in_specs=[_rows(tm, D_MODEL), _full(w_in_ext.shape), _full(w1.shape), _full(wk.shape),
                  _full(wv.shape), _full(g_q.shape), _full(g_kv.shape),
                  pl.BlockSpec((tm, LANES), lambda i: (i % ns, 1)),
                  pl.BlockSpec((tm, LANES), lambda i: (i % ns, 1)),
                  pl.BlockSpec((tm, LANES), lambda i: (i % ns, 0))],
        out_specs=tuple(_rows(tm, w) for w, _ in outs),
        compiler_params=_cp("parallel"),
    )(x, w_in_ext, w1, wk, wv, g_q, g_kv, cext, sext, cs128)


def _fwd_mix(o_a, o_b, gates, x, b_gate, w_oa, w_ob, w_out, ln_g, ln_b, *, seq):
    tokens = x.shape[0]
    tm = min(MIX_TILE, seq)

    def body(oa_ref, ob_ref, gt_ref, x_ref, bg_ref, woa_ref, wob_ref, wout_ref, g_ref, b_ref,
             hb_ref, xhat_ref, rstd_ref, ya_ref, yb_ref, mix_ref):
        ya = _dot(oa_ref[...], woa_ref[...])
        yb = _dot(ob_ref[...], wob_ref[...])
        g0 = _sigmoid(gt_ref[:, 0:D_MODEL].astype(F32) + bg_ref[0:1, :])
        g1 = _sigmoid(gt_ref[:, D_MODEL:].astype(F32) + bg_ref[1:2, :])
        mix = (g0 * ya + g1 * yb).astype(BF16)
        z = ALPHA * x_ref[...] + _dot(mix, wout_ref[...])
        zc = z - jnp.mean(z, axis=1, keepdims=True)
        rstd = lax.rsqrt(jnp.mean(zc * zc, axis=1, keepdims=True) + LN_EPS)
        xhat = zc * rstd
        hb_ref[...] = (xhat * g_ref[...] + b_ref[...]).astype(BF16)
        xhat_ref[...] = xhat
        rstd_ref[...] = jnp.broadcast_to(rstd, (tm, LANES))
        ya_ref[...] = ya.astype(BF16)
        yb_ref[...] = yb.astype(BF16)
        mix_ref[...] = mix

    outs = [(D_MODEL, BF16), (D_MODEL, F32), (LANES, F32), (D_MODEL, BF16), (D_MODEL, BF16), (D_MODEL, BF16)]
    return pl.pallas_call(
        body, name="fwd_mix", grid=(tokens // tm,),
        out_shape=tuple(jax.ShapeDtypeStruct((tokens, w), dt) for w, dt in outs),
        in_specs=[_rows(tm, DIL_WIDTH), _rows(tm, DIL_WIDTH), _rows(tm, 2 * D_MODEL), _rows(tm, D_MODEL),
                  _full(b_gate.shape), _full(w_oa.shape), _full(w_ob.shape), _full(w_out.shape),
                  _full(ln_g.shape), _full(ln_b.shape)],
        out_specs=tuple(_rows(tm, w) for w, _ in outs),
        compiler_params=_cp("parallel"),
    )(o_a, o_b, gates, x, b_gate, w_oa, w_ob, w_out, ln_g, ln_b)


def _fwd_mlp(hb, xhat1, target, w_ff1, w_ff2, ln1_g, ln1_b, ln_g, ln_b, *, seq):
    tokens = hb.shape[0]
    tm = min(2 * TOKEN_TILE, seq)
    tf = FF_SHARD
    nf = N_DEV // FF_STEP

    def body(hb_ref, xh_ref, tg_ref, w1_ref, w2_ref, g1_ref, b1_ref, g_ref, b_ref, u_ref, dz_ref, dzb_ref, stat_ref, acc):
        i, j = pl.program_id(0), pl.program_id(1)

        @pl.when((i == 0) & (j == 0))
        def _():
            stat_ref[...] = jnp.zeros_like(stat_ref)

        @pl.when(j == 0)
        def _():
            acc[...] = jnp.zeros_like(acc)

        acts = []
        for s in range(FF_STEP):
            u = _dot(hb_ref[...], w1_ref[s])
            u_ref[:, s * tf:(s + 1) * tf] = u.astype(BF16)
            acts.append(jnp.square(jnp.maximum(u, 0.0)).astype(BF16))
        acc[...] += _dot(jnp.concatenate(acts, axis=1), w2_ref[...])

        @pl.when(j == nf - 1)
        def _():
            z = ALPHA * (xh_ref[...] * g1_ref[...] + b1_ref[...]) + acc[...]
            zc = z - jnp.mean(z, axis=1, keepdims=True)
            rstd = lax.rsqrt(jnp.mean(zc * zc, axis=1, keepdims=True) + LN_EPS)
            xhat = zc * rstd
            err = xhat * g_ref[...] + b_ref[...] - tg_ref[...]
            dy = err * (1.0 / D_MODEL)
            dz = _ln_bwd(dy, xhat, rstd, g_ref[...])
            dz_ref[...] = dz
            dzb_ref[...] = dz.astype(BF16)
            stat_ref[0:1, :] += jnp.sum(dy * xhat, axis=0, keepdims=True)
            stat_ref[1:2, :] += jnp.sum(dy, axis=0, keepdims=True)
            stat_ref[2:3, :] += jnp.sum(jnp.sum(err * err, axis=1, keepdims=True), axis=0, keepdims=True) * (0.5 / D_MODEL)

    return pl.pallas_call(
        body, name="fwd_mlp", grid=(tokens // tm, nf),
        out_shape=(jax.ShapeDtypeStruct((tokens, D_FF), BF16), jax.ShapeDtypeStruct((tokens, D_MODEL), F32),
                   jax.ShapeDtypeStruct((tokens, D_MODEL), BF16), jax.ShapeDtypeStruct((8, D_MODEL), F32)),
        in_specs=[_rows(tm, D_MODEL), _rows(tm, D_MODEL), _rows(tm, D_MODEL),
                  pl.BlockSpec((FF_STEP, D_MODEL, tf), lambda i, j: (j, 0, 0)),
                  pl.BlockSpec((FF_STEP * tf, D_MODEL), lambda i, j: (j, 0)),
                  _full(ln1_g.shape), _full(ln1_b.shape), _full(ln_g.shape), _full(ln_b.shape)],
        out_specs=(pl.BlockSpec((tm, FF_STEP * tf), lambda i, j: (i, j)), _rows(tm, D_MODEL), _rows(tm, D_MODEL),
                   _full((8, D_MODEL))),
        scratch_shapes=[pltpu.VMEM((tm, D_MODEL), F32)],
        compiler_params=_cp("arbitrary", "arbitrary"),
    )(hb, xhat1, target, w_ff1, w_ff2, ln1_g, ln1_b, ln_g, ln_b)


def _bwd_mlp(dz2, dz2b, u, xhat1, rstd1, w_ff1, w_ff2, ln_g, *, seq):
    tokens = dz2.shape[0]
    tm = min(2 * TOKEN_TILE, seq)
    tf = FF_SHARD
    nf = N_DEV // FF_STEP

    def body(dz_ref, dzb_ref, u_ref, xh_ref, rs_ref, w1_ref, w2_ref, g_ref, du_ref, dz1_ref, dz1b_ref, stat_ref, acc):
        i, j = pl.program_id(0), pl.program_id(1)

        @pl.when((i == 0) & (j == 0))
        def _():
            stat_ref[...] = jnp.zeros_like(stat_ref)

        @pl.when(j == 0)
        def _():
            acc[...] = jnp.zeros_like(acc)

        da = _dot_nt(dzb_ref[...], w2_ref[...])
        du = (da * (2.0 * jnp.maximum(u_ref[...].astype(F32), 0.0))).astype(BF16)
        du_ref[...] = du
        part = _dot_nt(du[:, 0:tf], w1_ref[0])
        for s in range(1, FF_STEP):
            part = part + _dot_nt(du[:, s * tf:(s + 1) * tf], w1_ref[s])
        acc[...] += part

        @pl.when(j == nf - 1)
        def _():
            dh = ALPHA * dz_ref[...] + acc[...]
            xhat = xh_ref[...]
            dz1 = _ln_bwd(dh, xhat, rs_ref[:, 0:1], g_ref[...])
            dz1_ref[...] = dz1
            dz1b_ref[...] = dz1.astype(BF16)
            stat_ref[0:1, :] += jnp.sum(dh * xhat, axis=0, keepdims=True)
            stat_ref[1:2, :] += jnp.sum(dh, axis=0, keepdims=True)

    return pl.pallas_call(
        body, name="bwd_mlp", grid=(tokens // tm, nf),
        out_shape=(jax.ShapeDtypeStruct((tokens, D_FF), BF16), jax.ShapeDtypeStruct((tokens, D_MODEL), F32),
                   jax.ShapeDtypeStruct((tokens, D_MODEL), BF16), jax.ShapeDtypeStruct((8, D_MODEL), F32)),
        in_specs=[_rows(tm, D_MODEL), _rows(tm, D_MODEL), pl.BlockSpec((tm, FF_STEP * tf), lambda i, j: (i, j)),
                  _rows(tm, D_MODEL), _rows(tm, LANES),
                  pl.BlockSpec((FF_STEP, D_MODEL, tf), lambda i, j: (j, 0, 0)),
                  pl.BlockSpec((FF_STEP * tf, D_MODEL), lambda i, j: (j, 0)),
                  _full(ln_g.shape)],
        out_specs=(pl.BlockSpec((tm, FF_STEP * tf), lambda i, j: (i, j)), _rows(tm, D_MODEL), _rows(tm, D_MODEL),
                   _full((8, D_MODEL))),
        scratch_shapes=[pltpu.VMEM((tm, D_MODEL), F32)],
        compiler_params=_cp("arbitrary", "arbitrary"),
    )(dz2, dz2b, u, xhat1, rstd1, w_ff1, w_ff2, ln_g)


def _bwd_mix(dz1b, gates, y_a, y_b, b_gate, w_oa, w_ob, w_out, *, seq):
    tokens = dz1b.shape[0]
    tm = min(MIX_TILE, seq)

    def body(dz_ref, gt_ref, ya_ref, yb_ref, bg_ref, woa_ref, wob_ref, wout_ref,
             dgt_ref, dya_ref, dyb_ref, doa_ref, dob_ref, stat_ref):
        @pl.when(pl.program_id(0) == 0)
        def _():
            stat_ref[...] = jnp.zeros_like(stat_ref)

        dmix = _dot_nt(dz_ref[...], wout_ref[...])
        for k, (y_ref, w_ref, dy_ref, do_ref) in enumerate(((ya_ref, woa_ref, dya_ref, doa_ref), (yb_ref, wob_ref, dyb_ref, dob_ref))):
            g = _sigmoid(gt_ref[:, k * D_MODEL:(k + 1) * D_MODEL].astype(F32) + bg_ref[k:k + 1, :])
            dgate = dmix * y_ref[...].astype(F32) * g * (1.0 - g)
            dgt_ref[:, k * D_MODEL:(k + 1) * D_MODEL] = dgate.astype(BF16)
            stat_ref[k:k + 1, :] += jnp.sum(dgate, axis=0, keepdims=True)
            dy = (dmix * g).astype(BF16)
            dy_ref[...] = dy
            do_ref[...] = _dot_nt(dy, w_ref[...]).astype(BF16)

    outs = [(2 * D_MODEL, BF16), (D_MODEL, BF16), (D_MODEL, BF16), (DIL_WIDTH, BF16), (DIL_WIDTH, BF16)]
    return pl.pallas_call(
        body, name="bwd_mix", grid=(tokens // tm,),
        out_shape=tuple(jax.ShapeDtypeStruct((tokens, w), dt) for w, dt in outs) + (jax.ShapeDtypeStruct((8, D_MODEL), F32),),
        in_specs=[_rows(tm, D_MODEL), _rows(tm, 2 * D_MODEL), _rows(tm, D_MODEL), _rows(tm, D_MODEL),
                  _full(b_gate.shape), _full(w_oa.shape), _full(w_ob.shape), _full(w_out.shape)],
        out_specs=tuple(_rows(tm, w) for w, _ in outs) + (_full((8, D_MODEL)),),
        compiler_params=_cp("arbitrary"),
    )(dz1b, gates, y_a, y_b, b_gate, w_oa, w_ob, w_out)


def _bwd_proj(dqp, dkp, dvm, dq_d, dk_d, dv_d, dgates, dz1, low, w_in_ext, w1, wk, wv, g_q, g_kv, cext, sext, cs128, *, seq):
    tokens = dz1.shape[0]
    tm = min(TOKEN_TILE, seq)
    ns = seq // tm

    def body(dqp_ref, dkp_ref, dvm_ref, dqd_ref, dkd_ref, dvd_ref, dgt_ref, dz_ref, low_ref, win_ref, w1_ref, wk_ref,
             wv_ref, gq_ref, gkv_ref, c_ref, s_ref, cs_ref, dx_ref, dproj_ref, da_ref, dkn_ref, stat_ref):
        @pl.when(pl.program_id(0) == 0)
        def _():
            stat_ref[...] = jnp.zeros_like(stat_ref)

        low = low_ref[...]
        d_a = _rope_slabs(dqp_ref[...].astype(F32), c_ref[...], s_ref[...], True).astype(BF16)
        da_ref[...] = d_a
        q_a = low[:, 0:Q_LORA]
        _, rq = _rms(q_a, gq_ref[...])
        dq_a, gq_terms = _rms_bwd(_dot_nt(d_a, w1_ref[...]), q_a, rq, gq_ref[...])
        kv_a = low[:, Q_LORA:Q_LORA + KV_LORA]
        _, rkv = _rms(kv_a, gkv_ref[...])
        dkn = jnp.concatenate([dkp_ref[:, p * PAIR_W:p * PAIR_W + LANES] for p in range(N_PAIRS)], axis=1)
        dkn_ref[...] = dkn
        dkv_a, gkv_terms = _rms_bwd(_dot_nt(dkn, wk_ref[...]) + _dot_nt(dvm_ref[...], wv_ref[...]), kv_a, rkv, gkv_ref[...])
        dkr = sum(dkp_ref[:, p * PAIR_W + LANES:(p + 1) * PAIR_W].astype(F32) for p in range(N_PAIRS))
        dkr = dkr + pltpu.roll(dkr, LANES - ROPE, 1)
        dkr = jnp.where(lax.broadcasted_iota(I32, dkr.shape, 1) < ROPE, dkr, 0.0)
        dkr = (dkr + pltpu.roll(dkr, ROPE, 1)) * cs_ref[...]
        stat_ref[0:1, 0:Q_LORA] += jnp.sum(gq_terms, axis=0, keepdims=True)
        stat_ref[1:2, 0:KV_LORA] += jnp.sum(gkv_terms, axis=0, keepdims=True)
        dproj_ref[:, 0:Q_LORA] = dq_a.astype(BF16)
        dproj_ref[:, Q_LORA:Q_LORA + KV_LORA] = dkv_a.astype(BF16)
        dproj_ref[:, Q_LORA + KV_LORA:LOW_W] = dkr.astype(BF16)
        dproj_ref[:, LOW_W:LOW_W + DIL_WIDTH] = dqd_ref[...]
        dproj_ref[:, LOW_W + DIL_WIDTH:LOW_W + 2 * DIL_WIDTH] = dkd_ref[...]
        dproj_ref[:, LOW_W + 2 * DIL_WIDTH:LOW_W + 3 * DIL_WIDTH] = dvd_ref[...]
        dproj_ref[:, LOW_W + 3 * DIL_WIDTH:] = dgt_ref[...]
        dx_ref[...] = ALPHA * dz_ref[...] + _dot_nt(dproj_ref[...], win_ref[...])

    wide = N_PAIRS * PAIR_W
    return pl.pallas_call(
        body, name="bwd_proj", grid=(tokens // tm,),
        out_shape=(jax.ShapeDtypeStruct((tokens, D_MODEL), F32), jax.ShapeDtypeStruct((tokens, IN_EXT), BF16),
                   jax.ShapeDtypeStruct((tokens, wide), BF16), jax.ShapeDtypeStruct((tokens, N_HEADS * NOPE), BF16),
                   jax.ShapeDtypeStruct((8, D_MODEL), F32)),
        in_specs=[_rows(tm, wide), _rows(tm, wide), _rows(tm, DIL_WIDTH), _rows(tm, DIL_WIDTH), _rows(tm, DIL_WIDTH),
                  _rows(tm, DIL_WIDTH), _rows(tm, 2 * D_MODEL),
                  _rows(tm, D_MODEL), _rows(tm, LOW_W), _full(w_in_ext.shape), _full(w1.shape),
                  _full(wk.shape), _full(wv.shape), _full(g_q.shape), _full(g_kv.shape),
                  pl.BlockSpec((tm, LANES), lambda i: (i % ns, 1)), pl.BlockSpec((tm, LANES), lambda i: (i % ns, 1)),
                  pl.BlockSpec((tm, LANES), lambda i: (i % ns, 0))],
        out_specs=(_rows(tm, D_MODEL), _rows(tm, IN_EXT), _rows(tm, wide), _rows(tm, N_HEADS * NOPE), _full((8, D_MODEL))),
        compiler_params=_cp("arbitrary"),
    )(dqp, dkp, dvm, dq_d, dk_d, dv_d, dgates, dz1, low, w_in_ext, w1, wk, wv, g_q, g_kv, cext, sext, cs128)


def _wgrad(a, b, name, square_relu=False, by_shard=False):
    tokens, ka = a.shape
    n = b.shape[1]
    tka = min(ka, 512)
    shard = n // N_DEV
    tn = WGRAD_SHARDS * shard if by_shard else max(w for w in range(LANES, min(n, 2304) + 1, LANES) if n % w == 0)
    tt = min(tokens, 1024)
    nt = tokens // tt

    def body(a_ref, b_ref, o_ref, acc):
        kt = pl.program_id(2)

        @pl.when(kt == 0)
        def _():
            acc[...] = jnp.zeros_like(acc)

        at = a_ref[...]
        if square_relu:
            at = jnp.square(jnp.maximum(at.astype(F32), 0.0)).astype(BF16)
        acc[...] += _dot_tn(at, b_ref[...])

        @pl.when(kt == nt - 1)
        def _():
            if by_shard:
                for s in range(WGRAD_SHARDS):
                    o_ref[s] = acc[:, s * shard:(s + 1) * shard].astype(BF16)
            else:
                o_ref[...] = acc[...].astype(BF16)

    if by_shard:
        out_shape, out_spec = (N_DEV, ka, shard), pl.BlockSpec((WGRAD_SHARDS, tka, shard), lambda i, j, k: (j, i, 0))
    else:
        out_shape, out_spec = (ka, n), pl.BlockSpec((tka, tn), lambda i, j, k: (i, j))
    return pl.pallas_call(
        body, name=name, grid=(ka // tka, n // tn, nt), out_shape=jax.ShapeDtypeStruct(out_shape, BF16),
        in_specs=[pl.BlockSpec((tt, tka), lambda i, j, k: (k, i)), pl.BlockSpec((tt, tn), lambda i, j, k: (k, j))],
        out_specs=out_spec,
        scratch_shapes=[pltpu.VMEM((tka, tn), F32)],
        compiler_params=_cp("parallel", "parallel", "arbitrary"),
    )(a, b)


def _adam_math(w, g, m, v):
    m = ADAM_B1 * m + (1.0 - ADAM_B1) * g
    v = ADAM_B2 * v + (1.0 - ADAM_B2) * jnp.square(g)
    m_hat = m / (1.0 - ADAM_B1 ** ADAM_STEP)
    v_hat = v / (1.0 - ADAM_B2 ** ADAM_STEP)
    return -ADAM_LR * (m_hat / (jnp.sqrt(v_hat) + ADAM_EPS) + ADAM_WD * w), m, v


def _adamw(w, m, v, own, parts, name):
    rows, cols = w.shape
    tr = _row_tile(rows)
    n_parts = parts.shape[0]

    def body(slot_ref, w_ref, m_ref, v_ref, own_ref, p_ref, g_ref, d_ref, nm_ref, nv_ref):
        g = own_ref[...].astype(F32)
        for d in range(n_parts):
            g = g + p_ref[d].astype(F32)
        g_ref[...] = g
        d_ref[...], nm_ref[...], nv_ref[...] = _adam_math(w_ref[...], g, m_ref[...], v_ref[...])

    x, y, c = _place()
    blk = pl.BlockSpec((tr, cols), lambda i, slot: (i, 0))
    own_blk = blk if own.ndim == 2 else pl.BlockSpec((None, tr, cols), lambda i, slot: (slot[0], i, 0))
    return pl.pallas_call(
        body, name=name,
        grid_spec=pltpu.PrefetchScalarGridSpec(
            num_scalar_prefetch=1, grid=(rows // tr,),
            in_specs=[blk, blk, blk, own_blk, pl.BlockSpec((n_parts, tr, cols), lambda i, slot: (0, i, 0))],
            out_specs=(blk,) * 4),
        out_shape=(jax.ShapeDtypeStruct((rows, cols), F32),) * 4, compiler_params=_cp("parallel"),
    )(jnp.reshape(4 * x + 2 * y + c, (1,)).astype(I32), w, m, v, own, parts)


def _adamw_small(parts, w, m, v):
    _, rows, cols = parts.shape

    def body(p_ref, w_ref, m_ref, v_ref, g_ref, d_ref, nm_ref, nv_ref):
        g = p_ref[0]
        for d in range(1, N_DEV):
            g = g + p_ref[d]
        g_ref[...] = g
        d_ref[...], nm_ref[...], nv_ref[...] = _adam_math(w_ref[...], g, m_ref[...], v_ref[...])

    return pl.pallas_call(
        body, name="adamw_replicated", out_shape=(jax.ShapeDtypeStruct((rows, cols), F32),) * 4,
        in_specs=[_full(parts.shape)] + [_full((rows, cols))] * 3, out_specs=(_full((rows, cols)),) * 4, grid=(1,),
        compiler_params=_cp("arbitrary"),
    )(parts, w, m, v)


def _pad_rows(a2d, mult):
    pad = (-a2d.shape[-2]) % mult
    return jnp.pad(a2d, [(0, 0)] * (a2d.ndim - 2) + [(0, pad), (0, 0)]) if pad else a2d


def _pad_cols(a):
    pad = (-a.shape[-1]) % LANES
    return jnp.pad(a, [(0, 0)] * (a.ndim - 1) + [(0, pad)]) if pad else a


def _rot_cols(w):
    half = ROPE // 2
    return jnp.concatenate([-w[..., half:], w[..., :half]], axis=-1)


def _unrot_cols(dw):
    half = ROPE // 2
    return jnp.concatenate([dw[..., half:], -dw[..., :half]], axis=-1)


def _from_col_shards(stacked):
    return stacked.transpose(1, 0, 2).reshape(stacked.shape[1], -1)


def _to_col_shards(full):
    r = full.shape[0]
    return full.reshape(r, N_DEV, -1).transpose(1, 0, 2)


def _rope_tables(seq):
    half = ROPE // 2
    inv = jnp.power(ROPE_THETA, -jnp.arange(half, dtype=F32) / half)
    ang = jnp.arange(seq, dtype=F32)[:, None] * inv[None, :]
    cos = jnp.concatenate([jnp.cos(ang)] * 2, axis=1)
    sin = jnp.concatenate([jnp.sin(ang)] * 2, axis=1)
    ones, zeros = jnp.ones((seq, 2 * NOPE), F32), jnp.zeros((seq, 2 * NOPE), F32)
    pad = jnp.zeros((seq, PAIR_W - 2 * NOPE - 2 * ROPE), F32)
    cext = jnp.concatenate([ones, cos, cos, pad], axis=1)
    sext = jnp.concatenate([zeros, sin, sin, pad], axis=1)
    cs128 = jnp.concatenate([cos, sin, jnp.zeros((seq, LANES - 2 * ROPE), F32)], axis=1)
    return cext, sext, cs128


def _pair_slabs(nope, rope):
    k = nope.shape[0]
    nope = nope.reshape(k, N_PAIRS, 2 * NOPE)
    rope = jnp.zeros((k, N_PAIRS, 2 * ROPE), nope.dtype) if rope is None else rope.reshape(k, N_PAIRS, 2 * ROPE)
    pad = jnp.zeros((k, N_PAIRS, PAIR_W - 2 * NOPE - 2 * ROPE), nope.dtype)
    return jnp.concatenate([nope, rope, pad], axis=2).reshape(k, N_PAIRS * PAIR_W)


def _split_slabs(slabs):
    k = slabs.shape[0]
    s = slabs.reshape(k, N_PAIRS, PAIR_W)
    return s[:, :, :2 * NOPE].reshape(k, N_HEADS, NOPE), s[:, :, 2 * NOPE:2 * NOPE + 2 * ROPE].reshape(k, N_HEADS, ROPE)


def kernel(x, w_in, b_gate, g_q_a, w_uq, g_kv_a, w_ukv, w_o_mla, w_o_dil, w_out, ln1_g, ln1_b, w_ff1, w_ff2, ln2_g, ln2_b, loss_target, m_w_in, m_b_gate, m_g_q_a, m_w_uq, m_g_kv_a, m_w_ukv, m_w_o_mla, m_w_o_dil, m_w_out, m_ln1_g, m_ln1_b, m_w_ff1, m_w_ff2, m_ln2_g, m_ln2_b, v_w_in, v_b_gate, v_g_q_a, v_w_uq, v_g_kv_a, v_w_ukv, v_w_o_mla, v_w_o_dil, v_w_out, v_ln1_g, v_ln1_b, v_w_ff1, v_w_ff2, v_ln2_g, v_ln2_b):
    batch, seq, _ = x.shape
    tokens = batch * seq
    weights = dict(w_in=w_in, w_uq=w_uq, w_ukv=w_ukv, w_o_mla=w_o_mla, w_o_dil=w_o_dil, w_out=w_out, w_ff1=w_ff1, w_ff2=w_ff2, b_gate=b_gate)
    mom_m = dict(w_in=m_w_in, w_uq=m_w_uq, w_ukv=m_w_ukv, w_o_mla=m_w_o_mla, w_o_dil=m_w_o_dil, w_out=m_w_out, w_ff1=m_w_ff1, w_ff2=m_w_ff2, b_gate=m_b_gate)
    mom_v = dict(w_in=v_w_in, w_uq=v_w_uq, w_ukv=v_w_ukv, w_o_mla=v_w_o_mla, w_o_dil=v_w_o_dil, w_out=v_w_out, w_ff1=v_w_ff1, w_ff2=v_w_ff2, b_gate=v_b_gate)

    first = ["w_in", "w_uq", "w_ukv"]
    widths = [weights[n].shape[2] for n in first]
    shards = [_pad_cols(weights[n][0].astype(BF16)) for n in first]
    (g_in, g_uq, g_ukv), = _run_comm([(_Gather(shards), shards)], "all_gather_first_weights")
    g_uq, g_ukv = g_uq[:, :, :widths[1]], g_ukv[:, :, :widths[2]]

    s1, s2, n_in = Q_LORA + KV_LORA, Q_LORA + KV_LORA + ROPE, N_DEV * widths[0]

    def w_in_cols(lo, hi):
        out = []
        while lo < hi:
            d, off = divmod(lo, widths[0])
            take = min(hi - lo, widths[0] - off)
            out.append(g_in[d][:, off:off + take])
            lo += take
        return out

    w_in_ext = jnp.concatenate(w_in_cols(0, s2) + [_rot_cols(jnp.concatenate(w_in_cols(s1, s2), axis=1)),
                                                   jnp.zeros((D_MODEL, LOW_W - s2 - ROPE), BF16)] + w_in_cols(s2, n_in), axis=1)
    uq = _from_col_shards(g_uq).reshape(Q_LORA, N_HEADS, NOPE + ROPE)
    w1 = _pair_slabs(uq[:, :, :NOPE], uq[:, :, NOPE:])
    ukv = _from_col_shards(g_ukv).reshape(KV_LORA, N_HEADS, NOPE + HEAD_V)
    wk = ukv[:, :, :NOPE].reshape(KV_LORA, N_HEADS * NOPE)
    wv = ukv[:, :, NOPE:].reshape(KV_LORA, N_HEADS * HEAD_V)
    cext, sext, cs128 = _rope_tables(seq)
    dil_bias = _dilated_bias_table(seq)
    no_bias = jnp.zeros((1, 8, LANES), F32)

    x2 = x.reshape(tokens, D_MODEL)
    low, gates, qkvd, qp, kp, vm, qn, kvn, xb = _fwd_proj(x2, w_in_ext, w1, wk, wv, g_q_a, g_kv_a, cext, sext, cs128, seq=seq)
    bg = b_gate[0]
    bg_hi = bg.astype(BF16)
    bg_lo = (bg - bg_hi.astype(F32)).astype(BF16)
    later = [weights[n][0].astype(BF16) for n in ("w_o_mla", "w_o_dil", "w_out", "w_ff1", "w_ff2")]
    later.append(_pad_rows(jnp.concatenate([bg_hi, bg_lo], axis=0), 16))
    mla = dict(batch=batch, seq=seq, width=PAIR_W, col0=(0, 0, 0), dilated=False, scale=MLA_SCALE)
    dil = dict(batch=batch, seq=seq, width=LANES, col0=(0, N_PAIRS, 2 * N_PAIRS), dilated=True, scale=DIL_SCALE)
    o_a, lse_a, g_oa, g_ob, g_out, g_ff1, g_ff2, g_bg = _attn_fwd(
        qp, kp, vm, no_bias, name="mla_attention_fwd", comm=_Gather(later), comm_arrays=later, **mla)
    o_b, lse_b = _attn_fwd(qkvd, qkvd, qkvd, dil_bias, name="dilated_attention_fwd", **dil)
    w_oa, w_ob = _from_col_shards(g_oa), _from_col_shards(g_ob)
    w_out_full = g_out.reshape(D_MODEL, D_MODEL)
    w_ff2_full = g_ff2.reshape(D_FF, D_MODEL)
    bg_parts = g_bg.astype(F32)
    b_gate_full = _from_col_shards(bg_parts[:, 0:2] + bg_parts[:, 2:4])
    hb, xhat1, rstd1, y_a, y_b, mix = _fwd_mix(o_a, o_b, gates, x2, b_gate_full, w_oa, w_ob, w_out_full, ln1_g, ln1_b, seq=seq)
    u, dz2, dz2b, stat2 = _fwd_mlp(hb, xhat1, loss_target.reshape(tokens, D_MODEL), g_ff1, w_ff2_full, ln1_g, ln1_b, ln2_g, ln2_b, seq=seq)

    du, dz1, dz1b, stat1 = _bwd_mlp(dz2, dz2b, u, xhat1, rstd1, g_ff1, w_ff2_full, ln1_g, seq=seq)
    dw_ff = [_wgrad(hb, du, "wgrad_ff1", by_shard=True),
             _wgrad(u, dz2b, "wgrad_ff2", square_relu=True).reshape(N_DEV, FF_SHARD, D_MODEL)]
    dgates, dy_a, dy_b, do_a, do_b, stat_g = _bwd_mix(dz1b, gates, y_a, y_b, b_gate_full, w_oa, w_ob, w_out_full, seq=seq)
    dqp, dkp, dvm, r_ff1, r_ff2 = _attn_bwd(qp, kp, vm, o_a, do_a, lse_a, no_bias, name="mla_attention_bwd",
                                            comm=_Scatter(dw_ff), comm_arrays=dw_ff, **mla)
    dw_mid = [_to_col_shards(_wgrad(o_a, dy_a, "wgrad_o_mla")), _to_col_shards(_wgrad(o_b, dy_b, "wgrad_o_dil")),
              _wgrad(mix, dz1b, "wgrad_out").reshape(N_DEV, D_MODEL // N_DEV, D_MODEL),
              _pad_rows(_to_col_shards(stat_g[0:2]).astype(BF16), 16)]
    dq_d, dk_d, dv_d, r_oa, r_ob, r_out, r_bg = _attn_bwd(qkvd, qkvd, qkvd, o_b, do_b, lse_b, dil_bias, name="dilated_attention_bwd",
                                                          comm=_Scatter(dw_mid), comm_arrays=dw_mid, **dil)
    grad_x, dproj, d_a, dkn, stat_r = _bwd_proj(dqp, dkp, dvm, dq_d, dk_d, dv_d, dgates, dz1, low, w_in_ext, w1, wk, wv,
                                                g_q_a, g_kv_a, cext, sext, cs128, seq=seq)

    dw_in_ext = _wgrad(xb, dproj, "wgrad_in")
    dw1 = _wgrad(qn, d_a, "wgrad_uq")
    dwk = _wgrad(kvn, dkn, "wgrad_ukv_k")
    dwv = _wgrad(kvn, dvm, "wgrad_ukv_v")
    dw_kr = dw_in_ext[:, s1:s2] + _unrot_cols(dw_in_ext[:, s2:s2 + ROPE])

    def dw_in_cols(lo, hi):
        out = []
        for a, b, piece in ((0, s1, lambda u, v: dw_in_ext[:, u:v]), (s1, s2, lambda u, v: dw_kr[:, u - s1:v - s1]),
                            (s2, n_in, lambda u, v: dw_in_ext[:, u + LOW_W - s2:v + LOW_W - s2])):
            if max(lo, a) < min(hi, b):
                out.append(piece(max(lo, a), min(hi, b)))
        return out

    dw_in = jnp.stack([_pad_cols(jnp.concatenate(dw_in_cols(d * widths[0], (d + 1) * widths[0]), axis=1)) for d in range(N_DEV)])
    n1, r1 = _split_slabs(dw1)
    dw_uq = jnp.concatenate([n1, r1], axis=2).reshape(Q_LORA, N_HEADS * (NOPE + ROPE))
    dw_ukv = jnp.concatenate([dwk.reshape(KV_LORA, N_HEADS, NOPE), dwv.reshape(KV_LORA, N_HEADS, HEAD_V)], axis=2).reshape(KV_LORA, N_HEADS * (NOPE + HEAD_V))
    last = [dw_in] + [_pad_cols(_to_col_shards(dw)) for dw in (dw_uq, dw_ukv)]
    theirs = _rs_sibling(last, "rs_last_sibling_exchange")
    sums = [_pair_sum(a, b, "rs_last_pair_sum_" + n) for a, b, n in zip(last, theirs, first)]
    partial = jnp.concatenate([stat_r[0:1, :Q_LORA], stat_r[1:2, :KV_LORA], stat1[0:1], stat1[1:2], stat2[0:1], stat2[1:2],
                               stat2[2:3, :LANES]], axis=1)
    partial = _pad_rows(partial.reshape(-1, LANES), 8)
    rest = [s[1] for s in sums]
    got, (every,) = _run_comm([(_ChipExchange(rest), rest), (_Gather([partial]), [partial])], "rs_last_chip_exchange")

    upd = {}
    for n, w, (own, _), parts in zip(first, widths, sums, got):
        upd[n] = _adamw(weights[n][0], mom_m[n][0], mom_v[n][0], own[:, :w], parts[:, :, :w], "adamw_" + n)
    for n, own, parts in (("w_o_mla", dw_mid[0], r_oa), ("w_o_dil", dw_mid[1], r_ob), ("w_out", dw_mid[2], r_out),
                          ("w_ff1", dw_ff[0], r_ff1), ("w_ff2", dw_ff[1], r_ff2)):
        upd[n] = _adamw(weights[n][0], mom_m[n][0], mom_v[n][0], own, parts, "adamw_" + n)
    bg_upd = _adamw(_pad_rows(b_gate[0], 16), _pad_rows(m_b_gate[0], 16), _pad_rows(v_b_gate[0], 16), dw_mid[3], r_bg, "adamw_b_gate")
    upd["b_gate"] = tuple(t[0:2] for t in bg_upd)

    small_w = [g_q_a, g_kv_a, ln1_g, ln1_b, ln2_g, ln2_b]
    small_m = [m_g_q_a, m_g_kv_a, m_ln1_g, m_ln1_b, m_ln2_g, m_ln2_b]
    small_v = [v_g_q_a, v_g_kv_a, v_ln1_g, v_ln1_b, v_ln2_g, v_ln2_b]
    small_widths = [a.shape[1] for a in small_w]

    def as_rows(vecs, extra):
        flat = jnp.concatenate(vecs + [jnp.zeros((1, extra), F32)], axis=1)
        return _pad_rows(flat.reshape(-1, LANES), 8)

    g_s, d_s, nm_s, nv_s = _adamw_small(every, as_rows(small_w, LANES), as_rows(small_m, LANES), as_rows(small_v, LANES))

    def split_small(a):
        flat = a.reshape(1, -1)
        out, c0 = [], 0
        for w in small_widths:
            out.append(flat[:, c0:c0 + w])
            c0 += w
        return out, flat[0, c0]

    g_small, loss = split_small(g_s)
    small = [g_small, split_small(d_s)[0], split_small(nm_s)[0], split_small(nv_s)[0]]

    order = ["w_in", "b_gate", "g_q_a", "w_uq", "g_kv_a", "w_ukv", "w_o_mla", "w_o_dil", "w_out", "ln1_g", "ln1_b", "w_ff1", "w_ff2", "ln2_g", "ln2_b"]
    small_names = ["g_q_a", "g_kv_a", "ln1_g", "ln1_b", "ln2_g", "ln2_b"]

    def pick(kind):
        return [small[kind][small_names.index(n)] if n in small_names else upd[n][kind][None] for n in order]

    return (loss, grad_x.reshape(batch, seq, D_MODEL), *pick(0), *pick(1), *pick(2), *pick(3))
```

```python
import functools
import math

import jax
import jax.numpy as jnp
from jax import lax
from jax.experimental import pallas as pl
from jax.experimental.pallas import tpu as pltpu

F32 = jnp.float32
BF16 = jnp.bfloat16
I32 = jnp.int32

D_MODEL = 1024
N_HEADS = 8
NOPE = 64
ROPE = 32
HEAD_V = 64
Q_LORA = 384
KV_LORA = 256
DIL_WIDTH = 512
D_FF = 4096
ROPE_THETA = 10000.0
LN_EPS = 1e-5
RMS_EPS = 1e-6
NEG = -1e30
ALPHA = 2.0 ** 0.25
MLA_SCALE = (NOPE + ROPE) ** -0.5
DIL_SCALE = 64 ** -0.5
ADAM_LR, ADAM_B1, ADAM_B2, ADAM_EPS, ADAM_WD, ADAM_STEP = 0.001, 0.9, 0.999, 1e-08, 0.01, 10

LANES = 128
PAIR_W = 256
N_PAIRS = N_HEADS // 2
LOW_W = 768
IN_EXT = LOW_W + 3 * DIL_WIDTH + 2 * D_MODEL
N_DEV = 8
FF_SHARD = D_FF // N_DEV
FF_STEP = 4
WGRAD_SHARDS = 4
TOKEN_TILE = 256
MIX_TILE = 512
ATTN_TILE = 256
VMEM_LIMIT = 56 << 20

MESH = pl.DeviceIdType.MESH
ANY = pl.BlockSpec(memory_space=pl.ANY)
CHIP_FLIPS = ((0, 0), (0, 1), (1, 0), (1, 1))
PEER_FLIPS = tuple((fx, fy, fc) for fx in (0, 1) for fy in (0, 1) for fc in (0, 1))[1:]


def _cp(*sem):
    return pltpu.CompilerParams(dimension_semantics=sem or None, vmem_limit_bytes=VMEM_LIMIT)


def _full(shape):
    nd = len(shape)
    return pl.BlockSpec(shape, lambda *_: (0,) * nd)


def _rows(tm, width):
    return pl.BlockSpec((tm, width), lambda i, *_: (i, 0))


def _dot(a, b):
    return jnp.dot(a, b, preferred_element_type=F32)


def _dot_nt(a, b):
    return lax.dot_general(a, b, (((1,), (1,)), ((), ())), preferred_element_type=F32)


def _dot_tn(a, b):
    return lax.dot_general(a, b, (((0,), (0,)), ((), ())), preferred_element_type=F32)


def _sigmoid(z):
    return 1.0 / (1.0 + jnp.exp(-z))


def _place():
    return lax.axis_index("x"), lax.axis_index("y"), lax.axis_index("c")


def _flip(v, f):
    return 1 - v if f else v


class _Gather:
    def __init__(self, shards):
        self.n = len(shards)
        self.out_shape = [jax.ShapeDtypeStruct((N_DEV, *s.shape), s.dtype) for s in shards]
        self.scratch = [pltpu.SemaphoreType.DMA((7 * self.n,)), pltpu.SemaphoreType.DMA((7 * self.n,)),
                        pltpu.SemaphoreType.DMA((self.n,))]

    def _copies(self, what, srcs, dsts, send, recv, local):
        x, y, c = _place()
        chips = [(_flip(x, fx), _flip(y, fy)) for fx, fy in CHIP_FLIPS[1:]]
        out = []
        for a in range(self.n):
            def slot(px, py, pc, a=a):
                return dsts[a].at[4 * px + 2 * py + pc]

            def copy(k, block, to, src=None, a=a, slot=slot):
                return pltpu.make_async_remote_copy(
                    src_ref=slot(*block) if src is None else src, dst_ref=slot(*block),
                    send_sem=send.at[7 * a + k], recv_sem=recv.at[7 * a + k], device_id=to, device_id_type=MESH)

            if what == "mine":
                out.append(pltpu.make_async_copy(srcs[a], slot(x, y, c), local.at[a]))
            elif what == "first":
                out.append(copy(0, (x, y, c), (x, y, 1 - c), src=srcs[a]))
                out += [copy(1 + j, (x, y, c), (*chip, c), src=srcs[a]) for j, chip in enumerate(chips)]
            elif what == "landed":
                out += [copy(1 + j, (*chip, c), (x, y, c)) for j, chip in enumerate(chips)]
            elif what == "passed":
                out += [copy(4 + j, (*chip, c), (x, y, 1 - c)) for j, chip in enumerate(chips)]
            else:
                out.append(copy(0, (x, y, 1 - c), (x, y, c)))
                out += [copy(4 + j, (*chip, 1 - c), (x, y, c)) for j, chip in enumerate(chips)]
        return out

    def start(self, *refs):
        for cp in self._copies("first", *refs) + self._copies("mine", *refs):
            cp.start()

    def forward(self, *refs):
        for landed, passed in zip(self._copies("landed", *refs), self._copies("passed", *refs)):
            landed.wait_recv()
            passed.start()

    def finish(self, *refs):
        for cp in self._copies("from_sibling", *refs):
            cp.wait_recv()
        for cp in self._copies("first", *refs) + self._copies("passed", *refs):
            cp.wait_send()
        for cp in self._copies("mine", *refs):
            cp.wait()


class _Scatter:
    def __init__(self, arrays):
        self.n = len(arrays)
        self.out_shape = [jax.ShapeDtypeStruct((7, *a.shape[1:]), a.dtype) for a in arrays]
        self.scratch = [pltpu.SemaphoreType.DMA((7 * self.n,)), pltpu.SemaphoreType.DMA((7 * self.n,))]

    def _copies(self, srcs, dsts, send, recv):
        x, y, c = _place()
        out = []
        for a in range(self.n):
            for k, (fx, fy, fc) in enumerate(PEER_FLIPS):
                px, py, pc = _flip(x, fx), _flip(y, fy), _flip(c, fc)
                out.append(pltpu.make_async_remote_copy(
                    src_ref=srcs[a].at[4 * px + 2 * py + pc], dst_ref=dsts[a].at[k],
                    send_sem=send.at[7 * a + k], recv_sem=recv.at[7 * a + k], device_id=(px, py, pc), device_id_type=MESH))
        return out

    def start(self, *refs):
        for cp in self._copies(*refs):
            cp.start()

    def forward(self, *refs):
        pass

    def finish(self, *refs):
        for cp in self._copies(*refs):
            cp.wait_send()
        for cp in self._copies(*refs):
            cp.wait_recv()


class _ChipExchange:
    def __init__(self, arrays):
        self.n = len(arrays)
        self.out_shape = [jax.ShapeDtypeStruct(a.shape, a.dtype) for a in arrays]
        self.scratch = [pltpu.SemaphoreType.DMA((3 * self.n,)), pltpu.SemaphoreType.DMA((3 * self.n,))]

    def _copies(self, srcs, dsts, send, recv):
        x, y, c = _place()
        return [pltpu.make_async_remote_copy(
            src_ref=srcs[a].at[k], dst_ref=dsts[a].at[k], send_sem=send.at[3 * a + k], recv_sem=recv.at[3 * a + k],
            device_id=(_flip(x, fx), _flip(y, fy), c), device_id_type=MESH)
            for a in range(self.n) for k, (fx, fy) in enumerate(CHIP_FLIPS[1:])]

    def start(self, *refs):
        for cp in self._copies(*refs):
            cp.start()

    def forward(self, *refs):
        pass

    def finish(self, *refs):
        for cp in self._copies(*refs):
            cp.wait_send()
        for cp in self._copies(*refs):
            cp.wait_recv()


class _Plans:
    def __init__(self, plans):
        self.plans = plans
        self.n = sum(p.n for p in plans)
        self.out_shape = [s for p in plans for s in p.out_shape]
        self.scratch = [s for p in plans for s in p.scratch]

    def _each(self, phase, srcs, dsts, *sems):
        i0 = s0 = 0
        for p in self.plans:
            getattr(p, phase)(srcs[i0:i0 + p.n], dsts[i0:i0 + p.n], *sems[s0:s0 + len(p.scratch)])
            i0, s0 = i0 + p.n, s0 + len(p.scratch)

    def start(self, *refs):
        self._each("start", *refs)

    def forward(self, *refs):
        self._each("forward", *refs)

    def finish(self, *refs):
        self._each("finish", *refs)


def _run_comm(comm, arrays, name):
    n = comm.n

    def body(*refs):
        args = (refs[:n], refs[n:2 * n], *refs[2 * n:])
        comm.start(*args)
        comm.forward(*args)
        comm.finish(*args)

    return pl.pallas_call(body, name=name, out_shape=comm.out_shape, in_specs=[ANY] * n, out_specs=[ANY] * n,
                          scratch_shapes=comm.scratch)(*arrays)


def _rs_sibling(arrays, name):
    n = len(arrays)

    def body(*refs):
        srcs, got, (send, recv) = refs[:n], refs[n:2 * n], refs[2 * n:]
        x, y, c = _place()
        copies = []
        for a in range(n):
            for r, (fx, fy) in enumerate(CHIP_FLIPS):
                chip = 2 * _flip(x, fx) + _flip(y, fy)
                copies.append(pltpu.make_async_remote_copy(
                    src_ref=srcs[a].at[2 * chip + 1 - c], dst_ref=got[a].at[r], send_sem=send.at[4 * a + r],
                    recv_sem=recv.at[4 * a + r], device_id=(x, y, 1 - c), device_id_type=MESH))
        for cp in copies:
            cp.start()
        for cp in copies:
            cp.wait_send()
        for cp in copies:
            cp.wait_recv()

    return pl.pallas_call(
        body, name=name, out_shape=[jax.ShapeDtypeStruct((4, *a.shape[1:]), a.dtype) for a in arrays],
        in_specs=[ANY] * n, out_specs=[ANY] * n,
        scratch_shapes=[pltpu.SemaphoreType.DMA((4 * n,)), pltpu.SemaphoreType.DMA((4 * n,))],
    )(*arrays)


def _row_tile(rows):
    return 256 if rows % 256 == 0 else rows


def _chip_slots():
    x, y, c = _place()
    return jnp.stack([4 * _flip(x, fx) + 2 * _flip(y, fy) + c for fx, fy in CHIP_FLIPS]).astype(I32)


def _pair_sum(full, theirs, name):
    _, rows, cols = theirs.shape
    tr = _row_tile(rows)

    def body(slots_ref, m0_ref, m1_ref, m2_ref, m3_ref, b_ref, own_ref, rest_ref):
        own_ref[...] = m0_ref[...].astype(F32) + b_ref[0].astype(F32)
        for k, m_ref in enumerate((m1_ref, m2_ref, m3_ref)):
            rest_ref[k] = (m_ref[...].astype(F32) + b_ref[k + 1].astype(F32)).astype(BF16)

    def mine(k):
        return pl.BlockSpec((None, tr, cols), lambda i, slots: (slots[k], i, 0))

    return pl.pallas_call(
        body, name=name,
        grid_spec=pltpu.PrefetchScalarGridSpec(
            num_scalar_prefetch=1, grid=(rows // tr,),
            in_specs=[mine(0), mine(1), mine(2), mine(3), pl.BlockSpec((4, tr, cols), lambda i, slots: (0, i, 0))],
            out_specs=(pl.BlockSpec((tr, cols), lambda i, slots: (i, 0)), pl.BlockSpec((3, tr, cols), lambda i, slots: (0, i, 0)))),
        out_shape=(jax.ShapeDtypeStruct((rows, cols), F32), jax.ShapeDtypeStruct((3, rows, cols), BF16)),
        compiler_params=_cp("parallel"),
    )(_chip_slots(), full, full, full, full, theirs)


def _head_lanes(width, h):
    lane = lax.broadcasted_iota(I32, (1, width), 1)
    if width == LANES:
        return (lane >= 64 * h) & (lane < 64 * h + 64)
    nope = (lane >= NOPE * h) & (lane < NOPE * h + NOPE)
    rope = (lane >= 2 * NOPE + ROPE * h) & (lane < 2 * NOPE + ROPE * h + ROPE)
    return nope | rope


def _dilated_bias_table(seq):
    t = min(ATTN_TILE, seq)
    nd = seq // t

    def body(o_ref):
        delta = pl.program_id(0) * t + lax.broadcasted_iota(I32, (t, t), 1) - lax.broadcasted_iota(I32, (t, t), 0)
        mult = ((delta <= 128).astype(I32) + (((delta & 3) == 0) & (delta <= 512)).astype(I32)
                + ((delta & 15) == 0).astype(I32))
        logm = jnp.where(mult == 3, math.log(3.0), jnp.where(mult == 2, math.log(2.0), 0.0))
        valid = (delta >= 0) & (mult > 0)
        dist = delta.astype(F32)
        for h in range(N_HEADS):
            o_ref[h] = jnp.where(valid, logm - 2.0 ** (-(h + 1)) * dist, NEG)

    return pl.pallas_call(
        body, name="dilated_bias_table", grid=(nd,), out_shape=jax.ShapeDtypeStruct((N_HEADS, nd, t, t), F32),
        out_specs=pl.BlockSpec((N_HEADS, None, t, t), lambda d: (0, d, 0, 0)),
        compiler_params=_cp("parallel"),
    )()


def _comm_hooks(comm, refs, n_in, n_out):
    if comm is None:
        return refs[:n_in], refs[n_in:n_in + n_out], refs[n_in + n_out:], None
    n = comm.n
    ins, srcs = refs[:n_in], refs[n_in:n_in + n]
    outs, dsts = refs[n_in + n:n_in + n + n_out], refs[n_in + n + n_out:n_in + 2 * n + n_out]
    rest = refs[n_in + 2 * n + n_out:]
    own = len(rest) - len(comm.scratch)
    return ins, outs, rest[:own], (srcs, dsts, *rest[own:])


def _attn_fwd(q, k, v, bias, *, batch, seq, width, col0, dilated, scale, name, comm=None, comm_arrays=()):
    t = min(ATTN_TILE, seq)
    nq = seq // t
    cq, ck, cv = col0
    pre = scale if dilated else 1.0
    steps = batch * N_PAIRS

    def body(*refs):
        (q_ref, k_ref, v_ref, bias_ref), (o_ref, lse_ref), (v_heads,), plan = _comm_hooks(comm, refs, 4, 2)
        step_no = pl.program_id(0) * N_PAIRS + pl.program_id(1)
        if plan:
            pl.when(step_no == 0)(lambda: comm.start(*plan))
            pl.when(step_no == (3 * steps) // 4)(lambda: comm.forward(*plan))
        v_all = v_ref[...].astype(F32)
        for h in (0, 1):
            v_heads[h] = jnp.transpose(jnp.where(_head_lanes(LANES, h), v_all, 0.0)).astype(BF16)
        top = lax.broadcasted_iota(I32, (LANES, t), 0) < HEAD_V
        causal = lax.broadcasted_iota(I32, (t, t), 0) <= lax.broadcasted_iota(I32, (t, t), 1)
        lax.fori_loop(0, nq, functools.partial(query_tile, q_ref, k_ref, bias_ref, o_ref, lse_ref, v_heads, top, causal), 0)
        if plan:
            pl.when(step_no == steps - 1)(lambda: comm.finish(*plan))

    def query_tile(q_ref, k_ref, bias_ref, o_ref, lse_ref, v_heads, top, causal, i, _):
        qs = pl.multiple_of(i * t, t)
        q2 = q_ref[pl.ds(qs, t), :] * pre if dilated else q_ref[pl.ds(qs, t), :]
        qh = [jnp.where(_head_lanes(width, h), q2, jnp.zeros_like(q2)) for h in (0, 1)]

        def scores(j):
            kj = k_ref[pl.ds(pl.multiple_of(j * t, t), t), :]
            return [_dot_nt(kj, qh[h]) for h in (0, 1)]

        def step(j, carry, last):
            m0, l0, m1, l1, acc, s0, s1 = carry
            ahead = [] if last else scores(j + 1)
            ks = pl.multiple_of(j * t, t)
            new, alphas, pv = [], [], []
            for h, (m, l, s) in enumerate(((m0, l0, s0), (m1, l1, s1))):
                if dilated:
                    s = s + bias_ref[h, i - j]
                else:
                    s = s * scale
                    if last:
                        s = jnp.where(causal, s, NEG)
                m_new = jnp.maximum(m, jnp.max(s, axis=0, keepdims=True))
                a = jnp.exp(m - m_new)
                p = jnp.exp(s - m_new)
                new += [m_new, a * l + jnp.sum(p, axis=0, keepdims=True)]
                alphas.append(a)
                pv.append(_dot(v_heads[h, :, pl.ds(ks, t)], p.astype(BF16)))
            acc = jnp.where(top, alphas[0], alphas[1]) * acc + pv[0] + pv[1]
            return (*new, acc, *ahead)

        row = jnp.full((1, t), NEG, F32)
        zero = jnp.zeros((1, t), F32)
        init = (row, zero, row, zero, jnp.zeros((LANES, t), F32), *scores(0))
        m0, l0, m1, l1, acc = step(i, lax.fori_loop(0, i, functools.partial(step, last=False), init), True)
        o_ref[pl.ds(qs, t), :] = jnp.transpose(acc * jnp.where(top, 1.0 / l0, 1.0 / l1)).astype(BF16)
        r = lax.broadcasted_iota(I32, (8, t), 0)
        lse_ref[:, pl.ds(qs, t)] = jnp.where(r == 0, m0 + jnp.log(l0), jnp.where(r == 1, m1 + jnp.log(l1), 0.0))
        return 0

    bias_spec = (pl.BlockSpec((2, nq, t, t), lambda b, p: (p, 0, 0, 0)) if dilated
                 else pl.BlockSpec((None, 8, LANES), lambda b, p: (0, 0, 0)))
    n = comm.n if comm else 0
    return pl.pallas_call(
        body, name=name, grid=(batch, N_PAIRS),
        out_shape=[jax.ShapeDtypeStruct((batch * seq, DIL_WIDTH), BF16), jax.ShapeDtypeStruct((batch * N_PAIRS, 8, seq), F32)]
        + (comm.out_shape if comm else []),
        in_specs=[pl.BlockSpec((seq, width), lambda b, p: (b, cq + p)),
                  pl.BlockSpec((seq, width), lambda b, p: (b, ck + p)),
                  pl.BlockSpec((seq, LANES), lambda b, p: (b, cv + p)),
                  bias_spec] + [ANY] * n,
        out_specs=[pl.BlockSpec((seq, LANES), lambda b, p: (b, p)),
                   pl.BlockSpec((None, 8, seq), lambda b, p: (b * N_PAIRS + p, 0, 0))] + [ANY] * n,
        scratch_shapes=[pltpu.VMEM((2, LANES, seq), BF16)] + (comm.scratch if comm else []),
        compiler_params=_cp("arbitrary", "arbitrary") if comm else _cp("parallel", "parallel"),
    )(q, k, v, bias, *comm_arrays)


def _attn_bwd(q, k, v, o, do, lse, bias, *, batch, seq, width, col0, dilated, scale, name, comm=None, comm_arrays=()):
    t = min(ATTN_TILE, seq)
    nq = seq // t
    cq, ck, cv = col0
    pre = scale if dilated else 1.0
    dq_transposed = width == LANES
    steps = batch * N_PAIRS

    def body(*refs):
        ins, (dq_ref, dk_ref, dv_ref), (dq_acc, dk_acc, dv_acc, rowdot, q_heads, do_heads), plan = _comm_hooks(comm, refs, 7, 3)
        q_ref, k_ref, v_ref, o_ref, do_ref, lse_ref, bias_ref = ins
        step_no = pl.program_id(0) * N_PAIRS + pl.program_id(1)
        if plan:
            pl.when(step_no == 0)(lambda: comm.start(*plan))
        wlane = [_head_lanes(width, h) for h in (0, 1)]
        vlane = [_head_lanes(LANES, h) for h in (0, 1)]
        causal = lax.broadcasted_iota(I32, (t, t), 0) <= lax.broadcasted_iota(I32, (t, t), 1)
        q_all = q_ref[...] * pre if dilated else q_ref[...]
        for h in (0, 1):
            q_heads[h] = jnp.where(wlane[h], q_all, jnp.zeros_like(q_all))
            do_heads[h] = jnp.where(vlane[h], do_ref[...], jnp.zeros_like(do_ref[...]))
        prod = jnp.transpose(do_ref[...].astype(F32) * o_ref[...].astype(F32))
        rowdot[0:1, :] = jnp.sum(prod[0:HEAD_V], axis=0, keepdims=True)
        rowdot[1:2, :] = jnp.sum(prod[HEAD_V:], axis=0, keepdims=True)
        dq_acc[...] = jnp.zeros_like(dq_acc)

        def k_tile(j, _):
            ks = pl.multiple_of(j * t, t)
            kj = k_ref[pl.ds(ks, t), :]
            vj = v_ref[pl.ds(ks, t), :]
            kh = [jnp.where(wlane[h], kj, jnp.zeros_like(kj)) for h in (0, 1)]
            if dq_transposed:
                kh = [jnp.transpose(kh[h].astype(F32)).astype(BF16) for h in (0, 1)]
            dk_acc[...] = jnp.zeros_like(dk_acc)
            dv_acc[...] = jnp.zeros_like(dv_acc)

            def operands(i):
                qs = pl.multiple_of(i * t, t)
                return [q_heads[h, pl.ds(qs, t), :] for h in (0, 1)], [do_heads[h, pl.ds(qs, t), :] for h in (0, 1)]

            def products(i):
                qih, doih = operands(i)
                scores = tuple(_dot_nt(kj, qih[h]) for h in (0, 1))
                return scores + tuple(_dot_nt(vj, doih[h]) for h in (0, 1)) if width > LANES else scores

            def q_tile(n, carry, last):
                i = nq - 1 - n
                ahead = () if last else products(i - 1)
                qs = pl.multiple_of(i * t, t)
                qih, doih = operands(i)
                s0, s1 = carry[:2]
                dps = carry[2:] if width > LANES else [_dot_nt(vj, doih[h]) for h in (0, 1)]
                dq_i = jnp.zeros((width, t) if dq_transposed else (t, width), F32)
                for h, (s, dp) in enumerate(((s0, dps[0]), (s1, dps[1]))):
                    if dilated:
                        s = s + bias_ref[h, i - j]
                    else:
                        s = s * scale
                        if last:
                            s = jnp.where(causal, s, NEG)
                    p = jnp.exp(s - lse_ref[h:h + 1, pl.ds(qs, t)])
                    ds = p * (dp - rowdot[h:h + 1, pl.ds(qs, t)])
                    ds = (ds if dilated else ds * scale).astype(BF16)
                    dv_acc[...] += _dot(p.astype(BF16), doih[h])
                    dk_acc[...] += _dot(ds, qih[h])
                    dq_i = dq_i + (_dot(kh[h], ds) if dq_transposed else _dot_tn(ds, kh[h]))
                if dq_transposed:
                    dq_acc[:, pl.ds(qs, t)] += dq_i
                else:
                    dq_acc[pl.ds(qs, t), :] += dq_i
                return ahead

            q_tile(nq - 1 - j, lax.fori_loop(0, nq - 1 - j, functools.partial(q_tile, last=False), products(nq - 1)), True)
            dk_ref[pl.ds(ks, t), :] = dk_acc[...].astype(BF16)
            dv_ref[pl.ds(ks, t), :] = dv_acc[...].astype(BF16)
            return 0

        lax.fori_loop(0, nq, k_tile, 0)
        dq_ref[...] = ((jnp.transpose(dq_acc[...]) if dq_transposed else dq_acc[...]) * pre).astype(BF16)
        if plan:
            pl.when(step_no == steps - 1)(lambda: comm.finish(*plan))

    tokens = batch * seq
    bias_spec = (pl.BlockSpec((2, nq, t, t), lambda b, p: (p, 0, 0, 0)) if dilated
                 else pl.BlockSpec((None, 8, LANES), lambda b, p: (0, 0, 0)))
    n = comm.n if comm else 0
    return pl.pallas_call(
        body, name=name, grid=(batch, N_PAIRS),
        out_shape=[jax.ShapeDtypeStruct((tokens, N_PAIRS * width), BF16), jax.ShapeDtypeStruct((tokens, N_PAIRS * width), BF16),
                   jax.ShapeDtypeStruct((tokens, DIL_WIDTH), BF16)] + (comm.out_shape if comm else []),
        in_specs=[pl.BlockSpec((seq, width), lambda b, p: (b, cq + p)),
                  pl.BlockSpec((seq, width), lambda b, p: (b, ck + p)),
                  pl.BlockSpec((seq, LANES), lambda b, p: (b, cv + p)),
                  pl.BlockSpec((seq, LANES), lambda b, p: (b, p)),
                  pl.BlockSpec((seq, LANES), lambda b, p: (b, p)),
                  pl.BlockSpec((None, 8, seq), lambda b, p: (b * N_PAIRS + p, 0, 0)),
                  bias_spec] + [ANY] * n,
        out_specs=[pl.BlockSpec((seq, width), lambda b, p: (b, p)),
                   pl.BlockSpec((seq, width), lambda b, p: (b, p)),
                   pl.BlockSpec((seq, LANES), lambda b, p: (b, p))] + [ANY] * n,
        scratch_shapes=[pltpu.VMEM((width, seq) if dq_transposed else (seq, width), F32),
                        pltpu.VMEM((t, width), F32), pltpu.VMEM((t, LANES), F32),
                        pltpu.VMEM((8, seq), F32), pltpu.VMEM((2, seq, width), BF16), pltpu.VMEM((2, seq, LANES), BF16)]
        + (comm.scratch if comm else []),
        compiler_params=_cp("arbitrary", "arbitrary") if comm else _cp("parallel", "parallel"),
    )(q, k, v, o, do, lse, bias, *comm_arrays)


def _rms(xf, g):
    r = lax.rsqrt(jnp.mean(xf * xf, axis=1, keepdims=True) + RMS_EPS)
    return xf * r * g, r


def _rms_bwd(dy, xf, r, g):
    gy = dy * g
    dx = r * gy - xf * (r * r * r) * jnp.mean(gy * xf, axis=1, keepdims=True)
    return dx, dy * xf * r


def _ln_bwd(dy, xhat, rstd, g):
    dxh = dy * g
    return rstd * (dxh - jnp.mean(dxh, axis=1, keepdims=True) - xhat * jnp.mean(dxh * xhat, axis=1, keepdims=True))


def _rope_slabs(q, cos, sin, transpose):
    first_half = (lax.broadcasted_iota(I32, (1, LANES), 1) % ROPE) < ROPE // 2
    out = []
    for p in range(N_PAIRS):
        blk = q[:, p * PAIR_W + LANES:(p + 1) * PAIR_W]
        y = blk * sin if transpose else blk
        up, down = pltpu.roll(y, LANES - ROPE // 2, 1), pltpu.roll(y, ROPE // 2, 1)
        rot = jnp.where(first_half, up, -down) if transpose else jnp.where(first_half, -up, down) * sin
        out += [q[:, p * PAIR_W:p * PAIR_W + LANES], blk * cos + rot]
    return jnp.concatenate(out, axis=1)


def _fwd_proj(x, w_in_ext, w1, wk, wv, g_q, g_kv, cext, sext, cs128, *, seq):
    tokens = x.shape[0]
    tm = min(TOKEN_TILE, seq)
    ns = seq // tm

    def body(x_ref, win_ref, w1_ref, wk_ref, wv_ref, gq_ref, gkv_ref, c_ref, s_ref, cs_ref,
             low_ref, gates_ref, qkvd_ref, qp_ref, kp_ref, vm_ref, qn_ref, kvn_ref, xb_ref):
        xt = x_ref[...].astype(BF16)
        xb_ref[...] = xt
        low = _dot(xt, win_ref[:, 0:LOW_W])
        low_ref[...] = low
        qkvd_ref[...] = _dot(xt, win_ref[:, LOW_W:LOW_W + 3 * DIL_WIDTH]).astype(BF16)
        gates_ref[...] = _dot(xt, win_ref[:, LOW_W + 3 * DIL_WIDTH:]).astype(BF16)
        qn = _rms(low[:, 0:Q_LORA], gq_ref[...])[0].astype(BF16)
        kvn = _rms(low[:, Q_LORA:Q_LORA + KV_LORA], gkv_ref[...])[0].astype(BF16)
        qn_ref[...] = qn
        kvn_ref[...] = kvn
        qp_ref[...] = _rope_slabs(_dot(qn, w1_ref[...]), c_ref[...], s_ref[...], False).astype(BF16)
        kr = low[:, Q_LORA + KV_LORA:] * cs_ref[...]
        kr = kr + pltpu.roll(kr, LANES - ROPE, 1)
        lane = lax.broadcasted_iota(I32, kr.shape, 1)
        kr = jnp.where(lane < ROPE, kr, 0.0)
        kr = (kr + pltpu.roll(kr, ROPE, 1)).astype(BF16)
        kn = _dot(kvn, wk_ref[...]).astype(BF16)
        kp_ref[...] = jnp.concatenate([blk for p in range(N_PAIRS) for blk in (kn[:, p * LANES:(p + 1) * LANES], kr)], axis=1)
        vm_ref[...] = _dot(kvn, wv_ref[...]).astype(BF16)

    n_gates = 2 * D_MODEL
    outs = [(LOW_W, F32), (n_gates, BF16), (3 * DIL_WIDTH, BF16), (N_PAIRS * PAIR_W, BF16), (N_PAIRS * PAIR_W, BF16),
            (DIL_WIDTH, BF16), (Q_LORA, BF16), (KV_LORA, BF16), (D_MODEL, BF16)]
    return pl.pallas_call(
        body, name="fwd_proj", grid=(tokens // tm,),
        out_shape=tuple(jax.ShapeDtypeStruct((tokens, w), dt) for w, dt in outs),
        in_specs=[_rows(tm, D_MODEL), _full(w_in_ext.shape), _full(w1.shape), _full(wk.shape),
                  _full(wv.shape), _full(g_q.shape), _full(g_kv.shape),
                  pl.BlockSpec((tm, LANES), lambda i: (i % ns, 1)),
                  pl.BlockSpec((tm, LANES), lambda i: (i % ns, 1)),
                  pl.BlockSpec((tm, LANES), lambda i: (i % ns, 0))],
        out_specs=tuple(_rows(tm, w) for w, _ in outs),
        compiler_params=_cp("parallel"),
    )(x, w_in_ext, w1, wk, wv, g_q, g_kv, cext, sext, cs128)


def _fwd_mix(o_a, o_b, gates, x, b_gate, w_oa, w_ob, w_out, ln_g, ln_b, *, seq):
    tokens = x.shape[0]
    tm = min(MIX_TILE, seq)

    def body(oa_ref, ob_ref, gt_ref, x_ref, bg_ref, woa_ref, wob_ref, wout_ref, g_ref, b_ref,
             hb_ref, xhat_ref, rstd_ref, ya_ref, yb_ref, mix_ref):
        ya = _dot(oa_ref[...], woa_ref[...])
        yb = _dot(ob_ref[...], wob_ref[...])
        g0 = _sigmoid(gt_ref[:, 0:D_MODEL].astype(F32) + bg_ref[0:1, :])
        g1 = _sigmoid(gt_ref[:, D_MODEL:].astype(F32) + bg_ref[1:2, :])
        mix = (g0 * ya + g1 * yb).astype(BF16)
        z = ALPHA * x_ref[...] + _dot(mix, wout_ref[...])
        zc = z - jnp.mean(z, axis=1, keepdims=True)
        rstd = lax.rsqrt(jnp.mean(zc * zc, axis=1, keepdims=True) + LN_EPS)
        xhat = zc * rstd
        hb_ref[...] = (xhat * g_ref[...] + b_ref[...]).astype(BF16)
        xhat_ref[...] = xhat
        rstd_ref[...] = jnp.broadcast_to(rstd, (tm, LANES))
        ya_ref[...] = ya.astype(BF16)
        yb_ref[...] = yb.astype(BF16)
        mix_ref[...] = mix

    outs = [(D_MODEL, BF16), (D_MODEL, F32), (LANES, F32), (D_MODEL, BF16), (D_MODEL, BF16), (D_MODEL, BF16)]
    return pl.pallas_call(
        body, name="fwd_mix", grid=(tokens // tm,),
        out_shape=tuple(jax.ShapeDtypeStruct((tokens, w), dt) for w, dt in outs),
        in_specs=[_rows(tm, DIL_WIDTH), _rows(tm, DIL_WIDTH), _rows(tm, 2 * D_MODEL), _rows(tm, D_MODEL),
                  _full(b_gate.shape), _full(w_oa.shape), _full(w_ob.shape), _full(w_out.shape),
                  _full(ln_g.shape), _full(ln_b.shape)],
        out_specs=tuple(_rows(tm, w) for w, _ in outs),
        compiler_params=_cp("parallel"),
    )(o_a, o_b, gates, x, b_gate, w_oa, w_ob, w_out, ln_g, ln_b)


def _fwd_mlp(hb, xhat1, target, w_ff1, w_ff2, ln1_g, ln1_b, ln_g, ln_b, *, seq):
    tokens = hb.shape[0]
    tm = min(2 * TOKEN_TILE, seq)
    tf = FF_SHARD
    nf = N_DEV // FF_STEP

    def body(hb_ref, xh_ref, tg_ref, w1_ref, w2_ref, g1_ref, b1_ref, g_ref, b_ref, u_ref, dz_ref, dzb_ref, stat_ref, acc):
        i, j = pl.program_id(0), pl.program_id(1)

        @pl.when((i == 0) & (j == 0))
        def _():
            stat_ref[...] = jnp.zeros_like(stat_ref)

        @pl.when(j == 0)
        def _():
            acc[...] = jnp.zeros_like(acc)

        acts = []
        for s in range(FF_STEP):
            u = _dot(hb_ref[...], w1_ref[s])
            u_ref[:, s * tf:(s + 1) * tf] = u.astype(BF16)
            acts.append(jnp.square(jnp.maximum(u, 0.0)).astype(BF16))
        acc[...] += _dot(jnp.concatenate(acts, axis=1), w2_ref[...])

        @pl.when(j == nf - 1)
        def _():
            z = ALPHA * (xh_ref[...] * g1_ref[...] + b1_ref[...]) + acc[...]
            zc = z - jnp.mean(z, axis=1, keepdims=True)
            rstd = lax.rsqrt(jnp.mean(zc * zc, axis=1, keepdims=True) + LN_EPS)
            xhat = zc * rstd
            err = xhat * g_ref[...] + b_ref[...] - tg_ref[...]
            dy = err * (1.0 / D_MODEL)
            dz = _ln_bwd(dy, xhat, rstd, g_ref[...])
            dz_ref[...] = dz
            dzb_ref[...] = dz.astype(BF16)
            stat_ref[0:1, :] += jnp.sum(dy * xhat, axis=0, keepdims=True)
            stat_ref[1:2, :] += jnp.sum(dy, axis=0, keepdims=True)
            stat_ref[2:3, :] += jnp.sum(jnp.sum(err * err, axis=1, keepdims=True), axis=0, keepdims=True) * (0.5 / D_MODEL)

    return pl.pallas_call(
        body, name="fwd_mlp", grid=(tokens // tm, nf),
        out_shape=(jax.ShapeDtypeStruct((tokens, D_FF), BF16), jax.ShapeDtypeStruct((tokens, D_MODEL), F32),
                   jax.ShapeDtypeStruct((tokens, D_MODEL), BF16), jax.ShapeDtypeStruct((8, D_MODEL), F32)),
        in_specs=[_rows(tm, D_MODEL), _rows(tm, D_MODEL), _rows(tm, D_MODEL),
                  pl.BlockSpec((FF_STEP, D_MODEL, tf), lambda i, j: (j, 0, 0)),
                  pl.BlockSpec((FF_STEP * tf, D_MODEL), lambda i, j: (j, 0)),
                  _full(ln1_g.shape), _full(ln1_b.shape), _full(ln_g.shape), _full(ln_b.shape)],
        out_specs=(pl.BlockSpec((tm, FF_STEP * tf), lambda i, j: (i, j)), _rows(tm, D_MODEL), _rows(tm, D_MODEL),
                   _full((8, D_MODEL))),
        scratch_shapes=[pltpu.VMEM((tm, D_MODEL), F32)],
        compiler_params=_cp("arbitrary", "arbitrary"),
    )(hb, xhat1, target, w_ff1, w_ff2, ln1_g, ln1_b, ln_g, ln_b)


def _bwd_mlp(dz2, dz2b, u, xhat1, rstd1, w_ff1, w_ff2, ln_g, *, seq):
    tokens = dz2.shape[0]
    tm = min(2 * TOKEN_TILE, seq)
    tf = FF_SHARD
    nf = N_DEV // FF_STEP

    def body(dz_ref, dzb_ref, u_ref, xh_ref, rs_ref, w1_ref, w2_ref, g_ref, du_ref, dz1_ref, dz1b_ref, stat_ref, acc):
        i, j = pl.program_id(0), pl.program_id(1)

        @pl.when((i == 0) & (j == 0))
        def _():
            stat_ref[...] = jnp.zeros_like(stat_ref)

        @pl.when(j == 0)
        def _():
            acc[...] = jnp.zeros_like(acc)

        da = _dot_nt(dzb_ref[...], w2_ref[...])
        du = (da * (2.0 * jnp.maximum(u_ref[...].astype(F32), 0.0))).astype(BF16)
        du_ref[...] = du
        part = _dot_nt(du[:, 0:tf], w1_ref[0])
        for s in range(1, FF_STEP):
            part = part + _dot_nt(du[:, s * tf:(s + 1) * tf], w1_ref[s])
        acc[...] += part

        @pl.when(j == nf - 1)
        def _():
            dh = ALPHA * dz_ref[...] + acc[...]
            xhat = xh_ref[...]
            dz1 = _ln_bwd(dh, xhat, rs_ref[:, 0:1], g_ref[...])
            dz1_ref[...] = dz1
            dz1b_ref[...] = dz1.astype(BF16)
            stat_ref[0:1, :] += jnp.sum(dh * xhat, axis=0, keepdims=True)
            stat_ref[1:2, :] += jnp.sum(dh, axis=0, keepdims=True)

    return pl.pallas_call(
        body, name="bwd_mlp", grid=(tokens // tm, nf),
        out_shape=(jax.ShapeDtypeStruct((tokens, D_FF), BF16), jax.ShapeDtypeStruct((tokens, D_MODEL), F32),
                   jax.ShapeDtypeStruct((tokens, D_MODEL), BF16), jax.ShapeDtypeStruct((8, D_MODEL), F32)),
        in_specs=[_rows(tm, D_MODEL), _rows(tm, D_MODEL), pl.BlockSpec((tm, FF_STEP * tf), lambda i, j: (i, j)),
                  _rows(tm, D_MODEL), _rows(tm, LANES),
                  pl.BlockSpec((FF_STEP, D_MODEL, tf), lambda i, j: (j, 0, 0)),
                  pl.BlockSpec((FF_STEP * tf, D_MODEL), lambda i, j: (j, 0)),
                  _full(ln_g.shape)],
        out_specs=(pl.BlockSpec((tm, FF_STEP * tf), lambda i, j: (i, j)), _rows(tm, D_MODEL), _rows(tm, D_MODEL),
                   _full((8, D_MODEL))),
        scratch_shapes=[pltpu.VMEM((tm, D_MODEL), F32)],
        compiler_params=_cp("arbitrary", "arbitrary"),
    )(dz2, dz2b, u, xhat1, rstd1, w_ff1, w_ff2, ln_g)


def _bwd_mix(dz1b, gates, y_a, y_b, b_gate, w_oa, w_ob, w_out, *, seq):
    tokens = dz1b.shape[0]
    tm = min(MIX_TILE, seq)

    def body(dz_ref, gt_ref, ya_ref, yb_ref, bg_ref, woa_ref, wob_ref, wout_ref,
             dgt_ref, dya_ref, dyb_ref, doa_ref, dob_ref, stat_ref):
        @pl.when(pl.program_id(0) == 0)
        def _():
            stat_ref[...] = jnp.zeros_like(stat_ref)

        dmix = _dot_nt(dz_ref[...], wout_ref[...])
        for k, (y_ref, w_ref, dy_ref, do_ref) in enumerate(((ya_ref, woa_ref, dya_ref, doa_ref), (yb_ref, wob_ref, dyb_ref, dob_ref))):
            g = _sigmoid(gt_ref[:, k * D_MODEL:(k + 1) * D_MODEL].astype(F32) + bg_ref[k:k + 1, :])
            dgate = dmix * y_ref[...].astype(F32) * g * (1.0 - g)
            dgt_ref[:, k * D_MODEL:(k + 1) * D_MODEL] = dgate.astype(BF16)
            stat_ref[k:k + 1, :] += jnp.sum(dgate, axis=0, keepdims=True)
            dy = (dmix * g).astype(BF16)
            dy_ref[...] = dy
            do_ref[...] = _dot_nt(dy, w_ref[...]).astype(BF16)

    outs = [(2 * D_MODEL, BF16), (D_MODEL, BF16), (D_MODEL, BF16), (DIL_WIDTH, BF16), (DIL_WIDTH, BF16)]
    return pl.pallas_call(
        body, name="bwd_mix", grid=(tokens // tm,),
        out_shape=tuple(jax.ShapeDtypeStruct((tokens, w), dt) for w, dt in outs) + (jax.ShapeDtypeStruct((8, D_MODEL), F32),),
        in_specs=[_rows(tm, D_MODEL), _rows(tm, 2 * D_MODEL), _rows(tm, D_MODEL), _rows(tm, D_MODEL),
                  _full(b_gate.shape), _full(w_oa.shape), _full(w_ob.shape), _full(w_out.shape)],
        out_specs=tuple(_rows(tm, w) for w, _ in outs) + (_full((8, D_MODEL)),),
        compiler_params=_cp("arbitrary"),
    )(dz1b, gates, y_a, y_b, b_gate, w_oa, w_ob, w_out)


def _bwd_proj(dqp, dkp, dvm, dq_d, dk_d, dv_d, dgates, dz1, low, w_in_ext, w1, wk, wv, g_q, g_kv, cext, sext, cs128, *, seq):
    tokens = dz1.shape[0]
    tm = min(TOKEN_TILE, seq)
    ns = seq // tm

    def body(dqp_ref, dkp_ref, dvm_ref, dqd_ref, dkd_ref, dvd_ref, dgt_ref, dz_ref, low_ref, win_ref, w1_ref, wk_ref,
             wv_ref, gq_ref, gkv_ref, c_ref, s_ref, cs_ref, dx_ref, dproj_ref, da_ref, dkn_ref, stat_ref):
        @pl.when(pl.program_id(0) == 0)
        def _():
            stat_ref[...] = jnp.zeros_like(stat_ref)

        low = low_ref[...]
        d_a = _rope_slabs(dqp_ref[...].astype(F32), c_ref[...], s_ref[...], True).astype(BF16)
        da_ref[...] = d_a
        q_a = low[:, 0:Q_LORA]
        _, rq = _rms(q_a, gq_ref[...])
        dq_a, gq_terms = _rms_bwd(_dot_nt(d_a, w1_ref[...]), q_a, rq, gq_ref[...])
        kv_a = low[:, Q_LORA:Q_LORA + KV_LORA]
        _, rkv = _rms(kv_a, gkv_ref[...])
        dkn = jnp.concatenate([dkp_ref[:, p * PAIR_W:p * PAIR_W + LANES] for p in range(N_PAIRS)], axis=1)
        dkn_ref[...] = dkn
        dkv_a, gkv_terms = _rms_bwd(_dot_nt(dkn, wk_ref[...]) + _dot_nt(dvm_ref[...], wv_ref[...]), kv_a, rkv, gkv_ref[...])
        dkr = sum(dkp_ref[:, p * PAIR_W + LANES:(p + 1) * PAIR_W].astype(F32) for p in range(N_PAIRS))
        dkr = dkr + pltpu.roll(dkr, LANES - ROPE, 1)
        dkr = jnp.where(lax.broadcasted_iota(I32, dkr.shape, 1) < ROPE, dkr, 0.0)
        dkr = (dkr + pltpu.roll(dkr, ROPE, 1)) * cs_ref[...]
        stat_ref[0:1, 0:Q_LORA] += jnp.sum(gq_terms, axis=0, keepdims=True)
        stat_ref[1:2, 0:KV_LORA] += jnp.sum(gkv_terms, axis=0, keepdims=True)
        dproj_ref[:, 0:Q_LORA] = dq_a.astype(BF16)
        dproj_ref[:, Q_LORA:Q_LORA + KV_LORA] = dkv_a.astype(BF16)
        dproj_ref[:, Q_LORA + KV_LORA:LOW_W] = dkr.astype(BF16)
        dproj_ref[:, LOW_W:LOW_W + DIL_WIDTH] = dqd_ref[...]
        dproj_ref[:, LOW_W + DIL_WIDTH:LOW_W + 2 * DIL_WIDTH] = dkd_ref[...]
        dproj_ref[:, LOW_W + 2 * DIL_WIDTH:LOW_W + 3 * DIL_WIDTH] = dvd_ref[...]
        dproj_ref[:, LOW_W + 3 * DIL_WIDTH:] = dgt_ref[...]
        dx_ref[...] = ALPHA * dz_ref[...] + _dot_nt(dproj_ref[...], win_ref[...])

    wide = N_PAIRS * PAIR_W
    return pl.pallas_call(
        body, name="bwd_proj", grid=(tokens // tm,),
        out_shape=(jax.ShapeDtypeStruct((tokens, D_MODEL), F32), jax.ShapeDtypeStruct((tokens, IN_EXT), BF16),
                   jax.ShapeDtypeStruct((tokens, wide), BF16), jax.ShapeDtypeStruct((tokens, N_HEADS * NOPE), BF16),
                   jax.ShapeDtypeStruct((8, D_MODEL), F32)),
        in_specs=[_rows(tm, wide), _rows(tm, wide), _rows(tm, DIL_WIDTH), _rows(tm, DIL_WIDTH), _rows(tm, DIL_WIDTH),
                  _rows(tm, DIL_WIDTH), _rows(tm, 2 * D_MODEL),
                  _rows(tm, D_MODEL), _rows(tm, LOW_W), _full(w_in_ext.shape), _full(w1.shape),
                  _full(wk.shape), _full(wv.shape), _full(g_q.shape), _full(g_kv.shape),
                  pl.BlockSpec((tm, LANES), lambda i: (i % ns, 1)), pl.BlockSpec((tm, LANES), lambda i: (i % ns, 1)),
                  pl.BlockSpec((tm, LANES), lambda i: (i % ns, 0))],
        out_specs=(_rows(tm, D_MODEL), _rows(tm, IN_EXT), _rows(tm, wide), _rows(tm, N_HEADS * NOPE), _full((8, D_MODEL))),
        compiler_params=_cp("arbitrary"),
    )(dqp, dkp, dvm, dq_d, dk_d, dv_d, dgates, dz1, low, w_in_ext, w1, wk, wv, g_q, g_kv, cext, sext, cs128)


def _wgrad(a, b, name, square_relu=False, by_shard=False):
    tokens, ka = a.shape
    n = b.shape[1]
    tka = min(ka, 512)
    shard = n // N_DEV
    tn = WGRAD_SHARDS * shard if by_shard else max(w for w in range(LANES, min(n, 2304) + 1, LANES) if n % w == 0)
    tt = min(tokens, 1024)
    nt = tokens // tt

    def body(a_ref, b_ref, o_ref, acc):
        kt = pl.program_id(2)

        @pl.when(kt == 0)
        def _():
            acc[...] = jnp.zeros_like(acc)

        at = a_ref[...]
        if square_relu:
            at = jnp.square(jnp.maximum(at.astype(F32), 0.0)).astype(BF16)
        acc[...] += _dot_tn(at, b_ref[...])

        @pl.when(kt == nt - 1)
        def _():
            if by_shard:
                for s in range(WGRAD_SHARDS):
                    o_ref[s] = acc[:, s * shard:(s + 1) * shard].astype(BF16)
            else:
                o_ref[...] = acc[...].astype(BF16)

    if by_shard:
        out_shape, out_spec = (N_DEV, ka, shard), pl.BlockSpec((WGRAD_SHARDS, tka, shard), lambda i, j, k: (j, i, 0))
    else:
        out_shape, out_spec = (ka, n), pl.BlockSpec((tka, tn), lambda i, j, k: (i, j))
    return pl.pallas_call(
        body, name=name, grid=(ka // tka, n // tn, nt), out_shape=jax.ShapeDtypeStruct(out_shape, BF16),
        in_specs=[pl.BlockSpec((tt, tka), lambda i, j, k: (k, i)), pl.BlockSpec((tt, tn), lambda i, j, k: (k, j))],
        out_specs=out_spec,
        scratch_shapes=[pltpu.VMEM((tka, tn), F32)],
        compiler_params=_cp("parallel", "parallel", "arbitrary"),
    )(a, b)


def _adam_math(w, g, m, v):
    m = ADAM_B1 * m + (1.0 - ADAM_B1) * g
    v = ADAM_B2 * v + (1.0 - ADAM_B2) * jnp.square(g)
    m_hat = m / (1.0 - ADAM_B1 ** ADAM_STEP)
    v_hat = v / (1.0 - ADAM_B2 ** ADAM_STEP)
    return -ADAM_LR * (m_hat / (jnp.sqrt(v_hat) + ADAM_EPS) + ADAM_WD * w), m, v


def _adamw(items, name, steps=None, comm=None, comm_arrays=()):
    if steps is None:
        steps = items[0][0].shape[0] // _row_tile(items[0][0].shape[0])
    n_items = len(items)

    def body(slot_ref, *refs):
        ins, outs, _, plan = _comm_hooks(comm, refs, 5 * n_items, 4 * n_items)
        if plan:
            pl.when(pl.program_id(0) == 0)(lambda: comm.start(*plan))
            pl.when(pl.program_id(0) == steps // 2)(lambda: comm.forward(*plan))
        for k, (_, _, _, _, parts) in enumerate(items):
            w_ref, m_ref, v_ref, own_ref, p_ref = ins[5 * k:5 * k + 5]
            g_ref, d_ref, nm_ref, nv_ref = outs[4 * k:4 * k + 4]
            g = own_ref[...].astype(F32)
            for d in range(parts.shape[0]):
                g = g + p_ref[d].astype(F32)
            g_ref[...] = g
            d_ref[...], nm_ref[...], nv_ref[...] = _adam_math(w_ref[...], g, m_ref[...], v_ref[...])
        if plan:
            pl.when(pl.program_id(0) == steps - 1)(lambda: comm.finish(*plan))

    x, y, c = _place()
    in_specs, out_specs, out_shape, args = [], [], [], []
    for w, m, v, own, parts in items:
        rows, cols = w.shape
        tr = rows // steps
        blk = pl.BlockSpec((tr, cols), lambda i, slot: (i, 0))
        own_blk = blk if own.ndim == 2 else pl.BlockSpec((None, tr, cols), lambda i, slot: (slot[0], i, 0))
        in_specs += [blk, blk, blk, own_blk, pl.BlockSpec((parts.shape[0], tr, cols), lambda i, slot: (0, i, 0))]
        out_specs += [blk] * 4
        out_shape += [jax.ShapeDtypeStruct((rows, cols), F32)] * 4
        args += [w, m, v, own, parts]
    n = comm.n if comm else 0
    out = pl.pallas_call(
        body, name=name,
        grid_spec=pltpu.PrefetchScalarGridSpec(
            num_scalar_prefetch=1, grid=(steps,), in_specs=in_specs + [ANY] * n, out_specs=out_specs + [ANY] * n,
            scratch_shapes=comm.scratch if comm else []),
        out_shape=out_shape + (comm.out_shape if comm else []),
        compiler_params=_cp("arbitrary") if comm else _cp("parallel"),
    )(jnp.reshape(4 * x + 2 * y + c, (1,)).astype(I32), *args, *comm_arrays)
    return [tuple(out[4 * k:4 * k + 4]) for k in range(n_items)] + list(out[4 * n_items:])


def _adamw_small(parts, w, m, v):
    _, rows, cols = parts.shape

    def body(p_ref, w_ref, m_ref, v_ref, g_ref, d_ref, nm_ref, nv_ref):
        g = p_ref[0]
        for d in range(1, N_DEV):
            g = g + p_ref[d]
        g_ref[...] = g
        d_ref[...], nm_ref[...], nv_ref[...] = _adam_math(w_ref[...], g, m_ref[...], v_ref[...])

    return pl.pallas_call(
        body, name="adamw_replicated", out_shape=(jax.ShapeDtypeStruct((rows, cols), F32),) * 4,
        in_specs=[_full(parts.shape)] + [_full((rows, cols))] * 3, out_specs=(_full((rows, cols)),) * 4, grid=(1,),
        compiler_params=_cp("arbitrary"),
    )(parts, w, m, v)


def _pad_rows(a2d, mult):
    pad = (-a2d.shape[-2]) % mult
    return jnp.pad(a2d, [(0, 0)] * (a2d.ndim - 2) + [(0, pad), (0, 0)]) if pad else a2d


def _pad_cols(a):
    pad = (-a.shape[-1]) % LANES
    return jnp.pad(a, [(0, 0)] * (a.ndim - 1) + [(0, pad)]) if pad else a


def _rot_cols(w):
    half = ROPE // 2
    return jnp.concatenate([-w[..., half:], w[..., :half]], axis=-1)


def _unrot_cols(dw):
    half = ROPE // 2
    return jnp.concatenate([dw[..., half:], -dw[..., :half]], axis=-1)


def _from_col_shards(stacked):
    return stacked.transpose(1, 0, 2).reshape(stacked.shape[1], -1)


def _to_col_shards(full):
    r = full.shape[0]
    return full.reshape(r, N_DEV, -1).transpose(1, 0, 2)


def _rope_tables(seq):
    half = ROPE // 2
    inv = jnp.power(ROPE_THETA, -jnp.arange(half, dtype=F32) / half)
    ang = jnp.arange(seq, dtype=F32)[:, None] * inv[None, :]
    cos = jnp.concatenate([jnp.cos(ang)] * 2, axis=1)
    sin = jnp.concatenate([jnp.sin(ang)] * 2, axis=1)
    ones, zeros = jnp.ones((seq, 2 * NOPE), F32), jnp.zeros((seq, 2 * NOPE), F32)
    pad = jnp.zeros((seq, PAIR_W - 2 * NOPE - 2 * ROPE), F32)
    cext = jnp.concatenate([ones, cos, cos, pad], axis=1)
    sext = jnp.concatenate([zeros, sin, sin, pad], axis=1)
    cs128 = jnp.concatenate([cos, sin, jnp.zeros((seq, LANES - 2 * ROPE), F32)], axis=1)
    return cext, sext, cs128


def _pair_slabs(nope, rope):
    k = nope.shape[0]
    nope = nope.reshape(k, N_PAIRS, 2 * NOPE)
    rope = jnp.zeros((k, N_PAIRS, 2 * ROPE), nope.dtype) if rope is None else rope.reshape(k, N_PAIRS, 2 * ROPE)
    pad = jnp.zeros((k, N_PAIRS, PAIR_W - 2 * NOPE - 2 * ROPE), nope.dtype)
    return jnp.concatenate([nope, rope, pad], axis=2).reshape(k, N_PAIRS * PAIR_W)


def _split_slabs(slabs):
    k = slabs.shape[0]
    s = slabs.reshape(k, N_PAIRS, PAIR_W)
    return s[:, :, :2 * NOPE].reshape(k, N_HEADS, NOPE), s[:, :, 2 * NOPE:2 * NOPE + 2 * ROPE].reshape(k, N_HEADS, ROPE)


def kernel(x, w_in, b_gate, g_q_a, w_uq, g_kv_a, w_ukv, w_o_mla, w_o_dil, w_out, ln1_g, ln1_b, w_ff1, w_ff2, ln2_g, ln2_b, loss_target, m_w_in, m_b_gate, m_g_q_a, m_w_uq, m_g_kv_a, m_w_ukv, m_w_o_mla, m_w_o_dil, m_w_out, m_ln1_g, m_ln1_b, m_w_ff1, m_w_ff2, m_ln2_g, m_ln2_b, v_w_in, v_b_gate, v_g_q_a, v_w_uq, v_g_kv_a, v_w_ukv, v_w_o_mla, v_w_o_dil, v_w_out, v_ln1_g, v_ln1_b, v_w_ff1, v_w_ff2, v_ln2_g, v_ln2_b):
    batch, seq, _ = x.shape
    tokens = batch * seq
    weights = dict(w_in=w_in, w_uq=w_uq, w_ukv=w_ukv, w_o_mla=w_o_mla, w_o_dil=w_o_dil, w_out=w_out, w_ff1=w_ff1, w_ff2=w_ff2, b_gate=b_gate)
    mom_m = dict(w_in=m_w_in, w_uq=m_w_uq, w_ukv=m_w_ukv, w_o_mla=m_w_o_mla, w_o_dil=m_w_o_dil, w_out=m_w_out, w_ff1=m_w_ff1, w_ff2=m_w_ff2, b_gate=m_b_gate)
    mom_v = dict(w_in=v_w_in, w_uq=v_w_uq, w_ukv=v_w_ukv, w_o_mla=v_w_o_mla, w_o_dil=v_w_o_dil, w_out=v_w_out, w_ff1=v_w_ff1, w_ff2=v_w_ff2, b_gate=v_b_gate)

    first = ["w_in", "w_uq", "w_ukv"]
    widths = [weights[n].shape[2] for n in first]
    shards = [_pad_cols(weights[n][0].astype(BF16)) for n in first]
    g_in, g_uq, g_ukv = _run_comm(_Gather(shards), shards, "all_gather_first_weights")
    g_uq, g_ukv = g_uq[:, :, :widths[1]], g_ukv[:, :, :widths[2]]

    s1, s2, n_in = Q_LORA + KV_LORA, Q_LORA + KV_LORA + ROPE, N_DEV * widths[0]

    def w_in_cols(lo, hi):
        out = []
        while lo < hi:
            d, off = divmod(lo, widths[0])
            take = min(hi - lo, widths[0] - off)
            out.append(g_in[d][:, off:off + take])
            lo += take
        return out

    w_in_ext = jnp.concatenate(w_in_cols(0, s2) + [_rot_cols(jnp.concatenate(w_in_cols(s1, s2), axis=1)),
                                                   jnp.zeros((D_MODEL, LOW_W - s2 - ROPE), BF16)] + w_in_cols(s2, n_in), axis=1)
    uq = _from_col_shards(g_uq).reshape(Q_LORA, N_HEADS, NOPE + ROPE)
    w1 = _pair_slabs(uq[:, :, :NOPE], uq[:, :, NOPE:])
    ukv = _from_col_shards(g_ukv).reshape(KV_LORA, N_HEADS, NOPE + HEAD_V)
    wk = ukv[:, :, :NOPE].reshape(KV_LORA, N_HEADS * NOPE)
    wv = ukv[:, :, NOPE:].reshape(KV_LORA, N_HEADS * HEAD_V)
    cext, sext, cs128 = _rope_tables(seq)
    dil_bias = _dilated_bias_table(seq)
    no_bias = jnp.zeros((1, 8, LANES), F32)

    x2 = x.reshape(tokens, D_MODEL)
    low, gates, qkvd, qp, kp, vm, qn, kvn, xb = _fwd_proj(x2, w_in_ext, w1, wk, wv, g_q_a, g_kv_a, cext, sext, cs128, seq=seq)
    bg = b_gate[0]
    bg_hi = bg.astype(BF16)
    bg_lo = (bg - bg_hi.astype(F32)).astype(BF16)
    later = [weights[n][0].astype(BF16) for n in ("w_o_mla", "w_o_dil", "w_out", "w_ff1", "w_ff2")]
    later.append(_pad_rows(jnp.concatenate([bg_hi, bg_lo], axis=0), 16))
    mla = dict(batch=batch, seq=seq, width=PAIR_W, col0=(0, 0, 0), dilated=False, scale=MLA_SCALE)
    dil = dict(batch=batch, seq=seq, width=LANES, col0=(0, N_PAIRS, 2 * N_PAIRS), dilated=True, scale=DIL_SCALE)
    o_a, lse_a, g_oa, g_ob, g_out, g_ff1, g_ff2, g_bg = _attn_fwd(
        qp, kp, vm, no_bias, name="mla_attention_fwd", comm=_Gather(later), comm_arrays=later, **mla)
    o_b, lse_b = _attn_fwd(qkvd, qkvd, qkvd, dil_bias, name="dilated_attention_fwd", **dil)
    w_oa, w_ob = _from_col_shards(g_oa), _from_col_shards(g_ob)
    w_out_full = g_out.reshape(D_MODEL, D_MODEL)
    w_ff2_full = g_ff2.reshape(D_FF, D_MODEL)
    bg_parts = g_bg.astype(F32)
    b_gate_full = _from_col_shards(bg_parts[:, 0:2] + bg_parts[:, 2:4])
    hb, xhat1, rstd1, y_a, y_b, mix = _fwd_mix(o_a, o_b, gates, x2, b_gate_full, w_oa, w_ob, w_out_full, ln1_g, ln1_b, seq=seq)
    u, dz2, dz2b, stat2 = _fwd_mlp(hb, xhat1, loss_target.reshape(tokens, D_MODEL), g_ff1, w_ff2_full, ln1_g, ln1_b, ln2_g, ln2_b, seq=seq)

    du, dz1, dz1b, stat1 = _bwd_mlp(dz2, dz2b, u, xhat1, rstd1, g_ff1, w_ff2_full, ln1_g, seq=seq)
    dw_ff = [_wgrad(hb, du, "wgrad_ff1", by_shard=True),
             _wgrad(u, dz2b, "wgrad_ff2", square_relu=True).reshape(N_DEV, FF_SHARD, D_MODEL)]
    dgates, dy_a, dy_b, do_a, do_b, stat_g = _bwd_mix(dz1b, gates, y_a, y_b, b_gate_full, w_oa, w_ob, w_out_full, seq=seq)
    dqp, dkp, dvm, r_ff1, r_ff2 = _attn_bwd(qp, kp, vm, o_a, do_a, lse_a, no_bias, name="mla_attention_bwd",
                                            comm=_Scatter(dw_ff), comm_arrays=dw_ff, **mla)
    dw_mid = [_to_col_shards(_wgrad(o_a, dy_a, "wgrad_o_mla")), _to_col_shards(_wgrad(o_b, dy_b, "wgrad_o_dil")),
              _wgrad(mix, dz1b, "wgrad_out").reshape(N_DEV, D_MODEL // N_DEV, D_MODEL),
              _pad_rows(_to_col_shards(stat_g[0:2]).astype(BF16), 16)]
    dq_d, dk_d, dv_d, r_oa, r_ob, r_out, r_bg = _attn_bwd(qkvd, qkvd, qkvd, o_b, do_b, lse_b, dil_bias, name="dilated_attention_bwd",
                                                          comm=_Scatter(dw_mid), comm_arrays=dw_mid, **dil)
    grad_x, dproj, d_a, dkn, stat_r = _bwd_proj(dqp, dkp, dvm, dq_d, dk_d, dv_d, dgates, dz1, low, w_in_ext, w1, wk, wv,
                                                g_q_a, g_kv_a, cext, sext, cs128, seq=seq)

    dw_in_ext = _wgrad(xb, dproj, "wgrad_in")
    dw1 = _wgrad(qn, d_a, "wgrad_uq")
    dwk = _wgrad(kvn, dkn, "wgrad_ukv_k")
    dwv = _wgrad(kvn, dvm, "wgrad_ukv_v")
    dw_kr = dw_in_ext[:, s1:s2] + _unrot_cols(dw_in_ext[:, s2:s2 + ROPE])

    def dw_in_cols(lo, hi):
        out = []
        for a, b, piece in ((0, s1, lambda u, v: dw_in_ext[:, u:v]), (s1, s2, lambda u, v: dw_kr[:, u - s1:v - s1]),
                            (s2, n_in, lambda u, v: dw_in_ext[:, u + LOW_W - s2:v + LOW_W - s2])):
            if max(lo, a) < min(hi, b):
                out.append(piece(max(lo, a), min(hi, b)))
        return out

    dw_in = jnp.stack([_pad_cols(jnp.concatenate(dw_in_cols(d * widths[0], (d + 1) * widths[0]), axis=1)) for d in range(N_DEV)])
    n1, r1 = _split_slabs(dw1)
    dw_uq = jnp.concatenate([n1, r1], axis=2).reshape(Q_LORA, N_HEADS * (NOPE + ROPE))
    dw_ukv = jnp.concatenate([dwk.reshape(KV_LORA, N_HEADS, NOPE), dwv.reshape(KV_LORA, N_HEADS, HEAD_V)], axis=2).reshape(KV_LORA, N_HEADS * (NOPE + HEAD_V))
    last = [dw_in] + [_pad_cols(_to_col_shards(dw)) for dw in (dw_uq, dw_ukv)]
    theirs = _rs_sibling(last, "rs_last_sibling_exchange")
    sums = [_pair_sum(a, b, "rs_last_pair_sum_" + n) for a, b, n in zip(last, theirs, first)]
    partial = jnp.concatenate([stat_r[0:1, :Q_LORA], stat_r[1:2, :KV_LORA], stat1[0:1], stat1[1:2], stat2[0:1], stat2[1:2],
                               stat2[2:3, :LANES]], axis=1)
    partial = _pad_rows(partial.reshape(-1, LANES), 8)
    rest = [s[1] for s in sums]
    last_exchange = _Plans([_ChipExchange(rest), _Gather([partial])])

    upd = {}
    early = ["w_ff1", "w_ff2", "w_out", "w_o_mla", "w_o_dil"]
    items = [(weights[n][0], mom_m[n][0], mom_v[n][0], own, parts) for n, own, parts in
             zip(early, (dw_ff[0], dw_ff[1], dw_mid[2], dw_mid[0], dw_mid[1]), (r_ff1, r_ff2, r_out, r_oa, r_ob))]
    *done, got_in, got_uq, got_ukv, every = _adamw(items, "adamw_early_weights", steps=4, comm=last_exchange,
                                                   comm_arrays=rest + [partial])
    upd.update(zip(early, done))
    for n, w, (own, _), parts in zip(first, widths, sums, (got_in, got_uq, got_ukv)):
        (upd[n],) = _adamw([(weights[n][0], mom_m[n][0], mom_v[n][0], own[:, :w], parts[:, :, :w])], "adamw_" + n)
    (bg_upd,) = _adamw([(_pad_rows(b_gate[0], 16), _pad_rows(m_b_gate[0], 16), _pad_rows(v_b_gate[0], 16), dw_mid[3], r_bg)],
                       "adamw_b_gate")
    upd["b_gate"] = tuple(t[0:2] for t in bg_upd)

    small_w = [g_q_a, g_kv_a, ln1_g, ln1_b, ln2_g, ln2_b]
    small_m = [m_g_q_a, m_g_kv_a, m_ln1_g, m_ln1_b, m_ln2_g, m_ln2_b]
    small_v = [v_g_q_a, v_g_kv_a, v_ln1_g, v_ln1_b, v_ln2_g, v_ln2_b]
    small_widths = [a.shape[1] for a in small_w]

    def as_rows(vecs, extra):
        flat = jnp.concatenate(vecs + [jnp.zeros((1, extra), F32)], axis=1)
        return _pad_rows(flat.reshape(-1, LANES), 8)

    g_s, d_s, nm_s, nv_s = _adamw_small(every, as_rows(small_w, LANES), as_rows(small_m, LANES), as_rows(small_v, LANES))

    def split_small(a):
        flat = a.reshape(1, -1)
        out, c0 = [], 0
        for w in small_widths:
            out.append(flat[:, c0:c0 + w])
            c0 += w
        return out, flat[0, c0]

    g_small, loss = split_small(g_s)
    small = [g_small, split_small(d_s)[0], split_small(nm_s)[0], split_small(nv_s)[0]]

    order = ["w_in", "b_gate", "g_q_a", "w_uq", "g_kv_a", "w_ukv", "w_o_mla", "w_o_dil", "w_out", "ln1_g", "ln1_b", "w_ff1", "w_ff2", "ln2_g", "ln2_b"]
    small_names = ["g_q_a", "g_kv_a", "ln1_g", "ln1_b", "ln2_g", "ln2_b"]

    def pick(kind):
        return [small[kind][small_names.index(n)] if n in small_names else upd[n][kind][None] for n in order]

    return (loss, grad_x.reshape(batch, seq, D_MODEL), *pick(0), *pick(1), *pick(2), *pick(3))
```

```python
import functools
import math

import jax
import jax.numpy as jnp
from jax import lax
from jax.experimental import pallas as pl
from jax.experimental.pallas import tpu as pltpu

F32 = jnp.float32
BF16 = jnp.bfloat16
I32 = jnp.int32

D_MODEL = 1024
N_HEADS = 8
NOPE = 64
ROPE = 32
HEAD_V = 64
Q_LORA = 384
KV_LORA = 256
DIL_WIDTH = 512
D_FF = 4096
ROPE_THETA = 10000.0
LN_EPS = 1e-5
RMS_EPS = 1e-6
NEG = -1e30
ALPHA = 2.0 ** 0.25
MLA_SCALE = (NOPE + ROPE) ** -0.5
DIL_SCALE = 64 ** -0.5
ADAM_LR, ADAM_B1, ADAM_B2, ADAM_EPS, ADAM_WD, ADAM_STEP = 0.001, 0.9, 0.999, 1e-08, 0.01, 10

LANES = 128
PAIR_W = 256
N_PAIRS = N_HEADS // 2
LOW_W = 768
IN_EXT = LOW_W + 3 * DIL_WIDTH + 2 * D_MODEL
N_DEV = 8
FF_SHARD = D_FF // N_DEV
FF_STEP = 4
WGRAD_SHARDS = 4
TOKEN_TILE = 256
MIX_TILE = 512
ATTN_TILE = 256
VMEM_LIMIT = 56 << 20

MESH = pl.DeviceIdType.MESH
ANY = pl.BlockSpec(memory_space=pl.ANY)
CHIP_FLIPS = ((0, 0), (0, 1), (1, 0), (1, 1))
PEER_FLIPS = tuple((fx, fy, fc) for fx in (0, 1) for fy in (0, 1) for fc in (0, 1))[1:]


def _cp(*sem):
    return pltpu.CompilerParams(dimension_semantics=sem or None, vmem_limit_bytes=VMEM_LIMIT)


def _full(shape):
    nd = len(shape)
    return pl.BlockSpec(shape, lambda *_: (0,) * nd)


def _rows(tm, width):
    return pl.BlockSpec((tm, width), lambda i, *_: (i, 0))


def _dot(a, b):
    return jnp.dot(a, b, preferred_element_type=F32)


def _dot_nt(a, b):
    return lax.dot_general(a, b, (((1,), (1,)), ((), ())), preferred_element_type=F32)


def _dot_tn(a, b):
    return lax.dot_general(a, b, (((0,), (0,)), ((), ())), preferred_element_type=F32)


def _sigmoid(z):
    return 1.0 / (1.0 + jnp.exp(-z))


def _place():
    return lax.axis_index("x"), lax.axis_index("y"), lax.axis_index("c")


def _flip(v, f):
    return 1 - v if f else v


class _Gather:
    def __init__(self, shards):
        self.n = len(shards)
        self.out_shape = [jax.ShapeDtypeStruct((N_DEV, *s.shape), s.dtype) for s in shards]
        self.scratch = [pltpu.SemaphoreType.DMA((7 * self.n,)), pltpu.SemaphoreType.DMA((7 * self.n,)),
                        pltpu.SemaphoreType.DMA((self.n,))]

    def _copies(self, what, srcs, dsts, send, recv, local):
        x, y, c = _place()
        chips = [(_flip(x, fx), _flip(y, fy)) for fx, fy in CHIP_FLIPS[1:]]
        out = []
        for a in range(self.n):
            def slot(px, py, pc, a=a):
                return dsts[a].at[4 * px + 2 * py + pc]

            def copy(k, block, to, src=None, a=a, slot=slot):
                return pltpu.make_async_remote_copy(
                    src_ref=slot(*block) if src is None else src, dst_ref=slot(*block),
                    send_sem=send.at[7 * a + k], recv_sem=recv.at[7 * a + k], device_id=to, device_id_type=MESH)

            if what == "mine":
                out.append(pltpu.make_async_copy(srcs[a], slot(x, y, c), local.at[a]))
            elif what == "first":
                out.append(copy(0, (x, y, c), (x, y, 1 - c), src=srcs[a]))
                out += [copy(1 + j, (x, y, c), (*chip, c), src=srcs[a]) for j, chip in enumerate(chips)]
            elif what == "landed":
                out += [copy(1 + j, (*chip, c), (x, y, c)) for j, chip in enumerate(chips)]
            elif what == "passed":
                out += [copy(4 + j, (*chip, c), (x, y, 1 - c)) for j, chip in enumerate(chips)]
            else:
                out.append(copy(0, (x, y, 1 - c), (x, y, c)))
                out += [copy(4 + j, (*chip, 1 - c), (x, y, c)) for j, chip in enumerate(chips)]
        return out

    def start(self, *refs):
        for cp in self._copies("first", *refs) + self._copies("mine", *refs):
            cp.start()

    def forward(self, *refs):
        for landed, passed in zip(self._copies("landed", *refs), self._copies("passed", *refs)):
            landed.wait_recv()
            passed.start()

    def finish(self, *refs):
        for cp in self._copies("from_sibling", *refs):
            cp.wait_recv()
        for cp in self._copies("first", *refs) + self._copies("passed", *refs):
            cp.wait_send()
        for cp in self._copies("mine", *refs):
            cp.wait()


class _Scatter:
    def __init__(self, arrays):
        self.n = len(arrays)
        self.out_shape = [jax.ShapeDtypeStruct((7, *a.shape[1:]), a.dtype) for a in arrays]
        self.scratch = [pltpu.SemaphoreType.DMA((7 * self.n,)), pltpu.SemaphoreType.DMA((7 * self.n,))]

    def _copies(self, srcs, dsts, send, recv):
        x, y, c = _place()
        out = []
        for a in range(self.n):
            for k, (fx, fy, fc) in enumerate(PEER_FLIPS):
                px, py, pc = _flip(x, fx), _flip(y, fy), _flip(c, fc)
                out.append(pltpu.make_async_remote_copy(
                    src_ref=srcs[a].at[4 * px + 2 * py + pc], dst_ref=dsts[a].at[k],
                    send_sem=send.at[7 * a + k], recv_sem=recv.at[7 * a + k], device_id=(px, py, pc), device_id_type=MESH))
        return out

    def start(self, *refs):
        for cp in self._copies(*refs):
            cp.start()

    def forward(self, *refs):
        pass

    def finish(self, *refs):
        for cp in self._copies(*refs):
            cp.wait_send()
        for cp in self._copies(*refs):
            cp.wait_recv()


class _ChipExchange:
    def __init__(self, arrays):
        self.n = len(arrays)
        self.out_shape = [jax.ShapeDtypeStruct(a.shape, a.dtype) for a in arrays]
        self.scratch = [pltpu.SemaphoreType.DMA((3 * self.n,)), pltpu.SemaphoreType.DMA((3 * self.n,))]

    def _copies(self, srcs, dsts, send, recv):
        x, y, c = _place()
        return [pltpu.make_async_remote_copy(
            src_ref=srcs[a].at[k], dst_ref=dsts[a].at[k], send_sem=send.at[3 * a + k], recv_sem=recv.at[3 * a + k],
            device_id=(_flip(x, fx), _flip(y, fy), c), device_id_type=MESH)
            for a in range(self.n) for k, (fx, fy) in enumerate(CHIP_FLIPS[1:])]

    def start(self, *refs):
        for cp in self._copies(*refs):
            cp.start()

    def forward(self, *refs):
        pass

    def finish(self, *refs):
        for cp in self._copies(*refs):
            cp.wait_send()
        for cp in self._copies(*refs):
            cp.wait_recv()


class _Plans:
    def __init__(self, plans):
        self.plans = plans
        self.n = sum(p.n for p in plans)
        self.out_shape = [s for p in plans for s in p.out_shape]
        self.scratch = [s for p in plans for s in p.scratch]

    def _each(self, phase, srcs, dsts, *sems):
        i0 = s0 = 0
        for p in self.plans:
            getattr(p, phase)(srcs[i0:i0 + p.n], dsts[i0:i0 + p.n], *sems[s0:s0 + len(p.scratch)])
            i0, s0 = i0 + p.n, s0 + len(p.scratch)

    def start(self, *refs):
        self._each("start", *refs)

    def forward(self, *refs):
        self._each("forward", *refs)

    def finish(self, *refs):
        self._each("finish", *refs)


def _run_comm(comm, arrays, name):
    n = comm.n

    def body(*refs):
        args = (refs[:n], refs[n:2 * n], *refs[2 * n:])
        comm.start(*args)
        comm.forward(*args)
        comm.finish(*args)

    return pl.pallas_call(body, name=name, out_shape=comm.out_shape, in_specs=[ANY] * n, out_specs=[ANY] * n,
                          scratch_shapes=comm.scratch)(*arrays)


def _rs_sibling(arrays, name):
    n = len(arrays)

    def body(*refs):
        srcs, got, (send, recv) = refs[:n], refs[n:2 * n], refs[2 * n:]
        x, y, c = _place()
        copies = []
        for a in range(n):
            for r, (fx, fy) in enumerate(CHIP_FLIPS):
                chip = 2 * _flip(x, fx) + _flip(y, fy)
                copies.append(pltpu.make_async_remote_copy(
                    src_ref=srcs[a].at[2 * chip + 1 - c], dst_ref=got[a].at[r], send_sem=send.at[4 * a + r],
                    recv_sem=recv.at[4 * a + r], device_id=(x, y, 1 - c), device_id_type=MESH))
        for cp in copies:
            cp.start()
        for cp in copies:
            cp.wait_send()
        for cp in copies:
            cp.wait_recv()

    return pl.pallas_call(
        body, name=name, out_shape=[jax.ShapeDtypeStruct((4, *a.shape[1:]), a.dtype) for a in arrays],
        in_specs=[ANY] * n, out_specs=[ANY] * n,
        scratch_shapes=[pltpu.SemaphoreType.DMA((4 * n,)), pltpu.SemaphoreType.DMA((4 * n,))],
    )(*arrays)


def _row_tile(rows):
    return 256 if rows % 256 == 0 else rows


def _chip_slots():
    x, y, c = _place()
    return jnp.stack([4 * _flip(x, fx) + 2 * _flip(y, fy) + c for fx, fy in CHIP_FLIPS]).astype(I32)


def _pair_sum(full, theirs, name):
    _, rows, cols = theirs.shape
    tr = _row_tile(rows)

    def body(slots_ref, m0_ref, m1_ref, m2_ref, m3_ref, b_ref, own_ref, rest_ref):
        own_ref[...] = m0_ref[...].astype(F32) + b_ref[0].astype(F32)
        for k, m_ref in enumerate((m1_ref, m2_ref, m3_ref)):
            rest_ref[k] = (m_ref[...].astype(F32) + b_ref[k + 1].astype(F32)).astype(BF16)

    def mine(k):
        return pl.BlockSpec((None, tr, cols), lambda i, slots: (slots[k], i, 0))

    return pl.pallas_call(
        body, name=name,
        grid_spec=pltpu.PrefetchScalarGridSpec(
            num_scalar_prefetch=1, grid=(rows // tr,),
            in_specs=[mine(0), mine(1), mine(2), mine(3), pl.BlockSpec((4, tr, cols), lambda i, slots: (0, i, 0))],
            out_specs=(pl.BlockSpec((tr, cols), lambda i, slots: (i, 0)), pl.BlockSpec((3, tr, cols), lambda i, slots: (0, i, 0)))),
        out_shape=(jax.ShapeDtypeStruct((rows, cols), F32), jax.ShapeDtypeStruct((3, rows, cols), BF16)),
        compiler_params=_cp("parallel"),
    )(_chip_slots(), full, full, full, full, theirs)


def _head_lanes(width, h):
    lane = lax.broadcasted_iota(I32, (1, width), 1)
    if width == LANES:
        return (lane >= 64 * h) & (lane < 64 * h + 64)
    nope = (lane >= NOPE * h) & (lane < NOPE * h + NOPE)
    rope = (lane >= 2 * NOPE + ROPE * h) & (lane < 2 * NOPE + ROPE * h + ROPE)
    return nope | rope


def _dilated_bias_table(seq):
    t = min(ATTN_TILE, seq)
    nd = seq // t

    def body(o_ref):
        delta = pl.program_id(0) * t + lax.broadcasted_iota(I32, (t, t), 1) - lax.broadcasted_iota(I32, (t, t), 0)
        mult = ((delta <= 128).astype(I32) + (((delta & 3) == 0) & (delta <= 512)).astype(I32)
                + ((delta & 15) == 0).astype(I32))
        logm = jnp.where(mult == 3, math.log(3.0), jnp.where(mult == 2, math.log(2.0), 0.0))
        valid = (delta >= 0) & (mult > 0)
        dist = delta.astype(F32)
        for h in range(N_HEADS):
            o_ref[h] = jnp.where(valid, logm - 2.0 ** (-(h + 1)) * dist, NEG)

    return pl.pallas_call(
        body, name="dilated_bias_table", grid=(nd,), out_shape=jax.ShapeDtypeStruct((N_HEADS, nd, t, t), F32),
        out_specs=pl.BlockSpec((N_HEADS, None, t, t), lambda d: (0, d, 0, 0)),
        compiler_params=_cp("parallel"),
    )()


def _comm_hooks(comm, refs, n_in, n_out):
    if comm is None:
        return refs[:n_in], refs[n_in:n_in + n_out], refs[n_in + n_out:], None
    n = comm.n
    ins, srcs = refs[:n_in], refs[n_in:n_in + n]
    outs, dsts = refs[n_in + n:n_in + n + n_out], refs[n_in + n + n_out:n_in + 2 * n + n_out]
    rest = refs[n_in + 2 * n + n_out:]
    own = len(rest) - len(comm.scratch)
    return ins, outs, rest[:own], (srcs, dsts, *rest[own:])


def _attn_fwd(q, k, v, bias, *, batch, seq, width, col0, dilated, scale, name, comm=None, comm_arrays=()):
    t = min(ATTN_TILE, seq)
    nq = seq // t
    cq, ck, cv = col0
    pre = scale if dilated else 1.0
    steps = batch * N_PAIRS

    def body(*refs):
        (q_ref, k_ref, v_ref, bias_ref), (o_ref, lse_ref), (v_heads,), plan = _comm_hooks(comm, refs, 4, 2)
        step_no = pl.program_id(0) * N_PAIRS + pl.program_id(1)
        if plan:
            pl.when(step_no == 0)(lambda: comm.start(*plan))
            pl.when(step_no == (3 * steps) // 4)(lambda: comm.forward(*plan))
        v_all = v_ref[...].astype(F32)
        for h in (0, 1):
            v_heads[h] = jnp.transpose(jnp.where(_head_lanes(LANES, h), v_all, 0.0)).astype(BF16)
        top = lax.broadcasted_iota(I32, (LANES, t), 0) < HEAD_V
        causal = lax.broadcasted_iota(I32, (t, t), 0) <= lax.broadcasted_iota(I32, (t, t), 1)
        lax.fori_loop(0, nq, functools.partial(query_tile, q_ref, k_ref, bias_ref, o_ref, lse_ref, v_heads, top, causal), 0)
        if plan:
            pl.when(step_no == steps - 1)(lambda: comm.finish(*plan))

    def query_tile(q_ref, k_ref, bias_ref, o_ref, lse_ref, v_heads, top, causal, i, _):
        qs = pl.multiple_of(i * t, t)
        q2 = q_ref[pl.ds(qs, t), :] * pre if dilated else q_ref[pl.ds(qs, t), :]
        qh = [jnp.where(_head_lanes(width, h), q2, jnp.zeros_like(q2)) for h in (0, 1)]

        def scores(j):
            kj = k_ref[pl.ds(pl.multiple_of(j * t, t), t), :]
            return [_dot_nt(kj, qh[h]) for h in (0, 1)]

        def step(j, carry, last):
            m0, l0, m1, l1, acc, s0, s1 = carry
            ahead = [] if last else scores(j + 1)
            ks = pl.multiple_of(j * t, t)
            new, alphas, pv = [], [], []
            for h, (m, l, s) in enumerate(((m0, l0, s0), (m1, l1, s1))):
                if dilated:
                    s = s + bias_ref[h, i - j]
                else:
                    s = s * scale
                    if last:
                        s = jnp.where(causal, s, NEG)
                m_new = jnp.maximum(m, jnp.max(s, axis=0, keepdims=True))
                a = jnp.exp(m - m_new)
                p = jnp.exp(s - m_new)
                new += [m_new, a * l + jnp.sum(p, axis=0, keepdims=True)]
                alphas.append(a)
                pv.append(_dot(v_heads[h, :, pl.ds(ks, t)], p.astype(BF16)))
            acc = jnp.where(top, alphas[0], alphas[1]) * acc + pv[0] + pv[1]
            return (*new, acc, *ahead)

        row = jnp.full((1, t), NEG, F32)
        zero = jnp.zeros((1, t), F32)
        init = (row, zero, row, zero, jnp.zeros((LANES, t), F32), *scores(0))
        m0, l0, m1, l1, acc = step(i, lax.fori_loop(0, i, functools.partial(step, last=False), init), True)
        o_ref[pl.ds(qs, t), :] = jnp.transpose(acc * jnp.where(top, 1.0 / l0, 1.0 / l1)).astype(BF16)
        r = lax.broadcasted_iota(I32, (8, t), 0)
        lse_ref[:, pl.ds(qs, t)] = jnp.where(r == 0, m0 + jnp.log(l0), jnp.where(r == 1, m1 + jnp.log(l1), 0.0))
        return 0

    bias_spec = (pl.BlockSpec((2, nq, t, t), lambda b, p: (p, 0, 0, 0)) if dilated
                 else pl.BlockSpec((None, 8, LANES), lambda b, p: (0, 0, 0)))
    n = comm.n if comm else 0
    return pl.pallas_call(
        body, name=name, grid=(batch, N_PAIRS),
        out_shape=[jax.ShapeDtypeStruct((batch * seq, DIL_WIDTH), BF16), jax.ShapeDtypeStruct((batch * N_PAIRS, 8, seq), F32)]
        + (comm.out_shape if comm else []),
        in_specs=[pl.BlockSpec((seq, width), lambda b, p: (b, cq + p)),
                  pl.BlockSpec((seq, width), lambda b, p: (b, ck + p)),
                  pl.BlockSpec((seq, LANES), lambda b, p: (b, cv + p)),
                  bias_spec] + [ANY] * n,
        out_specs=[pl.BlockSpec((seq, LANES), lambda b, p: (b, p)),
                   pl.BlockSpec((None, 8, seq), lambda b, p: (b * N_PAIRS + p, 0, 0))] + [ANY] * n,
        scratch_shapes=[pltpu.VMEM((2, LANES, seq), BF16)] + (comm.scratch if comm else []),
        compiler_params=_cp("arbitrary", "arbitrary") if comm else _cp("parallel", "parallel"),
    )(q, k, v, bias, *comm_arrays)


def _attn_bwd(q, k, v, o, do, lse, bias, *, batch, seq, width, col0, dilated, scale, name, comm=None, comm_arrays=()):
    t = min(ATTN_TILE, seq)
    nq = seq // t
    cq, ck, cv = col0
    pre = scale if dilated else 1.0
    dq_transposed = width == LANES
    steps = batch * N_PAIRS

    def body(*refs):
        ins, (dq_ref, dk_ref, dv_ref), (dq_acc, dk_acc, dv_acc, rowdot, q_heads, do_heads), plan = _comm_hooks(comm, refs, 7, 3)
        q_ref, k_ref, v_ref, o_ref, do_ref, lse_ref, bias_ref = ins
        step_no = pl.program_id(0) * N_PAIRS + pl.program_id(1)
        if plan:
            pl.when(step_no == 0)(lambda: comm.start(*plan))
        wlane = [_head_lanes(width, h) for h in (0, 1)]
        vlane = [_head_lanes(LANES, h) for h in (0, 1)]
        causal = lax.broadcasted_iota(I32, (t, t), 0) <= lax.broadcasted_iota(I32, (t, t), 1)
        q_all = q_ref[...] * pre if dilated else q_ref[...]
        for h in (0, 1):
            q_heads[h] = jnp.where(wlane[h], q_all, jnp.zeros_like(q_all))
            do_heads[h] = jnp.where(vlane[h], do_ref[...], jnp.zeros_like(do_ref[...]))
        prod = jnp.transpose(do_ref[...].astype(F32) * o_ref[...].astype(F32))
        rowdot[0:1, :] = jnp.sum(prod[0:HEAD_V], axis=0, keepdims=True)
        rowdot[1:2, :] = jnp.sum(prod[HEAD_V:], axis=0, keepdims=True)
        dq_acc[...] = jnp.zeros_like(dq_acc)

        def k_tile(j, _):
            ks = pl.multiple_of(j * t, t)
            kj = k_ref[pl.ds(ks, t), :]
            vj = v_ref[pl.ds(ks, t), :]
            kh = [jnp.where(wlane[h], kj, jnp.zeros_like(kj)) for h in (0, 1)]
            if dq_transposed:
                kh = [jnp.transpose(kh[h].astype(F32)).astype(BF16) for h in (0, 1)]
            dk_acc[...] = jnp.zeros_like(dk_acc)
            dv_acc[...] = jnp.zeros_like(dv_acc)

            def operands(i):
                qs = pl.multiple_of(i * t, t)
                return [q_heads[h, pl.ds(qs, t), :] for h in (0, 1)], [do_heads[h, pl.ds(qs, t), :] for h in (0, 1)]

            def products(i):
                qih, doih = operands(i)
                scores = tuple(_dot_nt(kj, qih[h]) for h in (0, 1))
                return scores + tuple(_dot_nt(vj, doih[h]) for h in (0, 1)) if width > LANES else scores

            def q_tile(n, carry, last):
                i = nq - 1 - n
                ahead = () if last else products(i - 1)
                qs = pl.multiple_of(i * t, t)
                qih, doih = operands(i)
                s0, s1 = carry[:2]
                dps = carry[2:] if width > LANES else [_dot_nt(vj, doih[h]) for h in (0, 1)]
                dq_i = jnp.zeros((width, t) if dq_transposed else (t, width), F32)
                for h, (s, dp) in enumerate(((s0, dps[0]), (s1, dps[1]))):
                    if dilated:
                        s = s + bias_ref[h, i - j]
                    else:
                        s = s * scale
                        if last:
                            s = jnp.where(causal, s, NEG)
                    p = jnp.exp(s - lse_ref[h:h + 1, pl.ds(qs, t)])
                    ds = p * (dp - rowdot[h:h + 1, pl.ds(qs, t)])
                    ds = (ds if dilated else ds * scale).astype(BF16)
                    dv_acc[...] += _dot(p.astype(BF16), doih[h])
                    dk_acc[...] += _dot(ds, qih[h])
                    dq_i = dq_i + (_dot(kh[h], ds) if dq_transposed else _dot_tn(ds, kh[h]))
                if dq_transposed:
                    dq_acc[:, pl.ds(qs, t)] += dq_i
                else:
                    dq_acc[pl.ds(qs, t), :] += dq_i
                return ahead

            q_tile(nq - 1 - j, lax.fori_loop(0, nq - 1 - j, functools.partial(q_tile, last=False), products(nq - 1)), True)
            dk_ref[pl.ds(ks, t), :] = dk_acc[...].astype(BF16)
            dv_ref[pl.ds(ks, t), :] = dv_acc[...].astype(BF16)
            return 0

        lax.fori_loop(0, nq, k_tile, 0)
        dq_ref[...] = ((jnp.transpose(dq_acc[...]) if dq_transposed else dq_acc[...]) * pre).astype(BF16)
        if plan:
            pl.when(step_no == steps - 1)(lambda: comm.finish(*plan))

    tokens = batch * seq
    bias_spec = (pl.BlockSpec((2, nq, t, t), lambda b, p: (p, 0, 0, 0)) if dilated
                 else pl.BlockSpec((None, 8, LANES), lambda b, p: (0, 0, 0)))
    n = comm.n if comm else 0
    return pl.pallas_call(
        body, name=name, grid=(batch, N_PAIRS),
        out_shape=[jax.ShapeDtypeStruct((tokens, N_PAIRS * width), BF16), jax.ShapeDtypeStruct((tokens, N_PAIRS * width), BF16),
                   jax.ShapeDtypeStruct((tokens, DIL_WIDTH), BF16)] + (comm.out_shape if comm else []),
        in_specs=[pl.BlockSpec((seq, width), lambda b, p: (b, cq + p)),
                  pl.BlockSpec((seq, width), lambda b, p: (b, ck + p)),
                  pl.BlockSpec((seq, LANES), lambda b, p: (b, cv + p)),
                  pl.BlockSpec((seq, LANES), lambda b, p: (b, p)),
                  pl.BlockSpec((seq, LANES), lambda b, p: (b, p)),
                  pl.BlockSpec((None, 8, seq), lambda b, p: (b * N_PAIRS + p, 0, 0)),
                  bias_spec] + [ANY] * n,
        out_specs=[pl.BlockSpec((seq, width), lambda b, p: (b, p)),
                   pl.BlockSpec((seq, width), lambda b, p: (b, p)),
                   pl.BlockSpec((seq, LANES), lambda b, p: (b, p))] + [ANY] * n,
        scratch_shapes=[pltpu.VMEM((width, seq) if dq_transposed else (seq, width), F32),
                        pltpu.VMEM((t, width), F32), pltpu.VMEM((t, LANES), F32),
                        pltpu.VMEM((8, seq), F32), pltpu.VMEM((2, seq, width), BF16), pltpu.VMEM((2, seq, LANES), BF16)]
        + (comm.scratch if comm else []),
        compiler_params=_cp("arbitrary", "arbitrary") if comm else _cp("parallel", "parallel"),
    )(q, k, v, o, do, lse, bias, *comm_arrays)


def _rms(xf, g):
    r = lax.rsqrt(jnp.mean(xf * xf, axis=1, keepdims=True) + RMS_EPS)
    return xf * r * g, r


def _rms_bwd(dy, xf, r, g):
    gy = dy * g
    dx = r * gy - xf * (r * r * r) * jnp.mean(gy * xf, axis=1, keepdims=True)
    return dx, dy * xf * r


def _ln_bwd(dy, xhat, rstd, g):
    dxh = dy * g
    return rstd * (dxh - jnp.mean(dxh, axis=1, keepdims=True) - xhat * jnp.mean(dxh * xhat, axis=1, keepdims=True))


def _rope_slabs(q, cos, sin, transpose):
    first_half = (lax.broadcasted_iota(I32, (1, LANES), 1) % ROPE) < ROPE // 2
    out = []
    for p in range(N_PAIRS):
        blk = q[:, p * PAIR_W + LANES:(p + 1) * PAIR_W]
        y = blk * sin if transpose else blk
        up, down = pltpu.roll(y, LANES - ROPE // 2, 1), pltpu.roll(y, ROPE // 2, 1)
        rot = jnp.where(first_half, up, -down) if transpose else jnp.where(first_half, -up, down) * sin
        out += [q[:, p * PAIR_W:p * PAIR_W + LANES], blk * cos + rot]
    return jnp.concatenate(out, axis=1)


def _fwd_proj(x, w_in_ext, w1, wk, wv, g_q, g_kv, cext, sext, cs128, *, seq):
    tokens = x.shape[0]
    tm = min(TOKEN_TILE, seq)
    ns = seq // tm

    def body(x_ref, win_ref, w1_ref, wk_ref, wv_ref, gq_ref, gkv_ref, c_ref, s_ref, cs_ref,
             low_ref, gates_ref, qkvd_ref, qp_ref, kp_ref, vm_ref, qn_ref, kvn_ref, xb_ref):
        xt = x_ref[...].astype(BF16)
        xb_ref[...] = xt
        low = _dot(xt, win_ref[:, 0:LOW_W])
        low_ref[...] = low
        qkvd_ref[...] = _dot(xt, win_ref[:, LOW_W:LOW_W + 3 * DIL_WIDTH]).astype(BF16)
        gates_ref[...] = _dot(xt, win_ref[:, LOW_W + 3 * DIL_WIDTH:]).astype(BF16)
        qn = _rms(low[:, 0:Q_LORA], gq_ref[...])[0].astype(BF16)
        kvn = _rms(low[:, Q_LORA:Q_LORA + KV_LORA], gkv_ref[...])[0].astype(BF16)
        qn_ref[...] = qn
        kvn_ref[...] = kvn
        qp_ref[...] = _rope_slabs(_dot(qn, w1_ref[...]), c_ref[...], s_ref[...], False).astype(BF16)
        kr = low[:, Q_LORA + KV_LORA:] * cs_ref[...]
        kr = kr + pltpu.roll(kr, LANES - ROPE, 1)
        lane = lax.broadcasted_iota(I32, kr.shape, 1)
        kr = jnp.where(lane < ROPE, kr, 0.0)
        kr = (kr + pltpu.roll(kr, ROPE, 1)).astype(BF16)
        kn = _dot(kvn, wk_ref[...]).astype(BF16)
        kp_ref[...] = jnp.concatenate([blk for p in range(N_PAIRS) for blk in (kn[:, p * LANES:(p + 1) * LANES], kr)], axis=1)
        vm_ref[...] = _dot(kvn, wv_ref[...]).astype(BF16)

    n_gates = 2 * D_MODEL
    outs = [(LOW_W, F32), (n_gates, BF16), (3 * DIL_WIDTH, BF16), (N_PAIRS * PAIR_W, BF16), (N_PAIRS * PAIR_W, BF16),
            (DIL_WIDTH, BF16), (Q_LORA, BF16), (KV_LORA, BF16), (D_MODEL, BF16)]
    return pl.pallas_call(
        body, name="fwd_proj", grid=(tokens // tm,),
        out_shape=tuple(jax.ShapeDtypeStruct((tokens, w), dt) for w, dt in outs),
        in_specs=[_rows(tm, D_MODEL), _full(w_in_ext.shape), _full(w1.shape), _full(wk.shape),
                  _full(wv.shape), _full(g_q.shape), _full(g_kv.shape),
                  pl.BlockSpec((tm, LANES), lambda i: (i % ns, 1)),
                  pl.BlockSpec((tm, LANES), lambda i: (i % ns, 1)),
                  pl.BlockSpec((tm, LANES), lambda i: (i % ns, 0))],
        out_specs=tuple(_rows(tm, w) for w, _ in outs),
        compiler_params=_cp("parallel"),
    )(x, w_in_ext, w1, wk, wv, g_q, g_kv, cext, sext, cs128)


def _fwd_mix(o_a, o_b, gates, x, b_gate, w_oa, w_ob, w_out, ln_g, ln_b, *, seq):
    tokens = x.shape[0]
    tm = min(MIX_TILE, seq)

    def body(oa_ref, ob_ref, gt_ref, x_ref, bg_ref, woa_ref, wob_ref, wout_ref, g_ref, b_ref,
             hb_ref, xhat_ref, rstd_ref, ya_ref, yb_ref, mix_ref):
        ya = _dot(oa_ref[...], woa_ref[...])
        yb = _dot(ob_ref[...], wob_ref[...])
        g0 = _sigmoid(gt_ref[:, 0:D_MODEL].astype(F32) + bg_ref[0:1, :])
        g1 = _sigmoid(gt_ref[:, D_MODEL:].astype(F32) + bg_ref[1:2, :])
        mix = (g0 * ya + g1 * yb).astype(BF16)
        z = ALPHA * x_ref[...] + _dot(mix, wout_ref[...])
        zc = z - jnp.mean(z, axis=1, keepdims=True)
        rstd = lax.rsqrt(jnp.mean(zc * zc, axis=1, keepdims=True) + LN_EPS)
        xhat = zc * rstd
        hb_ref[...] = (xhat * g_ref[...] + b_ref[...]).astype(BF16)
        xhat_ref[...] = xhat
        rstd_ref[...] = jnp.broadcast_to(rstd, (tm, LANES))
        ya_ref[...] = ya.astype(BF16)
        yb_ref[...] = yb.astype(BF16)
        mix_ref[...] = mix

    outs = [(D_MODEL, BF16), (D_MODEL, F32), (LANES, F32), (D_MODEL, BF16), (D_MODEL, BF16), (D_MODEL, BF16)]
    return pl.pallas_call(
        body, name="fwd_mix", grid=(tokens // tm,),
        out_shape=tuple(jax.ShapeDtypeStruct((tokens, w), dt) for w, dt in outs),
        in_specs=[_rows(tm, DIL_WIDTH), _rows(tm, DIL_WIDTH), _rows(tm, 2 * D_MODEL), _rows(tm, D_MODEL),
                  _full(b_gate.shape), _full(w_oa.shape), _full(w_ob.shape), _full(w_out.shape),
                  _full(ln_g.shape), _full(ln_b.shape)],
        out_specs=tuple(_rows(tm, w) for w, _ in outs),
        compiler_params=_cp("parallel"),
    )(o_a, o_b, gates, x, b_gate, w_oa, w_ob, w_out, ln_g, ln_b)


def _fwd_mlp(hb, xhat1, target, w_ff1, w_ff2, ln1_g, ln1_b, ln_g, ln_b, *, seq):
    tokens = hb.shape[0]
    tm = min(2 * TOKEN_TILE, seq)
    tf = FF_SHARD
    nf = N_DEV // FF_STEP

    def body(hb_ref, xh_ref, tg_ref, w1_ref, w2_ref, g1_ref, b1_ref, g_ref, b_ref, u_ref, dz_ref, dzb_ref, stat_ref, acc):
        i, j = pl.program_id(0), pl.program_id(1)

        @pl.when((i == 0) & (j == 0))
        def _():
            stat_ref[...] = jnp.zeros_like(stat_ref)

        @pl.when(j == 0)
        def _():
            acc[...] = jnp.zeros_like(acc)

        acts = []
        for s in range(FF_STEP):
            u = _dot(hb_ref[...], w1_ref[s])
            u_ref[:, s * tf:(s + 1) * tf] = u.astype(BF16)
            acts.append(jnp.square(jnp.maximum(u, 0.0)).astype(BF16))
        acc[...] += _dot(jnp.concatenate(acts, axis=1), w2_ref[...])

        @pl.when(j == nf - 1)
        def _():
            z = ALPHA * (xh_ref[...] * g1_ref[...] + b1_ref[...]) + acc[...]
            zc = z - jnp.mean(z, axis=1, keepdims=True)
            rstd = lax.rsqrt(jnp.mean(zc * zc, axis=1, keepdims=True) + LN_EPS)
            xhat = zc * rstd
            err = xhat * g_ref[...] + b_ref[...] - tg_ref[...]
            dy = err * (1.0 / D_MODEL)
            dz = _ln_bwd(dy, xhat, rstd, g_ref[...])
            dz_ref[...] = dz
            dzb_ref[...] = dz.astype(BF16)
            stat_ref[0:1, :] += jnp.sum(dy * xhat, axis=0, keepdims=True)
            stat_ref[1:2, :] += jnp.sum(dy, axis=0, keepdims=True)
            stat_ref[2:3, :] += jnp.sum(jnp.sum(err * err, axis=1, keepdims=True), axis=0, keepdims=True) * (0.5 / D_MODEL)

    return pl.pallas_call(
        body, name="fwd_mlp", grid=(tokens // tm, nf),
        out_shape=(jax.ShapeDtypeStruct((tokens, D_FF), BF16), jax.ShapeDtypeStruct((tokens, D_MODEL), F32),
                   jax.ShapeDtypeStruct((tokens, D_MODEL), BF16), jax.ShapeDtypeStruct((8, D_MODEL), F32)),
        in_specs=[_rows(tm, D_MODEL), _rows(tm, D_MODEL), _rows(tm, D_MODEL),
                  pl.BlockSpec((FF_STEP, D_MODEL, tf), lambda i, j: (j, 0, 0)),
                  pl.BlockSpec((FF_STEP * tf, D_MODEL), lambda i, j: (j, 0)),
                  _full(ln1_g.shape), _full(ln1_b.shape), _full(ln_g.shape), _full(ln_b.shape)],
        out_specs=(pl.BlockSpec((tm, FF_STEP * tf), lambda i, j: (i, j)), _rows(tm, D_MODEL), _rows(tm, D_MODEL),
                   _full((8, D_MODEL))),
        scratch_shapes=[pltpu.VMEM((tm, D_MODEL), F32)],
        compiler_params=_cp("arbitrary", "arbitrary"),
    )(hb, xhat1, target, w_ff1, w_ff2, ln1_g, ln1_b, ln_g, ln_b)


def _bwd_mlp(dz2, dz2b, u, xhat1, rstd1, w_ff1, w_ff2, ln_g, *, seq):
    tokens = dz2.shape[0]
    tm = min(2 * TOKEN_TILE, seq)
    tf = FF_SHARD
    nf = N_DEV // FF_STEP

    def body(dz_ref, dzb_ref, u_ref, xh_ref, rs_ref, w1_ref, w2_ref, g_ref, du_ref, dz1_ref, dz1b_ref, stat_ref, acc):
        i, j = pl.program_id(0), pl.program_id(1)

        @pl.when((i == 0) & (j == 0))
        def _():
            stat_ref[...] = jnp.zeros_like(stat_ref)

        @pl.when(j == 0)
        def _():
            acc[...] = jnp.zeros_like(acc)

        da = _dot_nt(dzb_ref[...], w2_ref[...])
        du = (da * (2.0 * jnp.maximum(u_ref[...].astype(F32), 0.0))).astype(BF16)
        du_ref[...] = du
        part = _dot_nt(du[:, 0:tf], w1_ref[0])
        for s in range(1, FF_STEP):
            part = part + _dot_nt(du[:, s * tf:(s + 1) * tf], w1_ref[s])
        acc[...] += part

        @pl.when(j == nf - 1)
        def _():
            dh = ALPHA * dz_ref[...] + acc[...]
            xhat = xh_ref[...]
            dz1 = _ln_bwd(dh, xhat, rs_ref[:, 0:1], g_ref[...])
            dz1_ref[...] = dz1
            dz1b_ref[...] = dz1.astype(BF16)
            stat_ref[0:1, :] += jnp.sum(dh * xhat, axis=0, keepdims=True)
            stat_ref[1:2, :] += jnp.sum(dh, axis=0, keepdims=True)

    return pl.pallas_call(
        body, name="bwd_mlp", grid=(tokens // tm, nf),
        out_shape=(jax.ShapeDtypeStruct((tokens, D_FF), BF16), jax.ShapeDtypeStruct((tokens, D_MODEL), F32),
                   jax.ShapeDtypeStruct((tokens, D_MODEL), BF16), jax.ShapeDtypeStruct((8, D_MODEL), F32)),
        in_specs=[_rows(tm, D_MODEL), _rows(tm, D_MODEL), pl.BlockSpec((tm, FF_STEP * tf), lambda i, j: (i, j)),
                  _rows(tm, D_MODEL), _rows(tm, LANES),
                  pl.BlockSpec((FF_STEP, D_MODEL, tf), lambda i, j: (j, 0, 0)),
                  pl.BlockSpec((FF_STEP * tf, D_MODEL), lambda i, j: (j, 0)),
                  _full(ln_g.shape)],
        out_specs=(pl.BlockSpec((tm, FF_STEP * tf), lambda i, j: (i, j)), _rows(tm, D_MODEL), _rows(tm, D_MODEL),
                   _full((8, D_MODEL))),
        scratch_shapes=[pltpu.VMEM((tm, D_MODEL), F32)],
        compiler_params=_cp("arbitrary", "arbitrary"),
    )(dz2, dz2b, u, xhat1, rstd1, w_ff1, w_ff2, ln_g)


def _bwd_mix(dz1b, gates, y_a, y_b, b_gate, w_oa, w_ob, w_out, *, seq):
    tokens = dz1b.shape[0]
    tm = min(MIX_TILE, seq)

    def body(dz_ref, gt_ref, ya_ref, yb_ref, bg_ref, woa_ref, wob_ref, wout_ref,
             dgt_ref, dya_ref, dyb_ref, doa_ref, dob_ref, stat_ref):
        @pl.when(pl.program_id(0) == 0)
        def _():
            stat_ref[...] = jnp.zeros_like(stat_ref)

        dmix = _dot_nt(dz_ref[...], wout_ref[...])
        for k, (y_ref, w_ref, dy_ref, do_ref) in enumerate(((ya_ref, woa_ref, dya_ref, doa_ref), (yb_ref, wob_ref, dyb_ref, dob_ref))):
            g = _sigmoid(gt_ref[:, k * D_MODEL:(k + 1) * D_MODEL].astype(F32) + bg_ref[k:k + 1, :])
            dgate = dmix * y_ref[...].astype(F32) * g * (1.0 - g)
            dgt_ref[:, k * D_MODEL:(k + 1) * D_MODEL] = dgate.astype(BF16)
            stat_ref[k:k + 1, :] += jnp.sum(dgate, axis=0, keepdims=True)
            dy = (dmix * g).astype(BF16)
            dy_ref[...] = dy
            do_ref[...] = _dot_nt(dy, w_ref[...]).astype(BF16)

    outs = [(2 * D_MODEL, BF16), (D_MODEL, BF16), (D_MODEL, BF16), (DIL_WIDTH, BF16), (DIL_WIDTH, BF16)]
    return pl.pallas_call(
        body, name="bwd_mix", grid=(tokens // tm,),
        out_shape=tuple(jax.ShapeDtypeStruct((tokens, w), dt) for w, dt in outs) + (jax.ShapeDtypeStruct((8, D_MODEL), F32),),
        in_specs=[_rows(tm, D_MODEL), _rows(tm, 2 * D_MODEL), _rows(tm, D_MODEL), _rows(tm, D_MODEL),
                  _full(b_gate.shape), _full(w_oa.shape), _full(w_ob.shape), _full(w_out.shape)],
        out_specs=tuple(_rows(tm, w) for w, _ in outs) + (_full((8, D_MODEL)),),
        compiler_params=_cp("arbitrary"),
    )(dz1b, gates, y_a, y_b, b_gate, w_oa, w_ob, w_out)


def _bwd_proj(dqp, dkp, dvm, dq_d, dk_d, dv_d, dgates, dz1, low, w_in_ext, w1, wk, wv, g_q, g_kv, cext, sext, cs128, *, seq):
    tokens = dz1.shape[0]
    tm = min(TOKEN_TILE, seq)
    ns = seq // tm

    def body(dqp_ref, dkp_ref, dvm_ref, dqd_ref, dkd_ref, dvd_ref, dgt_ref, dz_ref, low_ref, win_ref, w1_ref, wk_ref,
             wv_ref, gq_ref, gkv_ref, c_ref, s_ref, cs_ref, dx_ref, dproj_ref, da_ref, dkn_ref, stat_ref):
        @pl.when(pl.program_id(0) == 0)
        def _():
            stat_ref[...] = jnp.zeros_like(stat_ref)

        low = low_ref[...]
        d_a = _rope_slabs(dqp_ref[...].astype(F32), c_ref[...], s_ref[...], True).astype(BF16)
        da_ref[...] = d_a
        q_a = low[:, 0:Q_LORA]
        _, rq = _rms(q_a, gq_ref[...])
        dq_a, gq_terms = _rms_bwd(_dot_nt(d_a, w1_ref[...]), q_a, rq, gq_ref[...])
        kv_a = low[:, Q_LORA:Q_LORA + KV_LORA]
        _, rkv = _rms(kv_a, gkv_ref[...])
        dkn = jnp.concatenate([dkp_ref[:, p * PAIR_W:p * PAIR_W + LANES] for p in range(N_PAIRS)], axis=1)
        dkn_ref[...] = dkn
        dkv_a, gkv_terms = _rms_bwd(_dot_nt(dkn, wk_ref[...]) + _dot_nt(dvm_ref[...], wv_ref[...]), kv_a, rkv, gkv_ref[...])
        dkr = sum(dkp_ref[:, p * PAIR_W + LANES:(p + 1) * PAIR_W].astype(F32) for p in range(N_PAIRS))
        dkr = dkr + pltpu.roll(dkr, LANES - ROPE, 1)
        dkr = jnp.where(lax.broadcasted_iota(I32, dkr.shape, 1) < ROPE, dkr, 0.0)
        dkr = (dkr + pltpu.roll(dkr, ROPE, 1)) * cs_ref[...]
        stat_ref[0:1, 0:Q_LORA] += jnp.sum(gq_terms, axis=0, keepdims=True)
        stat_ref[1:2, 0:KV_LORA] += jnp.sum(gkv_terms, axis=0, keepdims=True)
        dproj_ref[:, 0:Q_LORA] = dq_a.astype(BF16)
        dproj_ref[:, Q_LORA:Q_LORA + KV_LORA] = dkv_a.astype(BF16)
        dproj_ref[:, Q_LORA + KV_LORA:LOW_W] = dkr.astype(BF16)
        dproj_ref[:, LOW_W:LOW_W + DIL_WIDTH] = dqd_ref[...]
        dproj_ref[:, LOW_W + DIL_WIDTH:LOW_W + 2 * DIL_WIDTH] = dkd_ref[...]
        dproj_ref[:, LOW_W + 2 * DIL_WIDTH:LOW_W + 3 * DIL_WIDTH] = dvd_ref[...]
        dproj_ref[:, LOW_W + 3 * DIL_WIDTH:] = dgt_ref[...]
        dx_ref[...] = ALPHA * dz_ref[...] + _dot_nt(dproj_ref[...], win_ref[...])

    wide = N_PAIRS * PAIR_W
    return pl.pallas_call(
        body, name="bwd_proj", grid=(tokens // tm,),
        out_shape=(jax.ShapeDtypeStruct((tokens, D_MODEL), F32), jax.ShapeDtypeStruct((tokens, IN_EXT), BF16),
                   jax.ShapeDtypeStruct((tokens, wide), BF16), jax.ShapeDtypeStruct((tokens, N_HEADS * NOPE), BF16),
                   jax.ShapeDtypeStruct((8, D_MODEL), F32)),
        in_specs=[_rows(tm, wide), _rows(tm, wide), _rows(tm, DIL_WIDTH), _rows(tm, DIL_WIDTH), _rows(tm, DIL_WIDTH),
                  _rows(tm, DIL_WIDTH), _rows(tm, 2 * D_MODEL),
                  _rows(tm, D_MODEL), _rows(tm, LOW_W), _full(w_in_ext.shape), _full(w1.shape),
                  _full(wk.shape), _full(wv.shape), _full(g_q.shape), _full(g_kv.shape),
                  pl.BlockSpec((tm, LANES), lambda i: (i % ns, 1)), pl.BlockSpec((tm, LANES), lambda i: (i % ns, 1)),
                  pl.BlockSpec((tm, LANES), lambda i: (i % ns, 0))],
        out_specs=(_rows(tm, D_MODEL), _rows(tm, IN_EXT), _rows(tm, wide), _rows(tm, N_HEADS * NOPE), _full((8, D_MODEL))),
        compiler_params=_cp("arbitrary"),
    )(dqp, dkp, dvm, dq_d, dk_d, dv_d, dgates, dz1, low, w_in_ext, w1, wk, wv, g_q, g_kv, cext, sext, cs128)


def _wgrad(a, b, name, square_relu=False, by_shard=False):
    tokens, ka = a.shape
    n = b.shape[1]
    tka = min(ka, 512)
    shard = n // N_DEV
    tn = WGRAD_SHARDS * shard if by_shard else max(w for w in range(LANES, min(n, 2304) + 1, LANES) if n % w == 0)
    tt = min(tokens, 1024)
    nt = tokens // tt

    def body(a_ref, b_ref, o_ref, acc):
        kt = pl.program_id(2)

        @pl.when(kt == 0)
        def _():
            acc[...] = jnp.zeros_like(acc)

        at = a_ref[...]
        if square_relu:
            at = jnp.square(jnp.maximum(at.astype(F32), 0.0)).astype(BF16)
        acc[...] += _dot_tn(at, b_ref[...])

        @pl.when(kt == nt - 1)
        def _():
            if by_shard:
                for s in range(WGRAD_SHARDS):
                    o_ref[s] = acc[:, s * shard:(s + 1) * shard].astype(BF16)
            else:
                o_ref[...] = acc[...].astype(BF16)

    if by_shard:
        out_shape, out_spec = (N_DEV, ka, shard), pl.BlockSpec((WGRAD_SHARDS, tka, shard), lambda i, j, k: (j, i, 0))
    else:
        out_shape, out_spec = (ka, n), pl.BlockSpec((tka, tn), lambda i, j, k: (i, j))
    return pl.pallas_call(
        body, name=name, grid=(ka // tka, n // tn, nt), out_shape=jax.ShapeDtypeStruct(out_shape, BF16),
        in_specs=[pl.BlockSpec((tt, tka), lambda i, j, k: (k, i)), pl.BlockSpec((tt, tn), lambda i, j, k: (k, j))],
        out_specs=out_spec,
        scratch_shapes=[pltpu.VMEM((tka, tn), F32)],
        compiler_params=_cp("parallel", "parallel", "arbitrary"),
    )(a, b)


def _adam_math(w, g, m, v):
    m = ADAM_B1 * m + (1.0 - ADAM_B1) * g
    v = ADAM_B2 * v + (1.0 - ADAM_B2) * jnp.square(g)
    m_hat = m / (1.0 - ADAM_B1 ** ADAM_STEP)
    v_hat = v / (1.0 - ADAM_B2 ** ADAM_STEP)
    return -ADAM_LR * (m_hat / (jnp.sqrt(v_hat) + ADAM_EPS) + ADAM_WD * w), m, v


def _adamw(items, name, steps=None):
    if steps is None:
        steps = items[0][0].shape[0] // _row_tile(items[0][0].shape[0])
    n_items = len(items)

    def body(slot_ref, *refs):
        ins, outs = refs[:5 * n_items], refs[5 * n_items:]
        for k, (_, _, _, _, parts) in enumerate(items):
            w_ref, m_ref, v_ref, own_ref, p_ref = ins[5 * k:5 * k + 5]
            g_ref, d_ref, nm_ref, nv_ref = outs[4 * k:4 * k + 4]
            g = own_ref[...].astype(F32)
            for d in range(parts.shape[0]):
                g = g + p_ref[d].astype(F32)
            g_ref[...] = g
            d_ref[...], nm_ref[...], nv_ref[...] = _adam_math(w_ref[...], g, m_ref[...], v_ref[...])

    x, y, c = _place()
    in_specs, out_specs, out_shape, args = [], [], [], []
    for w, m, v, own, parts in items:
        rows, cols = w.shape
        tr = rows // steps
        blk = pl.BlockSpec((tr, cols), lambda i, slot: (i, 0))
        own_blk = blk if own.ndim == 2 else pl.BlockSpec((None, tr, cols), lambda i, slot: (slot[0], i, 0))
        in_specs += [blk, blk, blk, own_blk, pl.BlockSpec((parts.shape[0], tr, cols), lambda i, slot: (0, i, 0))]
        out_specs += [blk] * 4
        out_shape += [jax.ShapeDtypeStruct((rows, cols), F32)] * 4
        args += [w, m, v, own, parts]
    out = pl.pallas_call(
        body, name=name,
        grid_spec=pltpu.PrefetchScalarGridSpec(num_scalar_prefetch=1, grid=(steps,), in_specs=in_specs, out_specs=out_specs),
        out_shape=out_shape, compiler_params=_cp("parallel"),
    )(jnp.reshape(4 * x + 2 * y + c, (1,)).astype(I32), *args)
    return [tuple(out[4 * k:4 * k + 4]) for k in range(n_items)]


def _adamw_small(parts, w, m, v):
    _, rows, cols = parts.shape

    def body(p_ref, w_ref, m_ref, v_ref, g_ref, d_ref, nm_ref, nv_ref):
        g = p_ref[0]
        for d in range(1, N_DEV):
            g = g + p_ref[d]
        g_ref[...] = g
        d_ref[...], nm_ref[...], nv_ref[...] = _adam_math(w_ref[...], g, m_ref[...], v_ref[...])

    return pl.pallas_call(
        body, name="adamw_replicated", out_shape=(jax.ShapeDtypeStruct((rows, cols), F32),) * 4,
        in_specs=[_full(parts.shape)] + [_full((rows, cols))] * 3, out_specs=(_full((rows, cols)),) * 4, grid=(1,),
        compiler_params=_cp("arbitrary"),
    )(parts, w, m, v)


def _pad_rows(a2d, mult):
    pad = (-a2d.shape[-2]) % mult
    return jnp.pad(a2d, [(0, 0)] * (a2d.ndim - 2) + [(0, pad), (0, 0)]) if pad else a2d


def _pad_cols(a):
    pad = (-a.shape[-1]) % LANES
    return jnp.pad(a, [(0, 0)] * (a.ndim - 1) + [(0, pad)]) if pad else a


def _rot_cols(w):
    half = ROPE // 2
    return jnp.concatenate([-w[..., half:], w[..., :half]], axis=-1)


def _unrot_cols(dw):
    half = ROPE // 2
    return jnp.concatenate([dw[..., half:], -dw[..., :half]], axis=-1)


def _from_col_shards(stacked):
    return stacked.transpose(1, 0, 2).reshape(stacked.shape[1], -1)


def _to_col_shards(full):
    r = full.shape[0]
    return full.reshape(r, N_DEV, -1).transpose(1, 0, 2)


def _rope_tables(seq):
    half = ROPE // 2
    inv = jnp.power(ROPE_THETA, -jnp.arange(half, dtype=F32) / half)
    ang = jnp.arange(seq, dtype=F32)[:, None] * inv[None, :]
    cos = jnp.concatenate([jnp.cos(ang)] * 2, axis=1)
    sin = jnp.concatenate([jnp.sin(ang)] * 2, axis=1)
    ones, zeros = jnp.ones((seq, 2 * NOPE), F32), jnp.zeros((seq, 2 * NOPE), F32)
    pad = jnp.zeros((seq, PAIR_W - 2 * NOPE - 2 * ROPE), F32)
    cext = jnp.concatenate([ones, cos, cos, pad], axis=1)
    sext = jnp.concatenate([zeros, sin, sin, pad], axis=1)
    cs128 = jnp.concatenate([cos, sin, jnp.zeros((seq, LANES - 2 * ROPE), F32)], axis=1)
    return cext, sext, cs128


def _pair_slabs(nope, rope):
    k = nope.shape[0]
    nope = nope.reshape(k, N_PAIRS, 2 * NOPE)
    rope = jnp.zeros((k, N_PAIRS, 2 * ROPE), nope.dtype) if rope is None else rope.reshape(k, N_PAIRS, 2 * ROPE)
    pad = jnp.zeros((k, N_PAIRS, PAIR_W - 2 * NOPE - 2 * ROPE), nope.dtype)
    return jnp.concatenate([nope, rope, pad], axis=2).reshape(k, N_PAIRS * PAIR_W)


def _split_slabs(slabs):
    k = slabs.shape[0]
    s = slabs.reshape(k, N_PAIRS, PAIR_W)
    return s[:, :, :2 * NOPE].reshape(k, N_HEADS, NOPE), s[:, :, 2 * NOPE:2 * NOPE + 2 * ROPE].reshape(k, N_HEADS, ROPE)


def kernel(x, w_in, b_gate, g_q_a, w_uq, g_kv_a, w_ukv, w_o_mla, w_o_dil, w_out, ln1_g, ln1_b, w_ff1, w_ff2, ln2_g, ln2_b, loss_target, m_w_in, m_b_gate, m_g_q_a, m_w_uq, m_g_kv_a, m_w_ukv, m_w_o_mla, m_w_o_dil, m_w_out, m_ln1_g, m_ln1_b, m_w_ff1, m_w_ff2, m_ln2_g, m_ln2_b, v_w_in, v_b_gate, v_g_q_a, v_w_uq, v_g_kv_a, v_w_ukv, v_w_o_mla, v_w_o_dil, v_w_out, v_ln1_g, v_ln1_b, v_w_ff1, v_w_ff2, v_ln2_g, v_ln2_b):
    batch, seq, _ = x.shape
    tokens = batch * seq
    weights = dict(w_in=w_in, w_uq=w_uq, w_ukv=w_ukv, w_o_mla=w_o_mla, w_o_dil=w_o_dil, w_out=w_out, w_ff1=w_ff1, w_ff2=w_ff2, b_gate=b_gate)
    mom_m = dict(w_in=m_w_in, w_uq=m_w_uq, w_ukv=m_w_ukv, w_o_mla=m_w_o_mla, w_o_dil=m_w_o_dil, w_out=m_w_out, w_ff1=m_w_ff1, w_ff2=m_w_ff2, b_gate=m_b_gate)
    mom_v = dict(w_in=v_w_in, w_uq=v_w_uq, w_ukv=v_w_ukv, w_o_mla=v_w_o_mla, w_o_dil=v_w_o_dil, w_out=v_w_out, w_ff1=v_w_ff1, w_ff2=v_w_ff2, b_gate=v_b_gate)

    first = ["w_in", "w_uq", "w_ukv"]
    widths = [weights[n].shape[2] for n in first]
    shards = [_pad_cols(weights[n][0].astype(BF16)) for n in first]
    g_in, g_uq, g_ukv = _run_comm(_Gather(shards), shards, "all_gather_first_weights")
    g_uq, g_ukv = g_uq[:, :, :widths[1]], g_ukv[:, :, :widths[2]]

    s1, s2, n_in = Q_LORA + KV_LORA, Q_LORA + KV_LORA + ROPE, N_DEV * widths[0]

    def w_in_cols(lo, hi):
        out = []
        while lo < hi:
            d, off = divmod(lo, widths[0])
            take = min(hi - lo, widths[0] - off)
            out.append(g_in[d][:, off:off + take])
            lo += take
        return out

    w_in_ext = jnp.concatenate(w_in_cols(0, s2) + [_rot_cols(jnp.concatenate(w_in_cols(s1, s2), axis=1)),
                                                   jnp.zeros((D_MODEL, LOW_W - s2 - ROPE), BF16)] + w_in_cols(s2, n_in), axis=1)
    uq = _from_col_shards(g_uq).reshape(Q_LORA, N_HEADS, NOPE + ROPE)
    w1 = _pair_slabs(uq[:, :, :NOPE], uq[:, :, NOPE:])
    ukv = _from_col_shards(g_ukv).reshape(KV_LORA, N_HEADS, NOPE + HEAD_V)
    wk = ukv[:, :, :NOPE].reshape(KV_LORA, N_HEADS * NOPE)
    wv = ukv[:, :, NOPE:].reshape(KV_LORA, N_HEADS * HEAD_V)
    cext, sext, cs128 = _rope_tables(seq)
    dil_bias = _dilated_bias_table(seq)
    no_bias = jnp.zeros((1, 8, LANES), F32)

    x2 = x.reshape(tokens, D_MODEL)
    low, gates, qkvd, qp, kp, vm, qn, kvn, xb = _fwd_proj(x2, w_in_ext, w1, wk, wv, g_q_a, g_kv_a, cext, sext, cs128, seq=seq)
    bg = b_gate[0]
    bg_hi = bg.astype(BF16)
    bg_lo = (bg - bg_hi.astype(F32)).astype(BF16)
    later = [weights[n][0].astype(BF16) for n in ("w_o_mla", "w_o_dil", "w_out", "w_ff1", "w_ff2")]
    later.append(_pad_rows(jnp.concatenate([bg_hi, bg_lo], axis=0), 16))
    mla = dict(batch=batch, seq=seq, width=PAIR_W, col0=(0, 0, 0), dilated=False, scale=MLA_SCALE)
    dil = dict(batch=batch, seq=seq, width=LANES, col0=(0, N_PAIRS, 2 * N_PAIRS), dilated=True, scale=DIL_SCALE)
    o_a, lse_a, g_oa, g_ob, g_out, g_ff1, g_ff2, g_bg = _attn_fwd(
        qp, kp, vm, no_bias, name="mla_attention_fwd", comm=_Gather(later), comm_arrays=later, **mla)
    o_b, lse_b = _attn_fwd(qkvd, qkvd, qkvd, dil_bias, name="dilated_attention_fwd", **dil)
    w_oa, w_ob = _from_col_shards(g_oa), _from_col_shards(g_ob)
    w_out_full = g_out.reshape(D_MODEL, D_MODEL)
    w_ff2_full = g_ff2.reshape(D_FF, D_MODEL)
    bg_parts = g_bg.astype(F32)
    b_gate_full = _from_col_shards(bg_parts[:, 0:2] + bg_parts[:, 2:4])
    hb, xhat1, rstd1, y_a, y_b, mix = _fwd_mix(o_a, o_b, gates, x2, b_gate_full, w_oa, w_ob, w_out_full, ln1_g, ln1_b, seq=seq)
    u, dz2, dz2b, stat2 = _fwd_mlp(hb, xhat1, loss_target.reshape(tokens, D_MODEL), g_ff1, w_ff2_full, ln1_g, ln1_b, ln2_g, ln2_b, seq=seq)

    du, dz1, dz1b, stat1 = _bwd_mlp(dz2, dz2b, u, xhat1, rstd1, g_ff1, w_ff2_full, ln1_g, seq=seq)
    dw_ff = [_wgrad(hb, du, "wgrad_ff1", by_shard=True),
             _wgrad(u, dz2b, "wgrad_ff2", square_relu=True).reshape(N_DEV, FF_SHARD, D_MODEL)]
    dgates, dy_a, dy_b, do_a, do_b, stat_g = _bwd_mix(dz1b, gates, y_a, y_b, b_gate_full, w_oa, w_ob, w_out_full, seq=seq)
    dqp, dkp, dvm, r_ff1, r_ff2 = _attn_bwd(qp, kp, vm, o_a, do_a, lse_a, no_bias, name="mla_attention_bwd",
                                            comm=_Scatter(dw_ff), comm_arrays=dw_ff, **mla)
    dw_mid = [_to_col_shards(_wgrad(o_a, dy_a, "wgrad_o_mla")), _to_col_shards(_wgrad(o_b, dy_b, "wgrad_o_dil")),
              _wgrad(mix, dz1b, "wgrad_out").reshape(N_DEV, D_MODEL // N_DEV, D_MODEL),
              _pad_rows(_to_col_shards(stat_g[0:2]).astype(BF16), 16)]
    dq_d, dk_d, dv_d, r_oa, r_ob, r_out, r_bg = _attn_bwd(qkvd, qkvd, qkvd, o_b, do_b, lse_b, dil_bias, name="dilated_attention_bwd",
                                                          comm=_Scatter(dw_mid), comm_arrays=dw_mid, **dil)
    grad_x, dproj, d_a, dkn, stat_r = _bwd_proj(dqp, dkp, dvm, dq_d, dk_d, dv_d, dgates, dz1, low, w_in_ext, w1, wk, wv,
                                                g_q_a, g_kv_a, cext, sext, cs128, seq=seq)

    dw_in_ext = _wgrad(xb, dproj, "wgrad_in")
    dw1 = _wgrad(qn, d_a, "wgrad_uq")
    dwk = _wgrad(kvn, dkn, "wgrad_ukv_k")
    dwv = _wgrad(kvn, dvm, "wgrad_ukv_v")
    dw_kr = dw_in_ext[:, s1:s2] + _unrot_cols(dw_in_ext[:, s2:s2 + ROPE])

    def dw_in_cols(lo, hi):
        out = []
        for a, b, piece in ((0, s1, lambda u, v: dw_in_ext[:, u:v]), (s1, s2, lambda u, v: dw_kr[:, u - s1:v - s1]),
                            (s2, n_in, lambda u, v: dw_in_ext[:, u + LOW_W - s2:v + LOW_W - s2])):
            if max(lo, a) < min(hi, b):
                out.append(piece(max(lo, a), min(hi, b)))
        return out

    dw_in = jnp.stack([_pad_cols(jnp.concatenate(dw_in_cols(d * widths[0], (d + 1) * widths[0]), axis=1)) for d in range(N_DEV)])
    n1, r1 = _split_slabs(dw1)
    dw_uq = jnp.concatenate([n1, r1], axis=2).reshape(Q_LORA, N_HEADS * (NOPE + ROPE))
    dw_ukv = jnp.concatenate([dwk.reshape(KV_LORA, N_HEADS, NOPE), dwv.reshape(KV_LORA, N_HEADS, HEAD_V)], axis=2).reshape(KV_LORA, N_HEADS * (NOPE + HEAD_V))
    last = [dw_in] + [_pad_cols(_to_col_shards(dw)) for dw in (dw_uq, dw_ukv)]
    theirs = _rs_sibling(last, "rs_last_sibling_exchange")
    sums = [_pair_sum(a, b, "rs_last_pair_sum_" + n) for a, b, n in zip(last, theirs, first)]
    partial = jnp.concatenate([stat_r[0:1, :Q_LORA], stat_r[1:2, :KV_LORA], stat1[0:1], stat1[1:2], stat2[0:1], stat2[1:2],
                               stat2[2:3, :LANES]], axis=1)
    partial = _pad_rows(partial.reshape(-1, LANES), 8)
    rest = [s[1] for s in sums]
    got_in, got_uq, got_ukv, every = _run_comm(_Plans([_ChipExchange(rest), _Gather([partial])]), rest + [partial],
                                               "rs_last_chip_exchange")

    upd = {}
    early = ["w_ff1", "w_ff2", "w_out", "w_o_mla", "w_o_dil"]
    items = [(weights[n][0], mom_m[n][0], mom_v[n][0], own, parts) for n, own, parts in
             zip(early, (dw_ff[0], dw_ff[1], dw_mid[2], dw_mid[0], dw_mid[1]), (r_ff1, r_ff2, r_out, r_oa, r_ob))]
    upd.update(zip(early, _adamw(items, "adamw_early_weights", steps=4)))
    for n, w, (own, _), parts in zip(first, widths, sums, (got_in, got_uq, got_ukv)):
        (upd[n],) = _adamw([(weights[n][0], mom_m[n][0], mom_v[n][0], own[:, :w], parts[:, :, :w])], "adamw_" + n)
    (bg_upd,) = _adamw([(_pad_rows(b_gate[0], 16), _pad_rows(m_b_gate[0], 16), _pad_rows(v_b_gate[0], 16), dw_mid[3], r_bg)],
                       "adamw_b_gate")
    upd["b_gate"] = tuple(t[0:2] for t in bg_upd)

    small_w = [g_q_a, g_kv_a, ln1_g, ln1_b, ln2_g, ln2_b]
    small_m = [m_g_q_a, m_g_kv_a, m_ln1_g, m_ln1_b, m_ln2_g, m_ln2_b]
    small_v = [v_g_q_a, v_g_kv_a, v_ln1_g, v_ln1_b, v_ln2_g, v_ln2_b]
    small_widths = [a.shape[1] for a in small_w]

    def as_rows(vecs, extra):
        flat = jnp.concatenate(vecs + [jnp.zeros((1, extra), F32)], axis=1)
        return _pad_rows(flat.reshape(-1, LANES), 8)

    g_s, d_s, nm_s, nv_s = _adamw_small(every, as_rows(small_w, LANES), as_rows(small_m, LANES), as_rows(small_v, LANES))

    def split_small(a):
        flat = a.reshape(1, -1)
        out, c0 = [], 0
        for w in small_widths:
            out.append(flat[:, c0:c0 + w])
            c0 += w
        return out, flat[0, c0]

    g_small, loss = split_small(g_s)
    small = [g_small, split_small(d_s)[0], split_small(nm_s)[0], split_small(nv_s)[0]]

    order = ["w_in", "b_gate", "g_q_a", "w_uq", "g_kv_a", "w_ukv", "w_o_mla", "w_o_dil", "w_out", "ln1_g", "ln1_b", "w_ff1", "w_ff2", "ln2_g", "ln2_b"]
    small_names = ["g_q_a", "g_kv_a", "ln1_g", "ln1_b", "ln2_g", "ln2_b"]

    def pick(kind):
        return [small[kind][small_names.index(n)] if n in small_names else upd[n][kind][None] for n in order]

    return (loss, grad_x.reshape(batch, seq, D_MODEL), *pick(0), *pick(1), *pick(2), *pick(3))
```

```python
import functools
import math

import jax
import jax.numpy as jnp
from jax import lax
from jax.experimental import pallas as pl
from jax.experimental.pallas import tpu as pltpu

F32 = jnp.float32
BF16 = jnp.bfloat16
I32 = jnp.int32

D_MODEL = 1024
N_HEADS = 8
NOPE = 64
ROPE = 32
HEAD_V = 64
Q_LORA = 384
KV_LORA = 256
DIL_WIDTH = 512
D_FF = 4096
ROPE_THETA = 10000.0
LN_EPS = 1e-5
RMS_EPS = 1e-6
NEG = -1e30
ALPHA = 2.0 ** 0.25
MLA_SCALE = (NOPE + ROPE) ** -0.5
DIL_SCALE = 64 ** -0.5
ADAM_LR, ADAM_B1, ADAM_B2, ADAM_EPS, ADAM_WD, ADAM_STEP = 0.001, 0.9, 0.999, 1e-08, 0.01, 10

LANES = 128
PAIR_W = 256
N_PAIRS = N_HEADS // 2
LOW_W = 768
IN_EXT = LOW_W + 3 * DIL_WIDTH + 2 * D_MODEL
N_DEV = 8
FF_SHARD = D_FF // N_DEV
FF_STEP = 4
WGRAD_SHARDS = 4
TOKEN_TILE = 256
MIX_TILE = 512
ATTN_TILE = 256
VMEM_LIMIT = 56 << 20

MESH = pl.DeviceIdType.MESH
ANY = pl.BlockSpec(memory_space=pl.ANY)
CHIP_FLIPS = ((0, 0), (0, 1), (1, 0), (1, 1))
PEER_FLIPS = tuple((fx, fy, fc) for fx in (0, 1) for fy in (0, 1) for fc in (0, 1))[1:]


def _cp(*sem):
    return pltpu.CompilerParams(dimension_semantics=sem or None, vmem_limit_bytes=VMEM_LIMIT)


def _full(shape):
    nd = len(shape)
    return pl.BlockSpec(shape, lambda *_: (0,) * nd)


def _rows(tm, width):
    return pl.BlockSpec((tm, width), lambda i, *_: (i, 0))


def _dot(a, b):
    return jnp.dot(a, b, preferred_element_type=F32)


def _dot_nt(a, b):
    return lax.dot_general(a, b, (((1,), (1,)), ((), ())), preferred_element_type=F32)


def _dot_tn(a, b):
    return lax.dot_general(a, b, (((0,), (0,)), ((), ())), preferred_element_type=F32)


def _sigmoid(z):
    return 1.0 / (1.0 + jnp.exp(-z))


def _place():
    return lax.axis_index("x"), lax.axis_index("y"), lax.axis_index("c")


def _flip(v, f):
    return 1 - v if f else v


class _Gather:
    def __init__(self, shards):
        self.n = len(shards)
        self.out_shape = [jax.ShapeDtypeStruct((N_DEV, *s.shape), s.dtype) for s in shards]
        self.scratch = [pltpu.SemaphoreType.DMA((7 * self.n,)), pltpu.SemaphoreType.DMA((7 * self.n,)),
                        pltpu.SemaphoreType.DMA((self.n,))]

    def _copies(self, what, srcs, dsts, send, recv, local):
        x, y, c = _place()
        chips = [(_flip(x, fx), _flip(y, fy)) for fx, fy in CHIP_FLIPS[1:]]
        out = []
        for a in range(self.n):
            def slot(px, py, pc, a=a):
                return dsts[a].at[4 * px + 2 * py + pc]

            def copy(k, block, to, src=None, a=a, slot=slot):
                return pltpu.make_async_remote_copy(
                    src_ref=slot(*block) if src is None else src, dst_ref=slot(*block),
                    send_sem=send.at[7 * a + k], recv_sem=recv.at[7 * a + k], device_id=to, device_id_type=MESH)

            if what == "mine":
                out.append(pltpu.make_async_copy(srcs[a], slot(x, y, c), local.at[a]))
            elif what == "first":
                out.append(copy(0, (x, y, c), (x, y, 1 - c), src=srcs[a]))
                out += [copy(1 + j, (x, y, c), (*chip, c), src=srcs[a]) for j, chip in enumerate(chips)]
            elif what == "landed":
                out += [copy(1 + j, (*chip, c), (x, y, c)) for j, chip in enumerate(chips)]
            elif what == "passed":
                out += [copy(4 + j, (*chip, c), (x, y, 1 - c)) for j, chip in enumerate(chips)]
            else:
                out.append(copy(0, (x, y, 1 - c), (x, y, c)))
                out += [copy(4 + j, (*chip, 1 - c), (x, y, c)) for j, chip in enumerate(chips)]
        return out

    def start(self, *refs):
        for cp in self._copies("first", *refs) + self._copies("mine", *refs):
            cp.start()

    def forward(self, *refs):
        for landed, passed in zip(self._copies("landed", *refs), self._copies("passed", *refs)):
            landed.wait_recv()
            passed.start()

    def finish(self, *refs):
        for cp in self._copies("from_sibling", *refs):
            cp.wait_recv()
        for cp in self._copies("first", *refs) + self._copies("passed", *refs):
            cp.wait_send()
        for cp in self._copies("mine", *refs):
            cp.wait()


class _Scatter:
    def __init__(self, arrays):
        self.n = len(arrays)
        self.out_shape = [jax.ShapeDtypeStruct((7, *a.shape[1:]), a.dtype) for a in arrays]
        self.scratch = [pltpu.SemaphoreType.DMA((7 * self.n,)), pltpu.SemaphoreType.DMA((7 * self.n,))]

    def _copies(self, srcs, dsts, send, recv):
        x, y, c = _place()
        out = []
        for a in range(self.n):
            for k, (fx, fy, fc) in enumerate(PEER_FLIPS):
                px, py, pc = _flip(x, fx), _flip(y, fy), _flip(c, fc)
                out.append(pltpu.make_async_remote_copy(
                    src_ref=srcs[a].at[4 * px + 2 * py + pc], dst_ref=dsts[a].at[k],
                    send_sem=send.at[7 * a + k], recv_sem=recv.at[7 * a + k], device_id=(px, py, pc), device_id_type=MESH))
        return out

    def start(self, *refs):
        for cp in self._copies(*refs):
            cp.start()

    def forward(self, *refs):
        pass

    def finish(self, *refs):
        for cp in self._copies(*refs):
            cp.wait_send()
        for cp in self._copies(*refs):
            cp.wait_recv()


class _ChipExchange:
    def __init__(self, arrays):
        self.n = len(arrays)
        self.out_shape = [jax.ShapeDtypeStruct(a.shape, a.dtype) for a in arrays]
        self.scratch = [pltpu.SemaphoreType.DMA((3 * self.n,)), pltpu.SemaphoreType.DMA((3 * self.n,))]

    def _copies(self, srcs, dsts, send, recv):
        x, y, c = _place()
        return [pltpu.make_async_remote_copy(
            src_ref=srcs[a].at[k], dst_ref=dsts[a].at[k], send_sem=send.at[3 * a + k], recv_sem=recv.at[3 * a + k],
            device_id=(_flip(x, fx), _flip(y, fy), c), device_id_type=MESH)
            for a in range(self.n) for k, (fx, fy) in enumerate(CHIP_FLIPS[1:])]

    def start(self, *refs):
        for cp in self._copies(*refs):
            cp.start()

    def forward(self, *refs):
        pass

    def finish(self, *refs):
        for cp in self._copies(*refs):
            cp.wait_send()
        for cp in self._copies(*refs):
            cp.wait_recv()


class _Plans:
    def __init__(self, plans):
        self.plans = plans
        self.n = sum(p.n for p in plans)
        self.out_shape = [s for p in plans for s in p.out_shape]
        self.scratch = [s for p in plans for s in p.scratch]

    def _each(self, phase, srcs, dsts, *sems):
        i0 = s0 = 0
        for p in self.plans:
            getattr(p, phase)(srcs[i0:i0 + p.n], dsts[i0:i0 + p.n], *sems[s0:s0 + len(p.scratch)])
            i0, s0 = i0 + p.n, s0 + len(p.scratch)

    def start(self, *refs):
        self._each("start", *refs)

    def forward(self, *refs):
        self._each("forward", *refs)

    def finish(self, *refs):
        self._each("finish", *refs)


def _run_comm(comm, arrays, name):
    n = comm.n

    def body(*refs):
        args = (refs[:n], refs[n:2 * n], *refs[2 * n:])
        comm.start(*args)
        comm.forward(*args)
        comm.finish(*args)

    return pl.pallas_call(body, name=name, out_shape=comm.out_shape, in_specs=[ANY] * n, out_specs=[ANY] * n,
                          scratch_shapes=comm.scratch)(*arrays)


def _rs_sibling(arrays, name):
    n = len(arrays)

    def body(*refs):
        srcs, got, (send, recv) = refs[:n], refs[n:2 * n], refs[2 * n:]
        x, y, c = _place()
        copies = []
        for a in range(n):
            for r, (fx, fy) in enumerate(CHIP_FLIPS):
                chip = 2 * _flip(x, fx) + _flip(y, fy)
                copies.append(pltpu.make_async_remote_copy(
                    src_ref=srcs[a].at[2 * chip + 1 - c], dst_ref=got[a].at[r], send_sem=send.at[4 * a + r],
                    recv_sem=recv.at[4 * a + r], device_id=(x, y, 1 - c), device_id_type=MESH))
        for cp in copies:
            cp.start()
        for cp in copies:
            cp.wait_send()
        for cp in copies:
            cp.wait_recv()

    return pl.pallas_call(
        body, name=name, out_shape=[jax.ShapeDtypeStruct((4, *a.shape[1:]), a.dtype) for a in arrays],
        in_specs=[ANY] * n, out_specs=[ANY] * n,
        scratch_shapes=[pltpu.SemaphoreType.DMA((4 * n,)), pltpu.SemaphoreType.DMA((4 * n,))],
    )(*arrays)


def _chip_slots():
    x, y, c = _place()
    return jnp.stack([4 * _flip(x, fx) + 2 * _flip(y, fy) + c for fx, fy in CHIP_FLIPS]).astype(I32)


def _tiles(rows, cols, steps=4):
    if rows % (16 * steps) == 0:
        return steps, (rows // steps, cols), lambda i: (i, 0)
    if cols % (LANES * steps) == 0:
        return steps, (rows, cols // steps), lambda i: (0, i)
    return 1, (rows, cols), lambda i: (0, 0)


def _pair_sum(full, theirs, name):
    _, rows, cols = theirs.shape
    steps, tile, at = _tiles(rows, cols)

    def body(slots_ref, m0_ref, m1_ref, m2_ref, m3_ref, b_ref, own_ref, rest_ref):
        own_ref[...] = m0_ref[...].astype(F32) + b_ref[0].astype(F32)
        for k, m_ref in enumerate((m1_ref, m2_ref, m3_ref)):
            rest_ref[k] = (m_ref[...].astype(F32) + b_ref[k + 1].astype(F32)).astype(BF16)

    def mine(k):
        return pl.BlockSpec((None, *tile), lambda i, slots: (slots[k], *at(i)))

    return pl.pallas_call(
        body, name=name,
        grid_spec=pltpu.PrefetchScalarGridSpec(
            num_scalar_prefetch=1, grid=(steps,),
            in_specs=[mine(0), mine(1), mine(2), mine(3), pl.BlockSpec((4, *tile), lambda i, slots: (0, *at(i)))],
            out_specs=(pl.BlockSpec(tile, lambda i, slots: at(i)), pl.BlockSpec((3, *tile), lambda i, slots: (0, *at(i))))),
        out_shape=(jax.ShapeDtypeStruct((rows, cols), F32), jax.ShapeDtypeStruct((3, rows, cols), BF16)),
        compiler_params=_cp("parallel"),
    )(_chip_slots(), full, full, full, full, theirs)


def _head_lanes(width, h):
    lane = lax.broadcasted_iota(I32, (1, width), 1)
    if width == LANES:
        return (lane >= 64 * h) & (lane < 64 * h + 64)
    nope = (lane >= NOPE * h) & (lane < NOPE * h + NOPE)
    rope = (lane >= 2 * NOPE + ROPE * h) & (lane < 2 * NOPE + ROPE * h + ROPE)
    return nope | rope


def _dilated_bias_table(seq):
    t = min(ATTN_TILE, seq)
    nd = seq // t

    def body(o_ref):
        delta = pl.program_id(0) * t + lax.broadcasted_iota(I32, (t, t), 1) - lax.broadcasted_iota(I32, (t, t), 0)
        mult = ((delta <= 128).astype(I32) + (((delta & 3) == 0) & (delta <= 512)).astype(I32)
                + ((delta & 15) == 0).astype(I32))
        logm = jnp.where(mult == 3, math.log(3.0), jnp.where(mult == 2, math.log(2.0), 0.0))
        valid = (delta >= 0) & (mult > 0)
        dist = delta.astype(F32)
        for h in range(N_HEADS):
            o_ref[h] = jnp.where(valid, logm - 2.0 ** (-(h + 1)) * dist, NEG)

    return pl.pallas_call(
        body, name="dilated_bias_table", grid=(nd,), out_shape=jax.ShapeDtypeStruct((N_HEADS, nd, t, t), F32),
        out_specs=pl.BlockSpec((N_HEADS, None, t, t), lambda d: (0, d, 0, 0)),
        compiler_params=_cp("parallel"),
    )()


def _comm_hooks(comm, refs, n_in, n_out):
    if comm is None:
        return refs[:n_in], refs[n_in:n_in + n_out], refs[n_in + n_out:], None
    n = comm.n
    ins, srcs = refs[:n_in], refs[n_in:n_in + n]
    outs, dsts = refs[n_in + n:n_in + n + n_out], refs[n_in + n + n_out:n_in + 2 * n + n_out]
    rest = refs[n_in + 2 * n + n_out:]
    own = len(rest) - len(comm.scratch)
    return ins, outs, rest[:own], (srcs, dsts, *rest[own:])


def _attn_fwd(q, k, v, bias, *, batch, seq, width, col0, dilated, scale, name, comm=None, comm_arrays=()):
    t = min(ATTN_TILE, seq)
    nq = seq // t
    cq, ck, cv = col0
    pre = scale if dilated else 1.0
    steps = batch * N_PAIRS

    def body(*refs):
        (q_ref, k_ref, v_ref, bias_ref), (o_ref, lse_ref), (v_heads,), plan = _comm_hooks(comm, refs, 4, 2)
        step_no = pl.program_id(0) * N_PAIRS + pl.program_id(1)
        if plan:
            pl.when(step_no == 0)(lambda: comm.start(*plan))
            pl.when(step_no == (3 * steps) // 4)(lambda: comm.forward(*plan))
        v_all = v_ref[...].astype(F32)
        for h in (0, 1):
            v_heads[h] = jnp.transpose(jnp.where(_head_lanes(LANES, h), v_all, 0.0)).astype(BF16)
        top = lax.broadcasted_iota(I32, (LANES, t), 0) < HEAD_V
        causal = lax.broadcasted_iota(I32, (t, t), 0) <= lax.broadcasted_iota(I32, (t, t), 1)
        lax.fori_loop(0, nq, functools.partial(query_tile, q_ref, k_ref, bias_ref, o_ref, lse_ref, v_heads, top, causal), 0)
        if plan:
            pl.when(step_no == steps - 1)(lambda: comm.finish(*plan))

    def query_tile(q_ref, k_ref, bias_ref, o_ref, lse_ref, v_heads, top, causal, i, _):
        qs = pl.multiple_of(i * t, t)
        q2 = q_ref[pl.ds(qs, t), :] * pre if dilated else q_ref[pl.ds(qs, t), :]
        qh = [jnp.where(_head_lanes(width, h), q2, jnp.zeros_like(q2)) for h in (0, 1)]

        def scores(j):
            kj = k_ref[pl.ds(pl.multiple_of(j * t, t), t), :]
            return [_dot_nt(kj, qh[h]) for h in (0, 1)]

        def step(j, carry, last):
            m0, l0, m1, l1, acc, s0, s1 = carry
            ahead = [] if last else scores(j + 1)
            ks = pl.multiple_of(j * t, t)
            new, alphas, pv = [], [], []
            for h, (m, l, s) in enumerate(((m0, l0, s0), (m1, l1, s1))):
                if dilated:
                    s = s + bias_ref[h, i - j]
                else:
                    s = s * scale
                    if last:
                        s = jnp.where(causal, s, NEG)
                m_new = jnp.maximum(m, jnp.max(s, axis=0, keepdims=True))
                a = jnp.exp(m - m_new)
                p = jnp.exp(s - m_new)
                new += [m_new, a * l + jnp.sum(p, axis=0, keepdims=True)]
                alphas.append(a)
                pv.append(_dot(v_heads[h, :, pl.ds(ks, t)], p.astype(BF16)))
            acc = jnp.where(top, alphas[0], alphas[1]) * acc + pv[0] + pv[1]
            return (*new, acc, *ahead)

        row = jnp.full((1, t), NEG, F32)
        zero = jnp.zeros((1, t), F32)
        init = (row, zero, row, zero, jnp.zeros((LANES, t), F32), *scores(0))
        m0, l0, m1, l1, acc = step(i, lax.fori_loop(0, i, functools.partial(step, last=False), init), True)
        o_ref[pl.ds(qs, t), :] = jnp.transpose(acc * jnp.where(top, 1.0 / l0, 1.0 / l1)).astype(BF16)
        r = lax.broadcasted_iota(I32, (8, t), 0)
        lse_ref[:, pl.ds(qs, t)] = jnp.where(r == 0, m0 + jnp.log(l0), jnp.where(r == 1, m1 + jnp.log(l1), 0.0))
        return 0

    bias_spec = (pl.BlockSpec((2, nq, t, t), lambda b, p: (p, 0, 0, 0)) if dilated
                 else pl.BlockSpec((None, 8, LANES), lambda b, p: (0, 0, 0)))
    n = comm.n if comm else 0
    return pl.pallas_call(
        body, name=name, grid=(batch, N_PAIRS),
        out_shape=[jax.ShapeDtypeStruct((batch * seq, DIL_WIDTH), BF16), jax.ShapeDtypeStruct((batch * N_PAIRS, 8, seq), F32)]
        + (comm.out_shape if comm else []),
        in_specs=[pl.BlockSpec((seq, width), lambda b, p: (b, cq + p)),
                  pl.BlockSpec((seq, width), lambda b, p: (b, ck + p)),
                  pl.BlockSpec((seq, LANES), lambda b, p: (b, cv + p)),
                  bias_spec] + [ANY] * n,
        out_specs=[pl.BlockSpec((seq, LANES), lambda b, p: (b, p)),
                   pl.BlockSpec((None, 8, seq), lambda b, p: (b * N_PAIRS + p, 0, 0))] + [ANY] * n,
        scratch_shapes=[pltpu.VMEM((2, LANES, seq), BF16)] + (comm.scratch if comm else []),
        compiler_params=_cp("arbitrary", "arbitrary") if comm else _cp("parallel", "parallel"),
    )(q, k, v, bias, *comm_arrays)


def _attn_bwd(q, k, v, o, do, lse, bias, *, batch, seq, width, col0, dilated, scale, name, comm=None, comm_arrays=()):
    t = min(ATTN_TILE, seq)
    nq = seq // t
    cq, ck, cv = col0
    pre = scale if dilated else 1.0
    dq_transposed = width == LANES
    steps = batch * N_PAIRS

    def body(*refs):
        ins, (dq_ref, dk_ref, dv_ref), (dq_acc, dk_acc, dv_acc, rowdot, q_heads, do_heads), plan = _comm_hooks(comm, refs, 7, 3)
        q_ref, k_ref, v_ref, o_ref, do_ref, lse_ref, bias_ref = ins
        step_no = pl.program_id(0) * N_PAIRS + pl.program_id(1)
        if plan:
            pl.when(step_no == 0)(lambda: comm.start(*plan))
        wlane = [_head_lanes(width, h) for h in (0, 1)]
        vlane = [_head_lanes(LANES, h) for h in (0, 1)]
        causal = lax.broadcasted_iota(I32, (t, t), 0) <= lax.broadcasted_iota(I32, (t, t), 1)
        q_all = q_ref[...] * pre if dilated else q_ref[...]
        for h in (0, 1):
            q_heads[h] = jnp.where(wlane[h], q_all, jnp.zeros_like(q_all))
            do_heads[h] = jnp.where(vlane[h], do_ref[...], jnp.zeros_like(do_ref[...]))
        prod = jnp.transpose(do_ref[...].astype(F32) * o_ref[...].astype(F32))
        rowdot[0:1, :] = jnp.sum(prod[0:HEAD_V], axis=0, keepdims=True)
        rowdot[1:2, :] = jnp.sum(prod[HEAD_V:], axis=0, keepdims=True)
        dq_acc[...] = jnp.zeros_like(dq_acc)

        def k_tile(j, _):
            ks = pl.multiple_of(j * t, t)
            kj = k_ref[pl.ds(ks, t), :]
            vj = v_ref[pl.ds(ks, t), :]
            kh = [jnp.where(wlane[h], kj, jnp.zeros_like(kj)) for h in (0, 1)]
            if dq_transposed:
                kh = [jnp.transpose(kh[h].astype(F32)).astype(BF16) for h in (0, 1)]
            dk_acc[...] = jnp.zeros_like(dk_acc)
            dv_acc[...] = jnp.zeros_like(dv_acc)

            def operands(i):
                qs = pl.multiple_of(i * t, t)
                return [q_heads[h, pl.ds(qs, t), :] for h in (0, 1)], [do_heads[h, pl.ds(qs, t), :] for h in (0, 1)]

            def products(i):
                qih, doih = operands(i)
                scores = tuple(_dot_nt(kj, qih[h]) for h in (0, 1))
                return scores + tuple(_dot_nt(vj, doih[h]) for h in (0, 1)) if width > LANES else scores

            def q_tile(n, carry, last):
                i = nq - 1 - n
                ahead = () if last else products(i - 1)
                qs = pl.multiple_of(i * t, t)
                qih, doih = operands(i)
                s0, s1 = carry[:2]
                dps = carry[2:] if width > LANES else [_dot_nt(vj, doih[h]) for h in (0, 1)]
                dq_i = jnp.zeros((width, t) if dq_transposed else (t, width), F32)
                for h, (s, dp) in enumerate(((s0, dps[0]), (s1, dps[1]))):
                    if dilated:
                        s = s + bias_ref[h, i - j]
                    else:
                        s = s * scale
                        if last:
                            s = jnp.where(causal, s, NEG)
                    p = jnp.exp(s - lse_ref[h:h + 1, pl.ds(qs, t)])
                    ds = p * (dp - rowdot[h:h + 1, pl.ds(qs, t)])
                    ds = (ds if dilated else ds * scale).astype(BF16)
                    dv_acc[...] += _dot(p.astype(BF16), doih[h])
                    dk_acc[...] += _dot(ds, qih[h])
                    dq_i = dq_i + (_dot(kh[h], ds) if dq_transposed else _dot_tn(ds, kh[h]))
                if dq_transposed:
                    dq_acc[:, pl.ds(qs, t)] += dq_i
                else:
                    dq_acc[pl.ds(qs, t), :] += dq_i
                return ahead

            q_tile(nq - 1 - j, lax.fori_loop(0, nq - 1 - j, functools.partial(q_tile, last=False), products(nq - 1)), True)
            dk_ref[pl.ds(ks, t), :] = dk_acc[...].astype(BF16)
            dv_ref[pl.ds(ks, t), :] = dv_acc[...].astype(BF16)
            return 0

        lax.fori_loop(0, nq, k_tile, 0)
        dq_ref[...] = ((jnp.transpose(dq_acc[...]) if dq_transposed else dq_acc[...]) * pre).astype(BF16)
        if plan:
            pl.when(step_no == steps - 1)(lambda: comm.finish(*plan))

    tokens = batch * seq
    bias_spec = (pl.BlockSpec((2, nq, t, t), lambda b, p: (p, 0, 0, 0)) if dilated
                 else pl.BlockSpec((None, 8, LANES), lambda b, p: (0, 0, 0)))
    n = comm.n if comm else 0
    return pl.pallas_call(
        body, name=name, grid=(batch, N_PAIRS),
        out_shape=[jax.ShapeDtypeStruct((tokens, N_PAIRS * width), BF16), jax.ShapeDtypeStruct((tokens, N_PAIRS * width), BF16),
                   jax.ShapeDtypeStruct((tokens, DIL_WIDTH), BF16)] + (comm.out_shape if comm else []),
        in_specs=[pl.BlockSpec((seq, width), lambda b, p: (b, cq + p)),
                  pl.BlockSpec((seq, width), lambda b, p: (b, ck + p)),
                  pl.BlockSpec((seq, LANES), lambda b, p: (b, cv + p)),
                  pl.BlockSpec((seq, LANES), lambda b, p: (b, p)),
                  pl.BlockSpec((seq, LANES), lambda b, p: (b, p)),
                  pl.BlockSpec((None, 8, seq), lambda b, p: (b * N_PAIRS + p, 0, 0)),
                  bias_spec] + [ANY] * n,
        out_specs=[pl.BlockSpec((seq, width), lambda b, p: (b, p)),
                   pl.BlockSpec((seq, width), lambda b, p: (b, p)),
                   pl.BlockSpec((seq, LANES), lambda b, p: (b, p))] + [ANY] * n,
        scratch_shapes=[pltpu.VMEM((width, seq) if dq_transposed else (seq, width), F32),
                        pltpu.VMEM((t, width), F32), pltpu.VMEM((t, LANES), F32),
                        pltpu.VMEM((8, seq), F32), pltpu.VMEM((2, seq, width), BF16), pltpu.VMEM((2, seq, LANES), BF16)]
        + (comm.scratch if comm else []),
        compiler_params=_cp("arbitrary", "arbitrary") if comm else _cp("parallel", "parallel"),
    )(q, k, v, o, do, lse, bias, *comm_arrays)


def _rms(xf, g):
    r = lax.rsqrt(jnp.mean(xf * xf, axis=1, keepdims=True) + RMS_EPS)
    return xf * r * g, r


def _rms_bwd(dy, xf, r, g):
    gy = dy * g
    dx = r * gy - xf * (r * r * r) * jnp.mean(gy * xf, axis=1, keepdims=True)
    return dx, dy * xf * r


def _ln_bwd(dy, xhat, rstd, g):
    dxh = dy * g
    return rstd * (dxh - jnp.mean(dxh, axis=1, keepdims=True) - xhat * jnp.mean(dxh * xhat, axis=1, keepdims=True))


def _rope_slabs(q, cos, sin, transpose):
    first_half = (lax.broadcasted_iota(I32, (1, LANES), 1) % ROPE) < ROPE // 2
    out = []
    for p in range(N_PAIRS):
        blk = q[:, p * PAIR_W + LANES:(p + 1) * PAIR_W]
        y = blk * sin if transpose else blk
        up, down = pltpu.roll(y, LANES - ROPE // 2, 1), pltpu.roll(y, ROPE // 2, 1)
        rot = jnp.where(first_half, up, -down) if transpose else jnp.where(first_half, -up, down) * sin
        out += [q[:, p * PAIR_W:p * PAIR_W + LANES], blk * cos + rot]
    return jnp.concatenate(out, axis=1)


def _fwd_proj(x, w_in_ext, w1, wk, wv, g_q, g_kv, cext, sext, cs128, *, seq):
    tokens = x.shape[0]
    tm = min(TOKEN_TILE, seq)
    ns = seq // tm

    def body(x_ref, win_ref, w1_ref, wk_ref, wv_ref, gq_ref, gkv_ref, c_ref, s_ref, cs_ref,
             low_ref, gates_ref, qkvd_ref, qp_ref, kp_ref, vm_ref, qn_ref, kvn_ref, xb_ref):
        xt = x_ref[...].astype(BF16)
        xb_ref[...] = xt
        low = _dot_nt(xt, win_ref[0:LOW_W, :])
        low_ref[...] = low
        qkvd_ref[...] = _dot_nt(xt, win_ref[LOW_W:LOW_W + 3 * DIL_WIDTH, :]).astype(BF16)
        gates_ref[...] = _dot_nt(xt, win_ref[LOW_W + 3 * DIL_WIDTH:, :]).astype(BF16)
        qn = _rms(low[:, 0:Q_LORA], gq_ref[...])[0].astype(BF16)
        kvn = _rms(low[:, Q_LORA:Q_LORA + KV_LORA], gkv_ref[...])[0].astype(BF16)
        qn_ref[...] = qn
        kvn_ref[...] = kvn
        qp_ref[...] = _rope_slabs(_dot(qn, w1_ref[...]), c_ref[...], s_ref[...], False).astype(BF16)
        kr = low[:, Q_LORA + KV_LORA:] * cs_ref[...]
        kr = kr + pltpu.roll(kr, LANES - ROPE, 1)
        lane = lax.broadcasted_iota(I32, kr.shape, 1)
        kr = jnp.where(lane < ROPE, kr, 0.0)
        kr = (kr + pltpu.roll(kr, ROPE, 1)).astype(BF16)
        kn = _dot(kvn, wk_ref[...]).astype(BF16)
        kp_ref[...] = jnp.concatenate([blk for p in range(N_PAIRS) for blk in (kn[:, p * LANES:(p + 1) * LANES], kr)], axis=1)
        vm_ref[...] = _dot(kvn, wv_ref[...]).astype(BF16)

    n_gates = 2 * D_MODEL
    outs = [(LOW_W, F32), (n_gates, BF16), (3 * DIL_WIDTH, BF16), (N_PAIRS * PAIR_W, BF16), (N_PAIRS * PAIR_W, BF16),
            (DIL_WIDTH, BF16), (Q_LORA, BF16), (KV_LORA, BF16), (D_MODEL, BF16)]
    return pl.pallas_call(
        body, name="fwd_proj", grid=(tokens // tm,),
        out_shape=tuple(jax.ShapeDtypeStruct((tokens, w), dt) for w, dt in outs),
        in_specs=[_rows(tm, D_MODEL), _full(w_in_ext.shape), _full(w1.shape), _full(wk.shape),
                  _full(wv.shape), _full(g_q.shape), _full(g_kv.shape),
                  pl.BlockSpec((tm, LANES), lambda i: (i % ns, 1)),
                  pl.BlockSpec((tm, LANES), lambda i: (i % ns, 1)),
                  pl.BlockSpec((tm, LANES), lambda i: (i % ns, 0))],
        out_specs=tuple(_rows(tm, w) for w, _ in outs),
        compiler_params=_cp("parallel"),
    )(x, w_in_ext, w1, wk, wv, g_q, g_kv, cext, sext, cs128)


def _fwd_mix(o_a, o_b, gates, x, b_gate, w_oa, w_ob, w_out, ln_g, ln_b, *, seq):
    tokens = x.shape[0]
    tm = min(MIX_TILE, seq)

    def body(oa_ref, ob_ref, gt_ref, x_ref, bg_ref, woa_ref, wob_ref, wout_ref, g_ref, b_ref,
             hb_ref, xhat_ref, rstd_ref, ya_ref, yb_ref, mix_ref):
        ya = _dot(oa_ref[...], woa_ref[...])
        yb = _dot(ob_ref[...], wob_ref[...])
        g0 = _sigmoid(gt_ref[:, 0:D_MODEL].astype(F32) + bg_ref[0:1, :])
        g1 = _sigmoid(gt_ref[:, D_MODEL:].astype(F32) + bg_ref[1:2, :])
        mix = (g0 * ya + g1 * yb).astype(BF16)
        z = ALPHA * x_ref[...] + _dot(mix, wout_ref[...])
        zc = z - jnp.mean(z, axis=1, keepdims=True)
        rstd = lax.rsqrt(jnp.mean(zc * zc, axis=1, keepdims=True) + LN_EPS)
        xhat = zc * rstd
        hb_ref[...] = (xhat * g_ref[...] + b_ref[...]).astype(BF16)
        xhat_ref[...] = xhat
        rstd_ref[...] = jnp.broadcast_to(rstd, (tm, LANES))
        ya_ref[...] = ya.astype(BF16)
        yb_ref[...] = yb.astype(BF16)
        mix_ref[...] = mix

    outs = [(D_MODEL, BF16), (D_MODEL, F32), (LANES, F32), (D_MODEL, BF16), (D_MODEL, BF16), (D_MODEL, BF16)]
    return pl.pallas_call(
        body, name="fwd_mix", grid=(tokens // tm,),
        out_shape=tuple(jax.ShapeDtypeStruct((tokens, w), dt) for w, dt in outs),
        in_specs=[_rows(tm, DIL_WIDTH), _rows(tm, DIL_WIDTH), _rows(tm, 2 * D_MODEL), _rows(tm, D_MODEL),
                  _full(b_gate.shape), _full(w_oa.shape), _full(w_ob.shape), _full(w_out.shape),
                  _full(ln_g.shape), _full(ln_b.shape)],
        out_specs=tuple(_rows(tm, w) for w, _ in outs),
        compiler_params=_cp("parallel"),
    )(o_a, o_b, gates, x, b_gate, w_oa, w_ob, w_out, ln_g, ln_b)


def _fwd_mlp(hb, xhat1, target, w_ff1, w_ff2, ln1_g, ln1_b, ln_g, ln_b, *, seq):
    tokens = hb.shape[0]
    tm = min(2 * TOKEN_TILE, seq)
    tf = FF_SHARD
    nf = N_DEV // FF_STEP

    def body(hb_ref, xh_ref, tg_ref, w1_ref, w2_ref, g1_ref, b1_ref, g_ref, b_ref, u_ref, dz_ref, dzb_ref, stat_ref, acc):
        i, j = pl.program_id(0), pl.program_id(1)

        @pl.when((i == 0) & (j == 0))
        def _():
            stat_ref[...] = jnp.zeros_like(stat_ref)

        @pl.when(j == 0)
        def _():
            acc[...] = jnp.zeros_like(acc)

        acts = []
        for s in range(FF_STEP):
            u = _dot(hb_ref[...], w1_ref[s])
            u_ref[:, s * tf:(s + 1) * tf] = u.astype(BF16)
            acts.append(jnp.square(jnp.maximum(u, 0.0)).astype(BF16))
        acc[...] += _dot(jnp.concatenate(acts, axis=1), w2_ref[...])

        @pl.when(j == nf - 1)
        def _():
            z = ALPHA * (xh_ref[...] * g1_ref[...] + b1_ref[...]) + acc[...]
            zc = z - jnp.mean(z, axis=1, keepdims=True)
            rstd = lax.rsqrt(jnp.mean(zc * zc, axis=1, keepdims=True) + LN_EPS)
            xhat = zc * rstd
            err = xhat * g_ref[...] + b_ref[...] - tg_ref[...]
            dy = err * (1.0 / D_MODEL)
            dz = _ln_bwd(dy, xhat, rstd, g_ref[...])
            dz_ref[...] = dz
            dzb_ref[...] = dz.astype(BF16)
            stat_ref[0:1, :] += jnp.sum(dy * xhat, axis=0, keepdims=True)
            stat_ref[1:2, :] += jnp.sum(dy, axis=0, keepdims=True)
            stat_ref[2:3, :] += jnp.sum(jnp.sum(err * err, axis=1, keepdims=True), axis=0, keepdims=True) * (0.5 / D_MODEL)

    return pl.pallas_call(
        body, name="fwd_mlp", grid=(tokens // tm, nf),
        out_shape=(jax.ShapeDtypeStruct((tokens, D_FF), BF16), jax.ShapeDtypeStruct((tokens, D_MODEL), F32),
                   jax.ShapeDtypeStruct((tokens, D_MODEL), BF16), jax.ShapeDtypeStruct((8, D_MODEL), F32)),
        in_specs=[_rows(tm, D_MODEL), _rows(tm, D_MODEL), _rows(tm, D_MODEL),
                  pl.BlockSpec((FF_STEP, D_MODEL, tf), lambda i, j: (j, 0, 0)),
                  pl.BlockSpec((FF_STEP * tf, D_MODEL), lambda i, j: (j, 0)),
                  _full(ln1_g.shape), _full(ln1_b.shape), _full(ln_g.shape), _full(ln_b.shape)],
        out_specs=(pl.BlockSpec((tm, FF_STEP * tf), lambda i, j: (i, j)), _rows(tm, D_MODEL), _rows(tm, D_MODEL),
                   _full((8, D_MODEL))),
        scratch_shapes=[pltpu.VMEM((tm, D_MODEL), F32)],
        compiler_params=_cp("arbitrary", "arbitrary"),
    )(hb, xhat1, target, w_ff1, w_ff2, ln1_g, ln1_b, ln_g, ln_b)


def _bwd_mlp(dz2, dz2b, u, xhat1, rstd1, w_ff1, w_ff2, ln_g, *, seq):
    tokens = dz2.shape[0]
    tm = min(2 * TOKEN_TILE, seq)
    tf = FF_SHARD
    nf = N_DEV // FF_STEP

    def body(dz_ref, dzb_ref, u_ref, xh_ref, rs_ref, w1_ref, w2_ref, g_ref, du_ref, dz1_ref, dz1b_ref, stat_ref, acc):
        i, j = pl.program_id(0), pl.program_id(1)

        @pl.when((i == 0) & (j == 0))
        def _():
            stat_ref[...] = jnp.zeros_like(stat_ref)

        @pl.when(j == 0)
        def _():
            acc[...] = jnp.zeros_like(acc)

        da = _dot_nt(dzb_ref[...], w2_ref[...])
        du = (da * (2.0 * jnp.maximum(u_ref[...].astype(F32), 0.0))).astype(BF16)
        du_ref[...] = du
        part = _dot_nt(du[:, 0:tf], w1_ref[0])
        for s in range(1, FF_STEP):
            part = part + _dot_nt(du[:, s * tf:(s + 1) * tf], w1_ref[s])
        acc[...] += part

        @pl.when(j == nf - 1)
        def _():
            dh = ALPHA * dz_ref[...] + acc[...]
            xhat = xh_ref[...]
            dz1 = _ln_bwd(dh, xhat, rs_ref[:, 0:1], g_ref[...])
            dz1_ref[...] = dz1
            dz1b_ref[...] = dz1.astype(BF16)
            stat_ref[0:1, :] += jnp.sum(dh * xhat, axis=0, keepdims=True)
            stat_ref[1:2, :] += jnp.sum(dh, axis=0, keepdims=True)

    return pl.pallas_call(
        body, name="bwd_mlp", grid=(tokens // tm, nf),
        out_shape=(jax.ShapeDtypeStruct((tokens, D_FF), BF16), jax.ShapeDtypeStruct((tokens, D_MODEL), F32),
                   jax.ShapeDtypeStruct((tokens, D_MODEL), BF16), jax.ShapeDtypeStruct((8, D_MODEL), F32)),
        in_specs=[_rows(tm, D_MODEL), _rows(tm, D_MODEL), pl.BlockSpec((tm, FF_STEP * tf), lambda i, j: (i, j)),
                  _rows(tm, D_MODEL), _rows(tm, LANES),
                  pl.BlockSpec((FF_STEP, D_MODEL, tf), lambda i, j: (j, 0, 0)),
                  pl.BlockSpec((FF_STEP * tf, D_MODEL), lambda i, j: (j, 0)),
                  _full(ln_g.shape)],
        out_specs=(pl.BlockSpec((tm, FF_STEP * tf), lambda i, j: (i, j)), _rows(tm, D_MODEL), _rows(tm, D_MODEL),
                   _full((8, D_MODEL))),
        scratch_shapes=[pltpu.VMEM((tm, D_MODEL), F32)],
        compiler_params=_cp("arbitrary", "arbitrary"),
    )(dz2, dz2b, u, xhat1, rstd1, w_ff1, w_ff2, ln_g)


def _bwd_mix(dz1b, gates, y_a, y_b, b_gate, w_oa, w_ob, w_out, *, seq):
    tokens = dz1b.shape[0]
    tm = min(MIX_TILE, seq)

    def body(dz_ref, gt_ref, ya_ref, yb_ref, bg_ref, woa_ref, wob_ref, wout_ref,
             dgt_ref, dya_ref, dyb_ref, doa_ref, dob_ref, stat_ref):
        @pl.when(pl.program_id(0) == 0)
        def _():
            stat_ref[...] = jnp.zeros_like(stat_ref)

        dmix = _dot_nt(dz_ref[...], wout_ref[...])
        for k, (y_ref, w_ref, dy_ref, do_ref) in enumerate(((ya_ref, woa_ref, dya_ref, doa_ref), (yb_ref, wob_ref, dyb_ref, dob_ref))):
            g = _sigmoid(gt_ref[:, k * D_MODEL:(k + 1) * D_MODEL].astype(F32) + bg_ref[k:k + 1, :])
            dgate = dmix * y_ref[...].astype(F32) * g * (1.0 - g)
            dgt_ref[:, k * D_MODEL:(k + 1) * D_MODEL] = dgate.astype(BF16)
            stat_ref[k:k + 1, :] += jnp.sum(dgate, axis=0, keepdims=True)
            dy = (dmix * g).astype(BF16)
            dy_ref[...] = dy
            do_ref[...] = _dot_nt(dy, w_ref[...]).astype(BF16)

    outs = [(2 * D_MODEL, BF16), (D_MODEL, BF16), (D_MODEL, BF16), (DIL_WIDTH, BF16), (DIL_WIDTH, BF16)]
    return pl.pallas_call(
        body, name="bwd_mix", grid=(tokens // tm,),
        out_shape=tuple(jax.ShapeDtypeStruct((tokens, w), dt) for w, dt in outs) + (jax.ShapeDtypeStruct((8, D_MODEL), F32),),
        in_specs=[_rows(tm, D_MODEL), _rows(tm, 2 * D_MODEL), _rows(tm, D_MODEL), _rows(tm, D_MODEL),
                  _full(b_gate.shape), _full(w_oa.shape), _full(w_ob.shape), _full(w_out.shape)],
        out_specs=tuple(_rows(tm, w) for w, _ in outs) + (_full((8, D_MODEL)),),
        compiler_params=_cp("arbitrary"),
    )(dz1b, gates, y_a, y_b, b_gate, w_oa, w_ob, w_out)


def _bwd_proj(dqp, dkp, dvm, dq_d, dk_d, dv_d, dgates, dz1, low, w_in_ext, w1, wk, wv, g_q, g_kv, cext, sext, cs128, *, seq):
    tokens = dz1.shape[0]
    tm = min(TOKEN_TILE, seq)
    ns = seq // tm

    def body(dqp_ref, dkp_ref, dvm_ref, dqd_ref, dkd_ref, dvd_ref, dgt_ref, dz_ref, low_ref, win_ref, w1_ref, wk_ref,
             wv_ref, gq_ref, gkv_ref, c_ref, s_ref, cs_ref, dx_ref, dproj_ref, da_ref, dkn_ref, stat_ref):
        @pl.when(pl.program_id(0) == 0)
        def _():
            stat_ref[...] = jnp.zeros_like(stat_ref)

        low = low_ref[...]
        d_a = _rope_slabs(dqp_ref[...].astype(F32), c_ref[...], s_ref[...], True).astype(BF16)
        da_ref[...] = d_a
        q_a = low[:, 0:Q_LORA]
        _, rq = _rms(q_a, gq_ref[...])
        dq_a, gq_terms = _rms_bwd(_dot_nt(d_a, w1_ref[...]), q_a, rq, gq_ref[...])
        kv_a = low[:, Q_LORA:Q_LORA + KV_LORA]
        _, rkv = _rms(kv_a, gkv_ref[...])
        dkn = jnp.concatenate([dkp_ref[:, p * PAIR_W:p * PAIR_W + LANES] for p in range(N_PAIRS)], axis=1)
        dkn_ref[...] = dkn
        dkv_a, gkv_terms = _rms_bwd(_dot_nt(dkn, wk_ref[...]) + _dot_nt(dvm_ref[...], wv_ref[...]), kv_a, rkv, gkv_ref[...])
        dkr = sum(dkp_ref[:, p * PAIR_W + LANES:(p + 1) * PAIR_W].astype(F32) for p in range(N_PAIRS))
        dkr = dkr + pltpu.roll(dkr, LANES - ROPE, 1)
        dkr = jnp.where(lax.broadcasted_iota(I32, dkr.shape, 1) < ROPE, dkr, 0.0)
        dkr = (dkr + pltpu.roll(dkr, ROPE, 1)) * cs_ref[...]
        stat_ref[0:1, 0:Q_LORA] += jnp.sum(gq_terms, axis=0, keepdims=True)
        stat_ref[1:2, 0:KV_LORA] += jnp.sum(gkv_terms, axis=0, keepdims=True)
        dproj_ref[:, 0:Q_LORA] = dq_a.astype(BF16)
        dproj_ref[:, Q_LORA:Q_LORA + KV_LORA] = dkv_a.astype(BF16)
        dproj_ref[:, Q_LORA + KV_LORA:LOW_W] = dkr.astype(BF16)
        dproj_ref[:, LOW_W:LOW_W + DIL_WIDTH] = dqd_ref[...]
        dproj_ref[:, LOW_W + DIL_WIDTH:LOW_W + 2 * DIL_WIDTH] = dkd_ref[...]
        dproj_ref[:, LOW_W + 2 * DIL_WIDTH:LOW_W + 3 * DIL_WIDTH] = dvd_ref[...]
        dproj_ref[:, LOW_W + 3 * DIL_WIDTH:] = dgt_ref[...]
        dx_ref[...] = ALPHA * dz_ref[...] + _dot(dproj_ref[...], win_ref[...])

    wide = N_PAIRS * PAIR_W
    return pl.pallas_call(
        body, name="bwd_proj", grid=(tokens // tm,),
        out_shape=(jax.ShapeDtypeStruct((tokens, D_MODEL), F32), jax.ShapeDtypeStruct((tokens, IN_EXT), BF16),
                   jax.ShapeDtypeStruct((tokens, wide), BF16), jax.ShapeDtypeStruct((tokens, N_HEADS * NOPE), BF16),
                   jax.ShapeDtypeStruct((8, D_MODEL), F32)),
        in_specs=[_rows(tm, wide), _rows(tm, wide), _rows(tm, DIL_WIDTH), _rows(tm, DIL_WIDTH), _rows(tm, DIL_WIDTH),
                  _rows(tm, DIL_WIDTH), _rows(tm, 2 * D_MODEL),
                  _rows(tm, D_MODEL), _rows(tm, LOW_W), _full(w_in_ext.shape), _full(w1.shape),
                  _full(wk.shape), _full(wv.shape), _full(g_q.shape), _full(g_kv.shape),
                  pl.BlockSpec((tm, LANES), lambda i: (i % ns, 1)), pl.BlockSpec((tm, LANES), lambda i: (i % ns, 1)),
                  pl.BlockSpec((tm, LANES), lambda i: (i % ns, 0))],
        out_specs=(_rows(tm, D_MODEL), _rows(tm, IN_EXT), _rows(tm, wide), _rows(tm, N_HEADS * NOPE), _full((8, D_MODEL))),
        compiler_params=_cp("arbitrary"),
    )(dqp, dkp, dvm, dq_d, dk_d, dv_d, dgates, dz1, low, w_in_ext, w1, wk, wv, g_q, g_kv, cext, sext, cs128)


def _wgrad(a, b, name, square_relu=False, by_shard=False):
    tokens, ka = a.shape
    n = b.shape[1]
    if ka <= 512 or ka % 512 == 0:
        tka = min(ka, 512)
    else:
        tka = max(w for w in range(LANES, min(ka, 2304) + 1, LANES) if ka % w == 0)
    shard = n // N_DEV
    tn = WGRAD_SHARDS * shard if by_shard else max(w for w in range(LANES, min(n, 2304) + 1, LANES) if n % w == 0)
    tt = min(tokens, 1024)
    nt = tokens // tt

    def body(a_ref, b_ref, o_ref, acc):
        kt = pl.program_id(2)

        @pl.when(kt == 0)
        def _():
            acc[...] = jnp.zeros_like(acc)

        at = a_ref[...]
        if square_relu:
            at = jnp.square(jnp.maximum(at.astype(F32), 0.0)).astype(BF16)
        acc[...] += _dot_tn(at, b_ref[...])

        @pl.when(kt == nt - 1)
        def _():
            if by_shard:
                for s in range(WGRAD_SHARDS):
                    o_ref[s] = acc[:, s * shard:(s + 1) * shard].astype(BF16)
            else:
                o_ref[...] = acc[...].astype(BF16)

    if by_shard:
        out_shape, out_spec = (N_DEV, ka, shard), pl.BlockSpec((WGRAD_SHARDS, tka, shard), lambda i, j, k: (j, i, 0))
    else:
        out_shape, out_spec = (ka, n), pl.BlockSpec((tka, tn), lambda i, j, k: (i, j))
    return pl.pallas_call(
        body, name=name, grid=(ka // tka, n // tn, nt), out_shape=jax.ShapeDtypeStruct(out_shape, BF16),
        in_specs=[pl.BlockSpec((tt, tka), lambda i, j, k: (k, i)), pl.BlockSpec((tt, tn), lambda i, j, k: (k, j))],
        out_specs=out_spec,
        scratch_shapes=[pltpu.VMEM((tka, tn), F32)],
        compiler_params=_cp("parallel", "parallel", "arbitrary"),
    )(a, b)


def _adam_math(w, g, m, v):
    m = ADAM_B1 * m + (1.0 - ADAM_B1) * g
    v = ADAM_B2 * v + (1.0 - ADAM_B2) * jnp.square(g)
    m_hat = m / (1.0 - ADAM_B1 ** ADAM_STEP)
    v_hat = v / (1.0 - ADAM_B2 ** ADAM_STEP)
    return -ADAM_LR * (m_hat / (jnp.sqrt(v_hat) + ADAM_EPS) + ADAM_WD * w), m, v


def _adamw(items, name):
    steps = min(_tiles(*w.shape)[0] for w, *_ in items)
    n_items = len(items)

    def body(slot_ref, *refs):
        ins, outs = refs[:5 * n_items], refs[5 * n_items:]
        for k, (_, _, _, _, parts) in enumerate(items):
            w_ref, m_ref, v_ref, own_ref, p_ref = ins[5 * k:5 * k + 5]
            g_ref, d_ref, nm_ref, nv_ref = outs[4 * k:4 * k + 4]
            g = own_ref[...].astype(F32)
            for d in range(parts.shape[0]):
                g = g + p_ref[d].astype(F32)
            g_ref[...] = g
            d_ref[...], nm_ref[...], nv_ref[...] = _adam_math(w_ref[...], g, m_ref[...], v_ref[...])

    x, y, c = _place()
    in_specs, out_specs, out_shape, args = [], [], [], []
    for w, m, v, own, parts in items:
        rows, cols = w.shape
        _, tile, at = _tiles(rows, cols, steps)
        blk = pl.BlockSpec(tile, lambda i, slot, at=at: at(i))
        own_blk = blk if own.ndim == 2 else pl.BlockSpec((None, *tile), lambda i, slot, at=at: (slot[0], *at(i)))
        in_specs += [blk, blk, blk, own_blk, pl.BlockSpec((parts.shape[0], *tile), lambda i, slot, at=at: (0, *at(i)))]
        out_specs += [blk] * 4
        out_shape += [jax.ShapeDtypeStruct((rows, cols), F32)] * 4
        args += [w, m, v, own, parts]
    out = pl.pallas_call(
        body, name=name,
        grid_spec=pltpu.PrefetchScalarGridSpec(num_scalar_prefetch=1, grid=(steps,), in_specs=in_specs, out_specs=out_specs),
        out_shape=out_shape, compiler_params=_cp("parallel"),
    )(jnp.reshape(4 * x + 2 * y + c, (1,)).astype(I32), *args)
    return [tuple(out[4 * k:4 * k + 4]) for k in range(n_items)]


def _adamw_small(parts, w, m, v):
    _, rows, cols = parts.shape

    def body(p_ref, w_ref, m_ref, v_ref, g_ref, d_ref, nm_ref, nv_ref):
        g = p_ref[0]
        for d in range(1, N_DEV):
            g = g + p_ref[d]
        g_ref[...] = g
        d_ref[...], nm_ref[...], nv_ref[...] = _adam_math(w_ref[...], g, m_ref[...], v_ref[...])

    return pl.pallas_call(
        body, name="adamw_replicated", out_shape=(jax.ShapeDtypeStruct((rows, cols), F32),) * 4,
        in_specs=[_full(parts.shape)] + [_full((rows, cols))] * 3, out_specs=(_full((rows, cols)),) * 4, grid=(1,),
        compiler_params=_cp("arbitrary"),
    )(parts, w, m, v)


def _pad_rows(a2d, mult):
    pad = (-a2d.shape[-2]) % mult
    return jnp.pad(a2d, [(0, 0)] * (a2d.ndim - 2) + [(0, pad), (0, 0)]) if pad else a2d


def _pad_cols(a):
    pad = (-a.shape[-1]) % LANES
    return jnp.pad(a, [(0, 0)] * (a.ndim - 1) + [(0, pad)]) if pad else a


def _rot_cols(w):
    half = ROPE // 2
    return jnp.concatenate([-w[..., half:], w[..., :half]], axis=-1)


def _unrot_cols(dw):
    half = ROPE // 2
    return jnp.concatenate([dw[..., half:], -dw[..., :half]], axis=-1)


def _from_col_shards(stacked):
    return stacked.transpose(1, 0, 2).reshape(stacked.shape[1], -1)


def _to_col_shards(full):
    r = full.shape[0]
    return full.reshape(r, N_DEV, -1).transpose(1, 0, 2)


def _rope_tables(seq):
    half = ROPE // 2
    inv = jnp.power(ROPE_THETA, -jnp.arange(half, dtype=F32) / half)
    ang = jnp.arange(seq, dtype=F32)[:, None] * inv[None, :]
    cos = jnp.concatenate([jnp.cos(ang)] * 2, axis=1)
    sin = jnp.concatenate([jnp.sin(ang)] * 2, axis=1)
    ones, zeros = jnp.ones((seq, 2 * NOPE), F32), jnp.zeros((seq, 2 * NOPE), F32)
    pad = jnp.zeros((seq, PAIR_W - 2 * NOPE - 2 * ROPE), F32)
    cext = jnp.concatenate([ones, cos, cos, pad], axis=1)
    sext = jnp.concatenate([zeros, sin, sin, pad], axis=1)
    cs128 = jnp.concatenate([cos, sin, jnp.zeros((seq, LANES - 2 * ROPE), F32)], axis=1)
    return cext, sext, cs128


def _pair_slabs(nope, rope):
    k = nope.shape[0]
    nope = nope.reshape(k, N_PAIRS, 2 * NOPE)
    rope = jnp.zeros((k, N_PAIRS, 2 * ROPE), nope.dtype) if rope is None else rope.reshape(k, N_PAIRS, 2 * ROPE)
    pad = jnp.zeros((k, N_PAIRS, PAIR_W - 2 * NOPE - 2 * ROPE), nope.dtype)
    return jnp.concatenate([nope, rope, pad], axis=2).reshape(k, N_PAIRS * PAIR_W)


def _split_slabs(slabs):
    k = slabs.shape[0]
    s = slabs.reshape(k, N_PAIRS, PAIR_W)
    return s[:, :, :2 * NOPE].reshape(k, N_HEADS, NOPE), s[:, :, 2 * NOPE:2 * NOPE + 2 * ROPE].reshape(k, N_HEADS, ROPE)


def kernel(x, w_in, b_gate, g_q_a, w_uq, g_kv_a, w_ukv, w_o_mla, w_o_dil, w_out, ln1_g, ln1_b, w_ff1, w_ff2, ln2_g, ln2_b, loss_target, m_w_in, m_b_gate, m_g_q_a, m_w_uq, m_g_kv_a, m_w_ukv, m_w_o_mla, m_w_o_dil, m_w_out, m_ln1_g, m_ln1_b, m_w_ff1, m_w_ff2, m_ln2_g, m_ln2_b, v_w_in, v_b_gate, v_g_q_a, v_w_uq, v_g_kv_a, v_w_ukv, v_w_o_mla, v_w_o_dil, v_w_out, v_ln1_g, v_ln1_b, v_w_ff1, v_w_ff2, v_ln2_g, v_ln2_b):
    batch, seq, _ = x.shape
    tokens = batch * seq
    weights = dict(w_in=w_in, w_uq=w_uq, w_ukv=w_ukv, w_o_mla=w_o_mla, w_o_dil=w_o_dil, w_out=w_out, w_ff1=w_ff1, w_ff2=w_ff2, b_gate=b_gate)
    mom_m = dict(w_in=m_w_in, w_uq=m_w_uq, w_ukv=m_w_ukv, w_o_mla=m_w_o_mla, w_o_dil=m_w_o_dil, w_out=m_w_out, w_ff1=m_w_ff1, w_ff2=m_w_ff2, b_gate=m_b_gate)
    mom_v = dict(w_in=v_w_in, w_uq=v_w_uq, w_ukv=v_w_ukv, w_o_mla=v_w_o_mla, w_o_dil=v_w_o_dil, w_out=v_w_out, w_ff1=v_w_ff1, w_ff2=v_w_ff2, b_gate=v_b_gate)

    first = ["w_in", "w_uq", "w_ukv"]
    widths = [weights[n].shape[2] for n in first]
    shards = [weights["w_in"][0].T.astype(BF16)] + [_pad_cols(weights[n][0].astype(BF16)) for n in first[1:]]
    g_in, g_uq, g_ukv = _run_comm(_Gather(shards), shards, "all_gather_first_weights")
    g_uq, g_ukv = g_uq[:, :, :widths[1]], g_ukv[:, :, :widths[2]]

    s1, s2, n_in = Q_LORA + KV_LORA, Q_LORA + KV_LORA + ROPE, N_DEV * widths[0]

    def w_in_cols(lo, hi):
        out = []
        while lo < hi:
            d, off = divmod(lo, widths[0])
            take = min(hi - lo, widths[0] - off)
            out.append(g_in[d][off:off + take])
            lo += take
        return out

    k_r = jnp.concatenate(w_in_cols(s1, s2), axis=0)
    w_in_ext = jnp.concatenate(w_in_cols(0, s2) + [_rot_cols(k_r.T).T, jnp.zeros((LOW_W - s2 - ROPE, D_MODEL), BF16)]
                               + w_in_cols(s2, n_in), axis=0)
    uq = _from_col_shards(g_uq).reshape(Q_LORA, N_HEADS, NOPE + ROPE)
    w1 = _pair_slabs(uq[:, :, :NOPE], uq[:, :, NOPE:])
    ukv = _from_col_shards(g_ukv).reshape(KV_LORA, N_HEADS, NOPE + HEAD_V)
    wk = ukv[:, :, :NOPE].reshape(KV_LORA, N_HEADS * NOPE)
    wv = ukv[:, :, NOPE:].reshape(KV_LORA, N_HEADS * HEAD_V)
    cext, sext, cs128 = _rope_tables(seq)
    dil_bias = _dilated_bias_table(seq)
    no_bias = jnp.zeros((1, 8, LANES), F32)

    x2 = x.reshape(tokens, D_MODEL)
    low, gates, qkvd, qp, kp, vm, qn, kvn, xb = _fwd_proj(x2, w_in_ext, w1, wk, wv, g_q_a, g_kv_a, cext, sext, cs128, seq=seq)
    bg = b_gate[0]
    bg_hi = bg.astype(BF16)
    bg_lo = (bg - bg_hi.astype(F32)).astype(BF16)
    later = [weights[n][0].astype(BF16) for n in ("w_o_mla", "w_o_dil", "w_out", "w_ff1", "w_ff2")]
    later.append(_pad_rows(jnp.concatenate([bg_hi, bg_lo], axis=0), 16))
    mla = dict(batch=batch, seq=seq, width=PAIR_W, col0=(0, 0, 0), dilated=False, scale=MLA_SCALE)
    dil = dict(batch=batch, seq=seq, width=LANES, col0=(0, N_PAIRS, 2 * N_PAIRS), dilated=True, scale=DIL_SCALE)
    o_a, lse_a, g_oa, g_ob, g_out, g_ff1, g_ff2, g_bg = _attn_fwd(
        qp, kp, vm, no_bias, name="mla_attention_fwd", comm=_Gather(later), comm_arrays=later, **mla)
    o_b, lse_b = _attn_fwd(qkvd, qkvd, qkvd, dil_bias, name="dilated_attention_fwd", **dil)
    w_oa, w_ob = _from_col_shards(g_oa), _from_col_shards(g_ob)
    w_out_full = g_out.reshape(D_MODEL, D_MODEL)
    w_ff2_full = g_ff2.reshape(D_FF, D_MODEL)
    bg_parts = g_bg.astype(F32)
    b_gate_full = _from_col_shards(bg_parts[:, 0:2] + bg_parts[:, 2:4])
    hb, xhat1, rstd1, y_a, y_b, mix = _fwd_mix(o_a, o_b, gates, x2, b_gate_full, w_oa, w_ob, w_out_full, ln1_g, ln1_b, seq=seq)
    u, dz2, dz2b, stat2 = _fwd_mlp(hb, xhat1, loss_target.reshape(tokens, D_MODEL), g_ff1, w_ff2_full, ln1_g, ln1_b, ln2_g, ln2_b, seq=seq)

    du, dz1, dz1b, stat1 = _bwd_mlp(dz2, dz2b, u, xhat1, rstd1, g_ff1, w_ff2_full, ln1_g, seq=seq)
    dw_ff = [_wgrad(hb, du, "wgrad_ff1", by_shard=True),
             _wgrad(u, dz2b, "wgrad_ff2", square_relu=True).reshape(N_DEV, FF_SHARD, D_MODEL)]
    dgates, dy_a, dy_b, do_a, do_b, stat_g = _bwd_mix(dz1b, gates, y_a, y_b, b_gate_full, w_oa, w_ob, w_out_full, seq=seq)
    dqp, dkp, dvm, r_ff1, r_ff2 = _attn_bwd(qp, kp, vm, o_a, do_a, lse_a, no_bias, name="mla_attention_bwd",
                                            comm=_Scatter(dw_ff), comm_arrays=dw_ff, **mla)
    dw_mid = [_to_col_shards(_wgrad(o_a, dy_a, "wgrad_o_mla")), _to_col_shards(_wgrad(o_b, dy_b, "wgrad_o_dil")),
              _wgrad(mix, dz1b, "wgrad_out").reshape(N_DEV, D_MODEL // N_DEV, D_MODEL),
              _pad_rows(_to_col_shards(stat_g[0:2]).astype(BF16), 16)]
    dq_d, dk_d, dv_d, r_oa, r_ob, r_out, r_bg = _attn_bwd(qkvd, qkvd, qkvd, o_b, do_b, lse_b, dil_bias, name="dilated_attention_bwd",
                                                          comm=_Scatter(dw_mid), comm_arrays=dw_mid, **dil)
    grad_x, dproj, d_a, dkn, stat_r = _bwd_proj(dqp, dkp, dvm, dq_d, dk_d, dv_d, dgates, dz1, low, w_in_ext, w1, wk, wv,
                                                g_q_a, g_kv_a, cext, sext, cs128, seq=seq)

    dw_in_ext = _wgrad(dproj, xb, "wgrad_in")
    dw1 = _wgrad(qn, d_a, "wgrad_uq")
    dwk = _wgrad(kvn, dkn, "wgrad_ukv_k")
    dwv = _wgrad(kvn, dvm, "wgrad_ukv_v")
    dw_kr = dw_in_ext[s1:s2] + _unrot_cols(dw_in_ext[s2:s2 + ROPE].T).T

    def dw_in_cols(lo, hi):
        out = []
        for a, b, piece in ((0, s1, lambda u, v: dw_in_ext[u:v]), (s1, s2, lambda u, v: dw_kr[u - s1:v - s1]),
                            (s2, n_in, lambda u, v: dw_in_ext[u + LOW_W - s2:v + LOW_W - s2])):
            if max(lo, a) < min(hi, b):
                out.append(piece(max(lo, a), min(hi, b)))
        return out

    dw_in = jnp.stack([jnp.concatenate(dw_in_cols(d * widths[0], (d + 1) * widths[0]), axis=0) for d in range(N_DEV)])
    n1, r1 = _split_slabs(dw1)
    dw_uq = jnp.concatenate([n1, r1], axis=2).reshape(Q_LORA, N_HEADS * (NOPE + ROPE))
    dw_ukv = jnp.concatenate([dwk.reshape(KV_LORA, N_HEADS, NOPE), dwv.reshape(KV_LORA, N_HEADS, HEAD_V)], axis=2).reshape(KV_LORA, N_HEADS * (NOPE + HEAD_V))
    last = [dw_in] + [_pad_cols(_to_col_shards(dw)) for dw in (dw_uq, dw_ukv)]
    theirs = _rs_sibling(last, "rs_last_sibling_exchange")
    sums = [_pair_sum(a, b, "rs_last_pair_sum_" + n) for a, b, n in zip(last, theirs, first)]
    partial = jnp.concatenate([stat_r[0:1, :Q_LORA], stat_r[1:2, :KV_LORA], stat1[0:1], stat1[1:2], stat2[0:1], stat2[1:2],
                               stat2[2:3, :LANES]], axis=1)
    partial = _pad_rows(partial.reshape(-1, LANES), 8)
    rest = [s[1] for s in sums]
    got_in, got_uq, got_ukv, every = _run_comm(_Plans([_ChipExchange(rest), _Gather([partial])]), rest + [partial],
                                               "rs_last_chip_exchange")

    upd = {}
    early = ["w_ff1", "w_ff2", "w_out", "w_o_mla", "w_o_dil"]
    items = [(weights[n][0], mom_m[n][0], mom_v[n][0], own, parts) for n, own, parts in
             zip(early, (dw_ff[0], dw_ff[1], dw_mid[2], dw_mid[0], dw_mid[1]), (r_ff1, r_ff2, r_out, r_oa, r_ob))]
    upd.update(zip(early, _adamw(items, "adamw_early_weights")))
    (in_t,) = _adamw([(weights["w_in"][0].T, mom_m["w_in"][0].T, mom_v["w_in"][0].T, sums[0][0], got_in)], "adamw_w_in")
    upd["w_in"] = tuple(a.T for a in in_t)
    for n, w, (own, _), parts in zip(first[1:], widths[1:], sums[1:], (got_uq, got_ukv)):
        (upd[n],) = _adamw([(weights[n][0], mom_m[n][0], mom_v[n][0], own[:, :w], parts[:, :, :w])], "adamw_" + n)
    (bg_upd,) = _adamw([(_pad_rows(b_gate[0], 16), _pad_rows(m_b_gate[0], 16), _pad_rows(v_b_gate[0], 16), dw_mid[3], r_bg)],
                       "adamw_b_gate")
    upd["b_gate"] = tuple(t[0:2] for t in bg_upd)

    small_w = [g_q_a, g_kv_a, ln1_g, ln1_b, ln2_g, ln2_b]
    small_m = [m_g_q_a, m_g_kv_a, m_ln1_g, m_ln1_b, m_ln2_g, m_ln2_b]
    small_v = [v_g_q_a, v_g_kv_a, v_ln1_g, v_ln1_b, v_ln2_g, v_ln2_b]
    small_widths = [a.shape[1] for a in small_w]

    def as_rows(vecs, extra):
        flat = jnp.concatenate(vecs + [jnp.zeros((1, extra), F32)], axis=1)
        return _pad_rows(flat.reshape(-1, LANES), 8)

    g_s, d_s, nm_s, nv_s = _adamw_small(every, as_rows(small_w, LANES), as_rows(small_m, LANES), as_rows(small_v, LANES))

    def split_small(a):
        flat = a.reshape(1, -1)
        out, c0 = [], 0
        for w in small_widths:
            out.append(flat[:, c0:c0 + w])
            c0 += w
        return out, flat[0, c0]

    g_small, loss = split_small(g_s)
    small = [g_small, split_small(d_s)[0], split_small(nm_s)[0], split_small(nv_s)[0]]

    order = ["w_in", "b_gate", "g_q_a", "w_uq", "g_kv_a", "w_ukv", "w_o_mla", "w_o_dil", "w_out", "ln1_g", "ln1_b", "w_ff1", "w_ff2", "ln2_g", "ln2_b"]
    small_names = ["g_q_a", "g_kv_a", "ln1_g", "ln1_b", "ln2_g", "ln2_b"]

    def pick(kind):
        return [small[kind][small_names.index(n)] if n in small_names else upd[n][kind][None] for n in order]

    return (loss, grad_x.reshape(batch, seq, D_MODEL), *pick(0), *pick(1), *pick(2), *pick(3))
```

```python
import functools
import math

import jax
import jax.numpy as jnp
from jax import lax
from jax.experimental import pallas as pl
from jax.experimental.pallas import tpu as pltpu

F32 = jnp.float32
BF16 = jnp.bfloat16
I32 = jnp.int32

D_MODEL = 1024
N_HEADS = 8
NOPE = 64
ROPE = 32
HEAD_V = 64
Q_LORA = 384
KV_LORA = 256
DIL_WIDTH = 512
D_FF = 4096
ROPE_THETA = 10000.0
LN_EPS = 1e-5
RMS_EPS = 1e-6
NEG = -1e30
ALPHA = 2.0 ** 0.25
MLA_SCALE = (NOPE + ROPE) ** -0.5
DIL_SCALE = 64 ** -0.5
ADAM_LR, ADAM_B1, ADAM_B2, ADAM_EPS, ADAM_WD, ADAM_STEP = 0.001, 0.9, 0.999, 1e-08, 0.01, 10

LANES = 128
PAIR_W = 256
N_PAIRS = N_HEADS // 2
LOW_W = 768
IN_EXT = LOW_W + 3 * DIL_WIDTH + 2 * D_MODEL
N_DEV = 8
FF_SHARD = D_FF // N_DEV
FF_STEP = 4
WGRAD_SHARDS = 4
TOKEN_TILE = 256
MIX_TILE = 512
ATTN_TILE = 256
VMEM_LIMIT = 56 << 20

MESH = pl.DeviceIdType.MESH
ANY = pl.BlockSpec(memory_space=pl.ANY)
CHIP_FLIPS = ((0, 0), (0, 1), (1, 0), (1, 1))
PEER_FLIPS = tuple((fx, fy, fc) for fx in (0, 1) for fy in (0, 1) for fc in (0, 1))[1:]


def _cp(*sem):
    return pltpu.CompilerParams(dimension_semantics=sem or None, vmem_limit_bytes=VMEM_LIMIT)


def _full(shape):
    nd = len(shape)
    return pl.BlockSpec(shape, lambda *_: (0,) * nd)


def _rows(tm, width):
    return pl.BlockSpec((tm, width), lambda i, *_: (i, 0))


def _dot(a, b):
    return jnp.dot(a, b, preferred_element_type=F32)


def _dot_nt(a, b):
    return lax.dot_general(a, b, (((1,), (1,)), ((), ())), preferred_element_type=F32)


def _dot_tn(a, b):
    return lax.dot_general(a, b, (((0,), (0,)), ((), ())), preferred_element_type=F32)


def _sigmoid(z):
    return 1.0 / (1.0 + jnp.exp(-z))


def _place():
    return lax.axis_index("x"), lax.axis_index("y"), lax.axis_index("c")


def _flip(v, f):
    return 1 - v if f else v


class _Gather:
    def __init__(self, shards):
        self.n = len(shards)
        self.out_shape = [jax.ShapeDtypeStruct((N_DEV, *s.shape), s.dtype) for s in shards]
        self.scratch = [pltpu.SemaphoreType.DMA((7 * self.n,)), pltpu.SemaphoreType.DMA((7 * self.n,)),
                        pltpu.SemaphoreType.DMA((self.n,))]

    def _copies(self, what, srcs, dsts, send, recv, local):
        x, y, c = _place()
        chips = [(_flip(x, fx), _flip(y, fy)) for fx, fy in CHIP_FLIPS[1:]]
        out = []
        for a in range(self.n):
            def slot(px, py, pc, a=a):
                return dsts[a].at[4 * px + 2 * py + pc]

            def copy(k, block, to, src=None, a=a, slot=slot):
                return pltpu.make_async_remote_copy(
                    src_ref=slot(*block) if src is None else src, dst_ref=slot(*block),
                    send_sem=send.at[7 * a + k], recv_sem=recv.at[7 * a + k], device_id=to, device_id_type=MESH)

            if what == "mine":
                out.append(pltpu.make_async_copy(srcs[a], slot(x, y, c), local.at[a]))
            elif what == "first":
                out.append(copy(0, (x, y, c), (x, y, 1 - c), src=srcs[a]))
                out += [copy(1 + j, (x, y, c), (*chip, c), src=srcs[a]) for j, chip in enumerate(chips)]
            elif what == "landed":
                out += [copy(1 + j, (*chip, c), (x, y, c)) for j, chip in enumerate(chips)]
            elif what == "passed":
                out += [copy(4 + j, (*chip, c), (x, y, 1 - c)) for j, chip in enumerate(chips)]
            else:
                out.append(copy(0, (x, y, 1 - c), (x, y, c)))
                out += [copy(4 + j, (*chip, 1 - c), (x, y, c)) for j, chip in enumerate(chips)]
        return out

    def start(self, *refs):
        for cp in self._copies("first", *refs) + self._copies("mine", *refs):
            cp.start()

    def forward(self, *refs):
        for landed, passed in zip(self._copies("landed", *refs), self._copies("passed", *refs)):
            landed.wait_recv()
            passed.start()

    def finish(self, *refs):
        for cp in self._copies("from_sibling", *refs):
            cp.wait_recv()
        for cp in self._copies("first", *refs) + self._copies("passed", *refs):
            cp.wait_send()
        for cp in self._copies("mine", *refs):
            cp.wait()


class _Scatter:
    def __init__(self, arrays):
        self.n = len(arrays)
        self.out_shape = [jax.ShapeDtypeStruct((7, *a.shape[1:]), a.dtype) for a in arrays]
        self.scratch = [pltpu.SemaphoreType.DMA((7 * self.n,)), pltpu.SemaphoreType.DMA((7 * self.n,))]

    def _copies(self, srcs, dsts, send, recv):
        x, y, c = _place()
        out = []
        for a in range(self.n):
            for k, (fx, fy, fc) in enumerate(PEER_FLIPS):
                px, py, pc = _flip(x, fx), _flip(y, fy), _flip(c, fc)
                out.append(pltpu.make_async_remote_copy(
                    src_ref=srcs[a].at[4 * px + 2 * py + pc], dst_ref=dsts[a].at[k],
                    send_sem=send.at[7 * a + k], recv_sem=recv.at[7 * a + k], device_id=(px, py, pc), device_id_type=MESH))
        return out

    def start(self, *refs):
        for cp in self._copies(*refs):
            cp.start()

    def forward(self, *refs):
        pass

    def finish(self, *refs):
        for cp in self._copies(*refs):
            cp.wait_send()
        for cp in self._copies(*refs):
            cp.wait_recv()


class _ChipExchange:
    def __init__(self, arrays):
        self.n = len(arrays)
        self.out_shape = [jax.ShapeDtypeStruct(a.shape, a.dtype) for a in arrays]
        self.scratch = [pltpu.SemaphoreType.DMA((3 * self.n,)), pltpu.SemaphoreType.DMA((3 * self.n,))]

    def _copies(self, srcs, dsts, send, recv):
        x, y, c = _place()
        return [pltpu.make_async_remote_copy(
            src_ref=srcs[a].at[k], dst_ref=dsts[a].at[k], send_sem=send.at[3 * a + k], recv_sem=recv.at[3 * a + k],
            device_id=(_flip(x, fx), _flip(y, fy), c), device_id_type=MESH)
            for a in range(self.n) for k, (fx, fy) in enumerate(CHIP_FLIPS[1:])]

    def start(self, *refs):
        for cp in self._copies(*refs):
            cp.start()

    def forward(self, *refs):
        pass

    def finish(self, *refs):
        for cp in self._copies(*refs):
            cp.wait_send()
        for cp in self._copies(*refs):
            cp.wait_recv()


class _Plans:
    def __init__(self, plans):
        self.plans = plans
        self.n = sum(p.n for p in plans)
        self.out_shape = [s for p in plans for s in p.out_shape]
        self.scratch = [s for p in plans for s in p.scratch]

    def _each(self, phase, srcs, dsts, *sems):
        i0 = s0 = 0
        for p in self.plans:
            getattr(p, phase)(srcs[i0:i0 + p.n], dsts[i0:i0 + p.n], *sems[s0:s0 + len(p.scratch)])
            i0, s0 = i0 + p.n, s0 + len(p.scratch)

    def start(self, *refs):
        self._each("start", *refs)

    def forward(self, *refs):
        self._each("forward", *refs)

    def finish(self, *refs):
        self._each("finish", *refs)


def _run_comm(comm, arrays, name):
    n = comm.n

    def body(*refs):
        args = (refs[:n], refs[n:2 * n], *refs[2 * n:])
        comm.start(*args)
        comm.forward(*args)
        comm.finish(*args)

    return pl.pallas_call(body, name=name, out_shape=comm.out_shape, in_specs=[ANY] * n, out_specs=[ANY] * n,
                          scratch_shapes=comm.scratch)(*arrays)


def _rs_sibling(arrays, name):
    n = len(arrays)

    def body(*refs):
        srcs, got, (send, recv) = refs[:n], refs[n:2 * n], refs[2 * n:]
        x, y, c = _place()
        copies = []
        for a in range(n):
            for r, (fx, fy) in enumerate(CHIP_FLIPS):
                chip = 2 * _flip(x, fx) + _flip(y, fy)
                copies.append(pltpu.make_async_remote_copy(
                    src_ref=srcs[a].at[2 * chip + 1 - c], dst_ref=got[a].at[r], send_sem=send.at[4 * a + r],
                    recv_sem=recv.at[4 * a + r], device_id=(x, y, 1 - c), device_id_type=MESH))
        for cp in copies:
            cp.start()
        for cp in copies:
            cp.wait_send()
        for cp in copies:
            cp.wait_recv()

    return pl.pallas_call(
        body, name=name, out_shape=[jax.ShapeDtypeStruct((4, *a.shape[1:]), a.dtype) for a in arrays],
        in_specs=[ANY] * n, out_specs=[ANY] * n,
        scratch_shapes=[pltpu.SemaphoreType.DMA((4 * n,)), pltpu.SemaphoreType.DMA((4 * n,))],
    )(*arrays)


def _chip_slots():
    x, y, c = _place()
    return jnp.stack([4 * _flip(x, fx) + 2 * _flip(y, fy) + c for fx, fy in CHIP_FLIPS]).astype(I32)


def _tiles(rows, cols, steps=4):
    if rows % (16 * steps) == 0:
        return steps, (rows // steps, cols), lambda i: (i, 0)
    if cols % (LANES * steps) == 0:
        return steps, (rows, cols // steps), lambda i: (0, i)
    return 1, (rows, cols), lambda i: (0, 0)


def _pair_sum(full, theirs, name):
    _, rows, cols = theirs.shape
    steps, tile, at = _tiles(rows, cols)

    def body(slots_ref, m0_ref, m1_ref, m2_ref, m3_ref, b_ref, own_ref, rest_ref):
        own_ref[...] = m0_ref[...].astype(F32) + b_ref[0].astype(F32)
        for k, m_ref in enumerate((m1_ref, m2_ref, m3_ref)):
            rest_ref[k] = (m_ref[...].astype(F32) + b_ref[k + 1].astype(F32)).astype(BF16)

    def mine(k):
        return pl.BlockSpec((None, *tile), lambda i, slots: (slots[k], *at(i)))

    return pl.pallas_call(
        body, name=name,
        grid_spec=pltpu.PrefetchScalarGridSpec(
            num_scalar_prefetch=1, grid=(steps,),
            in_specs=[mine(0), mine(1), mine(2), mine(3), pl.BlockSpec((4, *tile), lambda i, slots: (0, *at(i)))],
            out_specs=(pl.BlockSpec(tile, lambda i, slots: at(i)), pl.BlockSpec((3, *tile), lambda i, slots: (0, *at(i))))),
        out_shape=(jax.ShapeDtypeStruct((rows, cols), F32), jax.ShapeDtypeStruct((3, rows, cols), BF16)),
        compiler_params=_cp("parallel"),
    )(_chip_slots(), full, full, full, full, theirs)


def _head_lanes(width, h):
    lane = lax.broadcasted_iota(I32, (1, width), 1)
    if width == LANES:
        return (lane >= 64 * h) & (lane < 64 * h + 64)
    nope = (lane >= NOPE * h) & (lane < NOPE * h + NOPE)
    rope = (lane >= 2 * NOPE + ROPE * h) & (lane < 2 * NOPE + ROPE * h + ROPE)
    return nope | rope


def _dilated_bias_table(seq):
    t = min(ATTN_TILE, seq)
    nd = seq // t

    def body(o_ref):
        delta = pl.program_id(0) * t + lax.broadcasted_iota(I32, (t, t), 1) - lax.broadcasted_iota(I32, (t, t), 0)
        mult = ((delta <= 128).astype(I32) + (((delta & 3) == 0) & (delta <= 512)).astype(I32)
                + ((delta & 15) == 0).astype(I32))
        logm = jnp.where(mult == 3, math.log(3.0), jnp.where(mult == 2, math.log(2.0), 0.0))
        valid = (delta >= 0) & (mult > 0)
        dist = delta.astype(F32)
        for h in range(N_HEADS):
            o_ref[h] = jnp.where(valid, logm - 2.0 ** (-(h + 1)) * dist, NEG)

    return pl.pallas_call(
        body, name="dilated_bias_table", grid=(nd,), out_shape=jax.ShapeDtypeStruct((N_HEADS, nd, t, t), F32),
        out_specs=pl.BlockSpec((N_HEADS, None, t, t), lambda d: (0, d, 0, 0)),
        compiler_params=_cp("parallel"),
    )()


def _comm_hooks(comm, refs, n_in, n_out):
    if comm is None:
        return refs[:n_in], refs[n_in:n_in + n_out], refs[n_in + n_out:], None
    n = comm.n
    ins, srcs = refs[:n_in], refs[n_in:n_in + n]
    outs, dsts = refs[n_in + n:n_in + n + n_out], refs[n_in + n + n_out:n_in + 2 * n + n_out]
    rest = refs[n_in + 2 * n + n_out:]
    own = len(rest) - len(comm.scratch)
    return ins, outs, rest[:own], (srcs, dsts, *rest[own:])


def _attn_fwd(q, k, v, bias, *, batch, seq, width, col0, dilated, scale, name, comm=None, comm_arrays=()):
    t = min(ATTN_TILE, seq)
    nq = seq // t
    cq, ck, cv = col0
    pre = scale if dilated else 1.0
    steps = batch * N_PAIRS

    def body(*refs):
        (q_ref, k_ref, v_ref, bias_ref), (o_ref, lse_ref), (v_heads,), plan = _comm_hooks(comm, refs, 4, 2)
        step_no = pl.program_id(0) * N_PAIRS + pl.program_id(1)
        if plan:
            pl.when(step_no == 0)(lambda: comm.start(*plan))
            pl.when(step_no == (3 * steps) // 4)(lambda: comm.forward(*plan))
        v_all = v_ref[...].astype(F32)
        for h in (0, 1):
            v_heads[h] = jnp.transpose(jnp.where(_head_lanes(LANES, h), v_all, 0.0)).astype(BF16)
        top = lax.broadcasted_iota(I32, (LANES, t), 0) < HEAD_V
        causal = lax.broadcasted_iota(I32, (t, t), 0) <= lax.broadcasted_iota(I32, (t, t), 1)
        lax.fori_loop(0, nq, functools.partial(query_tile, q_ref, k_ref, bias_ref, o_ref, lse_ref, v_heads, top, causal), 0)
        if plan:
            pl.when(step_no == steps - 1)(lambda: comm.finish(*plan))

    def query_tile(q_ref, k_ref, bias_ref, o_ref, lse_ref, v_heads, top, causal, i, _):
        qs = pl.multiple_of(i * t, t)
        q2 = q_ref[pl.ds(qs, t), :] * pre if dilated else q_ref[pl.ds(qs, t), :]
        qh = [jnp.where(_head_lanes(width, h), q2, jnp.zeros_like(q2)) for h in (0, 1)]

        def scores(j):
            kj = k_ref[pl.ds(pl.multiple_of(j * t, t), t), :]
            return [_dot_nt(kj, qh[h]) for h in (0, 1)]

        def step(j, carry, last):
            m0, l0, m1, l1, acc, s0, s1 = carry
            ahead = [] if last else scores(j + 1)
            ks = pl.multiple_of(j * t, t)
            new, alphas, pv = [], [], []
            for h, (m, l, s) in enumerate(((m0, l0, s0), (m1, l1, s1))):
                if dilated:
                    s = s + bias_ref[h, i - j]
                else:
                    s = s * scale
                    if last:
                        s = jnp.where(causal, s, NEG)
                m_new = jnp.maximum(m, jnp.max(s, axis=0, keepdims=True))
                a = jnp.exp(m - m_new)
                p = jnp.exp(s - m_new)
                new += [m_new, a * l + jnp.sum(p, axis=0, keepdims=True)]
                alphas.append(a)
                pv.append(_dot(v_heads[h, :, pl.ds(ks, t)], p.astype(BF16)))
            acc = jnp.where(top, alphas[0], alphas[1]) * acc + pv[0] + pv[1]
            return (*new, acc, *ahead)

        row = jnp.full((1, t), NEG, F32)
        zero = jnp.zeros((1, t), F32)
        init = (row, zero, row, zero, jnp.zeros((LANES, t), F32), *scores(0))
        m0, l0, m1, l1, acc = step(i, lax.fori_loop(0, i, functools.partial(step, last=False), init), True)
        o_ref[pl.ds(qs, t), :] = jnp.transpose(acc * jnp.where(top, 1.0 / l0, 1.0 / l1)).astype(BF16)
        r = lax.broadcasted_iota(I32, (8, t), 0)
        lse_ref[:, pl.ds(qs, t)] = jnp.where(r == 0, m0 + jnp.log(l0), jnp.where(r == 1, m1 + jnp.log(l1), 0.0))
        return 0

    bias_spec = (pl.BlockSpec((2, nq, t, t), lambda b, p: (p, 0, 0, 0)) if dilated
                 else pl.BlockSpec((None, 8, LANES), lambda b, p: (0, 0, 0)))
    n = comm.n if comm else 0
    return pl.pallas_call(
        body, name=name, grid=(batch, N_PAIRS),
        out_shape=[jax.ShapeDtypeStruct((batch * seq, DIL_WIDTH), BF16), jax.ShapeDtypeStruct((batch * N_PAIRS, 8, seq), F32)]
        + (comm.out_shape if comm else []),
        in_specs=[pl.BlockSpec((seq, width), lambda b, p: (b, cq + p)),
                  pl.BlockSpec((seq, width), lambda b, p: (b, ck + p)),
                  pl.BlockSpec((seq, LANES), lambda b, p: (b, cv + p)),
                  bias_spec] + [ANY] * n,
        out_specs=[pl.BlockSpec((seq, LANES), lambda b, p: (b, p)),
                   pl.BlockSpec((None, 8, seq), lambda b, p: (b * N_PAIRS + p, 0, 0))] + [ANY] * n,
        scratch_shapes=[pltpu.VMEM((2, LANES, seq), BF16)] + (comm.scratch if comm else []),
        compiler_params=_cp("arbitrary", "arbitrary") if comm else _cp("parallel", "parallel"),
    )(q, k, v, bias, *comm_arrays)


def _attn_bwd(q, k, v, o, do, lse, bias, *, batch, seq, width, col0, dilated, scale, name, comm=None, comm_arrays=()):
    t = min(ATTN_TILE, seq)
    nq = seq // t
    cq, ck, cv = col0
    pre = scale if dilated else 1.0
    dq_transposed = width == LANES
    steps = batch * N_PAIRS

    def body(*refs):
        ins, (dq_ref, dk_ref, dv_ref), (dq_acc, dk_acc, dv_acc, rowdot, q_heads, do_heads), plan = _comm_hooks(comm, refs, 7, 3)
        q_ref, k_ref, v_ref, o_ref, do_ref, lse_ref, bias_ref = ins
        step_no = pl.program_id(0) * N_PAIRS + pl.program_id(1)
        if plan:
            pl.when(step_no == 0)(lambda: comm.start(*plan))
        wlane = [_head_lanes(width, h) for h in (0, 1)]
        vlane = [_head_lanes(LANES, h) for h in (0, 1)]
        causal = lax.broadcasted_iota(I32, (t, t), 0) <= lax.broadcasted_iota(I32, (t, t), 1)
        q_all = q_ref[...] * pre if dilated else q_ref[...]
        for h in (0, 1):
            q_heads[h] = jnp.where(wlane[h], q_all, jnp.zeros_like(q_all))
            do_heads[h] = jnp.where(vlane[h], do_ref[...], jnp.zeros_like(do_ref[...]))
        prod = jnp.transpose(do_ref[...].astype(F32) * o_ref[...].astype(F32))
        rowdot[0:1, :] = jnp.sum(prod[0:HEAD_V], axis=0, keepdims=True)
        rowdot[1:2, :] = jnp.sum(prod[HEAD_V:], axis=0, keepdims=True)
        dq_acc[...] = jnp.zeros_like(dq_acc)

        def k_tile(j, _):
            ks = pl.multiple_of(j * t, t)
            kj = k_ref[pl.ds(ks, t), :]
            vj = v_ref[pl.ds(ks, t), :]
            kh = [jnp.where(wlane[h], kj, jnp.zeros_like(kj)) for h in (0, 1)]
            if dq_transposed:
                kh = [jnp.transpose(kh[h].astype(F32)).astype(BF16) for h in (0, 1)]
            dk_acc[...] = jnp.zeros_like(dk_acc)
            dv_acc[...] = jnp.zeros_like(dv_acc)

            def operands(i):
                qs = pl.multiple_of(i * t, t)
                return [q_heads[h, pl.ds(qs, t), :] for h in (0, 1)], [do_heads[h, pl.ds(qs, t), :] for h in (0, 1)]

            def products(i):
                qih, doih = operands(i)
                scores = tuple(_dot_nt(kj, qih[h]) for h in (0, 1))
                return scores + tuple(_dot_nt(vj, doih[h]) for h in (0, 1)) if width > LANES else scores

            def q_tile(n, carry, last):
                i = nq - 1 - n
                ahead = () if last else products(i - 1)
                qs = pl.multiple_of(i * t, t)
                qih, doih = operands(i)
                s0, s1 = carry[:2]
                dps = carry[2:] if width > LANES else [_dot_nt(vj, doih[h]) for h in (0, 1)]
                dq_i = jnp.zeros((width, t) if dq_transposed else (t, width), F32)
                for h, (s, dp) in enumerate(((s0, dps[0]), (s1, dps[1]))):
                    if dilated:
                        s = s + bias_ref[h, i - j]
                    else:
                        s = s * scale
                        if last:
                            s = jnp.where(causal, s, NEG)
                    p = jnp.exp(s - lse_ref[h:h + 1, pl.ds(qs, t)])
                    ds = p * (dp - rowdot[h:h + 1, pl.ds(qs, t)])
                    ds = (ds if dilated else ds * scale).astype(BF16)
                    dv_acc[...] += _dot(p.astype(BF16), doih[h])
                    dk_acc[...] += _dot(ds, qih[h])
                    dq_i = dq_i + (_dot(kh[h], ds) if dq_transposed else _dot_tn(ds, kh[h]))
                if dq_transposed:
                    dq_acc[:, pl.ds(qs, t)] += dq_i
                else:
                    dq_acc[pl.ds(qs, t), :] += dq_i
                return ahead

            q_tile(nq - 1 - j, lax.fori_loop(0, nq - 1 - j, functools.partial(q_tile, last=False), products(nq - 1)), True)
            dk_ref[pl.ds(ks, t), :] = dk_acc[...].astype(BF16)
            dv_ref[pl.ds(ks, t), :] = dv_acc[...].astype(BF16)
            return 0

        lax.fori_loop(0, nq, k_tile, 0)
        dq_ref[...] = ((jnp.transpose(dq_acc[...]) if dq_transposed else dq_acc[...]) * pre).astype(BF16)
        if plan:
            pl.when(step_no == steps - 1)(lambda: comm.finish(*plan))

    tokens = batch * seq
    bias_spec = (pl.BlockSpec((2, nq, t, t), lambda b, p: (p, 0, 0, 0)) if dilated
                 else pl.BlockSpec((None, 8, LANES), lambda b, p: (0, 0, 0)))
    n = comm.n if comm else 0
    return pl.pallas_call(
        body, name=name, grid=(batch, N_PAIRS),
        out_shape=[jax.ShapeDtypeStruct((tokens, N_PAIRS * width), BF16), jax.ShapeDtypeStruct((tokens, N_PAIRS * width), BF16),
                   jax.ShapeDtypeStruct((tokens, DIL_WIDTH), BF16)] + (comm.out_shape if comm else []),
        in_specs=[pl.BlockSpec((seq, width), lambda b, p: (b, cq + p)),
                  pl.BlockSpec((seq, width), lambda b, p: (b, ck + p)),
                  pl.BlockSpec((seq, LANES), lambda b, p: (b, cv + p)),
                  pl.BlockSpec((seq, LANES), lambda b, p: (b, p)),
                  pl.BlockSpec((seq, LANES), lambda b, p: (b, p)),
                  pl.BlockSpec((None, 8, seq), lambda b, p: (b * N_PAIRS + p, 0, 0)),
                  bias_spec] + [ANY] * n,
        out_specs=[pl.BlockSpec((seq, width), lambda b, p: (b, p)),
                   pl.BlockSpec((seq, width), lambda b, p: (b, p)),
                   pl.BlockSpec((seq, LANES), lambda b, p: (b, p))] + [ANY] * n,
        scratch_shapes=[pltpu.VMEM((width, seq) if dq_transposed else (seq, width), F32),
                        pltpu.VMEM((t, width), F32), pltpu.VMEM((t, LANES), F32),
                        pltpu.VMEM((8, seq), F32), pltpu.VMEM((2, seq, width), BF16), pltpu.VMEM((2, seq, LANES), BF16)]
        + (comm.scratch if comm else []),
        compiler_params=_cp("arbitrary", "arbitrary") if comm else _cp("parallel", "parallel"),
    )(q, k, v, o, do, lse, bias, *comm_arrays)


def _rms(xf, g):
    r = lax.rsqrt(jnp.mean(xf * xf, axis=1, keepdims=True) + RMS_EPS)
    return xf * r * g, r


def _rms_bwd(dy, xf, r, g):
    gy = dy * g
    dx = r * gy - xf * (r * r * r) * jnp.mean(gy * xf, axis=1, keepdims=True)
    return dx, dy * xf * r


def _ln_bwd(dy, xhat, rstd, g):
    dxh = dy * g
    return rstd * (dxh - jnp.mean(dxh, axis=1, keepdims=True) - xhat * jnp.mean(dxh * xhat, axis=1, keepdims=True))


def _rope_slabs(q, cos, sin, transpose):
    first_half = (lax.broadcasted_iota(I32, (1, LANES), 1) % ROPE) < ROPE // 2
    out = []
    for p in range(N_PAIRS):
        blk = q[:, p * PAIR_W + LANES:(p + 1) * PAIR_W]
        y = blk * sin if transpose else blk
        up, down = pltpu.roll(y, LANES - ROPE // 2, 1), pltpu.roll(y, ROPE // 2, 1)
        rot = jnp.where(first_half, up, -down) if transpose else jnp.where(first_half, -up, down) * sin
        out += [q[:, p * PAIR_W:p * PAIR_W + LANES], blk * cos + rot]
    return jnp.concatenate(out, axis=1)


def _fwd_proj(x, w_in_ext, w1, wk, wv, g_q, g_kv, cext, sext, cs128, *, seq):
    tokens = x.shape[0]
    tm = min(TOKEN_TILE, seq)
    ns = seq // tm

    def body(x_ref, win_ref, w1_ref, wk_ref, wv_ref, gq_ref, gkv_ref, c_ref, s_ref, cs_ref,
             low_ref, gates_ref, qkvd_ref, qp_ref, kp_ref, vm_ref, qn_ref, kvn_ref, xb_ref):
        xt = x_ref[...].astype(BF16)
        xb_ref[...] = xt
        low = _dot(xt, win_ref[:, 0:LOW_W])
        low_ref[...] = low
        qkvd_ref[...] = _dot(xt, win_ref[:, LOW_W:LOW_W + 3 * DIL_WIDTH]).astype(BF16)
        gates_ref[...] = _dot(xt, win_ref[:, LOW_W + 3 * DIL_WIDTH:]).astype(BF16)
        qn = _rms(low[:, 0:Q_LORA], gq_ref[...])[0].astype(BF16)
        kvn = _rms(low[:, Q_LORA:Q_LORA + KV_LORA], gkv_ref[...])[0].astype(BF16)
        qn_ref[...] = qn
        kvn_ref[...] = kvn
        qp_ref[...] = _rope_slabs(_dot(qn, w1_ref[...]), c_ref[...], s_ref[...], False).astype(BF16)
        kr = low[:, Q_LORA + KV_LORA:] * cs_ref[...]
        kr = kr + pltpu.roll(kr, LANES - ROPE, 1)
        lane = lax.broadcasted_iota(I32, kr.shape, 1)
        kr = jnp.where(lane < ROPE, kr, 0.0)
        kr = (kr + pltpu.roll(kr, ROPE, 1)).astype(BF16)
        kn = _dot(kvn, wk_ref[...]).astype(BF16)
        kp_ref[...] = jnp.concatenate([blk for p in range(N_PAIRS) for blk in (kn[:, p * LANES:(p + 1) * LANES], kr)], axis=1)
        vm_ref[...] = _dot(kvn, wv_ref[...]).astype(BF16)

    n_gates = 2 * D_MODEL
    outs = [(LOW_W, F32), (n_gates, BF16), (3 * DIL_WIDTH, BF16), (N_PAIRS * PAIR_W, BF16), (N_PAIRS * PAIR_W, BF16),
            (DIL_WIDTH, BF16), (Q_LORA, BF16), (KV_LORA, BF16), (D_MODEL, BF16)]
    return pl.pallas_call(
        body, name="fwd_proj", grid=(tokens // tm,),
        out_shape=tuple(jax.ShapeDtypeStruct((tokens, w), dt) for w, dt in outs),
        in_specs=[_rows(tm, D_MODEL), _full(w_in_ext.shape), _full(w1.shape), _full(wk.shape),
                  _full(wv.shape), _full(g_q.shape), _full(g_kv.shape),
                  pl.BlockSpec((tm, LANES), lambda i: (i % ns, 1)),
                  pl.BlockSpec((tm, LANES), lambda i: (i % ns, 1)),
                  pl.BlockSpec((tm, LANES), lambda i: (i % ns, 0))],
        out_specs=tuple(_rows(tm, w) for w, _ in outs),
        compiler_params=_cp("parallel"),
    )(x, w_in_ext, w1, wk, wv, g_q, g_kv, cext, sext, cs128)


def _fwd_mix(o_a, o_b, gates, x, b_gate, w_oa, w_ob, w_out, ln_g, ln_b, *, seq):
    tokens = x.shape[0]
    tm = min(MIX_TILE, seq)

    def body(oa_ref, ob_ref, gt_ref, x_ref, bg_ref, woa_ref, wob_ref, wout_ref, g_ref, b_ref,
             hb_ref, xhat_ref, rstd_ref, ya_ref, yb_ref, mix_ref):
        ya = _dot(oa_ref[...], woa_ref[...])
        yb = _dot(ob_ref[...], wob_ref[...])
        g0 = _sigmoid(gt_ref[:, 0:D_MODEL].astype(F32) + bg_ref[0:1, :])
        g1 = _sigmoid(gt_ref[:, D_MODEL:].astype(F32) + bg_ref[1:2, :])
        mix = (g0 * ya + g1 * yb).astype(BF16)
        z = ALPHA * x_ref[...] + _dot(mix, wout_ref[...])
        zc = z - jnp.mean(z, axis=1, keepdims=True)
        rstd = lax.rsqrt(jnp.mean(zc * zc, axis=1, keepdims=True) + LN_EPS)
        xhat = zc * rstd
        hb_ref[...] = (xhat * g_ref[...] + b_ref[...]).astype(BF16)
        xhat_ref[...] = xhat
        rstd_ref[...] = jnp.broadcast_to(rstd, (tm, LANES))
        ya_ref[...] = ya.astype(BF16)
        yb_ref[...] = yb.astype(BF16)
        mix_ref[...] = mix

    outs = [(D_MODEL, BF16), (D_MODEL, F32), (LANES, F32), (D_MODEL, BF16), (D_MODEL, BF16), (D_MODEL, BF16)]
    return pl.pallas_call(
        body, name="fwd_mix", grid=(tokens // tm,),
        out_shape=tuple(jax.ShapeDtypeStruct((tokens, w), dt) for w, dt in outs),
        in_specs=[_rows(tm, DIL_WIDTH), _rows(tm, DIL_WIDTH), _rows(tm, 2 * D_MODEL), _rows(tm, D_MODEL),
                  _full(b_gate.shape), _full(w_oa.shape), _full(w_ob.shape), _full(w_out.shape),
                  _full(ln_g.shape), _full(ln_b.shape)],
        out_specs=tuple(_rows(tm, w) for w, _ in outs),
        compiler_params=_cp("parallel"),
    )(o_a, o_b, gates, x, b_gate, w_oa, w_ob, w_out, ln_g, ln_b)


def _fwd_mlp(hb, xhat1, target, w_ff1, w_ff2, ln1_g, ln1_b, ln_g, ln_b, *, seq):
    tokens = hb.shape[0]
    tm = min(2 * TOKEN_TILE, seq)
    tf = FF_SHARD
    nf = N_DEV // FF_STEP

    def body(hb_ref, xh_ref, tg_ref, w1_ref, w2_ref, g1_ref, b1_ref, g_ref, b_ref, u_ref, dz_ref, dzb_ref, stat_ref, acc):
        i, j = pl.program_id(0), pl.program_id(1)

        @pl.when((i == 0) & (j == 0))
        def _():
            stat_ref[...] = jnp.zeros_like(stat_ref)

        @pl.when(j == 0)
        def _():
            acc[...] = jnp.zeros_like(acc)

        acts = []
        for s in range(FF_STEP):
            u = _dot(hb_ref[...], w1_ref[s])
            u_ref[:, s * tf:(s + 1) * tf] = u.astype(BF16)
            acts.append(jnp.square(jnp.maximum(u, 0.0)).astype(BF16))
        acc[...] += _dot(jnp.concatenate(acts, axis=1), w2_ref[...])

        @pl.when(j == nf - 1)
        def _():
            z = ALPHA * (xh_ref[...] * g1_ref[...] + b1_ref[...]) + acc[...]
            zc = z - jnp.mean(z, axis=1, keepdims=True)
            rstd = lax.rsqrt(jnp.mean(zc * zc, axis=1, keepdims=True) + LN_EPS)
            xhat = zc * rstd
            err = xhat * g_ref[...] + b_ref[...] - tg_ref[...]
            dy = err * (1.0 / D_MODEL)
            dz = _ln_bwd(dy, xhat, rstd, g_ref[...])
            dz_ref[...] = dz
            dzb_ref[...] = dz.astype(BF16)
            stat_ref[0:1, :] += jnp.sum(dy * xhat, axis=0, keepdims=True)
            stat_ref[1:2, :] += jnp.sum(dy, axis=0, keepdims=True)
            stat_ref[2:3, :] += jnp.sum(jnp.sum(err * err, axis=1, keepdims=True), axis=0, keepdims=True) * (0.5 / D_MODEL)

    return pl.pallas_call(
        body, name="fwd_mlp", grid=(tokens // tm, nf),
        out_shape=(jax.ShapeDtypeStruct((tokens, D_FF), BF16), jax.ShapeDtypeStruct((tokens, D_MODEL), F32),
                   jax.ShapeDtypeStruct((tokens, D_MODEL), BF16), jax.ShapeDtypeStruct((8, D_MODEL), F32)),
        in_specs=[_rows(tm, D_MODEL), _rows(tm, D_MODEL), _rows(tm, D_MODEL),
                  pl.BlockSpec((FF_STEP, D_MODEL, tf), lambda i, j: (j, 0, 0)),
                  pl.BlockSpec((FF_STEP * tf, D_MODEL), lambda i, j: (j, 0)),
                  _full(ln1_g.shape), _full(ln1_b.shape), _full(ln_g.shape), _full(ln_b.shape)],
        out_specs=(pl.BlockSpec((tm, FF_STEP * tf), lambda i, j: (i, j)), _rows(tm, D_MODEL), _rows(tm, D_MODEL),
                   _full((8, D_MODEL))),
        scratch_shapes=[pltpu.VMEM((tm, D_MODEL), F32)],
        compiler_params=_cp("arbitrary", "arbitrary"),
    )(hb, xhat1, target, w_ff1, w_ff2, ln1_g, ln1_b, ln_g, ln_b)


def _bwd_mlp(dz2, dz2b, u, xhat1, rstd1, w_ff1, w_ff2, ln_g, *, seq):
    tokens = dz2.shape[0]
    tm = min(2 * TOKEN_TILE, seq)
    tf = FF_SHARD
    nf = N_DEV // FF_STEP

    def body(dz_ref, dzb_ref, u_ref, xh_ref, rs_ref, w1_ref, w2_ref, g_ref, du_ref, dz1_ref, dz1b_ref, stat_ref, acc):
        i, j = pl.program_id(0), pl.program_id(1)

        @pl.when((i == 0) & (j == 0))
        def _():
            stat_ref[...] = jnp.zeros_like(stat_ref)

        @pl.when(j == 0)
        def _():
            acc[...] = jnp.zeros_like(acc)

        da = _dot_nt(dzb_ref[...], w2_ref[...])
        du = (da * (2.0 * jnp.maximum(u_ref[...].astype(F32), 0.0))).astype(BF16)
        du_ref[...] = du
        part = _dot_nt(du[:, 0:tf], w1_ref[0])
        for s in range(1, FF_STEP):
            part = part + _dot_nt(du[:, s * tf:(s + 1) * tf], w1_ref[s])
        acc[...] += part

        @pl.when(j == nf - 1)
        def _():
            dh = ALPHA * dz_ref[...] + acc[...]
            xhat = xh_ref[...]
            dz1 = _ln_bwd(dh, xhat, rs_ref[:, 0:1], g_ref[...])
            dz1_ref[...] = dz1
            dz1b_ref[...] = dz1.astype(BF16)
            stat_ref[0:1, :] += jnp.sum(dh * xhat, axis=0, keepdims=True)
            stat_ref[1:2, :] += jnp.sum(dh, axis=0, keepdims=True)

    return pl.pallas_call(
        body, name="bwd_mlp", grid=(tokens // tm, nf),
        out_shape=(jax.ShapeDtypeStruct((tokens, D_FF), BF16), jax.ShapeDtypeStruct((tokens, D_MODEL), F32),
                   jax.ShapeDtypeStruct((tokens, D_MODEL), BF16), jax.ShapeDtypeStruct((8, D_MODEL), F32)),
        in_specs=[_rows(tm, D_MODEL), _rows(tm, D_MODEL), pl.BlockSpec((tm, FF_STEP * tf), lambda i, j: (i, j)),
                  _rows(tm, D_MODEL), _rows(tm, LANES),
                  pl.BlockSpec((FF_STEP, D_MODEL, tf), lambda i, j: (j, 0, 0)),
                  pl.BlockSpec((FF_STEP * tf, D_MODEL), lambda i, j: (j, 0)),
                  _full(ln_g.shape)],
        out_specs=(pl.BlockSpec((tm, FF_STEP * tf), lambda i, j: (i, j)), _rows(tm, D_MODEL), _rows(tm, D_MODEL),
                   _full((8, D_MODEL))),
        scratch_shapes=[pltpu.VMEM((tm, D_MODEL), F32)],
        compiler_params=_cp("arbitrary", "arbitrary"),
    )(dz2, dz2b, u, xhat1, rstd1, w_ff1, w_ff2, ln_g)


def _bwd_mix(dz1b, gates, y_a, y_b, b_gate, w_oa, w_ob, w_out, *, seq):
    tokens = dz1b.shape[0]
    tm = min(MIX_TILE, seq)

    def body(dz_ref, gt_ref, ya_ref, yb_ref, bg_ref, woa_ref, wob_ref, wout_ref,
             dgt_ref, dya_ref, dyb_ref, doa_ref, dob_ref, stat_ref):
        @pl.when(pl.program_id(0) == 0)
        def _():
            stat_ref[...] = jnp.zeros_like(stat_ref)

        dmix = _dot_nt(dz_ref[...], wout_ref[...])
        for k, (y_ref, w_ref, dy_ref, do_ref) in enumerate(((ya_ref, woa_ref, dya_ref, doa_ref), (yb_ref, wob_ref, dyb_ref, dob_ref))):
            g = _sigmoid(gt_ref[:, k * D_MODEL:(k + 1) * D_MODEL].astype(F32) + bg_ref[k:k + 1, :])
            dgate = dmix * y_ref[...].astype(F32) * g * (1.0 - g)
            dgt_ref[:, k * D_MODEL:(k + 1) * D_MODEL] = dgate.astype(BF16)
            stat_ref[k:k + 1, :] += jnp.sum(dgate, axis=0, keepdims=True)
            dy = (dmix * g).astype(BF16)
            dy_ref[...] = dy
            do_ref[...] = _dot_nt(dy, w_ref[...]).astype(BF16)

    outs = [(2 * D_MODEL, BF16), (D_MODEL, BF16), (D_MODEL, BF16), (DIL_WIDTH, BF16), (DIL_WIDTH, BF16)]
    return pl.pallas_call(
        body, name="bwd_mix", grid=(tokens // tm,),
        out_shape=tuple(jax.ShapeDtypeStruct((tokens, w), dt) for w, dt in outs) + (jax.ShapeDtypeStruct((8, D_MODEL), F32),),
        in_specs=[_rows(tm, D_MODEL), _rows(tm, 2 * D_MODEL), _rows(tm, D_MODEL), _rows(tm, D_MODEL),
                  _full(b_gate.shape), _full(w_oa.shape), _full(w_ob.shape), _full(w_out.shape)],
        out_specs=tuple(_rows(tm, w) for w, _ in outs) + (_full((8, D_MODEL)),),
        compiler_params=_cp("arbitrary"),
    )(dz1b, gates, y_a, y_b, b_gate, w_oa, w_ob, w_out)


def _bwd_proj(dqp, dkp, dvm, dq_d, dk_d, dv_d, dgates, dz1, low, w_in_ext, w1, wk, wv, g_q, g_kv, cext, sext, cs128, *, seq):
    tokens = dz1.shape[0]
    tm = min(TOKEN_TILE, seq)
    ns = seq // tm

    def body(dqp_ref, dkp_ref, dvm_ref, dqd_ref, dkd_ref, dvd_ref, dgt_ref, dz_ref, low_ref, win_ref, w1_ref, wk_ref,
             wv_ref, gq_ref, gkv_ref, c_ref, s_ref, cs_ref, dx_ref, dproj_ref, da_ref, dkn_ref, stat_ref):
        @pl.when(pl.program_id(0) == 0)
        def _():
            stat_ref[...] = jnp.zeros_like(stat_ref)

        low = low_ref[...]
        d_a = _rope_slabs(dqp_ref[...].astype(F32), c_ref[...], s_ref[...], True).astype(BF16)
        da_ref[...] = d_a
        q_a = low[:, 0:Q_LORA]
        _, rq = _rms(q_a, gq_ref[...])
        dq_a, gq_terms = _rms_bwd(_dot_nt(d_a, w1_ref[...]), q_a, rq, gq_ref[...])
        kv_a = low[:, Q_LORA:Q_LORA + KV_LORA]
        _, rkv = _rms(kv_a, gkv_ref[...])
        dkn = jnp.concatenate([dkp_ref[:, p * PAIR_W:p * PAIR_W + LANES] for p in range(N_PAIRS)], axis=1)
        dkn_ref[...] = dkn
        dkv_a, gkv_terms = _rms_bwd(_dot_nt(dkn, wk_ref[...]) + _dot_nt(dvm_ref[...], wv_ref[...]), kv_a, rkv, gkv_ref[...])
        dkr = sum(dkp_ref[:, p * PAIR_W + LANES:(p + 1) * PAIR_W].astype(F32) for p in range(N_PAIRS))
        dkr = dkr + pltpu.roll(dkr, LANES - ROPE, 1)
        dkr = jnp.where(lax.broadcasted_iota(I32, dkr.shape, 1) < ROPE, dkr, 0.0)
        dkr = (dkr + pltpu.roll(dkr, ROPE, 1)) * cs_ref[...]
        stat_ref[0:1, 0:Q_LORA] += jnp.sum(gq_terms, axis=0, keepdims=True)
        stat_ref[1:2, 0:KV_LORA] += jnp.sum(gkv_terms, axis=0, keepdims=True)
        dproj_ref[:, 0:Q_LORA] = dq_a.astype(BF16)
        dproj_ref[:, Q_LORA:Q_LORA + KV_LORA] = dkv_a.astype(BF16)
        dproj_ref[:, Q_LORA + KV_LORA:LOW_W] = dkr.astype(BF16)
        dproj_ref[:, LOW_W:LOW_W + DIL_WIDTH] = dqd_ref[...]
        dproj_ref[:, LOW_W + DIL_WIDTH:LOW_W + 2 * DIL_WIDTH] = dkd_ref[...]
        dproj_ref[:, LOW_W + 2 * DIL_WIDTH:LOW_W + 3 * DIL_WIDTH] = dvd_ref[...]
        dproj_ref[:, LOW_W + 3 * DIL_WIDTH:] = dgt_ref[...]
        dx_ref[...] = ALPHA * dz_ref[...] + _dot_nt(dproj_ref[...], win_ref[...])

    wide = N_PAIRS * PAIR_W
    return pl.pallas_call(
        body, name="bwd_proj", grid=(tokens // tm,),
        out_shape=(jax.ShapeDtypeStruct((tokens, D_MODEL), F32), jax.ShapeDtypeStruct((tokens, IN_EXT), BF16),
                   jax.ShapeDtypeStruct((tokens, wide), BF16), jax.ShapeDtypeStruct((tokens, N_HEADS * NOPE), BF16),
                   jax.ShapeDtypeStruct((8, D_MODEL), F32)),
        in_specs=[_rows(tm, wide), _rows(tm, wide), _rows(tm, DIL_WIDTH), _rows(tm, DIL_WIDTH), _rows(tm, DIL_WIDTH),
                  _rows(tm, DIL_WIDTH), _rows(tm, 2 * D_MODEL),
                  _rows(tm, D_MODEL), _rows(tm, LOW_W), _full(w_in_ext.shape), _full(w1.shape),
                  _full(wk.shape), _full(wv.shape), _full(g_q.shape), _full(g_kv.shape),
                  pl.BlockSpec((tm, LANES), lambda i: (i % ns, 1)), pl.BlockSpec((tm, LANES), lambda i: (i % ns, 1)),
                  pl.BlockSpec((tm, LANES), lambda i: (i % ns, 0))],
        out_specs=(_rows(tm, D_MODEL), _rows(tm, IN_EXT), _rows(tm, wide), _rows(tm, N_HEADS * NOPE), _full((8, D_MODEL))),
        compiler_params=_cp("arbitrary"),
    )(dqp, dkp, dvm, dq_d, dk_d, dv_d, dgates, dz1, low, w_in_ext, w1, wk, wv, g_q, g_kv, cext, sext, cs128)


def _wgrad(a, b, name, square_relu=False, by_shard=False):
    tokens, ka = a.shape
    n = b.shape[1]
    if ka <= 512 or ka % 512 == 0:
        tka = min(ka, 512)
    else:
        tka = max(w for w in range(LANES, min(ka, 2304) + 1, LANES) if ka % w == 0)
    shard = n // N_DEV
    tn = WGRAD_SHARDS * shard if by_shard else max(w for w in range(LANES, min(n, 2304) + 1, LANES) if n % w == 0)
    tt = min(tokens, 1024)
    nt = tokens // tt

    def body(a_ref, b_ref, o_ref, acc):
        kt = pl.program_id(2)

        @pl.when(kt == 0)
        def _():
            acc[...] = jnp.zeros_like(acc)

        at = a_ref[...]
        if square_relu:
            at = jnp.square(jnp.maximum(at.astype(F32), 0.0)).astype(BF16)
        acc[...] += _dot_tn(at, b_ref[...])

        @pl.when(kt == nt - 1)
        def _():
            if by_shard:
                for s in range(WGRAD_SHARDS):
                    o_ref[s] = acc[:, s * shard:(s + 1) * shard].astype(BF16)
            else:
                o_ref[...] = acc[...].astype(BF16)

    if by_shard:
        out_shape, out_spec = (N_DEV, ka, shard), pl.BlockSpec((WGRAD_SHARDS, tka, shard), lambda i, j, k: (j, i, 0))
    else:
        out_shape, out_spec = (ka, n), pl.BlockSpec((tka, tn), lambda i, j, k: (i, j))
    return pl.pallas_call(
        body, name=name, grid=(ka // tka, n // tn, nt), out_shape=jax.ShapeDtypeStruct(out_shape, BF16),
        in_specs=[pl.BlockSpec((tt, tka), lambda i, j, k: (k, i)), pl.BlockSpec((tt, tn), lambda i, j, k: (k, j))],
        out_specs=out_spec,
        scratch_shapes=[pltpu.VMEM((tka, tn), F32)],
        compiler_params=_cp("parallel", "parallel", "arbitrary"),
    )(a, b)


def _adam_math(w, g, m, v):
    m = ADAM_B1 * m + (1.0 - ADAM_B1) * g
    v = ADAM_B2 * v + (1.0 - ADAM_B2) * jnp.square(g)
    m_hat = m / (1.0 - ADAM_B1 ** ADAM_STEP)
    v_hat = v / (1.0 - ADAM_B2 ** ADAM_STEP)
    return -ADAM_LR * (m_hat / (jnp.sqrt(v_hat) + ADAM_EPS) + ADAM_WD * w), m, v


def _adamw(items, name):
    steps = min(_tiles(*w.shape)[0] for w, *_ in items)
    n_items = len(items)

    def body(slot_ref, *refs):
        ins, outs = refs[:5 * n_items], refs[5 * n_items:]
        for k, (_, _, _, _, parts) in enumerate(items):
            w_ref, m_ref, v_ref, own_ref, p_ref = ins[5 * k:5 * k + 5]
            g_ref, d_ref, nm_ref, nv_ref = outs[4 * k:4 * k + 4]
            g = own_ref[...].astype(F32)
            for d in range(parts.shape[0]):
                g = g + p_ref[d].astype(F32)
            g_ref[...] = g
            d_ref[...], nm_ref[...], nv_ref[...] = _adam_math(w_ref[...], g, m_ref[...], v_ref[...])

    x, y, c = _place()
    in_specs, out_specs, out_shape, args = [], [], [], []
    for w, m, v, own, parts in items:
        rows, cols = w.shape
        _, tile, at = _tiles(rows, cols, steps)
        blk = pl.BlockSpec(tile, lambda i, slot, at=at: at(i))
        own_blk = blk if own.ndim == 2 else pl.BlockSpec((None, *tile), lambda i, slot, at=at: (slot[0], *at(i)))
        in_specs += [blk, blk, blk, own_blk, pl.BlockSpec((parts.shape[0], *tile), lambda i, slot, at=at: (0, *at(i)))]
        out_specs += [blk] * 4
        out_shape += [jax.ShapeDtypeStruct((rows, cols), F32)] * 4
        args += [w, m, v, own, parts]
    out = pl.pallas_call(
        body, name=name,
        grid_spec=pltpu.PrefetchScalarGridSpec(num_scalar_prefetch=1, grid=(steps,), in_specs=in_specs, out_specs=out_specs),
        out_shape=out_shape, compiler_params=_cp("parallel"),
    )(jnp.reshape(4 * x + 2 * y + c, (1,)).astype(I32), *args)
    return [tuple(out[4 * k:4 * k + 4]) for k in range(n_items)]


def _adamw_small(parts, w, m, v):
    _, rows, cols = parts.shape

    def body(p_ref, w_ref, m_ref, v_ref, g_ref, d_ref, nm_ref, nv_ref):
        g = p_ref[0]
        for d in range(1, N_DEV):
            g = g + p_ref[d]
        g_ref[...] = g
        d_ref[...], nm_ref[...], nv_ref[...] = _adam_math(w_ref[...], g, m_ref[...], v_ref[...])

    return pl.pallas_call(
        body, name="adamw_replicated", out_shape=(jax.ShapeDtypeStruct((rows, cols), F32),) * 4,
        in_specs=[_full(parts.shape)] + [_full((rows, cols))] * 3, out_specs=(_full((rows, cols)),) * 4, grid=(1,),
        compiler_params=_cp("arbitrary"),
    )(parts, w, m, v)


def _pad_rows(a2d, mult):
    pad = (-a2d.shape[-2]) % mult
    return jnp.pad(a2d, [(0, 0)] * (a2d.ndim - 2) + [(0, pad), (0, 0)]) if pad else a2d


def _pad_cols(a):
    pad = (-a.shape[-1]) % LANES
    return jnp.pad(a, [(0, 0)] * (a.ndim - 1) + [(0, pad)]) if pad else a


def _rot_cols(w):
    half = ROPE // 2
    return jnp.concatenate([-w[..., half:], w[..., :half]], axis=-1)


def _unrot_cols(dw):
    half = ROPE // 2
    return jnp.concatenate([dw[..., half:], -dw[..., :half]], axis=-1)


def _from_col_shards(stacked):
    return stacked.transpose(1, 0, 2).reshape(stacked.shape[1], -1)


def _to_col_shards(full):
    r = full.shape[0]
    return full.reshape(r, N_DEV, -1).transpose(1, 0, 2)


def _rope_tables(seq):
    half = ROPE // 2
    inv = jnp.power(ROPE_THETA, -jnp.arange(half, dtype=F32) / half)
    ang = jnp.arange(seq, dtype=F32)[:, None] * inv[None, :]
    cos = jnp.concatenate([jnp.cos(ang)] * 2, axis=1)
    sin = jnp.concatenate([jnp.sin(ang)] * 2, axis=1)
    ones, zeros = jnp.ones((seq, 2 * NOPE), F32), jnp.zeros((seq, 2 * NOPE), F32)
    pad = jnp.zeros((seq, PAIR_W - 2 * NOPE - 2 * ROPE), F32)
    cext = jnp.concatenate([ones, cos, cos, pad], axis=1)
    sext = jnp.concatenate([zeros, sin, sin, pad], axis=1)
    cs128 = jnp.concatenate([cos, sin, jnp.zeros((seq, LANES - 2 * ROPE), F32)], axis=1)
    return cext, sext, cs128


def _pair_slabs(nope, rope):
    k = nope.shape[0]
    nope = nope.reshape(k, N_PAIRS, 2 * NOPE)
    rope = jnp.zeros((k, N_PAIRS, 2 * ROPE), nope.dtype) if rope is None else rope.reshape(k, N_PAIRS, 2 * ROPE)
    pad = jnp.zeros((k, N_PAIRS, PAIR_W - 2 * NOPE - 2 * ROPE), nope.dtype)
    return jnp.concatenate([nope, rope, pad], axis=2).reshape(k, N_PAIRS * PAIR_W)


def _split_slabs(slabs):
    k = slabs.shape[0]
    s = slabs.reshape(k, N_PAIRS, PAIR_W)
    return s[:, :, :2 * NOPE].reshape(k, N_HEADS, NOPE), s[:, :, 2 * NOPE:2 * NOPE + 2 * ROPE].reshape(k, N_HEADS, ROPE)


def kernel(x, w_in, b_gate, g_q_a, w_uq, g_kv_a, w_ukv, w_o_mla, w_o_dil, w_out, ln1_g, ln1_b, w_ff1, w_ff2, ln2_g, ln2_b, loss_target, m_w_in, m_b_gate, m_g_q_a, m_w_uq, m_g_kv_a, m_w_ukv, m_w_o_mla, m_w_o_dil, m_w_out, m_ln1_g, m_ln1_b, m_w_ff1, m_w_ff2, m_ln2_g, m_ln2_b, v_w_in, v_b_gate, v_g_q_a, v_w_uq, v_g_kv_a, v_w_ukv, v_w_o_mla, v_w_o_dil, v_w_out, v_ln1_g, v_ln1_b, v_w_ff1, v_w_ff2, v_ln2_g, v_ln2_b):
    batch, seq, _ = x.shape
    tokens = batch * seq
    weights = dict(w_in=w_in, w_uq=w_uq, w_ukv=w_ukv, w_o_mla=w_o_mla, w_o_dil=w_o_dil, w_out=w_out, w_ff1=w_ff1, w_ff2=w_ff2, b_gate=b_gate)
    mom_m = dict(w_in=m_w_in, w_uq=m_w_uq, w_ukv=m_w_ukv, w_o_mla=m_w_o_mla, w_o_dil=m_w_o_dil, w_out=m_w_out, w_ff1=m_w_ff1, w_ff2=m_w_ff2, b_gate=m_b_gate)
    mom_v = dict(w_in=v_w_in, w_uq=v_w_uq, w_ukv=v_w_ukv, w_o_mla=v_w_o_mla, w_o_dil=v_w_o_dil, w_out=v_w_out, w_ff1=v_w_ff1, w_ff2=v_w_ff2, b_gate=v_b_gate)

    first = ["w_in", "w_uq", "w_ukv"]
    widths = [weights[n].shape[2] for n in first]
    shards = [weights["w_in"][0].T.astype(BF16)] + [_pad_cols(weights[n][0].astype(BF16)) for n in first[1:]]
    g_in, g_uq, g_ukv = _run_comm(_Gather(shards), shards, "all_gather_first_weights")
    g_uq, g_ukv = g_uq[:, :, :widths[1]], g_ukv[:, :, :widths[2]]

    s1, s2, n_in = Q_LORA + KV_LORA, Q_LORA + KV_LORA + ROPE, N_DEV * widths[0]

    def w_in_cols(lo, hi):
        out = []
        while lo < hi:
            d, off = divmod(lo, widths[0])
            take = min(hi - lo, widths[0] - off)
            out.append(g_in[d][off:off + take].T)
            lo += take
        return out

    w_in_ext = jnp.concatenate(w_in_cols(0, s2) + [_rot_cols(jnp.concatenate(w_in_cols(s1, s2), axis=1)),
                                                   jnp.zeros((D_MODEL, LOW_W - s2 - ROPE), BF16)] + w_in_cols(s2, n_in), axis=1)
    uq = _from_col_shards(g_uq).reshape(Q_LORA, N_HEADS, NOPE + ROPE)
    w1 = _pair_slabs(uq[:, :, :NOPE], uq[:, :, NOPE:])
    ukv = _from_col_shards(g_ukv).reshape(KV_LORA, N_HEADS, NOPE + HEAD_V)
    wk = ukv[:, :, :NOPE].reshape(KV_LORA, N_HEADS * NOPE)
    wv = ukv[:, :, NOPE:].reshape(KV_LORA, N_HEADS * HEAD_V)
    cext, sext, cs128 = _rope_tables(seq)
    dil_bias = _dilated_bias_table(seq)
    no_bias = jnp.zeros((1, 8, LANES), F32)

    x2 = x.reshape(tokens, D_MODEL)
    low, gates, qkvd, qp, kp, vm, qn, kvn, xb = _fwd_proj(x2, w_in_ext, w1, wk, wv, g_q_a, g_kv_a, cext, sext, cs128, seq=seq)
    bg = b_gate[0]
    bg_hi = bg.astype(BF16)
    bg_lo = (bg - bg_hi.astype(F32)).astype(BF16)
    later = [weights[n][0].astype(BF16) for n in ("w_o_mla", "w_o_dil", "w_out", "w_ff1", "w_ff2")]
    later.append(_pad_rows(jnp.concatenate([bg_hi, bg_lo], axis=0), 16))
    mla = dict(batch=batch, seq=seq, width=PAIR_W, col0=(0, 0, 0), dilated=False, scale=MLA_SCALE)
    dil = dict(batch=batch, seq=seq, width=LANES, col0=(0, N_PAIRS, 2 * N_PAIRS), dilated=True, scale=DIL_SCALE)
    o_a, lse_a, g_oa, g_ob, g_out, g_ff1, g_ff2, g_bg = _attn_fwd(
        qp, kp, vm, no_bias, name="mla_attention_fwd", comm=_Gather(later), comm_arrays=later, **mla)
    o_b, lse_b = _attn_fwd(qkvd, qkvd, qkvd, dil_bias, name="dilated_attention_fwd", **dil)
    w_oa, w_ob = _from_col_shards(g_oa), _from_col_shards(g_ob)
    w_out_full = g_out.reshape(D_MODEL, D_MODEL)
    w_ff2_full = g_ff2.reshape(D_FF, D_MODEL)
    bg_parts = g_bg.astype(F32)
    b_gate_full = _from_col_shards(bg_parts[:, 0:2] + bg_parts[:, 2:4])
    hb, xhat1, rstd1, y_a, y_b, mix = _fwd_mix(o_a, o_b, gates, x2, b_gate_full, w_oa, w_ob, w_out_full, ln1_g, ln1_b, seq=seq)
    u, dz2, dz2b, stat2 = _fwd_mlp(hb, xhat1, loss_target.reshape(tokens, D_MODEL), g_ff1, w_ff2_full, ln1_g, ln1_b, ln2_g, ln2_b, seq=seq)

    du, dz1, dz1b, stat1 = _bwd_mlp(dz2, dz2b, u, xhat1, rstd1, g_ff1, w_ff2_full, ln1_g, seq=seq)
    dw_ff = [_wgrad(hb, du, "wgrad_ff1", by_shard=True),
             _wgrad(u, dz2b, "wgrad_ff2", square_relu=True).reshape(N_DEV, FF_SHARD, D_MODEL)]
    dgates, dy_a, dy_b, do_a, do_b, stat_g = _bwd_mix(dz1b, gates, y_a, y_b, b_gate_full, w_oa, w_ob, w_out_full, seq=seq)
    dqp, dkp, dvm, r_ff1, r_ff2 = _attn_bwd(qp, kp, vm, o_a, do_a, lse_a, no_bias, name="mla_attention_bwd",
                                            comm=_Scatter(dw_ff), comm_arrays=dw_ff, **mla)
    dw_mid = [_to_col_shards(_wgrad(o_a, dy_a, "wgrad_o_mla")), _to_col_shards(_wgrad(o_b, dy_b, "wgrad_o_dil")),
              _wgrad(mix, dz1b, "wgrad_out").reshape(N_DEV, D_MODEL // N_DEV, D_MODEL),
              _pad_rows(_to_col_shards(stat_g[0:2]).astype(BF16), 16)]
    dq_d, dk_d, dv_d, r_oa, r_ob, r_out, r_bg = _attn_bwd(qkvd, qkvd, qkvd, o_b, do_b, lse_b, dil_bias, name="dilated_attention_bwd",
                                                          comm=_Scatter(dw_mid), comm_arrays=dw_mid, **dil)
    grad_x, dproj, d_a, dkn, stat_r = _bwd_proj(dqp, dkp, dvm, dq_d, dk_d, dv_d, dgates, dz1, low, w_in_ext, w1, wk, wv,
                                                g_q_a, g_kv_a, cext, sext, cs128, seq=seq)

    dw_in_ext = _wgrad(dproj, xb, "wgrad_in")
    dw1 = _wgrad(qn, d_a, "wgrad_uq")
    dwk = _wgrad(kvn, dkn, "wgrad_ukv_k")
    dwv = _wgrad(kvn, dvm, "wgrad_ukv_v")
    dw_kr = dw_in_ext[s1:s2] + _unrot_cols(dw_in_ext[s2:s2 + ROPE].T).T

    def dw_in_cols(lo, hi):
        out = []
        for a, b, piece in ((0, s1, lambda u, v: dw_in_ext[u:v]), (s1, s2, lambda u, v: dw_kr[u - s1:v - s1]),
                            (s2, n_in, lambda u, v: dw_in_ext[u + LOW_W - s2:v + LOW_W - s2])):
            if max(lo, a) < min(hi, b):
                out.append(piece(max(lo, a), min(hi, b)))
        return out

    dw_in = jnp.stack([jnp.concatenate(dw_in_cols(d * widths[0], (d + 1) * widths[0]), axis=0) for d in range(N_DEV)])
    n1, r1 = _split_slabs(dw1)
    dw_uq = jnp.concatenate([n1, r1], axis=2).reshape(Q_LORA, N_HEADS * (NOPE + ROPE))
    dw_ukv = jnp.concatenate([dwk.reshape(KV_LORA, N_HEADS, NOPE), dwv.reshape(KV_LORA, N_HEADS, HEAD_V)], axis=2).reshape(KV_LORA, N_HEADS * (NOPE + HEAD_V))
    last = [dw_in] + [_pad_cols(_to_col_shards(dw)) for dw in (dw_uq, dw_ukv)]
    theirs = _rs_sibling(last, "rs_last_sibling_exchange")
    sums = [_pair_sum(a, b, "rs_last_pair_sum_" + n) for a, b, n in zip(last, theirs, first)]
    partial = jnp.concatenate([stat_r[0:1, :Q_LORA], stat_r[1:2, :KV_LORA], stat1[0:1], stat1[1:2], stat2[0:1], stat2[1:2],
                               stat2[2:3, :LANES]], axis=1)
    partial = _pad_rows(partial.reshape(-1, LANES), 8)
    rest = [s[1] for s in sums]
    got_in, got_uq, got_ukv, every = _run_comm(_Plans([_ChipExchange(rest), _Gather([partial])]), rest + [partial],
                                               "rs_last_chip_exchange")

    upd = {}
    early = ["w_ff1", "w_ff2", "w_out", "w_o_mla", "w_o_dil"]
    items = [(weights[n][0], mom_m[n][0], mom_v[n][0], own, parts) for n, own, parts in
             zip(early, (dw_ff[0], dw_ff[1], dw_mid[2], dw_mid[0], dw_mid[1]), (r_ff1, r_ff2, r_out, r_oa, r_ob))]
    upd.update(zip(early, _adamw(items, "adamw_early_weights")))
    (in_t,) = _adamw([(weights["w_in"][0].T, mom_m["w_in"][0].T, mom_v["w_in"][0].T, sums[0][0], got_in)], "adamw_w_in")
    upd["w_in"] = tuple(a.T for a in in_t)
    for n, w, (own, _), parts in zip(first[1:], widths[1:], sums[1:], (got_uq, got_ukv)):
        (upd[n],) = _adamw([(weights[n][0], mom_m[n][0], mom_v[n][0], own[:, :w], parts[:, :, :w])], "adamw_" + n)
    (bg_upd,) = _adamw([(_pad_rows(b_gate[0], 16), _pad_rows(m_b_gate[0], 16), _pad_rows(v_b_gate[0], 16), dw_mid[3], r_bg)],
                       "adamw_b_gate")
    upd["b_gate"] = tuple(t[0:2] for t in bg_upd)

    small_w = [g_q_a, g_kv_a, ln1_g, ln1_b, ln2_g, ln2_b]
    small_m = [m_g_q_a, m_g_kv_a, m_ln1_g, m_ln1_b, m_ln2_g, m_ln2_b]
    small_v = [v_g_q_a, v_g_kv_a, v_ln1_g, v_ln1_b, v_ln2_g, v_ln2_b]
    small_widths = [a.shape[1] for a in small_w]

    def as_rows(vecs, extra):
        flat = jnp.concatenate(vecs + [jnp.zeros((1, extra), F32)], axis=1)
        return _pad_rows(flat.reshape(-1, LANES), 8)

    g_s, d_s, nm_s, nv_s = _adamw_small(every, as_rows(small_w, LANES), as_rows(small_m, LANES), as_rows(small_v, LANES))

    def split_small(a):
        flat = a.reshape(1, -1)
        out, c0 = [], 0
        for w in small_widths:
            out.append(flat[:, c0:c0 + w])
            c0 += w
        return out, flat[0, c0]

    g_small, loss = split_small(g_s)
    small = [g_small, split_small(d_s)[0], split_small(nm_s)[0], split_small(nv_s)[0]]

    order = ["w_in", "b_gate", "g_q_a", "w_uq", "g_kv_a", "w_ukv", "w_o_mla", "w_o_dil", "w_out", "ln1_g", "ln1_b", "w_ff1", "w_ff2", "ln2_g", "ln2_b"]
    small_names = ["g_q_a", "g_kv_a", "ln1_g", "ln1_b", "ln2_g", "ln2_b"]

    def pick(kind):
        return [small[kind][small_names.index(n)] if n in small_names else upd[n][kind][None] for n in order]

    return (loss, grad_x.reshape(batch, seq, D_MODEL), *pick(0), *pick(1), *pick(2), *pick(3))
```

```python
import functools
import math

import jax
import jax.numpy as jnp
from jax import lax
from jax.experimental import pallas as pl
from jax.experimental.pallas import tpu as pltpu

F32 = jnp.float32
BF16 = jnp.bfloat16
I32 = jnp.int32

D_MODEL = 1024
N_HEADS = 8
NOPE = 64
ROPE = 32
HEAD_V = 64
Q_LORA = 384
KV_LORA = 256
DIL_WIDTH = 512
D_FF = 4096
ROPE_THETA = 10000.0
LN_EPS = 1e-5
RMS_EPS = 1e-6
NEG = -1e30
ALPHA = 2.0 ** 0.25
MLA_SCALE = (NOPE + ROPE) ** -0.5
DIL_SCALE = 64 ** -0.5
ADAM_LR, ADAM_B1, ADAM_B2, ADAM_EPS, ADAM_WD, ADAM_STEP = 0.001, 0.9, 0.999, 1e-08, 0.01, 10

LANES = 128
PAIR_W = 256
N_PAIRS = N_HEADS // 2
LOW_W = 768
IN_EXT = LOW_W + 3 * DIL_WIDTH + 2 * D_MODEL
N_DEV = 8
FF_SHARD = D_FF // N_DEV
FF_STEP = 4
WGRAD_SHARDS = 4
TOKEN_TILE = 256
MIX_TILE = 512
ATTN_TILE = 256
VMEM_LIMIT = 56 << 20

MESH = pl.DeviceIdType.MESH
ANY = pl.BlockSpec(memory_space=pl.ANY)
CHIP_FLIPS = ((0, 0), (0, 1), (1, 0), (1, 1))
PEER_FLIPS = tuple((fx, fy, fc) for fx in (0, 1) for fy in (0, 1) for fc in (0, 1))[1:]


def _cp(*sem):
    return pltpu.CompilerParams(dimension_semantics=sem or None, vmem_limit_bytes=VMEM_LIMIT)


def _full(shape):
    nd = len(shape)
    return pl.BlockSpec(shape, lambda *_: (0,) * nd)


def _rows(tm, width):
    return pl.BlockSpec((tm, width), lambda i, *_: (i, 0))


def _dot(a, b):
    return jnp.dot(a, b, preferred_element_type=F32)


def _dot_nt(a, b):
    return lax.dot_general(a, b, (((1,), (1,)), ((), ())), preferred_element_type=F32)


def _dot_tn(a, b):
    return lax.dot_general(a, b, (((0,), (0,)), ((), ())), preferred_element_type=F32)


def _sigmoid(z):
    return 1.0 / (1.0 + jnp.exp(-z))


def _place():
    return lax.axis_index("x"), lax.axis_index("y"), lax.axis_index("c")


def _flip(v, f):
    return 1 - v if f else v


class _Gather:
    def __init__(self, shards):
        self.n = len(shards)
        self.out_shape = [jax.ShapeDtypeStruct((N_DEV, *s.shape), s.dtype) for s in shards]
        self.scratch = [pltpu.SemaphoreType.DMA((7 * self.n,)), pltpu.SemaphoreType.DMA((7 * self.n,)),
                        pltpu.SemaphoreType.DMA((self.n,))]

    def _copies(self, what, srcs, dsts, send, recv, local):
        x, y, c = _place()
        chips = [(_flip(x, fx), _flip(y, fy)) for fx, fy in CHIP_FLIPS[1:]]
        out = []
        for a in range(self.n):
            def slot(px, py, pc, a=a):
                return dsts[a].at[4 * px + 2 * py + pc]

            def copy(k, block, to, src=None, a=a, slot=slot):
                return pltpu.make_async_remote_copy(
                    src_ref=slot(*block) if src is None else src, dst_ref=slot(*block),
                    send_sem=send.at[7 * a + k], recv_sem=recv.at[7 * a + k], device_id=to, device_id_type=MESH)

            if what == "mine":
                out.append(pltpu.make_async_copy(srcs[a], slot(x, y, c), local.at[a]))
            elif what == "first":
                out.append(copy(0, (x, y, c), (x, y, 1 - c), src=srcs[a]))
                out += [copy(1 + j, (x, y, c), (*chip, c), src=srcs[a]) for j, chip in enumerate(chips)]
            elif what == "landed":
                out += [copy(1 + j, (*chip, c), (x, y, c)) for j, chip in enumerate(chips)]
            elif what == "passed":
                out += [copy(4 + j, (*chip, c), (x, y, 1 - c)) for j, chip in enumerate(chips)]
            else:
                out.append(copy(0, (x, y, 1 - c), (x, y, c)))
                out += [copy(4 + j, (*chip, 1 - c), (x, y, c)) for j, chip in enumerate(chips)]
        return out

    def start(self, *refs):
        for cp in self._copies("first", *refs) + self._copies("mine", *refs):
            cp.start()

    def forward(self, *refs):
        for landed, passed in zip(self._copies("landed", *refs), self._copies("passed", *refs)):
            landed.wait_recv()
            passed.start()

    def finish(self, *refs):
        for cp in self._copies("from_sibling", *refs):
            cp.wait_recv()
        for cp in self._copies("first", *refs) + self._copies("passed", *refs):
            cp.wait_send()
        for cp in self._copies("mine", *refs):
            cp.wait()


class _Scatter:
    def __init__(self, arrays):
        self.n = len(arrays)
        self.out_shape = [jax.ShapeDtypeStruct((7, *a.shape[1:]), a.dtype) for a in arrays]
        self.scratch = [pltpu.SemaphoreType.DMA((7 * self.n,)), pltpu.SemaphoreType.DMA((7 * self.n,))]

    def _copies(self, srcs, dsts, send, recv):
        x, y, c = _place()
        out = []
        for a in range(self.n):
            for k, (fx, fy, fc) in enumerate(PEER_FLIPS):
                px, py, pc = _flip(x, fx), _flip(y, fy), _flip(c, fc)
                out.append(pltpu.make_async_remote_copy(
                    src_ref=srcs[a].at[4 * px + 2 * py + pc], dst_ref=dsts[a].at[k],
                    send_sem=send.at[7 * a + k], recv_sem=recv.at[7 * a + k], device_id=(px, py, pc), device_id_type=MESH))
        return out

    def start(self, *refs):
        for cp in self._copies(*refs):
            cp.start()

    def forward(self, *refs):
        pass

    def finish(self, *refs):
        for cp in self._copies(*refs):
            cp.wait_send()
        for cp in self._copies(*refs):
            cp.wait_recv()


class _ChipExchange:
    def __init__(self, arrays):
        self.n = len(arrays)
        self.out_shape = [jax.ShapeDtypeStruct(a.shape, a.dtype) for a in arrays]
        self.scratch = [pltpu.SemaphoreType.DMA((3 * self.n,)), pltpu.SemaphoreType.DMA((3 * self.n,))]

    def _copies(self, srcs, dsts, send, recv):
        x, y, c = _place()
        return [pltpu.make_async_remote_copy(
            src_ref=srcs[a].at[k], dst_ref=dsts[a].at[k], send_sem=send.at[3 * a + k], recv_sem=recv.at[3 * a + k],
            device_id=(_flip(x, fx), _flip(y, fy), c), device_id_type=MESH)
            for a in range(self.n) for k, (fx, fy) in enumerate(CHIP_FLIPS[1:])]

    def start(self, *refs):
        for cp in self._copies(*refs):
            cp.start()

    def forward(self, *refs):
        pass

    def finish(self, *refs):
        for cp in self._copies(*refs):
            cp.wait_send()
        for cp in self._copies(*refs):
            cp.wait_recv()


class _Plans:
    def __init__(self, plans):
        self.plans = plans
        self.n = sum(p.n for p in plans)
        self.out_shape = [s for p in plans for s in p.out_shape]
        self.scratch = [s for p in plans for s in p.scratch]

    def _each(self, phase, srcs, dsts, *sems):
        i0 = s0 = 0
        for p in self.plans:
            getattr(p, phase)(srcs[i0:i0 + p.n], dsts[i0:i0 + p.n], *sems[s0:s0 + len(p.scratch)])
            i0, s0 = i0 + p.n, s0 + len(p.scratch)

    def start(self, *refs):
        self._each("start", *refs)

    def forward(self, *refs):
        self._each("forward", *refs)

    def finish(self, *refs):
        self._each("finish", *refs)


def _run_comm(comm, arrays, name):
    n = comm.n

    def body(*refs):
        args = (refs[:n], refs[n:2 * n], *refs[2 * n:])
        comm.start(*args)
        comm.forward(*args)
        comm.finish(*args)

    return pl.pallas_call(body, name=name, out_shape=comm.out_shape, in_specs=[ANY] * n, out_specs=[ANY] * n,
                          scratch_shapes=comm.scratch)(*arrays)


def _rs_sibling(arrays, name):
    n = len(arrays)

    def body(*refs):
        srcs, got, (send, recv) = refs[:n], refs[n:2 * n], refs[2 * n:]
        x, y, c = _place()
        copies = []
        for a in range(n):
            for r, (fx, fy) in enumerate(CHIP_FLIPS):
                chip = 2 * _flip(x, fx) + _flip(y, fy)
                copies.append(pltpu.make_async_remote_copy(
                    src_ref=srcs[a].at[2 * chip + 1 - c], dst_ref=got[a].at[r], send_sem=send.at[4 * a + r],
                    recv_sem=recv.at[4 * a + r], device_id=(x, y, 1 - c), device_id_type=MESH))
        for cp in copies:
            cp.start()
        for cp in copies:
            cp.wait_send()
        for cp in copies:
            cp.wait_recv()

    return pl.pallas_call(
        body, name=name, out_shape=[jax.ShapeDtypeStruct((4, *a.shape[1:]), a.dtype) for a in arrays],
        in_specs=[ANY] * n, out_specs=[ANY] * n,
        scratch_shapes=[pltpu.SemaphoreType.DMA((4 * n,)), pltpu.SemaphoreType.DMA((4 * n,))],
    )(*arrays)


def _chip_slots():
    x, y, c = _place()
    return jnp.stack([4 * _flip(x, fx) + 2 * _flip(y, fy) + c for fx, fy in CHIP_FLIPS]).astype(I32)


def _tiles(rows, cols, steps=4):
    if rows % (16 * steps) == 0:
        return steps, (rows // steps, cols), lambda i: (i, 0)
    if cols % (LANES * steps) == 0:
        return steps, (rows, cols // steps), lambda i: (0, i)
    return 1, (rows, cols), lambda i: (0, 0)


def _pair_sum(full, theirs, name):
    _, rows, cols = theirs.shape
    steps, tile, at = _tiles(rows, cols)

    def body(slots_ref, m0_ref, m1_ref, m2_ref, m3_ref, b_ref, own_ref, rest_ref):
        own_ref[...] = m0_ref[...].astype(F32) + b_ref[0].astype(F32)
        for k, m_ref in enumerate((m1_ref, m2_ref, m3_ref)):
            rest_ref[k] = (m_ref[...].astype(F32) + b_ref[k + 1].astype(F32)).astype(BF16)

    def mine(k):
        return pl.BlockSpec((None, *tile), lambda i, slots: (slots[k], *at(i)))

    return pl.pallas_call(
        body, name=name,
        grid_spec=pltpu.PrefetchScalarGridSpec(
            num_scalar_prefetch=1, grid=(steps,),
            in_specs=[mine(0), mine(1), mine(2), mine(3), pl.BlockSpec((4, *tile), lambda i, slots: (0, *at(i)))],
            out_specs=(pl.BlockSpec(tile, lambda i, slots: at(i)), pl.BlockSpec((3, *tile), lambda i, slots: (0, *at(i))))),
        out_shape=(jax.ShapeDtypeStruct((rows, cols), F32), jax.ShapeDtypeStruct((3, rows, cols), BF16)),
        compiler_params=_cp("parallel"),
    )(_chip_slots(), full, full, full, full, theirs)


def _head_lanes(width, h):
    lane = lax.broadcasted_iota(I32, (1, width), 1)
    if width == LANES:
        return (lane >= 64 * h) & (lane < 64 * h + 64)
    nope = (lane >= NOPE * h) & (lane < NOPE * h + NOPE)
    rope = (lane >= 2 * NOPE + ROPE * h) & (lane < 2 * NOPE + ROPE * h + ROPE)
    return nope | rope


def _dilated_bias_table(seq):
    t = min(ATTN_TILE, seq)
    nd = seq // t

    def body(o_ref):
        delta = pl.program_id(0) * t + lax.broadcasted_iota(I32, (t, t), 1) - lax.broadcasted_iota(I32, (t, t), 0)
        mult = ((delta <= 128).astype(I32) + (((delta & 3) == 0) & (delta <= 512)).astype(I32)
                + ((delta & 15) == 0).astype(I32))
        logm = jnp.where(mult == 3, math.log(3.0), jnp.where(mult == 2, math.log(2.0), 0.0))
        valid = (delta >= 0) & (mult > 0)
        dist = delta.astype(F32)
        for h in range(N_HEADS):
            o_ref[h] = jnp.where(valid, logm - 2.0 ** (-(h + 1)) * dist, NEG)

    return pl.pallas_call(
        body, name="dilated_bias_table", grid=(nd,), out_shape=jax.ShapeDtypeStruct((N_HEADS, nd, t, t), F32),
        out_specs=pl.BlockSpec((N_HEADS, None, t, t), lambda d: (0, d, 0, 0)),
        compiler_params=_cp("parallel"),
    )()


def _comm_hooks(comm, refs, n_in, n_out):
    if comm is None:
        return refs[:n_in], refs[n_in:n_in + n_out], refs[n_in + n_out:], None
    n = comm.n
    ins, srcs = refs[:n_in], refs[n_in:n_in + n]
    outs, dsts = refs[n_in + n:n_in + n + n_out], refs[n_in + n + n_out:n_in + 2 * n + n_out]
    rest = refs[n_in + 2 * n + n_out:]
    own = len(rest) - len(comm.scratch)
    return ins, outs, rest[:own], (srcs, dsts, *rest[own:])


def _attn_fwd(q, k, v, bias, *, batch, seq, width, col0, dilated, scale, name, comm=None, comm_arrays=()):
    t = min(ATTN_TILE, seq)
    nq = seq // t
    cq, ck, cv = col0
    pre = scale if dilated else 1.0
    steps = batch * N_PAIRS

    def body(*refs):
        (q_ref, k_ref, v_ref, bias_ref), (o_ref, lse_ref), (v_heads,), plan = _comm_hooks(comm, refs, 4, 2)
        step_no = pl.program_id(0) * N_PAIRS + pl.program_id(1)
        if plan:
            pl.when(step_no == 0)(lambda: comm.start(*plan))
            pl.when(step_no == (3 * steps) // 4)(lambda: comm.forward(*plan))
        v_all = v_ref[...].astype(F32)
        for h in (0, 1):
            v_heads[h] = jnp.transpose(jnp.where(_head_lanes(LANES, h), v_all, 0.0)).astype(BF16)
        top = lax.broadcasted_iota(I32, (LANES, t), 0) < HEAD_V
        causal = lax.broadcasted_iota(I32, (t, t), 0) <= lax.broadcasted_iota(I32, (t, t), 1)
        def heads(i):
            q2 = q_ref[pl.ds(pl.multiple_of(i * t, t), t), :]
            q2 = q2 * pre if dilated else q2
            return [jnp.where(_head_lanes(width, h), q2, jnp.zeros_like(q2)) for h in (0, 1)]

        def scores(qh, j):
            kj = k_ref[pl.ds(pl.multiple_of(j * t, t), t), :]
            return tuple(_dot_nt(kj, qh[h]) for h in (0, 1))

        lax.fori_loop(0, nq, functools.partial(query_tile, heads, scores, bias_ref, o_ref, lse_ref, v_heads, top, causal),
                      scores(heads(0), 0))
        if plan:
            pl.when(step_no == steps - 1)(lambda: comm.finish(*plan))

    def query_tile(heads, scores, bias_ref, o_ref, lse_ref, v_heads, top, causal, i, first):
        qs = pl.multiple_of(i * t, t)
        qh = heads(i)

        def step(j, carry, last):
            m0, l0, m1, l1, acc, s0, s1 = carry
            ahead = scores(heads(jnp.minimum(i + 1, nq - 1)), 0) if last else scores(qh, j + 1)
            ks = pl.multiple_of(j * t, t)
            new, alphas, pv = [], [], []
            for h, (m, l, s) in enumerate(((m0, l0, s0), (m1, l1, s1))):
                if dilated:
                    s = s + bias_ref[h, i - j]
                else:
                    s = s * scale
                    if last:
                        s = jnp.where(causal, s, NEG)
                m_new = jnp.maximum(m, jnp.max(s, axis=0, keepdims=True))
                a = jnp.exp(m - m_new)
                p = jnp.exp(s - m_new)
                new += [m_new, a * l + jnp.sum(p, axis=0, keepdims=True)]
                alphas.append(a)
                pv.append(_dot(v_heads[h, :, pl.ds(ks, t)], p.astype(BF16)))
            acc = jnp.where(top, alphas[0], alphas[1]) * acc + pv[0] + pv[1]
            return (*new, acc, *ahead)

        row = jnp.full((1, t), NEG, F32)
        zero = jnp.zeros((1, t), F32)
        init = (row, zero, row, zero, jnp.zeros((LANES, t), F32), *first)
        m0, l0, m1, l1, acc, *following = step(i, lax.fori_loop(0, i, functools.partial(step, last=False), init), True)
        o_ref[pl.ds(qs, t), :] = jnp.transpose(acc * jnp.where(top, 1.0 / l0, 1.0 / l1)).astype(BF16)
        r = lax.broadcasted_iota(I32, (8, t), 0)
        lse_ref[:, pl.ds(qs, t)] = jnp.where(r == 0, m0 + jnp.log(l0), jnp.where(r == 1, m1 + jnp.log(l1), 0.0))
        return tuple(following)

    bias_spec = (pl.BlockSpec((2, nq, t, t), lambda b, p: (p, 0, 0, 0)) if dilated
                 else pl.BlockSpec((None, 8, LANES), lambda b, p: (0, 0, 0)))
    n = comm.n if comm else 0
    return pl.pallas_call(
        body, name=name, grid=(batch, N_PAIRS),
        out_shape=[jax.ShapeDtypeStruct((batch * seq, DIL_WIDTH), BF16), jax.ShapeDtypeStruct((batch * N_PAIRS, 8, seq), F32)]
        + (comm.out_shape if comm else []),
        in_specs=[pl.BlockSpec((seq, width), lambda b, p: (b, cq + p)),
                  pl.BlockSpec((seq, width), lambda b, p: (b, ck + p)),
                  pl.BlockSpec((seq, LANES), lambda b, p: (b, cv + p)),
                  bias_spec] + [ANY] * n,
        out_specs=[pl.BlockSpec((seq, LANES), lambda b, p: (b, p)),
                   pl.BlockSpec((None, 8, seq), lambda b, p: (b * N_PAIRS + p, 0, 0))] + [ANY] * n,
        scratch_shapes=[pltpu.VMEM((2, LANES, seq), BF16)] + (comm.scratch if comm else []),
        compiler_params=_cp("arbitrary", "arbitrary") if comm else _cp("parallel", "parallel"),
    )(q, k, v, bias, *comm_arrays)


def _attn_bwd(q, k, v, o, do, lse, bias, *, batch, seq, width, col0, dilated, scale, name, comm=None, comm_arrays=()):
    t = min(ATTN_TILE, seq)
    nq = seq // t
    cq, ck, cv = col0
    pre = scale if dilated else 1.0
    dq_transposed = width == LANES
    steps = batch * N_PAIRS

    def body(*refs):
        ins, (dq_ref, dk_ref, dv_ref), (dq_acc, dk_acc, dv_acc, rowdot, q_heads, do_heads), plan = _comm_hooks(comm, refs, 7, 3)
        q_ref, k_ref, v_ref, o_ref, do_ref, lse_ref, bias_ref = ins
        step_no = pl.program_id(0) * N_PAIRS + pl.program_id(1)
        if plan:
            pl.when(step_no == 0)(lambda: comm.start(*plan))
        wlane = [_head_lanes(width, h) for h in (0, 1)]
        vlane = [_head_lanes(LANES, h) for h in (0, 1)]
        causal = lax.broadcasted_iota(I32, (t, t), 0) <= lax.broadcasted_iota(I32, (t, t), 1)
        q_all = q_ref[...] * pre if dilated else q_ref[...]
        for h in (0, 1):
            q_heads[h] = jnp.where(wlane[h], q_all, jnp.zeros_like(q_all))
            do_heads[h] = jnp.where(vlane[h], do_ref[...], jnp.zeros_like(do_ref[...]))
        prod = jnp.transpose(do_ref[...].astype(F32) * o_ref[...].astype(F32))
        rowdot[0:1, :] = jnp.sum(prod[0:HEAD_V], axis=0, keepdims=True)
        rowdot[1:2, :] = jnp.sum(prod[HEAD_V:], axis=0, keepdims=True)
        dq_acc[...] = jnp.zeros_like(dq_acc)

        def k_tile(j, _):
            ks = pl.multiple_of(j * t, t)
            kj = k_ref[pl.ds(ks, t), :]
            vj = v_ref[pl.ds(ks, t), :]
            kh = [jnp.where(wlane[h], kj, jnp.zeros_like(kj)) for h in (0, 1)]
            if dq_transposed:
                kh = [jnp.transpose(kh[h].astype(F32)).astype(BF16) for h in (0, 1)]
            dk_acc[...] = jnp.zeros_like(dk_acc)
            dv_acc[...] = jnp.zeros_like(dv_acc)

            def operands(i):
                qs = pl.multiple_of(i * t, t)
                return [q_heads[h, pl.ds(qs, t), :] for h in (0, 1)], [do_heads[h, pl.ds(qs, t), :] for h in (0, 1)]

            def products(i):
                qih, doih = operands(i)
                scores = tuple(_dot_nt(kj, qih[h]) for h in (0, 1))
                return scores + tuple(_dot_nt(vj, doih[h]) for h in (0, 1)) if width > LANES else scores

            def q_tile(n, carry, last):
                i = nq - 1 - n
                ahead = () if last else products(i - 1)
                qs = pl.multiple_of(i * t, t)
                qih, doih = operands(i)
                s0, s1 = carry[:2]
                dps = carry[2:] if width > LANES else [_dot_nt(vj, doih[h]) for h in (0, 1)]
                dq_i = jnp.zeros((width, t) if dq_transposed else (t, width), F32)
                for h, (s, dp) in enumerate(((s0, dps[0]), (s1, dps[1]))):
                    if dilated:
                        s = s + bias_ref[h, i - j]
                    else:
                        s = s * scale
                        if last:
                            s = jnp.where(causal, s, NEG)
                    p = jnp.exp(s - lse_ref[h:h + 1, pl.ds(qs, t)])
                    ds = p * (dp - rowdot[h:h + 1, pl.ds(qs, t)])
                    ds = (ds if dilated else ds * scale).astype(BF16)
                    dv_acc[...] += _dot(p.astype(BF16), doih[h])
                    dk_acc[...] += _dot(ds, qih[h])
                    dq_i = dq_i + (_dot(kh[h], ds) if dq_transposed else _dot_tn(ds, kh[h]))
                if dq_transposed:
                    dq_acc[:, pl.ds(qs, t)] += dq_i
                else:
                    dq_acc[pl.ds(qs, t), :] += dq_i
                return ahead

            q_tile(nq - 1 - j, lax.fori_loop(0, nq - 1 - j, functools.partial(q_tile, last=False), products(nq - 1)), True)
            dk_ref[pl.ds(ks, t), :] = dk_acc[...].astype(BF16)
            dv_ref[pl.ds(ks, t), :] = dv_acc[...].astype(BF16)
            return 0

        lax.fori_loop(0, nq, k_tile, 0)
        dq_ref[...] = ((jnp.transpose(dq_acc[...]) if dq_transposed else dq_acc[...]) * pre).astype(BF16)
        if plan:
            pl.when(step_no == steps - 1)(lambda: comm.finish(*plan))

    tokens = batch * seq
    bias_spec = (pl.BlockSpec((2, nq, t, t), lambda b, p: (p, 0, 0, 0)) if dilated
                 else pl.BlockSpec((None, 8, LANES), lambda b, p: (0, 0, 0)))
    n = comm.n if comm else 0
    return pl.pallas_call(
        body, name=name, grid=(batch, N_PAIRS),
        out_shape=[jax.ShapeDtypeStruct((tokens, N_PAIRS * width), BF16), jax.ShapeDtypeStruct((tokens, N_PAIRS * width), BF16),
                   jax.ShapeDtypeStruct((tokens, DIL_WIDTH), BF16)] + (comm.out_shape if comm else []),
        in_specs=[pl.BlockSpec((seq, width), lambda b, p: (b, cq + p)),
                  pl.BlockSpec((seq, width), lambda b, p: (b, ck + p)),
                  pl.BlockSpec((seq, LANES), lambda b, p: (b, cv + p)),
                  pl.BlockSpec((seq, LANES), lambda b, p: (b, p)),
                  pl.BlockSpec((seq, LANES), lambda b, p: (b, p)),
                  pl.BlockSpec((None, 8, seq), lambda b, p: (b * N_PAIRS + p, 0, 0)),
                  bias_spec] + [ANY] * n,
        out_specs=[pl.BlockSpec((seq, width), lambda b, p: (b, p)),
                   pl.BlockSpec((seq, width), lambda b, p: (b, p)),
                   pl.BlockSpec((seq, LANES), lambda b, p: (b, p))] + [ANY] * n,
        scratch_shapes=[pltpu.VMEM((width, seq) if dq_transposed else (seq, width), F32),
                        pltpu.VMEM((t, width), F32), pltpu.VMEM((t, LANES), F32),
                        pltpu.VMEM((8, seq), F32), pltpu.VMEM((2, seq, width), BF16), pltpu.VMEM((2, seq, LANES), BF16)]
        + (comm.scratch if comm else []),
        compiler_params=_cp("arbitrary", "arbitrary") if comm else _cp("parallel", "parallel"),
    )(q, k, v, o, do, lse, bias, *comm_arrays)


def _rms(xf, g):
    r = lax.rsqrt(jnp.mean(xf * xf, axis=1, keepdims=True) + RMS_EPS)
    return xf * r * g, r


def _rms_bwd(dy, xf, r, g):
    gy = dy * g
    dx = r * gy - xf * (r * r * r) * jnp.mean(gy * xf, axis=1, keepdims=True)
    return dx, dy * xf * r


def _ln_bwd(dy, xhat, rstd, g):
    dxh = dy * g
    return rstd * (dxh - jnp.mean(dxh, axis=1, keepdims=True) - xhat * jnp.mean(dxh * xhat, axis=1, keepdims=True))


def _rope_slabs(q, cos, sin, transpose):
    first_half = (lax.broadcasted_iota(I32, (1, LANES), 1) % ROPE) < ROPE // 2
    out = []
    for p in range(N_PAIRS):
        blk = q[:, p * PAIR_W + LANES:(p + 1) * PAIR_W]
        y = blk * sin if transpose else blk
        up, down = pltpu.roll(y, LANES - ROPE // 2, 1), pltpu.roll(y, ROPE // 2, 1)
        rot = jnp.where(first_half, up, -down) if transpose else jnp.where(first_half, -up, down) * sin
        out += [q[:, p * PAIR_W:p * PAIR_W + LANES], blk * cos + rot]
    return jnp.concatenate(out, axis=1)


def _fwd_proj(x, w_in_ext, w1, wk, wv, g_q, g_kv, cext, sext, cs128, *, seq):
    tokens = x.shape[0]
    tm = min(TOKEN_TILE, seq)
    ns = seq // tm

    def body(x_ref, win_ref, w1_ref, wk_ref, wv_ref, gq_ref, gkv_ref, c_ref, s_ref, cs_ref,
             low_ref, gates_ref, qkvd_ref, qp_ref, kp_ref, vm_ref, qn_ref, kvn_ref, xb_ref):
        xt = x_ref[...].astype(BF16)
        xb_ref[...] = xt
        low = _dot(xt, win_ref[:, 0:LOW_W])
        low_ref[...] = low
        qkvd_ref[...] = _dot(xt, win_ref[:, LOW_W:LOW_W + 3 * DIL_WIDTH]).astype(BF16)
        gates_ref[...] = _dot(xt, win_ref[:, LOW_W + 3 * DIL_WIDTH:]).astype(BF16)
        qn = _rms(low[:, 0:Q_LORA], gq_ref[...])[0].astype(BF16)
        kvn = _rms(low[:, Q_LORA:Q_LORA + KV_LORA], gkv_ref[...])[0].astype(BF16)
        qn_ref[...] = qn
        kvn_ref[...] = kvn
        qp_ref[...] = _rope_slabs(_dot(qn, w1_ref[...]), c_ref[...], s_ref[...], False).astype(BF16)
        kr = low[:, Q_LORA + KV_LORA:] * cs_ref[...]
        kr = kr + pltpu.roll(kr, LANES - ROPE, 1)
        lane = lax.broadcasted_iota(I32, kr.shape, 1)
        kr = jnp.where(lane < ROPE, kr, 0.0)
        kr = (kr + pltpu.roll(kr, ROPE, 1)).astype(BF16)
        kn = _dot(kvn, wk_ref[...]).astype(BF16)
        kp_ref[...] = jnp.concatenate([blk for p in range(N_PAIRS) for blk in (kn[:, p * LANES:(p + 1) * LANES], kr)], axis=1)
        vm_ref[...] = _dot(kvn, wv_ref[...]).astype(BF16)

    n_gates = 2 * D_MODEL
    outs = [(LOW_W, F32), (n_gates, BF16), (3 * DIL_WIDTH, BF16), (N_PAIRS * PAIR_W, BF16), (N_PAIRS * PAIR_W, BF16),
            (DIL_WIDTH, BF16), (Q_LORA, BF16), (KV_LORA, BF16), (D_MODEL, BF16)]
    return pl.pallas_call(
        body, name="fwd_proj", grid=(tokens // tm,),
        out_shape=tuple(jax.ShapeDtypeStruct((tokens, w), dt) for w, dt in outs),
        in_specs=[_rows(tm, D_MODEL), _full(w_in_ext.shape), _full(w1.shape), _full(wk.shape),
                  _full(wv.shape), _full(g_q.shape), _full(g_kv.shape),
                  pl.BlockSpec((tm, LANES), lambda i: (i % ns, 1)),
                  pl.BlockSpec((tm, LANES), lambda i: (i % ns, 1)),
                  pl.BlockSpec((tm, LANES), lambda i: (i % ns, 0))],
        out_specs=tuple(_rows(tm, w) for w, _ in outs),
        compiler_params=_cp("parallel"),
    )(x, w_in_ext, w1, wk, wv, g_q, g_kv, cext, sext, cs128)


def _fwd_mix(o_a, o_b, gates, x, b_gate, w_oa, w_ob, w_out, ln_g, ln_b, *, seq):
    tokens = x.shape[0]
    tm = min(MIX_TILE, seq)

    def body(oa_ref, ob_ref, gt_ref, x_ref, bg_ref, woa_ref, wob_ref, wout_ref, g_ref, b_ref,
             hb_ref, xhat_ref, rstd_ref, ya_ref, yb_ref, mix_ref):
        ya = _dot(oa_ref[...], woa_ref[...])
        yb = _dot(ob_ref[...], wob_ref[...])
        g0 = _sigmoid(gt_ref[:, 0:D_MODEL].astype(F32) + bg_ref[0:1, :])
        g1 = _sigmoid(gt_ref[:, D_MODEL:].astype(F32) + bg_ref[1:2, :])
        mix = (g0 * ya + g1 * yb).astype(BF16)
        z = ALPHA * x_ref[...] + _dot(mix, wout_ref[...])
        zc = z - jnp.mean(z, axis=1, keepdims=True)
        rstd = lax.rsqrt(jnp.mean(zc * zc, axis=1, keepdims=True) + LN_EPS)
        xhat = zc * rstd
        hb_ref[...] = (xhat * g_ref[...] + b_ref[...]).astype(BF16)
        xhat_ref[...] = xhat
        rstd_ref[...] = jnp.broadcast_to(rstd, (tm, LANES))
        ya_ref[...] = ya.astype(BF16)
        yb_ref[...] = yb.astype(BF16)
        mix_ref[...] = mix

    outs = [(D_MODEL, BF16), (D_MODEL, F32), (LANES, F32), (D_MODEL, BF16), (D_MODEL, BF16), (D_MODEL, BF16)]
    return pl.pallas_call(
        body, name="fwd_mix", grid=(tokens // tm,),
        out_shape=tuple(jax.ShapeDtypeStruct((tokens, w), dt) for w, dt in outs),
        in_specs=[_rows(tm, DIL_WIDTH), _rows(tm, DIL_WIDTH), _rows(tm, 2 * D_MODEL), _rows(tm, D_MODEL),
                  _full(b_gate.shape), _full(w_oa.shape), _full(w_ob.shape), _full(w_out.shape),
                  _full(ln_g.shape), _full(ln_b.shape)],
        out_specs=tuple(_rows(tm, w) for w, _ in outs),
        compiler_params=_cp("parallel"),
    )(o_a, o_b, gates, x, b_gate, w_oa, w_ob, w_out, ln_g, ln_b)


def _fwd_mlp(hb, xhat1, target, w_ff1, w_ff2, ln1_g, ln1_b, ln_g, ln_b, *, seq):
    tokens = hb.shape[0]
    tm = min(2 * TOKEN_TILE, seq)
    tf = FF_SHARD
    nf = N_DEV // FF_STEP

    def body(hb_ref, xh_ref, tg_ref, w1_ref, w2_ref, g1_ref, b1_ref, g_ref, b_ref, u_ref, dz_ref, dzb_ref, stat_ref, acc):
        i, j = pl.program_id(0), pl.program_id(1)

        @pl.when((i == 0) & (j == 0))
        def _():
            stat_ref[...] = jnp.zeros_like(stat_ref)

        @pl.when(j == 0)
        def _():
            acc[...] = jnp.zeros_like(acc)

        acts = []
        for s in range(FF_STEP):
            u = _dot(hb_ref[...], w1_ref[s])
            u_ref[:, s * tf:(s + 1) * tf] = u.astype(BF16)
            acts.append(jnp.square(jnp.maximum(u, 0.0)).astype(BF16))
        acc[...] += _dot(jnp.concatenate(acts, axis=1), w2_ref[...])

        @pl.when(j == nf - 1)
        def _():
            z = ALPHA * (xh_ref[...] * g1_ref[...] + b1_ref[...]) + acc[...]
            zc = z - jnp.mean(z, axis=1, keepdims=True)
            rstd = lax.rsqrt(jnp.mean(zc * zc, axis=1, keepdims=True) + LN_EPS)
            xhat = zc * rstd
            err = xhat * g_ref[...] + b_ref[...] - tg_ref[...]
            dy = err * (1.0 / D_MODEL)
            dz = _ln_bwd(dy, xhat, rstd, g_ref[...])
            dz_ref[...] = dz
            dzb_ref[...] = dz.astype(BF16)
            stat_ref[0:1, :] += jnp.sum(dy * xhat, axis=0, keepdims=True)
            stat_ref[1:2, :] += jnp.sum(dy, axis=0, keepdims=True)
            stat_ref[2:3, :] += jnp.sum(jnp.sum(err * err, axis=1, keepdims=True), axis=0, keepdims=True) * (0.5 / D_MODEL)

    return pl.pallas_call(
        body, name="fwd_mlp", grid=(tokens // tm, nf),
        out_shape=(jax.ShapeDtypeStruct((tokens, D_FF), BF16), jax.ShapeDtypeStruct((tokens, D_MODEL), F32),
                   jax.ShapeDtypeStruct((tokens, D_MODEL), BF16), jax.ShapeDtypeStruct((8, D_MODEL), F32)),
        in_specs=[_rows(tm, D_MODEL), _rows(tm, D_MODEL), _rows(tm, D_MODEL),
                  pl.BlockSpec((FF_STEP, D_MODEL, tf), lambda i, j: (j, 0, 0)),
                  pl.BlockSpec((FF_STEP * tf, D_MODEL), lambda i, j: (j, 0)),
                  _full(ln1_g.shape), _full(ln1_b.shape), _full(ln_g.shape), _full(ln_b.shape)],
        out_specs=(pl.BlockSpec((tm, FF_STEP * tf), lambda i, j: (i, j)), _rows(tm, D_MODEL), _rows(tm, D_MODEL),
                   _full((8, D_MODEL))),
        scratch_shapes=[pltpu.VMEM((tm, D_MODEL), F32)],
        compiler_params=_cp("arbitrary", "arbitrary"),
    )(hb, xhat1, target, w_ff1, w_ff2, ln1_g, ln1_b, ln_g, ln_b)


def _bwd_mlp(dz2, dz2b, u, xhat1, rstd1, w_ff1, w_ff2, ln_g, *, seq):
    tokens = dz2.shape[0]
    tm = min(2 * TOKEN_TILE, seq)
    tf = FF_SHARD
    nf = N_DEV // FF_STEP

    def body(dz_ref, dzb_ref, u_ref, xh_ref, rs_ref, w1_ref, w2_ref, g_ref, du_ref, dz1_ref, dz1b_ref, stat_ref, acc):
        i, j = pl.program_id(0), pl.program_id(1)

        @pl.when((i == 0) & (j == 0))
        def _():
            stat_ref[...] = jnp.zeros_like(stat_ref)

        @pl.when(j == 0)
        def _():
            acc[...] = jnp.zeros_like(acc)

        da = _dot_nt(dzb_ref[...], w2_ref[...])
        du = (da * (2.0 * jnp.maximum(u_ref[...].astype(F32), 0.0))).astype(BF16)
        du_ref[...] = du
        part = _dot_nt(du[:, 0:tf], w1_ref[0])
        for s in range(1, FF_STEP):
            part = part + _dot_nt(du[:, s * tf:(s + 1) * tf], w1_ref[s])
        acc[...] += part

        @pl.when(j == nf - 1)
        def _():
            dh = ALPHA * dz_ref[...] + acc[...]
            xhat = xh_ref[...]
            dz1 = _ln_bwd(dh, xhat, rs_ref[:, 0:1], g_ref[...])
            dz1_ref[...] = dz1
            dz1b_ref[...] = dz1.astype(BF16)
            stat_ref[0:1, :] += jnp.sum(dh * xhat, axis=0, keepdims=True)
            stat_ref[1:2, :] += jnp.sum(dh, axis=0, keepdims=True)

    return pl.pallas_call(
        body, name="bwd_mlp", grid=(tokens // tm, nf),
        out_shape=(jax.ShapeDtypeStruct((tokens, D_FF), BF16), jax.ShapeDtypeStruct((tokens, D_MODEL), F32),
                   jax.ShapeDtypeStruct((tokens, D_MODEL), BF16), jax.ShapeDtypeStruct((8, D_MODEL), F32)),
        in_specs=[_rows(tm, D_MODEL), _rows(tm, D_MODEL), pl.BlockSpec((tm, FF_STEP * tf), lambda i, j: (i, j)),
                  _rows(tm, D_MODEL), _rows(tm, LANES),
                  pl.BlockSpec((FF_STEP, D_MODEL, tf), lambda i, j: (j, 0, 0)),
                  pl.BlockSpec((FF_STEP * tf, D_MODEL), lambda i, j: (j, 0)),
                  _full(ln_g.shape)],
        out_specs=(pl.BlockSpec((tm, FF_STEP * tf), lambda i, j: (i, j)), _rows(tm, D_MODEL), _rows(tm, D_MODEL),
                   _full((8, D_MODEL))),
        scratch_shapes=[pltpu.VMEM((tm, D_MODEL), F32)],
        compiler_params=_cp("arbitrary", "arbitrary"),
    )(dz2, dz2b, u, xhat1, rstd1, w_ff1, w_ff2, ln_g)


def _bwd_mix(dz1b, gates, y_a, y_b, b_gate, w_oa, w_ob, w_out, *, seq):
    tokens = dz1b.shape[0]
    tm = min(MIX_TILE, seq)

    def body(dz_ref, gt_ref, ya_ref, yb_ref, bg_ref, woa_ref, wob_ref, wout_ref,
             dgt_ref, dya_ref, dyb_ref, doa_ref, dob_ref, stat_ref):
        @pl.when(pl.program_id(0) == 0)
        def _():
            stat_ref[...] = jnp.zeros_like(stat_ref)

        dmix = _dot_nt(dz_ref[...], wout_ref[...])
        for k, (y_ref, w_ref, dy_ref, do_ref) in enumerate(((ya_ref, woa_ref, dya_ref, doa_ref), (yb_ref, wob_ref, dyb_ref, dob_ref))):
            g = _sigmoid(gt_ref[:, k * D_MODEL:(k + 1) * D_MODEL].astype(F32) + bg_ref[k:k + 1, :])
            dgate = dmix * y_ref[...].astype(F32) * g * (1.0 - g)
            dgt_ref[:, k * D_MODEL:(k + 1) * D_MODEL] = dgate.astype(BF16)
            stat_ref[k:k + 1, :] += jnp.sum(dgate, axis=0, keepdims=True)
            dy = (dmix * g).astype(BF16)
            dy_ref[...] = dy
            do_ref[...] = _dot_nt(dy, w_ref[...]).astype(BF16)

    outs = [(2 * D_MODEL, BF16), (D_MODEL, BF16), (D_MODEL, BF16), (DIL_WIDTH, BF16), (DIL_WIDTH, BF16)]
    return pl.pallas_call(
        body, name="bwd_mix", grid=(tokens // tm,),
        out_shape=tuple(jax.ShapeDtypeStruct((tokens, w), dt) for w, dt in outs) + (jax.ShapeDtypeStruct((8, D_MODEL), F32),),
        in_specs=[_rows(tm, D_MODEL), _rows(tm, 2 * D_MODEL), _rows(tm, D_MODEL), _rows(tm, D_MODEL),
                  _full(b_gate.shape), _full(w_oa.shape), _full(w_ob.shape), _full(w_out.shape)],
        out_specs=tuple(_rows(tm, w) for w, _ in outs) + (_full((8, D_MODEL)),),
        compiler_params=_cp("arbitrary"),
    )(dz1b, gates, y_a, y_b, b_gate, w_oa, w_ob, w_out)


def _bwd_proj(dqp, dkp, dvm, dq_d, dk_d, dv_d, dgates, dz1, low, w_in_ext, w1, wk, wv, g_q, g_kv, cext, sext, cs128, *, seq):
    tokens = dz1.shape[0]
    tm = min(TOKEN_TILE, seq)
    ns = seq // tm

    def body(dqp_ref, dkp_ref, dvm_ref, dqd_ref, dkd_ref, dvd_ref, dgt_ref, dz_ref, low_ref, win_ref, w1_ref, wk_ref,
             wv_ref, gq_ref, gkv_ref, c_ref, s_ref, cs_ref, dx_ref, dproj_ref, da_ref, dkn_ref, stat_ref):
        @pl.when(pl.program_id(0) == 0)
        def _():
            stat_ref[...] = jnp.zeros_like(stat_ref)

        low = low_ref[...]
        d_a = _rope_slabs(dqp_ref[...].astype(F32), c_ref[...], s_ref[...], True).astype(BF16)
        da_ref[...] = d_a
        q_a = low[:, 0:Q_LORA]
        _, rq = _rms(q_a, gq_ref[...])
        dq_a, gq_terms = _rms_bwd(_dot_nt(d_a, w1_ref[...]), q_a, rq, gq_ref[...])
        kv_a = low[:, Q_LORA:Q_LORA + KV_LORA]
        _, rkv = _rms(kv_a, gkv_ref[...])
        dkn = jnp.concatenate([dkp_ref[:, p * PAIR_W:p * PAIR_W + LANES] for p in range(N_PAIRS)], axis=1)
        dkn_ref[...] = dkn
        dkv_a, gkv_terms = _rms_bwd(_dot_nt(dkn, wk_ref[...]) + _dot_nt(dvm_ref[...], wv_ref[...]), kv_a, rkv, gkv_ref[...])
        dkr = sum(dkp_ref[:, p * PAIR_W + LANES:(p + 1) * PAIR_W].astype(F32) for p in range(N_PAIRS))
        dkr = dkr + pltpu.roll(dkr, LANES - ROPE, 1)
        dkr = jnp.where(lax.broadcasted_iota(I32, dkr.shape, 1) < ROPE, dkr, 0.0)
        dkr = (dkr + pltpu.roll(dkr, ROPE, 1)) * cs_ref[...]
        stat_ref[0:1, 0:Q_LORA] += jnp.sum(gq_terms, axis=0, keepdims=True)
        stat_ref[1:2, 0:KV_LORA] += jnp.sum(gkv_terms, axis=0, keepdims=True)
        dproj_ref[:, 0:Q_LORA] = dq_a.astype(BF16)
        dproj_ref[:, Q_LORA:Q_LORA + KV_LORA] = dkv_a.astype(BF16)
        dproj_ref[:, Q_LORA + KV_LORA:LOW_W] = dkr.astype(BF16)
        dproj_ref[:, LOW_W:LOW_W + DIL_WIDTH] = dqd_ref[...]
        dproj_ref[:, LOW_W + DIL_WIDTH:LOW_W + 2 * DIL_WIDTH] = dkd_ref[...]
        dproj_ref[:, LOW_W + 2 * DIL_WIDTH:LOW_W + 3 * DIL_WIDTH] = dvd_ref[...]
        dproj_ref[:, LOW_W + 3 * DIL_WIDTH:] = dgt_ref[...]
        dx_ref[...] = ALPHA * dz_ref[...] + _dot_nt(dproj_ref[...], win_ref[...])

    wide = N_PAIRS * PAIR_W
    return pl.pallas_call(
        body, name="bwd_proj", grid=(tokens // tm,),
        out_shape=(jax.ShapeDtypeStruct((tokens, D_MODEL), F32), jax.ShapeDtypeStruct((tokens, IN_EXT), BF16),
                   jax.ShapeDtypeStruct((tokens, wide), BF16), jax.ShapeDtypeStruct((tokens, N_HEADS * NOPE), BF16),
                   jax.ShapeDtypeStruct((8, D_MODEL), F32)),
        in_specs=[_rows(tm, wide), _rows(tm, wide), _rows(tm, DIL_WIDTH), _rows(tm, DIL_WIDTH), _rows(tm, DIL_WIDTH),
                  _rows(tm, DIL_WIDTH), _rows(tm, 2 * D_MODEL),
                  _rows(tm, D_MODEL), _rows(tm, LOW_W), _full(w_in_ext.shape), _full(w1.shape),
                  _full(wk.shape), _full(wv.shape), _full(g_q.shape), _full(g_kv.shape),
                  pl.BlockSpec((tm, LANES), lambda i: (i % ns, 1)), pl.BlockSpec((tm, LANES), lambda i: (i % ns, 1)),
                  pl.BlockSpec((tm, LANES), lambda i: (i % ns, 0))],
        out_specs=(_rows(tm, D_MODEL), _rows(tm, IN_EXT), _rows(tm, wide), _rows(tm, N_HEADS * NOPE), _full((8, D_MODEL))),
        compiler_params=_cp("arbitrary"),
    )(dqp, dkp, dvm, dq_d, dk_d, dv_d, dgates, dz1, low, w_in_ext, w1, wk, wv, g_q, g_kv, cext, sext, cs128)


def _wgrad(a, b, name, square_relu=False, by_shard=False):
    tokens, ka = a.shape
    n = b.shape[1]
    if ka <= 512 or ka % 512 == 0:
        tka = min(ka, 512)
    else:
        tka = max(w for w in range(LANES, min(ka, 2304) + 1, LANES) if ka % w == 0)
    shard = n // N_DEV
    tn = WGRAD_SHARDS * shard if by_shard else max(w for w in range(LANES, min(n, 2304) + 1, LANES) if n % w == 0)
    tt = min(tokens, 1024)
    nt = tokens // tt

    def body(a_ref, b_ref, o_ref, acc):
        kt = pl.program_id(2)

        @pl.when(kt == 0)
        def _():
            acc[...] = jnp.zeros_like(acc)

        at = a_ref[...]
        if square_relu:
            at = jnp.square(jnp.maximum(at.astype(F32), 0.0)).astype(BF16)
        acc[...] += _dot_tn(at, b_ref[...])

        @pl.when(kt == nt - 1)
        def _():
            if by_shard:
                for s in range(WGRAD_SHARDS):
                    o_ref[s] = acc[:, s * shard:(s + 1) * shard].astype(BF16)
            else:
                o_ref[...] = acc[...].astype(BF16)

    if by_shard:
        out_shape, out_spec = (N_DEV, ka, shard), pl.BlockSpec((WGRAD_SHARDS, tka, shard), lambda i, j, k: (j, i, 0))
    else:
        out_shape, out_spec = (ka, n), pl.BlockSpec((tka, tn), lambda i, j, k: (i, j))
    return pl.pallas_call(
        body, name=name, grid=(ka // tka, n // tn, nt), out_shape=jax.ShapeDtypeStruct(out_shape, BF16),
        in_specs=[pl.BlockSpec((tt, tka), lambda i, j, k: (k, i)), pl.BlockSpec((tt, tn), lambda i, j, k: (k, j))],
        out_specs=out_spec,
        scratch_shapes=[pltpu.VMEM((tka, tn), F32)],
        compiler_params=_cp("parallel", "parallel", "arbitrary"),
    )(a, b)


def _adam_math(w, g, m, v):
    m = ADAM_B1 * m + (1.0 - ADAM_B1) * g
    v = ADAM_B2 * v + (1.0 - ADAM_B2) * jnp.square(g)
    m_hat = m / (1.0 - ADAM_B1 ** ADAM_STEP)
    v_hat = v / (1.0 - ADAM_B2 ** ADAM_STEP)
    return -ADAM_LR * (m_hat / (jnp.sqrt(v_hat) + ADAM_EPS) + ADAM_WD * w), m, v


def _adamw(items, name):
    steps = min(_tiles(*w.shape)[0] for w, *_ in items)
    n_items = len(items)

    def body(slot_ref, *refs):
        ins, outs = refs[:5 * n_items], refs[5 * n_items:]
        for k, (_, _, _, _, parts) in enumerate(items):
            w_ref, m_ref, v_ref, own_ref, p_ref = ins[5 * k:5 * k + 5]
            g_ref, d_ref, nm_ref, nv_ref = outs[4 * k:4 * k + 4]
            g = own_ref[...].astype(F32)
            for d in range(parts.shape[0]):
                g = g + p_ref[d].astype(F32)
            g_ref[...] = g
            d_ref[...], nm_ref[...], nv_ref[...] = _adam_math(w_ref[...], g, m_ref[...], v_ref[...])

    x, y, c = _place()
    in_specs, out_specs, out_shape, args = [], [], [], []
    for w, m, v, own, parts in items:
        rows, cols = w.shape
        _, tile, at = _tiles(rows, cols, steps)
        blk = pl.BlockSpec(tile, lambda i, slot, at=at: at(i))
        own_blk = blk if own.ndim == 2 else pl.BlockSpec((None, *tile), lambda i, slot, at=at: (slot[0], *at(i)))
        in_specs += [blk, blk, blk, own_blk, pl.BlockSpec((parts.shape[0], *tile), lambda i, slot, at=at: (0, *at(i)))]
        out_specs += [blk] * 4
        out_shape += [jax.ShapeDtypeStruct((rows, cols), F32)] * 4
        args += [w, m, v, own, parts]
    out = pl.pallas_call(
        body, name=name,
        grid_spec=pltpu.PrefetchScalarGridSpec(num_scalar_prefetch=1, grid=(steps,), in_specs=in_specs, out_specs=out_specs),
        out_shape=out_shape, compiler_params=_cp("parallel"),
    )(jnp.reshape(4 * x + 2 * y + c, (1,)).astype(I32), *args)
    return [tuple(out[4 * k:4 * k + 4]) for k in range(n_items)]


def _adamw_small(parts, w, m, v):
    _, rows, cols = parts.shape

    def body(p_ref, w_ref, m_ref, v_ref, g_ref, d_ref, nm_ref, nv_ref):
        g = p_ref[0]
        for d in range(1, N_DEV):
            g = g + p_ref[d]
        g_ref[...] = g
        d_ref[...], nm_ref[...], nv_ref[...] = _adam_math(w_ref[...], g, m_ref[...], v_ref[...])

    return pl.pallas_call(
        body, name="adamw_replicated", out_shape=(jax.ShapeDtypeStruct((rows, cols), F32),) * 4,
        in_specs=[_full(parts.shape)] + [_full((rows, cols))] * 3, out_specs=(_full((rows, cols)),) * 4, grid=(1,),
        compiler_params=_cp("arbitrary"),
    )(parts, w, m, v)


def _pad_rows(a2d, mult):
    pad = (-a2d.shape[-2]) % mult
    return jnp.pad(a2d, [(0, 0)] * (a2d.ndim - 2) + [(0, pad), (0, 0)]) if pad else a2d


def _pad_cols(a):
    pad = (-a.shape[-1]) % LANES
    return jnp.pad(a, [(0, 0)] * (a.ndim - 1) + [(0, pad)]) if pad else a


def _rot_cols(w):
    half = ROPE // 2
    return jnp.concatenate([-w[..., half:], w[..., :half]], axis=-1)


def _unrot_cols(dw):
    half = ROPE // 2
    return jnp.concatenate([dw[..., half:], -dw[..., :half]], axis=-1)


def _from_col_shards(stacked):
    return stacked.transpose(1, 0, 2).reshape(stacked.shape[1], -1)


def _to_col_shards(full):
    r = full.shape[0]
    return full.reshape(r, N_DEV, -1).transpose(1, 0, 2)


def _rope_tables(seq):
    half = ROPE // 2
    inv = jnp.power(ROPE_THETA, -jnp.arange(half, dtype=F32) / half)
    ang = jnp.arange(seq, dtype=F32)[:, None] * inv[None, :]
    cos = jnp.concatenate([jnp.cos(ang)] * 2, axis=1)
    sin = jnp.concatenate([jnp.sin(ang)] * 2, axis=1)
    ones, zeros = jnp.ones((seq, 2 * NOPE), F32), jnp.zeros((seq, 2 * NOPE), F32)
    pad = jnp.zeros((seq, PAIR_W - 2 * NOPE - 2 * ROPE), F32)
    cext = jnp.concatenate([ones, cos, cos, pad], axis=1)
    sext = jnp.concatenate([zeros, sin, sin, pad], axis=1)
    cs128 = jnp.concatenate([cos, sin, jnp.zeros((seq, LANES - 2 * ROPE), F32)], axis=1)
    return cext, sext, cs128


def _pair_slabs(nope, rope):
    k = nope.shape[0]
    nope = nope.reshape(k, N_PAIRS, 2 * NOPE)
    rope = jnp.zeros((k, N_PAIRS, 2 * ROPE), nope.dtype) if rope is None else rope.reshape(k, N_PAIRS, 2 * ROPE)
    pad = jnp.zeros((k, N_PAIRS, PAIR_W - 2 * NOPE - 2 * ROPE), nope.dtype)
    return jnp.concatenate([nope, rope, pad], axis=2).reshape(k, N_PAIRS * PAIR_W)


def _split_slabs(slabs):
    k = slabs.shape[0]
    s = slabs.reshape(k, N_PAIRS, PAIR_W)
    return s[:, :, :2 * NOPE].reshape(k, N_HEADS, NOPE), s[:, :, 2 * NOPE:2 * NOPE + 2 * ROPE].reshape(k, N_HEADS, ROPE)


def kernel(x, w_in, b_gate, g_q_a, w_uq, g_kv_a, w_ukv, w_o_mla, w_o_dil, w_out, ln1_g, ln1_b, w_ff1, w_ff2, ln2_g, ln2_b, loss_target, m_w_in, m_b_gate, m_g_q_a, m_w_uq, m_g_kv_a, m_w_ukv, m_w_o_mla, m_w_o_dil, m_w_out, m_ln1_g, m_ln1_b, m_w_ff1, m_w_ff2, m_ln2_g, m_ln2_b, v_w_in, v_b_gate, v_g_q_a, v_w_uq, v_g_kv_a, v_w_ukv, v_w_o_mla, v_w_o_dil, v_w_out, v_ln1_g, v_ln1_b, v_w_ff1, v_w_ff2, v_ln2_g, v_ln2_b):
    batch, seq, _ = x.shape
    tokens = batch * seq
    weights = dict(w_in=w_in, w_uq=w_uq, w_ukv=w_ukv, w_o_mla=w_o_mla, w_o_dil=w_o_dil, w_out=w_out, w_ff1=w_ff1, w_ff2=w_ff2, b_gate=b_gate)
    mom_m = dict(w_in=m_w_in, w_uq=m_w_uq, w_ukv=m_w_ukv, w_o_mla=m_w_o_mla, w_o_dil=m_w_o_dil, w_out=m_w_out, w_ff1=m_w_ff1, w_ff2=m_w_ff2, b_gate=m_b_gate)
    mom_v = dict(w_in=v_w_in, w_uq=v_w_uq, w_ukv=v_w_ukv, w_o_mla=v_w_o_mla, w_o_dil=v_w_o_dil, w_out=v_w_out, w_ff1=v_w_ff1, w_ff2=v_w_ff2, b_gate=v_b_gate)

    first = ["w_in", "w_uq", "w_ukv"]
    widths = [weights[n].shape[2] for n in first]
    shards = [weights["w_in"][0].T.astype(BF16)] + [_pad_cols(weights[n][0].astype(BF16)) for n in first[1:]]
    g_in, g_uq, g_ukv = _run_comm(_Gather(shards), shards, "all_gather_first_weights")
    g_uq, g_ukv = g_uq[:, :, :widths[1]], g_ukv[:, :, :widths[2]]

    s1, s2, n_in = Q_LORA + KV_LORA, Q_LORA + KV_LORA + ROPE, N_DEV * widths[0]

    def w_in_cols(lo, hi):
        out = []
        while lo < hi:
            d, off = divmod(lo, widths[0])
            take = min(hi - lo, widths[0] - off)
            out.append(g_in[d][off:off + take].T)
            lo += take
        return out

    w_in_ext = jnp.concatenate(w_in_cols(0, s2) + [_rot_cols(jnp.concatenate(w_in_cols(s1, s2), axis=1)),
                                                   jnp.zeros((D_MODEL, LOW_W - s2 - ROPE), BF16)] + w_in_cols(s2, n_in), axis=1)
    uq = _from_col_shards(g_uq).reshape(Q_LORA, N_HEADS, NOPE + ROPE)
    w1 = _pair_slabs(uq[:, :, :NOPE], uq[:, :, NOPE:])
    ukv = _from_col_shards(g_ukv).reshape(KV_LORA, N_HEADS, NOPE + HEAD_V)
    wk = ukv[:, :, :NOPE].reshape(KV_LORA, N_HEADS * NOPE)
    wv = ukv[:, :, NOPE:].reshape(KV_LORA, N_HEADS * HEAD_V)
    cext, sext, cs128 = _rope_tables(seq)
    dil_bias = _dilated_bias_table(seq)
    no_bias = jnp.zeros((1, 8, LANES), F32)

    x2 = x.reshape(tokens, D_MODEL)
    low, gates, qkvd, qp, kp, vm, qn, kvn, xb = _fwd_proj(x2, w_in_ext, w1, wk, wv, g_q_a, g_kv_a, cext, sext, cs128, seq=seq)
    bg = b_gate[0]
    bg_hi = bg.astype(BF16)
    bg_lo = (bg - bg_hi.astype(F32)).astype(BF16)
    later = [weights[n][0].astype(BF16) for n in ("w_o_mla", "w_o_dil", "w_out", "w_ff1", "w_ff2")]
    later.append(_pad_rows(jnp.concatenate([bg_hi, bg_lo], axis=0), 16))
    mla = dict(batch=batch, seq=seq, width=PAIR_W, col0=(0, 0, 0), dilated=False, scale=MLA_SCALE)
    dil = dict(batch=batch, seq=seq, width=LANES, col0=(0, N_PAIRS, 2 * N_PAIRS), dilated=True, scale=DIL_SCALE)
    o_a, lse_a, g_oa, g_ob, g_out, g_ff1, g_ff2, g_bg = _attn_fwd(
        qp, kp, vm, no_bias, name="mla_attention_fwd", comm=_Gather(later), comm_arrays=later, **mla)
    o_b, lse_b = _attn_fwd(qkvd, qkvd, qkvd, dil_bias, name="dilated_attention_fwd", **dil)
    w_oa, w_ob = _from_col_shards(g_oa), _from_col_shards(g_ob)
    w_out_full = g_out.reshape(D_MODEL, D_MODEL)
    w_ff2_full = g_ff2.reshape(D_FF, D_MODEL)
    bg_parts = g_bg.astype(F32)
    b_gate_full = _from_col_shards(bg_parts[:, 0:2] + bg_parts[:, 2:4])
    hb, xhat1, rstd1, y_a, y_b, mix = _fwd_mix(o_a, o_b, gates, x2, b_gate_full, w_oa, w_ob, w_out_full, ln1_g, ln1_b, seq=seq)
    u, dz2, dz2b, stat2 = _fwd_mlp(hb, xhat1, loss_target.reshape(tokens, D_MODEL), g_ff1, w_ff2_full, ln1_g, ln1_b, ln2_g, ln2_b, seq=seq)

    du, dz1, dz1b, stat1 = _bwd_mlp(dz2, dz2b, u, xhat1, rstd1, g_ff1, w_ff2_full, ln1_g, seq=seq)
    dw_ff = [_wgrad(hb, du, "wgrad_ff1", by_shard=True),
             _wgrad(u, dz2b, "wgrad_ff2", square_relu=True).reshape(N_DEV, FF_SHARD, D_MODEL)]
    dgates, dy_a, dy_b, do_a, do_b, stat_g = _bwd_mix(dz1b, gates, y_a, y_b, b_gate_full, w_oa, w_ob, w_out_full, seq=seq)
    dqp, dkp, dvm, r_ff1, r_ff2 = _attn_bwd(qp, kp, vm, o_a, do_a, lse_a, no_bias, name="mla_attention_bwd",
                                            comm=_Scatter(dw_ff), comm_arrays=dw_ff, **mla)
    dw_mid = [_to_col_shards(_wgrad(o_a, dy_a, "wgrad_o_mla")), _to_col_shards(_wgrad(o_b, dy_b, "wgrad_o_dil")),
              _wgrad(mix, dz1b, "wgrad_out").reshape(N_DEV, D_MODEL // N_DEV, D_MODEL),
              _pad_rows(_to_col_shards(stat_g[0:2]).astype(BF16), 16)]
    dq_d, dk_d, dv_d, r_oa, r_ob, r_out, r_bg = _attn_bwd(qkvd, qkvd, qkvd, o_b, do_b, lse_b, dil_bias, name="dilated_attention_bwd",
                                                          comm=_Scatter(dw_mid), comm_arrays=dw_mid, **dil)
    grad_x, dproj, d_a, dkn, stat_r = _bwd_proj(dqp, dkp, dvm, dq_d, dk_d, dv_d, dgates, dz1, low, w_in_ext, w1, wk, wv,
                                                g_q_a, g_kv_a, cext, sext, cs128, seq=seq)

    dw_in_ext = _wgrad(dproj, xb, "wgrad_in")
    dw1 = _wgrad(qn, d_a, "wgrad_uq")
    dwk = _wgrad(kvn, dkn, "wgrad_ukv_k")
    dwv = _wgrad(kvn, dvm, "wgrad_ukv_v")
    dw_kr = dw_in_ext[s1:s2] + _unrot_cols(dw_in_ext[s2:s2 + ROPE].T).T

    def dw_in_cols(lo, hi):
        out = []
        for a, b, piece in ((0, s1, lambda u, v: dw_in_ext[u:v]), (s1, s2, lambda u, v: dw_kr[u - s1:v - s1]),
                            (s2, n_in, lambda u, v: dw_in_ext[u + LOW_W - s2:v + LOW_W - s2])):
            if max(lo, a) < min(hi, b):
                out.append(piece(max(lo, a), min(hi, b)))
        return out

    dw_in = jnp.stack([jnp.concatenate(dw_in_cols(d * widths[0], (d + 1) * widths[0]), axis=0) for d in range(N_DEV)])
    n1, r1 = _split_slabs(dw1)
    dw_uq = jnp.concatenate([n1, r1], axis=2).reshape(Q_LORA, N_HEADS * (NOPE + ROPE))
    dw_ukv = jnp.concatenate([dwk.reshape(KV_LORA, N_HEADS, NOPE), dwv.reshape(KV_LORA, N_HEADS, HEAD_V)], axis=2).reshape(KV_LORA, N_HEADS * (NOPE + HEAD_V))
    last = [dw_in] + [_pad_cols(_to_col_shards(dw)) for dw in (dw_uq, dw_ukv)]
    theirs = _rs_sibling(last, "rs_last_sibling_exchange")
    sums = [_pair_sum(a, b, "rs_last_pair_sum_" + n) for a, b, n in zip(last, theirs, first)]
    partial = jnp.concatenate([stat_r[0:1, :Q_LORA], stat_r[1:2, :KV_LORA], stat1[0:1], stat1[1:2], stat2[0:1], stat2[1:2],
                               stat2[2:3, :LANES]], axis=1)
    partial = _pad_rows(partial.reshape(-1, LANES), 8)
    rest = [s[1] for s in sums]
    got_in, got_uq, got_ukv, every = _run_comm(_Plans([_ChipExchange(rest), _Gather([partial])]), rest + [partial],
                                               "rs_last_chip_exchange")

    upd = {}
    early = ["w_ff1", "w_ff2", "w_out", "w_o_mla", "w_o_dil"]
    items = [(weights[n][0], mom_m[n][0], mom_v[n][0], own, parts) for n, own, parts in
             zip(early, (dw_ff[0], dw_ff[1], dw_mid[2], dw_mid[0], dw_mid[1]), (r_ff1, r_ff2, r_out, r_oa, r_ob))]
    upd.update(zip(early, _adamw(items, "adamw_early_weights")))
    (in_t,) = _adamw([(weights["w_in"][0].T, mom_m["w_in"][0].T, mom_v["w_in"][0].T, sums[0][0], got_in)], "adamw_w_in")
    upd["w_in"] = tuple(a.T for a in in_t)
    for n, w, (own, _), parts in zip(first[1:], widths[1:], sums[1:], (got_uq, got_ukv)):
        (upd[n],) = _adamw([(weights[n][0], mom_m[n][0], mom_v[n][0], own[:, :w], parts[:, :, :w])], "adamw_" + n)
    (bg_upd,) = _adamw([(_pad_rows(b_gate[0], 16), _pad_rows(m_b_gate[0], 16), _pad_rows(v_b_gate[0], 16), dw_mid[3], r_bg)],
                       "adamw_b_gate")
    upd["b_gate"] = tuple(t[0:2] for t in bg_upd)

    small_w = [g_q_a, g_kv_a, ln1_g, ln1_b, ln2_g, ln2_b]
    small_m = [m_g_q_a, m_g_kv_a, m_ln1_g, m_ln1_b, m_ln2_g, m_ln2_b]
    small_v = [v_g_q_a, v_g_kv_a, v_ln1_g, v_ln1_b, v_ln2_g, v_ln2_b]
    small_widths = [a.shape[1] for a in small_w]

    def as_rows(vecs, extra):
        flat = jnp.concatenate(vecs + [jnp.zeros((1, extra), F32)], axis=1)
        return _pad_rows(flat.reshape(-1, LANES), 8)

    g_s, d_s, nm_s, nv_s = _adamw_small(every, as_rows(small_w, LANES), as_rows(small_m, LANES), as_rows(small_v, LANES))

    def split_small(a):
        flat = a.reshape(1, -1)
        out, c0 = [], 0
        for w in small_widths:
            out.append(flat[:, c0:c0 + w])
            c0 += w
        return out, flat[0, c0]

    g_small, loss = split_small(g_s)
    small = [g_small, split_small(d_s)[0], split_small(nm_s)[0], split_small(nv_s)[0]]

    order = ["w_in", "b_gate", "g_q_a", "w_uq", "g_kv_a", "w_ukv", "w_o_mla", "w_o_dil", "w_out", "ln1_g", "ln1_b", "w_ff1", "w_ff2", "ln2_g", "ln2_b"]
    small_names = ["g_q_a", "g_kv_a", "ln1_g", "ln1_b", "ln2_g", "ln2_b"]

    def pick(kind):
        return [small[kind][small_names.index(n)] if n in small_names else upd[n][kind][None] for n in order]

    return (loss, grad_x.reshape(batch, seq, D_MODEL), *pick(0), *pick(1), *pick(2), *pick(3))
```

```python
import functools
import math

import jax
import jax.numpy as jnp
from jax import lax
from jax.experimental import pallas as pl
from jax.experimental.pallas import tpu as pltpu

F32 = jnp.float32
BF16 = jnp.bfloat16
I32 = jnp.int32

D_MODEL = 1024
N_HEADS = 8
NOPE = 64
ROPE = 32
HEAD_V = 64
Q_LORA = 384
KV_LORA = 256
DIL_WIDTH = 512
D_FF = 4096
ROPE_THETA = 10000.0
LN_EPS = 1e-5
RMS_EPS = 1e-6
NEG = -1e30
ALPHA = 2.0 ** 0.25
MLA_SCALE = (NOPE + ROPE) ** -0.5
DIL_SCALE = 64 ** -0.5
ADAM_LR, ADAM_B1, ADAM_B2, ADAM_EPS, ADAM_WD, ADAM_STEP = 0.001, 0.9, 0.999, 1e-08, 0.01, 10

LANES = 128
PAIR_W = 256
N_PAIRS = N_HEADS // 2
LOW_W = 768
IN_EXT = LOW_W + 3 * DIL_WIDTH + 2 * D_MODEL
N_DEV = 8
FF_SHARD = D_FF // N_DEV
FF_STEP = 4
WGRAD_SHARDS = 4
TOKEN_TILE = 256
MIX_TILE = 512
ATTN_TILE = 256
VMEM_LIMIT = 56 << 20

MESH = pl.DeviceIdType.MESH
ANY = pl.BlockSpec(memory_space=pl.ANY)
CHIP_FLIPS = ((0, 0), (0, 1), (1, 0), (1, 1))
PEER_FLIPS = tuple((fx, fy, fc) for fx in (0, 1) for fy in (0, 1) for fc in (0, 1))[1:]


def _cp(*sem):
    return pltpu.CompilerParams(dimension_semantics=sem or None, vmem_limit_bytes=VMEM_LIMIT)


def _full(shape):
    nd = len(shape)
    return pl.BlockSpec(shape, lambda *_: (0,) * nd)


def _rows(tm, width):
    return pl.BlockSpec((tm, width), lambda i, *_: (i, 0))


def _dot(a, b):
    return jnp.dot(a, b, preferred_element_type=F32)


def _dot_nt(a, b):
    return lax.dot_general(a, b, (((1,), (1,)), ((), ())), preferred_element_type=F32)


def _dot_tn(a, b):
    return lax.dot_general(a, b, (((0,), (0,)), ((), ())), preferred_element_type=F32)


def _sigmoid(z):
    return 1.0 / (1.0 + jnp.exp(-z))


def _place():
    return lax.axis_index("x"), lax.axis_index("y"), lax.axis_index("c")


def _flip(v, f):
    return 1 - v if f else v


class _Gather:
    def __init__(self, shards):
        self.n = len(shards)
        self.out_shape = [jax.ShapeDtypeStruct((N_DEV, *s.shape), s.dtype) for s in shards]
        self.scratch = [pltpu.SemaphoreType.DMA((7 * self.n,)), pltpu.SemaphoreType.DMA((7 * self.n,)),
                        pltpu.SemaphoreType.DMA((self.n,))]

    def _copies(self, what, srcs, dsts, send, recv, local):
        x, y, c = _place()
        chips = [(_flip(x, fx), _flip(y, fy)) for fx, fy in CHIP_FLIPS[1:]]
        out = []
        for a in range(self.n):
            def slot(px, py, pc, a=a):
                return dsts[a].at[4 * px + 2 * py + pc]

            def copy(k, block, to, src=None, a=a, slot=slot):
                return pltpu.make_async_remote_copy(
                    src_ref=slot(*block) if src is None else src, dst_ref=slot(*block),
                    send_sem=send.at[7 * a + k], recv_sem=recv.at[7 * a + k], device_id=to, device_id_type=MESH)

            if what == "mine":
                out.append(pltpu.make_async_copy(srcs[a], slot(x, y, c), local.at[a]))
            elif what == "first":
                out.append(copy(0, (x, y, c), (x, y, 1 - c), src=srcs[a]))
                out += [copy(1 + j, (x, y, c), (*chip, c), src=srcs[a]) for j, chip in enumerate(chips)]
            elif what == "landed":
                out += [copy(1 + j, (*chip, c), (x, y, c)) for j, chip in enumerate(chips)]
            elif what == "passed":
                out += [copy(4 + j, (*chip, c), (x, y, 1 - c)) for j, chip in enumerate(chips)]
            else:
                out.append(copy(0, (x, y, 1 - c), (x, y, c)))
                out += [copy(4 + j, (*chip, 1 - c), (x, y, c)) for j, chip in enumerate(chips)]
        return out

    def start(self, *refs):
        for cp in self._copies("first", *refs) + self._copies("mine", *refs):
            cp.start()

    def forward(self, *refs):
        for landed, passed in zip(self._copies("landed", *refs), self._copies("passed", *refs)):
            landed.wait_recv()
            passed.start()

    def finish(self, *refs):
        for cp in self._copies("from_sibling", *refs):
            cp.wait_recv()
        for cp in self._copies("first", *refs) + self._copies("passed", *refs):
            cp.wait_send()
        for cp in self._copies("mine", *refs):
            cp.wait()


class _Scatter:
    def __init__(self, arrays):
        self.n = len(arrays)
        self.out_shape = [jax.ShapeDtypeStruct((7, *a.shape[1:]), a.dtype) for a in arrays]
        self.scratch = [pltpu.SemaphoreType.DMA((7 * self.n,)), pltpu.SemaphoreType.DMA((7 * self.n,))]

    def _copies(self, srcs, dsts, send, recv):
        x, y, c = _place()
        out = []
        for a in range(self.n):
            for k, (fx, fy, fc) in enumerate(PEER_FLIPS):
                px, py, pc = _flip(x, fx), _flip(y, fy), _flip(c, fc)
                out.append(pltpu.make_async_remote_copy(
                    src_ref=srcs[a].at[4 * px + 2 * py + pc], dst_ref=dsts[a].at[k],
                    send_sem=send.at[7 * a + k], recv_sem=recv.at[7 * a + k], device_id=(px, py, pc), device_id_type=MESH))
        return out

    def start(self, *refs):
        for cp in self._copies(*refs):
            cp.start()

    def forward(self, *refs):
        pass

    def finish(self, *refs):
        for cp in self._copies(*refs):
            cp.wait_send()
        for cp in self._copies(*refs):
            cp.wait_recv()


class _ChipExchange:
    def __init__(self, arrays):
        self.n = len(arrays)
        self.out_shape = [jax.ShapeDtypeStruct(a.shape, a.dtype) for a in arrays]
        self.scratch = [pltpu.SemaphoreType.DMA((3 * self.n,)), pltpu.SemaphoreType.DMA((3 * self.n,))]

    def _copies(self, srcs, dsts, send, recv):
        x, y, c = _place()
        return [pltpu.make_async_remote_copy(
            src_ref=srcs[a].at[k], dst_ref=dsts[a].at[k], send_sem=send.at[3 * a + k], recv_sem=recv.at[3 * a + k],
            device_id=(_flip(x, fx), _flip(y, fy), c), device_id_type=MESH)
            for a in range(self.n) for k, (fx, fy) in enumerate(CHIP_FLIPS[1:])]

    def start(self, *refs):
        for cp in self._copies(*refs):
            cp.start()

    def forward(self, *refs):
        pass

    def finish(self, *refs):
        for cp in self._copies(*refs):
            cp.wait_send()
        for cp in self._copies(*refs):
            cp.wait_recv()


class _Plans:
    def __init__(self, plans):
        self.plans = plans
        self.n = sum(p.n for p in plans)
        self.out_shape = [s for p in plans for s in p.out_shape]
        self.scratch = [s for p in plans for s in p.scratch]

    def _each(self, phase, srcs, dsts, *sems):
        i0 = s0 = 0
        for p in self.plans:
            getattr(p, phase)(srcs[i0:i0 + p.n], dsts[i0:i0 + p.n], *sems[s0:s0 + len(p.scratch)])
            i0, s0 = i0 + p.n, s0 + len(p.scratch)

    def start(self, *refs):
        self._each("start", *refs)

    def forward(self, *refs):
        self._each("forward", *refs)

    def finish(self, *refs):
        self._each("finish", *refs)


def _run_comm(comm, arrays, name):
    n = comm.n

    def body(*refs):
        args = (refs[:n], refs[n:2 * n], *refs[2 * n:])
        comm.start(*args)
        comm.forward(*args)
        comm.finish(*args)

    return pl.pallas_call(body, name=name, out_shape=comm.out_shape, in_specs=[ANY] * n, out_specs=[ANY] * n,
                          scratch_shapes=comm.scratch)(*arrays)


def _rs_sibling(arrays, name):
    n = len(arrays)

    def body(*refs):
        srcs, got, (send, recv) = refs[:n], refs[n:2 * n], refs[2 * n:]
        x, y, c = _place()
        copies = []
        for a in range(n):
            for r, (fx, fy) in enumerate(CHIP_FLIPS):
                chip = 2 * _flip(x, fx) + _flip(y, fy)
                copies.append(pltpu.make_async_remote_copy(
                    src_ref=srcs[a].at[2 * chip + 1 - c], dst_ref=got[a].at[r], send_sem=send.at[4 * a + r],
                    recv_sem=recv.at[4 * a + r], device_id=(x, y, 1 - c), device_id_type=MESH))
        for cp in copies:
            cp.start()
        for cp in copies:
            cp.wait_send()
        for cp in copies:
            cp.wait_recv()

    return pl.pallas_call(
        body, name=name, out_shape=[jax.ShapeDtypeStruct((4, *a.shape[1:]), a.dtype) for a in arrays],
        in_specs=[ANY] * n, out_specs=[ANY] * n,
        scratch_shapes=[pltpu.SemaphoreType.DMA((4 * n,)), pltpu.SemaphoreType.DMA((4 * n,))],
    )(*arrays)


def _chip_slots():
    x, y, c = _place()
    return jnp.stack([4 * _flip(x, fx) + 2 * _flip(y, fy) + c for fx, fy in CHIP_FLIPS]).astype(I32)


def _tiles(rows, cols, steps=4):
    if rows % (16 * steps) == 0:
        return steps, (rows // steps, cols), lambda i: (i, 0)
    if cols % (LANES * steps) == 0:
        return steps, (rows, cols // steps), lambda i: (0, i)
    return 1, (rows, cols), lambda i: (0, 0)


def _pair_sum(full, theirs, name):
    _, rows, cols = theirs.shape
    steps, tile, at = _tiles(rows, cols)

    def body(slots_ref, m0_ref, m1_ref, m2_ref, m3_ref, b_ref, own_ref, rest_ref):
        own_ref[...] = m0_ref[...].astype(F32) + b_ref[0].astype(F32)
        for k, m_ref in enumerate((m1_ref, m2_ref, m3_ref)):
            rest_ref[k] = (m_ref[...].astype(F32) + b_ref[k + 1].astype(F32)).astype(BF16)

    def mine(k):
        return pl.BlockSpec((None, *tile), lambda i, slots: (slots[k], *at(i)))

    return pl.pallas_call(
        body, name=name,
        grid_spec=pltpu.PrefetchScalarGridSpec(
            num_scalar_prefetch=1, grid=(steps,),
            in_specs=[mine(0), mine(1), mine(2), mine(3), pl.BlockSpec((4, *tile), lambda i, slots: (0, *at(i)))],
            out_specs=(pl.BlockSpec(tile, lambda i, slots: at(i)), pl.BlockSpec((3, *tile), lambda i, slots: (0, *at(i))))),
        out_shape=(jax.ShapeDtypeStruct((rows, cols), F32), jax.ShapeDtypeStruct((3, rows, cols), BF16)),
        compiler_params=_cp("parallel"),
    )(_chip_slots(), full, full, full, full, theirs)


def _head_lanes(width, h):
    lane = lax.broadcasted_iota(I32, (1, width), 1)
    if width == LANES:
        return (lane >= 64 * h) & (lane < 64 * h + 64)
    nope = (lane >= NOPE * h) & (lane < NOPE * h + NOPE)
    rope = (lane >= 2 * NOPE + ROPE * h) & (lane < 2 * NOPE + ROPE * h + ROPE)
    return nope | rope


def _dilated_bias_table(seq):
    t = min(ATTN_TILE, seq)
    nd = seq // t

    def body(o_ref):
        delta = pl.program_id(0) * t + lax.broadcasted_iota(I32, (t, t), 1) - lax.broadcasted_iota(I32, (t, t), 0)
        mult = ((delta <= 128).astype(I32) + (((delta & 3) == 0) & (delta <= 512)).astype(I32)
                + ((delta & 15) == 0).astype(I32))
        logm = jnp.where(mult == 3, math.log(3.0), jnp.where(mult == 2, math.log(2.0), 0.0))
        valid = (delta >= 0) & (mult > 0)
        dist = delta.astype(F32)
        for h in range(N_HEADS):
            o_ref[h] = jnp.where(valid, logm - 2.0 ** (-(h + 1)) * dist, NEG)

    return pl.pallas_call(
        body, name="dilated_bias_table", grid=(nd,), out_shape=jax.ShapeDtypeStruct((N_HEADS, nd, t, t), F32),
        out_specs=pl.BlockSpec((N_HEADS, None, t, t), lambda d: (0, d, 0, 0)),
        compiler_params=_cp("parallel"),
    )()


def _comm_hooks(comm, refs, n_in, n_out):
    if comm is None:
        return refs[:n_in], refs[n_in:n_in + n_out], refs[n_in + n_out:], None
    n = comm.n
    ins, srcs = refs[:n_in], refs[n_in:n_in + n]
    outs, dsts = refs[n_in + n:n_in + n + n_out], refs[n_in + n + n_out:n_in + 2 * n + n_out]
    rest = refs[n_in + 2 * n + n_out:]
    own = len(rest) - len(comm.scratch)
    return ins, outs, rest[:own], (srcs, dsts, *rest[own:])


def _attn_fwd(q, k, v, bias, *, batch, seq, width, col0, dilated, scale, name, comm=None, comm_arrays=()):
    t = min(ATTN_TILE, seq)
    nq = seq // t
    cq, ck, cv = col0
    pre = scale if dilated else 1.0
    steps = batch * N_PAIRS

    def body(*refs):
        (q_ref, k_ref, v_ref, bias_ref), (o_ref, lse_ref), (v_heads,), plan = _comm_hooks(comm, refs, 4, 2)
        step_no = pl.program_id(0) * N_PAIRS + pl.program_id(1)
        if plan:
            pl.when(step_no == 0)(lambda: comm.start(*plan))
            pl.when(step_no == (3 * steps) // 4)(lambda: comm.forward(*plan))
        v_all = v_ref[...].astype(F32)
        for h in (0, 1):
            v_heads[h] = jnp.transpose(jnp.where(_head_lanes(LANES, h), v_all, 0.0)).astype(BF16)
        top = lax.broadcasted_iota(I32, (LANES, t), 0) < HEAD_V
        causal = lax.broadcasted_iota(I32, (t, t), 0) <= lax.broadcasted_iota(I32, (t, t), 1)
        def heads(i):
            q2 = q_ref[pl.ds(pl.multiple_of(i * t, t), t), :]
            q2 = q2 * pre if dilated else q2
            return [jnp.where(_head_lanes(width, h), q2, jnp.zeros_like(q2)) for h in (0, 1)]

        def scores(qh, j):
            kj = k_ref[pl.ds(pl.multiple_of(j * t, t), t), :]
            return tuple(_dot_nt(kj, qh[h]) for h in (0, 1))

        lax.fori_loop(0, nq, functools.partial(query_tile, heads, scores, bias_ref, o_ref, lse_ref, v_heads, top, causal),
                      scores(heads(0), 0))
        if plan:
            pl.when(step_no == steps - 1)(lambda: comm.finish(*plan))

    def query_tile(heads, scores, bias_ref, o_ref, lse_ref, v_heads, top, causal, i, first):
        qs = pl.multiple_of(i * t, t)
        qh = heads(i)

        def step(j, carry, last):
            m0, l0, m1, l1, acc, s0, s1 = carry
            ahead = scores(heads(jnp.minimum(i + 1, nq - 1)), 0) if last else scores(qh, j + 1)
            ks = pl.multiple_of(j * t, t)
            new, alphas, pv = [], [], []
            for h, (m, l, s) in enumerate(((m0, l0, s0), (m1, l1, s1))):
                if dilated:
                    s = s + bias_ref[h, i - j]
                else:
                    s = s * scale
                    if last:
                        s = jnp.where(causal, s, NEG)
                m_new = jnp.maximum(m, jnp.max(s, axis=0, keepdims=True))
                a = jnp.exp(m - m_new)
                p = jnp.exp(s - m_new)
                new += [m_new, a * l + jnp.sum(p, axis=0, keepdims=True)]
                alphas.append(a)
                pv.append(_dot(v_heads[h, :, pl.ds(ks, t)], p.astype(BF16)))
            acc = jnp.where(top, alphas[0], alphas[1]) * acc + pv[0] + pv[1]
            return (*new, acc, *ahead)

        row = jnp.full((1, t), NEG, F32)
        zero = jnp.zeros((1, t), F32)
        init = (row, zero, row, zero, jnp.zeros((LANES, t), F32), *first)
        m0, l0, m1, l1, acc, *following = step(i, lax.fori_loop(0, i, functools.partial(step, last=False), init), True)
        o_ref[pl.ds(qs, t), :] = jnp.transpose(acc * jnp.where(top, 1.0 / l0, 1.0 / l1)).astype(BF16)
        r = lax.broadcasted_iota(I32, (8, t), 0)
        lse_ref[:, pl.ds(qs, t)] = jnp.where(r == 0, m0 + jnp.log(l0), jnp.where(r == 1, m1 + jnp.log(l1), 0.0))
        return tuple(following)

    bias_spec = (pl.BlockSpec((2, nq, t, t), lambda b, p: (p, 0, 0, 0)) if dilated
                 else pl.BlockSpec((None, 8, LANES), lambda b, p: (0, 0, 0)))
    n = comm.n if comm else 0
    return pl.pallas_call(
        body, name=name, grid=(batch, N_PAIRS),
        out_shape=[jax.ShapeDtypeStruct((batch * seq, DIL_WIDTH), BF16), jax.ShapeDtypeStruct((batch * N_PAIRS, 8, seq), F32)]
        + (comm.out_shape if comm else []),
        in_specs=[pl.BlockSpec((seq, width), lambda b, p: (b, cq + p)),
                  pl.BlockSpec((seq, width), lambda b, p: (b, ck + p)),
                  pl.BlockSpec((seq, LANES), lambda b, p: (b, cv + p)),
                  bias_spec] + [ANY] * n,
        out_specs=[pl.BlockSpec((seq, LANES), lambda b, p: (b, p)),
                   pl.BlockSpec((None, 8, seq), lambda b, p: (b * N_PAIRS + p, 0, 0))] + [ANY] * n,
        scratch_shapes=[pltpu.VMEM((2, LANES, seq), BF16)] + (comm.scratch if comm else []),
        compiler_params=_cp("arbitrary", "arbitrary") if comm else _cp("parallel", "parallel"),
    )(q, k, v, bias, *comm_arrays)


def _attn_bwd(q, k, v, o, do, lse, bias, *, batch, seq, width, col0, dilated, scale, name, comm=None, comm_arrays=()):
    t = min(ATTN_TILE, seq)
    nq = seq // t
    cq, ck, cv = col0
    pre = scale if dilated else 1.0
    dq_transposed = width == LANES
    steps = batch * N_PAIRS

    def body(*refs):
        ins, (dq_ref, dk_ref, dv_ref), (dq_acc, dk_acc, dv_acc, rowdot, q_heads, do_heads), plan = _comm_hooks(comm, refs, 7, 3)
        q_ref, k_ref, v_ref, o_ref, do_ref, lse_ref, bias_ref = ins
        step_no = pl.program_id(0) * N_PAIRS + pl.program_id(1)
        if plan:
            pl.when(step_no == 0)(lambda: comm.start(*plan))
        wlane = [_head_lanes(width, h) for h in (0, 1)]
        vlane = [_head_lanes(LANES, h) for h in (0, 1)]
        causal = lax.broadcasted_iota(I32, (t, t), 0) <= lax.broadcasted_iota(I32, (t, t), 1)
        q_all = q_ref[...] * pre if dilated else q_ref[...]
        for h in (0, 1):
            q_heads[h] = jnp.where(wlane[h], q_all, jnp.zeros_like(q_all))
            do_heads[h] = jnp.where(vlane[h], do_ref[...], jnp.zeros_like(do_ref[...]))
        prod = jnp.transpose(do_ref[...].astype(F32) * o_ref[...].astype(F32))
        rowdot[0:1, :] = jnp.sum(prod[0:HEAD_V], axis=0, keepdims=True)
        rowdot[1:2, :] = jnp.sum(prod[HEAD_V:], axis=0, keepdims=True)
        dq_acc[...] = jnp.zeros_like(dq_acc)

        def k_tile(j, _):
            ks = pl.multiple_of(j * t, t)
            kj = k_ref[pl.ds(ks, t), :]
            vj = v_ref[pl.ds(ks, t), :]
            kh = [jnp.where(wlane[h], kj, jnp.zeros_like(kj)) for h in (0, 1)]
            if dq_transposed:
                kh = [jnp.transpose(kh[h].astype(F32)).astype(BF16) for h in (0, 1)]
            dk_acc[...] = jnp.zeros_like(dk_acc)
            dv_acc[...] = jnp.zeros_like(dv_acc)

            def operands(i):
                qs = pl.multiple_of(i * t, t)
                return [q_heads[h, pl.ds(qs, t), :] for h in (0, 1)], [do_heads[h, pl.ds(qs, t), :] for h in (0, 1)]

            def products(i):
                qih, doih = operands(i)
                scores = tuple(_dot_nt(kj, qih[h]) for h in (0, 1))
                return scores + tuple(_dot_nt(vj, doih[h]) for h in (0, 1)) if width > LANES else scores

            def q_tile(n, carry, last):
                i = nq - 1 - n
                ahead = () if last else products(i - 1)
                qs = pl.multiple_of(i * t, t)
                qih, doih = operands(i)
                s0, s1 = carry[:2]
                dps = carry[2:] if width > LANES else [_dot_nt(vj, doih[h]) for h in (0, 1)]
                dq_i = jnp.zeros((width, t) if dq_transposed else (t, width), F32)
                for h, (s, dp) in enumerate(((s0, dps[0]), (s1, dps[1]))):
                    if dilated:
                        s = s + bias_ref[h, i - j]
                    else:
                        s = s * scale
                        if last:
                            s = jnp.where(causal, s, NEG)
                    p = jnp.exp(s - lse_ref[h:h + 1, pl.ds(qs, t)])
                    ds = p * (dp - rowdot[h:h + 1, pl.ds(qs, t)])
                    ds = (ds if dilated else ds * scale).astype(BF16)
                    dv_acc[...] += _dot(p.astype(BF16), doih[h])
                    dk_acc[...] += _dot(ds, qih[h])
                    dq_i = dq_i + (_dot(kh[h], ds) if dq_transposed else _dot_tn(ds, kh[h]))
                if dq_transposed:
                    dq_acc[:, pl.ds(qs, t)] += dq_i
                else:
                    dq_acc[pl.ds(qs, t), :] += dq_i
                return ahead

            q_tile(nq - 1 - j, lax.fori_loop(0, nq - 1 - j, functools.partial(q_tile, last=False), products(nq - 1)), True)
            dk_ref[pl.ds(ks, t), :] = dk_acc[...].astype(BF16)
            dv_ref[pl.ds(ks, t), :] = dv_acc[...].astype(BF16)
            return 0

        lax.fori_loop(0, nq, k_tile, 0)
        dq_ref[...] = ((jnp.transpose(dq_acc[...]) if dq_transposed else dq_acc[...]) * pre).astype(BF16)
        if plan:
            pl.when(step_no == steps - 1)(lambda: comm.finish(*plan))

    tokens = batch * seq
    bias_spec = (pl.BlockSpec((2, nq, t, t), lambda b, p: (p, 0, 0, 0)) if dilated
                 else pl.BlockSpec((None, 8, LANES), lambda b, p: (0, 0, 0)))
    n = comm.n if comm else 0
    return pl.pallas_call(
        body, name=name, grid=(batch, N_PAIRS),
        out_shape=[jax.ShapeDtypeStruct((tokens, N_PAIRS * width), BF16), jax.ShapeDtypeStruct((tokens, N_PAIRS * width), BF16),
                   jax.ShapeDtypeStruct((tokens, DIL_WIDTH), BF16)] + (comm.out_shape if comm else []),
        in_specs=[pl.BlockSpec((seq, width), lambda b, p: (b, cq + p)),
                  pl.BlockSpec((seq, width), lambda b, p: (b, ck + p)),
                  pl.BlockSpec((seq, LANES), lambda b, p: (b, cv + p)),
                  pl.BlockSpec((seq, LANES), lambda b, p: (b, p)),
                  pl.BlockSpec((seq, LANES), lambda b, p: (b, p)),
                  pl.BlockSpec((None, 8, seq), lambda b, p: (b * N_PAIRS + p, 0, 0)),
                  bias_spec] + [ANY] * n,
        out_specs=[pl.BlockSpec((seq, width), lambda b, p: (b, p)),
                   pl.BlockSpec((seq, width), lambda b, p: (b, p)),
                   pl.BlockSpec((seq, LANES), lambda b, p: (b, p))] + [ANY] * n,
        scratch_shapes=[pltpu.VMEM((width, seq) if dq_transposed else (seq, width), F32),
                        pltpu.VMEM((t, width), F32), pltpu.VMEM((t, LANES), F32),
                        pltpu.VMEM((8, seq), F32), pltpu.VMEM((2, seq, width), BF16), pltpu.VMEM((2, seq, LANES), BF16)]
        + (comm.scratch if comm else []),
        compiler_params=_cp("arbitrary", "arbitrary") if comm else _cp("parallel", "parallel"),
    )(q, k, v, o, do, lse, bias, *comm_arrays)


def _rms(xf, g):
    r = lax.rsqrt(jnp.mean(xf * xf, axis=1, keepdims=True) + RMS_EPS)
    return xf * r * g, r


def _rms_bwd(dy, xf, r, g):
    gy = dy * g
    dx = r * gy - xf * (r * r * r) * jnp.mean(gy * xf, axis=1, keepdims=True)
    return dx, dy * xf * r


def _ln_bwd(dy, xhat, rstd, g):
    dxh = dy * g
    return rstd * (dxh - jnp.mean(dxh, axis=1, keepdims=True) - xhat * jnp.mean(dxh * xhat, axis=1, keepdims=True))


def _rope_slabs(q, cos, sin, transpose):
    first_half = (lax.broadcasted_iota(I32, (1, LANES), 1) % ROPE) < ROPE // 2
    out = []
    for p in range(N_PAIRS):
        blk = q[:, p * PAIR_W + LANES:(p + 1) * PAIR_W]
        y = blk * sin if transpose else blk
        up, down = pltpu.roll(y, LANES - ROPE // 2, 1), pltpu.roll(y, ROPE // 2, 1)
        rot = jnp.where(first_half, up, -down) if transpose else jnp.where(first_half, -up, down) * sin
        out += [q[:, p * PAIR_W:p * PAIR_W + LANES], blk * cos + rot]
    return jnp.concatenate(out, axis=1)


def _fwd_proj(x, w_in_ext, w1, wk, wv, g_q, g_kv, cext, sext, cs128, *, seq):
    tokens = x.shape[0]
    tm = min(MIX_TILE, seq)
    ns = seq // tm

    def body(x_ref, win_ref, w1_ref, wk_ref, wv_ref, gq_ref, gkv_ref, c_ref, s_ref, cs_ref,
             low_ref, gates_ref, qkvd_ref, qp_ref, kp_ref, vm_ref, qn_ref, kvn_ref, xb_ref):
        xt = x_ref[...].astype(BF16)
        xb_ref[...] = xt
        low = _dot(xt, win_ref[:, 0:LOW_W])
        low_ref[...] = low
        qkvd_ref[...] = _dot(xt, win_ref[:, LOW_W:LOW_W + 3 * DIL_WIDTH]).astype(BF16)
        gates_ref[...] = _dot(xt, win_ref[:, LOW_W + 3 * DIL_WIDTH:]).astype(BF16)
        qn = _rms(low[:, 0:Q_LORA], gq_ref[...])[0].astype(BF16)
        kvn = _rms(low[:, Q_LORA:Q_LORA + KV_LORA], gkv_ref[...])[0].astype(BF16)
        qn_ref[...] = qn
        kvn_ref[...] = kvn
        qp_ref[...] = _rope_slabs(_dot(qn, w1_ref[...]), c_ref[...], s_ref[...], False).astype(BF16)
        kr = low[:, Q_LORA + KV_LORA:] * cs_ref[...]
        kr = kr + pltpu.roll(kr, LANES - ROPE, 1)
        lane = lax.broadcasted_iota(I32, kr.shape, 1)
        kr = jnp.where(lane < ROPE, kr, 0.0)
        kr = (kr + pltpu.roll(kr, ROPE, 1)).astype(BF16)
        kn = _dot(kvn, wk_ref[...]).astype(BF16)
        kp_ref[...] = jnp.concatenate([blk for p in range(N_PAIRS) for blk in (kn[:, p * LANES:(p + 1) * LANES], kr)], axis=1)
        vm_ref[...] = _dot(kvn, wv_ref[...]).astype(BF16)

    n_gates = 2 * D_MODEL
    outs = [(LOW_W, F32), (n_gates, BF16), (3 * DIL_WIDTH, BF16), (N_PAIRS * PAIR_W, BF16), (N_PAIRS * PAIR_W, BF16),
            (DIL_WIDTH, BF16), (Q_LORA, BF16), (KV_LORA, BF16), (D_MODEL, BF16)]
    return pl.pallas_call(
        body, name="fwd_proj", grid=(tokens // tm,),
        out_shape=tuple(jax.ShapeDtypeStruct((tokens, w), dt) for w, dt in outs),
        in_specs=[_rows(tm, D_MODEL), _full(w_in_ext.shape), _full(w1.shape), _full(wk.shape),
                  _full(wv.shape), _full(g_q.shape), _full(g_kv.shape),
                  pl.BlockSpec((tm, LANES), lambda i: (i % ns, 1)),
                  pl.BlockSpec((tm, LANES), lambda i: (i % ns, 1)),
                  pl.BlockSpec((tm, LANES), lambda i: (i % ns, 0))],
        out_specs=tuple(_rows(tm, w) for w, _ in outs),
        compiler_params=_cp("parallel"),
    )(x, w_in_ext, w1, wk, wv, g_q, g_kv, cext, sext, cs128)


def _fwd_mix(o_a, o_b, gates, x, b_gate, w_oa, w_ob, w_out, ln_g, ln_b, *, seq):
    tokens = x.shape[0]
    tm = min(MIX_TILE, seq)

    def body(oa_ref, ob_ref, gt_ref, x_ref, bg_ref, woa_ref, wob_ref, wout_ref, g_ref, b_ref,
             hb_ref, xhat_ref, rstd_ref, ya_ref, yb_ref, mix_ref):
        ya = _dot(oa_ref[...], woa_ref[...])
        yb = _dot(ob_ref[...], wob_ref[...])
        g0 = _sigmoid(gt_ref[:, 0:D_MODEL].astype(F32) + bg_ref[0:1, :])
        g1 = _sigmoid(gt_ref[:, D_MODEL:].astype(F32) + bg_ref[1:2, :])
        mix = (g0 * ya + g1 * yb).astype(BF16)
        z = ALPHA * x_ref[...] + _dot(mix, wout_ref[...])
        zc = z - jnp.mean(z, axis=1, keepdims=True)
        rstd = lax.rsqrt(jnp.mean(zc * zc, axis=1, keepdims=True) + LN_EPS)
        xhat = zc * rstd
        hb_ref[...] = (xhat * g_ref[...] + b_ref[...]).astype(BF16)
        xhat_ref[...] = xhat
        rstd_ref[...] = jnp.broadcast_to(rstd, (tm, LANES))
        ya_ref[...] = ya.astype(BF16)
        yb_ref[...] = yb.astype(BF16)
        mix_ref[...] = mix

    outs = [(D_MODEL, BF16), (D_MODEL, F32), (LANES, F32), (D_MODEL, BF16), (D_MODEL, BF16), (D_MODEL, BF16)]
    return pl.pallas_call(
        body, name="fwd_mix", grid=(tokens // tm,),
        out_shape=tuple(jax.ShapeDtypeStruct((tokens, w), dt) for w, dt in outs),
        in_specs=[_rows(tm, DIL_WIDTH), _rows(tm, DIL_WIDTH), _rows(tm, 2 * D_MODEL), _rows(tm, D_MODEL),
                  _full(b_gate.shape), _full(w_oa.shape), _full(w_ob.shape), _full(w_out.shape),
                  _full(ln_g.shape), _full(ln_b.shape)],
        out_specs=tuple(_rows(tm, w) for w, _ in outs),
        compiler_params=_cp("parallel"),
    )(o_a, o_b, gates, x, b_gate, w_oa, w_ob, w_out, ln_g, ln_b)


def _fwd_mlp(hb, xhat1, target, w_ff1, w_ff2, ln1_g, ln1_b, ln_g, ln_b, *, seq):
    tokens = hb.shape[0]
    tm = min(2 * TOKEN_TILE, seq)
    tf = FF_SHARD
    nf = N_DEV // FF_STEP

    def body(hb_ref, xh_ref, tg_ref, w1_ref, w2_ref, g1_ref, b1_ref, g_ref, b_ref, u_ref, dz_ref, dzb_ref, stat_ref, acc):
        i, j = pl.program_id(0), pl.program_id(1)

        @pl.when((i == 0) & (j == 0))
        def _():
            stat_ref[...] = jnp.zeros_like(stat_ref)

        @pl.when(j == 0)
        def _():
            acc[...] = jnp.zeros_like(acc)

        acts = []
        for s in range(FF_STEP):
            u = _dot(hb_ref[...], w1_ref[s])
            u_ref[:, s * tf:(s + 1) * tf] = u.astype(BF16)
            acts.append(jnp.square(jnp.maximum(u, 0.0)).astype(BF16))
        acc[...] += _dot(jnp.concatenate(acts, axis=1), w2_ref[...])

        @pl.when(j == nf - 1)
        def _():
            z = ALPHA * (xh_ref[...] * g1_ref[...] + b1_ref[...]) + acc[...]
            zc = z - jnp.mean(z, axis=1, keepdims=True)
            rstd = lax.rsqrt(jnp.mean(zc * zc, axis=1, keepdims=True) + LN_EPS)
            xhat = zc * rstd
            err = xhat * g_ref[...] + b_ref[...] - tg_ref[...]
            dy = err * (1.0 / D_MODEL)
            dz = _ln_bwd(dy, xhat, rstd, g_ref[...])
            dz_ref[...] = dz
            dzb_ref[...] = dz.astype(BF16)
            stat_ref[0:1, :] += jnp.sum(dy * xhat, axis=0, keepdims=True)
            stat_ref[1:2, :] += jnp.sum(dy, axis=0, keepdims=True)
            stat_ref[2:3, :] += jnp.sum(jnp.sum(err * err, axis=1, keepdims=True), axis=0, keepdims=True) * (0.5 / D_MODEL)

    return pl.pallas_call(
        body, name="fwd_mlp", grid=(tokens // tm, nf),
        out_shape=(jax.ShapeDtypeStruct((tokens, D_FF), BF16), jax.ShapeDtypeStruct((tokens, D_MODEL), F32),
                   jax.ShapeDtypeStruct((tokens, D_MODEL), BF16), jax.ShapeDtypeStruct((8, D_MODEL), F32)),
        in_specs=[_rows(tm, D_MODEL), _rows(tm, D_MODEL), _rows(tm, D_MODEL),
                  pl.BlockSpec((FF_STEP, D_MODEL, tf), lambda i, j: (j, 0, 0)),
                  pl.BlockSpec((FF_STEP * tf, D_MODEL), lambda i, j: (j, 0)),
                  _full(ln1_g.shape), _full(ln1_b.shape), _full(ln_g.shape), _full(ln_b.shape)],
        out_specs=(pl.BlockSpec((tm, FF_STEP * tf), lambda i, j: (i, j)), _rows(tm, D_MODEL), _rows(tm, D_MODEL),
                   _full((8, D_MODEL))),
        scratch_shapes=[pltpu.VMEM((tm, D_MODEL), F32)],
        compiler_params=_cp("arbitrary", "arbitrary"),
    )(hb, xhat1, target, w_ff1, w_ff2, ln1_g, ln1_b, ln_g, ln_b)


def _bwd_mlp(dz2, dz2b, u, xhat1, rstd1, w_ff1, w_ff2, ln_g, *, seq):
    tokens = dz2.shape[0]
    tm = min(2 * TOKEN_TILE, seq)
    tf = FF_SHARD
    nf = N_DEV // FF_STEP

    def body(dz_ref, dzb_ref, u_ref, xh_ref, rs_ref, w1_ref, w2_ref, g_ref, du_ref, dz1_ref, dz1b_ref, stat_ref, acc):
        i, j = pl.program_id(0), pl.program_id(1)

        @pl.when((i == 0) & (j == 0))
        def _():
            stat_ref[...] = jnp.zeros_like(stat_ref)

        @pl.when(j == 0)
        def _():
            acc[...] = jnp.zeros_like(acc)

        da = _dot_nt(dzb_ref[...], w2_ref[...])
        du = (da * (2.0 * jnp.maximum(u_ref[...].astype(F32), 0.0))).astype(BF16)
        du_ref[...] = du
        part = _dot_nt(du[:, 0:tf], w1_ref[0])
        for s in range(1, FF_STEP):
            part = part + _dot_nt(du[:, s * tf:(s + 1) * tf], w1_ref[s])
        acc[...] += part

        @pl.when(j == nf - 1)
        def _():
            dh = ALPHA * dz_ref[...] + acc[...]
            xhat = xh_ref[...]
            dz1 = _ln_bwd(dh, xhat, rs_ref[:, 0:1], g_ref[...])
            dz1_ref[...] = dz1
            dz1b_ref[...] = dz1.astype(BF16)
            stat_ref[0:1, :] += jnp.sum(dh * xhat, axis=0, keepdims=True)
            stat_ref[1:2, :] += jnp.sum(dh, axis=0, keepdims=True)

    return pl.pallas_call(
        body, name="bwd_mlp", grid=(tokens // tm, nf),
        out_shape=(jax.ShapeDtypeStruct((tokens, D_FF), BF16), jax.ShapeDtypeStruct((tokens, D_MODEL), F32),
                   jax.ShapeDtypeStruct((tokens, D_MODEL), BF16), jax.ShapeDtypeStruct((8, D_MODEL), F32)),
        in_specs=[_rows(tm, D_MODEL), _rows(tm, D_MODEL), pl.BlockSpec((tm, FF_STEP * tf), lambda i, j: (i, j)),
                  _rows(tm, D_MODEL), _rows(tm, LANES),
                  pl.BlockSpec((FF_STEP, D_MODEL, tf), lambda i, j: (j, 0, 0)),
                  pl.BlockSpec((FF_STEP * tf, D_MODEL), lambda i, j: (j, 0)),
                  _full(ln_g.shape)],
        out_specs=(pl.BlockSpec((tm, FF_STEP * tf), lambda i, j: (i, j)), _rows(tm, D_MODEL), _rows(tm, D_MODEL),
                   _full((8, D_MODEL))),
        scratch_shapes=[pltpu.VMEM((tm, D_MODEL), F32)],
        compiler_params=_cp("arbitrary", "arbitrary"),
    )(dz2, dz2b, u, xhat1, rstd1, w_ff1, w_ff2, ln_g)


def _bwd_mix(dz1b, gates, y_a, y_b, b_gate, w_oa, w_ob, w_out, *, seq):
    tokens = dz1b.shape[0]
    tm = min(MIX_TILE, seq)

    def body(dz_ref, gt_ref, ya_ref, yb_ref, bg_ref, woa_ref, wob_ref, wout_ref,
             dgt_ref, dya_ref, dyb_ref, doa_ref, dob_ref, stat_ref):
        @pl.when(pl.program_id(0) == 0)
        def _():
            stat_ref[...] = jnp.zeros_like(stat_ref)

        dmix = _dot_nt(dz_ref[...], wout_ref[...])
        for k, (y_ref, w_ref, dy_ref, do_ref) in enumerate(((ya_ref, woa_ref, dya_ref, doa_ref), (yb_ref, wob_ref, dyb_ref, dob_ref))):
            g = _sigmoid(gt_ref[:, k * D_MODEL:(k + 1) * D_MODEL].astype(F32) + bg_ref[k:k + 1, :])
            dgate = dmix * y_ref[...].astype(F32) * g * (1.0 - g)
            dgt_ref[:, k * D_MODEL:(k + 1) * D_MODEL] = dgate.astype(BF16)
            stat_ref[k:k + 1, :] += jnp.sum(dgate, axis=0, keepdims=True)
            dy = (dmix * g).astype(BF16)
            dy_ref[...] = dy
            do_ref[...] = _dot_nt(dy, w_ref[...]).astype(BF16)

    outs = [(2 * D_MODEL, BF16), (D_MODEL, BF16), (D_MODEL, BF16), (DIL_WIDTH, BF16), (DIL_WIDTH, BF16)]
    return pl.pallas_call(
        body, name="bwd_mix", grid=(tokens // tm,),
        out_shape=tuple(jax.ShapeDtypeStruct((tokens, w), dt) for w, dt in outs) + (jax.ShapeDtypeStruct((8, D_MODEL), F32),),
        in_specs=[_rows(tm, D_MODEL), _rows(tm, 2 * D_MODEL), _rows(tm, D_MODEL), _rows(tm, D_MODEL),
                  _full(b_gate.shape), _full(w_oa.shape), _full(w_ob.shape), _full(w_out.shape)],
        out_specs=tuple(_rows(tm, w) for w, _ in outs) + (_full((8, D_MODEL)),),
        compiler_params=_cp("arbitrary"),
    )(dz1b, gates, y_a, y_b, b_gate, w_oa, w_ob, w_out)


def _bwd_proj(dqp, dkp, dvm, dq_d, dk_d, dv_d, dgates, dz1, low, w_in_ext, w1, wk, wv, g_q, g_kv, cext, sext, cs128, *, seq):
    tokens = dz1.shape[0]
    tm = min(TOKEN_TILE, seq)
    ns = seq // tm

    def body(dqp_ref, dkp_ref, dvm_ref, dqd_ref, dkd_ref, dvd_ref, dgt_ref, dz_ref, low_ref, win_ref, w1_ref, wk_ref,
             wv_ref, gq_ref, gkv_ref, c_ref, s_ref, cs_ref, dx_ref, dproj_ref, da_ref, dkn_ref, stat_ref):
        @pl.when(pl.program_id(0) == 0)
        def _():
            stat_ref[...] = jnp.zeros_like(stat_ref)

        low = low_ref[...]
        d_a = _rope_slabs(dqp_ref[...].astype(F32), c_ref[...], s_ref[...], True).astype(BF16)
        da_ref[...] = d_a
        q_a = low[:, 0:Q_LORA]
        _, rq = _rms(q_a, gq_ref[...])
        dq_a, gq_terms = _rms_bwd(_dot_nt(d_a, w1_ref[...]), q_a, rq, gq_ref[...])
        kv_a = low[:, Q_LORA:Q_LORA + KV_LORA]
        _, rkv = _rms(kv_a, gkv_ref[...])
        dkn = jnp.concatenate([dkp_ref[:, p * PAIR_W:p * PAIR_W + LANES] for p in range(N_PAIRS)], axis=1)
        dkn_ref[...] = dkn
        dkv_a, gkv_terms = _rms_bwd(_dot_nt(dkn, wk_ref[...]) + _dot_nt(dvm_ref[...], wv_ref[...]), kv_a, rkv, gkv_ref[...])
        dkr = sum(dkp_ref[:, p * PAIR_W + LANES:(p + 1) * PAIR_W].astype(F32) for p in range(N_PAIRS))
        dkr = dkr + pltpu.roll(dkr, LANES - ROPE, 1)
        dkr = jnp.where(lax.broadcasted_iota(I32, dkr.shape, 1) < ROPE, dkr, 0.0)
        dkr = (dkr + pltpu.roll(dkr, ROPE, 1)) * cs_ref[...]
        stat_ref[0:1, 0:Q_LORA] += jnp.sum(gq_terms, axis=0, keepdims=True)
        stat_ref[1:2, 0:KV_LORA] += jnp.sum(gkv_terms, axis=0, keepdims=True)
        dproj_ref[:, 0:Q_LORA] = dq_a.astype(BF16)
        dproj_ref[:, Q_LORA:Q_LORA + KV_LORA] = dkv_a.astype(BF16)
        dproj_ref[:, Q_LORA + KV_LORA:LOW_W] = dkr.astype(BF16)
        dproj_ref[:, LOW_W:LOW_W + DIL_WIDTH] = dqd_ref[...]
        dproj_ref[:, LOW_W + DIL_WIDTH:LOW_W + 2 * DIL_WIDTH] = dkd_ref[...]
        dproj_ref[:, LOW_W + 2 * DIL_WIDTH:LOW_W + 3 * DIL_WIDTH] = dvd_ref[...]
        dproj_ref[:, LOW_W + 3 * DIL_WIDTH:] = dgt_ref[...]
        dx_ref[...] = ALPHA * dz_ref[...] + _dot_nt(dproj_ref[...], win_ref[...])

    wide = N_PAIRS * PAIR_W
    return pl.pallas_call(
        body, name="bwd_proj", grid=(tokens // tm,),
        out_shape=(jax.ShapeDtypeStruct((tokens, D_MODEL), F32), jax.ShapeDtypeStruct((tokens, IN_EXT), BF16),
                   jax.ShapeDtypeStruct((tokens, wide), BF16), jax.ShapeDtypeStruct((tokens, N_HEADS * NOPE), BF16),
                   jax.ShapeDtypeStruct((8, D_MODEL), F32)),
        in_specs=[_rows(tm, wide), _rows(tm, wide), _rows(tm, DIL_WIDTH), _rows(tm, DIL_WIDTH), _rows(tm, DIL_WIDTH),
                  _rows(tm, DIL_WIDTH), _rows(tm, 2 * D_MODEL),
                  _rows(tm, D_MODEL), _rows(tm, LOW_W), _full(w_in_ext.shape), _full(w1.shape),
                  _full(wk.shape), _full(wv.shape), _full(g_q.shape), _full(g_kv.shape),
                  pl.BlockSpec((tm, LANES), lambda i: (i % ns, 1)), pl.BlockSpec((tm, LANES), lambda i: (i % ns, 1)),
                  pl.BlockSpec((tm, LANES), lambda i: (i % ns, 0))],
        out_specs=(_rows(tm, D_MODEL), _rows(tm, IN_EXT), _rows(tm, wide), _rows(tm, N_HEADS * NOPE), _full((8, D_MODEL))),
        compiler_params=_cp("arbitrary"),
    )(dqp, dkp, dvm, dq_d, dk_d, dv_d, dgates, dz1, low, w_in_ext, w1, wk, wv, g_q, g_kv, cext, sext, cs128)


def _wgrad(a, b, name, square_relu=False, by_shard=False):
    tokens, ka = a.shape
    n = b.shape[1]
    if ka <= 512 or ka % 512 == 0:
        tka = min(ka, 512)
    else:
        tka = max(w for w in range(LANES, min(ka, 2304) + 1, LANES) if ka % w == 0)
    shard = n // N_DEV
    tn = WGRAD_SHARDS * shard if by_shard else max(w for w in range(LANES, min(n, 2304) + 1, LANES) if n % w == 0)
    tt = min(tokens, 2048 if tka <= 512 else 1024)
    nt = tokens // tt

    def body(a_ref, b_ref, o_ref, acc):
        kt = pl.program_id(2)

        @pl.when(kt == 0)
        def _():
            acc[...] = jnp.zeros_like(acc)

        at = a_ref[...]
        if square_relu:
            at = jnp.square(jnp.maximum(at.astype(F32), 0.0)).astype(BF16)
        acc[...] += _dot_tn(at, b_ref[...])

        @pl.when(kt == nt - 1)
        def _():
            if by_shard:
                for s in range(WGRAD_SHARDS):
                    o_ref[s] = acc[:, s * shard:(s + 1) * shard].astype(BF16)
            else:
                o_ref[...] = acc[...].astype(BF16)

    if by_shard:
        out_shape, out_spec = (N_DEV, ka, shard), pl.BlockSpec((WGRAD_SHARDS, tka, shard), lambda i, j, k: (j, i, 0))
    else:
        out_shape, out_spec = (ka, n), pl.BlockSpec((tka, tn), lambda i, j, k: (i, j))
    return pl.pallas_call(
        body, name=name, grid=(ka // tka, n // tn, nt), out_shape=jax.ShapeDtypeStruct(out_shape, BF16),
        in_specs=[pl.BlockSpec((tt, tka), lambda i, j, k: (k, i)), pl.BlockSpec((tt, tn), lambda i, j, k: (k, j))],
        out_specs=out_spec,
        scratch_shapes=[pltpu.VMEM((tka, tn), F32)],
        compiler_params=_cp("parallel", "parallel", "arbitrary"),
    )(a, b)


def _adam_math(w, g, m, v):
    m = ADAM_B1 * m + (1.0 - ADAM_B1) * g
    v = ADAM_B2 * v + (1.0 - ADAM_B2) * jnp.square(g)
    m_hat = m / (1.0 - ADAM_B1 ** ADAM_STEP)
    v_hat = v / (1.0 - ADAM_B2 ** ADAM_STEP)
    return -ADAM_LR * (m_hat / (jnp.sqrt(v_hat) + ADAM_EPS) + ADAM_WD * w), m, v


def _adamw(items, name):
    steps = min(_tiles(*w.shape)[0] for w, *_ in items)
    n_items = len(items)

    def body(slot_ref, *refs):
        ins, outs = refs[:5 * n_items], refs[5 * n_items:]
        for k, (_, _, _, _, parts) in enumerate(items):
            w_ref, m_ref, v_ref, own_ref, p_ref = ins[5 * k:5 * k + 5]
            g_ref, d_ref, nm_ref, nv_ref = outs[4 * k:4 * k + 4]
            g = own_ref[...].astype(F32)
            for d in range(parts.shape[0]):
                g = g + p_ref[d].astype(F32)
            g_ref[...] = g
            d_ref[...], nm_ref[...], nv_ref[...] = _adam_math(w_ref[...], g, m_ref[...], v_ref[...])

    x, y, c = _place()
    in_specs, out_specs, out_shape, args = [], [], [], []
    for w, m, v, own, parts in items:
        rows, cols = w.shape
        _, tile, at = _tiles(rows, cols, steps)
        blk = pl.BlockSpec(tile, lambda i, slot, at=at: at(i))
        own_blk = blk if own.ndim == 2 else pl.BlockSpec((None, *tile), lambda i, slot, at=at: (slot[0], *at(i)))
        in_specs += [blk, blk, blk, own_blk, pl.BlockSpec((parts.shape[0], *tile), lambda i, slot, at=at: (0, *at(i)))]
        out_specs += [blk] * 4
        out_shape += [jax.ShapeDtypeStruct((rows, cols), F32)] * 4
        args += [w, m, v, own, parts]
    out = pl.pallas_call(
        body, name=name,
        grid_spec=pltpu.PrefetchScalarGridSpec(num_scalar_prefetch=1, grid=(steps,), in_specs=in_specs, out_specs=out_specs),
        out_shape=out_shape, compiler_params=_cp("parallel"),
    )(jnp.reshape(4 * x + 2 * y + c, (1,)).astype(I32), *args)
    return [tuple(out[4 * k:4 * k + 4]) for k in range(n_items)]


def _adamw_small(parts, w, m, v):
    _, rows, cols = parts.shape

    def body(p_ref, w_ref, m_ref, v_ref, g_ref, d_ref, nm_ref, nv_ref):
        g = p_ref[0]
        for d in range(1, N_DEV):
            g = g + p_ref[d]
        g_ref[...] = g
        d_ref[...], nm_ref[...], nv_ref[...] = _adam_math(w_ref[...], g, m_ref[...], v_ref[...])

    return pl.pallas_call(
        body, name="adamw_replicated", out_shape=(jax.ShapeDtypeStruct((rows, cols), F32),) * 4,
        in_specs=[_full(parts.shape)] + [_full((rows, cols))] * 3, out_specs=(_full((rows, cols)),) * 4, grid=(1,),
        compiler_params=_cp("arbitrary"),
    )(parts, w, m, v)


def _pad_rows(a2d, mult):
    pad = (-a2d.shape[-2]) % mult
    return jnp.pad(a2d, [(0, 0)] * (a2d.ndim - 2) + [(0, pad), (0, 0)]) if pad else a2d


def _pad_cols(a):
    pad = (-a.shape[-1]) % LANES
    return jnp.pad(a, [(0, 0)] * (a.ndim - 1) + [(0, pad)]) if pad else a


def _rot_cols(w):
    half = ROPE // 2
    return jnp.concatenate([-w[..., half:], w[..., :half]], axis=-1)


def _unrot_cols(dw):
    half = ROPE // 2
    return jnp.concatenate([dw[..., half:], -dw[..., :half]], axis=-1)


def _from_col_shards(stacked):
    return stacked.transpose(1, 0, 2).reshape(stacked.shape[1], -1)


def _to_col_shards(full):
    r = full.shape[0]
    return full.reshape(r, N_DEV, -1).transpose(1, 0, 2)


def _rope_tables(seq):
    half = ROPE // 2
    inv = jnp.power(ROPE_THETA, -jnp.arange(half, dtype=F32) / half)
    ang = jnp.arange(seq, dtype=F32)[:, None] * inv[None, :]
    cos = jnp.concatenate([jnp.cos(ang)] * 2, axis=1)
    sin = jnp.concatenate([jnp.sin(ang)] * 2, axis=1)
    ones, zeros = jnp.ones((seq, 2 * NOPE), F32), jnp.zeros((seq, 2 * NOPE), F32)
    pad = jnp.zeros((seq, PAIR_W - 2 * NOPE - 2 * ROPE), F32)
    cext = jnp.concatenate([ones, cos, cos, pad], axis=1)
    sext = jnp.concatenate([zeros, sin, sin, pad], axis=1)
    cs128 = jnp.concatenate([cos, sin, jnp.zeros((seq, LANES - 2 * ROPE), F32)], axis=1)
    return cext, sext, cs128


def _pair_slabs(nope, rope):
    k = nope.shape[0]
    nope = nope.reshape(k, N_PAIRS, 2 * NOPE)
    rope = jnp.zeros((k, N_PAIRS, 2 * ROPE), nope.dtype) if rope is None else rope.reshape(k, N_PAIRS, 2 * ROPE)
    pad = jnp.zeros((k, N_PAIRS, PAIR_W - 2 * NOPE - 2 * ROPE), nope.dtype)
    return jnp.concatenate([nope, rope, pad], axis=2).reshape(k, N_PAIRS * PAIR_W)


def _split_slabs(slabs):
    k = slabs.shape[0]
    s = slabs.reshape(k, N_PAIRS, PAIR_W)
    return s[:, :, :2 * NOPE].reshape(k, N_HEADS, NOPE), s[:, :, 2 * NOPE:2 * NOPE + 2 * ROPE].reshape(k, N_HEADS, ROPE)


def kernel(x, w_in, b_gate, g_q_a, w_uq, g_kv_a, w_ukv, w_o_mla, w_o_dil, w_out, ln1_g, ln1_b, w_ff1, w_ff2, ln2_g, ln2_b, loss_target, m_w_in, m_b_gate, m_g_q_a, m_w_uq, m_g_kv_a, m_w_ukv, m_w_o_mla, m_w_o_dil, m_w_out, m_ln1_g, m_ln1_b, m_w_ff1, m_w_ff2, m_ln2_g, m_ln2_b, v_w_in, v_b_gate, v_g_q_a, v_w_uq, v_g_kv_a, v_w_ukv, v_w_o_mla, v_w_o_dil, v_w_out, v_ln1_g, v_ln1_b, v_w_ff1, v_w_ff2, v_ln2_g, v_ln2_b):
    batch, seq, _ = x.shape
    tokens = batch * seq
    weights = dict(w_in=w_in, w_uq=w_uq, w_ukv=w_ukv, w_o_mla=w_o_mla, w_o_dil=w_o_dil, w_out=w_out, w_ff1=w_ff1, w_ff2=w_ff2, b_gate=b_gate)
    mom_m = dict(w_in=m_w_in, w_uq=m_w_uq, w_ukv=m_w_ukv, w_o_mla=m_w_o_mla, w_o_dil=m_w_o_dil, w_out=m_w_out, w_ff1=m_w_ff1, w_ff2=m_w_ff2, b_gate=m_b_gate)
    mom_v = dict(w_in=v_w_in, w_uq=v_w_uq, w_ukv=v_w_ukv, w_o_mla=v_w_o_mla, w_o_dil=v_w_o_dil, w_out=v_w_out, w_ff1=v_w_ff1, w_ff2=v_w_ff2, b_gate=v_b_gate)

    first = ["w_in", "w_uq", "w_ukv"]
    widths = [weights[n].shape[2] for n in first]
    shards = [weights["w_in"][0].T.astype(BF16)] + [_pad_cols(weights[n][0].astype(BF16)) for n in first[1:]]
    g_in, g_uq, g_ukv = _run_comm(_Gather(shards), shards, "all_gather_first_weights")
    g_uq, g_ukv = g_uq[:, :, :widths[1]], g_ukv[:, :, :widths[2]]

    s1, s2, n_in = Q_LORA + KV_LORA, Q_LORA + KV_LORA + ROPE, N_DEV * widths[0]

    def w_in_cols(lo, hi):
        out = []
        while lo < hi:
            d, off = divmod(lo, widths[0])
            take = min(hi - lo, widths[0] - off)
            out.append(g_in[d][off:off + take].T)
            lo += take
        return out

    w_in_ext = jnp.concatenate(w_in_cols(0, s2) + [_rot_cols(jnp.concatenate(w_in_cols(s1, s2), axis=1)),
                                                   jnp.zeros((D_MODEL, LOW_W - s2 - ROPE), BF16)] + w_in_cols(s2, n_in), axis=1)
    uq = _from_col_shards(g_uq).reshape(Q_LORA, N_HEADS, NOPE + ROPE)
    w1 = _pair_slabs(uq[:, :, :NOPE], uq[:, :, NOPE:])
    ukv = _from_col_shards(g_ukv).reshape(KV_LORA, N_HEADS, NOPE + HEAD_V)
    wk = ukv[:, :, :NOPE].reshape(KV_LORA, N_HEADS * NOPE)
    wv = ukv[:, :, NOPE:].reshape(KV_LORA, N_HEADS * HEAD_V)
    cext, sext, cs128 = _rope_tables(seq)
    dil_bias = _dilated_bias_table(seq)
    no_bias = jnp.zeros((1, 8, LANES), F32)

    x2 = x.reshape(tokens, D_MODEL)
    low, gates, qkvd, qp, kp, vm, qn, kvn, xb = _fwd_proj(x2, w_in_ext, w1, wk, wv, g_q_a, g_kv_a, cext, sext, cs128, seq=seq)
    bg = b_gate[0]
    bg_hi = bg.astype(BF16)
    bg_lo = (bg - bg_hi.astype(F32)).astype(BF16)
    later = [weights[n][0].astype(BF16) for n in ("w_o_mla", "w_o_dil", "w_out", "w_ff1", "w_ff2")]
    later.append(_pad_rows(jnp.concatenate([bg_hi, bg_lo], axis=0), 16))
    mla = dict(batch=batch, seq=seq, width=PAIR_W, col0=(0, 0, 0), dilated=False, scale=MLA_SCALE)
    dil = dict(batch=batch, seq=seq, width=LANES, col0=(0, N_PAIRS, 2 * N_PAIRS), dilated=True, scale=DIL_SCALE)
    o_a, lse_a, g_oa, g_ob, g_out, g_ff1, g_ff2, g_bg = _attn_fwd(
        qp, kp, vm, no_bias, name="mla_attention_fwd", comm=_Gather(later), comm_arrays=later, **mla)
    o_b, lse_b = _attn_fwd(qkvd, qkvd, qkvd, dil_bias, name="dilated_attention_fwd", **dil)
    w_oa, w_ob = _from_col_shards(g_oa), _from_col_shards(g_ob)
    w_out_full = g_out.reshape(D_MODEL, D_MODEL)
    w_ff2_full = g_ff2.reshape(D_FF, D_MODEL)
    bg_parts = g_bg.astype(F32)
    b_gate_full = _from_col_shards(bg_parts[:, 0:2] + bg_parts[:, 2:4])
    hb, xhat1, rstd1, y_a, y_b, mix = _fwd_mix(o_a, o_b, gates, x2, b_gate_full, w_oa, w_ob, w_out_full, ln1_g, ln1_b, seq=seq)
    u, dz2, dz2b, stat2 = _fwd_mlp(hb, xhat1, loss_target.reshape(tokens, D_MODEL), g_ff1, w_ff2_full, ln1_g, ln1_b, ln2_g, ln2_b, seq=seq)

    du, dz1, dz1b, stat1 = _bwd_mlp(dz2, dz2b, u, xhat1, rstd1, g_ff1, w_ff2_full, ln1_g, seq=seq)
    dw_ff = [_wgrad(hb, du, "wgrad_ff1", by_shard=True),
             _wgrad(u, dz2b, "wgrad_ff2", square_relu=True).reshape(N_DEV, FF_SHARD, D_MODEL)]
    dgates, dy_a, dy_b, do_a, do_b, stat_g = _bwd_mix(dz1b, gates, y_a, y_b, b_gate_full, w_oa, w_ob, w_out_full, seq=seq)
    dqp, dkp, dvm, r_ff1, r_ff2 = _attn_bwd(qp, kp, vm, o_a, do_a, lse_a, no_bias, name="mla_attention_bwd",
                                            comm=_Scatter(dw_ff), comm_arrays=dw_ff, **mla)
    dw_mid = [_to_col_shards(_wgrad(o_a, dy_a, "wgrad_o_mla")), _to_col_shards(_wgrad(o_b, dy_b, "wgrad_o_dil")),
              _wgrad(mix, dz1b, "wgrad_out").reshape(N_DEV, D_MODEL // N_DEV, D_MODEL),
              _pad_rows(_to_col_shards(stat_g[0:2]).astype(BF16), 16)]
    dq_d, dk_d, dv_d, r_oa, r_ob, r_out, r_bg = _attn_bwd(qkvd, qkvd, qkvd, o_b, do_b, lse_b, dil_bias, name="dilated_attention_bwd",
                                                          comm=_Scatter(dw_mid), comm_arrays=dw_mid, **dil)
    grad_x, dproj, d_a, dkn, stat_r = _bwd_proj(dqp, dkp, dvm, dq_d, dk_d, dv_d, dgates, dz1, low, w_in_ext, w1, wk, wv,
                                                g_q_a, g_kv_a, cext, sext, cs128, seq=seq)

    dw_in_ext = _wgrad(dproj, xb, "wgrad_in")
    dw1 = _wgrad(qn, d_a, "wgrad_uq")
    dwk = _wgrad(kvn, dkn, "wgrad_ukv_k")
    dwv = _wgrad(kvn, dvm, "wgrad_ukv_v")
    dw_kr = dw_in_ext[s1:s2] + _unrot_cols(dw_in_ext[s2:s2 + ROPE].T).T

    def dw_in_cols(lo, hi):
        out = []
        for a, b, piece in ((0, s1, lambda u, v: dw_in_ext[u:v]), (s1, s2, lambda u, v: dw_kr[u - s1:v - s1]),
                            (s2, n_in, lambda u, v: dw_in_ext[u + LOW_W - s2:v + LOW_W - s2])):
            if max(lo, a) < min(hi, b):
                out.append(piece(max(lo, a), min(hi, b)))
        return out

    dw_in = jnp.stack([jnp.concatenate(dw_in_cols(d * widths[0], (d + 1) * widths[0]), axis=0) for d in range(N_DEV)])
    n1, r1 = _split_slabs(dw1)
    dw_uq = jnp.concatenate([n1, r1], axis=2).reshape(Q_LORA, N_HEADS * (NOPE + ROPE))
    dw_ukv = jnp.concatenate([dwk.reshape(KV_LORA, N_HEADS, NOPE), dwv.reshape(KV_LORA, N_HEADS, HEAD_V)], axis=2).reshape(KV_LORA, N_HEADS * (NOPE + HEAD_V))
    last = [dw_in] + [_pad_cols(_to_col_shards(dw)) for dw in (dw_uq, dw_ukv)]
    theirs = _rs_sibling(last, "rs_last_sibling_exchange")
    sums = [_pair_sum(a, b, "rs_last_pair_sum_" + n) for a, b, n in zip(last, theirs, first)]
    partial = jnp.concatenate([stat_r[0:1, :Q_LORA], stat_r[1:2, :KV_LORA], stat1[0:1], stat1[1:2], stat2[0:1], stat2[1:2],
                               stat2[2:3, :LANES]], axis=1)
    partial = _pad_rows(partial.reshape(-1, LANES), 8)
    rest = [s[1] for s in sums]
    got_in, got_uq, got_ukv, every = _run_comm(_Plans([_ChipExchange(rest), _Gather([partial])]), rest + [partial],
                                               "rs_last_chip_exchange")

    upd = {}
    early = ["w_ff1", "w_ff2", "w_out", "w_o_mla", "w_o_dil"]
    items = [(weights[n][0], mom_m[n][0], mom_v[n][0], own, parts) for n, own, parts in
             zip(early, (dw_ff[0], dw_ff[1], dw_mid[2], dw_mid[0], dw_mid[1]), (r_ff1, r_ff2, r_out, r_oa, r_ob))]
    upd.update(zip(early, _adamw(items, "adamw_early_weights")))
    (in_t,) = _adamw([(weights["w_in"][0].T, mom_m["w_in"][0].T, mom_v["w_in"][0].T, sums[0][0], got_in)], "adamw_w_in")
    upd["w_in"] = tuple(a.T for a in in_t)
    for n, w, (own, _), parts in zip(first[1:], widths[1:], sums[1:], (got_uq, got_ukv)):
        (upd[n],) = _adamw([(weights[n][0], mom_m[n][0], mom_v[n][0], own[:, :w], parts[:, :, :w])], "adamw_" + n)
    (bg_upd,) = _adamw([(_pad_rows(b_gate[0], 16), _pad_rows(m_b_gate[0], 16), _pad_rows(v_b_gate[0], 16), dw_mid[3], r_bg)],
                       "adamw_b_gate")
    upd["b_gate"] = tuple(t[0:2] for t in bg_upd)

    small_w = [g_q_a, g_kv_a, ln1_g, ln1_b, ln2_g, ln2_b]
    small_m = [m_g_q_a, m_g_kv_a, m_ln1_g, m_ln1_b, m_ln2_g, m_ln2_b]
    small_v = [v_g_q_a, v_g_kv_a, v_ln1_g, v_ln1_b, v_ln2_g, v_ln2_b]
    small_widths = [a.shape[1] for a in small_w]

    def as_rows(vecs, extra):
        flat = jnp.concatenate(vecs + [jnp.zeros((1, extra), F32)], axis=1)
        return _pad_rows(flat.reshape(-1, LANES), 8)

    g_s, d_s, nm_s, nv_s = _adamw_small(every, as_rows(small_w, LANES), as_rows(small_m, LANES), as_rows(small_v, LANES))

    def split_small(a):
        flat = a.reshape(1, -1)
        out, c0 = [], 0
        for w in small_widths:
            out.append(flat[:, c0:c0 + w])
            c0 += w
        return out, flat[0, c0]

    g_small, loss = split_small(g_s)
    small = [g_small, split_small(d_s)[0], split_small(nm_s)[0], split_small(nv_s)[0]]

    order = ["w_in", "b_gate", "g_q_a", "w_uq", "g_kv_a", "w_ukv", "w_o_mla", "w_o_dil", "w_out", "ln1_g", "ln1_b", "w_ff1", "w_ff2", "ln2_g", "ln2_b"]
    small_names = ["g_q_a", "g_kv_a", "ln1_g", "ln1_b", "ln2_g", "ln2_b"]

    def pick(kind):
        return [small[kind][small_names.index(n)] if n in small_names else upd[n][kind][None] for n in order]

    return (loss, grad_x.reshape(batch, seq, D_MODEL), *pick(0), *pick(1), *pick(2), *pick(3))
```

```python
import functools
import math

import jax
import jax.numpy as jnp
from jax import lax
from jax.experimental import pallas as pl
from jax.experimental.pallas import tpu as pltpu

F32 = jnp.float32
BF16 = jnp.bfloat16
I32 = jnp.int32

D_MODEL = 1024
N_HEADS = 8
NOPE = 64
ROPE = 32
HEAD_V = 64
Q_LORA = 384
KV_LORA = 256
DIL_WIDTH = 512
D_FF = 4096
ROPE_THETA = 10000.0
LN_EPS = 1e-5
RMS_EPS = 1e-6
NEG = -1e30
ALPHA = 2.0 ** 0.25
MLA_SCALE = (NOPE + ROPE) ** -0.5
DIL_SCALE = 64 ** -0.5
ADAM_LR, ADAM_B1, ADAM_B2, ADAM_EPS, ADAM_WD, ADAM_STEP = 0.001, 0.9, 0.999, 1e-08, 0.01, 10

LANES = 128
PAIR_W = 256
N_PAIRS = N_HEADS // 2
LOW_W = 768
IN_EXT = LOW_W + 3 * DIL_WIDTH + 2 * D_MODEL
N_DEV = 8
FF_SHARD = D_FF // N_DEV
FF_STEP = 4
WGRAD_SHARDS = 4
WGRAD_BLOCK_BYTES = 24 << 20
TOKEN_TILE = 256
MIX_TILE = 512
ATTN_TILE = 256
VMEM_LIMIT = 56 << 20

MESH = pl.DeviceIdType.MESH
ANY = pl.BlockSpec(memory_space=pl.ANY)
CHIP_FLIPS = ((0, 0), (0, 1), (1, 0), (1, 1))
PEER_FLIPS = tuple((fx, fy, fc) for fx in (0, 1) for fy in (0, 1) for fc in (0, 1))[1:]


def _cp(*sem):
    return pltpu.CompilerParams(dimension_semantics=sem or None, vmem_limit_bytes=VMEM_LIMIT)


def _full(shape):
    nd = len(shape)
    return pl.BlockSpec(shape, lambda *_: (0,) * nd)


def _rows(tm, width):
    return pl.BlockSpec((tm, width), lambda i, *_: (i, 0))


def _dot(a, b):
    return jnp.dot(a, b, preferred_element_type=F32)


def _dot_nt(a, b):
    return lax.dot_general(a, b, (((1,), (1,)), ((), ())), preferred_element_type=F32)


def _dot_tn(a, b):
    return lax.dot_general(a, b, (((0,), (0,)), ((), ())), preferred_element_type=F32)


def _sigmoid(z):
    return 1.0 / (1.0 + jnp.exp(-z))


def _place():
    return lax.axis_index("x"), lax.axis_index("y"), lax.axis_index("c")


def _flip(v, f):
    return 1 - v if f else v


class _Gather:
    def __init__(self, shards):
        self.n = len(shards)
        self.out_shape = [jax.ShapeDtypeStruct((N_DEV, *s.shape), s.dtype) for s in shards]
        self.scratch = [pltpu.SemaphoreType.DMA((7 * self.n,)), pltpu.SemaphoreType.DMA((7 * self.n,)),
                        pltpu.SemaphoreType.DMA((self.n,))]

    def _copies(self, what, srcs, dsts, send, recv, local):
        x, y, c = _place()
        chips = [(_flip(x, fx), _flip(y, fy)) for fx, fy in CHIP_FLIPS[1:]]
        out = []
        for a in range(self.n):
            def slot(px, py, pc, a=a):
                return dsts[a].at[4 * px + 2 * py + pc]

            def copy(k, block, to, src=None, a=a, slot=slot):
                return pltpu.make_async_remote_copy(
                    src_ref=slot(*block) if src is None else src, dst_ref=slot(*block),
                    send_sem=send.at[7 * a + k], recv_sem=recv.at[7 * a + k], device_id=to, device_id_type=MESH)

            if what == "mine":
                out.append(pltpu.make_async_copy(srcs[a], slot(x, y, c), local.at[a]))
            elif what == "first":
                out.append(copy(0, (x, y, c), (x, y, 1 - c), src=srcs[a]))
                out += [copy(1 + j, (x, y, c), (*chip, c), src=srcs[a]) for j, chip in enumerate(chips)]
            elif what == "landed":
                out += [copy(1 + j, (*chip, c), (x, y, c)) for j, chip in enumerate(chips)]
            elif what == "passed":
                out += [copy(4 + j, (*chip, c), (x, y, 1 - c)) for j, chip in enumerate(chips)]
            else:
                out.append(copy(0, (x, y, 1 - c), (x, y, c)))
                out += [copy(4 + j, (*chip, 1 - c), (x, y, c)) for j, chip in enumerate(chips)]
        return out

    def start(self, *refs):
        for cp in self._copies("first", *refs) + self._copies("mine", *refs):
            cp.start()

    def forward(self, *refs):
        for landed, passed in zip(self._copies("landed", *refs), self._copies("passed", *refs)):
            landed.wait_recv()
            passed.start()

    def finish(self, *refs):
        for cp in self._copies("from_sibling", *refs):
            cp.wait_recv()
        for cp in self._copies("first", *refs) + self._copies("passed", *refs):
            cp.wait_send()
        for cp in self._copies("mine", *refs):
            cp.wait()


class _Scatter:
    def __init__(self, arrays):
        self.n = len(arrays)
        self.out_shape = [jax.ShapeDtypeStruct((7, *a.shape[1:]), a.dtype) for a in arrays]
        self.scratch = [pltpu.SemaphoreType.DMA((7 * self.n,)), pltpu.SemaphoreType.DMA((7 * self.n,))]

    def _copies(self, srcs, dsts, send, recv):
        x, y, c = _place()
        out = []
        for a in range(self.n):
            for k, (fx, fy, fc) in enumerate(PEER_FLIPS):
                px, py, pc = _flip(x, fx), _flip(y, fy), _flip(c, fc)
                out.append(pltpu.make_async_remote_copy(
                    src_ref=srcs[a].at[4 * px + 2 * py + pc], dst_ref=dsts[a].at[k],
                    send_sem=send.at[7 * a + k], recv_sem=recv.at[7 * a + k], device_id=(px, py, pc), device_id_type=MESH))
        return out

    def start(self, *refs):
        for cp in self._copies(*refs):
            cp.start()

    def forward(self, *refs):
        pass

    def finish(self, *refs):
        for cp in self._copies(*refs):
            cp.wait_send()
        for cp in self._copies(*refs):
            cp.wait_recv()


class _ChipExchange:
    def __init__(self, arrays):
        self.n = len(arrays)
        self.out_shape = [jax.ShapeDtypeStruct(a.shape, a.dtype) for a in arrays]
        self.scratch = [pltpu.SemaphoreType.DMA((3 * self.n,)), pltpu.SemaphoreType.DMA((3 * self.n,))]

    def _copies(self, srcs, dsts, send, recv):
        x, y, c = _place()
        return [pltpu.make_async_remote_copy(
            src_ref=srcs[a].at[k], dst_ref=dsts[a].at[k], send_sem=send.at[3 * a + k], recv_sem=recv.at[3 * a + k],
            device_id=(_flip(x, fx), _flip(y, fy), c), device_id_type=MESH)
            for a in range(self.n) for k, (fx, fy) in enumerate(CHIP_FLIPS[1:])]

    def start(self, *refs):
        for cp in self._copies(*refs):
            cp.start()

    def forward(self, *refs):
        pass

    def finish(self, *refs):
        for cp in self._copies(*refs):
            cp.wait_send()
        for cp in self._copies(*refs):
            cp.wait_recv()


class _Plans:
    def __init__(self, plans):
        self.plans = plans
        self.n = sum(p.n for p in plans)
        self.out_shape = [s for p in plans for s in p.out_shape]
        self.scratch = [s for p in plans for s in p.scratch]

    def _each(self, phase, srcs, dsts, *sems):
        i0 = s0 = 0
        for p in self.plans:
            getattr(p, phase)(srcs[i0:i0 + p.n], dsts[i0:i0 + p.n], *sems[s0:s0 + len(p.scratch)])
            i0, s0 = i0 + p.n, s0 + len(p.scratch)

    def start(self, *refs):
        self._each("start", *refs)

    def forward(self, *refs):
        self._each("forward", *refs)

    def finish(self, *refs):
        self._each("finish", *refs)


def _run_comm(comm, arrays, name):
    n = comm.n

    def body(*refs):
        args = (refs[:n], refs[n:2 * n], *refs[2 * n:])
        comm.start(*args)
        comm.forward(*args)
        comm.finish(*args)

    return pl.pallas_call(body, name=name, out_shape=comm.out_shape, in_specs=[ANY] * n, out_specs=[ANY] * n,
                          scratch_shapes=comm.scratch)(*arrays)


def _rs_sibling(arrays, name):
    n = len(arrays)

    def body(*refs):
        srcs, got, (send, recv) = refs[:n], refs[n:2 * n], refs[2 * n:]
        x, y, c = _place()
        copies = []
        for a in range(n):
            for r, (fx, fy) in enumerate(CHIP_FLIPS):
                chip = 2 * _flip(x, fx) + _flip(y, fy)
                copies.append(pltpu.make_async_remote_copy(
                    src_ref=srcs[a].at[2 * chip + 1 - c], dst_ref=got[a].at[r], send_sem=send.at[4 * a + r],
                    recv_sem=recv.at[4 * a + r], device_id=(x, y, 1 - c), device_id_type=MESH))
        for cp in copies:
            cp.start()
        for cp in copies:
            cp.wait_send()
        for cp in copies:
            cp.wait_recv()

    return pl.pallas_call(
        body, name=name, out_shape=[jax.ShapeDtypeStruct((4, *a.shape[1:]), a.dtype) for a in arrays],
        in_specs=[ANY] * n, out_specs=[ANY] * n,
        scratch_shapes=[pltpu.SemaphoreType.DMA((4 * n,)), pltpu.SemaphoreType.DMA((4 * n,))],
    )(*arrays)


def _chip_slots():
    x, y, c = _place()
    return jnp.stack([4 * _flip(x, fx) + 2 * _flip(y, fy) + c for fx, fy in CHIP_FLIPS]).astype(I32)


def _tiles(rows, cols, steps=4):
    if rows % (16 * steps) == 0:
        return steps, (rows // steps, cols), lambda i: (i, 0)
    if cols % (LANES * steps) == 0:
        return steps, (rows, cols // steps), lambda i: (0, i)
    return 1, (rows, cols), lambda i: (0, 0)


def _pair_sum(full, theirs, name):
    _, rows, cols = theirs.shape
    steps, tile, at = _tiles(rows, cols)

    def body(slots_ref, m0_ref, m1_ref, m2_ref, m3_ref, b_ref, own_ref, rest_ref):
        own_ref[...] = m0_ref[...].astype(F32) + b_ref[0].astype(F32)
        for k, m_ref in enumerate((m1_ref, m2_ref, m3_ref)):
            rest_ref[k] = (m_ref[...].astype(F32) + b_ref[k + 1].astype(F32)).astype(BF16)

    def mine(k):
        return pl.BlockSpec((None, *tile), lambda i, slots: (slots[k], *at(i)))

    return pl.pallas_call(
        body, name=name,
        grid_spec=pltpu.PrefetchScalarGridSpec(
            num_scalar_prefetch=1, grid=(steps,),
            in_specs=[mine(0), mine(1), mine(2), mine(3), pl.BlockSpec((4, *tile), lambda i, slots: (0, *at(i)))],
            out_specs=(pl.BlockSpec(tile, lambda i, slots: at(i)), pl.BlockSpec((3, *tile), lambda i, slots: (0, *at(i))))),
        out_shape=(jax.ShapeDtypeStruct((rows, cols), F32), jax.ShapeDtypeStruct((3, rows, cols), BF16)),
        compiler_params=_cp("parallel"),
    )(_chip_slots(), full, full, full, full, theirs)


def _head_lanes(width, h):
    lane = lax.broadcasted_iota(I32, (1, width), 1)
    if width == LANES:
        return (lane >= 64 * h) & (lane < 64 * h + 64)
    nope = (lane >= NOPE * h) & (lane < NOPE * h + NOPE)
    rope = (lane >= 2 * NOPE + ROPE * h) & (lane < 2 * NOPE + ROPE * h + ROPE)
    return nope | rope


def _dilated_bias_table(seq):
    t = min(ATTN_TILE, seq)
    nd = seq // t

    def body(o_ref):
        delta = pl.program_id(0) * t + lax.broadcasted_iota(I32, (t, t), 1) - lax.broadcasted_iota(I32, (t, t), 0)
        mult = ((delta <= 128).astype(I32) + (((delta & 3) == 0) & (delta <= 512)).astype(I32)
                + ((delta & 15) == 0).astype(I32))
        logm = jnp.where(mult == 3, math.log(3.0), jnp.where(mult == 2, math.log(2.0), 0.0))
        valid = (delta >= 0) & (mult > 0)
        dist = delta.astype(F32)
        for h in range(N_HEADS):
            o_ref[h] = jnp.where(valid, logm - 2.0 ** (-(h + 1)) * dist, NEG)

    return pl.pallas_call(
        body, name="dilated_bias_table", grid=(nd,), out_shape=jax.ShapeDtypeStruct((N_HEADS, nd, t, t), F32),
        out_specs=pl.BlockSpec((N_HEADS, None, t, t), lambda d: (0, d, 0, 0)),
        compiler_params=_cp("parallel"),
    )()


def _comm_hooks(comm, refs, n_in, n_out):
    if comm is None:
        return refs[:n_in], refs[n_in:n_in + n_out], refs[n_in + n_out:], None
    n = comm.n
    ins, srcs = refs[:n_in], refs[n_in:n_in + n]
    outs, dsts = refs[n_in + n:n_in + n + n_out], refs[n_in + n + n_out:n_in + 2 * n + n_out]
    rest = refs[n_in + 2 * n + n_out:]
    own = len(rest) - len(comm.scratch)
    return ins, outs, rest[:own], (srcs, dsts, *rest[own:])


def _attn_fwd(q, k, v, bias, *, batch, seq, width, col0, dilated, scale, name, comm=None, comm_arrays=()):
    t = min(ATTN_TILE, seq)
    nq = seq // t
    cq, ck, cv = col0
    pre = scale if dilated else 1.0
    steps = batch * N_PAIRS

    def body(*refs):
        (q_ref, k_ref, v_ref, bias_ref), (o_ref, lse_ref), (v_heads,), plan = _comm_hooks(comm, refs, 4, 2)
        step_no = pl.program_id(0) * N_PAIRS + pl.program_id(1)
        if plan:
            pl.when(step_no == 0)(lambda: comm.start(*plan))
            pl.when(step_no == (3 * steps) // 4)(lambda: comm.forward(*plan))
        v_all = v_ref[...].astype(F32)
        for h in (0, 1):
            v_heads[h] = jnp.transpose(jnp.where(_head_lanes(LANES, h), v_all, 0.0)).astype(BF16)
        top = lax.broadcasted_iota(I32, (LANES, t), 0) < HEAD_V
        causal = lax.broadcasted_iota(I32, (t, t), 0) <= lax.broadcasted_iota(I32, (t, t), 1)
        def heads(i):
            q2 = q_ref[pl.ds(pl.multiple_of(i * t, t), t), :]
            q2 = q2 * pre if dilated else q2
            return [jnp.where(_head_lanes(width, h), q2, jnp.zeros_like(q2)) for h in (0, 1)]

        def scores(qh, j):
            kj = k_ref[pl.ds(pl.multiple_of(j * t, t), t), :]
            return tuple(_dot_nt(kj, qh[h]) for h in (0, 1))

        lax.fori_loop(0, nq, functools.partial(query_tile, heads, scores, bias_ref, o_ref, lse_ref, v_heads, top, causal),
                      scores(heads(0), 0))
        if plan:
            pl.when(step_no == steps - 1)(lambda: comm.finish(*plan))

    def query_tile(heads, scores, bias_ref, o_ref, lse_ref, v_heads, top, causal, i, first):
        qs = pl.multiple_of(i * t, t)
        qh = heads(i)

        def step(j, carry, last):
            m0, l0, m1, l1, acc, s0, s1 = carry
            ahead = scores(heads(jnp.minimum(i + 1, nq - 1)), 0) if last else scores(qh, j + 1)
            ks = pl.multiple_of(j * t, t)
            new, alphas, pv = [], [], []
            for h, (m, l, s) in enumerate(((m0, l0, s0), (m1, l1, s1))):
                if dilated:
                    s = s + bias_ref[h, i - j]
                else:
                    s = s * scale
                    if last:
                        s = jnp.where(causal, s, NEG)
                m_new = jnp.maximum(m, jnp.max(s, axis=0, keepdims=True))
                a = jnp.exp(m - m_new)
                p = jnp.exp(s - m_new)
                new += [m_new, a * l + jnp.sum(p, axis=0, keepdims=True)]
                alphas.append(a)
                pv.append(_dot(v_heads[h, :, pl.ds(ks, t)], p.astype(BF16)))
            acc = jnp.where(top, alphas[0], alphas[1]) * acc + pv[0] + pv[1]
            return (*new, acc, *ahead)

        row = jnp.full((1, t), NEG, F32)
        zero = jnp.zeros((1, t), F32)
        init = (row, zero, row, zero, jnp.zeros((LANES, t), F32), *first)
        m0, l0, m1, l1, acc, *following = step(i, lax.fori_loop(0, i, functools.partial(step, last=False), init), True)
        o_ref[pl.ds(qs, t), :] = jnp.transpose(acc * jnp.where(top, 1.0 / l0, 1.0 / l1)).astype(BF16)
        r = lax.broadcasted_iota(I32, (8, t), 0)
        lse_ref[:, pl.ds(qs, t)] = jnp.where(r == 0, m0 + jnp.log(l0), jnp.where(r == 1, m1 + jnp.log(l1), 0.0))
        return tuple(following)

    bias_spec = (pl.BlockSpec((2, nq, t, t), lambda b, p: (p, 0, 0, 0)) if dilated
                 else pl.BlockSpec((None, 8, LANES), lambda b, p: (0, 0, 0)))
    n = comm.n if comm else 0
    return pl.pallas_call(
        body, name=name, grid=(batch, N_PAIRS),
        out_shape=[jax.ShapeDtypeStruct((batch * seq, DIL_WIDTH), BF16), jax.ShapeDtypeStruct((batch * N_PAIRS, 8, seq), F32)]
        + (comm.out_shape if comm else []),
        in_specs=[pl.BlockSpec((seq, width), lambda b, p: (b, cq + p)),
                  pl.BlockSpec((seq, width), lambda b, p: (b, ck + p)),
                  pl.BlockSpec((seq, LANES), lambda b, p: (b, cv + p)),
                  bias_spec] + [ANY] * n,
        out_specs=[pl.BlockSpec((seq, LANES), lambda b, p: (b, p)),
                   pl.BlockSpec((None, 8, seq), lambda b, p: (b * N_PAIRS + p, 0, 0))] + [ANY] * n,
        scratch_shapes=[pltpu.VMEM((2, LANES, seq), BF16)] + (comm.scratch if comm else []),
        compiler_params=_cp("arbitrary", "arbitrary") if comm else _cp("parallel", "parallel"),
    )(q, k, v, bias, *comm_arrays)


def _attn_bwd(q, k, v, o, do, lse, bias, *, batch, seq, width, col0, dilated, scale, name, comm=None, comm_arrays=()):
    t = min(ATTN_TILE, seq)
    nq = seq // t
    cq, ck, cv = col0
    pre = scale if dilated else 1.0
    dq_transposed = width == LANES
    steps = batch * N_PAIRS

    def body(*refs):
        ins, (dq_ref, dk_ref, dv_ref), (dq_acc, dk_acc, dv_acc, rowdot, q_heads, do_heads), plan = _comm_hooks(comm, refs, 7, 3)
        q_ref, k_ref, v_ref, o_ref, do_ref, lse_ref, bias_ref = ins
        step_no = pl.program_id(0) * N_PAIRS + pl.program_id(1)
        if plan:
            pl.when(step_no == 0)(lambda: comm.start(*plan))
        wlane = [_head_lanes(width, h) for h in (0, 1)]
        vlane = [_head_lanes(LANES, h) for h in (0, 1)]
        causal = lax.broadcasted_iota(I32, (t, t), 0) <= lax.broadcasted_iota(I32, (t, t), 1)
        q_all = q_ref[...] * pre if dilated else q_ref[...]
        for h in (0, 1):
            q_heads[h] = jnp.where(wlane[h], q_all, jnp.zeros_like(q_all))
            do_heads[h] = jnp.where(vlane[h], do_ref[...], jnp.zeros_like(do_ref[...]))
        prod = jnp.transpose(do_ref[...].astype(F32) * o_ref[...].astype(F32))
        rowdot[0:1, :] = jnp.sum(prod[0:HEAD_V], axis=0, keepdims=True)
        rowdot[1:2, :] = jnp.sum(prod[HEAD_V:], axis=0, keepdims=True)
        dq_acc[...] = jnp.zeros_like(dq_acc)

        def k_tile(j, _):
            ks = pl.multiple_of(j * t, t)
            kj = k_ref[pl.ds(ks, t), :]
            vj = v_ref[pl.ds(ks, t), :]
            kh = [jnp.where(wlane[h], kj, jnp.zeros_like(kj)) for h in (0, 1)]
            if dq_transposed:
                kh = [jnp.transpose(kh[h].astype(F32)).astype(BF16) for h in (0, 1)]
            dk_acc[...] = jnp.zeros_like(dk_acc)
            dv_acc[...] = jnp.zeros_like(dv_acc)

            def operands(i):
                qs = pl.multiple_of(i * t, t)
                return [q_heads[h, pl.ds(qs, t), :] for h in (0, 1)], [do_heads[h, pl.ds(qs, t), :] for h in (0, 1)]

            def products(i):
                qih, doih = operands(i)
                scores = tuple(_dot_nt(kj, qih[h]) for h in (0, 1))
                return scores + tuple(_dot_nt(vj, doih[h]) for h in (0, 1)) if width > LANES else scores

            def q_tile(n, carry, last):
                i = nq - 1 - n
                ahead = () if last else products(i - 1)
                qs = pl.multiple_of(i * t, t)
                qih, doih = operands(i)
                s0, s1 = carry[:2]
                dps = carry[2:] if width > LANES else [_dot_nt(vj, doih[h]) for h in (0, 1)]
                dq_i = jnp.zeros((width, t) if dq_transposed else (t, width), F32)
                for h, (s, dp) in enumerate(((s0, dps[0]), (s1, dps[1]))):
                    if dilated:
                        s = s + bias_ref[h, i - j]
                    else:
                        s = s * scale
                        if last:
                            s = jnp.where(causal, s, NEG)
                    p = jnp.exp(s - lse_ref[h:h + 1, pl.ds(qs, t)])
                    ds = p * (dp - rowdot[h:h + 1, pl.ds(qs, t)])
                    ds = (ds if dilated else ds * scale).astype(BF16)
                    dv_acc[...] += _dot(p.astype(BF16), doih[h])
                    dk_acc[...] += _dot(ds, qih[h])
                    dq_i = dq_i + (_dot(kh[h], ds) if dq_transposed else _dot_tn(ds, kh[h]))
                if dq_transposed:
                    dq_acc[:, pl.ds(qs, t)] += dq_i
                else:
                    dq_acc[pl.ds(qs, t), :] += dq_i
                return ahead

            q_tile(nq - 1 - j, lax.fori_loop(0, nq - 1 - j, functools.partial(q_tile, last=False), products(nq - 1)), True)
            dk_ref[pl.ds(ks, t), :] = dk_acc[...].astype(BF16)
            dv_ref[pl.ds(ks, t), :] = dv_acc[...].astype(BF16)
            return 0

        lax.fori_loop(0, nq, k_tile, 0)
        dq_ref[...] = ((jnp.transpose(dq_acc[...]) if dq_transposed else dq_acc[...]) * pre).astype(BF16)
        if plan:
            pl.when(step_no == steps - 1)(lambda: comm.finish(*plan))

    tokens = batch * seq
    bias_spec = (pl.BlockSpec((2, nq, t, t), lambda b, p: (p, 0, 0, 0)) if dilated
                 else pl.BlockSpec((None, 8, LANES), lambda b, p: (0, 0, 0)))
    n = comm.n if comm else 0
    return pl.pallas_call(
        body, name=name, grid=(batch, N_PAIRS),
        out_shape=[jax.ShapeDtypeStruct((tokens, N_PAIRS * width), BF16), jax.ShapeDtypeStruct((tokens, N_PAIRS * width), BF16),
                   jax.ShapeDtypeStruct((tokens, DIL_WIDTH), BF16)] + (comm.out_shape if comm else []),
        in_specs=[pl.BlockSpec((seq, width), lambda b, p: (b, cq + p)),
                  pl.BlockSpec((seq, width), lambda b, p: (b, ck + p)),
                  pl.BlockSpec((seq, LANES), lambda b, p: (b, cv + p)),
                  pl.BlockSpec((seq, LANES), lambda b, p: (b, p)),
                  pl.BlockSpec((seq, LANES), lambda b, p: (b, p)),
                  pl.BlockSpec((None, 8, seq), lambda b, p: (b * N_PAIRS + p, 0, 0)),
                  bias_spec] + [ANY] * n,
        out_specs=[pl.BlockSpec((seq, width), lambda b, p: (b, p)),
                   pl.BlockSpec((seq, width), lambda b, p: (b, p)),
                   pl.BlockSpec((seq, LANES), lambda b, p: (b, p))] + [ANY] * n,
        scratch_shapes=[pltpu.VMEM((width, seq) if dq_transposed else (seq, width), F32),
                        pltpu.VMEM((t, width), F32), pltpu.VMEM((t, LANES), F32),
                        pltpu.VMEM((8, seq), F32), pltpu.VMEM((2, seq, width), BF16), pltpu.VMEM((2, seq, LANES), BF16)]
        + (comm.scratch if comm else []),
        compiler_params=_cp("arbitrary", "arbitrary") if comm else _cp("parallel", "parallel"),
    )(q, k, v, o, do, lse, bias, *comm_arrays)


def _rms(xf, g):
    r = lax.rsqrt(jnp.mean(xf * xf, axis=1, keepdims=True) + RMS_EPS)
    return xf * r * g, r


def _rms_bwd(dy, xf, r, g):
    gy = dy * g
    dx = r * gy - xf * (r * r * r) * jnp.mean(gy * xf, axis=1, keepdims=True)
    return dx, dy * xf * r


def _ln_bwd(dy, xhat, rstd, g):
    dxh = dy * g
    return rstd * (dxh - jnp.mean(dxh, axis=1, keepdims=True) - xhat * jnp.mean(dxh * xhat, axis=1, keepdims=True))


def _rope_slabs(q, cos, sin, transpose):
    first_half = (lax.broadcasted_iota(I32, (1, LANES), 1) % ROPE) < ROPE // 2
    out = []
    for p in range(N_PAIRS):
        blk = q[:, p * PAIR_W + LANES:(p + 1) * PAIR_W]
        y = blk * sin if transpose else blk
        up, down = pltpu.roll(y, LANES - ROPE // 2, 1), pltpu.roll(y, ROPE // 2, 1)
        rot = jnp.where(first_half, up, -down) if transpose else jnp.where(first_half, -up, down) * sin
        out += [q[:, p * PAIR_W:p * PAIR_W + LANES], blk * cos + rot]
    return jnp.concatenate(out, axis=1)


def _fwd_proj(x, w_in_ext, w1, wk, wv, g_q, g_kv, cext, sext, cs128, *, seq):
    tokens = x.shape[0]
    tm = min(2 * TOKEN_TILE, seq)
    ns = seq // tm

    def body(x_ref, win_ref, w1_ref, wk_ref, wv_ref, gq_ref, gkv_ref, c_ref, s_ref, cs_ref,
             low_ref, gates_ref, qkvd_ref, qp_ref, kp_ref, vm_ref, qn_ref, kvn_ref, xb_ref):
        xt = x_ref[...].astype(BF16)
        xb_ref[...] = xt
        low = _dot(xt, win_ref[:, 0:LOW_W])
        low_ref[...] = low
        qkvd_ref[...] = _dot(xt, win_ref[:, LOW_W:LOW_W + 3 * DIL_WIDTH]).astype(BF16)
        gates_ref[...] = _dot(xt, win_ref[:, LOW_W + 3 * DIL_WIDTH:]).astype(BF16)
        qn = _rms(low[:, 0:Q_LORA], gq_ref[...])[0].astype(BF16)
        kvn = _rms(low[:, Q_LORA:Q_LORA + KV_LORA], gkv_ref[...])[0].astype(BF16)
        qn_ref[...] = qn
        kvn_ref[...] = kvn
        qp_ref[...] = _rope_slabs(_dot(qn, w1_ref[...]), c_ref[...], s_ref[...], False).astype(BF16)
        kr = low[:, Q_LORA + KV_LORA:] * cs_ref[...]
        kr = kr + pltpu.roll(kr, LANES - ROPE, 1)
        lane = lax.broadcasted_iota(I32, kr.shape, 1)
        kr = jnp.where(lane < ROPE, kr, 0.0)
        kr = (kr + pltpu.roll(kr, ROPE, 1)).astype(BF16)
        kn = _dot(kvn, wk_ref[...]).astype(BF16)
        kp_ref[...] = jnp.concatenate([blk for p in range(N_PAIRS) for blk in (kn[:, p * LANES:(p + 1) * LANES], kr)], axis=1)
        vm_ref[...] = _dot(kvn, wv_ref[...]).astype(BF16)

    n_gates = 2 * D_MODEL
    outs = [(LOW_W, F32), (n_gates, BF16), (3 * DIL_WIDTH, BF16), (N_PAIRS * PAIR_W, BF16), (N_PAIRS * PAIR_W, BF16),
            (DIL_WIDTH, BF16), (Q_LORA, BF16), (KV_LORA, BF16), (D_MODEL, BF16)]
    return pl.pallas_call(
        body, name="fwd_proj", grid=(tokens // tm,),
        out_shape=tuple(jax.ShapeDtypeStruct((tokens, w), dt) for w, dt in outs),
        in_specs=[_rows(tm, D_MODEL), _full(w_in_ext.shape), _full(w1.shape), _full(wk.shape),
                  _full(wv.shape), _full(g_q.shape), _full(g_kv.shape),
                  pl.BlockSpec((tm, LANES), lambda i: (i % ns, 1)),
                  pl.BlockSpec((tm, LANES), lambda i: (i % ns, 1)),
                  pl.BlockSpec((tm, LANES), lambda i: (i % ns, 0))],
        out_specs=tuple(_rows(tm, w) for w, _ in outs),
        compiler_params=_cp("parallel"),
    )(x, w_in_ext, w1, wk, wv, g_q, g_kv, cext, sext, cs128)


def _fwd_mix(o_a, o_b, gates, x, b_gate, w_oa, w_ob, w_out, ln_g, ln_b, *, seq):
    tokens = x.shape[0]
    tm = min(MIX_TILE, seq)

    def body(oa_ref, ob_ref, gt_ref, x_ref, bg_ref, woa_ref, wob_ref, wout_ref, g_ref, b_ref,
             hb_ref, xhat_ref, rstd_ref, ya_ref, yb_ref, mix_ref):
        ya = _dot(oa_ref[...], woa_ref[...])
        yb = _dot(ob_ref[...], wob_ref[...])
        g0 = _sigmoid(gt_ref[:, 0:D_MODEL].astype(F32) + bg_ref[0:1, :])
        g1 = _sigmoid(gt_ref[:, D_MODEL:].astype(F32) + bg_ref[1:2, :])
        mix = (g0 * ya + g1 * yb).astype(BF16)
        z = ALPHA * x_ref[...] + _dot(mix, wout_ref[...])
        zc = z - jnp.mean(z, axis=1, keepdims=True)
        rstd = lax.rsqrt(jnp.mean(zc * zc, axis=1, keepdims=True) + LN_EPS)
        xhat = zc * rstd
        hb_ref[...] = (xhat * g_ref[...] + b_ref[...]).astype(BF16)
        xhat_ref[...] = xhat
        rstd_ref[...] = jnp.broadcast_to(rstd, (tm, LANES))
        ya_ref[...] = ya.astype(BF16)
        yb_ref[...] = yb.astype(BF16)
        mix_ref[...] = mix

    outs = [(D_MODEL, BF16), (D_MODEL, F32), (LANES, F32), (D_MODEL, BF16), (D_MODEL, BF16), (D_MODEL, BF16)]
    return pl.pallas_call(
        body, name="fwd_mix", grid=(tokens // tm,),
        out_shape=tuple(jax.ShapeDtypeStruct((tokens, w), dt) for w, dt in outs),
        in_specs=[_rows(tm, DIL_WIDTH), _rows(tm, DIL_WIDTH), _rows(tm, 2 * D_MODEL), _rows(tm, D_MODEL),
                  _full(b_gate.shape), _full(w_oa.shape), _full(w_ob.shape), _full(w_out.shape),
                  _full(ln_g.shape), _full(ln_b.shape)],
        out_specs=tuple(_rows(tm, w) for w, _ in outs),
        compiler_params=_cp("parallel"),
    )(o_a, o_b, gates, x, b_gate, w_oa, w_ob, w_out, ln_g, ln_b)


def _fwd_mlp(hb, xhat1, target, w_ff1, w_ff2, ln1_g, ln1_b, ln_g, ln_b, *, seq):
    tokens = hb.shape[0]
    tm = min(2 * TOKEN_TILE, seq)
    tf = FF_SHARD
    nf = N_DEV // FF_STEP

    def body(hb_ref, xh_ref, tg_ref, w1_ref, w2_ref, g1_ref, b1_ref, g_ref, b_ref, u_ref, dz_ref, dzb_ref, stat_ref, acc):
        i, j = pl.program_id(0), pl.program_id(1)

        @pl.when((i == 0) & (j == 0))
        def _():
            stat_ref[...] = jnp.zeros_like(stat_ref)

        @pl.when(j == 0)
        def _():
            acc[...] = jnp.zeros_like(acc)

        acts = []
        for s in range(FF_STEP):
            u = _dot(hb_ref[...], w1_ref[s])
            u_ref[:, s * tf:(s + 1) * tf] = u.astype(BF16)
            acts.append(jnp.square(jnp.maximum(u, 0.0)).astype(BF16))
        acc[...] += _dot(jnp.concatenate(acts, axis=1), w2_ref[...])

        @pl.when(j == nf - 1)
        def _():
            z = ALPHA * (xh_ref[...] * g1_ref[...] + b1_ref[...]) + acc[...]
            zc = z - jnp.mean(z, axis=1, keepdims=True)
            rstd = lax.rsqrt(jnp.mean(zc * zc, axis=1, keepdims=True) + LN_EPS)
            xhat = zc * rstd
            err = xhat * g_ref[...] + b_ref[...] - tg_ref[...]
            dy = err * (1.0 / D_MODEL)
            dz = _ln_bwd(dy, xhat, rstd, g_ref[...])
            dz_ref[...] = dz
            dzb_ref[...] = dz.astype(BF16)
            stat_ref[0:1, :] += jnp.sum(dy * xhat, axis=0, keepdims=True)
            stat_ref[1:2, :] += jnp.sum(dy, axis=0, keepdims=True)
            stat_ref[2:3, :] += jnp.sum(jnp.sum(err * err, axis=1, keepdims=True), axis=0, keepdims=True) * (0.5 / D_MODEL)

    return pl.pallas_call(
        body, name="fwd_mlp", grid=(tokens // tm, nf),
        out_shape=(jax.ShapeDtypeStruct((tokens, D_FF), BF16), jax.ShapeDtypeStruct((tokens, D_MODEL), F32),
                   jax.ShapeDtypeStruct((tokens, D_MODEL), BF16), jax.ShapeDtypeStruct((8, D_MODEL), F32)),
        in_specs=[_rows(tm, D_MODEL), _rows(tm, D_MODEL), _rows(tm, D_MODEL),
                  pl.BlockSpec((FF_STEP, D_MODEL, tf), lambda i, j: (j, 0, 0)),
                  pl.BlockSpec((FF_STEP * tf, D_MODEL), lambda i, j: (j, 0)),
                  _full(ln1_g.shape), _full(ln1_b.shape), _full(ln_g.shape), _full(ln_b.shape)],
        out_specs=(pl.BlockSpec((tm, FF_STEP * tf), lambda i, j: (i, j)), _rows(tm, D_MODEL), _rows(tm, D_MODEL),
                   _full((8, D_MODEL))),
        scratch_shapes=[pltpu.VMEM((tm, D_MODEL), F32)],
        compiler_params=_cp("arbitrary", "arbitrary"),
    )(hb, xhat1, target, w_ff1, w_ff2, ln1_g, ln1_b, ln_g, ln_b)


def _bwd_mlp(dz2, dz2b, u, xhat1, rstd1, w_ff1, w_ff2, ln_g, *, seq):
    tokens = dz2.shape[0]
    tm = min(2 * TOKEN_TILE, seq)
    tf = FF_SHARD
    nf = N_DEV // FF_STEP

    def body(dz_ref, dzb_ref, u_ref, xh_ref, rs_ref, w1_ref, w2_ref, g_ref, du_ref, dz1_ref, dz1b_ref, stat_ref, acc):
        i, j = pl.program_id(0), pl.program_id(1)

        @pl.when((i == 0) & (j == 0))
        def _():
            stat_ref[...] = jnp.zeros_like(stat_ref)

        @pl.when(j == 0)
        def _():
            acc[...] = jnp.zeros_like(acc)

        da = _dot_nt(dzb_ref[...], w2_ref[...])
        du = (da * (2.0 * jnp.maximum(u_ref[...].astype(F32), 0.0))).astype(BF16)
        du_ref[...] = du
        part = _dot_nt(du[:, 0:tf], w1_ref[0])
        for s in range(1, FF_STEP):
            part = part + _dot_nt(du[:, s * tf:(s + 1) * tf], w1_ref[s])
        acc[...] += part

        @pl.when(j == nf - 1)
        def _():
            dh = ALPHA * dz_ref[...] + acc[...]
            xhat = xh_ref[...]
            dz1 = _ln_bwd(dh, xhat, rs_ref[:, 0:1], g_ref[...])
            dz1_ref[...] = dz1
            dz1b_ref[...] = dz1.astype(BF16)
            stat_ref[0:1, :] += jnp.sum(dh * xhat, axis=0, keepdims=True)
            stat_ref[1:2, :] += jnp.sum(dh, axis=0, keepdims=True)

    return pl.pallas_call(
        body, name="bwd_mlp", grid=(tokens // tm, nf),
        out_shape=(jax.ShapeDtypeStruct((tokens, D_FF), BF16), jax.ShapeDtypeStruct((tokens, D_MODEL), F32),
                   jax.ShapeDtypeStruct((tokens, D_MODEL), BF16), jax.ShapeDtypeStruct((8, D_MODEL), F32)),
        in_specs=[_rows(tm, D_MODEL), _rows(tm, D_MODEL), pl.BlockSpec((tm, FF_STEP * tf), lambda i, j: (i, j)),
                  _rows(tm, D_MODEL), _rows(tm, LANES),
                  pl.BlockSpec((FF_STEP, D_MODEL, tf), lambda i, j: (j, 0, 0)),
                  pl.BlockSpec((FF_STEP * tf, D_MODEL), lambda i, j: (j, 0)),
                  _full(ln_g.shape)],
        out_specs=(pl.BlockSpec((tm, FF_STEP * tf), lambda i, j: (i, j)), _rows(tm, D_MODEL), _rows(tm, D_MODEL),
                   _full((8, D_MODEL))),
        scratch_shapes=[pltpu.VMEM((tm, D_MODEL), F32)],
        compiler_params=_cp("arbitrary", "arbitrary"),
    )(dz2, dz2b, u, xhat1, rstd1, w_ff1, w_ff2, ln_g)


def _bwd_mix(dz1b, gates, y_a, y_b, b_gate, w_oa, w_ob, w_out, *, seq):
    tokens = dz1b.shape[0]
    tm = min(MIX_TILE, seq)

    def body(dz_ref, gt_ref, ya_ref, yb_ref, bg_ref, woa_ref, wob_ref, wout_ref,
             dgt_ref, dya_ref, dyb_ref, doa_ref, dob_ref, stat_ref):
        @pl.when(pl.program_id(0) == 0)
        def _():
            stat_ref[...] = jnp.zeros_like(stat_ref)

        dmix = _dot_nt(dz_ref[...], wout_ref[...])
        for k, (y_ref, w_ref, dy_ref, do_ref) in enumerate(((ya_ref, woa_ref, dya_ref, doa_ref), (yb_ref, wob_ref, dyb_ref, dob_ref))):
            g = _sigmoid(gt_ref[:, k * D_MODEL:(k + 1) * D_MODEL].astype(F32) + bg_ref[k:k + 1, :])
            dgate = dmix * y_ref[...].astype(F32) * g * (1.0 - g)
            dgt_ref[:, k * D_MODEL:(k + 1) * D_MODEL] = dgate.astype(BF16)
            stat_ref[k:k + 1, :] += jnp.sum(dgate, axis=0, keepdims=True)
            dy = (dmix * g).astype(BF16)
            dy_ref[...] = dy
            do_ref[...] = _dot_nt(dy, w_ref[...]).astype(BF16)

    outs = [(2 * D_MODEL, BF16), (D_MODEL, BF16), (D_MODEL, BF16), (DIL_WIDTH, BF16), (DIL_WIDTH, BF16)]
    return pl.pallas_call(
        body, name="bwd_mix", grid=(tokens // tm,),
        out_shape=tuple(jax.ShapeDtypeStruct((tokens, w), dt) for w, dt in outs) + (jax.ShapeDtypeStruct((8, D_MODEL), F32),),
        in_specs=[_rows(tm, D_MODEL), _rows(tm, 2 * D_MODEL), _rows(tm, D_MODEL), _rows(tm, D_MODEL),
                  _full(b_gate.shape), _full(w_oa.shape), _full(w_ob.shape), _full(w_out.shape)],
        out_specs=tuple(_rows(tm, w) for w, _ in outs) + (_full((8, D_MODEL)),),
        compiler_params=_cp("arbitrary"),
    )(dz1b, gates, y_a, y_b, b_gate, w_oa, w_ob, w_out)


def _bwd_proj(dqp, dkp, dvm, dq_d, dk_d, dv_d, dgates, dz1, low, w_in_ext, w1, wk, wv, g_q, g_kv, cext, sext, cs128, *, seq):
    tokens = dz1.shape[0]
    tm = min(TOKEN_TILE, seq)
    ns = seq // tm

    def body(dqp_ref, dkp_ref, dvm_ref, dqd_ref, dkd_ref, dvd_ref, dgt_ref, dz_ref, low_ref, win_ref, w1_ref, wk_ref,
             wv_ref, gq_ref, gkv_ref, c_ref, s_ref, cs_ref, dx_ref, dproj_ref, da_ref, dkn_ref, stat_ref):
        @pl.when(pl.program_id(0) == 0)
        def _():
            stat_ref[...] = jnp.zeros_like(stat_ref)

        low = low_ref[...]
        d_a = _rope_slabs(dqp_ref[...].astype(F32), c_ref[...], s_ref[...], True).astype(BF16)
        da_ref[...] = d_a
        q_a = low[:, 0:Q_LORA]
        _, rq = _rms(q_a, gq_ref[...])
        dq_a, gq_terms = _rms_bwd(_dot_nt(d_a, w1_ref[...]), q_a, rq, gq_ref[...])
        kv_a = low[:, Q_LORA:Q_LORA + KV_LORA]
        _, rkv = _rms(kv_a, gkv_ref[...])
        dkn = jnp.concatenate([dkp_ref[:, p * PAIR_W:p * PAIR_W + LANES] for p in range(N_PAIRS)], axis=1)
        dkn_ref[...] = dkn
        dkv_a, gkv_terms = _rms_bwd(_dot_nt(dkn, wk_ref[...]) + _dot_nt(dvm_ref[...], wv_ref[...]), kv_a, rkv, gkv_ref[...])
        dkr = sum(dkp_ref[:, p * PAIR_W + LANES:(p + 1) * PAIR_W].astype(F32) for p in range(N_PAIRS))
        dkr = dkr + pltpu.roll(dkr, LANES - ROPE, 1)
        dkr = jnp.where(lax.broadcasted_iota(I32, dkr.shape, 1) < ROPE, dkr, 0.0)
        dkr = (dkr + pltpu.roll(dkr, ROPE, 1)) * cs_ref[...]
        stat_ref[0:1, 0:Q_LORA] += jnp.sum(gq_terms, axis=0, keepdims=True)
        stat_ref[1:2, 0:KV_LORA] += jnp.sum(gkv_terms, axis=0, keepdims=True)
        dproj_ref[:, 0:Q_LORA] = dq_a.astype(BF16)
        dproj_ref[:, Q_LORA:Q_LORA + KV_LORA] = dkv_a.astype(BF16)
        dproj_ref[:, Q_LORA + KV_LORA:LOW_W] = dkr.astype(BF16)
        dproj_ref[:, LOW_W:LOW_W + DIL_WIDTH] = dqd_ref[...]
        dproj_ref[:, LOW_W + DIL_WIDTH:LOW_W + 2 * DIL_WIDTH] = dkd_ref[...]
        dproj_ref[:, LOW_W + 2 * DIL_WIDTH:LOW_W + 3 * DIL_WIDTH] = dvd_ref[...]
        dproj_ref[:, LOW_W + 3 * DIL_WIDTH:] = dgt_ref[...]
        dx_ref[...] = ALPHA * dz_ref[...] + _dot_nt(dproj_ref[...], win_ref[...])

    wide = N_PAIRS * PAIR_W
    return pl.pallas_call(
        body, name="bwd_proj", grid=(tokens // tm,),
        out_shape=(jax.ShapeDtypeStruct((tokens, D_MODEL), F32), jax.ShapeDtypeStruct((tokens, IN_EXT), BF16),
                   jax.ShapeDtypeStruct((tokens, wide), BF16), jax.ShapeDtypeStruct((tokens, N_HEADS * NOPE), BF16),
                   jax.ShapeDtypeStruct((8, D_MODEL), F32)),
        in_specs=[_rows(tm, wide), _rows(tm, wide), _rows(tm, DIL_WIDTH), _rows(tm, DIL_WIDTH), _rows(tm, DIL_WIDTH),
                  _rows(tm, DIL_WIDTH), _rows(tm, 2 * D_MODEL),
                  _rows(tm, D_MODEL), _rows(tm, LOW_W), _full(w_in_ext.shape), _full(w1.shape),
                  _full(wk.shape), _full(wv.shape), _full(g_q.shape), _full(g_kv.shape),
                  pl.BlockSpec((tm, LANES), lambda i: (i % ns, 1)), pl.BlockSpec((tm, LANES), lambda i: (i % ns, 1)),
                  pl.BlockSpec((tm, LANES), lambda i: (i % ns, 0))],
        out_specs=(_rows(tm, D_MODEL), _rows(tm, IN_EXT), _rows(tm, wide), _rows(tm, N_HEADS * NOPE), _full((8, D_MODEL))),
        compiler_params=_cp("arbitrary"),
    )(dqp, dkp, dvm, dq_d, dk_d, dv_d, dgates, dz1, low, w_in_ext, w1, wk, wv, g_q, g_kv, cext, sext, cs128)


def _wgrad(a, b, name, square_relu=False, by_shard=False):
    tokens, ka = a.shape
    n = b.shape[1]
    if ka <= 512 or ka % 512 == 0:
        tka = min(ka, 512)
    else:
        tka = max(w for w in range(LANES, min(ka, 2304) + 1, LANES) if ka % w == 0)
    shard = n // N_DEV
    tn = WGRAD_SHARDS * shard if by_shard else max(w for w in range(LANES, min(n, 2304) + 1, LANES) if n % w == 0)
    tt = tokens
    while tt * (tka + tn) * 2 * 2 > WGRAD_BLOCK_BYTES:
        tt //= 2
    nt = tokens // tt

    def body(a_ref, b_ref, o_ref, acc):
        kt = pl.program_id(2)

        @pl.when(kt == 0)
        def _():
            acc[...] = jnp.zeros_like(acc)

        at = a_ref[...]
        if square_relu:
            at = jnp.square(jnp.maximum(at.astype(F32), 0.0)).astype(BF16)
        acc[...] += _dot_tn(at, b_ref[...])

        @pl.when(kt == nt - 1)
        def _():
            if by_shard:
                for s in range(WGRAD_SHARDS):
                    o_ref[s] = acc[:, s * shard:(s + 1) * shard].astype(BF16)
            else:
                o_ref[...] = acc[...].astype(BF16)

    if by_shard:
        out_shape, out_spec = (N_DEV, ka, shard), pl.BlockSpec((WGRAD_SHARDS, tka, shard), lambda i, j, k: (j, i, 0))
    else:
        out_shape, out_spec = (ka, n), pl.BlockSpec((tka, tn), lambda i, j, k: (i, j))
    return pl.pallas_call(
        body, name=name, grid=(ka // tka, n // tn, nt), out_shape=jax.ShapeDtypeStruct(out_shape, BF16),
        in_specs=[pl.BlockSpec((tt, tka), lambda i, j, k: (k, i)), pl.BlockSpec((tt, tn), lambda i, j, k: (k, j))],
        out_specs=out_spec,
        scratch_shapes=[pltpu.VMEM((tka, tn), F32)],
        compiler_params=_cp("parallel", "parallel", "arbitrary"),
    )(a, b)


def _adam_math(w, g, m, v):
    m = ADAM_B1 * m + (1.0 - ADAM_B1) * g
    v = ADAM_B2 * v + (1.0 - ADAM_B2) * jnp.square(g)
    m_hat = m / (1.0 - ADAM_B1 ** ADAM_STEP)
    v_hat = v / (1.0 - ADAM_B2 ** ADAM_STEP)
    return -ADAM_LR * (m_hat / (jnp.sqrt(v_hat) + ADAM_EPS) + ADAM_WD * w), m, v


def _adamw(items, name):
    steps = min(_tiles(*w.shape)[0] for w, *_ in items)
    n_items = len(items)

    def body(slot_ref, *refs):
        ins, outs = refs[:5 * n_items], refs[5 * n_items:]
        for k, (_, _, _, _, parts) in enumerate(items):
            w_ref, m_ref, v_ref, own_ref, p_ref = ins[5 * k:5 * k + 5]
            g_ref, d_ref, nm_ref, nv_ref = outs[4 * k:4 * k + 4]
            g = own_ref[...].astype(F32)
            for d in range(parts.shape[0]):
                g = g + p_ref[d].astype(F32)
            g_ref[...] = g
            d_ref[...], nm_ref[...], nv_ref[...] = _adam_math(w_ref[...], g, m_ref[...], v_ref[...])

    x, y, c = _place()
    in_specs, out_specs, out_shape, args = [], [], [], []
    for w, m, v, own, parts in items:
        rows, cols = w.shape
        _, tile, at = _tiles(rows, cols, steps)
        blk = pl.BlockSpec(tile, lambda i, slot, at=at: at(i))
        own_blk = blk if own.ndim == 2 else pl.BlockSpec((None, *tile), lambda i, slot, at=at: (slot[0], *at(i)))
        in_specs += [blk, blk, blk, own_blk, pl.BlockSpec((parts.shape[0], *tile), lambda i, slot, at=at: (0, *at(i)))]
        out_specs += [blk] * 4
        out_shape += [jax.ShapeDtypeStruct((rows, cols), F32)] * 4
        args += [w, m, v, own, parts]
    out = pl.pallas_call(
        body, name=name,
        grid_spec=pltpu.PrefetchScalarGridSpec(num_scalar_prefetch=1, grid=(steps,), in_specs=in_specs, out_specs=out_specs),
        out_shape=out_shape, compiler_params=_cp("parallel"),
    )(jnp.reshape(4 * x + 2 * y + c, (1,)).astype(I32), *args)
    return [tuple(out[4 * k:4 * k + 4]) for k in range(n_items)]


def _adamw_small(parts, w, m, v):
    _, rows, cols = parts.shape

    def body(p_ref, w_ref, m_ref, v_ref, g_ref, d_ref, nm_ref, nv_ref):
        g = p_ref[0]
        for d in range(1, N_DEV):
            g = g + p_ref[d]
        g_ref[...] = g
        d_ref[...], nm_ref[...], nv_ref[...] = _adam_math(w_ref[...], g, m_ref[...], v_ref[...])

    return pl.pallas_call(
        body, name="adamw_replicated", out_shape=(jax.ShapeDtypeStruct((rows, cols), F32),) * 4,
        in_specs=[_full(parts.shape)] + [_full((rows, cols))] * 3, out_specs=(_full((rows, cols)),) * 4, grid=(1,),
        compiler_params=_cp("arbitrary"),
    )(parts, w, m, v)


def _pad_rows(a2d, mult):
    pad = (-a2d.shape[-2]) % mult
    return jnp.pad(a2d, [(0, 0)] * (a2d.ndim - 2) + [(0, pad), (0, 0)]) if pad else a2d


def _pad_cols(a):
    pad = (-a.shape[-1]) % LANES
    return jnp.pad(a, [(0, 0)] * (a.ndim - 1) + [(0, pad)]) if pad else a


def _rot_cols(w):
    half = ROPE // 2
    return jnp.concatenate([-w[..., half:], w[..., :half]], axis=-1)


def _unrot_cols(dw):
    half = ROPE // 2
    return jnp.concatenate([dw[..., half:], -dw[..., :half]], axis=-1)


def _from_col_shards(stacked):
    return stacked.transpose(1, 0, 2).reshape(stacked.shape[1], -1)


def _to_col_shards(full):
    r = full.shape[0]
    return full.reshape(r, N_DEV, -1).transpose(1, 0, 2)


def _rope_tables(seq):
    half = ROPE // 2
    inv = jnp.power(ROPE_THETA, -jnp.arange(half, dtype=F32) / half)
    ang = jnp.arange(seq, dtype=F32)[:, None] * inv[None, :]
    cos = jnp.concatenate([jnp.cos(ang)] * 2, axis=1)
    sin = jnp.concatenate([jnp.sin(ang)] * 2, axis=1)
    ones, zeros = jnp.ones((seq, 2 * NOPE), F32), jnp.zeros((seq, 2 * NOPE), F32)
    pad = jnp.zeros((seq, PAIR_W - 2 * NOPE - 2 * ROPE), F32)
    cext = jnp.concatenate([ones, cos, cos, pad], axis=1)
    sext = jnp.concatenate([zeros, sin, sin, pad], axis=1)
    cs128 = jnp.concatenate([cos, sin, jnp.zeros((seq, LANES - 2 * ROPE), F32)], axis=1)
    return cext, sext, cs128


def _pair_slabs(nope, rope):
    k = nope.shape[0]
    nope = nope.reshape(k, N_PAIRS, 2 * NOPE)
    rope = jnp.zeros((k, N_PAIRS, 2 * ROPE), nope.dtype) if rope is None else rope.reshape(k, N_PAIRS, 2 * ROPE)
    pad = jnp.zeros((k, N_PAIRS, PAIR_W - 2 * NOPE - 2 * ROPE), nope.dtype)
    return jnp.concatenate([nope, rope, pad], axis=2).reshape(k, N_PAIRS * PAIR_W)


def _split_slabs(slabs):
    k = slabs.shape[0]
    s = slabs.reshape(k, N_PAIRS, PAIR_W)
    return s[:, :, :2 * NOPE].reshape(k, N_HEADS, NOPE), s[:, :, 2 * NOPE:2 * NOPE + 2 * ROPE].reshape(k, N_HEADS, ROPE)


def kernel(x, w_in, b_gate, g_q_a, w_uq, g_kv_a, w_ukv, w_o_mla, w_o_dil, w_out, ln1_g, ln1_b, w_ff1, w_ff2, ln2_g, ln2_b, loss_target, m_w_in, m_b_gate, m_g_q_a, m_w_uq, m_g_kv_a, m_w_ukv, m_w_o_mla, m_w_o_dil, m_w_out, m_ln1_g, m_ln1_b, m_w_ff1, m_w_ff2, m_ln2_g, m_ln2_b, v_w_in, v_b_gate, v_g_q_a, v_w_uq, v_g_kv_a, v_w_ukv, v_w_o_mla, v_w_o_dil, v_w_out, v_ln1_g, v_ln1_b, v_w_ff1, v_w_ff2, v_ln2_g, v_ln2_b):
    batch, seq, _ = x.shape
    tokens = batch * seq
    weights = dict(w_in=w_in, w_uq=w_uq, w_ukv=w_ukv, w_o_mla=w_o_mla, w_o_dil=w_o_dil, w_out=w_out, w_ff1=w_ff1, w_ff2=w_ff2, b_gate=b_gate)
    mom_m = dict(w_in=m_w_in, w_uq=m_w_uq, w_ukv=m_w_ukv, w_o_mla=m_w_o_mla, w_o_dil=m_w_o_dil, w_out=m_w_out, w_ff1=m_w_ff1, w_ff2=m_w_ff2, b_gate=m_b_gate)
    mom_v = dict(w_in=v_w_in, w_uq=v_w_uq, w_ukv=v_w_ukv, w_o_mla=v_w_o_mla, w_o_dil=v_w_o_dil, w_out=v_w_out, w_ff1=v_w_ff1, w_ff2=v_w_ff2, b_gate=v_b_gate)

    first = ["w_in", "w_uq", "w_ukv"]
    widths = [weights[n].shape[2] for n in first]
    shards = [weights["w_in"][0].T.astype(BF16)] + [_pad_cols(weights[n][0].astype(BF16)) for n in first[1:]]
    g_in, g_uq, g_ukv = _run_comm(_Gather(shards), shards, "all_gather_first_weights")
    g_uq, g_ukv = g_uq[:, :, :widths[1]], g_ukv[:, :, :widths[2]]

    s1, s2, n_in = Q_LORA + KV_LORA, Q_LORA + KV_LORA + ROPE, N_DEV * widths[0]

    def w_in_cols(lo, hi):
        out = []
        while lo < hi:
            d, off = divmod(lo, widths[0])
            take = min(hi - lo, widths[0] - off)
            out.append(g_in[d][off:off + take].T)
            lo += take
        return out

    w_in_ext = jnp.concatenate(w_in_cols(0, s2) + [_rot_cols(jnp.concatenate(w_in_cols(s1, s2), axis=1)),
                                                   jnp.zeros((D_MODEL, LOW_W - s2 - ROPE), BF16)] + w_in_cols(s2, n_in), axis=1)
    uq = _from_col_shards(g_uq).reshape(Q_LORA, N_HEADS, NOPE + ROPE)
    w1 = _pair_slabs(uq[:, :, :NOPE], uq[:, :, NOPE:])
    ukv = _from_col_shards(g_ukv).reshape(KV_LORA, N_HEADS, NOPE + HEAD_V)
    wk = ukv[:, :, :NOPE].reshape(KV_LORA, N_HEADS * NOPE)
    wv = ukv[:, :, NOPE:].reshape(KV_LORA, N_HEADS * HEAD_V)
    cext, sext, cs128 = _rope_tables(seq)
    dil_bias = _dilated_bias_table(seq)
    no_bias = jnp.zeros((1, 8, LANES), F32)

    x2 = x.reshape(tokens, D_MODEL)
    low, gates, qkvd, qp, kp, vm, qn, kvn, xb = _fwd_proj(x2, w_in_ext, w1, wk, wv, g_q_a, g_kv_a, cext, sext, cs128, seq=seq)
    bg = b_gate[0]
    bg_hi = bg.astype(BF16)
    bg_lo = (bg - bg_hi.astype(F32)).astype(BF16)
    later = [weights[n][0].astype(BF16) for n in ("w_o_mla", "w_o_dil", "w_out", "w_ff1", "w_ff2")]
    later.append(_pad_rows(jnp.concatenate([bg_hi, bg_lo], axis=0), 16))
    mla = dict(batch=batch, seq=seq, width=PAIR_W, col0=(0, 0, 0), dilated=False, scale=MLA_SCALE)
    dil = dict(batch=batch, seq=seq, width=LANES, col0=(0, N_PAIRS, 2 * N_PAIRS), dilated=True, scale=DIL_SCALE)
    o_a, lse_a, g_oa, g_ob, g_out, g_ff1, g_ff2, g_bg = _attn_fwd(
        qp, kp, vm, no_bias, name="mla_attention_fwd", comm=_Gather(later), comm_arrays=later, **mla)
    o_b, lse_b = _attn_fwd(qkvd, qkvd, qkvd, dil_bias, name="dilated_attention_fwd", **dil)
    w_oa, w_ob = _from_col_shards(g_oa), _from_col_shards(g_ob)
    w_out_full = g_out.reshape(D_MODEL, D_MODEL)
    w_ff2_full = g_ff2.reshape(D_FF, D_MODEL)
    bg_parts = g_bg.astype(F32)
    b_gate_full = _from_col_shards(bg_parts[:, 0:2] + bg_parts[:, 2:4])
    hb, xhat1, rstd1, y_a, y_b, mix = _fwd_mix(o_a, o_b, gates, x2, b_gate_full, w_oa, w_ob, w_out_full, ln1_g, ln1_b, seq=seq)
    u, dz2, dz2b, stat2 = _fwd_mlp(hb, xhat1, loss_target.reshape(tokens, D_MODEL), g_ff1, w_ff2_full, ln1_g, ln1_b, ln2_g, ln2_b, seq=seq)

    du, dz1, dz1b, stat1 = _bwd_mlp(dz2, dz2b, u, xhat1, rstd1, g_ff1, w_ff2_full, ln1_g, seq=seq)
    dw_ff = [_wgrad(hb, du, "wgrad_ff1", by_shard=True),
             _wgrad(u, dz2b, "wgrad_ff2", square_relu=True).reshape(N_DEV, FF_SHARD, D_MODEL)]
    dgates, dy_a, dy_b, do_a, do_b, stat_g = _bwd_mix(dz1b, gates, y_a, y_b, b_gate_full, w_oa, w_ob, w_out_full, seq=seq)
    dqp, dkp, dvm, r_ff1, r_ff2 = _attn_bwd(qp, kp, vm, o_a, do_a, lse_a, no_bias, name="mla_attention_bwd",
                                            comm=_Scatter(dw_ff), comm_arrays=dw_ff, **mla)
    dw_mid = [_to_col_shards(_wgrad(o_a, dy_a, "wgrad_o_mla")), _to_col_shards(_wgrad(o_b, dy_b, "wgrad_o_dil")),
              _wgrad(mix, dz1b, "wgrad_out").reshape(N_DEV, D_MODEL // N_DEV, D_MODEL),
              _pad_rows(_to_col_shards(stat_g[0:2]).astype(BF16), 16)]
    dq_d, dk_d, dv_d, r_oa, r_ob, r_out, r_bg = _attn_bwd(qkvd, qkvd, qkvd, o_b, do_b, lse_b, dil_bias, name="dilated_attention_bwd",
                                                          comm=_Scatter(dw_mid), comm_arrays=dw_mid, **dil)
    grad_x, dproj, d_a, dkn, stat_r = _bwd_proj(dqp, dkp, dvm, dq_d, dk_d, dv_d, dgates, dz1, low, w_in_ext, w1, wk, wv,
                                                g_q_a, g_kv_a, cext, sext, cs128, seq=seq)

    dw_in_ext = _wgrad(dproj, xb, "wgrad_in")
    dw1 = _wgrad(qn, d_a, "wgrad_uq")
    dwk = _wgrad(kvn, dkn, "wgrad_ukv_k")
    dwv = _wgrad(kvn, dvm, "wgrad_ukv_v")
    dw_kr = dw_in_ext[s1:s2] + _unrot_cols(dw_in_ext[s2:s2 + ROPE].T).T

    def dw_in_cols(lo, hi):
        out = []
        for a, b, piece in ((0, s1, lambda u, v: dw_in_ext[u:v]), (s1, s2, lambda u, v: dw_kr[u - s1:v - s1]),
                            (s2, n_in, lambda u, v: dw_in_ext[u + LOW_W - s2:v + LOW_W - s2])):
            if max(lo, a) < min(hi, b):
                out.append(piece(max(lo, a), min(hi, b)))
        return out

    dw_in = jnp.stack([jnp.concatenate(dw_in_cols(d * widths[0], (d + 1) * widths[0]), axis=0) for d in range(N_DEV)])
    n1, r1 = _split_slabs(dw1)
    dw_uq = jnp.concatenate([n1, r1], axis=2).reshape(Q_LORA, N_HEADS * (NOPE + ROPE))
    dw_ukv = jnp.concatenate([dwk.reshape(KV_LORA, N_HEADS, NOPE), dwv.reshape(KV_LORA, N_HEADS, HEAD_V)], axis=2).reshape(KV_LORA, N_HEADS * (NOPE + HEAD_V))
    last = [dw_in] + [_pad_cols(_to_col_shards(dw)) for dw in (dw_uq, dw_ukv)]
    theirs = _rs_sibling(last, "rs_last_sibling_exchange")
    sums = [_pair_sum(a, b, "rs_last_pair_sum_" + n) for a, b, n in zip(last, theirs, first)]
    partial = jnp.concatenate([stat_r[0:1, :Q_LORA], stat_r[1:2, :KV_LORA], stat1[0:1], stat1[1:2], stat2[0:1], stat2[1:2],
                               stat2[2:3, :LANES]], axis=1)
    partial = _pad_rows(partial.reshape(-1, LANES), 8)
    rest = [s[1] for s in sums]
    got_in, got_uq, got_ukv, every = _run_comm(_Plans([_ChipExchange(rest), _Gather([partial])]), rest + [partial],
                                               "rs_last_chip_exchange")

    upd = {}
    early = ["w_ff1", "w_ff2", "w_out", "w_o_mla", "w_o_dil"]
    items = [(weights[n][0], mom_m[n][0], mom_v[n][0], own, parts) for n, own, parts in
             zip(early, (dw_ff[0], dw_ff[1], dw_mid[2], dw_mid[0], dw_mid[1]), (r_ff1, r_ff2, r_out, r_oa, r_ob))]
    upd.update(zip(early, _adamw(items, "adamw_early_weights")))
    (in_t,) = _adamw([(weights["w_in"][0].T, mom_m["w_in"][0].T, mom_v["w_in"][0].T, sums[0][0], got_in)], "adamw_w_in")
    upd["w_in"] = tuple(a.T for a in in_t)
    for n, w, (own, _), parts in zip(first[1:], widths[1:], sums[1:], (got_uq, got_ukv)):
        (upd[n],) = _adamw([(weights[n][0], mom_m[n][0], mom_v[n][0], own[:, :w], parts[:, :, :w])], "adamw_" + n)
    (bg_upd,) = _adamw([(_pad_rows(b_gate[0], 16), _pad_rows(m_b_gate[0], 16), _pad_rows(v_b_gate[0], 16), dw_mid[3], r_bg)],
                       "adamw_b_gate")
    upd["b_gate"] = tuple(t[0:2] for t in bg_upd)

    small_w = [g_q_a, g_kv_a, ln1_g, ln1_b, ln2_g, ln2_b]
    small_m = [m_g_q_a, m_g_kv_a, m_ln1_g, m_ln1_b, m_ln2_g, m_ln2_b]
    small_v = [v_g_q_a, v_g_kv_a, v_ln1_g, v_ln1_b, v_ln2_g, v_ln2_b]
    small_widths = [a.shape[1] for a in small_w]

    def as_rows(vecs, extra):
        flat = jnp.concatenate(vecs + [jnp.zeros((1, extra), F32)], axis=1)
        return _pad_rows(flat.reshape(-1, LANES), 8)

    g_s, d_s, nm_s, nv_s = _adamw_small(every, as_rows(small_w, LANES), as_rows(small_m, LANES), as_rows(small_v, LANES))

    def split_small(a):
        flat = a.reshape(1, -1)
        out, c0 = [], 0
        for w in small_widths:
            out.append(flat[:, c0:c0 + w])
            c0 += w
        return out, flat[0, c0]

    g_small, loss = split_small(g_s)
    small = [g_small, split_small(d_s)[0], split_small(nm_s)[0], split_small(nv_s)[0]]

    order = ["w_in", "b_gate", "g_q_a", "w_uq", "g_kv_a", "w_ukv", "w_o_mla", "w_o_dil", "w_out", "ln1_g", "ln1_b", "w_ff1", "w_ff2", "ln2_g", "ln2_b"]
    small_names = ["g_q_a", "g_kv_a", "ln1_g", "ln1_b", "ln2_g", "ln2_b"]

    def pick(kind):
        return [small[kind][small_names.index(n)] if n in small_names else upd[n][kind][None] for n in order]

    return (loss, grad_x.reshape(batch, seq, D_MODEL), *pick(0), *pick(1), *pick(2), *pick(3))
```

```python
import functools
import math

import jax
import jax.numpy as jnp
from jax import lax
from jax.experimental import pallas as pl
from jax.experimental.pallas import tpu as pltpu

F32 = jnp.float32
BF16 = jnp.bfloat16
I32 = jnp.int32

D_MODEL = 1024
N_HEADS = 8
NOPE = 64
ROPE = 32
HEAD_V = 64
Q_LORA = 384
KV_LORA = 256
DIL_WIDTH = 512
D_FF = 4096
ROPE_THETA = 10000.0
LN_EPS = 1e-5
RMS_EPS = 1e-6
NEG = -1e30
ALPHA = 2.0 ** 0.25
MLA_SCALE = (NOPE + ROPE) ** -0.5
DIL_SCALE = 64 ** -0.5
ADAM_LR, ADAM_B1, ADAM_B2, ADAM_EPS, ADAM_WD, ADAM_STEP = 0.001, 0.9, 0.999, 1e-08, 0.01, 10

LANES = 128
PAIR_W = 256
N_PAIRS = N_HEADS // 2
LOW_W = 768
IN_EXT = LOW_W + 3 * DIL_WIDTH + 2 * D_MODEL
N_DEV = 8
FF_SHARD = D_FF // N_DEV
FF_STEP = 4
WGRAD_SHARDS = 4
TOKEN_TILE = 256
MIX_TILE = 512
ATTN_TILE = 512
VMEM_LIMIT = 56 << 20

MESH = pl.DeviceIdType.MESH
ANY = pl.BlockSpec(memory_space=pl.ANY)
CHIP_FLIPS = ((0, 0), (0, 1), (1, 0), (1, 1))
PEER_FLIPS = tuple((fx, fy, fc) for fx in (0, 1) for fy in (0, 1) for fc in (0, 1))[1:]


def _cp(*sem):
    return pltpu.CompilerParams(dimension_semantics=sem or None, vmem_limit_bytes=VMEM_LIMIT)


def _full(shape):
    nd = len(shape)
    return pl.BlockSpec(shape, lambda *_: (0,) * nd)


def _rows(tm, width):
    return pl.BlockSpec((tm, width), lambda i, *_: (i, 0))


def _dot(a, b):
    return jnp.dot(a, b, preferred_element_type=F32)


def _dot_nt(a, b):
    return lax.dot_general(a, b, (((1,), (1,)), ((), ())), preferred_element_type=F32)


def _dot_tn(a, b):
    return lax.dot_general(a, b, (((0,), (0,)), ((), ())), preferred_element_type=F32)


def _sigmoid(z):
    return 1.0 / (1.0 + jnp.exp(-z))


def _place():
    return lax.axis_index("x"), lax.axis_index("y"), lax.axis_index("c")


def _flip(v, f):
    return 1 - v if f else v


class _Gather:
    def __init__(self, shards):
        self.n = len(shards)
        self.out_shape = [jax.ShapeDtypeStruct((N_DEV, *s.shape), s.dtype) for s in shards]
        self.scratch = [pltpu.SemaphoreType.DMA((7 * self.n,)), pltpu.SemaphoreType.DMA((7 * self.n,)),
                        pltpu.SemaphoreType.DMA((self.n,))]

    def _copies(self, what, srcs, dsts, send, recv, local):
        x, y, c = _place()
        chips = [(_flip(x, fx), _flip(y, fy)) for fx, fy in CHIP_FLIPS[1:]]
        out = []
        for a in range(self.n):
            def slot(px, py, pc, a=a):
                return dsts[a].at[4 * px + 2 * py + pc]

            def copy(k, block, to, src=None, a=a, slot=slot):
                return pltpu.make_async_remote_copy(
                    src_ref=slot(*block) if src is None else src, dst_ref=slot(*block),
                    send_sem=send.at[7 * a + k], recv_sem=recv.at[7 * a + k], device_id=to, device_id_type=MESH)

            if what == "mine":
                out.append(pltpu.make_async_copy(srcs[a], slot(x, y, c), local.at[a]))
            elif what == "first":
                out.append(copy(0, (x, y, c), (x, y, 1 - c), src=srcs[a]))
                out += [copy(1 + j, (x, y, c), (*chip, c), src=srcs[a]) for j, chip in enumerate(chips)]
            elif what == "landed":
                out += [copy(1 + j, (*chip, c), (x, y, c)) for j, chip in enumerate(chips)]
            elif what == "passed":
                out += [copy(4 + j, (*chip, c), (x, y, 1 - c)) for j, chip in enumerate(chips)]
            else:
                out.append(copy(0, (x, y, 1 - c), (x, y, c)))
                out += [copy(4 + j, (*chip, 1 - c), (x, y, c)) for j, chip in enumerate(chips)]
        return out

    def start(self, *refs):
        for cp in self._copies("first", *refs) + self._copies("mine", *refs):
            cp.start()

    def forward(self, *refs):
        for landed, passed in zip(self._copies("landed", *refs), self._copies("passed", *refs)):
            landed.wait_recv()
            passed.start()

    def finish(self, *refs):
        for cp in self._copies("from_sibling", *refs):
            cp.wait_recv()
        for cp in self._copies("first", *refs) + self._copies("passed", *refs):
            cp.wait_send()
        for cp in self._copies("mine", *refs):
            cp.wait()


class _Scatter:
    def __init__(self, arrays):
        self.n = len(arrays)
        self.out_shape = [jax.ShapeDtypeStruct((7, *a.shape[1:]), a.dtype) for a in arrays]
        self.scratch = [pltpu.SemaphoreType.DMA((7 * self.n,)), pltpu.SemaphoreType.DMA((7 * self.n,))]

    def _copies(self, srcs, dsts, send, recv):
        x, y, c = _place()
        out = []
        for a in range(self.n):
            for k, (fx, fy, fc) in enumerate(PEER_FLIPS):
                px, py, pc = _flip(x, fx), _flip(y, fy), _flip(c, fc)
                out.append(pltpu.make_async_remote_copy(
                    src_ref=srcs[a].at[4 * px + 2 * py + pc], dst_ref=dsts[a].at[k],
                    send_sem=send.at[7 * a + k], recv_sem=recv.at[7 * a + k], device_id=(px, py, pc), device_id_type=MESH))
        return out

    def start(self, *refs):
        for cp in self._copies(*refs):
            cp.start()

    def forward(self, *refs):
        pass

    def finish(self, *refs):
        for cp in self._copies(*refs):
            cp.wait_send()
        for cp in self._copies(*refs):
            cp.wait_recv()


class _ChipExchange:
    def __init__(self, arrays):
        self.n = len(arrays)
        self.out_shape = [jax.ShapeDtypeStruct(a.shape, a.dtype) for a in arrays]
        self.scratch = [pltpu.SemaphoreType.DMA((3 * self.n,)), pltpu.SemaphoreType.DMA((3 * self.n,))]

    def _copies(self, srcs, dsts, send, recv):
        x, y, c = _place()
        return [pltpu.make_async_remote_copy(
            src_ref=srcs[a].at[k], dst_ref=dsts[a].at[k], send_sem=send.at[3 * a + k], recv_sem=recv.at[3 * a + k],
            device_id=(_flip(x, fx), _flip(y, fy), c), device_id_type=MESH)
            for a in range(self.n) for k, (fx, fy) in enumerate(CHIP_FLIPS[1:])]

    def start(self, *refs):
        for cp in self._copies(*refs):
            cp.start()

    def forward(self, *refs):
        pass

    def finish(self, *refs):
        for cp in self._copies(*refs):
            cp.wait_send()
        for cp in self._copies(*refs):
            cp.wait_recv()


class _Plans:
    def __init__(self, plans):
        self.plans = plans
        self.n = sum(p.n for p in plans)
        self.out_shape = [s for p in plans for s in p.out_shape]
        self.scratch = [s for p in plans for s in p.scratch]

    def _each(self, phase, srcs, dsts, *sems):
        i0 = s0 = 0
        for p in self.plans:
            getattr(p, phase)(srcs[i0:i0 + p.n], dsts[i0:i0 + p.n], *sems[s0:s0 + len(p.scratch)])
            i0, s0 = i0 + p.n, s0 + len(p.scratch)

    def start(self, *refs):
        self._each("start", *refs)

    def forward(self, *refs):
        self._each("forward", *refs)

    def finish(self, *refs):
        self._each("finish", *refs)


def _run_comm(comm, arrays, name):
    n = comm.n

    def body(*refs):
        args = (refs[:n], refs[n:2 * n], *refs[2 * n:])
        comm.start(*args)
        comm.forward(*args)
        comm.finish(*args)

    return pl.pallas_call(body, name=name, out_shape=comm.out_shape, in_specs=[ANY] * n, out_specs=[ANY] * n,
                          scratch_shapes=comm.scratch)(*arrays)


def _rs_sibling(arrays, name):
    n = len(arrays)

    def body(*refs):
        srcs, got, (send, recv) = refs[:n], refs[n:2 * n], refs[2 * n:]
        x, y, c = _place()
        copies = []
        for a in range(n):
            for r, (fx, fy) in enumerate(CHIP_FLIPS):
                chip = 2 * _flip(x, fx) + _flip(y, fy)
                copies.append(pltpu.make_async_remote_copy(
                    src_ref=srcs[a].at[2 * chip + 1 - c], dst_ref=got[a].at[r], send_sem=send.at[4 * a + r],
                    recv_sem=recv.at[4 * a + r], device_id=(x, y, 1 - c), device_id_type=MESH))
        for cp in copies:
            cp.start()
        for cp in copies:
            cp.wait_send()
        for cp in copies:
            cp.wait_recv()

    return pl.pallas_call(
        body, name=name, out_shape=[jax.ShapeDtypeStruct((4, *a.shape[1:]), a.dtype) for a in arrays],
        in_specs=[ANY] * n, out_specs=[ANY] * n,
        scratch_shapes=[pltpu.SemaphoreType.DMA((4 * n,)), pltpu.SemaphoreType.DMA((4 * n,))],
    )(*arrays)


def _chip_slots():
    x, y, c = _place()
    return jnp.stack([4 * _flip(x, fx) + 2 * _flip(y, fy) + c for fx, fy in CHIP_FLIPS]).astype(I32)


def _tiles(rows, cols, steps=4):
    if rows % (16 * steps) == 0:
        return steps, (rows // steps, cols), lambda i: (i, 0)
    if cols % (LANES * steps) == 0:
        return steps, (rows, cols // steps), lambda i: (0, i)
    return 1, (rows, cols), lambda i: (0, 0)


def _pair_sum(full, theirs, name):
    _, rows, cols = theirs.shape
    steps, tile, at = _tiles(rows, cols)

    def body(slots_ref, m0_ref, m1_ref, m2_ref, m3_ref, b_ref, own_ref, rest_ref):
        own_ref[...] = m0_ref[...].astype(F32) + b_ref[0].astype(F32)
        for k, m_ref in enumerate((m1_ref, m2_ref, m3_ref)):
            rest_ref[k] = (m_ref[...].astype(F32) + b_ref[k + 1].astype(F32)).astype(BF16)

    def mine(k):
        return pl.BlockSpec((None, *tile), lambda i, slots: (slots[k], *at(i)))

    return pl.pallas_call(
        body, name=name,
        grid_spec=pltpu.PrefetchScalarGridSpec(
            num_scalar_prefetch=1, grid=(steps,),
            in_specs=[mine(0), mine(1), mine(2), mine(3), pl.BlockSpec((4, *tile), lambda i, slots: (0, *at(i)))],
            out_specs=(pl.BlockSpec(tile, lambda i, slots: at(i)), pl.BlockSpec((3, *tile), lambda i, slots: (0, *at(i))))),
        out_shape=(jax.ShapeDtypeStruct((rows, cols), F32), jax.ShapeDtypeStruct((3, rows, cols), BF16)),
        compiler_params=_cp("parallel"),
    )(_chip_slots(), full, full, full, full, theirs)


def _head_lanes(width, h):
    lane = lax.broadcasted_iota(I32, (1, width), 1)
    if width == LANES:
        return (lane >= 64 * h) & (lane < 64 * h + 64)
    nope = (lane >= NOPE * h) & (lane < NOPE * h + NOPE)
    rope = (lane >= 2 * NOPE + ROPE * h) & (lane < 2 * NOPE + ROPE * h + ROPE)
    return nope | rope


def _dilated_bias_table(seq):
    t = min(ATTN_TILE, seq)
    nd = seq // t

    def body(o_ref):
        delta = pl.program_id(0) * t + lax.broadcasted_iota(I32, (t, t), 1) - lax.broadcasted_iota(I32, (t, t), 0)
        mult = ((delta <= 128).astype(I32) + (((delta & 3) == 0) & (delta <= 512)).astype(I32)
                + ((delta & 15) == 0).astype(I32))
        logm = jnp.where(mult == 3, math.log(3.0), jnp.where(mult == 2, math.log(2.0), 0.0))
        valid = (delta >= 0) & (mult > 0)
        dist = delta.astype(F32)
        for h in range(N_HEADS):
            o_ref[h] = jnp.where(valid, logm - 2.0 ** (-(h + 1)) * dist, NEG)

    return pl.pallas_call(
        body, name="dilated_bias_table", grid=(nd,), out_shape=jax.ShapeDtypeStruct((N_HEADS, nd, t, t), F32),
        out_specs=pl.BlockSpec((N_HEADS, None, t, t), lambda d: (0, d, 0, 0)),
        compiler_params=_cp("parallel"),
    )()


def _comm_hooks(comm, refs, n_in, n_out):
    if comm is None:
        return refs[:n_in], refs[n_in:n_in + n_out], refs[n_in + n_out:], None
    n = comm.n
    ins, srcs = refs[:n_in], refs[n_in:n_in + n]
    outs, dsts = refs[n_in + n:n_in + n + n_out], refs[n_in + n + n_out:n_in + 2 * n + n_out]
    rest = refs[n_in + 2 * n + n_out:]
    own = len(rest) - len(comm.scratch)
    return ins, outs, rest[:own], (srcs, dsts, *rest[own:])


def _attn_fwd(q, k, v, bias, *, batch, seq, width, col0, dilated, scale, name, comm=None, comm_arrays=()):
    t = min(ATTN_TILE, seq)
    nq = seq // t
    cq, ck, cv = col0
    pre = scale if dilated else 1.0
    steps = batch * N_PAIRS

    def body(*refs):
        (q_ref, k_ref, v_ref, bias_ref), (o_ref, lse_ref), (v_heads,), plan = _comm_hooks(comm, refs, 4, 2)
        step_no = pl.program_id(0) * N_PAIRS + pl.program_id(1)
        if plan:
            pl.when(step_no == 0)(lambda: comm.start(*plan))
            pl.when(step_no == (3 * steps) // 4)(lambda: comm.forward(*plan))
        v_all = v_ref[...].astype(F32)
        for h in (0, 1):
            v_heads[h] = jnp.transpose(jnp.where(_head_lanes(LANES, h), v_all, 0.0)).astype(BF16)
        top = lax.broadcasted_iota(I32, (LANES, t), 0) < HEAD_V
        causal = lax.broadcasted_iota(I32, (t, t), 0) <= lax.broadcasted_iota(I32, (t, t), 1)
        def heads(i):
            q2 = q_ref[pl.ds(pl.multiple_of(i * t, t), t), :]
            q2 = q2 * pre if dilated else q2
            return [jnp.where(_head_lanes(width, h), q2, jnp.zeros_like(q2)) for h in (0, 1)]

        def scores(qh, j):
            kj = k_ref[pl.ds(pl.multiple_of(j * t, t), t), :]
            return tuple(_dot_nt(kj, qh[h]) for h in (0, 1))

        lax.fori_loop(0, nq, functools.partial(query_tile, heads, scores, bias_ref, o_ref, lse_ref, v_heads, top, causal),
                      scores(heads(0), 0))
        if plan:
            pl.when(step_no == steps - 1)(lambda: comm.finish(*plan))

    def query_tile(heads, scores, bias_ref, o_ref, lse_ref, v_heads, top, causal, i, first):
        qs = pl.multiple_of(i * t, t)
        qh = heads(i)

        def step(j, carry, last):
            m0, l0, m1, l1, acc, s0, s1 = carry
            ahead = scores(heads(jnp.minimum(i + 1, nq - 1)), 0) if last else scores(qh, j + 1)
            ks = pl.multiple_of(j * t, t)
            new, alphas, pv = [], [], []
            for h, (m, l, s) in enumerate(((m0, l0, s0), (m1, l1, s1))):
                if dilated:
                    s = s + bias_ref[h, i - j]
                else:
                    s = s * scale
                    if last:
                        s = jnp.where(causal, s, NEG)
                m_new = jnp.maximum(m, jnp.max(s, axis=0, keepdims=True))
                a = jnp.exp(m - m_new)
                p = jnp.exp(s - m_new)
                new += [m_new, a * l + jnp.sum(p, axis=0, keepdims=True)]
                alphas.append(a)
                pv.append(_dot(v_heads[h, :, pl.ds(ks, t)], p.astype(BF16)))
            acc = jnp.where(top, alphas[0], alphas[1]) * acc + pv[0] + pv[1]
            return (*new, acc, *ahead)

        row = jnp.full((1, t), NEG, F32)
        zero = jnp.zeros((1, t), F32)
        init = (row, zero, row, zero, jnp.zeros((LANES, t), F32), *first)
        m0, l0, m1, l1, acc, *following = step(i, lax.fori_loop(0, i, functools.partial(step, last=False), init), True)
        o_ref[pl.ds(qs, t), :] = jnp.transpose(acc * jnp.where(top, 1.0 / l0, 1.0 / l1)).astype(BF16)
        r = lax.broadcasted_iota(I32, (8, t), 0)
        lse_ref[:, pl.ds(qs, t)] = jnp.where(r == 0, m0 + jnp.log(l0), jnp.where(r == 1, m1 + jnp.log(l1), 0.0))
        return tuple(following)

    bias_spec = (pl.BlockSpec((2, nq, t, t), lambda b, p: (p, 0, 0, 0)) if dilated
                 else pl.BlockSpec((None, 8, LANES), lambda b, p: (0, 0, 0)))
    n = comm.n if comm else 0
    return pl.pallas_call(
        body, name=name, grid=(batch, N_PAIRS),
        out_shape=[jax.ShapeDtypeStruct((batch * seq, DIL_WIDTH), BF16), jax.ShapeDtypeStruct((batch * N_PAIRS, 8, seq), F32)]
        + (comm.out_shape if comm else []),
        in_specs=[pl.BlockSpec((seq, width), lambda b, p: (b, cq + p)),
                  pl.BlockSpec((seq, width), lambda b, p: (b, ck + p)),
                  pl.BlockSpec((seq, LANES), lambda b, p: (b, cv + p)),
                  bias_spec] + [ANY] * n,
        out_specs=[pl.BlockSpec((seq, LANES), lambda b, p: (b, p)),
                   pl.BlockSpec((None, 8, seq), lambda b, p: (b * N_PAIRS + p, 0, 0))] + [ANY] * n,
        scratch_shapes=[pltpu.VMEM((2, LANES, seq), BF16)] + (comm.scratch if comm else []),
        compiler_params=_cp("arbitrary", "arbitrary") if comm else _cp("parallel", "parallel"),
    )(q, k, v, bias, *comm_arrays)


def _attn_bwd(q, k, v, o, do, lse, bias, *, batch, seq, width, col0, dilated, scale, name, comm=None, comm_arrays=()):
    t = min(ATTN_TILE, seq)
    nq = seq // t
    cq, ck, cv = col0
    pre = scale if dilated else 1.0
    dq_transposed = width == LANES
    steps = batch * N_PAIRS

    def body(*refs):
        ins, (dq_ref, dk_ref, dv_ref), (dq_acc, dk_acc, dv_acc, rowdot, q_heads, do_heads), plan = _comm_hooks(comm, refs, 7, 3)
        q_ref, k_ref, v_ref, o_ref, do_ref, lse_ref, bias_ref = ins
        step_no = pl.program_id(0) * N_PAIRS + pl.program_id(1)
        if plan:
            pl.when(step_no == 0)(lambda: comm.start(*plan))
        wlane = [_head_lanes(width, h) for h in (0, 1)]
        vlane = [_head_lanes(LANES, h) for h in (0, 1)]
        causal = lax.broadcasted_iota(I32, (t, t), 0) <= lax.broadcasted_iota(I32, (t, t), 1)
        q_all = q_ref[...] * pre if dilated else q_ref[...]
        for h in (0, 1):
            q_heads[h] = jnp.where(wlane[h], q_all, jnp.zeros_like(q_all))
            do_heads[h] = jnp.where(vlane[h], do_ref[...], jnp.zeros_like(do_ref[...]))
        prod = jnp.transpose(do_ref[...].astype(F32) * o_ref[...].astype(F32))
        rowdot[0:1, :] = jnp.sum(prod[0:HEAD_V], axis=0, keepdims=True)
        rowdot[1:2, :] = jnp.sum(prod[HEAD_V:], axis=0, keepdims=True)
        dq_acc[...] = jnp.zeros_like(dq_acc)

        def k_tile(j, _):
            ks = pl.multiple_of(j * t, t)
            kj = k_ref[pl.ds(ks, t), :]
            vj = v_ref[pl.ds(ks, t), :]
            kh = [jnp.where(wlane[h], kj, jnp.zeros_like(kj)) for h in (0, 1)]
            if dq_transposed:
                kh = [jnp.transpose(kh[h].astype(F32)).astype(BF16) for h in (0, 1)]
            dk_acc[...] = jnp.zeros_like(dk_acc)
            dv_acc[...] = jnp.zeros_like(dv_acc)

            def operands(i):
                qs = pl.multiple_of(i * t, t)
                return [q_heads[h, pl.ds(qs, t), :] for h in (0, 1)], [do_heads[h, pl.ds(qs, t), :] for h in (0, 1)]

            def products(i):
                qih, doih = operands(i)
                scores = tuple(_dot_nt(kj, qih[h]) for h in (0, 1))
                return scores + tuple(_dot_nt(vj, doih[h]) for h in (0, 1)) if width > LANES else scores

            def q_tile(n, carry, last):
                i = nq - 1 - n
                ahead = () if last else products(i - 1)
                qs = pl.multiple_of(i * t, t)
                qih, doih = operands(i)
                s0, s1 = carry[:2]
                dps = carry[2:] if width > LANES else [_dot_nt(vj, doih[h]) for h in (0, 1)]
                dq_i = jnp.zeros((width, t) if dq_transposed else (t, width), F32)
                for h, (s, dp) in enumerate(((s0, dps[0]), (s1, dps[1]))):
                    if dilated:
                        s = s + bias_ref[h, i - j]
                    else:
                        s = s * scale
                        if last:
                            s = jnp.where(causal, s, NEG)
                    p = jnp.exp(s - lse_ref[h:h + 1, pl.ds(qs, t)])
                    ds = p * (dp - rowdot[h:h + 1, pl.ds(qs, t)])
                    ds = (ds if dilated else ds * scale).astype(BF16)
                    dv_acc[...] += _dot(p.astype(BF16), doih[h])
                    dk_acc[...] += _dot(ds, qih[h])
                    dq_i = dq_i + (_dot(kh[h], ds) if dq_transposed else _dot_tn(ds, kh[h]))
                if dq_transposed:
                    dq_acc[:, pl.ds(qs, t)] += dq_i
                else:
                    dq_acc[pl.ds(qs, t), :] += dq_i
                return ahead

            q_tile(nq - 1 - j, lax.fori_loop(0, nq - 1 - j, functools.partial(q_tile, last=False), products(nq - 1)), True)
            dk_ref[pl.ds(ks, t), :] = dk_acc[...].astype(BF16)
            dv_ref[pl.ds(ks, t), :] = dv_acc[...].astype(BF16)
            return 0

        lax.fori_loop(0, nq, k_tile, 0)
        dq_ref[...] = ((jnp.transpose(dq_acc[...]) if dq_transposed else dq_acc[...]) * pre).astype(BF16)
        if plan:
            pl.when(step_no == steps - 1)(lambda: comm.finish(*plan))

    tokens = batch * seq
    bias_spec = (pl.BlockSpec((2, nq, t, t), lambda b, p: (p, 0, 0, 0)) if dilated
                 else pl.BlockSpec((None, 8, LANES), lambda b, p: (0, 0, 0)))
    n = comm.n if comm else 0
    return pl.pallas_call(
        body, name=name, grid=(batch, N_PAIRS),
        out_shape=[jax.ShapeDtypeStruct((tokens, N_PAIRS * width), BF16), jax.ShapeDtypeStruct((tokens, N_PAIRS * width), BF16),
                   jax.ShapeDtypeStruct((tokens, DIL_WIDTH), BF16)] + (comm.out_shape if comm else []),
        in_specs=[pl.BlockSpec((seq, width), lambda b, p: (b, cq + p)),
                  pl.BlockSpec((seq, width), lambda b, p: (b, ck + p)),
                  pl.BlockSpec((seq, LANES), lambda b, p: (b, cv + p)),
                  pl.BlockSpec((seq, LANES), lambda b, p: (b, p)),
                  pl.BlockSpec((seq, LANES), lambda b, p: (b, p)),
                  pl.BlockSpec((None, 8, seq), lambda b, p: (b * N_PAIRS + p, 0, 0)),
                  bias_spec] + [ANY] * n,
        out_specs=[pl.BlockSpec((seq, width), lambda b, p: (b, p)),
                   pl.BlockSpec((seq, width), lambda b, p: (b, p)),
                   pl.BlockSpec((seq, LANES), lambda b, p: (b, p))] + [ANY] * n,
        scratch_shapes=[pltpu.VMEM((width, seq) if dq_transposed else (seq, width), F32),
                        pltpu.VMEM((t, width), F32), pltpu.VMEM((t, LANES), F32),
                        pltpu.VMEM((8, seq), F32), pltpu.VMEM((2, seq, width), BF16), pltpu.VMEM((2, seq, LANES), BF16)]
        + (comm.scratch if comm else []),
        compiler_params=_cp("arbitrary", "arbitrary") if comm else _cp("parallel", "parallel"),
    )(q, k, v, o, do, lse, bias, *comm_arrays)


def _rms(xf, g):
    r = lax.rsqrt(jnp.mean(xf * xf, axis=1, keepdims=True) + RMS_EPS)
    return xf * r * g, r


def _rms_bwd(dy, xf, r, g):
    gy = dy * g
    dx = r * gy - xf * (r * r * r) * jnp.mean(gy * xf, axis=1, keepdims=True)
    return dx, dy * xf * r


def _ln_bwd(dy, xhat, rstd, g):
    dxh = dy * g
    return rstd * (dxh - jnp.mean(dxh, axis=1, keepdims=True) - xhat * jnp.mean(dxh * xhat, axis=1, keepdims=True))


def _rope_slabs(q, cos, sin, transpose):
    first_half = (lax.broadcasted_iota(I32, (1, LANES), 1) % ROPE) < ROPE // 2
    out = []
    for p in range(N_PAIRS):
        blk = q[:, p * PAIR_W + LANES:(p + 1) * PAIR_W]
        y = blk * sin if transpose else blk
        up, down = pltpu.roll(y, LANES - ROPE // 2, 1), pltpu.roll(y, ROPE // 2, 1)
        rot = jnp.where(first_half, up, -down) if transpose else jnp.where(first_half, -up, down) * sin
        out += [q[:, p * PAIR_W:p * PAIR_W + LANES], blk * cos + rot]
    return jnp.concatenate(out, axis=1)


def _fwd_proj(x, w_in_ext, w1, wk, wv, g_q, g_kv, cext, sext, cs128, *, seq):
    tokens = x.shape[0]
    tm = min(MIX_TILE, seq)
    ns = seq // tm

    def body(x_ref, win_ref, w1_ref, wk_ref, wv_ref, gq_ref, gkv_ref, c_ref, s_ref, cs_ref,
             low_ref, gates_ref, qkvd_ref, qp_ref, kp_ref, vm_ref, qn_ref, kvn_ref, xb_ref):
        xt = x_ref[...].astype(BF16)
        xb_ref[...] = xt
        low = _dot(xt, win_ref[:, 0:LOW_W])
        low_ref[...] = low
        qkvd_ref[...] = _dot(xt, win_ref[:, LOW_W:LOW_W + 3 * DIL_WIDTH]).astype(BF16)
        gates_ref[...] = _dot(xt, win_ref[:, LOW_W + 3 * DIL_WIDTH:]).astype(BF16)
        qn = _rms(low[:, 0:Q_LORA], gq_ref[...])[0].astype(BF16)
        kvn = _rms(low[:, Q_LORA:Q_LORA + KV_LORA], gkv_ref[...])[0].astype(BF16)
        qn_ref[...] = qn
        kvn_ref[...] = kvn
        qp_ref[...] = _rope_slabs(_dot(qn, w1_ref[...]), c_ref[...], s_ref[...], False).astype(BF16)
        kr = low[:, Q_LORA + KV_LORA:] * cs_ref[...]
        kr = kr + pltpu.roll(kr, LANES - ROPE, 1)
        lane = lax.broadcasted_iota(I32, kr.shape, 1)
        kr = jnp.where(lane < ROPE, kr, 0.0)
        kr = (kr + pltpu.roll(kr, ROPE, 1)).astype(BF16)
        kn = _dot(kvn, wk_ref[...]).astype(BF16)
        kp_ref[...] = jnp.concatenate([blk for p in range(N_PAIRS) for blk in (kn[:, p * LANES:(p + 1) * LANES], kr)], axis=1)
        vm_ref[...] = _dot(kvn, wv_ref[...]).astype(BF16)

    n_gates = 2 * D_MODEL
    outs = [(LOW_W, F32), (n_gates, BF16), (3 * DIL_WIDTH, BF16), (N_PAIRS * PAIR_W, BF16), (N_PAIRS * PAIR_W, BF16),
            (DIL_WIDTH, BF16), (Q_LORA, BF16), (KV_LORA, BF16), (D_MODEL, BF16)]
    return pl.pallas_call(
        body, name="fwd_proj", grid=(tokens // tm,),
        out_shape=tuple(jax.ShapeDtypeStruct((tokens, w), dt) for w, dt in outs),
        in_specs=[_rows(tm, D_MODEL), _full(w_in_ext.shape), _full(w1.shape), _full(wk.shape),
                  _full(wv.shape), _full(g_q.shape), _full(g_kv.shape),
                  pl.BlockSpec((tm, LANES), lambda i: (i % ns, 1)),
                  pl.BlockSpec((tm, LANES), lambda i: (i % ns, 1)),
                  pl.BlockSpec((tm, LANES), lambda i: (i % ns, 0))],
        out_specs=tuple(_rows(tm, w) for w, _ in outs),
        compiler_params=_cp("parallel"),
    )(x, w_in_ext, w1, wk, wv, g_q, g_kv, cext, sext, cs128)


def _fwd_mix(o_a, o_b, gates, x, b_gate, w_oa, w_ob, w_out, ln_g, ln_b, *, seq):
    tokens = x.shape[0]
    tm = min(MIX_TILE, seq)

    def body(oa_ref, ob_ref, gt_ref, x_ref, bg_ref, woa_ref, wob_ref, wout_ref, g_ref, b_ref,
             hb_ref, xhat_ref, rstd_ref, ya_ref, yb_ref, mix_ref):
        ya = _dot(oa_ref[...], woa_ref[...])
        yb = _dot(ob_ref[...], wob_ref[...])
        g0 = _sigmoid(gt_ref[:, 0:D_MODEL].astype(F32) + bg_ref[0:1, :])
        g1 = _sigmoid(gt_ref[:, D_MODEL:].astype(F32) + bg_ref[1:2, :])
        mix = (g0 * ya + g1 * yb).astype(BF16)
        z = ALPHA * x_ref[...] + _dot(mix, wout_ref[...])
        zc = z - jnp.mean(z, axis=1, keepdims=True)
        rstd = lax.rsqrt(jnp.mean(zc * zc, axis=1, keepdims=True) + LN_EPS)
        xhat = zc * rstd
        hb_ref[...] = (xhat * g_ref[...] + b_ref[...]).astype(BF16)
        xhat_ref[...] = xhat
        rstd_ref[...] = jnp.broadcast_to(rstd, (tm, LANES))
        ya_ref[...] = ya.astype(BF16)
        yb_ref[...] = yb.astype(BF16)
        mix_ref[...] = mix

    outs = [(D_MODEL, BF16), (D_MODEL, F32), (LANES, F32), (D_MODEL, BF16), (D_MODEL, BF16), (D_MODEL, BF16)]
    return pl.pallas_call(
        body, name="fwd_mix", grid=(tokens // tm,),
        out_shape=tuple(jax.ShapeDtypeStruct((tokens, w), dt) for w, dt in outs),
        in_specs=[_rows(tm, DIL_WIDTH), _rows(tm, DIL_WIDTH), _rows(tm, 2 * D_MODEL), _rows(tm, D_MODEL),
                  _full(b_gate.shape), _full(w_oa.shape), _full(w_ob.shape), _full(w_out.shape),
                  _full(ln_g.shape), _full(ln_b.shape)],
        out_specs=tuple(_rows(tm, w) for w, _ in outs),
        compiler_params=_cp("parallel"),
    )(o_a, o_b, gates, x, b_gate, w_oa, w_ob, w_out, ln_g, ln_b)


def _fwd_mlp(hb, xhat1, target, w_ff1, w_ff2, ln1_g, ln1_b, ln_g, ln_b, *, seq):
    tokens = hb.shape[0]
    tm = min(2 * TOKEN_TILE, seq)
    tf = FF_SHARD
    nf = N_DEV // FF_STEP

    def body(hb_ref, xh_ref, tg_ref, w1_ref, w2_ref, g1_ref, b1_ref, g_ref, b_ref, u_ref, dz_ref, dzb_ref, stat_ref, acc):
        i, j = pl.program_id(0), pl.program_id(1)

        @pl.when((i == 0) & (j == 0))
        def _():
            stat_ref[...] = jnp.zeros_like(stat_ref)

        @pl.when(j == 0)
        def _():
            acc[...] = jnp.zeros_like(acc)

        acts = []
        for s in range(FF_STEP):
            u = _dot(hb_ref[...], w1_ref[s])
            u_ref[:, s * tf:(s + 1) * tf] = u.astype(BF16)
            acts.append(jnp.square(jnp.maximum(u, 0.0)).astype(BF16))
        acc[...] += _dot(jnp.concatenate(acts, axis=1), w2_ref[...])

        @pl.when(j == nf - 1)
        def _():
            z = ALPHA * (xh_ref[...] * g1_ref[...] + b1_ref[...]) + acc[...]
            zc = z - jnp.mean(z, axis=1, keepdims=True)
            rstd = lax.rsqrt(jnp.mean(zc * zc, axis=1, keepdims=True) + LN_EPS)
            xhat = zc * rstd
            err = xhat * g_ref[...] + b_ref[...] - tg_ref[...]
            dy = err * (1.0 / D_MODEL)
            dz = _ln_bwd(dy, xhat, rstd, g_ref[...])
            dz_ref[...] = dz
            dzb_ref[...] = dz.astype(BF16)
            stat_ref[0:1, :] += jnp.sum(dy * xhat, axis=0, keepdims=True)
            stat_ref[1:2, :] += jnp.sum(dy, axis=0, keepdims=True)
            stat_ref[2:3, :] += jnp.sum(jnp.sum(err * err, axis=1, keepdims=True), axis=0, keepdims=True) * (0.5 / D_MODEL)

    return pl.pallas_call(
        body, name="fwd_mlp", grid=(tokens // tm, nf),
        out_shape=(jax.ShapeDtypeStruct((tokens, D_FF), BF16), jax.ShapeDtypeStruct((tokens, D_MODEL), F32),
                   jax.ShapeDtypeStruct((tokens, D_MODEL), BF16), jax.ShapeDtypeStruct((8, D_MODEL), F32)),
        in_specs=[_rows(tm, D_MODEL), _rows(tm, D_MODEL), _rows(tm, D_MODEL),
                  pl.BlockSpec((FF_STEP, D_MODEL, tf), lambda i, j: (j, 0, 0)),
                  pl.BlockSpec((FF_STEP * tf, D_MODEL), lambda i, j: (j, 0)),
                  _full(ln1_g.shape), _full(ln1_b.shape), _full(ln_g.shape), _full(ln_b.shape)],
        out_specs=(pl.BlockSpec((tm, FF_STEP * tf), lambda i, j: (i, j)), _rows(tm, D_MODEL), _rows(tm, D_MODEL),
                   _full((8, D_MODEL))),
        scratch_shapes=[pltpu.VMEM((tm, D_MODEL), F32)],
        compiler_params=_cp("arbitrary", "arbitrary"),
    )(hb, xhat1, target, w_ff1, w_ff2, ln1_g, ln1_b, ln_g, ln_b)


def _bwd_mlp(dz2, dz2b, u, xhat1, rstd1, w_ff1, w_ff2, ln_g, *, seq):
    tokens = dz2.shape[0]
    tm = min(2 * TOKEN_TILE, seq)
    tf = FF_SHARD
    nf = N_DEV // FF_STEP

    def body(dz_ref, dzb_ref, u_ref, xh_ref, rs_ref, w1_ref, w2_ref, g_ref, du_ref, dz1_ref, dz1b_ref, stat_ref, acc):
        i, j = pl.program_id(0), pl.program_id(1)

        @pl.when((i == 0) & (j == 0))
        def _():
            stat_ref[...] = jnp.zeros_like(stat_ref)

        @pl.when(j == 0)
        def _():
            acc[...] = jnp.zeros_like(acc)

        da = _dot_nt(dzb_ref[...], w2_ref[...])
        du = (da * (2.0 * jnp.maximum(u_ref[...].astype(F32), 0.0))).astype(BF16)
        du_ref[...] = du
        part = _dot_nt(du[:, 0:tf], w1_ref[0])
        for s in range(1, FF_STEP):
            part = part + _dot_nt(du[:, s * tf:(s + 1) * tf], w1_ref[s])
        acc[...] += part

        @pl.when(j == nf - 1)
        def _():
            dh = ALPHA * dz_ref[...] + acc[...]
            xhat = xh_ref[...]
            dz1 = _ln_bwd(dh, xhat, rs_ref[:, 0:1], g_ref[...])
            dz1_ref[...] = dz1
            dz1b_ref[...] = dz1.astype(BF16)
            stat_ref[0:1, :] += jnp.sum(dh * xhat, axis=0, keepdims=True)
            stat_ref[1:2, :] += jnp.sum(dh, axis=0, keepdims=True)

    return pl.pallas_call(
        body, name="bwd_mlp", grid=(tokens // tm, nf),
        out_shape=(jax.ShapeDtypeStruct((tokens, D_FF), BF16), jax.ShapeDtypeStruct((tokens, D_MODEL), F32),
                   jax.ShapeDtypeStruct((tokens, D_MODEL), BF16), jax.ShapeDtypeStruct((8, D_MODEL), F32)),
        in_specs=[_rows(tm, D_MODEL), _rows(tm, D_MODEL), pl.BlockSpec((tm, FF_STEP * tf), lambda i, j: (i, j)),
                  _rows(tm, D_MODEL), _rows(tm, LANES),
                  pl.BlockSpec((FF_STEP, D_MODEL, tf), lambda i, j: (j, 0, 0)),
                  pl.BlockSpec((FF_STEP * tf, D_MODEL), lambda i, j: (j, 0)),
                  _full(ln_g.shape)],
        out_specs=(pl.BlockSpec((tm, FF_STEP * tf), lambda i, j: (i, j)), _rows(tm, D_MODEL), _rows(tm, D_MODEL),
                   _full((8, D_MODEL))),
        scratch_shapes=[pltpu.VMEM((tm, D_MODEL), F32)],
        compiler_params=_cp("arbitrary", "arbitrary"),
    )(dz2, dz2b, u, xhat1, rstd1, w_ff1, w_ff2, ln_g)


def _bwd_mix(dz1b, gates, y_a, y_b, b_gate, w_oa, w_ob, w_out, *, seq):
    tokens = dz1b.shape[0]
    tm = min(MIX_TILE, seq)

    def body(dz_ref, gt_ref, ya_ref, yb_ref, bg_ref, woa_ref, wob_ref, wout_ref,
             dgt_ref, dya_ref, dyb_ref, doa_ref, dob_ref, stat_ref):
        @pl.when(pl.program_id(0) == 0)
        def _():
            stat_ref[...] = jnp.zeros_like(stat_ref)

        dmix = _dot_nt(dz_ref[...], wout_ref[...])
        for k, (y_ref, w_ref, dy_ref, do_ref) in enumerate(((ya_ref, woa_ref, dya_ref, doa_ref), (yb_ref, wob_ref, dyb_ref, dob_ref))):
            g = _sigmoid(gt_ref[:, k * D_MODEL:(k + 1) * D_MODEL].astype(F32) + bg_ref[k:k + 1, :])
            dgate = dmix * y_ref[...].astype(F32) * g * (1.0 - g)
            dgt_ref[:, k * D_MODEL:(k + 1) * D_MODEL] = dgate.astype(BF16)
            stat_ref[k:k + 1, :] += jnp.sum(dgate, axis=0, keepdims=True)
            dy = (dmix * g).astype(BF16)
            dy_ref[...] = dy
            do_ref[...] = _dot_nt(dy, w_ref[...]).astype(BF16)

    outs = [(2 * D_MODEL, BF16), (D_MODEL, BF16), (D_MODEL, BF16), (DIL_WIDTH, BF16), (DIL_WIDTH, BF16)]
    return pl.pallas_call(
        body, name="bwd_mix", grid=(tokens // tm,),
        out_shape=tuple(jax.ShapeDtypeStruct((tokens, w), dt) for w, dt in outs) + (jax.ShapeDtypeStruct((8, D_MODEL), F32),),
        in_specs=[_rows(tm, D_MODEL), _rows(tm, 2 * D_MODEL), _rows(tm, D_MODEL), _rows(tm, D_MODEL),
                  _full(b_gate.shape), _full(w_oa.shape), _full(w_ob.shape), _full(w_out.shape)],
        out_specs=tuple(_rows(tm, w) for w, _ in outs) + (_full((8, D_MODEL)),),
        compiler_params=_cp("arbitrary"),
    )(dz1b, gates, y_a, y_b, b_gate, w_oa, w_ob, w_out)


def _bwd_proj(dqp, dkp, dvm, dq_d, dk_d, dv_d, dgates, dz1, low, w_in_ext, w1, wk, wv, g_q, g_kv, cext, sext, cs128, *, seq):
    tokens = dz1.shape[0]
    tm = min(TOKEN_TILE, seq)
    ns = seq // tm

    def body(dqp_ref, dkp_ref, dvm_ref, dqd_ref, dkd_ref, dvd_ref, dgt_ref, dz_ref, low_ref, win_ref, w1_ref, wk_ref,
             wv_ref, gq_ref, gkv_ref, c_ref, s_ref, cs_ref, dx_ref, dproj_ref, da_ref, dkn_ref, stat_ref):
        @pl.when(pl.program_id(0) == 0)
        def _():
            stat_ref[...] = jnp.zeros_like(stat_ref)

        low = low_ref[...]
        d_a = _rope_slabs(dqp_ref[...].astype(F32), c_ref[...], s_ref[...], True).astype(BF16)
        da_ref[...] = d_a
        q_a = low[:, 0:Q_LORA]
        _, rq = _rms(q_a, gq_ref[...])
        dq_a, gq_terms = _rms_bwd(_dot_nt(d_a, w1_ref[...]), q_a, rq, gq_ref[...])
        kv_a = low[:, Q_LORA:Q_LORA + KV_LORA]
        _, rkv = _rms(kv_a, gkv_ref[...])
        dkn = jnp.concatenate([dkp_ref[:, p * PAIR_W:p * PAIR_W + LANES] for p in range(N_PAIRS)], axis=1)
        dkn_ref[...] = dkn
        dkv_a, gkv_terms = _rms_bwd(_dot_nt(dkn, wk_ref[...]) + _dot_nt(dvm_ref[...], wv_ref[...]), kv_a, rkv, gkv_ref[...])
        dkr = sum(dkp_ref[:, p * PAIR_W + LANES:(p + 1) * PAIR_W].astype(F32) for p in range(N_PAIRS))
        dkr = dkr + pltpu.roll(dkr, LANES - ROPE, 1)
        dkr = jnp.where(lax.broadcasted_iota(I32, dkr.shape, 1) < ROPE, dkr, 0.0)
        dkr = (dkr + pltpu.roll(dkr, ROPE, 1)) * cs_ref[...]
        stat_ref[0:1, 0:Q_LORA] += jnp.sum(gq_terms, axis=0, keepdims=True)
        stat_ref[1:2, 0:KV_LORA] += jnp.sum(gkv_terms, axis=0, keepdims=True)
        dproj_ref[:, 0:Q_LORA] = dq_a.astype(BF16)
        dproj_ref[:, Q_LORA:Q_LORA + KV_LORA] = dkv_a.astype(BF16)
        dproj_ref[:, Q_LORA + KV_LORA:LOW_W] = dkr.astype(BF16)
        dproj_ref[:, LOW_W:LOW_W + DIL_WIDTH] = dqd_ref[...]
        dproj_ref[:, LOW_W + DIL_WIDTH:LOW_W + 2 * DIL_WIDTH] = dkd_ref[...]
        dproj_ref[:, LOW_W + 2 * DIL_WIDTH:LOW_W + 3 * DIL_WIDTH] = dvd_ref[...]
        dproj_ref[:, LOW_W + 3 * DIL_WIDTH:] = dgt_ref[...]
        dx_ref[...] = ALPHA * dz_ref[...] + _dot_nt(dproj_ref[...], win_ref[...])

    wide = N_PAIRS * PAIR_W
    return pl.pallas_call(
        body, name="bwd_proj", grid=(tokens // tm,),
        out_shape=(jax.ShapeDtypeStruct((tokens, D_MODEL), F32), jax.ShapeDtypeStruct((tokens, IN_EXT), BF16),
                   jax.ShapeDtypeStruct((tokens, wide), BF16), jax.ShapeDtypeStruct((tokens, N_HEADS * NOPE), BF16),
                   jax.ShapeDtypeStruct((8, D_MODEL), F32)),
        in_specs=[_rows(tm, wide), _rows(tm, wide), _rows(tm, DIL_WIDTH), _rows(tm, DIL_WIDTH), _rows(tm, DIL_WIDTH),
                  _rows(tm, DIL_WIDTH), _rows(tm, 2 * D_MODEL),
                  _rows(tm, D_MODEL), _rows(tm, LOW_W), _full(w_in_ext.shape), _full(w1.shape),
                  _full(wk.shape), _full(wv.shape), _full(g_q.shape), _full(g_kv.shape),
                  pl.BlockSpec((tm, LANES), lambda i: (i % ns, 1)), pl.BlockSpec((tm, LANES), lambda i: (i % ns, 1)),
                  pl.BlockSpec((tm, LANES), lambda i: (i % ns, 0))],
        out_specs=(_rows(tm, D_MODEL), _rows(tm, IN_EXT), _rows(tm, wide), _rows(tm, N_HEADS * NOPE), _full((8, D_MODEL))),
        compiler_params=_cp("arbitrary"),
    )(dqp, dkp, dvm, dq_d, dk_d, dv_d, dgates, dz1, low, w_in_ext, w1, wk, wv, g_q, g_kv, cext, sext, cs128)


def _wgrad(a, b, name, square_relu=False, by_shard=False):
    tokens, ka = a.shape
    n = b.shape[1]
    if ka <= 512 or ka % 512 == 0:
        tka = min(ka, 512)
    else:
        tka = max(w for w in range(LANES, min(ka, 2304) + 1, LANES) if ka % w == 0)
    shard = n // N_DEV
    tn = WGRAD_SHARDS * shard if by_shard else max(w for w in range(LANES, min(n, 2304) + 1, LANES) if n % w == 0)
    tt = min(tokens, 2048 if tka <= 512 else 1024)
    nt = tokens // tt

    def body(a_ref, b_ref, o_ref, acc):
        kt = pl.program_id(2)

        @pl.when(kt == 0)
        def _():
            acc[...] = jnp.zeros_like(acc)

        at = a_ref[...]
        if square_relu:
            at = jnp.square(jnp.maximum(at.astype(F32), 0.0)).astype(BF16)
        acc[...] += _dot_tn(at, b_ref[...])

        @pl.when(kt == nt - 1)
        def _():
            if by_shard:
                for s in range(WGRAD_SHARDS):
                    o_ref[s] = acc[:, s * shard:(s + 1) * shard].astype(BF16)
            else:
                o_ref[...] = acc[...].astype(BF16)

    if by_shard:
        out_shape, out_spec = (N_DEV, ka, shard), pl.BlockSpec((WGRAD_SHARDS, tka, shard), lambda i, j, k: (j, i, 0))
    else:
        out_shape, out_spec = (ka, n), pl.BlockSpec((tka, tn), lambda i, j, k: (i, j))
    return pl.pallas_call(
        body, name=name, grid=(ka // tka, n // tn, nt), out_shape=jax.ShapeDtypeStruct(out_shape, BF16),
        in_specs=[pl.BlockSpec((tt, tka), lambda i, j, k: (k, i)), pl.BlockSpec((tt, tn), lambda i, j, k: (k, j))],
        out_specs=out_spec,
        scratch_shapes=[pltpu.VMEM((tka, tn), F32)],
        compiler_params=_cp("parallel", "parallel", "arbitrary"),
    )(a, b)


def _adam_math(w, g, m, v):
    m = ADAM_B1 * m + (1.0 - ADAM_B1) * g
    v = ADAM_B2 * v + (1.0 - ADAM_B2) * jnp.square(g)
    m_hat = m / (1.0 - ADAM_B1 ** ADAM_STEP)
    v_hat = v / (1.0 - ADAM_B2 ** ADAM_STEP)
    return -ADAM_LR * (m_hat / (jnp.sqrt(v_hat) + ADAM_EPS) + ADAM_WD * w), m, v


def _adamw(items, name):
    steps = min(_tiles(*w.shape)[0] for w, *_ in items)
    n_items = len(items)

    def body(slot_ref, *refs):
        ins, outs = refs[:5 * n_items], refs[5 * n_items:]
        for k, (_, _, _, _, parts) in enumerate(items):
            w_ref, m_ref, v_ref, own_ref, p_ref = ins[5 * k:5 * k + 5]
            g_ref, d_ref, nm_ref, nv_ref = outs[4 * k:4 * k + 4]
            g = own_ref[...].astype(F32)
            for d in range(parts.shape[0]):
                g = g + p_ref[d].astype(F32)
            g_ref[...] = g
            d_ref[...], nm_ref[...], nv_ref[...] = _adam_math(w_ref[...], g, m_ref[...], v_ref[...])

    x, y, c = _place()
    in_specs, out_specs, out_shape, args = [], [], [], []
    for w, m, v, own, parts in items:
        rows, cols = w.shape
        _, tile, at = _tiles(rows, cols, steps)
        blk = pl.BlockSpec(tile, lambda i, slot, at=at: at(i))
        own_blk = blk if own.ndim == 2 else pl.BlockSpec((None, *tile), lambda i, slot, at=at: (slot[0], *at(i)))
        in_specs += [blk, blk, blk, own_blk, pl.BlockSpec((parts.shape[0], *tile), lambda i, slot, at=at: (0, *at(i)))]
        out_specs += [blk] * 4
        out_shape += [jax.ShapeDtypeStruct((rows, cols), F32)] * 4
        args += [w, m, v, own, parts]
    out = pl.pallas_call(
        body, name=name,
        grid_spec=pltpu.PrefetchScalarGridSpec(num_scalar_prefetch=1, grid=(steps,), in_specs=in_specs, out_specs=out_specs),
        out_shape=out_shape, compiler_params=_cp("parallel"),
    )(jnp.reshape(4 * x + 2 * y + c, (1,)).astype(I32), *args)
    return [tuple(out[4 * k:4 * k + 4]) for k in range(n_items)]


def _adamw_small(parts, w, m, v):
    _, rows, cols = parts.shape

    def body(p_ref, w_ref, m_ref, v_ref, g_ref, d_ref, nm_ref, nv_ref):
        g = p_ref[0]
        for d in range(1, N_DEV):
            g = g + p_ref[d]
        g_ref[...] = g
        d_ref[...], nm_ref[...], nv_ref[...] = _adam_math(w_ref[...], g, m_ref[...], v_ref[...])

    return pl.pallas_call(
        body, name="adamw_replicated", out_shape=(jax.ShapeDtypeStruct((rows, cols), F32),) * 4,
        in_specs=[_full(parts.shape)] + [_full((rows, cols))] * 3, out_specs=(_full((rows, cols)),) * 4, grid=(1,),
        compiler_params=_cp("arbitrary"),
    )(parts, w, m, v)


def _pad_rows(a2d, mult):
    pad = (-a2d.shape[-2]) % mult
    return jnp.pad(a2d, [(0, 0)] * (a2d.ndim - 2) + [(0, pad), (0, 0)]) if pad else a2d


def _pad_cols(a):
    pad = (-a.shape[-1]) % LANES
    return jnp.pad(a, [(0, 0)] * (a.ndim - 1) + [(0, pad)]) if pad else a


def _rot_cols(w):
    half = ROPE // 2
    return jnp.concatenate([-w[..., half:], w[..., :half]], axis=-1)


def _unrot_cols(dw):
    half = ROPE // 2
    return jnp.concatenate([dw[..., half:], -dw[..., :half]], axis=-1)


def _from_col_shards(stacked):
    return stacked.transpose(1, 0, 2).reshape(stacked.shape[1], -1)


def _to_col_shards(full):
    r = full.shape[0]
    return full.reshape(r, N_DEV, -1).transpose(1, 0, 2)


def _rope_tables(seq):
    half = ROPE // 2
    inv = jnp.power(ROPE_THETA, -jnp.arange(half, dtype=F32) / half)
    ang = jnp.arange(seq, dtype=F32)[:, None] * inv[None, :]
    cos = jnp.concatenate([jnp.cos(ang)] * 2, axis=1)
    sin = jnp.concatenate([jnp.sin(ang)] * 2, axis=1)
    ones, zeros = jnp.ones((seq, 2 * NOPE), F32), jnp.zeros((seq, 2 * NOPE), F32)
    pad = jnp.zeros((seq, PAIR_W - 2 * NOPE - 2 * ROPE), F32)
    cext = jnp.concatenate([ones, cos, cos, pad], axis=1)
    sext = jnp.concatenate([zeros, sin, sin, pad], axis=1)
    cs128 = jnp.concatenate([cos, sin, jnp.zeros((seq, LANES - 2 * ROPE), F32)], axis=1)
    return cext, sext, cs128


def _pair_slabs(nope, rope):
    k = nope.shape[0]
    nope = nope.reshape(k, N_PAIRS, 2 * NOPE)
    rope = rope.reshape(k, N_PAIRS, 2 * ROPE)
    pad = jnp.zeros((k, N_PAIRS, PAIR_W - 2 * NOPE - 2 * ROPE), nope.dtype)
    return jnp.concatenate([nope, rope, pad], axis=2).reshape(k, N_PAIRS * PAIR_W)


def _split_slabs(slabs):
    k = slabs.shape[0]
    s = slabs.reshape(k, N_PAIRS, PAIR_W)
    return s[:, :, :2 * NOPE].reshape(k, N_HEADS, NOPE), s[:, :, 2 * NOPE:2 * NOPE + 2 * ROPE].reshape(k, N_HEADS, ROPE)


def kernel(x, w_in, b_gate, g_q_a, w_uq, g_kv_a, w_ukv, w_o_mla, w_o_dil, w_out, ln1_g, ln1_b, w_ff1, w_ff2, ln2_g, ln2_b, loss_target, m_w_in, m_b_gate, m_g_q_a, m_w_uq, m_g_kv_a, m_w_ukv, m_w_o_mla, m_w_o_dil, m_w_out, m_ln1_g, m_ln1_b, m_w_ff1, m_w_ff2, m_ln2_g, m_ln2_b, v_w_in, v_b_gate, v_g_q_a, v_w_uq, v_g_kv_a, v_w_ukv, v_w_o_mla, v_w_o_dil, v_w_out, v_ln1_g, v_ln1_b, v_w_ff1, v_w_ff2, v_ln2_g, v_ln2_b):
    batch, seq, _ = x.shape
    tokens = batch * seq
    weights = dict(w_in=w_in, w_uq=w_uq, w_ukv=w_ukv, w_o_mla=w_o_mla, w_o_dil=w_o_dil, w_out=w_out, w_ff1=w_ff1, w_ff2=w_ff2, b_gate=b_gate)
    mom_m = dict(w_in=m_w_in, w_uq=m_w_uq, w_ukv=m_w_ukv, w_o_mla=m_w_o_mla, w_o_dil=m_w_o_dil, w_out=m_w_out, w_ff1=m_w_ff1, w_ff2=m_w_ff2, b_gate=m_b_gate)
    mom_v = dict(w_in=v_w_in, w_uq=v_w_uq, w_ukv=v_w_ukv, w_o_mla=v_w_o_mla, w_o_dil=v_w_o_dil, w_out=v_w_out, w_ff1=v_w_ff1, w_ff2=v_w_ff2, b_gate=v_b_gate)

    first = ["w_in", "w_uq", "w_ukv"]
    widths = [weights[n].shape[2] for n in first]
    shards = [weights["w_in"][0].T.astype(BF16)] + [_pad_cols(weights[n][0].astype(BF16)) for n in first[1:]]
    g_in, g_uq, g_ukv = _run_comm(_Gather(shards), shards, "all_gather_first_weights")
    g_uq, g_ukv = g_uq[:, :, :widths[1]], g_ukv[:, :, :widths[2]]

    s1, s2, n_in = Q_LORA + KV_LORA, Q_LORA + KV_LORA + ROPE, N_DEV * widths[0]

    def w_in_cols(lo, hi):
        out = []
        while lo < hi:
            d, off = divmod(lo, widths[0])
            take = min(hi - lo, widths[0] - off)
            out.append(g_in[d][off:off + take].T)
            lo += take
        return out

    w_in_ext = jnp.concatenate(w_in_cols(0, s2) + [_rot_cols(jnp.concatenate(w_in_cols(s1, s2), axis=1)),
                                                   jnp.zeros((D_MODEL, LOW_W - s2 - ROPE), BF16)] + w_in_cols(s2, n_in), axis=1)
    uq = _from_col_shards(g_uq).reshape(Q_LORA, N_HEADS, NOPE + ROPE)
    w1 = _pair_slabs(uq[:, :, :NOPE], uq[:, :, NOPE:])
    ukv = _from_col_shards(g_ukv).reshape(KV_LORA, N_HEADS, NOPE + HEAD_V)
    wk = ukv[:, :, :NOPE].reshape(KV_LORA, N_HEADS * NOPE)
    wv = ukv[:, :, NOPE:].reshape(KV_LORA, N_HEADS * HEAD_V)
    cext, sext, cs128 = _rope_tables(seq)
    dil_bias = _dilated_bias_table(seq)
    no_bias = jnp.zeros((1, 8, LANES), F32)

    x2 = x.reshape(tokens, D_MODEL)
    low, gates, qkvd, qp, kp, vm, qn, kvn, xb = _fwd_proj(x2, w_in_ext, w1, wk, wv, g_q_a, g_kv_a, cext, sext, cs128, seq=seq)
    bg = b_gate[0]
    bg_hi = bg.astype(BF16)
    bg_lo = (bg - bg_hi.astype(F32)).astype(BF16)
    later = [weights[n][0].astype(BF16) for n in ("w_o_mla", "w_o_dil", "w_out", "w_ff1", "w_ff2")]
    later.append(_pad_rows(jnp.concatenate([bg_hi, bg_lo], axis=0), 16))
    mla = dict(batch=batch, seq=seq, width=PAIR_W, col0=(0, 0, 0), dilated=False, scale=MLA_SCALE)
    dil = dict(batch=batch, seq=seq, width=LANES, col0=(0, N_PAIRS, 2 * N_PAIRS), dilated=True, scale=DIL_SCALE)
    o_a, lse_a, g_oa, g_ob, g_out, g_ff1, g_ff2, g_bg = _attn_fwd(
        qp, kp, vm, no_bias, name="mla_attention_fwd", comm=_Gather(later), comm_arrays=later, **mla)
    o_b, lse_b = _attn_fwd(qkvd, qkvd, qkvd, dil_bias, name="dilated_attention_fwd", **dil)
    w_oa, w_ob = _from_col_shards(g_oa), _from_col_shards(g_ob)
    w_out_full = g_out.reshape(D_MODEL, D_MODEL)
    w_ff2_full = g_ff2.reshape(D_FF, D_MODEL)
    bg_parts = g_bg.astype(F32)
    b_gate_full = _from_col_shards(bg_parts[:, 0:2] + bg_parts[:, 2:4])
    hb, xhat1, rstd1, y_a, y_b, mix = _fwd_mix(o_a, o_b, gates, x2, b_gate_full, w_oa, w_ob, w_out_full, ln1_g, ln1_b, seq=seq)
    u, dz2, dz2b, stat2 = _fwd_mlp(hb, xhat1, loss_target.reshape(tokens, D_MODEL), g_ff1, w_ff2_full, ln1_g, ln1_b, ln2_g, ln2_b, seq=seq)

    du, dz1, dz1b, stat1 = _bwd_mlp(dz2, dz2b, u, xhat1, rstd1, g_ff1, w_ff2_full, ln1_g, seq=seq)
    dw_ff = [_wgrad(hb, du, "wgrad_ff1", by_shard=True),
             _wgrad(u, dz2b, "wgrad_ff2", square_relu=True).reshape(N_DEV, FF_SHARD, D_MODEL)]
    dgates, dy_a, dy_b, do_a, do_b, stat_g = _bwd_mix(dz1b, gates, y_a, y_b, b_gate_full, w_oa, w_ob, w_out_full, seq=seq)
    dqp, dkp, dvm, r_ff1, r_ff2 = _attn_bwd(qp, kp, vm, o_a, do_a, lse_a, no_bias, name="mla_attention_bwd",
                                            comm=_Scatter(dw_ff), comm_arrays=dw_ff, **mla)
    dw_mid = [_to_col_shards(_wgrad(o_a, dy_a, "wgrad_o_mla")), _to_col_shards(_wgrad(o_b, dy_b, "wgrad_o_dil")),
              _wgrad(mix, dz1b, "wgrad_out").reshape(N_DEV, D_MODEL // N_DEV, D_MODEL),
              _pad_rows(_to_col_shards(stat_g[0:2]).astype(BF16), 16)]
    dq_d, dk_d, dv_d, r_oa, r_ob, r_out, r_bg = _attn_bwd(qkvd, qkvd, qkvd, o_b, do_b, lse_b, dil_bias, name="dilated_attention_bwd",
                                                          comm=_Scatter(dw_mid), comm_arrays=dw_mid, **dil)
    grad_x, dproj, d_a, dkn, stat_r = _bwd_proj(dqp, dkp, dvm, dq_d, dk_d, dv_d, dgates, dz1, low, w_in_ext, w1, wk, wv,
                                                g_q_a, g_kv_a, cext, sext, cs128, seq=seq)

    dw_in_ext = _wgrad(dproj, xb, "wgrad_in")
    dw1 = _wgrad(qn, d_a, "wgrad_uq")
    dwk = _wgrad(kvn, dkn, "wgrad_ukv_k")
    dwv = _wgrad(kvn, dvm, "wgrad_ukv_v")
    dw_kr = dw_in_ext[s1:s2] + _unrot_cols(dw_in_ext[s2:s2 + ROPE].T).T

    def dw_in_cols(lo, hi):
        out = []
        for a, b, piece in ((0, s1, lambda u, v: dw_in_ext[u:v]), (s1, s2, lambda u, v: dw_kr[u - s1:v - s1]),
                            (s2, n_in, lambda u, v: dw_in_ext[u + LOW_W - s2:v + LOW_W - s2])):
            if max(lo, a) < min(hi, b):
                out.append(piece(max(lo, a), min(hi, b)))
        return out

    dw_in = jnp.stack([jnp.concatenate(dw_in_cols(d * widths[0], (d + 1) * widths[0]), axis=0) for d in range(N_DEV)])
    n1, r1 = _split_slabs(dw1)
    dw_uq = jnp.concatenate([n1, r1], axis=2).reshape(Q_LORA, N_HEADS * (NOPE + ROPE))
    dw_ukv = jnp.concatenate([dwk.reshape(KV_LORA, N_HEADS, NOPE), dwv.reshape(KV_LORA, N_HEADS, HEAD_V)], axis=2).reshape(KV_LORA, N_HEADS * (NOPE + HEAD_V))
    last = [dw_in] + [_pad_cols(_to_col_shards(dw)) for dw in (dw_uq, dw_ukv)]
    theirs = _rs_sibling(last, "rs_last_sibling_exchange")
    sums = [_pair_sum(a, b, "rs_last_pair_sum_" + n) for a, b, n in zip(last, theirs, first)]
    partial = jnp.concatenate([stat_r[0:1, :Q_LORA], stat_r[1:2, :KV_LORA], stat1[0:1], stat1[1:2], stat2[0:1], stat2[1:2],
                               stat2[2:3, :LANES]], axis=1)
    partial = _pad_rows(partial.reshape(-1, LANES), 8)
    rest = [s[1] for s in sums]
    got_in, got_uq, got_ukv, every = _run_comm(_Plans([_ChipExchange(rest), _Gather([partial])]), rest + [partial],
                                               "rs_last_chip_exchange")

    upd = {}
    early = ["w_ff1", "w_ff2", "w_out", "w_o_mla", "w_o_dil"]
    items = [(weights[n][0], mom_m[n][0], mom_v[n][0], own, parts) for n, own, parts in
             zip(early, (dw_ff[0], dw_ff[1], dw_mid[2], dw_mid[0], dw_mid[1]), (r_ff1, r_ff2, r_out, r_oa, r_ob))]
    upd.update(zip(early, _adamw(items, "adamw_early_weights")))
    (in_t,) = _adamw([(weights["w_in"][0].T, mom_m["w_in"][0].T, mom_v["w_in"][0].T, sums[0][0], got_in)], "adamw_w_in")
    upd["w_in"] = tuple(a.T for a in in_t)
    for n, w, (own, _), parts in zip(first[1:], widths[1:], sums[1:], (got_uq, got_ukv)):
        (upd[n],) = _adamw([(weights[n][0], mom_m[n][0], mom_v[n][0], own[:, :w], parts[:, :, :w])], "adamw_" + n)
    (bg_upd,) = _adamw([(_pad_rows(b_gate[0], 16), _pad_rows(m_b_gate[0], 16), _pad_rows(v_b_gate[0], 16), dw_mid[3], r_bg)],
                       "adamw_b_gate")
    upd["b_gate"] = tuple(t[0:2] for t in bg_upd)

    small_w = [g_q_a, g_kv_a, ln1_g, ln1_b, ln2_g, ln2_b]
    small_m = [m_g_q_a, m_g_kv_a, m_ln1_g, m_ln1_b, m_ln2_g, m_ln2_b]
    small_v = [v_g_q_a, v_g_kv_a, v_ln1_g, v_ln1_b, v_ln2_g, v_ln2_b]
    small_widths = [a.shape[1] for a in small_w]

    def as_rows(vecs, extra):
        flat = jnp.concatenate(vecs + [jnp.zeros((1, extra), F32)], axis=1)
        return _pad_rows(flat.reshape(-1, LANES), 8)

    g_s, d_s, nm_s, nv_s = _adamw_small(every, as_rows(small_w, LANES), as_rows(small_m, LANES), as_rows(small_v, LANES))

    def split_small(a):
        flat = a.reshape(1, -1)
        out, c0 = [], 0
        for w in small_widths:
            out.append(flat[:, c0:c0 + w])
            c0 += w
        return out, flat[0, c0]

    g_small, loss = split_small(g_s)
    small = [g_small, split_small(d_s)[0], split_small(nm_s)[0], split_small(nv_s)[0]]

    order = ["w_in", "b_gate", "g_q_a", "w_uq", "g_kv_a", "w_ukv", "w_o_mla", "w_o_dil", "w_out", "ln1_g", "ln1_b", "w_ff1", "w_ff2", "ln2_g", "ln2_b"]
    small_names = ["g_q_a", "g_kv_a", "ln1_g", "ln1_b", "ln2_g", "ln2_b"]

    def pick(kind):
        return [small[kind][small_names.index(n)] if n in small_names else upd[n][kind][None] for n in order]

    return (loss, grad_x.reshape(batch, seq, D_MODEL), *pick(0), *pick(1), *pick(2), *pick(3))
```

```python
import functools
import math

import jax
import jax.numpy as jnp
from jax import lax
from jax.experimental import pallas as pl
from jax.experimental.pallas import tpu as pltpu

F32 = jnp.float32
BF16 = jnp.bfloat16
I32 = jnp.int32

D_MODEL = 1024
N_HEADS = 8
NOPE = 64
ROPE = 32
HEAD_V = 64
Q_LORA = 384
KV_LORA = 256
DIL_WIDTH = 512
D_FF = 4096
ROPE_THETA = 10000.0
LN_EPS = 1e-5
RMS_EPS = 1e-6
NEG = -1e30
ALPHA = 2.0 ** 0.25
MLA_SCALE = (NOPE + ROPE) ** -0.5
DIL_SCALE = 64 ** -0.5
ADAM_LR, ADAM_B1, ADAM_B2, ADAM_EPS, ADAM_WD, ADAM_STEP = 0.001, 0.9, 0.999, 1e-08, 0.01, 10

LANES = 128
PAIR_W = 256
N_PAIRS = N_HEADS // 2
LOW_W = 768
IN_EXT = LOW_W + 3 * DIL_WIDTH + 2 * D_MODEL
N_DEV = 8
FF_SHARD = D_FF // N_DEV
FF_STEP = 4
WGRAD_SHARDS = 4
TOKEN_TILE = 256
MIX_TILE = 512
ATTN_TILE = 512
VMEM_LIMIT = 56 << 20

MESH = pl.DeviceIdType.MESH
ANY = pl.BlockSpec(memory_space=pl.ANY)
CHIP_FLIPS = ((0, 0), (0, 1), (1, 0), (1, 1))
PEER_FLIPS = tuple((fx, fy, fc) for fx in (0, 1) for fy in (0, 1) for fc in (0, 1))[1:]


def _cp(*sem):
    return pltpu.CompilerParams(dimension_semantics=sem or None, vmem_limit_bytes=VMEM_LIMIT)


def _full(shape):
    nd = len(shape)
    return pl.BlockSpec(shape, lambda *_: (0,) * nd)


def _rows(tm, width):
    return pl.BlockSpec((tm, width), lambda i, *_: (i, 0))


def _dot(a, b):
    return jnp.dot(a, b, preferred_element_type=F32)


def _dot_nt(a, b):
    return lax.dot_general(a, b, (((1,), (1,)), ((), ())), preferred_element_type=F32)


def _dot_tn(a, b):
    return lax.dot_general(a, b, (((0,), (0,)), ((), ())), preferred_element_type=F32)


def _sigmoid(z):
    return 1.0 / (1.0 + jnp.exp(-z))


def _place():
    return lax.axis_index("x"), lax.axis_index("y"), lax.axis_index("c")


def _flip(v, f):
    return 1 - v if f else v


class _Gather:
    def __init__(self, shards):
        self.n = len(shards)
        self.out_shape = [jax.ShapeDtypeStruct((N_DEV, *s.shape), s.dtype) for s in shards]
        self.scratch = [pltpu.SemaphoreType.DMA((7 * self.n,)), pltpu.SemaphoreType.DMA((7 * self.n,)),
                        pltpu.SemaphoreType.DMA((self.n,))]

    def _copies(self, what, srcs, dsts, send, recv, local):
        x, y, c = _place()
        chips = [(_flip(x, fx), _flip(y, fy)) for fx, fy in CHIP_FLIPS[1:]]
        out = []
        for a in range(self.n):
            def slot(px, py, pc, a=a):
                return dsts[a].at[4 * px + 2 * py + pc]

            def copy(k, block, to, src=None, a=a, slot=slot):
                return pltpu.make_async_remote_copy(
                    src_ref=slot(*block) if src is None else src, dst_ref=slot(*block),
                    send_sem=send.at[7 * a + k], recv_sem=recv.at[7 * a + k], device_id=to, device_id_type=MESH)

            if what == "mine":
                out.append(pltpu.make_async_copy(srcs[a], slot(x, y, c), local.at[a]))
            elif what == "first":
                out.append(copy(0, (x, y, c), (x, y, 1 - c), src=srcs[a]))
                out += [copy(1 + j, (x, y, c), (*chip, c), src=srcs[a]) for j, chip in enumerate(chips)]
            elif what == "landed":
                out += [copy(1 + j, (*chip, c), (x, y, c)) for j, chip in enumerate(chips)]
            elif what == "passed":
                out += [copy(4 + j, (*chip, c), (x, y, 1 - c)) for j, chip in enumerate(chips)]
            else:
                out.append(copy(0, (x, y, 1 - c), (x, y, c)))
                out += [copy(4 + j, (*chip, 1 - c), (x, y, c)) for j, chip in enumerate(chips)]
        return out

    def start(self, *refs):
        for cp in self._copies("first", *refs) + self._copies("mine", *refs):
            cp.start()

    def forward(self, *refs):
        for landed, passed in zip(self._copies("landed", *refs), self._copies("passed", *refs)):
            landed.wait_recv()
            passed.start()

    def finish(self, *refs):
        for cp in self._copies("from_sibling", *refs):
            cp.wait_recv()
        for cp in self._copies("first", *refs) + self._copies("passed", *refs):
            cp.wait_send()
        for cp in self._copies("mine", *refs):
            cp.wait()


class _Scatter:
    def __init__(self, arrays):
        self.n = len(arrays)
        self.out_shape = [jax.ShapeDtypeStruct((7, *a.shape[1:]), a.dtype) for a in arrays]
        self.scratch = [pltpu.SemaphoreType.DMA((7 * self.n,)), pltpu.SemaphoreType.DMA((7 * self.n,))]

    def _copies(self, srcs, dsts, send, recv):
        x, y, c = _place()
        out = []
        for a in range(self.n):
            for k, (fx, fy, fc) in enumerate(PEER_FLIPS):
                px, py, pc = _flip(x, fx), _flip(y, fy), _flip(c, fc)
                out.append(pltpu.make_async_remote_copy(
                    src_ref=srcs[a].at[4 * px + 2 * py + pc], dst_ref=dsts[a].at[k],
                    send_sem=send.at[7 * a + k], recv_sem=recv.at[7 * a + k], device_id=(px, py, pc), device_id_type=MESH))
        return out

    def start(self, *refs):
        for cp in self._copies(*refs):
            cp.start()

    def forward(self, *refs):
        pass

    def finish(self, *refs):
        for cp in self._copies(*refs):
            cp.wait_send()
        for cp in self._copies(*refs):
            cp.wait_recv()


class _ChipExchange:
    def __init__(self, arrays):
        self.n = len(arrays)
        self.out_shape = [jax.ShapeDtypeStruct(a.shape, a.dtype) for a in arrays]
        self.scratch = [pltpu.SemaphoreType.DMA((3 * self.n,)), pltpu.SemaphoreType.DMA((3 * self.n,))]

    def _copies(self, srcs, dsts, send, recv):
        x, y, c = _place()
        return [pltpu.make_async_remote_copy(
            src_ref=srcs[a].at[k], dst_ref=dsts[a].at[k], send_sem=send.at[3 * a + k], recv_sem=recv.at[3 * a + k],
            device_id=(_flip(x, fx), _flip(y, fy), c), device_id_type=MESH)
            for a in range(self.n) for k, (fx, fy) in enumerate(CHIP_FLIPS[1:])]

    def start(self, *refs):
        for cp in self._copies(*refs):
            cp.start()

    def forward(self, *refs):
        pass

    def finish(self, *refs):
        for cp in self._copies(*refs):
            cp.wait_send()
        for cp in self._copies(*refs):
            cp.wait_recv()


class _Plans:
    def __init__(self, plans):
        self.plans = plans
        self.n = sum(p.n for p in plans)
        self.out_shape = [s for p in plans for s in p.out_shape]
        self.scratch = [s for p in plans for s in p.scratch]

    def _each(self, phase, srcs, dsts, *sems):
        i0 = s0 = 0
        for p in self.plans:
            getattr(p, phase)(srcs[i0:i0 + p.n], dsts[i0:i0 + p.n], *sems[s0:s0 + len(p.scratch)])
            i0, s0 = i0 + p.n, s0 + len(p.scratch)

    def start(self, *refs):
        self._each("start", *refs)

    def forward(self, *refs):
        self._each("forward", *refs)

    def finish(self, *refs):
        self._each("finish", *refs)


def _run_comm(comm, arrays, name):
    n = comm.n

    def body(*refs):
        args = (refs[:n], refs[n:2 * n], *refs[2 * n:])
        comm.start(*args)
        comm.forward(*args)
        comm.finish(*args)

    return pl.pallas_call(body, name=name, out_shape=comm.out_shape, in_specs=[ANY] * n, out_specs=[ANY] * n,
                          scratch_shapes=comm.scratch)(*arrays)


def _rs_sibling(arrays, name):
    n = len(arrays)

    def body(*refs):
        srcs, got, (send, recv) = refs[:n], refs[n:2 * n], refs[2 * n:]
        x, y, c = _place()
        copies = []
        for a in range(n):
            for r, (fx, fy) in enumerate(CHIP_FLIPS):
                chip = 2 * _flip(x, fx) + _flip(y, fy)
                copies.append(pltpu.make_async_remote_copy(
                    src_ref=srcs[a].at[2 * chip + 1 - c], dst_ref=got[a].at[r], send_sem=send.at[4 * a + r],
                    recv_sem=recv.at[4 * a + r], device_id=(x, y, 1 - c), device_id_type=MESH))
        for cp in copies:
            cp.start()
        for cp in copies:
            cp.wait_send()
        for cp in copies:
            cp.wait_recv()

    return pl.pallas_call(
        body, name=name, out_shape=[jax.ShapeDtypeStruct((4, *a.shape[1:]), a.dtype) for a in arrays],
        in_specs=[ANY] * n, out_specs=[ANY] * n,
        scratch_shapes=[pltpu.SemaphoreType.DMA((4 * n,)), pltpu.SemaphoreType.DMA((4 * n,))],
    )(*arrays)


def _chip_slots():
    x, y, c = _place()
    return jnp.stack([4 * _flip(x, fx) + 2 * _flip(y, fy) + c for fx, fy in CHIP_FLIPS]).astype(I32)


def _tiles(rows, cols, steps=4):
    if rows % (16 * steps) == 0:
        return steps, (rows // steps, cols), lambda i: (i, 0)
    if cols % (LANES * steps) == 0:
        return steps, (rows, cols // steps), lambda i: (0, i)
    return 1, (rows, cols), lambda i: (0, 0)


def _pair_sum(full, theirs, name):
    _, rows, cols = theirs.shape
    steps, tile, at = _tiles(rows, cols)

    def body(slots_ref, m0_ref, m1_ref, m2_ref, m3_ref, b_ref, own_ref, rest_ref):
        own_ref[...] = m0_ref[...].astype(F32) + b_ref[0].astype(F32)
        for k, m_ref in enumerate((m1_ref, m2_ref, m3_ref)):
            rest_ref[k] = (m_ref[...].astype(F32) + b_ref[k + 1].astype(F32)).astype(BF16)

    def mine(k):
        return pl.BlockSpec((None, *tile), lambda i, slots: (slots[k], *at(i)))

    return pl.pallas_call(
        body, name=name,
        grid_spec=pltpu.PrefetchScalarGridSpec(
            num_scalar_prefetch=1, grid=(steps,),
            in_specs=[mine(0), mine(1), mine(2), mine(3), pl.BlockSpec((4, *tile), lambda i, slots: (0, *at(i)))],
            out_specs=(pl.BlockSpec(tile, lambda i, slots: at(i)), pl.BlockSpec((3, *tile), lambda i, slots: (0, *at(i))))),
        out_shape=(jax.ShapeDtypeStruct((rows, cols), F32), jax.ShapeDtypeStruct((3, rows, cols), BF16)),
        compiler_params=_cp("parallel"),
    )(_chip_slots(), full, full, full, full, theirs)


def _head_lanes(width, h):
    lane = lax.broadcasted_iota(I32, (1, width), 1)
    if width == LANES:
        return (lane >= 64 * h) & (lane < 64 * h + 64)
    nope = (lane >= NOPE * h) & (lane < NOPE * h + NOPE)
    rope = (lane >= 2 * NOPE + ROPE * h) & (lane < 2 * NOPE + ROPE * h + ROPE)
    return nope | rope


def _dilated_bias_table(seq):
    t = min(ATTN_TILE, seq)
    nd = seq // t

    def body(o_ref):
        delta = pl.program_id(0) * t + lax.broadcasted_iota(I32, (t, t), 1) - lax.broadcasted_iota(I32, (t, t), 0)
        mult = ((delta <= 128).astype(I32) + (((delta & 3) == 0) & (delta <= 512)).astype(I32)
                + ((delta & 15) == 0).astype(I32))
        logm = jnp.where(mult == 3, math.log(3.0), jnp.where(mult == 2, math.log(2.0), 0.0))
        valid = (delta >= 0) & (mult > 0)
        dist = delta.astype(F32)
        for h in range(N_HEADS):
            o_ref[h] = jnp.where(valid, logm - 2.0 ** (-(h + 1)) * dist, NEG)

    return pl.pallas_call(
        body, name="dilated_bias_table", grid=(nd,), out_shape=jax.ShapeDtypeStruct((N_HEADS, nd, t, t), F32),
        out_specs=pl.BlockSpec((N_HEADS, None, t, t), lambda d: (0, d, 0, 0)),
        compiler_params=_cp("parallel"),
    )()


def _comm_hooks(comm, refs, n_in, n_out):
    if comm is None:
        return refs[:n_in], refs[n_in:n_in + n_out], refs[n_in + n_out:], None
    n = comm.n
    ins, srcs = refs[:n_in], refs[n_in:n_in + n]
    outs, dsts = refs[n_in + n:n_in + n + n_out], refs[n_in + n + n_out:n_in + 2 * n + n_out]
    rest = refs[n_in + 2 * n + n_out:]
    own = len(rest) - len(comm.scratch)
    return ins, outs, rest[:own], (srcs, dsts, *rest[own:])


def _attn_fwd(q, k, v, bias, *, batch, seq, width, col0, dilated, scale, name, comm=None, comm_arrays=()):
    t = min(ATTN_TILE, seq)
    nq = seq // t
    cq, ck, cv = col0
    pre = scale if dilated else 1.0
    steps = batch * N_PAIRS

    def body(*refs):
        (q_ref, k_ref, v_ref, bias_ref), (o_ref, lse_ref), (v_heads,), plan = _comm_hooks(comm, refs, 4, 2)
        step_no = pl.program_id(0) * N_PAIRS + pl.program_id(1)
        if plan:
            pl.when(step_no == 0)(lambda: comm.start(*plan))
            pl.when(step_no == (3 * steps) // 4)(lambda: comm.forward(*plan))
        v_all = v_ref[...].astype(F32)
        for h in (0, 1):
            v_heads[h] = jnp.transpose(jnp.where(_head_lanes(LANES, h), v_all, 0.0)).astype(BF16)
        top = lax.broadcasted_iota(I32, (LANES, t), 0) < HEAD_V
        causal = lax.broadcasted_iota(I32, (t, t), 0) <= lax.broadcasted_iota(I32, (t, t), 1)
        def heads(i):
            q2 = q_ref[pl.ds(pl.multiple_of(i * t, t), t), :]
            q2 = q2 * pre if dilated else q2
            return [jnp.where(_head_lanes(width, h), q2, jnp.zeros_like(q2)) for h in (0, 1)]

        def scores(qh, j):
            kj = k_ref[pl.ds(pl.multiple_of(j * t, t), t), :]
            return tuple(_dot_nt(kj, qh[h]) for h in (0, 1))

        lax.fori_loop(0, nq, functools.partial(query_tile, heads, scores, bias_ref, o_ref, lse_ref, v_heads, top, causal),
                      scores(heads(0), 0))
        if plan:
            pl.when(step_no == steps - 1)(lambda: comm.finish(*plan))

    def query_tile(heads, scores, bias_ref, o_ref, lse_ref, v_heads, top, causal, i, first):
        qs = pl.multiple_of(i * t, t)
        qh = heads(i)

        def step(j, carry, last):
            m0, l0, m1, l1, acc, s0, s1 = carry
            ahead = scores(heads(jnp.minimum(i + 1, nq - 1)), 0) if last else scores(qh, j + 1)
            ks = pl.multiple_of(j * t, t)
            new, alphas, pv = [], [], []
            for h, (m, l, s) in enumerate(((m0, l0, s0), (m1, l1, s1))):
                if dilated:
                    s = s + bias_ref[h, i - j]
                else:
                    s = s * scale
                    if last:
                        s = jnp.where(causal, s, NEG)
                m_new = jnp.maximum(m, jnp.max(s, axis=0, keepdims=True))
                a = jnp.exp(m - m_new)
                p = jnp.exp(s - m_new)
                new += [m_new, a * l + jnp.sum(p, axis=0, keepdims=True)]
                alphas.append(a)
                pv.append(_dot(v_heads[h, :, pl.ds(ks, t)], p.astype(BF16)))
            acc = jnp.where(top, alphas[0], alphas[1]) * acc + pv[0] + pv[1]
            return (*new, acc, *ahead)

        row = jnp.full((1, t), NEG, F32)
        zero = jnp.zeros((1, t), F32)
        init = (row, zero, row, zero, jnp.zeros((LANES, t), F32), *first)
        m0, l0, m1, l1, acc, *following = step(i, lax.fori_loop(0, i, functools.partial(step, last=False), init), True)
        o_ref[pl.ds(qs, t), :] = jnp.transpose(acc * jnp.where(top, 1.0 / l0, 1.0 / l1)).astype(BF16)
        r = lax.broadcasted_iota(I32, (8, t), 0)
        lse_ref[:, pl.ds(qs, t)] = jnp.where(r == 0, m0 + jnp.log(l0), jnp.where(r == 1, m1 + jnp.log(l1), 0.0))
        return tuple(following)

    bias_spec = (pl.BlockSpec((2, nq, t, t), lambda b, p: (p, 0, 0, 0)) if dilated
                 else pl.BlockSpec((None, 8, LANES), lambda b, p: (0, 0, 0)))
    n = comm.n if comm else 0
    return pl.pallas_call(
        body, name=name, grid=(batch, N_PAIRS),
        out_shape=[jax.ShapeDtypeStruct((batch * seq, DIL_WIDTH), BF16), jax.ShapeDtypeStruct((batch * N_PAIRS, 8, seq), F32)]
        + (comm.out_shape if comm else []),
        in_specs=[pl.BlockSpec((seq, width), lambda b, p: (b, cq + p)),
                  pl.BlockSpec((seq, width), lambda b, p: (b, ck + p)),
                  pl.BlockSpec((seq, LANES), lambda b, p: (b, cv + p)),
                  bias_spec] + [ANY] * n,
        out_specs=[pl.BlockSpec((seq, LANES), lambda b, p: (b, p)),
                   pl.BlockSpec((None, 8, seq), lambda b, p: (b * N_PAIRS + p, 0, 0))] + [ANY] * n,
        scratch_shapes=[pltpu.VMEM((2, LANES, seq), BF16)] + (comm.scratch if comm else []),
        compiler_params=_cp("arbitrary", "arbitrary") if comm else _cp("parallel", "parallel"),
    )(q, k, v, bias, *comm_arrays)


def _attn_bwd(q, k, v, o, do, lse, bias, *, batch, seq, width, col0, dilated, scale, name, comm=None, comm_arrays=()):
    t = min(ATTN_TILE, seq)
    nq = seq // t
    cq, ck, cv = col0
    pre = scale if dilated else 1.0
    dq_transposed = width == LANES
    steps = batch * N_PAIRS

    def body(*refs):
        ins, (dq_ref, dk_ref, dv_ref), (dq_acc, dk_acc, dv_acc, rowdot, q_heads, do_heads), plan = _comm_hooks(comm, refs, 7, 3)
        q_ref, k_ref, v_ref, o_ref, do_ref, lse_ref, bias_ref = ins
        step_no = pl.program_id(0) * N_PAIRS + pl.program_id(1)
        if plan:
            pl.when(step_no == 0)(lambda: comm.start(*plan))
        wlane = [_head_lanes(width, h) for h in (0, 1)]
        vlane = [_head_lanes(LANES, h) for h in (0, 1)]
        causal = lax.broadcasted_iota(I32, (t, t), 0) <= lax.broadcasted_iota(I32, (t, t), 1)
        q_all = q_ref[...] * pre if dilated else q_ref[...]
        for h in (0, 1):
            q_heads[h] = jnp.where(wlane[h], q_all, jnp.zeros_like(q_all))
            do_heads[h] = jnp.where(vlane[h], do_ref[...], jnp.zeros_like(do_ref[...]))
        prod = jnp.transpose(do_ref[...].astype(F32) * o_ref[...].astype(F32))
        rowdot[0:1, :] = jnp.sum(prod[0:HEAD_V], axis=0, keepdims=True)
        rowdot[1:2, :] = jnp.sum(prod[HEAD_V:], axis=0, keepdims=True)
        dq_acc[...] = jnp.zeros_like(dq_acc)

        def k_tile(j, _):
            ks = pl.multiple_of(j * t, t)
            kj = k_ref[pl.ds(ks, t), :]
            vj = v_ref[pl.ds(ks, t), :]
            kh = [jnp.where(wlane[h], kj, jnp.zeros_like(kj)) for h in (0, 1)]
            if dq_transposed:
                kh = [jnp.transpose(kh[h].astype(F32)).astype(BF16) for h in (0, 1)]
            dk_acc[...] = jnp.zeros_like(dk_acc)
            dv_acc[...] = jnp.zeros_like(dv_acc)

            def operands(i):
                qs = pl.multiple_of(i * t, t)
                return [q_heads[h, pl.ds(qs, t), :] for h in (0, 1)], [do_heads[h, pl.ds(qs, t), :] for h in (0, 1)]

            def scores(i):
                qih, _ = operands(i)
                return tuple(_dot_nt(kj, qih[h]) for h in (0, 1))

            def q_tile(n, carry, last):
                i = nq - 1 - n
                ahead = () if last else scores(i - 1)
                qs = pl.multiple_of(i * t, t)
                qih, doih = operands(i)
                s0, s1 = carry
                dps = [_dot_nt(vj, doih[h]) for h in (0, 1)]
                dq_i = jnp.zeros((width, t) if dq_transposed else (t, width), F32)
                for h, (s, dp) in enumerate(((s0, dps[0]), (s1, dps[1]))):
                    if dilated:
                        s = s + bias_ref[h, i - j]
                    else:
                        s = s * scale
                        if last:
                            s = jnp.where(causal, s, NEG)
                    p = jnp.exp(s - lse_ref[h:h + 1, pl.ds(qs, t)])
                    ds = p * (dp - rowdot[h:h + 1, pl.ds(qs, t)])
                    ds = (ds if dilated else ds * scale).astype(BF16)
                    dv_acc[...] += _dot(p.astype(BF16), doih[h])
                    dk_acc[...] += _dot(ds, qih[h])
                    dq_i = dq_i + (_dot(kh[h], ds) if dq_transposed else _dot_tn(ds, kh[h]))
                if dq_transposed:
                    dq_acc[:, pl.ds(qs, t)] += dq_i
                else:
                    dq_acc[pl.ds(qs, t), :] += dq_i
                return ahead

            q_tile(nq - 1 - j, lax.fori_loop(0, nq - 1 - j, functools.partial(q_tile, last=False), scores(nq - 1)), True)
            dk_ref[pl.ds(ks, t), :] = dk_acc[...].astype(BF16)
            dv_ref[pl.ds(ks, t), :] = dv_acc[...].astype(BF16)
            return 0

        lax.fori_loop(0, nq, k_tile, 0)
        dq_ref[...] = ((jnp.transpose(dq_acc[...]) if dq_transposed else dq_acc[...]) * pre).astype(BF16)
        if plan:
            pl.when(step_no == steps - 1)(lambda: comm.finish(*plan))

    tokens = batch * seq
    bias_spec = (pl.BlockSpec((2, nq, t, t), lambda b, p: (p, 0, 0, 0)) if dilated
                 else pl.BlockSpec((None, 8, LANES), lambda b, p: (0, 0, 0)))
    n = comm.n if comm else 0
    return pl.pallas_call(
        body, name=name, grid=(batch, N_PAIRS),
        out_shape=[jax.ShapeDtypeStruct((tokens, N_PAIRS * width), BF16), jax.ShapeDtypeStruct((tokens, N_PAIRS * width), BF16),
                   jax.ShapeDtypeStruct((tokens, DIL_WIDTH), BF16)] + (comm.out_shape if comm else []),
        in_specs=[pl.BlockSpec((seq, width), lambda b, p: (b, cq + p)),
                  pl.BlockSpec((seq, width), lambda b, p: (b, ck + p)),
                  pl.BlockSpec((seq, LANES), lambda b, p: (b, cv + p)),
                  pl.BlockSpec((seq, LANES), lambda b, p: (b, p)),
                  pl.BlockSpec((seq, LANES), lambda b, p: (b, p)),
                  pl.BlockSpec((None, 8, seq), lambda b, p: (b * N_PAIRS + p, 0, 0)),
                  bias_spec] + [ANY] * n,
        out_specs=[pl.BlockSpec((seq, width), lambda b, p: (b, p)),
                   pl.BlockSpec((seq, width), lambda b, p: (b, p)),
                   pl.BlockSpec((seq, LANES), lambda b, p: (b, p))] + [ANY] * n,
        scratch_shapes=[pltpu.VMEM((width, seq) if dq_transposed else (seq, width), F32),
                        pltpu.VMEM((t, width), F32), pltpu.VMEM((t, LANES), F32),
                        pltpu.VMEM((8, seq), F32), pltpu.VMEM((2, seq, width), BF16), pltpu.VMEM((2, seq, LANES), BF16)]
        + (comm.scratch if comm else []),
        compiler_params=_cp("arbitrary", "arbitrary") if comm else _cp("parallel", "parallel"),
    )(q, k, v, o, do, lse, bias, *comm_arrays)


def _rms(xf, g):
    r = lax.rsqrt(jnp.mean(xf * xf, axis=1, keepdims=True) + RMS_EPS)
    return xf * r * g, r


def _rms_bwd(dy, xf, r, g):
    gy = dy * g
    dx = r * gy - xf * (r * r * r) * jnp.mean(gy * xf, axis=1, keepdims=True)
    return dx, dy * xf * r


def _ln_bwd(dy, xhat, rstd, g):
    dxh = dy * g
    return rstd * (dxh - jnp.mean(dxh, axis=1, keepdims=True) - xhat * jnp.mean(dxh * xhat, axis=1, keepdims=True))


def _rope_slabs(q, cos, sin, transpose):
    first_half = (lax.broadcasted_iota(I32, (1, LANES), 1) % ROPE) < ROPE // 2
    out = []
    for p in range(N_PAIRS):
        blk = q[:, p * PAIR_W + LANES:(p + 1) * PAIR_W]
        y = blk * sin if transpose else blk
        up, down = pltpu.roll(y, LANES - ROPE // 2, 1), pltpu.roll(y, ROPE // 2, 1)
        rot = jnp.where(first_half, up, -down) if transpose else jnp.where(first_half, -up, down) * sin
        out += [q[:, p * PAIR_W:p * PAIR_W + LANES], blk * cos + rot]
    return jnp.concatenate(out, axis=1)


def _fwd_proj(x, w_in_ext, w1, wk, wv, g_q, g_kv, cext, sext, cs128, *, seq):
    tokens = x.shape[0]
    tm = min(MIX_TILE, seq)
    ns = seq // tm

    def body(x_ref, win_ref, w1_ref, wk_ref, wv_ref, gq_ref, gkv_ref, c_ref, s_ref, cs_ref,
             low_ref, gates_ref, qkvd_ref, qp_ref, kp_ref, vm_ref, qn_ref, kvn_ref, xb_ref):
        xt = x_ref[...].astype(BF16)
        xb_ref[...] = xt
        low = _dot(xt, win_ref[:, 0:LOW_W])
        low_ref[...] = low
        qkvd_ref[...] = _dot(xt, win_ref[:, LOW_W:LOW_W + 3 * DIL_WIDTH]).astype(BF16)
        gates_ref[...] = _dot(xt, win_ref[:, LOW_W + 3 * DIL_WIDTH:]).astype(BF16)
        qn = _rms(low[:, 0:Q_LORA], gq_ref[...])[0].astype(BF16)
        kvn = _rms(low[:, Q_LORA:Q_LORA + KV_LORA], gkv_ref[...])[0].astype(BF16)
        qn_ref[...] = qn
        kvn_ref[...] = kvn
        qp_ref[...] = _rope_slabs(_dot(qn, w1_ref[...]), c_ref[...], s_ref[...], False).astype(BF16)
        kr = low[:, Q_LORA + KV_LORA:] * cs_ref[...]
        kr = kr + pltpu.roll(kr, LANES - ROPE, 1)
        lane = lax.broadcasted_iota(I32, kr.shape, 1)
        kr = jnp.where(lane < ROPE, kr, 0.0)
        kr = (kr + pltpu.roll(kr, ROPE, 1)).astype(BF16)
        kn = _dot(kvn, wk_ref[...]).astype(BF16)
        kp_ref[...] = jnp.concatenate([blk for p in range(N_PAIRS) for blk in (kn[:, p * LANES:(p + 1) * LANES], kr)], axis=1)
        vm_ref[...] = _dot(kvn, wv_ref[...]).astype(BF16)

    n_gates = 2 * D_MODEL
    outs = [(LOW_W, F32), (n_gates, BF16), (3 * DIL_WIDTH, BF16), (N_PAIRS * PAIR_W, BF16), (N_PAIRS * PAIR_W, BF16),
            (DIL_WIDTH, BF16), (Q_LORA, BF16), (KV_LORA, BF16), (D_MODEL, BF16)]
    return pl.pallas_call(
        body, name="fwd_proj", grid=(tokens // tm,),
        out_shape=tuple(jax.ShapeDtypeStruct((tokens, w), dt) for w, dt in outs),
        in_specs=[_rows(tm, D_MODEL), _full(w_in_ext.shape), _full(w1.shape), _full(wk.shape),
                  _full(wv.shape), _full(g_q.shape), _full(g_kv.shape),
                  pl.BlockSpec((tm, LANES), lambda i: (i % ns, 1)),
                  pl.BlockSpec((tm, LANES), lambda i: (i % ns, 1)),
                  pl.BlockSpec((tm, LANES), lambda i: (i % ns, 0))],
        out_specs=tuple(_rows(tm, w) for w, _ in outs),
        compiler_params=_cp("parallel"),
    )(x, w_in_ext, w1, wk, wv, g_q, g_kv, cext, sext, cs128)


def _fwd_mix(o_a, o_b, gates, x, b_gate, w_oa, w_ob, w_out, ln_g, ln_b, *, seq):
    tokens = x.shape[0]
    tm = min(MIX_TILE, seq)

    def body(oa_ref, ob_ref, gt_ref, x_ref, bg_ref, woa_ref, wob_ref, wout_ref, g_ref, b_ref,
             hb_ref, xhat_ref, rstd_ref, ya_ref, yb_ref, mix_ref):
        ya = _dot(oa_ref[...], woa_ref[...])
        yb = _dot(ob_ref[...], wob_ref[...])
        g0 = _sigmoid(gt_ref[:, 0:D_MODEL].astype(F32) + bg_ref[0:1, :])
        g1 = _sigmoid(gt_ref[:, D_MODEL:].astype(F32) + bg_ref[1:2, :])
        mix = (g0 * ya + g1 * yb).astype(BF16)
        z = ALPHA * x_ref[...] + _dot(mix, wout_ref[...])
        zc = z - jnp.mean(z, axis=1, keepdims=True)
        rstd = lax.rsqrt(jnp.mean(zc * zc, axis=1, keepdims=True) + LN_EPS)
        xhat = zc * rstd
        hb_ref[...] = (xhat * g_ref[...] + b_ref[...]).astype(BF16)
        xhat_ref[...] = xhat
        rstd_ref[...] = jnp.broadcast_to(rstd, (tm, LANES))
        ya_ref[...] = ya.astype(BF16)
        yb_ref[...] = yb.astype(BF16)
        mix_ref[...] = mix

    outs = [(D_MODEL, BF16), (D_MODEL, F32), (LANES, F32), (D_MODEL, BF16), (D_MODEL, BF16), (D_MODEL, BF16)]
    return pl.pallas_call(
        body, name="fwd_mix", grid=(tokens // tm,),
        out_shape=tuple(jax.ShapeDtypeStruct((tokens, w), dt) for w, dt in outs),
        in_specs=[_rows(tm, DIL_WIDTH), _rows(tm, DIL_WIDTH), _rows(tm, 2 * D_MODEL), _rows(tm, D_MODEL),
                  _full(b_gate.shape), _full(w_oa.shape), _full(w_ob.shape), _full(w_out.shape),
                  _full(ln_g.shape), _full(ln_b.shape)],
        out_specs=tuple(_rows(tm, w) for w, _ in outs),
        compiler_params=_cp("parallel"),
    )(o_a, o_b, gates, x, b_gate, w_oa, w_ob, w_out, ln_g, ln_b)


def _fwd_mlp(hb, xhat1, target, w_ff1, w_ff2, ln1_g, ln1_b, ln_g, ln_b, *, seq):
    tokens = hb.shape[0]
    tm = min(2 * TOKEN_TILE, seq)
    tf = FF_SHARD
    nf = N_DEV // FF_STEP

    def body(hb_ref, xh_ref, tg_ref, w1_ref, w2_ref, g1_ref, b1_ref, g_ref, b_ref, u_ref, dz_ref, dzb_ref, stat_ref, acc):
        i, j = pl.program_id(0), pl.program_id(1)

        @pl.when((i == 0) & (j == 0))
        def _():
            stat_ref[...] = jnp.zeros_like(stat_ref)

        @pl.when(j == 0)
        def _():
            acc[...] = jnp.zeros_like(acc)

        acts = []
        for s in range(FF_STEP):
            u = _dot(hb_ref[...], w1_ref[s])
            u_ref[:, s * tf:(s + 1) * tf] = u.astype(BF16)
            acts.append(jnp.square(jnp.maximum(u, 0.0)).astype(BF16))
        acc[...] += _dot(jnp.concatenate(acts, axis=1), w2_ref[...])

        @pl.when(j == nf - 1)
        def _():
            z = ALPHA * (xh_ref[...] * g1_ref[...] + b1_ref[...]) + acc[...]
            zc = z - jnp.mean(z, axis=1, keepdims=True)
            rstd = lax.rsqrt(jnp.mean(zc * zc, axis=1, keepdims=True) + LN_EPS)
            xhat = zc * rstd
            err = xhat * g_ref[...] + b_ref[...] - tg_ref[...]
            dy = err * (1.0 / D_MODEL)
            dz = _ln_bwd(dy, xhat, rstd, g_ref[...])
            dz_ref[...] = dz
            dzb_ref[...] = dz.astype(BF16)
            stat_ref[0:1, :] += jnp.sum(dy * xhat, axis=0, keepdims=True)
            stat_ref[1:2, :] += jnp.sum(dy, axis=0, keepdims=True)
            stat_ref[2:3, :] += jnp.sum(jnp.sum(err * err, axis=1, keepdims=True), axis=0, keepdims=True) * (0.5 / D_MODEL)

    return pl.pallas_call(
        body, name="fwd_mlp", grid=(tokens // tm, nf),
        out_shape=(jax.ShapeDtypeStruct((tokens, D_FF), BF16), jax.ShapeDtypeStruct((tokens, D_MODEL), F32),
                   jax.ShapeDtypeStruct((tokens, D_MODEL), BF16), jax.ShapeDtypeStruct((8, D_MODEL), F32)),
        in_specs=[_rows(tm, D_MODEL), _rows(tm, D_MODEL), _rows(tm, D_MODEL),
                  pl.BlockSpec((FF_STEP, D_MODEL, tf), lambda i, j: (j, 0, 0)),
                  pl.BlockSpec((FF_STEP * tf, D_MODEL), lambda i, j: (j, 0)),
                  _full(ln1_g.shape), _full(ln1_b.shape), _full(ln_g.shape), _full(ln_b.shape)],
        out_specs=(pl.BlockSpec((tm, FF_STEP * tf), lambda i, j: (i, j)), _rows(tm, D_MODEL), _rows(tm, D_MODEL),
                   _full((8, D_MODEL))),
        scratch_shapes=[pltpu.VMEM((tm, D_MODEL), F32)],
        compiler_params=_cp("arbitrary", "arbitrary"),
    )(hb, xhat1, target, w_ff1, w_ff2, ln1_g, ln1_b, ln_g, ln_b)


def _bwd_mlp(dz2, dz2b, u, xhat1, rstd1, w_ff1, w_ff2, ln_g, *, seq):
    tokens = dz2.shape[0]
    tm = min(2 * TOKEN_TILE, seq)
    tf = FF_SHARD
    nf = N_DEV // FF_STEP

    def body(dz_ref, dzb_ref, u_ref, xh_ref, rs_ref, w1_ref, w2_ref, g_ref, du_ref, dz1_ref, dz1b_ref, stat_ref, acc):
        i, j = pl.program_id(0), pl.program_id(1)

        @pl.when((i == 0) & (j == 0))
        def _():
            stat_ref[...] = jnp.zeros_like(stat_ref)

        @pl.when(j == 0)
        def _():
            acc[...] = jnp.zeros_like(acc)

        da = _dot_nt(dzb_ref[...], w2_ref[...])
        du = (da * (2.0 * jnp.maximum(u_ref[...].astype(F32), 0.0))).astype(BF16)
        du_ref[...] = du
        part = _dot_nt(du[:, 0:tf], w1_ref[0])
        for s in range(1, FF_STEP):
            part = part + _dot_nt(du[:, s * tf:(s + 1) * tf], w1_ref[s])
        acc[...] += part

        @pl.when(j == nf - 1)
        def _():
            dh = ALPHA * dz_ref[...] + acc[...]
            xhat = xh_ref[...]
            dz1 = _ln_bwd(dh, xhat, rs_ref[:, 0:1], g_ref[...])
            dz1_ref[...] = dz1
            dz1b_ref[...] = dz1.astype(BF16)
            stat_ref[0:1, :] += jnp.sum(dh * xhat, axis=0, keepdims=True)
            stat_ref[1:2, :] += jnp.sum(dh, axis=0, keepdims=True)

    return pl.pallas_call(
        body, name="bwd_mlp", grid=(tokens // tm, nf),
        out_shape=(jax.ShapeDtypeStruct((tokens, D_FF), BF16), jax.ShapeDtypeStruct((tokens, D_MODEL), F32),
                   jax.ShapeDtypeStruct((tokens, D_MODEL), BF16), jax.ShapeDtypeStruct((8, D_MODEL), F32)),
        in_specs=[_rows(tm, D_MODEL), _rows(tm, D_MODEL), pl.BlockSpec((tm, FF_STEP * tf), lambda i, j: (i, j)),
                  _rows(tm, D_MODEL), _rows(tm, LANES),
                  pl.BlockSpec((FF_STEP, D_MODEL, tf), lambda i, j: (j, 0, 0)),
                  pl.BlockSpec((FF_STEP * tf, D_MODEL), lambda i, j: (j, 0)),
                  _full(ln_g.shape)],
        out_specs=(pl.BlockSpec((tm, FF_STEP * tf), lambda i, j: (i, j)), _rows(tm, D_MODEL), _rows(tm, D_MODEL),
                   _full((8, D_MODEL))),
        scratch_shapes=[pltpu.VMEM((tm, D_MODEL), F32)],
        compiler_params=_cp("arbitrary", "arbitrary"),
    )(dz2, dz2b, u, xhat1, rstd1, w_ff1, w_ff2, ln_g)


def _bwd_mix(dz1b, gates, y_a, y_b, b_gate, w_oa, w_ob, w_out, *, seq):
    tokens = dz1b.shape[0]
    tm = min(MIX_TILE, seq)

    def body(dz_ref, gt_ref, ya_ref, yb_ref, bg_ref, woa_ref, wob_ref, wout_ref,
             dgt_ref, dya_ref, dyb_ref, doa_ref, dob_ref, stat_ref):
        @pl.when(pl.program_id(0) == 0)
        def _():
            stat_ref[...] = jnp.zeros_like(stat_ref)

        dmix = _dot_nt(dz_ref[...], wout_ref[...])
        for k, (y_ref, w_ref, dy_ref, do_ref) in enumerate(((ya_ref, woa_ref, dya_ref, doa_ref), (yb_ref, wob_ref, dyb_ref, dob_ref))):
            g = _sigmoid(gt_ref[:, k * D_MODEL:(k + 1) * D_MODEL].astype(F32) + bg_ref[k:k + 1, :])
            dgate = dmix * y_ref[...].astype(F32) * g * (1.0 - g)
            dgt_ref[:, k * D_MODEL:(k + 1) * D_MODEL] = dgate.astype(BF16)
            stat_ref[k:k + 1, :] += jnp.sum(dgate, axis=0, keepdims=True)
            dy = (dmix * g).astype(BF16)
            dy_ref[...] = dy
            do_ref[...] = _dot_nt(dy, w_ref[...]).astype(BF16)

    outs = [(2 * D_MODEL, BF16), (D_MODEL, BF16), (D_MODEL, BF16), (DIL_WIDTH, BF16), (DIL_WIDTH, BF16)]
    return pl.pallas_call(
        body, name="bwd_mix", grid=(tokens // tm,),
        out_shape=tuple(jax.ShapeDtypeStruct((tokens, w), dt) for w, dt in outs) + (jax.ShapeDtypeStruct((8, D_MODEL), F32),),
        in_specs=[_rows(tm, D_MODEL), _rows(tm, 2 * D_MODEL), _rows(tm, D_MODEL), _rows(tm, D_MODEL),
                  _full(b_gate.shape), _full(w_oa.shape), _full(w_ob.shape), _full(w_out.shape)],
        out_specs=tuple(_rows(tm, w) for w, _ in outs) + (_full((8, D_MODEL)),),
        compiler_params=_cp("arbitrary"),
    )(dz1b, gates, y_a, y_b, b_gate, w_oa, w_ob, w_out)


def _bwd_proj(dqp, dkp, dvm, dq_d, dk_d, dv_d, dgates, dz1, low, w_in_ext, w1, wk, wv, g_q, g_kv, cext, sext, cs128, *, seq):
    tokens = dz1.shape[0]
    tm = min(TOKEN_TILE, seq)
    ns = seq // tm

    def body(dqp_ref, dkp_ref, dvm_ref, dqd_ref, dkd_ref, dvd_ref, dgt_ref, dz_ref, low_ref, win_ref, w1_ref, wk_ref,
             wv_ref, gq_ref, gkv_ref, c_ref, s_ref, cs_ref, dx_ref, dproj_ref, da_ref, dkn_ref, stat_ref):
        @pl.when(pl.program_id(0) == 0)
        def _():
            stat_ref[...] = jnp.zeros_like(stat_ref)

        low = low_ref[...]
        d_a = _rope_slabs(dqp_ref[...].astype(F32), c_ref[...], s_ref[...], True).astype(BF16)
        da_ref[...] = d_a
        q_a = low[:, 0:Q_LORA]
        _, rq = _rms(q_a, gq_ref[...])
        dq_a, gq_terms = _rms_bwd(_dot_nt(d_a, w1_ref[...]), q_a, rq, gq_ref[...])
        kv_a = low[:, Q_LORA:Q_LORA + KV_LORA]
        _, rkv = _rms(kv_a, gkv_ref[...])
        dkn = jnp.concatenate([dkp_ref[:, p * PAIR_W:p * PAIR_W + LANES] for p in range(N_PAIRS)], axis=1)
        dkn_ref[...] = dkn
        dkv_a, gkv_terms = _rms_bwd(_dot_nt(dkn, wk_ref[...]) + _dot_nt(dvm_ref[...], wv_ref[...]), kv_a, rkv, gkv_ref[...])
        dkr = sum(dkp_ref[:, p * PAIR_W + LANES:(p + 1) * PAIR_W].astype(F32) for p in range(N_PAIRS))
        dkr = dkr + pltpu.roll(dkr, LANES - ROPE, 1)
        dkr = jnp.where(lax.broadcasted_iota(I32, dkr.shape, 1) < ROPE, dkr, 0.0)
        dkr = (dkr + pltpu.roll(dkr, ROPE, 1)) * cs_ref[...]
        stat_ref[0:1, 0:Q_LORA] += jnp.sum(gq_terms, axis=0, keepdims=True)
        stat_ref[1:2, 0:KV_LORA] += jnp.sum(gkv_terms, axis=0, keepdims=True)
        dproj_ref[:, 0:Q_LORA] = dq_a.astype(BF16)
        dproj_ref[:, Q_LORA:Q_LORA + KV_LORA] = dkv_a.astype(BF16)
        dproj_ref[:, Q_LORA + KV_LORA:LOW_W] = dkr.astype(BF16)
        dproj_ref[:, LOW_W:LOW_W + DIL_WIDTH] = dqd_ref[...]
        dproj_ref[:, LOW_W + DIL_WIDTH:LOW_W + 2 * DIL_WIDTH] = dkd_ref[...]
        dproj_ref[:, LOW_W + 2 * DIL_WIDTH:LOW_W + 3 * DIL_WIDTH] = dvd_ref[...]
        dproj_ref[:, LOW_W + 3 * DIL_WIDTH:] = dgt_ref[...]
        dx_ref[...] = ALPHA * dz_ref[...] + _dot_nt(dproj_ref[...], win_ref[...])

    wide = N_PAIRS * PAIR_W
    return pl.pallas_call(
        body, name="bwd_proj", grid=(tokens // tm,),
        out_shape=(jax.ShapeDtypeStruct((tokens, D_MODEL), F32), jax.ShapeDtypeStruct((tokens, IN_EXT), BF16),
                   jax.ShapeDtypeStruct((tokens, wide), BF16), jax.ShapeDtypeStruct((tokens, N_HEADS * NOPE), BF16),
                   jax.ShapeDtypeStruct((8, D_MODEL), F32)),
        in_specs=[_rows(tm, wide), _rows(tm, wide), _rows(tm, DIL_WIDTH), _rows(tm, DIL_WIDTH), _rows(tm, DIL_WIDTH),
                  _rows(tm, DIL_WIDTH), _rows(tm, 2 * D_MODEL),
                  _rows(tm, D_MODEL), _rows(tm, LOW_W), _full(w_in_ext.shape), _full(w1.shape),
                  _full(wk.shape), _full(wv.shape), _full(g_q.shape), _full(g_kv.shape),
                  pl.BlockSpec((tm, LANES), lambda i: (i % ns, 1)), pl.BlockSpec((tm, LANES), lambda i: (i % ns, 1)),
                  pl.BlockSpec((tm, LANES), lambda i: (i % ns, 0))],
        out_specs=(_rows(tm, D_MODEL), _rows(tm, IN_EXT), _rows(tm, wide), _rows(tm, N_HEADS * NOPE), _full((8, D_MODEL))),
        compiler_params=_cp("arbitrary"),
    )(dqp, dkp, dvm, dq_d, dk_d, dv_d, dgates, dz1, low, w_in_ext, w1, wk, wv, g_q, g_kv, cext, sext, cs128)


def _wgrad(a, b, name, square_relu=False, by_shard=False):
    tokens, ka = a.shape
    n = b.shape[1]
    if ka <= 512 or ka % 512 == 0:
        tka = min(ka, 512)
    else:
        tka = max(w for w in range(LANES, min(ka, 2304) + 1, LANES) if ka % w == 0)
    shard = n // N_DEV
    tn = WGRAD_SHARDS * shard if by_shard else max(w for w in range(LANES, min(n, 2304) + 1, LANES) if n % w == 0)
    tt = min(tokens, 2048 if tka <= 512 else 1024)
    nt = tokens // tt

    def body(a_ref, b_ref, o_ref, acc):
        kt = pl.program_id(2)

        @pl.when(kt == 0)
        def _():
            acc[...] = jnp.zeros_like(acc)

        at = a_ref[...]
        if square_relu:
            at = jnp.square(jnp.maximum(at.astype(F32), 0.0)).astype(BF16)
        acc[...] += _dot_tn(at, b_ref[...])

        @pl.when(kt == nt - 1)
        def _():
            if by_shard:
                for s in range(WGRAD_SHARDS):
                    o_ref[s] = acc[:, s * shard:(s + 1) * shard].astype(BF16)
            else:
                o_ref[...] = acc[...].astype(BF16)

    if by_shard:
        out_shape, out_spec = (N_DEV, ka, shard), pl.BlockSpec((WGRAD_SHARDS, tka, shard), lambda i, j, k: (j, i, 0))
    else:
        out_shape, out_spec = (ka, n), pl.BlockSpec((tka, tn), lambda i, j, k: (i, j))
    return pl.pallas_call(
        body, name=name, grid=(ka // tka, n // tn, nt), out_shape=jax.ShapeDtypeStruct(out_shape, BF16),
        in_specs=[pl.BlockSpec((tt, tka), lambda i, j, k: (k, i)), pl.BlockSpec((tt, tn), lambda i, j, k: (k, j))],
        out_specs=out_spec,
        scratch_shapes=[pltpu.VMEM((tka, tn), F32)],
        compiler_params=_cp("parallel", "parallel", "arbitrary"),
    )(a, b)


def _adam_math(w, g, m, v):
    m = ADAM_B1 * m + (1.0 - ADAM_B1) * g
    v = ADAM_B2 * v + (1.0 - ADAM_B2) * jnp.square(g)
    m_hat = m / (1.0 - ADAM_B1 ** ADAM_STEP)
    v_hat = v / (1.0 - ADAM_B2 ** ADAM_STEP)
    return -ADAM_LR * (m_hat / (jnp.sqrt(v_hat) + ADAM_EPS) + ADAM_WD * w), m, v


def _adamw(items, name):
    steps = min(_tiles(*w.shape)[0] for w, *_ in items)
    n_items = len(items)

    def body(slot_ref, *refs):
        ins, outs = refs[:5 * n_items], refs[5 * n_items:]
        for k, (_, _, _, _, parts) in enumerate(items):
            w_ref, m_ref, v_ref, own_ref, p_ref = ins[5 * k:5 * k + 5]
            g_ref, d_ref, nm_ref, nv_ref = outs[4 * k:4 * k + 4]
            g = own_ref[...].astype(F32)
            for d in range(parts.shape[0]):
                g = g + p_ref[d].astype(F32)
            g_ref[...] = g
            d_ref[...], nm_ref[...], nv_ref[...] = _adam_math(w_ref[...], g, m_ref[...], v_ref[...])

    x, y, c = _place()
    in_specs, out_specs, out_shape, args = [], [], [], []
    for w, m, v, own, parts in items:
        rows, cols = w.shape
        _, tile, at = _tiles(rows, cols, steps)
        blk = pl.BlockSpec(tile, lambda i, slot, at=at: at(i))
        own_blk = blk if own.ndim == 2 else pl.BlockSpec((None, *tile), lambda i, slot, at=at: (slot[0], *at(i)))
        in_specs += [blk, blk, blk, own_blk, pl.BlockSpec((parts.shape[0], *tile), lambda i, slot, at=at: (0, *at(i)))]
        out_specs += [blk] * 4
        out_shape += [jax.ShapeDtypeStruct((rows, cols), F32)] * 4
        args += [w, m, v, own, parts]
    out = pl.pallas_call(
        body, name=name,
        grid_spec=pltpu.PrefetchScalarGridSpec(num_scalar_prefetch=1, grid=(steps,), in_specs=in_specs, out_specs=out_specs),
        out_shape=out_shape, compiler_params=_cp("parallel"),
    )(jnp.reshape(4 * x + 2 * y + c, (1,)).astype(I32), *args)
    return [tuple(out[4 * k:4 * k + 4]) for k in range(n_items)]


def _adamw_small(parts, w, m, v):
    _, rows, cols = parts.shape

    def body(p_ref, w_ref, m_ref, v_ref, g_ref, d_ref, nm_ref, nv_ref):
        g = p_ref[0]
        for d in range(1, N_DEV):
            g = g + p_ref[d]
        g_ref[...] = g
        d_ref[...], nm_ref[...], nv_ref[...] = _adam_math(w_ref[...], g, m_ref[...], v_ref[...])

    return pl.pallas_call(
        body, name="adamw_replicated", out_shape=(jax.ShapeDtypeStruct((rows, cols), F32),) * 4,
        in_specs=[_full(parts.shape)] + [_full((rows, cols))] * 3, out_specs=(_full((rows, cols)),) * 4, grid=(1,),
        compiler_params=_cp("arbitrary"),
    )(parts, w, m, v)


def _pad_rows(a2d, mult):
    pad = (-a2d.shape[-2]) % mult
    return jnp.pad(a2d, [(0, 0)] * (a2d.ndim - 2) + [(0, pad), (0, 0)]) if pad else a2d


def _pad_cols(a):
    pad = (-a.shape[-1]) % LANES
    return jnp.pad(a, [(0, 0)] * (a.ndim - 1) + [(0, pad)]) if pad else a


def _rot_cols(w):
    half = ROPE // 2
    return jnp.concatenate([-w[..., half:], w[..., :half]], axis=-1)


def _unrot_cols(dw):
    half = ROPE // 2
    return jnp.concatenate([dw[..., half:], -dw[..., :half]], axis=-1)


def _from_col_shards(stacked):
    return stacked.transpose(1, 0, 2).reshape(stacked.shape[1], -1)


def _to_col_shards(full):
    r = full.shape[0]
    return full.reshape(r, N_DEV, -1).transpose(1, 0, 2)


def _rope_tables(seq):
    half = ROPE // 2
    inv = jnp.power(ROPE_THETA, -jnp.arange(half, dtype=F32) / half)
    ang = jnp.arange(seq, dtype=F32)[:, None] * inv[None, :]
    cos = jnp.concatenate([jnp.cos(ang)] * 2, axis=1)
    sin = jnp.concatenate([jnp.sin(ang)] * 2, axis=1)
    ones, zeros = jnp.ones((seq, 2 * NOPE), F32), jnp.zeros((seq, 2 * NOPE), F32)
    pad = jnp.zeros((seq, PAIR_W - 2 * NOPE - 2 * ROPE), F32)
    cext = jnp.concatenate([ones, cos, cos, pad], axis=1)
    sext = jnp.concatenate([zeros, sin, sin, pad], axis=1)
    cs128 = jnp.concatenate([cos, sin, jnp.zeros((seq, LANES - 2 * ROPE), F32)], axis=1)
    return cext, sext, cs128


def _pair_slabs(nope, rope):
    k = nope.shape[0]
    nope = nope.reshape(k, N_PAIRS, 2 * NOPE)
    rope = rope.reshape(k, N_PAIRS, 2 * ROPE)
    pad = jnp.zeros((k, N_PAIRS, PAIR_W - 2 * NOPE - 2 * ROPE), nope.dtype)
    return jnp.concatenate([nope, rope, pad], axis=2).reshape(k, N_PAIRS * PAIR_W)


def _split_slabs(slabs):
    k = slabs.shape[0]
    s = slabs.reshape(k, N_PAIRS, PAIR_W)
    return s[:, :, :2 * NOPE].reshape(k, N_HEADS, NOPE), s[:, :, 2 * NOPE:2 * NOPE + 2 * ROPE].reshape(k, N_HEADS, ROPE)


def kernel(x, w_in, b_gate, g_q_a, w_uq, g_kv_a, w_ukv, w_o_mla, w_o_dil, w_out, ln1_g, ln1_b, w_ff1, w_ff2, ln2_g, ln2_b, loss_target, m_w_in, m_b_gate, m_g_q_a, m_w_uq, m_g_kv_a, m_w_ukv, m_w_o_mla, m_w_o_dil, m_w_out, m_ln1_g, m_ln1_b, m_w_ff1, m_w_ff2, m_ln2_g, m_ln2_b, v_w_in, v_b_gate, v_g_q_a, v_w_uq, v_g_kv_a, v_w_ukv, v_w_o_mla, v_w_o_dil, v_w_out, v_ln1_g, v_ln1_b, v_w_ff1, v_w_ff2, v_ln2_g, v_ln2_b):
    batch, seq, _ = x.shape
    tokens = batch * seq
    weights = dict(w_in=w_in, w_uq=w_uq, w_ukv=w_ukv, w_o_mla=w_o_mla, w_o_dil=w_o_dil, w_out=w_out, w_ff1=w_ff1, w_ff2=w_ff2, b_gate=b_gate)
    mom_m = dict(w_in=m_w_in, w_uq=m_w_uq, w_ukv=m_w_ukv, w_o_mla=m_w_o_mla, w_o_dil=m_w_o_dil, w_out=m_w_out, w_ff1=m_w_ff1, w_ff2=m_w_ff2, b_gate=m_b_gate)
    mom_v = dict(w_in=v_w_in, w_uq=v_w_uq, w_ukv=v_w_ukv, w_o_mla=v_w_o_mla, w_o_dil=v_w_o_dil, w_out=v_w_out, w_ff1=v_w_ff1, w_ff2=v_w_ff2, b_gate=v_b_gate)

    first = ["w_in", "w_uq", "w_ukv"]
    widths = [weights[n].shape[2] for n in first]
    shards = [weights["w_in"][0].T.astype(BF16)] + [_pad_cols(weights[n][0].astype(BF16)) for n in first[1:]]
    g_in, g_uq, g_ukv = _run_comm(_Gather(shards), shards, "all_gather_first_weights")
    g_uq, g_ukv = g_uq[:, :, :widths[1]], g_ukv[:, :, :widths[2]]

    s1, s2, n_in = Q_LORA + KV_LORA, Q_LORA + KV_LORA + ROPE, N_DEV * widths[0]

    def w_in_cols(lo, hi):
        out = []
        while lo < hi:
            d, off = divmod(lo, widths[0])
            take = min(hi - lo, widths[0] - off)
            out.append(g_in[d][off:off + take].T)
            lo += take
        return out

    w_in_ext = jnp.concatenate(w_in_cols(0, s2) + [_rot_cols(jnp.concatenate(w_in_cols(s1, s2), axis=1)),
                                                   jnp.zeros((D_MODEL, LOW_W - s2 - ROPE), BF16)] + w_in_cols(s2, n_in), axis=1)
    uq = _from_col_shards(g_uq).reshape(Q_LORA, N_HEADS, NOPE + ROPE)
    w1 = _pair_slabs(uq[:, :, :NOPE], uq[:, :, NOPE:])
    ukv = _from_col_shards(g_ukv).reshape(KV_LORA, N_HEADS, NOPE + HEAD_V)
    wk = ukv[:, :, :NOPE].reshape(KV_LORA, N_HEADS * NOPE)
    wv = ukv[:, :, NOPE:].reshape(KV_LORA, N_HEADS * HEAD_V)
    cext, sext, cs128 = _rope_tables(seq)
    dil_bias = _dilated_bias_table(seq)
    no_bias = jnp.zeros((1, 8, LANES), F32)

    x2 = x.reshape(tokens, D_MODEL)
    low, gates, qkvd, qp, kp, vm, qn, kvn, xb = _fwd_proj(x2, w_in_ext, w1, wk, wv, g_q_a, g_kv_a, cext, sext, cs128, seq=seq)
    bg = b_gate[0]
    bg_hi = bg.astype(BF16)
    bg_lo = (bg - bg_hi.astype(F32)).astype(BF16)
    later = [weights[n][0].astype(BF16) for n in ("w_o_mla", "w_o_dil", "w_out", "w_ff1", "w_ff2")]
    later.append(_pad_rows(jnp.concatenate([bg_hi, bg_lo], axis=0), 16))
    mla = dict(batch=batch, seq=seq, width=PAIR_W, col0=(0, 0, 0), dilated=False, scale=MLA_SCALE)
    dil = dict(batch=batch, seq=seq, width=LANES, col0=(0, N_PAIRS, 2 * N_PAIRS), dilated=True, scale=DIL_SCALE)
    o_a, lse_a, g_oa, g_ob, g_out, g_ff1, g_ff2, g_bg = _attn_fwd(
        qp, kp, vm, no_bias, name="mla_attention_fwd", comm=_Gather(later), comm_arrays=later, **mla)
    o_b, lse_b = _attn_fwd(qkvd, qkvd, qkvd, dil_bias, name="dilated_attention_fwd", **dil)
    w_oa, w_ob = _from_col_shards(g_oa), _from_col_shards(g_ob)
    w_out_full = g_out.reshape(D_MODEL, D_MODEL)
    w_ff2_full = g_ff2.reshape(D_FF, D_MODEL)
    bg_parts = g_bg.astype(F32)
    b_gate_full = _from_col_shards(bg_parts[:, 0:2] + bg_parts[:, 2:4])
    hb, xhat1, rstd1, y_a, y_b, mix = _fwd_mix(o_a, o_b, gates, x2, b_gate_full, w_oa, w_ob, w_out_full, ln1_g, ln1_b, seq=seq)
    u, dz2, dz2b, stat2 = _fwd_mlp(hb, xhat1, loss_target.reshape(tokens, D_MODEL), g_ff1, w_ff2_full, ln1_g, ln1_b, ln2_g, ln2_b, seq=seq)

    du, dz1, dz1b, stat1 = _bwd_mlp(dz2, dz2b, u, xhat1, rstd1, g_ff1, w_ff2_full, ln1_g, seq=seq)
    dw_ff = [_wgrad(hb, du, "wgrad_ff1", by_shard=True),
             _wgrad(u, dz2b, "wgrad_ff2", square_relu=True).reshape(N_DEV, FF_SHARD, D_MODEL)]
    dgates, dy_a, dy_b, do_a, do_b, stat_g = _bwd_mix(dz1b, gates, y_a, y_b, b_gate_full, w_oa, w_ob, w_out_full, seq=seq)
    dqp, dkp, dvm, r_ff1, r_ff2 = _attn_bwd(qp, kp, vm, o_a, do_a, lse_a, no_bias, name="mla_attention_bwd",
                                            comm=_Scatter(dw_ff), comm_arrays=dw_ff, **mla)
    dw_mid = [_to_col_shards(_wgrad(o_a, dy_a, "wgrad_o_mla")), _to_col_shards(_wgrad(o_b, dy_b, "wgrad_o_dil")),
              _wgrad(mix, dz1b, "wgrad_out").reshape(N_DEV, D_MODEL // N_DEV, D_MODEL),
              _pad_rows(_to_col_shards(stat_g[0:2]).astype(BF16), 16)]
    dq_d, dk_d, dv_d, r_oa, r_ob, r_out, r_bg = _attn_bwd(qkvd, qkvd, qkvd, o_b, do_b, lse_b, dil_bias, name="dilated_attention_bwd",
                                                          comm=_Scatter(dw_mid), comm_arrays=dw_mid, **dil)
    grad_x, dproj, d_a, dkn, stat_r = _bwd_proj(dqp, dkp, dvm, dq_d, dk_d, dv_d, dgates, dz1, low, w_in_ext, w1, wk, wv,
                                                g_q_a, g_kv_a, cext, sext, cs128, seq=seq)

    dw_in_ext = _wgrad(dproj, xb, "wgrad_in")
    dw1 = _wgrad(qn, d_a, "wgrad_uq")
    dwk = _wgrad(kvn, dkn, "wgrad_ukv_k")
    dwv = _wgrad(kvn, dvm, "wgrad_ukv_v")
    dw_kr = dw_in_ext[s1:s2] + _unrot_cols(dw_in_ext[s2:s2 + ROPE].T).T

    def dw_in_cols(lo, hi):
        out = []
        for a, b, piece in ((0, s1, lambda u, v: dw_in_ext[u:v]), (s1, s2, lambda u, v: dw_kr[u - s1:v - s1]),
                            (s2, n_in, lambda u, v: dw_in_ext[u + LOW_W - s2:v + LOW_W - s2])):
            if max(lo, a) < min(hi, b):
                out.append(piece(max(lo, a), min(hi, b)))
        return out

    dw_in = jnp.stack([jnp.concatenate(dw_in_cols(d * widths[0], (d + 1) * widths[0]), axis=0) for d in range(N_DEV)])
    n1, r1 = _split_slabs(dw1)
    dw_uq = jnp.concatenate([n1, r1], axis=2).reshape(Q_LORA, N_HEADS * (NOPE + ROPE))
    dw_ukv = jnp.concatenate([dwk.reshape(KV_LORA, N_HEADS, NOPE), dwv.reshape(KV_LORA, N_HEADS, HEAD_V)], axis=2).reshape(KV_LORA, N_HEADS * (NOPE + HEAD_V))
    last = [dw_in] + [_pad_cols(_to_col_shards(dw)) for dw in (dw_uq, dw_ukv)]
    theirs = _rs_sibling(last, "rs_last_sibling_exchange")
    sums = [_pair_sum(a, b, "rs_last_pair_sum_" + n) for a, b, n in zip(last, theirs, first)]
    partial = jnp.concatenate([stat_r[0:1, :Q_LORA], stat_r[1:2, :KV_LORA], stat1[0:1], stat1[1:2], stat2[0:1], stat2[1:2],
                               stat2[2:3, :LANES]], axis=1)
    partial = _pad_rows(partial.reshape(-1, LANES), 8)
    rest = [s[1] for s in sums]
    got_in, got_uq, got_ukv, every = _run_comm(_Plans([_ChipExchange(rest), _Gather([partial])]), rest + [partial],
                                               "rs_last_chip_exchange")

    upd = {}
    early = ["w_ff1", "w_ff2", "w_out", "w_o_mla", "w_o_dil"]
    items = [(weights[n][0], mom_m[n][0], mom_v[n][0], own, parts) for n, own, parts in
             zip(early, (dw_ff[0], dw_ff[1], dw_mid[2], dw_mid[0], dw_mid[1]), (r_ff1, r_ff2, r_out, r_oa, r_ob))]
    upd.update(zip(early, _adamw(items, "adamw_early_weights")))
    (in_t,) = _adamw([(weights["w_in"][0].T, mom_m["w_in"][0].T, mom_v["w_in"][0].T, sums[0][0], got_in)], "adamw_w_in")
    upd["w_in"] = tuple(a.T for a in in_t)
    for n, w, (own, _), parts in zip(first[1:], widths[1:], sums[1:], (got_uq, got_ukv)):
        (upd[n],) = _adamw([(weights[n][0], mom_m[n][0], mom_v[n][0], own[:, :w], parts[:, :, :w])], "adamw_" + n)
    (bg_upd,) = _adamw([(_pad_rows(b_gate[0], 16), _pad_rows(m_b_gate[0], 16), _pad_rows(v_b_gate[0], 16), dw_mid[3], r_bg)],
                       "adamw_b_gate")
    upd["b_gate"] = tuple(t[0:2] for t in bg_upd)

    small_w = [g_q_a, g_kv_a, ln1_g, ln1_b, ln2_g, ln2_b]
    small_m = [m_g_q_a, m_g_kv_a, m_ln1_g, m_ln1_b, m_ln2_g, m_ln2_b]
    small_v = [v_g_q_a, v_g_kv_a, v_ln1_g, v_ln1_b, v_ln2_g, v_ln2_b]
    small_widths = [a.shape[1] for a in small_w]

    def as_rows(vecs, extra):
        flat = jnp.concatenate(vecs + [jnp.zeros((1, extra), F32)], axis=1)
        return _pad_rows(flat.reshape(-1, LANES), 8)

    g_s, d_s, nm_s, nv_s = _adamw_small(every, as_rows(small_w, LANES), as_rows(small_m, LANES), as_rows(small_v, LANES))

    def split_small(a):
        flat = a.reshape(1, -1)
        out, c0 = [], 0
        for w in small_widths:
            out.append(flat[:, c0:c0 + w])
            c0 += w
        return out, flat[0, c0]

    g_small, loss = split_small(g_s)
    small = [g_small, split_small(d_s)[0], split_small(nm_s)[0], split_small(nv_s)[0]]

    order = ["w_in", "b_gate", "g_q_a", "w_uq", "g_kv_a", "w_ukv", "w_o_mla", "w_o_dil", "w_out", "ln1_g", "ln1_b", "w_ff1", "w_ff2", "ln2_g", "ln2_b"]
    small_names = ["g_q_a", "g_kv_a", "ln1_g", "ln1_b", "ln2_g", "ln2_b"]

    def pick(kind):
        return [small[kind][small_names.index(n)] if n in small_names else upd[n][kind][None] for n in order]

    return (loss, grad_x.reshape(batch, seq, D_MODEL), *pick(0), *pick(1), *pick(2), *pick(3))
```

```python
import functools
import math

import jax
import jax.numpy as jnp
from jax import lax
from jax.experimental import pallas as pl
from jax.experimental.pallas import tpu as pltpu

F32 = jnp.float32
BF16 = jnp.bfloat16
I32 = jnp.int32

D_MODEL = 1024
N_HEADS = 8
NOPE = 64
ROPE = 32
HEAD_V = 64
Q_LORA = 384
KV_LORA = 256
DIL_WIDTH = 512
D_FF = 4096
ROPE_THETA = 10000.0
LN_EPS = 1e-5
RMS_EPS = 1e-6
NEG = -1e30
ALPHA = 2.0 ** 0.25
MLA_SCALE = (NOPE + ROPE) ** -0.5
DIL_SCALE = 64 ** -0.5
ADAM_LR, ADAM_B1, ADAM_B2, ADAM_EPS, ADAM_WD, ADAM_STEP = 0.001, 0.9, 0.999, 1e-08, 0.01, 10

LANES = 128
PAIR_W = 256
N_PAIRS = N_HEADS // 2
LOW_W = 768
IN_EXT = LOW_W + 3 * DIL_WIDTH + 2 * D_MODEL
N_DEV = 8
FF_SHARD = D_FF // N_DEV
FF_STEP = 4
WGRAD_SHARDS = 4
TOKEN_TILE = 256
MIX_TILE = 512
ATTN_TILE = 512
VMEM_LIMIT = 56 << 20

MESH = pl.DeviceIdType.MESH
ANY = pl.BlockSpec(memory_space=pl.ANY)
CHIP_FLIPS = ((0, 0), (0, 1), (1, 0), (1, 1))
PEER_FLIPS = tuple((fx, fy, fc) for fx in (0, 1) for fy in (0, 1) for fc in (0, 1))[1:]


def _cp(*sem):
    return pltpu.CompilerParams(dimension_semantics=sem or None, vmem_limit_bytes=VMEM_LIMIT)


def _full(shape):
    nd = len(shape)
    return pl.BlockSpec(shape, lambda *_: (0,) * nd)


def _rows(tm, width):
    return pl.BlockSpec((tm, width), lambda i, *_: (i, 0))


def _dot(a, b):
    return jnp.dot(a, b, preferred_element_type=F32)


def _dot_nt(a, b):
    return lax.dot_general(a, b, (((1,), (1,)), ((), ())), preferred_element_type=F32)


def _dot_tn(a, b):
    return lax.dot_general(a, b, (((0,), (0,)), ((), ())), preferred_element_type=F32)


def _sigmoid(z):
    return 1.0 / (1.0 + jnp.exp(-z))


def _place():
    return lax.axis_index("x"), lax.axis_index("y"), lax.axis_index("c")


def _flip(v, f):
    return 1 - v if f else v


class _Gather:
    def __init__(self, shards):
        self.n = len(shards)
        self.out_shape = [jax.ShapeDtypeStruct((N_DEV, *s.shape), s.dtype) for s in shards]
        self.scratch = [pltpu.SemaphoreType.DMA((7 * self.n,)), pltpu.SemaphoreType.DMA((7 * self.n,)),
                        pltpu.SemaphoreType.DMA((self.n,))]

    def _copies(self, what, srcs, dsts, send, recv, local):
        x, y, c = _place()
        chips = [(_flip(x, fx), _flip(y, fy)) for fx, fy in CHIP_FLIPS[1:]]
        out = []
        for a in range(self.n):
            def slot(px, py, pc, a=a):
                return dsts[a].at[4 * px + 2 * py + pc]

            def copy(k, block, to, src=None, a=a, slot=slot):
                return pltpu.make_async_remote_copy(
                    src_ref=slot(*block) if src is None else src, dst_ref=slot(*block),
                    send_sem=send.at[7 * a + k], recv_sem=recv.at[7 * a + k], device_id=to, device_id_type=MESH)

            if what == "mine":
                out.append(pltpu.make_async_copy(srcs[a], slot(x, y, c), local.at[a]))
            elif what == "first":
                out.append(copy(0, (x, y, c), (x, y, 1 - c), src=srcs[a]))
                out += [copy(1 + j, (x, y, c), (*chip, c), src=srcs[a]) for j, chip in enumerate(chips)]
            elif what == "landed":
                out += [copy(1 + j, (*chip, c), (x, y, c)) for j, chip in enumerate(chips)]
            elif what == "passed":
                out += [copy(4 + j, (*chip, c), (x, y, 1 - c)) for j, chip in enumerate(chips)]
            else:
                out.append(copy(0, (x, y, 1 - c), (x, y, c)))
                out += [copy(4 + j, (*chip, 1 - c), (x, y, c)) for j, chip in enumerate(chips)]
        return out

    def start(self, *refs):
        for cp in self._copies("first", *refs) + self._copies("mine", *refs):
            cp.start()

    def forward(self, *refs):
        for landed, passed in zip(self._copies("landed", *refs), self._copies("passed", *refs)):
            landed.wait_recv()
            passed.start()

    def finish(self, *refs):
        for cp in self._copies("from_sibling", *refs):
            cp.wait_recv()
        for cp in self._copies("first", *refs) + self._copies("passed", *refs):
            cp.wait_send()
        for cp in self._copies("mine", *refs):
            cp.wait()


class _Scatter:
    def __init__(self, arrays):
        self.n = len(arrays)
        self.out_shape = [jax.ShapeDtypeStruct((7, *a.shape[1:]), a.dtype) for a in arrays]
        self.scratch = [pltpu.SemaphoreType.DMA((7 * self.n,)), pltpu.SemaphoreType.DMA((7 * self.n,))]

    def _copies(self, srcs, dsts, send, recv):
        x, y, c = _place()
        out = []
        for a in range(self.n):
            for k, (fx, fy, fc) in enumerate(PEER_FLIPS):
                px, py, pc = _flip(x, fx), _flip(y, fy), _flip(c, fc)
                out.append(pltpu.make_async_remote_copy(
                    src_ref=srcs[a].at[4 * px + 2 * py + pc], dst_ref=dsts[a].at[k],
                    send_sem=send.at[7 * a + k], recv_sem=recv.at[7 * a + k], device_id=(px, py, pc), device_id_type=MESH))
        return out

    def start(self, *refs):
        for cp in self._copies(*refs):
            cp.start()

    def forward(self, *refs):
        pass

    def finish(self, *refs):
        for cp in self._copies(*refs):
            cp.wait_send()
        for cp in self._copies(*refs):
            cp.wait_recv()


class _ChipExchange:
    def __init__(self, arrays):
        self.n = len(arrays)
        self.out_shape = [jax.ShapeDtypeStruct(a.shape, a.dtype) for a in arrays]
        self.scratch = [pltpu.SemaphoreType.DMA((3 * self.n,)), pltpu.SemaphoreType.DMA((3 * self.n,))]

    def _copies(self, srcs, dsts, send, recv):
        x, y, c = _place()
        return [pltpu.make_async_remote_copy(
            src_ref=srcs[a].at[k], dst_ref=dsts[a].at[k], send_sem=send.at[3 * a + k], recv_sem=recv.at[3 * a + k],
            device_id=(_flip(x, fx), _flip(y, fy), c), device_id_type=MESH)
            for a in range(self.n) for k, (fx, fy) in enumerate(CHIP_FLIPS[1:])]

    def start(self, *refs):
        for cp in self._copies(*refs):
            cp.start()

    def forward(self, *refs):
        pass

    def finish(self, *refs):
        for cp in self._copies(*refs):
            cp.wait_send()
        for cp in self._copies(*refs):
            cp.wait_recv()


class _Plans:
    def __init__(self, plans):
        self.plans = plans
        self.n = sum(p.n for p in plans)
        self.out_shape = [s for p in plans for s in p.out_shape]
        self.scratch = [s for p in plans for s in p.scratch]

    def _each(self, phase, srcs, dsts, *sems):
        i0 = s0 = 0
        for p in self.plans:
            getattr(p, phase)(srcs[i0:i0 + p.n], dsts[i0:i0 + p.n], *sems[s0:s0 + len(p.scratch)])
            i0, s0 = i0 + p.n, s0 + len(p.scratch)

    def start(self, *refs):
        self._each("start", *refs)

    def forward(self, *refs):
        self._each("forward", *refs)

    def finish(self, *refs):
        self._each("finish", *refs)


def _run_comm(comm, arrays, name):
    n = comm.n

    def body(*refs):
        args = (refs[:n], refs[n:2 * n], *refs[2 * n:])
        comm.start(*args)
        comm.forward(*args)
        comm.finish(*args)

    return pl.pallas_call(body, name=name, out_shape=comm.out_shape, in_specs=[ANY] * n, out_specs=[ANY] * n,
                          scratch_shapes=comm.scratch)(*arrays)


def _rs_sibling(arrays, name):
    n = len(arrays)

    def body(*refs):
        srcs, got, (send, recv) = refs[:n], refs[n:2 * n], refs[2 * n:]
        x, y, c = _place()
        copies = []
        for a in range(n):
            for r, (fx, fy) in enumerate(CHIP_FLIPS):
                chip = 2 * _flip(x, fx) + _flip(y, fy)
                copies.append(pltpu.make_async_remote_copy(
                    src_ref=srcs[a].at[2 * chip + 1 - c], dst_ref=got[a].at[r], send_sem=send.at[4 * a + r],
                    recv_sem=recv.at[4 * a + r], device_id=(x, y, 1 - c), device_id_type=MESH))
        for cp in copies:
            cp.start()
        for cp in copies:
            cp.wait_send()
        for cp in copies:
            cp.wait_recv()

    return pl.pallas_call(
        body, name=name, out_shape=[jax.ShapeDtypeStruct((4, *a.shape[1:]), a.dtype) for a in arrays],
        in_specs=[ANY] * n, out_specs=[ANY] * n,
        scratch_shapes=[pltpu.SemaphoreType.DMA((4 * n,)), pltpu.SemaphoreType.DMA((4 * n,))],
    )(*arrays)


def _chip_slots():
    x, y, c = _place()
    return jnp.stack([4 * _flip(x, fx) + 2 * _flip(y, fy) + c for fx, fy in CHIP_FLIPS]).astype(I32)


def _tiles(rows, cols, steps=4):
    if rows % (16 * steps) == 0:
        return steps, (rows // steps, cols), lambda i: (i, 0)
    if cols % (LANES * steps) == 0:
        return steps, (rows, cols // steps), lambda i: (0, i)
    return 1, (rows, cols), lambda i: (0, 0)


def _pair_sum(full, theirs, name):
    _, rows, cols = theirs.shape
    steps, tile, at = _tiles(rows, cols)

    def body(slots_ref, m0_ref, m1_ref, m2_ref, m3_ref, b_ref, own_ref, rest_ref):
        own_ref[...] = m0_ref[...].astype(F32) + b_ref[0].astype(F32)
        for k, m_ref in enumerate((m1_ref, m2_ref, m3_ref)):
            rest_ref[k] = (m_ref[...].astype(F32) + b_ref[k + 1].astype(F32)).astype(BF16)

    def mine(k):
        return pl.BlockSpec((None, *tile), lambda i, slots: (slots[k], *at(i)))

    return pl.pallas_call(
        body, name=name,
        grid_spec=pltpu.PrefetchScalarGridSpec(
            num_scalar_prefetch=1, grid=(steps,),
            in_specs=[mine(0), mine(1), mine(2), mine(3), pl.BlockSpec((4, *tile), lambda i, slots: (0, *at(i)))],
            out_specs=(pl.BlockSpec(tile, lambda i, slots: at(i)), pl.BlockSpec((3, *tile), lambda i, slots: (0, *at(i))))),
        out_shape=(jax.ShapeDtypeStruct((rows, cols), F32), jax.ShapeDtypeStruct((3, rows, cols), BF16)),
        compiler_params=_cp("parallel"),
    )(_chip_slots(), full, full, full, full, theirs)


def _head_lanes(width, h):
    lane = lax.broadcasted_iota(I32, (1, width), 1)
    if width == LANES:
        return (lane >= 64 * h) & (lane < 64 * h + 64)
    nope = (lane >= NOPE * h) & (lane < NOPE * h + NOPE)
    rope = (lane >= 2 * NOPE + ROPE * h) & (lane < 2 * NOPE + ROPE * h + ROPE)
    return nope | rope


def _dilated_bias_table(seq):
    t = min(ATTN_TILE, seq)
    nd = seq // t

    def body(o_ref):
        delta = pl.program_id(0) * t + lax.broadcasted_iota(I32, (t, t), 1) - lax.broadcasted_iota(I32, (t, t), 0)
        mult = ((delta <= 128).astype(I32) + (((delta & 3) == 0) & (delta <= 512)).astype(I32)
                + ((delta & 15) == 0).astype(I32))
        logm = jnp.where(mult == 3, math.log(3.0), jnp.where(mult == 2, math.log(2.0), 0.0))
        valid = (delta >= 0) & (mult > 0)
        dist = delta.astype(F32)
        for h in range(N_HEADS):
            o_ref[h] = jnp.where(valid, logm - 2.0 ** (-(h + 1)) * dist, NEG)

    return pl.pallas_call(
        body, name="dilated_bias_table", grid=(nd,), out_shape=jax.ShapeDtypeStruct((N_HEADS, nd, t, t), F32),
        out_specs=pl.BlockSpec((N_HEADS, None, t, t), lambda d: (0, d, 0, 0)),
        compiler_params=_cp("parallel"),
    )()


def _comm_hooks(comm, refs, n_in, n_out):
    if comm is None:
        return refs[:n_in], refs[n_in:n_in + n_out], refs[n_in + n_out:], None
    n = comm.n
    ins, srcs = refs[:n_in], refs[n_in:n_in + n]
    outs, dsts = refs[n_in + n:n_in + n + n_out], refs[n_in + n + n_out:n_in + 2 * n + n_out]
    rest = refs[n_in + 2 * n + n_out:]
    own = len(rest) - len(comm.scratch)
    return ins, outs, rest[:own], (srcs, dsts, *rest[own:])


def _attn_fwd(q, k, v, bias, *, batch, seq, width, col0, dilated, scale, name, comm=None, comm_arrays=()):
    t = min(ATTN_TILE, seq)
    nq = seq // t
    half = t // 2
    cq, ck, cv = col0
    pre = scale if dilated else 1.0
    steps = batch * N_PAIRS

    def body(*refs):
        (q_ref, k_ref, v_ref, bias_ref), (o_ref, lse_ref), (v_heads,), plan = _comm_hooks(comm, refs, 4, 2)
        step_no = pl.program_id(0) * N_PAIRS + pl.program_id(1)
        if plan:
            pl.when(step_no == 0)(lambda: comm.start(*plan))
            pl.when(step_no == (3 * steps) // 4)(lambda: comm.forward(*plan))
        v_all = v_ref[...].astype(F32)
        for h in (0, 1):
            v_heads[h] = jnp.transpose(jnp.where(_head_lanes(LANES, h), v_all, 0.0)).astype(BF16)
        top = lax.broadcasted_iota(I32, (LANES, t), 0) < HEAD_V
        causal = lax.broadcasted_iota(I32, (t, t), 0) <= lax.broadcasted_iota(I32, (t, t), 1)
        def heads(i):
            q2 = q_ref[pl.ds(pl.multiple_of(i * t, t), t), :]
            q2 = q2 * pre if dilated else q2
            return [jnp.where(_head_lanes(width, h), q2, jnp.zeros_like(q2)) for h in (0, 1)]

        def scores(qh, j):
            kj = k_ref[pl.ds(pl.multiple_of(j * t, t), t), :]
            return tuple(_dot_nt(kj, qh[h]) for h in (0, 1))

        lax.fori_loop(0, nq, functools.partial(query_tile, heads, scores, bias_ref, o_ref, lse_ref, v_heads, top, causal),
                      scores(heads(0), 0))
        if plan:
            pl.when(step_no == steps - 1)(lambda: comm.finish(*plan))

    def query_tile(heads, scores, bias_ref, o_ref, lse_ref, v_heads, top, causal, i, first):
        qs = pl.multiple_of(i * t, t)
        qh = heads(i)

        def step(j, carry, last):
            m0, l0, m1, l1, acc, s0, s1 = carry
            ahead = scores(heads(jnp.minimum(i + 1, nq - 1)), 0) if last else scores(qh, j + 1)
            ks = pl.multiple_of(j * t, t)
            new, alphas, pv = [], [], []

            def online(h, m, l, s, keys, queries):
                s = s[keys, queries]
                if dilated:
                    s = s + (bias_ref[h, 0, keys, queries] if last else bias_ref[h, i - j])
                else:
                    s = s * scale
                    if last:
                        s = jnp.where(causal[keys, queries], s, NEG)
                m, l = m[:, queries], l[:, queries]
                m_new = jnp.maximum(m, jnp.max(s, axis=0, keepdims=True))
                a = jnp.exp(m - m_new)
                p = jnp.exp(s - m_new)
                v_keys = v_heads[h, :, pl.ds(ks, t)]
                return m_new, a * l + jnp.sum(p, axis=0, keepdims=True), a, _dot(v_keys[:, keys], p.astype(BF16))

            for h, (m, l, s) in enumerate(((m0, l0, s0), (m1, l1, s1))):
                if last and half % LANES == 0:
                    parts = [online(h, m, l, s, slice(0, half), slice(0, half)),
                             online(h, m, l, s, slice(0, t), slice(half, t))]
                    m_new, l_new, a, pv_h = (jnp.concatenate(x, axis=1) for x in zip(*parts))
                else:
                    m_new, l_new, a, pv_h = online(h, m, l, s, slice(0, t), slice(0, t))
                new += [m_new, l_new]
                alphas.append(a)
                pv.append(pv_h)
            acc = jnp.where(top, alphas[0], alphas[1]) * acc + pv[0] + pv[1]
            return (*new, acc, *ahead)

        row = jnp.full((1, t), NEG, F32)
        zero = jnp.zeros((1, t), F32)
        init = (row, zero, row, zero, jnp.zeros((LANES, t), F32), *first)
        m0, l0, m1, l1, acc, *following = step(i, lax.fori_loop(0, i, functools.partial(step, last=False), init), True)
        o_ref[pl.ds(qs, t), :] = jnp.transpose(acc * jnp.where(top, 1.0 / l0, 1.0 / l1)).astype(BF16)
        r = lax.broadcasted_iota(I32, (8, t), 0)
        lse_ref[:, pl.ds(qs, t)] = jnp.where(r == 0, m0 + jnp.log(l0), jnp.where(r == 1, m1 + jnp.log(l1), 0.0))
        return tuple(following)

    bias_spec = (pl.BlockSpec((2, nq, t, t), lambda b, p: (p, 0, 0, 0)) if dilated
                 else pl.BlockSpec((None, 8, LANES), lambda b, p: (0, 0, 0)))
    n = comm.n if comm else 0
    return pl.pallas_call(
        body, name=name, grid=(batch, N_PAIRS),
        out_shape=[jax.ShapeDtypeStruct((batch * seq, DIL_WIDTH), BF16), jax.ShapeDtypeStruct((batch * N_PAIRS, 8, seq), F32)]
        + (comm.out_shape if comm else []),
        in_specs=[pl.BlockSpec((seq, width), lambda b, p: (b, cq + p)),
                  pl.BlockSpec((seq, width), lambda b, p: (b, ck + p)),
                  pl.BlockSpec((seq, LANES), lambda b, p: (b, cv + p)),
                  bias_spec] + [ANY] * n,
        out_specs=[pl.BlockSpec((seq, LANES), lambda b, p: (b, p)),
                   pl.BlockSpec((None, 8, seq), lambda b, p: (b * N_PAIRS + p, 0, 0))] + [ANY] * n,
        scratch_shapes=[pltpu.VMEM((2, LANES, seq), BF16)] + (comm.scratch if comm else []),
        compiler_params=_cp("arbitrary", "arbitrary") if comm else _cp("parallel", "parallel"),
    )(q, k, v, bias, *comm_arrays)


def _attn_bwd(q, k, v, o, do, lse, bias, *, batch, seq, width, col0, dilated, scale, name, comm=None, comm_arrays=()):
    t = min(ATTN_TILE, seq)
    nq = seq // t
    half = t // 2
    cq, ck, cv = col0
    pre = scale if dilated else 1.0
    dq_transposed = width == LANES
    steps = batch * N_PAIRS

    def body(*refs):
        ins, (dq_ref, dk_ref, dv_ref), (dq_acc, dk_acc, dv_acc, rowdot, q_heads, do_heads), plan = _comm_hooks(comm, refs, 7, 3)
        q_ref, k_ref, v_ref, o_ref, do_ref, lse_ref, bias_ref = ins
        step_no = pl.program_id(0) * N_PAIRS + pl.program_id(1)
        if plan:
            pl.when(step_no == 0)(lambda: comm.start(*plan))
        wlane = [_head_lanes(width, h) for h in (0, 1)]
        vlane = [_head_lanes(LANES, h) for h in (0, 1)]
        causal = lax.broadcasted_iota(I32, (t, t), 0) <= lax.broadcasted_iota(I32, (t, t), 1)
        q_all = q_ref[...] * pre if dilated else q_ref[...]
        for h in (0, 1):
            q_heads[h] = jnp.where(wlane[h], q_all, jnp.zeros_like(q_all))
            do_heads[h] = jnp.where(vlane[h], do_ref[...], jnp.zeros_like(do_ref[...]))
        prod = jnp.transpose(do_ref[...].astype(F32) * o_ref[...].astype(F32))
        rowdot[0:1, :] = jnp.sum(prod[0:HEAD_V], axis=0, keepdims=True)
        rowdot[1:2, :] = jnp.sum(prod[HEAD_V:], axis=0, keepdims=True)
        dq_acc[...] = jnp.zeros_like(dq_acc)

        def k_tile(j, _):
            ks = pl.multiple_of(j * t, t)
            kj = k_ref[pl.ds(ks, t), :]
            vj = v_ref[pl.ds(ks, t), :]
            kh = [jnp.where(wlane[h], kj, jnp.zeros_like(kj)) for h in (0, 1)]
            if dq_transposed:
                kh = [jnp.transpose(kh[h].astype(F32)).astype(BF16) for h in (0, 1)]
            dk_acc[...] = jnp.zeros_like(dk_acc)
            dv_acc[...] = jnp.zeros_like(dv_acc)

            def operands(i):
                qs = pl.multiple_of(i * t, t)
                return [q_heads[h, pl.ds(qs, t), :] for h in (0, 1)], [do_heads[h, pl.ds(qs, t), :] for h in (0, 1)]

            def scores(i):
                qih, _ = operands(i)
                return tuple(_dot_nt(kj, qih[h]) for h in (0, 1))

            def q_tile(n, carry, last):
                i = nq - 1 - n
                ahead = () if last else scores(i - 1)
                qs = pl.multiple_of(i * t, t)
                qih, doih = operands(i)

                def block(keys, queries):
                    count = queries.stop - queries.start
                    at = pl.ds(qs + queries.start, count)
                    dps = [_dot_nt(vj[keys], doih[h][queries]) for h in (0, 1)]
                    dq_b = jnp.zeros((width, count) if dq_transposed else (count, width), F32)
                    for h, (s, dp) in enumerate(zip(carry, dps)):
                        s = s[keys, queries]
                        if dilated:
                            s = s + (bias_ref[h, 0, keys, queries] if last else bias_ref[h, i - j])
                        else:
                            s = s * scale
                            if last:
                                s = jnp.where(causal[keys, queries], s, NEG)
                        p = jnp.exp(s - lse_ref[h:h + 1, at])
                        ds = p * (dp - rowdot[h:h + 1, at])
                        ds = (ds if dilated else ds * scale).astype(BF16)
                        dv_acc[keys, :] += _dot(p.astype(BF16), doih[h][queries])
                        dk_acc[keys, :] += _dot(ds, qih[h][queries])
                        dq_b = dq_b + (_dot(kh[h][:, keys], ds) if dq_transposed else _dot_tn(ds, kh[h][keys]))
                    if dq_transposed:
                        dq_acc[:, at] += dq_b
                    else:
                        dq_acc[at, :] += dq_b

                if last and half % LANES == 0:
                    block(slice(0, half), slice(0, half))
                    block(slice(0, t), slice(half, t))
                else:
                    block(slice(0, t), slice(0, t))
                return ahead

            q_tile(nq - 1 - j, lax.fori_loop(0, nq - 1 - j, functools.partial(q_tile, last=False), scores(nq - 1)), True)
            dk_ref[pl.ds(ks, t), :] = dk_acc[...].astype(BF16)
            dv_ref[pl.ds(ks, t), :] = dv_acc[...].astype(BF16)
            return 0

        lax.fori_loop(0, nq, k_tile, 0)
        dq_ref[...] = ((jnp.transpose(dq_acc[...]) if dq_transposed else dq_acc[...]) * pre).astype(BF16)
        if plan:
            pl.when(step_no == steps - 1)(lambda: comm.finish(*plan))

    tokens = batch * seq
    bias_spec = (pl.BlockSpec((2, nq, t, t), lambda b, p: (p, 0, 0, 0)) if dilated
                 else pl.BlockSpec((None, 8, LANES), lambda b, p: (0, 0, 0)))
    n = comm.n if comm else 0
    return pl.pallas_call(
        body, name=name, grid=(batch, N_PAIRS),
        out_shape=[jax.ShapeDtypeStruct((tokens, N_PAIRS * width), BF16), jax.ShapeDtypeStruct((tokens, N_PAIRS * width), BF16),
                   jax.ShapeDtypeStruct((tokens, DIL_WIDTH), BF16)] + (comm.out_shape if comm else []),
        in_specs=[pl.BlockSpec((seq, width), lambda b, p: (b, cq + p)),
                  pl.BlockSpec((seq, width), lambda b, p: (b, ck + p)),
                  pl.BlockSpec((seq, LANES), lambda b, p: (b, cv + p)),
                  pl.BlockSpec((seq, LANES), lambda b, p: (b, p)),
                  pl.BlockSpec((seq, LANES), lambda b, p: (b, p)),
                  pl.BlockSpec((None, 8, seq), lambda b, p: (b * N_PAIRS + p, 0, 0)),
                  bias_spec] + [ANY] * n,
        out_specs=[pl.BlockSpec((seq, width), lambda b, p: (b, p)),
                   pl.BlockSpec((seq, width), lambda b, p: (b, p)),
                   pl.BlockSpec((seq, LANES), lambda b, p: (b, p))] + [ANY] * n,
        scratch_shapes=[pltpu.VMEM((width, seq) if dq_transposed else (seq, width), F32),
                        pltpu.VMEM((t, width), F32), pltpu.VMEM((t, LANES), F32),
                        pltpu.VMEM((8, seq), F32), pltpu.VMEM((2, seq, width), BF16), pltpu.VMEM((2, seq, LANES), BF16)]
        + (comm.scratch if comm else []),
        compiler_params=_cp("arbitrary", "arbitrary") if comm else _cp("parallel", "parallel"),
    )(q, k, v, o, do, lse, bias, *comm_arrays)


def _rms(xf, g):
    r = lax.rsqrt(jnp.mean(xf * xf, axis=1, keepdims=True) + RMS_EPS)
    return xf * r * g, r


def _rms_bwd(dy, xf, r, g):
    gy = dy * g
    dx = r * gy - xf * (r * r * r) * jnp.mean(gy * xf, axis=1, keepdims=True)
    return dx, dy * xf * r


def _ln_bwd(dy, xhat, rstd, g):
    dxh = dy * g
    return rstd * (dxh - jnp.mean(dxh, axis=1, keepdims=True) - xhat * jnp.mean(dxh * xhat, axis=1, keepdims=True))


def _rope_slabs(q, cos, sin, transpose):
    first_half = (lax.broadcasted_iota(I32, (1, LANES), 1) % ROPE) < ROPE // 2
    out = []
    for p in range(N_PAIRS):
        blk = q[:, p * PAIR_W + LANES:(p + 1) * PAIR_W]
        y = blk * sin if transpose else blk
        up, down = pltpu.roll(y, LANES - ROPE // 2, 1), pltpu.roll(y, ROPE // 2, 1)
        rot = jnp.where(first_half, up, -down) if transpose else jnp.where(first_half, -up, down) * sin
        out += [q[:, p * PAIR_W:p * PAIR_W + LANES], blk * cos + rot]
    return jnp.concatenate(out, axis=1)


def _fwd_proj(x, w_in_ext, w1, wk, wv, g_q, g_kv, cext, sext, cs128, *, seq):
    tokens = x.shape[0]
    tm = min(MIX_TILE, seq)
    ns = seq // tm

    def body(x_ref, win_ref, w1_ref, wk_ref, wv_ref, gq_ref, gkv_ref, c_ref, s_ref, cs_ref,
             low_ref, gates_ref, qkvd_ref, qp_ref, kp_ref, vm_ref, qn_ref, kvn_ref, xb_ref):
        xt = x_ref[...].astype(BF16)
        xb_ref[...] = xt
        low = _dot(xt, win_ref[:, 0:LOW_W])
        low_ref[...] = low
        qkvd_ref[...] = _dot(xt, win_ref[:, LOW_W:LOW_W + 3 * DIL_WIDTH]).astype(BF16)
        gates_ref[...] = _dot(xt, win_ref[:, LOW_W + 3 * DIL_WIDTH:]).astype(BF16)
        qn = _rms(low[:, 0:Q_LORA], gq_ref[...])[0].astype(BF16)
        kvn = _rms(low[:, Q_LORA:Q_LORA + KV_LORA], gkv_ref[...])[0].astype(BF16)
        qn_ref[...] = qn
        kvn_ref[...] = kvn
        qp_ref[...] = _rope_slabs(_dot(qn, w1_ref[...]), c_ref[...], s_ref[...], False).astype(BF16)
        kr = low[:, Q_LORA + KV_LORA:] * cs_ref[...]
        kr = kr + pltpu.roll(kr, LANES - ROPE, 1)
        lane = lax.broadcasted_iota(I32, kr.shape, 1)
        kr = jnp.where(lane < ROPE, kr, 0.0)
        kr = (kr + pltpu.roll(kr, ROPE, 1)).astype(BF16)
        kn = _dot(kvn, wk_ref[...]).astype(BF16)
        kp_ref[...] = jnp.concatenate([blk for p in range(N_PAIRS) for blk in (kn[:, p * LANES:(p + 1) * LANES], kr)], axis=1)
        vm_ref[...] = _dot(kvn, wv_ref[...]).astype(BF16)

    n_gates = 2 * D_MODEL
    outs = [(LOW_W, F32), (n_gates, BF16), (3 * DIL_WIDTH, BF16), (N_PAIRS * PAIR_W, BF16), (N_PAIRS * PAIR_W, BF16),
            (DIL_WIDTH, BF16), (Q_LORA, BF16), (KV_LORA, BF16), (D_MODEL, BF16)]
    return pl.pallas_call(
        body, name="fwd_proj", grid=(tokens // tm,),
        out_shape=tuple(jax.ShapeDtypeStruct((tokens, w), dt) for w, dt in outs),
        in_specs=[_rows(tm, D_MODEL), _full(w_in_ext.shape), _full(w1.shape), _full(wk.shape),
                  _full(wv.shape), _full(g_q.shape), _full(g_kv.shape),
                  pl.BlockSpec((tm, LANES), lambda i: (i % ns, 1)),
                  pl.BlockSpec((tm, LANES), lambda i: (i % ns, 1)),
                  pl.BlockSpec((tm, LANES), lambda i: (i % ns, 0))],
        out_specs=tuple(_rows(tm, w) for w, _ in outs),
        compiler_params=_cp("parallel"),
    )(x, w_in_ext, w1, wk, wv, g_q, g_kv, cext, sext, cs128)


def _fwd_mix(o_a, o_b, gates, x, b_gate, w_oa, w_ob, w_out, ln_g, ln_b, *, seq):
    tokens = x.shape[0]
    tm = min(MIX_TILE, seq)

    def body(oa_ref, ob_ref, gt_ref, x_ref, bg_ref, woa_ref, wob_ref, wout_ref, g_ref, b_ref,
             hb_ref, xhat_ref, rstd_ref, ya_ref, yb_ref, mix_ref):
        ya = _dot(oa_ref[...], woa_ref[...])
        yb = _dot(ob_ref[...], wob_ref[...])
        g0 = _sigmoid(gt_ref[:, 0:D_MODEL].astype(F32) + bg_ref[0:1, :])
        g1 = _sigmoid(gt_ref[:, D_MODEL:].astype(F32) + bg_ref[1:2, :])
        mix = (g0 * ya + g1 * yb).astype(BF16)
        z = ALPHA * x_ref[...] + _dot(mix, wout_ref[...])
        zc = z - jnp.mean(z, axis=1, keepdims=True)
        rstd = lax.rsqrt(jnp.mean(zc * zc, axis=1, keepdims=True) + LN_EPS)
        xhat = zc * rstd
        hb_ref[...] = (xhat * g_ref[...] + b_ref[...]).astype(BF16)
        xhat_ref[...] = xhat
        rstd_ref[...] = jnp.broadcast_to(rstd, (tm, LANES))
        ya_ref[...] = ya.astype(BF16)
        yb_ref[...] = yb.astype(BF16)
        mix_ref[...] = mix

    outs = [(D_MODEL, BF16), (D_MODEL, F32), (LANES, F32), (D_MODEL, BF16), (D_MODEL, BF16), (D_MODEL, BF16)]
    return pl.pallas_call(
        body, name="fwd_mix", grid=(tokens // tm,),
        out_shape=tuple(jax.ShapeDtypeStruct((tokens, w), dt) for w, dt in outs),
        in_specs=[_rows(tm, DIL_WIDTH), _rows(tm, DIL_WIDTH), _rows(tm, 2 * D_MODEL), _rows(tm, D_MODEL),
                  _full(b_gate.shape), _full(w_oa.shape), _full(w_ob.shape), _full(w_out.shape),
                  _full(ln_g.shape), _full(ln_b.shape)],
        out_specs=tuple(_rows(tm, w) for w, _ in outs),
        compiler_params=_cp("parallel"),
    )(o_a, o_b, gates, x, b_gate, w_oa, w_ob, w_out, ln_g, ln_b)


def _fwd_mlp(hb, xhat1, target, w_ff1, w_ff2, ln1_g, ln1_b, ln_g, ln_b, *, seq):
    tokens = hb.shape[0]
    tm = min(2 * TOKEN_TILE, seq)
    tf = FF_SHARD
    nf = N_DEV // FF_STEP

    def body(hb_ref, xh_ref, tg_ref, w1_ref, w2_ref, g1_ref, b1_ref, g_ref, b_ref, u_ref, dz_ref, dzb_ref, stat_ref, acc):
        i, j = pl.program_id(0), pl.program_id(1)

        @pl.when((i == 0) & (j == 0))
        def _():
            stat_ref[...] = jnp.zeros_like(stat_ref)

        @pl.when(j == 0)
        def _():
            acc[...] = jnp.zeros_like(acc)

        acts = []
        for s in range(FF_STEP):
            u = _dot(hb_ref[...], w1_ref[s])
            u_ref[:, s * tf:(s + 1) * tf] = u.astype(BF16)
            acts.append(jnp.square(jnp.maximum(u, 0.0)).astype(BF16))
        acc[...] += _dot(jnp.concatenate(acts, axis=1), w2_ref[...])

        @pl.when(j == nf - 1)
        def _():
            z = ALPHA * (xh_ref[...] * g1_ref[...] + b1_ref[...]) + acc[...]
            zc = z - jnp.mean(z, axis=1, keepdims=True)
            rstd = lax.rsqrt(jnp.mean(zc * zc, axis=1, keepdims=True) + LN_EPS)
            xhat = zc * rstd
            err = xhat * g_ref[...] + b_ref[...] - tg_ref[...]
            dy = err * (1.0 / D_MODEL)
            dz = _ln_bwd(dy, xhat, rstd, g_ref[...])
            dz_ref[...] = dz
            dzb_ref[...] = dz.astype(BF16)
            stat_ref[0:1, :] += jnp.sum(dy * xhat, axis=0, keepdims=True)
            stat_ref[1:2, :] += jnp.sum(dy, axis=0, keepdims=True)
            stat_ref[2:3, :] += jnp.sum(jnp.sum(err * err, axis=1, keepdims=True), axis=0, keepdims=True) * (0.5 / D_MODEL)

    return pl.pallas_call(
        body, name="fwd_mlp", grid=(tokens // tm, nf),
        out_shape=(jax.ShapeDtypeStruct((tokens, D_FF), BF16), jax.ShapeDtypeStruct((tokens, D_MODEL), F32),
                   jax.ShapeDtypeStruct((tokens, D_MODEL), BF16), jax.ShapeDtypeStruct((8, D_MODEL), F32)),
        in_specs=[_rows(tm, D_MODEL), _rows(tm, D_MODEL), _rows(tm, D_MODEL),
                  pl.BlockSpec((FF_STEP, D_MODEL, tf), lambda i, j: (j, 0, 0)),
                  pl.BlockSpec((FF_STEP * tf, D_MODEL), lambda i, j: (j, 0)),
                  _full(ln1_g.shape), _full(ln1_b.shape), _full(ln_g.shape), _full(ln_b.shape)],
        out_specs=(pl.BlockSpec((tm, FF_STEP * tf), lambda i, j: (i, j)), _rows(tm, D_MODEL), _rows(tm, D_MODEL),
                   _full((8, D_MODEL))),
        scratch_shapes=[pltpu.VMEM((tm, D_MODEL), F32)],
        compiler_params=_cp("arbitrary", "arbitrary"),
    )(hb, xhat1, target, w_ff1, w_ff2, ln1_g, ln1_b, ln_g, ln_b)


def _bwd_mlp(dz2, dz2b, u, xhat1, rstd1, w_ff1, w_ff2, ln_g, *, seq):
    tokens = dz2.shape[0]
    tm = min(2 * TOKEN_TILE, seq)
    tf = FF_SHARD
    nf = N_DEV // FF_STEP

    def body(dz_ref, dzb_ref, u_ref, xh_ref, rs_ref, w1_ref, w2_ref, g_ref, du_ref, dz1_ref, dz1b_ref, stat_ref, acc):
        i, j = pl.program_id(0), pl.program_id(1)

        @pl.when((i == 0) & (j == 0))
        def _():
            stat_ref[...] = jnp.zeros_like(stat_ref)

        @pl.when(j == 0)
        def _():
            acc[...] = jnp.zeros_like(acc)

        da = _dot_nt(dzb_ref[...], w2_ref[...])
        du = (da * (2.0 * jnp.maximum(u_ref[...].astype(F32), 0.0))).astype(BF16)
        du_ref[...] = du
        part = _dot_nt(du[:, 0:tf], w1_ref[0])
        for s in range(1, FF_STEP):
            part = part + _dot_nt(du[:, s * tf:(s + 1) * tf], w1_ref[s])
        acc[...] += part

        @pl.when(j == nf - 1)
        def _():
            dh = ALPHA * dz_ref[...] + acc[...]
            xhat = xh_ref[...]
            dz1 = _ln_bwd(dh, xhat, rs_ref[:, 0:1], g_ref[...])
            dz1_ref[...] = dz1
            dz1b_ref[...] = dz1.astype(BF16)
            stat_ref[0:1, :] += jnp.sum(dh * xhat, axis=0, keepdims=True)
            stat_ref[1:2, :] += jnp.sum(dh, axis=0, keepdims=True)

    return pl.pallas_call(
        body, name="bwd_mlp", grid=(tokens // tm, nf),
        out_shape=(jax.ShapeDtypeStruct((tokens, D_FF), BF16), jax.ShapeDtypeStruct((tokens, D_MODEL), F32),
                   jax.ShapeDtypeStruct((tokens, D_MODEL), BF16), jax.ShapeDtypeStruct((8, D_MODEL), F32)),
        in_specs=[_rows(tm, D_MODEL), _rows(tm, D_MODEL), pl.BlockSpec((tm, FF_STEP * tf), lambda i, j: (i, j)),
                  _rows(tm, D_MODEL), _rows(tm, LANES),
                  pl.BlockSpec((FF_STEP, D_MODEL, tf), lambda i, j: (j, 0, 0)),
                  pl.BlockSpec((FF_STEP * tf, D_MODEL), lambda i, j: (j, 0)),
                  _full(ln_g.shape)],
        out_specs=(pl.BlockSpec((tm, FF_STEP * tf), lambda i, j: (i, j)), _rows(tm, D_MODEL), _rows(tm, D_MODEL),
                   _full((8, D_MODEL))),
        scratch_shapes=[pltpu.VMEM((tm, D_MODEL), F32)],
        compiler_params=_cp("arbitrary", "arbitrary"),
    )(dz2, dz2b, u, xhat1, rstd1, w_ff1, w_ff2, ln_g)


def _bwd_mix(dz1b, gates, y_a, y_b, b_gate, w_oa, w_ob, w_out, *, seq):
    tokens = dz1b.shape[0]
    tm = min(MIX_TILE, seq)

    def body(dz_ref, gt_ref, ya_ref, yb_ref, bg_ref, woa_ref, wob_ref, wout_ref,
             dgt_ref, dya_ref, dyb_ref, doa_ref, dob_ref, stat_ref):
        @pl.when(pl.program_id(0) == 0)
        def _():
            stat_ref[...] = jnp.zeros_like(stat_ref)

        dmix = _dot_nt(dz_ref[...], wout_ref[...])
        for k, (y_ref, w_ref, dy_ref, do_ref) in enumerate(((ya_ref, woa_ref, dya_ref, doa_ref), (yb_ref, wob_ref, dyb_ref, dob_ref))):
            g = _sigmoid(gt_ref[:, k * D_MODEL:(k + 1) * D_MODEL].astype(F32) + bg_ref[k:k + 1, :])
            dgate = dmix * y_ref[...].astype(F32) * g * (1.0 - g)
            dgt_ref[:, k * D_MODEL:(k + 1) * D_MODEL] = dgate.astype(BF16)
            stat_ref[k:k + 1, :] += jnp.sum(dgate, axis=0, keepdims=True)
            dy = (dmix * g).astype(BF16)
            dy_ref[...] = dy
            do_ref[...] = _dot_nt(dy, w_ref[...]).astype(BF16)

    outs = [(2 * D_MODEL, BF16), (D_MODEL, BF16), (D_MODEL, BF16), (DIL_WIDTH, BF16), (DIL_WIDTH, BF16)]
    return pl.pallas_call(
        body, name="bwd_mix", grid=(tokens // tm,),
        out_shape=tuple(jax.ShapeDtypeStruct((tokens, w), dt) for w, dt in outs) + (jax.ShapeDtypeStruct((8, D_MODEL), F32),),
        in_specs=[_rows(tm, D_MODEL), _rows(tm, 2 * D_MODEL), _rows(tm, D_MODEL), _rows(tm, D_MODEL),
                  _full(b_gate.shape), _full(w_oa.shape), _full(w_ob.shape), _full(w_out.shape)],
        out_specs=tuple(_rows(tm, w) for w, _ in outs) + (_full((8, D_MODEL)),),
        compiler_params=_cp("arbitrary"),
    )(dz1b, gates, y_a, y_b, b_gate, w_oa, w_ob, w_out)


def _bwd_proj(dqp, dkp, dvm, dq_d, dk_d, dv_d, dgates, dz1, low, w_in_ext, w1, wk, wv, g_q, g_kv, cext, sext, cs128, *, seq):
    tokens = dz1.shape[0]
    tm = min(TOKEN_TILE, seq)
    ns = seq // tm

    def body(dqp_ref, dkp_ref, dvm_ref, dqd_ref, dkd_ref, dvd_ref, dgt_ref, dz_ref, low_ref, win_ref, w1_ref, wk_ref,
             wv_ref, gq_ref, gkv_ref, c_ref, s_ref, cs_ref, dx_ref, dproj_ref, da_ref, dkn_ref, stat_ref):
        @pl.when(pl.program_id(0) == 0)
        def _():
            stat_ref[...] = jnp.zeros_like(stat_ref)

        low = low_ref[...]
        d_a = _rope_slabs(dqp_ref[...].astype(F32), c_ref[...], s_ref[...], True).astype(BF16)
        da_ref[...] = d_a
        q_a = low[:, 0:Q_LORA]
        _, rq = _rms(q_a, gq_ref[...])
        dq_a, gq_terms = _rms_bwd(_dot_nt(d_a, w1_ref[...]), q_a, rq, gq_ref[...])
        kv_a = low[:, Q_LORA:Q_LORA + KV_LORA]
        _, rkv = _rms(kv_a, gkv_ref[...])
        dkn = jnp.concatenate([dkp_ref[:, p * PAIR_W:p * PAIR_W + LANES] for p in range(N_PAIRS)], axis=1)
        dkn_ref[...] = dkn
        dkv_a, gkv_terms = _rms_bwd(_dot_nt(dkn, wk_ref[...]) + _dot_nt(dvm_ref[...], wv_ref[...]), kv_a, rkv, gkv_ref[...])
        dkr = sum(dkp_ref[:, p * PAIR_W + LANES:(p + 1) * PAIR_W].astype(F32) for p in range(N_PAIRS))
        dkr = dkr + pltpu.roll(dkr, LANES - ROPE, 1)
        dkr = jnp.where(lax.broadcasted_iota(I32, dkr.shape, 1) < ROPE, dkr, 0.0)
        dkr = (dkr + pltpu.roll(dkr, ROPE, 1)) * cs_ref[...]
        stat_ref[0:1, 0:Q_LORA] += jnp.sum(gq_terms, axis=0, keepdims=True)
        stat_ref[1:2, 0:KV_LORA] += jnp.sum(gkv_terms, axis=0, keepdims=True)
        dproj_ref[:, 0:Q_LORA] = dq_a.astype(BF16)
        dproj_ref[:, Q_LORA:Q_LORA + KV_LORA] = dkv_a.astype(BF16)
        dproj_ref[:, Q_LORA + KV_LORA:LOW_W] = dkr.astype(BF16)
        dproj_ref[:, LOW_W:LOW_W + DIL_WIDTH] = dqd_ref[...]
        dproj_ref[:, LOW_W + DIL_WIDTH:LOW_W + 2 * DIL_WIDTH] = dkd_ref[...]
        dproj_ref[:, LOW_W + 2 * DIL_WIDTH:LOW_W + 3 * DIL_WIDTH] = dvd_ref[...]
        dproj_ref[:, LOW_W + 3 * DIL_WIDTH:] = dgt_ref[...]
        dx_ref[...] = ALPHA * dz_ref[...] + _dot_nt(dproj_ref[...], win_ref[...])

    wide = N_PAIRS * PAIR_W
    return pl.pallas_call(
        body, name="bwd_proj", grid=(tokens // tm,),
        out_shape=(jax.ShapeDtypeStruct((tokens, D_MODEL), F32), jax.ShapeDtypeStruct((tokens, IN_EXT), BF16),
                   jax.ShapeDtypeStruct((tokens, wide), BF16), jax.ShapeDtypeStruct((tokens, N_HEADS * NOPE), BF16),
                   jax.ShapeDtypeStruct((8, D_MODEL), F32)),
        in_specs=[_rows(tm, wide), _rows(tm, wide), _rows(tm, DIL_WIDTH), _rows(tm, DIL_WIDTH), _rows(tm, DIL_WIDTH),
                  _rows(tm, DIL_WIDTH), _rows(tm, 2 * D_MODEL),
                  _rows(tm, D_MODEL), _rows(tm, LOW_W), _full(w_in_ext.shape), _full(w1.shape),
                  _full(wk.shape), _full(wv.shape), _full(g_q.shape), _full(g_kv.shape),
                  pl.BlockSpec((tm, LANES), lambda i: (i % ns, 1)), pl.BlockSpec((tm, LANES), lambda i: (i % ns, 1)),
                  pl.BlockSpec((tm, LANES), lambda i: (i % ns, 0))],
        out_specs=(_rows(tm, D_MODEL), _rows(tm, IN_EXT), _rows(tm, wide), _rows(tm, N_HEADS * NOPE), _full((8, D_MODEL))),
        compiler_params=_cp("arbitrary"),
    )(dqp, dkp, dvm, dq_d, dk_d, dv_d, dgates, dz1, low, w_in_ext, w1, wk, wv, g_q, g_kv, cext, sext, cs128)


def _wgrad(a, b, name, square_relu=False, by_shard=False):
    tokens, ka = a.shape
    n = b.shape[1]
    if ka <= 512 or ka % 512 == 0:
        tka = min(ka, 512)
    else:
        tka = max(w for w in range(LANES, min(ka, 2304) + 1, LANES) if ka % w == 0)
    shard = n // N_DEV
    tn = WGRAD_SHARDS * shard if by_shard else max(w for w in range(LANES, min(n, 2304) + 1, LANES) if n % w == 0)
    tt = min(tokens, 2048 if tka <= 512 else 1024)
    nt = tokens // tt

    def body(a_ref, b_ref, o_ref, acc):
        kt = pl.program_id(2)

        @pl.when(kt == 0)
        def _():
            acc[...] = jnp.zeros_like(acc)

        at = a_ref[...]
        if square_relu:
            at = jnp.square(jnp.maximum(at.astype(F32), 0.0)).astype(BF16)
        acc[...] += _dot_tn(at, b_ref[...])

        @pl.when(kt == nt - 1)
        def _():
            if by_shard:
                for s in range(WGRAD_SHARDS):
                    o_ref[s] = acc[:, s * shard:(s + 1) * shard].astype(BF16)
            else:
                o_ref[...] = acc[...].astype(BF16)

    if by_shard:
        out_shape, out_spec = (N_DEV, ka, shard), pl.BlockSpec((WGRAD_SHARDS, tka, shard), lambda i, j, k: (j, i, 0))
    else:
        out_shape, out_spec = (ka, n), pl.BlockSpec((tka, tn), lambda i, j, k: (i, j))
    return pl.pallas_call(
        body, name=name, grid=(ka // tka, n // tn, nt), out_shape=jax.ShapeDtypeStruct(out_shape, BF16),
        in_specs=[pl.BlockSpec((tt, tka), lambda i, j, k: (k, i)), pl.BlockSpec((tt, tn), lambda i, j, k: (k, j))],
        out_specs=out_spec,
        scratch_shapes=[pltpu.VMEM((tka, tn), F32)],
        compiler_params=_cp("parallel", "parallel", "arbitrary"),
    )(a, b)


def _adam_math(w, g, m, v):
    m = ADAM_B1 * m + (1.0 - ADAM_B1) * g
    v = ADAM_B2 * v + (1.0 - ADAM_B2) * jnp.square(g)
    m_hat = m / (1.0 - ADAM_B1 ** ADAM_STEP)
    v_hat = v / (1.0 - ADAM_B2 ** ADAM_STEP)
    return -ADAM_LR * (m_hat / (jnp.sqrt(v_hat) + ADAM_EPS) + ADAM_WD * w), m, v


def _adamw(items, name):
    steps = min(_tiles(*w.shape)[0] for w, *_ in items)
    n_items = len(items)

    def body(slot_ref, *refs):
        ins, outs = refs[:5 * n_items], refs[5 * n_items:]
        for k, (_, _, _, _, parts) in enumerate(items):
            w_ref, m_ref, v_ref, own_ref, p_ref = ins[5 * k:5 * k + 5]
            g_ref, d_ref, nm_ref, nv_ref = outs[4 * k:4 * k + 4]
            g = own_ref[...].astype(F32)
            for d in range(parts.shape[0]):
                g = g + p_ref[d].astype(F32)
            g_ref[...] = g
            d_ref[...], nm_ref[...], nv_ref[...] = _adam_math(w_ref[...], g, m_ref[...], v_ref[...])

    x, y, c = _place()
    in_specs, out_specs, out_shape, args = [], [], [], []
    for w, m, v, own, parts in items:
        rows, cols = w.shape
        _, tile, at = _tiles(rows, cols, steps)
        blk = pl.BlockSpec(tile, lambda i, slot, at=at: at(i))
        own_blk = blk if own.ndim == 2 else pl.BlockSpec((None, *tile), lambda i, slot, at=at: (slot[0], *at(i)))
        in_specs += [blk, blk, blk, own_blk, pl.BlockSpec((parts.shape[0], *tile), lambda i, slot, at=at: (0, *at(i)))]
        out_specs += [blk] * 4
        out_shape += [jax.ShapeDtypeStruct((rows, cols), F32)] * 4
        args += [w, m, v, own, parts]
    out = pl.pallas_call(
        body, name=name,
        grid_spec=pltpu.PrefetchScalarGridSpec(num_scalar_prefetch=1, grid=(steps,), in_specs=in_specs, out_specs=out_specs),
        out_shape=out_shape, compiler_params=_cp("parallel"),
    )(jnp.reshape(4 * x + 2 * y + c, (1,)).astype(I32), *args)
    return [tuple(out[4 * k:4 * k + 4]) for k in range(n_items)]


def _adamw_small(parts, w, m, v):
    _, rows, cols = parts.shape

    def body(p_ref, w_ref, m_ref, v_ref, g_ref, d_ref, nm_ref, nv_ref):
        g = p_ref[0]
        for d in range(1, N_DEV):
            g = g + p_ref[d]
        g_ref[...] = g
        d_ref[...], nm_ref[...], nv_ref[...] = _adam_math(w_ref[...], g, m_ref[...], v_ref[...])

    return pl.pallas_call(
        body, name="adamw_replicated", out_shape=(jax.ShapeDtypeStruct((rows, cols), F32),) * 4,
        in_specs=[_full(parts.shape)] + [_full((rows, cols))] * 3, out_specs=(_full((rows, cols)),) * 4, grid=(1,),
        compiler_params=_cp("arbitrary"),
    )(parts, w, m, v)


def _pad_rows(a2d, mult):
    pad = (-a2d.shape[-2]) % mult
    return jnp.pad(a2d, [(0, 0)] * (a2d.ndim - 2) + [(0, pad), (0, 0)]) if pad else a2d


def _pad_cols(a):
    pad = (-a.shape[-1]) % LANES
    return jnp.pad(a, [(0, 0)] * (a.ndim - 1) + [(0, pad)]) if pad else a


def _rot_cols(w):
    half = ROPE // 2
    return jnp.concatenate([-w[..., half:], w[..., :half]], axis=-1)


def _unrot_cols(dw):
    half = ROPE // 2
    return jnp.concatenate([dw[..., half:], -dw[..., :half]], axis=-1)


def _from_col_shards(stacked):
    return stacked.transpose(1, 0, 2).reshape(stacked.shape[1], -1)


def _to_col_shards(full):
    r = full.shape[0]
    return full.reshape(r, N_DEV, -1).transpose(1, 0, 2)


def _rope_tables(seq):
    half = ROPE // 2
    inv = jnp.power(ROPE_THETA, -jnp.arange(half, dtype=F32) / half)
    ang = jnp.arange(seq, dtype=F32)[:, None] * inv[None, :]
    cos = jnp.concatenate([jnp.cos(ang)] * 2, axis=1)
    sin = jnp.concatenate([jnp.sin(ang)] * 2, axis=1)
    ones, zeros = jnp.ones((seq, 2 * NOPE), F32), jnp.zeros((seq, 2 * NOPE), F32)
    pad = jnp.zeros((seq, PAIR_W - 2 * NOPE - 2 * ROPE), F32)
    cext = jnp.concatenate([ones, cos, cos, pad], axis=1)
    sext = jnp.concatenate([zeros, sin, sin, pad], axis=1)
    cs128 = jnp.concatenate([cos, sin, jnp.zeros((seq, LANES - 2 * ROPE), F32)], axis=1)
    return cext, sext, cs128


def _pair_slabs(nope, rope):
    k = nope.shape[0]
    nope = nope.reshape(k, N_PAIRS, 2 * NOPE)
    rope = rope.reshape(k, N_PAIRS, 2 * ROPE)
    pad = jnp.zeros((k, N_PAIRS, PAIR_W - 2 * NOPE - 2 * ROPE), nope.dtype)
    return jnp.concatenate([nope, rope, pad], axis=2).reshape(k, N_PAIRS * PAIR_W)


def _split_slabs(slabs):
    k = slabs.shape[0]
    s = slabs.reshape(k, N_PAIRS, PAIR_W)
    return s[:, :, :2 * NOPE].reshape(k, N_HEADS, NOPE), s[:, :, 2 * NOPE:2 * NOPE + 2 * ROPE].reshape(k, N_HEADS, ROPE)


def kernel(x, w_in, b_gate, g_q_a, w_uq, g_kv_a, w_ukv, w_o_mla, w_o_dil, w_out, ln1_g, ln1_b, w_ff1, w_ff2, ln2_g, ln2_b, loss_target, m_w_in, m_b_gate, m_g_q_a, m_w_uq, m_g_kv_a, m_w_ukv, m_w_o_mla, m_w_o_dil, m_w_out, m_ln1_g, m_ln1_b, m_w_ff1, m_w_ff2, m_ln2_g, m_ln2_b, v_w_in, v_b_gate, v_g_q_a, v_w_uq, v_g_kv_a, v_w_ukv, v_w_o_mla, v_w_o_dil, v_w_out, v_ln1_g, v_ln1_b, v_w_ff1, v_w_ff2, v_ln2_g, v_ln2_b):
    batch, seq, _ = x.shape
    tokens = batch * seq
    weights = dict(w_in=w_in, w_uq=w_uq, w_ukv=w_ukv, w_o_mla=w_o_mla, w_o_dil=w_o_dil, w_out=w_out, w_ff1=w_ff1, w_ff2=w_ff2, b_gate=b_gate)
    mom_m = dict(w_in=m_w_in, w_uq=m_w_uq, w_ukv=m_w_ukv, w_o_mla=m_w_o_mla, w_o_dil=m_w_o_dil, w_out=m_w_out, w_ff1=m_w_ff1, w_ff2=m_w_ff2, b_gate=m_b_gate)
    mom_v = dict(w_in=v_w_in, w_uq=v_w_uq, w_ukv=v_w_ukv, w_o_mla=v_w_o_mla, w_o_dil=v_w_o_dil, w_out=v_w_out, w_ff1=v_w_ff1, w_ff2=v_w_ff2, b_gate=v_b_gate)

    first = ["w_in", "w_uq", "w_ukv"]
    widths = [weights[n].shape[2] for n in first]
    shards = [weights["w_in"][0].T.astype(BF16)] + [_pad_cols(weights[n][0].astype(BF16)) for n in first[1:]]
    g_in, g_uq, g_ukv = _run_comm(_Gather(shards), shards, "all_gather_first_weights")
    g_uq, g_ukv = g_uq[:, :, :widths[1]], g_ukv[:, :, :widths[2]]

    s1, s2, n_in = Q_LORA + KV_LORA, Q_LORA + KV_LORA + ROPE, N_DEV * widths[0]

    def w_in_cols(lo, hi):
        out = []
        while lo < hi:
            d, off = divmod(lo, widths[0])
            take = min(hi - lo, widths[0] - off)
            out.append(g_in[d][off:off + take].T)
            lo += take
        return out

    w_in_ext = jnp.concatenate(w_in_cols(0, s2) + [_rot_cols(jnp.concatenate(w_in_cols(s1, s2), axis=1)),
                                                   jnp.zeros((D_MODEL, LOW_W - s2 - ROPE), BF16)] + w_in_cols(s2, n_in), axis=1)
    uq = _from_col_shards(g_uq).reshape(Q_LORA, N_HEADS, NOPE + ROPE)
    w1 = _pair_slabs(uq[:, :, :NOPE], uq[:, :, NOPE:])
    ukv = _from_col_shards(g_ukv).reshape(KV_LORA, N_HEADS, NOPE + HEAD_V)
    wk = ukv[:, :, :NOPE].reshape(KV_LORA, N_HEADS * NOPE)
    wv = ukv[:, :, NOPE:].reshape(KV_LORA, N_HEADS * HEAD_V)
    cext, sext, cs128 = _rope_tables(seq)
    dil_bias = _dilated_bias_table(seq)
    no_bias = jnp.zeros((1, 8, LANES), F32)

    x2 = x.reshape(tokens, D_MODEL)
    low, gates, qkvd, qp, kp, vm, qn, kvn, xb = _fwd_proj(x2, w_in_ext, w1, wk, wv, g_q_a, g_kv_a, cext, sext, cs128, seq=seq)
    bg = b_gate[0]
    bg_hi = bg.astype(BF16)
    bg_lo = (bg - bg_hi.astype(F32)).astype(BF16)
    later = [weights[n][0].astype(BF16) for n in ("w_o_mla", "w_o_dil", "w_out", "w_ff1", "w_ff2")]
    later.append(_pad_rows(jnp.concatenate([bg_hi, bg_lo], axis=0), 16))
    mla = dict(batch=batch, seq=seq, width=PAIR_W, col0=(0, 0, 0), dilated=False, scale=MLA_SCALE)
    dil = dict(batch=batch, seq=seq, width=LANES, col0=(0, N_PAIRS, 2 * N_PAIRS), dilated=True, scale=DIL_SCALE)
    o_a, lse_a, g_oa, g_ob, g_out, g_ff1, g_ff2, g_bg = _attn_fwd(
        qp, kp, vm, no_bias, name="mla_attention_fwd", comm=_Gather(later), comm_arrays=later, **mla)
    o_b, lse_b = _attn_fwd(qkvd, qkvd, qkvd, dil_bias, name="dilated_attention_fwd", **dil)
    w_oa, w_ob = _from_col_shards(g_oa), _from_col_shards(g_ob)
    w_out_full = g_out.reshape(D_MODEL, D_MODEL)
    w_ff2_full = g_ff2.reshape(D_FF, D_MODEL)
    bg_parts = g_bg.astype(F32)
    b_gate_full = _from_col_shards(bg_parts[:, 0:2] + bg_parts[:, 2:4])
    hb, xhat1, rstd1, y_a, y_b, mix = _fwd_mix(o_a, o_b, gates, x2, b_gate_full, w_oa, w_ob, w_out_full, ln1_g, ln1_b, seq=seq)
    u, dz2, dz2b, stat2 = _fwd_mlp(hb, xhat1, loss_target.reshape(tokens, D_MODEL), g_ff1, w_ff2_full, ln1_g, ln1_b, ln2_g, ln2_b, seq=seq)

    du, dz1, dz1b, stat1 = _bwd_mlp(dz2, dz2b, u, xhat1, rstd1, g_ff1, w_ff2_full, ln1_g, seq=seq)
    dw_ff = [_wgrad(hb, du, "wgrad_ff1", by_shard=True),
             _wgrad(u, dz2b, "wgrad_ff2", square_relu=True).reshape(N_DEV, FF_SHARD, D_MODEL)]
    dgates, dy_a, dy_b, do_a, do_b, stat_g = _bwd_mix(dz1b, gates, y_a, y_b, b_gate_full, w_oa, w_ob, w_out_full, seq=seq)
    dqp, dkp, dvm, r_ff1, r_ff2 = _attn_bwd(qp, kp, vm, o_a, do_a, lse_a, no_bias, name="mla_attention_bwd",
                                            comm=_Scatter(dw_ff), comm_arrays=dw_ff, **mla)
    dw_mid = [_to_col_shards(_wgrad(o_a, dy_a, "wgrad_o_mla")), _to_col_shards(_wgrad(o_b, dy_b, "wgrad_o_dil")),
              _wgrad(mix, dz1b, "wgrad_out").reshape(N_DEV, D_MODEL // N_DEV, D_MODEL),
              _pad_rows(_to_col_shards(stat_g[0:2]).astype(BF16), 16)]
    dq_d, dk_d, dv_d, r_oa, r_ob, r_out, r_bg = _attn_bwd(qkvd, qkvd, qkvd, o_b, do_b, lse_b, dil_bias, name="dilated_attention_bwd",
                                                          comm=_Scatter(dw_mid), comm_arrays=dw_mid, **dil)
    grad_x, dproj, d_a, dkn, stat_r = _bwd_proj(dqp, dkp, dvm, dq_d, dk_d, dv_d, dgates, dz1, low, w_in_ext, w1, wk, wv,
                                                g_q_a, g_kv_a, cext, sext, cs128, seq=seq)

    dw_in_ext = _wgrad(dproj, xb, "wgrad_in")
    dw1 = _wgrad(qn, d_a, "wgrad_uq")
    dwk = _wgrad(kvn, dkn, "wgrad_ukv_k")
    dwv = _wgrad(kvn, dvm, "wgrad_ukv_v")
    dw_kr = dw_in_ext[s1:s2] + _unrot_cols(dw_in_ext[s2:s2 + ROPE].T).T

    def dw_in_cols(lo, hi):
        out = []
        for a, b, piece in ((0, s1, lambda u, v: dw_in_ext[u:v]), (s1, s2, lambda u, v: dw_kr[u - s1:v - s1]),
                            (s2, n_in, lambda u, v: dw_in_ext[u + LOW_W - s2:v + LOW_W - s2])):
            if max(lo, a) < min(hi, b):
                out.append(piece(max(lo, a), min(hi, b)))
        return out

    dw_in = jnp.stack([jnp.concatenate(dw_in_cols(d * widths[0], (d + 1) * widths[0]), axis=0) for d in range(N_DEV)])
    n1, r1 = _split_slabs(dw1)
    dw_uq = jnp.concatenate([n1, r1], axis=2).reshape(Q_LORA, N_HEADS * (NOPE + ROPE))
    dw_ukv = jnp.concatenate([dwk.reshape(KV_LORA, N_HEADS, NOPE), dwv.reshape(KV_LORA, N_HEADS, HEAD_V)], axis=2).reshape(KV_LORA, N_HEADS * (NOPE + HEAD_V))
    last = [dw_in] + [_pad_cols(_to_col_shards(dw)) for dw in (dw_uq, dw_ukv)]
    theirs = _rs_sibling(last, "rs_last_sibling_exchange")
    sums = [_pair_sum(a, b, "rs_last_pair_sum_" + n) for a, b, n in zip(last, theirs, first)]
    partial = jnp.concatenate([stat_r[0:1, :Q_LORA], stat_r[1:2, :KV_LORA], stat1[0:1], stat1[1:2], stat2[0:1], stat2[1:2],
                               stat2[2:3, :LANES]], axis=1)
    partial = _pad_rows(partial.reshape(-1, LANES), 8)
    rest = [s[1] for s in sums]
    got_in, got_uq, got_ukv, every = _run_comm(_Plans([_ChipExchange(rest), _Gather([partial])]), rest + [partial],
                                               "rs_last_chip_exchange")

    upd = {}
    early = ["w_ff1", "w_ff2", "w_out", "w_o_mla", "w_o_dil"]
    items = [(weights[n][0], mom_m[n][0], mom_v[n][0], own, parts) for n, own, parts in
             zip(early, (dw_ff[0], dw_ff[1], dw_mid[2], dw_mid[0], dw_mid[1]), (r_ff1, r_ff2, r_out, r_oa, r_ob))]
    upd.update(zip(early, _adamw(items, "adamw_early_weights")))
    (in_t,) = _adamw([(weights["w_in"][0].T, mom_m["w_in"][0].T, mom_v["w_in"][0].T, sums[0][0], got_in)], "adamw_w_in")
    upd["w_in"] = tuple(a.T for a in in_t)
    for n, w, (own, _), parts in zip(first[1:], widths[1:], sums[1:], (got_uq, got_ukv)):
        (upd[n],) = _adamw([(weights[n][0], mom_m[n][0], mom_v[n][0], own[:, :w], parts[:, :, :w])], "adamw_" + n)
    (bg_upd,) = _adamw([(_pad_rows(b_gate[0], 16), _pad_rows(m_b_gate[0], 16), _pad_rows(v_b_gate[0], 16), dw_mid[3], r_bg)],
                       "adamw_b_gate")
    upd["b_gate"] = tuple(t[0:2] for t in bg_upd)

    small_w = [g_q_a, g_kv_a, ln1_g, ln1_b, ln2_g, ln2_b]
    small_m = [m_g_q_a, m_g_kv_a, m_ln1_g, m_ln1_b, m_ln2_g, m_ln2_b]
    small_v = [v_g_q_a, v_g_kv_a, v_ln1_g, v_ln1_b, v_ln2_g, v_ln2_b]
    small_widths = [a.shape[1] for a in small_w]

    def as_rows(vecs, extra):
        flat = jnp.concatenate(vecs + [jnp.zeros((1, extra), F32)], axis=1)
        return _pad_rows(flat.reshape(-1, LANES), 8)

    g_s, d_s, nm_s, nv_s = _adamw_small(every, as_rows(small_w, LANES), as_rows(small_m, LANES), as_rows(small_v, LANES))

    def split_small(a):
        flat = a.reshape(1, -1)
        out, c0 = [], 0
        for w in small_widths:
            out.append(flat[:, c0:c0 + w])
            c0 += w
        return out, flat[0, c0]

    g_small, loss = split_small(g_s)
    small = [g_small, split_small(d_s)[0], split_small(nm_s)[0], split_small(nv_s)[0]]

    order = ["w_in", "b_gate", "g_q_a", "w_uq", "g_kv_a", "w_ukv", "w_o_mla", "w_o_dil", "w_out", "ln1_g", "ln1_b", "w_ff1", "w_ff2", "ln2_g", "ln2_b"]
    small_names = ["g_q_a", "g_kv_a", "ln1_g", "ln1_b", "ln2_g", "ln2_b"]

    def pick(kind):
        return [small[kind][small_names.index(n)] if n in small_names else upd[n][kind][None] for n in order]

    return (loss, grad_x.reshape(batch, seq, D_MODEL), *pick(0), *pick(1), *pick(2), *pick(3))
```

```python
import functools
import math

import jax
import jax.numpy as jnp
from jax import lax
from jax.experimental import pallas as pl
from jax.experimental.pallas import tpu as pltpu

F32 = jnp.float32
BF16 = jnp.bfloat16
I32 = jnp.int32

D_MODEL = 1024
N_HEADS = 8
NOPE = 64
ROPE = 32
HEAD_V = 64
Q_LORA = 384
KV_LORA = 256
DIL_WIDTH = 512
D_FF = 4096
ROPE_THETA = 10000.0
LN_EPS = 1e-5
RMS_EPS = 1e-6
NEG = -1e30
ALPHA = 2.0 ** 0.25
MLA_SCALE = (NOPE + ROPE) ** -0.5
DIL_SCALE = 64 ** -0.5
ADAM_LR, ADAM_B1, ADAM_B2, ADAM_EPS, ADAM_WD, ADAM_STEP = 0.001, 0.9, 0.999, 1e-08, 0.01, 10

LANES = 128
PAIR_W = 256
N_PAIRS = N_HEADS // 2
LOW_W = 768
IN_EXT = LOW_W + 3 * DIL_WIDTH + 2 * D_MODEL
N_DEV = 8
FF_SHARD = D_FF // N_DEV
FF_STEP = 4
WGRAD_SHARDS = 4
TOKEN_TILE = 256
MIX_TILE = 512
ATTN_TILE = 512
VMEM_LIMIT = 56 << 20

MESH = pl.DeviceIdType.MESH
ANY = pl.BlockSpec(memory_space=pl.ANY)
CHIP_FLIPS = ((0, 0), (0, 1), (1, 0), (1, 1))
PEER_FLIPS = tuple((fx, fy, fc) for fx in (0, 1) for fy in (0, 1) for fc in (0, 1))[1:]


def _cp(*sem):
    return pltpu.CompilerParams(dimension_semantics=sem or None, vmem_limit_bytes=VMEM_LIMIT)


def _full(shape):
    nd = len(shape)
    return pl.BlockSpec(shape, lambda *_: (0,) * nd)


def _rows(tm, width):
    return pl.BlockSpec((tm, width), lambda i, *_: (i, 0))


def _dot(a, b):
    return jnp.dot(a, b, preferred_element_type=F32)


def _dot_nt(a, b):
    return lax.dot_general(a, b, (((1,), (1,)), ((), ())), preferred_element_type=F32)


def _dot_tn(a, b):
    return lax.dot_general(a, b, (((0,), (0,)), ((), ())), preferred_element_type=F32)


def _sigmoid(z):
    return 1.0 / (1.0 + jnp.exp(-z))


def _place():
    return lax.axis_index("x"), lax.axis_index("y"), lax.axis_index("c")


def _flip(v, f):
    return 1 - v if f else v


class _Gather:
    def __init__(self, shards):
        self.n = len(shards)
        self.out_shape = [jax.ShapeDtypeStruct((N_DEV, *s.shape), s.dtype) for s in shards]
        self.scratch = [pltpu.SemaphoreType.DMA((7 * self.n,)), pltpu.SemaphoreType.DMA((7 * self.n,)),
                        pltpu.SemaphoreType.DMA((self.n,))]

    def _copies(self, what, srcs, dsts, send, recv, local):
        x, y, c = _place()
        chips = [(_flip(x, fx), _flip(y, fy)) for fx, fy in CHIP_FLIPS[1:]]
        out = []
        for a in range(self.n):
            def slot(px, py, pc, a=a):
                return dsts[a].at[4 * px + 2 * py + pc]

            def copy(k, block, to, src=None, a=a, slot=slot):
                return pltpu.make_async_remote_copy(
                    src_ref=slot(*block) if src is None else src, dst_ref=slot(*block),
                    send_sem=send.at[7 * a + k], recv_sem=recv.at[7 * a + k], device_id=to, device_id_type=MESH)

            if what == "mine":
                out.append(pltpu.make_async_copy(srcs[a], slot(x, y, c), local.at[a]))
            elif what == "first":
                out.append(copy(0, (x, y, c), (x, y, 1 - c), src=srcs[a]))
                out += [copy(1 + j, (x, y, c), (*chip, c), src=srcs[a]) for j, chip in enumerate(chips)]
            elif what == "landed":
                out += [copy(1 + j, (*chip, c), (x, y, c)) for j, chip in enumerate(chips)]
            elif what == "passed":
                out += [copy(4 + j, (*chip, c), (x, y, 1 - c)) for j, chip in enumerate(chips)]
            else:
                out.append(copy(0, (x, y, 1 - c), (x, y, c)))
                out += [copy(4 + j, (*chip, 1 - c), (x, y, c)) for j, chip in enumerate(chips)]
        return out

    def start(self, *refs):
        for cp in self._copies("first", *refs) + self._copies("mine", *refs):
            cp.start()

    def forward(self, *refs):
        for landed, passed in zip(self._copies("landed", *refs), self._copies("passed", *refs)):
            landed.wait_recv()
            passed.start()

    def finish(self, *refs):
        for cp in self._copies("from_sibling", *refs):
            cp.wait_recv()
        for cp in self._copies("first", *refs) + self._copies("passed", *refs):
            cp.wait_send()
        for cp in self._copies("mine", *refs):
            cp.wait()


class _Scatter:
    def __init__(self, arrays):
        self.n = len(arrays)
        self.out_shape = [jax.ShapeDtypeStruct((7, *a.shape[1:]), a.dtype) for a in arrays]
        self.scratch = [pltpu.SemaphoreType.DMA((7 * self.n,)), pltpu.SemaphoreType.DMA((7 * self.n,))]

    def _copies(self, srcs, dsts, send, recv):
        x, y, c = _place()
        out = []
        for a in range(self.n):
            for k, (fx, fy, fc) in enumerate(PEER_FLIPS):
                px, py, pc = _flip(x, fx), _flip(y, fy), _flip(c, fc)
                out.append(pltpu.make_async_remote_copy(
                    src_ref=srcs[a].at[4 * px + 2 * py + pc], dst_ref=dsts[a].at[k],
                    send_sem=send.at[7 * a + k], recv_sem=recv.at[7 * a + k], device_id=(px, py, pc), device_id_type=MESH))
        return out

    def start(self, *refs):
        for cp in self._copies(*refs):
            cp.start()

    def forward(self, *refs):
        pass

    def finish(self, *refs):
        for cp in self._copies(*refs):
            cp.wait_send()
        for cp in self._copies(*refs):
            cp.wait_recv()


class _ChipExchange:
    def __init__(self, arrays):
        self.n = len(arrays)
        self.out_shape = [jax.ShapeDtypeStruct(a.shape, a.dtype) for a in arrays]
        self.scratch = [pltpu.SemaphoreType.DMA((3 * self.n,)), pltpu.SemaphoreType.DMA((3 * self.n,))]

    def _copies(self, srcs, dsts, send, recv):
        x, y, c = _place()
        return [pltpu.make_async_remote_copy(
            src_ref=srcs[a].at[k], dst_ref=dsts[a].at[k], send_sem=send.at[3 * a + k], recv_sem=recv.at[3 * a + k],
            device_id=(_flip(x, fx), _flip(y, fy), c), device_id_type=MESH)
            for a in range(self.n) for k, (fx, fy) in enumerate(CHIP_FLIPS[1:])]

    def start(self, *refs):
        for cp in self._copies(*refs):
            cp.start()

    def forward(self, *refs):
        pass

    def finish(self, *refs):
        for cp in self._copies(*refs):
            cp.wait_send()
        for cp in self._copies(*refs):
            cp.wait_recv()


class _Plans:
    def __init__(self, plans):
        self.plans = plans
        self.n = sum(p.n for p in plans)
        self.out_shape = [s for p in plans for s in p.out_shape]
        self.scratch = [s for p in plans for s in p.scratch]

    def _each(self, phase, srcs, dsts, *sems):
        i0 = s0 = 0
        for p in self.plans:
            getattr(p, phase)(srcs[i0:i0 + p.n], dsts[i0:i0 + p.n], *sems[s0:s0 + len(p.scratch)])
            i0, s0 = i0 + p.n, s0 + len(p.scratch)

    def start(self, *refs):
        self._each("start", *refs)

    def forward(self, *refs):
        self._each("forward", *refs)

    def finish(self, *refs):
        self._each("finish", *refs)


def _run_comm(comm, arrays, name):
    n = comm.n

    def body(*refs):
        args = (refs[:n], refs[n:2 * n], *refs[2 * n:])
        comm.start(*args)
        comm.forward(*args)
        comm.finish(*args)

    return pl.pallas_call(body, name=name, out_shape=comm.out_shape, in_specs=[ANY] * n, out_specs=[ANY] * n,
                          scratch_shapes=comm.scratch)(*arrays)


def _rs_sibling(arrays, name):
    n = len(arrays)

    def body(*refs):
        srcs, got, (send, recv) = refs[:n], refs[n:2 * n], refs[2 * n:]
        x, y, c = _place()
        copies = []
        for a in range(n):
            for r, (fx, fy) in enumerate(CHIP_FLIPS):
                chip = 2 * _flip(x, fx) + _flip(y, fy)
                copies.append(pltpu.make_async_remote_copy(
                    src_ref=srcs[a].at[2 * chip + 1 - c], dst_ref=got[a].at[r], send_sem=send.at[4 * a + r],
                    recv_sem=recv.at[4 * a + r], device_id=(x, y, 1 - c), device_id_type=MESH))
        for cp in copies:
            cp.start()
        for cp in copies:
            cp.wait_send()
        for cp in copies:
            cp.wait_recv()

    return pl.pallas_call(
        body, name=name, out_shape=[jax.ShapeDtypeStruct((4, *a.shape[1:]), a.dtype) for a in arrays],
        in_specs=[ANY] * n, out_specs=[ANY] * n,
        scratch_shapes=[pltpu.SemaphoreType.DMA((4 * n,)), pltpu.SemaphoreType.DMA((4 * n,))],
    )(*arrays)


def _chip_slots():
    x, y, c = _place()
    return jnp.stack([4 * _flip(x, fx) + 2 * _flip(y, fy) + c for fx, fy in CHIP_FLIPS]).astype(I32)


def _tiles(rows, cols, steps=4):
    if rows % (16 * steps) == 0:
        return steps, (rows // steps, cols), lambda i: (i, 0)
    if cols % (LANES * steps) == 0:
        return steps, (rows, cols // steps), lambda i: (0, i)
    return 1, (rows, cols), lambda i: (0, 0)


def _pair_sum(full, theirs, name):
    _, rows, cols = theirs.shape
    steps, tile, at = _tiles(rows, cols)

    def body(slots_ref, m0_ref, m1_ref, m2_ref, m3_ref, b_ref, own_ref, rest_ref):
        own_ref[...] = m0_ref[...].astype(F32) + b_ref[0].astype(F32)
        for k, m_ref in enumerate((m1_ref, m2_ref, m3_ref)):
            rest_ref[k] = (m_ref[...].astype(F32) + b_ref[k + 1].astype(F32)).astype(BF16)

    def mine(k):
        return pl.BlockSpec((None, *tile), lambda i, slots: (slots[k], *at(i)))

    return pl.pallas_call(
        body, name=name,
        grid_spec=pltpu.PrefetchScalarGridSpec(
            num_scalar_prefetch=1, grid=(steps,),
            in_specs=[mine(0), mine(1), mine(2), mine(3), pl.BlockSpec((4, *tile), lambda i, slots: (0, *at(i)))],
            out_specs=(pl.BlockSpec(tile, lambda i, slots: at(i)), pl.BlockSpec((3, *tile), lambda i, slots: (0, *at(i))))),
        out_shape=(jax.ShapeDtypeStruct((rows, cols), F32), jax.ShapeDtypeStruct((3, rows, cols), BF16)),
        compiler_params=_cp("parallel"),
    )(_chip_slots(), full, full, full, full, theirs)


def _head_lanes(width, h):
    lane = lax.broadcasted_iota(I32, (1, width), 1)
    if width == LANES:
        return (lane >= 64 * h) & (lane < 64 * h + 64)
    nope = (lane >= NOPE * h) & (lane < NOPE * h + NOPE)
    rope = (lane >= 2 * NOPE + ROPE * h) & (lane < 2 * NOPE + ROPE * h + ROPE)
    return nope | rope


def _dilated_bias_table(seq):
    t = min(ATTN_TILE, seq)
    nd = seq // t

    def body(o_ref):
        delta = pl.program_id(0) * t + lax.broadcasted_iota(I32, (t, t), 1) - lax.broadcasted_iota(I32, (t, t), 0)
        mult = ((delta <= 128).astype(I32) + (((delta & 3) == 0) & (delta <= 512)).astype(I32)
                + ((delta & 15) == 0).astype(I32))
        logm = jnp.where(mult == 3, math.log(3.0), jnp.where(mult == 2, math.log(2.0), 0.0))
        valid = (delta >= 0) & (mult > 0)
        dist = delta.astype(F32)
        for h in range(N_HEADS):
            o_ref[h] = jnp.where(valid, logm - 2.0 ** (-(h + 1)) * dist, NEG)

    return pl.pallas_call(
        body, name="dilated_bias_table", grid=(nd,), out_shape=jax.ShapeDtypeStruct((N_HEADS, nd, t, t), F32),
        out_specs=pl.BlockSpec((N_HEADS, None, t, t), lambda d: (0, d, 0, 0)),
        compiler_params=_cp("parallel"),
    )()


def _comm_hooks(comm, refs, n_in, n_out):
    if comm is None:
        return refs[:n_in], refs[n_in:n_in + n_out], refs[n_in + n_out:], None
    n = comm.n
    ins, srcs = refs[:n_in], refs[n_in:n_in + n]
    outs, dsts = refs[n_in + n:n_in + n + n_out], refs[n_in + n + n_out:n_in + 2 * n + n_out]
    rest = refs[n_in + 2 * n + n_out:]
    own = len(rest) - len(comm.scratch)
    return ins, outs, rest[:own], (srcs, dsts, *rest[own:])


def _attn_fwd(q, k, v, bias, *, batch, seq, width, col0, dilated, scale, name, comm=None, comm_arrays=()):
    t = min(ATTN_TILE, seq)
    nq = seq // t
    half = t // 2
    cq, ck, cv = col0
    pre = scale if dilated else 1.0
    steps = batch * N_PAIRS

    def body(*refs):
        (q_ref, k_ref, v_ref, bias_ref), (o_ref, lse_ref), (v_heads,), plan = _comm_hooks(comm, refs, 4, 2)
        step_no = pl.program_id(0) * N_PAIRS + pl.program_id(1)
        if plan:
            pl.when(step_no == 0)(lambda: comm.start(*plan))
            pl.when(step_no == (3 * steps) // 4)(lambda: comm.forward(*plan))
        v_all = v_ref[...].astype(F32)
        for h in (0, 1):
            v_heads[h] = jnp.transpose(jnp.where(_head_lanes(LANES, h), v_all, 0.0)).astype(BF16)
        top = lax.broadcasted_iota(I32, (LANES, t), 0) < HEAD_V
        causal = lax.broadcasted_iota(I32, (t, t), 0) <= lax.broadcasted_iota(I32, (t, t), 1)
        def heads(i):
            q2 = q_ref[pl.ds(pl.multiple_of(i * t, t), t), :]
            q2 = q2 * pre if dilated else q2
            return [jnp.where(_head_lanes(width, h), q2, jnp.zeros_like(q2)) for h in (0, 1)]

        def scores(qh, j):
            kj = k_ref[pl.ds(pl.multiple_of(j * t, t), t), :]
            return tuple(_dot_nt(kj, qh[h]) for h in (0, 1))

        lax.fori_loop(0, nq, functools.partial(query_tile, heads, scores, bias_ref, o_ref, lse_ref, v_heads, top, causal),
                      0)
        if plan:
            pl.when(step_no == steps - 1)(lambda: comm.finish(*plan))

    def query_tile(heads, scores, bias_ref, o_ref, lse_ref, v_heads, top, causal, i, _):
        qs = pl.multiple_of(i * t, t)
        qh = heads(i)

        def step(j, carry, last):
            m0, l0, m1, l1, acc, s0, s1 = carry
            ahead = () if last else scores(qh, j + 1)
            ks = pl.multiple_of(j * t, t)
            new, alphas, pv = [], [], []

            def online(h, m, l, s, keys, queries):
                s = s[keys, queries]
                if dilated:
                    s = s + (bias_ref[h, 0, keys, queries] if last else bias_ref[h, i - j])
                else:
                    s = s * scale
                    if last:
                        s = jnp.where(causal[keys, queries], s, NEG)
                m, l = m[:, queries], l[:, queries]
                m_new = jnp.maximum(m, jnp.max(s, axis=0, keepdims=True))
                a = jnp.exp(m - m_new)
                p = jnp.exp(s - m_new)
                v_keys = v_heads[h, :, pl.ds(ks, t)]
                return m_new, a * l + jnp.sum(p, axis=0, keepdims=True), a, _dot(v_keys[:, keys], p.astype(BF16))

            for h, (m, l, s) in enumerate(((m0, l0, s0), (m1, l1, s1))):
                if last and half % LANES == 0:
                    parts = [online(h, m, l, s, slice(0, half), slice(0, half)),
                             online(h, m, l, s, slice(0, t), slice(half, t))]
                    m_new, l_new, a, pv_h = (jnp.concatenate(x, axis=1) for x in zip(*parts))
                else:
                    m_new, l_new, a, pv_h = online(h, m, l, s, slice(0, t), slice(0, t))
                new += [m_new, l_new]
                alphas.append(a)
                pv.append(pv_h)
            acc = jnp.where(top, alphas[0], alphas[1]) * acc + pv[0] + pv[1]
            return (*new, acc, *ahead)

        row = jnp.full((1, t), NEG, F32)
        zero = jnp.zeros((1, t), F32)
        init = (row, zero, row, zero, jnp.zeros((LANES, t), F32), *scores(qh, 0))
        m0, l0, m1, l1, acc = step(i, lax.fori_loop(0, i, functools.partial(step, last=False), init), True)
        o_ref[pl.ds(qs, t), :] = jnp.transpose(acc * jnp.where(top, 1.0 / l0, 1.0 / l1)).astype(BF16)
        r = lax.broadcasted_iota(I32, (8, t), 0)
        lse_ref[:, pl.ds(qs, t)] = jnp.where(r == 0, m0 + jnp.log(l0), jnp.where(r == 1, m1 + jnp.log(l1), 0.0))
        return 0

    bias_spec = (pl.BlockSpec((2, nq, t, t), lambda b, p: (p, 0, 0, 0)) if dilated
                 else pl.BlockSpec((None, 8, LANES), lambda b, p: (0, 0, 0)))
    n = comm.n if comm else 0
    return pl.pallas_call(
        body, name=name, grid=(batch, N_PAIRS),
        out_shape=[jax.ShapeDtypeStruct((batch * seq, DIL_WIDTH), BF16), jax.ShapeDtypeStruct((batch * N_PAIRS, 8, seq), F32)]
        + (comm.out_shape if comm else []),
        in_specs=[pl.BlockSpec((seq, width), lambda b, p: (b, cq + p)),
                  pl.BlockSpec((seq, width), lambda b, p: (b, ck + p)),
                  pl.BlockSpec((seq, LANES), lambda b, p: (b, cv + p)),
                  bias_spec] + [ANY] * n,
        out_specs=[pl.BlockSpec((seq, LANES), lambda b, p: (b, p)),
                   pl.BlockSpec((None, 8, seq), lambda b, p: (b * N_PAIRS + p, 0, 0))] + [ANY] * n,
        scratch_shapes=[pltpu.VMEM((2, LANES, seq), BF16)] + (comm.scratch if comm else []),
        compiler_params=_cp("arbitrary", "arbitrary") if comm else _cp("parallel", "parallel"),
    )(q, k, v, bias, *comm_arrays)


def _attn_bwd(q, k, v, o, do, lse, bias, *, batch, seq, width, col0, dilated, scale, name, comm=None, comm_arrays=()):
    t = min(ATTN_TILE, seq)
    nq = seq // t
    half = t // 2
    cq, ck, cv = col0
    pre = scale if dilated else 1.0
    dq_transposed = width == LANES
    steps = batch * N_PAIRS

    def body(*refs):
        ins, (dq_ref, dk_ref, dv_ref), (dq_acc, dk_acc, dv_acc, rowdot, q_heads, do_heads), plan = _comm_hooks(comm, refs, 7, 3)
        q_ref, k_ref, v_ref, o_ref, do_ref, lse_ref, bias_ref = ins
        step_no = pl.program_id(0) * N_PAIRS + pl.program_id(1)
        if plan:
            pl.when(step_no == 0)(lambda: comm.start(*plan))
        wlane = [_head_lanes(width, h) for h in (0, 1)]
        vlane = [_head_lanes(LANES, h) for h in (0, 1)]
        causal = lax.broadcasted_iota(I32, (t, t), 0) <= lax.broadcasted_iota(I32, (t, t), 1)
        q_all = q_ref[...] * pre if dilated else q_ref[...]
        for h in (0, 1):
            q_heads[h] = jnp.where(wlane[h], q_all, jnp.zeros_like(q_all))
            do_heads[h] = jnp.where(vlane[h], do_ref[...], jnp.zeros_like(do_ref[...]))
        prod = jnp.transpose(do_ref[...].astype(F32) * o_ref[...].astype(F32))
        rowdot[0:1, :] = jnp.sum(prod[0:HEAD_V], axis=0, keepdims=True)
        rowdot[1:2, :] = jnp.sum(prod[HEAD_V:], axis=0, keepdims=True)
        dq_acc[...] = jnp.zeros_like(dq_acc)

        def k_tile(j, _):
            ks = pl.multiple_of(j * t, t)
            kj = k_ref[pl.ds(ks, t), :]
            vj = v_ref[pl.ds(ks, t), :]
            kh = [jnp.where(wlane[h], kj, jnp.zeros_like(kj)) for h in (0, 1)]
            if dq_transposed:
                kh = [jnp.transpose(kh[h].astype(F32)).astype(BF16) for h in (0, 1)]
            dk_acc[...] = jnp.zeros_like(dk_acc)
            dv_acc[...] = jnp.zeros_like(dv_acc)

            def operands(i):
                qs = pl.multiple_of(i * t, t)
                return [q_heads[h, pl.ds(qs, t), :] for h in (0, 1)], [do_heads[h, pl.ds(qs, t), :] for h in (0, 1)]

            def scores(i):
                qih, _ = operands(i)
                return tuple(_dot_nt(kj, qih[h]) for h in (0, 1))

            def q_tile(n, carry, last):
                i = nq - 1 - n
                ahead = () if last else scores(i - 1)
                qs = pl.multiple_of(i * t, t)
                qih, doih = operands(i)

                def block(keys, queries):
                    count = queries.stop - queries.start
                    at = pl.ds(qs + queries.start, count)
                    dps = [_dot_nt(vj[keys], doih[h][queries]) for h in (0, 1)]
                    dq_b = jnp.zeros((width, count) if dq_transposed else (count, width), F32)
                    for h, (s, dp) in enumerate(zip(carry, dps)):
                        s = s[keys, queries]
                        if dilated:
                            s = s + (bias_ref[h, 0, keys, queries] if last else bias_ref[h, i - j])
                        else:
                            s = s * scale
                            if last:
                                s = jnp.where(causal[keys, queries], s, NEG)
                        p = jnp.exp(s - lse_ref[h:h + 1, at])
                        ds = p * (dp - rowdot[h:h + 1, at])
                        ds = (ds if dilated else ds * scale).astype(BF16)
                        dv_acc[keys, :] += _dot(p.astype(BF16), doih[h][queries])
                        dk_acc[keys, :] += _dot(ds, qih[h][queries])
                        dq_b = dq_b + (_dot(kh[h][:, keys], ds) if dq_transposed else _dot_tn(ds, kh[h][keys]))
                    if dq_transposed:
                        dq_acc[:, at] += dq_b
                    else:
                        dq_acc[at, :] += dq_b

                if last and half % LANES == 0:
                    block(slice(0, half), slice(0, half))
                    block(slice(0, t), slice(half, t))
                else:
                    block(slice(0, t), slice(0, t))
                return ahead

            q_tile(nq - 1 - j, lax.fori_loop(0, nq - 1 - j, functools.partial(q_tile, last=False), scores(nq - 1)), True)
            dk_ref[pl.ds(ks, t), :] = dk_acc[...].astype(BF16)
            dv_ref[pl.ds(ks, t), :] = dv_acc[...].astype(BF16)
            return 0

        lax.fori_loop(0, nq, k_tile, 0)
        dq_ref[...] = ((jnp.transpose(dq_acc[...]) if dq_transposed else dq_acc[...]) * pre).astype(BF16)
        if plan:
            pl.when(step_no == steps - 1)(lambda: comm.finish(*plan))

    tokens = batch * seq
    bias_spec = (pl.BlockSpec((2, nq, t, t), lambda b, p: (p, 0, 0, 0)) if dilated
                 else pl.BlockSpec((None, 8, LANES), lambda b, p: (0, 0, 0)))
    n = comm.n if comm else 0
    return pl.pallas_call(
        body, name=name, grid=(batch, N_PAIRS),
        out_shape=[jax.ShapeDtypeStruct((tokens, N_PAIRS * width), BF16), jax.ShapeDtypeStruct((tokens, N_PAIRS * width), BF16),
                   jax.ShapeDtypeStruct((tokens, DIL_WIDTH), BF16)] + (comm.out_shape if comm else []),
        in_specs=[pl.BlockSpec((seq, width), lambda b, p: (b, cq + p)),
                  pl.BlockSpec((seq, width), lambda b, p: (b, ck + p)),
                  pl.BlockSpec((seq, LANES), lambda b, p: (b, cv + p)),
                  pl.BlockSpec((seq, LANES), lambda b, p: (b, p)),
                  pl.BlockSpec((seq, LANES), lambda b, p: (b, p)),
                  pl.BlockSpec((None, 8, seq), lambda b, p: (b * N_PAIRS + p, 0, 0)),
                  bias_spec] + [ANY] * n,
        out_specs=[pl.BlockSpec((seq, width), lambda b, p: (b, p)),
                   pl.BlockSpec((seq, width), lambda b, p: (b, p)),
                   pl.BlockSpec((seq, LANES), lambda b, p: (b, p))] + [ANY] * n,
        scratch_shapes=[pltpu.VMEM((width, seq) if dq_transposed else (seq, width), F32),
                        pltpu.VMEM((t, width), F32), pltpu.VMEM((t, LANES), F32),
                        pltpu.VMEM((8, seq), F32), pltpu.VMEM((2, seq, width), BF16), pltpu.VMEM((2, seq, LANES), BF16)]
        + (comm.scratch if comm else []),
        compiler_params=_cp("arbitrary", "arbitrary") if comm else _cp("parallel", "parallel"),
    )(q, k, v, o, do, lse, bias, *comm_arrays)


def _rms(xf, g):
    r = lax.rsqrt(jnp.mean(xf * xf, axis=1, keepdims=True) + RMS_EPS)
    return xf * r * g, r


def _rms_bwd(dy, xf, r, g):
    gy = dy * g
    dx = r * gy - xf * (r * r * r) * jnp.mean(gy * xf, axis=1, keepdims=True)
    return dx, dy * xf * r


def _ln_bwd(dy, xhat, rstd, g):
    dxh = dy * g
    return rstd * (dxh - jnp.mean(dxh, axis=1, keepdims=True) - xhat * jnp.mean(dxh * xhat, axis=1, keepdims=True))


def _rope_slabs(q, cos, sin, transpose):
    first_half = (lax.broadcasted_iota(I32, (1, LANES), 1) % ROPE) < ROPE // 2
    out = []
    for p in range(N_PAIRS):
        blk = q[:, p * PAIR_W + LANES:(p + 1) * PAIR_W]
        y = blk * sin if transpose else blk
        up, down = pltpu.roll(y, LANES - ROPE // 2, 1), pltpu.roll(y, ROPE // 2, 1)
        rot = jnp.where(first_half, up, -down) if transpose else jnp.where(first_half, -up, down) * sin
        out += [q[:, p * PAIR_W:p * PAIR_W + LANES], blk * cos + rot]
    return jnp.concatenate(out, axis=1)


def _fwd_proj(x, w_in_ext, w1, wk, wv, g_q, g_kv, cext, sext, cs128, *, seq):
    tokens = x.shape[0]
    tm = min(MIX_TILE, seq)
    ns = seq // tm

    def body(x_ref, win_ref, w1_ref, wk_ref, wv_ref, gq_ref, gkv_ref, c_ref, s_ref, cs_ref,
             low_ref, gates_ref, qkvd_ref, qp_ref, kp_ref, vm_ref, qn_ref, kvn_ref, xb_ref):
        xt = x_ref[...].astype(BF16)
        xb_ref[...] = xt
        low = _dot(xt, win_ref[:, 0:LOW_W])
        low_ref[...] = low
        qkvd_ref[...] = _dot(xt, win_ref[:, LOW_W:LOW_W + 3 * DIL_WIDTH]).astype(BF16)
        gates_ref[...] = _dot(xt, win_ref[:, LOW_W + 3 * DIL_WIDTH:]).astype(BF16)
        qn = _rms(low[:, 0:Q_LORA], gq_ref[...])[0].astype(BF16)
        kvn = _rms(low[:, Q_LORA:Q_LORA + KV_LORA], gkv_ref[...])[0].astype(BF16)
        qn_ref[...] = qn
        kvn_ref[...] = kvn
        qp_ref[...] = _rope_slabs(_dot(qn, w1_ref[...]), c_ref[...], s_ref[...], False).astype(BF16)
        kr = low[:, Q_LORA + KV_LORA:] * cs_ref[...]
        kr = kr + pltpu.roll(kr, LANES - ROPE, 1)
        lane = lax.broadcasted_iota(I32, kr.shape, 1)
        kr = jnp.where(lane < ROPE, kr, 0.0)
        kr = (kr + pltpu.roll(kr, ROPE, 1)).astype(BF16)
        kn = _dot(kvn, wk_ref[...]).astype(BF16)
        kp_ref[...] = jnp.concatenate([blk for p in range(N_PAIRS) for blk in (kn[:, p * LANES:(p + 1) * LANES], kr)], axis=1)
        vm_ref[...] = _dot(kvn, wv_ref[...]).astype(BF16)

    n_gates = 2 * D_MODEL
    outs = [(LOW_W, F32), (n_gates, BF16), (3 * DIL_WIDTH, BF16), (N_PAIRS * PAIR_W, BF16), (N_PAIRS * PAIR_W, BF16),
            (DIL_WIDTH, BF16), (Q_LORA, BF16), (KV_LORA, BF16), (D_MODEL, BF16)]
    return pl.pallas_call(
        body, name="fwd_proj", grid=(tokens // tm,),
        out_shape=tuple(jax.ShapeDtypeStruct((tokens, w), dt) for w, dt in outs),
        in_specs=[_rows(tm, D_MODEL), _full(w_in_ext.shape), _full(w1.shape), _full(wk.shape),
                  _full(wv.shape), _full(g_q.shape), _full(g_kv.shape),
                  pl.BlockSpec((tm, LANES), lambda i: (i % ns, 1)),
                  pl.BlockSpec((tm, LANES), lambda i: (i % ns, 1)),
                  pl.BlockSpec((tm, LANES), lambda i: (i % ns, 0))],
        out_specs=tuple(_rows(tm, w) for w, _ in outs),
        compiler_params=_cp("parallel"),
    )(x, w_in_ext, w1, wk, wv, g_q, g_kv, cext, sext, cs128)


def _fwd_mix(o_a, o_b, gates, x, b_gate, w_oa, w_ob, w_out, ln_g, ln_b, *, seq):
    tokens = x.shape[0]
    tm = min(MIX_TILE, seq)

    def body(oa_ref, ob_ref, gt_ref, x_ref, bg_ref, woa_ref, wob_ref, wout_ref, g_ref, b_ref,
             hb_ref, xhat_ref, rstd_ref, ya_ref, yb_ref, mix_ref):
        ya = _dot(oa_ref[...], woa_ref[...])
        yb = _dot(ob_ref[...], wob_ref[...])
        g0 = _sigmoid(gt_ref[:, 0:D_MODEL].astype(F32) + bg_ref[0:1, :])
        g1 = _sigmoid(gt_ref[:, D_MODEL:].astype(F32) + bg_ref[1:2, :])
        mix = (g0 * ya + g1 * yb).astype(BF16)
        z = ALPHA * x_ref[...] + _dot(mix, wout_ref[...])
        zc = z - jnp.mean(z, axis=1, keepdims=True)
        rstd = lax.rsqrt(jnp.mean(zc * zc, axis=1, keepdims=True) + LN_EPS)
        xhat = zc * rstd
        hb_ref[...] = (xhat * g_ref[...] + b_ref[...]).astype(BF16)
        xhat_ref[...] = xhat
        rstd_ref[...] = jnp.broadcast_to(rstd, (tm, LANES))
        ya_ref[...] = ya.astype(BF16)
        yb_ref[...] = yb.astype(BF16)
        mix_ref[...] = mix

    outs = [(D_MODEL, BF16), (D_MODEL, F32), (LANES, F32), (D_MODEL, BF16), (D_MODEL, BF16), (D_MODEL, BF16)]
    return pl.pallas_call(
        body, name="fwd_mix", grid=(tokens // tm,),
        out_shape=tuple(jax.ShapeDtypeStruct((tokens, w), dt) for w, dt in outs),
        in_specs=[_rows(tm, DIL_WIDTH), _rows(tm, DIL_WIDTH), _rows(tm, 2 * D_MODEL), _rows(tm, D_MODEL),
                  _full(b_gate.shape), _full(w_oa.shape), _full(w_ob.shape), _full(w_out.shape),
                  _full(ln_g.shape), _full(ln_b.shape)],
        out_specs=tuple(_rows(tm, w) for w, _ in outs),
        compiler_params=_cp("parallel"),
    )(o_a, o_b, gates, x, b_gate, w_oa, w_ob, w_out, ln_g, ln_b)


def _fwd_mlp(hb, xhat1, target, w_ff1, w_ff2, ln1_g, ln1_b, ln_g, ln_b, *, seq):
    tokens = hb.shape[0]
    tm = min(2 * TOKEN_TILE, seq)
    tf = FF_SHARD
    nf = N_DEV // FF_STEP

    def body(hb_ref, xh_ref, tg_ref, w1_ref, w2_ref, g1_ref, b1_ref, g_ref, b_ref, u_ref, dz_ref, dzb_ref, stat_ref, acc):
        i, j = pl.program_id(0), pl.program_id(1)

        @pl.when((i == 0) & (j == 0))
        def _():
            stat_ref[...] = jnp.zeros_like(stat_ref)

        @pl.when(j == 0)
        def _():
            acc[...] = jnp.zeros_like(acc)

        acts = []
        for s in range(FF_STEP):
            u = _dot(hb_ref[...], w1_ref[s])
            u_ref[:, s * tf:(s + 1) * tf] = u.astype(BF16)
            acts.append(jnp.square(jnp.maximum(u, 0.0)).astype(BF16))
        acc[...] += _dot(jnp.concatenate(acts, axis=1), w2_ref[...])

        @pl.when(j == nf - 1)
        def _():
            z = ALPHA * (xh_ref[...] * g1_ref[...] + b1_ref[...]) + acc[...]
            zc = z - jnp.mean(z, axis=1, keepdims=True)
            rstd = lax.rsqrt(jnp.mean(zc * zc, axis=1, keepdims=True) + LN_EPS)
            xhat = zc * rstd
            err = xhat * g_ref[...] + b_ref[...] - tg_ref[...]
            dy = err * (1.0 / D_MODEL)
            dz = _ln_bwd(dy, xhat, rstd, g_ref[...])
            dz_ref[...] = dz
            dzb_ref[...] = dz.astype(BF16)
            stat_ref[0:1, :] += jnp.sum(dy * xhat, axis=0, keepdims=True)
            stat_ref[1:2, :] += jnp.sum(dy, axis=0, keepdims=True)
            stat_ref[2:3, :] += jnp.sum(jnp.sum(err * err, axis=1, keepdims=True), axis=0, keepdims=True) * (0.5 / D_MODEL)

    return pl.pallas_call(
        body, name="fwd_mlp", grid=(tokens // tm, nf),
        out_shape=(jax.ShapeDtypeStruct((tokens, D_FF), BF16), jax.ShapeDtypeStruct((tokens, D_MODEL), F32),
                   jax.ShapeDtypeStruct((tokens, D_MODEL), BF16), jax.ShapeDtypeStruct((8, D_MODEL), F32)),
        in_specs=[_rows(tm, D_MODEL), _rows(tm, D_MODEL), _rows(tm, D_MODEL),
                  pl.BlockSpec((FF_STEP, D_MODEL, tf), lambda i, j: (j, 0, 0)),
                  pl.BlockSpec((FF_STEP * tf, D_MODEL), lambda i, j: (j, 0)),
                  _full(ln1_g.shape), _full(ln1_b.shape), _full(ln_g.shape), _full(ln_b.shape)],
        out_specs=(pl.BlockSpec((tm, FF_STEP * tf), lambda i, j: (i, j)), _rows(tm, D_MODEL), _rows(tm, D_MODEL),
                   _full((8, D_MODEL))),
        scratch_shapes=[pltpu.VMEM((tm, D_MODEL), F32)],
        compiler_params=_cp("arbitrary", "arbitrary"),
    )(hb, xhat1, target, w_ff1, w_ff2, ln1_g, ln1_b, ln_g, ln_b)


def _bwd_mlp(dz2, dz2b, u, xhat1, rstd1, w_ff1, w_ff2, ln_g, *, seq):
    tokens = dz2.shape[0]
    tm = min(2 * TOKEN_TILE, seq)
    tf = FF_SHARD
    nf = N_DEV // FF_STEP

    def body(dz_ref, dzb_ref, u_ref, xh_ref, rs_ref, w1_ref, w2_ref, g_ref, du_ref, dz1_ref, dz1b_ref, stat_ref, acc):
        i, j = pl.program_id(0), pl.program_id(1)

        @pl.when((i == 0) & (j == 0))
        def _():
            stat_ref[...] = jnp.zeros_like(stat_ref)

        @pl.when(j == 0)
        def _():
            acc[...] = jnp.zeros_like(acc)

        da = _dot_nt(dzb_ref[...], w2_ref[...])
        du = (da * (2.0 * jnp.maximum(u_ref[...].astype(F32), 0.0))).astype(BF16)
        du_ref[...] = du
        part = _dot_nt(du[:, 0:tf], w1_ref[0])
        for s in range(1, FF_STEP):
            part = part + _dot_nt(du[:, s * tf:(s + 1) * tf], w1_ref[s])
        acc[...] += part

        @pl.when(j == nf - 1)
        def _():
            dh = ALPHA * dz_ref[...] + acc[...]
            xhat = xh_ref[...]
            dz1 = _ln_bwd(dh, xhat, rs_ref[:, 0:1], g_ref[...])
            dz1_ref[...] = dz1
            dz1b_ref[...] = dz1.astype(BF16)
            stat_ref[0:1, :] += jnp.sum(dh * xhat, axis=0, keepdims=True)
            stat_ref[1:2, :] += jnp.sum(dh, axis=0, keepdims=True)

    return pl.pallas_call(
        body, name="bwd_mlp", grid=(tokens // tm, nf),
        out_shape=(jax.ShapeDtypeStruct((tokens, D_FF), BF16), jax.ShapeDtypeStruct((tokens, D_MODEL), F32),
                   jax.ShapeDtypeStruct((tokens, D_MODEL), BF16), jax.ShapeDtypeStruct((8, D_MODEL), F32)),
        in_specs=[_rows(tm, D_MODEL), _rows(tm, D_MODEL), pl.BlockSpec((tm, FF_STEP * tf), lambda i, j: (i, j)),
                  _rows(tm, D_MODEL), _rows(tm, LANES),
                  pl.BlockSpec((FF_STEP, D_MODEL, tf), lambda i, j: (j, 0, 0)),
                  pl.BlockSpec((FF_STEP * tf, D_MODEL), lambda i, j: (j, 0)),
                  _full(ln_g.shape)],
        out_specs=(pl.BlockSpec((tm, FF_STEP * tf), lambda i, j: (i, j)), _rows(tm, D_MODEL), _rows(tm, D_MODEL),
                   _full((8, D_MODEL))),
        scratch_shapes=[pltpu.VMEM((tm, D_MODEL), F32)],
        compiler_params=_cp("arbitrary", "arbitrary"),
    )(dz2, dz2b, u, xhat1, rstd1, w_ff1, w_ff2, ln_g)


def _bwd_mix(dz1b, gates, y_a, y_b, b_gate, w_oa, w_ob, w_out, *, seq):
    tokens = dz1b.shape[0]
    tm = min(MIX_TILE, seq)

    def body(dz_ref, gt_ref, ya_ref, yb_ref, bg_ref, woa_ref, wob_ref, wout_ref,
             dgt_ref, dya_ref, dyb_ref, doa_ref, dob_ref, stat_ref):
        @pl.when(pl.program_id(0) == 0)
        def _():
            stat_ref[...] = jnp.zeros_like(stat_ref)

        dmix = _dot_nt(dz_ref[...], wout_ref[...])
        for k, (y_ref, w_ref, dy_ref, do_ref) in enumerate(((ya_ref, woa_ref, dya_ref, doa_ref), (yb_ref, wob_ref, dyb_ref, dob_ref))):
            g = _sigmoid(gt_ref[:, k * D_MODEL:(k + 1) * D_MODEL].astype(F32) + bg_ref[k:k + 1, :])
            dgate = dmix * y_ref[...].astype(F32) * g * (1.0 - g)
            dgt_ref[:, k * D_MODEL:(k + 1) * D_MODEL] = dgate.astype(BF16)
            stat_ref[k:k + 1, :] += jnp.sum(dgate, axis=0, keepdims=True)
            dy = (dmix * g).astype(BF16)
            dy_ref[...] = dy
            do_ref[...] = _dot_nt(dy, w_ref[...]).astype(BF16)

    outs = [(2 * D_MODEL, BF16), (D_MODEL, BF16), (D_MODEL, BF16), (DIL_WIDTH, BF16), (DIL_WIDTH, BF16)]
    return pl.pallas_call(
        body, name="bwd_mix", grid=(tokens // tm,),
        out_shape=tuple(jax.ShapeDtypeStruct((tokens, w), dt) for w, dt in outs) + (jax.ShapeDtypeStruct((8, D_MODEL), F32),),
        in_specs=[_rows(tm, D_MODEL), _rows(tm, 2 * D_MODEL), _rows(tm, D_MODEL), _rows(tm, D_MODEL),
                  _full(b_gate.shape), _full(w_oa.shape), _full(w_ob.shape), _full(w_out.shape)],
        out_specs=tuple(_rows(tm, w) for w, _ in outs) + (_full((8, D_MODEL)),),
        compiler_params=_cp("arbitrary"),
    )(dz1b, gates, y_a, y_b, b_gate, w_oa, w_ob, w_out)


def _bwd_proj(dqp, dkp, dvm, dq_d, dk_d, dv_d, dgates, dz1, low, w_in_ext, w1, wk, wv, g_q, g_kv, cext, sext, cs128, *, seq):
    tokens = dz1.shape[0]
    tm = min(TOKEN_TILE, seq)
    ns = seq // tm

    def body(dqp_ref, dkp_ref, dvm_ref, dqd_ref, dkd_ref, dvd_ref, dgt_ref, dz_ref, low_ref, win_ref, w1_ref, wk_ref,
             wv_ref, gq_ref, gkv_ref, c_ref, s_ref, cs_ref, dx_ref, dproj_ref, da_ref, dkn_ref, stat_ref):
        @pl.when(pl.program_id(0) == 0)
        def _():
            stat_ref[...] = jnp.zeros_like(stat_ref)

        low = low_ref[...]
        d_a = _rope_slabs(dqp_ref[...].astype(F32), c_ref[...], s_ref[...], True).astype(BF16)
        da_ref[...] = d_a
        q_a = low[:, 0:Q_LORA]
        _, rq = _rms(q_a, gq_ref[...])
        dq_a, gq_terms = _rms_bwd(_dot_nt(d_a, w1_ref[...]), q_a, rq, gq_ref[...])
        kv_a = low[:, Q_LORA:Q_LORA + KV_LORA]
        _, rkv = _rms(kv_a, gkv_ref[...])
        dkn = jnp.concatenate([dkp_ref[:, p * PAIR_W:p * PAIR_W + LANES] for p in range(N_PAIRS)], axis=1)
        dkn_ref[...] = dkn
        dkv_a, gkv_terms = _rms_bwd(_dot_nt(dkn, wk_ref[...]) + _dot_nt(dvm_ref[...], wv_ref[...]), kv_a, rkv, gkv_ref[...])
        dkr = sum(dkp_ref[:, p * PAIR_W + LANES:(p + 1) * PAIR_W].astype(F32) for p in range(N_PAIRS))
        dkr = dkr + pltpu.roll(dkr, LANES - ROPE, 1)
        dkr = jnp.where(lax.broadcasted_iota(I32, dkr.shape, 1) < ROPE, dkr, 0.0)
        dkr = (dkr + pltpu.roll(dkr, ROPE, 1)) * cs_ref[...]
        stat_ref[0:1, 0:Q_LORA] += jnp.sum(gq_terms, axis=0, keepdims=True)
        stat_ref[1:2, 0:KV_LORA] += jnp.sum(gkv_terms, axis=0, keepdims=True)
        dproj_ref[:, 0:Q_LORA] = dq_a.astype(BF16)
        dproj_ref[:, Q_LORA:Q_LORA + KV_LORA] = dkv_a.astype(BF16)
        dproj_ref[:, Q_LORA + KV_LORA:LOW_W] = dkr.astype(BF16)
        dproj_ref[:, LOW_W:LOW_W + DIL_WIDTH] = dqd_ref[...]
        dproj_ref[:, LOW_W + DIL_WIDTH:LOW_W + 2 * DIL_WIDTH] = dkd_ref[...]
        dproj_ref[:, LOW_W + 2 * DIL_WIDTH:LOW_W + 3 * DIL_WIDTH] = dvd_ref[...]
        dproj_ref[:, LOW_W + 3 * DIL_WIDTH:] = dgt_ref[...]
        dx_ref[...] = ALPHA * dz_ref[...] + _dot_nt(dproj_ref[...], win_ref[...])

    wide = N_PAIRS * PAIR_W
    return pl.pallas_call(
        body, name="bwd_proj", grid=(tokens // tm,),
        out_shape=(jax.ShapeDtypeStruct((tokens, D_MODEL), F32), jax.ShapeDtypeStruct((tokens, IN_EXT), BF16),
                   jax.ShapeDtypeStruct((tokens, wide), BF16), jax.ShapeDtypeStruct((tokens, N_HEADS * NOPE), BF16),
                   jax.ShapeDtypeStruct((8, D_MODEL), F32)),
        in_specs=[_rows(tm, wide), _rows(tm, wide), _rows(tm, DIL_WIDTH), _rows(tm, DIL_WIDTH), _rows(tm, DIL_WIDTH),
                  _rows(tm, DIL_WIDTH), _rows(tm, 2 * D_MODEL),
                  _rows(tm, D_MODEL), _rows(tm, LOW_W), _full(w_in_ext.shape), _full(w1.shape),
                  _full(wk.shape), _full(wv.shape), _full(g_q.shape), _full(g_kv.shape),
                  pl.BlockSpec((tm, LANES), lambda i: (i % ns, 1)), pl.BlockSpec((tm, LANES), lambda i: (i % ns, 1)),
                  pl.BlockSpec((tm, LANES), lambda i: (i % ns, 0))],
        out_specs=(_rows(tm, D_MODEL), _rows(tm, IN_EXT), _rows(tm, wide), _rows(tm, N_HEADS * NOPE), _full((8, D_MODEL))),
        compiler_params=_cp("arbitrary"),
    )(dqp, dkp, dvm, dq_d, dk_d, dv_d, dgates, dz1, low, w_in_ext, w1, wk, wv, g_q, g_kv, cext, sext, cs128)


def _wgrad(a, b, name, square_relu=False, by_shard=False):
    tokens, ka = a.shape
    n = b.shape[1]
    if ka <= 512 or ka % 512 == 0:
        tka = min(ka, 512)
    else:
        tka = max(w for w in range(LANES, min(ka, 2304) + 1, LANES) if ka % w == 0)
    shard = n // N_DEV
    tn = WGRAD_SHARDS * shard if by_shard else max(w for w in range(LANES, min(n, 2304) + 1, LANES) if n % w == 0)
    tt = min(tokens, 2048 if tka <= 512 else 1024)
    nt = tokens // tt

    def body(a_ref, b_ref, o_ref, acc):
        kt = pl.program_id(2)

        @pl.when(kt == 0)
        def _():
            acc[...] = jnp.zeros_like(acc)

        at = a_ref[...]
        if square_relu:
            at = jnp.square(jnp.maximum(at.astype(F32), 0.0)).astype(BF16)
        acc[...] += _dot_tn(at, b_ref[...])

        @pl.when(kt == nt - 1)
        def _():
            if by_shard:
                for s in range(WGRAD_SHARDS):
                    o_ref[s] = acc[:, s * shard:(s + 1) * shard].astype(BF16)
            else:
                o_ref[...] = acc[...].astype(BF16)

    if by_shard:
        out_shape, out_spec = (N_DEV, ka, shard), pl.BlockSpec((WGRAD_SHARDS, tka, shard), lambda i, j, k: (j, i, 0))
    else:
        out_shape, out_spec = (ka, n), pl.BlockSpec((tka, tn), lambda i, j, k: (i, j))
    return pl.pallas_call(
        body, name=name, grid=(ka // tka, n // tn, nt), out_shape=jax.ShapeDtypeStruct(out_shape, BF16),
        in_specs=[pl.BlockSpec((tt, tka), lambda i, j, k: (k, i)), pl.BlockSpec((tt, tn), lambda i, j, k: (k, j))],
        out_specs=out_spec,
        scratch_shapes=[pltpu.VMEM((tka, tn), F32)],
        compiler_params=_cp("parallel", "parallel", "arbitrary"),
    )(a, b)


def _adam_math(w, g, m, v):
    m = ADAM_B1 * m + (1.0 - ADAM_B1) * g
    v = ADAM_B2 * v + (1.0 - ADAM_B2) * jnp.square(g)
    m_hat = m / (1.0 - ADAM_B1 ** ADAM_STEP)
    v_hat = v / (1.0 - ADAM_B2 ** ADAM_STEP)
    return -ADAM_LR * (m_hat / (jnp.sqrt(v_hat) + ADAM_EPS) + ADAM_WD * w), m, v


def _adamw(items, name):
    steps = min(_tiles(*w.shape)[0] for w, *_ in items)
    n_items = len(items)

    def body(slot_ref, *refs):
        ins, outs = refs[:5 * n_items], refs[5 * n_items:]
        for k, (_, _, _, _, parts) in enumerate(items):
            w_ref, m_ref, v_ref, own_ref, p_ref = ins[5 * k:5 * k + 5]
            g_ref, d_ref, nm_ref, nv_ref = outs[4 * k:4 * k + 4]
            g = own_ref[...].astype(F32)
            for d in range(parts.shape[0]):
                g = g + p_ref[d].astype(F32)
            g_ref[...] = g
            d_ref[...], nm_ref[...], nv_ref[...] = _adam_math(w_ref[...], g, m_ref[...], v_ref[...])

    x, y, c = _place()
    in_specs, out_specs, out_shape, args = [], [], [], []
    for w, m, v, own, parts in items:
        rows, cols = w.shape
        _, tile, at = _tiles(rows, cols, steps)
        blk = pl.BlockSpec(tile, lambda i, slot, at=at: at(i))
        own_blk = blk if own.ndim == 2 else pl.BlockSpec((None, *tile), lambda i, slot, at=at: (slot[0], *at(i)))
        in_specs += [blk, blk, blk, own_blk, pl.BlockSpec((parts.shape[0], *tile), lambda i, slot, at=at: (0, *at(i)))]
        out_specs += [blk] * 4
        out_shape += [jax.ShapeDtypeStruct((rows, cols), F32)] * 4
        args += [w, m, v, own, parts]
    out = pl.pallas_call(
        body, name=name,
        grid_spec=pltpu.PrefetchScalarGridSpec(num_scalar_prefetch=1, grid=(steps,), in_specs=in_specs, out_specs=out_specs),
        out_shape=out_shape, compiler_params=_cp("parallel"),
    )(jnp.reshape(4 * x + 2 * y + c, (1,)).astype(I32), *args)
    return [tuple(out[4 * k:4 * k + 4]) for k in range(n_items)]


def _adamw_small(parts, w, m, v):
    _, rows, cols = parts.shape

    def body(p_ref, w_ref, m_ref, v_ref, g_ref, d_ref, nm_ref, nv_ref):
        g = p_ref[0]
        for d in range(1, N_DEV):
            g = g + p_ref[d]
        g_ref[...] = g
        d_ref[...], nm_ref[...], nv_ref[...] = _adam_math(w_ref[...], g, m_ref[...], v_ref[...])

    return pl.pallas_call(
        body, name="adamw_replicated", out_shape=(jax.ShapeDtypeStruct((rows, cols), F32),) * 4,
        in_specs=[_full(parts.shape)] + [_full((rows, cols))] * 3, out_specs=(_full((rows, cols)),) * 4, grid=(1,),
        compiler_params=_cp("arbitrary"),
    )(parts, w, m, v)


def _pad_rows(a2d, mult):
    pad = (-a2d.shape[-2]) % mult
    return jnp.pad(a2d, [(0, 0)] * (a2d.ndim - 2) + [(0, pad), (0, 0)]) if pad else a2d


def _pad_cols(a):
    pad = (-a.shape[-1]) % LANES
    return jnp.pad(a, [(0, 0)] * (a.ndim - 1) + [(0, pad)]) if pad else a


def _rot_cols(w):
    half = ROPE // 2
    return jnp.concatenate([-w[..., half:], w[..., :half]], axis=-1)


def _unrot_cols(dw):
    half = ROPE // 2
    return jnp.concatenate([dw[..., half:], -dw[..., :half]], axis=-1)


def _from_col_shards(stacked):
    return stacked.transpose(1, 0, 2).reshape(stacked.shape[1], -1)


def _to_col_shards(full):
    r = full.shape[0]
    return full.reshape(r, N_DEV, -1).transpose(1, 0, 2)


def _rope_tables(seq):
    half = ROPE // 2
    inv = jnp.power(ROPE_THETA, -jnp.arange(half, dtype=F32) / half)
    ang = jnp.arange(seq, dtype=F32)[:, None] * inv[None, :]
    cos = jnp.concatenate([jnp.cos(ang)] * 2, axis=1)
    sin = jnp.concatenate([jnp.sin(ang)] * 2, axis=1)
    ones, zeros = jnp.ones((seq, 2 * NOPE), F32), jnp.zeros((seq, 2 * NOPE), F32)
    pad = jnp.zeros((seq, PAIR_W - 2 * NOPE - 2 * ROPE), F32)
    cext = jnp.concatenate([ones, cos, cos, pad], axis=1)
    sext = jnp.concatenate([zeros, sin, sin, pad], axis=1)
    cs128 = jnp.concatenate([cos, sin, jnp.zeros((seq, LANES - 2 * ROPE), F32)], axis=1)
    return cext, sext, cs128


def _pair_slabs(nope, rope):
    k = nope.shape[0]
    nope = nope.reshape(k, N_PAIRS, 2 * NOPE)
    rope = rope.reshape(k, N_PAIRS, 2 * ROPE)
    pad = jnp.zeros((k, N_PAIRS, PAIR_W - 2 * NOPE - 2 * ROPE), nope.dtype)
    return jnp.concatenate([nope, rope, pad], axis=2).reshape(k, N_PAIRS * PAIR_W)


def _split_slabs(slabs):
    k = slabs.shape[0]
    s = slabs.reshape(k, N_PAIRS, PAIR_W)
    return s[:, :, :2 * NOPE].reshape(k, N_HEADS, NOPE), s[:, :, 2 * NOPE:2 * NOPE + 2 * ROPE].reshape(k, N_HEADS, ROPE)


def kernel(x, w_in, b_gate, g_q_a, w_uq, g_kv_a, w_ukv, w_o_mla, w_o_dil, w_out, ln1_g, ln1_b, w_ff1, w_ff2, ln2_g, ln2_b, loss_target, m_w_in, m_b_gate, m_g_q_a, m_w_uq, m_g_kv_a, m_w_ukv, m_w_o_mla, m_w_o_dil, m_w_out, m_ln1_g, m_ln1_b, m_w_ff1, m_w_ff2, m_ln2_g, m_ln2_b, v_w_in, v_b_gate, v_g_q_a, v_w_uq, v_g_kv_a, v_w_ukv, v_w_o_mla, v_w_o_dil, v_w_out, v_ln1_g, v_ln1_b, v_w_ff1, v_w_ff2, v_ln2_g, v_ln2_b):
    batch, seq, _ = x.shape
    tokens = batch * seq
    weights = dict(w_in=w_in, w_uq=w_uq, w_ukv=w_ukv, w_o_mla=w_o_mla, w_o_dil=w_o_dil, w_out=w_out, w_ff1=w_ff1, w_ff2=w_ff2, b_gate=b_gate)
    mom_m = dict(w_in=m_w_in, w_uq=m_w_uq, w_ukv=m_w_ukv, w_o_mla=m_w_o_mla, w_o_dil=m_w_o_dil, w_out=m_w_out, w_ff1=m_w_ff1, w_ff2=m_w_ff2, b_gate=m_b_gate)
    mom_v = dict(w_in=v_w_in, w_uq=v_w_uq, w_ukv=v_w_ukv, w_o_mla=v_w_o_mla, w_o_dil=v_w_o_dil, w_out=v_w_out, w_ff1=v_w_ff1, w_ff2=v_w_ff2, b_gate=v_b_gate)

    first = ["w_in", "w_uq", "w_ukv"]
    widths = [weights[n].shape[2] for n in first]
    shards = [weights["w_in"][0].T.astype(BF16)] + [_pad_cols(weights[n][0].astype(BF16)) for n in first[1:]]
    g_in, g_uq, g_ukv = _run_comm(_Gather(shards), shards, "all_gather_first_weights")
    g_uq, g_ukv = g_uq[:, :, :widths[1]], g_ukv[:, :, :widths[2]]

    s1, s2, n_in = Q_LORA + KV_LORA, Q_LORA + KV_LORA + ROPE, N_DEV * widths[0]

    def w_in_cols(lo, hi):
        out = []
        while lo < hi:
            d, off = divmod(lo, widths[0])
            take = min(hi - lo, widths[0] - off)
            out.append(g_in[d][off:off + take].T)
            lo += take
        return out

    w_in_ext = jnp.concatenate(w_in_cols(0, s2) + [_rot_cols(jnp.concatenate(w_in_cols(s1, s2), axis=1)),
                                                   jnp.zeros((D_MODEL, LOW_W - s2 - ROPE), BF16)] + w_in_cols(s2, n_in), axis=1)
    uq = _from_col_shards(g_uq).reshape(Q_LORA, N_HEADS, NOPE + ROPE)
    w1 = _pair_slabs(uq[:, :, :NOPE], uq[:, :, NOPE:])
    ukv = _from_col_shards(g_ukv).reshape(KV_LORA, N_HEADS, NOPE + HEAD_V)
    wk = ukv[:, :, :NOPE].reshape(KV_LORA, N_HEADS * NOPE)
    wv = ukv[:, :, NOPE:].reshape(KV_LORA, N_HEADS * HEAD_V)
    cext, sext, cs128 = _rope_tables(seq)
    dil_bias = _dilated_bias_table(seq)
    no_bias = jnp.zeros((1, 8, LANES), F32)

    x2 = x.reshape(tokens, D_MODEL)
    low, gates, qkvd, qp, kp, vm, qn, kvn, xb = _fwd_proj(x2, w_in_ext, w1, wk, wv, g_q_a, g_kv_a, cext, sext, cs128, seq=seq)
    bg = b_gate[0]
    bg_hi = bg.astype(BF16)
    bg_lo = (bg - bg_hi.astype(F32)).astype(BF16)
    later = [weights[n][0].astype(BF16) for n in ("w_o_mla", "w_o_dil", "w_out", "w_ff1", "w_ff2")]
    later.append(_pad_rows(jnp.concatenate([bg_hi, bg_lo], axis=0), 16))
    mla = dict(batch=batch, seq=seq, width=PAIR_W, col0=(0, 0, 0), dilated=False, scale=MLA_SCALE)
    dil = dict(batch=batch, seq=seq, width=LANES, col0=(0, N_PAIRS, 2 * N_PAIRS), dilated=True, scale=DIL_SCALE)
    o_a, lse_a, g_oa, g_ob, g_out, g_ff1, g_ff2, g_bg = _attn_fwd(
        qp, kp, vm, no_bias, name="mla_attention_fwd", comm=_Gather(later), comm_arrays=later, **mla)
    o_b, lse_b = _attn_fwd(qkvd, qkvd, qkvd, dil_bias, name="dilated_attention_fwd", **dil)
    w_oa, w_ob = _from_col_shards(g_oa), _from_col_shards(g_ob)
    w_out_full = g_out.reshape(D_MODEL, D_MODEL)
    w_ff2_full = g_ff2.reshape(D_FF, D_MODEL)
    bg_parts = g_bg.astype(F32)
    b_gate_full = _from_col_shards(bg_parts[:, 0:2] + bg_parts[:, 2:4])
    hb, xhat1, rstd1, y_a, y_b, mix = _fwd_mix(o_a, o_b, gates, x2, b_gate_full, w_oa, w_ob, w_out_full, ln1_g, ln1_b, seq=seq)
    u, dz2, dz2b, stat2 = _fwd_mlp(hb, xhat1, loss_target.reshape(tokens, D_MODEL), g_ff1, w_ff2_full, ln1_g, ln1_b, ln2_g, ln2_b, seq=seq)

    du, dz1, dz1b, stat1 = _bwd_mlp(dz2, dz2b, u, xhat1, rstd1, g_ff1, w_ff2_full, ln1_g, seq=seq)
    dw_ff = [_wgrad(hb, du, "wgrad_ff1", by_shard=True),
             _wgrad(u, dz2b, "wgrad_ff2", square_relu=True).reshape(N_DEV, FF_SHARD, D_MODEL)]
    dgates, dy_a, dy_b, do_a, do_b, stat_g = _bwd_mix(dz1b, gates, y_a, y_b, b_gate_full, w_oa, w_ob, w_out_full, seq=seq)
    dqp, dkp, dvm, r_ff1, r_ff2 = _attn_bwd(qp, kp, vm, o_a, do_a, lse_a, no_bias, name="mla_attention_bwd",
                                            comm=_Scatter(dw_ff), comm_arrays=dw_ff, **mla)
    dw_mid = [_to_col_shards(_wgrad(o_a, dy_a, "wgrad_o_mla")), _to_col_shards(_wgrad(o_b, dy_b, "wgrad_o_dil")),
              _wgrad(mix, dz1b, "wgrad_out").reshape(N_DEV, D_MODEL // N_DEV, D_MODEL),
              _pad_rows(_to_col_shards(stat_g[0:2]).astype(BF16), 16)]
    dq_d, dk_d, dv_d, r_oa, r_ob, r_out, r_bg = _attn_bwd(qkvd, qkvd, qkvd, o_b, do_b, lse_b, dil_bias, name="dilated_attention_bwd",
                                                          comm=_Scatter(dw_mid), comm_arrays=dw_mid, **dil)
    grad_x, dproj, d_a, dkn, stat_r = _bwd_proj(dqp, dkp, dvm, dq_d, dk_d, dv_d, dgates, dz1, low, w_in_ext, w1, wk, wv,
                                                g_q_a, g_kv_a, cext, sext, cs128, seq=seq)

    dw_in_ext = _wgrad(dproj, xb, "wgrad_in")
    dw1 = _wgrad(qn, d_a, "wgrad_uq")
    dwk = _wgrad(kvn, dkn, "wgrad_ukv_k")
    dwv = _wgrad(kvn, dvm, "wgrad_ukv_v")
    dw_kr = dw_in_ext[s1:s2] + _unrot_cols(dw_in_ext[s2:s2 + ROPE].T).T

    def dw_in_cols(lo, hi):
        out = []
        for a, b, piece in ((0, s1, lambda u, v: dw_in_ext[u:v]), (s1, s2, lambda u, v: dw_kr[u - s1:v - s1]),
                            (s2, n_in, lambda u, v: dw_in_ext[u + LOW_W - s2:v + LOW_W - s2])):
            if max(lo, a) < min(hi, b):
                out.append(piece(max(lo, a), min(hi, b)))
        return out

    dw_in = jnp.stack([jnp.concatenate(dw_in_cols(d * widths[0], (d + 1) * widths[0]), axis=0) for d in range(N_DEV)])
    n1, r1 = _split_slabs(dw1)
    dw_uq = jnp.concatenate([n1, r1], axis=2).reshape(Q_LORA, N_HEADS * (NOPE + ROPE))
    dw_ukv = jnp.concatenate([dwk.reshape(KV_LORA, N_HEADS, NOPE), dwv.reshape(KV_LORA, N_HEADS, HEAD_V)], axis=2).reshape(KV_LORA, N_HEADS * (NOPE + HEAD_V))
    last = [dw_in] + [_pad_cols(_to_col_shards(dw)) for dw in (dw_uq, dw_ukv)]
    theirs = _rs_sibling(last, "rs_last_sibling_exchange")
    sums = [_pair_sum(a, b, "rs_last_pair_sum_" + n) for a, b, n in zip(last, theirs, first)]
    partial = jnp.concatenate([stat_r[0:1, :Q_LORA], stat_r[1:2, :KV_LORA], stat1[0:1], stat1[1:2], stat2[0:1], stat2[1:2],
                               stat2[2:3, :LANES]], axis=1)
    partial = _pad_rows(partial.reshape(-1, LANES), 8)
    rest = [s[1] for s in sums]
    got_in, got_uq, got_ukv, every = _run_comm(_Plans([_ChipExchange(rest), _Gather([partial])]), rest + [partial],
                                               "rs_last_chip_exchange")

    upd = {}
    early = ["w_ff1", "w_ff2", "w_out", "w_o_mla", "w_o_dil"]
    items = [(weights[n][0], mom_m[n][0], mom_v[n][0], own, parts) for n, own, parts in
             zip(early, (dw_ff[0], dw_ff[1], dw_mid[2], dw_mid[0], dw_mid[1]), (r_ff1, r_ff2, r_out, r_oa, r_ob))]
    upd.update(zip(early, _adamw(items, "adamw_early_weights")))
    (in_t,) = _adamw([(weights["w_in"][0].T, mom_m["w_in"][0].T, mom_v["w_in"][0].T, sums[0][0], got_in)], "adamw_w_in")
    upd["w_in"] = tuple(a.T for a in in_t)
    for n, w, (own, _), parts in zip(first[1:], widths[1:], sums[1:], (got_uq, got_ukv)):
        (upd[n],) = _adamw([(weights[n][0], mom_m[n][0], mom_v[n][0], own[:, :w], parts[:, :, :w])], "adamw_" + n)
    (bg_upd,) = _adamw([(_pad_rows(b_gate[0], 16), _pad_rows(m_b_gate[0], 16), _pad_rows(v_b_gate[0], 16), dw_mid[3], r_bg)],
                       "adamw_b_gate")
    upd["b_gate"] = tuple(t[0:2] for t in bg_upd)

    small_w = [g_q_a, g_kv_a, ln1_g, ln1_b, ln2_g, ln2_b]
    small_m = [m_g_q_a, m_g_kv_a, m_ln1_g, m_ln1_b, m_ln2_g, m_ln2_b]
    small_v = [v_g_q_a, v_g_kv_a, v_ln1_g, v_ln1_b, v_ln2_g, v_ln2_b]
    small_widths = [a.shape[1] for a in small_w]

    def as_rows(vecs, extra):
        flat = jnp.concatenate(vecs + [jnp.zeros((1, extra), F32)], axis=1)
        return _pad_rows(flat.reshape(-1, LANES), 8)

    g_s, d_s, nm_s, nv_s = _adamw_small(every, as_rows(small_w, LANES), as_rows(small_m, LANES), as_rows(small_v, LANES))

    def split_small(a):
        flat = a.reshape(1, -1)
        out, c0 = [], 0
        for w in small_widths:
            out.append(flat[:, c0:c0 + w])
            c0 += w
        return out, flat[0, c0]

    g_small, loss = split_small(g_s)
    small = [g_small, split_small(d_s)[0], split_small(nm_s)[0], split_small(nv_s)[0]]

    order = ["w_in", "b_gate", "g_q_a", "w_uq", "g_kv_a", "w_ukv", "w_o_mla", "w_o_dil", "w_out", "ln1_g", "ln1_b", "w_ff1", "w_ff2", "ln2_g", "ln2_b"]
    small_names = ["g_q_a", "g_kv_a", "ln1_g", "ln1_b", "ln2_g", "ln2_b"]

    def pick(kind):
        return [small[kind][small_names.index(n)] if n in small_names else upd[n][kind][None] for n in order]

    return (loss, grad_x.reshape(batch, seq, D_MODEL), *pick(0), *pick(1), *pick(2), *pick(3))
```

```python
import functools
import math

import jax
import jax.numpy as jnp
from jax import lax
from jax.experimental import pallas as pl
from jax.experimental.pallas import tpu as pltpu

F32 = jnp.float32
BF16 = jnp.bfloat16
I32 = jnp.int32

D_MODEL = 1024
N_HEADS = 8
NOPE = 64
ROPE = 32
HEAD_V = 64
Q_LORA = 384
KV_LORA = 256
DIL_WIDTH = 512
D_FF = 4096
ROPE_THETA = 10000.0
LN_EPS = 1e-5
RMS_EPS = 1e-6
NEG = -1e30
ALPHA = 2.0 ** 0.25
MLA_SCALE = (NOPE + ROPE) ** -0.5
DIL_SCALE = 64 ** -0.5
ADAM_LR, ADAM_B1, ADAM_B2, ADAM_EPS, ADAM_WD, ADAM_STEP = 0.001, 0.9, 0.999, 1e-08, 0.01, 10

LANES = 128
PAIR_W = 256
N_PAIRS = N_HEADS // 2
LOW_W = 768
IN_EXT = LOW_W + 3 * DIL_WIDTH + 2 * D_MODEL
N_DEV = 8
FF_SHARD = D_FF // N_DEV
FF_STEP = 4
WGRAD_SHARDS = 4
TOKEN_TILE = 256
MIX_TILE = 512
ATTN_TILE = 512
VMEM_LIMIT = 56 << 20

MESH = pl.DeviceIdType.MESH
ANY = pl.BlockSpec(memory_space=pl.ANY)
CHIP_FLIPS = ((0, 0), (0, 1), (1, 0), (1, 1))
PEER_FLIPS = tuple((fx, fy, fc) for fx in (0, 1) for fy in (0, 1) for fc in (0, 1))[1:]


def _cp(*sem):
    return pltpu.CompilerParams(dimension_semantics=sem or None, vmem_limit_bytes=VMEM_LIMIT)


def _full(shape):
    nd = len(shape)
    return pl.BlockSpec(shape, lambda *_: (0,) * nd)


def _rows(tm, width):
    return pl.BlockSpec((tm, width), lambda i, *_: (i, 0))


def _dot(a, b):
    return jnp.dot(a, b, preferred_element_type=F32)


def _dot_nt(a, b):
    return lax.dot_general(a, b, (((1,), (1,)), ((), ())), preferred_element_type=F32)


def _dot_tn(a, b):
    return lax.dot_general(a, b, (((0,), (0,)), ((), ())), preferred_element_type=F32)


def _sigmoid(z):
    return 1.0 / (1.0 + jnp.exp(-z))


def _place():
    return lax.axis_index("x"), lax.axis_index("y"), lax.axis_index("c")


def _flip(v, f):
    return 1 - v if f else v


class _Gather:
    def __init__(self, shards):
        self.n = len(shards)
        self.out_shape = [jax.ShapeDtypeStruct((N_DEV, *s.shape), s.dtype) for s in shards]
        self.scratch = [pltpu.SemaphoreType.DMA((7 * self.n,)), pltpu.SemaphoreType.DMA((7 * self.n,)),
                        pltpu.SemaphoreType.DMA((self.n,))]

    def _copies(self, what, srcs, dsts, send, recv, local):
        x, y, c = _place()
        chips = [(_flip(x, fx), _flip(y, fy)) for fx, fy in CHIP_FLIPS[1:]]
        out = []
        for a in range(self.n):
            def slot(px, py, pc, a=a):
                return dsts[a].at[4 * px + 2 * py + pc]

            def copy(k, block, to, src=None, a=a, slot=slot):
                return pltpu.make_async_remote_copy(
                    src_ref=slot(*block) if src is None else src, dst_ref=slot(*block),
                    send_sem=send.at[7 * a + k], recv_sem=recv.at[7 * a + k], device_id=to, device_id_type=MESH)

            if what == "mine":
                out.append(pltpu.make_async_copy(srcs[a], slot(x, y, c), local.at[a]))
            elif what == "first":
                out.append(copy(0, (x, y, c), (x, y, 1 - c), src=srcs[a]))
                out += [copy(1 + j, (x, y, c), (*chip, c), src=srcs[a]) for j, chip in enumerate(chips)]
            elif what == "landed":
                out += [copy(1 + j, (*chip, c), (x, y, c)) for j, chip in enumerate(chips)]
            elif what == "passed":
                out += [copy(4 + j, (*chip, c), (x, y, 1 - c)) for j, chip in enumerate(chips)]
            else:
                out.append(copy(0, (x, y, 1 - c), (x, y, c)))
                out += [copy(4 + j, (*chip, 1 - c), (x, y, c)) for j, chip in enumerate(chips)]
        return out

    def start(self, *refs):
        for cp in self._copies("first", *refs) + self._copies("mine", *refs):
            cp.start()

    def forward(self, *refs):
        for landed, passed in zip(self._copies("landed", *refs), self._copies("passed", *refs)):
            landed.wait_recv()
            passed.start()

    def finish(self, *refs):
        for cp in self._copies("from_sibling", *refs):
            cp.wait_recv()
        for cp in self._copies("first", *refs) + self._copies("passed", *refs):
            cp.wait_send()
        for cp in self._copies("mine", *refs):
            cp.wait()


class _Scatter:
    def __init__(self, arrays):
        self.n = len(arrays)
        self.out_shape = [jax.ShapeDtypeStruct((7, *a.shape[1:]), a.dtype) for a in arrays]
        self.scratch = [pltpu.SemaphoreType.DMA((7 * self.n,)), pltpu.SemaphoreType.DMA((7 * self.n,))]

    def _copies(self, srcs, dsts, send, recv):
        x, y, c = _place()
        out = []
        for a in range(self.n):
            for k, (fx, fy, fc) in enumerate(PEER_FLIPS):
                px, py, pc = _flip(x, fx), _flip(y, fy), _flip(c, fc)
                out.append(pltpu.make_async_remote_copy(
                    src_ref=srcs[a].at[4 * px + 2 * py + pc], dst_ref=dsts[a].at[k],
                    send_sem=send.at[7 * a + k], recv_sem=recv.at[7 * a + k], device_id=(px, py, pc), device_id_type=MESH))
        return out

    def start(self, *refs):
        for cp in self._copies(*refs):
            cp.start()

    def forward(self, *refs):
        pass

    def finish(self, *refs):
        for cp in self._copies(*refs):
            cp.wait_send()
        for cp in self._copies(*refs):
            cp.wait_recv()


class _ChipExchange:
    def __init__(self, arrays):
        self.n = len(arrays)
        self.out_shape = [jax.ShapeDtypeStruct(a.shape, a.dtype) for a in arrays]
        self.scratch = [pltpu.SemaphoreType.DMA((3 * self.n,)), pltpu.SemaphoreType.DMA((3 * self.n,))]

    def _copies(self, srcs, dsts, send, recv):
        x, y, c = _place()
        return [pltpu.make_async_remote_copy(
            src_ref=srcs[a].at[k], dst_ref=dsts[a].at[k], send_sem=send.at[3 * a + k], recv_sem=recv.at[3 * a + k],
            device_id=(_flip(x, fx), _flip(y, fy), c), device_id_type=MESH)
            for a in range(self.n) for k, (fx, fy) in enumerate(CHIP_FLIPS[1:])]

    def start(self, *refs):
        for cp in self._copies(*refs):
            cp.start()

    def forward(self, *refs):
        pass

    def finish(self, *refs):
        for cp in self._copies(*refs):
            cp.wait_send()
        for cp in self._copies(*refs):
            cp.wait_recv()


class _Plans:
    def __init__(self, plans):
        self.plans = plans
        self.n = sum(p.n for p in plans)
        self.out_shape = [s for p in plans for s in p.out_shape]
        self.scratch = [s for p in plans for s in p.scratch]

    def _each(self, phase, srcs, dsts, *sems):
        i0 = s0 = 0
        for p in self.plans:
            getattr(p, phase)(srcs[i0:i0 + p.n], dsts[i0:i0 + p.n], *sems[s0:s0 + len(p.scratch)])
            i0, s0 = i0 + p.n, s0 + len(p.scratch)

    def start(self, *refs):
        self._each("start", *refs)

    def forward(self, *refs):
        self._each("forward", *refs)

    def finish(self, *refs):
        self._each("finish", *refs)


def _run_comm(comm, arrays, name):
    n = comm.n

    def body(*refs):
        args = (refs[:n], refs[n:2 * n], *refs[2 * n:])
        comm.start(*args)
        comm.forward(*args)
        comm.finish(*args)

    return pl.pallas_call(body, name=name, out_shape=comm.out_shape, in_specs=[ANY] * n, out_specs=[ANY] * n,
                          scratch_shapes=comm.scratch)(*arrays)


def _rs_sibling(arrays, name):
    n = len(arrays)

    def body(*refs):
        srcs, got, (send, recv) = refs[:n], refs[n:2 * n], refs[2 * n:]
        x, y, c = _place()
        copies = []
        for a in range(n):
            for r, (fx, fy) in enumerate(CHIP_FLIPS):
                chip = 2 * _flip(x, fx) + _flip(y, fy)
                copies.append(pltpu.make_async_remote_copy(
                    src_ref=srcs[a].at[2 * chip + 1 - c], dst_ref=got[a].at[r], send_sem=send.at[4 * a + r],
                    recv_sem=recv.at[4 * a + r], device_id=(x, y, 1 - c), device_id_type=MESH))
        for cp in copies:
            cp.start()
        for cp in copies:
            cp.wait_send()
        for cp in copies:
            cp.wait_recv()

    return pl.pallas_call(
        body, name=name, out_shape=[jax.ShapeDtypeStruct((4, *a.shape[1:]), a.dtype) for a in arrays],
        in_specs=[ANY] * n, out_specs=[ANY] * n,
        scratch_shapes=[pltpu.SemaphoreType.DMA((4 * n,)), pltpu.SemaphoreType.DMA((4 * n,))],
    )(*arrays)


def _chip_slots():
    x, y, c = _place()
    return jnp.stack([4 * _flip(x, fx) + 2 * _flip(y, fy) + c for fx, fy in CHIP_FLIPS]).astype(I32)


def _tiles(rows, cols, steps=4):
    if rows % (16 * steps) == 0:
        return steps, (rows // steps, cols), lambda i: (i, 0)
    if cols % (LANES * steps) == 0:
        return steps, (rows, cols // steps), lambda i: (0, i)
    return 1, (rows, cols), lambda i: (0, 0)


def _pair_sum(full, theirs, name):
    _, rows, cols = theirs.shape
    steps, tile, at = _tiles(rows, cols)

    def body(slots_ref, m0_ref, m1_ref, m2_ref, m3_ref, b_ref, own_ref, rest_ref):
        own_ref[...] = m0_ref[...].astype(F32) + b_ref[0].astype(F32)
        for k, m_ref in enumerate((m1_ref, m2_ref, m3_ref)):
            rest_ref[k] = (m_ref[...].astype(F32) + b_ref[k + 1].astype(F32)).astype(BF16)

    def mine(k):
        return pl.BlockSpec((None, *tile), lambda i, slots: (slots[k], *at(i)))

    return pl.pallas_call(
        body, name=name,
        grid_spec=pltpu.PrefetchScalarGridSpec(
            num_scalar_prefetch=1, grid=(steps,),
            in_specs=[mine(0), mine(1), mine(2), mine(3), pl.BlockSpec((4, *tile), lambda i, slots: (0, *at(i)))],
            out_specs=(pl.BlockSpec(tile, lambda i, slots: at(i)), pl.BlockSpec((3, *tile), lambda i, slots: (0, *at(i))))),
        out_shape=(jax.ShapeDtypeStruct((rows, cols), F32), jax.ShapeDtypeStruct((3, rows, cols), BF16)),
        compiler_params=_cp("parallel"),
    )(_chip_slots(), full, full, full, full, theirs)


def _head_lanes(width, h):
    lane = lax.broadcasted_iota(I32, (1, width), 1)
    if width == LANES:
        return (lane >= 64 * h) & (lane < 64 * h + 64)
    nope = (lane >= NOPE * h) & (lane < NOPE * h + NOPE)
    rope = (lane >= 2 * NOPE + ROPE * h) & (lane < 2 * NOPE + ROPE * h + ROPE)
    return nope | rope


def _dilated_bias_table(seq):
    t = min(ATTN_TILE, seq)
    nd = seq // t

    def body(o_ref):
        delta = pl.program_id(0) * t + lax.broadcasted_iota(I32, (t, t), 1) - lax.broadcasted_iota(I32, (t, t), 0)
        mult = ((delta <= 128).astype(I32) + (((delta & 3) == 0) & (delta <= 512)).astype(I32)
                + ((delta & 15) == 0).astype(I32))
        logm = jnp.where(mult == 3, math.log(3.0), jnp.where(mult == 2, math.log(2.0), 0.0))
        valid = (delta >= 0) & (mult > 0)
        dist = delta.astype(F32)
        for h in range(N_HEADS):
            o_ref[h] = jnp.where(valid, logm - 2.0 ** (-(h + 1)) * dist, NEG)

    return pl.pallas_call(
        body, name="dilated_bias_table", grid=(nd,), out_shape=jax.ShapeDtypeStruct((N_HEADS, nd, t, t), F32),
        out_specs=pl.BlockSpec((N_HEADS, None, t, t), lambda d: (0, d, 0, 0)),
        compiler_params=_cp("parallel"),
    )()


def _comm_hooks(comm, refs, n_in, n_out):
    if comm is None:
        return refs[:n_in], refs[n_in:n_in + n_out], refs[n_in + n_out:], None
    n = comm.n
    ins, srcs = refs[:n_in], refs[n_in:n_in + n]
    outs, dsts = refs[n_in + n:n_in + n + n_out], refs[n_in + n + n_out:n_in + 2 * n + n_out]
    rest = refs[n_in + 2 * n + n_out:]
    own = len(rest) - len(comm.scratch)
    return ins, outs, rest[:own], (srcs, dsts, *rest[own:])


def _attn_fwd(q, k, v, bias, *, batch, seq, width, col0, dilated, scale, name, comm=None, comm_arrays=()):
    t = min(ATTN_TILE, seq)
    nq = seq // t
    half = t // 2
    cq, ck, cv = col0
    pre = scale if dilated else 1.0
    steps = batch * N_PAIRS

    def body(*refs):
        (q_ref, k_ref, v_ref, bias_ref), (o_ref, lse_ref), (v_heads,), plan = _comm_hooks(comm, refs, 4, 2)
        step_no = pl.program_id(0) * N_PAIRS + pl.program_id(1)
        if plan:
            pl.when(step_no == 0)(lambda: comm.start(*plan))
            pl.when(step_no == (3 * steps) // 4)(lambda: comm.forward(*plan))
        v_all = v_ref[...].astype(F32)
        for h in (0, 1):
            v_heads[h] = jnp.transpose(jnp.where(_head_lanes(LANES, h), v_all, 0.0)).astype(BF16)
        top = lax.broadcasted_iota(I32, (LANES, t), 0) < HEAD_V
        causal = lax.broadcasted_iota(I32, (t, t), 0) <= lax.broadcasted_iota(I32, (t, t), 1)
        def heads(i):
            q2 = q_ref[pl.ds(pl.multiple_of(i * t, t), t), :]
            q2 = q2 * pre if dilated else q2
            return [jnp.where(_head_lanes(width, h), q2, jnp.zeros_like(q2)) for h in (0, 1)]

        def scores(qh, j):
            kj = k_ref[pl.ds(pl.multiple_of(j * t, t), t), :]
            return tuple(_dot_nt(kj, qh[h]) for h in (0, 1))

        lax.fori_loop(0, nq, functools.partial(query_tile, heads, scores, bias_ref, o_ref, lse_ref, v_heads, top, causal),
                      0)
        if plan:
            pl.when(step_no == steps - 1)(lambda: comm.finish(*plan))

    def query_tile(heads, scores, bias_ref, o_ref, lse_ref, v_heads, top, causal, i, _):
        qs = pl.multiple_of(i * t, t)
        qh = heads(i)

        def step(j, carry, last):
            m0, l0, m1, l1, acc = carry
            s0, s1 = scores(qh, j)
            ks = pl.multiple_of(j * t, t)
            new, alphas, pv = [], [], []

            def online(h, m, l, s, keys, queries):
                s = s[keys, queries]
                if dilated:
                    s = s + (bias_ref[h, 0, keys, queries] if last else bias_ref[h, i - j])
                else:
                    s = s * scale
                    if last:
                        s = jnp.where(causal[keys, queries], s, NEG)
                m, l = m[:, queries], l[:, queries]
                m_new = jnp.maximum(m, jnp.max(s, axis=0, keepdims=True))
                a = jnp.exp(m - m_new)
                p = jnp.exp(s - m_new)
                v_keys = v_heads[h, :, pl.ds(ks, t)]
                return m_new, a * l + jnp.sum(p, axis=0, keepdims=True), a, _dot(v_keys[:, keys], p.astype(BF16))

            for h, (m, l, s) in enumerate(((m0, l0, s0), (m1, l1, s1))):
                if last and half % LANES == 0:
                    parts = [online(h, m, l, s, slice(0, half), slice(0, half)),
                             online(h, m, l, s, slice(0, t), slice(half, t))]
                    m_new, l_new, a, pv_h = (jnp.concatenate(x, axis=1) for x in zip(*parts))
                else:
                    m_new, l_new, a, pv_h = online(h, m, l, s, slice(0, t), slice(0, t))
                new += [m_new, l_new]
                alphas.append(a)
                pv.append(pv_h)
            acc = jnp.where(top, alphas[0], alphas[1]) * acc + pv[0] + pv[1]
            return (*new, acc)

        row = jnp.full((1, t), NEG, F32)
        zero = jnp.zeros((1, t), F32)
        init = (row, zero, row, zero, jnp.zeros((LANES, t), F32))
        m0, l0, m1, l1, acc = step(i, lax.fori_loop(0, i, functools.partial(step, last=False), init), True)
        o_ref[pl.ds(qs, t), :] = jnp.transpose(acc * jnp.where(top, 1.0 / l0, 1.0 / l1)).astype(BF16)
        r = lax.broadcasted_iota(I32, (8, t), 0)
        lse_ref[:, pl.ds(qs, t)] = jnp.where(r == 0, m0 + jnp.log(l0), jnp.where(r == 1, m1 + jnp.log(l1), 0.0))
        return 0

    bias_spec = (pl.BlockSpec((2, nq, t, t), lambda b, p: (p, 0, 0, 0)) if dilated
                 else pl.BlockSpec((None, 8, LANES), lambda b, p: (0, 0, 0)))
    n = comm.n if comm else 0
    return pl.pallas_call(
        body, name=name, grid=(batch, N_PAIRS),
        out_shape=[jax.ShapeDtypeStruct((batch * seq, DIL_WIDTH), BF16), jax.ShapeDtypeStruct((batch * N_PAIRS, 8, seq), F32)]
        + (comm.out_shape if comm else []),
        in_specs=[pl.BlockSpec((seq, width), lambda b, p: (b, cq + p)),
                  pl.BlockSpec((seq, width), lambda b, p: (b, ck + p)),
                  pl.BlockSpec((seq, LANES), lambda b, p: (b, cv + p)),
                  bias_spec] + [ANY] * n,
        out_specs=[pl.BlockSpec((seq, LANES), lambda b, p: (b, p)),
                   pl.BlockSpec((None, 8, seq), lambda b, p: (b * N_PAIRS + p, 0, 0))] + [ANY] * n,
        scratch_shapes=[pltpu.VMEM((2, LANES, seq), BF16)] + (comm.scratch if comm else []),
        compiler_params=_cp("arbitrary", "arbitrary") if comm else _cp("parallel", "parallel"),
    )(q, k, v, bias, *comm_arrays)


def _attn_bwd(q, k, v, o, do, lse, bias, *, batch, seq, width, col0, dilated, scale, name, comm=None, comm_arrays=()):
    t = min(ATTN_TILE, seq)
    nq = seq // t
    half = t // 2
    cq, ck, cv = col0
    pre = scale if dilated else 1.0
    dq_transposed = width == LANES
    steps = batch * N_PAIRS

    def body(*refs):
        ins, (dq_ref, dk_ref, dv_ref), (dq_acc, dk_acc, dv_acc, rowdot, q_heads, do_heads), plan = _comm_hooks(comm, refs, 7, 3)
        q_ref, k_ref, v_ref, o_ref, do_ref, lse_ref, bias_ref = ins
        step_no = pl.program_id(0) * N_PAIRS + pl.program_id(1)
        if plan:
            pl.when(step_no == 0)(lambda: comm.start(*plan))
        wlane = [_head_lanes(width, h) for h in (0, 1)]
        vlane = [_head_lanes(LANES, h) for h in (0, 1)]
        causal = lax.broadcasted_iota(I32, (t, t), 0) <= lax.broadcasted_iota(I32, (t, t), 1)
        q_all = q_ref[...] * pre if dilated else q_ref[...]
        for h in (0, 1):
            q_heads[h] = jnp.where(wlane[h], q_all, jnp.zeros_like(q_all))
            do_heads[h] = jnp.where(vlane[h], do_ref[...], jnp.zeros_like(do_ref[...]))
        prod = jnp.transpose(do_ref[...].astype(F32) * o_ref[...].astype(F32))
        rowdot[0:1, :] = jnp.sum(prod[0:HEAD_V], axis=0, keepdims=True)
        rowdot[1:2, :] = jnp.sum(prod[HEAD_V:], axis=0, keepdims=True)
        dq_acc[...] = jnp.zeros_like(dq_acc)

        def k_tile(j, _):
            ks = pl.multiple_of(j * t, t)
            kj = k_ref[pl.ds(ks, t), :]
            vj = v_ref[pl.ds(ks, t), :]
            kh = [jnp.where(wlane[h], kj, jnp.zeros_like(kj)) for h in (0, 1)]
            if dq_transposed:
                kh = [jnp.transpose(kh[h].astype(F32)).astype(BF16) for h in (0, 1)]
            dk_acc[...] = jnp.zeros_like(dk_acc)
            dv_acc[...] = jnp.zeros_like(dv_acc)

            def operands(i):
                qs = pl.multiple_of(i * t, t)
                return [q_heads[h, pl.ds(qs, t), :] for h in (0, 1)], [do_heads[h, pl.ds(qs, t), :] for h in (0, 1)]

            def q_tile(n, _, last):
                i = nq - 1 - n
                qs = pl.multiple_of(i * t, t)
                qih, doih = operands(i)

                def block(keys, queries):
                    count = queries.stop - queries.start
                    at = pl.ds(qs + queries.start, count)
                    ss = [_dot_nt(kj[keys], qih[h][queries]) for h in (0, 1)]
                    dps = [_dot_nt(vj[keys], doih[h][queries]) for h in (0, 1)]
                    dq_b = jnp.zeros((width, count) if dq_transposed else (count, width), F32)
                    for h, (s, dp) in enumerate(zip(ss, dps)):
                        if dilated:
                            s = s + (bias_ref[h, 0, keys, queries] if last else bias_ref[h, i - j])
                        else:
                            s = s * scale
                            if last:
                                s = jnp.where(causal[keys, queries], s, NEG)
                        p = jnp.exp(s - lse_ref[h:h + 1, at])
                        ds = p * (dp - rowdot[h:h + 1, at])
                        ds = (ds if dilated else ds * scale).astype(BF16)
                        dv_acc[keys, :] += _dot(p.astype(BF16), doih[h][queries])
                        dk_acc[keys, :] += _dot(ds, qih[h][queries])
                        dq_b = dq_b + (_dot(kh[h][:, keys], ds) if dq_transposed else _dot_tn(ds, kh[h][keys]))
                    if dq_transposed:
                        dq_acc[:, at] += dq_b
                    else:
                        dq_acc[at, :] += dq_b

                if last and half % LANES == 0:
                    block(slice(0, half), slice(0, half))
                    block(slice(0, t), slice(half, t))
                else:
                    block(slice(0, t), slice(0, t))
                return 0

            q_tile(nq - 1 - j, lax.fori_loop(0, nq - 1 - j, functools.partial(q_tile, last=False), 0), True)
            dk_ref[pl.ds(ks, t), :] = dk_acc[...].astype(BF16)
            dv_ref[pl.ds(ks, t), :] = dv_acc[...].astype(BF16)
            return 0

        lax.fori_loop(0, nq, k_tile, 0)
        dq_ref[...] = ((jnp.transpose(dq_acc[...]) if dq_transposed else dq_acc[...]) * pre).astype(BF16)
        if plan:
            pl.when(step_no == steps - 1)(lambda: comm.finish(*plan))

    tokens = batch * seq
    bias_spec = (pl.BlockSpec((2, nq, t, t), lambda b, p: (p, 0, 0, 0)) if dilated
                 else pl.BlockSpec((None, 8, LANES), lambda b, p: (0, 0, 0)))
    n = comm.n if comm else 0
    return pl.pallas_call(
        body, name=name, grid=(batch, N_PAIRS),
        out_shape=[jax.ShapeDtypeStruct((tokens, N_PAIRS * width), BF16), jax.ShapeDtypeStruct((tokens, N_PAIRS * width), BF16),
                   jax.ShapeDtypeStruct((tokens, DIL_WIDTH), BF16)] + (comm.out_shape if comm else []),
        in_specs=[pl.BlockSpec((seq, width), lambda b, p: (b, cq + p)),
                  pl.BlockSpec((seq, width), lambda b, p: (b, ck + p)),
                  pl.BlockSpec((seq, LANES), lambda b, p: (b, cv + p)),
                  pl.BlockSpec((seq, LANES), lambda b, p: (b, p)),
                  pl.BlockSpec((seq, LANES), lambda b, p: (b, p)),
                  pl.BlockSpec((None, 8, seq), lambda b, p: (b * N_PAIRS + p, 0, 0)),
                  bias_spec] + [ANY] * n,
        out_specs=[pl.BlockSpec((seq, width), lambda b, p: (b, p)),
                   pl.BlockSpec((seq, width), lambda b, p: (b, p)),
                   pl.BlockSpec((seq, LANES), lambda b, p: (b, p))] + [ANY] * n,
        scratch_shapes=[pltpu.VMEM((width, seq) if dq_transposed else (seq, width), F32),
                        pltpu.VMEM((t, width), F32), pltpu.VMEM((t, LANES), F32),
                        pltpu.VMEM((8, seq), F32), pltpu.VMEM((2, seq, width), BF16), pltpu.VMEM((2, seq, LANES), BF16)]
        + (comm.scratch if comm else []),
        compiler_params=_cp("arbitrary", "arbitrary") if comm else _cp("parallel", "parallel"),
    )(q, k, v, o, do, lse, bias, *comm_arrays)


def _rms(xf, g):
    r = lax.rsqrt(jnp.mean(xf * xf, axis=1, keepdims=True) + RMS_EPS)
    return xf * r * g, r


def _rms_bwd(dy, xf, r, g):
    gy = dy * g
    dx = r * gy - xf * (r * r * r) * jnp.mean(gy * xf, axis=1, keepdims=True)
    return dx, dy * xf * r


def _ln_bwd(dy, xhat, rstd, g):
    dxh = dy * g
    return rstd * (dxh - jnp.mean(dxh, axis=1, keepdims=True) - xhat * jnp.mean(dxh * xhat, axis=1, keepdims=True))


def _rope_slabs(q, cos, sin, transpose):
    first_half = (lax.broadcasted_iota(I32, (1, LANES), 1) % ROPE) < ROPE // 2
    out = []
    for p in range(N_PAIRS):
        blk = q[:, p * PAIR_W + LANES:(p + 1) * PAIR_W]
        y = blk * sin if transpose else blk
        up, down = pltpu.roll(y, LANES - ROPE // 2, 1), pltpu.roll(y, ROPE // 2, 1)
        rot = jnp.where(first_half, up, -down) if transpose else jnp.where(first_half, -up, down) * sin
        out += [q[:, p * PAIR_W:p * PAIR_W + LANES], blk * cos + rot]
    return jnp.concatenate(out, axis=1)


def _fwd_proj(x, w_in_ext, w1, wk, wv, g_q, g_kv, cext, sext, cs128, *, seq):
    tokens = x.shape[0]
    tm = min(MIX_TILE, seq)
    ns = seq // tm

    def body(x_ref, win_ref, w1_ref, wk_ref, wv_ref, gq_ref, gkv_ref, c_ref, s_ref, cs_ref,
             low_ref, gates_ref, qkvd_ref, qp_ref, kp_ref, vm_ref, qn_ref, kvn_ref, xb_ref):
        xt = x_ref[...].astype(BF16)
        xb_ref[...] = xt
        low = _dot(xt, win_ref[:, 0:LOW_W])
        low_ref[...] = low
        qkvd_ref[...] = _dot(xt, win_ref[:, LOW_W:LOW_W + 3 * DIL_WIDTH]).astype(BF16)
        gates_ref[...] = _dot(xt, win_ref[:, LOW_W + 3 * DIL_WIDTH:]).astype(BF16)
        qn = _rms(low[:, 0:Q_LORA], gq_ref[...])[0].astype(BF16)
        kvn = _rms(low[:, Q_LORA:Q_LORA + KV_LORA], gkv_ref[...])[0].astype(BF16)
        qn_ref[...] = qn
        kvn_ref[...] = kvn
        qp_ref[...] = _rope_slabs(_dot(qn, w1_ref[...]), c_ref[...], s_ref[...], False).astype(BF16)
        kr = low[:, Q_LORA + KV_LORA:] * cs_ref[...]
        kr = kr + pltpu.roll(kr, LANES - ROPE, 1)
        lane = lax.broadcasted_iota(I32, kr.shape, 1)
        kr = jnp.where(lane < ROPE, kr, 0.0)
        kr = (kr + pltpu.roll(kr, ROPE, 1)).astype(BF16)
        kn = _dot(kvn, wk_ref[...]).astype(BF16)
        kp_ref[...] = jnp.concatenate([blk for p in range(N_PAIRS) for blk in (kn[:, p * LANES:(p + 1) * LANES], kr)], axis=1)
        vm_ref[...] = _dot(kvn, wv_ref[...]).astype(BF16)

    n_gates = 2 * D_MODEL
    outs = [(LOW_W, F32), (n_gates, BF16), (3 * DIL_WIDTH, BF16), (N_PAIRS * PAIR_W, BF16), (N_PAIRS * PAIR_W, BF16),
            (DIL_WIDTH, BF16), (Q_LORA, BF16), (KV_LORA, BF16), (D_MODEL, BF16)]
    return pl.pallas_call(
        body, name="fwd_proj", grid=(tokens // tm,),
        out_shape=tuple(jax.ShapeDtypeStruct((tokens, w), dt) for w, dt in outs),
        in_specs=[_rows(tm, D_MODEL), _full(w_in_ext.shape), _full(w1.shape), _full(wk.shape),
                  _full(wv.shape), _full(g_q.shape), _full(g_kv.shape),
                  pl.BlockSpec((tm, LANES), lambda i: (i % ns, 1)),
                  pl.BlockSpec((tm, LANES), lambda i: (i % ns, 1)),
                  pl.BlockSpec((tm, LANES), lambda i: (i % ns, 0))],
        out_specs=tuple(_rows(tm, w) for w, _ in outs),
        compiler_params=_cp("parallel"),
    )(x, w_in_ext, w1, wk, wv, g_q, g_kv, cext, sext, cs128)


def _fwd_mix(o_a, o_b, gates, x, b_gate, w_oa, w_ob, w_out, ln_g, ln_b, *, seq):
    tokens = x.shape[0]
    tm = min(MIX_TILE, seq)

    def body(oa_ref, ob_ref, gt_ref, x_ref, bg_ref, woa_ref, wob_ref, wout_ref, g_ref, b_ref,
             hb_ref, xhat_ref, rstd_ref, ya_ref, yb_ref, mix_ref):
        ya = _dot(oa_ref[...], woa_ref[...])
        yb = _dot(ob_ref[...], wob_ref[...])
        g0 = _sigmoid(gt_ref[:, 0:D_MODEL].astype(F32) + bg_ref[0:1, :])
        g1 = _sigmoid(gt_ref[:, D_MODEL:].astype(F32) + bg_ref[1:2, :])
        mix = (g0 * ya + g1 * yb).astype(BF16)
        z = ALPHA * x_ref[...] + _dot(mix, wout_ref[...])
        zc = z - jnp.mean(z, axis=1, keepdims=True)
        rstd = lax.rsqrt(jnp.mean(zc * zc, axis=1, keepdims=True) + LN_EPS)
        xhat = zc * rstd
        hb_ref[...] = (xhat * g_ref[...] + b_ref[...]).astype(BF16)
        xhat_ref[...] = xhat
        rstd_ref[...] = jnp.broadcast_to(rstd, (tm, LANES))
        ya_ref[...] = ya.astype(BF16)
        yb_ref[...] = yb.astype(BF16)
        mix_ref[...] = mix

    outs = [(D_MODEL, BF16), (D_MODEL, F32), (LANES, F32), (D_MODEL, BF16), (D_MODEL, BF16), (D_MODEL, BF16)]
    return pl.pallas_call(
        body, name="fwd_mix", grid=(tokens // tm,),
        out_shape=tuple(jax.ShapeDtypeStruct((tokens, w), dt) for w, dt in outs),
        in_specs=[_rows(tm, DIL_WIDTH), _rows(tm, DIL_WIDTH), _rows(tm, 2 * D_MODEL), _rows(tm, D_MODEL),
                  _full(b_gate.shape), _full(w_oa.shape), _full(w_ob.shape), _full(w_out.shape),
                  _full(ln_g.shape), _full(ln_b.shape)],
        out_specs=tuple(_rows(tm, w) for w, _ in outs),
        compiler_params=_cp("parallel"),
    )(o_a, o_b, gates, x, b_gate, w_oa, w_ob, w_out, ln_g, ln_b)


def _fwd_mlp(hb, xhat1, target, w_ff1, w_ff2, ln1_g, ln1_b, ln_g, ln_b, *, seq):
    tokens = hb.shape[0]
    tm = min(2 * TOKEN_TILE, seq)
    tf = FF_SHARD
    nf = N_DEV // FF_STEP

    def body(hb_ref, xh_ref, tg_ref, w1_ref, w2_ref, g1_ref, b1_ref, g_ref, b_ref, u_ref, dz_ref, dzb_ref, stat_ref, acc):
        i, j = pl.program_id(0), pl.program_id(1)

        @pl.when((i == 0) & (j == 0))
        def _():
            stat_ref[...] = jnp.zeros_like(stat_ref)

        @pl.when(j == 0)
        def _():
            acc[...] = jnp.zeros_like(acc)

        acts = []
        for s in range(FF_STEP):
            u = _dot(hb_ref[...], w1_ref[s])
            u_ref[:, s * tf:(s + 1) * tf] = u.astype(BF16)
            acts.append(jnp.square(jnp.maximum(u, 0.0)).astype(BF16))
        acc[...] += _dot(jnp.concatenate(acts, axis=1), w2_ref[...])

        @pl.when(j == nf - 1)
        def _():
            z = ALPHA * (xh_ref[...] * g1_ref[...] + b1_ref[...]) + acc[...]
            zc = z - jnp.mean(z, axis=1, keepdims=True)
            rstd = lax.rsqrt(jnp.mean(zc * zc, axis=1, keepdims=True) + LN_EPS)
            xhat = zc * rstd
            err = xhat * g_ref[...] + b_ref[...] - tg_ref[...]
            dy = err * (1.0 / D_MODEL)
            dz = _ln_bwd(dy, xhat, rstd, g_ref[...])
            dz_ref[...] = dz
            dzb_ref[...] = dz.astype(BF16)
            stat_ref[0:1, :] += jnp.sum(dy * xhat, axis=0, keepdims=True)
            stat_ref[1:2, :] += jnp.sum(dy, axis=0, keepdims=True)
            stat_ref[2:3, :] += jnp.sum(jnp.sum(err * err, axis=1, keepdims=True), axis=0, keepdims=True) * (0.5 / D_MODEL)

    return pl.pallas_call(
        body, name="fwd_mlp", grid=(tokens // tm, nf),
        out_shape=(jax.ShapeDtypeStruct((tokens, D_FF), BF16), jax.ShapeDtypeStruct((tokens, D_MODEL), F32),
                   jax.ShapeDtypeStruct((tokens, D_MODEL), BF16), jax.ShapeDtypeStruct((8, D_MODEL), F32)),
        in_specs=[_rows(tm, D_MODEL), _rows(tm, D_MODEL), _rows(tm, D_MODEL),
                  pl.BlockSpec((FF_STEP, D_MODEL, tf), lambda i, j: (j, 0, 0)),
                  pl.BlockSpec((FF_STEP * tf, D_MODEL), lambda i, j: (j, 0)),
                  _full(ln1_g.shape), _full(ln1_b.shape), _full(ln_g.shape), _full(ln_b.shape)],
        out_specs=(pl.BlockSpec((tm, FF_STEP * tf), lambda i, j: (i, j)), _rows(tm, D_MODEL), _rows(tm, D_MODEL),
                   _full((8, D_MODEL))),
        scratch_shapes=[pltpu.VMEM((tm, D_MODEL), F32)],
        compiler_params=_cp("arbitrary", "arbitrary"),
    )(hb, xhat1, target, w_ff1, w_ff2, ln1_g, ln1_b, ln_g, ln_b)


def _bwd_mlp(dz2, dz2b, u, xhat1, rstd1, w_ff1, w_ff2, ln_g, *, seq):
    tokens = dz2.shape[0]
    tm = min(2 * TOKEN_TILE, seq)
    tf = FF_SHARD
    nf = N_DEV // FF_STEP

    def body(dz_ref, dzb_ref, u_ref, xh_ref, rs_ref, w1_ref, w2_ref, g_ref, du_ref, dz1_ref, dz1b_ref, stat_ref, acc):
        i, j = pl.program_id(0), pl.program_id(1)

        @pl.when((i == 0) & (j == 0))
        def _():
            stat_ref[...] = jnp.zeros_like(stat_ref)

        @pl.when(j == 0)
        def _():
            acc[...] = jnp.zeros_like(acc)

        da = _dot_nt(dzb_ref[...], w2_ref[...])
        du = (da * (2.0 * jnp.maximum(u_ref[...].astype(F32), 0.0))).astype(BF16)
        du_ref[...] = du
        part = _dot_nt(du[:, 0:tf], w1_ref[0])
        for s in range(1, FF_STEP):
            part = part + _dot_nt(du[:, s * tf:(s + 1) * tf], w1_ref[s])
        acc[...] += part

        @pl.when(j == nf - 1)
        def _():
            dh = ALPHA * dz_ref[...] + acc[...]
            xhat = xh_ref[...]
            dz1 = _ln_bwd(dh, xhat, rs_ref[:, 0:1], g_ref[...])
            dz1_ref[...] = dz1
            dz1b_ref[...] = dz1.astype(BF16)
            stat_ref[0:1, :] += jnp.sum(dh * xhat, axis=0, keepdims=True)
            stat_ref[1:2, :] += jnp.sum(dh, axis=0, keepdims=True)

    return pl.pallas_call(
        body, name="bwd_mlp", grid=(tokens // tm, nf),
        out_shape=(jax.ShapeDtypeStruct((tokens, D_FF), BF16), jax.ShapeDtypeStruct((tokens, D_MODEL), F32),
                   jax.ShapeDtypeStruct((tokens, D_MODEL), BF16), jax.ShapeDtypeStruct((8, D_MODEL), F32)),
        in_specs=[_rows(tm, D_MODEL), _rows(tm, D_MODEL), pl.BlockSpec((tm, FF_STEP * tf), lambda i, j: (i, j)),
                  _rows(tm, D_MODEL), _rows(tm, LANES),
                  pl.BlockSpec((FF_STEP, D_MODEL, tf), lambda i, j: (j, 0, 0)),
                  pl.BlockSpec((FF_STEP * tf, D_MODEL), lambda i, j: (j, 0)),
                  _full(ln_g.shape)],
        out_specs=(pl.BlockSpec((tm, FF_STEP * tf), lambda i, j: (i, j)), _rows(tm, D_MODEL), _rows(tm, D_MODEL),
                   _full((8, D_MODEL))),
        scratch_shapes=[pltpu.VMEM((tm, D_MODEL), F32)],
        compiler_params=_cp("arbitrary", "arbitrary"),
    )(dz2, dz2b, u, xhat1, rstd1, w_ff1, w_ff2, ln_g)


def _bwd_mix(dz1b, gates, y_a, y_b, b_gate, w_oa, w_ob, w_out, *, seq):
    tokens = dz1b.shape[0]
    tm = min(MIX_TILE, seq)

    def body(dz_ref, gt_ref, ya_ref, yb_ref, bg_ref, woa_ref, wob_ref, wout_ref,
             dgt_ref, dya_ref, dyb_ref, doa_ref, dob_ref, stat_ref):
        @pl.when(pl.program_id(0) == 0)
        def _():
            stat_ref[...] = jnp.zeros_like(stat_ref)

        dmix = _dot_nt(dz_ref[...], wout_ref[...])
        for k, (y_ref, w_ref, dy_ref, do_ref) in enumerate(((ya_ref, woa_ref, dya_ref, doa_ref), (yb_ref, wob_ref, dyb_ref, dob_ref))):
            g = _sigmoid(gt_ref[:, k * D_MODEL:(k + 1) * D_MODEL].astype(F32) + bg_ref[k:k + 1, :])
            dgate = dmix * y_ref[...].astype(F32) * g * (1.0 - g)
            dgt_ref[:, k * D_MODEL:(k + 1) * D_MODEL] = dgate.astype(BF16)
            stat_ref[k:k + 1, :] += jnp.sum(dgate, axis=0, keepdims=True)
            dy = (dmix * g).astype(BF16)
            dy_ref[...] = dy
            do_ref[...] = _dot_nt(dy, w_ref[...]).astype(BF16)

    outs = [(2 * D_MODEL, BF16), (D_MODEL, BF16), (D_MODEL, BF16), (DIL_WIDTH, BF16), (DIL_WIDTH, BF16)]
    return pl.pallas_call(
        body, name="bwd_mix", grid=(tokens // tm,),
        out_shape=tuple(jax.ShapeDtypeStruct((tokens, w), dt) for w, dt in outs) + (jax.ShapeDtypeStruct((8, D_MODEL), F32),),
        in_specs=[_rows(tm, D_MODEL), _rows(tm, 2 * D_MODEL), _rows(tm, D_MODEL), _rows(tm, D_MODEL),
                  _full(b_gate.shape), _full(w_oa.shape), _full(w_ob.shape), _full(w_out.shape)],
        out_specs=tuple(_rows(tm, w) for w, _ in outs) + (_full((8, D_MODEL)),),
        compiler_params=_cp("arbitrary"),
    )(dz1b, gates, y_a, y_b, b_gate, w_oa, w_ob, w_out)


def _bwd_proj(dqp, dkp, dvm, dq_d, dk_d, dv_d, dgates, dz1, low, w_in_ext, w1, wk, wv, g_q, g_kv, cext, sext, cs128, *, seq):
    tokens = dz1.shape[0]
    tm = min(TOKEN_TILE, seq)
    ns = seq // tm

    def body(dqp_ref, dkp_ref, dvm_ref, dqd_ref, dkd_ref, dvd_ref, dgt_ref, dz_ref, low_ref, win_ref, w1_ref, wk_ref,
             wv_ref, gq_ref, gkv_ref, c_ref, s_ref, cs_ref, dx_ref, dproj_ref, da_ref, dkn_ref, stat_ref):
        @pl.when(pl.program_id(0) == 0)
        def _():
            stat_ref[...] = jnp.zeros_like(stat_ref)

        low = low_ref[...]
        d_a = _rope_slabs(dqp_ref[...].astype(F32), c_ref[...], s_ref[...], True).astype(BF16)
        da_ref[...] = d_a
        q_a = low[:, 0:Q_LORA]
        _, rq = _rms(q_a, gq_ref[...])
        dq_a, gq_terms = _rms_bwd(_dot_nt(d_a, w1_ref[...]), q_a, rq, gq_ref[...])
        kv_a = low[:, Q_LORA:Q_LORA + KV_LORA]
        _, rkv = _rms(kv_a, gkv_ref[...])
        dkn = jnp.concatenate([dkp_ref[:, p * PAIR_W:p * PAIR_W + LANES] for p in range(N_PAIRS)], axis=1)
        dkn_ref[...] = dkn
        dkv_a, gkv_terms = _rms_bwd(_dot_nt(dkn, wk_ref[...]) + _dot_nt(dvm_ref[...], wv_ref[...]), kv_a, rkv, gkv_ref[...])
        dkr = sum(dkp_ref[:, p * PAIR_W + LANES:(p + 1) * PAIR_W].astype(F32) for p in range(N_PAIRS))
        dkr = dkr + pltpu.roll(dkr, LANES - ROPE, 1)
        dkr = jnp.where(lax.broadcasted_iota(I32, dkr.shape, 1) < ROPE, dkr, 0.0)
        dkr = (dkr + pltpu.roll(dkr, ROPE, 1)) * cs_ref[...]
        stat_ref[0:1, 0:Q_LORA] += jnp.sum(gq_terms, axis=0, keepdims=True)
        stat_ref[1:2, 0:KV_LORA] += jnp.sum(gkv_terms, axis=0, keepdims=True)
        dproj_ref[:, 0:Q_LORA] = dq_a.astype(BF16)
        dproj_ref[:, Q_LORA:Q_LORA + KV_LORA] = dkv_a.astype(BF16)
        dproj_ref[:, Q_LORA + KV_LORA:LOW_W] = dkr.astype(BF16)
        dproj_ref[:, LOW_W:LOW_W + DIL_WIDTH] = dqd_ref[...]
        dproj_ref[:, LOW_W + DIL_WIDTH:LOW_W + 2 * DIL_WIDTH] = dkd_ref[...]
        dproj_ref[:, LOW_W + 2 * DIL_WIDTH:LOW_W + 3 * DIL_WIDTH] = dvd_ref[...]
        dproj_ref[:, LOW_W + 3 * DIL_WIDTH:] = dgt_ref[...]
        dx_ref[...] = ALPHA * dz_ref[...] + _dot_nt(dproj_ref[...], win_ref[...])

    wide = N_PAIRS * PAIR_W
    return pl.pallas_call(
        body, name="bwd_proj", grid=(tokens // tm,),
        out_shape=(jax.ShapeDtypeStruct((tokens, D_MODEL), F32), jax.ShapeDtypeStruct((tokens, IN_EXT), BF16),
                   jax.ShapeDtypeStruct((tokens, wide), BF16), jax.ShapeDtypeStruct((tokens, N_HEADS * NOPE), BF16),
                   jax.ShapeDtypeStruct((8, D_MODEL), F32)),
        in_specs=[_rows(tm, wide), _rows(tm, wide), _rows(tm, DIL_WIDTH), _rows(tm, DIL_WIDTH), _rows(tm, DIL_WIDTH),
                  _rows(tm, DIL_WIDTH), _rows(tm, 2 * D_MODEL),
                  _rows(tm, D_MODEL), _rows(tm, LOW_W), _full(w_in_ext.shape), _full(w1.shape),
                  _full(wk.shape), _full(wv.shape), _full(g_q.shape), _full(g_kv.shape),
                  pl.BlockSpec((tm, LANES), lambda i: (i % ns, 1)), pl.BlockSpec((tm, LANES), lambda i: (i % ns, 1)),
                  pl.BlockSpec((tm, LANES), lambda i: (i % ns, 0))],
        out_specs=(_rows(tm, D_MODEL), _rows(tm, IN_EXT), _rows(tm, wide), _rows(tm, N_HEADS * NOPE), _full((8, D_MODEL))),
        compiler_params=_cp("arbitrary"),
    )(dqp, dkp, dvm, dq_d, dk_d, dv_d, dgates, dz1, low, w_in_ext, w1, wk, wv, g_q, g_kv, cext, sext, cs128)


def _wgrad(a, b, name, square_relu=False, by_shard=False):
    tokens, ka = a.shape
    n = b.shape[1]
    if ka <= 512 or ka % 512 == 0:
        tka = min(ka, 512)
    else:
        tka = max(w for w in range(LANES, min(ka, 2304) + 1, LANES) if ka % w == 0)
    shard = n // N_DEV
    tn = WGRAD_SHARDS * shard if by_shard else max(w for w in range(LANES, min(n, 2304) + 1, LANES) if n % w == 0)
    tt = min(tokens, 2048 if tka <= 512 else 1024)
    nt = tokens // tt

    def body(a_ref, b_ref, o_ref, acc):
        kt = pl.program_id(2)

        @pl.when(kt == 0)
        def _():
            acc[...] = jnp.zeros_like(acc)

        at = a_ref[...]
        if square_relu:
            at = jnp.square(jnp.maximum(at.astype(F32), 0.0)).astype(BF16)
        acc[...] += _dot_tn(at, b_ref[...])

        @pl.when(kt == nt - 1)
        def _():
            if by_shard:
                for s in range(WGRAD_SHARDS):
                    o_ref[s] = acc[:, s * shard:(s + 1) * shard].astype(BF16)
            else:
                o_ref[...] = acc[...].astype(BF16)

    if by_shard:
        out_shape, out_spec = (N_DEV, ka, shard), pl.BlockSpec((WGRAD_SHARDS, tka, shard), lambda i, j, k: (j, i, 0))
    else:
        out_shape, out_spec = (ka, n), pl.BlockSpec((tka, tn), lambda i, j, k: (i, j))
    return pl.pallas_call(
        body, name=name, grid=(ka // tka, n // tn, nt), out_shape=jax.ShapeDtypeStruct(out_shape, BF16),
        in_specs=[pl.BlockSpec((tt, tka), lambda i, j, k: (k, i)), pl.BlockSpec((tt, tn), lambda i, j, k: (k, j))],
        out_specs=out_spec,
        scratch_shapes=[pltpu.VMEM((tka, tn), F32)],
        compiler_params=_cp("parallel", "parallel", "arbitrary"),
    )(a, b)


def _adam_math(w, g, m, v):
    m = ADAM_B1 * m + (1.0 - ADAM_B1) * g
    v = ADAM_B2 * v + (1.0 - ADAM_B2) * jnp.square(g)
    m_hat = m / (1.0 - ADAM_B1 ** ADAM_STEP)
    v_hat = v / (1.0 - ADAM_B2 ** ADAM_STEP)
    return -ADAM_LR * (m_hat / (jnp.sqrt(v_hat) + ADAM_EPS) + ADAM_WD * w), m, v


def _adamw(items, name):
    steps = min(_tiles(*w.shape)[0] for w, *_ in items)
    n_items = len(items)

    def body(slot_ref, *refs):
        ins, outs = refs[:5 * n_items], refs[5 * n_items:]
        for k, (_, _, _, _, parts) in enumerate(items):
            w_ref, m_ref, v_ref, own_ref, p_ref = ins[5 * k:5 * k + 5]
            g_ref, d_ref, nm_ref, nv_ref = outs[4 * k:4 * k + 4]
            g = own_ref[...].astype(F32)
            for d in range(parts.shape[0]):
                g = g + p_ref[d].astype(F32)
            g_ref[...] = g
            d_ref[...], nm_ref[...], nv_ref[...] = _adam_math(w_ref[...], g, m_ref[...], v_ref[...])

    x, y, c = _place()
    in_specs, out_specs, out_shape, args = [], [], [], []
    for w, m, v, own, parts in items:
        rows, cols = w.shape
        _, tile, at = _tiles(rows, cols, steps)
        blk = pl.BlockSpec(tile, lambda i, slot, at=at: at(i))
        own_blk = blk if own.ndim == 2 else pl.BlockSpec((None, *tile), lambda i, slot, at=at: (slot[0], *at(i)))
        in_specs += [blk, blk, blk, own_blk, pl.BlockSpec((parts.shape[0], *tile), lambda i, slot, at=at: (0, *at(i)))]
        out_specs += [blk] * 4
        out_shape += [jax.ShapeDtypeStruct((rows, cols), F32)] * 4
        args += [w, m, v, own, parts]
    out = pl.pallas_call(
        body, name=name,
        grid_spec=pltpu.PrefetchScalarGridSpec(num_scalar_prefetch=1, grid=(steps,), in_specs=in_specs, out_specs=out_specs),
        out_shape=out_shape, compiler_params=_cp("parallel"),
    )(jnp.reshape(4 * x + 2 * y + c, (1,)).astype(I32), *args)
    return [tuple(out[4 * k:4 * k + 4]) for k in range(n_items)]


def _adamw_small(parts, w, m, v):
    _, rows, cols = parts.shape

    def body(p_ref, w_ref, m_ref, v_ref, g_ref, d_ref, nm_ref, nv_ref):
        g = p_ref[0]
        for d in range(1, N_DEV):
            g = g + p_ref[d]
        g_ref[...] = g
        d_ref[...], nm_ref[...], nv_ref[...] = _adam_math(w_ref[...], g, m_ref[...], v_ref[...])

    return pl.pallas_call(
        body, name="adamw_replicated", out_shape=(jax.ShapeDtypeStruct((rows, cols), F32),) * 4,
        in_specs=[_full(parts.shape)] + [_full((rows, cols))] * 3, out_specs=(_full((rows, cols)),) * 4, grid=(1,),
        compiler_params=_cp("arbitrary"),
    )(parts, w, m, v)


def _pad_rows(a2d, mult):
    pad = (-a2d.shape[-2]) % mult
    return jnp.pad(a2d, [(0, 0)] * (a2d.ndim - 2) + [(0, pad), (0, 0)]) if pad else a2d


def _pad_cols(a):
    pad = (-a.shape[-1]) % LANES
    return jnp.pad(a, [(0, 0)] * (a.ndim - 1) + [(0, pad)]) if pad else a


def _rot_cols(w):
    half = ROPE // 2
    return jnp.concatenate([-w[..., half:], w[..., :half]], axis=-1)


def _unrot_cols(dw):
    half = ROPE // 2
    return jnp.concatenate([dw[..., half:], -dw[..., :half]], axis=-1)


def _from_col_shards(stacked):
    return stacked.transpose(1, 0, 2).reshape(stacked.shape[1], -1)


def _to_col_shards(full):
    r = full.shape[0]
    return full.reshape(r, N_DEV, -1).transpose(1, 0, 2)


def _rope_tables(seq):
    half = ROPE // 2
    inv = jnp.power(ROPE_THETA, -jnp.arange(half, dtype=F32) / half)
    ang = jnp.arange(seq, dtype=F32)[:, None] * inv[None, :]
    cos = jnp.concatenate([jnp.cos(ang)] * 2, axis=1)
    sin = jnp.concatenate([jnp.sin(ang)] * 2, axis=1)
    ones, zeros = jnp.ones((seq, 2 * NOPE), F32), jnp.zeros((seq, 2 * NOPE), F32)
    pad = jnp.zeros((seq, PAIR_W - 2 * NOPE - 2 * ROPE), F32)
    cext = jnp.concatenate([ones, cos, cos, pad], axis=1)
    sext = jnp.concatenate([zeros, sin, sin, pad], axis=1)
    cs128 = jnp.concatenate([cos, sin, jnp.zeros((seq, LANES - 2 * ROPE), F32)], axis=1)
    return cext, sext, cs128


def _pair_slabs(nope, rope):
    k = nope.shape[0]
    nope = nope.reshape(k, N_PAIRS, 2 * NOPE)
    rope = rope.reshape(k, N_PAIRS, 2 * ROPE)
    pad = jnp.zeros((k, N_PAIRS, PAIR_W - 2 * NOPE - 2 * ROPE), nope.dtype)
    return jnp.concatenate([nope, rope, pad], axis=2).reshape(k, N_PAIRS * PAIR_W)


def _split_slabs(slabs):
    k = slabs.shape[0]
    s = slabs.reshape(k, N_PAIRS, PAIR_W)
    return s[:, :, :2 * NOPE].reshape(k, N_HEADS, NOPE), s[:, :, 2 * NOPE:2 * NOPE + 2 * ROPE].reshape(k, N_HEADS, ROPE)


def kernel(x, w_in, b_gate, g_q_a, w_uq, g_kv_a, w_ukv, w_o_mla, w_o_dil, w_out, ln1_g, ln1_b, w_ff1, w_ff2, ln2_g, ln2_b, loss_target, m_w_in, m_b_gate, m_g_q_a, m_w_uq, m_g_kv_a, m_w_ukv, m_w_o_mla, m_w_o_dil, m_w_out, m_ln1_g, m_ln1_b, m_w_ff1, m_w_ff2, m_ln2_g, m_ln2_b, v_w_in, v_b_gate, v_g_q_a, v_w_uq, v_g_kv_a, v_w_ukv, v_w_o_mla, v_w_o_dil, v_w_out, v_ln1_g, v_ln1_b, v_w_ff1, v_w_ff2, v_ln2_g, v_ln2_b):
    batch, seq, _ = x.shape
    tokens = batch * seq
    weights = dict(w_in=w_in, w_uq=w_uq, w_ukv=w_ukv, w_o_mla=w_o_mla, w_o_dil=w_o_dil, w_out=w_out, w_ff1=w_ff1, w_ff2=w_ff2, b_gate=b_gate)
    mom_m = dict(w_in=m_w_in, w_uq=m_w_uq, w_ukv=m_w_ukv, w_o_mla=m_w_o_mla, w_o_dil=m_w_o_dil, w_out=m_w_out, w_ff1=m_w_ff1, w_ff2=m_w_ff2, b_gate=m_b_gate)
    mom_v = dict(w_in=v_w_in, w_uq=v_w_uq, w_ukv=v_w_ukv, w_o_mla=v_w_o_mla, w_o_dil=v_w_o_dil, w_out=v_w_out, w_ff1=v_w_ff1, w_ff2=v_w_ff2, b_gate=v_b_gate)

    first = ["w_in", "w_uq", "w_ukv"]
    widths = [weights[n].shape[2] for n in first]
    shards = [weights["w_in"][0].T.astype(BF16)] + [_pad_cols(weights[n][0].astype(BF16)) for n in first[1:]]
    g_in, g_uq, g_ukv = _run_comm(_Gather(shards), shards, "all_gather_first_weights")
    g_uq, g_ukv = g_uq[:, :, :widths[1]], g_ukv[:, :, :widths[2]]

    s1, s2, n_in = Q_LORA + KV_LORA, Q_LORA + KV_LORA + ROPE, N_DEV * widths[0]

    def w_in_cols(lo, hi):
        out = []
        while lo < hi:
            d, off = divmod(lo, widths[0])
            take = min(hi - lo, widths[0] - off)
            out.append(g_in[d][off:off + take].T)
            lo += take
        return out

    w_in_ext = jnp.concatenate(w_in_cols(0, s2) + [_rot_cols(jnp.concatenate(w_in_cols(s1, s2), axis=1)),
                                                   jnp.zeros((D_MODEL, LOW_W - s2 - ROPE), BF16)] + w_in_cols(s2, n_in), axis=1)
    uq = _from_col_shards(g_uq).reshape(Q_LORA, N_HEADS, NOPE + ROPE)
    w1 = _pair_slabs(uq[:, :, :NOPE], uq[:, :, NOPE:])
    ukv = _from_col_shards(g_ukv).reshape(KV_LORA, N_HEADS, NOPE + HEAD_V)
    wk = ukv[:, :, :NOPE].reshape(KV_LORA, N_HEADS * NOPE)
    wv = ukv[:, :, NOPE:].reshape(KV_LORA, N_HEADS * HEAD_V)
    cext, sext, cs128 = _rope_tables(seq)
    dil_bias = _dilated_bias_table(seq)
    no_bias = jnp.zeros((1, 8, LANES), F32)

    x2 = x.reshape(tokens, D_MODEL)
    low, gates, qkvd, qp, kp, vm, qn, kvn, xb = _fwd_proj(x2, w_in_ext, w1, wk, wv, g_q_a, g_kv_a, cext, sext, cs128, seq=seq)
    bg = b_gate[0]
    bg_hi = bg.astype(BF16)
    bg_lo = (bg - bg_hi.astype(F32)).astype(BF16)
    later = [weights[n][0].astype(BF16) for n in ("w_o_mla", "w_o_dil", "w_out", "w_ff1", "w_ff2")]
    later.append(_pad_rows(jnp.concatenate([bg_hi, bg_lo], axis=0), 16))
    mla = dict(batch=batch, seq=seq, width=PAIR_W, col0=(0, 0, 0), dilated=False, scale=MLA_SCALE)
    dil = dict(batch=batch, seq=seq, width=LANES, col0=(0, N_PAIRS, 2 * N_PAIRS), dilated=True, scale=DIL_SCALE)
    o_a, lse_a, g_oa, g_ob, g_out, g_ff1, g_ff2, g_bg = _attn_fwd(
        qp, kp, vm, no_bias, name="mla_attention_fwd", comm=_Gather(later), comm_arrays=later, **mla)
    o_b, lse_b = _attn_fwd(qkvd, qkvd, qkvd, dil_bias, name="dilated_attention_fwd", **dil)
    w_oa, w_ob = _from_col_shards(g_oa), _from_col_shards(g_ob)
    w_out_full = g_out.reshape(D_MODEL, D_MODEL)
    w_ff2_full = g_ff2.reshape(D_FF, D_MODEL)
    bg_parts = g_bg.astype(F32)
    b_gate_full = _from_col_shards(bg_parts[:, 0:2] + bg_parts[:, 2:4])
    hb, xhat1, rstd1, y_a, y_b, mix = _fwd_mix(o_a, o_b, gates, x2, b_gate_full, w_oa, w_ob, w_out_full, ln1_g, ln1_b, seq=seq)
    u, dz2, dz2b, stat2 = _fwd_mlp(hb, xhat1, loss_target.reshape(tokens, D_MODEL), g_ff1, w_ff2_full, ln1_g, ln1_b, ln2_g, ln2_b, seq=seq)

    du, dz1, dz1b, stat1 = _bwd_mlp(dz2, dz2b, u, xhat1, rstd1, g_ff1, w_ff2_full, ln1_g, seq=seq)
    dw_ff = [_wgrad(hb, du, "wgrad_ff1", by_shard=True),
             _wgrad(u, dz2b, "wgrad_ff2", square_relu=True).reshape(N_DEV, FF_SHARD, D_MODEL)]
    dgates, dy_a, dy_b, do_a, do_b, stat_g = _bwd_mix(dz1b, gates, y_a, y_b, b_gate_full, w_oa, w_ob, w_out_full, seq=seq)
    dqp, dkp, dvm, r_ff1, r_ff2 = _attn_bwd(qp, kp, vm, o_a, do_a, lse_a, no_bias, name="mla_attention_bwd",
                                            comm=_Scatter(dw_ff), comm_arrays=dw_ff, **mla)
    dw_mid = [_to_col_shards(_wgrad(o_a, dy_a, "wgrad_o_mla")), _to_col_shards(_wgrad(o_b, dy_b, "wgrad_o_dil")),
              _wgrad(mix, dz1b, "wgrad_out").reshape(N_DEV, D_MODEL // N_DEV, D_MODEL),
              _pad_rows(_to_col_shards(stat_g[0:2]).astype(BF16), 16)]
    dq_d, dk_d, dv_d, r_oa, r_ob, r_out, r_bg = _attn_bwd(qkvd, qkvd, qkvd, o_b, do_b, lse_b, dil_bias, name="dilated_attention_bwd",
                                                          comm=_Scatter(dw_mid), comm_arrays=dw_mid, **dil)
    grad_x, dproj, d_a, dkn, stat_r = _bwd_proj(dqp, dkp, dvm, dq_d, dk_d, dv_d, dgates, dz1, low, w_in_ext, w1, wk, wv,
                                                g_q_a, g_kv_a, cext, sext, cs128, seq=seq)

    dw_in_ext = _wgrad(dproj, xb, "wgrad_in")
    dw1 = _wgrad(qn, d_a, "wgrad_uq")
    dwk = _wgrad(kvn, dkn, "wgrad_ukv_k")
    dwv = _wgrad(kvn, dvm, "wgrad_ukv_v")
    dw_kr = dw_in_ext[s1:s2] + _unrot_cols(dw_in_ext[s2:s2 + ROPE].T).T

    def dw_in_cols(lo, hi):
        out = []
        for a, b, piece in ((0, s1, lambda u, v: dw_in_ext[u:v]), (s1, s2, lambda u, v: dw_kr[u - s1:v - s1]),
                            (s2, n_in, lambda u, v: dw_in_ext[u + LOW_W - s2:v + LOW_W - s2])):
            if max(lo, a) < min(hi, b):
                out.append(piece(max(lo, a), min(hi, b)))
        return out

    dw_in = jnp.stack([jnp.concatenate(dw_in_cols(d * widths[0], (d + 1) * widths[0]), axis=0) for d in range(N_DEV)])
    n1, r1 = _split_slabs(dw1)
    dw_uq = jnp.concatenate([n1, r1], axis=2).reshape(Q_LORA, N_HEADS * (NOPE + ROPE))
    dw_ukv = jnp.concatenate([dwk.reshape(KV_LORA, N_HEADS, NOPE), dwv.reshape(KV_LORA, N_HEADS, HEAD_V)], axis=2).reshape(KV_LORA, N_HEADS * (NOPE + HEAD_V))
    last = [dw_in] + [_pad_cols(_to_col_shards(dw)) for dw in (dw_uq, dw_ukv)]
    theirs = _rs_sibling(last, "rs_last_sibling_exchange")
    sums = [_pair_sum(a, b, "rs_last_pair_sum_" + n) for a, b, n in zip(last, theirs, first)]
    partial = jnp.concatenate([stat_r[0:1, :Q_LORA], stat_r[1:2, :KV_LORA], stat1[0:1], stat1[1:2], stat2[0:1], stat2[1:2],
                               stat2[2:3, :LANES]], axis=1)
    partial = _pad_rows(partial.reshape(-1, LANES), 8)
    rest = [s[1] for s in sums]
    got_in, got_uq, got_ukv, every = _run_comm(_Plans([_ChipExchange(rest), _Gather([partial])]), rest + [partial],
                                               "rs_last_chip_exchange")

    upd = {}
    early = ["w_ff1", "w_ff2", "w_out", "w_o_mla", "w_o_dil"]
    items = [(weights[n][0], mom_m[n][0], mom_v[n][0], own, parts) for n, own, parts in
             zip(early, (dw_ff[0], dw_ff[1], dw_mid[2], dw_mid[0], dw_mid[1]), (r_ff1, r_ff2, r_out, r_oa, r_ob))]
    upd.update(zip(early, _adamw(items, "adamw_early_weights")))
    (in_t,) = _adamw([(weights["w_in"][0].T, mom_m["w_in"][0].T, mom_v["w_in"][0].T, sums[0][0], got_in)], "adamw_w_in")
    upd["w_in"] = tuple(a.T for a in in_t)
    for n, w, (own, _), parts in zip(first[1:], widths[1:], sums[1:], (got_uq, got_ukv)):
        (upd[n],) = _adamw([(weights[n][0], mom_m[n][0], mom_v[n][0], own[:, :w], parts[:, :, :w])], "adamw_" + n)
    (bg_upd,) = _adamw([(_pad_rows(b_gate[0], 16), _pad_rows(m_b_gate[0], 16), _pad_rows(v_b_gate[0], 16), dw_mid[3], r_bg)],
                       "adamw_b_gate")
    upd["b_gate"] = tuple(t[0:2] for t in bg_upd)

    small_w = [g_q_a, g_kv_a, ln1_g, ln1_b, ln2_g, ln2_b]
    small_m = [m_g_q_a, m_g_kv_a, m_ln1_g, m_ln1_b, m_ln2_g, m_ln2_b]
    small_v = [v_g_q_a, v_g_kv_a, v_ln1_g, v_ln1_b, v_ln2_g, v_ln2_b]
    small_widths = [a.shape[1] for a in small_w]

    def as_rows(vecs, extra):
        flat = jnp.concatenate(vecs + [jnp.zeros((1, extra), F32)], axis=1)
        return _pad_rows(flat.reshape(-1, LANES), 8)

    g_s, d_s, nm_s, nv_s = _adamw_small(every, as_rows(small_w, LANES), as_rows(small_m, LANES), as_rows(small_v, LANES))

    def split_small(a):
        flat = a.reshape(1, -1)
        out, c0 = [], 0
        for w in small_widths:
            out.append(flat[:, c0:c0 + w])
            c0 += w
        return out, flat[0, c0]

    g_small, loss = split_small(g_s)
    small = [g_small, split_small(d_s)[0], split_small(nm_s)[0], split_small(nv_s)[0]]

    order = ["w_in", "b_gate", "g_q_a", "w_uq", "g_kv_a", "w_ukv", "w_o_mla", "w_o_dil", "w_out", "ln1_g", "ln1_b", "w_ff1", "w_ff2", "ln2_g", "ln2_b"]
    small_names = ["g_q_a", "g_kv_a", "ln1_g", "ln1_b", "ln2_g", "ln2_b"]

    def pick(kind):
        return [small[kind][small_names.index(n)] if n in small_names else upd[n][kind][None] for n in order]

    return (loss, grad_x.reshape(batch, seq, D_MODEL), *pick(0), *pick(1), *pick(2), *pick(3))
```

```python
import functools
import math

import jax
import jax.numpy as jnp
from jax import lax
from jax.experimental import pallas as pl
from jax.experimental.pallas import tpu as pltpu

F32 = jnp.float32
BF16 = jnp.bfloat16
I32 = jnp.int32

D_MODEL = 1024
N_HEADS = 8
NOPE = 64
ROPE = 32
HEAD_V = 64
Q_LORA = 384
KV_LORA = 256
DIL_WIDTH = 512
D_FF = 4096
ROPE_THETA = 10000.0
LN_EPS = 1e-5
RMS_EPS = 1e-6
NEG = -1e30
ALPHA = 2.0 ** 0.25
MLA_SCALE = (NOPE + ROPE) ** -0.5
DIL_SCALE = 64 ** -0.5
ADAM_LR, ADAM_B1, ADAM_B2, ADAM_EPS, ADAM_WD, ADAM_STEP = 0.001, 0.9, 0.999, 1e-08, 0.01, 10

LANES = 128
PAIR_W = 256
N_PAIRS = N_HEADS // 2
LOW_W = 768
IN_EXT = LOW_W + 3 * DIL_WIDTH + 2 * D_MODEL
N_DEV = 8
FF_SHARD = D_FF // N_DEV
FF_STEP = 4
WGRAD_SHARDS = 4
TOKEN_TILE = 256
MIX_TILE = 512
ATTN_TILE = 512
VMEM_LIMIT = 56 << 20

MESH = pl.DeviceIdType.MESH
ANY = pl.BlockSpec(memory_space=pl.ANY)
CHIP_FLIPS = ((0, 0), (0, 1), (1, 0), (1, 1))
PEER_FLIPS = tuple((fx, fy, fc) for fx in (0, 1) for fy in (0, 1) for fc in (0, 1))[1:]


def _cp(*sem):
    return pltpu.CompilerParams(dimension_semantics=sem or None, vmem_limit_bytes=VMEM_LIMIT)


def _full(shape):
    nd = len(shape)
    return pl.BlockSpec(shape, lambda *_: (0,) * nd)


def _rows(tm, width):
    return pl.BlockSpec((tm, width), lambda i, *_: (i, 0))


def _dot(a, b):
    return jnp.dot(a, b, preferred_element_type=F32)


def _dot_nt(a, b):
    return lax.dot_general(a, b, (((1,), (1,)), ((), ())), preferred_element_type=F32)


def _dot_tn(a, b):
    return lax.dot_general(a, b, (((0,), (0,)), ((), ())), preferred_element_type=F32)


def _sigmoid(z):
    return 1.0 / (1.0 + jnp.exp(-z))


def _place():
    return lax.axis_index("x"), lax.axis_index("y"), lax.axis_index("c")


def _flip(v, f):
    return 1 - v if f else v


class _Gather:
    def __init__(self, shards):
        self.n = len(shards)
        self.out_shape = [jax.ShapeDtypeStruct((N_DEV, *s.shape), s.dtype) for s in shards]
        self.scratch = [pltpu.SemaphoreType.DMA((7 * self.n,)), pltpu.SemaphoreType.DMA((7 * self.n,)),
                        pltpu.SemaphoreType.DMA((self.n,))]

    def _copies(self, what, srcs, dsts, send, recv, local):
        x, y, c = _place()
        chips = [(_flip(x, fx), _flip(y, fy)) for fx, fy in CHIP_FLIPS[1:]]
        out = []
        for a in range(self.n):
            def slot(px, py, pc, a=a):
                return dsts[a].at[4 * px + 2 * py + pc]

            def copy(k, block, to, src=None, a=a, slot=slot):
                return pltpu.make_async_remote_copy(
                    src_ref=slot(*block) if src is None else src, dst_ref=slot(*block),
                    send_sem=send.at[7 * a + k], recv_sem=recv.at[7 * a + k], device_id=to, device_id_type=MESH)

            if what == "mine":
                out.append(pltpu.make_async_copy(srcs[a], slot(x, y, c), local.at[a]))
            elif what == "first":
                out.append(copy(0, (x, y, c), (x, y, 1 - c), src=srcs[a]))
                out += [copy(1 + j, (x, y, c), (*chip, c), src=srcs[a]) for j, chip in enumerate(chips)]
            elif what == "landed":
                out += [copy(1 + j, (*chip, c), (x, y, c)) for j, chip in enumerate(chips)]
            elif what == "passed":
                out += [copy(4 + j, (*chip, c), (x, y, 1 - c)) for j, chip in enumerate(chips)]
            else:
                out.append(copy(0, (x, y, 1 - c), (x, y, c)))
                out += [copy(4 + j, (*chip, 1 - c), (x, y, c)) for j, chip in enumerate(chips)]
        return out

    def start(self, *refs):
        for cp in self._copies("first", *refs) + self._copies("mine", *refs):
            cp.start()

    def forward(self, *refs):
        for landed, passed in zip(self._copies("landed", *refs), self._copies("passed", *refs)):
            landed.wait_recv()
            passed.start()

    def finish(self, *refs):
        for cp in self._copies("from_sibling", *refs):
            cp.wait_recv()
        for cp in self._copies("first", *refs) + self._copies("passed", *refs):
            cp.wait_send()
        for cp in self._copies("mine", *refs):
            cp.wait()


class _Scatter:
    def __init__(self, arrays):
        self.n = len(arrays)
        self.out_shape = [jax.ShapeDtypeStruct((7, *a.shape[1:]), a.dtype) for a in arrays]
        self.scratch = [pltpu.SemaphoreType.DMA((7 * self.n,)), pltpu.SemaphoreType.DMA((7 * self.n,))]

    def _copies(self, srcs, dsts, send, recv):
        x, y, c = _place()
        out = []
        for a in range(self.n):
            for k, (fx, fy, fc) in enumerate(PEER_FLIPS):
                px, py, pc = _flip(x, fx), _flip(y, fy), _flip(c, fc)
                out.append(pltpu.make_async_remote_copy(
                    src_ref=srcs[a].at[4 * px + 2 * py + pc], dst_ref=dsts[a].at[k],
                    send_sem=send.at[7 * a + k], recv_sem=recv.at[7 * a + k], device_id=(px, py, pc), device_id_type=MESH))
        return out

    def start(self, *refs):
        for cp in self._copies(*refs):
            cp.start()

    def forward(self, *refs):
        pass

    def finish(self, *refs):
        for cp in self._copies(*refs):
            cp.wait_send()
        for cp in self._copies(*refs):
            cp.wait_recv()


class _ChipExchange:
    def __init__(self, arrays):
        self.n = len(arrays)
        self.out_shape = [jax.ShapeDtypeStruct(a.shape, a.dtype) for a in arrays]
        self.scratch = [pltpu.SemaphoreType.DMA((3 * self.n,)), pltpu.SemaphoreType.DMA((3 * self.n,))]

    def _copies(self, srcs, dsts, send, recv):
        x, y, c = _place()
        return [pltpu.make_async_remote_copy(
            src_ref=srcs[a].at[k], dst_ref=dsts[a].at[k], send_sem=send.at[3 * a + k], recv_sem=recv.at[3 * a + k],
            device_id=(_flip(x, fx), _flip(y, fy), c), device_id_type=MESH)
            for a in range(self.n) for k, (fx, fy) in enumerate(CHIP_FLIPS[1:])]

    def start(self, *refs):
        for cp in self._copies(*refs):
            cp.start()

    def forward(self, *refs):
        pass

    def finish(self, *refs):
        for cp in self._copies(*refs):
            cp.wait_send()
        for cp in self._copies(*refs):
            cp.wait_recv()


class _Plans:
    def __init__(self, plans):
        self.plans = plans
        self.n = sum(p.n for p in plans)
        self.out_shape = [s for p in plans for s in p.out_shape]
        self.scratch = [s for p in plans for s in p.scratch]

    def _each(self, phase, srcs, dsts, *sems):
        i0 = s0 = 0
        for p in self.plans:
            getattr(p, phase)(srcs[i0:i0 + p.n], dsts[i0:i0 + p.n], *sems[s0:s0 + len(p.scratch)])
            i0, s0 = i0 + p.n, s0 + len(p.scratch)

    def start(self, *refs):
        self._each("start", *refs)

    def forward(self, *refs):
        self._each("forward", *refs)

    def finish(self, *refs):
        self._each("finish", *refs)


def _run_comm(comm, arrays, name):
    n = comm.n

    def body(*refs):
        args = (refs[:n], refs[n:2 * n], *refs[2 * n:])
        comm.start(*args)
        comm.forward(*args)
        comm.finish(*args)

    return pl.pallas_call(body, name=name, out_shape=comm.out_shape, in_specs=[ANY] * n, out_specs=[ANY] * n,
                          scratch_shapes=comm.scratch)(*arrays)


def _rs_sibling(arrays, name):
    n = len(arrays)

    def body(*refs):
        srcs, got, (send, recv) = refs[:n], refs[n:2 * n], refs[2 * n:]
        x, y, c = _place()
        copies = []
        for a in range(n):
            for r, (fx, fy) in enumerate(CHIP_FLIPS):
                chip = 2 * _flip(x, fx) + _flip(y, fy)
                copies.append(pltpu.make_async_remote_copy(
                    src_ref=srcs[a].at[2 * chip + 1 - c], dst_ref=got[a].at[r], send_sem=send.at[4 * a + r],
                    recv_sem=recv.at[4 * a + r], device_id=(x, y, 1 - c), device_id_type=MESH))
        for cp in copies:
            cp.start()
        for cp in copies:
            cp.wait_send()
        for cp in copies:
            cp.wait_recv()

    return pl.pallas_call(
        body, name=name, out_shape=[jax.ShapeDtypeStruct((4, *a.shape[1:]), a.dtype) for a in arrays],
        in_specs=[ANY] * n, out_specs=[ANY] * n,
        scratch_shapes=[pltpu.SemaphoreType.DMA((4 * n,)), pltpu.SemaphoreType.DMA((4 * n,))],
    )(*arrays)


def _chip_slots():
    x, y, c = _place()
    return jnp.stack([4 * _flip(x, fx) + 2 * _flip(y, fy) + c for fx, fy in CHIP_FLIPS]).astype(I32)


def _tiles(rows, cols, steps=4):
    if rows % (16 * steps) == 0:
        return steps, (rows // steps, cols), lambda i: (i, 0)
    if cols % (LANES * steps) == 0:
        return steps, (rows, cols // steps), lambda i: (0, i)
    return 1, (rows, cols), lambda i: (0, 0)


def _pair_sum(full, theirs, name):
    _, rows, cols = theirs.shape
    steps, tile, at = _tiles(rows, cols)

    def body(slots_ref, m0_ref, m1_ref, m2_ref, m3_ref, b_ref, own_ref, rest_ref):
        own_ref[...] = m0_ref[...].astype(F32) + b_ref[0].astype(F32)
        for k, m_ref in enumerate((m1_ref, m2_ref, m3_ref)):
            rest_ref[k] = (m_ref[...].astype(F32) + b_ref[k + 1].astype(F32)).astype(BF16)

    def mine(k):
        return pl.BlockSpec((None, *tile), lambda i, slots: (slots[k], *at(i)))

    return pl.pallas_call(
        body, name=name,
        grid_spec=pltpu.PrefetchScalarGridSpec(
            num_scalar_prefetch=1, grid=(steps,),
            in_specs=[mine(0), mine(1), mine(2), mine(3), pl.BlockSpec((4, *tile), lambda i, slots: (0, *at(i)))],
            out_specs=(pl.BlockSpec(tile, lambda i, slots: at(i)), pl.BlockSpec((3, *tile), lambda i, slots: (0, *at(i))))),
        out_shape=(jax.ShapeDtypeStruct((rows, cols), F32), jax.ShapeDtypeStruct((3, rows, cols), BF16)),
        compiler_params=_cp("parallel"),
    )(_chip_slots(), full, full, full, full, theirs)


def _head_lanes(width, h):
    lane = lax.broadcasted_iota(I32, (1, width), 1)
    if width == LANES:
        return (lane >= 64 * h) & (lane < 64 * h + 64)
    nope = (lane >= NOPE * h) & (lane < NOPE * h + NOPE)
    rope = (lane >= 2 * NOPE + ROPE * h) & (lane < 2 * NOPE + ROPE * h + ROPE)
    return nope | rope


def _dilated_bias_table(seq):
    t = min(ATTN_TILE, seq)
    nd = seq // t

    def body(o_ref):
        delta = pl.program_id(0) * t + lax.broadcasted_iota(I32, (t, t), 1) - lax.broadcasted_iota(I32, (t, t), 0)
        mult = ((delta <= 128).astype(I32) + (((delta & 3) == 0) & (delta <= 512)).astype(I32)
                + ((delta & 15) == 0).astype(I32))
        logm = jnp.where(mult == 3, math.log(3.0), jnp.where(mult == 2, math.log(2.0), 0.0))
        valid = (delta >= 0) & (mult > 0)
        dist = delta.astype(F32)
        for h in range(N_HEADS):
            o_ref[h] = jnp.where(valid, logm - 2.0 ** (-(h + 1)) * dist, NEG)

    return pl.pallas_call(
        body, name="dilated_bias_table", grid=(nd,), out_shape=jax.ShapeDtypeStruct((N_HEADS, nd, t, t), F32),
        out_specs=pl.BlockSpec((N_HEADS, None, t, t), lambda d: (0, d, 0, 0)),
        compiler_params=_cp("parallel"),
    )()


def _comm_hooks(comm, refs, n_in, n_out):
    if comm is None:
        return refs[:n_in], refs[n_in:n_in + n_out], refs[n_in + n_out:], None
    n = comm.n
    ins, srcs = refs[:n_in], refs[n_in:n_in + n]
    outs, dsts = refs[n_in + n:n_in + n + n_out], refs[n_in + n + n_out:n_in + 2 * n + n_out]
    rest = refs[n_in + 2 * n + n_out:]
    own = len(rest) - len(comm.scratch)
    return ins, outs, rest[:own], (srcs, dsts, *rest[own:])


def _attn_fwd(q, k, v, bias, *, batch, seq, width, col0, dilated, scale, name, comm=None, comm_arrays=()):
    t = min(ATTN_TILE, seq)
    nq = seq // t
    half = t // 2
    cq, ck, cv = col0
    pre = scale if dilated else 1.0
    steps = batch * N_PAIRS

    def body(*refs):
        (q_ref, k_ref, v_ref, bias_ref), (o_ref, lse_ref), (v_heads,), plan = _comm_hooks(comm, refs, 4, 2)
        step_no = pl.program_id(0) * N_PAIRS + pl.program_id(1)
        if plan:
            pl.when(step_no == 0)(lambda: comm.start(*plan))
            pl.when(step_no == (3 * steps) // 4)(lambda: comm.forward(*plan))
        v_all = v_ref[...].astype(F32)
        for h in (0, 1):
            v_heads[h] = jnp.transpose(jnp.where(_head_lanes(LANES, h), v_all, 0.0)).astype(BF16)
        top = lax.broadcasted_iota(I32, (LANES, t), 0) < HEAD_V
        causal = lax.broadcasted_iota(I32, (t, t), 0) <= lax.broadcasted_iota(I32, (t, t), 1)
        def heads(i):
            q2 = q_ref[pl.ds(pl.multiple_of(i * t, t), t), :]
            q2 = q2 * pre if dilated else q2
            return [jnp.where(_head_lanes(width, h), q2, jnp.zeros_like(q2)) for h in (0, 1)]

        def scores(qh, j):
            kj = k_ref[pl.ds(pl.multiple_of(j * t, t), t), :]
            return tuple(_dot_nt(kj, qh[h]) for h in (0, 1))

        lax.fori_loop(0, nq, functools.partial(query_tile, heads, scores, bias_ref, o_ref, lse_ref, v_heads, top, causal),
                      0)
        if plan:
            pl.when(step_no == steps - 1)(lambda: comm.finish(*plan))

    def query_tile(heads, scores, bias_ref, o_ref, lse_ref, v_heads, top, causal, i, _):
        qs = pl.multiple_of(i * t, t)
        qh = heads(i)

        def step(j, carry, last):
            m0, l0, m1, l1, acc = carry
            s0, s1 = scores(qh, j)
            ks = pl.multiple_of(j * t, t)
            new, alphas, pv = [], [], []

            def online(h, m, l, s, keys, queries):
                s = s[keys, queries]
                if dilated:
                    s = s + (bias_ref[h, 0, keys, queries] if last else bias_ref[h, i - j])
                else:
                    s = s * scale
                    if last:
                        s = jnp.where(causal[keys, queries], s, NEG)
                m, l = m[:, queries], l[:, queries]
                m_new = jnp.maximum(m, jnp.max(s, axis=0, keepdims=True))
                a = jnp.exp(m - m_new)
                p = jnp.exp(s - m_new)
                v_keys = v_heads[h, :, pl.ds(ks, t)]
                return m_new, a * l + jnp.sum(p, axis=0, keepdims=True), a, _dot(v_keys[:, keys], p.astype(BF16))

            for h, (m, l, s) in enumerate(((m0, l0, s0), (m1, l1, s1))):
                if last and half % LANES == 0:
                    parts = [online(h, m, l, s, slice(0, half), slice(0, half)),
                             online(h, m, l, s, slice(0, t), slice(half, t))]
                    m_new, l_new, a, pv_h = (jnp.concatenate(x, axis=1) for x in zip(*parts))
                else:
                    m_new, l_new, a, pv_h = online(h, m, l, s, slice(0, t), slice(0, t))
                new += [m_new, l_new]
                alphas.append(a)
                pv.append(pv_h)
            acc = jnp.where(top, alphas[0], alphas[1]) * acc + pv[0] + pv[1]
            return (*new, acc)

        row = jnp.full((1, t), NEG, F32)
        zero = jnp.zeros((1, t), F32)
        init = (row, zero, row, zero, jnp.zeros((LANES, t), F32))
        m0, l0, m1, l1, acc = step(i, lax.fori_loop(0, i, functools.partial(step, last=False), init), True)
        o_ref[pl.ds(qs, t), :] = jnp.transpose(acc * jnp.where(top, 1.0 / l0, 1.0 / l1)).astype(BF16)
        r = lax.broadcasted_iota(I32, (8, t), 0)
        lse_ref[:, pl.ds(qs, t)] = jnp.where(r == 0, m0 + jnp.log(l0), jnp.where(r == 1, m1 + jnp.log(l1), 0.0))
        return 0

    bias_spec = (pl.BlockSpec((2, nq, t, t), lambda b, p: (p, 0, 0, 0)) if dilated
                 else pl.BlockSpec((None, 8, LANES), lambda b, p: (0, 0, 0)))
    n = comm.n if comm else 0
    return pl.pallas_call(
        body, name=name, grid=(batch, N_PAIRS),
        out_shape=[jax.ShapeDtypeStruct((batch * seq, DIL_WIDTH), BF16), jax.ShapeDtypeStruct((batch * N_PAIRS, 8, seq), F32)]
        + (comm.out_shape if comm else []),
        in_specs=[pl.BlockSpec((seq, width), lambda b, p: (b, cq + p)),
                  pl.BlockSpec((seq, width), lambda b, p: (b, ck + p)),
                  pl.BlockSpec((seq, LANES), lambda b, p: (b, cv + p)),
                  bias_spec] + [ANY] * n,
        out_specs=[pl.BlockSpec((seq, LANES), lambda b, p: (b, p)),
                   pl.BlockSpec((None, 8, seq), lambda b, p: (b * N_PAIRS + p, 0, 0))] + [ANY] * n,
        scratch_shapes=[pltpu.VMEM((2, LANES, seq), BF16)] + (comm.scratch if comm else []),
        compiler_params=_cp("arbitrary", "arbitrary") if comm else _cp("parallel", "parallel"),
    )(q, k, v, bias, *comm_arrays)


def _attn_bwd(q, k, v, o, do, lse, bias, *, batch, seq, width, col0, dilated, scale, name, comm=None, comm_arrays=()):
    t = min(ATTN_TILE, seq)
    nq = seq // t
    half = t // 2
    cq, ck, cv = col0
    pre = scale if dilated else 1.0
    dq_transposed = width == LANES
    steps = batch * N_PAIRS

    def body(*refs):
        ins, (dq_ref, dk_ref, dv_ref), (dq_acc, dk_acc, dv_acc, rowdot, q_heads, do_heads), plan = _comm_hooks(comm, refs, 7, 3)
        q_ref, k_ref, v_ref, o_ref, do_ref, lse_ref, bias_ref = ins
        step_no = pl.program_id(0) * N_PAIRS + pl.program_id(1)
        if plan:
            pl.when(step_no == 0)(lambda: comm.start(*plan))
        wlane = [_head_lanes(width, h) for h in (0, 1)]
        vlane = [_head_lanes(LANES, h) for h in (0, 1)]
        causal = lax.broadcasted_iota(I32, (t, t), 0) <= lax.broadcasted_iota(I32, (t, t), 1)
        q_all = q_ref[...] * pre if dilated else q_ref[...]
        for h in (0, 1):
            q_heads[h] = jnp.where(wlane[h], q_all, jnp.zeros_like(q_all))
            do_heads[h] = jnp.where(vlane[h], do_ref[...], jnp.zeros_like(do_ref[...]))
        prod = jnp.transpose(do_ref[...].astype(F32) * o_ref[...].astype(F32))
        rowdot[0:1, :] = jnp.sum(prod[0:HEAD_V], axis=0, keepdims=True)
        rowdot[1:2, :] = jnp.sum(prod[HEAD_V:], axis=0, keepdims=True)
        dq_acc[...] = jnp.zeros_like(dq_acc)

        def k_tile(j, _):
            ks = pl.multiple_of(j * t, t)
            kj = k_ref[pl.ds(ks, t), :]
            vj = v_ref[pl.ds(ks, t), :]
            kh = [jnp.where(wlane[h], kj, jnp.zeros_like(kj)) for h in (0, 1)]
            if dq_transposed:
                kh = [jnp.transpose(kh[h].astype(F32)).astype(BF16) for h in (0, 1)]
            dk_acc[...] = jnp.zeros_like(dk_acc)
            dv_acc[...] = jnp.zeros_like(dv_acc)

            def operands(i):
                qs = pl.multiple_of(i * t, t)
                return [q_heads[h, pl.ds(qs, t), :] for h in (0, 1)], [do_heads[h, pl.ds(qs, t), :] for h in (0, 1)]

            def q_tile(n, _, last):
                i = nq - 1 - n
                qs = pl.multiple_of(i * t, t)
                qih, doih = operands(i)

                def block(keys, queries):
                    count = queries.stop - queries.start
                    at = pl.ds(qs + queries.start, count)
                    ss = [_dot_nt(kj[keys], qih[h][queries]) for h in (0, 1)]
                    dps = [_dot_nt(vj[keys], doih[h][queries]) for h in (0, 1)]
                    dq_b = jnp.zeros((width, count) if dq_transposed else (count, width), F32)
                    for h, (s, dp) in enumerate(zip(ss, dps)):
                        if dilated:
                            s = s + (bias_ref[h, 0, keys, queries] if last else bias_ref[h, i - j])
                        else:
                            s = s * scale
                            if last:
                                s = jnp.where(causal[keys, queries], s, NEG)
                        p = jnp.exp(s - lse_ref[h:h + 1, at])
                        ds = p * (dp - rowdot[h:h + 1, at])
                        ds = (ds if dilated else ds * scale).astype(BF16)
                        dv_acc[keys, :] += _dot(p.astype(BF16), doih[h][queries])
                        dk_acc[keys, :] += _dot(ds, qih[h][queries])
                        dq_b = dq_b + (_dot(kh[h][:, keys], ds) if dq_transposed else _dot_tn(ds, kh[h][keys]))
                    if dq_transposed:
                        dq_acc[:, at] += dq_b
                    else:
                        dq_acc[at, :] += dq_b

                if last and half % LANES == 0:
                    block(slice(0, half), slice(0, half))
                    block(slice(0, t), slice(half, t))
                else:
                    block(slice(0, t), slice(0, t))
                return 0

            q_tile(nq - 1 - j, lax.fori_loop(0, nq - 1 - j, functools.partial(q_tile, last=False), 0), True)
            dk_ref[pl.ds(ks, t), :] = dk_acc[...].astype(BF16)
            dv_ref[pl.ds(ks, t), :] = dv_acc[...].astype(BF16)
            return 0

        lax.fori_loop(0, nq, k_tile, 0)
        dq_ref[...] = ((jnp.transpose(dq_acc[...]) if dq_transposed else dq_acc[...]) * pre).astype(BF16)
        if plan:
            pl.when(step_no == steps - 1)(lambda: comm.finish(*plan))

    tokens = batch * seq
    bias_spec = (pl.BlockSpec((2, nq, t, t), lambda b, p: (p, 0, 0, 0)) if dilated
                 else pl.BlockSpec((None, 8, LANES), lambda b, p: (0, 0, 0)))
    n = comm.n if comm else 0
    return pl.pallas_call(
        body, name=name, grid=(batch, N_PAIRS),
        out_shape=[jax.ShapeDtypeStruct((tokens, N_PAIRS * width), BF16), jax.ShapeDtypeStruct((tokens, N_PAIRS * width), BF16),
                   jax.ShapeDtypeStruct((tokens, DIL_WIDTH), BF16)] + (comm.out_shape if comm else []),
        in_specs=[pl.BlockSpec((seq, width), lambda b, p: (b, cq + p)),
                  pl.BlockSpec((seq, width), lambda b, p: (b, ck + p)),
                  pl.BlockSpec((seq, LANES), lambda b, p: (b, cv + p)),
                  pl.BlockSpec((seq, LANES), lambda b, p: (b, p)),
                  pl.BlockSpec((seq, LANES), lambda b, p: (b, p)),
                  pl.BlockSpec((None, 8, seq), lambda b, p: (b * N_PAIRS + p, 0, 0)),
                  bias_spec] + [ANY] * n,
        out_specs=[pl.BlockSpec((seq, width), lambda b, p: (b, p)),
                   pl.BlockSpec((seq, width), lambda b, p: (b, p)),
                   pl.BlockSpec((seq, LANES), lambda b, p: (b, p))] + [ANY] * n,
        scratch_shapes=[pltpu.VMEM((width, seq) if dq_transposed else (seq, width), F32),
                        pltpu.VMEM((t, width), F32), pltpu.VMEM((t, LANES), F32),
                        pltpu.VMEM((8, seq), F32), pltpu.VMEM((2, seq, width), BF16), pltpu.VMEM((2, seq, LANES), BF16)]
        + (comm.scratch if comm else []),
        compiler_params=_cp("arbitrary", "arbitrary") if comm else _cp("parallel", "parallel"),
    )(q, k, v, o, do, lse, bias, *comm_arrays)


def _rms(xf, g):
    r = lax.rsqrt(jnp.mean(xf * xf, axis=1, keepdims=True) + RMS_EPS)
    return xf * r * g, r


def _rms_bwd(dy, xf, r, g):
    gy = dy * g
    dx = r * gy - xf * (r * r * r) * jnp.mean(gy * xf, axis=1, keepdims=True)
    return dx, dy * xf * r


def _ln_bwd(dy, xhat, rstd, g):
    dxh = dy * g
    return rstd * (dxh - jnp.mean(dxh, axis=1, keepdims=True) - xhat * jnp.mean(dxh * xhat, axis=1, keepdims=True))


def _rope_slabs(q, cos, sin, transpose):
    first_half = (lax.broadcasted_iota(I32, (1, LANES), 1) % ROPE) < ROPE // 2
    out = []
    for p in range(N_PAIRS):
        blk = q[:, p * PAIR_W + LANES:(p + 1) * PAIR_W]
        y = blk * sin if transpose else blk
        up, down = pltpu.roll(y, LANES - ROPE // 2, 1), pltpu.roll(y, ROPE // 2, 1)
        rot = jnp.where(first_half, up, -down) if transpose else jnp.where(first_half, -up, down) * sin
        out += [q[:, p * PAIR_W:p * PAIR_W + LANES], blk * cos + rot]
    return jnp.concatenate(out, axis=1)


def _fwd_proj(x, w_in_ext, w1, wk, wv, g_q, g_kv, cext, sext, cs128, *, seq):
    tokens = x.shape[0]
    tm = min(MIX_TILE, seq)
    ns = seq // tm

    def body(x_ref, win_ref, w1_ref, wk_ref, wv_ref, gq_ref, gkv_ref, c_ref, s_ref, cs_ref,
             low_ref, gates_ref, qkvd_ref, qp_ref, kp_ref, vm_ref, qn_ref, kvn_ref, xb_ref):
        xt = x_ref[...].astype(BF16)
        xb_ref[...] = xt
        low = _dot(xt, win_ref[:, 0:LOW_W])
        low_ref[...] = low
        qkvd_ref[...] = _dot(xt, win_ref[:, LOW_W:LOW_W + 3 * DIL_WIDTH]).astype(BF16)
        gates_ref[...] = _dot(xt, win_ref[:, LOW_W + 3 * DIL_WIDTH:]).astype(BF16)
        qn = _rms(low[:, 0:Q_LORA], gq_ref[...])[0].astype(BF16)
        kvn = _rms(low[:, Q_LORA:Q_LORA + KV_LORA], gkv_ref[...])[0].astype(BF16)
        qn_ref[...] = qn
        kvn_ref[...] = kvn
        qp_ref[...] = _rope_slabs(_dot(qn, w1_ref[...]), c_ref[...], s_ref[...], False).astype(BF16)
        kr = low[:, Q_LORA + KV_LORA:] * cs_ref[...]
        kr = kr + pltpu.roll(kr, LANES - ROPE, 1)
        lane = lax.broadcasted_iota(I32, kr.shape, 1)
        kr = jnp.where(lane < ROPE, kr, 0.0)
        kr = (kr + pltpu.roll(kr, ROPE, 1)).astype(BF16)
        kn = _dot(kvn, wk_ref[...]).astype(BF16)
        kp_ref[...] = jnp.concatenate([blk for p in range(N_PAIRS) for blk in (kn[:, p * LANES:(p + 1) * LANES], kr)], axis=1)
        vm_ref[...] = _dot(kvn, wv_ref[...]).astype(BF16)

    n_gates = 2 * D_MODEL
    outs = [(LOW_W, F32), (n_gates, BF16), (3 * DIL_WIDTH, BF16), (N_PAIRS * PAIR_W, BF16), (N_PAIRS * PAIR_W, BF16),
            (DIL_WIDTH, BF16), (Q_LORA, BF16), (KV_LORA, BF16), (D_MODEL, BF16)]
    return pl.pallas_call(
        body, name="fwd_proj", grid=(tokens // tm,),
        out_shape=tuple(jax.ShapeDtypeStruct((tokens, w), dt) for w, dt in outs),
        in_specs=[_rows(tm, D_MODEL), _full(w_in_ext.shape), _full(w1.shape), _full(wk.shape),
                  _full(wv.shape), _full(g_q.shape), _full(g_kv.shape),
                  pl.BlockSpec((tm, LANES), lambda i: (i % ns, 1)),
                  pl.BlockSpec((tm, LANES), lambda i: (i % ns, 1)),
                  pl.BlockSpec((tm, LANES), lambda i: (i % ns, 0))],
        out_specs=tuple(_rows(tm, w) for w, _ in outs),
        compiler_params=_cp("parallel"),
    )(x, w_in_ext, w1, wk, wv, g_q, g_kv, cext, sext, cs128)


def _fwd_mix(o_a, o_b, gates, x, b_gate, w_oa, w_ob, w_out, ln_g, ln_b, *, seq):
    tokens = x.shape[0]
    tm = min(MIX_TILE, seq)

    def body(oa_ref, ob_ref, gt_ref, x_ref, bg_ref, woa_ref, wob_ref, wout_ref, g_ref, b_ref,
             hb_ref, xhat_ref, rstd_ref, ya_ref, yb_ref, mix_ref):
        ya = _dot(oa_ref[...], woa_ref[...])
        yb = _dot(ob_ref[...], wob_ref[...])
        g0 = _sigmoid(gt_ref[:, 0:D_MODEL].astype(F32) + bg_ref[0:1, :])
        g1 = _sigmoid(gt_ref[:, D_MODEL:].astype(F32) + bg_ref[1:2, :])
        mix = (g0 * ya + g1 * yb).astype(BF16)
        z = ALPHA * x_ref[...] + _dot(mix, wout_ref[...])
        zc = z - jnp.mean(z, axis=1, keepdims=True)
        rstd = lax.rsqrt(jnp.mean(zc * zc, axis=1, keepdims=True) + LN_EPS)
        xhat = zc * rstd
        hb_ref[...] = (xhat * g_ref[...] + b_ref[...]).astype(BF16)
        xhat_ref[...] = xhat
        rstd_ref[...] = jnp.broadcast_to(rstd, (tm, LANES))
        ya_ref[...] = ya.astype(BF16)
        yb_ref[...] = yb.astype(BF16)
        mix_ref[...] = mix

    outs = [(D_MODEL, BF16), (D_MODEL, F32), (LANES, F32), (D_MODEL, BF16), (D_MODEL, BF16), (D_MODEL, BF16)]
    return pl.pallas_call(
        body, name="fwd_mix", grid=(tokens // tm,),
        out_shape=tuple(jax.ShapeDtypeStruct((tokens, w), dt) for w, dt in outs),
        in_specs=[_rows(tm, DIL_WIDTH), _rows(tm, DIL_WIDTH), _rows(tm, 2 * D_MODEL), _rows(tm, D_MODEL),
                  _full(b_gate.shape), _full(w_oa.shape), _full(w_ob.shape), _full(w_out.shape),
                  _full(ln_g.shape), _full(ln_b.shape)],
        out_specs=tuple(_rows(tm, w) for w, _ in outs),
        compiler_params=_cp("parallel"),
    )(o_a, o_b, gates, x, b_gate, w_oa, w_ob, w_out, ln_g, ln_b)


def _fwd_mlp(hb, xhat1, target, w_ff1, w_ff2, ln1_g, ln1_b, ln_g, ln_b, *, seq):
    tokens = hb.shape[0]
    tm = min(2 * TOKEN_TILE, seq)
    tf = FF_SHARD
    nf = N_DEV // FF_STEP

    def body(hb_ref, xh_ref, tg_ref, w1_ref, w2_ref, g1_ref, b1_ref, g_ref, b_ref, u_ref, dz_ref, dzb_ref, stat_ref, acc):
        i, j = pl.program_id(0), pl.program_id(1)

        @pl.when((i == 0) & (j == 0))
        def _():
            stat_ref[...] = jnp.zeros_like(stat_ref)

        @pl.when(j == 0)
        def _():
            acc[...] = jnp.zeros_like(acc)

        acts = []
        for s in range(FF_STEP):
            u = _dot(hb_ref[...], w1_ref[s])
            u_ref[:, s * tf:(s + 1) * tf] = u.astype(BF16)
            acts.append(jnp.square(jnp.maximum(u, 0.0)).astype(BF16))
        acc[...] += _dot(jnp.concatenate(acts, axis=1), w2_ref[...])

        @pl.when(j == nf - 1)
        def _():
            z = ALPHA * (xh_ref[...] * g1_ref[...] + b1_ref[...]) + acc[...]
            zc = z - jnp.mean(z, axis=1, keepdims=True)
            rstd = lax.rsqrt(jnp.mean(zc * zc, axis=1, keepdims=True) + LN_EPS)
            xhat = zc * rstd
            err = xhat * g_ref[...] + b_ref[...] - tg_ref[...]
            dy = err * (1.0 / D_MODEL)
            dz = _ln_bwd(dy, xhat, rstd, g_ref[...])
            dz_ref[...] = dz
            dzb_ref[...] = dz.astype(BF16)
            stat_ref[0:1, :] += jnp.sum(dy * xhat, axis=0, keepdims=True)
            stat_ref[1:2, :] += jnp.sum(dy, axis=0, keepdims=True)
            stat_ref[2:3, :] += jnp.sum(jnp.sum(err * err, axis=1, keepdims=True), axis=0, keepdims=True) * (0.5 / D_MODEL)

    return pl.pallas_call(
        body, name="fwd_mlp", grid=(tokens // tm, nf),
        out_shape=(jax.ShapeDtypeStruct((tokens, D_FF), BF16), jax.ShapeDtypeStruct((tokens, D_MODEL), F32),
                   jax.ShapeDtypeStruct((tokens, D_MODEL), BF16), jax.ShapeDtypeStruct((8, D_MODEL), F32)),
        in_specs=[_rows(tm, D_MODEL), _rows(tm, D_MODEL), _rows(tm, D_MODEL),
                  pl.BlockSpec((FF_STEP, D_MODEL, tf), lambda i, j: (j, 0, 0)),
                  pl.BlockSpec((FF_STEP * tf, D_MODEL), lambda i, j: (j, 0)),
                  _full(ln1_g.shape), _full(ln1_b.shape), _full(ln_g.shape), _full(ln_b.shape)],
        out_specs=(pl.BlockSpec((tm, FF_STEP * tf), lambda i, j: (i, j)), _rows(tm, D_MODEL), _rows(tm, D_MODEL),
                   _full((8, D_MODEL))),
        scratch_shapes=[pltpu.VMEM((tm, D_MODEL), F32)],
        compiler_params=_cp("arbitrary", "arbitrary"),
    )(hb, xhat1, target, w_ff1, w_ff2, ln1_g, ln1_b, ln_g, ln_b)


def _bwd_mlp(dz2, dz2b, u, xhat1, rstd1, w_ff1, w_ff2, ln_g, *, seq):
    tokens = dz2.shape[0]
    tm = min(2 * TOKEN_TILE, seq)
    tf = FF_SHARD
    nf = N_DEV // FF_STEP

    def body(dz_ref, dzb_ref, u_ref, xh_ref, rs_ref, w1_ref, w2_ref, g_ref, du_ref, dz1_ref, dz1b_ref, stat_ref, acc):
        i, j = pl.program_id(0), pl.program_id(1)

        @pl.when((i == 0) & (j == 0))
        def _():
            stat_ref[...] = jnp.zeros_like(stat_ref)

        @pl.when(j == 0)
        def _():
            acc[...] = jnp.zeros_like(acc)

        da = _dot_nt(dzb_ref[...], w2_ref[...])
        du = (da * (2.0 * jnp.maximum(u_ref[...].astype(F32), 0.0))).astype(BF16)
        du_ref[...] = du
        part = _dot_nt(du[:, 0:tf], w1_ref[0])
        for s in range(1, FF_STEP):
            part = part + _dot_nt(du[:, s * tf:(s + 1) * tf], w1_ref[s])
        acc[...] += part

        @pl.when(j == nf - 1)
        def _():
            dh = ALPHA * dz_ref[...] + acc[...]
            xhat = xh_ref[...]
            dz1 = _ln_bwd(dh, xhat, rs_ref[:, 0:1], g_ref[...])
            dz1_ref[...] = dz1
            dz1b_ref[...] = dz1.astype(BF16)
            stat_ref[0:1, :] += jnp.sum(dh * xhat, axis=0, keepdims=True)
            stat_ref[1:2, :] += jnp.sum(dh, axis=0, keepdims=True)

    return pl.pallas_call(
        body, name="bwd_mlp", grid=(tokens // tm, nf),
        out_shape=(jax.ShapeDtypeStruct((tokens, D_FF), BF16), jax.ShapeDtypeStruct((tokens, D_MODEL), F32),
                   jax.ShapeDtypeStruct((tokens, D_MODEL), BF16), jax.ShapeDtypeStruct((8, D_MODEL), F32)),
        in_specs=[_rows(tm, D_MODEL), _rows(tm, D_MODEL), pl.BlockSpec((tm, FF_STEP * tf), lambda i, j: (i, j)),
                  _rows(tm, D_MODEL), _rows(tm, LANES),
                  pl.BlockSpec((FF_STEP, D_MODEL, tf), lambda i, j: (j, 0, 0)),
                  pl.BlockSpec((FF_STEP * tf, D_MODEL), lambda i, j: (j, 0)),
                  _full(ln_g.shape)],
        out_specs=(pl.BlockSpec((tm, FF_STEP * tf), lambda i, j: (i, j)), _rows(tm, D_MODEL), _rows(tm, D_MODEL),
                   _full((8, D_MODEL))),
        scratch_shapes=[pltpu.VMEM((tm, D_MODEL), F32)],
        compiler_params=_cp("arbitrary", "arbitrary"),
    )(dz2, dz2b, u, xhat1, rstd1, w_ff1, w_ff2, ln_g)


def _bwd_mix(dz1b, gates, y_a, y_b, b_gate, w_oa, w_ob, w_out, *, seq):
    tokens = dz1b.shape[0]
    tm = min(MIX_TILE, seq)

    def body(dz_ref, gt_ref, ya_ref, yb_ref, bg_ref, woa_ref, wob_ref, wout_ref,
             dgt_ref, dya_ref, dyb_ref, doa_ref, dob_ref, stat_ref):
        @pl.when(pl.program_id(0) == 0)
        def _():
            stat_ref[...] = jnp.zeros_like(stat_ref)

        dmix = _dot_nt(dz_ref[...], wout_ref[...])
        for k, (y_ref, w_ref, dy_ref, do_ref) in enumerate(((ya_ref, woa_ref, dya_ref, doa_ref), (yb_ref, wob_ref, dyb_ref, dob_ref))):
            g = _sigmoid(gt_ref[:, k * D_MODEL:(k + 1) * D_MODEL].astype(F32) + bg_ref[k:k + 1, :])
            dgate = dmix * y_ref[...].astype(F32) * g * (1.0 - g)
            dgt_ref[:, k * D_MODEL:(k + 1) * D_MODEL] = dgate.astype(BF16)
            stat_ref[k:k + 1, :] += jnp.sum(dgate, axis=0, keepdims=True)
            dy = (dmix * g).astype(BF16)
            dy_ref[...] = dy
            do_ref[...] = _dot_nt(dy, w_ref[...]).astype(BF16)

    outs = [(2 * D_MODEL, BF16), (D_MODEL, BF16), (D_MODEL, BF16), (DIL_WIDTH, BF16), (DIL_WIDTH, BF16)]
    return pl.pallas_call(
        body, name="bwd_mix", grid=(tokens // tm,),
        out_shape=tuple(jax.ShapeDtypeStruct((tokens, w), dt) for w, dt in outs) + (jax.ShapeDtypeStruct((8, D_MODEL), F32),),
        in_specs=[_rows(tm, D_MODEL), _rows(tm, 2 * D_MODEL), _rows(tm, D_MODEL), _rows(tm, D_MODEL),
                  _full(b_gate.shape), _full(w_oa.shape), _full(w_ob.shape), _full(w_out.shape)],
        out_specs=tuple(_rows(tm, w) for w, _ in outs) + (_full((8, D_MODEL)),),
        compiler_params=_cp("arbitrary"),
    )(dz1b, gates, y_a, y_b, b_gate, w_oa, w_ob, w_out)


def _bwd_proj(dqp, dkp, dvm, dq_d, dk_d, dv_d, dgates, dz1, low, w_in_ext, w1, wk, wv, g_q, g_kv, cext, sext, cs128, *, seq):
    tokens = dz1.shape[0]
    tm = min(TOKEN_TILE, seq)
    ns = seq // tm

    def body(dqp_ref, dkp_ref, dvm_ref, dqd_ref, dkd_ref, dvd_ref, dgt_ref, dz_ref, low_ref, win_ref, w1_ref, wk_ref,
             wv_ref, gq_ref, gkv_ref, c_ref, s_ref, cs_ref, dx_ref, dproj_ref, da_ref, dkn_ref, stat_ref):
        @pl.when(pl.program_id(0) == 0)
        def _():
            stat_ref[...] = jnp.zeros_like(stat_ref)

        low = low_ref[...]
        d_a = _rope_slabs(dqp_ref[...].astype(F32), c_ref[...], s_ref[...], True).astype(BF16)
        da_ref[...] = d_a
        q_a = low[:, 0:Q_LORA]
        _, rq = _rms(q_a, gq_ref[...])
        dq_a, gq_terms = _rms_bwd(_dot_nt(d_a, w1_ref[...]), q_a, rq, gq_ref[...])
        kv_a = low[:, Q_LORA:Q_LORA + KV_LORA]
        _, rkv = _rms(kv_a, gkv_ref[...])
        dkn = jnp.concatenate([dkp_ref[:, p * PAIR_W:p * PAIR_W + LANES] for p in range(N_PAIRS)], axis=1)
        dkn_ref[...] = dkn
        dkv_a, gkv_terms = _rms_bwd(_dot_nt(dkn, wk_ref[...]) + _dot_nt(dvm_ref[...], wv_ref[...]), kv_a, rkv, gkv_ref[...])
        dkr = sum(dkp_ref[:, p * PAIR_W + LANES:(p + 1) * PAIR_W].astype(F32) for p in range(N_PAIRS))
        dkr = dkr + pltpu.roll(dkr, LANES - ROPE, 1)
        dkr = jnp.where(lax.broadcasted_iota(I32, dkr.shape, 1) < ROPE, dkr, 0.0)
        dkr = (dkr + pltpu.roll(dkr, ROPE, 1)) * cs_ref[...]
        stat_ref[0:1, 0:Q_LORA] += jnp.sum(gq_terms, axis=0, keepdims=True)
        stat_ref[1:2, 0:KV_LORA] += jnp.sum(gkv_terms, axis=0, keepdims=True)
        dproj_ref[:, 0:Q_LORA] = dq_a.astype(BF16)
        dproj_ref[:, Q_LORA:Q_LORA + KV_LORA] = dkv_a.astype(BF16)
        dproj_ref[:, Q_LORA + KV_LORA:LOW_W] = dkr.astype(BF16)
        dproj_ref[:, LOW_W:LOW_W + DIL_WIDTH] = dqd_ref[...]
        dproj_ref[:, LOW_W + DIL_WIDTH:LOW_W + 2 * DIL_WIDTH] = dkd_ref[...]
        dproj_ref[:, LOW_W + 2 * DIL_WIDTH:LOW_W + 3 * DIL_WIDTH] = dvd_ref[...]
        dproj_ref[:, LOW_W + 3 * DIL_WIDTH:] = dgt_ref[...]
        dx_ref[...] = ALPHA * dz_ref[...] + _dot_nt(dproj_ref[...], win_ref[...])

    wide = N_PAIRS * PAIR_W
    return pl.pallas_call(
        body, name="bwd_proj", grid=(tokens // tm,),
        out_shape=(jax.ShapeDtypeStruct((tokens, D_MODEL), F32), jax.ShapeDtypeStruct((tokens, IN_EXT), BF16),
                   jax.ShapeDtypeStruct((tokens, wide), BF16), jax.ShapeDtypeStruct((tokens, N_HEADS * NOPE), BF16),
                   jax.ShapeDtypeStruct((8, D_MODEL), F32)),
        in_specs=[_rows(tm, wide), _rows(tm, wide), _rows(tm, DIL_WIDTH), _rows(tm, DIL_WIDTH), _rows(tm, DIL_WIDTH),
                  _rows(tm, DIL_WIDTH), _rows(tm, 2 * D_MODEL),
                  _rows(tm, D_MODEL), _rows(tm, LOW_W), _full(w_in_ext.shape), _full(w1.shape),
                  _full(wk.shape), _full(wv.shape), _full(g_q.shape), _full(g_kv.shape),
                  pl.BlockSpec((tm, LANES), lambda i: (i % ns, 1)), pl.BlockSpec((tm, LANES), lambda i: (i % ns, 1)),
                  pl.BlockSpec((tm, LANES), lambda i: (i % ns, 0))],
        out_specs=(_rows(tm, D_MODEL), _rows(tm, IN_EXT), _rows(tm, wide), _rows(tm, N_HEADS * NOPE), _full((8, D_MODEL))),
        compiler_params=_cp("arbitrary"),
    )(dqp, dkp, dvm, dq_d, dk_d, dv_d, dgates, dz1, low, w_in_ext, w1, wk, wv, g_q, g_kv, cext, sext, cs128)


def _wgrad(a, b, name, square_relu=False, by_shard=False):
    tokens, ka = a.shape
    n = b.shape[1]
    if ka <= 512 or ka % 512 == 0:
        tka = min(ka, 512)
    else:
        tka = max(w for w in range(LANES, min(ka, 2304) + 1, LANES) if ka % w == 0)
    shard = n // N_DEV
    tn = WGRAD_SHARDS * shard if by_shard else max(w for w in range(LANES, min(n, 2304) + 1, LANES) if n % w == 0)
    tt = min(tokens, 2048 if tka <= 512 else 1024)
    nt = tokens // tt

    def body(a_ref, b_ref, o_ref, acc):
        kt = pl.program_id(2)

        @pl.when(kt == 0)
        def _():
            acc[...] = jnp.zeros_like(acc)

        at = a_ref[...]
        if square_relu:
            at = jnp.square(jnp.maximum(at.astype(F32), 0.0)).astype(BF16)
        acc[...] += _dot_tn(at, b_ref[...])

        @pl.when(kt == nt - 1)
        def _():
            if by_shard:
                for s in range(WGRAD_SHARDS):
                    o_ref[s] = acc[:, s * shard:(s + 1) * shard].astype(BF16)
            else:
                o_ref[...] = acc[...].astype(BF16)

    if by_shard:
        out_shape, out_spec = (N_DEV, ka, shard), pl.BlockSpec((WGRAD_SHARDS, tka, shard), lambda i, j, k: (j, i, 0))
    else:
        out_shape, out_spec = (ka, n), pl.BlockSpec((tka, tn), lambda i, j, k: (i, j))
    return pl.pallas_call(
        body, name=name, grid=(ka // tka, n // tn, nt), out_shape=jax.ShapeDtypeStruct(out_shape, BF16),
        in_specs=[pl.BlockSpec((tt, tka), lambda i, j, k: (k, i)), pl.BlockSpec((tt, tn), lambda i, j, k: (k, j))],
        out_specs=out_spec,
        scratch_shapes=[pltpu.VMEM((tka, tn), F32)],
        compiler_params=_cp("parallel", "parallel", "arbitrary"),
    )(a, b)


def _adam_math(w, g, m, v):
    m = ADAM_B1 * m + (1.0 - ADAM_B1) * g
    v = ADAM_B2 * v + (1.0 - ADAM_B2) * jnp.square(g)
    m_hat = m / (1.0 - ADAM_B1 ** ADAM_STEP)
    v_hat = v / (1.0 - ADAM_B2 ** ADAM_STEP)
    return -ADAM_LR * (m_hat / (jnp.sqrt(v_hat) + ADAM_EPS) + ADAM_WD * w), m, v


def _adamw(items, name):
    steps = min(_tiles(*w.shape)[0] for w, *_ in items)
    n_items = len(items)

    def body(slot_ref, *refs):
        ins, outs = refs[:5 * n_items], refs[5 * n_items:]
        for k, (_, _, _, _, parts) in enumerate(items):
            w_ref, m_ref, v_ref, own_ref, p_ref = ins[5 * k:5 * k + 5]
            g_ref, d_ref, nm_ref, nv_ref = outs[4 * k:4 * k + 4]
            g = own_ref[...].astype(F32)
            for d in range(parts.shape[0]):
                g = g + p_ref[d].astype(F32)
            g_ref[...] = g
            d_ref[...], nm_ref[...], nv_ref[...] = _adam_math(w_ref[...], g, m_ref[...], v_ref[...])

    x, y, c = _place()
    in_specs, out_specs, out_shape, args = [], [], [], []
    for w, m, v, own, parts in items:
        rows, cols = w.shape
        _, tile, at = _tiles(rows, cols, steps)
        blk = pl.BlockSpec(tile, lambda i, slot, at=at: at(i))
        own_blk = blk if own.ndim == 2 else pl.BlockSpec((None, *tile), lambda i, slot, at=at: (slot[0], *at(i)))
        in_specs += [blk, blk, blk, own_blk, pl.BlockSpec((parts.shape[0], *tile), lambda i, slot, at=at: (0, *at(i)))]
        out_specs += [blk] * 4
        out_shape += [jax.ShapeDtypeStruct((rows, cols), F32)] * 4
        args += [w, m, v, own, parts]
    out = pl.pallas_call(
        body, name=name,
        grid_spec=pltpu.PrefetchScalarGridSpec(num_scalar_prefetch=1, grid=(steps,), in_specs=in_specs, out_specs=out_specs),
        out_shape=out_shape, compiler_params=_cp("parallel"),
    )(jnp.reshape(4 * x + 2 * y + c, (1,)).astype(I32), *args)
    return [tuple(out[4 * k:4 * k + 4]) for k in range(n_items)]


def _adamw_small(parts, w, m, v):
    _, rows, cols = parts.shape

    def body(p_ref, w_ref, m_ref, v_ref, g_ref, d_ref, nm_ref, nv_ref):
        g = p_ref[0]
        for d in range(1, N_DEV):
            g = g + p_ref[d]
        g_ref[...] = g
        d_ref[...], nm_ref[...], nv_ref[...] = _adam_math(w_ref[...], g, m_ref[...], v_ref[...])

    return pl.pallas_call(
        body, name="adamw_replicated", out_shape=(jax.ShapeDtypeStruct((rows, cols), F32),) * 4,
        in_specs=[_full(parts.shape)] + [_full((rows, cols))] * 3, out_specs=(_full((rows, cols)),) * 4, grid=(1,),
        compiler_params=_cp("arbitrary"),
    )(parts, w, m, v)


def _pad_rows(a2d, mult):
    pad = (-a2d.shape[-2]) % mult
    return jnp.pad(a2d, [(0, 0)] * (a2d.ndim - 2) + [(0, pad), (0, 0)]) if pad else a2d


def _pad_cols(a):
    pad = (-a.shape[-1]) % LANES
    return jnp.pad(a, [(0, 0)] * (a.ndim - 1) + [(0, pad)]) if pad else a


def _rot_cols(w):
    half = ROPE // 2
    return jnp.concatenate([-w[..., half:], w[..., :half]], axis=-1)


def _unrot_cols(dw):
    half = ROPE // 2
    return jnp.concatenate([dw[..., half:], -dw[..., :half]], axis=-1)


def _from_col_shards(stacked):
    return stacked.transpose(1, 0, 2).reshape(stacked.shape[1], -1)


def _to_col_shards(full):
    r = full.shape[0]
    return full.reshape(r, N_DEV, -1).transpose(1, 0, 2)


def _rope_tables(seq):
    half = ROPE // 2
    inv = jnp.power(ROPE_THETA, -jnp.arange(half, dtype=F32) / half)
    ang = jnp.arange(seq, dtype=F32)[:, None] * inv[None, :]
    cos = jnp.concatenate([jnp.cos(ang)] * 2, axis=1)
    sin = jnp.concatenate([jnp.sin(ang)] * 2, axis=1)
    ones, zeros = jnp.ones((seq, 2 * NOPE), F32), jnp.zeros((seq, 2 * NOPE), F32)
    pad = jnp.zeros((seq, PAIR_W - 2 * NOPE - 2 * ROPE), F32)
    cext = jnp.concatenate([ones, cos, cos, pad], axis=1)
    sext = jnp.concatenate([zeros, sin, sin, pad], axis=1)
    cs128 = jnp.concatenate([cos, sin, jnp.zeros((seq, LANES - 2 * ROPE), F32)], axis=1)
    return cext, sext, cs128


def _pair_slabs(nope, rope):
    k = nope.shape[0]
    nope = nope.reshape(k, N_PAIRS, 2 * NOPE)
    rope = rope.reshape(k, N_PAIRS, 2 * ROPE)
    pad = jnp.zeros((k, N_PAIRS, PAIR_W - 2 * NOPE - 2 * ROPE), nope.dtype)
    return jnp.concatenate([nope, rope, pad], axis=2).reshape(k, N_PAIRS * PAIR_W)


def _split_slabs(slabs):
    k = slabs.shape[0]
    s = slabs.reshape(k, N_PAIRS, PAIR_W)
    return s[:, :, :2 * NOPE].reshape(k, N_HEADS, NOPE), s[:, :, 2 * NOPE:2 * NOPE + 2 * ROPE].reshape(k, N_HEADS, ROPE)


def kernel(x, w_in, b_gate, g_q_a, w_uq, g_kv_a, w_ukv, w_o_mla, w_o_dil, w_out, ln1_g, ln1_b, w_ff1, w_ff2, ln2_g, ln2_b, loss_target, m_w_in, m_b_gate, m_g_q_a, m_w_uq, m_g_kv_a, m_w_ukv, m_w_o_mla, m_w_o_dil, m_w_out, m_ln1_g, m_ln1_b, m_w_ff1, m_w_ff2, m_ln2_g, m_ln2_b, v_w_in, v_b_gate, v_g_q_a, v_w_uq, v_g_kv_a, v_w_ukv, v_w_o_mla, v_w_o_dil, v_w_out, v_ln1_g, v_ln1_b, v_w_ff1, v_w_ff2, v_ln2_g, v_ln2_b):
    batch, seq, _ = x.shape
    tokens = batch * seq
    weights = dict(w_in=w_in, w_uq=w_uq, w_ukv=w_ukv, w_o_mla=w_o_mla, w_o_dil=w_o_dil, w_out=w_out, w_ff1=w_ff1, w_ff2=w_ff2, b_gate=b_gate)
    mom_m = dict(w_in=m_w_in, w_uq=m_w_uq, w_ukv=m_w_ukv, w_o_mla=m_w_o_mla, w_o_dil=m_w_o_dil, w_out=m_w_out, w_ff1=m_w_ff1, w_ff2=m_w_ff2, b_gate=m_b_gate)
    mom_v = dict(w_in=v_w_in, w_uq=v_w_uq, w_ukv=v_w_ukv, w_o_mla=v_w_o_mla, w_o_dil=v_w_o_dil, w_out=v_w_out, w_ff1=v_w_ff1, w_ff2=v_w_ff2, b_gate=v_b_gate)

    first = ["w_in", "w_uq", "w_ukv"]
    widths = [weights[n].shape[2] for n in first]
    shards = [weights["w_in"][0].T.astype(BF16)] + [_pad_cols(weights[n][0].astype(BF16)) for n in first[1:]]
    g_in, g_uq, g_ukv = _run_comm(_Gather(shards), shards, "all_gather_first_weights")
    g_uq, g_ukv = g_uq[:, :, :widths[1]], g_ukv[:, :, :widths[2]]

    s1, s2, n_in = Q_LORA + KV_LORA, Q_LORA + KV_LORA + ROPE, N_DEV * widths[0]

    def w_in_cols(lo, hi):
        out = []
        while lo < hi:
            d, off = divmod(lo, widths[0])
            take = min(hi - lo, widths[0] - off)
            out.append(g_in[d][off:off + take].T)
            lo += take
        return out

    w_in_ext = jnp.concatenate(w_in_cols(0, s2) + [_rot_cols(jnp.concatenate(w_in_cols(s1, s2), axis=1)),
                                                   jnp.zeros((D_MODEL, LOW_W - s2 - ROPE), BF16)] + w_in_cols(s2, n_in), axis=1)
    uq = _from_col_shards(g_uq).reshape(Q_LORA, N_HEADS, NOPE + ROPE)
    w1 = _pair_slabs(uq[:, :, :NOPE], uq[:, :, NOPE:])
    ukv = _from_col_shards(g_ukv).reshape(KV_LORA, N_HEADS, NOPE + HEAD_V)
    wk = ukv[:, :, :NOPE].reshape(KV_LORA, N_HEADS * NOPE)
    wv = ukv[:, :, NOPE:].reshape(KV_LORA, N_HEADS * HEAD_V)
    cext, sext, cs128 = _rope_tables(seq)
    dil_bias = _dilated_bias_table(seq)
    no_bias = jnp.zeros((1, 8, LANES), F32)

    x2 = x.reshape(tokens, D_MODEL)
    low, gates, qkvd, qp, kp, vm, qn, kvn, xb = _fwd_proj(x2, w_in_ext, w1, wk, wv, g_q_a, g_kv_a, cext, sext, cs128, seq=seq)
    bg = b_gate[0]
    bg_hi = bg.astype(BF16)
    bg_lo = (bg - bg_hi.astype(F32)).astype(BF16)
    later = [weights[n][0].astype(BF16) for n in ("w_o_mla", "w_o_dil", "w_out", "w_ff1", "w_ff2")]
    later.append(_pad_rows(jnp.concatenate([bg_hi, bg_lo], axis=0), 16))
    mla = dict(batch=batch, seq=seq, width=PAIR_W, col0=(0, 0, 0), dilated=False, scale=MLA_SCALE)
    dil = dict(batch=batch, seq=seq, width=LANES, col0=(0, N_PAIRS, 2 * N_PAIRS), dilated=True, scale=DIL_SCALE)
    o_a, lse_a, g_oa, g_ob, g_out, g_ff1, g_ff2, g_bg = _attn_fwd(
        qp, kp, vm, no_bias, name="mla_attention_fwd", comm=_Gather(later), comm_arrays=later, **mla)
    o_b, lse_b = _attn_fwd(qkvd, qkvd, qkvd, dil_bias, name="dilated_attention_fwd", **dil)
    w_oa, w_ob = _from_col_shards(g_oa), _from_col_shards(g_ob)
    w_out_full = g_out.reshape(D_MODEL, D_MODEL)
    w_ff2_full = g_ff2.reshape(D_FF, D_MODEL)
    bg_parts = g_bg.astype(F32)
    b_gate_full = _from_col_shards(bg_parts[:, 0:2] + bg_parts[:, 2:4])
    hb, xhat1, rstd1, y_a, y_b, mix = _fwd_mix(o_a, o_b, gates, x2, b_gate_full, w_oa, w_ob, w_out_full, ln1_g, ln1_b, seq=seq)
    u, dz2, dz2b, stat2 = _fwd_mlp(hb, xhat1, loss_target.reshape(tokens, D_MODEL), g_ff1, w_ff2_full, ln1_g, ln1_b, ln2_g, ln2_b, seq=seq)

    du, dz1, dz1b, stat1 = _bwd_mlp(dz2, dz2b, u, xhat1, rstd1, g_ff1, w_ff2_full, ln1_g, seq=seq)
    dw_ff = [_wgrad(hb, du, "wgrad_ff1", by_shard=True),
             _wgrad(u, dz2b, "wgrad_ff2", square_relu=True).reshape(N_DEV, FF_SHARD, D_MODEL)]
    dgates, dy_a, dy_b, do_a, do_b, stat_g = _bwd_mix(dz1b, gates, y_a, y_b, b_gate_full, w_oa, w_ob, w_out_full, seq=seq)
    dqp, dkp, dvm, r_ff1 = _attn_bwd(qp, kp, vm, o_a, do_a, lse_a, no_bias, name="mla_attention_bwd",
                                     comm=_Scatter(dw_ff[:1]), comm_arrays=dw_ff[:1], **mla)
    dw_mid = [_to_col_shards(_wgrad(o_a, dy_a, "wgrad_o_mla")), _to_col_shards(_wgrad(o_b, dy_b, "wgrad_o_dil")),
              _wgrad(mix, dz1b, "wgrad_out").reshape(N_DEV, D_MODEL // N_DEV, D_MODEL),
              _pad_rows(_to_col_shards(stat_g[0:2]).astype(BF16), 16)]
    second = dw_ff[1:] + dw_mid
    dq_d, dk_d, dv_d, r_ff2, r_oa, r_ob, r_out, r_bg = _attn_bwd(qkvd, qkvd, qkvd, o_b, do_b, lse_b, dil_bias, name="dilated_attention_bwd",
                                                                 comm=_Scatter(second), comm_arrays=second, **dil)
    grad_x, dproj, d_a, dkn, stat_r = _bwd_proj(dqp, dkp, dvm, dq_d, dk_d, dv_d, dgates, dz1, low, w_in_ext, w1, wk, wv,
                                                g_q_a, g_kv_a, cext, sext, cs128, seq=seq)

    dw_in_ext = _wgrad(dproj, xb, "wgrad_in")
    dw1 = _wgrad(qn, d_a, "wgrad_uq")
    dwk = _wgrad(kvn, dkn, "wgrad_ukv_k")
    dwv = _wgrad(kvn, dvm, "wgrad_ukv_v")
    dw_kr = dw_in_ext[s1:s2] + _unrot_cols(dw_in_ext[s2:s2 + ROPE].T).T

    def dw_in_cols(lo, hi):
        out = []
        for a, b, piece in ((0, s1, lambda u, v: dw_in_ext[u:v]), (s1, s2, lambda u, v: dw_kr[u - s1:v - s1]),
                            (s2, n_in, lambda u, v: dw_in_ext[u + LOW_W - s2:v + LOW_W - s2])):
            if max(lo, a) < min(hi, b):
                out.append(piece(max(lo, a), min(hi, b)))
        return out

    dw_in = jnp.stack([jnp.concatenate(dw_in_cols(d * widths[0], (d + 1) * widths[0]), axis=0) for d in range(N_DEV)])
    n1, r1 = _split_slabs(dw1)
    dw_uq = jnp.concatenate([n1, r1], axis=2).reshape(Q_LORA, N_HEADS * (NOPE + ROPE))
    dw_ukv = jnp.concatenate([dwk.reshape(KV_LORA, N_HEADS, NOPE), dwv.reshape(KV_LORA, N_HEADS, HEAD_V)], axis=2).reshape(KV_LORA, N_HEADS * (NOPE + HEAD_V))
    last = [dw_in] + [_pad_cols(_to_col_shards(dw)) for dw in (dw_uq, dw_ukv)]
    theirs = _rs_sibling(last, "rs_last_sibling_exchange")
    sums = [_pair_sum(a, b, "rs_last_pair_sum_" + n) for a, b, n in zip(last, theirs, first)]
    partial = jnp.concatenate([stat_r[0:1, :Q_LORA], stat_r[1:2, :KV_LORA], stat1[0:1], stat1[1:2], stat2[0:1], stat2[1:2],
                               stat2[2:3, :LANES]], axis=1)
    partial = _pad_rows(partial.reshape(-1, LANES), 8)
    rest = [s[1] for s in sums]
    got_in, got_uq, got_ukv, every = _run_comm(_Plans([_ChipExchange(rest), _Gather([partial])]), rest + [partial],
                                               "rs_last_chip_exchange")

    upd = {}
    early = ["w_ff1", "w_ff2", "w_out", "w_o_mla", "w_o_dil"]
    items = [(weights[n][0], mom_m[n][0], mom_v[n][0], own, parts) for n, own, parts in
             zip(early, (dw_ff[0], dw_ff[1], dw_mid[2], dw_mid[0], dw_mid[1]), (r_ff1, r_ff2, r_out, r_oa, r_ob))]
    upd.update(zip(early, _adamw(items, "adamw_early_weights")))
    (in_t,) = _adamw([(weights["w_in"][0].T, mom_m["w_in"][0].T, mom_v["w_in"][0].T, sums[0][0], got_in)], "adamw_w_in")
    upd["w_in"] = tuple(a.T for a in in_t)
    for n, w, (own, _), parts in zip(first[1:], widths[1:], sums[1:], (got_uq, got_ukv)):
        (upd[n],) = _adamw([(weights[n][0], mom_m[n][0], mom_v[n][0], own[:, :w], parts[:, :, :w])], "adamw_" + n)
    (bg_upd,) = _adamw([(_pad_rows(b_gate[0], 16), _pad_rows(m_b_gate[0], 16), _pad_rows(v_b_gate[0], 16), dw_mid[3], r_bg)],
                       "adamw_b_gate")
    upd["b_gate"] = tuple(t[0:2] for t in bg_upd)

    small_w = [g_q_a, g_kv_a, ln1_g, ln1_b, ln2_g, ln2_b]
    small_m = [m_g_q_a, m_g_kv_a, m_ln1_g, m_ln1_b, m_ln2_g, m_ln2_b]
    small_v = [v_g_q_a, v_g_kv_a, v_ln1_g, v_ln1_b, v_ln2_g, v_ln2_b]
    small_widths = [a.shape[1] for a in small_w]

    def as_rows(vecs, extra):
        flat = jnp.concatenate(vecs + [jnp.zeros((1, extra), F32)], axis=1)
        return _pad_rows(flat.reshape(-1, LANES), 8)

    g_s, d_s, nm_s, nv_s = _adamw_small(every, as_rows(small_w, LANES), as_rows(small_m, LANES), as_rows(small_v, LANES))

    def split_small(a):
        flat = a.reshape(1, -1)
        out, c0 = [], 0
        for w in small_widths:
            out.append(flat[:, c0:c0 + w])
            c0 += w
        return out, flat[0, c0]

    g_small, loss = split_small(g_s)
    small = [g_small, split_small(d_s)[0], split_small(nm_s)[0], split_small(nv_s)[0]]

    order = ["w_in", "b_gate", "g_q_a", "w_uq", "g_kv_a", "w_ukv", "w_o_mla", "w_o_dil", "w_out", "ln1_g", "ln1_b", "w_ff1", "w_ff2", "ln2_g", "ln2_b"]
    small_names = ["g_q_a", "g_kv_a", "ln1_g", "ln1_b", "ln2_g", "ln2_b"]

    def pick(kind):
        return [small[kind][small_names.index(n)] if n in small_names else upd[n][kind][None] for n in order]

    return (loss, grad_x.reshape(batch, seq, D_MODEL), *pick(0), *pick(1), *pick(2), *pick(3))
```

```python
import functools
import math

import jax
import jax.numpy as jnp
from jax import lax
from jax.experimental import pallas as pl
from jax.experimental.pallas import tpu as pltpu

F32 = jnp.float32
BF16 = jnp.bfloat16
I32 = jnp.int32

D_MODEL = 1024
N_HEADS = 8
NOPE = 64
ROPE = 32
HEAD_V = 64
Q_LORA = 384
KV_LORA = 256
DIL_WIDTH = 512
D_FF = 4096
ROPE_THETA = 10000.0
LN_EPS = 1e-5
RMS_EPS = 1e-6
NEG = -1e30
ALPHA = 2.0 ** 0.25
MLA_SCALE = (NOPE + ROPE) ** -0.5
DIL_SCALE = 64 ** -0.5
ADAM_LR, ADAM_B1, ADAM_B2, ADAM_EPS, ADAM_WD, ADAM_STEP = 0.001, 0.9, 0.999, 1e-08, 0.01, 10

LANES = 128
PAIR_W = 256
N_PAIRS = N_HEADS // 2
LOW_W = 768
IN_EXT = LOW_W + 3 * DIL_WIDTH + 2 * D_MODEL
N_DEV = 8
FF_SHARD = D_FF // N_DEV
FF_STEP = 4
WGRAD_SHARDS = 4
TOKEN_TILE = 256
MIX_TILE = 512
ATTN_TILE = 512
VMEM_LIMIT = 56 << 20

MESH = pl.DeviceIdType.MESH
ANY = pl.BlockSpec(memory_space=pl.ANY)
CHIP_FLIPS = ((0, 0), (0, 1), (1, 0), (1, 1))
PEER_FLIPS = tuple((fx, fy, fc) for fx in (0, 1) for fy in (0, 1) for fc in (0, 1))[1:]


def _cp(*sem):
    return pltpu.CompilerParams(dimension_semantics=sem or None, vmem_limit_bytes=VMEM_LIMIT)


def _full(shape):
    nd = len(shape)
    return pl.BlockSpec(shape, lambda *_: (0,) * nd)


def _rows(tm, width):
    return pl.BlockSpec((tm, width), lambda i, *_: (i, 0))


def _dot(a, b):
    return jnp.dot(a, b, preferred_element_type=F32)


def _dot_nt(a, b):
    return lax.dot_general(a, b, (((1,), (1,)), ((), ())), preferred_element_type=F32)


def _dot_tn(a, b):
    return lax.dot_general(a, b, (((0,), (0,)), ((), ())), preferred_element_type=F32)


def _sigmoid(z):
    return 1.0 / (1.0 + jnp.exp(-z))


def _place():
    return lax.axis_index("x"), lax.axis_index("y"), lax.axis_index("c")


def _flip(v, f):
    return 1 - v if f else v


class _Gather:
    def __init__(self, shards):
        self.n = len(shards)
        self.out_shape = [jax.ShapeDtypeStruct((N_DEV, *s.shape), s.dtype) for s in shards]
        self.scratch = [pltpu.SemaphoreType.DMA((7 * self.n,)), pltpu.SemaphoreType.DMA((7 * self.n,)),
                        pltpu.SemaphoreType.DMA((self.n,))]

    def _copies(self, what, srcs, dsts, send, recv, local):
        x, y, c = _place()
        chips = [(_flip(x, fx), _flip(y, fy)) for fx, fy in CHIP_FLIPS[1:]]
        out = []
        for a in range(self.n):
            def slot(px, py, pc, a=a):
                return dsts[a].at[4 * px + 2 * py + pc]

            def copy(k, block, to, src=None, a=a, slot=slot):
                return pltpu.make_async_remote_copy(
                    src_ref=slot(*block) if src is None else src, dst_ref=slot(*block),
                    send_sem=send.at[7 * a + k], recv_sem=recv.at[7 * a + k], device_id=to, device_id_type=MESH)

            if what == "mine":
                out.append(pltpu.make_async_copy(srcs[a], slot(x, y, c), local.at[a]))
            elif what == "first":
                out.append(copy(0, (x, y, c), (x, y, 1 - c), src=srcs[a]))
                out += [copy(1 + j, (x, y, c), (*chip, c), src=srcs[a]) for j, chip in enumerate(chips)]
            elif what == "landed":
                out += [copy(1 + j, (*chip, c), (x, y, c)) for j, chip in enumerate(chips)]
            elif what == "passed":
                out += [copy(4 + j, (*chip, c), (x, y, 1 - c)) for j, chip in enumerate(chips)]
            else:
                out.append(copy(0, (x, y, 1 - c), (x, y, c)))
                out += [copy(4 + j, (*chip, 1 - c), (x, y, c)) for j, chip in enumerate(chips)]
        return out

    def start(self, *refs):
        for cp in self._copies("first", *refs) + self._copies("mine", *refs):
            cp.start()

    def forward(self, *refs):
        for landed, passed in zip(self._copies("landed", *refs), self._copies("passed", *refs)):
            landed.wait_recv()
            passed.start()

    def finish(self, *refs):
        for cp in self._copies("from_sibling", *refs):
            cp.wait_recv()
        for cp in self._copies("first", *refs) + self._copies("passed", *refs):
            cp.wait_send()
        for cp in self._copies("mine", *refs):
            cp.wait()


class _Scatter:
    def __init__(self, arrays):
        self.n = len(arrays)
        self.out_shape = [jax.ShapeDtypeStruct((7, *a.shape[1:]), a.dtype) for a in arrays]
        self.scratch = [pltpu.SemaphoreType.DMA((7 * self.n,)), pltpu.SemaphoreType.DMA((7 * self.n,))]

    def _copies(self, srcs, dsts, send, recv):
        x, y, c = _place()
        out = []
        for a in range(self.n):
            for k, (fx, fy, fc) in enumerate(PEER_FLIPS):
                px, py, pc = _flip(x, fx), _flip(y, fy), _flip(c, fc)
                out.append(pltpu.make_async_remote_copy(
                    src_ref=srcs[a].at[4 * px + 2 * py + pc], dst_ref=dsts[a].at[k],
                    send_sem=send.at[7 * a + k], recv_sem=recv.at[7 * a + k], device_id=(px, py, pc), device_id_type=MESH))
        return out

    def start(self, *refs):
        for cp in self._copies(*refs):
            cp.start()

    def forward(self, *refs):
        pass

    def finish(self, *refs):
        for cp in self._copies(*refs):
            cp.wait_send()
        for cp in self._copies(*refs):
            cp.wait_recv()


class _ChipExchange:
    def __init__(self, arrays):
        self.n = len(arrays)
        self.out_shape = [jax.ShapeDtypeStruct(a.shape, a.dtype) for a in arrays]
        self.scratch = [pltpu.SemaphoreType.DMA((3 * self.n,)), pltpu.SemaphoreType.DMA((3 * self.n,))]

    def _copies(self, srcs, dsts, send, recv):
        x, y, c = _place()
        return [pltpu.make_async_remote_copy(
            src_ref=srcs[a].at[k], dst_ref=dsts[a].at[k], send_sem=send.at[3 * a + k], recv_sem=recv.at[3 * a + k],
            device_id=(_flip(x, fx), _flip(y, fy), c), device_id_type=MESH)
            for a in range(self.n) for k, (fx, fy) in enumerate(CHIP_FLIPS[1:])]

    def start(self, *refs):
        for cp in self._copies(*refs):
            cp.start()

    def forward(self, *refs):
        pass

    def finish(self, *refs):
        for cp in self._copies(*refs):
            cp.wait_send()
        for cp in self._copies(*refs):
            cp.wait_recv()


class _Plans:
    def __init__(self, plans):
        self.plans = plans
        self.n = sum(p.n for p in plans)
        self.out_shape = [s for p in plans for s in p.out_shape]
        self.scratch = [s for p in plans for s in p.scratch]

    def _each(self, phase, srcs, dsts, *sems):
        i0 = s0 = 0
        for p in self.plans:
            getattr(p, phase)(srcs[i0:i0 + p.n], dsts[i0:i0 + p.n], *sems[s0:s0 + len(p.scratch)])
            i0, s0 = i0 + p.n, s0 + len(p.scratch)

    def start(self, *refs):
        self._each("start", *refs)

    def forward(self, *refs):
        self._each("forward", *refs)

    def finish(self, *refs):
        self._each("finish", *refs)


def _run_comm(comm, arrays, name):
    n = comm.n

    def body(*refs):
        args = (refs[:n], refs[n:2 * n], *refs[2 * n:])
        comm.start(*args)
        comm.forward(*args)
        comm.finish(*args)

    return pl.pallas_call(body, name=name, out_shape=comm.out_shape, in_specs=[ANY] * n, out_specs=[ANY] * n,
                          scratch_shapes=comm.scratch)(*arrays)


def _rs_sibling(arrays, name):
    n = len(arrays)

    def body(*refs):
        srcs, got, (send, recv) = refs[:n], refs[n:2 * n], refs[2 * n:]
        x, y, c = _place()
        copies = []
        for a in range(n):
            for r, (fx, fy) in enumerate(CHIP_FLIPS):
                chip = 2 * _flip(x, fx) + _flip(y, fy)
                copies.append(pltpu.make_async_remote_copy(
                    src_ref=srcs[a].at[2 * chip + 1 - c], dst_ref=got[a].at[r], send_sem=send.at[4 * a + r],
                    recv_sem=recv.at[4 * a + r], device_id=(x, y, 1 - c), device_id_type=MESH))
        for cp in copies:
            cp.start()
        for cp in copies:
            cp.wait_send()
        for cp in copies:
            cp.wait_recv()

    return pl.pallas_call(
        body, name=name, out_shape=[jax.ShapeDtypeStruct((4, *a.shape[1:]), a.dtype) for a in arrays],
        in_specs=[ANY] * n, out_specs=[ANY] * n,
        scratch_shapes=[pltpu.SemaphoreType.DMA((4 * n,)), pltpu.SemaphoreType.DMA((4 * n,))],
    )(*arrays)


def _chip_slots():
    x, y, c = _place()
    return jnp.stack([4 * _flip(x, fx) + 2 * _flip(y, fy) + c for fx, fy in CHIP_FLIPS]).astype(I32)


def _tiles(rows, cols, steps=4):
    if rows % (16 * steps) == 0:
        return steps, (rows // steps, cols), lambda i: (i, 0)
    if cols % (LANES * steps) == 0:
        return steps, (rows, cols // steps), lambda i: (0, i)
    return 1, (rows, cols), lambda i: (0, 0)


def _pair_sum(full, theirs, name):
    _, rows, cols = theirs.shape
    steps, tile, at = _tiles(rows, cols)

    def body(slots_ref, m0_ref, m1_ref, m2_ref, m3_ref, b_ref, own_ref, rest_ref):
        own_ref[...] = m0_ref[...].astype(F32) + b_ref[0].astype(F32)
        for k, m_ref in enumerate((m1_ref, m2_ref, m3_ref)):
            rest_ref[k] = (m_ref[...].astype(F32) + b_ref[k + 1].astype(F32)).astype(BF16)

    def mine(k):
        return pl.BlockSpec((None, *tile), lambda i, slots: (slots[k], *at(i)))

    return pl.pallas_call(
        body, name=name,
        grid_spec=pltpu.PrefetchScalarGridSpec(
            num_scalar_prefetch=1, grid=(steps,),
            in_specs=[mine(0), mine(1), mine(2), mine(3), pl.BlockSpec((4, *tile), lambda i, slots: (0, *at(i)))],
            out_specs=(pl.BlockSpec(tile, lambda i, slots: at(i)), pl.BlockSpec((3, *tile), lambda i, slots: (0, *at(i))))),
        out_shape=(jax.ShapeDtypeStruct((rows, cols), F32), jax.ShapeDtypeStruct((3, rows, cols), BF16)),
        compiler_params=_cp("parallel"),
    )(_chip_slots(), full, full, full, full, theirs)


def _head_lanes(width, h):
    lane = lax.broadcasted_iota(I32, (1, width), 1)
    if width == LANES:
        return (lane >= 64 * h) & (lane < 64 * h + 64)
    nope = (lane >= NOPE * h) & (lane < NOPE * h + NOPE)
    rope = (lane >= 2 * NOPE + ROPE * h) & (lane < 2 * NOPE + ROPE * h + ROPE)
    return nope | rope


def _dilated_bias_table(seq):
    t = min(ATTN_TILE, seq)
    nd = seq // t

    def body(o_ref):
        delta = pl.program_id(0) * t + lax.broadcasted_iota(I32, (t, t), 1) - lax.broadcasted_iota(I32, (t, t), 0)
        mult = ((delta <= 128).astype(I32) + (((delta & 3) == 0) & (delta <= 512)).astype(I32)
                + ((delta & 15) == 0).astype(I32))
        logm = jnp.where(mult == 3, math.log(3.0), jnp.where(mult == 2, math.log(2.0), 0.0))
        valid = (delta >= 0) & (mult > 0)
        dist = delta.astype(F32)
        for h in range(N_HEADS):
            o_ref[h] = jnp.where(valid, logm - 2.0 ** (-(h + 1)) * dist, NEG)

    return pl.pallas_call(
        body, name="dilated_bias_table", grid=(nd,), out_shape=jax.ShapeDtypeStruct((N_HEADS, nd, t, t), F32),
        out_specs=pl.BlockSpec((N_HEADS, None, t, t), lambda d: (0, d, 0, 0)),
        compiler_params=_cp("parallel"),
    )()


def _comm_hooks(comm, refs, n_in, n_out):
    if comm is None:
        return refs[:n_in], refs[n_in:n_in + n_out], refs[n_in + n_out:], None
    n = comm.n
    ins, srcs = refs[:n_in], refs[n_in:n_in + n]
    outs, dsts = refs[n_in + n:n_in + n + n_out], refs[n_in + n + n_out:n_in + 2 * n + n_out]
    rest = refs[n_in + 2 * n + n_out:]
    own = len(rest) - len(comm.scratch)
    return ins, outs, rest[:own], (srcs, dsts, *rest[own:])


def _attn_fwd(q, k, v, bias, *, batch, seq, width, col0, dilated, scale, name, comm=None, comm_arrays=()):
    t = min(ATTN_TILE, seq)
    nq = seq // t
    half = t // 2
    cq, ck, cv = col0
    pre = scale if dilated else 1.0
    steps = batch * N_PAIRS

    def body(*refs):
        (q_ref, k_ref, v_ref, bias_ref), (o_ref, lse_ref), (v_heads,), plan = _comm_hooks(comm, refs, 4, 2)
        step_no = pl.program_id(0) * N_PAIRS + pl.program_id(1)
        if plan:
            pl.when(step_no == 0)(lambda: comm.start(*plan))
            pl.when(step_no == (3 * steps) // 4)(lambda: comm.forward(*plan))
        v_all = v_ref[...].astype(F32)
        for h in (0, 1):
            v_heads[h] = jnp.transpose(jnp.where(_head_lanes(LANES, h), v_all, 0.0)).astype(BF16)
        top = lax.broadcasted_iota(I32, (LANES, t), 0) < HEAD_V
        causal = lax.broadcasted_iota(I32, (t, t), 0) <= lax.broadcasted_iota(I32, (t, t), 1)
        def heads(i):
            q2 = q_ref[pl.ds(pl.multiple_of(i * t, t), t), :]
            q2 = q2 * pre if dilated else q2
            return [jnp.where(_head_lanes(width, h), q2, jnp.zeros_like(q2)) for h in (0, 1)]

        def scores(qh, j):
            kj = k_ref[pl.ds(pl.multiple_of(j * t, t), t), :]
            return tuple(_dot_nt(kj, qh[h]) for h in (0, 1))

        lax.fori_loop(0, nq, functools.partial(query_tile, heads, scores, bias_ref, o_ref, lse_ref, v_heads, top, causal),
                      0)
        if plan:
            pl.when(step_no == steps - 1)(lambda: comm.finish(*plan))

    def query_tile(heads, scores, bias_ref, o_ref, lse_ref, v_heads, top, causal, i, _):
        qs = pl.multiple_of(i * t, t)
        qh = heads(i)

        def step(j, carry, last):
            m0, l0, m1, l1, acc = carry
            s0, s1 = scores(qh, j)
            ks = pl.multiple_of(j * t, t)
            new, alphas, pv = [], [], []

            def online(h, m, l, s, keys, queries):
                s = s[keys, queries]
                if dilated:
                    s = s + (bias_ref[h, 0, keys, queries] if last else bias_ref[h, i - j])
                else:
                    s = s * scale
                    if last:
                        s = jnp.where(causal[keys, queries], s, NEG)
                m, l = m[:, queries], l[:, queries]
                m_new = jnp.maximum(m, jnp.max(s, axis=0, keepdims=True))
                a = jnp.exp(m - m_new)
                p = jnp.exp(s - m_new)
                v_keys = v_heads[h, :, pl.ds(ks, t)]
                return m_new, a * l + jnp.sum(p, axis=0, keepdims=True), a, _dot(v_keys[:, keys], p.astype(BF16))

            for h, (m, l, s) in enumerate(((m0, l0, s0), (m1, l1, s1))):
                if last and half % LANES == 0:
                    parts = [online(h, m, l, s, slice(0, half), slice(0, half)),
                             online(h, m, l, s, slice(0, t), slice(half, t))]
                    m_new, l_new, a, pv_h = (jnp.concatenate(x, axis=1) for x in zip(*parts))
                else:
                    m_new, l_new, a, pv_h = online(h, m, l, s, slice(0, t), slice(0, t))
                new += [m_new, l_new]
                alphas.append(a)
                pv.append(pv_h)
            acc = jnp.where(top, alphas[0], alphas[1]) * acc + pv[0] + pv[1]
            return (*new, acc)

        row = jnp.full((1, t), NEG, F32)
        zero = jnp.zeros((1, t), F32)
        init = (row, zero, row, zero, jnp.zeros((LANES, t), F32))
        m0, l0, m1, l1, acc = step(i, lax.fori_loop(0, i, functools.partial(step, last=False), init), True)
        o_ref[pl.ds(qs, t), :] = jnp.transpose(acc * jnp.where(top, 1.0 / l0, 1.0 / l1)).astype(BF16)
        r = lax.broadcasted_iota(I32, (8, t), 0)
        lse_ref[:, pl.ds(qs, t)] = jnp.where(r == 0, m0 + jnp.log(l0), jnp.where(r == 1, m1 + jnp.log(l1), 0.0))
        return 0

    bias_spec = (pl.BlockSpec((2, nq, t, t), lambda b, p: (p, 0, 0, 0)) if dilated
                 else pl.BlockSpec((None, 8, LANES), lambda b, p: (0, 0, 0)))
    n = comm.n if comm else 0
    return pl.pallas_call(
        body, name=name, grid=(batch, N_PAIRS),
        out_shape=[jax.ShapeDtypeStruct((batch * seq, DIL_WIDTH), BF16), jax.ShapeDtypeStruct((batch * N_PAIRS, 8, seq), F32)]
        + (comm.out_shape if comm else []),
        in_specs=[pl.BlockSpec((seq, width), lambda b, p: (b, cq + p)),
                  pl.BlockSpec((seq, width), lambda b, p: (b, ck + p)),
                  pl.BlockSpec((seq, LANES), lambda b, p: (b, cv + p)),
                  bias_spec] + [ANY] * n,
        out_specs=[pl.BlockSpec((seq, LANES), lambda b, p: (b, p)),
                   pl.BlockSpec((None, 8, seq), lambda b, p: (b * N_PAIRS + p, 0, 0))] + [ANY] * n,
        scratch_shapes=[pltpu.VMEM((2, LANES, seq), BF16)] + (comm.scratch if comm else []),
        compiler_params=_cp("arbitrary", "arbitrary") if comm else _cp("parallel", "parallel"),
    )(q, k, v, bias, *comm_arrays)


def _attn_bwd(q, k, v, o, do, lse, bias, *, batch, seq, width, col0, dilated, scale, name, comm=None, comm_arrays=()):
    t = min(ATTN_TILE, seq)
    nq = seq // t
    half = t // 2
    cq, ck, cv = col0
    pre = scale if dilated else 1.0
    dq_transposed = width == LANES
    steps = batch * N_PAIRS

    def body(*refs):
        ins, (dq_ref, dk_ref, dv_ref), (dq_acc, dk_acc, dv_acc, rowdot, q_heads, do_heads), plan = _comm_hooks(comm, refs, 7, 3)
        q_ref, k_ref, v_ref, o_ref, do_ref, lse_ref, bias_ref = ins
        step_no = pl.program_id(0) * N_PAIRS + pl.program_id(1)
        if plan:
            pl.when(step_no == 0)(lambda: comm.start(*plan))
        wlane = [_head_lanes(width, h) for h in (0, 1)]
        vlane = [_head_lanes(LANES, h) for h in (0, 1)]
        causal = lax.broadcasted_iota(I32, (t, t), 0) <= lax.broadcasted_iota(I32, (t, t), 1)
        q_all = q_ref[...] * pre if dilated else q_ref[...]
        for h in (0, 1):
            q_heads[h] = jnp.where(wlane[h], q_all, jnp.zeros_like(q_all))
            do_heads[h] = jnp.where(vlane[h], do_ref[...], jnp.zeros_like(do_ref[...]))
        prod = jnp.transpose(do_ref[...].astype(F32) * o_ref[...].astype(F32))
        rowdot[0:1, :] = jnp.sum(prod[0:HEAD_V], axis=0, keepdims=True)
        rowdot[1:2, :] = jnp.sum(prod[HEAD_V:], axis=0, keepdims=True)
        dq_acc[...] = jnp.zeros_like(dq_acc)

        def k_tile(j, _):
            ks = pl.multiple_of(j * t, t)
            kj = k_ref[pl.ds(ks, t), :]
            vj = v_ref[pl.ds(ks, t), :]
            kh = [jnp.where(wlane[h], kj, jnp.zeros_like(kj)) for h in (0, 1)]
            if dq_transposed:
                kh = [jnp.transpose(kh[h].astype(F32)).astype(BF16) for h in (0, 1)]
            dk_acc[...] = jnp.zeros_like(dk_acc)
            dv_acc[...] = jnp.zeros_like(dv_acc)

            def operands(i):
                qs = pl.multiple_of(i * t, t)
                return [q_heads[h, pl.ds(qs, t), :] for h in (0, 1)], [do_heads[h, pl.ds(qs, t), :] for h in (0, 1)]

            def q_tile(n, _, last):
                i = nq - 1 - n
                qs = pl.multiple_of(i * t, t)
                qih, doih = operands(i)

                def block(keys, queries):
                    count = queries.stop - queries.start
                    at = pl.ds(qs + queries.start, count)
                    ss = [_dot_nt(kj[keys], qih[h][queries]) for h in (0, 1)]
                    dps = [_dot_nt(vj[keys], doih[h][queries]) for h in (0, 1)]
                    dq_b = jnp.zeros((width, count) if dq_transposed else (count, width), F32)
                    for h, (s, dp) in enumerate(zip(ss, dps)):
                        if dilated:
                            s = s + (bias_ref[h, 0, keys, queries] if last else bias_ref[h, i - j])
                        else:
                            s = s * scale
                            if last:
                                s = jnp.where(causal[keys, queries], s, NEG)
                        p = jnp.exp(s - lse_ref[h:h + 1, at])
                        ds = p * (dp - rowdot[h:h + 1, at])
                        ds = (ds if dilated else ds * scale).astype(BF16)
                        dv_acc[keys, :] += _dot(p.astype(BF16), doih[h][queries])
                        dk_acc[keys, :] += _dot(ds, qih[h][queries])
                        dq_b = dq_b + (_dot(kh[h][:, keys], ds) if dq_transposed else _dot_tn(ds, kh[h][keys]))
                    if dq_transposed:
                        dq_acc[:, at] += dq_b
                    else:
                        dq_acc[at, :] += dq_b

                if last and half % LANES == 0:
                    block(slice(0, half), slice(0, half))
                    block(slice(0, t), slice(half, t))
                else:
                    block(slice(0, t), slice(0, t))
                return 0

            q_tile(nq - 1 - j, lax.fori_loop(0, nq - 1 - j, functools.partial(q_tile, last=False), 0), True)
            dk_ref[pl.ds(ks, t), :] = dk_acc[...].astype(BF16)
            dv_ref[pl.ds(ks, t), :] = dv_acc[...].astype(BF16)
            return 0

        lax.fori_loop(0, nq, k_tile, 0)
        dq_ref[...] = ((jnp.transpose(dq_acc[...]) if dq_transposed else dq_acc[...]) * pre).astype(BF16)
        if plan:
            pl.when(step_no == steps - 1)(lambda: comm.finish(*plan))

    tokens = batch * seq
    bias_spec = (pl.BlockSpec((2, nq, t, t), lambda b, p: (p, 0, 0, 0)) if dilated
                 else pl.BlockSpec((None, 8, LANES), lambda b, p: (0, 0, 0)))
    n = comm.n if comm else 0
    return pl.pallas_call(
        body, name=name, grid=(batch, N_PAIRS),
        out_shape=[jax.ShapeDtypeStruct((tokens, N_PAIRS * width), BF16), jax.ShapeDtypeStruct((tokens, N_PAIRS * width), BF16),
                   jax.ShapeDtypeStruct((tokens, DIL_WIDTH), BF16)] + (comm.out_shape if comm else []),
        in_specs=[pl.BlockSpec((seq, width), lambda b, p: (b, cq + p)),
                  pl.BlockSpec((seq, width), lambda b, p: (b, ck + p)),
                  pl.BlockSpec((seq, LANES), lambda b, p: (b, cv + p)),
                  pl.BlockSpec((seq, LANES), lambda b, p: (b, p)),
                  pl.BlockSpec((seq, LANES), lambda b, p: (b, p)),
                  pl.BlockSpec((None, 8, seq), lambda b, p: (b * N_PAIRS + p, 0, 0)),
                  bias_spec] + [ANY] * n,
        out_specs=[pl.BlockSpec((seq, width), lambda b, p: (b, p)),
                   pl.BlockSpec((seq, width), lambda b, p: (b, p)),
                   pl.BlockSpec((seq, LANES), lambda b, p: (b, p))] + [ANY] * n,
        scratch_shapes=[pltpu.VMEM((width, seq) if dq_transposed else (seq, width), F32),
                        pltpu.VMEM((t, width), F32), pltpu.VMEM((t, LANES), F32),
                        pltpu.VMEM((8, seq), F32), pltpu.VMEM((2, seq, width), BF16), pltpu.VMEM((2, seq, LANES), BF16)]
        + (comm.scratch if comm else []),
        compiler_params=_cp("arbitrary", "arbitrary") if comm else _cp("parallel", "parallel"),
    )(q, k, v, o, do, lse, bias, *comm_arrays)


def _rms(xf, g):
    r = lax.rsqrt(jnp.mean(xf * xf, axis=1, keepdims=True) + RMS_EPS)
    return xf * r * g, r


def _rms_bwd(dy, xf, r, g):
    gy = dy * g
    dx = r * gy - xf * (r * r * r) * jnp.mean(gy * xf, axis=1, keepdims=True)
    return dx, dy * xf * r


def _ln_bwd(dy, xhat, rstd, g):
    dxh = dy * g
    return rstd * (dxh - jnp.mean(dxh, axis=1, keepdims=True) - xhat * jnp.mean(dxh * xhat, axis=1, keepdims=True))


def _rope_slabs(q, cos, sin, transpose):
    first_half = (lax.broadcasted_iota(I32, (1, LANES), 1) % ROPE) < ROPE // 2
    out = []
    for p in range(N_PAIRS):
        blk = q[:, p * PAIR_W + LANES:(p + 1) * PAIR_W]
        y = blk * sin if transpose else blk
        up, down = pltpu.roll(y, LANES - ROPE // 2, 1), pltpu.roll(y, ROPE // 2, 1)
        rot = jnp.where(first_half, up, -down) if transpose else jnp.where(first_half, -up, down) * sin
        out += [q[:, p * PAIR_W:p * PAIR_W + LANES], blk * cos + rot]
    return jnp.concatenate(out, axis=1)


def _fwd_proj(x, w_in_ext, w1, wk, wv, g_q, g_kv, cext, sext, cs128, *, seq):
    tokens = x.shape[0]
    tm = min(MIX_TILE, seq)
    ns = seq // tm

    def body(x_ref, win_ref, w1_ref, wk_ref, wv_ref, gq_ref, gkv_ref, c_ref, s_ref, cs_ref,
             low_ref, gates_ref, qkvd_ref, qp_ref, kp_ref, vm_ref, qn_ref, kvn_ref, xb_ref):
        xt = x_ref[...].astype(BF16)
        xb_ref[...] = xt
        low = _dot(xt, win_ref[:, 0:LOW_W])
        low_ref[...] = low
        qkvd_ref[...] = _dot(xt, win_ref[:, LOW_W:LOW_W + 3 * DIL_WIDTH]).astype(BF16)
        gates_ref[...] = _dot(xt, win_ref[:, LOW_W + 3 * DIL_WIDTH:]).astype(BF16)
        qn = _rms(low[:, 0:Q_LORA], gq_ref[...])[0].astype(BF16)
        kvn = _rms(low[:, Q_LORA:Q_LORA + KV_LORA], gkv_ref[...])[0].astype(BF16)
        qn_ref[...] = qn
        kvn_ref[...] = kvn
        qp_ref[...] = _rope_slabs(_dot(qn, w1_ref[...]), c_ref[...], s_ref[...], False).astype(BF16)
        kr = low[:, Q_LORA + KV_LORA:] * cs_ref[...]
        kr = kr + pltpu.roll(kr, LANES - ROPE, 1)
        lane = lax.broadcasted_iota(I32, kr.shape, 1)
        kr = jnp.where(lane < ROPE, kr, 0.0)
        kr = (kr + pltpu.roll(kr, ROPE, 1)).astype(BF16)
        kn = _dot(kvn, wk_ref[...]).astype(BF16)
        kp_ref[...] = jnp.concatenate([blk for p in range(N_PAIRS) for blk in (kn[:, p * LANES:(p + 1) * LANES], kr)], axis=1)
        vm_ref[...] = _dot(kvn, wv_ref[...]).astype(BF16)

    n_gates = 2 * D_MODEL
    outs = [(LOW_W, F32), (n_gates, BF16), (3 * DIL_WIDTH, BF16), (N_PAIRS * PAIR_W, BF16), (N_PAIRS * PAIR_W, BF16),
            (DIL_WIDTH, BF16), (Q_LORA, BF16), (KV_LORA, BF16), (D_MODEL, BF16)]
    return pl.pallas_call(
        body, name="fwd_proj", grid=(tokens // tm,),
        out_shape=tuple(jax.ShapeDtypeStruct((tokens, w), dt) for w, dt in outs),
        in_specs=[_rows(tm, D_MODEL), _full(w_in_ext.shape), _full(w1.shape), _full(wk.shape),
                  _full(wv.shape), _full(g_q.shape), _full(g_kv.shape),
                  pl.BlockSpec((tm, LANES), lambda i: (i % ns, 1)),
                  pl.BlockSpec((tm, LANES), lambda i: (i % ns, 1)),
                  pl.BlockSpec((tm, LANES), lambda i: (i % ns, 0))],
        out_specs=tuple(_rows(tm, w) for w, _ in outs),
        compiler_params=_cp("parallel"),
    )(x, w_in_ext, w1, wk, wv, g_q, g_kv, cext, sext, cs128)


def _fwd_mix(o_a, o_b, gates, x, b_gate, w_oa, w_ob, w_out, ln_g, ln_b, *, seq):
    tokens = x.shape[0]
    tm = min(MIX_TILE, seq)

    def body(oa_ref, ob_ref, gt_ref, x_ref, bg_ref, woa_ref, wob_ref, wout_ref, g_ref, b_ref,
             hb_ref, xhat_ref, rstd_ref, ya_ref, yb_ref, mix_ref):
        ya = _dot(oa_ref[...], woa_ref[...])
        yb = _dot(ob_ref[...], wob_ref[...])
        g0 = _sigmoid(gt_ref[:, 0:D_MODEL].astype(F32) + bg_ref[0:1, :])
        g1 = _sigmoid(gt_ref[:, D_MODEL:].astype(F32) + bg_ref[1:2, :])
        mix = (g0 * ya + g1 * yb).astype(BF16)
        z = ALPHA * x_ref[...] + _dot(mix, wout_ref[...])
        zc = z - jnp.mean(z, axis=1, keepdims=True)
        rstd = lax.rsqrt(jnp.mean(zc * zc, axis=1, keepdims=True) + LN_EPS)
        xhat = zc * rstd
        hb_ref[...] = (xhat * g_ref[...] + b_ref[...]).astype(BF16)
        xhat_ref[...] = xhat
        rstd_ref[...] = jnp.broadcast_to(rstd, (tm, LANES))
        ya_ref[...] = ya.astype(BF16)
        yb_ref[...] = yb.astype(BF16)
        mix_ref[...] = mix

    outs = [(D_MODEL, BF16), (D_MODEL, F32), (LANES, F32), (D_MODEL, BF16), (D_MODEL, BF16), (D_MODEL, BF16)]
    return pl.pallas_call(
        body, name="fwd_mix", grid=(tokens // tm,),
        out_shape=tuple(jax.ShapeDtypeStruct((tokens, w), dt) for w, dt in outs),
        in_specs=[_rows(tm, DIL_WIDTH), _rows(tm, DIL_WIDTH), _rows(tm, 2 * D_MODEL), _rows(tm, D_MODEL),
                  _full(b_gate.shape), _full(w_oa.shape), _full(w_ob.shape), _full(w_out.shape),
                  _full(ln_g.shape), _full(ln_b.shape)],
        out_specs=tuple(_rows(tm, w) for w, _ in outs),
        compiler_params=_cp("parallel"),
    )(o_a, o_b, gates, x, b_gate, w_oa, w_ob, w_out, ln_g, ln_b)


def _fwd_mlp(hb, xhat1, target, w_ff1, w_ff2, ln1_g, ln1_b, ln_g, ln_b, *, seq):
    tokens = hb.shape[0]
    tm = min(2 * TOKEN_TILE, seq)
    tf = FF_SHARD
    nf = N_DEV // FF_STEP

    def body(hb_ref, xh_ref, tg_ref, w1_ref, w2_ref, g1_ref, b1_ref, g_ref, b_ref, u_ref, dz_ref, dzb_ref, stat_ref, acc):
        i, j = pl.program_id(0), pl.program_id(1)

        @pl.when((i == 0) & (j == 0))
        def _():
            stat_ref[...] = jnp.zeros_like(stat_ref)

        @pl.when(j == 0)
        def _():
            acc[...] = jnp.zeros_like(acc)

        acts = []
        for s in range(FF_STEP):
            u = _dot(hb_ref[...], w1_ref[s])
            u_ref[:, s * tf:(s + 1) * tf] = u.astype(BF16)
            acts.append(jnp.square(jnp.maximum(u, 0.0)).astype(BF16))
        acc[...] += _dot(jnp.concatenate(acts, axis=1), w2_ref[...])

        @pl.when(j == nf - 1)
        def _():
            z = ALPHA * (xh_ref[...] * g1_ref[...] + b1_ref[...]) + acc[...]
            zc = z - jnp.mean(z, axis=1, keepdims=True)
            rstd = lax.rsqrt(jnp.mean(zc * zc, axis=1, keepdims=True) + LN_EPS)
            xhat = zc * rstd
            err = xhat * g_ref[...] + b_ref[...] - tg_ref[...]
            dy = err * (1.0 / D_MODEL)
            dz = _ln_bwd(dy, xhat, rstd, g_ref[...])
            dz_ref[...] = dz
            dzb_ref[...] = dz.astype(BF16)
            stat_ref[0:1, :] += jnp.sum(dy * xhat, axis=0, keepdims=True)
            stat_ref[1:2, :] += jnp.sum(dy, axis=0, keepdims=True)
            stat_ref[2:3, :] += jnp.sum(jnp.sum(err * err, axis=1, keepdims=True), axis=0, keepdims=True) * (0.5 / D_MODEL)

    return pl.pallas_call(
        body, name="fwd_mlp", grid=(tokens // tm, nf),
        out_shape=(jax.ShapeDtypeStruct((tokens, D_FF), BF16), jax.ShapeDtypeStruct((tokens, D_MODEL), F32),
                   jax.ShapeDtypeStruct((tokens, D_MODEL), BF16), jax.ShapeDtypeStruct((8, D_MODEL), F32)),
        in_specs=[_rows(tm, D_MODEL), _rows(tm, D_MODEL), _rows(tm, D_MODEL),
                  pl.BlockSpec((FF_STEP, D_MODEL, tf), lambda i, j: (j, 0, 0)),
                  pl.BlockSpec((FF_STEP * tf, D_MODEL), lambda i, j: (j, 0)),
                  _full(ln1_g.shape), _full(ln1_b.shape), _full(ln_g.shape), _full(ln_b.shape)],
        out_specs=(pl.BlockSpec((tm, FF_STEP * tf), lambda i, j: (i, j)), _rows(tm, D_MODEL), _rows(tm, D_MODEL),
                   _full((8, D_MODEL))),
        scratch_shapes=[pltpu.VMEM((tm, D_MODEL), F32)],
        compiler_params=_cp("arbitrary", "arbitrary"),
    )(hb, xhat1, target, w_ff1, w_ff2, ln1_g, ln1_b, ln_g, ln_b)


def _bwd_mlp(dz2, dz2b, u, xhat1, rstd1, w_ff1, w_ff2, ln_g, *, seq):
    tokens = dz2.shape[0]
    tm = min(2 * TOKEN_TILE, seq)
    tf = FF_SHARD
    nf = N_DEV // FF_STEP

    def body(dz_ref, dzb_ref, u_ref, xh_ref, rs_ref, w1_ref, w2_ref, g_ref, du_ref, dz1_ref, dz1b_ref, stat_ref, acc):
        i, j = pl.program_id(0), pl.program_id(1)

        @pl.when((i == 0) & (j == 0))
        def _():
            stat_ref[...] = jnp.zeros_like(stat_ref)

        @pl.when(j == 0)
        def _():
            acc[...] = jnp.zeros_like(acc)

        da = _dot_nt(dzb_ref[...], w2_ref[...])
        du = (da * (2.0 * jnp.maximum(u_ref[...].astype(F32), 0.0))).astype(BF16)
        du_ref[...] = du
        part = _dot_nt(du[:, 0:tf], w1_ref[0])
        for s in range(1, FF_STEP):
            part = part + _dot_nt(du[:, s * tf:(s + 1) * tf], w1_ref[s])
        acc[...] += part

        @pl.when(j == nf - 1)
        def _():
            dh = ALPHA * dz_ref[...] + acc[...]
            xhat = xh_ref[...]
            dz1 = _ln_bwd(dh, xhat, rs_ref[:, 0:1], g_ref[...])
            dz1_ref[...] = dz1
            dz1b_ref[...] = dz1.astype(BF16)
            stat_ref[0:1, :] += jnp.sum(dh * xhat, axis=0, keepdims=True)
            stat_ref[1:2, :] += jnp.sum(dh, axis=0, keepdims=True)

    return pl.pallas_call(
        body, name="bwd_mlp", grid=(tokens // tm, nf),
        out_shape=(jax.ShapeDtypeStruct((tokens, D_FF), BF16), jax.ShapeDtypeStruct((tokens, D_MODEL), F32),
                   jax.ShapeDtypeStruct((tokens, D_MODEL), BF16), jax.ShapeDtypeStruct((8, D_MODEL), F32)),
        in_specs=[_rows(tm, D_MODEL), _rows(tm, D_MODEL), pl.BlockSpec((tm, FF_STEP * tf), lambda i, j: (i, j)),
                  _rows(tm, D_MODEL), _rows(tm, LANES),
                  pl.BlockSpec((FF_STEP, D_MODEL, tf), lambda i, j: (j, 0, 0)),
                  pl.BlockSpec((FF_STEP * tf, D_MODEL), lambda i, j: (j, 0)),
                  _full(ln_g.shape)],
        out_specs=(pl.BlockSpec((tm, FF_STEP * tf), lambda i, j: (i, j)), _rows(tm, D_MODEL), _rows(tm, D_MODEL),
                   _full((8, D_MODEL))),
        scratch_shapes=[pltpu.VMEM((tm, D_MODEL), F32)],
        compiler_params=_cp("arbitrary", "arbitrary"),
    )(dz2, dz2b, u, xhat1, rstd1, w_ff1, w_ff2, ln_g)


def _bwd_mix(dz1b, gates, y_a, y_b, b_gate, w_oa, w_ob, w_out, *, seq):
    tokens = dz1b.shape[0]
    tm = min(MIX_TILE, seq)

    def body(dz_ref, gt_ref, ya_ref, yb_ref, bg_ref, woa_ref, wob_ref, wout_ref,
             dgt_ref, dya_ref, dyb_ref, doa_ref, dob_ref, stat_ref):
        @pl.when(pl.program_id(0) == 0)
        def _():
            stat_ref[...] = jnp.zeros_like(stat_ref)

        dmix = _dot_nt(dz_ref[...], wout_ref[...])
        for k, (y_ref, w_ref, dy_ref, do_ref) in enumerate(((ya_ref, woa_ref, dya_ref, doa_ref), (yb_ref, wob_ref, dyb_ref, dob_ref))):
            g = _sigmoid(gt_ref[:, k * D_MODEL:(k + 1) * D_MODEL].astype(F32) + bg_ref[k:k + 1, :])
            dgate = dmix * y_ref[...].astype(F32) * g * (1.0 - g)
            dgt_ref[:, k * D_MODEL:(k + 1) * D_MODEL] = dgate.astype(BF16)
            stat_ref[k:k + 1, :] += jnp.sum(dgate, axis=0, keepdims=True)
            dy = (dmix * g).astype(BF16)
            dy_ref[...] = dy
            do_ref[...] = _dot_nt(dy, w_ref[...]).astype(BF16)

    outs = [(2 * D_MODEL, BF16), (D_MODEL, BF16), (D_MODEL, BF16), (DIL_WIDTH, BF16), (DIL_WIDTH, BF16)]
    return pl.pallas_call(
        body, name="bwd_mix", grid=(tokens // tm,),
        out_shape=tuple(jax.ShapeDtypeStruct((tokens, w), dt) for w, dt in outs) + (jax.ShapeDtypeStruct((8, D_MODEL), F32),),
        in_specs=[_rows(tm, D_MODEL), _rows(tm, 2 * D_MODEL), _rows(tm, D_MODEL), _rows(tm, D_MODEL),
                  _full(b_gate.shape), _full(w_oa.shape), _full(w_ob.shape), _full(w_out.shape)],
        out_specs=tuple(_rows(tm, w) for w, _ in outs) + (_full((8, D_MODEL)),),
        compiler_params=_cp("arbitrary"),
    )(dz1b, gates, y_a, y_b, b_gate, w_oa, w_ob, w_out)


def _bwd_proj(dqp, dkp, dvm, dq_d, dk_d, dv_d, dgates, dz1, low, w_in_ext, w1, wk, wv, g_q, g_kv, cext, sext, cs128, *, seq):
    tokens = dz1.shape[0]
    tm = min(TOKEN_TILE, seq)
    ns = seq // tm

    def body(dqp_ref, dkp_ref, dvm_ref, dqd_ref, dkd_ref, dvd_ref, dgt_ref, dz_ref, low_ref, win_ref, w1_ref, wk_ref,
             wv_ref, gq_ref, gkv_ref, c_ref, s_ref, cs_ref, dx_ref, dproj_ref, da_ref, dkn_ref, stat_ref):
        @pl.when(pl.program_id(0) == 0)
        def _():
            stat_ref[...] = jnp.zeros_like(stat_ref)

        low = low_ref[...]
        d_a = _rope_slabs(dqp_ref[...].astype(F32), c_ref[...], s_ref[...], True).astype(BF16)
        da_ref[...] = d_a
        q_a = low[:, 0:Q_LORA]
        _, rq = _rms(q_a, gq_ref[...])
        dq_a, gq_terms = _rms_bwd(_dot_nt(d_a, w1_ref[...]), q_a, rq, gq_ref[...])
        kv_a = low[:, Q_LORA:Q_LORA + KV_LORA]
        _, rkv = _rms(kv_a, gkv_ref[...])
        dkn = jnp.concatenate([dkp_ref[:, p * PAIR_W:p * PAIR_W + LANES] for p in range(N_PAIRS)], axis=1)
        dkn_ref[...] = dkn
        dkv_a, gkv_terms = _rms_bwd(_dot_nt(dkn, wk_ref[...]) + _dot_nt(dvm_ref[...], wv_ref[...]), kv_a, rkv, gkv_ref[...])
        dkr = sum(dkp_ref[:, p * PAIR_W + LANES:(p + 1) * PAIR_W].astype(F32) for p in range(N_PAIRS))
        dkr = dkr + pltpu.roll(dkr, LANES - ROPE, 1)
        dkr = jnp.where(lax.broadcasted_iota(I32, dkr.shape, 1) < ROPE, dkr, 0.0)
        dkr = (dkr + pltpu.roll(dkr, ROPE, 1)) * cs_ref[...]
        stat_ref[0:1, 0:Q_LORA] += jnp.sum(gq_terms, axis=0, keepdims=True)
        stat_ref[1:2, 0:KV_LORA] += jnp.sum(gkv_terms, axis=0, keepdims=True)
        dproj_ref[:, 0:Q_LORA] = dq_a.astype(BF16)
        dproj_ref[:, Q_LORA:Q_LORA + KV_LORA] = dkv_a.astype(BF16)
        dproj_ref[:, Q_LORA + KV_LORA:LOW_W] = dkr.astype(BF16)
        dproj_ref[:, LOW_W:LOW_W + DIL_WIDTH] = dqd_ref[...]
        dproj_ref[:, LOW_W + DIL_WIDTH:LOW_W + 2 * DIL_WIDTH] = dkd_ref[...]
        dproj_ref[:, LOW_W + 2 * DIL_WIDTH:LOW_W + 3 * DIL_WIDTH] = dvd_ref[...]
        dproj_ref[:, LOW_W + 3 * DIL_WIDTH:] = dgt_ref[...]
        dx_ref[...] = ALPHA * dz_ref[...] + _dot_nt(dproj_ref[...], win_ref[...])

    wide = N_PAIRS * PAIR_W
    return pl.pallas_call(
        body, name="bwd_proj", grid=(tokens // tm,),
        out_shape=(jax.ShapeDtypeStruct((tokens, D_MODEL), F32), jax.ShapeDtypeStruct((tokens, IN_EXT), BF16),
                   jax.ShapeDtypeStruct((tokens, wide), BF16), jax.ShapeDtypeStruct((tokens, N_HEADS * NOPE), BF16),
                   jax.ShapeDtypeStruct((8, D_MODEL), F32)),
        in_specs=[_rows(tm, wide), _rows(tm, wide), _rows(tm, DIL_WIDTH), _rows(tm, DIL_WIDTH), _rows(tm, DIL_WIDTH),
                  _rows(tm, DIL_WIDTH), _rows(tm, 2 * D_MODEL),
                  _rows(tm, D_MODEL), _rows(tm, LOW_W), _full(w_in_ext.shape), _full(w1.shape),
                  _full(wk.shape), _full(wv.shape), _full(g_q.shape), _full(g_kv.shape),
                  pl.BlockSpec((tm, LANES), lambda i: (i % ns, 1)), pl.BlockSpec((tm, LANES), lambda i: (i % ns, 1)),
                  pl.BlockSpec((tm, LANES), lambda i: (i % ns, 0))],
        out_specs=(_rows(tm, D_MODEL), _rows(tm, IN_EXT), _rows(tm, wide), _rows(tm, N_HEADS * NOPE), _full((8, D_MODEL))),
        compiler_params=_cp("arbitrary"),
    )(dqp, dkp, dvm, dq_d, dk_d, dv_d, dgates, dz1, low, w_in_ext, w1, wk, wv, g_q, g_kv, cext, sext, cs128)


def _wgrad(a, b, name, square_relu=False, by_shard=False):
    tokens, ka = a.shape
    n = b.shape[1]
    if ka <= 512 or ka % 512 == 0:
        tka = min(ka, 512)
    else:
        tka = max(w for w in range(LANES, min(ka, 2304) + 1, LANES) if ka % w == 0)
    shard = n // N_DEV
    tn = WGRAD_SHARDS * shard if by_shard else max(w for w in range(LANES, min(n, 2304) + 1, LANES) if n % w == 0)
    tt = min(tokens, 2048 if tka <= 512 else 1024)
    nt = tokens // tt

    def body(a_ref, b_ref, o_ref, acc):
        kt = pl.program_id(2)

        @pl.when(kt == 0)
        def _():
            acc[...] = jnp.zeros_like(acc)

        at = a_ref[...]
        if square_relu:
            at = jnp.square(jnp.maximum(at.astype(F32), 0.0)).astype(BF16)
        acc[...] += _dot_tn(at, b_ref[...])

        @pl.when(kt == nt - 1)
        def _():
            if by_shard:
                for s in range(WGRAD_SHARDS):
                    o_ref[s] = acc[:, s * shard:(s + 1) * shard].astype(BF16)
            else:
                o_ref[...] = acc[...].astype(BF16)

    if by_shard:
        out_shape, out_spec = (N_DEV, ka, shard), pl.BlockSpec((WGRAD_SHARDS, tka, shard), lambda i, j, k: (j, i, 0))
    else:
        out_shape, out_spec = (ka, n), pl.BlockSpec((tka, tn), lambda i, j, k: (i, j))
    return pl.pallas_call(
        body, name=name, grid=(ka // tka, n // tn, nt), out_shape=jax.ShapeDtypeStruct(out_shape, BF16),
        in_specs=[pl.BlockSpec((tt, tka), lambda i, j, k: (k, i)), pl.BlockSpec((tt, tn), lambda i, j, k: (k, j))],
        out_specs=out_spec,
        scratch_shapes=[pltpu.VMEM((tka, tn), F32)],
        compiler_params=_cp("parallel", "parallel", "arbitrary"),
    )(a, b)


def _adam_math(w, g, m, v):
    m = ADAM_B1 * m + (1.0 - ADAM_B1) * g
    v = ADAM_B2 * v + (1.0 - ADAM_B2) * jnp.square(g)
    m_hat = m / (1.0 - ADAM_B1 ** ADAM_STEP)
    v_hat = v / (1.0 - ADAM_B2 ** ADAM_STEP)
    return -ADAM_LR * (m_hat / (jnp.sqrt(v_hat) + ADAM_EPS) + ADAM_WD * w), m, v


def _adamw(items, name):
    steps = min(_tiles(*w.shape)[0] for w, *_ in items)
    n_items = len(items)

    def body(slot_ref, *refs):
        ins, outs = refs[:5 * n_items], refs[5 * n_items:]
        for k, (_, _, _, _, parts) in enumerate(items):
            w_ref, m_ref, v_ref, own_ref, p_ref = ins[5 * k:5 * k + 5]
            g_ref, d_ref, nm_ref, nv_ref = outs[4 * k:4 * k + 4]
            g = own_ref[...].astype(F32)
            for d in range(parts.shape[0]):
                g = g + p_ref[d].astype(F32)
            g_ref[...] = g
            d_ref[...], nm_ref[...], nv_ref[...] = _adam_math(w_ref[...], g, m_ref[...], v_ref[...])

    x, y, c = _place()
    in_specs, out_specs, out_shape, args = [], [], [], []
    for w, m, v, own, parts in items:
        rows, cols = w.shape
        _, tile, at = _tiles(rows, cols, steps)
        blk = pl.BlockSpec(tile, lambda i, slot, at=at: at(i))
        own_blk = blk if own.ndim == 2 else pl.BlockSpec((None, *tile), lambda i, slot, at=at: (slot[0], *at(i)))
        in_specs += [blk, blk, blk, own_blk, pl.BlockSpec((parts.shape[0], *tile), lambda i, slot, at=at: (0, *at(i)))]
        out_specs += [blk] * 4
        out_shape += [jax.ShapeDtypeStruct((rows, cols), F32)] * 4
        args += [w, m, v, own, parts]
    out = pl.pallas_call(
        body, name=name,
        grid_spec=pltpu.PrefetchScalarGridSpec(num_scalar_prefetch=1, grid=(steps,), in_specs=in_specs, out_specs=out_specs),
        out_shape=out_shape, compiler_params=_cp("parallel"),
    )(jnp.reshape(4 * x + 2 * y + c, (1,)).astype(I32), *args)
    return [tuple(out[4 * k:4 * k + 4]) for k in range(n_items)]


def _adamw_small(parts, w, m, v):
    _, rows, cols = parts.shape

    def body(p_ref, w_ref, m_ref, v_ref, g_ref, d_ref, nm_ref, nv_ref):
        g = p_ref[0]
        for d in range(1, N_DEV):
            g = g + p_ref[d]
        g_ref[...] = g
        d_ref[...], nm_ref[...], nv_ref[...] = _adam_math(w_ref[...], g, m_ref[...], v_ref[...])

    return pl.pallas_call(
        body, name="adamw_replicated", out_shape=(jax.ShapeDtypeStruct((rows, cols), F32),) * 4,
        in_specs=[_full(parts.shape)] + [_full((rows, cols))] * 3, out_specs=(_full((rows, cols)),) * 4, grid=(1,),
        compiler_params=_cp("arbitrary"),
    )(parts, w, m, v)


def _pad_rows(a2d, mult):
    pad = (-a2d.shape[-2]) % mult
    return jnp.pad(a2d, [(0, 0)] * (a2d.ndim - 2) + [(0, pad), (0, 0)]) if pad else a2d


def _pad_cols(a):
    pad = (-a.shape[-1]) % LANES
    return jnp.pad(a, [(0, 0)] * (a.ndim - 1) + [(0, pad)]) if pad else a


def _rot_cols(w):
    half = ROPE // 2
    return jnp.concatenate([-w[..., half:], w[..., :half]], axis=-1)


def _unrot_cols(dw):
    half = ROPE // 2
    return jnp.concatenate([dw[..., half:], -dw[..., :half]], axis=-1)


def _from_col_shards(stacked):
    return stacked.transpose(1, 0, 2).reshape(stacked.shape[1], -1)


def _to_col_shards(full):
    r = full.shape[0]
    return full.reshape(r, N_DEV, -1).transpose(1, 0, 2)


def _rope_tables(seq):
    half = ROPE // 2
    inv = jnp.power(ROPE_THETA, -jnp.arange(half, dtype=F32) / half)
    ang = jnp.arange(seq, dtype=F32)[:, None] * inv[None, :]
    cos = jnp.concatenate([jnp.cos(ang)] * 2, axis=1)
    sin = jnp.concatenate([jnp.sin(ang)] * 2, axis=1)
    ones, zeros = jnp.ones((seq, 2 * NOPE), F32), jnp.zeros((seq, 2 * NOPE), F32)
    pad = jnp.zeros((seq, PAIR_W - 2 * NOPE - 2 * ROPE), F32)
    cext = jnp.concatenate([ones, cos, cos, pad], axis=1)
    sext = jnp.concatenate([zeros, sin, sin, pad], axis=1)
    cs128 = jnp.concatenate([cos, sin, jnp.zeros((seq, LANES - 2 * ROPE), F32)], axis=1)
    return cext, sext, cs128


def _pair_slabs(nope, rope):
    k = nope.shape[0]
    nope = nope.reshape(k, N_PAIRS, 2 * NOPE)
    rope = rope.reshape(k, N_PAIRS, 2 * ROPE)
    pad = jnp.zeros((k, N_PAIRS, PAIR_W - 2 * NOPE - 2 * ROPE), nope.dtype)
    return jnp.concatenate([nope, rope, pad], axis=2).reshape(k, N_PAIRS * PAIR_W)


def _split_slabs(slabs):
    k = slabs.shape[0]
    s = slabs.reshape(k, N_PAIRS, PAIR_W)
    return s[:, :, :2 * NOPE].reshape(k, N_HEADS, NOPE), s[:, :, 2 * NOPE:2 * NOPE + 2 * ROPE].reshape(k, N_HEADS, ROPE)


def kernel(x, w_in, b_gate, g_q_a, w_uq, g_kv_a, w_ukv, w_o_mla, w_o_dil, w_out, ln1_g, ln1_b, w_ff1, w_ff2, ln2_g, ln2_b, loss_target, m_w_in, m_b_gate, m_g_q_a, m_w_uq, m_g_kv_a, m_w_ukv, m_w_o_mla, m_w_o_dil, m_w_out, m_ln1_g, m_ln1_b, m_w_ff1, m_w_ff2, m_ln2_g, m_ln2_b, v_w_in, v_b_gate, v_g_q_a, v_w_uq, v_g_kv_a, v_w_ukv, v_w_o_mla, v_w_o_dil, v_w_out, v_ln1_g, v_ln1_b, v_w_ff1, v_w_ff2, v_ln2_g, v_ln2_b):
    batch, seq, _ = x.shape
    tokens = batch * seq
    weights = dict(w_in=w_in, w_uq=w_uq, w_ukv=w_ukv, w_o_mla=w_o_mla, w_o_dil=w_o_dil, w_out=w_out, w_ff1=w_ff1, w_ff2=w_ff2, b_gate=b_gate)
    mom_m = dict(w_in=m_w_in, w_uq=m_w_uq, w_ukv=m_w_ukv, w_o_mla=m_w_o_mla, w_o_dil=m_w_o_dil, w_out=m_w_out, w_ff1=m_w_ff1, w_ff2=m_w_ff2, b_gate=m_b_gate)
    mom_v = dict(w_in=v_w_in, w_uq=v_w_uq, w_ukv=v_w_ukv, w_o_mla=v_w_o_mla, w_o_dil=v_w_o_dil, w_out=v_w_out, w_ff1=v_w_ff1, w_ff2=v_w_ff2, b_gate=v_b_gate)

    first = ["w_in", "w_uq", "w_ukv"]
    widths = [weights[n].shape[2] for n in first]
    shards = [weights["w_in"][0].T.astype(BF16)] + [_pad_cols(weights[n][0].astype(BF16)) for n in first[1:]]
    g_in, g_uq, g_ukv = _run_comm(_Gather(shards), shards, "all_gather_first_weights")
    g_uq, g_ukv = g_uq[:, :, :widths[1]], g_ukv[:, :, :widths[2]]

    s1, s2, n_in = Q_LORA + KV_LORA, Q_LORA + KV_LORA + ROPE, N_DEV * widths[0]

    def w_in_cols(lo, hi):
        out = []
        while lo < hi:
            d, off = divmod(lo, widths[0])
            take = min(hi - lo, widths[0] - off)
            out.append(g_in[d][off:off + take].T)
            lo += take
        return out

    w_in_ext = jnp.concatenate(w_in_cols(0, s2) + [_rot_cols(jnp.concatenate(w_in_cols(s1, s2), axis=1)),
                                                   jnp.zeros((D_MODEL, LOW_W - s2 - ROPE), BF16)] + w_in_cols(s2, n_in), axis=1)
    uq = _from_col_shards(g_uq).reshape(Q_LORA, N_HEADS, NOPE + ROPE)
    w1 = _pair_slabs(uq[:, :, :NOPE], uq[:, :, NOPE:])
    ukv = _from_col_shards(g_ukv).reshape(KV_LORA, N_HEADS, NOPE + HEAD_V)
    wk = ukv[:, :, :NOPE].reshape(KV_LORA, N_HEADS * NOPE)
    wv = ukv[:, :, NOPE:].reshape(KV_LORA, N_HEADS * HEAD_V)
    cext, sext, cs128 = _rope_tables(seq)
    dil_bias = _dilated_bias_table(seq)
    no_bias = jnp.zeros((1, 8, LANES), F32)

    x2 = x.reshape(tokens, D_MODEL)
    low, gates, qkvd, qp, kp, vm, qn, kvn, xb = _fwd_proj(x2, w_in_ext, w1, wk, wv, g_q_a, g_kv_a, cext, sext, cs128, seq=seq)
    bg = b_gate[0]
    bg_hi = bg.astype(BF16)
    bg_lo = (bg - bg_hi.astype(F32)).astype(BF16)
    later = [weights[n][0].astype(BF16) for n in ("w_o_mla", "w_o_dil", "w_out", "w_ff1")]
    later.append(_pad_rows(jnp.concatenate([bg_hi, bg_lo], axis=0), 16))
    latest = [weights["w_ff2"][0].astype(BF16)]
    mla = dict(batch=batch, seq=seq, width=PAIR_W, col0=(0, 0, 0), dilated=False, scale=MLA_SCALE)
    dil = dict(batch=batch, seq=seq, width=LANES, col0=(0, N_PAIRS, 2 * N_PAIRS), dilated=True, scale=DIL_SCALE)
    o_a, lse_a, g_oa, g_ob, g_out, g_ff1, g_bg = _attn_fwd(
        qp, kp, vm, no_bias, name="mla_attention_fwd", comm=_Gather(later), comm_arrays=later, **mla)
    o_b, lse_b, g_ff2 = _attn_fwd(qkvd, qkvd, qkvd, dil_bias, name="dilated_attention_fwd",
                                  comm=_Gather(latest), comm_arrays=latest, **dil)
    w_oa, w_ob = _from_col_shards(g_oa), _from_col_shards(g_ob)
    w_out_full = g_out.reshape(D_MODEL, D_MODEL)
    w_ff2_full = g_ff2.reshape(D_FF, D_MODEL)
    bg_parts = g_bg.astype(F32)
    b_gate_full = _from_col_shards(bg_parts[:, 0:2] + bg_parts[:, 2:4])
    hb, xhat1, rstd1, y_a, y_b, mix = _fwd_mix(o_a, o_b, gates, x2, b_gate_full, w_oa, w_ob, w_out_full, ln1_g, ln1_b, seq=seq)
    u, dz2, dz2b, stat2 = _fwd_mlp(hb, xhat1, loss_target.reshape(tokens, D_MODEL), g_ff1, w_ff2_full, ln1_g, ln1_b, ln2_g, ln2_b, seq=seq)

    du, dz1, dz1b, stat1 = _bwd_mlp(dz2, dz2b, u, xhat1, rstd1, g_ff1, w_ff2_full, ln1_g, seq=seq)
    dw_ff = [_wgrad(hb, du, "wgrad_ff1", by_shard=True),
             _wgrad(u, dz2b, "wgrad_ff2", square_relu=True).reshape(N_DEV, FF_SHARD, D_MODEL)]
    dgates, dy_a, dy_b, do_a, do_b, stat_g = _bwd_mix(dz1b, gates, y_a, y_b, b_gate_full, w_oa, w_ob, w_out_full, seq=seq)
    dqp, dkp, dvm, r_ff1 = _attn_bwd(qp, kp, vm, o_a, do_a, lse_a, no_bias, name="mla_attention_bwd",
                                     comm=_Scatter(dw_ff[:1]), comm_arrays=dw_ff[:1], **mla)
    dw_mid = [_to_col_shards(_wgrad(o_a, dy_a, "wgrad_o_mla")), _to_col_shards(_wgrad(o_b, dy_b, "wgrad_o_dil")),
              _wgrad(mix, dz1b, "wgrad_out").reshape(N_DEV, D_MODEL // N_DEV, D_MODEL),
              _pad_rows(_to_col_shards(stat_g[0:2]).astype(BF16), 16)]
    second = dw_ff[1:] + dw_mid
    dq_d, dk_d, dv_d, r_ff2, r_oa, r_ob, r_out, r_bg = _attn_bwd(qkvd, qkvd, qkvd, o_b, do_b, lse_b, dil_bias, name="dilated_attention_bwd",
                                                                 comm=_Scatter(second), comm_arrays=second, **dil)
    grad_x, dproj, d_a, dkn, stat_r = _bwd_proj(dqp, dkp, dvm, dq_d, dk_d, dv_d, dgates, dz1, low, w_in_ext, w1, wk, wv,
                                                g_q_a, g_kv_a, cext, sext, cs128, seq=seq)

    dw_in_ext = _wgrad(dproj, xb, "wgrad_in")
    dw1 = _wgrad(qn, d_a, "wgrad_uq")
    dwk = _wgrad(kvn, dkn, "wgrad_ukv_k")
    dwv = _wgrad(kvn, dvm, "wgrad_ukv_v")
    dw_kr = dw_in_ext[s1:s2] + _unrot_cols(dw_in_ext[s2:s2 + ROPE].T).T

    def dw_in_cols(lo, hi):
        out = []
        for a, b, piece in ((0, s1, lambda u, v: dw_in_ext[u:v]), (s1, s2, lambda u, v: dw_kr[u - s1:v - s1]),
                            (s2, n_in, lambda u, v: dw_in_ext[u + LOW_W - s2:v + LOW_W - s2])):
            if max(lo, a) < min(hi, b):
                out.append(piece(max(lo, a), min(hi, b)))
        return out

    dw_in = jnp.stack([jnp.concatenate(dw_in_cols(d * widths[0], (d + 1) * widths[0]), axis=0) for d in range(N_DEV)])
    n1, r1 = _split_slabs(dw1)
    dw_uq = jnp.concatenate([n1, r1], axis=2).reshape(Q_LORA, N_HEADS * (NOPE + ROPE))
    dw_ukv = jnp.concatenate([dwk.reshape(KV_LORA, N_HEADS, NOPE), dwv.reshape(KV_LORA, N_HEADS, HEAD_V)], axis=2).reshape(KV_LORA, N_HEADS * (NOPE + HEAD_V))
    last = [dw_in] + [_pad_cols(_to_col_shards(dw)) for dw in (dw_uq, dw_ukv)]
    theirs = _rs_sibling(last, "rs_last_sibling_exchange")
    sums = [_pair_sum(a, b, "rs_last_pair_sum_" + n) for a, b, n in zip(last, theirs, first)]
    partial = jnp.concatenate([stat_r[0:1, :Q_LORA], stat_r[1:2, :KV_LORA], stat1[0:1], stat1[1:2], stat2[0:1], stat2[1:2],
                               stat2[2:3, :LANES]], axis=1)
    partial = _pad_rows(partial.reshape(-1, LANES), 8)
    rest = [s[1] for s in sums]
    got_in, got_uq, got_ukv, every = _run_comm(_Plans([_ChipExchange(rest), _Gather([partial])]), rest + [partial],
                                               "rs_last_chip_exchange")

    upd = {}
    early = ["w_ff1", "w_ff2", "w_out", "w_o_mla", "w_o_dil"]
    items = [(weights[n][0], mom_m[n][0], mom_v[n][0], own, parts) for n, own, parts in
             zip(early, (dw_ff[0], dw_ff[1], dw_mid[2], dw_mid[0], dw_mid[1]), (r_ff1, r_ff2, r_out, r_oa, r_ob))]
    upd.update(zip(early, _adamw(items, "adamw_early_weights")))
    (in_t,) = _adamw([(weights["w_in"][0].T, mom_m["w_in"][0].T, mom_v["w_in"][0].T, sums[0][0], got_in)], "adamw_w_in")
    upd["w_in"] = tuple(a.T for a in in_t)
    for n, w, (own, _), parts in zip(first[1:], widths[1:], sums[1:], (got_uq, got_ukv)):
        (upd[n],) = _adamw([(weights[n][0], mom_m[n][0], mom_v[n][0], own[:, :w], parts[:, :, :w])], "adamw_" + n)
    (bg_upd,) = _adamw([(_pad_rows(b_gate[0], 16), _pad_rows(m_b_gate[0], 16), _pad_rows(v_b_gate[0], 16), dw_mid[3], r_bg)],
                       "adamw_b_gate")
    upd["b_gate"] = tuple(t[0:2] for t in bg_upd)

    small_w = [g_q_a, g_kv_a, ln1_g, ln1_b, ln2_g, ln2_b]
    small_m = [m_g_q_a, m_g_kv_a, m_ln1_g, m_ln1_b, m_ln2_g, m_ln2_b]
    small_v = [v_g_q_a, v_g_kv_a, v_ln1_g, v_ln1_b, v_ln2_g, v_ln2_b]
    small_widths = [a.shape[1] for a in small_w]

    def as_rows(vecs, extra):
        flat = jnp.concatenate(vecs + [jnp.zeros((1, extra), F32)], axis=1)
        return _pad_rows(flat.reshape(-1, LANES), 8)

    g_s, d_s, nm_s, nv_s = _adamw_small(every, as_rows(small_w, LANES), as_rows(small_m, LANES), as_rows(small_v, LANES))

    def split_small(a):
        flat = a.reshape(1, -1)
        out, c0 = [], 0
        for w in small_widths:
            out.append(flat[:, c0:c0 + w])
            c0 += w
        return out, flat[0, c0]

    g_small, loss = split_small(g_s)
    small = [g_small, split_small(d_s)[0], split_small(nm_s)[0], split_small(nv_s)[0]]

    order = ["w_in", "b_gate", "g_q_a", "w_uq", "g_kv_a", "w_ukv", "w_o_mla", "w_o_dil", "w_out", "ln1_g", "ln1_b", "w_ff1", "w_ff2", "ln2_g", "ln2_b"]
    small_names = ["g_q_a", "g_kv_a", "ln1_g", "ln1_b", "ln2_g", "ln2_b"]

    def pick(kind):
        return [small[kind][small_names.index(n)] if n in small_names else upd[n][kind][None] for n in order]

    return (loss, grad_x.reshape(batch, seq, D_MODEL), *pick(0), *pick(1), *pick(2), *pick(3))
```

```python
import functools
import math

import jax
import jax.numpy as jnp
from jax import lax
from jax.experimental import pallas as pl
from jax.experimental.pallas import tpu as pltpu

F32 = jnp.float32
BF16 = jnp.bfloat16
I32 = jnp.int32

D_MODEL = 1024
N_HEADS = 8
NOPE = 64
ROPE = 32
HEAD_V = 64
Q_LORA = 384
KV_LORA = 256
DIL_WIDTH = 512
D_FF = 4096
ROPE_THETA = 10000.0
LN_EPS = 1e-5
RMS_EPS = 1e-6
NEG = -1e30
ALPHA = 2.0 ** 0.25
MLA_SCALE = (NOPE + ROPE) ** -0.5
DIL_SCALE = 64 ** -0.5
ADAM_LR, ADAM_B1, ADAM_B2, ADAM_EPS, ADAM_WD, ADAM_STEP = 0.001, 0.9, 0.999, 1e-08, 0.01, 10

LANES = 128
PAIR_W = 256
N_PAIRS = N_HEADS // 2
LOW_W = 768
IN_EXT = LOW_W + 3 * DIL_WIDTH + 2 * D_MODEL
N_DEV = 8
FF_SHARD = D_FF // N_DEV
FF_STEP = 4
WGRAD_SHARDS = 4
TOKEN_TILE = 256
MIX_TILE = 512
ATTN_TILE = 512
VMEM_LIMIT = 56 << 20

MESH = pl.DeviceIdType.MESH
ANY = pl.BlockSpec(memory_space=pl.ANY)
CHIP_FLIPS = ((0, 0), (0, 1), (1, 0), (1, 1))
PEER_FLIPS = tuple((fx, fy, fc) for fx in (0, 1) for fy in (0, 1) for fc in (0, 1))[1:]


def _cp(*sem):
    return pltpu.CompilerParams(dimension_semantics=sem or None, vmem_limit_bytes=VMEM_LIMIT)


def _full(shape):
    nd = len(shape)
    return pl.BlockSpec(shape, lambda *_: (0,) * nd)


def _rows(tm, width):
    return pl.BlockSpec((tm, width), lambda i, *_: (i, 0))


def _dot(a, b):
    return jnp.dot(a, b, preferred_element_type=F32)


def _dot_nt(a, b):
    return lax.dot_general(a, b, (((1,), (1,)), ((), ())), preferred_element_type=F32)


def _dot_tn(a, b):
    return lax.dot_general(a, b, (((0,), (0,)), ((), ())), preferred_element_type=F32)


def _sigmoid(z):
    return 1.0 / (1.0 + jnp.exp(-z))


def _place():
    return lax.axis_index("x"), lax.axis_index("y"), lax.axis_index("c")


def _flip(v, f):
    return 1 - v if f else v


class _Gather:
    def __init__(self, shards):
        self.n = len(shards)
        self.out_shape = [jax.ShapeDtypeStruct((N_DEV, *s.shape), s.dtype) for s in shards]
        self.scratch = [pltpu.SemaphoreType.DMA((7 * self.n,)), pltpu.SemaphoreType.DMA((7 * self.n,)),
                        pltpu.SemaphoreType.DMA((self.n,))]

    def _copies(self, what, srcs, dsts, send, recv, local):
        x, y, c = _place()
        chips = [(_flip(x, fx), _flip(y, fy)) for fx, fy in CHIP_FLIPS[1:]]
        out = []
        for a in range(self.n):
            def slot(px, py, pc, a=a):
                return dsts[a].at[4 * px + 2 * py + pc]

            def copy(k, block, to, src=None, a=a, slot=slot):
                return pltpu.make_async_remote_copy(
                    src_ref=slot(*block) if src is None else src, dst_ref=slot(*block),
                    send_sem=send.at[7 * a + k], recv_sem=recv.at[7 * a + k], device_id=to, device_id_type=MESH)

            if what == "mine":
                out.append(pltpu.make_async_copy(srcs[a], slot(x, y, c), local.at[a]))
            elif what == "first":
                out.append(copy(0, (x, y, c), (x, y, 1 - c), src=srcs[a]))
                out += [copy(1 + j, (x, y, c), (*chip, c), src=srcs[a]) for j, chip in enumerate(chips)]
            elif what == "landed":
                out += [copy(1 + j, (*chip, c), (x, y, c)) for j, chip in enumerate(chips)]
            elif what == "passed":
                out += [copy(4 + j, (*chip, c), (x, y, 1 - c)) for j, chip in enumerate(chips)]
            else:
                out.append(copy(0, (x, y, 1 - c), (x, y, c)))
                out += [copy(4 + j, (*chip, 1 - c), (x, y, c)) for j, chip in enumerate(chips)]
        return out

    def start(self, *refs):
        for cp in self._copies("first", *refs) + self._copies("mine", *refs):
            cp.start()

    def forward(self, *refs):
        for landed, passed in zip(self._copies("landed", *refs), self._copies("passed", *refs)):
            landed.wait_recv()
            passed.start()

    def finish(self, *refs):
        for cp in self._copies("from_sibling", *refs):
            cp.wait_recv()
        for cp in self._copies("first", *refs) + self._copies("passed", *refs):
            cp.wait_send()
        for cp in self._copies("mine", *refs):
            cp.wait()


class _Scatter:
    def __init__(self, arrays):
        self.n = len(arrays)
        self.out_shape = [jax.ShapeDtypeStruct((7, *a.shape[1:]), a.dtype) for a in arrays]
        self.scratch = [pltpu.SemaphoreType.DMA((7 * self.n,)), pltpu.SemaphoreType.DMA((7 * self.n,))]

    def _copies(self, srcs, dsts, send, recv):
        x, y, c = _place()
        out = []
        for a in range(self.n):
            for k, (fx, fy, fc) in enumerate(PEER_FLIPS):
                px, py, pc = _flip(x, fx), _flip(y, fy), _flip(c, fc)
                out.append(pltpu.make_async_remote_copy(
                    src_ref=srcs[a].at[4 * px + 2 * py + pc], dst_ref=dsts[a].at[k],
                    send_sem=send.at[7 * a + k], recv_sem=recv.at[7 * a + k], device_id=(px, py, pc), device_id_type=MESH))
        return out

    def start(self, *refs):
        for cp in self._copies(*refs):
            cp.start()

    def forward(self, *refs):
        pass

    def finish(self, *refs):
        for cp in self._copies(*refs):
            cp.wait_send()
        for cp in self._copies(*refs):
            cp.wait_recv()


class _ChipExchange:
    def __init__(self, arrays):
        self.n = len(arrays)
        self.out_shape = [jax.ShapeDtypeStruct(a.shape, a.dtype) for a in arrays]
        self.scratch = [pltpu.SemaphoreType.DMA((3 * self.n,)), pltpu.SemaphoreType.DMA((3 * self.n,))]

    def _copies(self, srcs, dsts, send, recv):
        x, y, c = _place()
        return [pltpu.make_async_remote_copy(
            src_ref=srcs[a].at[k], dst_ref=dsts[a].at[k], send_sem=send.at[3 * a + k], recv_sem=recv.at[3 * a + k],
            device_id=(_flip(x, fx), _flip(y, fy), c), device_id_type=MESH)
            for a in range(self.n) for k, (fx, fy) in enumerate(CHIP_FLIPS[1:])]

    def start(self, *refs):
        for cp in self._copies(*refs):
            cp.start()

    def forward(self, *refs):
        pass

    def finish(self, *refs):
        for cp in self._copies(*refs):
            cp.wait_send()
        for cp in self._copies(*refs):
            cp.wait_recv()


class _Plans:
    def __init__(self, plans):
        self.plans = plans
        self.n = sum(p.n for p in plans)
        self.out_shape = [s for p in plans for s in p.out_shape]
        self.scratch = [s for p in plans for s in p.scratch]

    def _each(self, phase, srcs, dsts, *sems):
        i0 = s0 = 0
        for p in self.plans:
            getattr(p, phase)(srcs[i0:i0 + p.n], dsts[i0:i0 + p.n], *sems[s0:s0 + len(p.scratch)])
            i0, s0 = i0 + p.n, s0 + len(p.scratch)

    def start(self, *refs):
        self._each("start", *refs)

    def forward(self, *refs):
        self._each("forward", *refs)

    def finish(self, *refs):
        self._each("finish", *refs)


def _run_comm(comm, arrays, name):
    n = comm.n

    def body(*refs):
        args = (refs[:n], refs[n:2 * n], *refs[2 * n:])
        comm.start(*args)
        comm.forward(*args)
        comm.finish(*args)

    return pl.pallas_call(body, name=name, out_shape=comm.out_shape, in_specs=[ANY] * n, out_specs=[ANY] * n,
                          scratch_shapes=comm.scratch)(*arrays)


def _rs_sibling(arrays, name):
    n = len(arrays)

    def body(*refs):
        srcs, got, (send, recv) = refs[:n], refs[n:2 * n], refs[2 * n:]
        x, y, c = _place()
        copies = []
        for a in range(n):
            for r, (fx, fy) in enumerate(CHIP_FLIPS):
                chip = 2 * _flip(x, fx) + _flip(y, fy)
                copies.append(pltpu.make_async_remote_copy(
                    src_ref=srcs[a].at[2 * chip + 1 - c], dst_ref=got[a].at[r], send_sem=send.at[4 * a + r],
                    recv_sem=recv.at[4 * a + r], device_id=(x, y, 1 - c), device_id_type=MESH))
        for cp in copies:
            cp.start()
        for cp in copies:
            cp.wait_send()
        for cp in copies:
            cp.wait_recv()

    return pl.pallas_call(
        body, name=name, out_shape=[jax.ShapeDtypeStruct((4, *a.shape[1:]), a.dtype) for a in arrays],
        in_specs=[ANY] * n, out_specs=[ANY] * n,
        scratch_shapes=[pltpu.SemaphoreType.DMA((4 * n,)), pltpu.SemaphoreType.DMA((4 * n,))],
    )(*arrays)


def _chip_slots():
    x, y, c = _place()
    return jnp.stack([4 * _flip(x, fx) + 2 * _flip(y, fy) + c for fx, fy in CHIP_FLIPS]).astype(I32)


def _tiles(rows, cols, steps=4):
    if rows % (16 * steps) == 0:
        return steps, (rows // steps, cols), lambda i: (i, 0)
    if cols % (LANES * steps) == 0:
        return steps, (rows, cols // steps), lambda i: (0, i)
    return 1, (rows, cols), lambda i: (0, 0)


def _pair_sum(full, theirs, name):
    _, rows, cols = theirs.shape
    steps, tile, at = _tiles(rows, cols)

    def body(slots_ref, m0_ref, m1_ref, m2_ref, m3_ref, b_ref, own_ref, rest_ref):
        own_ref[...] = m0_ref[...].astype(F32) + b_ref[0].astype(F32)
        for k, m_ref in enumerate((m1_ref, m2_ref, m3_ref)):
            rest_ref[k] = (m_ref[...].astype(F32) + b_ref[k + 1].astype(F32)).astype(BF16)

    def mine(k):
        return pl.BlockSpec((None, *tile), lambda i, slots: (slots[k], *at(i)))

    return pl.pallas_call(
        body, name=name,
        grid_spec=pltpu.PrefetchScalarGridSpec(
            num_scalar_prefetch=1, grid=(steps,),
            in_specs=[mine(0), mine(1), mine(2), mine(3), pl.BlockSpec((4, *tile), lambda i, slots: (0, *at(i)))],
            out_specs=(pl.BlockSpec(tile, lambda i, slots: at(i)), pl.BlockSpec((3, *tile), lambda i, slots: (0, *at(i))))),
        out_shape=(jax.ShapeDtypeStruct((rows, cols), F32), jax.ShapeDtypeStruct((3, rows, cols), BF16)),
        compiler_params=_cp("parallel"),
    )(_chip_slots(), full, full, full, full, theirs)


def _head_lanes(width, h):
    lane = lax.broadcasted_iota(I32, (1, width), 1)
    if width == LANES:
        return (lane >= 64 * h) & (lane < 64 * h + 64)
    nope = (lane >= NOPE * h) & (lane < NOPE * h + NOPE)
    rope = (lane >= 2 * NOPE + ROPE * h) & (lane < 2 * NOPE + ROPE * h + ROPE)
    return nope | rope


def _dilated_bias_table(seq):
    t = min(ATTN_TILE, seq)
    nd = seq // t

    def body(o_ref):
        delta = pl.program_id(0) * t + lax.broadcasted_iota(I32, (t, t), 1) - lax.broadcasted_iota(I32, (t, t), 0)
        mult = ((delta <= 128).astype(I32) + (((delta & 3) == 0) & (delta <= 512)).astype(I32)
                + ((delta & 15) == 0).astype(I32))
        logm = jnp.where(mult == 3, math.log(3.0), jnp.where(mult == 2, math.log(2.0), 0.0))
        valid = (delta >= 0) & (mult > 0)
        dist = delta.astype(F32)
        for h in range(N_HEADS):
            o_ref[h] = jnp.where(valid, logm - 2.0 ** (-(h + 1)) * dist, NEG)

    return pl.pallas_call(
        body, name="dilated_bias_table", grid=(nd,), out_shape=jax.ShapeDtypeStruct((N_HEADS, nd, t, t), F32),
        out_specs=pl.BlockSpec((N_HEADS, None, t, t), lambda d: (0, d, 0, 0)),
        compiler_params=_cp("parallel"),
    )()


def _comm_hooks(comm, refs, n_in, n_out):
    if comm is None:
        return refs[:n_in], refs[n_in:n_in + n_out], refs[n_in + n_out:], None
    n = comm.n
    ins, srcs = refs[:n_in], refs[n_in:n_in + n]
    outs, dsts = refs[n_in + n:n_in + n + n_out], refs[n_in + n + n_out:n_in + 2 * n + n_out]
    rest = refs[n_in + 2 * n + n_out:]
    own = len(rest) - len(comm.scratch)
    return ins, outs, rest[:own], (srcs, dsts, *rest[own:])


def _attn_fwd(q, k, v, bias, *, batch, seq, width, col0, dilated, scale, name, comm=None, comm_arrays=()):
    t = min(ATTN_TILE, seq)
    nq = seq // t
    half = t // 2
    cq, ck, cv = col0
    pre = scale if dilated else 1.0
    steps = batch * N_PAIRS

    def body(*refs):
        (q_ref, k_ref, v_ref, bias_ref), (o_ref, lse_ref), (v_heads,), plan = _comm_hooks(comm, refs, 4, 2)
        step_no = pl.program_id(0) * N_PAIRS + pl.program_id(1)
        if plan:
            pl.when(step_no == 0)(lambda: comm.start(*plan))
            pl.when(step_no == (3 * steps) // 4)(lambda: comm.forward(*plan))
        v_all = jnp.transpose(v_ref[...].astype(F32)).astype(BF16)
        head0 = lax.broadcasted_iota(I32, (LANES, seq), 0) < HEAD_V
        v_heads[0] = jnp.where(head0, v_all, jnp.zeros_like(v_all))
        v_heads[1] = jnp.where(head0, jnp.zeros_like(v_all), v_all)
        top = lax.broadcasted_iota(I32, (LANES, t), 0) < HEAD_V
        causal = lax.broadcasted_iota(I32, (t, t), 0) <= lax.broadcasted_iota(I32, (t, t), 1)
        def heads(i):
            q2 = q_ref[pl.ds(pl.multiple_of(i * t, t), t), :]
            q2 = q2 * pre if dilated else q2
            return [jnp.where(_head_lanes(width, h), q2, jnp.zeros_like(q2)) for h in (0, 1)]

        def scores(qh, j):
            kj = k_ref[pl.ds(pl.multiple_of(j * t, t), t), :]
            return tuple(_dot_nt(kj, qh[h]) for h in (0, 1))

        lax.fori_loop(0, nq, functools.partial(query_tile, heads, scores, bias_ref, o_ref, lse_ref, v_heads, top, causal),
                      0)
        if plan:
            pl.when(step_no == steps - 1)(lambda: comm.finish(*plan))

    def query_tile(heads, scores, bias_ref, o_ref, lse_ref, v_heads, top, causal, i, _):
        qs = pl.multiple_of(i * t, t)
        qh = heads(i)

        def step(j, carry, last):
            m0, l0, m1, l1, acc = carry
            s0, s1 = scores(qh, j)
            ks = pl.multiple_of(j * t, t)
            new, alphas, pv = [], [], []

            def online(h, m, l, s, keys, queries):
                s = s[keys, queries]
                if dilated:
                    s = s + (bias_ref[h, 0, keys, queries] if last else bias_ref[h, i - j])
                else:
                    s = s * scale
                    if last:
                        s = jnp.where(causal[keys, queries], s, NEG)
                m, l = m[:, queries], l[:, queries]
                m_new = jnp.maximum(m, jnp.max(s, axis=0, keepdims=True))
                a = jnp.exp(m - m_new)
                p = jnp.exp(s - m_new)
                v_keys = v_heads[h, :, pl.ds(ks, t)]
                return m_new, a * l + jnp.sum(p, axis=0, keepdims=True), a, _dot(v_keys[:, keys], p.astype(BF16))

            for h, (m, l, s) in enumerate(((m0, l0, s0), (m1, l1, s1))):
                if last and half % LANES == 0:
                    parts = [online(h, m, l, s, slice(0, half), slice(0, half)),
                             online(h, m, l, s, slice(0, t), slice(half, t))]
                    m_new, l_new, a, pv_h = (jnp.concatenate(x, axis=1) for x in zip(*parts))
                else:
                    m_new, l_new, a, pv_h = online(h, m, l, s, slice(0, t), slice(0, t))
                new += [m_new, l_new]
                alphas.append(a)
                pv.append(pv_h)
            acc = jnp.where(top, alphas[0], alphas[1]) * acc + pv[0] + pv[1]
            return (*new, acc)

        row = jnp.full((1, t), NEG, F32)
        zero = jnp.zeros((1, t), F32)
        init = (row, zero, row, zero, jnp.zeros((LANES, t), F32))
        m0, l0, m1, l1, acc = step(i, lax.fori_loop(0, i, functools.partial(step, last=False), init), True)
        o_ref[pl.ds(qs, t), :] = jnp.transpose(acc * jnp.where(top, 1.0 / l0, 1.0 / l1)).astype(BF16)
        r = lax.broadcasted_iota(I32, (8, t), 0)
        lse_ref[:, pl.ds(qs, t)] = jnp.where(r == 0, m0 + jnp.log(l0), jnp.where(r == 1, m1 + jnp.log(l1), 0.0))
        return 0

    bias_spec = (pl.BlockSpec((2, nq, t, t), lambda b, p: (p, 0, 0, 0)) if dilated
                 else pl.BlockSpec((None, 8, LANES), lambda b, p: (0, 0, 0)))
    n = comm.n if comm else 0
    return pl.pallas_call(
        body, name=name, grid=(batch, N_PAIRS),
        out_shape=[jax.ShapeDtypeStruct((batch * seq, DIL_WIDTH), BF16), jax.ShapeDtypeStruct((batch * N_PAIRS, 8, seq), F32)]
        + (comm.out_shape if comm else []),
        in_specs=[pl.BlockSpec((seq, width), lambda b, p: (b, cq + p)),
                  pl.BlockSpec((seq, width), lambda b, p: (b, ck + p)),
                  pl.BlockSpec((seq, LANES), lambda b, p: (b, cv + p)),
                  bias_spec] + [ANY] * n,
        out_specs=[pl.BlockSpec((seq, LANES), lambda b, p: (b, p)),
                   pl.BlockSpec((None, 8, seq), lambda b, p: (b * N_PAIRS + p, 0, 0))] + [ANY] * n,
        scratch_shapes=[pltpu.VMEM((2, LANES, seq), BF16)] + (comm.scratch if comm else []),
        compiler_params=_cp("arbitrary", "arbitrary") if comm else _cp("parallel", "parallel"),
    )(q, k, v, bias, *comm_arrays)


def _attn_bwd(q, k, v, o, do, lse, bias, *, batch, seq, width, col0, dilated, scale, name, comm=None, comm_arrays=()):
    t = min(ATTN_TILE, seq)
    nq = seq // t
    half = t // 2
    cq, ck, cv = col0
    pre = scale if dilated else 1.0
    dq_transposed = width == LANES
    steps = batch * N_PAIRS

    def body(*refs):
        ins, (dq_ref, dk_ref, dv_ref), (dq_acc, dk_acc, dv_acc, rowdot, q_heads, do_heads), plan = _comm_hooks(comm, refs, 7, 3)
        q_ref, k_ref, v_ref, o_ref, do_ref, lse_ref, bias_ref = ins
        step_no = pl.program_id(0) * N_PAIRS + pl.program_id(1)
        if plan:
            pl.when(step_no == 0)(lambda: comm.start(*plan))
        wlane = [_head_lanes(width, h) for h in (0, 1)]
        vlane = [_head_lanes(LANES, h) for h in (0, 1)]
        causal = lax.broadcasted_iota(I32, (t, t), 0) <= lax.broadcasted_iota(I32, (t, t), 1)
        q_all = q_ref[...] * pre if dilated else q_ref[...]
        for h in (0, 1):
            q_heads[h] = jnp.where(wlane[h], q_all, jnp.zeros_like(q_all))
            do_heads[h] = jnp.where(vlane[h], do_ref[...], jnp.zeros_like(do_ref[...]))
        prod = jnp.transpose(do_ref[...].astype(F32) * o_ref[...].astype(F32))
        rowdot[0:1, :] = jnp.sum(prod[0:HEAD_V], axis=0, keepdims=True)
        rowdot[1:2, :] = jnp.sum(prod[HEAD_V:], axis=0, keepdims=True)
        dq_acc[...] = jnp.zeros_like(dq_acc)

        def k_tile(j, _):
            ks = pl.multiple_of(j * t, t)
            kj = k_ref[pl.ds(ks, t), :]
            vj = v_ref[pl.ds(ks, t), :]
            kh = [jnp.where(wlane[h], kj, jnp.zeros_like(kj)) for h in (0, 1)]
            if dq_transposed:
                kh = [jnp.transpose(kh[h].astype(F32)).astype(BF16) for h in (0, 1)]
            dk_acc[...] = jnp.zeros_like(dk_acc)
            dv_acc[...] = jnp.zeros_like(dv_acc)

            def operands(i):
                qs = pl.multiple_of(i * t, t)
                return [q_heads[h, pl.ds(qs, t), :] for h in (0, 1)], [do_heads[h, pl.ds(qs, t), :] for h in (0, 1)]

            def q_tile(n, _, last):
                i = nq - 1 - n
                qs = pl.multiple_of(i * t, t)
                qih, doih = operands(i)

                def block(keys, queries):
                    count = queries.stop - queries.start
                    at = pl.ds(qs + queries.start, count)
                    ss = [_dot_nt(kj[keys], qih[h][queries]) for h in (0, 1)]
                    dps = [_dot_nt(vj[keys], doih[h][queries]) for h in (0, 1)]
                    dq_b = jnp.zeros((width, count) if dq_transposed else (count, width), F32)
                    for h, (s, dp) in enumerate(zip(ss, dps)):
                        if dilated:
                            s = s + (bias_ref[h, 0, keys, queries] if last else bias_ref[h, i - j])
                        else:
                            s = s * scale
                            if last:
                                s = jnp.where(causal[keys, queries], s, NEG)
                        p = jnp.exp(s - lse_ref[h:h + 1, at])
                        ds = p * (dp - rowdot[h:h + 1, at])
                        ds = (ds if dilated else ds * scale).astype(BF16)
                        dv_acc[keys, :] += _dot(p.astype(BF16), doih[h][queries])
                        dk_acc[keys, :] += _dot(ds, qih[h][queries])
                        dq_b = dq_b + (_dot(kh[h][:, keys], ds) if dq_transposed else _dot_tn(ds, kh[h][keys]))
                    if dq_transposed:
                        dq_acc[:, at] += dq_b
                    else:
                        dq_acc[at, :] += dq_b

                if last and half % LANES == 0:
                    block(slice(0, half), slice(0, half))
                    block(slice(0, t), slice(half, t))
                else:
                    block(slice(0, t), slice(0, t))
                return 0

            q_tile(nq - 1 - j, lax.fori_loop(0, nq - 1 - j, functools.partial(q_tile, last=False), 0), True)
            dk_ref[pl.ds(ks, t), :] = dk_acc[...].astype(BF16)
            dv_ref[pl.ds(ks, t), :] = dv_acc[...].astype(BF16)
            return 0

        lax.fori_loop(0, nq, k_tile, 0)
        dq_ref[...] = ((jnp.transpose(dq_acc[...]) if dq_transposed else dq_acc[...]) * pre).astype(BF16)
        if plan:
            pl.when(step_no == steps - 1)(lambda: comm.finish(*plan))

    tokens = batch * seq
    bias_spec = (pl.BlockSpec((2, nq, t, t), lambda b, p: (p, 0, 0, 0)) if dilated
                 else pl.BlockSpec((None, 8, LANES), lambda b, p: (0, 0, 0)))
    n = comm.n if comm else 0
    return pl.pallas_call(
        body, name=name, grid=(batch, N_PAIRS),
        out_shape=[jax.ShapeDtypeStruct((tokens, N_PAIRS * width), BF16), jax.ShapeDtypeStruct((tokens, N_PAIRS * width), BF16),
                   jax.ShapeDtypeStruct((tokens, DIL_WIDTH), BF16)] + (comm.out_shape if comm else []),
        in_specs=[pl.BlockSpec((seq, width), lambda b, p: (b, cq + p)),
                  pl.BlockSpec((seq, width), lambda b, p: (b, ck + p)),
                  pl.BlockSpec((seq, LANES), lambda b, p: (b, cv + p)),
                  pl.BlockSpec((seq, LANES), lambda b, p: (b, p)),
                  pl.BlockSpec((seq, LANES), lambda b, p: (b, p)),
                  pl.BlockSpec((None, 8, seq), lambda b, p: (b * N_PAIRS + p, 0, 0)),
                  bias_spec] + [ANY] * n,
        out_specs=[pl.BlockSpec((seq, width), lambda b, p: (b, p)),
                   pl.BlockSpec((seq, width), lambda b, p: (b, p)),
                   pl.BlockSpec((seq, LANES), lambda b, p: (b, p))] + [ANY] * n,
        scratch_shapes=[pltpu.VMEM((width, seq) if dq_transposed else (seq, width), F32),
                        pltpu.VMEM((t, width), F32), pltpu.VMEM((t, LANES), F32),
                        pltpu.VMEM((8, seq), F32), pltpu.VMEM((2, seq, width), BF16), pltpu.VMEM((2, seq, LANES), BF16)]
        + (comm.scratch if comm else []),
        compiler_params=_cp("arbitrary", "arbitrary") if comm else _cp("parallel", "parallel"),
    )(q, k, v, o, do, lse, bias, *comm_arrays)


def _rms(xf, g):
    r = lax.rsqrt(jnp.mean(xf * xf, axis=1, keepdims=True) + RMS_EPS)
    return xf * r * g, r


def _rms_bwd(dy, xf, r, g):
    gy = dy * g
    dx = r * gy - xf * (r * r * r) * jnp.mean(gy * xf, axis=1, keepdims=True)
    return dx, dy * xf * r


def _ln_bwd(dy, xhat, rstd, g):
    dxh = dy * g
    return rstd * (dxh - jnp.mean(dxh, axis=1, keepdims=True) - xhat * jnp.mean(dxh * xhat, axis=1, keepdims=True))


def _rope_slabs(q, cos, sin, transpose):
    first_half = (lax.broadcasted_iota(I32, (1, LANES), 1) % ROPE) < ROPE // 2
    out = []
    for p in range(N_PAIRS):
        blk = q[:, p * PAIR_W + LANES:(p + 1) * PAIR_W]
        y = blk * sin if transpose else blk
        up, down = pltpu.roll(y, LANES - ROPE // 2, 1), pltpu.roll(y, ROPE // 2, 1)
        rot = jnp.where(first_half, up, -down) if transpose else jnp.where(first_half, -up, down) * sin
        out += [q[:, p * PAIR_W:p * PAIR_W + LANES], blk * cos + rot]
    return jnp.concatenate(out, axis=1)


def _fwd_proj(x, w_in_ext, w1, wk, wv, g_q, g_kv, cext, sext, cs128, *, seq):
    tokens = x.shape[0]
    tm = min(MIX_TILE, seq)
    ns = seq // tm

    def body(x_ref, win_ref, w1_ref, wk_ref, wv_ref, gq_ref, gkv_ref, c_ref, s_ref, cs_ref,
             low_ref, gates_ref, qkvd_ref, qp_ref, kp_ref, vm_ref, qn_ref, kvn_ref, xb_ref):
        xt = x_ref[...].astype(BF16)
        xb_ref[...] = xt
        low = _dot(xt, win_ref[:, 0:LOW_W])
        low_ref[...] = low
        qkvd_ref[...] = _dot(xt, win_ref[:, LOW_W:LOW_W + 3 * DIL_WIDTH]).astype(BF16)
        gates_ref[...] = _dot(xt, win_ref[:, LOW_W + 3 * DIL_WIDTH:]).astype(BF16)
        qn = _rms(low[:, 0:Q_LORA], gq_ref[...])[0].astype(BF16)
        kvn = _rms(low[:, Q_LORA:Q_LORA + KV_LORA], gkv_ref[...])[0].astype(BF16)
        qn_ref[...] = qn
        kvn_ref[...] = kvn
        qp_ref[...] = _rope_slabs(_dot(qn, w1_ref[...]), c_ref[...], s_ref[...], False).astype(BF16)
        kr = low[:, Q_LORA + KV_LORA:] * cs_ref[...]
        kr = kr + pltpu.roll(kr, LANES - ROPE, 1)
        lane = lax.broadcasted_iota(I32, kr.shape, 1)
        kr = jnp.where(lane < ROPE, kr, 0.0)
        kr = (kr + pltpu.roll(kr, ROPE, 1)).astype(BF16)
        kn = _dot(kvn, wk_ref[...]).astype(BF16)
        kp_ref[...] = jnp.concatenate([blk for p in range(N_PAIRS) for blk in (kn[:, p * LANES:(p + 1) * LANES], kr)], axis=1)
        vm_ref[...] = _dot(kvn, wv_ref[...]).astype(BF16)

    n_gates = 2 * D_MODEL
    outs = [(LOW_W, F32), (n_gates, BF16), (3 * DIL_WIDTH, BF16), (N_PAIRS * PAIR_W, BF16), (N_PAIRS * PAIR_W, BF16),
            (DIL_WIDTH, BF16), (Q_LORA, BF16), (KV_LORA, BF16), (D_MODEL, BF16)]
    return pl.pallas_call(
        body, name="fwd_proj", grid=(tokens // tm,),
        out_shape=tuple(jax.ShapeDtypeStruct((tokens, w), dt) for w, dt in outs),
        in_specs=[_rows(tm, D_MODEL), _full(w_in_ext.shape), _full(w1.shape), _full(wk.shape),
                  _full(wv.shape), _full(g_q.shape), _full(g_kv.shape),
                  pl.BlockSpec((tm, LANES), lambda i: (i % ns, 1)),
                  pl.BlockSpec((tm, LANES), lambda i: (i % ns, 1)),
                  pl.BlockSpec((tm, LANES), lambda i: (i % ns, 0))],
        out_specs=tuple(_rows(tm, w) for w, _ in outs),
        compiler_params=_cp("parallel"),
    )(x, w_in_ext, w1, wk, wv, g_q, g_kv, cext, sext, cs128)


def _fwd_mix(o_a, o_b, gates, x, b_gate, w_oa, w_ob, w_out, ln_g, ln_b, *, seq):
    tokens = x.shape[0]
    tm = min(MIX_TILE, seq)

    def body(oa_ref, ob_ref, gt_ref, x_ref, bg_ref, woa_ref, wob_ref, wout_ref, g_ref, b_ref,
             hb_ref, xhat_ref, rstd_ref, ya_ref, yb_ref, mix_ref):
        ya = _dot(oa_ref[...], woa_ref[...])
        yb = _dot(ob_ref[...], wob_ref[...])
        g0 = _sigmoid(gt_ref[:, 0:D_MODEL].astype(F32) + bg_ref[0:1, :])
        g1 = _sigmoid(gt_ref[:, D_MODEL:].astype(F32) + bg_ref[1:2, :])
        mix = (g0 * ya + g1 * yb).astype(BF16)
        z = ALPHA * x_ref[...] + _dot(mix, wout_ref[...])
        zc = z - jnp.mean(z, axis=1, keepdims=True)
        rstd = lax.rsqrt(jnp.mean(zc * zc, axis=1, keepdims=True) + LN_EPS)
        xhat = zc * rstd
        hb_ref[...] = (xhat * g_ref[...] + b_ref[...]).astype(BF16)
        xhat_ref[...] = xhat
        rstd_ref[...] = jnp.broadcast_to(rstd, (tm, LANES))
        ya_ref[...] = ya.astype(BF16)
        yb_ref[...] = yb.astype(BF16)
        mix_ref[...] = mix

    outs = [(D_MODEL, BF16), (D_MODEL, F32), (LANES, F32), (D_MODEL, BF16), (D_MODEL, BF16), (D_MODEL, BF16)]
    return pl.pallas_call(
        body, name="fwd_mix", grid=(tokens // tm,),
        out_shape=tuple(jax.ShapeDtypeStruct((tokens, w), dt) for w, dt in outs),
        in_specs=[_rows(tm, DIL_WIDTH), _rows(tm, DIL_WIDTH), _rows(tm, 2 * D_MODEL), _rows(tm, D_MODEL),
                  _full(b_gate.shape), _full(w_oa.shape), _full(w_ob.shape), _full(w_out.shape),
                  _full(ln_g.shape), _full(ln_b.shape)],
        out_specs=tuple(_rows(tm, w) for w, _ in outs),
        compiler_params=_cp("parallel"),
    )(o_a, o_b, gates, x, b_gate, w_oa, w_ob, w_out, ln_g, ln_b)


def _fwd_mlp(hb, xhat1, target, w_ff1, w_ff2, ln1_g, ln1_b, ln_g, ln_b, *, seq):
    tokens = hb.shape[0]
    tm = min(2 * TOKEN_TILE, seq)
    tf = FF_SHARD
    nf = N_DEV // FF_STEP

    def body(hb_ref, xh_ref, tg_ref, w1_ref, w2_ref, g1_ref, b1_ref, g_ref, b_ref, u_ref, dz_ref, dzb_ref, stat_ref, acc):
        i, j = pl.program_id(0), pl.program_id(1)

        @pl.when((i == 0) & (j == 0))
        def _():
            stat_ref[...] = jnp.zeros_like(stat_ref)

        @pl.when(j == 0)
        def _():
            acc[...] = jnp.zeros_like(acc)

        acts = []
        for s in range(FF_STEP):
            u = _dot(hb_ref[...], w1_ref[s])
            u_ref[:, s * tf:(s + 1) * tf] = u.astype(BF16)
            acts.append(jnp.square(jnp.maximum(u, 0.0)).astype(BF16))
        acc[...] += _dot(jnp.concatenate(acts, axis=1), w2_ref[...])

        @pl.when(j == nf - 1)
        def _():
            z = ALPHA * (xh_ref[...] * g1_ref[...] + b1_ref[...]) + acc[...]
            zc = z - jnp.mean(z, axis=1, keepdims=True)
            rstd = lax.rsqrt(jnp.mean(zc * zc, axis=1, keepdims=True) + LN_EPS)
            xhat = zc * rstd
            err = xhat * g_ref[...] + b_ref[...] - tg_ref[...]
            dy = err * (1.0 / D_MODEL)
            dz = _ln_bwd(dy, xhat, rstd, g_ref[...])
            dz_ref[...] = dz
            dzb_ref[...] = dz.astype(BF16)
            stat_ref[0:1, :] += jnp.sum(dy * xhat, axis=0, keepdims=True)
            stat_ref[1:2, :] += jnp.sum(dy, axis=0, keepdims=True)
            stat_ref[2:3, :] += jnp.sum(jnp.sum(err * err, axis=1, keepdims=True), axis=0, keepdims=True) * (0.5 / D_MODEL)

    return pl.pallas_call(
        body, name="fwd_mlp", grid=(tokens // tm, nf),
        out_shape=(jax.ShapeDtypeStruct((tokens, D_FF), BF16), jax.ShapeDtypeStruct((tokens, D_MODEL), F32),
                   jax.ShapeDtypeStruct((tokens, D_MODEL), BF16), jax.ShapeDtypeStruct((8, D_MODEL), F32)),
        in_specs=[_rows(tm, D_MODEL), _rows(tm, D_MODEL), _rows(tm, D_MODEL),
                  pl.BlockSpec((FF_STEP, D_MODEL, tf), lambda i, j: (j, 0, 0)),
                  pl.BlockSpec((FF_STEP * tf, D_MODEL), lambda i, j: (j, 0)),
                  _full(ln1_g.shape), _full(ln1_b.shape), _full(ln_g.shape), _full(ln_b.shape)],
        out_specs=(pl.BlockSpec((tm, FF_STEP * tf), lambda i, j: (i, j)), _rows(tm, D_MODEL), _rows(tm, D_MODEL),
                   _full((8, D_MODEL))),
        scratch_shapes=[pltpu.VMEM((tm, D_MODEL), F32)],
        compiler_params=_cp("arbitrary", "arbitrary"),
    )(hb, xhat1, target, w_ff1, w_ff2, ln1_g, ln1_b, ln_g, ln_b)


def _bwd_mlp(dz2, dz2b, u, xhat1, rstd1, w_ff1, w_ff2, ln_g, *, seq):
    tokens = dz2.shape[0]
    tm = min(2 * TOKEN_TILE, seq)
    tf = FF_SHARD
    nf = N_DEV // FF_STEP

    def body(dz_ref, dzb_ref, u_ref, xh_ref, rs_ref, w1_ref, w2_ref, g_ref, du_ref, dz1_ref, dz1b_ref, stat_ref, acc):
        i, j = pl.program_id(0), pl.program_id(1)

        @pl.when((i == 0) & (j == 0))
        def _():
            stat_ref[...] = jnp.zeros_like(stat_ref)

        @pl.when(j == 0)
        def _():
            acc[...] = jnp.zeros_like(acc)

        da = _dot_nt(dzb_ref[...], w2_ref[...])
        du = (da * (2.0 * jnp.maximum(u_ref[...].astype(F32), 0.0))).astype(BF16)
        du_ref[...] = du
        part = _dot_nt(du[:, 0:tf], w1_ref[0])
        for s in range(1, FF_STEP):
            part = part + _dot_nt(du[:, s * tf:(s + 1) * tf], w1_ref[s])
        acc[...] += part

        @pl.when(j == nf - 1)
        def _():
            dh = ALPHA * dz_ref[...] + acc[...]
            xhat = xh_ref[...]
            dz1 = _ln_bwd(dh, xhat, rs_ref[:, 0:1], g_ref[...])
            dz1_ref[...] = dz1
            dz1b_ref[...] = dz1.astype(BF16)
            stat_ref[0:1, :] += jnp.sum(dh * xhat, axis=0, keepdims=True)
            stat_ref[1:2, :] += jnp.sum(dh, axis=0, keepdims=True)

    return pl.pallas_call(
        body, name="bwd_mlp", grid=(tokens // tm, nf),
        out_shape=(jax.ShapeDtypeStruct((tokens, D_FF), BF16), jax.ShapeDtypeStruct((tokens, D_MODEL), F32),
                   jax.ShapeDtypeStruct((tokens, D_MODEL), BF16), jax.ShapeDtypeStruct((8, D_MODEL), F32)),
        in_specs=[_rows(tm, D_MODEL), _rows(tm, D_MODEL), pl.BlockSpec((tm, FF_STEP * tf), lambda i, j: (i, j)),
                  _rows(tm, D_MODEL), _rows(tm, LANES),
                  pl.BlockSpec((FF_STEP, D_MODEL, tf), lambda i, j: (j, 0, 0)),
                  pl.BlockSpec((FF_STEP * tf, D_MODEL), lambda i, j: (j, 0)),
                  _full(ln_g.shape)],
        out_specs=(pl.BlockSpec((tm, FF_STEP * tf), lambda i, j: (i, j)), _rows(tm, D_MODEL), _rows(tm, D_MODEL),
                   _full((8, D_MODEL))),
        scratch_shapes=[pltpu.VMEM((tm, D_MODEL), F32)],
        compiler_params=_cp("arbitrary", "arbitrary"),
    )(dz2, dz2b, u, xhat1, rstd1, w_ff1, w_ff2, ln_g)


def _bwd_mix(dz1b, gates, y_a, y_b, b_gate, w_oa, w_ob, w_out, *, seq):
    tokens = dz1b.shape[0]
    tm = min(MIX_TILE, seq)

    def body(dz_ref, gt_ref, ya_ref, yb_ref, bg_ref, woa_ref, wob_ref, wout_ref,
             dgt_ref, dya_ref, dyb_ref, doa_ref, dob_ref, stat_ref):
        @pl.when(pl.program_id(0) == 0)
        def _():
            stat_ref[...] = jnp.zeros_like(stat_ref)

        dmix = _dot_nt(dz_ref[...], wout_ref[...])
        for k, (y_ref, w_ref, dy_ref, do_ref) in enumerate(((ya_ref, woa_ref, dya_ref, doa_ref), (yb_ref, wob_ref, dyb_ref, dob_ref))):
            g = _sigmoid(gt_ref[:, k * D_MODEL:(k + 1) * D_MODEL].astype(F32) + bg_ref[k:k + 1, :])
            dgate = dmix * y_ref[...].astype(F32) * g * (1.0 - g)
            dgt_ref[:, k * D_MODEL:(k + 1) * D_MODEL] = dgate.astype(BF16)
            stat_ref[k:k + 1, :] += jnp.sum(dgate, axis=0, keepdims=True)
            dy = (dmix * g).astype(BF16)
            dy_ref[...] = dy
            do_ref[...] = _dot_nt(dy, w_ref[...]).astype(BF16)

    outs = [(2 * D_MODEL, BF16), (D_MODEL, BF16), (D_MODEL, BF16), (DIL_WIDTH, BF16), (DIL_WIDTH, BF16)]
    return pl.pallas_call(
        body, name="bwd_mix", grid=(tokens // tm,),
        out_shape=tuple(jax.ShapeDtypeStruct((tokens, w), dt) for w, dt in outs) + (jax.ShapeDtypeStruct((8, D_MODEL), F32),),
        in_specs=[_rows(tm, D_MODEL), _rows(tm, 2 * D_MODEL), _rows(tm, D_MODEL), _rows(tm, D_MODEL),
                  _full(b_gate.shape), _full(w_oa.shape), _full(w_ob.shape), _full(w_out.shape)],
        out_specs=tuple(_rows(tm, w) for w, _ in outs) + (_full((8, D_MODEL)),),
        compiler_params=_cp("arbitrary"),
    )(dz1b, gates, y_a, y_b, b_gate, w_oa, w_ob, w_out)


def _bwd_proj(dqp, dkp, dvm, dq_d, dk_d, dv_d, dgates, dz1, low, w_in_ext, w1, wk, wv, g_q, g_kv, cext, sext, cs128, *, seq):
    tokens = dz1.shape[0]
    tm = min(TOKEN_TILE, seq)
    ns = seq // tm

    def body(dqp_ref, dkp_ref, dvm_ref, dqd_ref, dkd_ref, dvd_ref, dgt_ref, dz_ref, low_ref, win_ref, w1_ref, wk_ref,
             wv_ref, gq_ref, gkv_ref, c_ref, s_ref, cs_ref, dx_ref, dproj_ref, da_ref, dkn_ref, stat_ref):
        @pl.when(pl.program_id(0) == 0)
        def _():
            stat_ref[...] = jnp.zeros_like(stat_ref)

        low = low_ref[...]
        d_a = _rope_slabs(dqp_ref[...].astype(F32), c_ref[...], s_ref[...], True).astype(BF16)
        da_ref[...] = d_a
        q_a = low[:, 0:Q_LORA]
        _, rq = _rms(q_a, gq_ref[...])
        dq_a, gq_terms = _rms_bwd(_dot_nt(d_a, w1_ref[...]), q_a, rq, gq_ref[...])
        kv_a = low[:, Q_LORA:Q_LORA + KV_LORA]
        _, rkv = _rms(kv_a, gkv_ref[...])
        dkn = jnp.concatenate([dkp_ref[:, p * PAIR_W:p * PAIR_W + LANES] for p in range(N_PAIRS)], axis=1)
        dkn_ref[...] = dkn
        dkv_a, gkv_terms = _rms_bwd(_dot_nt(dkn, wk_ref[...]) + _dot_nt(dvm_ref[...], wv_ref[...]), kv_a, rkv, gkv_ref[...])
        dkr = sum(dkp_ref[:, p * PAIR_W + LANES:(p + 1) * PAIR_W].astype(F32) for p in range(N_PAIRS))
        dkr = dkr + pltpu.roll(dkr, LANES - ROPE, 1)
        dkr = jnp.where(lax.broadcasted_iota(I32, dkr.shape, 1) < ROPE, dkr, 0.0)
        dkr = (dkr + pltpu.roll(dkr, ROPE, 1)) * cs_ref[...]
        stat_ref[0:1, 0:Q_LORA] += jnp.sum(gq_terms, axis=0, keepdims=True)
        stat_ref[1:2, 0:KV_LORA] += jnp.sum(gkv_terms, axis=0, keepdims=True)
        dproj_ref[:, 0:Q_LORA] = dq_a.astype(BF16)
        dproj_ref[:, Q_LORA:Q_LORA + KV_LORA] = dkv_a.astype(BF16)
        dproj_ref[:, Q_LORA + KV_LORA:LOW_W] = dkr.astype(BF16)
        dproj_ref[:, LOW_W:LOW_W + DIL_WIDTH] = dqd_ref[...]
        dproj_ref[:, LOW_W + DIL_WIDTH:LOW_W + 2 * DIL_WIDTH] = dkd_ref[...]
        dproj_ref[:, LOW_W + 2 * DIL_WIDTH:LOW_W + 3 * DIL_WIDTH] = dvd_ref[...]
        dproj_ref[:, LOW_W + 3 * DIL_WIDTH:] = dgt_ref[...]
        dx_ref[...] = ALPHA * dz_ref[...] + _dot_nt(dproj_ref[...], win_ref[...])

    wide = N_PAIRS * PAIR_W
    return pl.pallas_call(
        body, name="bwd_proj", grid=(tokens // tm,),
        out_shape=(jax.ShapeDtypeStruct((tokens, D_MODEL), F32), jax.ShapeDtypeStruct((tokens, IN_EXT), BF16),
                   jax.ShapeDtypeStruct((tokens, wide), BF16), jax.ShapeDtypeStruct((tokens, N_HEADS * NOPE), BF16),
                   jax.ShapeDtypeStruct((8, D_MODEL), F32)),
        in_specs=[_rows(tm, wide), _rows(tm, wide), _rows(tm, DIL_WIDTH), _rows(tm, DIL_WIDTH), _rows(tm, DIL_WIDTH),
                  _rows(tm, DIL_WIDTH), _rows(tm, 2 * D_MODEL),
                  _rows(tm, D_MODEL), _rows(tm, LOW_W), _full(w_in_ext.shape), _full(w1.shape),
                  _full(wk.shape), _full(wv.shape), _full(g_q.shape), _full(g_kv.shape),
                  pl.BlockSpec((tm, LANES), lambda i: (i % ns, 1)), pl.BlockSpec((tm, LANES), lambda i: (i % ns, 1)),
                  pl.BlockSpec((tm, LANES), lambda i: (i % ns, 0))],
        out_specs=(_rows(tm, D_MODEL), _rows(tm, IN_EXT), _rows(tm, wide), _rows(tm, N_HEADS * NOPE), _full((8, D_MODEL))),
        compiler_params=_cp("arbitrary"),
    )(dqp, dkp, dvm, dq_d, dk_d, dv_d, dgates, dz1, low, w_in_ext, w1, wk, wv, g_q, g_kv, cext, sext, cs128)


def _wgrad(a, b, name, square_relu=False, by_shard=False):
    tokens, ka = a.shape
    n = b.shape[1]
    if ka <= 512 or ka % 512 == 0:
        tka = min(ka, 512)
    else:
        tka = max(w for w in range(LANES, min(ka, 2304) + 1, LANES) if ka % w == 0)
    shard = n // N_DEV
    tn = WGRAD_SHARDS * shard if by_shard else max(w for w in range(LANES, min(n, 2304) + 1, LANES) if n % w == 0)
    tt = min(tokens, 2048 if tka <= 512 else 1024)
    nt = tokens // tt

    def body(a_ref, b_ref, o_ref, acc):
        kt = pl.program_id(2)

        @pl.when(kt == 0)
        def _():
            acc[...] = jnp.zeros_like(acc)

        at = a_ref[...]
        if square_relu:
            at = jnp.square(jnp.maximum(at.astype(F32), 0.0)).astype(BF16)
        acc[...] += _dot_tn(at, b_ref[...])

        @pl.when(kt == nt - 1)
        def _():
            if by_shard:
                for s in range(WGRAD_SHARDS):
                    o_ref[s] = acc[:, s * shard:(s + 1) * shard].astype(BF16)
            else:
                o_ref[...] = acc[...].astype(BF16)

    if by_shard:
        out_shape, out_spec = (N_DEV, ka, shard), pl.BlockSpec((WGRAD_SHARDS, tka, shard), lambda i, j, k: (j, i, 0))
    else:
        out_shape, out_spec = (ka, n), pl.BlockSpec((tka, tn), lambda i, j, k: (i, j))
    return pl.pallas_call(
        body, name=name, grid=(ka // tka, n // tn, nt), out_shape=jax.ShapeDtypeStruct(out_shape, BF16),
        in_specs=[pl.BlockSpec((tt, tka), lambda i, j, k: (k, i)), pl.BlockSpec((tt, tn), lambda i, j, k: (k, j))],
        out_specs=out_spec,
        scratch_shapes=[pltpu.VMEM((tka, tn), F32)],
        compiler_params=_cp("parallel", "parallel", "arbitrary"),
    )(a, b)


def _adam_math(w, g, m, v):
    m = ADAM_B1 * m + (1.0 - ADAM_B1) * g
    v = ADAM_B2 * v + (1.0 - ADAM_B2) * jnp.square(g)
    m_hat = m / (1.0 - ADAM_B1 ** ADAM_STEP)
    v_hat = v / (1.0 - ADAM_B2 ** ADAM_STEP)
    return -ADAM_LR * (m_hat / (jnp.sqrt(v_hat) + ADAM_EPS) + ADAM_WD * w), m, v


def _adamw(items, name):
    steps = min(_tiles(*w.shape)[0] for w, *_ in items)
    n_items = len(items)

    def body(slot_ref, *refs):
        ins, outs = refs[:5 * n_items], refs[5 * n_items:]
        for k, (_, _, _, _, parts) in enumerate(items):
            w_ref, m_ref, v_ref, own_ref, p_ref = ins[5 * k:5 * k + 5]
            g_ref, d_ref, nm_ref, nv_ref = outs[4 * k:4 * k + 4]
            g = own_ref[...].astype(F32)
            for d in range(parts.shape[0]):
                g = g + p_ref[d].astype(F32)
            g_ref[...] = g
            d_ref[...], nm_ref[...], nv_ref[...] = _adam_math(w_ref[...], g, m_ref[...], v_ref[...])

    x, y, c = _place()
    in_specs, out_specs, out_shape, args = [], [], [], []
    for w, m, v, own, parts in items:
        rows, cols = w.shape
        _, tile, at = _tiles(rows, cols, steps)
        blk = pl.BlockSpec(tile, lambda i, slot, at=at: at(i))
        own_blk = blk if own.ndim == 2 else pl.BlockSpec((None, *tile), lambda i, slot, at=at: (slot[0], *at(i)))
        in_specs += [blk, blk, blk, own_blk, pl.BlockSpec((parts.shape[0], *tile), lambda i, slot, at=at: (0, *at(i)))]
        out_specs += [blk] * 4
        out_shape += [jax.ShapeDtypeStruct((rows, cols), F32)] * 4
        args += [w, m, v, own, parts]
    out = pl.pallas_call(
        body, name=name,
        grid_spec=pltpu.PrefetchScalarGridSpec(num_scalar_prefetch=1, grid=(steps,), in_specs=in_specs, out_specs=out_specs),
        out_shape=out_shape, compiler_params=_cp("parallel"),
    )(jnp.reshape(4 * x + 2 * y + c, (1,)).astype(I32), *args)
    return [tuple(out[4 * k:4 * k + 4]) for k in range(n_items)]


def _adamw_small(parts, w, m, v):
    _, rows, cols = parts.shape

    def body(p_ref, w_ref, m_ref, v_ref, g_ref, d_ref, nm_ref, nv_ref):
        g = p_ref[0]
        for d in range(1, N_DEV):
            g = g + p_ref[d]
        g_ref[...] = g
        d_ref[...], nm_ref[...], nv_ref[...] = _adam_math(w_ref[...], g, m_ref[...], v_ref[...])

    return pl.pallas_call(
        body, name="adamw_replicated", out_shape=(jax.ShapeDtypeStruct((rows, cols), F32),) * 4,
        in_specs=[_full(parts.shape)] + [_full((rows, cols))] * 3, out_specs=(_full((rows, cols)),) * 4, grid=(1,),
        compiler_params=_cp("arbitrary"),
    )(parts, w, m, v)


def _pad_rows(a2d, mult):
    pad = (-a2d.shape[-2]) % mult
    return jnp.pad(a2d, [(0, 0)] * (a2d.ndim - 2) + [(0, pad), (0, 0)]) if pad else a2d


def _pad_cols(a):
    pad = (-a.shape[-1]) % LANES
    return jnp.pad(a, [(0, 0)] * (a.ndim - 1) + [(0, pad)]) if pad else a


def _rot_cols(w):
    half = ROPE // 2
    return jnp.concatenate([-w[..., half:], w[..., :half]], axis=-1)


def _unrot_cols(dw):
    half = ROPE // 2
    return jnp.concatenate([dw[..., half:], -dw[..., :half]], axis=-1)


def _from_col_shards(stacked):
    return stacked.transpose(1, 0, 2).reshape(stacked.shape[1], -1)


def _to_col_shards(full):
    r = full.shape[0]
    return full.reshape(r, N_DEV, -1).transpose(1, 0, 2)


def _rope_tables(seq):
    half = ROPE // 2
    inv = jnp.power(ROPE_THETA, -jnp.arange(half, dtype=F32) / half)
    ang = jnp.arange(seq, dtype=F32)[:, None] * inv[None, :]
    cos = jnp.concatenate([jnp.cos(ang)] * 2, axis=1)
    sin = jnp.concatenate([jnp.sin(ang)] * 2, axis=1)
    ones, zeros = jnp.ones((seq, 2 * NOPE), F32), jnp.zeros((seq, 2 * NOPE), F32)
    pad = jnp.zeros((seq, PAIR_W - 2 * NOPE - 2 * ROPE), F32)
    cext = jnp.concatenate([ones, cos, cos, pad], axis=1)
    sext = jnp.concatenate([zeros, sin, sin, pad], axis=1)
    cs128 = jnp.concatenate([cos, sin, jnp.zeros((seq, LANES - 2 * ROPE), F32)], axis=1)
    return cext, sext, cs128


def _pair_slabs(nope, rope):
    k = nope.shape[0]
    nope = nope.reshape(k, N_PAIRS, 2 * NOPE)
    rope = rope.reshape(k, N_PAIRS, 2 * ROPE)
    pad = jnp.zeros((k, N_PAIRS, PAIR_W - 2 * NOPE - 2 * ROPE), nope.dtype)
    return jnp.concatenate([nope, rope, pad], axis=2).reshape(k, N_PAIRS * PAIR_W)


def _split_slabs(slabs):
    k = slabs.shape[0]
    s = slabs.reshape(k, N_PAIRS, PAIR_W)
    return s[:, :, :2 * NOPE].reshape(k, N_HEADS, NOPE), s[:, :, 2 * NOPE:2 * NOPE + 2 * ROPE].reshape(k, N_HEADS, ROPE)


def kernel(x, w_in, b_gate, g_q_a, w_uq, g_kv_a, w_ukv, w_o_mla, w_o_dil, w_out, ln1_g, ln1_b, w_ff1, w_ff2, ln2_g, ln2_b, loss_target, m_w_in, m_b_gate, m_g_q_a, m_w_uq, m_g_kv_a, m_w_ukv, m_w_o_mla, m_w_o_dil, m_w_out, m_ln1_g, m_ln1_b, m_w_ff1, m_w_ff2, m_ln2_g, m_ln2_b, v_w_in, v_b_gate, v_g_q_a, v_w_uq, v_g_kv_a, v_w_ukv, v_w_o_mla, v_w_o_dil, v_w_out, v_ln1_g, v_ln1_b, v_w_ff1, v_w_ff2, v_ln2_g, v_ln2_b):
    batch, seq, _ = x.shape
    tokens = batch * seq
    weights = dict(w_in=w_in, w_uq=w_uq, w_ukv=w_ukv, w_o_mla=w_o_mla, w_o_dil=w_o_dil, w_out=w_out, w_ff1=w_ff1, w_ff2=w_ff2, b_gate=b_gate)
    mom_m = dict(w_in=m_w_in, w_uq=m_w_uq, w_ukv=m_w_ukv, w_o_mla=m_w_o_mla, w_o_dil=m_w_o_dil, w_out=m_w_out, w_ff1=m_w_ff1, w_ff2=m_w_ff2, b_gate=m_b_gate)
    mom_v = dict(w_in=v_w_in, w_uq=v_w_uq, w_ukv=v_w_ukv, w_o_mla=v_w_o_mla, w_o_dil=v_w_o_dil, w_out=v_w_out, w_ff1=v_w_ff1, w_ff2=v_w_ff2, b_gate=v_b_gate)

    first = ["w_in", "w_uq", "w_ukv"]
    widths = [weights[n].shape[2] for n in first]
    shards = [weights["w_in"][0].T.astype(BF16)] + [_pad_cols(weights[n][0].astype(BF16)) for n in first[1:]]
    g_in, g_uq, g_ukv = _run_comm(_Gather(shards), shards, "all_gather_first_weights")
    g_uq, g_ukv = g_uq[:, :, :widths[1]], g_ukv[:, :, :widths[2]]

    s1, s2, n_in = Q_LORA + KV_LORA, Q_LORA + KV_LORA + ROPE, N_DEV * widths[0]

    def w_in_cols(lo, hi):
        out = []
        while lo < hi:
            d, off = divmod(lo, widths[0])
            take = min(hi - lo, widths[0] - off)
            out.append(g_in[d][off:off + take].T)
            lo += take
        return out

    w_in_ext = jnp.concatenate(w_in_cols(0, s2) + [_rot_cols(jnp.concatenate(w_in_cols(s1, s2), axis=1)),
                                                   jnp.zeros((D_MODEL, LOW_W - s2 - ROPE), BF16)] + w_in_cols(s2, n_in), axis=1)
    uq = _from_col_shards(g_uq).reshape(Q_LORA, N_HEADS, NOPE + ROPE)
    w1 = _pair_slabs(uq[:, :, :NOPE], uq[:, :, NOPE:])
    ukv = _from_col_shards(g_ukv).reshape(KV_LORA, N_HEADS, NOPE + HEAD_V)
    wk = ukv[:, :, :NOPE].reshape(KV_LORA, N_HEADS * NOPE)
    wv = ukv[:, :, NOPE:].reshape(KV_LORA, N_HEADS * HEAD_V)
    cext, sext, cs128 = _rope_tables(seq)
    dil_bias = _dilated_bias_table(seq)
    no_bias = jnp.zeros((1, 8, LANES), F32)

    x2 = x.reshape(tokens, D_MODEL)
    low, gates, qkvd, qp, kp, vm, qn, kvn, xb = _fwd_proj(x2, w_in_ext, w1, wk, wv, g_q_a, g_kv_a, cext, sext, cs128, seq=seq)
    bg = b_gate[0]
    bg_hi = bg.astype(BF16)
    bg_lo = (bg - bg_hi.astype(F32)).astype(BF16)
    later = [weights[n][0].astype(BF16) for n in ("w_o_mla", "w_o_dil", "w_out", "w_ff1")]
    later.append(_pad_rows(jnp.concatenate([bg_hi, bg_lo], axis=0), 16))
    latest = [weights["w_ff2"][0].astype(BF16)]
    mla = dict(batch=batch, seq=seq, width=PAIR_W, col0=(0, 0, 0), dilated=False, scale=MLA_SCALE)
    dil = dict(batch=batch, seq=seq, width=LANES, col0=(0, N_PAIRS, 2 * N_PAIRS), dilated=True, scale=DIL_SCALE)
    o_a, lse_a, g_oa, g_ob, g_out, g_ff1, g_bg = _attn_fwd(
        qp, kp, vm, no_bias, name="mla_attention_fwd", comm=_Gather(later), comm_arrays=later, **mla)
    o_b, lse_b, g_ff2 = _attn_fwd(qkvd, qkvd, qkvd, dil_bias, name="dilated_attention_fwd",
                                  comm=_Gather(latest), comm_arrays=latest, **dil)
    w_oa, w_ob = _from_col_shards(g_oa), _from_col_shards(g_ob)
    w_out_full = g_out.reshape(D_MODEL, D_MODEL)
    w_ff2_full = g_ff2.reshape(D_FF, D_MODEL)
    bg_parts = g_bg.astype(F32)
    b_gate_full = _from_col_shards(bg_parts[:, 0:2] + bg_parts[:, 2:4])
    hb, xhat1, rstd1, y_a, y_b, mix = _fwd_mix(o_a, o_b, gates, x2, b_gate_full, w_oa, w_ob, w_out_full, ln1_g, ln1_b, seq=seq)
    u, dz2, dz2b, stat2 = _fwd_mlp(hb, xhat1, loss_target.reshape(tokens, D_MODEL), g_ff1, w_ff2_full, ln1_g, ln1_b, ln2_g, ln2_b, seq=seq)

    du, dz1, dz1b, stat1 = _bwd_mlp(dz2, dz2b, u, xhat1, rstd1, g_ff1, w_ff2_full, ln1_g, seq=seq)
    dw_ff = [_wgrad(hb, du, "wgrad_ff1", by_shard=True),
             _wgrad(u, dz2b, "wgrad_ff2", square_relu=True).reshape(N_DEV, FF_SHARD, D_MODEL)]
    dgates, dy_a, dy_b, do_a, do_b, stat_g = _bwd_mix(dz1b, gates, y_a, y_b, b_gate_full, w_oa, w_ob, w_out_full, seq=seq)
    dqp, dkp, dvm, r_ff1 = _attn_bwd(qp, kp, vm, o_a, do_a, lse_a, no_bias, name="mla_attention_bwd",
                                     comm=_Scatter(dw_ff[:1]), comm_arrays=dw_ff[:1], **mla)
    dw_mid = [_to_col_shards(_wgrad(o_a, dy_a, "wgrad_o_mla")), _to_col_shards(_wgrad(o_b, dy_b, "wgrad_o_dil")),
              _wgrad(mix, dz1b, "wgrad_out").reshape(N_DEV, D_MODEL // N_DEV, D_MODEL),
              _pad_rows(_to_col_shards(stat_g[0:2]).astype(BF16), 16)]
    second = dw_ff[1:] + dw_mid
    dq_d, dk_d, dv_d, r_ff2, r_oa, r_ob, r_out, r_bg = _attn_bwd(qkvd, qkvd, qkvd, o_b, do_b, lse_b, dil_bias, name="dilated_attention_bwd",
                                                                 comm=_Scatter(second), comm_arrays=second, **dil)
    grad_x, dproj, d_a, dkn, stat_r = _bwd_proj(dqp, dkp, dvm, dq_d, dk_d, dv_d, dgates, dz1, low, w_in_ext, w1, wk, wv,
                                                g_q_a, g_kv_a, cext, sext, cs128, seq=seq)

    dw_in_ext = _wgrad(dproj, xb, "wgrad_in")
    dw1 = _wgrad(qn, d_a, "wgrad_uq")
    dwk = _wgrad(kvn, dkn, "wgrad_ukv_k")
    dwv = _wgrad(kvn, dvm, "wgrad_ukv_v")
    dw_kr = dw_in_ext[s1:s2] + _unrot_cols(dw_in_ext[s2:s2 + ROPE].T).T

    def dw_in_cols(lo, hi):
        out = []
        for a, b, piece in ((0, s1, lambda u, v: dw_in_ext[u:v]), (s1, s2, lambda u, v: dw_kr[u - s1:v - s1]),
                            (s2, n_in, lambda u, v: dw_in_ext[u + LOW_W - s2:v + LOW_W - s2])):
            if max(lo, a) < min(hi, b):
                out.append(piece(max(lo, a), min(hi, b)))
        return out

    dw_in = jnp.stack([jnp.concatenate(dw_in_cols(d * widths[0], (d + 1) * widths[0]), axis=0) for d in range(N_DEV)])
    n1, r1 = _split_slabs(dw1)
    dw_uq = jnp.concatenate([n1, r1], axis=2).reshape(Q_LORA, N_HEADS * (NOPE + ROPE))
    dw_ukv = jnp.concatenate([dwk.reshape(KV_LORA, N_HEADS, NOPE), dwv.reshape(KV_LORA, N_HEADS, HEAD_V)], axis=2).reshape(KV_LORA, N_HEADS * (NOPE + HEAD_V))
    last = [dw_in] + [_pad_cols(_to_col_shards(dw)) for dw in (dw_uq, dw_ukv)]
    theirs = _rs_sibling(last, "rs_last_sibling_exchange")
    sums = [_pair_sum(a, b, "rs_last_pair_sum_" + n) for a, b, n in zip(last, theirs, first)]
    partial = jnp.concatenate([stat_r[0:1, :Q_LORA], stat_r[1:2, :KV_LORA], stat1[0:1], stat1[1:2], stat2[0:1], stat2[1:2],
                               stat2[2:3, :LANES]], axis=1)
    partial = _pad_rows(partial.reshape(-1, LANES), 8)
    rest = [s[1] for s in sums]
    got_in, got_uq, got_ukv, every = _run_comm(_Plans([_ChipExchange(rest), _Gather([partial])]), rest + [partial],
                                               "rs_last_chip_exchange")

    upd = {}
    early = ["w_ff1", "w_ff2", "w_out", "w_o_mla", "w_o_dil"]
    items = [(weights[n][0], mom_m[n][0], mom_v[n][0], own, parts) for n, own, parts in
             zip(early, (dw_ff[0], dw_ff[1], dw_mid[2], dw_mid[0], dw_mid[1]), (r_ff1, r_ff2, r_out, r_oa, r_ob))]
    upd.update(zip(early, _adamw(items, "adamw_early_weights")))
    (in_t,) = _adamw([(weights["w_in"][0].T, mom_m["w_in"][0].T, mom_v["w_in"][0].T, sums[0][0], got_in)], "adamw_w_in")
    upd["w_in"] = tuple(a.T for a in in_t)
    for n, w, (own, _), parts in zip(first[1:], widths[1:], sums[1:], (got_uq, got_ukv)):
        (upd[n],) = _adamw([(weights[n][0], mom_m[n][0], mom_v[n][0], own[:, :w], parts[:, :, :w])], "adamw_" + n)
    (bg_upd,) = _adamw([(_pad_rows(b_gate[0], 16), _pad_rows(m_b_gate[0], 16), _pad_rows(v_b_gate[0], 16), dw_mid[3], r_bg)],
                       "adamw_b_gate")
    upd["b_gate"] = tuple(t[0:2] for t in bg_upd)

    small_w = [g_q_a, g_kv_a, ln1_g, ln1_b, ln2_g, ln2_b]
    small_m = [m_g_q_a, m_g_kv_a, m_ln1_g, m_ln1_b, m_ln2_g, m_ln2_b]
    small_v = [v_g_q_a, v_g_kv_a, v_ln1_g, v_ln1_b, v_ln2_g, v_ln2_b]
    small_widths = [a.shape[1] for a in small_w]

    def as_rows(vecs, extra):
        flat = jnp.concatenate(vecs + [jnp.zeros((1, extra), F32)], axis=1)
        return _pad_rows(flat.reshape(-1, LANES), 8)

    g_s, d_s, nm_s, nv_s = _adamw_small(every, as_rows(small_w, LANES), as_rows(small_m, LANES), as_rows(small_v, LANES))

    def split_small(a):
        flat = a.reshape(1, -1)
        out, c0 = [], 0
        for w in small_widths:
            out.append(flat[:, c0:c0 + w])
            c0 += w
        return out, flat[0, c0]

    g_small, loss = split_small(g_s)
    small = [g_small, split_small(d_s)[0], split_small(nm_s)[0], split_small(nv_s)[0]]

    order = ["w_in", "b_gate", "g_q_a", "w_uq", "g_kv_a", "w_ukv", "w_o_mla", "w_o_dil", "w_out", "ln1_g", "ln1_b", "w_ff1", "w_ff2", "ln2_g", "ln2_b"]
    small_names = ["g_q_a", "g_kv_a", "ln1_g", "ln1_b", "ln2_g", "ln2_b"]

    def pick(kind):
        return [small[kind][small_names.index(n)] if n in small_names else upd[n][kind][None] for n in order]

    return (loss, grad_x.reshape(batch, seq, D_MODEL), *pick(0), *pick(1), *pick(2), *pick(3))
```
